```python
import jax, jax.numpy as jnp
from jax import lax
import numpy as np

D_MODEL = 1024
BATCH = 16
SEQ = 2048
DEPTH = 1

N_META = 16
BLOCK = 128
WINDOW = 128
PREFIX = BLOCK
N_PAD = PREFIX - N_META
HEAD_DIM = 64
A_HEADS = D_MODEL // 128
A_KV_HEADS = A_HEADS // 4
A_GROUP = A_HEADS // A_KV_HEADS
B_HEADS = D_MODEL // 128
A_WIDTH = A_HEADS * HEAD_DIM
A_KV_WIDTH = A_KV_HEADS * HEAD_DIM
B_WIDTH = B_HEADS * HEAD_DIM
W_IN_COLS = A_WIDTH + 2 * A_KV_WIDTH + 3 * B_WIDTH + B_HEADS + 2 * D_MODEL
D_FF = ((8 * D_MODEL // 3 + 127) // 128) * 128
EPS = 1e-6
NEG = -1e30

kernel_name = "hybrid_swa_sink_fox_macaron_block"


def rms_norm(x, g):
    xf = x.astype(jnp.float32)
    y = xf * lax.rsqrt(jnp.mean(xf * xf, axis=-1, keepdims=True) + EPS)
    return (y * g.astype(jnp.float32)).astype(x.dtype)


def swiglu(x, w_in, w_out):
    gu = x @ w_in
    g, u = jnp.split(gu, 2, axis=-1)
    return (jax.nn.silu(g) * u) @ w_out


def alibi_slopes(n_heads):
    return jnp.exp2(-8.0 * jnp.arange(1, n_heads + 1, dtype=jnp.float32) / n_heads)


def sliding_window_sink_attention(q, k, v, sinks):
    b, l, _, dh = q.shape
    nb = l // BLOCK
    qb = q.reshape(b, nb, BLOCK, A_KV_HEADS, A_GROUP, dh)
    kb = k.reshape(b, nb, BLOCK, A_KV_HEADS, dh)
    vb = v.reshape(b, nb, BLOCK, A_KV_HEADS, dh)
    pad_blk = ((0, 0), (1, 0), (0, 0), (0, 0), (0, 0))
    band_k = jnp.concatenate([jnp.pad(kb[:, :-1], pad_blk), kb], axis=2)
    band_v = jnp.concatenate([jnp.pad(vb[:, :-1], pad_blk), vb], axis=2)
    meta_k = jnp.broadcast_to(k[:, None, N_PAD:PREFIX], (b, nb, N_META, A_KV_HEADS, dh))
    meta_v = jnp.broadcast_to(v[:, None, N_PAD:PREFIX], (b, nb, N_META, A_KV_HEADS, dh))
    keys = jnp.concatenate([meta_k, band_k], axis=2)
    vals = jnp.concatenate([meta_v, band_v], axis=2)

    q_pos = jnp.arange(l).reshape(nb, BLOCK)
    band_pos = (jnp.arange(nb)[:, None] - 1) * BLOCK + jnp.arange(2 * BLOCK)[None, :]
    meta_pos = jnp.broadcast_to(N_PAD + jnp.arange(N_META)[None, :], (nb, N_META))
    k_pos = jnp.concatenate([meta_pos, band_pos], axis=1)
    is_band = jnp.concatenate([jnp.zeros((N_META,), bool), jnp.ones((2 * BLOCK,), bool)])
    dist = q_pos[:, :, None] - k_pos[:, None, :]
    band_ok = (dist < WINDOW) & (k_pos[:, None, :] >= PREFIX)
    allowed = (dist >= 0) & jnp.where(is_band[None, None, :], band_ok, True)

    slopes = alibi_slopes(A_HEADS).reshape(A_KV_HEADS, A_GROUP)
    s = jnp.einsum('bnqkgd,bnskd->bnkgqs', qb, keys).astype(jnp.float32) * (dh ** -0.5)
    s = s - slopes[None, None, :, :, None, None] * dist.astype(jnp.float32)[None, :, None, None, :, :]
    s = jnp.where(allowed[None, :, None, None, :, :], s, NEG)
    sink = jnp.broadcast_to(
        sinks.astype(jnp.float32).reshape(A_KV_HEADS, A_GROUP)[None, None, :, :, None, None],
        s.shape[:-1] + (1,))
    p = jax.nn.softmax(jnp.concatenate([s, sink], axis=-1), axis=-1)[..., :-1]
    o = jnp.einsum('bnkgqs,bnskd->bnqkgd', p.astype(v.dtype), vals)
    return o.reshape(b, l, A_HEADS * dh)


def forgetting_attention(q, k, v, log_f):
    b, l, h, dh = q.shape
    nb = l // BLOCK
    c = jnp.cumsum(log_f, axis=1).transpose(0, 2, 1)
    outs = []
    for i in range(nb):
        q_lo, k_hi = i * BLOCK, (i + 1) * BLOCK
        s = jnp.einsum('bqhd,bshd->bhqs', q[:, q_lo:k_hi], k[:, :k_hi]).astype(jnp.float32) * (dh ** -0.5)
        s = s + c[:, :, q_lo:k_hi, None] - c[:, :, None, :k_hi]
        q_pos = q_lo + jnp.arange(BLOCK)
        k_pos = jnp.arange(k_hi)
        allowed = (k_pos[None, :] <= q_pos[:, None]) & (k_pos[None, :] >= N_PAD)
        s = jnp.where(allowed[None, None], s, NEG)
        p = jax.nn.softmax(s, axis=-1)
        outs.append(jnp.einsum('bhqs,bshd->bqhd', p.astype(v.dtype), v[:, :k_hi]))
    return jnp.concatenate(outs, axis=1).reshape(b, l, h * dh)


def _fwd_setup_inputs(seed: int = 0) -> dict:
    key = jax.random.key(seed)
    ks = jax.random.split(key, 20)
    f32 = jnp.float32
    n = lambda k, shape, scale: jax.random.normal(k, shape, f32) * scale
    return {
        "x": n(ks[0], (BATCH, SEQ, D_MODEL), 1.0),
        "meta_tokens": n(ks[1], (N_META, D_MODEL), 1.0),
        "ffn1_norm": 1.0 + n(ks[2], (DEPTH, D_MODEL), 0.02),
        "ffn1_w_in": n(ks[3], (DEPTH, D_MODEL, 2 * D_FF), D_MODEL ** -0.5),
        "ffn1_w_out": n(ks[4], (DEPTH, D_FF, D_MODEL), D_FF ** -0.5),
        "mix_norm": 1.0 + n(ks[5], (DEPTH, D_MODEL), 0.02),
        "w_in": n(ks[6], (DEPTH, D_MODEL, W_IN_COLS), D_MODEL ** -0.5),
        "b_forget": 2.0 + n(ks[7], (DEPTH, B_HEADS), 0.1),
        "attn_sinks": n(ks[8], (DEPTH, A_HEADS), 0.5),
        "w_branch_a": n(ks[9], (DEPTH, A_WIDTH, D_MODEL), A_WIDTH ** -0.5),
        "w_branch_b": n(ks[10], (DEPTH, B_WIDTH, D_MODEL), B_WIDTH ** -0.5),
        "w_out": n(ks[11], (DEPTH, D_MODEL, D_MODEL), D_MODEL ** -0.5),
        "ffn2_norm": 1.0 + n(ks[12], (DEPTH, D_MODEL), 0.02),
        "ffn2_w_in": n(ks[13], (DEPTH, D_MODEL, 2 * D_FF), D_MODEL ** -0.5),
        "ffn2_w_out": n(ks[14], (DEPTH, D_FF, D_MODEL), D_FF ** -0.5),
        "final_norm": 1.0 + n(ks[15], (D_MODEL,), 0.02),
    }


def _fwd_reference(x, meta_tokens, ffn1_norm, ffn1_w_in, ffn1_w_out, mix_norm, w_in, b_forget,
              attn_sinks, w_branch_a, w_branch_b, w_out, ffn2_norm, ffn2_w_in, ffn2_w_out,
              final_norm):
    b = x.shape[0]
    pads = jnp.zeros((b, N_PAD, D_MODEL), x.dtype)
    meta = jnp.broadcast_to(meta_tokens.astype(x.dtype)[None], (b, N_META, D_MODEL))
    h = jnp.concatenate([pads, meta, x], axis=1)
    l = h.shape[1]

    sizes = [A_WIDTH, A_KV_WIDTH, A_KV_WIDTH, B_WIDTH, B_WIDTH, B_WIDTH, B_HEADS, D_MODEL, D_MODEL]
    offsets = []
    acc = 0
    for sz in sizes[:-1]:
        acc += sz
        offsets.append(acc)

    for i in range(DEPTH):
        h = h + 0.5 * swiglu(rms_norm(h, ffn1_norm[i]), ffn1_w_in[i], ffn1_w_out[i])

        u = rms_norm(h, mix_norm[i])
        proj = u @ w_in[i]
        qa, ka, va, qb, kb, vb, f_logit, g_a, g_b = jnp.split(proj, offsets, axis=-1)
        qa = qa.reshape(b, l, A_HEADS, HEAD_DIM)
        ka = ka.reshape(b, l, A_KV_HEADS, HEAD_DIM)
        va = va.reshape(b, l, A_KV_HEADS, HEAD_DIM)
        qb = qb.reshape(b, l, B_HEADS, HEAD_DIM)
        kb = kb.reshape(b, l, B_HEADS, HEAD_DIM)
        vb = vb.reshape(b, l, B_HEADS, HEAD_DIM)
        log_f = jax.nn.log_sigmoid((f_logit + b_forget[i]).astype(jnp.float32))

        y_a = sliding_window_sink_attention(qa, ka, va, attn_sinks[i]) @ w_branch_a[i]
        y_b = forgetting_attention(qb, kb, vb, log_f) @ w_branch_b[i]
        mixed = jax.nn.sigmoid(g_a) * y_a + jax.nn.sigmoid(g_b) * y_b
        h = h + mixed @ w_out[i]

        h = h + 0.5 * swiglu(rms_norm(h, ffn2_norm[i]), ffn2_w_in[i], ffn2_w_out[i])

    return rms_norm(h, final_norm)[:, PREFIX:]


import jax as _jax
import jax.numpy as _jnp

TWIN_FORMAT = 'train_step'
FWD_PARAMS = ['x', 'meta_tokens', 'ffn1_norm', 'ffn1_w_in', 'ffn1_w_out', 'mix_norm', 'w_in', 'b_forget', 'attn_sinks', 'w_branch_a', 'w_branch_b', 'w_out', 'ffn2_norm', 'ffn2_w_in', 'ffn2_w_out', 'final_norm']
TWIN_WEIGHTS = ['meta_tokens', 'ffn1_norm', 'ffn1_w_in', 'ffn1_w_out', 'mix_norm', 'w_in', 'b_forget', 'attn_sinks', 'w_branch_a', 'w_branch_b', 'w_out', 'ffn2_norm', 'ffn2_w_in', 'ffn2_w_out', 'final_norm']
TWIN_DIFF_INPUT = 'x'
TWIN_INPUTS = ['x', 'meta_tokens', 'ffn1_norm', 'ffn1_w_in', 'ffn1_w_out', 'mix_norm', 'w_in', 'b_forget', 'attn_sinks', 'w_branch_a', 'w_branch_b', 'w_out', 'ffn2_norm', 'ffn2_w_in', 'ffn2_w_out', 'final_norm', 'loss_target', 'm_meta_tokens', 'm_ffn1_norm', 'm_ffn1_w_in', 'm_ffn1_w_out', 'm_mix_norm', 'm_w_in', 'm_b_forget', 'm_attn_sinks', 'm_w_branch_a', 'm_w_branch_b', 'm_w_out', 'm_ffn2_norm', 'm_ffn2_w_in', 'm_ffn2_w_out', 'm_final_norm', 'v_meta_tokens', 'v_ffn1_norm', 'v_ffn1_w_in', 'v_ffn1_w_out', 'v_mix_norm', 'v_w_in', 'v_b_forget', 'v_attn_sinks', 'v_w_branch_a', 'v_w_branch_b', 'v_w_out', 'v_ffn2_norm', 'v_ffn2_w_in', 'v_ffn2_w_out', 'v_final_norm']
TWIN_OUTPUTS = ['loss', 'grad_x', 'grad_meta_tokens', 'grad_ffn1_norm', 'grad_ffn1_w_in', 'grad_ffn1_w_out', 'grad_mix_norm', 'grad_w_in', 'grad_b_forget', 'grad_attn_sinks', 'grad_w_branch_a', 'grad_w_branch_b', 'grad_w_out', 'grad_ffn2_norm', 'grad_ffn2_w_in', 'grad_ffn2_w_out', 'grad_final_norm', 'delta_meta_tokens', 'delta_ffn1_norm', 'delta_ffn1_w_in', 'delta_ffn1_w_out', 'delta_mix_norm', 'delta_w_in', 'delta_b_forget', 'delta_attn_sinks', 'delta_w_branch_a', 'delta_w_branch_b', 'delta_w_out', 'delta_ffn2_norm', 'delta_ffn2_w_in', 'delta_ffn2_w_out', 'delta_final_norm', 'new_m_meta_tokens', 'new_m_ffn1_norm', 'new_m_ffn1_w_in', 'new_m_ffn1_w_out', 'new_m_mix_norm', 'new_m_w_in', 'new_m_b_forget', 'new_m_attn_sinks', 'new_m_w_branch_a', 'new_m_w_branch_b', 'new_m_w_out', 'new_m_ffn2_norm', 'new_m_ffn2_w_in', 'new_m_ffn2_w_out', 'new_m_final_norm', 'new_v_meta_tokens', 'new_v_ffn1_norm', 'new_v_ffn1_w_in', 'new_v_ffn1_w_out', 'new_v_mix_norm', 'new_v_w_in', 'new_v_b_forget', 'new_v_attn_sinks', 'new_v_w_branch_a', 'new_v_w_branch_b', 'new_v_w_out', 'new_v_ffn2_norm', 'new_v_ffn2_w_in', 'new_v_ffn2_w_out', 'new_v_final_norm']
TWIN_LEAF_KINDS = {'loss': 'loss', 'grad_x': 'grad_x', 'grad_meta_tokens': 'grad_w', 'grad_ffn1_norm': 'grad_w', 'grad_ffn1_w_in': 'grad_w', 'grad_ffn1_w_out': 'grad_w', 'grad_mix_norm': 'grad_w', 'grad_w_in': 'grad_w', 'grad_b_forget': 'grad_w', 'grad_attn_sinks': 'grad_w', 'grad_w_branch_a': 'grad_w', 'grad_w_branch_b': 'grad_w', 'grad_w_out': 'grad_w', 'grad_ffn2_norm': 'grad_w', 'grad_ffn2_w_in': 'grad_w', 'grad_ffn2_w_out': 'grad_w', 'grad_final_norm': 'grad_w', 'delta_meta_tokens': 'delta_w', 'delta_ffn1_norm': 'delta_w', 'delta_ffn1_w_in': 'delta_w', 'delta_ffn1_w_out': 'delta_w', 'delta_mix_norm': 'delta_w', 'delta_w_in': 'delta_w', 'delta_b_forget': 'delta_w', 'delta_attn_sinks': 'delta_w', 'delta_w_branch_a': 'delta_w', 'delta_w_branch_b': 'delta_w', 'delta_w_out': 'delta_w', 'delta_ffn2_norm': 'delta_w', 'delta_ffn2_w_in': 'delta_w', 'delta_ffn2_w_out': 'delta_w', 'delta_final_norm': 'delta_w', 'new_m_meta_tokens': 'new_m', 'new_m_ffn1_norm': 'new_m', 'new_m_ffn1_w_in': 'new_m', 'new_m_ffn1_w_out': 'new_m', 'new_m_mix_norm': 'new_m', 'new_m_w_in': 'new_m', 'new_m_b_forget': 'new_m', 'new_m_attn_sinks': 'new_m', 'new_m_w_branch_a': 'new_m', 'new_m_w_branch_b': 'new_m', 'new_m_w_out': 'new_m', 'new_m_ffn2_norm': 'new_m', 'new_m_ffn2_w_in': 'new_m', 'new_m_ffn2_w_out': 'new_m', 'new_m_final_norm': 'new_m', 'new_v_meta_tokens': 'new_v', 'new_v_ffn1_norm': 'new_v', 'new_v_ffn1_w_in': 'new_v', 'new_v_ffn1_w_out': 'new_v', 'new_v_mix_norm': 'new_v', 'new_v_w_in': 'new_v', 'new_v_b_forget': 'new_v', 'new_v_attn_sinks': 'new_v', 'new_v_w_branch_a': 'new_v', 'new_v_w_branch_b': 'new_v', 'new_v_w_out': 'new_v', 'new_v_ffn2_norm': 'new_v', 'new_v_ffn2_w_in': 'new_v', 'new_v_ffn2_w_out': 'new_v', 'new_v_final_norm': 'new_v'}


def _forward(args):
    return _fwd_reference(*[args[k] for k in FWD_PARAMS])


def _output_shape():
    out = _jax.eval_shape(lambda: _forward(_fwd_setup_inputs(0)))
    return out.shape, out.dtype

N_MICROBATCH = 1
ADAM_LR = 0.001
ADAM_B1 = 0.9
ADAM_B2 = 0.999
ADAM_EPS = 1e-08
ADAM_WD = 0.01
ADAM_STEP = 10
PER_EXAMPLE_BATCH_AXIS = {'x': 0, 'loss_target': 0}
SHARED_INPUTS = []
_WEIGHT_DTYPES = {'meta_tokens': _jnp.float32, 'ffn1_norm': _jnp.float32, 'ffn1_w_in': _jnp.float32, 'ffn1_w_out': _jnp.float32, 'mix_norm': _jnp.float32, 'w_in': _jnp.float32, 'b_forget': _jnp.float32, 'attn_sinks': _jnp.float32, 'w_branch_a': _jnp.float32, 'w_branch_b': _jnp.float32, 'w_out': _jnp.float32, 'ffn2_norm': _jnp.float32, 'ffn2_w_in': _jnp.float32, 'ffn2_w_out': _jnp.float32, 'final_norm': _jnp.float32}
MOMENT_SCALE = {'meta_tokens': 2.486078e-03, 'ffn1_norm': 8.281959e-02, 'ffn1_w_in': 3.360207e-02, 'ffn1_w_out': 5.486699e-02, 'mix_norm': 7.446424e-02, 'w_in': 3.507454e-02, 'b_forget': 2.622835e-01, 'attn_sinks': 3.403210e-02, 'w_branch_a': 2.756592e-02, 'w_branch_b': 3.853234e-02, 'w_out': 4.687900e-02, 'ffn2_norm': 7.276587e-02, 'ffn2_w_in': 2.901302e-02, 'ffn2_w_out': 4.737649e-02, 'final_norm': 3.197980e+01}


def _to_microbatches(a, axis):
    t = _jnp.moveaxis(a, axis, 0)
    t = t.reshape((N_MICROBATCH, t.shape[0] // N_MICROBATCH) + t.shape[1:])
    return _jnp.moveaxis(t, 1, axis + 1)


def setup_inputs(seed: int = 0) -> dict:
    inp = _fwd_setup_inputs(seed)
    key = _jax.random.fold_in(_jax.random.key(seed), 7919)
    shape, _ = _output_shape()
    out = dict(inp)
    out["loss_target"] = _jax.random.normal(_jax.random.fold_in(key, 0), shape, _jnp.float32)
    for i, name in enumerate(TWIN_WEIGHTS):
        w = inp[name].astype(_jnp.float32)
        if MOMENT_SCALE is None:
            s = _jnp.sqrt(_jnp.mean(_jnp.square(w)) + 1e-30)
        else:
            s = MOMENT_SCALE[name]
        km, kv = _jax.random.split(_jax.random.fold_in(key, i + 1))
        out[name] = w
        out["m_" + name] = s * _jax.random.normal(km, w.shape, _jnp.float32)
        out["v_" + name] = (s * s) * _jax.random.uniform(kv, w.shape, _jnp.float32, 0.5, 1.5)
    if N_MICROBATCH > 1:
        for name, axis in PER_EXAMPLE_BATCH_AXIS.items():
            out[name] = _to_microbatches(out[name], axis)
    return {'x': out['x'], 'meta_tokens': out['meta_tokens'], 'ffn1_norm': out['ffn1_norm'], 'ffn1_w_in': out['ffn1_w_in'], 'ffn1_w_out': out['ffn1_w_out'], 'mix_norm': out['mix_norm'], 'w_in': out['w_in'], 'b_forget': out['b_forget'], 'attn_sinks': out['attn_sinks'], 'w_branch_a': out['w_branch_a'], 'w_branch_b': out['w_branch_b'], 'w_out': out['w_out'], 'ffn2_norm': out['ffn2_norm'], 'ffn2_w_in': out['ffn2_w_in'], 'ffn2_w_out': out['ffn2_w_out'], 'final_norm': out['final_norm'], 'loss_target': out['loss_target'], 'm_meta_tokens': out['m_meta_tokens'], 'm_ffn1_norm': out['m_ffn1_norm'], 'm_ffn1_w_in': out['m_ffn1_w_in'], 'm_ffn1_w_out': out['m_ffn1_w_out'], 'm_mix_norm': out['m_mix_norm'], 'm_w_in': out['m_w_in'], 'm_b_forget': out['m_b_forget'], 'm_attn_sinks': out['m_attn_sinks'], 'm_w_branch_a': out['m_w_branch_a'], 'm_w_branch_b': out['m_w_branch_b'], 'm_w_out': out['m_w_out'], 'm_ffn2_norm': out['m_ffn2_norm'], 'm_ffn2_w_in': out['m_ffn2_w_in'], 'm_ffn2_w_out': out['m_ffn2_w_out'], 'm_final_norm': out['m_final_norm'], 'v_meta_tokens': out['v_meta_tokens'], 'v_ffn1_norm': out['v_ffn1_norm'], 'v_ffn1_w_in': out['v_ffn1_w_in'], 'v_ffn1_w_out': out['v_ffn1_w_out'], 'v_mix_norm': out['v_mix_norm'], 'v_w_in': out['v_w_in'], 'v_b_forget': out['v_b_forget'], 'v_attn_sinks': out['v_attn_sinks'], 'v_w_branch_a': out['v_w_branch_a'], 'v_w_branch_b': out['v_w_branch_b'], 'v_w_out': out['v_w_out'], 'v_ffn2_norm': out['v_ffn2_norm'], 'v_ffn2_w_in': out['v_ffn2_w_in'], 'v_ffn2_w_out': out['v_ffn2_w_out'], 'v_final_norm': out['v_final_norm']}


def _loss(weights, diff, rest, loss_target):
    with _jax.named_scope("forward"):
        args = {**rest, TWIN_DIFF_INPUT: diff, **{k: w.astype(_WEIGHT_DTYPES[k]) for k, w in weights.items()}}
        y = _forward(args)
    with _jax.named_scope("loss_head"):
        err = _jnp.square(y.astype(_jnp.float32) - loss_target)
        return 0.5 * _jnp.sum(_jnp.mean(err, axis=-1)) if err.ndim else 0.5 * err


def _adamw(w, g, m, v):
    m = ADAM_B1 * m + (1.0 - ADAM_B1) * g
    v = ADAM_B2 * v + (1.0 - ADAM_B2) * _jnp.square(g)
    m_hat = m / (1.0 - ADAM_B1 ** ADAM_STEP)
    v_hat = v / (1.0 - ADAM_B2 ** ADAM_STEP)
    delta = -ADAM_LR * (m_hat / (_jnp.sqrt(v_hat) + ADAM_EPS) + ADAM_WD * w)
    return delta, m, v


def reference(x, meta_tokens, ffn1_norm, ffn1_w_in, ffn1_w_out, mix_norm, w_in, b_forget, attn_sinks, w_branch_a, w_branch_b, w_out, ffn2_norm, ffn2_w_in, ffn2_w_out, final_norm, loss_target, m_meta_tokens, m_ffn1_norm, m_ffn1_w_in, m_ffn1_w_out, m_mix_norm, m_w_in, m_b_forget, m_attn_sinks, m_w_branch_a, m_w_branch_b, m_w_out, m_ffn2_norm, m_ffn2_w_in, m_ffn2_w_out, m_final_norm, v_meta_tokens, v_ffn1_norm, v_ffn1_w_in, v_ffn1_w_out, v_mix_norm, v_w_in, v_b_forget, v_attn_sinks, v_w_branch_a, v_w_branch_b, v_w_out, v_ffn2_norm, v_ffn2_w_in, v_ffn2_w_out, v_final_norm):
    given = dict(x=x, meta_tokens=meta_tokens, ffn1_norm=ffn1_norm, ffn1_w_in=ffn1_w_in, ffn1_w_out=ffn1_w_out, mix_norm=mix_norm, w_in=w_in, b_forget=b_forget, attn_sinks=attn_sinks, w_branch_a=w_branch_a, w_branch_b=w_branch_b, w_out=w_out, ffn2_norm=ffn2_norm, ffn2_w_in=ffn2_w_in, ffn2_w_out=ffn2_w_out, final_norm=final_norm, loss_target=loss_target, m_meta_tokens=m_meta_tokens, m_ffn1_norm=m_ffn1_norm, m_ffn1_w_in=m_ffn1_w_in, m_ffn1_w_out=m_ffn1_w_out, m_mix_norm=m_mix_norm, m_w_in=m_w_in, m_b_forget=m_b_forget, m_attn_sinks=m_attn_sinks, m_w_branch_a=m_w_branch_a, m_w_branch_b=m_w_branch_b, m_w_out=m_w_out, m_ffn2_norm=m_ffn2_norm, m_ffn2_w_in=m_ffn2_w_in, m_ffn2_w_out=m_ffn2_w_out, m_final_norm=m_final_norm, v_meta_tokens=v_meta_tokens, v_ffn1_norm=v_ffn1_norm, v_ffn1_w_in=v_ffn1_w_in, v_ffn1_w_out=v_ffn1_w_out, v_mix_norm=v_mix_norm, v_w_in=v_w_in, v_b_forget=v_b_forget, v_attn_sinks=v_attn_sinks, v_w_branch_a=v_w_branch_a, v_w_branch_b=v_w_branch_b, v_w_out=v_w_out, v_ffn2_norm=v_ffn2_norm, v_ffn2_w_in=v_ffn2_w_in, v_ffn2_w_out=v_ffn2_w_out, v_final_norm=v_final_norm)
    weights = {n: given[n] for n in TWIN_WEIGHTS}
    shared = {n: given[n] for n in SHARED_INPUTS}
    per_example = {n: given[n] for n in ['x']}
    grad_fn = _jax.value_and_grad(_loss, argnums=(0, 1))

    def one_microbatch(ex, loss_target):
        ex = dict(ex)
        diff = ex.pop(TWIN_DIFF_INPUT)
        return grad_fn(weights, diff, {**shared, **ex}, loss_target)

    if N_MICROBATCH == 1:
        loss, (grad_w, grad_x) = one_microbatch(per_example, given["loss_target"])
    else:
        def body(carry, xs):
            loss_sum, grad_sum = carry
            l_k, (gw_k, gx_k) = one_microbatch(xs[0], xs[1])
            with _jax.named_scope("update"):
                return (loss_sum + l_k, _jax.tree.map(_jnp.add, grad_sum, gw_k)), gx_k

        init = (_jnp.zeros((), _jnp.float32), _jax.tree.map(_jnp.zeros_like, weights))
        (loss, grad_w), grad_x = _jax.lax.scan(body, init, (per_example, given["loss_target"]))
    with _jax.named_scope("update"):
        delta_w, new_m, new_v = {}, {}, {}
        for n in TWIN_WEIGHTS:
            delta_w[n], new_m[n], new_v[n] = _adamw(weights[n], grad_w[n], given["m_" + n], given["v_" + n])
    return (loss, grad_x, *[grad_w[n] for n in TWIN_WEIGHTS], *[delta_w[n] for n in TWIN_WEIGHTS],
            *[new_m[n] for n in TWIN_WEIGHTS], *[new_v[n] for n in TWIN_WEIGHTS])
```

```python
import jax
import jax.numpy as jnp
from jax import lax
from jax.experimental import pallas as pl
from jax.experimental.pallas import tpu as pltpu

F32 = jnp.float32
BF16 = jnp.bfloat16

D_MODEL = 1024
N_META = 16
BLOCK = 128
PREFIX = 128
N_PAD = PREFIX - N_META
HEAD_DIM = 64
A_HEADS = 8
A_KV_HEADS = 2
A_GROUP = 4
B_HEADS = 8
A_WIDTH = 512
A_KV_WIDTH = 128
B_WIDTH = 512
D_FF = 2816
N_DEV = 8
FF_SHARD = 2 * D_FF // N_DEV
FF_SHARD_P = 768
FFO_SHARD = D_FF // N_DEV
FFO_SHARD_P = FF_SHARD_P // 2
D_FF_P = 4 * FF_SHARD_P
W_IN_COLS = 4360
WIN_SHARD = W_IN_COLS // N_DEV
WIN_SHARD_P = 640
P_GA, P_GB, P_QA, P_KA, P_VA, P_QB, P_KB, P_VB, P_F, PROJ_P = 0, 1024, 2048, 2560, 2688, 2816, 3328, 3840, 4352, 4480
EPS = 1e-6
NEG = -1e30
SCALE = HEAD_DIM ** -0.5
ADAM_LR = 0.001
ADAM_B1 = 0.9
ADAM_B2 = 0.999
ADAM_EPS = 1e-08
ADAM_WD = 0.01
ADAM_STEP = 10
VMEM_LIMIT = 48 * 1024 * 1024
MESH_ID = pl.DeviceIdType.MESH
SMALL_ROWS = 40

_NN = (((1,), (0,)), ((), ()))
_NT = (((1,), (1,)), ((), ()))
_TN = (((0,), (0,)), ((), ()))


def _params(sem=None):
    return pltpu.CompilerParams(dimension_semantics=sem, vmem_limit_bytes=VMEM_LIMIT)


def _pick(n, cands):
    for c in cands:
        if n % c == 0:
            return c
    raise ValueError(f"no tile for {n}")


def _bf(v):
    return v if v.dtype == BF16 else v.astype(BF16)


def _mm(name, a, b, dims, grid, a_spec, b_spec, o_spec, out_shape, out_dtype, acc_shape, k_axis=None, nk=1,
        alpha=1.0, res=None, res_spec=None):
    has_res = res is not None

    def body(*refs):
        a_ref, b_ref = refs[0], refs[1]
        r_ref = refs[2] if has_res else None
        o_ref = refs[3] if has_res else refs[2]

        def finish(acc):
            if alpha != 1.0:
                acc = acc * alpha
            if has_res:
                acc = acc + r_ref[...]
            o_ref[...] = acc.astype(o_ref.dtype)

        part = lax.dot_general(_bf(a_ref[...]), _bf(b_ref[...]), dims, preferred_element_type=F32)
        if nk == 1:
            finish(part)
        else:
            acc_ref = refs[-1]
            k = pl.program_id(k_axis)

            @pl.when(k == 0)
            def _():
                acc_ref[...] = part

            @pl.when(k > 0)
            def _():
                acc_ref[...] += part

            @pl.when(k == nk - 1)
            def _():
                finish(acc_ref[...])

    in_specs = [a_spec, b_spec] + ([res_spec] if has_res else [])
    args = (a, b) + ((res,) if has_res else ())
    sem = tuple("arbitrary" if (nk > 1 and i == k_axis) else "parallel" for i in range(len(grid)))
    return pl.pallas_call(
        body, name=name, grid=grid, in_specs=in_specs, out_specs=o_spec,
        out_shape=jax.ShapeDtypeStruct(out_shape, out_dtype),
        scratch_shapes=[pltpu.VMEM(acc_shape, F32)] if nk > 1 else [],
        compiler_params=_params(sem),
    )(*args)


def _mm_nn(name, a, b, out_dtype=F32, alpha=1.0, res=None, tn_c=(512, 640, 256, 128)):
    t, k = a.shape
    n = b.shape[1]
    tm = _pick(t, (1088, 768, 512, 256, 128))
    tn = _pick(n, tn_c)
    return _mm(name, a, b, _NN, (t // tm, n // tn),
               pl.BlockSpec((tm, k), lambda i, j: (i, 0)), pl.BlockSpec((k, tn), lambda i, j: (0, j)),
               pl.BlockSpec((tm, tn), lambda i, j: (i, j)), (t, n), out_dtype, None,
               alpha=alpha, res=res, res_spec=pl.BlockSpec((tm, tn), lambda i, j: (i, j)))


def _mm_nt(name, a, b, out_dtype=F32, alpha=1.0, tn_c=(768, 512, 256, 128), tk_c=None):
    t, k = a.shape
    n = b.shape[0]
    tm = _pick(t, (1088, 768, 512, 256, 128))
    tn = _pick(n, tn_c)
    tk = k if tk_c is None else _pick(k, tk_c)
    nk = k // tk
    return _mm(name, a, b, _NT, (t // tm, n // tn, nk),
               pl.BlockSpec((tm, tk), lambda i, j, kk: (i, kk)), pl.BlockSpec((tn, tk), lambda i, j, kk: (j, kk)),
               pl.BlockSpec((tm, tn), lambda i, j, kk: (i, j)), (t, n), out_dtype, (tm, tn), k_axis=2, nk=nk, alpha=alpha)


def _mm_tn(name, a, b, out_dtype=BF16, alpha=1.0, tm_c=(768, 512, 256, 128), tn_c=(512, 640, 256, 128)):
    t, m = a.shape
    n = b.shape[1]
    tm = _pick(m, tm_c)
    tn = _pick(n, tn_c)
    tk = _pick(t, (1088, 768, 512, 256, 128))
    nk = t // tk
    return _mm(name, a, b, _TN, (m // tm, n // tn, nk),
               pl.BlockSpec((tk, tm), lambda i, j, kk: (kk, i)), pl.BlockSpec((tk, tn), lambda i, j, kk: (kk, j)),
               pl.BlockSpec((tm, tn), lambda i, j, kk: (i, j)), (m, n), out_dtype, (tm, tn), k_axis=2, nk=nk, alpha=alpha)


def _ffn_in_fwd(name, n, wblk):
    t = n.shape[0]
    tm = _pick(t, (1088, 768, 512, 256, 128))
    return _mm(name, n, wblk, _NN, (t // tm, 2, 4),
               pl.BlockSpec((tm, D_MODEL), lambda i, s, j: (i, 0)),
               pl.BlockSpec((None, None, D_MODEL, FF_SHARD_P), lambda i, s, j: (s, j, 0, 0)),
               pl.BlockSpec((None, tm, FF_SHARD_P), lambda i, s, j: (s, i, j)), (2, t, D_FF_P), F32, None)


def _ffn_in_bwd_x(name, dgu, wblk):
    t = dgu.shape[1]
    tm = _pick(t, (1088, 768, 512, 256, 128))
    return _mm(name, dgu, wblk, _NT, (t // tm, 8),
               pl.BlockSpec((None, tm, FF_SHARD_P), lambda i, kk: (kk // 4, i, kk % 4)),
               pl.BlockSpec((None, None, D_MODEL, FF_SHARD_P), lambda i, kk: (kk // 4, kk % 4, 0, 0)),
               pl.BlockSpec((tm, D_MODEL), lambda i, kk: (i, 0)), (t, D_MODEL), F32, (tm, D_MODEL), k_axis=1, nk=8)


def _ffn_in_bwd_w(name, n, dgu):
    t = n.shape[0]
    tk = _pick(t, (1088, 768, 512, 256, 128))
    nk = t // tk
    return _mm(name, n, dgu, _TN, (2, 4, nk),
               pl.BlockSpec((tk, D_MODEL), lambda s, j, kk: (kk, 0)),
               pl.BlockSpec((None, tk, FF_SHARD_P), lambda s, j, kk: (s, kk, j)),
               pl.BlockSpec((None, None, D_MODEL, FF_SHARD_P), lambda s, j, kk: (s, j, 0, 0)),
               (2, 4, D_MODEL, FF_SHARD_P), BF16, (D_MODEL, FF_SHARD_P), k_axis=2, nk=nk)


def _rms_fwd(name, h, g):
    t = h.shape[0]
    tm = _pick(t, (544, 384, 256, 128))

    def body(h_ref, g_ref, o_ref):
        hv = h_ref[...]
        r = lax.rsqrt(jnp.mean(hv * hv, axis=-1, keepdims=True) + EPS)
        o_ref[...] = ((hv * r) * g_ref[...]).astype(BF16)

    return pl.pallas_call(
        body, name=name, grid=(t // tm,),
        in_specs=[pl.BlockSpec((tm, D_MODEL), lambda i: (i, 0)), pl.BlockSpec((1, D_MODEL), lambda i: (0, 0))],
        out_specs=pl.BlockSpec((tm, D_MODEL), lambda i: (i, 0)),
        out_shape=jax.ShapeDtypeStruct((t, D_MODEL), BF16), compiler_params=_params(("parallel",)),
    )(h, g)


def _rms_bwd(name, h, g, dn, dres):
    t = h.shape[0]
    tm = _pick(t, (544, 384, 256, 128))

    def body(h_ref, g_ref, dn_ref, dres_ref, dh_ref, dg_ref):
        i = pl.program_id(0)
        hv = h_ref[...]
        dnv = dn_ref[...]
        r = lax.rsqrt(jnp.mean(hv * hv, axis=-1, keepdims=True) + EPS)
        tv = dnv * g_ref[...]
        dot = jnp.mean(tv * hv, axis=-1, keepdims=True)
        dh_ref[...] = dres_ref[...] + (r * tv - hv * (r * r * r * dot))
        part = jnp.sum(dnv * (hv * r), axis=0, keepdims=True)

        @pl.when(i == 0)
        def _():
            dg_ref[...] = part

        @pl.when(i > 0)
        def _():
            dg_ref[...] += part

    row = pl.BlockSpec((tm, D_MODEL), lambda i: (i, 0))
    vec = pl.BlockSpec((1, D_MODEL), lambda i: (0, 0))
    return pl.pallas_call(
        body, name=name, grid=(t // tm,), in_specs=[row, vec, row, row], out_specs=[row, vec],
        out_shape=[jax.ShapeDtypeStruct((t, D_MODEL), F32), jax.ShapeDtypeStruct((1, D_MODEL), F32)],
        compiler_params=_params(("arbitrary",)),
    )(h, g, dn, dres)


def _swiglu_fwd(name, gu):
    t = gu.shape[1]
    tm = _pick(t, (544, 384, 256, 128))

    def body(gu_ref, a_ref):
        g = gu_ref[0]
        u = gu_ref[1]
        a_ref[...] = ((g * jax.nn.sigmoid(g)) * u).astype(BF16)

    return pl.pallas_call(
        body, name=name, grid=(t // tm, 4),
        in_specs=[pl.BlockSpec((2, tm, FF_SHARD_P), lambda i, j: (0, i, j))],
        out_specs=pl.BlockSpec((tm, FF_SHARD_P), lambda i, j: (i, j)),
        out_shape=jax.ShapeDtypeStruct((t, D_FF_P), BF16), compiler_params=_params(("parallel", "parallel")),
    )(gu)


def _swiglu_bwd(name, gu, da):
    t = gu.shape[1]
    tm = _pick(t, (544, 384, 256, 128))

    def body(gu_ref, da_ref, o_ref):
        g = gu_ref[0]
        u = gu_ref[1]
        d = da_ref[...]
        sg = jax.nn.sigmoid(g)
        o_ref[0] = (d * u * (sg * (1.0 + g * (1.0 - sg)))).astype(BF16)
        o_ref[1] = (d * (g * sg)).astype(BF16)

    return pl.pallas_call(
        body, name=name, grid=(t // tm, 4),
        in_specs=[pl.BlockSpec((2, tm, FF_SHARD_P), lambda i, j: (0, i, j)),
                  pl.BlockSpec((tm, FF_SHARD_P), lambda i, j: (i, j))],
        out_specs=pl.BlockSpec((2, tm, FF_SHARD_P), lambda i, j: (0, i, j)),
        out_shape=jax.ShapeDtypeStruct((2, t, D_FF_P), BF16), compiler_params=_params(("parallel", "parallel")),
    )(gu, da)


def _gate_fwd(name, proj, ya, yb):
    t = proj.shape[0]
    tm = _pick(t, (544, 384, 256, 128))
    tn = 512
    nb = D_MODEL // tn

    def body(ga_ref, gb_ref, ya_ref, yb_ref, o_ref):
        o_ref[...] = (jax.nn.sigmoid(ga_ref[...]) * ya_ref[...] + jax.nn.sigmoid(gb_ref[...]) * yb_ref[...]).astype(BF16)

    blk = pl.BlockSpec((tm, tn), lambda i, j: (i, j))
    return pl.pallas_call(
        body, name=name, grid=(t // tm, nb),
        in_specs=[pl.BlockSpec((tm, tn), lambda i, j: (i, P_GA // tn + j)),
                  pl.BlockSpec((tm, tn), lambda i, j: (i, P_GB // tn + j)), blk, blk],
        out_specs=blk, out_shape=jax.ShapeDtypeStruct((t, D_MODEL), BF16),
        compiler_params=_params(("parallel", "parallel")),
    )(proj, proj, ya, yb)


def _gate_bwd(name, proj, ya, yb, dmix):
    t = proj.shape[0]
    tm = _pick(t, (544, 384, 256, 128))
    tn = 512
    nb = D_MODEL // tn

    def body(ga_ref, gb_ref, ya_ref, yb_ref, dm_ref, dya_ref, dyb_ref, dga_ref, dgb_ref):
        dm = dm_ref[...]
        sa = jax.nn.sigmoid(ga_ref[...])
        sb = jax.nn.sigmoid(gb_ref[...])
        dya_ref[...] = (dm * sa).astype(BF16)
        dyb_ref[...] = (dm * sb).astype(BF16)
        dga_ref[...] = (dm * ya_ref[...] * (sa * (1.0 - sa))).astype(BF16)
        dgb_ref[...] = (dm * yb_ref[...] * (sb * (1.0 - sb))).astype(BF16)

    blk = pl.BlockSpec((tm, tn), lambda i, j: (i, j))
    out = jax.ShapeDtypeStruct((t, D_MODEL), BF16)
    return pl.pallas_call(
        body, name=name, grid=(t // tm, nb),
        in_specs=[pl.BlockSpec((tm, tn), lambda i, j: (i, P_GA // tn + j)),
                  pl.BlockSpec((tm, tn), lambda i, j: (i, P_GB // tn + j)), blk, blk, blk],
        out_specs=[blk, blk, blk, blk], out_shape=[out, out, out, out],
        compiler_params=_params(("parallel", "parallel")),
    )(proj, proj, ya, yb, dmix)


def _loss_head(name, h3, gf, tgt):
    b, l, _ = h3.shape
    nb = l // BLOCK

    def body(h_ref, g_ref, t_ref, dh_ref, loss_ref, dg_ref):
        first = (pl.program_id(0) == 0) & (pl.program_id(1) == 0)
        real = (pl.program_id(1) > 0).astype(F32)
        hv = h_ref[...]
        g = g_ref[...]
        r = lax.rsqrt(jnp.mean(hv * hv, axis=-1, keepdims=True) + EPS)
        xn = hv * r
        err = (xn * g - t_ref[...]) * real
        lpart = 0.5 * jnp.sum(jnp.mean(err * err, axis=-1, keepdims=True), axis=0, keepdims=True)
        dy = err * (1.0 / D_MODEL)
        tv = dy * g
        dot = jnp.mean(tv * hv, axis=-1, keepdims=True)
        dh_ref[...] = r * tv - hv * (r * r * r * dot)
        gpart = jnp.sum(dy * xn, axis=0, keepdims=True)

        @pl.when(first)
        def _():
            loss_ref[...] = jnp.zeros_like(loss_ref)
            dg_ref[...] = jnp.zeros_like(dg_ref)

        loss_ref[...] += jnp.broadcast_to(lpart, loss_ref.shape)
        dg_ref[...] += gpart

    return pl.pallas_call(
        body, name=name, grid=(b, nb),
        in_specs=[pl.BlockSpec((None, BLOCK, D_MODEL), lambda bi, n: (bi, n, 0)),
                  pl.BlockSpec((1, D_MODEL), lambda bi, n: (0, 0)),
                  pl.BlockSpec((None, BLOCK, D_MODEL), lambda bi, n: (bi, jnp.maximum(n - 1, 0), 0))],
        out_specs=[pl.BlockSpec((None, BLOCK, D_MODEL), lambda bi, n: (bi, n, 0)),
                   pl.BlockSpec((8, 128), lambda bi, n: (0, 0)),
                   pl.BlockSpec((1, D_MODEL), lambda bi, n: (0, 0))],
        out_shape=[jax.ShapeDtypeStruct(h3.shape, F32), jax.ShapeDtypeStruct((8, 128), F32),
                   jax.ShapeDtypeStruct((1, D_MODEL), F32)],
        compiler_params=_params(("arbitrary", "arbitrary")),
    )(h3, gf, tgt)


def _fgate_fwd(name, proj3, bf_row):
    b, l, _ = proj3.shape
    nb = l // BLOCK

    def body(f_ref, b_ref, c_ref):
        r_i = lax.broadcasted_iota(jnp.int32, (BLOCK, BLOCK), 0)
        c_i = lax.broadcasted_iota(jnp.int32, (BLOCK, BLOCK), 1)
        tri = (r_i >= c_i).astype(F32)
        carry = jnp.zeros((1, 128), F32)
        for blk in range(nb):
            rows = slice(blk * BLOCK, (blk + 1) * BLOCK)
            z = f_ref[rows, :] + b_ref[...]
            lf = jnp.minimum(z, 0.0) - jnp.log(1.0 + jnp.exp(-jnp.abs(z)))
            cb = jnp.dot(tri, lf, preferred_element_type=F32, precision=lax.Precision.HIGHEST) + carry
            c_ref[rows, :] = cb
            carry = cb[BLOCK - 1:BLOCK, :]

    return pl.pallas_call(
        body, name=name, grid=(b,),
        in_specs=[pl.BlockSpec((None, l, 128), lambda bi: (bi, 0, P_F // 128)),
                  pl.BlockSpec((1, 128), lambda bi: (0, 0))],
        out_specs=pl.BlockSpec((None, l, 128), lambda bi: (bi, 0, 0)),
        out_shape=jax.ShapeDtypeStruct((b, l, 128), F32), compiler_params=_params(("parallel",)),
    )(proj3, bf_row)


def _fgate_bwd(name, proj3, bf_row, dc):
    b, l, _ = proj3.shape
    nb = l // BLOCK

    def body(f_ref, b_ref, dc_ref, df_ref, db_ref):
        r_i = lax.broadcasted_iota(jnp.int32, (BLOCK, BLOCK), 0)
        c_i = lax.broadcasted_iota(jnp.int32, (BLOCK, BLOCK), 1)
        tri = (r_i <= c_i).astype(F32)
        carry = jnp.zeros((1, 128), F32)
        total = jnp.zeros((1, 128), F32)
        for blk in range(nb - 1, -1, -1):
            rows = slice(blk * BLOCK, (blk + 1) * BLOCK)
            rc = jnp.dot(tri, dc_ref[rows, :], preferred_element_type=F32, precision=lax.Precision.HIGHEST) + carry
            carry = rc[0:1, :]
            z = f_ref[rows, :] + b_ref[...]
            df = rc * (1.0 / (1.0 + jnp.exp(z)))
            df_ref[rows, :] = df
            total = total + jnp.sum(df, axis=0, keepdims=True)

        @pl.when(pl.program_id(0) == 0)
        def _():
            db_ref[...] = total

        @pl.when(pl.program_id(0) > 0)
        def _():
            db_ref[...] += total

    return pl.pallas_call(
        body, name=name, grid=(b,),
        in_specs=[pl.BlockSpec((None, l, 128), lambda bi: (bi, 0, P_F // 128)),
                  pl.BlockSpec((1, 128), lambda bi: (0, 0)),
                  pl.BlockSpec((None, l, 128), lambda bi: (bi, 0, 0))],
        out_specs=[pl.BlockSpec((None, l, 128), lambda bi: (bi, 0, 0)), pl.BlockSpec((1, 128), lambda bi: (0, 0))],
        out_shape=[jax.ShapeDtypeStruct((b, l, 128), F32), jax.ShapeDtypeStruct((1, 128), F32)],
        compiler_params=_params(("arbitrary",)),
    )(proj3, bf_row, dc)


def _swa_scores(q, kcat, n, slope):
    rows = A_GROUP * BLOCK
    r_i = lax.broadcasted_iota(jnp.int32, (rows, 3 * BLOCK), 0)
    c_i = lax.broadcasted_iota(jnp.int32, (rows, 3 * BLOCK), 1)
    qpos = n * BLOCK + (r_i & (BLOCK - 1))
    seg = c_i >> 7
    kpos = jnp.where(seg == 0, c_i, (n - 2) * BLOCK + c_i)
    dist = qpos - kpos
    s = lax.dot_general(q, kcat, _NT, preferred_element_type=F32) * SCALE
    s = s - slope * dist.astype(F32)
    is_meta = seg == 0
    band = jnp.logical_not(is_meta) & (dist < BLOCK) & (kpos >= PREFIX)
    meta = is_meta & (c_i >= N_PAD)
    return jnp.where((dist >= 0) & (band | meta), s, NEG)


def _swa_specs(l):
    q_spec = pl.BlockSpec((None, A_GROUP, BLOCK, HEAD_DIM), lambda g, b, n: (b, g, n, 0))
    k0 = pl.BlockSpec((None, None, BLOCK, HEAD_DIM), lambda g, b, n: (b, g, 0, 0))
    kp = pl.BlockSpec((None, None, BLOCK, HEAD_DIM), lambda g, b, n: (b, g, jnp.maximum(n - 1, 0), 0))
    kc = pl.BlockSpec((None, None, BLOCK, HEAD_DIM), lambda g, b, n: (b, g, n, 0))
    col = pl.BlockSpec((None, A_GROUP * BLOCK, 1), lambda g, b, n: (g, 0, 0))
    lse = pl.BlockSpec((None, A_GROUP, BLOCK, 1), lambda g, b, n: (b, g, n, 0))
    return q_spec, k0, kp, kc, col, lse


def _swa_fwd(name, q, k, v, slopes, sinks):
    b, _, l, _ = q.shape
    nb = l // BLOCK
    rows = A_GROUP * BLOCK

    def body(q_ref, k0_ref, kp_ref, kc_ref, v0_ref, vp_ref, vc_ref, sl_ref, sk_ref, o_ref, lse_ref):
        n = pl.program_id(2)
        qv = q_ref[...].reshape(rows, HEAD_DIM)
        kcat = jnp.concatenate([k0_ref[...], kp_ref[...], kc_ref[...]], axis=0)
        vcat = jnp.concatenate([v0_ref[...], vp_ref[...], vc_ref[...]], axis=0)
        s = _swa_scores(qv, kcat, n, sl_ref[...])
        sink = sk_ref[...]
        m = jnp.maximum(jnp.max(s, axis=-1, keepdims=True), sink)
        p = jnp.exp(s - m)
        den = jnp.sum(p, axis=-1, keepdims=True) + jnp.exp(sink - m)
        o = lax.dot_general(p.astype(BF16), vcat, _NN, preferred_element_type=F32) / den
        o_ref[...] = o.astype(BF16).reshape(A_GROUP, BLOCK, HEAD_DIM)
        lse_ref[...] = (m + jnp.log(den)).reshape(A_GROUP, BLOCK, 1)

    q_spec, k0, kp, kc, col, lse_spec = _swa_specs(l)
    return pl.pallas_call(
        body, name=name, grid=(A_KV_HEADS, b, nb),
        in_specs=[q_spec, k0, kp, kc, k0, kp, kc, col, col], out_specs=[q_spec, lse_spec],
        out_shape=[jax.ShapeDtypeStruct(q.shape, BF16), jax.ShapeDtypeStruct((b, A_HEADS, l, 1), F32)],
        compiler_params=_params(("parallel", "parallel", "parallel")),
    )(q, k, k, k, v, v, v, slopes, sinks)


def _swa_bwd(name, q, k, v, o, lse, do, slopes, sinks):
    b, _, l, _ = q.shape
    nb = l // BLOCK
    rows = A_GROUP * BLOCK

    def body(q_ref, k0_ref, kp_ref, kc_ref, v0_ref, vp_ref, vc_ref, o_ref, lse_ref, do_ref, sl_ref, sk_ref,
             dq_ref, dk_ref, dv_ref, ds_ref):
        bi = pl.program_id(1)
        n = pl.program_id(2)
        qv = q_ref[...].reshape(rows, HEAD_DIM)
        dov = do_ref[...].reshape(rows, HEAD_DIM)
        ov = o_ref[...].reshape(rows, HEAD_DIM)
        lsev = lse_ref[...].reshape(rows, 1)
        kcat = jnp.concatenate([k0_ref[...], kp_ref[...], kc_ref[...]], axis=0)
        vcat = jnp.concatenate([v0_ref[...], vp_ref[...], vc_ref[...]], axis=0)
        s = _swa_scores(qv, kcat, n, sl_ref[...])
        p = jnp.exp(s - lsev)
        dsum = jnp.sum(dov.astype(F32) * ov.astype(F32), axis=-1, keepdims=True)
        dp = lax.dot_general(dov, vcat, _NT, preferred_element_type=F32)
        dsc = (p * (dp - dsum)).astype(BF16)
        dq = lax.dot_general(dsc, kcat, _NN, preferred_element_type=F32) * SCALE
        dq_ref[...] = dq.astype(BF16).reshape(A_GROUP, BLOCK, HEAD_DIM)
        dkc = lax.dot_general(dsc, qv, _TN, preferred_element_type=F32) * SCALE
        dvc = lax.dot_general(p.astype(BF16), dov, _TN, preferred_element_type=F32)

        @pl.when(n == 0)
        def _():
            dk_ref[...] = jnp.zeros_like(dk_ref)
            dv_ref[...] = jnp.zeros_like(dv_ref)

        starts = (0, pl.multiple_of(jnp.maximum(n - 1, 0) * BLOCK, BLOCK), pl.multiple_of(n * BLOCK, BLOCK))
        for t, st in enumerate(starts):
            dk_ref[pl.ds(st, BLOCK), :] += dkc[t * BLOCK:(t + 1) * BLOCK, :]
            dv_ref[pl.ds(st, BLOCK), :] += dvc[t * BLOCK:(t + 1) * BLOCK, :]

        dsink = -(jnp.exp(sk_ref[...] - lsev) * dsum)
        r8 = lax.broadcasted_iota(jnp.int32, (8, 128), 0)
        acc = jnp.zeros((8, 128), F32)
        for hh in range(A_GROUP):
            acc = acc + jnp.where(r8 == hh, jnp.sum(dsink[hh * BLOCK:(hh + 1) * BLOCK, :]), 0.0)

        @pl.when((bi == 0) & (n == 0))
        def _():
            ds_ref[...] = jnp.zeros_like(ds_ref)

        ds_ref[...] += acc

    q_spec, k0, kp, kc, col, lse_spec = _swa_specs(l)
    full = pl.BlockSpec((None, None, l, HEAD_DIM), lambda g, bb, n: (bb, g, 0, 0))
    return pl.pallas_call(
        body, name=name, grid=(A_KV_HEADS, b, nb),
        in_specs=[q_spec, k0, kp, kc, k0, kp, kc, q_spec, lse_spec, q_spec, col, col],
        out_specs=[q_spec, full, full, pl.BlockSpec((None, 8, 128), lambda g, bb, n: (g, 0, 0))],
        out_shape=[jax.ShapeDtypeStruct(q.shape, BF16), jax.ShapeDtypeStruct(k.shape, F32),
                   jax.ShapeDtypeStruct(k.shape, F32), jax.ShapeDtypeStruct((A_KV_HEADS, 8, 128), F32)],
        compiler_params=_params(("arbitrary", "arbitrary", "arbitrary")),
    )(q, k, k, k, v, v, v, o, lse, do, slopes, sinks)


def _fox_scores(q, k, cq, ck, i):
    kh = k.shape[0]
    s = lax.dot_general(q, k, _NT, preferred_element_type=F32) * SCALE
    s = (s + cq) - ck
    qpos = i * BLOCK + lax.broadcasted_iota(jnp.int32, (BLOCK, kh), 0)
    kpos = lax.broadcasted_iota(jnp.int32, (BLOCK, kh), 1)
    return jnp.where((kpos <= qpos) & (kpos >= N_PAD), s, NEG)


def _fox_fwd(name, q, k, v, c_col, c_row):
    b, h, l, _ = q.shape
    nb = l // BLOCK

    def body(q_ref, k_ref, v_ref, cc_ref, cr_ref, o_ref, lse_ref):
        for i in range(nb):
            rows = slice(i * BLOCK, (i + 1) * BLOCK)
            kh = (i + 1) * BLOCK
            s = _fox_scores(q_ref[rows, :], k_ref[0:kh, :], cc_ref[rows, :], cr_ref[:, 0:kh], i)
            m = jnp.max(s, axis=-1, keepdims=True)
            p = jnp.exp(s - m)
            den = jnp.sum(p, axis=-1, keepdims=True)
            o = lax.dot_general(p.astype(BF16), v_ref[0:kh, :], _NN, preferred_element_type=F32) / den
            o_ref[rows, :] = o.astype(BF16)
            lse_ref[rows, :] = m + jnp.log(den)

    mat = pl.BlockSpec((None, None, l, HEAD_DIM), lambda bi, hi: (bi, hi, 0, 0))
    colv = pl.BlockSpec((None, None, l, 1), lambda bi, hi: (bi, hi, 0, 0))
    rowv = pl.BlockSpec((None, None, 1, l), lambda bi, hi: (bi, hi, 0, 0))
    return pl.pallas_call(
        body, name=name, grid=(b, h), in_specs=[mat, mat, mat, colv, rowv], out_specs=[mat, colv],
        out_shape=[jax.ShapeDtypeStruct(q.shape, BF16), jax.ShapeDtypeStruct((b, h, l, 1), F32)],
        compiler_params=_params(("parallel", "parallel")),
    )(q, k, v, c_col, c_row)


def _fox_bwd(name, q, k, v, c_col, c_row, o, lse, do):
    b, h, l, _ = q.shape
    nb = l // BLOCK

    def body(q_ref, k_ref, v_ref, cc_ref, cr_ref, o_ref, lse_ref, do_ref, dq_ref, dk_ref, dv_ref, dcq_ref, dck_ref):
        dk_ref[...] = jnp.zeros_like(dk_ref)
        dv_ref[...] = jnp.zeros_like(dv_ref)
        dck_ref[...] = jnp.zeros_like(dck_ref)
        for i in range(nb):
            rows = slice(i * BLOCK, (i + 1) * BLOCK)
            kh = (i + 1) * BLOCK
            qv = q_ref[rows, :]
            dov = do_ref[rows, :]
            kv = k_ref[0:kh, :]
            s = _fox_scores(qv, kv, cc_ref[rows, :], cr_ref[:, 0:kh], i)
            p = jnp.exp(s - lse_ref[rows, :])
            dsum = jnp.sum(dov.astype(F32) * o_ref[rows, :].astype(F32), axis=-1, keepdims=True)
            dp = lax.dot_general(dov, v_ref[0:kh, :], _NT, preferred_element_type=F32)
            ds = p * (dp - dsum)
            dsc = ds.astype(BF16)
            dq_ref[rows, :] = (lax.dot_general(dsc, kv, _NN, preferred_element_type=F32) * SCALE).astype(BF16)
            dk_ref[0:kh, :] += lax.dot_general(dsc, qv, _TN, preferred_element_type=F32) * SCALE
            dv_ref[0:kh, :] += lax.dot_general(p.astype(BF16), dov, _TN, preferred_element_type=F32)
            dcq_ref[rows, :] = jnp.sum(ds, axis=-1, keepdims=True)
            dck_ref[:, 0:kh] -= jnp.sum(ds, axis=0, keepdims=True)

    mat = pl.BlockSpec((None, None, l, HEAD_DIM), lambda bi, hi: (bi, hi, 0, 0))
    colv = pl.BlockSpec((None, None, l, 1), lambda bi, hi: (bi, hi, 0, 0))
    rowv = pl.BlockSpec((None, None, 1, l), lambda bi, hi: (bi, hi, 0, 0))
    return pl.pallas_call(
        body, name=name, grid=(b, h), in_specs=[mat, mat, mat, colv, rowv, mat, colv, mat],
        out_specs=[mat, mat, mat, colv, rowv],
        out_shape=[jax.ShapeDtypeStruct(q.shape, BF16), jax.ShapeDtypeStruct(q.shape, F32),
                   jax.ShapeDtypeStruct(q.shape, F32), jax.ShapeDtypeStruct((b, h, l, 1), F32),
                   jax.ShapeDtypeStruct((b, h, 1, l), F32)],
        compiler_params=_params(("parallel", "parallel")),
    )(q, k, v, c_col, c_row, o, lse, do)


_FLIPS = ((0, 0, 1), (0, 1, 0), (0, 1, 1), (1, 0, 0), (1, 0, 1), (1, 1, 0), (1, 1, 1))


def _exchange(name, gather, scatter):
    ng, ns = len(gather), len(scatter)
    na = ng + ns
    npeer = len(_FLIPS)

    def body(*refs):
        ins = refs[:na]
        outs = refs[na:2 * na]
        send_sems, recv_sems, loc_sems = refs[2 * na:]
        x, y, c = lax.axis_index("x"), lax.axis_index("y"), lax.axis_index("c")
        me = 4 * x + 2 * y + c
        peers = []
        for fx, fy, fc in _FLIPS:
            px = 1 - x if fx else x
            py = 1 - y if fy else y
            pc = 1 - c if fc else c
            peers.append(((px, py, pc), 4 * px + 2 * py + pc))

        def remote(a, kk):
            dev, lin = peers[kk]
            src = ins[a] if a < ng else ins[a].at[lin]
            return pltpu.make_async_remote_copy(src_ref=src, dst_ref=outs[a].at[me], send_sem=send_sems.at[a * npeer + kk],
                                                recv_sem=recv_sems.at[a * npeer + kk], device_id=dev, device_id_type=MESH_ID)

        def arrival(a, kk):
            dev, lin = peers[kk]
            src = ins[a] if a < ng else ins[a].at[lin]
            return pltpu.make_async_remote_copy(src_ref=src, dst_ref=outs[a].at[lin], send_sem=send_sems.at[a * npeer + kk],
                                                recv_sem=recv_sems.at[a * npeer + kk], device_id=dev, device_id_type=MESH_ID)

        local = []
        for a in range(na):
            src = ins[a] if a < ng else ins[a].at[me]
            cp = pltpu.make_async_copy(src, outs[a].at[me], loc_sems.at[a])
            cp.start()
            local.append(cp)
        sent = [remote(a, kk) for kk in range(npeer) for a in range(na)]
        for cp in sent:
            cp.start()
        for kk in range(npeer):
            for a in range(na):
                arrival(a, kk).wait_recv()
        for cp in sent:
            cp.wait_send()
        for cp in local:
            cp.wait()

    arrs = list(gather) + list(scatter)
    out_shape = [jax.ShapeDtypeStruct((N_DEV,) + tuple(a.shape), a.dtype) for a in gather]
    out_shape += [jax.ShapeDtypeStruct(tuple(a.shape), a.dtype) for a in scatter]
    anyspec = pl.BlockSpec(memory_space=pl.ANY)
    return pl.pallas_call(
        body, name=name, in_specs=[anyspec] * na, out_specs=[anyspec] * na, out_shape=out_shape,
        scratch_shapes=[pltpu.SemaphoreType.DMA((na * npeer,)), pltpu.SemaphoreType.DMA((na * npeer,)),
                        pltpu.SemaphoreType.DMA((na,))],
        compiler_params=pltpu.CompilerParams(has_side_effects=True),
    )(*arrs)


def _adam_math(w, g, m, v):
    m = ADAM_B1 * m + (1.0 - ADAM_B1) * g
    v = ADAM_B2 * v + (1.0 - ADAM_B2) * (g * g)
    m_hat = m / (1.0 - ADAM_B1 ** ADAM_STEP)
    v_hat = v / (1.0 - ADAM_B2 ** ADAM_STEP)
    delta = -ADAM_LR * (m_hat / (jnp.sqrt(v_hat) + ADAM_EPS) + ADAM_WD * w)
    return delta, m, v


def _adam(name, w, m, v, parts):
    r, c = w.shape
    npart, _, cp = parts.shape
    tr = _pick(r, (256, 176, 128, 64, 16, 8, 1))

    def body(w_ref, m_ref, v_ref, p_ref, g_ref, d_ref, mo_ref, vo_ref):
        g = p_ref[0].astype(F32)
        for pp in range(1, npart):
            g = g + p_ref[pp].astype(F32)
        g = g[:, 0:c]
        delta, mn, vn = _adam_math(w_ref[...], g, m_ref[...], v_ref[...])
        g_ref[...] = g
        d_ref[...] = delta
        mo_ref[...] = mn
        vo_ref[...] = vn

    blk = pl.BlockSpec((tr, c), lambda i: (i, 0))
    out = jax.ShapeDtypeStruct((r, c), F32)
    return pl.pallas_call(
        body, name=name, grid=(r // tr,),
        in_specs=[blk, blk, blk, pl.BlockSpec((npart, tr, cp), lambda i: (0, i, 0))],
        out_specs=[blk, blk, blk, blk], out_shape=[out, out, out, out], compiler_params=_params(("parallel",)),
    )(w, m, v, parts)


def _small_sum(name, packs):
    def body(p_ref, o_ref):
        tot = p_ref[0]
        for pp in range(1, N_DEV):
            tot = tot + p_ref[pp]
        o_ref[0:8, :] = tot[0:8, :]
        o_ref[8:24, :] = tot[8:24, :] + tot[24:40, :]

    return pl.pallas_call(body, name=name, out_shape=jax.ShapeDtypeStruct((24, D_MODEL), F32),
                          compiler_params=_params())(packs)


def _heads(t2d, b, l, nh):
    return t2d.reshape(b, l, nh, HEAD_DIM).transpose(0, 2, 1, 3)


def _unheads(t4d):
    b, nh, l, _ = t4d.shape
    return t4d.transpose(0, 2, 1, 3).reshape(b * l, nh * HEAD_DIM)


def _local_step(x, tgt, meta, g1, gm, g2, gf, b_forget, sinks, w1i, w1o, wi, wa, wb, wo, w2i, w2o):
    b, s, _ = x.shape
    l = s + PREFIX
    t = b * l
    h0 = jnp.concatenate([jnp.zeros((b, N_PAD, D_MODEL), F32), jnp.broadcast_to(meta[None], (b, N_META, D_MODEL)), x],
                         axis=1).reshape(t, D_MODEL)

    n1 = _rms_fwd("rms1_fwd", h0, g1)
    gu1 = _ffn_in_fwd("ffn1_in_fwd", n1, w1i)
    a1 = _swiglu_fwd("swiglu1_fwd", gu1)
    h1 = _mm_nn("ffn1_out_fwd", a1, w1o, alpha=0.5, res=h0)
    um = _rms_fwd("rmsm_fwd", h1, gm)
    proj = _mm_nn("proj_fwd", um, wi, tn_c=(640,))
    proj3 = proj.reshape(b, l, PROJ_P)
    bf_row = jnp.pad(b_forget, ((0, 0), (0, 128 - B_HEADS)))
    cfull = _fgate_fwd("fgate_fwd", proj3, bf_row)
    c8 = cfull[:, :, 0:B_HEADS].transpose(0, 2, 1)
    c_col = c8[..., None]
    c_row = c8[:, :, None, :]
    qa = _heads(proj[:, P_QA:P_KA].astype(BF16), b, l, A_HEADS)
    ka = _heads(proj[:, P_KA:P_VA].astype(BF16), b, l, A_KV_HEADS)
    va = _heads(proj[:, P_VA:P_QB].astype(BF16), b, l, A_KV_HEADS)
    qb = _heads(proj[:, P_QB:P_KB].astype(BF16), b, l, B_HEADS)
    kb = _heads(proj[:, P_KB:P_VB].astype(BF16), b, l, B_HEADS)
    vb = _heads(proj[:, P_VB:P_F].astype(BF16), b, l, B_HEADS)
    head_of_row = jnp.arange(A_HEADS * BLOCK) // BLOCK
    slopes = jnp.exp2(-8.0 * (head_of_row + 1).astype(F32) / A_HEADS).reshape(A_KV_HEADS, A_GROUP * BLOCK, 1)
    sink_rows = jnp.repeat(sinks.reshape(A_HEADS), BLOCK).reshape(A_KV_HEADS, A_GROUP * BLOCK, 1)
    oa4, lse_a = _swa_fwd("swa_fwd", qa, ka, va, slopes, sink_rows)
    ob4, lse_b = _fox_fwd("fox_fwd", qb, kb, vb, c_col, c_row)
    oa = _unheads(oa4)
    ob = _unheads(ob4)
    ya = _mm_nn("branch_a_fwd", oa, wa)
    yb = _mm_nn("branch_b_fwd", ob, wb)
    mixed = _gate_fwd("gate_fwd", proj, ya, yb)
    h2 = _mm_nn("mix_out_fwd", mixed, wo, res=h1)
    n2 = _rms_fwd("rms2_fwd", h2, g2)
    gu2 = _ffn_in_fwd("ffn2_in_fwd", n2, w2i)
    a2 = _swiglu_fwd("swiglu2_fwd", gu2)
    h3 = _mm_nn("ffn2_out_fwd", a2, w2o, alpha=0.5, res=h2)

    dh3_3, loss_blk, dgf = _loss_head("loss_head", h3.reshape(b, l, D_MODEL), gf, tgt)
    dh3 = dh3_3.reshape(t, D_MODEL)

    def ffn_bwd(tag, dh, h_in, g_norm, n_in, gu, a, w_in_blk, w_out):
        da = _mm_nt(tag + "_out_bwd_x", dh, w_out, alpha=0.5)
        dw_out = _mm_tn(tag + "_out_bwd_w", a, dh, alpha=0.5)
        dgu = _swiglu_bwd("swiglu" + tag[-1] + "_bwd", gu, da)
        dw_in = _ffn_in_bwd_w(tag + "_in_bwd_w", n_in, dgu)
        dn = _ffn_in_bwd_x(tag + "_in_bwd_x", dgu, w_in_blk)
        dh_in, dg = _rms_bwd("rms" + tag[-1] + "_bwd", h_in, g_norm, dn, dh)
        return dh_in, dg, dw_in, dw_out

    dh2, dg2, dw2i, dw2o = ffn_bwd("ffn2", dh3, h2, g2, n2, gu2, a2, w2i, w2o)

    dmix = _mm_nt("mix_out_bwd_x", dh2, wo, tn_c=(512,))
    dwo = _mm_tn("mix_out_bwd_w", mixed, dh2)
    dya, dyb, dga, dgb = _gate_bwd("gate_bwd", proj, ya, yb, dmix)
    doa = _mm_nt("branch_a_bwd_x", dya, wa, out_dtype=BF16, tn_c=(512,))
    dob = _mm_nt("branch_b_bwd_x", dyb, wb, out_dtype=BF16, tn_c=(512,))
    dwa = _mm_tn("branch_a_bwd_w", oa, dya, tm_c=(512,))
    dwb = _mm_tn("branch_b_bwd_w", ob, dyb, tm_c=(512,))
    dqa, dka, dva, dsink = _swa_bwd("swa_bwd", qa, ka, va, oa4, lse_a, _heads(doa, b, l, A_HEADS), slopes, sink_rows)
    dqb, dkb, dvb, dcq, dck = _fox_bwd("fox_bwd", qb, kb, vb, c_col, c_row, ob4, lse_b, _heads(dob, b, l, B_HEADS))
    dc = (dcq[..., 0] + dck[:, :, 0, :]).transpose(0, 2, 1)
    dc = jnp.pad(dc, ((0, 0), (0, 0), (0, 128 - B_HEADS)))
    df3, dbf = _fgate_bwd("fgate_bwd", proj3, bf_row, dc)
    dproj = jnp.concatenate(
        [dga, dgb, _unheads(dqa), _unheads(dka).astype(BF16), _unheads(dva).astype(BF16), _unheads(dqb),
         _unheads(dkb).astype(BF16), _unheads(dvb).astype(BF16), df3.reshape(t, 128).astype(BF16)], axis=1)
    dwi = _mm_tn("proj_bwd_w", um, dproj, tm_c=(512,), tn_c=(640,))
    dum = _mm_nt("proj_bwd_x", dproj, wi, tn_c=(512,), tk_c=(640,))
    dh1, dgm = _rms_bwd("rmsm_bwd", h1, gm, dum, dh2)

    dh0, dg1, dw1i, dw1o = ffn_bwd("ffn1", dh1, h0, g1, n1, gu1, a1, w1i, w1o)
    dh0_3 = dh0.reshape(b, l, D_MODEL)
    grad_x = dh0_3[:, PREFIX:, :]
    dmeta = dh0_3[:, N_PAD:PREFIX, :].reshape(b * N_META, D_MODEL)

    misc = jnp.concatenate([dbf[:, 0:B_HEADS], dsink[:, 0:A_GROUP, 0].reshape(1, A_HEADS), loss_blk[0:1, 0:1]], axis=1)
    misc = jnp.pad(misc, ((0, 0), (0, D_MODEL - misc.shape[1])))
    row = lax.broadcasted_iota(jnp.int32, (8, D_MODEL), 0)
    vec = jnp.zeros((8, D_MODEL), F32)
    for i, piece in enumerate((dg1, dgm, dg2, dgf, misc)):
        vec = jnp.where(row == i, piece, vec)
    small = jnp.concatenate([vec, dmeta], axis=0)
    return grad_x, (dw1i, dw1o, dwi, dwa, dwb, dwo, dw2i, dw2o), small


def _pad_to(a, rows, cols):
    return jnp.pad(a, ((0, rows - a.shape[0]), (0, cols - a.shape[1])))


def _ffn_out_from_gathered(g):
    w = g.reshape(4, FF_SHARD, D_MODEL)
    return jnp.pad(w, ((0, 0), (0, FF_SHARD_P - FF_SHARD), (0, 0))).reshape(D_FF_P, D_MODEL)


def _ffn_out_to_scatter(dw):
    return dw.reshape(4, FF_SHARD_P, D_MODEL)[:, 0:FF_SHARD, :].reshape(N_DEV, FFO_SHARD, D_MODEL)


def _proj_from_gathered(g):
    w = g[:, :, 0:WIN_SHARD].transpose(1, 0, 2).reshape(D_MODEL, W_IN_COLS)
    qkv, f, gates = w[:, 0:2304], w[:, 2304:2312], w[:, 2312:]
    return jnp.concatenate([gates, qkv, jnp.pad(f, ((0, 0), (0, 128 - B_HEADS)))], axis=1)


def _proj_to_scatter(dw):
    w = jnp.concatenate([dw[:, P_QA:P_F], dw[:, P_F:P_F + B_HEADS], dw[:, 0:P_QA]], axis=1)
    w = w.reshape(D_MODEL, N_DEV, WIN_SHARD).transpose(1, 0, 2)
    return jnp.pad(w, ((0, 0), (0, 0), (0, WIN_SHARD_P - WIN_SHARD)))


def kernel(x, meta_tokens, ffn1_norm, ffn1_w_in, ffn1_w_out, mix_norm, w_in, b_forget, attn_sinks, w_branch_a, w_branch_b, w_out, ffn2_norm, ffn2_w_in, ffn2_w_out, final_norm, loss_target, m_meta_tokens, m_ffn1_norm, m_ffn1_w_in, m_ffn1_w_out, m_mix_norm, m_w_in, m_b_forget, m_attn_sinks, m_w_branch_a, m_w_branch_b, m_w_out, m_ffn2_norm, m_ffn2_w_in, m_ffn2_w_out, m_final_norm, v_meta_tokens, v_ffn1_norm, v_ffn1_w_in, v_ffn1_w_out, v_mix_norm, v_w_in, v_b_forget, v_attn_sinks, v_w_branch_a, v_w_branch_b, v_w_out, v_ffn2_norm, v_ffn2_w_in, v_ffn2_w_out, v_final_norm):
    me = 4 * lax.axis_index("x") + 2 * lax.axis_index("y") + lax.axis_index("c")

    shards = (
        _pad_to(ffn1_w_in[0].astype(BF16), D_MODEL, FF_SHARD_P),
        ffn1_w_out[0].astype(BF16),
        _pad_to(w_in[0].astype(BF16), D_MODEL, WIN_SHARD_P),
        w_branch_a[0].astype(BF16), w_branch_b[0].astype(BF16), w_out[0].astype(BF16),
        _pad_to(ffn2_w_in[0].astype(BF16), D_MODEL, FF_SHARD_P),
        ffn2_w_out[0].astype(BF16),
        meta_tokens,
    )
    g1i, g1o, gwi, gwa, gwb, gwo, g2i, g2o, gmeta = _exchange("gather_weights", shards, ())
    w1i = g1i.reshape(2, 4, D_MODEL, FF_SHARD_P)
    w2i = g2i.reshape(2, 4, D_MODEL, FF_SHARD_P)
    w1o = _ffn_out_from_gathered(g1o)
    w2o = _ffn_out_from_gathered(g2o)
    wi = _proj_from_gathered(gwi)
    wa = gwa.transpose(1, 0, 2).reshape(A_WIDTH, D_MODEL)
    wb = gwb.transpose(1, 0, 2).reshape(B_WIDTH, D_MODEL)
    wo = gwo.reshape(D_MODEL, D_MODEL)
    meta = gmeta.transpose(1, 0, 2).reshape(N_META, D_MODEL)
    gf = final_norm.reshape(1, D_MODEL)

    grad_x, (dw1i, dw1o, dwi, dwa, dwb, dwo, dw2i, dw2o), small = _local_step(
        x, loss_target, meta, ffn1_norm, mix_norm, ffn2_norm, gf, b_forget, attn_sinks, w1i, w1o, wi, wa, wb, wo, w2i, w2o)

    blocked = (
        dw1i.reshape(N_DEV, D_MODEL, FF_SHARD_P), _ffn_out_to_scatter(dw1o), _proj_to_scatter(dwi),
        dwa.reshape(A_WIDTH, N_DEV, 128).transpose(1, 0, 2), dwb.reshape(B_WIDTH, N_DEV, 128).transpose(1, 0, 2),
        dwo.reshape(N_DEV, 128, D_MODEL),
        dw2i.reshape(N_DEV, D_MODEL, FF_SHARD_P), _ffn_out_to_scatter(dw2o),
    )
    res = _exchange("exchange_grads", (small,), blocked)
    packs, r1i, r1o, rwi, rwa, rwb, rwo, r2i, r2o = res

    out = {}
    for nm, w, m, v, parts in (
        ("ffn1_w_in", ffn1_w_in, m_ffn1_w_in, v_ffn1_w_in, r1i), ("ffn1_w_out", ffn1_w_out, m_ffn1_w_out, v_ffn1_w_out, r1o),
        ("w_in", w_in, m_w_in, v_w_in, rwi), ("w_branch_a", w_branch_a, m_w_branch_a, v_w_branch_a, rwa),
        ("w_branch_b", w_branch_b, m_w_branch_b, v_w_branch_b, rwb), ("w_out", w_out, m_w_out, v_w_out, rwo),
        ("ffn2_w_in", ffn2_w_in, m_ffn2_w_in, v_ffn2_w_in, r2i), ("ffn2_w_out", ffn2_w_out, m_ffn2_w_out, v_ffn2_w_out, r2o),
    ):
        res4 = _adam("adam_" + nm, w[0], m[0], v[0], parts)
        out[nm] = tuple(r[None] for r in res4)

    tot = _small_sum("small_sum", packs)
    loss = tot[4, 2 * B_HEADS]
    g_meta = lax.dynamic_slice(tot[8:24, :], (0, me * 128), (N_META, 128))
    out["meta_tokens"] = tuple(_adam("adam_meta_tokens", meta_tokens, m_meta_tokens, v_meta_tokens, g_meta[None]))

    def pack_small(n1, nm, n2, nf, bfv, skv):
        misc = jnp.pad(jnp.concatenate([bfv, skv], axis=1), ((0, 0), (0, D_MODEL - 2 * B_HEADS)))
        row = lax.broadcasted_iota(jnp.int32, (8, D_MODEL), 0)
        vec = jnp.zeros((8, D_MODEL), F32)
        for i, piece in enumerate((n1, nm, n2, nf.reshape(1, D_MODEL), misc)):
            vec = jnp.where(row == i, piece, vec)
        return vec

    w_pack = pack_small(ffn1_norm, mix_norm, ffn2_norm, final_norm, b_forget, attn_sinks)
    m_pack = pack_small(m_ffn1_norm, m_mix_norm, m_ffn2_norm, m_final_norm, m_b_forget, m_attn_sinks)
    v_pack = pack_small(v_ffn1_norm, v_mix_norm, v_ffn2_norm, v_final_norm, v_b_forget, v_attn_sinks)
    small4 = _adam("adam_small", w_pack, m_pack, v_pack, tot[0:8][None])
    for i, nm in enumerate(("ffn1_norm", "mix_norm", "ffn2_norm")):
        out[nm] = tuple(r[i:i + 1] for r in small4)
    out["final_norm"] = tuple(r[3] for r in small4)
    out["b_forget"] = tuple(r[4:5, 0:B_HEADS] for r in small4)
    out["attn_sinks"] = tuple(r[4:5, B_HEADS:2 * B_HEADS] for r in small4)

    names = ("meta_tokens", "ffn1_norm", "ffn1_w_in", "ffn1_w_out", "mix_norm", "w_in", "b_forget", "attn_sinks",
             "w_branch_a", "w_branch_b", "w_out", "ffn2_norm", "ffn2_w_in", "ffn2_w_out", "final_norm")
    return (loss, grad_x) + tuple(out[nm][kind] for kind in range(4) for nm in names)
```

```python
import jax
import jax.numpy as jnp
from jax import lax
from jax.experimental import pallas as pl
from jax.experimental.pallas import tpu as pltpu

F32 = jnp.float32
BF16 = jnp.bfloat16

D_MODEL = 1024
N_META = 16
BLOCK = 128
PREFIX = 128
N_PAD = PREFIX - N_META
HEAD_DIM = 64
A_HEADS = 8
A_KV_HEADS = 2
A_GROUP = 4
B_HEADS = 8
A_WIDTH = 512
A_KV_WIDTH = 128
B_WIDTH = 512
D_FF = 2816
N_DEV = 8
FF_SHARD = 2 * D_FF // N_DEV
FF_SHARD_P = 768
FFO_SHARD = D_FF // N_DEV
FFO_SHARD_P = FF_SHARD_P // 2
D_FF_P = 4 * FF_SHARD_P
W_IN_COLS = 4360
WIN_SHARD = W_IN_COLS // N_DEV
WIN_SHARD_P = 640
P_GA, P_GB, P_QA, P_KA, P_VA, P_QB, P_KB, P_VB, P_F, PROJ_P = 0, 1024, 2048, 2560, 2688, 2816, 3328, 3840, 4352, 4480
EPS = 1e-6
NEG = -1e30
SCALE = HEAD_DIM ** -0.5
ADAM_LR = 0.001
ADAM_B1 = 0.9
ADAM_B2 = 0.999
ADAM_EPS = 1e-08
ADAM_WD = 0.01
ADAM_STEP = 10
VMEM_LIMIT = 48 * 1024 * 1024
MESH_ID = pl.DeviceIdType.MESH
SMALL_ROWS = 40

_NN = (((1,), (0,)), ((), ()))
_NT = (((1,), (1,)), ((), ()))
_TN = (((0,), (0,)), ((), ()))


def _params(sem=None):
    return pltpu.CompilerParams(dimension_semantics=sem, vmem_limit_bytes=VMEM_LIMIT)


def _pick(n, cands):
    for c in cands:
        if n % c == 0:
            return c
    raise ValueError(f"no tile for {n}")


def _bf(v):
    return v if v.dtype == BF16 else v.astype(BF16)


def _mm(name, a, b, dims, grid, a_spec, b_spec, o_spec, out_shape, out_dtype, acc_shape, k_axis=None, nk=1,
        alpha=1.0, res=None, res_spec=None):
    has_res = res is not None

    def body(*refs):
        a_ref, b_ref = refs[0], refs[1]
        r_ref = refs[2] if has_res else None
        o_ref = refs[3] if has_res else refs[2]

        def finish(acc):
            if alpha != 1.0:
                acc = acc * alpha
            if has_res:
                acc = acc + r_ref[...]
            o_ref[...] = acc.astype(o_ref.dtype)

        part = lax.dot_general(_bf(a_ref[...]), _bf(b_ref[...]), dims, preferred_element_type=F32)
        if nk == 1:
            finish(part)
        else:
            acc_ref = refs[-1]
            k = pl.program_id(k_axis)

            @pl.when(k == 0)
            def _():
                acc_ref[...] = part

            @pl.when(k > 0)
            def _():
                acc_ref[...] += part

            @pl.when(k == nk - 1)
            def _():
                finish(acc_ref[...])

    in_specs = [a_spec, b_spec] + ([res_spec] if has_res else [])
    args = (a, b) + ((res,) if has_res else ())
    sem = tuple("arbitrary" if (nk > 1 and i == k_axis) else "parallel" for i in range(len(grid)))
    return pl.pallas_call(
        body, name=name, grid=grid, in_specs=in_specs, out_specs=o_spec,
        out_shape=jax.ShapeDtypeStruct(out_shape, out_dtype),
        scratch_shapes=[pltpu.VMEM(acc_shape, F32)] if nk > 1 else [],
        compiler_params=_params(sem),
    )(*args)


def _mm_nn(name, a, b, out_dtype=F32, alpha=1.0, res=None, tn_c=(512, 640, 256, 128)):
    t, k = a.shape
    n = b.shape[1]
    tm = _pick(t, (1088, 768, 512, 256, 128))
    tn = _pick(n, tn_c)
    return _mm(name, a, b, _NN, (t // tm, n // tn),
               pl.BlockSpec((tm, k), lambda i, j: (i, 0)), pl.BlockSpec((k, tn), lambda i, j: (0, j)),
               pl.BlockSpec((tm, tn), lambda i, j: (i, j)), (t, n), out_dtype, None,
               alpha=alpha, res=res, res_spec=pl.BlockSpec((tm, tn), lambda i, j: (i, j)))


def _mm_nt(name, a, b, out_dtype=F32, alpha=1.0, tn_c=(768, 512, 256, 128), tk_c=None):
    t, k = a.shape
    n = b.shape[0]
    tm = _pick(t, (1088, 768, 512, 256, 128))
    tn = _pick(n, tn_c)
    tk = k if tk_c is None else _pick(k, tk_c)
    nk = k // tk
    return _mm(name, a, b, _NT, (t // tm, n // tn, nk),
               pl.BlockSpec((tm, tk), lambda i, j, kk: (i, kk)), pl.BlockSpec((tn, tk), lambda i, j, kk: (j, kk)),
               pl.BlockSpec((tm, tn), lambda i, j, kk: (i, j)), (t, n), out_dtype, (tm, tn), k_axis=2, nk=nk, alpha=alpha)


def _mm_tn(name, a, b, out_dtype=BF16, alpha=1.0, tm_c=(768, 512, 256, 128), tn_c=(512, 640, 256, 128)):
    t, m = a.shape
    n = b.shape[1]
    tm = _pick(m, tm_c)
    tn = _pick(n, tn_c)
    tk = _pick(t, (1088, 768, 512, 256, 128))
    nk = t // tk
    return _mm(name, a, b, _TN, (m // tm, n // tn, nk),
               pl.BlockSpec((tk, tm), lambda i, j, kk: (kk, i)), pl.BlockSpec((tk, tn), lambda i, j, kk: (kk, j)),
               pl.BlockSpec((tm, tn), lambda i, j, kk: (i, j)), (m, n), out_dtype, (tm, tn), k_axis=2, nk=nk, alpha=alpha)


def _ffn_in_fwd(name, n, wblk):
    t = n.shape[0]
    tm = _pick(t, (1088, 768, 512, 256, 128))
    return _mm(name, n, wblk, _NN, (t // tm, 2, 4),
               pl.BlockSpec((tm, D_MODEL), lambda i, s, j: (i, 0)),
               pl.BlockSpec((None, None, D_MODEL, FF_SHARD_P), lambda i, s, j: (s, j, 0, 0)),
               pl.BlockSpec((None, tm, FF_SHARD_P), lambda i, s, j: (s, i, j)), (2, t, D_FF_P), F32, None)


def _ffn_in_bwd_x(name, dgu, wblk):
    t = dgu.shape[1]
    tm = _pick(t, (1088, 768, 512, 256, 128))
    return _mm(name, dgu, wblk, _NT, (t // tm, 8),
               pl.BlockSpec((None, tm, FF_SHARD_P), lambda i, kk: (kk // 4, i, kk % 4)),
               pl.BlockSpec((None, None, D_MODEL, FF_SHARD_P), lambda i, kk: (kk // 4, kk % 4, 0, 0)),
               pl.BlockSpec((tm, D_MODEL), lambda i, kk: (i, 0)), (t, D_MODEL), F32, (tm, D_MODEL), k_axis=1, nk=8)


def _ffn_in_bwd_w(name, n, dgu):
    t = n.shape[0]
    tk = _pick(t, (1088, 768, 512, 256, 128))
    nk = t // tk
    return _mm(name, n, dgu, _TN, (2, 4, nk),
               pl.BlockSpec((tk, D_MODEL), lambda s, j, kk: (kk, 0)),
               pl.BlockSpec((None, tk, FF_SHARD_P), lambda s, j, kk: (s, kk, j)),
               pl.BlockSpec((None, None, D_MODEL, FF_SHARD_P), lambda s, j, kk: (s, j, 0, 0)),
               (2, 4, D_MODEL, FF_SHARD_P), BF16, (D_MODEL, FF_SHARD_P), k_axis=2, nk=nk)


def _rms_fwd(name, h, g):
    t = h.shape[0]
    tm = _pick(t, (544, 384, 256, 128))

    def body(h_ref, g_ref, o_ref):
        hv = h_ref[...]
        r = lax.rsqrt(jnp.mean(hv * hv, axis=-1, keepdims=True) + EPS)
        o_ref[...] = ((hv * r) * g_ref[...]).astype(BF16)

    return pl.pallas_call(
        body, name=name, grid=(t // tm,),
        in_specs=[pl.BlockSpec((tm, D_MODEL), lambda i: (i, 0)), pl.BlockSpec((1, D_MODEL), lambda i: (0, 0))],
        out_specs=pl.BlockSpec((tm, D_MODEL), lambda i: (i, 0)),
        out_shape=jax.ShapeDtypeStruct((t, D_MODEL), BF16), compiler_params=_params(("parallel",)),
    )(h, g)


def _rms_bwd(name, h, g, dn, dres):
    t = h.shape[0]
    tm = _pick(t, (544, 384, 256, 128))

    def body(h_ref, g_ref, dn_ref, dres_ref, dh_ref, dg_ref):
        i = pl.program_id(0)
        hv = h_ref[...]
        dnv = dn_ref[...]
        r = lax.rsqrt(jnp.mean(hv * hv, axis=-1, keepdims=True) + EPS)
        tv = dnv * g_ref[...]
        dot = jnp.mean(tv * hv, axis=-1, keepdims=True)
        dh_ref[...] = dres_ref[...] + (r * tv - hv * (r * r * r * dot))
        part = jnp.sum(dnv * (hv * r), axis=0, keepdims=True)

        @pl.when(i == 0)
        def _():
            dg_ref[...] = part

        @pl.when(i > 0)
        def _():
            dg_ref[...] += part

    row = pl.BlockSpec((tm, D_MODEL), lambda i: (i, 0))
    vec = pl.BlockSpec((1, D_MODEL), lambda i: (0, 0))
    return pl.pallas_call(
        body, name=name, grid=(t // tm,), in_specs=[row, vec, row, row], out_specs=[row, vec],
        out_shape=[jax.ShapeDtypeStruct((t, D_MODEL), F32), jax.ShapeDtypeStruct((1, D_MODEL), F32)],
        compiler_params=_params(("arbitrary",)),
    )(h, g, dn, dres)


def _swiglu_fwd(name, gu):
    t = gu.shape[1]
    tm = _pick(t, (544, 384, 256, 128))

    def body(gu_ref, a_ref):
        g = gu_ref[0]
        u = gu_ref[1]
        a_ref[...] = ((g * jax.nn.sigmoid(g)) * u).astype(BF16)

    return pl.pallas_call(
        body, name=name, grid=(t // tm, 4),
        in_specs=[pl.BlockSpec((2, tm, FF_SHARD_P), lambda i, j: (0, i, j))],
        out_specs=pl.BlockSpec((tm, FF_SHARD_P), lambda i, j: (i, j)),
        out_shape=jax.ShapeDtypeStruct((t, D_FF_P), BF16), compiler_params=_params(("parallel", "parallel")),
    )(gu)


def _swiglu_bwd(name, gu, da):
    t = gu.shape[1]
    tm = _pick(t, (544, 384, 256, 128))

    def body(gu_ref, da_ref, o_ref):
        g = gu_ref[0]
        u = gu_ref[1]
        d = da_ref[...]
        sg = jax.nn.sigmoid(g)
        o_ref[0] = (d * u * (sg * (1.0 + g * (1.0 - sg)))).astype(BF16)
        o_ref[1] = (d * (g * sg)).astype(BF16)

    return pl.pallas_call(
        body, name=name, grid=(t // tm, 4),
        in_specs=[pl.BlockSpec((2, tm, FF_SHARD_P), lambda i, j: (0, i, j)),
                  pl.BlockSpec((tm, FF_SHARD_P), lambda i, j: (i, j))],
        out_specs=pl.BlockSpec((2, tm, FF_SHARD_P), lambda i, j: (0, i, j)),
        out_shape=jax.ShapeDtypeStruct((2, t, D_FF_P), BF16), compiler_params=_params(("parallel", "parallel")),
    )(gu, da)


def _gate_fwd(name, proj, ya, yb):
    t = proj.shape[0]
    tm = _pick(t, (544, 384, 256, 128))
    tn = 512
    nb = D_MODEL // tn

    def body(ga_ref, gb_ref, ya_ref, yb_ref, o_ref):
        o_ref[...] = (jax.nn.sigmoid(ga_ref[...]) * ya_ref[...] + jax.nn.sigmoid(gb_ref[...]) * yb_ref[...]).astype(BF16)

    blk = pl.BlockSpec((tm, tn), lambda i, j: (i, j))
    return pl.pallas_call(
        body, name=name, grid=(t // tm, nb),
        in_specs=[pl.BlockSpec((tm, tn), lambda i, j: (i, P_GA // tn + j)),
                  pl.BlockSpec((tm, tn), lambda i, j: (i, P_GB // tn + j)), blk, blk],
        out_specs=blk, out_shape=jax.ShapeDtypeStruct((t, D_MODEL), BF16),
        compiler_params=_params(("parallel", "parallel")),
    )(proj, proj, ya, yb)


def _gate_bwd(name, proj, ya, yb, dmix):
    t = proj.shape[0]
    tm = _pick(t, (544, 384, 256, 128))
    tn = 512
    nb = D_MODEL // tn

    def body(ga_ref, gb_ref, ya_ref, yb_ref, dm_ref, dya_ref, dyb_ref, dga_ref, dgb_ref):
        dm = dm_ref[...]
        sa = jax.nn.sigmoid(ga_ref[...])
        sb = jax.nn.sigmoid(gb_ref[...])
        dya_ref[...] = (dm * sa).astype(BF16)
        dyb_ref[...] = (dm * sb).astype(BF16)
        dga_ref[...] = (dm * ya_ref[...] * (sa * (1.0 - sa))).astype(BF16)
        dgb_ref[...] = (dm * yb_ref[...] * (sb * (1.0 - sb))).astype(BF16)

    blk = pl.BlockSpec((tm, tn), lambda i, j: (i, j))
    out = jax.ShapeDtypeStruct((t, D_MODEL), BF16)
    return pl.pallas_call(
        body, name=name, grid=(t // tm, nb),
        in_specs=[pl.BlockSpec((tm, tn), lambda i, j: (i, P_GA // tn + j)),
                  pl.BlockSpec((tm, tn), lambda i, j: (i, P_GB // tn + j)), blk, blk, blk],
        out_specs=[blk, blk, blk, blk], out_shape=[out, out, out, out],
        compiler_params=_params(("parallel", "parallel")),
    )(proj, proj, ya, yb, dmix)


def _loss_head(name, h3, gf, tgt):
    b, l, _ = h3.shape
    nb = l // BLOCK

    def body(h_ref, g_ref, t_ref, dh_ref, loss_ref, dg_ref):
        first = (pl.program_id(0) == 0) & (pl.program_id(1) == 0)
        real = (pl.program_id(1) > 0).astype(F32)
        hv = h_ref[...]
        g = g_ref[...]
        r = lax.rsqrt(jnp.mean(hv * hv, axis=-1, keepdims=True) + EPS)
        xn = hv * r
        err = (xn * g - t_ref[...]) * real
        lpart = 0.5 * jnp.sum(jnp.mean(err * err, axis=-1, keepdims=True), axis=0, keepdims=True)
        dy = err * (1.0 / D_MODEL)
        tv = dy * g
        dot = jnp.mean(tv * hv, axis=-1, keepdims=True)
        dh_ref[...] = r * tv - hv * (r * r * r * dot)
        gpart = jnp.sum(dy * xn, axis=0, keepdims=True)

        @pl.when(first)
        def _():
            loss_ref[...] = jnp.zeros_like(loss_ref)
            dg_ref[...] = jnp.zeros_like(dg_ref)

        loss_ref[...] += jnp.broadcast_to(lpart, loss_ref.shape)
        dg_ref[...] += gpart

    return pl.pallas_call(
        body, name=name, grid=(b, nb),
        in_specs=[pl.BlockSpec((None, BLOCK, D_MODEL), lambda bi, n: (bi, n, 0)),
                  pl.BlockSpec((1, D_MODEL), lambda bi, n: (0, 0)),
                  pl.BlockSpec((None, BLOCK, D_MODEL), lambda bi, n: (bi, jnp.maximum(n - 1, 0), 0))],
        out_specs=[pl.BlockSpec((None, BLOCK, D_MODEL), lambda bi, n: (bi, n, 0)),
                   pl.BlockSpec((8, 128), lambda bi, n: (0, 0)),
                   pl.BlockSpec((1, D_MODEL), lambda bi, n: (0, 0))],
        out_shape=[jax.ShapeDtypeStruct(h3.shape, F32), jax.ShapeDtypeStruct((8, 128), F32),
                   jax.ShapeDtypeStruct((1, D_MODEL), F32)],
        compiler_params=_params(("arbitrary", "arbitrary")),
    )(h3, gf, tgt)


def _fgate_fwd(name, proj3, bf_row):
    b, l, _ = proj3.shape
    nb = l // BLOCK

    def body(f_ref, b_ref, c_ref):
        r_i = lax.broadcasted_iota(jnp.int32, (BLOCK, BLOCK), 0)
        c_i = lax.broadcasted_iota(jnp.int32, (BLOCK, BLOCK), 1)
        tri = (r_i >= c_i).astype(F32)
        carry = jnp.zeros((1, 128), F32)
        for blk in range(nb):
            rows = slice(blk * BLOCK, (blk + 1) * BLOCK)
            z = f_ref[rows, :] + b_ref[...]
            lf = jnp.minimum(z, 0.0) - jnp.log(1.0 + jnp.exp(-jnp.abs(z)))
            cb = jnp.dot(tri, lf, preferred_element_type=F32, precision=lax.Precision.HIGHEST) + carry
            c_ref[rows, :] = cb
            carry = cb[BLOCK - 1:BLOCK, :]

    return pl.pallas_call(
        body, name=name, grid=(b,),
        in_specs=[pl.BlockSpec((None, l, 128), lambda bi: (bi, 0, P_F // 128)),
                  pl.BlockSpec((1, 128), lambda bi: (0, 0))],
        out_specs=pl.BlockSpec((None, l, 128), lambda bi: (bi, 0, 0)),
        out_shape=jax.ShapeDtypeStruct((b, l, 128), F32), compiler_params=_params(("parallel",)),
    )(proj3, bf_row)


def _fgate_bwd(name, proj3, bf_row, dc):
    b, l, _ = proj3.shape
    nb = l // BLOCK

    def body(f_ref, b_ref, dc_ref, df_ref, db_ref):
        r_i = lax.broadcasted_iota(jnp.int32, (BLOCK, BLOCK), 0)
        c_i = lax.broadcasted_iota(jnp.int32, (BLOCK, BLOCK), 1)
        tri = (r_i <= c_i).astype(F32)
        carry = jnp.zeros((1, 128), F32)
        total = jnp.zeros((1, 128), F32)
        for blk in range(nb - 1, -1, -1):
            rows = slice(blk * BLOCK, (blk + 1) * BLOCK)
            rc = jnp.dot(tri, dc_ref[rows, :], preferred_element_type=F32, precision=lax.Precision.HIGHEST) + carry
            carry = rc[0:1, :]
            z = f_ref[rows, :] + b_ref[...]
            df = rc * (1.0 / (1.0 + jnp.exp(z)))
            df_ref[rows, :] = df
            total = total + jnp.sum(df, axis=0, keepdims=True)

        @pl.when(pl.program_id(0) == 0)
        def _():
            db_ref[...] = total

        @pl.when(pl.program_id(0) > 0)
        def _():
            db_ref[...] += total

    return pl.pallas_call(
        body, name=name, grid=(b,),
        in_specs=[pl.BlockSpec((None, l, 128), lambda bi: (bi, 0, P_F // 128)),
                  pl.BlockSpec((1, 128), lambda bi: (0, 0)),
                  pl.BlockSpec((None, l, 128), lambda bi: (bi, 0, 0))],
        out_specs=[pl.BlockSpec((None, l, 128), lambda bi: (bi, 0, 0)), pl.BlockSpec((1, 128), lambda bi: (0, 0))],
        out_shape=[jax.ShapeDtypeStruct((b, l, 128), F32), jax.ShapeDtypeStruct((1, 128), F32)],
        compiler_params=_params(("arbitrary",)),
    )(proj3, bf_row, dc)


def _swa_scores(q, kcat, n, slope):
    rows = A_GROUP * BLOCK
    r_i = lax.broadcasted_iota(jnp.int32, (rows, 3 * BLOCK), 0)
    c_i = lax.broadcasted_iota(jnp.int32, (rows, 3 * BLOCK), 1)
    qpos = n * BLOCK + (r_i & (BLOCK - 1))
    seg = c_i >> 7
    kpos = jnp.where(seg == 0, c_i, (n - 2) * BLOCK + c_i)
    dist = qpos - kpos
    s = lax.dot_general(q, kcat, _NT, preferred_element_type=F32) * SCALE
    s = s - slope * dist.astype(F32)
    is_meta = seg == 0
    band = jnp.logical_not(is_meta) & (dist < BLOCK) & (kpos >= PREFIX)
    meta = is_meta & (c_i >= N_PAD)
    return jnp.where((dist >= 0) & (band | meta), s, NEG)


def _swa_specs(l):
    q_spec = pl.BlockSpec((None, A_GROUP, BLOCK, HEAD_DIM), lambda g, b, n: (b, g, n, 0))
    k0 = pl.BlockSpec((None, None, BLOCK, HEAD_DIM), lambda g, b, n: (b, g, 0, 0))
    kp = pl.BlockSpec((None, None, BLOCK, HEAD_DIM), lambda g, b, n: (b, g, jnp.maximum(n - 1, 0), 0))
    kc = pl.BlockSpec((None, None, BLOCK, HEAD_DIM), lambda g, b, n: (b, g, n, 0))
    col = pl.BlockSpec((None, A_GROUP * BLOCK, 1), lambda g, b, n: (g, 0, 0))
    lse = pl.BlockSpec((None, A_GROUP, BLOCK, 1), lambda g, b, n: (b, g, n, 0))
    return q_spec, k0, kp, kc, col, lse


def _swa_fwd(name, q, k, v, slopes, sinks):
    b, _, l, _ = q.shape
    nb = l // BLOCK
    rows = A_GROUP * BLOCK

    def body(q_ref, k0_ref, kp_ref, kc_ref, v0_ref, vp_ref, vc_ref, sl_ref, sk_ref, o_ref, lse_ref):
        n = pl.program_id(2)
        qv = q_ref[...].reshape(rows, HEAD_DIM)
        kcat = jnp.concatenate([k0_ref[...], kp_ref[...], kc_ref[...]], axis=0)
        vcat = jnp.concatenate([v0_ref[...], vp_ref[...], vc_ref[...]], axis=0)
        s = _swa_scores(qv, kcat, n, sl_ref[...])
        sink = sk_ref[...]
        m = jnp.maximum(jnp.max(s, axis=-1, keepdims=True), sink)
        p = jnp.exp(s - m)
        den = jnp.sum(p, axis=-1, keepdims=True) + jnp.exp(sink - m)
        o = lax.dot_general(p.astype(BF16), vcat, _NN, preferred_element_type=F32) / den
        o_ref[...] = o.astype(BF16).reshape(A_GROUP, BLOCK, HEAD_DIM)
        lse_ref[...] = (m + jnp.log(den)).reshape(A_GROUP, BLOCK, 1)

    q_spec, k0, kp, kc, col, lse_spec = _swa_specs(l)
    return pl.pallas_call(
        body, name=name, grid=(A_KV_HEADS, b, nb),
        in_specs=[q_spec, k0, kp, kc, k0, kp, kc, col, col], out_specs=[q_spec, lse_spec],
        out_shape=[jax.ShapeDtypeStruct(q.shape, BF16), jax.ShapeDtypeStruct((b, A_HEADS, l, 1), F32)],
        compiler_params=_params(("parallel", "parallel", "parallel")),
    )(q, k, k, k, v, v, v, slopes, sinks)


def _swa_bwd(name, q, k, v, o, lse, do, slopes, sinks):
    b, _, l, _ = q.shape
    nb = l // BLOCK
    rows = A_GROUP * BLOCK

    def body(q_ref, k0_ref, kp_ref, kc_ref, v0_ref, vp_ref, vc_ref, o_ref, lse_ref, do_ref, sl_ref, sk_ref,
             dq_ref, dk_ref, dv_ref, ds_ref):
        bi = pl.program_id(1)
        n = pl.program_id(2)
        qv = q_ref[...].reshape(rows, HEAD_DIM)
        dov = do_ref[...].reshape(rows, HEAD_DIM)
        ov = o_ref[...].reshape(rows, HEAD_DIM)
        lsev = lse_ref[...].reshape(rows, 1)
        kcat = jnp.concatenate([k0_ref[...], kp_ref[...], kc_ref[...]], axis=0)
        vcat = jnp.concatenate([v0_ref[...], vp_ref[...], vc_ref[...]], axis=0)
        s = _swa_scores(qv, kcat, n, sl_ref[...])
        p = jnp.exp(s - lsev)
        dsum = jnp.sum(dov.astype(F32) * ov.astype(F32), axis=-1, keepdims=True)
        dp = lax.dot_general(dov, vcat, _NT, preferred_element_type=F32)
        dsc = (p * (dp - dsum)).astype(BF16)
        dq = lax.dot_general(dsc, kcat, _NN, preferred_element_type=F32) * SCALE
        dq_ref[...] = dq.astype(BF16).reshape(A_GROUP, BLOCK, HEAD_DIM)
        dkc = lax.dot_general(dsc, qv, _TN, preferred_element_type=F32) * SCALE
        dvc = lax.dot_general(p.astype(BF16), dov, _TN, preferred_element_type=F32)

        @pl.when(n == 0)
        def _():
            dk_ref[...] = jnp.zeros_like(dk_ref)
            dv_ref[...] = jnp.zeros_like(dv_ref)

        starts = (0, pl.multiple_of(jnp.maximum(n - 1, 0) * BLOCK, BLOCK), pl.multiple_of(n * BLOCK, BLOCK))
        for t, st in enumerate(starts):
            dk_ref[pl.ds(st, BLOCK), :] += dkc[t * BLOCK:(t + 1) * BLOCK, :]
            dv_ref[pl.ds(st, BLOCK), :] += dvc[t * BLOCK:(t + 1) * BLOCK, :]

        dsink = -(jnp.exp(sk_ref[...] - lsev) * dsum)
        r8 = lax.broadcasted_iota(jnp.int32, (8, 128), 0)
        acc = jnp.zeros((8, 128), F32)
        for hh in range(A_GROUP):
            acc = acc + jnp.where(r8 == hh, jnp.sum(dsink[hh * BLOCK:(hh + 1) * BLOCK, :]), 0.0)

        @pl.when((bi == 0) & (n == 0))
        def _():
            ds_ref[...] = jnp.zeros_like(ds_ref)

        ds_ref[...] += acc

    q_spec, k0, kp, kc, col, lse_spec = _swa_specs(l)
    full = pl.BlockSpec((None, None, l, HEAD_DIM), lambda g, bb, n: (bb, g, 0, 0))
    return pl.pallas_call(
        body, name=name, grid=(A_KV_HEADS, b, nb),
        in_specs=[q_spec, k0, kp, kc, k0, kp, kc, q_spec, lse_spec, q_spec, col, col],
        out_specs=[q_spec, full, full, pl.BlockSpec((None, 8, 128), lambda g, bb, n: (g, 0, 0))],
        out_shape=[jax.ShapeDtypeStruct(q.shape, BF16), jax.ShapeDtypeStruct(k.shape, F32),
                   jax.ShapeDtypeStruct(k.shape, F32), jax.ShapeDtypeStruct((A_KV_HEADS, 8, 128), F32)],
        compiler_params=_params(("arbitrary", "arbitrary", "arbitrary")),
    )(q, k, k, k, v, v, v, o, lse, do, slopes, sinks)


def _fox_scores(q, k, cq, ck, i):
    kh = k.shape[0]
    s = lax.dot_general(q, k, _NT, preferred_element_type=F32) * SCALE
    s = (s + cq) - ck
    qpos = i * BLOCK + lax.broadcasted_iota(jnp.int32, (BLOCK, kh), 0)
    kpos = lax.broadcasted_iota(jnp.int32, (BLOCK, kh), 1)
    return jnp.where((kpos <= qpos) & (kpos >= N_PAD), s, NEG)


def _fox_fwd(name, q, k, v, c_col, c_row):
    b, h, l, _ = q.shape
    nb = l // BLOCK

    def body(q_ref, k_ref, v_ref, cc_ref, cr_ref, o_ref, lse_ref):
        for i in range(nb):
            rows = slice(i * BLOCK, (i + 1) * BLOCK)
            kh = (i + 1) * BLOCK
            s = _fox_scores(q_ref[rows, :], k_ref[0:kh, :], cc_ref[rows, :], cr_ref[:, 0:kh], i)
            m = jnp.max(s, axis=-1, keepdims=True)
            p = jnp.exp(s - m)
            den = jnp.sum(p, axis=-1, keepdims=True)
            o = lax.dot_general(p.astype(BF16), v_ref[0:kh, :], _NN, preferred_element_type=F32) / den
            o_ref[rows, :] = o.astype(BF16)
            lse_ref[rows, :] = m + jnp.log(den)

    mat = pl.BlockSpec((None, None, l, HEAD_DIM), lambda bi, hi: (bi, hi, 0, 0))
    colv = pl.BlockSpec((None, None, l, 1), lambda bi, hi: (bi, hi, 0, 0))
    rowv = pl.BlockSpec((None, None, 1, l), lambda bi, hi: (bi, hi, 0, 0))
    return pl.pallas_call(
        body, name=name, grid=(b, h), in_specs=[mat, mat, mat, colv, rowv], out_specs=[mat, colv],
        out_shape=[jax.ShapeDtypeStruct(q.shape, BF16), jax.ShapeDtypeStruct((b, h, l, 1), F32)],
        compiler_params=_params(("parallel", "parallel")),
    )(q, k, v, c_col, c_row)


def _fox_bwd(name, q, k, v, c_col, c_row, o, lse, do):
    b, h, l, _ = q.shape
    nb = l // BLOCK

    def body(q_ref, k_ref, v_ref, cc_ref, cr_ref, o_ref, lse_ref, do_ref, dq_ref, dk_ref, dv_ref, dcq_ref, dck_ref):
        dk_ref[...] = jnp.zeros_like(dk_ref)
        dv_ref[...] = jnp.zeros_like(dv_ref)
        dck_ref[...] = jnp.zeros_like(dck_ref)
        for i in range(nb):
            rows = slice(i * BLOCK, (i + 1) * BLOCK)
            kh = (i + 1) * BLOCK
            qv = q_ref[rows, :]
            dov = do_ref[rows, :]
            kv = k_ref[0:kh, :]
            s = _fox_scores(qv, kv, cc_ref[rows, :], cr_ref[:, 0:kh], i)
            p = jnp.exp(s - lse_ref[rows, :])
            dsum = jnp.sum(dov.astype(F32) * o_ref[rows, :].astype(F32), axis=-1, keepdims=True)
            dp = lax.dot_general(dov, v_ref[0:kh, :], _NT, preferred_element_type=F32)
            ds = p * (dp - dsum)
            dsc = ds.astype(BF16)
            dq_ref[rows, :] = (lax.dot_general(dsc, kv, _NN, preferred_element_type=F32) * SCALE).astype(BF16)
            dk_ref[0:kh, :] += lax.dot_general(dsc, qv, _TN, preferred_element_type=F32) * SCALE
            dv_ref[0:kh, :] += lax.dot_general(p.astype(BF16), dov, _TN, preferred_element_type=F32)
            dcq_ref[rows, :] = jnp.sum(ds, axis=-1, keepdims=True)
            dck_ref[:, 0:kh] -= jnp.sum(ds, axis=0, keepdims=True)

    mat = pl.BlockSpec((None, None, l, HEAD_DIM), lambda bi, hi: (bi, hi, 0, 0))
    colv = pl.BlockSpec((None, None, l, 1), lambda bi, hi: (bi, hi, 0, 0))
    rowv = pl.BlockSpec((None, None, 1, l), lambda bi, hi: (bi, hi, 0, 0))
    return pl.pallas_call(
        body, name=name, grid=(b, h), in_specs=[mat, mat, mat, colv, rowv, mat, colv, mat],
        out_specs=[mat, mat, mat, colv, rowv],
        out_shape=[jax.ShapeDtypeStruct(q.shape, BF16), jax.ShapeDtypeStruct(q.shape, F32),
                   jax.ShapeDtypeStruct(q.shape, F32), jax.ShapeDtypeStruct((b, h, l, 1), F32),
                   jax.ShapeDtypeStruct((b, h, 1, l), F32)],
        compiler_params=_params(("parallel", "parallel")),
    )(q, k, v, c_col, c_row, o, lse, do)


_FLIPS = ((0, 0, 1), (0, 1, 0), (0, 1, 1), (1, 0, 0), (1, 0, 1), (1, 1, 0), (1, 1, 1))


def _exchange(name, gather, scatter):
    ng, ns = len(gather), len(scatter)
    na = ng + ns
    npeer = len(_FLIPS)

    def body(*refs):
        ins = refs[:na]
        outs = refs[na:2 * na]
        send_sems, recv_sems, loc_sems = refs[2 * na:]
        x, y, c = lax.axis_index("x"), lax.axis_index("y"), lax.axis_index("c")
        me = 4 * x + 2 * y + c
        peers = []
        for fx, fy, fc in _FLIPS:
            px = 1 - x if fx else x
            py = 1 - y if fy else y
            pc = 1 - c if fc else c
            peers.append(((px, py, pc), 4 * px + 2 * py + pc))

        def remote(a, kk):
            dev, lin = peers[kk]
            src = ins[a] if a < ng else ins[a].at[lin]
            return pltpu.make_async_remote_copy(src_ref=src, dst_ref=outs[a].at[me], send_sem=send_sems.at[a * npeer + kk],
                                                recv_sem=recv_sems.at[a * npeer + kk], device_id=dev, device_id_type=MESH_ID)

        def arrival(a, kk):
            dev, lin = peers[kk]
            src = ins[a] if a < ng else ins[a].at[lin]
            return pltpu.make_async_remote_copy(src_ref=src, dst_ref=outs[a].at[lin], send_sem=send_sems.at[a * npeer + kk],
                                                recv_sem=recv_sems.at[a * npeer + kk], device_id=dev, device_id_type=MESH_ID)

        local = []
        for a in range(na):
            src = ins[a] if a < ng else ins[a].at[me]
            cp = pltpu.make_async_copy(src, outs[a].at[me], loc_sems.at[a])
            cp.start()
            local.append(cp)
        sent = [remote(a, kk) for kk in range(npeer) for a in range(na)]
        for cp in sent:
            cp.start()
        for kk in range(npeer):
            for a in range(na):
                arrival(a, kk).wait_recv()
        for cp in sent:
            cp.wait_send()
        for cp in local:
            cp.wait()

    arrs = list(gather) + list(scatter)
    out_shape = [jax.ShapeDtypeStruct((N_DEV,) + tuple(a.shape), a.dtype) for a in gather]
    out_shape += [jax.ShapeDtypeStruct(tuple(a.shape), a.dtype) for a in scatter]
    anyspec = pl.BlockSpec(memory_space=pl.ANY)
    return pl.pallas_call(
        body, name=name, in_specs=[anyspec] * na, out_specs=[anyspec] * na, out_shape=out_shape,
        scratch_shapes=[pltpu.SemaphoreType.DMA((na * npeer,)), pltpu.SemaphoreType.DMA((na * npeer,)),
                        pltpu.SemaphoreType.DMA((na,))],
        compiler_params=pltpu.CompilerParams(has_side_effects=True),
    )(*arrs)


def _peer_table():
    x, y, c = lax.axis_index("x"), lax.axis_index("y"), lax.axis_index("c")
    me = 4 * x + 2 * y + c
    peers = []
    for fx, fy, fc in _FLIPS:
        px = 1 - x if fx else x
        py = 1 - y if fy else y
        pc = 1 - c if fc else c
        peers.append(((px, py, pc), 4 * px + 2 * py + pc))
    return me, peers


_HBM = pl.BlockSpec(memory_space=pltpu.HBM)
_SEM = pl.BlockSpec(memory_space=pltpu.SEMAPHORE)
_ANY = pl.BlockSpec(memory_space=pl.ANY)
_EFFECT = pltpu.SideEffectType.DATAFLOW_SIDE_EFFECTING


def _split_copy(srcs_are_pieces, src_refs, land_refs, send_sem, recv_sem, a, kk, me, peers, arriving):
    dev, lin = peers[kk]
    npeer = len(_FLIPS)
    src = src_refs[a] if srcs_are_pieces[a] else src_refs[a].at[lin]
    dst = land_refs[a].at[lin] if arriving else land_refs[a].at[me]
    return pltpu.make_async_remote_copy(src_ref=src, dst_ref=dst, send_sem=send_sem.at[a * npeer + kk],
                                        recv_sem=recv_sem.at[a * npeer + kk], device_id=dev, device_id_type=MESH_ID)


def _xchg_start(name, gather, scatter, after=None):
    me_out = 4 * lax.axis_index("x") + 2 * lax.axis_index("y") + lax.axis_index("c")
    srcs = list(gather) + list(scatter)
    is_piece = [True] * len(gather) + [False] * len(scatter)
    lands = []
    for a, piece in zip(srcs, is_piece):
        own = a[None] if piece else lax.dynamic_slice_in_dim(a, me_out, 1, axis=0)
        shape = ((N_DEV,) + tuple(a.shape)) if piece else tuple(a.shape)
        start = (me_out,) + (0,) * (len(shape) - 1)
        lands.append(lax.dynamic_update_slice(jnp.zeros(shape, a.dtype), own, start))
    n = len(srcs)
    nsem = n * len(_FLIPS)
    has_after = after is not None

    def body(*refs):
        src_refs = refs[:n]
        land_refs = refs[n:2 * n]
        outs = refs[2 * n + (1 if has_after else 0):]
        send_sem, recv_sem = outs[0], outs[1]
        token = outs[-1]
        me, peers = _peer_table()
        for kk in range(len(_FLIPS)):
            for a in range(n):
                _split_copy(is_piece, src_refs, land_refs, send_sem, recv_sem, a, kk, me, peers, False).start()
        token[...] = jnp.zeros_like(token)

    out_shape = ([pltpu.SemaphoreType.DMA((nsem,)), pltpu.SemaphoreType.DMA((nsem,))]
                 + [pltpu.HBM(tuple(a.shape), a.dtype) for a in srcs] + [pltpu.HBM(tuple(a.shape), a.dtype) for a in lands]
                 + [jax.ShapeDtypeStruct((8, 128), F32)])
    args = [pltpu.with_memory_space_constraint(a, pltpu.HBM) for a in srcs + lands] + ([after] if has_after else [])
    res = pl.pallas_call(
        body, name=name, out_shape=out_shape,
        in_specs=[_HBM] * (2 * n) + ([_ANY] if has_after else []),
        out_specs=[_SEM, _SEM] + [_HBM] * (2 * n) + [pl.BlockSpec(memory_space=pltpu.VMEM)],
        input_output_aliases={i: 2 + i for i in range(2 * n)},
        compiler_params=pltpu.CompilerParams(has_side_effects=_EFFECT),
    )(*args)
    state = (res[0], res[1], list(res[2:2 + n]), list(res[2 + n:2 + 2 * n]), is_piece)
    return state, res[-1]


def _xchg_wait(name, state, after):
    send_sem, recv_sem, srcs, lands, is_piece = state
    n = len(srcs)

    def body(*refs):
        src_refs = refs[:n]
        land_refs = refs[n:2 * n]
        s_sem, r_sem = refs[2 * n], refs[2 * n + 1]
        me, peers = _peer_table()
        for kk in range(len(_FLIPS)):
            for a in range(n):
                cp = _split_copy(is_piece, src_refs, land_refs, s_sem, r_sem, a, kk, me, peers, True)
                cp.wait_send()
                cp.wait_recv()

    out_shape = [pltpu.HBM(tuple(a.shape), a.dtype) for a in srcs] + [pltpu.HBM(tuple(a.shape), a.dtype) for a in lands]
    res = pl.pallas_call(
        body, name=name, out_shape=out_shape,
        in_specs=[_HBM] * (2 * n) + [_SEM, _SEM, _ANY], out_specs=[_HBM] * (2 * n),
        input_output_aliases={i: i for i in range(2 * n)},
        compiler_params=pltpu.CompilerParams(has_side_effects=_EFFECT),
    )(*srcs, *lands, send_sem, recv_sem, after)
    return list(res[n:])


def _tie(v, token):
    return lax.optimization_barrier((v, token))[0]


def _adam_math(w, g, m, v):
    m = ADAM_B1 * m + (1.0 - ADAM_B1) * g
    v = ADAM_B2 * v + (1.0 - ADAM_B2) * (g * g)
    m_hat = m / (1.0 - ADAM_B1 ** ADAM_STEP)
    v_hat = v / (1.0 - ADAM_B2 ** ADAM_STEP)
    delta = -ADAM_LR * (m_hat / (jnp.sqrt(v_hat) + ADAM_EPS) + ADAM_WD * w)
    return delta, m, v


def _adam(name, w, m, v, parts):
    r, c = w.shape
    npart, _, cp = parts.shape
    tr = _pick(r, (256, 176, 128, 64, 16, 8, 1))

    def body(w_ref, m_ref, v_ref, p_ref, g_ref, d_ref, mo_ref, vo_ref):
        g = p_ref[0].astype(F32)
        for pp in range(1, npart):
            g = g + p_ref[pp].astype(F32)
        g = g[:, 0:c]
        delta, mn, vn = _adam_math(w_ref[...], g, m_ref[...], v_ref[...])
        g_ref[...] = g
        d_ref[...] = delta
        mo_ref[...] = mn
        vo_ref[...] = vn

    blk = pl.BlockSpec((tr, c), lambda i: (i, 0))
    out = jax.ShapeDtypeStruct((r, c), F32)
    return pl.pallas_call(
        body, name=name, grid=(r // tr,),
        in_specs=[blk, blk, blk, pl.BlockSpec((npart, tr, cp), lambda i: (0, i, 0))],
        out_specs=[blk, blk, blk, blk], out_shape=[out, out, out, out], compiler_params=_params(("parallel",)),
    )(w, m, v, parts)


def _small_sum(name, packs):
    def body(p_ref, o_ref):
        tot = p_ref[0]
        for pp in range(1, N_DEV):
            tot = tot + p_ref[pp]
        o_ref[0:8, :] = tot[0:8, :]
        o_ref[8:24, :] = tot[8:24, :] + tot[24:40, :]

    return pl.pallas_call(body, name=name, out_shape=jax.ShapeDtypeStruct((24, D_MODEL), F32),
                          compiler_params=_params())(packs)


def _heads(t2d, b, l, nh):
    return t2d.reshape(b, l, nh, HEAD_DIM).transpose(0, 2, 1, 3)


def _unheads(t4d):
    b, nh, l, _ = t4d.shape
    return t4d.transpose(0, 2, 1, 3).reshape(b * l, nh * HEAD_DIM)


def _local_step(x, tgt, g1, gm, g2, gf, b_forget, sinks, weights, send):
    b, s, _ = x.shape
    l = s + PREFIX
    t = b * l
    w1i, w1o, meta = weights("ffn1", x)
    h0 = jnp.concatenate([jnp.zeros((b, N_PAD, D_MODEL), F32), jnp.broadcast_to(meta[None], (b, N_META, D_MODEL)), x],
                         axis=1).reshape(t, D_MODEL)

    n1 = _rms_fwd("rms1_fwd", h0, g1)
    gu1 = _ffn_in_fwd("ffn1_in_fwd", n1, w1i)
    a1 = _swiglu_fwd("swiglu1_fwd", gu1)
    h1 = _mm_nn("ffn1_out_fwd", a1, w1o, alpha=0.5, res=h0)
    wi, wa, wb, wo = weights("mix", h1)
    um = _rms_fwd("rmsm_fwd", h1, gm)
    proj = _mm_nn("proj_fwd", um, wi, tn_c=(640,))
    proj3 = proj.reshape(b, l, PROJ_P)
    bf_row = jnp.pad(b_forget, ((0, 0), (0, 128 - B_HEADS)))
    cfull = _fgate_fwd("fgate_fwd", proj3, bf_row)
    c8 = cfull[:, :, 0:B_HEADS].transpose(0, 2, 1)
    c_col = c8[..., None]
    c_row = c8[:, :, None, :]
    qa = _heads(proj[:, P_QA:P_KA].astype(BF16), b, l, A_HEADS)
    ka = _heads(proj[:, P_KA:P_VA].astype(BF16), b, l, A_KV_HEADS)
    va = _heads(proj[:, P_VA:P_QB].astype(BF16), b, l, A_KV_HEADS)
    qb = _heads(proj[:, P_QB:P_KB].astype(BF16), b, l, B_HEADS)
    kb = _heads(proj[:, P_KB:P_VB].astype(BF16), b, l, B_HEADS)
    vb = _heads(proj[:, P_VB:P_F].astype(BF16), b, l, B_HEADS)
    head_of_row = jnp.arange(A_HEADS * BLOCK) // BLOCK
    slopes = jnp.exp2(-8.0 * (head_of_row + 1).astype(F32) / A_HEADS).reshape(A_KV_HEADS, A_GROUP * BLOCK, 1)
    sink_rows = jnp.repeat(sinks.reshape(A_HEADS), BLOCK).reshape(A_KV_HEADS, A_GROUP * BLOCK, 1)
    oa4, lse_a = _swa_fwd("swa_fwd", qa, ka, va, slopes, sink_rows)
    ob4, lse_b = _fox_fwd("fox_fwd", qb, kb, vb, c_col, c_row)
    oa = _unheads(oa4)
    ob = _unheads(ob4)
    ya = _mm_nn("branch_a_fwd", oa, wa)
    yb = _mm_nn("branch_b_fwd", ob, wb)
    mixed = _gate_fwd("gate_fwd", proj, ya, yb)
    h2 = _mm_nn("mix_out_fwd", mixed, wo, res=h1)
    w2i, w2o = weights("ffn2", h2)
    n2 = _rms_fwd("rms2_fwd", h2, g2)
    gu2 = _ffn_in_fwd("ffn2_in_fwd", n2, w2i)
    a2 = _swiglu_fwd("swiglu2_fwd", gu2)
    h3 = _mm_nn("ffn2_out_fwd", a2, w2o, alpha=0.5, res=h2)

    dh3_3, loss_blk, dgf = _loss_head("loss_head", h3.reshape(b, l, D_MODEL), gf, tgt)
    dh3 = dh3_3.reshape(t, D_MODEL)

    def tie(v, token):
        return v if token is None else _tie(v, token)

    def ffn_bwd(tag, dh, h_in, g_norm, n_in, gu, a, w_in_blk, w_out):
        dw_out = _mm_tn(tag + "_out_bwd_w", a, dh, alpha=0.5)
        dh = tie(dh, send(tag + "_out", (dw_out,)))
        da = _mm_nt(tag + "_out_bwd_x", dh, w_out, alpha=0.5)
        dgu = _swiglu_bwd("swiglu" + tag[-1] + "_bwd", gu, da)
        dw_in = _ffn_in_bwd_w(tag + "_in_bwd_w", n_in, dgu)
        dgu = tie(dgu, send(tag + "_in", (dw_in,)))
        dn = _ffn_in_bwd_x(tag + "_in_bwd_x", dgu, w_in_blk)
        dh_in, dg = _rms_bwd("rms" + tag[-1] + "_bwd", h_in, g_norm, dn, dh)
        return dh_in, dg

    dh2, dg2 = ffn_bwd("ffn2", dh3, h2, g2, n2, gu2, a2, w2i, w2o)

    dmix = _mm_nt("mix_out_bwd_x", dh2, wo, tn_c=(512,))
    dwo = _mm_tn("mix_out_bwd_w", mixed, dh2)
    dya, dyb, dga, dgb = _gate_bwd("gate_bwd", proj, ya, yb, dmix)
    doa = _mm_nt("branch_a_bwd_x", dya, wa, out_dtype=BF16, tn_c=(512,))
    dob = _mm_nt("branch_b_bwd_x", dyb, wb, out_dtype=BF16, tn_c=(512,))
    dwa = _mm_tn("branch_a_bwd_w", oa, dya, tm_c=(512,))
    dwb = _mm_tn("branch_b_bwd_w", ob, dyb, tm_c=(512,))
    dqa, dka, dva, dsink = _swa_bwd("swa_bwd", qa, ka, va, oa4, lse_a, _heads(doa, b, l, A_HEADS), slopes, sink_rows)
    dqb, dkb, dvb, dcq, dck = _fox_bwd("fox_bwd", qb, kb, vb, c_col, c_row, ob4, lse_b, _heads(dob, b, l, B_HEADS))
    dc = (dcq[..., 0] + dck[:, :, 0, :]).transpose(0, 2, 1)
    dc = jnp.pad(dc, ((0, 0), (0, 0), (0, 128 - B_HEADS)))
    df3, dbf = _fgate_bwd("fgate_bwd", proj3, bf_row, dc)
    dproj = jnp.concatenate(
        [dga, dgb, _unheads(dqa), _unheads(dka).astype(BF16), _unheads(dva).astype(BF16), _unheads(dqb),
         _unheads(dkb).astype(BF16), _unheads(dvb).astype(BF16), df3.reshape(t, 128).astype(BF16)], axis=1)
    dwi = _mm_tn("proj_bwd_w", um, dproj, tm_c=(512,), tn_c=(640,))
    dproj = tie(dproj, send("mix", (dwi, dwa, dwb, dwo)))
    dum = _mm_nt("proj_bwd_x", dproj, wi, tn_c=(512,), tk_c=(640,))
    dh1, dgm = _rms_bwd("rmsm_bwd", h1, gm, dum, dh2)

    dh0, dg1 = ffn_bwd("ffn1", dh1, h0, g1, n1, gu1, a1, w1i, w1o)
    dh0_3 = dh0.reshape(b, l, D_MODEL)
    grad_x = dh0_3[:, PREFIX:, :]
    dmeta = dh0_3[:, N_PAD:PREFIX, :].reshape(b * N_META, D_MODEL)

    misc = jnp.concatenate([dbf[:, 0:B_HEADS], dsink[:, 0:A_GROUP, 0].reshape(1, A_HEADS), loss_blk[0:1, 0:1]], axis=1)
    misc = jnp.pad(misc, ((0, 0), (0, D_MODEL - misc.shape[1])))
    row = lax.broadcasted_iota(jnp.int32, (8, D_MODEL), 0)
    vec = jnp.zeros((8, D_MODEL), F32)
    for i, piece in enumerate((dg1, dgm, dg2, dgf, misc)):
        vec = jnp.where(row == i, piece, vec)
    small = jnp.concatenate([vec, dmeta], axis=0)
    return grad_x, small


def _pad_to(a, rows, cols):
    return jnp.pad(a, ((0, rows - a.shape[0]), (0, cols - a.shape[1])))


def _ffn_out_from_gathered(g):
    w = g.reshape(4, FF_SHARD, D_MODEL)
    return jnp.pad(w, ((0, 0), (0, FF_SHARD_P - FF_SHARD), (0, 0))).reshape(D_FF_P, D_MODEL)


def _ffn_out_to_scatter(dw):
    return dw.reshape(4, FF_SHARD_P, D_MODEL)[:, 0:FF_SHARD, :].reshape(N_DEV, FFO_SHARD, D_MODEL)


def _proj_from_gathered(g):
    w = g[:, :, 0:WIN_SHARD].transpose(1, 0, 2).reshape(D_MODEL, W_IN_COLS)
    qkv, f, gates = w[:, 0:2304], w[:, 2304:2312], w[:, 2312:]
    return jnp.concatenate([gates, qkv, jnp.pad(f, ((0, 0), (0, 128 - B_HEADS)))], axis=1)


def _proj_to_scatter(dw):
    w = jnp.concatenate([dw[:, P_QA:P_F], dw[:, P_F:P_F + B_HEADS], dw[:, 0:P_QA]], axis=1)
    w = w.reshape(D_MODEL, N_DEV, WIN_SHARD).transpose(1, 0, 2)
    return jnp.pad(w, ((0, 0), (0, 0), (0, WIN_SHARD_P - WIN_SHARD)))


def kernel(x, meta_tokens, ffn1_norm, ffn1_w_in, ffn1_w_out, mix_norm, w_in, b_forget, attn_sinks, w_branch_a, w_branch_b, w_out, ffn2_norm, ffn2_w_in, ffn2_w_out, final_norm, loss_target, m_meta_tokens, m_ffn1_norm, m_ffn1_w_in, m_ffn1_w_out, m_mix_norm, m_w_in, m_b_forget, m_attn_sinks, m_w_branch_a, m_w_branch_b, m_w_out, m_ffn2_norm, m_ffn2_w_in, m_ffn2_w_out, m_final_norm, v_meta_tokens, v_ffn1_norm, v_ffn1_w_in, v_ffn1_w_out, v_mix_norm, v_w_in, v_b_forget, v_attn_sinks, v_w_branch_a, v_w_branch_b, v_w_out, v_ffn2_norm, v_ffn2_w_in, v_ffn2_w_out, v_final_norm):
    me = 4 * lax.axis_index("x") + 2 * lax.axis_index("y") + lax.axis_index("c")

    shards = (
        _pad_to(ffn1_w_in[0].astype(BF16), D_MODEL, FF_SHARD_P),
        ffn1_w_out[0].astype(BF16),
        _pad_to(w_in[0].astype(BF16), D_MODEL, WIN_SHARD_P),
        w_branch_a[0].astype(BF16), w_branch_b[0].astype(BF16), w_out[0].astype(BF16),
        _pad_to(ffn2_w_in[0].astype(BF16), D_MODEL, FF_SHARD_P),
        ffn2_w_out[0].astype(BF16),
        meta_tokens,
    )
    s1i, s1o, swi, swa, swb, swo, s2i, s2o, smeta = shards
    gather_state = {}
    gather_state["ffn1"], tok = _xchg_start("gather_ffn1_start", (s1i, s1o, smeta), ())
    gather_state["mix"], tok = _xchg_start("gather_mix_start", (swi, swa, swb, swo), (), after=tok)
    gather_state["ffn2"], tok = _xchg_start("gather_ffn2_start", (s2i, s2o), (), after=tok)
    started = {"tok": tok}

    def weights(group, after):
        if group == "ffn1":
            after = started["tok"]
        got = _xchg_wait("gather_" + group + "_wait", gather_state[group], after)
        if group == "mix":
            gwi, gwa, gwb, gwo = got
            return (_proj_from_gathered(gwi), gwa.transpose(1, 0, 2).reshape(A_WIDTH, D_MODEL),
                    gwb.transpose(1, 0, 2).reshape(B_WIDTH, D_MODEL), gwo.reshape(D_MODEL, D_MODEL))
        w_in_blk = got[0].reshape(2, 4, D_MODEL, FF_SHARD_P)
        w_out_p = _ffn_out_from_gathered(got[1])
        if group == "ffn1":
            return w_in_blk, w_out_p, got[2].transpose(1, 0, 2).reshape(N_META, D_MODEL)
        return w_in_blk, w_out_p

    scatter_state = {}

    def send(group, grads):
        if group == "mix":
            dwi, dwa, dwb, dwo = grads
            blocks = (_proj_to_scatter(dwi), dwa.reshape(A_WIDTH, N_DEV, 128).transpose(1, 0, 2),
                      dwb.reshape(B_WIDTH, N_DEV, 128).transpose(1, 0, 2), dwo.reshape(N_DEV, 128, D_MODEL))
        elif group.endswith("_in"):
            blocks = (grads[0].reshape(N_DEV, D_MODEL, FF_SHARD_P),)
        else:
            blocks = (_ffn_out_to_scatter(grads[0]),)
        scatter_state[group], token = _xchg_start("scatter_" + group + "_start", (), blocks)
        return token

    gf = final_norm.reshape(1, D_MODEL)
    grad_x, small = _local_step(x, loss_target, ffn1_norm, mix_norm, ffn2_norm, gf, b_forget, attn_sinks, weights, send)

    (packs,) = _exchange("gather_small", (small,), ())
    recv = {}
    after = packs
    for group in ("ffn2_out", "ffn2_in", "mix", "ffn1_out", "ffn1_in"):
        recv[group] = _xchg_wait("scatter_" + group + "_wait", scatter_state[group], after)
        after = recv[group][0]
    (r2o,), (r2i,), (rwi, rwa, rwb, rwo), (r1o,), (r1i,) = (recv[g] for g in ("ffn2_out", "ffn2_in", "mix", "ffn1_out", "ffn1_in"))

    out = {}
    for nm, w, m, v, parts in (
        ("ffn1_w_in", ffn1_w_in, m_ffn1_w_in, v_ffn1_w_in, r1i), ("ffn1_w_out", ffn1_w_out, m_ffn1_w_out, v_ffn1_w_out, r1o),
        ("w_in", w_in, m_w_in, v_w_in, rwi), ("w_branch_a", w_branch_a, m_w_branch_a, v_w_branch_a, rwa),
        ("w_branch_b", w_branch_b, m_w_branch_b, v_w_branch_b, rwb), ("w_out", w_out, m_w_out, v_w_out, rwo),
        ("ffn2_w_in", ffn2_w_in, m_ffn2_w_in, v_ffn2_w_in, r2i), ("ffn2_w_out", ffn2_w_out, m_ffn2_w_out, v_ffn2_w_out, r2o),
    ):
        res4 = _adam("adam_" + nm, w[0], m[0], v[0], parts)
        out[nm] = tuple(r[None] for r in res4)

    tot = _small_sum("small_sum", packs)
    loss = tot[4, 2 * B_HEADS]
    g_meta = lax.dynamic_slice(tot[8:24, :], (0, me * 128), (N_META, 128))
    out["meta_tokens"] = tuple(_adam("adam_meta_tokens", meta_tokens, m_meta_tokens, v_meta_tokens, g_meta[None]))

    def pack_small(n1, nm, n2, nf, bfv, skv):
        misc = jnp.pad(jnp.concatenate([bfv, skv], axis=1), ((0, 0), (0, D_MODEL - 2 * B_HEADS)))
        row = lax.broadcasted_iota(jnp.int32, (8, D_MODEL), 0)
        vec = jnp.zeros((8, D_MODEL), F32)
        for i, piece in enumerate((n1, nm, n2, nf.reshape(1, D_MODEL), misc)):
            vec = jnp.where(row == i, piece, vec)
        return vec

    w_pack = pack_small(ffn1_norm, mix_norm, ffn2_norm, final_norm, b_forget, attn_sinks)
    m_pack = pack_small(m_ffn1_norm, m_mix_norm, m_ffn2_norm, m_final_norm, m_b_forget, m_attn_sinks)
    v_pack = pack_small(v_ffn1_norm, v_mix_norm, v_ffn2_norm, v_final_norm, v_b_forget, v_attn_sinks)
    small4 = _adam("adam_small", w_pack, m_pack, v_pack, tot[0:8][None])
    for i, nm in enumerate(("ffn1_norm", "mix_norm", "ffn2_norm")):
        out[nm] = tuple(r[i:i + 1] for r in small4)
    out["final_norm"] = tuple(r[3] for r in small4)
    out["b_forget"] = tuple(r[4:5, 0:B_HEADS] for r in small4)
    out["attn_sinks"] = tuple(r[4:5, B_HEADS:2 * B_HEADS] for r in small4)

    names = ("meta_tokens", "ffn1_norm", "ffn1_w_in", "ffn1_w_out", "mix_norm", "w_in", "b_forget", "attn_sinks",
             "w_branch_a", "w_branch_b", "w_out", "ffn2_norm", "ffn2_w_in", "ffn2_w_out", "final_norm")
    return (loss, grad_x) + tuple(out[nm][kind] for kind in range(4) for nm in names)
```

```python
import jax
import jax.numpy as jnp
from jax import lax
from jax.experimental import pallas as pl
from jax.experimental.pallas import tpu as pltpu

F32 = jnp.float32
BF16 = jnp.bfloat16

D_MODEL = 1024
N_META = 16
BLOCK = 128
PREFIX = 128
N_PAD = PREFIX - N_META
HEAD_DIM = 64
A_HEADS = 8
A_KV_HEADS = 2
A_GROUP = 4
B_HEADS = 8
A_WIDTH = 512
A_KV_WIDTH = 128
B_WIDTH = 512
D_FF = 2816
N_DEV = 8
FF_SHARD = 2 * D_FF // N_DEV
FF_SHARD_P = 768
FFO_SHARD = D_FF // N_DEV
FFO_SHARD_P = FF_SHARD_P // 2
D_FF_P = 4 * FF_SHARD_P
W_IN_COLS = 4360
WIN_SHARD = W_IN_COLS // N_DEV
WIN_SHARD_P = 640
P_GA, P_GB, P_QA, P_KA, P_VA, P_QB, P_KB, P_VB, P_F, PROJ_P = 0, 1024, 2048, 2560, 2688, 2816, 3328, 3840, 4352, 4480
EPS = 1e-6
NEG = -1e30
SCALE = HEAD_DIM ** -0.5
ADAM_LR = 0.001
ADAM_B1 = 0.9
ADAM_B2 = 0.999
ADAM_EPS = 1e-08
ADAM_WD = 0.01
ADAM_STEP = 10
VMEM_LIMIT = 48 * 1024 * 1024
MESH_ID = pl.DeviceIdType.MESH
SMALL_ROWS = 40

_NN = (((1,), (0,)), ((), ()))
_NT = (((1,), (1,)), ((), ()))
_TN = (((0,), (0,)), ((), ()))


def _params(sem=None):
    return pltpu.CompilerParams(dimension_semantics=sem, vmem_limit_bytes=VMEM_LIMIT)


def _pick(n, cands):
    for c in cands:
        if n % c == 0:
            return c
    raise ValueError(f"no tile for {n}")


def _bf(v):
    return v if v.dtype == BF16 else v.astype(BF16)


def _mm(name, a, b, dims, grid, a_spec, b_spec, o_spec, out_shape, out_dtype, acc_shape, k_axis=None, nk=1,
        alpha=1.0, res=None, res_spec=None, dep=None):
    has_res = res is not None
    has_dep = dep is not None

    def body(*refs):
        a_ref, b_ref = refs[0], refs[1]
        r_ref = refs[2] if has_res else None
        o_ref = refs[2 + has_res + has_dep]

        def finish(acc):
            if alpha != 1.0:
                acc = acc * alpha
            if has_res:
                acc = acc + r_ref[...]
            o_ref[...] = acc.astype(o_ref.dtype)

        part = lax.dot_general(_bf(a_ref[...]), _bf(b_ref[...]), dims, preferred_element_type=F32)
        if nk == 1:
            finish(part)
        else:
            acc_ref = refs[-1]
            k = pl.program_id(k_axis)

            @pl.when(k == 0)
            def _():
                acc_ref[...] = part

            @pl.when(k > 0)
            def _():
                acc_ref[...] += part

            @pl.when(k == nk - 1)
            def _():
                finish(acc_ref[...])

    in_specs = [a_spec, b_spec] + ([res_spec] if has_res else []) + ([pl.BlockSpec(memory_space=pl.ANY)] if has_dep else [])
    args = (a, b) + ((res,) if has_res else ()) + ((dep,) if has_dep else ())
    sem = tuple("arbitrary" if (nk > 1 and i == k_axis) else "parallel" for i in range(len(grid)))
    return pl.pallas_call(
        body, name=name, grid=grid, in_specs=in_specs, out_specs=o_spec,
        out_shape=jax.ShapeDtypeStruct(out_shape, out_dtype),
        scratch_shapes=[pltpu.VMEM(acc_shape, F32)] if nk > 1 else [],
        compiler_params=_params(sem),
    )(*args)


def _mm_nn(name, a, b, out_dtype=F32, alpha=1.0, res=None, tn_c=(512, 640, 256, 128)):
    t, k = a.shape
    n = b.shape[1]
    tm = _pick(t, (1088, 768, 512, 256, 128))
    tn = _pick(n, tn_c)
    return _mm(name, a, b, _NN, (t // tm, n // tn),
               pl.BlockSpec((tm, k), lambda i, j: (i, 0)), pl.BlockSpec((k, tn), lambda i, j: (0, j)),
               pl.BlockSpec((tm, tn), lambda i, j: (i, j)), (t, n), out_dtype, None,
               alpha=alpha, res=res, res_spec=pl.BlockSpec((tm, tn), lambda i, j: (i, j)))


def _mm_nt(name, a, b, out_dtype=F32, alpha=1.0, tn_c=(768, 512, 256, 128), tk_c=None, dep=None):
    t, k = a.shape
    n = b.shape[0]
    tm = _pick(t, (1088, 768, 512, 256, 128))
    tn = _pick(n, tn_c)
    tk = k if tk_c is None else _pick(k, tk_c)
    nk = k // tk
    return _mm(name, a, b, _NT, (t // tm, n // tn, nk),
               pl.BlockSpec((tm, tk), lambda i, j, kk: (i, kk)), pl.BlockSpec((tn, tk), lambda i, j, kk: (j, kk)),
               pl.BlockSpec((tm, tn), lambda i, j, kk: (i, j)), (t, n), out_dtype, (tm, tn), k_axis=2, nk=nk, alpha=alpha,
               dep=dep)


def _mm_tn(name, a, b, out_dtype=BF16, alpha=1.0, tm_c=(768, 512, 256, 128), tn_c=(512, 640, 256, 128)):
    t, m = a.shape
    n = b.shape[1]
    tm = _pick(m, tm_c)
    tn = _pick(n, tn_c)
    tk = _pick(t, (1088, 768, 512, 256, 128))
    nk = t // tk
    return _mm(name, a, b, _TN, (m // tm, n // tn, nk),
               pl.BlockSpec((tk, tm), lambda i, j, kk: (kk, i)), pl.BlockSpec((tk, tn), lambda i, j, kk: (kk, j)),
               pl.BlockSpec((tm, tn), lambda i, j, kk: (i, j)), (m, n), out_dtype, (tm, tn), k_axis=2, nk=nk, alpha=alpha)


def _ffn_in_fwd(name, n, wblk):
    t = n.shape[0]
    tm = _pick(t, (1088, 768, 512, 256, 128))
    return _mm(name, n, wblk, _NN, (t // tm, 2, 4),
               pl.BlockSpec((tm, D_MODEL), lambda i, s, j: (i, 0)),
               pl.BlockSpec((None, None, D_MODEL, FF_SHARD_P), lambda i, s, j: (s, j, 0, 0)),
               pl.BlockSpec((None, tm, FF_SHARD_P), lambda i, s, j: (s, i, j)), (2, t, D_FF_P), F32, None)


def _ffn_in_bwd_x(name, dgu, wblk, dep=None):
    t = dgu.shape[1]
    tm = _pick(t, (1088, 768, 512, 256, 128))
    return _mm(name, dgu, wblk, _NT, (t // tm, 8),
               pl.BlockSpec((None, tm, FF_SHARD_P), lambda i, kk: (kk // 4, i, kk % 4)),
               pl.BlockSpec((None, None, D_MODEL, FF_SHARD_P), lambda i, kk: (kk // 4, kk % 4, 0, 0)),
               pl.BlockSpec((tm, D_MODEL), lambda i, kk: (i, 0)), (t, D_MODEL), F32, (tm, D_MODEL), k_axis=1, nk=8, dep=dep)


def _ffn_in_bwd_w(name, n, dgu):
    t = n.shape[0]
    tk = _pick(t, (1088, 768, 512, 256, 128))
    nk = t // tk
    return _mm(name, n, dgu, _TN, (2, 4, nk),
               pl.BlockSpec((tk, D_MODEL), lambda s, j, kk: (kk, 0)),
               pl.BlockSpec((None, tk, FF_SHARD_P), lambda s, j, kk: (s, kk, j)),
               pl.BlockSpec((None, None, D_MODEL, FF_SHARD_P), lambda s, j, kk: (s, j, 0, 0)),
               (2, 4, D_MODEL, FF_SHARD_P), BF16, (D_MODEL, FF_SHARD_P), k_axis=2, nk=nk)


def _rms_fwd(name, h, g):
    t = h.shape[0]
    tm = _pick(t, (544, 384, 256, 128))

    def body(h_ref, g_ref, o_ref):
        hv = h_ref[...]
        r = lax.rsqrt(jnp.mean(hv * hv, axis=-1, keepdims=True) + EPS)
        o_ref[...] = ((hv * r) * g_ref[...]).astype(BF16)

    return pl.pallas_call(
        body, name=name, grid=(t // tm,),
        in_specs=[pl.BlockSpec((tm, D_MODEL), lambda i: (i, 0)), pl.BlockSpec((1, D_MODEL), lambda i: (0, 0))],
        out_specs=pl.BlockSpec((tm, D_MODEL), lambda i: (i, 0)),
        out_shape=jax.ShapeDtypeStruct((t, D_MODEL), BF16), compiler_params=_params(("parallel",)),
    )(h, g)


def _rms_bwd(name, h, g, dn, dres):
    t = h.shape[0]
    tm = _pick(t, (544, 384, 256, 128))

    def body(h_ref, g_ref, dn_ref, dres_ref, dh_ref, dg_ref):
        i = pl.program_id(0)
        hv = h_ref[...]
        dnv = dn_ref[...]
        r = lax.rsqrt(jnp.mean(hv * hv, axis=-1, keepdims=True) + EPS)
        tv = dnv * g_ref[...]
        dot = jnp.mean(tv * hv, axis=-1, keepdims=True)
        dh_ref[...] = dres_ref[...] + (r * tv - hv * (r * r * r * dot))
        part = jnp.sum(dnv * (hv * r), axis=0, keepdims=True)

        @pl.when(i == 0)
        def _():
            dg_ref[...] = part

        @pl.when(i > 0)
        def _():
            dg_ref[...] += part

    row = pl.BlockSpec((tm, D_MODEL), lambda i: (i, 0))
    vec = pl.BlockSpec((1, D_MODEL), lambda i: (0, 0))
    return pl.pallas_call(
        body, name=name, grid=(t // tm,), in_specs=[row, vec, row, row], out_specs=[row, vec],
        out_shape=[jax.ShapeDtypeStruct((t, D_MODEL), F32), jax.ShapeDtypeStruct((1, D_MODEL), F32)],
        compiler_params=_params(("arbitrary",)),
    )(h, g, dn, dres)


def _swiglu_fwd(name, gu):
    t = gu.shape[1]
    tm = _pick(t, (544, 384, 256, 128))

    def body(gu_ref, a_ref):
        g = gu_ref[0]
        u = gu_ref[1]
        a_ref[...] = ((g * jax.nn.sigmoid(g)) * u).astype(BF16)

    return pl.pallas_call(
        body, name=name, grid=(t // tm, 4),
        in_specs=[pl.BlockSpec((2, tm, FF_SHARD_P), lambda i, j: (0, i, j))],
        out_specs=pl.BlockSpec((tm, FF_SHARD_P), lambda i, j: (i, j)),
        out_shape=jax.ShapeDtypeStruct((t, D_FF_P), BF16), compiler_params=_params(("parallel", "parallel")),
    )(gu)


def _swiglu_bwd(name, gu, da):
    t = gu.shape[1]
    tm = _pick(t, (544, 384, 256, 128))

    def body(gu_ref, da_ref, o_ref):
        g = gu_ref[0]
        u = gu_ref[1]
        d = da_ref[...]
        sg = jax.nn.sigmoid(g)
        o_ref[0] = (d * u * (sg * (1.0 + g * (1.0 - sg)))).astype(BF16)
        o_ref[1] = (d * (g * sg)).astype(BF16)

    return pl.pallas_call(
        body, name=name, grid=(t // tm, 4),
        in_specs=[pl.BlockSpec((2, tm, FF_SHARD_P), lambda i, j: (0, i, j)),
                  pl.BlockSpec((tm, FF_SHARD_P), lambda i, j: (i, j))],
        out_specs=pl.BlockSpec((2, tm, FF_SHARD_P), lambda i, j: (0, i, j)),
        out_shape=jax.ShapeDtypeStruct((2, t, D_FF_P), BF16), compiler_params=_params(("parallel", "parallel")),
    )(gu, da)


def _gate_fwd(name, proj, ya, yb):
    t = proj.shape[0]
    tm = _pick(t, (544, 384, 256, 128))
    tn = 512
    nb = D_MODEL // tn

    def body(ga_ref, gb_ref, ya_ref, yb_ref, o_ref):
        o_ref[...] = (jax.nn.sigmoid(ga_ref[...]) * ya_ref[...] + jax.nn.sigmoid(gb_ref[...]) * yb_ref[...]).astype(BF16)

    blk = pl.BlockSpec((tm, tn), lambda i, j: (i, j))
    return pl.pallas_call(
        body, name=name, grid=(t // tm, nb),
        in_specs=[pl.BlockSpec((tm, tn), lambda i, j: (i, P_GA // tn + j)),
                  pl.BlockSpec((tm, tn), lambda i, j: (i, P_GB // tn + j)), blk, blk],
        out_specs=blk, out_shape=jax.ShapeDtypeStruct((t, D_MODEL), BF16),
        compiler_params=_params(("parallel", "parallel")),
    )(proj, proj, ya, yb)


def _gate_bwd(name, proj, ya, yb, dmix):
    t = proj.shape[0]
    tm = _pick(t, (544, 384, 256, 128))
    tn = 512
    nb = D_MODEL // tn

    def body(ga_ref, gb_ref, ya_ref, yb_ref, dm_ref, dya_ref, dyb_ref, dga_ref, dgb_ref):
        dm = dm_ref[...]
        sa = jax.nn.sigmoid(ga_ref[...])
        sb = jax.nn.sigmoid(gb_ref[...])
        dya_ref[...] = (dm * sa).astype(BF16)
        dyb_ref[...] = (dm * sb).astype(BF16)
        dga_ref[...] = (dm * ya_ref[...] * (sa * (1.0 - sa))).astype(BF16)
        dgb_ref[...] = (dm * yb_ref[...] * (sb * (1.0 - sb))).astype(BF16)

    blk = pl.BlockSpec((tm, tn), lambda i, j: (i, j))
    out = jax.ShapeDtypeStruct((t, D_MODEL), BF16)
    return pl.pallas_call(
        body, name=name, grid=(t // tm, nb),
        in_specs=[pl.BlockSpec((tm, tn), lambda i, j: (i, P_GA // tn + j)),
                  pl.BlockSpec((tm, tn), lambda i, j: (i, P_GB // tn + j)), blk, blk, blk],
        out_specs=[blk, blk, blk, blk], out_shape=[out, out, out, out],
        compiler_params=_params(("parallel", "parallel")),
    )(proj, proj, ya, yb, dmix)


def _loss_head(name, h3, gf, tgt):
    b, l, _ = h3.shape
    nb = l // BLOCK

    def body(h_ref, g_ref, t_ref, dh_ref, loss_ref, dg_ref):
        first = (pl.program_id(0) == 0) & (pl.program_id(1) == 0)
        real = (pl.program_id(1) > 0).astype(F32)
        hv = h_ref[...]
        g = g_ref[...]
        r = lax.rsqrt(jnp.mean(hv * hv, axis=-1, keepdims=True) + EPS)
        xn = hv * r
        err = (xn * g - t_ref[...]) * real
        lpart = 0.5 * jnp.sum(jnp.mean(err * err, axis=-1, keepdims=True), axis=0, keepdims=True)
        dy = err * (1.0 / D_MODEL)
        tv = dy * g
        dot = jnp.mean(tv * hv, axis=-1, keepdims=True)
        dh_ref[...] = r * tv - hv * (r * r * r * dot)
        gpart = jnp.sum(dy * xn, axis=0, keepdims=True)

        @pl.when(first)
        def _():
            loss_ref[...] = jnp.zeros_like(loss_ref)
            dg_ref[...] = jnp.zeros_like(dg_ref)

        loss_ref[...] += jnp.broadcast_to(lpart, loss_ref.shape)
        dg_ref[...] += gpart

    return pl.pallas_call(
        body, name=name, grid=(b, nb),
        in_specs=[pl.BlockSpec((None, BLOCK, D_MODEL), lambda bi, n: (bi, n, 0)),
                  pl.BlockSpec((1, D_MODEL), lambda bi, n: (0, 0)),
                  pl.BlockSpec((None, BLOCK, D_MODEL), lambda bi, n: (bi, jnp.maximum(n - 1, 0), 0))],
        out_specs=[pl.BlockSpec((None, BLOCK, D_MODEL), lambda bi, n: (bi, n, 0)),
                   pl.BlockSpec((8, 128), lambda bi, n: (0, 0)),
                   pl.BlockSpec((1, D_MODEL), lambda bi, n: (0, 0))],
        out_shape=[jax.ShapeDtypeStruct(h3.shape, F32), jax.ShapeDtypeStruct((8, 128), F32),
                   jax.ShapeDtypeStruct((1, D_MODEL), F32)],
        compiler_params=_params(("arbitrary", "arbitrary")),
    )(h3, gf, tgt)


def _fgate_fwd(name, proj3, bf_row):
    b, l, _ = proj3.shape
    nb = l // BLOCK

    def body(f_ref, b_ref, c_ref):
        r_i = lax.broadcasted_iota(jnp.int32, (BLOCK, BLOCK), 0)
        c_i = lax.broadcasted_iota(jnp.int32, (BLOCK, BLOCK), 1)
        tri = (r_i >= c_i).astype(F32)
        carry = jnp.zeros((1, 128), F32)
        for blk in range(nb):
            rows = slice(blk * BLOCK, (blk + 1) * BLOCK)
            z = f_ref[rows, :] + b_ref[...]
            lf = jnp.minimum(z, 0.0) - jnp.log(1.0 + jnp.exp(-jnp.abs(z)))
            cb = jnp.dot(tri, lf, preferred_element_type=F32, precision=lax.Precision.HIGHEST) + carry
            c_ref[rows, :] = cb
            carry = cb[BLOCK - 1:BLOCK, :]

    return pl.pallas_call(
        body, name=name, grid=(b,),
        in_specs=[pl.BlockSpec((None, l, 128), lambda bi: (bi, 0, P_F // 128)),
                  pl.BlockSpec((1, 128), lambda bi: (0, 0))],
        out_specs=pl.BlockSpec((None, l, 128), lambda bi: (bi, 0, 0)),
        out_shape=jax.ShapeDtypeStruct((b, l, 128), F32), compiler_params=_params(("parallel",)),
    )(proj3, bf_row)


def _fgate_bwd(name, proj3, bf_row, dc):
    b, l, _ = proj3.shape
    nb = l // BLOCK

    def body(f_ref, b_ref, dc_ref, df_ref, db_ref):
        r_i = lax.broadcasted_iota(jnp.int32, (BLOCK, BLOCK), 0)
        c_i = lax.broadcasted_iota(jnp.int32, (BLOCK, BLOCK), 1)
        tri = (r_i <= c_i).astype(F32)
        carry = jnp.zeros((1, 128), F32)
        total = jnp.zeros((1, 128), F32)
        for blk in range(nb - 1, -1, -1):
            rows = slice(blk * BLOCK, (blk + 1) * BLOCK)
            rc = jnp.dot(tri, dc_ref[rows, :], preferred_element_type=F32, precision=lax.Precision.HIGHEST) + carry
            carry = rc[0:1, :]
            z = f_ref[rows, :] + b_ref[...]
            df = rc * (1.0 / (1.0 + jnp.exp(z)))
            df_ref[rows, :] = df
            total = total + jnp.sum(df, axis=0, keepdims=True)

        @pl.when(pl.program_id(0) == 0)
        def _():
            db_ref[...] = total

        @pl.when(pl.program_id(0) > 0)
        def _():
            db_ref[...] += total

    return pl.pallas_call(
        body, name=name, grid=(b,),
        in_specs=[pl.BlockSpec((None, l, 128), lambda bi: (bi, 0, P_F // 128)),
                  pl.BlockSpec((1, 128), lambda bi: (0, 0)),
                  pl.BlockSpec((None, l, 128), lambda bi: (bi, 0, 0))],
        out_specs=[pl.BlockSpec((None, l, 128), lambda bi: (bi, 0, 0)), pl.BlockSpec((1, 128), lambda bi: (0, 0))],
        out_shape=[jax.ShapeDtypeStruct((b, l, 128), F32), jax.ShapeDtypeStruct((1, 128), F32)],
        compiler_params=_params(("arbitrary",)),
    )(proj3, bf_row, dc)


def _swa_scores(q, kcat, n, slope):
    rows = A_GROUP * BLOCK
    r_i = lax.broadcasted_iota(jnp.int32, (rows, 3 * BLOCK), 0)
    c_i = lax.broadcasted_iota(jnp.int32, (rows, 3 * BLOCK), 1)
    qpos = n * BLOCK + (r_i & (BLOCK - 1))
    seg = c_i >> 7
    kpos = jnp.where(seg == 0, c_i, (n - 2) * BLOCK + c_i)
    dist = qpos - kpos
    s = lax.dot_general(q, kcat, _NT, preferred_element_type=F32) * SCALE
    s = s - slope * dist.astype(F32)
    is_meta = seg == 0
    band = jnp.logical_not(is_meta) & (dist < BLOCK) & (kpos >= PREFIX)
    meta = is_meta & (c_i >= N_PAD)
    return jnp.where((dist >= 0) & (band | meta), s, NEG)


def _swa_specs(l):
    q_spec = pl.BlockSpec((None, A_GROUP, BLOCK, HEAD_DIM), lambda g, b, n: (b, g, n, 0))
    k0 = pl.BlockSpec((None, None, BLOCK, HEAD_DIM), lambda g, b, n: (b, g, 0, 0))
    kp = pl.BlockSpec((None, None, BLOCK, HEAD_DIM), lambda g, b, n: (b, g, jnp.maximum(n - 1, 0), 0))
    kc = pl.BlockSpec((None, None, BLOCK, HEAD_DIM), lambda g, b, n: (b, g, n, 0))
    col = pl.BlockSpec((None, A_GROUP * BLOCK, 1), lambda g, b, n: (g, 0, 0))
    lse = pl.BlockSpec((None, A_GROUP, BLOCK, 1), lambda g, b, n: (b, g, n, 0))
    return q_spec, k0, kp, kc, col, lse


def _swa_fwd(name, q, k, v, slopes, sinks):
    b, _, l, _ = q.shape
    nb = l // BLOCK
    rows = A_GROUP * BLOCK

    def body(q_ref, k0_ref, kp_ref, kc_ref, v0_ref, vp_ref, vc_ref, sl_ref, sk_ref, o_ref, lse_ref):
        n = pl.program_id(2)
        qv = q_ref[...].reshape(rows, HEAD_DIM)
        kcat = jnp.concatenate([k0_ref[...], kp_ref[...], kc_ref[...]], axis=0)
        vcat = jnp.concatenate([v0_ref[...], vp_ref[...], vc_ref[...]], axis=0)
        s = _swa_scores(qv, kcat, n, sl_ref[...])
        sink = sk_ref[...]
        m = jnp.maximum(jnp.max(s, axis=-1, keepdims=True), sink)
        p = jnp.exp(s - m)
        den = jnp.sum(p, axis=-1, keepdims=True) + jnp.exp(sink - m)
        o = lax.dot_general(p.astype(BF16), vcat, _NN, preferred_element_type=F32) / den
        o_ref[...] = o.astype(BF16).reshape(A_GROUP, BLOCK, HEAD_DIM)
        lse_ref[...] = (m + jnp.log(den)).reshape(A_GROUP, BLOCK, 1)

    q_spec, k0, kp, kc, col, lse_spec = _swa_specs(l)
    return pl.pallas_call(
        body, name=name, grid=(A_KV_HEADS, b, nb),
        in_specs=[q_spec, k0, kp, kc, k0, kp, kc, col, col], out_specs=[q_spec, lse_spec],
        out_shape=[jax.ShapeDtypeStruct(q.shape, BF16), jax.ShapeDtypeStruct((b, A_HEADS, l, 1), F32)],
        compiler_params=_params(("parallel", "parallel", "parallel")),
    )(q, k, k, k, v, v, v, slopes, sinks)


def _swa_bwd(name, q, k, v, o, lse, do, slopes, sinks):
    b, _, l, _ = q.shape
    nb = l // BLOCK
    rows = A_GROUP * BLOCK

    def body(q_ref, k0_ref, kp_ref, kc_ref, v0_ref, vp_ref, vc_ref, o_ref, lse_ref, do_ref, sl_ref, sk_ref,
             dq_ref, dk_ref, dv_ref, ds_ref):
        bi = pl.program_id(1)
        n = pl.program_id(2)
        qv = q_ref[...].reshape(rows, HEAD_DIM)
        dov = do_ref[...].reshape(rows, HEAD_DIM)
        ov = o_ref[...].reshape(rows, HEAD_DIM)
        lsev = lse_ref[...].reshape(rows, 1)
        kcat = jnp.concatenate([k0_ref[...], kp_ref[...], kc_ref[...]], axis=0)
        vcat = jnp.concatenate([v0_ref[...], vp_ref[...], vc_ref[...]], axis=0)
        s = _swa_scores(qv, kcat, n, sl_ref[...])
        p = jnp.exp(s - lsev)
        dsum = jnp.sum(dov.astype(F32) * ov.astype(F32), axis=-1, keepdims=True)
        dp = lax.dot_general(dov, vcat, _NT, preferred_element_type=F32)
        dsc = (p * (dp - dsum)).astype(BF16)
        dq = lax.dot_general(dsc, kcat, _NN, preferred_element_type=F32) * SCALE
        dq_ref[...] = dq.astype(BF16).reshape(A_GROUP, BLOCK, HEAD_DIM)
        dkc = lax.dot_general(dsc, qv, _TN, preferred_element_type=F32) * SCALE
        dvc = lax.dot_general(p.astype(BF16), dov, _TN, preferred_element_type=F32)

        @pl.when(n == 0)
        def _():
            dk_ref[...] = jnp.zeros_like(dk_ref)
            dv_ref[...] = jnp.zeros_like(dv_ref)

        starts = (0, pl.multiple_of(jnp.maximum(n - 1, 0) * BLOCK, BLOCK), pl.multiple_of(n * BLOCK, BLOCK))
        for t, st in enumerate(starts):
            dk_ref[pl.ds(st, BLOCK), :] += dkc[t * BLOCK:(t + 1) * BLOCK, :]
            dv_ref[pl.ds(st, BLOCK), :] += dvc[t * BLOCK:(t + 1) * BLOCK, :]

        dsink = -(jnp.exp(sk_ref[...] - lsev) * dsum)
        r8 = lax.broadcasted_iota(jnp.int32, (8, 128), 0)
        acc = jnp.zeros((8, 128), F32)
        for hh in range(A_GROUP):
            acc = acc + jnp.where(r8 == hh, jnp.sum(dsink[hh * BLOCK:(hh + 1) * BLOCK, :]), 0.0)

        @pl.when((bi == 0) & (n == 0))
        def _():
            ds_ref[...] = jnp.zeros_like(ds_ref)

        ds_ref[...] += acc

    q_spec, k0, kp, kc, col, lse_spec = _swa_specs(l)
    full = pl.BlockSpec((None, None, l, HEAD_DIM), lambda g, bb, n: (bb, g, 0, 0))
    return pl.pallas_call(
        body, name=name, grid=(A_KV_HEADS, b, nb),
        in_specs=[q_spec, k0, kp, kc, k0, kp, kc, q_spec, lse_spec, q_spec, col, col],
        out_specs=[q_spec, full, full, pl.BlockSpec((None, 8, 128), lambda g, bb, n: (g, 0, 0))],
        out_shape=[jax.ShapeDtypeStruct(q.shape, BF16), jax.ShapeDtypeStruct(k.shape, F32),
                   jax.ShapeDtypeStruct(k.shape, F32), jax.ShapeDtypeStruct((A_KV_HEADS, 8, 128), F32)],
        compiler_params=_params(("arbitrary", "arbitrary", "arbitrary")),
    )(q, k, k, k, v, v, v, o, lse, do, slopes, sinks)


def _fox_scores(q, k, cq, ck, i):
    kh = k.shape[0]
    s = lax.dot_general(q, k, _NT, preferred_element_type=F32) * SCALE
    s = (s + cq) - ck
    qpos = i * BLOCK + lax.broadcasted_iota(jnp.int32, (BLOCK, kh), 0)
    kpos = lax.broadcasted_iota(jnp.int32, (BLOCK, kh), 1)
    return jnp.where((kpos <= qpos) & (kpos >= N_PAD), s, NEG)


def _fox_fwd(name, q, k, v, c_col, c_row):
    b, h, l, _ = q.shape
    nb = l // BLOCK

    def body(q_ref, k_ref, v_ref, cc_ref, cr_ref, o_ref, lse_ref):
        for i in range(nb):
            rows = slice(i * BLOCK, (i + 1) * BLOCK)
            kh = (i + 1) * BLOCK
            s = _fox_scores(q_ref[rows, :], k_ref[0:kh, :], cc_ref[rows, :], cr_ref[:, 0:kh], i)
            m = jnp.max(s, axis=-1, keepdims=True)
            p = jnp.exp(s - m)
            den = jnp.sum(p, axis=-1, keepdims=True)
            o = lax.dot_general(p.astype(BF16), v_ref[0:kh, :], _NN, preferred_element_type=F32) / den
            o_ref[rows, :] = o.astype(BF16)
            lse_ref[rows, :] = m + jnp.log(den)

    mat = pl.BlockSpec((None, None, l, HEAD_DIM), lambda bi, hi: (bi, hi, 0, 0))
    colv = pl.BlockSpec((None, None, l, 1), lambda bi, hi: (bi, hi, 0, 0))
    rowv = pl.BlockSpec((None, None, 1, l), lambda bi, hi: (bi, hi, 0, 0))
    return pl.pallas_call(
        body, name=name, grid=(b, h), in_specs=[mat, mat, mat, colv, rowv], out_specs=[mat, colv],
        out_shape=[jax.ShapeDtypeStruct(q.shape, BF16), jax.ShapeDtypeStruct((b, h, l, 1), F32)],
        compiler_params=_params(("parallel", "parallel")),
    )(q, k, v, c_col, c_row)


def _fox_bwd(name, q, k, v, c_col, c_row, o, lse, do):
    b, h, l, _ = q.shape
    nb = l // BLOCK

    def body(q_ref, k_ref, v_ref, cc_ref, cr_ref, o_ref, lse_ref, do_ref, dq_ref, dk_ref, dv_ref, dcq_ref, dck_ref):
        dk_ref[...] = jnp.zeros_like(dk_ref)
        dv_ref[...] = jnp.zeros_like(dv_ref)
        dck_ref[...] = jnp.zeros_like(dck_ref)
        for i in range(nb):
            rows = slice(i * BLOCK, (i + 1) * BLOCK)
            kh = (i + 1) * BLOCK
            qv = q_ref[rows, :]
            dov = do_ref[rows, :]
            kv = k_ref[0:kh, :]
            s = _fox_scores(qv, kv, cc_ref[rows, :], cr_ref[:, 0:kh], i)
            p = jnp.exp(s - lse_ref[rows, :])
            dsum = jnp.sum(dov.astype(F32) * o_ref[rows, :].astype(F32), axis=-1, keepdims=True)
            dp = lax.dot_general(dov, v_ref[0:kh, :], _NT, preferred_element_type=F32)
            ds = p * (dp - dsum)
            dsc = ds.astype(BF16)
            dq_ref[rows, :] = (lax.dot_general(dsc, kv, _NN, preferred_element_type=F32) * SCALE).astype(BF16)
            dk_ref[0:kh, :] += lax.dot_general(dsc, qv, _TN, preferred_element_type=F32) * SCALE
            dv_ref[0:kh, :] += lax.dot_general(p.astype(BF16), dov, _TN, preferred_element_type=F32)
            dcq_ref[rows, :] = jnp.sum(ds, axis=-1, keepdims=True)
            dck_ref[:, 0:kh] -= jnp.sum(ds, axis=0, keepdims=True)

    mat = pl.BlockSpec((None, None, l, HEAD_DIM), lambda bi, hi: (bi, hi, 0, 0))
    colv = pl.BlockSpec((None, None, l, 1), lambda bi, hi: (bi, hi, 0, 0))
    rowv = pl.BlockSpec((None, None, 1, l), lambda bi, hi: (bi, hi, 0, 0))
    return pl.pallas_call(
        body, name=name, grid=(b, h), in_specs=[mat, mat, mat, colv, rowv, mat, colv, mat],
        out_specs=[mat, mat, mat, colv, rowv],
        out_shape=[jax.ShapeDtypeStruct(q.shape, BF16), jax.ShapeDtypeStruct(q.shape, F32),
                   jax.ShapeDtypeStruct(q.shape, F32), jax.ShapeDtypeStruct((b, h, l, 1), F32),
                   jax.ShapeDtypeStruct((b, h, 1, l), F32)],
        compiler_params=_params(("parallel", "parallel")),
    )(q, k, v, c_col, c_row, o, lse, do)


_FLIPS = ((0, 0, 1), (0, 1, 0), (0, 1, 1), (1, 0, 0), (1, 0, 1), (1, 1, 0), (1, 1, 1))


def _exchange(name, gather, scatter):
    ng, ns = len(gather), len(scatter)
    na = ng + ns
    npeer = len(_FLIPS)

    def body(*refs):
        ins = refs[:na]
        outs = refs[na:2 * na]
        send_sems, recv_sems, loc_sems = refs[2 * na:]
        x, y, c = lax.axis_index("x"), lax.axis_index("y"), lax.axis_index("c")
        me = 4 * x + 2 * y + c
        peers = []
        for fx, fy, fc in _FLIPS:
            px = 1 - x if fx else x
            py = 1 - y if fy else y
            pc = 1 - c if fc else c
            peers.append(((px, py, pc), 4 * px + 2 * py + pc))

        def remote(a, kk):
            dev, lin = peers[kk]
            src = ins[a] if a < ng else ins[a].at[lin]
            return pltpu.make_async_remote_copy(src_ref=src, dst_ref=outs[a].at[me], send_sem=send_sems.at[a * npeer + kk],
                                                recv_sem=recv_sems.at[a * npeer + kk], device_id=dev, device_id_type=MESH_ID)

        def arrival(a, kk):
            dev, lin = peers[kk]
            src = ins[a] if a < ng else ins[a].at[lin]
            return pltpu.make_async_remote_copy(src_ref=src, dst_ref=outs[a].at[lin], send_sem=send_sems.at[a * npeer + kk],
                                                recv_sem=recv_sems.at[a * npeer + kk], device_id=dev, device_id_type=MESH_ID)

        local = []
        for a in range(na):
            src = ins[a] if a < ng else ins[a].at[me]
            cp = pltpu.make_async_copy(src, outs[a].at[me], loc_sems.at[a])
            cp.start()
            local.append(cp)
        sent = [remote(a, kk) for kk in range(npeer) for a in range(na)]
        for cp in sent:
            cp.start()
        for kk in range(npeer):
            for a in range(na):
                arrival(a, kk).wait_recv()
        for cp in sent:
            cp.wait_send()
        for cp in local:
            cp.wait()

    arrs = list(gather) + list(scatter)
    out_shape = [jax.ShapeDtypeStruct((N_DEV,) + tuple(a.shape), a.dtype) for a in gather]
    out_shape += [jax.ShapeDtypeStruct(tuple(a.shape), a.dtype) for a in scatter]
    anyspec = pl.BlockSpec(memory_space=pl.ANY)
    return pl.pallas_call(
        body, name=name, in_specs=[anyspec] * na, out_specs=[anyspec] * na, out_shape=out_shape,
        scratch_shapes=[pltpu.SemaphoreType.DMA((na * npeer,)), pltpu.SemaphoreType.DMA((na * npeer,)),
                        pltpu.SemaphoreType.DMA((na,))],
        compiler_params=pltpu.CompilerParams(has_side_effects=True),
    )(*arrs)


def _peer_table():
    x, y, c = lax.axis_index("x"), lax.axis_index("y"), lax.axis_index("c")
    me = 4 * x + 2 * y + c
    peers = []
    for fx, fy, fc in _FLIPS:
        px = 1 - x if fx else x
        py = 1 - y if fy else y
        pc = 1 - c if fc else c
        peers.append(((px, py, pc), 4 * px + 2 * py + pc))
    return me, peers


_HBM = pl.BlockSpec(memory_space=pltpu.HBM)
_SEM = pl.BlockSpec(memory_space=pltpu.SEMAPHORE)
_ANY = pl.BlockSpec(memory_space=pl.ANY)
_EFFECT = pltpu.SideEffectType.DATAFLOW_SIDE_EFFECTING


def _split_copy(srcs_are_pieces, src_refs, land_refs, send_sem, recv_sem, a, kk, me, peers, arriving):
    dev, lin = peers[kk]
    npeer = len(_FLIPS)
    src = src_refs[a] if srcs_are_pieces[a] else src_refs[a].at[lin]
    dst = land_refs[a].at[lin] if arriving else land_refs[a].at[me]
    return pltpu.make_async_remote_copy(src_ref=src, dst_ref=dst, send_sem=send_sem.at[a * npeer + kk],
                                        recv_sem=recv_sem.at[a * npeer + kk], device_id=dev, device_id_type=MESH_ID)


def _xchg_start(name, gather, scatter, after=None):
    me_out = 4 * lax.axis_index("x") + 2 * lax.axis_index("y") + lax.axis_index("c")
    srcs = list(gather) + list(scatter)
    is_piece = [True] * len(gather) + [False] * len(scatter)
    lands = []
    for a, piece in zip(srcs, is_piece):
        own = a[None] if piece else lax.dynamic_slice_in_dim(a, me_out, 1, axis=0)
        shape = ((N_DEV,) + tuple(a.shape)) if piece else tuple(a.shape)
        start = (me_out,) + (0,) * (len(shape) - 1)
        lands.append(lax.dynamic_update_slice(jnp.zeros(shape, a.dtype), own, start))
    n = len(srcs)
    nsem = n * len(_FLIPS)
    has_after = after is not None

    def body(*refs):
        src_refs = refs[:n]
        land_refs = refs[n:2 * n]
        outs = refs[2 * n + (1 if has_after else 0):]
        send_sem, recv_sem = outs[0], outs[1]
        token = outs[-1]
        me, peers = _peer_table()
        for kk in range(len(_FLIPS)):
            for a in range(n):
                _split_copy(is_piece, src_refs, land_refs, send_sem, recv_sem, a, kk, me, peers, False).start()
        token[...] = jnp.zeros_like(token)

    out_shape = ([pltpu.SemaphoreType.DMA((nsem,)), pltpu.SemaphoreType.DMA((nsem,))]
                 + [pltpu.HBM(tuple(a.shape), a.dtype) for a in srcs] + [pltpu.HBM(tuple(a.shape), a.dtype) for a in lands]
                 + [jax.ShapeDtypeStruct((8, 128), F32)])
    args = [pltpu.with_memory_space_constraint(a, pltpu.HBM) for a in srcs + lands] + ([after] if has_after else [])
    res = pl.pallas_call(
        body, name=name, out_shape=out_shape,
        in_specs=[_HBM] * (2 * n) + ([_ANY] if has_after else []),
        out_specs=[_SEM, _SEM] + [_HBM] * (2 * n) + [pl.BlockSpec(memory_space=pltpu.VMEM)],
        input_output_aliases={i: 2 + i for i in range(2 * n)},
        compiler_params=pltpu.CompilerParams(has_side_effects=_EFFECT),
    )(*args)
    state = (res[0], res[1], list(res[2:2 + n]), list(res[2 + n:2 + 2 * n]), is_piece)
    return state, res[-1]


def _xchg_wait(name, state, after):
    send_sem, recv_sem, srcs, lands, is_piece = state
    n = len(srcs)

    def body(*refs):
        src_refs = refs[:n]
        land_refs = refs[n:2 * n]
        s_sem, r_sem = refs[2 * n], refs[2 * n + 1]
        me, peers = _peer_table()
        for kk in range(len(_FLIPS)):
            for a in range(n):
                cp = _split_copy(is_piece, src_refs, land_refs, s_sem, r_sem, a, kk, me, peers, True)
                cp.wait_send()
                cp.wait_recv()

    out_shape = [pltpu.HBM(tuple(a.shape), a.dtype) for a in srcs] + [pltpu.HBM(tuple(a.shape), a.dtype) for a in lands]
    res = pl.pallas_call(
        body, name=name, out_shape=out_shape,
        in_specs=[_HBM] * (2 * n) + [_SEM, _SEM, _ANY], out_specs=[_HBM] * (2 * n),
        input_output_aliases={i: i for i in range(2 * n)},
        compiler_params=pltpu.CompilerParams(has_side_effects=_EFFECT),
    )(*srcs, *lands, send_sem, recv_sem, after)
    return list(res[n:])


def _adam_math(w, g, m, v):
    m = ADAM_B1 * m + (1.0 - ADAM_B1) * g
    v = ADAM_B2 * v + (1.0 - ADAM_B2) * (g * g)
    m_hat = m / (1.0 - ADAM_B1 ** ADAM_STEP)
    v_hat = v / (1.0 - ADAM_B2 ** ADAM_STEP)
    delta = -ADAM_LR * (m_hat / (jnp.sqrt(v_hat) + ADAM_EPS) + ADAM_WD * w)
    return delta, m, v


def _adam(name, w, m, v, parts):
    r, c = w.shape
    npart, _, cp = parts.shape
    tr = _pick(r, (256, 176, 128, 64, 16, 8, 1))

    def body(w_ref, m_ref, v_ref, p_ref, g_ref, d_ref, mo_ref, vo_ref):
        g = p_ref[0].astype(F32)
        for pp in range(1, npart):
            g = g + p_ref[pp].astype(F32)
        g = g[:, 0:c]
        delta, mn, vn = _adam_math(w_ref[...], g, m_ref[...], v_ref[...])
        g_ref[...] = g
        d_ref[...] = delta
        mo_ref[...] = mn
        vo_ref[...] = vn

    blk = pl.BlockSpec((tr, c), lambda i: (i, 0))
    out = jax.ShapeDtypeStruct((r, c), F32)
    return pl.pallas_call(
        body, name=name, grid=(r // tr,),
        in_specs=[blk, blk, blk, pl.BlockSpec((npart, tr, cp), lambda i: (0, i, 0))],
        out_specs=[blk, blk, blk, blk], out_shape=[out, out, out, out], compiler_params=_params(("parallel",)),
    )(w, m, v, parts)


def _small_sum(name, packs):
    def body(p_ref, o_ref):
        tot = p_ref[0]
        for pp in range(1, N_DEV):
            tot = tot + p_ref[pp]
        o_ref[0:8, :] = tot[0:8, :]
        o_ref[8:24, :] = tot[8:24, :] + tot[24:40, :]

    return pl.pallas_call(body, name=name, out_shape=jax.ShapeDtypeStruct((24, D_MODEL), F32),
                          compiler_params=_params())(packs)


def _heads(t2d, b, l, nh):
    return t2d.reshape(b, l, nh, HEAD_DIM).transpose(0, 2, 1, 3)


def _unheads(t4d):
    b, nh, l, _ = t4d.shape
    return t4d.transpose(0, 2, 1, 3).reshape(b * l, nh * HEAD_DIM)


def _local_step(x, tgt, g1, gm, g2, gf, b_forget, sinks, weights, send):
    b, s, _ = x.shape
    l = s + PREFIX
    t = b * l
    w1i, w1o, meta = weights("ffn1", x)
    h0 = jnp.concatenate([jnp.zeros((b, N_PAD, D_MODEL), F32), jnp.broadcast_to(meta[None], (b, N_META, D_MODEL)), x],
                         axis=1).reshape(t, D_MODEL)

    n1 = _rms_fwd("rms1_fwd", h0, g1)
    gu1 = _ffn_in_fwd("ffn1_in_fwd", n1, w1i)
    a1 = _swiglu_fwd("swiglu1_fwd", gu1)
    h1 = _mm_nn("ffn1_out_fwd", a1, w1o, alpha=0.5, res=h0)
    wi, wa, wb, wo = weights("mix", h1)
    um = _rms_fwd("rmsm_fwd", h1, gm)
    proj = _mm_nn("proj_fwd", um, wi, tn_c=(640,))
    proj3 = proj.reshape(b, l, PROJ_P)
    bf_row = jnp.pad(b_forget, ((0, 0), (0, 128 - B_HEADS)))
    cfull = _fgate_fwd("fgate_fwd", proj3, bf_row)
    c8 = cfull[:, :, 0:B_HEADS].transpose(0, 2, 1)
    c_col = c8[..., None]
    c_row = c8[:, :, None, :]
    qa = _heads(proj[:, P_QA:P_KA].astype(BF16), b, l, A_HEADS)
    ka = _heads(proj[:, P_KA:P_VA].astype(BF16), b, l, A_KV_HEADS)
    va = _heads(proj[:, P_VA:P_QB].astype(BF16), b, l, A_KV_HEADS)
    qb = _heads(proj[:, P_QB:P_KB].astype(BF16), b, l, B_HEADS)
    kb = _heads(proj[:, P_KB:P_VB].astype(BF16), b, l, B_HEADS)
    vb = _heads(proj[:, P_VB:P_F].astype(BF16), b, l, B_HEADS)
    head_of_row = jnp.arange(A_HEADS * BLOCK) // BLOCK
    slopes = jnp.exp2(-8.0 * (head_of_row + 1).astype(F32) / A_HEADS).reshape(A_KV_HEADS, A_GROUP * BLOCK, 1)
    sink_rows = jnp.repeat(sinks.reshape(A_HEADS), BLOCK).reshape(A_KV_HEADS, A_GROUP * BLOCK, 1)
    oa4, lse_a = _swa_fwd("swa_fwd", qa, ka, va, slopes, sink_rows)
    ob4, lse_b = _fox_fwd("fox_fwd", qb, kb, vb, c_col, c_row)
    oa = _unheads(oa4)
    ob = _unheads(ob4)
    ya = _mm_nn("branch_a_fwd", oa, wa)
    yb = _mm_nn("branch_b_fwd", ob, wb)
    mixed = _gate_fwd("gate_fwd", proj, ya, yb)
    h2 = _mm_nn("mix_out_fwd", mixed, wo, res=h1)
    w2i, w2o = weights("ffn2", h2)
    n2 = _rms_fwd("rms2_fwd", h2, g2)
    gu2 = _ffn_in_fwd("ffn2_in_fwd", n2, w2i)
    a2 = _swiglu_fwd("swiglu2_fwd", gu2)
    h3 = _mm_nn("ffn2_out_fwd", a2, w2o, alpha=0.5, res=h2)

    dh3_3, loss_blk, dgf = _loss_head("loss_head", h3.reshape(b, l, D_MODEL), gf, tgt)
    dh3 = dh3_3.reshape(t, D_MODEL)

    def ffn_bwd(tag, dh, h_in, g_norm, n_in, gu, a, w_in_blk, w_out):
        dw_out = _mm_tn(tag + "_out_bwd_w", a, dh, alpha=0.5)
        da = _mm_nt(tag + "_out_bwd_x", dh, w_out, alpha=0.5, dep=send(tag + "_out", (dw_out,)))
        dgu = _swiglu_bwd("swiglu" + tag[-1] + "_bwd", gu, da)
        dw_in = _ffn_in_bwd_w(tag + "_in_bwd_w", n_in, dgu)
        dn = _ffn_in_bwd_x(tag + "_in_bwd_x", dgu, w_in_blk, dep=send(tag + "_in", (dw_in,)))
        dh_in, dg = _rms_bwd("rms" + tag[-1] + "_bwd", h_in, g_norm, dn, dh)
        return dh_in, dg

    dh2, dg2 = ffn_bwd("ffn2", dh3, h2, g2, n2, gu2, a2, w2i, w2o)

    dmix = _mm_nt("mix_out_bwd_x", dh2, wo, tn_c=(512,))
    dwo = _mm_tn("mix_out_bwd_w", mixed, dh2)
    dya, dyb, dga, dgb = _gate_bwd("gate_bwd", proj, ya, yb, dmix)
    doa = _mm_nt("branch_a_bwd_x", dya, wa, out_dtype=BF16, tn_c=(512,))
    dob = _mm_nt("branch_b_bwd_x", dyb, wb, out_dtype=BF16, tn_c=(512,))
    dwa = _mm_tn("branch_a_bwd_w", oa, dya, tm_c=(512,))
    dwb = _mm_tn("branch_b_bwd_w", ob, dyb, tm_c=(512,))
    dqa, dka, dva, dsink = _swa_bwd("swa_bwd", qa, ka, va, oa4, lse_a, _heads(doa, b, l, A_HEADS), slopes, sink_rows)
    dqb, dkb, dvb, dcq, dck = _fox_bwd("fox_bwd", qb, kb, vb, c_col, c_row, ob4, lse_b, _heads(dob, b, l, B_HEADS))
    dc = (dcq[..., 0] + dck[:, :, 0, :]).transpose(0, 2, 1)
    dc = jnp.pad(dc, ((0, 0), (0, 0), (0, 128 - B_HEADS)))
    df3, dbf = _fgate_bwd("fgate_bwd", proj3, bf_row, dc)
    dproj = jnp.concatenate(
        [dga, dgb, _unheads(dqa), _unheads(dka).astype(BF16), _unheads(dva).astype(BF16), _unheads(dqb),
         _unheads(dkb).astype(BF16), _unheads(dvb).astype(BF16), df3.reshape(t, 128).astype(BF16)], axis=1)
    dwi = _mm_tn("proj_bwd_w", um, dproj, tm_c=(512,), tn_c=(640,))
    dum = _mm_nt("proj_bwd_x", dproj, wi, tn_c=(512,), tk_c=(640,), dep=send("mix", (dwi, dwa, dwb, dwo)))
    dh1, dgm = _rms_bwd("rmsm_bwd", h1, gm, dum, dh2)

    dh0, dg1 = ffn_bwd("ffn1", dh1, h0, g1, n1, gu1, a1, w1i, w1o)
    dh0_3 = dh0.reshape(b, l, D_MODEL)
    grad_x = dh0_3[:, PREFIX:, :]
    dmeta = dh0_3[:, N_PAD:PREFIX, :].reshape(b * N_META, D_MODEL)

    misc = jnp.concatenate([dbf[:, 0:B_HEADS], dsink[:, 0:A_GROUP, 0].reshape(1, A_HEADS), loss_blk[0:1, 0:1]], axis=1)
    misc = jnp.pad(misc, ((0, 0), (0, D_MODEL - misc.shape[1])))
    row = lax.broadcasted_iota(jnp.int32, (8, D_MODEL), 0)
    vec = jnp.zeros((8, D_MODEL), F32)
    for i, piece in enumerate((dg1, dgm, dg2, dgf, misc)):
        vec = jnp.where(row == i, piece, vec)
    small = jnp.concatenate([vec, dmeta], axis=0)
    return grad_x, small


def _pad_to(a, rows, cols):
    return jnp.pad(a, ((0, rows - a.shape[0]), (0, cols - a.shape[1])))


def _ffn_out_from_gathered(g):
    w = g.reshape(4, FF_SHARD, D_MODEL)
    return jnp.pad(w, ((0, 0), (0, FF_SHARD_P - FF_SHARD), (0, 0))).reshape(D_FF_P, D_MODEL)


def _ffn_out_to_scatter(dw):
    return dw.reshape(4, FF_SHARD_P, D_MODEL)[:, 0:FF_SHARD, :].reshape(N_DEV, FFO_SHARD, D_MODEL)


def _proj_from_gathered(g):
    w = g[:, :, 0:WIN_SHARD].transpose(1, 0, 2).reshape(D_MODEL, W_IN_COLS)
    qkv, f, gates = w[:, 0:2304], w[:, 2304:2312], w[:, 2312:]
    return jnp.concatenate([gates, qkv, jnp.pad(f, ((0, 0), (0, 128 - B_HEADS)))], axis=1)


def _proj_to_scatter(dw):
    w = jnp.concatenate([dw[:, P_QA:P_F], dw[:, P_F:P_F + B_HEADS], dw[:, 0:P_QA]], axis=1)
    w = w.reshape(D_MODEL, N_DEV, WIN_SHARD).transpose(1, 0, 2)
    return jnp.pad(w, ((0, 0), (0, 0), (0, WIN_SHARD_P - WIN_SHARD)))


def kernel(x, meta_tokens, ffn1_norm, ffn1_w_in, ffn1_w_out, mix_norm, w_in, b_forget, attn_sinks, w_branch_a, w_branch_b, w_out, ffn2_norm, ffn2_w_in, ffn2_w_out, final_norm, loss_target, m_meta_tokens, m_ffn1_norm, m_ffn1_w_in, m_ffn1_w_out, m_mix_norm, m_w_in, m_b_forget, m_attn_sinks, m_w_branch_a, m_w_branch_b, m_w_out, m_ffn2_norm, m_ffn2_w_in, m_ffn2_w_out, m_final_norm, v_meta_tokens, v_ffn1_norm, v_ffn1_w_in, v_ffn1_w_out, v_mix_norm, v_w_in, v_b_forget, v_attn_sinks, v_w_branch_a, v_w_branch_b, v_w_out, v_ffn2_norm, v_ffn2_w_in, v_ffn2_w_out, v_final_norm):
    me = 4 * lax.axis_index("x") + 2 * lax.axis_index("y") + lax.axis_index("c")

    shards = (
        _pad_to(ffn1_w_in[0].astype(BF16), D_MODEL, FF_SHARD_P),
        ffn1_w_out[0].astype(BF16),
        _pad_to(w_in[0].astype(BF16), D_MODEL, WIN_SHARD_P),
        w_branch_a[0].astype(BF16), w_branch_b[0].astype(BF16), w_out[0].astype(BF16),
        _pad_to(ffn2_w_in[0].astype(BF16), D_MODEL, FF_SHARD_P),
        ffn2_w_out[0].astype(BF16),
        meta_tokens,
    )
    s1i, s1o, swi, swa, swb, swo, s2i, s2o, smeta = shards
    gather_state = {}
    gather_state["ffn1"], tok = _xchg_start("gather_ffn1_start", (s1i, s1o, smeta), ())
    gather_state["mix"], tok = _xchg_start("gather_mix_start", (swi, swa, swb, swo), (), after=tok)
    gather_state["ffn2"], tok = _xchg_start("gather_ffn2_start", (s2i, s2o), (), after=tok)
    started = {"tok": tok}

    def weights(group, after):
        if group == "ffn1":
            after = started["tok"]
        got = _xchg_wait("gather_" + group + "_wait", gather_state[group], after)
        if group == "mix":
            gwi, gwa, gwb, gwo = got
            return (_proj_from_gathered(gwi), gwa.transpose(1, 0, 2).reshape(A_WIDTH, D_MODEL),
                    gwb.transpose(1, 0, 2).reshape(B_WIDTH, D_MODEL), gwo.reshape(D_MODEL, D_MODEL))
        w_in_blk = got[0].reshape(2, 4, D_MODEL, FF_SHARD_P)
        w_out_p = _ffn_out_from_gathered(got[1])
        if group == "ffn1":
            return w_in_blk, w_out_p, got[2].transpose(1, 0, 2).reshape(N_META, D_MODEL)
        return w_in_blk, w_out_p

    scatter_state = {}

    def send(group, grads):
        if group == "mix":
            dwi, dwa, dwb, dwo = grads
            blocks = (_proj_to_scatter(dwi), dwa.reshape(A_WIDTH, N_DEV, 128).transpose(1, 0, 2),
                      dwb.reshape(B_WIDTH, N_DEV, 128).transpose(1, 0, 2), dwo.reshape(N_DEV, 128, D_MODEL))
        elif group.endswith("_in"):
            blocks = (grads[0].reshape(N_DEV, D_MODEL, FF_SHARD_P),)
        else:
            blocks = (_ffn_out_to_scatter(grads[0]),)
        scatter_state[group], token = _xchg_start("scatter_" + group + "_start", (), blocks)
        return token

    gf = final_norm.reshape(1, D_MODEL)
    grad_x, small = _local_step(x, loss_target, ffn1_norm, mix_norm, ffn2_norm, gf, b_forget, attn_sinks, weights, send)

    (packs,) = _exchange("gather_small", (small,), ())
    recv = {}
    after = packs
    for group in ("ffn2_out", "ffn2_in", "mix", "ffn1_out", "ffn1_in"):
        recv[group] = _xchg_wait("scatter_" + group + "_wait", scatter_state[group], after)
        after = recv[group][0]
    (r2o,), (r2i,), (rwi, rwa, rwb, rwo), (r1o,), (r1i,) = (recv[g] for g in ("ffn2_out", "ffn2_in", "mix", "ffn1_out", "ffn1_in"))

    out = {}
    for nm, w, m, v, parts in (
        ("ffn1_w_in", ffn1_w_in, m_ffn1_w_in, v_ffn1_w_in, r1i), ("ffn1_w_out", ffn1_w_out, m_ffn1_w_out, v_ffn1_w_out, r1o),
        ("w_in", w_in, m_w_in, v_w_in, rwi), ("w_branch_a", w_branch_a, m_w_branch_a, v_w_branch_a, rwa),
        ("w_branch_b", w_branch_b, m_w_branch_b, v_w_branch_b, rwb), ("w_out", w_out, m_w_out, v_w_out, rwo),
        ("ffn2_w_in", ffn2_w_in, m_ffn2_w_in, v_ffn2_w_in, r2i), ("ffn2_w_out", ffn2_w_out, m_ffn2_w_out, v_ffn2_w_out, r2o),
    ):
        res4 = _adam("adam_" + nm, w[0], m[0], v[0], parts)
        out[nm] = tuple(r[None] for r in res4)

    tot = _small_sum("small_sum", packs)
    loss = tot[4, 2 * B_HEADS]
    g_meta = lax.dynamic_slice(tot[8:24, :], (0, me * 128), (N_META, 128))
    out["meta_tokens"] = tuple(_adam("adam_meta_tokens", meta_tokens, m_meta_tokens, v_meta_tokens, g_meta[None]))

    def pack_small(n1, nm, n2, nf, bfv, skv):
        misc = jnp.pad(jnp.concatenate([bfv, skv], axis=1), ((0, 0), (0, D_MODEL - 2 * B_HEADS)))
        row = lax.broadcasted_iota(jnp.int32, (8, D_MODEL), 0)
        vec = jnp.zeros((8, D_MODEL), F32)
        for i, piece in enumerate((n1, nm, n2, nf.reshape(1, D_MODEL), misc)):
            vec = jnp.where(row == i, piece, vec)
        return vec

    w_pack = pack_small(ffn1_norm, mix_norm, ffn2_norm, final_norm, b_forget, attn_sinks)
    m_pack = pack_small(m_ffn1_norm, m_mix_norm, m_ffn2_norm, m_final_norm, m_b_forget, m_attn_sinks)
    v_pack = pack_small(v_ffn1_norm, v_mix_norm, v_ffn2_norm, v_final_norm, v_b_forget, v_attn_sinks)
    small4 = _adam("adam_small", w_pack, m_pack, v_pack, tot[0:8][None])
    for i, nm in enumerate(("ffn1_norm", "mix_norm", "ffn2_norm")):
        out[nm] = tuple(r[i:i + 1] for r in small4)
    out["final_norm"] = tuple(r[3] for r in small4)
    out["b_forget"] = tuple(r[4:5, 0:B_HEADS] for r in small4)
    out["attn_sinks"] = tuple(r[4:5, B_HEADS:2 * B_HEADS] for r in small4)

    names = ("meta_tokens", "ffn1_norm", "ffn1_w_in", "ffn1_w_out", "mix_norm", "w_in", "b_forget", "attn_sinks",
             "w_branch_a", "w_branch_b", "w_out", "ffn2_norm", "ffn2_w_in", "ffn2_w_out", "final_norm")
    return (loss, grad_x) + tuple(out[nm][kind] for kind in range(4) for nm in names)
```

```python
import jax
import jax.numpy as jnp
from jax import lax
from jax.experimental import pallas as pl
from jax.experimental.pallas import tpu as pltpu

F32 = jnp.float32
BF16 = jnp.bfloat16

D_MODEL = 1024
N_META = 16
BLOCK = 128
PREFIX = 128
N_PAD = PREFIX - N_META
HEAD_DIM = 64
A_HEADS = 8
A_KV_HEADS = 2
A_GROUP = 4
B_HEADS = 8
A_WIDTH = 512
A_KV_WIDTH = 128
B_WIDTH = 512
D_FF = 2816
N_DEV = 8
FF_SHARD = 2 * D_FF // N_DEV
FF_SHARD_P = 768
FFO_SHARD = D_FF // N_DEV
FFO_SHARD_P = FF_SHARD_P // 2
D_FF_P = 4 * FF_SHARD_P
W_IN_COLS = 4360
WIN_SHARD = W_IN_COLS // N_DEV
WIN_SHARD_P = 640
PAIR_W = 3 * 128
B_SEG = 4 * PAIR_W
A_SEG = A_WIDTH + 2 * A_KV_WIDTH
QKV_W = B_SEG + A_SEG
P_GATES = QKV_W
P_F = QKV_W + 2 * D_MODEL
PROJ_P = P_F + 128
A_HEAD_ORDER = (0, 4, 1, 5, 2, 6, 3, 7)
EPS = 1e-6
NEG = -1e30
SCALE = HEAD_DIM ** -0.5
ADAM_LR = 0.001
ADAM_B1 = 0.9
ADAM_B2 = 0.999
ADAM_EPS = 1e-08
ADAM_WD = 0.01
ADAM_STEP = 10
VMEM_LIMIT = 48 * 1024 * 1024
MESH_ID = pl.DeviceIdType.MESH
SMALL_ROWS = 40

_NN = (((1,), (0,)), ((), ()))
_NT = (((1,), (1,)), ((), ()))
_TN = (((0,), (0,)), ((), ()))


def _params(sem=None):
    return pltpu.CompilerParams(dimension_semantics=sem, vmem_limit_bytes=VMEM_LIMIT)


def _pick(n, cands):
    for c in cands:
        if n % c == 0:
            return c
    raise ValueError(f"no tile for {n}")


def _bf(v):
    return v if v.dtype == BF16 else v.astype(BF16)


def _mm(name, a, b, dims, grid, a_spec, b_spec, o_spec, out_shape, out_dtype, acc_shape, k_axis=None, nk=1,
        alpha=1.0, res=None, res_spec=None, dep=None):
    has_res = res is not None
    has_dep = dep is not None

    def body(*refs):
        a_ref, b_ref = refs[0], refs[1]
        r_ref = refs[2] if has_res else None
        o_ref = refs[2 + has_res + has_dep]

        def finish(acc):
            if alpha != 1.0:
                acc = acc * alpha
            if has_res:
                acc = acc + r_ref[...]
            o_ref[...] = acc.astype(o_ref.dtype)

        part = lax.dot_general(_bf(a_ref[...]), _bf(b_ref[...]), dims, preferred_element_type=F32)
        if nk == 1:
            finish(part)
        else:
            acc_ref = refs[-1]
            k = pl.program_id(k_axis)

            @pl.when(k == 0)
            def _():
                acc_ref[...] = part

            @pl.when(k > 0)
            def _():
                acc_ref[...] += part

            @pl.when(k == nk - 1)
            def _():
                finish(acc_ref[...])

    in_specs = [a_spec, b_spec] + ([res_spec] if has_res else []) + ([pl.BlockSpec(memory_space=pl.ANY)] if has_dep else [])
    args = (a, b) + ((res,) if has_res else ()) + ((dep,) if has_dep else ())
    sem = tuple("arbitrary" if (nk > 1 and i == k_axis) else "parallel" for i in range(len(grid)))
    return pl.pallas_call(
        body, name=name, grid=grid, in_specs=in_specs, out_specs=o_spec,
        out_shape=jax.ShapeDtypeStruct(out_shape, out_dtype),
        scratch_shapes=[pltpu.VMEM(acc_shape, F32)] if nk > 1 else [],
        compiler_params=_params(sem),
    )(*args)


def _mm_nn(name, a, b, out_dtype=F32, alpha=1.0, res=None, tn_c=(512, 640, 256, 128), cols=None):
    t, k = a.shape
    c0, n = (0, b.shape[1]) if cols is None else cols
    tm = _pick(t, (1088, 768, 512, 256, 128))
    tn = _pick(n, tn_c)
    assert c0 % tn == 0
    jb = c0 // tn
    return _mm(name, a, b, _NN, (t // tm, n // tn),
               pl.BlockSpec((tm, k), lambda i, j: (i, 0)), pl.BlockSpec((k, tn), lambda i, j: (0, jb + j)),
               pl.BlockSpec((tm, tn), lambda i, j: (i, j)), (t, n), out_dtype, None,
               alpha=alpha, res=res, res_spec=pl.BlockSpec((tm, tn), lambda i, j: (i, j)))


def _mm_nt(name, a, b, out_dtype=F32, alpha=1.0, tn_c=(768, 512, 256, 128), tk_c=None, dep=None, res=None, kcols=None):
    t, k = a.shape
    n = b.shape[0]
    c0 = 0 if kcols is None else kcols[0]
    tm = _pick(t, (1088, 768, 512, 256, 128))
    tn = _pick(n, tn_c)
    tk = k if tk_c is None else _pick(k, tk_c)
    nk = k // tk
    assert c0 % tk == 0
    kb = c0 // tk
    return _mm(name, a, b, _NT, (t // tm, n // tn, nk),
               pl.BlockSpec((tm, tk), lambda i, j, kk: (i, kk)), pl.BlockSpec((tn, tk), lambda i, j, kk: (j, kb + kk)),
               pl.BlockSpec((tm, tn), lambda i, j, kk: (i, j)), (t, n), out_dtype, (tm, tn), k_axis=2, nk=nk, alpha=alpha,
               dep=dep, res=res, res_spec=pl.BlockSpec((tm, tn), lambda i, j, kk: (i, j)))


def _mm_tn(name, a, b, out_dtype=BF16, alpha=1.0, tm_c=(768, 512, 256, 128), tn_c=(512, 640, 256, 128)):
    t, m = a.shape
    n = b.shape[1]
    tm = _pick(m, tm_c)
    tn = _pick(n, tn_c)
    tk = _pick(t, (1088, 768, 512, 256, 128))
    nk = t // tk
    return _mm(name, a, b, _TN, (m // tm, n // tn, nk),
               pl.BlockSpec((tk, tm), lambda i, j, kk: (kk, i)), pl.BlockSpec((tk, tn), lambda i, j, kk: (kk, j)),
               pl.BlockSpec((tm, tn), lambda i, j, kk: (i, j)), (m, n), out_dtype, (tm, tn), k_axis=2, nk=nk, alpha=alpha)


def _ffn_in_fwd(name, n, wblk):
    t = n.shape[0]
    tm = _pick(t, (1088, 768, 512, 256, 128))
    return _mm(name, n, wblk, _NN, (t // tm, 2, 4),
               pl.BlockSpec((tm, D_MODEL), lambda i, s, j: (i, 0)),
               pl.BlockSpec((None, None, D_MODEL, FF_SHARD_P), lambda i, s, j: (s, j, 0, 0)),
               pl.BlockSpec((None, tm, FF_SHARD_P), lambda i, s, j: (s, i, j)), (2, t, D_FF_P), F32, None)


def _ffn_in_bwd_x(name, dgu, wblk, dep=None):
    t = dgu.shape[1]
    tm = _pick(t, (1088, 768, 512, 256, 128))
    return _mm(name, dgu, wblk, _NT, (t // tm, 8),
               pl.BlockSpec((None, tm, FF_SHARD_P), lambda i, kk: (kk // 4, i, kk % 4)),
               pl.BlockSpec((None, None, D_MODEL, FF_SHARD_P), lambda i, kk: (kk // 4, kk % 4, 0, 0)),
               pl.BlockSpec((tm, D_MODEL), lambda i, kk: (i, 0)), (t, D_MODEL), F32, (tm, D_MODEL), k_axis=1, nk=8, dep=dep)


def _ffn_in_bwd_w(name, n, dgu):
    t = n.shape[0]
    tk = _pick(t, (1088, 768, 512, 256, 128))
    nk = t // tk
    return _mm(name, n, dgu, _TN, (2, 4, nk),
               pl.BlockSpec((tk, D_MODEL), lambda s, j, kk: (kk, 0)),
               pl.BlockSpec((None, tk, FF_SHARD_P), lambda s, j, kk: (s, kk, j)),
               pl.BlockSpec((None, None, D_MODEL, FF_SHARD_P), lambda s, j, kk: (s, j, 0, 0)),
               (2, 4, D_MODEL, FF_SHARD_P), BF16, (D_MODEL, FF_SHARD_P), k_axis=2, nk=nk)


def _rms_fwd(name, h, g):
    t = h.shape[0]
    tm = _pick(t, (544, 384, 256, 128))

    def body(h_ref, g_ref, o_ref):
        hv = h_ref[...]
        r = lax.rsqrt(jnp.mean(hv * hv, axis=-1, keepdims=True) + EPS)
        o_ref[...] = ((hv * r) * g_ref[...]).astype(BF16)

    return pl.pallas_call(
        body, name=name, grid=(t // tm,),
        in_specs=[pl.BlockSpec((tm, D_MODEL), lambda i: (i, 0)), pl.BlockSpec((1, D_MODEL), lambda i: (0, 0))],
        out_specs=pl.BlockSpec((tm, D_MODEL), lambda i: (i, 0)),
        out_shape=jax.ShapeDtypeStruct((t, D_MODEL), BF16), compiler_params=_params(("parallel",)),
    )(h, g)


def _rms_bwd(name, h, g, dn, dres):
    t = h.shape[0]
    tm = _pick(t, (544, 384, 256, 128))

    def body(h_ref, g_ref, dn_ref, dres_ref, dh_ref, dg_ref):
        i = pl.program_id(0)
        hv = h_ref[...]
        dnv = dn_ref[...]
        r = lax.rsqrt(jnp.mean(hv * hv, axis=-1, keepdims=True) + EPS)
        tv = dnv * g_ref[...]
        dot = jnp.mean(tv * hv, axis=-1, keepdims=True)
        dh_ref[...] = dres_ref[...] + (r * tv - hv * (r * r * r * dot))
        part = jnp.sum(dnv * (hv * r), axis=0, keepdims=True)

        @pl.when(i == 0)
        def _():
            dg_ref[...] = part

        @pl.when(i > 0)
        def _():
            dg_ref[...] += part

    row = pl.BlockSpec((tm, D_MODEL), lambda i: (i, 0))
    vec = pl.BlockSpec((1, D_MODEL), lambda i: (0, 0))
    return pl.pallas_call(
        body, name=name, grid=(t // tm,), in_specs=[row, vec, row, row], out_specs=[row, vec],
        out_shape=[jax.ShapeDtypeStruct((t, D_MODEL), F32), jax.ShapeDtypeStruct((1, D_MODEL), F32)],
        compiler_params=_params(("arbitrary",)),
    )(h, g, dn, dres)


def _swiglu_fwd(name, gu):
    t = gu.shape[1]
    tm = _pick(t, (544, 384, 256, 128))

    def body(gu_ref, a_ref):
        g = gu_ref[0]
        u = gu_ref[1]
        a_ref[...] = ((g * jax.nn.sigmoid(g)) * u).astype(BF16)

    return pl.pallas_call(
        body, name=name, grid=(t // tm, 4),
        in_specs=[pl.BlockSpec((2, tm, FF_SHARD_P), lambda i, j: (0, i, j))],
        out_specs=pl.BlockSpec((tm, FF_SHARD_P), lambda i, j: (i, j)),
        out_shape=jax.ShapeDtypeStruct((t, D_FF_P), BF16), compiler_params=_params(("parallel", "parallel")),
    )(gu)


def _swiglu_bwd(name, gu, da):
    t = gu.shape[1]
    tm = _pick(t, (544, 384, 256, 128))

    def body(gu_ref, da_ref, o_ref):
        g = gu_ref[0]
        u = gu_ref[1]
        d = da_ref[...]
        sg = jax.nn.sigmoid(g)
        o_ref[0] = (d * u * (sg * (1.0 + g * (1.0 - sg)))).astype(BF16)
        o_ref[1] = (d * (g * sg)).astype(BF16)

    return pl.pallas_call(
        body, name=name, grid=(t // tm, 4),
        in_specs=[pl.BlockSpec((2, tm, FF_SHARD_P), lambda i, j: (0, i, j)),
                  pl.BlockSpec((tm, FF_SHARD_P), lambda i, j: (i, j))],
        out_specs=pl.BlockSpec((2, tm, FF_SHARD_P), lambda i, j: (0, i, j)),
        out_shape=jax.ShapeDtypeStruct((2, t, D_FF_P), BF16), compiler_params=_params(("parallel", "parallel")),
    )(gu, da)


def _gate_fwd(name, gates, ya, yb):
    t = gates.shape[0]
    tm = _pick(t, (272, 256, 128))

    def body(g_ref, ya_ref, yb_ref, o_ref):
        sa = jax.nn.sigmoid(g_ref[:, 0:D_MODEL])
        sb = jax.nn.sigmoid(g_ref[:, D_MODEL:2 * D_MODEL])
        o_ref[...] = (sa * ya_ref[...] + sb * yb_ref[...]).astype(BF16)

    blk = pl.BlockSpec((tm, D_MODEL), lambda i: (i, 0))
    wide = pl.BlockSpec((tm, 2 * D_MODEL), lambda i: (i, 0))
    return pl.pallas_call(
        body, name=name, grid=(t // tm,), in_specs=[wide, blk, blk], out_specs=blk,
        out_shape=jax.ShapeDtypeStruct((t, D_MODEL), BF16), compiler_params=_params(("parallel",)),
    )(gates, ya, yb)


def _gate_bwd(name, gates, ya, yb, dmix):
    t = gates.shape[0]
    tm = _pick(t, (272, 256, 128))

    def body(g_ref, ya_ref, yb_ref, dm_ref, dya_ref, dyb_ref, dg_ref):
        dm = dm_ref[...]
        sa = jax.nn.sigmoid(g_ref[:, 0:D_MODEL])
        sb = jax.nn.sigmoid(g_ref[:, D_MODEL:2 * D_MODEL])
        dya_ref[...] = (dm * sa).astype(BF16)
        dyb_ref[...] = (dm * sb).astype(BF16)
        dg_ref[:, 0:D_MODEL] = (dm * ya_ref[...] * (sa * (1.0 - sa))).astype(BF16)
        dg_ref[:, D_MODEL:2 * D_MODEL] = (dm * yb_ref[...] * (sb * (1.0 - sb))).astype(BF16)

    blk = pl.BlockSpec((tm, D_MODEL), lambda i: (i, 0))
    wide = pl.BlockSpec((tm, 2 * D_MODEL), lambda i: (i, 0))
    out = jax.ShapeDtypeStruct((t, D_MODEL), BF16)
    return pl.pallas_call(
        body, name=name, grid=(t // tm,), in_specs=[wide, blk, blk, blk], out_specs=[blk, blk, wide],
        out_shape=[out, out, jax.ShapeDtypeStruct((t, 2 * D_MODEL), BF16)], compiler_params=_params(("parallel",)),
    )(gates, ya, yb, dmix)


def _loss_head(name, h3, gf, tgt):
    b, l, _ = h3.shape
    nb = l // BLOCK

    def body(h_ref, g_ref, t_ref, dh_ref, loss_ref, dg_ref):
        first = (pl.program_id(0) == 0) & (pl.program_id(1) == 0)
        real = (pl.program_id(1) > 0).astype(F32)
        hv = h_ref[...]
        g = g_ref[...]
        r = lax.rsqrt(jnp.mean(hv * hv, axis=-1, keepdims=True) + EPS)
        xn = hv * r
        err = (xn * g - t_ref[...]) * real
        lpart = 0.5 * jnp.sum(jnp.mean(err * err, axis=-1, keepdims=True), axis=0, keepdims=True)
        dy = err * (1.0 / D_MODEL)
        tv = dy * g
        dot = jnp.mean(tv * hv, axis=-1, keepdims=True)
        dh_ref[...] = r * tv - hv * (r * r * r * dot)
        gpart = jnp.sum(dy * xn, axis=0, keepdims=True)

        @pl.when(first)
        def _():
            loss_ref[...] = jnp.zeros_like(loss_ref)
            dg_ref[...] = jnp.zeros_like(dg_ref)

        loss_ref[...] += jnp.broadcast_to(lpart, loss_ref.shape)
        dg_ref[...] += gpart

    return pl.pallas_call(
        body, name=name, grid=(b, nb),
        in_specs=[pl.BlockSpec((None, BLOCK, D_MODEL), lambda bi, n: (bi, n, 0)),
                  pl.BlockSpec((1, D_MODEL), lambda bi, n: (0, 0)),
                  pl.BlockSpec((None, BLOCK, D_MODEL), lambda bi, n: (bi, jnp.maximum(n - 1, 0), 0))],
        out_specs=[pl.BlockSpec((None, BLOCK, D_MODEL), lambda bi, n: (bi, n, 0)),
                   pl.BlockSpec((8, 128), lambda bi, n: (0, 0)),
                   pl.BlockSpec((1, D_MODEL), lambda bi, n: (0, 0))],
        out_shape=[jax.ShapeDtypeStruct(h3.shape, F32), jax.ShapeDtypeStruct((8, 128), F32),
                   jax.ShapeDtypeStruct((1, D_MODEL), F32)],
        compiler_params=_params(("arbitrary", "arbitrary")),
    )(h3, gf, tgt)


def _fgate_fwd(name, f3, bf_row):
    b, l, _ = f3.shape
    nb = l // BLOCK

    def body(f_ref, b_ref, c_ref):
        r_i = lax.broadcasted_iota(jnp.int32, (BLOCK, BLOCK), 0)
        c_i = lax.broadcasted_iota(jnp.int32, (BLOCK, BLOCK), 1)
        tri = (r_i >= c_i).astype(F32)
        carry = jnp.zeros((1, 128), F32)
        for blk in range(nb):
            rows = slice(blk * BLOCK, (blk + 1) * BLOCK)
            z = f_ref[rows, :] + b_ref[...]
            lf = jnp.minimum(z, 0.0) - jnp.log(1.0 + jnp.exp(-jnp.abs(z)))
            cb = jnp.dot(tri, lf, preferred_element_type=F32, precision=lax.Precision.HIGHEST) + carry
            c_ref[rows, :] = cb
            carry = cb[BLOCK - 1:BLOCK, :]

    return pl.pallas_call(
        body, name=name, grid=(b,),
        in_specs=[pl.BlockSpec((None, l, 128), lambda bi: (bi, 0, 0)),
                  pl.BlockSpec((1, 128), lambda bi: (0, 0))],
        out_specs=pl.BlockSpec((None, l, 128), lambda bi: (bi, 0, 0)),
        out_shape=jax.ShapeDtypeStruct((b, l, 128), F32), compiler_params=_params(("parallel",)),
    )(f3, bf_row)


def _fgate_bwd(name, f3, bf_row, dc):
    b, l, _ = f3.shape
    nb = l // BLOCK

    def body(f_ref, b_ref, dc_ref, df_ref, db_ref):
        r_i = lax.broadcasted_iota(jnp.int32, (BLOCK, BLOCK), 0)
        c_i = lax.broadcasted_iota(jnp.int32, (BLOCK, BLOCK), 1)
        tri = (r_i <= c_i).astype(F32)
        carry = jnp.zeros((1, 128), F32)
        total = jnp.zeros((1, 128), F32)
        for blk in range(nb - 1, -1, -1):
            rows = slice(blk * BLOCK, (blk + 1) * BLOCK)
            rc = jnp.dot(tri, dc_ref[rows, :], preferred_element_type=F32, precision=lax.Precision.HIGHEST) + carry
            carry = rc[0:1, :]
            z = f_ref[rows, :] + b_ref[...]
            df = rc * (1.0 / (1.0 + jnp.exp(z)))
            df_ref[rows, :] = df.astype(BF16)
            total = total + jnp.sum(df, axis=0, keepdims=True)

        @pl.when(pl.program_id(0) == 0)
        def _():
            db_ref[...] = total

        @pl.when(pl.program_id(0) > 0)
        def _():
            db_ref[...] += total

    return pl.pallas_call(
        body, name=name, grid=(b,),
        in_specs=[pl.BlockSpec((None, l, 128), lambda bi: (bi, 0, 0)),
                  pl.BlockSpec((1, 128), lambda bi: (0, 0)),
                  pl.BlockSpec((None, l, 128), lambda bi: (bi, 0, 0))],
        out_specs=[pl.BlockSpec((None, l, 128), lambda bi: (bi, 0, 0)), pl.BlockSpec((1, 128), lambda bi: (0, 0))],
        out_shape=[jax.ShapeDtypeStruct((b, l, 128), BF16), jax.ShapeDtypeStruct((1, 128), F32)],
        compiler_params=_params(("arbitrary",)),
    )(f3, bf_row, dc)


A_Q_BLK = B_SEG // A_WIDTH
A_K_BLK = (B_SEG + A_WIDTH) // 128
A_V_BLK = A_K_BLK + 1
A_SEG_BLK = B_SEG // A_SEG
STACK = A_HEADS * BLOCK


def _lane_lo():
    return lax.broadcasted_iota(jnp.int32, (1, 128), 1) < HEAD_DIM


def _stack_heads(x, masked):
    lo = _lane_lo()
    blks = [x[:, 128 * j:128 * (j + 1)] for j in range(4)]
    if not masked:
        return jnp.concatenate(blks + blks, axis=0)
    zero = jnp.zeros_like(blks[0])
    return jnp.concatenate([jnp.where(lo, bk, zero) for bk in blks] + [jnp.where(lo, zero, bk) for bk in blks], axis=0)


def _unstack_heads(y):
    lo = _lane_lo()
    return jnp.concatenate([jnp.where(lo, y[128 * j:128 * (j + 1)], y[128 * (4 + j):128 * (5 + j)]) for j in range(4)], axis=1)


def _swa_scores(q, kcat, n, slope):
    rows = STACK
    r_i = lax.broadcasted_iota(jnp.int32, (rows, 3 * BLOCK), 0)
    c_i = lax.broadcasted_iota(jnp.int32, (rows, 3 * BLOCK), 1)
    qpos = n * BLOCK + (r_i & (BLOCK - 1))
    seg = c_i >> 7
    kpos = jnp.where(seg == 0, c_i, (n - 2) * BLOCK + c_i)
    dist = qpos - kpos
    s = lax.dot_general(q, kcat, _NT, preferred_element_type=F32) * SCALE
    s = s - slope * dist.astype(F32)
    is_meta = seg == 0
    band = jnp.logical_not(is_meta) & (dist < BLOCK) & (kpos >= PREFIX)
    meta = is_meta & (c_i >= N_PAD)
    return jnp.where((dist >= 0) & (band | meta), s, NEG)


def _swa_specs():
    def kv(col_blk):
        return [pl.BlockSpec((None, BLOCK, 128), lambda b, n: (b, 0, col_blk)),
                pl.BlockSpec((None, BLOCK, 128), lambda b, n: (b, jnp.maximum(n - 1, 0), col_blk)),
                pl.BlockSpec((None, BLOCK, 128), lambda b, n: (b, n, col_blk))]

    q_spec = pl.BlockSpec((None, BLOCK, A_WIDTH), lambda b, n: (b, n, A_Q_BLK))
    o_spec = pl.BlockSpec((None, BLOCK, A_WIDTH), lambda b, n: (b, n, 0))
    col = pl.BlockSpec((STACK, 1), lambda b, n: (0, 0))
    lse_spec = pl.BlockSpec((None, A_HEADS, BLOCK, 1), lambda b, n: (b, 0, n, 0))
    return q_spec, kv(A_K_BLK), kv(A_V_BLK), o_spec, col, lse_spec


def _swa_fwd(name, qkv, slopes, sinks):
    b, l, _ = qkv.shape
    nb = l // BLOCK

    def body(q_ref, k0_ref, kp_ref, kc_ref, v0_ref, vp_ref, vc_ref, sl_ref, sk_ref, o_ref, lse_ref):
        n = pl.program_id(1)
        qs = _stack_heads(q_ref[...], True)
        kcat = jnp.concatenate([k0_ref[...], kp_ref[...], kc_ref[...]], axis=0)
        vcat = jnp.concatenate([v0_ref[...], vp_ref[...], vc_ref[...]], axis=0)
        s = _swa_scores(qs, kcat, n, sl_ref[...])
        sink = sk_ref[...]
        m = jnp.maximum(jnp.max(s, axis=-1, keepdims=True), sink)
        p = jnp.exp(s - m)
        den = jnp.sum(p, axis=-1, keepdims=True) + jnp.exp(sink - m)
        o = lax.dot_general(p.astype(BF16), vcat, _NN, preferred_element_type=F32) / den
        o_ref[...] = _unstack_heads(o).astype(BF16)
        lse_ref[...] = (m + jnp.log(den)).reshape(A_HEADS, BLOCK, 1)

    q_spec, k_specs, v_specs, o_spec, col, lse_spec = _swa_specs()
    return pl.pallas_call(
        body, name=name, grid=(b, nb),
        in_specs=[q_spec] + k_specs + v_specs + [col, col], out_specs=[o_spec, lse_spec],
        out_shape=[jax.ShapeDtypeStruct((b, l, A_WIDTH), BF16), jax.ShapeDtypeStruct((b, A_HEADS, l, 1), F32)],
        compiler_params=_params(("parallel", "parallel")),
    )(qkv, qkv, qkv, qkv, qkv, qkv, qkv, slopes, sinks)


def _swa_bwd(name, qkv, o, lse, do, slopes, sinks, dqkv):
    b, l, _ = qkv.shape
    nb = l // BLOCK

    def body(q_ref, k0_ref, kp_ref, kc_ref, v0_ref, vp_ref, vc_ref, o_ref, lse_ref, do_ref, sl_ref, sk_ref, _,
             dx_ref, ds_ref, dk_acc, dv_acc):
        bi = pl.program_id(0)
        n = pl.program_id(1)
        qs = _stack_heads(q_ref[...], True)
        dos = _stack_heads(do_ref[...], True)
        os_ = _stack_heads(o_ref[...], False)
        lsev = lse_ref[...].reshape(STACK, 1)
        kcat = jnp.concatenate([k0_ref[...], kp_ref[...], kc_ref[...]], axis=0)
        vcat = jnp.concatenate([v0_ref[...], vp_ref[...], vc_ref[...]], axis=0)
        s = _swa_scores(qs, kcat, n, sl_ref[...])
        p = jnp.exp(s - lsev)
        dsum = jnp.sum(dos.astype(F32) * os_.astype(F32), axis=-1, keepdims=True)
        dp = lax.dot_general(dos, vcat, _NT, preferred_element_type=F32)
        dsc = (p * (dp - dsum)).astype(BF16)
        dq = lax.dot_general(dsc, kcat, _NN, preferred_element_type=F32) * SCALE
        row0 = pl.multiple_of(n * BLOCK, BLOCK)
        dx_ref[pl.ds(row0, BLOCK), 0:A_WIDTH] = _unstack_heads(dq).astype(BF16)
        dkc = lax.dot_general(dsc, qs, _TN, preferred_element_type=F32) * SCALE
        dvc = lax.dot_general(p.astype(BF16), dos, _TN, preferred_element_type=F32)

        @pl.when(n == 0)
        def _():
            dk_acc[...] = jnp.zeros_like(dk_acc)
            dv_acc[...] = jnp.zeros_like(dv_acc)

        starts = (0, pl.multiple_of(jnp.maximum(n - 1, 0) * BLOCK, BLOCK), row0)
        for t, st in enumerate(starts):
            dk_acc[pl.ds(st, BLOCK), :] += dkc[t * BLOCK:(t + 1) * BLOCK, :]
            dv_acc[pl.ds(st, BLOCK), :] += dvc[t * BLOCK:(t + 1) * BLOCK, :]

        @pl.when(n == nb - 1)
        def _():
            dx_ref[:, A_WIDTH:A_WIDTH + 128] = dk_acc[...].astype(BF16)
            dx_ref[:, A_WIDTH + 128:A_SEG] = dv_acc[...].astype(BF16)

        dsink = -(jnp.exp(sk_ref[...] - lsev) * dsum)
        r8 = lax.broadcasted_iota(jnp.int32, (8, 128), 0)
        acc = jnp.zeros((8, 128), F32)
        for hh in range(A_HEADS):
            acc = acc + jnp.where(r8 == hh, jnp.sum(dsink[hh * BLOCK:(hh + 1) * BLOCK, :]), 0.0)

        @pl.when((bi == 0) & (n == 0))
        def _():
            ds_ref[...] = jnp.zeros_like(ds_ref)

        ds_ref[...] += acc

    q_spec, k_specs, v_specs, o_spec, col, lse_spec = _swa_specs()
    return pl.pallas_call(
        body, name=name, grid=(b, nb),
        in_specs=[q_spec] + k_specs + v_specs + [o_spec, lse_spec, o_spec, col, col, pl.BlockSpec(memory_space=pl.ANY)],
        out_specs=[pl.BlockSpec((None, l, A_SEG), lambda bb, n: (bb, 0, A_SEG_BLK)),
                   pl.BlockSpec((8, 128), lambda bb, n: (0, 0))],
        out_shape=[jax.ShapeDtypeStruct(dqkv.shape, BF16), jax.ShapeDtypeStruct((8, 128), F32)],
        scratch_shapes=[pltpu.VMEM((l, 128), F32), pltpu.VMEM((l, 128), F32)],
        input_output_aliases={12: 0},
        compiler_params=_params(("arbitrary", "arbitrary")),
    )(qkv, qkv, qkv, qkv, qkv, qkv, qkv, o, lse, do, slopes, sinks, dqkv)


def _fox_scores(q, k, cq, ck, i):
    kh = k.shape[0]
    s = lax.dot_general(q, k, _NT, preferred_element_type=F32) * SCALE
    s = (s + cq) - ck
    qpos = i * BLOCK + lax.broadcasted_iota(jnp.int32, (BLOCK, kh), 0)
    kpos = lax.broadcasted_iota(jnp.int32, (BLOCK, kh), 1)
    return jnp.where((kpos <= qpos) & (kpos >= N_PAD), s, NEG)


def _pick_head(x, hh):
    lo = _lane_lo()
    return jnp.where(lo if hh == 0 else jnp.logical_not(lo), x, jnp.zeros_like(x))


def _fox_specs(l):
    pair = pl.BlockSpec((None, l, PAIR_W), lambda bi, hp: (bi, 0, hp))
    half = pl.BlockSpec((None, l, 128), lambda bi, hp: (bi, 0, hp))
    colv = pl.BlockSpec((None, 2, l, 1), lambda bi, hp: (bi, hp, 0, 0))
    rowv = pl.BlockSpec((None, 2, 1, l), lambda bi, hp: (bi, hp, 0, 0))
    return pair, half, colv, rowv


def _fox_fwd(name, qkv, c_col, c_row):
    b, l, _ = qkv.shape
    nb = l // BLOCK

    def body(x_ref, cc_ref, cr_ref, o_ref, lse_ref):
        for i in range(nb):
            rows = slice(i * BLOCK, (i + 1) * BLOCK)
            kh = (i + 1) * BLOCK
            qblk = x_ref[rows, 0:128]
            kv = x_ref[0:kh, 128:256]
            vv = x_ref[0:kh, 256:384]
            outs = []
            for hh in range(2):
                s = _fox_scores(_pick_head(qblk, hh), kv, cc_ref[hh, rows, :], cr_ref[hh, :, 0:kh], i)
                m = jnp.max(s, axis=-1, keepdims=True)
                p = jnp.exp(s - m)
                den = jnp.sum(p, axis=-1, keepdims=True)
                outs.append(lax.dot_general(p.astype(BF16), vv, _NN, preferred_element_type=F32) / den)
                lse_ref[hh, rows, :] = m + jnp.log(den)
            o_ref[rows, :] = jnp.where(_lane_lo(), outs[0], outs[1]).astype(BF16)

    pair, half, colv, rowv = _fox_specs(l)
    return pl.pallas_call(
        body, name=name, grid=(b, 4), in_specs=[pair, colv, rowv], out_specs=[half, colv],
        out_shape=[jax.ShapeDtypeStruct((b, l, B_WIDTH), BF16), jax.ShapeDtypeStruct((b, B_HEADS, l, 1), F32)],
        compiler_params=_params(("parallel", "parallel")),
    )(qkv, c_col, c_row)


def _fox_bwd(name, qkv, c_col, c_row, o, lse, do):
    b, l, _ = qkv.shape
    nb = l // BLOCK

    def body(x_ref, cc_ref, cr_ref, o_ref, lse_ref, do_ref, dx_ref, dcq_ref, dck_ref, dk_acc, dv_acc):
        dk_acc[...] = jnp.zeros_like(dk_acc)
        dv_acc[...] = jnp.zeros_like(dv_acc)
        dck_ref[...] = jnp.zeros_like(dck_ref)
        for i in range(nb):
            rows = slice(i * BLOCK, (i + 1) * BLOCK)
            kh = (i + 1) * BLOCK
            qblk = x_ref[rows, 0:128]
            kv = x_ref[0:kh, 128:256]
            vv = x_ref[0:kh, 256:384]
            doblk = do_ref[rows, :]
            ov = o_ref[rows, :].astype(F32)
            dqs = []
            for hh in range(2):
                qm = _pick_head(qblk, hh)
                dom = _pick_head(doblk, hh)
                s = _fox_scores(qm, kv, cc_ref[hh, rows, :], cr_ref[hh, :, 0:kh], i)
                p = jnp.exp(s - lse_ref[hh, rows, :])
                dsum = jnp.sum(dom.astype(F32) * ov, axis=-1, keepdims=True)
                dp = lax.dot_general(dom, vv, _NT, preferred_element_type=F32)
                ds = p * (dp - dsum)
                dsc = ds.astype(BF16)
                dqs.append(lax.dot_general(dsc, kv, _NN, preferred_element_type=F32) * SCALE)
                dk_acc[0:kh, :] += lax.dot_general(dsc, qm, _TN, preferred_element_type=F32) * SCALE
                dv_acc[0:kh, :] += lax.dot_general(p.astype(BF16), dom, _TN, preferred_element_type=F32)
                dcq_ref[hh, rows, :] = jnp.sum(ds, axis=-1, keepdims=True)
                dck_ref[hh, :, 0:kh] -= jnp.sum(ds, axis=0, keepdims=True)
            dx_ref[rows, 0:128] = jnp.where(_lane_lo(), dqs[0], dqs[1]).astype(BF16)
        dx_ref[:, 128:256] = dk_acc[...].astype(BF16)
        dx_ref[:, 256:384] = dv_acc[...].astype(BF16)

    pair, half, colv, rowv = _fox_specs(l)
    return pl.pallas_call(
        body, name=name, grid=(b, 4), in_specs=[pair, colv, rowv, half, colv, half],
        out_specs=[pair, colv, rowv],
        out_shape=[jax.ShapeDtypeStruct(qkv.shape, BF16), jax.ShapeDtypeStruct((b, B_HEADS, l, 1), F32),
                   jax.ShapeDtypeStruct((b, B_HEADS, 1, l), F32)],
        scratch_shapes=[pltpu.VMEM((l, 128), F32), pltpu.VMEM((l, 128), F32)],
        compiler_params=_params(("parallel", "parallel")),
    )(qkv, c_col, c_row, o, lse, do)


_FLIPS = ((0, 0, 1), (0, 1, 0), (0, 1, 1), (1, 0, 0), (1, 0, 1), (1, 1, 0), (1, 1, 1))


def _exchange(name, gather, scatter):
    ng, ns = len(gather), len(scatter)
    na = ng + ns
    npeer = len(_FLIPS)

    def body(*refs):
        ins = refs[:na]
        outs = refs[na:2 * na]
        send_sems, recv_sems, loc_sems = refs[2 * na:]
        x, y, c = lax.axis_index("x"), lax.axis_index("y"), lax.axis_index("c")
        me = 4 * x + 2 * y + c
        peers = []
        for fx, fy, fc in _FLIPS:
            px = 1 - x if fx else x
            py = 1 - y if fy else y
            pc = 1 - c if fc else c
            peers.append(((px, py, pc), 4 * px + 2 * py + pc))

        def remote(a, kk):
            dev, lin = peers[kk]
            src = ins[a] if a < ng else ins[a].at[lin]
            return pltpu.make_async_remote_copy(src_ref=src, dst_ref=outs[a].at[me], send_sem=send_sems.at[a * npeer + kk],
                                                recv_sem=recv_sems.at[a * npeer + kk], device_id=dev, device_id_type=MESH_ID)

        def arrival(a, kk):
            dev, lin = peers[kk]
            src = ins[a] if a < ng else ins[a].at[lin]
            return pltpu.make_async_remote_copy(src_ref=src, dst_ref=outs[a].at[lin], send_sem=send_sems.at[a * npeer + kk],
                                                recv_sem=recv_sems.at[a * npeer + kk], device_id=dev, device_id_type=MESH_ID)

        local = []
        for a in range(na):
            src = ins[a] if a < ng else ins[a].at[me]
            cp = pltpu.make_async_copy(src, outs[a].at[me], loc_sems.at[a])
            cp.start()
            local.append(cp)
        sent = [remote(a, kk) for kk in range(npeer) for a in range(na)]
        for cp in sent:
            cp.start()
        for kk in range(npeer):
            for a in range(na):
                arrival(a, kk).wait_recv()
        for cp in sent:
            cp.wait_send()
        for cp in local:
            cp.wait()

    arrs = list(gather) + list(scatter)
    out_shape = [jax.ShapeDtypeStruct((N_DEV,) + tuple(a.shape), a.dtype) for a in gather]
    out_shape += [jax.ShapeDtypeStruct(tuple(a.shape), a.dtype) for a in scatter]
    anyspec = pl.BlockSpec(memory_space=pl.ANY)
    return pl.pallas_call(
        body, name=name, in_specs=[anyspec] * na, out_specs=[anyspec] * na, out_shape=out_shape,
        scratch_shapes=[pltpu.SemaphoreType.DMA((na * npeer,)), pltpu.SemaphoreType.DMA((na * npeer,)),
                        pltpu.SemaphoreType.DMA((na,))],
        compiler_params=pltpu.CompilerParams(has_side_effects=True),
    )(*arrs)


def _peer_table():
    x, y, c = lax.axis_index("x"), lax.axis_index("y"), lax.axis_index("c")
    me = 4 * x + 2 * y + c
    peers = []
    for fx, fy, fc in _FLIPS:
        px = 1 - x if fx else x
        py = 1 - y if fy else y
        pc = 1 - c if fc else c
        peers.append(((px, py, pc), 4 * px + 2 * py + pc))
    return me, peers


_HBM = pl.BlockSpec(memory_space=pltpu.HBM)
_SEM = pl.BlockSpec(memory_space=pltpu.SEMAPHORE)
_ANY = pl.BlockSpec(memory_space=pl.ANY)
_EFFECT = pltpu.SideEffectType.DATAFLOW_SIDE_EFFECTING


def _split_copy(srcs_are_pieces, src_refs, land_refs, send_sem, recv_sem, a, kk, me, peers, arriving):
    dev, lin = peers[kk]
    npeer = len(_FLIPS)
    src = src_refs[a] if srcs_are_pieces[a] else src_refs[a].at[lin]
    dst = land_refs[a].at[lin] if arriving else land_refs[a].at[me]
    return pltpu.make_async_remote_copy(src_ref=src, dst_ref=dst, send_sem=send_sem.at[a * npeer + kk],
                                        recv_sem=recv_sem.at[a * npeer + kk], device_id=dev, device_id_type=MESH_ID)


def _xchg_start(name, gather, scatter, after=None):
    me_out = 4 * lax.axis_index("x") + 2 * lax.axis_index("y") + lax.axis_index("c")
    srcs = list(gather) + list(scatter)
    is_piece = [True] * len(gather) + [False] * len(scatter)
    lands = []
    for a, piece in zip(srcs, is_piece):
        own = a[None] if piece else lax.dynamic_slice_in_dim(a, me_out, 1, axis=0)
        shape = ((N_DEV,) + tuple(a.shape)) if piece else tuple(a.shape)
        start = (me_out,) + (0,) * (len(shape) - 1)
        lands.append(lax.dynamic_update_slice(jnp.zeros(shape, a.dtype), own, start))
    n = len(srcs)
    nsem = n * len(_FLIPS)
    has_after = after is not None

    def body(*refs):
        src_refs = refs[:n]
        land_refs = refs[n:2 * n]
        outs = refs[2 * n + (1 if has_after else 0):]
        send_sem, recv_sem = outs[0], outs[1]
        token = outs[-1]
        me, peers = _peer_table()
        for kk in range(len(_FLIPS)):
            for a in range(n):
                _split_copy(is_piece, src_refs, land_refs, send_sem, recv_sem, a, kk, me, peers, False).start()
        token[...] = jnp.zeros_like(token)

    out_shape = ([pltpu.SemaphoreType.DMA((nsem,)), pltpu.SemaphoreType.DMA((nsem,))]
                 + [pltpu.HBM(tuple(a.shape), a.dtype) for a in srcs] + [pltpu.HBM(tuple(a.shape), a.dtype) for a in lands]
                 + [jax.ShapeDtypeStruct((8, 128), F32)])
    args = [pltpu.with_memory_space_constraint(a, pltpu.HBM) for a in srcs + lands] + ([after] if has_after else [])
    res = pl.pallas_call(
        body, name=name, out_shape=out_shape,
        in_specs=[_HBM] * (2 * n) + ([_ANY] if has_after else []),
        out_specs=[_SEM, _SEM] + [_HBM] * (2 * n) + [pl.BlockSpec(memory_space=pltpu.VMEM)],
        input_output_aliases={i: 2 + i for i in range(2 * n)},
        compiler_params=pltpu.CompilerParams(has_side_effects=_EFFECT),
    )(*args)
    state = (res[0], res[1], list(res[2:2 + n]), list(res[2 + n:2 + 2 * n]), is_piece)
    return state, res[-1]


def _xchg_wait(name, state, after):
    send_sem, recv_sem, srcs, lands, is_piece = state
    n = len(srcs)

    def body(*refs):
        src_refs = refs[:n]
        land_refs = refs[n:2 * n]
        s_sem, r_sem = refs[2 * n], refs[2 * n + 1]
        me, peers = _peer_table()
        for kk in range(len(_FLIPS)):
            for a in range(n):
                cp = _split_copy(is_piece, src_refs, land_refs, s_sem, r_sem, a, kk, me, peers, True)
                cp.wait_send()
                cp.wait_recv()

    out_shape = [pltpu.HBM(tuple(a.shape), a.dtype) for a in srcs] + [pltpu.HBM(tuple(a.shape), a.dtype) for a in lands]
    res = pl.pallas_call(
        body, name=name, out_shape=out_shape,
        in_specs=[_HBM] * (2 * n) + [_SEM, _SEM, _ANY], out_specs=[_HBM] * (2 * n),
        input_output_aliases={i: i for i in range(2 * n)},
        compiler_params=pltpu.CompilerParams(has_side_effects=_EFFECT),
    )(*srcs, *lands, send_sem, recv_sem, after)
    return list(res[n:])


def _adam_math(w, g, m, v):
    m = ADAM_B1 * m + (1.0 - ADAM_B1) * g
    v = ADAM_B2 * v + (1.0 - ADAM_B2) * (g * g)
    m_hat = m / (1.0 - ADAM_B1 ** ADAM_STEP)
    v_hat = v / (1.0 - ADAM_B2 ** ADAM_STEP)
    delta = -ADAM_LR * (m_hat / (jnp.sqrt(v_hat) + ADAM_EPS) + ADAM_WD * w)
    return delta, m, v


def _adam(name, w, m, v, parts):
    r, c = w.shape
    npart, _, cp = parts.shape
    tr = _pick(r, (256, 176, 128, 64, 16, 8, 1))

    def body(w_ref, m_ref, v_ref, p_ref, g_ref, d_ref, mo_ref, vo_ref):
        g = p_ref[0].astype(F32)
        for pp in range(1, npart):
            g = g + p_ref[pp].astype(F32)
        g = g[:, 0:c]
        delta, mn, vn = _adam_math(w_ref[...], g, m_ref[...], v_ref[...])
        g_ref[...] = g
        d_ref[...] = delta
        mo_ref[...] = mn
        vo_ref[...] = vn

    blk = pl.BlockSpec((tr, c), lambda i: (i, 0))
    out = jax.ShapeDtypeStruct((r, c), F32)
    return pl.pallas_call(
        body, name=name, grid=(r // tr,),
        in_specs=[blk, blk, blk, pl.BlockSpec((npart, tr, cp), lambda i: (0, i, 0))],
        out_specs=[blk, blk, blk, blk], out_shape=[out, out, out, out], compiler_params=_params(("parallel",)),
    )(w, m, v, parts)


def _small_sum(name, packs):
    def body(p_ref, o_ref):
        tot = p_ref[0]
        for pp in range(1, N_DEV):
            tot = tot + p_ref[pp]
        o_ref[0:8, :] = tot[0:8, :]
        o_ref[8:24, :] = tot[8:24, :] + tot[24:40, :]

    return pl.pallas_call(body, name=name, out_shape=jax.ShapeDtypeStruct((24, D_MODEL), F32),
                          compiler_params=_params())(packs)


def _local_step(x, tgt, g1, gm, g2, gf, b_forget, sinks, weights, send):
    b, s, _ = x.shape
    l = s + PREFIX
    t = b * l
    w1i, w1o, meta = weights("ffn1", x)
    h0 = jnp.concatenate([jnp.zeros((b, N_PAD, D_MODEL), F32), jnp.broadcast_to(meta[None], (b, N_META, D_MODEL)), x],
                         axis=1).reshape(t, D_MODEL)

    n1 = _rms_fwd("rms1_fwd", h0, g1)
    gu1 = _ffn_in_fwd("ffn1_in_fwd", n1, w1i)
    a1 = _swiglu_fwd("swiglu1_fwd", gu1)
    h1 = _mm_nn("ffn1_out_fwd", a1, w1o, alpha=0.5, res=h0)
    wi, wa, wb, wo = weights("mix", h1)
    um = _rms_fwd("rmsm_fwd", h1, gm)
    qkv = _mm_nn("proj_qkv_fwd", um, wi, out_dtype=BF16, tn_c=(768,), cols=(0, QKV_W))
    gates = _mm_nn("proj_gates_fwd", um, wi, tn_c=(256,), cols=(P_GATES, 2 * D_MODEL))
    f2 = _mm_nn("proj_f_fwd", um, wi, tn_c=(128,), cols=(P_F, 128))
    qkv3 = qkv.reshape(b, l, QKV_W)
    f3 = f2.reshape(b, l, 128)
    bf_row = jnp.pad(b_forget, ((0, 0), (0, 128 - B_HEADS)))
    cfull = _fgate_fwd("fgate_fwd", f3, bf_row)
    c8 = cfull[:, :, 0:B_HEADS].transpose(0, 2, 1)
    c_col = c8[..., None]
    c_row = c8[:, :, None, :]
    head_of_row = jnp.arange(STACK) // BLOCK
    slopes = jnp.exp2(-8.0 * (head_of_row + 1).astype(F32) / A_HEADS).reshape(STACK, 1)
    sink_rows = jnp.repeat(sinks.reshape(A_HEADS), BLOCK).reshape(STACK, 1)
    oa3, lse_a = _swa_fwd("swa_fwd", qkv3, slopes, sink_rows)
    ob3, lse_b = _fox_fwd("fox_fwd", qkv3, c_col, c_row)
    oa = oa3.reshape(t, A_WIDTH)
    ob = ob3.reshape(t, B_WIDTH)
    ya = _mm_nn("branch_a_fwd", oa, wa)
    yb = _mm_nn("branch_b_fwd", ob, wb)
    mixed = _gate_fwd("gate_fwd", gates, ya, yb)
    h2 = _mm_nn("mix_out_fwd", mixed, wo, res=h1)
    w2i, w2o = weights("ffn2", h2)
    n2 = _rms_fwd("rms2_fwd", h2, g2)
    gu2 = _ffn_in_fwd("ffn2_in_fwd", n2, w2i)
    a2 = _swiglu_fwd("swiglu2_fwd", gu2)
    h3 = _mm_nn("ffn2_out_fwd", a2, w2o, alpha=0.5, res=h2)

    dh3_3, loss_blk, dgf = _loss_head("loss_head", h3.reshape(b, l, D_MODEL), gf, tgt)
    dh3 = dh3_3.reshape(t, D_MODEL)

    def ffn_bwd(tag, dh, h_in, g_norm, n_in, gu, a, w_in_blk, w_out):
        dw_out = _mm_tn(tag + "_out_bwd_w", a, dh, alpha=0.5)
        da = _mm_nt(tag + "_out_bwd_x", dh, w_out, alpha=0.5, dep=send(tag + "_out", (dw_out,)))
        dgu = _swiglu_bwd("swiglu" + tag[-1] + "_bwd", gu, da)
        dw_in = _ffn_in_bwd_w(tag + "_in_bwd_w", n_in, dgu)
        dn = _ffn_in_bwd_x(tag + "_in_bwd_x", dgu, w_in_blk, dep=send(tag + "_in", (dw_in,)))
        dh_in, dg = _rms_bwd("rms" + tag[-1] + "_bwd", h_in, g_norm, dn, dh)
        return dh_in, dg

    dh2, dg2 = ffn_bwd("ffn2", dh3, h2, g2, n2, gu2, a2, w2i, w2o)

    dmix = _mm_nt("mix_out_bwd_x", dh2, wo, tn_c=(512,))
    dwo = _mm_tn("mix_out_bwd_w", mixed, dh2)
    dya, dyb, dgates = _gate_bwd("gate_bwd", gates, ya, yb, dmix)
    doa = _mm_nt("branch_a_bwd_x", dya, wa, out_dtype=BF16, tn_c=(512,))
    dob = _mm_nt("branch_b_bwd_x", dyb, wb, out_dtype=BF16, tn_c=(512,))
    dwa = _mm_tn("branch_a_bwd_w", oa, dya, tm_c=(512,))
    dwb = _mm_tn("branch_b_bwd_w", ob, dyb, tm_c=(512,))
    dqkv3, dcq, dck = _fox_bwd("fox_bwd", qkv3, c_col, c_row, ob3, lse_b, dob.reshape(b, l, B_WIDTH))
    dqkv3, dsink = _swa_bwd("swa_bwd", qkv3, oa3, lse_a, doa.reshape(b, l, A_WIDTH), slopes, sink_rows, dqkv3)
    dqkv = dqkv3.reshape(t, QKV_W)
    dc = (dcq[..., 0] + dck[:, :, 0, :]).transpose(0, 2, 1)
    dc = jnp.pad(dc, ((0, 0), (0, 0), (0, 128 - B_HEADS)))
    df3, dbf = _fgate_bwd("fgate_bwd", f3, bf_row, dc)
    df = df3.reshape(t, 128)
    dwi_qkv = _mm_tn("proj_qkv_bwd_w", um, dqkv, tm_c=(512,), tn_c=(768,))
    dwi_g = _mm_tn("proj_gates_bwd_w", um, dgates, tm_c=(512,), tn_c=(512,))
    dwi_f = _mm_tn("proj_f_bwd_w", um, df, tm_c=(512,), tn_c=(128,))
    token = send("mix", (dwi_qkv, dwi_g, dwi_f, dwa, dwb, dwo))
    dum = _mm_nt("proj_gates_bwd_x", dgates, wi, tn_c=(512,), tk_c=(256,), kcols=(P_GATES, 2 * D_MODEL), dep=token)
    dum = _mm_nt("proj_qkv_bwd_x", dqkv, wi, tn_c=(512,), tk_c=(768,), kcols=(0, QKV_W), res=dum)
    dum = _mm_nt("proj_f_bwd_x", df, wi, tn_c=(512,), tk_c=(128,), kcols=(P_F, 128), res=dum)
    dh1, dgm = _rms_bwd("rmsm_bwd", h1, gm, dum, dh2)

    dh0, dg1 = ffn_bwd("ffn1", dh1, h0, g1, n1, gu1, a1, w1i, w1o)
    dh0_3 = dh0.reshape(b, l, D_MODEL)
    grad_x = dh0_3[:, PREFIX:, :]
    dmeta = dh0_3[:, N_PAD:PREFIX, :].reshape(b * N_META, D_MODEL)

    misc = jnp.concatenate([dbf[:, 0:B_HEADS], dsink[:, 0].reshape(1, A_HEADS), loss_blk[0:1, 0:1]], axis=1)
    misc = jnp.pad(misc, ((0, 0), (0, D_MODEL - misc.shape[1])))
    row = lax.broadcasted_iota(jnp.int32, (8, D_MODEL), 0)
    vec = jnp.zeros((8, D_MODEL), F32)
    for i, piece in enumerate((dg1, dgm, dg2, dgf, misc)):
        vec = jnp.where(row == i, piece, vec)
    small = jnp.concatenate([vec, dmeta], axis=0)
    return grad_x, small


def _pad_to(a, rows, cols):
    return jnp.pad(a, ((0, rows - a.shape[0]), (0, cols - a.shape[1])))


def _ffn_out_from_gathered(g):
    w = g.reshape(4, FF_SHARD, D_MODEL)
    return jnp.pad(w, ((0, 0), (0, FF_SHARD_P - FF_SHARD), (0, 0))).reshape(D_FF_P, D_MODEL)


def _ffn_out_to_scatter(dw):
    return dw.reshape(4, FF_SHARD_P, D_MODEL)[:, 0:FF_SHARD, :].reshape(N_DEV, FFO_SHARD, D_MODEL)


def _proj_from_gathered(g):
    w = g[:, :, 0:WIN_SHARD].transpose(1, 0, 2).reshape(D_MODEL, W_IN_COLS)
    qa, ka, va = w[:, 0:512], w[:, 512:640], w[:, 640:768]
    qb, kb, vb = w[:, 768:1280], w[:, 1280:1792], w[:, 1792:2304]
    f, gates = w[:, 2304:2312], w[:, 2312:]
    parts = []
    for hp in range(4):
        cols = slice(128 * hp, 128 * (hp + 1))
        parts += [qb[:, cols], kb[:, cols], vb[:, cols]]
    parts += [qa[:, HEAD_DIM * h:HEAD_DIM * (h + 1)] for h in A_HEAD_ORDER]
    parts += [ka, va, gates, jnp.pad(f, ((0, 0), (0, 128 - B_HEADS)))]
    return jnp.concatenate(parts, axis=1)


def _proj_to_scatter(dqkv_w, dg_w, df_w):
    bseg, aseg = dqkv_w[:, 0:B_SEG], dqkv_w[:, B_SEG:QKV_W]
    qb = [bseg[:, PAIR_W * hp:PAIR_W * hp + 128] for hp in range(4)]
    kb = [bseg[:, PAIR_W * hp + 128:PAIR_W * hp + 256] for hp in range(4)]
    vb = [bseg[:, PAIR_W * hp + 256:PAIR_W * hp + 384] for hp in range(4)]
    qa = [aseg[:, HEAD_DIM * A_HEAD_ORDER.index(h):HEAD_DIM * (A_HEAD_ORDER.index(h) + 1)] for h in range(A_HEADS)]
    w = jnp.concatenate(qa + [aseg[:, A_WIDTH:A_SEG]] + qb + kb + vb + [df_w[:, 0:B_HEADS], dg_w], axis=1)
    w = w.reshape(D_MODEL, N_DEV, WIN_SHARD).transpose(1, 0, 2)
    return jnp.pad(w, ((0, 0), (0, 0), (0, WIN_SHARD_P - WIN_SHARD)))


def _a_rows_from_natural(w):
    return jnp.concatenate([w[HEAD_DIM * h:HEAD_DIM * (h + 1)] for h in A_HEAD_ORDER], axis=0)


def _a_rows_to_natural(w):
    return jnp.concatenate([w[HEAD_DIM * A_HEAD_ORDER.index(h):HEAD_DIM * (A_HEAD_ORDER.index(h) + 1)]
                            for h in range(A_HEADS)], axis=0)


def kernel(x, meta_tokens, ffn1_norm, ffn1_w_in, ffn1_w_out, mix_norm, w_in, b_forget, attn_sinks, w_branch_a, w_branch_b, w_out, ffn2_norm, ffn2_w_in, ffn2_w_out, final_norm, loss_target, m_meta_tokens, m_ffn1_norm, m_ffn1_w_in, m_ffn1_w_out, m_mix_norm, m_w_in, m_b_forget, m_attn_sinks, m_w_branch_a, m_w_branch_b, m_w_out, m_ffn2_norm, m_ffn2_w_in, m_ffn2_w_out, m_final_norm, v_meta_tokens, v_ffn1_norm, v_ffn1_w_in, v_ffn1_w_out, v_mix_norm, v_w_in, v_b_forget, v_attn_sinks, v_w_branch_a, v_w_branch_b, v_w_out, v_ffn2_norm, v_ffn2_w_in, v_ffn2_w_out, v_final_norm):
    me = 4 * lax.axis_index("x") + 2 * lax.axis_index("y") + lax.axis_index("c")

    shards = (
        _pad_to(ffn1_w_in[0].astype(BF16), D_MODEL, FF_SHARD_P),
        ffn1_w_out[0].astype(BF16),
        _pad_to(w_in[0].astype(BF16), D_MODEL, WIN_SHARD_P),
        w_branch_a[0].astype(BF16), w_branch_b[0].astype(BF16), w_out[0].astype(BF16),
        _pad_to(ffn2_w_in[0].astype(BF16), D_MODEL, FF_SHARD_P),
        ffn2_w_out[0].astype(BF16),
        meta_tokens,
    )
    s1i, s1o, swi, swa, swb, swo, s2i, s2o, smeta = shards
    gather_state = {}
    gather_state["ffn1"], tok = _xchg_start("gather_ffn1_start", (s1i, s1o, smeta), ())
    gather_state["mix"], tok = _xchg_start("gather_mix_start", (swi, swa, swb, swo), (), after=tok)
    gather_state["ffn2"], tok = _xchg_start("gather_ffn2_start", (s2i, s2o), (), after=tok)
    started = {"tok": tok}

    def weights(group, after):
        if group == "ffn1":
            after = started["tok"]
        got = _xchg_wait("gather_" + group + "_wait", gather_state[group], after)
        if group == "mix":
            gwi, gwa, gwb, gwo = got
            return (_proj_from_gathered(gwi), _a_rows_from_natural(gwa.transpose(1, 0, 2).reshape(A_WIDTH, D_MODEL)),
                    gwb.transpose(1, 0, 2).reshape(B_WIDTH, D_MODEL), gwo.reshape(D_MODEL, D_MODEL))
        w_in_blk = got[0].reshape(2, 4, D_MODEL, FF_SHARD_P)
        w_out_p = _ffn_out_from_gathered(got[1])
        if group == "ffn1":
            return w_in_blk, w_out_p, got[2].transpose(1, 0, 2).reshape(N_META, D_MODEL)
        return w_in_blk, w_out_p

    scatter_state = {}

    def send(group, grads):
        if group == "mix":
            dwi_qkv, dwi_g, dwi_f, dwa, dwb, dwo = grads
            dwa = _a_rows_to_natural(dwa)
            blocks = (_proj_to_scatter(dwi_qkv, dwi_g, dwi_f), dwa.reshape(A_WIDTH, N_DEV, 128).transpose(1, 0, 2),
                      dwb.reshape(B_WIDTH, N_DEV, 128).transpose(1, 0, 2), dwo.reshape(N_DEV, 128, D_MODEL))
        elif group.endswith("_in"):
            blocks = (grads[0].reshape(N_DEV, D_MODEL, FF_SHARD_P),)
        else:
            blocks = (_ffn_out_to_scatter(grads[0]),)
        scatter_state[group], token = _xchg_start("scatter_" + group + "_start", (), blocks)
        return token

    gf = final_norm.reshape(1, D_MODEL)
    grad_x, small = _local_step(x, loss_target, ffn1_norm, mix_norm, ffn2_norm, gf, b_forget, attn_sinks, weights, send)

    (packs,) = _exchange("gather_small", (small,), ())
    recv = {}
    after = packs
    for group in ("ffn2_out", "ffn2_in", "mix", "ffn1_out", "ffn1_in"):
        recv[group] = _xchg_wait("scatter_" + group + "_wait", scatter_state[group], after)
        after = recv[group][0]
    (r2o,), (r2i,), (rwi, rwa, rwb, rwo), (r1o,), (r1i,) = (recv[g] for g in ("ffn2_out", "ffn2_in", "mix", "ffn1_out", "ffn1_in"))

    out = {}
    for nm, w, m, v, parts in (
        ("ffn1_w_in", ffn1_w_in, m_ffn1_w_in, v_ffn1_w_in, r1i), ("ffn1_w_out", ffn1_w_out, m_ffn1_w_out, v_ffn1_w_out, r1o),
        ("w_in", w_in, m_w_in, v_w_in, rwi), ("w_branch_a", w_branch_a, m_w_branch_a, v_w_branch_a, rwa),
        ("w_branch_b", w_branch_b, m_w_branch_b, v_w_branch_b, rwb), ("w_out", w_out, m_w_out, v_w_out, rwo),
        ("ffn2_w_in", ffn2_w_in, m_ffn2_w_in, v_ffn2_w_in, r2i), ("ffn2_w_out", ffn2_w_out, m_ffn2_w_out, v_ffn2_w_out, r2o),
    ):
        res4 = _adam("adam_" + nm, w[0], m[0], v[0], parts)
        out[nm] = tuple(r[None] for r in res4)

    tot = _small_sum("small_sum", packs)
    loss = tot[4, 2 * B_HEADS]
    g_meta = lax.dynamic_slice(tot[8:24, :], (0, me * 128), (N_META, 128))
    out["meta_tokens"] = tuple(_adam("adam_meta_tokens", meta_tokens, m_meta_tokens, v_meta_tokens, g_meta[None]))

    def pack_small(n1, nm, n2, nf, bfv, skv):
        misc = jnp.pad(jnp.concatenate([bfv, skv], axis=1), ((0, 0), (0, D_MODEL - 2 * B_HEADS)))
        row = lax.broadcasted_iota(jnp.int32, (8, D_MODEL), 0)
        vec = jnp.zeros((8, D_MODEL), F32)
        for i, piece in enumerate((n1, nm, n2, nf.reshape(1, D_MODEL), misc)):
            vec = jnp.where(row == i, piece, vec)
        return vec

    w_pack = pack_small(ffn1_norm, mix_norm, ffn2_norm, final_norm, b_forget, attn_sinks)
    m_pack = pack_small(m_ffn1_norm, m_mix_norm, m_ffn2_norm, m_final_norm, m_b_forget, m_attn_sinks)
    v_pack = pack_small(v_ffn1_norm, v_mix_norm, v_ffn2_norm, v_final_norm, v_b_forget, v_attn_sinks)
    small4 = _adam("adam_small", w_pack, m_pack, v_pack, tot[0:8][None])
    for i, nm in enumerate(("ffn1_norm", "mix_norm", "ffn2_norm")):
        out[nm] = tuple(r[i:i + 1] for r in small4)
    out["final_norm"] = tuple(r[3] for r in small4)
    out["b_forget"] = tuple(r[4:5, 0:B_HEADS] for r in small4)
    out["attn_sinks"] = tuple(r[4:5, B_HEADS:2 * B_HEADS] for r in small4)

    names = ("meta_tokens", "ffn1_norm", "ffn1_w_in", "ffn1_w_out", "mix_norm", "w_in", "b_forget", "attn_sinks",
             "w_branch_a", "w_branch_b", "w_out", "ffn2_norm", "ffn2_w_in", "ffn2_w_out", "final_norm")
    return (loss, grad_x) + tuple(out[nm][kind] for kind in range(4) for nm in names)
```

```python
import jax
import jax.numpy as jnp
from jax import lax
from jax.experimental import pallas as pl
from jax.experimental.pallas import tpu as pltpu

F32 = jnp.float32
BF16 = jnp.bfloat16

D_MODEL = 1024
N_META = 16
BLOCK = 128
PREFIX = 128
N_PAD = PREFIX - N_META
HEAD_DIM = 64
A_HEADS = 8
A_KV_HEADS = 2
A_GROUP = 4
B_HEADS = 8
A_WIDTH = 512
A_KV_WIDTH = 128
B_WIDTH = 512
D_FF = 2816
N_DEV = 8
FF_SHARD = 2 * D_FF // N_DEV
FF_SHARD_P = 768
FFO_SHARD = D_FF // N_DEV
FFO_SHARD_P = FF_SHARD_P // 2
D_FF_P = 4 * FF_SHARD_P
W_IN_COLS = 4360
WIN_SHARD = W_IN_COLS // N_DEV
WIN_SHARD_P = 640
PAIR_W = 3 * 128
B_SEG = 4 * PAIR_W
A_SEG = A_WIDTH + 2 * A_KV_WIDTH
QKV_W = B_SEG + A_SEG
P_GATES = QKV_W
P_F = QKV_W + 2 * D_MODEL
PROJ_P = P_F + 128
A_HEAD_ORDER = (0, 4, 1, 5, 2, 6, 3, 7)
EPS = 1e-6
NEG = -1e30
SCALE = HEAD_DIM ** -0.5
ADAM_LR = 0.001
ADAM_B1 = 0.9
ADAM_B2 = 0.999
ADAM_EPS = 1e-08
ADAM_WD = 0.01
ADAM_STEP = 10
VMEM_LIMIT = 48 * 1024 * 1024
MESH_ID = pl.DeviceIdType.MESH
SMALL_ROWS = 40

_NN = (((1,), (0,)), ((), ()))
_NT = (((1,), (1,)), ((), ()))
_TN = (((0,), (0,)), ((), ()))


def _params(sem=None):
    return pltpu.CompilerParams(dimension_semantics=sem, vmem_limit_bytes=VMEM_LIMIT)


def _pick(n, cands):
    for c in cands:
        if n % c == 0:
            return c
    raise ValueError(f"no tile for {n}")


def _bf(v):
    return v if v.dtype == BF16 else v.astype(BF16)


def _mm(name, a, b, dims, grid, a_spec, b_spec, o_spec, out_shape, out_dtype, acc_shape, k_axis=None, nk=1,
        alpha=1.0, res=None, res_spec=None, dep=None):
    has_res = res is not None
    has_dep = dep is not None

    def body(*refs):
        a_ref, b_ref = refs[0], refs[1]
        r_ref = refs[2] if has_res else None
        o_ref = refs[2 + has_res + has_dep]

        def finish(acc):
            if alpha != 1.0:
                acc = acc * alpha
            if has_res:
                acc = acc + r_ref[...]
            o_ref[...] = acc.astype(o_ref.dtype)

        part = lax.dot_general(_bf(a_ref[...]), _bf(b_ref[...]), dims, preferred_element_type=F32)
        if nk == 1:
            finish(part)
        else:
            acc_ref = refs[-1]
            k = pl.program_id(k_axis)

            @pl.when(k == 0)
            def _():
                acc_ref[...] = part

            @pl.when(k > 0)
            def _():
                acc_ref[...] += part

            @pl.when(k == nk - 1)
            def _():
                finish(acc_ref[...])

    in_specs = [a_spec, b_spec] + ([res_spec] if has_res else []) + ([pl.BlockSpec(memory_space=pl.ANY)] if has_dep else [])
    args = (a, b) + ((res,) if has_res else ()) + ((dep,) if has_dep else ())
    sem = tuple("arbitrary" if (nk > 1 and i == k_axis) else "parallel" for i in range(len(grid)))
    return pl.pallas_call(
        body, name=name, grid=grid, in_specs=in_specs, out_specs=o_spec,
        out_shape=jax.ShapeDtypeStruct(out_shape, out_dtype),
        scratch_shapes=[pltpu.VMEM(acc_shape, F32)] if nk > 1 else [],
        compiler_params=_params(sem),
    )(*args)


def _mm_nn(name, a, b, out_dtype=F32, alpha=1.0, res=None, tn_c=(512, 640, 256, 128), cols=None):
    t, k = a.shape
    c0, n = (0, b.shape[1]) if cols is None else cols
    tm = _pick(t, (1088, 768, 512, 256, 128))
    tn = _pick(n, tn_c)
    assert c0 % tn == 0
    jb = c0 // tn
    return _mm(name, a, b, _NN, (t // tm, n // tn),
               pl.BlockSpec((tm, k), lambda i, j: (i, 0)), pl.BlockSpec((k, tn), lambda i, j: (0, jb + j)),
               pl.BlockSpec((tm, tn), lambda i, j: (i, j)), (t, n), out_dtype, None,
               alpha=alpha, res=res, res_spec=pl.BlockSpec((tm, tn), lambda i, j: (i, j)))


def _mm_nt(name, a, b, out_dtype=F32, alpha=1.0, tn_c=(768, 512, 256, 128), tk_c=None, dep=None, res=None, kcols=None):
    t, k = a.shape
    n = b.shape[0]
    c0 = 0 if kcols is None else kcols[0]
    tm = _pick(t, (1088, 768, 512, 256, 128))
    tn = _pick(n, tn_c)
    tk = k if tk_c is None else _pick(k, tk_c)
    nk = k // tk
    assert c0 % tk == 0
    kb = c0 // tk
    return _mm(name, a, b, _NT, (t // tm, n // tn, nk),
               pl.BlockSpec((tm, tk), lambda i, j, kk: (i, kk)), pl.BlockSpec((tn, tk), lambda i, j, kk: (j, kb + kk)),
               pl.BlockSpec((tm, tn), lambda i, j, kk: (i, j)), (t, n), out_dtype, (tm, tn), k_axis=2, nk=nk, alpha=alpha,
               dep=dep, res=res, res_spec=pl.BlockSpec((tm, tn), lambda i, j, kk: (i, j)))


def _mm_tn(name, a, b, out_dtype=BF16, alpha=1.0, tm_c=(768, 512, 256, 128), tn_c=(512, 640, 256, 128)):
    t, m = a.shape
    n = b.shape[1]
    tm = _pick(m, tm_c)
    tn = _pick(n, tn_c)
    tk = _pick(t, (1088, 768, 512, 256, 128))
    nk = t // tk
    return _mm(name, a, b, _TN, (m // tm, n // tn, nk),
               pl.BlockSpec((tk, tm), lambda i, j, kk: (kk, i)), pl.BlockSpec((tk, tn), lambda i, j, kk: (kk, j)),
               pl.BlockSpec((tm, tn), lambda i, j, kk: (i, j)), (m, n), out_dtype, (tm, tn), k_axis=2, nk=nk, alpha=alpha)


def _silu_grads(g, u, d):
    sg = jax.nn.sigmoid(g)
    return d * u * (sg * (1.0 + g * (1.0 - sg))), d * (g * sg)


def _ffn_in_fwd(name, n, wblk):
    t = n.shape[0]
    tm = _pick(t, (1088, 768, 512, 256, 128))

    def body(n_ref, w_ref, gu_ref, a_ref):
        nv = n_ref[...]
        g = lax.dot_general(nv, w_ref[0], _NN, preferred_element_type=F32)
        u = lax.dot_general(nv, w_ref[1], _NN, preferred_element_type=F32)
        gu_ref[0] = g.astype(BF16)
        gu_ref[1] = u.astype(BF16)
        a_ref[...] = ((g * jax.nn.sigmoid(g)) * u).astype(BF16)

    return pl.pallas_call(
        body, name=name, grid=(t // tm, 4),
        in_specs=[pl.BlockSpec((tm, D_MODEL), lambda i, j: (i, 0)),
                  pl.BlockSpec((2, None, D_MODEL, FF_SHARD_P), lambda i, j: (0, j, 0, 0))],
        out_specs=[pl.BlockSpec((2, tm, FF_SHARD_P), lambda i, j: (0, i, j)),
                   pl.BlockSpec((tm, FF_SHARD_P), lambda i, j: (i, j))],
        out_shape=[jax.ShapeDtypeStruct((2, t, D_FF_P), BF16), jax.ShapeDtypeStruct((t, D_FF_P), BF16)],
        compiler_params=_params(("parallel", "parallel")),
    )(n, wblk)


def _ffn_out_bwd_x(name, dh, w_out, gu, dep=None):
    t = dh.shape[0]
    tm = _pick(t, (1088, 768, 512, 256, 128))
    has_dep = dep is not None

    def body(dh_ref, w_ref, gu_ref, *rest):
        o_ref = rest[-1]
        da = lax.dot_general(_bf(dh_ref[...]), w_ref[...], _NT, preferred_element_type=F32) * 0.5
        dg, du = _silu_grads(gu_ref[0].astype(F32), gu_ref[1].astype(F32), da)
        o_ref[0] = dg.astype(BF16)
        o_ref[1] = du.astype(BF16)

    gu_spec = pl.BlockSpec((2, tm, FF_SHARD_P), lambda i, j: (0, i, j))
    return pl.pallas_call(
        body, name=name, grid=(t // tm, 4),
        in_specs=[pl.BlockSpec((tm, D_MODEL), lambda i, j: (i, 0)), pl.BlockSpec((FF_SHARD_P, D_MODEL), lambda i, j: (j, 0)),
                  gu_spec] + ([pl.BlockSpec(memory_space=pl.ANY)] if has_dep else []),
        out_specs=gu_spec, out_shape=jax.ShapeDtypeStruct((2, t, D_FF_P), BF16),
        compiler_params=_params(("parallel", "parallel")),
    )(*((dh, w_out, gu) + ((dep,) if has_dep else ())))


def _ffn_in_bwd_x(name, dgu, wblk, dep=None):
    t = dgu.shape[1]
    tm = _pick(t, (1088, 768, 512, 256, 128))
    return _mm(name, dgu, wblk, _NT, (t // tm, 8),
               pl.BlockSpec((None, tm, FF_SHARD_P), lambda i, kk: (kk // 4, i, kk % 4)),
               pl.BlockSpec((None, None, D_MODEL, FF_SHARD_P), lambda i, kk: (kk // 4, kk % 4, 0, 0)),
               pl.BlockSpec((tm, D_MODEL), lambda i, kk: (i, 0)), (t, D_MODEL), F32, (tm, D_MODEL), k_axis=1, nk=8, dep=dep)


def _ffn_in_bwd_w(name, n, dgu):
    t = n.shape[0]
    tk = _pick(t, (1088, 768, 512, 256, 128))
    nk = t // tk
    return _mm(name, n, dgu, _TN, (2, 4, nk),
               pl.BlockSpec((tk, D_MODEL), lambda s, j, kk: (kk, 0)),
               pl.BlockSpec((None, tk, FF_SHARD_P), lambda s, j, kk: (s, kk, j)),
               pl.BlockSpec((None, None, D_MODEL, FF_SHARD_P), lambda s, j, kk: (s, j, 0, 0)),
               (2, 4, D_MODEL, FF_SHARD_P), BF16, (D_MODEL, FF_SHARD_P), k_axis=2, nk=nk)


def _rms_fwd(name, h, g):
    t = h.shape[0]
    tm = _pick(t, (544, 384, 256, 128))

    def body(h_ref, g_ref, o_ref):
        hv = h_ref[...]
        r = lax.rsqrt(jnp.mean(hv * hv, axis=-1, keepdims=True) + EPS)
        o_ref[...] = ((hv * r) * g_ref[...]).astype(BF16)

    return pl.pallas_call(
        body, name=name, grid=(t // tm,),
        in_specs=[pl.BlockSpec((tm, D_MODEL), lambda i: (i, 0)), pl.BlockSpec((1, D_MODEL), lambda i: (0, 0))],
        out_specs=pl.BlockSpec((tm, D_MODEL), lambda i: (i, 0)),
        out_shape=jax.ShapeDtypeStruct((t, D_MODEL), BF16), compiler_params=_params(("parallel",)),
    )(h, g)


def _rms_bwd(name, h, g, dn, dres):
    t = h.shape[0]
    tm = _pick(t, (544, 384, 256, 128))

    def body(h_ref, g_ref, dn_ref, dres_ref, dh_ref, dg_ref):
        i = pl.program_id(0)
        hv = h_ref[...]
        dnv = dn_ref[...]
        r = lax.rsqrt(jnp.mean(hv * hv, axis=-1, keepdims=True) + EPS)
        tv = dnv * g_ref[...]
        dot = jnp.mean(tv * hv, axis=-1, keepdims=True)
        dh_ref[...] = dres_ref[...] + (r * tv - hv * (r * r * r * dot))
        part = jnp.sum(dnv * (hv * r), axis=0, keepdims=True)

        @pl.when(i == 0)
        def _():
            dg_ref[...] = part

        @pl.when(i > 0)
        def _():
            dg_ref[...] += part

    row = pl.BlockSpec((tm, D_MODEL), lambda i: (i, 0))
    vec = pl.BlockSpec((1, D_MODEL), lambda i: (0, 0))
    return pl.pallas_call(
        body, name=name, grid=(t // tm,), in_specs=[row, vec, row, row], out_specs=[row, vec],
        out_shape=[jax.ShapeDtypeStruct((t, D_MODEL), F32), jax.ShapeDtypeStruct((1, D_MODEL), F32)],
        compiler_params=_params(("arbitrary",)),
    )(h, g, dn, dres)


def _gate_fwd(name, gates, ya, yb):
    t = gates.shape[0]
    tm = _pick(t, (272, 256, 128))

    def body(g_ref, ya_ref, yb_ref, o_ref):
        sa = jax.nn.sigmoid(g_ref[:, 0:D_MODEL])
        sb = jax.nn.sigmoid(g_ref[:, D_MODEL:2 * D_MODEL])
        o_ref[...] = (sa * ya_ref[...] + sb * yb_ref[...]).astype(BF16)

    blk = pl.BlockSpec((tm, D_MODEL), lambda i: (i, 0))
    wide = pl.BlockSpec((tm, 2 * D_MODEL), lambda i: (i, 0))
    return pl.pallas_call(
        body, name=name, grid=(t // tm,), in_specs=[wide, blk, blk], out_specs=blk,
        out_shape=jax.ShapeDtypeStruct((t, D_MODEL), BF16), compiler_params=_params(("parallel",)),
    )(gates, ya, yb)


def _gate_bwd(name, gates, ya, yb, dmix):
    t = gates.shape[0]
    tm = _pick(t, (272, 256, 128))

    def body(g_ref, ya_ref, yb_ref, dm_ref, dya_ref, dyb_ref, dg_ref):
        dm = dm_ref[...]
        sa = jax.nn.sigmoid(g_ref[:, 0:D_MODEL])
        sb = jax.nn.sigmoid(g_ref[:, D_MODEL:2 * D_MODEL])
        dya_ref[...] = (dm * sa).astype(BF16)
        dyb_ref[...] = (dm * sb).astype(BF16)
        dg_ref[:, 0:D_MODEL] = (dm * ya_ref[...] * (sa * (1.0 - sa))).astype(BF16)
        dg_ref[:, D_MODEL:2 * D_MODEL] = (dm * yb_ref[...] * (sb * (1.0 - sb))).astype(BF16)

    blk = pl.BlockSpec((tm, D_MODEL), lambda i: (i, 0))
    wide = pl.BlockSpec((tm, 2 * D_MODEL), lambda i: (i, 0))
    out = jax.ShapeDtypeStruct((t, D_MODEL), BF16)
    return pl.pallas_call(
        body, name=name, grid=(t // tm,), in_specs=[wide, blk, blk, blk], out_specs=[blk, blk, wide],
        out_shape=[out, out, jax.ShapeDtypeStruct((t, 2 * D_MODEL), BF16)], compiler_params=_params(("parallel",)),
    )(gates, ya, yb, dmix)


def _loss_head(name, h3, gf, tgt):
    b, l, _ = h3.shape
    nb = l // BLOCK

    def body(h_ref, g_ref, t_ref, dh_ref, loss_ref, dg_ref):
        first = (pl.program_id(0) == 0) & (pl.program_id(1) == 0)
        real = (pl.program_id(1) > 0).astype(F32)
        hv = h_ref[...]
        g = g_ref[...]
        r = lax.rsqrt(jnp.mean(hv * hv, axis=-1, keepdims=True) + EPS)
        xn = hv * r
        err = (xn * g - t_ref[...]) * real
        lpart = 0.5 * jnp.sum(jnp.mean(err * err, axis=-1, keepdims=True), axis=0, keepdims=True)
        dy = err * (1.0 / D_MODEL)
        tv = dy * g
        dot = jnp.mean(tv * hv, axis=-1, keepdims=True)
        dh_ref[...] = r * tv - hv * (r * r * r * dot)
        gpart = jnp.sum(dy * xn, axis=0, keepdims=True)

        @pl.when(first)
        def _():
            loss_ref[...] = jnp.zeros_like(loss_ref)
            dg_ref[...] = jnp.zeros_like(dg_ref)

        loss_ref[...] += jnp.broadcast_to(lpart, loss_ref.shape)
        dg_ref[...] += gpart

    return pl.pallas_call(
        body, name=name, grid=(b, nb),
        in_specs=[pl.BlockSpec((None, BLOCK, D_MODEL), lambda bi, n: (bi, n, 0)),
                  pl.BlockSpec((1, D_MODEL), lambda bi, n: (0, 0)),
                  pl.BlockSpec((None, BLOCK, D_MODEL), lambda bi, n: (bi, jnp.maximum(n - 1, 0), 0))],
        out_specs=[pl.BlockSpec((None, BLOCK, D_MODEL), lambda bi, n: (bi, n, 0)),
                   pl.BlockSpec((8, 128), lambda bi, n: (0, 0)),
                   pl.BlockSpec((1, D_MODEL), lambda bi, n: (0, 0))],
        out_shape=[jax.ShapeDtypeStruct(h3.shape, F32), jax.ShapeDtypeStruct((8, 128), F32),
                   jax.ShapeDtypeStruct((1, D_MODEL), F32)],
        compiler_params=_params(("arbitrary", "arbitrary")),
    )(h3, gf, tgt)


def _fgate_fwd(name, f3, bf_row):
    b, l, _ = f3.shape
    nb = l // BLOCK

    def body(f_ref, b_ref, c_ref):
        r_i = lax.broadcasted_iota(jnp.int32, (BLOCK, BLOCK), 0)
        c_i = lax.broadcasted_iota(jnp.int32, (BLOCK, BLOCK), 1)
        tri = (r_i >= c_i).astype(F32)
        carry = jnp.zeros((1, 128), F32)
        for blk in range(nb):
            rows = slice(blk * BLOCK, (blk + 1) * BLOCK)
            z = f_ref[rows, :] + b_ref[...]
            lf = jnp.minimum(z, 0.0) - jnp.log(1.0 + jnp.exp(-jnp.abs(z)))
            cb = jnp.dot(tri, lf, preferred_element_type=F32, precision=lax.Precision.HIGHEST) + carry
            c_ref[rows, :] = cb
            carry = cb[BLOCK - 1:BLOCK, :]

    return pl.pallas_call(
        body, name=name, grid=(b,),
        in_specs=[pl.BlockSpec((None, l, 128), lambda bi: (bi, 0, 0)),
                  pl.BlockSpec((1, 128), lambda bi: (0, 0))],
        out_specs=pl.BlockSpec((None, l, 128), lambda bi: (bi, 0, 0)),
        out_shape=jax.ShapeDtypeStruct((b, l, 128), F32), compiler_params=_params(("parallel",)),
    )(f3, bf_row)


def _fgate_bwd(name, f3, bf_row, dc):
    b, l, _ = f3.shape
    nb = l // BLOCK

    def body(f_ref, b_ref, dc_ref, df_ref, db_ref):
        r_i = lax.broadcasted_iota(jnp.int32, (BLOCK, BLOCK), 0)
        c_i = lax.broadcasted_iota(jnp.int32, (BLOCK, BLOCK), 1)
        tri = (r_i <= c_i).astype(F32)
        carry = jnp.zeros((1, 128), F32)
        total = jnp.zeros((1, 128), F32)
        for blk in range(nb - 1, -1, -1):
            rows = slice(blk * BLOCK, (blk + 1) * BLOCK)
            rc = jnp.dot(tri, dc_ref[rows, :], preferred_element_type=F32, precision=lax.Precision.HIGHEST) + carry
            carry = rc[0:1, :]
            z = f_ref[rows, :] + b_ref[...]
            df = rc * (1.0 / (1.0 + jnp.exp(z)))
            df_ref[rows, :] = df.astype(BF16)
            total = total + jnp.sum(df, axis=0, keepdims=True)

        @pl.when(pl.program_id(0) == 0)
        def _():
            db_ref[...] = total

        @pl.when(pl.program_id(0) > 0)
        def _():
            db_ref[...] += total

    return pl.pallas_call(
        body, name=name, grid=(b,),
        in_specs=[pl.BlockSpec((None, l, 128), lambda bi: (bi, 0, 0)),
                  pl.BlockSpec((1, 128), lambda bi: (0, 0)),
                  pl.BlockSpec((None, l, 128), lambda bi: (bi, 0, 0))],
        out_specs=[pl.BlockSpec((None, l, 128), lambda bi: (bi, 0, 0)), pl.BlockSpec((1, 128), lambda bi: (0, 0))],
        out_shape=[jax.ShapeDtypeStruct((b, l, 128), BF16), jax.ShapeDtypeStruct((1, 128), F32)],
        compiler_params=_params(("arbitrary",)),
    )(f3, bf_row, dc)


A_Q_BLK = B_SEG // A_WIDTH
A_K_BLK = (B_SEG + A_WIDTH) // 128
A_V_BLK = A_K_BLK + 1
A_SEG_BLK = B_SEG // A_SEG
STACK = A_HEADS * BLOCK


def _lane_lo():
    return lax.broadcasted_iota(jnp.int32, (1, 128), 1) < HEAD_DIM


def _stack_heads(x, masked):
    lo = _lane_lo()
    blks = [x[:, 128 * j:128 * (j + 1)] for j in range(4)]
    if not masked:
        return jnp.concatenate(blks + blks, axis=0)
    zero = jnp.zeros_like(blks[0])
    return jnp.concatenate([jnp.where(lo, bk, zero) for bk in blks] + [jnp.where(lo, zero, bk) for bk in blks], axis=0)


def _unstack_heads(y):
    lo = _lane_lo()
    return jnp.concatenate([jnp.where(lo, y[128 * j:128 * (j + 1)], y[128 * (4 + j):128 * (5 + j)]) for j in range(4)], axis=1)


def _swa_scores(q, kcat, n, slope):
    rows = STACK
    r_i = lax.broadcasted_iota(jnp.int32, (rows, 3 * BLOCK), 0)
    c_i = lax.broadcasted_iota(jnp.int32, (rows, 3 * BLOCK), 1)
    qpos = n * BLOCK + (r_i & (BLOCK - 1))
    seg = c_i >> 7
    kpos = jnp.where(seg == 0, c_i, (n - 2) * BLOCK + c_i)
    dist = qpos - kpos
    s = lax.dot_general(q, kcat, _NT, preferred_element_type=F32) * SCALE
    s = s - slope * dist.astype(F32)
    is_meta = seg == 0
    band = jnp.logical_not(is_meta) & (dist < BLOCK) & (kpos >= PREFIX)
    meta = is_meta & (c_i >= N_PAD)
    return jnp.where((dist >= 0) & (band | meta), s, NEG)


def _swa_specs():
    def kv(col_blk):
        return [pl.BlockSpec((None, BLOCK, 128), lambda b, n: (b, 0, col_blk)),
                pl.BlockSpec((None, BLOCK, 128), lambda b, n: (b, jnp.maximum(n - 1, 0), col_blk)),
                pl.BlockSpec((None, BLOCK, 128), lambda b, n: (b, n, col_blk))]

    q_spec = pl.BlockSpec((None, BLOCK, A_WIDTH), lambda b, n: (b, n, A_Q_BLK))
    o_spec = pl.BlockSpec((None, BLOCK, A_WIDTH), lambda b, n: (b, n, 0))
    col = pl.BlockSpec((STACK, 1), lambda b, n: (0, 0))
    lse_spec = pl.BlockSpec((None, A_HEADS, BLOCK, 1), lambda b, n: (b, 0, n, 0))
    return q_spec, kv(A_K_BLK), kv(A_V_BLK), o_spec, col, lse_spec


def _swa_fwd(name, qkv, slopes, sinks):
    b, l, _ = qkv.shape
    nb = l // BLOCK

    def body(q_ref, k0_ref, kp_ref, kc_ref, v0_ref, vp_ref, vc_ref, sl_ref, sk_ref, o_ref, lse_ref):
        n = pl.program_id(1)
        qs = _stack_heads(q_ref[...], True)
        kcat = jnp.concatenate([k0_ref[...], kp_ref[...], kc_ref[...]], axis=0)
        vcat = jnp.concatenate([v0_ref[...], vp_ref[...], vc_ref[...]], axis=0)
        s = _swa_scores(qs, kcat, n, sl_ref[...])
        sink = sk_ref[...]
        m = jnp.maximum(jnp.max(s, axis=-1, keepdims=True), sink)
        p = jnp.exp(s - m)
        den = jnp.sum(p, axis=-1, keepdims=True) + jnp.exp(sink - m)
        o = lax.dot_general(p.astype(BF16), vcat, _NN, preferred_element_type=F32) / den
        o_ref[...] = _unstack_heads(o).astype(BF16)
        lse_ref[...] = (m + jnp.log(den)).reshape(A_HEADS, BLOCK, 1)

    q_spec, k_specs, v_specs, o_spec, col, lse_spec = _swa_specs()
    return pl.pallas_call(
        body, name=name, grid=(b, nb),
        in_specs=[q_spec] + k_specs + v_specs + [col, col], out_specs=[o_spec, lse_spec],
        out_shape=[jax.ShapeDtypeStruct((b, l, A_WIDTH), BF16), jax.ShapeDtypeStruct((b, A_HEADS, l, 1), F32)],
        compiler_params=_params(("parallel", "parallel")),
    )(qkv, qkv, qkv, qkv, qkv, qkv, qkv, slopes, sinks)


def _swa_bwd(name, qkv, o, lse, do, slopes, sinks, dqkv):
    b, l, _ = qkv.shape
    nb = l // BLOCK

    def body(q_ref, k0_ref, kp_ref, kc_ref, v0_ref, vp_ref, vc_ref, o_ref, lse_ref, do_ref, sl_ref, sk_ref, _,
             dx_ref, ds_ref, dk_acc, dv_acc):
        bi = pl.program_id(0)
        n = pl.program_id(1)
        qs = _stack_heads(q_ref[...], True)
        dos = _stack_heads(do_ref[...], True)
        os_ = _stack_heads(o_ref[...], False)
        lsev = lse_ref[...].reshape(STACK, 1)
        kcat = jnp.concatenate([k0_ref[...], kp_ref[...], kc_ref[...]], axis=0)
        vcat = jnp.concatenate([v0_ref[...], vp_ref[...], vc_ref[...]], axis=0)
        s = _swa_scores(qs, kcat, n, sl_ref[...])
        p = jnp.exp(s - lsev)
        dsum = jnp.sum(dos.astype(F32) * os_.astype(F32), axis=-1, keepdims=True)
        dp = lax.dot_general(dos, vcat, _NT, preferred_element_type=F32)
        dsc = (p * (dp - dsum)).astype(BF16)
        dq = lax.dot_general(dsc, kcat, _NN, preferred_element_type=F32) * SCALE
        row0 = pl.multiple_of(n * BLOCK, BLOCK)
        dx_ref[pl.ds(row0, BLOCK), 0:A_WIDTH] = _unstack_heads(dq).astype(BF16)
        dkc = lax.dot_general(dsc, qs, _TN, preferred_element_type=F32) * SCALE
        dvc = lax.dot_general(p.astype(BF16), dos, _TN, preferred_element_type=F32)

        @pl.when(n == 0)
        def _():
            dk_acc[...] = jnp.zeros_like(dk_acc)
            dv_acc[...] = jnp.zeros_like(dv_acc)

        starts = (0, pl.multiple_of(jnp.maximum(n - 1, 0) * BLOCK, BLOCK), row0)
        for t, st in enumerate(starts):
            dk_acc[pl.ds(st, BLOCK), :] += dkc[t * BLOCK:(t + 1) * BLOCK, :]
            dv_acc[pl.ds(st, BLOCK), :] += dvc[t * BLOCK:(t + 1) * BLOCK, :]

        @pl.when(n == nb - 1)
        def _():
            dx_ref[:, A_WIDTH:A_WIDTH + 128] = dk_acc[...].astype(BF16)
            dx_ref[:, A_WIDTH + 128:A_SEG] = dv_acc[...].astype(BF16)

        dsink = -(jnp.exp(sk_ref[...] - lsev) * dsum)
        r8 = lax.broadcasted_iota(jnp.int32, (8, 128), 0)
        acc = jnp.zeros((8, 128), F32)
        for hh in range(A_HEADS):
            acc = acc + jnp.where(r8 == hh, jnp.sum(dsink[hh * BLOCK:(hh + 1) * BLOCK, :]), 0.0)

        @pl.when((bi == 0) & (n == 0))
        def _():
            ds_ref[...] = jnp.zeros_like(ds_ref)

        ds_ref[...] += acc

    q_spec, k_specs, v_specs, o_spec, col, lse_spec = _swa_specs()
    return pl.pallas_call(
        body, name=name, grid=(b, nb),
        in_specs=[q_spec] + k_specs + v_specs + [o_spec, lse_spec, o_spec, col, col, pl.BlockSpec(memory_space=pl.ANY)],
        out_specs=[pl.BlockSpec((None, l, A_SEG), lambda bb, n: (bb, 0, A_SEG_BLK)),
                   pl.BlockSpec((8, 128), lambda bb, n: (0, 0))],
        out_shape=[jax.ShapeDtypeStruct(dqkv.shape, BF16), jax.ShapeDtypeStruct((8, 128), F32)],
        scratch_shapes=[pltpu.VMEM((l, 128), F32), pltpu.VMEM((l, 128), F32)],
        input_output_aliases={12: 0},
        compiler_params=_params(("arbitrary", "arbitrary")),
    )(qkv, qkv, qkv, qkv, qkv, qkv, qkv, o, lse, do, slopes, sinks, dqkv)


def _fox_scores(q, k, cq, ck, i):
    kh = k.shape[0]
    s = lax.dot_general(q, k, _NT, preferred_element_type=F32) * SCALE
    s = (s + cq) - ck
    qpos = i * BLOCK + lax.broadcasted_iota(jnp.int32, (BLOCK, kh), 0)
    kpos = lax.broadcasted_iota(jnp.int32, (BLOCK, kh), 1)
    return jnp.where((kpos <= qpos) & (kpos >= N_PAD), s, NEG)


def _pick_head(x, hh):
    lo = _lane_lo()
    return jnp.where(lo if hh == 0 else jnp.logical_not(lo), x, jnp.zeros_like(x))


def _fox_specs(l):
    pair = pl.BlockSpec((None, l, PAIR_W), lambda bi, hp: (bi, 0, hp))
    half = pl.BlockSpec((None, l, 128), lambda bi, hp: (bi, 0, hp))
    colv = pl.BlockSpec((None, 2, l, 1), lambda bi, hp: (bi, hp, 0, 0))
    rowv = pl.BlockSpec((None, 2, 1, l), lambda bi, hp: (bi, hp, 0, 0))
    return pair, half, colv, rowv


def _fox_fwd(name, qkv, c_col, c_row):
    b, l, _ = qkv.shape
    nb = l // BLOCK

    def body(x_ref, cc_ref, cr_ref, o_ref, lse_ref):
        for i in range(nb):
            rows = slice(i * BLOCK, (i + 1) * BLOCK)
            kh = (i + 1) * BLOCK
            qblk = x_ref[rows, 0:128]
            kv = x_ref[0:kh, 128:256]
            vv = x_ref[0:kh, 256:384]
            outs = []
            for hh in range(2):
                s = _fox_scores(_pick_head(qblk, hh), kv, cc_ref[hh, rows, :], cr_ref[hh, :, 0:kh], i)
                m = jnp.max(s, axis=-1, keepdims=True)
                p = jnp.exp(s - m)
                den = jnp.sum(p, axis=-1, keepdims=True)
                outs.append(lax.dot_general(p.astype(BF16), vv, _NN, preferred_element_type=F32) / den)
                lse_ref[hh, rows, :] = m + jnp.log(den)
            o_ref[rows, :] = jnp.where(_lane_lo(), outs[0], outs[1]).astype(BF16)

    pair, half, colv, rowv = _fox_specs(l)
    return pl.pallas_call(
        body, name=name, grid=(b, 4), in_specs=[pair, colv, rowv], out_specs=[half, colv],
        out_shape=[jax.ShapeDtypeStruct((b, l, B_WIDTH), BF16), jax.ShapeDtypeStruct((b, B_HEADS, l, 1), F32)],
        compiler_params=_params(("parallel", "parallel")),
    )(qkv, c_col, c_row)


def _fox_bwd(name, qkv, c_col, c_row, o, lse, do):
    b, l, _ = qkv.shape
    nb = l // BLOCK

    def body(x_ref, cc_ref, cr_ref, o_ref, lse_ref, do_ref, dx_ref, dcq_ref, dck_ref, dk_acc, dv_acc):
        dk_acc[...] = jnp.zeros_like(dk_acc)
        dv_acc[...] = jnp.zeros_like(dv_acc)
        dck_ref[...] = jnp.zeros_like(dck_ref)
        for i in range(nb):
            rows = slice(i * BLOCK, (i + 1) * BLOCK)
            kh = (i + 1) * BLOCK
            qblk = x_ref[rows, 0:128]
            kv = x_ref[0:kh, 128:256]
            vv = x_ref[0:kh, 256:384]
            doblk = do_ref[rows, :]
            ov = o_ref[rows, :].astype(F32)
            dqs = []
            for hh in range(2):
                qm = _pick_head(qblk, hh)
                dom = _pick_head(doblk, hh)
                s = _fox_scores(qm, kv, cc_ref[hh, rows, :], cr_ref[hh, :, 0:kh], i)
                p = jnp.exp(s - lse_ref[hh, rows, :])
                dsum = jnp.sum(dom.astype(F32) * ov, axis=-1, keepdims=True)
                dp = lax.dot_general(dom, vv, _NT, preferred_element_type=F32)
                ds = p * (dp - dsum)
                dsc = ds.astype(BF16)
                dqs.append(lax.dot_general(dsc, kv, _NN, preferred_element_type=F32) * SCALE)
                dk_acc[0:kh, :] += lax.dot_general(dsc, qm, _TN, preferred_element_type=F32) * SCALE
                dv_acc[0:kh, :] += lax.dot_general(p.astype(BF16), dom, _TN, preferred_element_type=F32)
                dcq_ref[hh, rows, :] = jnp.sum(ds, axis=-1, keepdims=True)
                dck_ref[hh, :, 0:kh] -= jnp.sum(ds, axis=0, keepdims=True)
            dx_ref[rows, 0:128] = jnp.where(_lane_lo(), dqs[0], dqs[1]).astype(BF16)
        dx_ref[:, 128:256] = dk_acc[...].astype(BF16)
        dx_ref[:, 256:384] = dv_acc[...].astype(BF16)

    pair, half, colv, rowv = _fox_specs(l)
    return pl.pallas_call(
        body, name=name, grid=(b, 4), in_specs=[pair, colv, rowv, half, colv, half],
        out_specs=[pair, colv, rowv],
        out_shape=[jax.ShapeDtypeStruct(qkv.shape, BF16), jax.ShapeDtypeStruct((b, B_HEADS, l, 1), F32),
                   jax.ShapeDtypeStruct((b, B_HEADS, 1, l), F32)],
        scratch_shapes=[pltpu.VMEM((l, 128), F32), pltpu.VMEM((l, 128), F32)],
        compiler_params=_params(("parallel", "parallel")),
    )(qkv, c_col, c_row, o, lse, do)


_FLIPS = ((0, 0, 1), (0, 1, 0), (0, 1, 1), (1, 0, 0), (1, 0, 1), (1, 1, 0), (1, 1, 1))


def _exchange(name, gather, scatter):
    ng, ns = len(gather), len(scatter)
    na = ng + ns
    npeer = len(_FLIPS)

    def body(*refs):
        ins = refs[:na]
        outs = refs[na:2 * na]
        send_sems, recv_sems, loc_sems = refs[2 * na:]
        x, y, c = lax.axis_index("x"), lax.axis_index("y"), lax.axis_index("c")
        me = 4 * x + 2 * y + c
        peers = []
        for fx, fy, fc in _FLIPS:
            px = 1 - x if fx else x
            py = 1 - y if fy else y
            pc = 1 - c if fc else c
            peers.append(((px, py, pc), 4 * px + 2 * py + pc))

        def remote(a, kk):
            dev, lin = peers[kk]
            src = ins[a] if a < ng else ins[a].at[lin]
            return pltpu.make_async_remote_copy(src_ref=src, dst_ref=outs[a].at[me], send_sem=send_sems.at[a * npeer + kk],
                                                recv_sem=recv_sems.at[a * npeer + kk], device_id=dev, device_id_type=MESH_ID)

        def arrival(a, kk):
            dev, lin = peers[kk]
            src = ins[a] if a < ng else ins[a].at[lin]
            return pltpu.make_async_remote_copy(src_ref=src, dst_ref=outs[a].at[lin], send_sem=send_sems.at[a * npeer + kk],
                                                recv_sem=recv_sems.at[a * npeer + kk], device_id=dev, device_id_type=MESH_ID)

        local = []
        for a in range(na):
            src = ins[a] if a < ng else ins[a].at[me]
            cp = pltpu.make_async_copy(src, outs[a].at[me], loc_sems.at[a])
            cp.start()
            local.append(cp)
        sent = [remote(a, kk) for kk in range(npeer) for a in range(na)]
        for cp in sent:
            cp.start()
        for kk in range(npeer):
            for a in range(na):
                arrival(a, kk).wait_recv()
        for cp in sent:
            cp.wait_send()
        for cp in local:
            cp.wait()

    arrs = list(gather) + list(scatter)
    out_shape = [jax.ShapeDtypeStruct((N_DEV,) + tuple(a.shape), a.dtype) for a in gather]
    out_shape += [jax.ShapeDtypeStruct(tuple(a.shape), a.dtype) for a in scatter]
    anyspec = pl.BlockSpec(memory_space=pl.ANY)
    return pl.pallas_call(
        body, name=name, in_specs=[anyspec] * na, out_specs=[anyspec] * na, out_shape=out_shape,
        scratch_shapes=[pltpu.SemaphoreType.DMA((na * npeer,)), pltpu.SemaphoreType.DMA((na * npeer,)),
                        pltpu.SemaphoreType.DMA((na,))],
        compiler_params=pltpu.CompilerParams(has_side_effects=True),
    )(*arrs)


def _peer_table():
    x, y, c = lax.axis_index("x"), lax.axis_index("y"), lax.axis_index("c")
    me = 4 * x + 2 * y + c
    peers = []
    for fx, fy, fc in _FLIPS:
        px = 1 - x if fx else x
        py = 1 - y if fy else y
        pc = 1 - c if fc else c
        peers.append(((px, py, pc), 4 * px + 2 * py + pc))
    return me, peers


_HBM = pl.BlockSpec(memory_space=pltpu.HBM)
_SEM = pl.BlockSpec(memory_space=pltpu.SEMAPHORE)
_ANY = pl.BlockSpec(memory_space=pl.ANY)
_EFFECT = pltpu.SideEffectType.DATAFLOW_SIDE_EFFECTING


def _split_copy(srcs_are_pieces, src_refs, land_refs, send_sem, recv_sem, a, kk, me, peers, arriving):
    dev, lin = peers[kk]
    npeer = len(_FLIPS)
    src = src_refs[a] if srcs_are_pieces[a] else src_refs[a].at[lin]
    dst = land_refs[a].at[lin] if arriving else land_refs[a].at[me]
    return pltpu.make_async_remote_copy(src_ref=src, dst_ref=dst, send_sem=send_sem.at[a * npeer + kk],
                                        recv_sem=recv_sem.at[a * npeer + kk], device_id=dev, device_id_type=MESH_ID)


def _xchg_start(name, gather, scatter, after=None):
    me_out = 4 * lax.axis_index("x") + 2 * lax.axis_index("y") + lax.axis_index("c")
    srcs = list(gather) + list(scatter)
    is_piece = [True] * len(gather) + [False] * len(scatter)
    lands = []
    for a, piece in zip(srcs, is_piece):
        own = a[None] if piece else lax.dynamic_slice_in_dim(a, me_out, 1, axis=0)
        shape = ((N_DEV,) + tuple(a.shape)) if piece else tuple(a.shape)
        start = (me_out,) + (0,) * (len(shape) - 1)
        lands.append(lax.dynamic_update_slice(jnp.zeros(shape, a.dtype), own, start))
    n = len(srcs)
    nsem = n * len(_FLIPS)
    has_after = after is not None

    def body(*refs):
        src_refs = refs[:n]
        land_refs = refs[n:2 * n]
        outs = refs[2 * n + (1 if has_after else 0):]
        send_sem, recv_sem = outs[0], outs[1]
        token = outs[-1]
        me, peers = _peer_table()
        for kk in range(len(_FLIPS)):
            for a in range(n):
                _split_copy(is_piece, src_refs, land_refs, send_sem, recv_sem, a, kk, me, peers, False).start()
        token[...] = jnp.zeros_like(token)

    out_shape = ([pltpu.SemaphoreType.DMA((nsem,)), pltpu.SemaphoreType.DMA((nsem,))]
                 + [pltpu.HBM(tuple(a.shape), a.dtype) for a in srcs] + [pltpu.HBM(tuple(a.shape), a.dtype) for a in lands]
                 + [jax.ShapeDtypeStruct((8, 128), F32)])
    args = [pltpu.with_memory_space_constraint(a, pltpu.HBM) for a in srcs + lands] + ([after] if has_after else [])
    res = pl.pallas_call(
        body, name=name, out_shape=out_shape,
        in_specs=[_HBM] * (2 * n) + ([_ANY] if has_after else []),
        out_specs=[_SEM, _SEM] + [_HBM] * (2 * n) + [pl.BlockSpec(memory_space=pltpu.VMEM)],
        input_output_aliases={i: 2 + i for i in range(2 * n)},
        compiler_params=pltpu.CompilerParams(has_side_effects=_EFFECT),
    )(*args)
    state = (res[0], res[1], list(res[2:2 + n]), list(res[2 + n:2 + 2 * n]), is_piece)
    return state, res[-1]


def _xchg_wait(name, state, after):
    send_sem, recv_sem, srcs, lands, is_piece = state
    n = len(srcs)

    def body(*refs):
        src_refs = refs[:n]
        land_refs = refs[n:2 * n]
        s_sem, r_sem = refs[2 * n], refs[2 * n + 1]
        me, peers = _peer_table()
        for kk in range(len(_FLIPS)):
            for a in range(n):
                cp = _split_copy(is_piece, src_refs, land_refs, s_sem, r_sem, a, kk, me, peers, True)
                cp.wait_send()
                cp.wait_recv()

    out_shape = [pltpu.HBM(tuple(a.shape), a.dtype) for a in srcs] + [pltpu.HBM(tuple(a.shape), a.dtype) for a in lands]
    res = pl.pallas_call(
        body, name=name, out_shape=out_shape,
        in_specs=[_HBM] * (2 * n) + [_SEM, _SEM, _ANY], out_specs=[_HBM] * (2 * n),
        input_output_aliases={i: i for i in range(2 * n)},
        compiler_params=pltpu.CompilerParams(has_side_effects=_EFFECT),
    )(*srcs, *lands, send_sem, recv_sem, after)
    return list(res[n:])


def _adam_math(w, g, m, v):
    m = ADAM_B1 * m + (1.0 - ADAM_B1) * g
    v = ADAM_B2 * v + (1.0 - ADAM_B2) * (g * g)
    m_hat = m / (1.0 - ADAM_B1 ** ADAM_STEP)
    v_hat = v / (1.0 - ADAM_B2 ** ADAM_STEP)
    delta = -ADAM_LR * (m_hat / (jnp.sqrt(v_hat) + ADAM_EPS) + ADAM_WD * w)
    return delta, m, v


def _adam(name, w, m, v, parts):
    r, c = w.shape
    npart, _, cp = parts.shape
    tr = _pick(r, (256, 176, 128, 64, 16, 8, 1))

    def body(w_ref, m_ref, v_ref, p_ref, g_ref, d_ref, mo_ref, vo_ref):
        g = p_ref[0].astype(F32)
        for pp in range(1, npart):
            g = g + p_ref[pp].astype(F32)
        g = g[:, 0:c]
        delta, mn, vn = _adam_math(w_ref[...], g, m_ref[...], v_ref[...])
        g_ref[...] = g
        d_ref[...] = delta
        mo_ref[...] = mn
        vo_ref[...] = vn

    blk = pl.BlockSpec((tr, c), lambda i: (i, 0))
    out = jax.ShapeDtypeStruct((r, c), F32)
    return pl.pallas_call(
        body, name=name, grid=(r // tr,),
        in_specs=[blk, blk, blk, pl.BlockSpec((npart, tr, cp), lambda i: (0, i, 0))],
        out_specs=[blk, blk, blk, blk], out_shape=[out, out, out, out], compiler_params=_params(("parallel",)),
    )(w, m, v, parts)


def _small_sum(name, packs):
    def body(p_ref, o_ref):
        tot = p_ref[0]
        for pp in range(1, N_DEV):
            tot = tot + p_ref[pp]
        o_ref[0:8, :] = tot[0:8, :]
        o_ref[8:24, :] = tot[8:24, :] + tot[24:40, :]

    return pl.pallas_call(body, name=name, out_shape=jax.ShapeDtypeStruct((24, D_MODEL), F32),
                          compiler_params=_params())(packs)


def _local_step(x, tgt, g1, gm, g2, gf, b_forget, sinks, weights, send):
    b, s, _ = x.shape
    l = s + PREFIX
    t = b * l
    w1i, meta = weights("ffn1_in", x)
    h0 = jnp.concatenate([jnp.zeros((b, N_PAD, D_MODEL), F32), jnp.broadcast_to(meta[None], (b, N_META, D_MODEL)), x],
                         axis=1).reshape(t, D_MODEL)

    n1 = _rms_fwd("rms1_fwd", h0, g1)
    gu1, a1 = _ffn_in_fwd("ffn1_in_fwd", n1, w1i)
    (w1o,) = weights("ffn1_out", a1)
    h1 = _mm_nn("ffn1_out_fwd", a1, w1o, alpha=0.5, res=h0)
    wi, wa, wb, wo = weights("mix", h1)
    um = _rms_fwd("rmsm_fwd", h1, gm)
    qkv = _mm_nn("proj_qkv_fwd", um, wi, out_dtype=BF16, tn_c=(768,), cols=(0, QKV_W))
    gates = _mm_nn("proj_gates_fwd", um, wi, tn_c=(256,), cols=(P_GATES, 2 * D_MODEL))
    f2 = _mm_nn("proj_f_fwd", um, wi, tn_c=(128,), cols=(P_F, 128))
    qkv3 = qkv.reshape(b, l, QKV_W)
    f3 = f2.reshape(b, l, 128)
    bf_row = jnp.pad(b_forget, ((0, 0), (0, 128 - B_HEADS)))
    cfull = _fgate_fwd("fgate_fwd", f3, bf_row)
    c8 = cfull[:, :, 0:B_HEADS].transpose(0, 2, 1)
    c_col = c8[..., None]
    c_row = c8[:, :, None, :]
    head_of_row = jnp.arange(STACK) // BLOCK
    slopes = jnp.exp2(-8.0 * (head_of_row + 1).astype(F32) / A_HEADS).reshape(STACK, 1)
    sink_rows = jnp.repeat(sinks.reshape(A_HEADS), BLOCK).reshape(STACK, 1)
    oa3, lse_a = _swa_fwd("swa_fwd", qkv3, slopes, sink_rows)
    ob3, lse_b = _fox_fwd("fox_fwd", qkv3, c_col, c_row)
    oa = oa3.reshape(t, A_WIDTH)
    ob = ob3.reshape(t, B_WIDTH)
    ya = _mm_nn("branch_a_fwd", oa, wa)
    yb = _mm_nn("branch_b_fwd", ob, wb)
    mixed = _gate_fwd("gate_fwd", gates, ya, yb)
    h2 = _mm_nn("mix_out_fwd", mixed, wo, res=h1)
    w2i, w2o = weights("ffn2", h2)
    n2 = _rms_fwd("rms2_fwd", h2, g2)
    gu2, a2 = _ffn_in_fwd("ffn2_in_fwd", n2, w2i)
    h3 = _mm_nn("ffn2_out_fwd", a2, w2o, alpha=0.5, res=h2)

    dh3_3, loss_blk, dgf = _loss_head("loss_head", h3.reshape(b, l, D_MODEL), gf, tgt)
    dh3 = dh3_3.reshape(t, D_MODEL)

    def ffn_bwd(tag, dh, h_in, g_norm, n_in, gu, a, w_in_blk, w_out):
        dw_out = _mm_tn(tag + "_out_bwd_w", a, dh, alpha=0.5)
        dgu = _ffn_out_bwd_x(tag + "_out_bwd_x", dh, w_out, gu, dep=send(tag + "_out", (dw_out,)))
        dw_in = _ffn_in_bwd_w(tag + "_in_bwd_w", n_in, dgu)
        dn = _ffn_in_bwd_x(tag + "_in_bwd_x", dgu, w_in_blk, dep=send(tag + "_in", (dw_in,)))
        dh_in, dg = _rms_bwd("rms" + tag[-1] + "_bwd", h_in, g_norm, dn, dh)
        return dh_in, dg

    dh2, dg2 = ffn_bwd("ffn2", dh3, h2, g2, n2, gu2, a2, w2i, w2o)

    dmix = _mm_nt("mix_out_bwd_x", dh2, wo, tn_c=(512,))
    dwo = _mm_tn("mix_out_bwd_w", mixed, dh2)
    dya, dyb, dgates = _gate_bwd("gate_bwd", gates, ya, yb, dmix)
    doa = _mm_nt("branch_a_bwd_x", dya, wa, out_dtype=BF16, tn_c=(512,))
    dob = _mm_nt("branch_b_bwd_x", dyb, wb, out_dtype=BF16, tn_c=(512,))
    dwa = _mm_tn("branch_a_bwd_w", oa, dya, tm_c=(512,))
    dwb = _mm_tn("branch_b_bwd_w", ob, dyb, tm_c=(512,))
    dqkv3, dcq, dck = _fox_bwd("fox_bwd", qkv3, c_col, c_row, ob3, lse_b, dob.reshape(b, l, B_WIDTH))
    dqkv3, dsink = _swa_bwd("swa_bwd", qkv3, oa3, lse_a, doa.reshape(b, l, A_WIDTH), slopes, sink_rows, dqkv3)
    dqkv = dqkv3.reshape(t, QKV_W)
    dc = (dcq[..., 0] + dck[:, :, 0, :]).transpose(0, 2, 1)
    dc = jnp.pad(dc, ((0, 0), (0, 0), (0, 128 - B_HEADS)))
    df3, dbf = _fgate_bwd("fgate_bwd", f3, bf_row, dc)
    df = df3.reshape(t, 128)
    dwi_qkv = _mm_tn("proj_qkv_bwd_w", um, dqkv, tm_c=(512,), tn_c=(768,))
    dwi_g = _mm_tn("proj_gates_bwd_w", um, dgates, tm_c=(512,), tn_c=(512,))
    dwi_f = _mm_tn("proj_f_bwd_w", um, df, tm_c=(512,), tn_c=(128,))
    token = send("mix", (dwi_qkv, dwi_g, dwi_f, dwa, dwb, dwo))
    dum = _mm_nt("proj_gates_bwd_x", dgates, wi, tn_c=(512,), tk_c=(256,), kcols=(P_GATES, 2 * D_MODEL), dep=token)
    dum = _mm_nt("proj_qkv_bwd_x", dqkv, wi, tn_c=(512,), tk_c=(768,), kcols=(0, QKV_W), res=dum)
    dum = _mm_nt("proj_f_bwd_x", df, wi, tn_c=(512,), tk_c=(128,), kcols=(P_F, 128), res=dum)
    dh1, dgm = _rms_bwd("rmsm_bwd", h1, gm, dum, dh2)

    dh0, dg1 = ffn_bwd("ffn1", dh1, h0, g1, n1, gu1, a1, w1i, w1o)
    dh0_3 = dh0.reshape(b, l, D_MODEL)
    grad_x = dh0_3[:, PREFIX:, :]
    dmeta = dh0_3[:, N_PAD:PREFIX, :].reshape(b * N_META, D_MODEL)

    misc = jnp.concatenate([dbf[:, 0:B_HEADS], dsink[:, 0].reshape(1, A_HEADS), loss_blk[0:1, 0:1]], axis=1)
    misc = jnp.pad(misc, ((0, 0), (0, D_MODEL - misc.shape[1])))
    row = lax.broadcasted_iota(jnp.int32, (8, D_MODEL), 0)
    vec = jnp.zeros((8, D_MODEL), F32)
    for i, piece in enumerate((dg1, dgm, dg2, dgf, misc)):
        vec = jnp.where(row == i, piece, vec)
    small = jnp.concatenate([vec, dmeta], axis=0)
    return grad_x, small


def _pad_to(a, rows, cols):
    return jnp.pad(a, ((0, rows - a.shape[0]), (0, cols - a.shape[1])))


def _ffn_out_from_gathered(g):
    w = g.reshape(4, FF_SHARD, D_MODEL)
    return jnp.pad(w, ((0, 0), (0, FF_SHARD_P - FF_SHARD), (0, 0))).reshape(D_FF_P, D_MODEL)


def _ffn_out_to_scatter(dw):
    return dw.reshape(4, FF_SHARD_P, D_MODEL)[:, 0:FF_SHARD, :].reshape(N_DEV, FFO_SHARD, D_MODEL)


def _proj_from_gathered(g):
    w = g[:, :, 0:WIN_SHARD].transpose(1, 0, 2).reshape(D_MODEL, W_IN_COLS)
    qa, ka, va = w[:, 0:512], w[:, 512:640], w[:, 640:768]
    qb, kb, vb = w[:, 768:1280], w[:, 1280:1792], w[:, 1792:2304]
    f, gates = w[:, 2304:2312], w[:, 2312:]
    parts = []
    for hp in range(4):
        cols = slice(128 * hp, 128 * (hp + 1))
        parts += [qb[:, cols], kb[:, cols], vb[:, cols]]
    parts += [qa[:, HEAD_DIM * h:HEAD_DIM * (h + 1)] for h in A_HEAD_ORDER]
    parts += [ka, va, gates, jnp.pad(f, ((0, 0), (0, 128 - B_HEADS)))]
    return jnp.concatenate(parts, axis=1)


def _proj_to_scatter(dqkv_w, dg_w, df_w):
    bseg, aseg = dqkv_w[:, 0:B_SEG], dqkv_w[:, B_SEG:QKV_W]
    qb = [bseg[:, PAIR_W * hp:PAIR_W * hp + 128] for hp in range(4)]
    kb = [bseg[:, PAIR_W * hp + 128:PAIR_W * hp + 256] for hp in range(4)]
    vb = [bseg[:, PAIR_W * hp + 256:PAIR_W * hp + 384] for hp in range(4)]
    qa = [aseg[:, HEAD_DIM * A_HEAD_ORDER.index(h):HEAD_DIM * (A_HEAD_ORDER.index(h) + 1)] for h in range(A_HEADS)]
    w = jnp.concatenate(qa + [aseg[:, A_WIDTH:A_SEG]] + qb + kb + vb + [df_w[:, 0:B_HEADS], dg_w], axis=1)
    w = w.reshape(D_MODEL, N_DEV, WIN_SHARD).transpose(1, 0, 2)
    return jnp.pad(w, ((0, 0), (0, 0), (0, WIN_SHARD_P - WIN_SHARD)))


def _a_rows_from_natural(w):
    return jnp.concatenate([w[HEAD_DIM * h:HEAD_DIM * (h + 1)] for h in A_HEAD_ORDER], axis=0)


def _a_rows_to_natural(w):
    return jnp.concatenate([w[HEAD_DIM * A_HEAD_ORDER.index(h):HEAD_DIM * (A_HEAD_ORDER.index(h) + 1)]
                            for h in range(A_HEADS)], axis=0)


def kernel(x, meta_tokens, ffn1_norm, ffn1_w_in, ffn1_w_out, mix_norm, w_in, b_forget, attn_sinks, w_branch_a, w_branch_b, w_out, ffn2_norm, ffn2_w_in, ffn2_w_out, final_norm, loss_target, m_meta_tokens, m_ffn1_norm, m_ffn1_w_in, m_ffn1_w_out, m_mix_norm, m_w_in, m_b_forget, m_attn_sinks, m_w_branch_a, m_w_branch_b, m_w_out, m_ffn2_norm, m_ffn2_w_in, m_ffn2_w_out, m_final_norm, v_meta_tokens, v_ffn1_norm, v_ffn1_w_in, v_ffn1_w_out, v_mix_norm, v_w_in, v_b_forget, v_attn_sinks, v_w_branch_a, v_w_branch_b, v_w_out, v_ffn2_norm, v_ffn2_w_in, v_ffn2_w_out, v_final_norm):
    me = 4 * lax.axis_index("x") + 2 * lax.axis_index("y") + lax.axis_index("c")

    shards = (
        _pad_to(ffn1_w_in[0].astype(BF16), D_MODEL, FF_SHARD_P),
        ffn1_w_out[0].astype(BF16),
        _pad_to(w_in[0].astype(BF16), D_MODEL, WIN_SHARD_P),
        w_branch_a[0].astype(BF16), w_branch_b[0].astype(BF16), w_out[0].astype(BF16),
        _pad_to(ffn2_w_in[0].astype(BF16), D_MODEL, FF_SHARD_P),
        ffn2_w_out[0].astype(BF16),
        meta_tokens,
    )
    s1i, s1o, swi, swa, swb, swo, s2i, s2o, smeta = shards
    gather_state = {}
    gather_state["ffn1_in"], tok = _xchg_start("gather_ffn1_in_start", (s1i, smeta), ())
    gather_state["ffn1_out"], tok = _xchg_start("gather_ffn1_out_start", (s1o,), (), after=tok)
    gather_state["mix"], tok = _xchg_start("gather_mix_start", (swi, swa, swb, swo), (), after=tok)
    gather_state["ffn2"], tok = _xchg_start("gather_ffn2_start", (s2i, s2o), (), after=tok)
    started = {"tok": tok}

    def weights(group, after):
        if group == "ffn1_in":
            after = started["tok"]
        got = _xchg_wait("gather_" + group + "_wait", gather_state[group], after)
        if group == "mix":
            gwi, gwa, gwb, gwo = got
            return (_proj_from_gathered(gwi), _a_rows_from_natural(gwa.transpose(1, 0, 2).reshape(A_WIDTH, D_MODEL)),
                    gwb.transpose(1, 0, 2).reshape(B_WIDTH, D_MODEL), gwo.reshape(D_MODEL, D_MODEL))
        if group == "ffn1_in":
            return got[0].reshape(2, 4, D_MODEL, FF_SHARD_P), got[1].transpose(1, 0, 2).reshape(N_META, D_MODEL)
        if group == "ffn1_out":
            return (_ffn_out_from_gathered(got[0]),)
        return got[0].reshape(2, 4, D_MODEL, FF_SHARD_P), _ffn_out_from_gathered(got[1])

    scatter_state = {}

    def send(group, grads):
        if group == "mix":
            dwi_qkv, dwi_g, dwi_f, dwa, dwb, dwo = grads
            dwa = _a_rows_to_natural(dwa)
            blocks = (_proj_to_scatter(dwi_qkv, dwi_g, dwi_f), dwa.reshape(A_WIDTH, N_DEV, 128).transpose(1, 0, 2),
                      dwb.reshape(B_WIDTH, N_DEV, 128).transpose(1, 0, 2), dwo.reshape(N_DEV, 128, D_MODEL))
        elif group.endswith("_in"):
            blocks = (grads[0].reshape(N_DEV, D_MODEL, FF_SHARD_P),)
        else:
            blocks = (_ffn_out_to_scatter(grads[0]),)
        scatter_state[group], token = _xchg_start("scatter_" + group + "_start", (), blocks)
        return token

    gf = final_norm.reshape(1, D_MODEL)
    grad_x, small = _local_step(x, loss_target, ffn1_norm, mix_norm, ffn2_norm, gf, b_forget, attn_sinks, weights, send)

    (packs,) = _exchange("gather_small", (small,), ())
    recv = {}
    after = packs
    for group in ("ffn2_out", "ffn2_in", "mix", "ffn1_out", "ffn1_in"):
        recv[group] = _xchg_wait("scatter_" + group + "_wait", scatter_state[group], after)
        after = recv[group][0]
    (r2o,), (r2i,), (rwi, rwa, rwb, rwo), (r1o,), (r1i,) = (recv[g] for g in ("ffn2_out", "ffn2_in", "mix", "ffn1_out", "ffn1_in"))

    out = {}
    for nm, w, m, v, parts in (
        ("ffn1_w_in", ffn1_w_in, m_ffn1_w_in, v_ffn1_w_in, r1i), ("ffn1_w_out", ffn1_w_out, m_ffn1_w_out, v_ffn1_w_out, r1o),
        ("w_in", w_in, m_w_in, v_w_in, rwi), ("w_branch_a", w_branch_a, m_w_branch_a, v_w_branch_a, rwa),
        ("w_branch_b", w_branch_b, m_w_branch_b, v_w_branch_b, rwb), ("w_out", w_out, m_w_out, v_w_out, rwo),
        ("ffn2_w_in", ffn2_w_in, m_ffn2_w_in, v_ffn2_w_in, r2i), ("ffn2_w_out", ffn2_w_out, m_ffn2_w_out, v_ffn2_w_out, r2o),
    ):
        res4 = _adam("adam_" + nm, w[0], m[0], v[0], parts)
        out[nm] = tuple(r[None] for r in res4)

    tot = _small_sum("small_sum", packs)
    loss = tot[4, 2 * B_HEADS]
    g_meta = lax.dynamic_slice(tot[8:24, :], (0, me * 128), (N_META, 128))
    out["meta_tokens"] = tuple(_adam("adam_meta_tokens", meta_tokens, m_meta_tokens, v_meta_tokens, g_meta[None]))

    def pack_small(n1, nm, n2, nf, bfv, skv):
        misc = jnp.pad(jnp.concatenate([bfv, skv], axis=1), ((0, 0), (0, D_MODEL - 2 * B_HEADS)))
        row = lax.broadcasted_iota(jnp.int32, (8, D_MODEL), 0)
        vec = jnp.zeros((8, D_MODEL), F32)
        for i, piece in enumerate((n1, nm, n2, nf.reshape(1, D_MODEL), misc)):
            vec = jnp.where(row == i, piece, vec)
        return vec

    w_pack = pack_small(ffn1_norm, mix_norm, ffn2_norm, final_norm, b_forget, attn_sinks)
    m_pack = pack_small(m_ffn1_norm, m_mix_norm, m_ffn2_norm, m_final_norm, m_b_forget, m_attn_sinks)
    v_pack = pack_small(v_ffn1_norm, v_mix_norm, v_ffn2_norm, v_final_norm, v_b_forget, v_attn_sinks)
    small4 = _adam("adam_small", w_pack, m_pack, v_pack, tot[0:8][None])
    for i, nm in enumerate(("ffn1_norm", "mix_norm", "ffn2_norm")):
        out[nm] = tuple(r[i:i + 1] for r in small4)
    out["final_norm"] = tuple(r[3] for r in small4)
    out["b_forget"] = tuple(r[4:5, 0:B_HEADS] for r in small4)
    out["attn_sinks"] = tuple(r[4:5, B_HEADS:2 * B_HEADS] for r in small4)

    names = ("meta_tokens", "ffn1_norm", "ffn1_w_in", "ffn1_w_out", "mix_norm", "w_in", "b_forget", "attn_sinks",
             "w_branch_a", "w_branch_b", "w_out", "ffn2_norm", "ffn2_w_in", "ffn2_w_out", "final_norm")
    return (loss, grad_x) + tuple(out[nm][kind] for kind in range(4) for nm in names)
```

```python
import jax
import jax.numpy as jnp
from jax import lax
from jax.experimental import pallas as pl
from jax.experimental.pallas import tpu as pltpu

F32 = jnp.float32
BF16 = jnp.bfloat16

D_MODEL = 1024
N_META = 16
BLOCK = 128
PREFIX = 128
N_PAD = PREFIX - N_META
HEAD_DIM = 64
A_HEADS = 8
A_KV_HEADS = 2
A_GROUP = 4
B_HEADS = 8
A_WIDTH = 512
A_KV_WIDTH = 128
B_WIDTH = 512
D_FF = 2816
N_DEV = 8
FF_SHARD = 2 * D_FF // N_DEV
FF_SHARD_P = 768
FFO_SHARD = D_FF // N_DEV
FFO_SHARD_P = FF_SHARD_P // 2
D_FF_P = 4 * FF_SHARD_P
W_IN_COLS = 4360
WIN_SHARD = W_IN_COLS // N_DEV
WIN_SHARD_P = 640
PAIR_W = 3 * 128
B_SEG = 4 * PAIR_W
A_SEG = A_WIDTH + 2 * A_KV_WIDTH
QKV_W = B_SEG + A_SEG
P_GATES = 2 * (2 * D_MODEL)
P_F = P_GATES + 2 * D_MODEL
PROJ_P = P_F + 128
A_HEAD_ORDER = (0, 4, 1, 5, 2, 6, 3, 7)
EPS = 1e-6
NEG = -1e30
SCALE = HEAD_DIM ** -0.5
ADAM_LR = 0.001
ADAM_B1 = 0.9
ADAM_B2 = 0.999
ADAM_EPS = 1e-08
ADAM_WD = 0.01
ADAM_STEP = 10
VMEM_LIMIT = 48 * 1024 * 1024
MESH_ID = pl.DeviceIdType.MESH
SMALL_ROWS = 40

_NN = (((1,), (0,)), ((), ()))
_NT = (((1,), (1,)), ((), ()))
_TN = (((0,), (0,)), ((), ()))


def _params(sem=None):
    return pltpu.CompilerParams(dimension_semantics=sem, vmem_limit_bytes=VMEM_LIMIT)


def _pick(n, cands):
    for c in cands:
        if n % c == 0:
            return c
    raise ValueError(f"no tile for {n}")


def _bf(v):
    return v if v.dtype == BF16 else v.astype(BF16)


def _mm(name, a, b, dims, grid, a_spec, b_spec, o_spec, out_shape, out_dtype, acc_shape, k_axis=None, nk=1,
        alpha=1.0, res=None, res_spec=None, dep=None):
    has_res = res is not None
    has_dep = dep is not None

    def body(*refs):
        a_ref, b_ref = refs[0], refs[1]
        r_ref = refs[2] if has_res else None
        o_ref = refs[2 + has_res + has_dep]

        def finish(acc):
            if alpha != 1.0:
                acc = acc * alpha
            if has_res:
                acc = acc + r_ref[...]
            o_ref[...] = acc.astype(o_ref.dtype)

        part = lax.dot_general(_bf(a_ref[...]), _bf(b_ref[...]), dims, preferred_element_type=F32)
        if nk == 1:
            finish(part)
        else:
            acc_ref = refs[-1]
            k = pl.program_id(k_axis)

            @pl.when(k == 0)
            def _():
                acc_ref[...] = part

            @pl.when(k > 0)
            def _():
                acc_ref[...] += part

            @pl.when(k == nk - 1)
            def _():
                finish(acc_ref[...])

    in_specs = [a_spec, b_spec] + ([res_spec] if has_res else []) + ([pl.BlockSpec(memory_space=pl.ANY)] if has_dep else [])
    args = (a, b) + ((res,) if has_res else ()) + ((dep,) if has_dep else ())
    sem = tuple("arbitrary" if (nk > 1 and i == k_axis) else "parallel" for i in range(len(grid)))
    return pl.pallas_call(
        body, name=name, grid=grid, in_specs=in_specs, out_specs=o_spec,
        out_shape=jax.ShapeDtypeStruct(out_shape, out_dtype),
        scratch_shapes=[pltpu.VMEM(acc_shape, F32)] if nk > 1 else [],
        compiler_params=_params(sem),
    )(*args)


def _mm_nn(name, a, b, out_dtype=F32, alpha=1.0, res=None, tn_c=(512, 640, 256, 128), cols=None):
    t, k = a.shape
    c0, n = (0, b.shape[1]) if cols is None else cols
    tm = _pick(t, (1088, 768, 512, 256, 128))
    tn = _pick(n, tn_c)
    assert c0 % tn == 0
    jb = c0 // tn
    return _mm(name, a, b, _NN, (t // tm, n // tn),
               pl.BlockSpec((tm, k), lambda i, j: (i, 0)), pl.BlockSpec((k, tn), lambda i, j: (0, jb + j)),
               pl.BlockSpec((tm, tn), lambda i, j: (i, j)), (t, n), out_dtype, None,
               alpha=alpha, res=res, res_spec=pl.BlockSpec((tm, tn), lambda i, j: (i, j)))


def _mm_nt(name, a, b, out_dtype=F32, alpha=1.0, tn_c=(768, 512, 256, 128), tk_c=None, dep=None, res=None, kcols=None):
    t, k = a.shape
    n = b.shape[0]
    c0 = 0 if kcols is None else kcols[0]
    tm = _pick(t, (1088, 768, 512, 256, 128))
    tn = _pick(n, tn_c)
    tk = k if tk_c is None else _pick(k, tk_c)
    nk = k // tk
    assert c0 % tk == 0
    kb = c0 // tk
    return _mm(name, a, b, _NT, (t // tm, n // tn, nk),
               pl.BlockSpec((tm, tk), lambda i, j, kk: (i, kk)), pl.BlockSpec((tn, tk), lambda i, j, kk: (j, kb + kk)),
               pl.BlockSpec((tm, tn), lambda i, j, kk: (i, j)), (t, n), out_dtype, (tm, tn), k_axis=2, nk=nk, alpha=alpha,
               dep=dep, res=res, res_spec=pl.BlockSpec((tm, tn), lambda i, j, kk: (i, j)))


def _mm_tn(name, a, b, out_dtype=BF16, alpha=1.0, tm_c=(768, 512, 256, 128), tn_c=(512, 640, 256, 128)):
    t, m = a.shape
    n = b.shape[1]
    tm = _pick(m, tm_c)
    tn = _pick(n, tn_c)
    tk = _pick(t, (1088, 768, 512, 256, 128))
    nk = t // tk
    return _mm(name, a, b, _TN, (m // tm, n // tn, nk),
               pl.BlockSpec((tk, tm), lambda i, j, kk: (kk, i)), pl.BlockSpec((tk, tn), lambda i, j, kk: (kk, j)),
               pl.BlockSpec((tm, tn), lambda i, j, kk: (i, j)), (m, n), out_dtype, (tm, tn), k_axis=2, nk=nk, alpha=alpha)


def _silu_grads(g, u, d):
    sg = jax.nn.sigmoid(g)
    return d * u * (sg * (1.0 + g * (1.0 - sg))), d * (g * sg)


def _ffn_in_fwd(name, n, wblk):
    t = n.shape[0]
    tm = _pick(t, (1088, 768, 512, 256, 128))

    def body(n_ref, w_ref, gu_ref, a_ref):
        nv = n_ref[...]
        g = lax.dot_general(nv, w_ref[0], _NN, preferred_element_type=F32)
        u = lax.dot_general(nv, w_ref[1], _NN, preferred_element_type=F32)
        gu_ref[0] = g.astype(BF16)
        gu_ref[1] = u.astype(BF16)
        a_ref[...] = ((g * jax.nn.sigmoid(g)) * u).astype(BF16)

    return pl.pallas_call(
        body, name=name, grid=(t // tm, 4),
        in_specs=[pl.BlockSpec((tm, D_MODEL), lambda i, j: (i, 0)),
                  pl.BlockSpec((2, None, D_MODEL, FF_SHARD_P), lambda i, j: (0, j, 0, 0))],
        out_specs=[pl.BlockSpec((2, tm, FF_SHARD_P), lambda i, j: (0, i, j)),
                   pl.BlockSpec((tm, FF_SHARD_P), lambda i, j: (i, j))],
        out_shape=[jax.ShapeDtypeStruct((2, t, D_FF_P), BF16), jax.ShapeDtypeStruct((t, D_FF_P), BF16)],
        compiler_params=_params(("parallel", "parallel")),
    )(n, wblk)


def _ffn_out_bwd_x(name, dh, w_out, gu, dep=None):
    t = dh.shape[0]
    tm = _pick(t, (1088, 768, 512, 256, 128))
    has_dep = dep is not None

    def body(dh_ref, w_ref, gu_ref, *rest):
        o_ref = rest[-1]
        da = lax.dot_general(_bf(dh_ref[...]), w_ref[...], _NT, preferred_element_type=F32) * 0.5
        dg, du = _silu_grads(gu_ref[0].astype(F32), gu_ref[1].astype(F32), da)
        o_ref[0] = dg.astype(BF16)
        o_ref[1] = du.astype(BF16)

    gu_spec = pl.BlockSpec((2, tm, FF_SHARD_P), lambda i, j: (0, i, j))
    return pl.pallas_call(
        body, name=name, grid=(t // tm, 4),
        in_specs=[pl.BlockSpec((tm, D_MODEL), lambda i, j: (i, 0)), pl.BlockSpec((FF_SHARD_P, D_MODEL), lambda i, j: (j, 0)),
                  gu_spec] + ([pl.BlockSpec(memory_space=pl.ANY)] if has_dep else []),
        out_specs=gu_spec, out_shape=jax.ShapeDtypeStruct((2, t, D_FF_P), BF16),
        compiler_params=_params(("parallel", "parallel")),
    )(*((dh, w_out, gu) + ((dep,) if has_dep else ())))


def _ffn_in_bwd_x(name, dgu, wblk, dep=None):
    t = dgu.shape[1]
    tm = _pick(t, (272, 256, 128))
    has_dep = dep is not None

    def body(d_ref, w_ref, *rest):
        o_ref = rest[-1]
        acc = None
        for s in range(2):
            for j in range(4):
                part = lax.dot_general(d_ref[s, :, FF_SHARD_P * j:FF_SHARD_P * (j + 1)], w_ref[s, j], _NT,
                                       preferred_element_type=F32)
                acc = part if acc is None else acc + part
        o_ref[...] = acc

    return pl.pallas_call(
        body, name=name, grid=(t // tm,),
        in_specs=[pl.BlockSpec((2, tm, D_FF_P), lambda i: (0, i, 0)),
                  pl.BlockSpec((2, 4, D_MODEL, FF_SHARD_P), lambda i: (0, 0, 0, 0))]
        + ([pl.BlockSpec(memory_space=pl.ANY)] if has_dep else []),
        out_specs=pl.BlockSpec((tm, D_MODEL), lambda i: (i, 0)),
        out_shape=jax.ShapeDtypeStruct((t, D_MODEL), F32), compiler_params=_params(("parallel",)),
    )(*((dgu, wblk) + ((dep,) if has_dep else ())))


def _ffn_in_bwd_w(name, n, dgu):
    t = n.shape[0]
    tk = _pick(t, (1088, 768, 512, 256, 128))
    nk = t // tk
    return _mm(name, n, dgu, _TN, (2, 4, nk),
               pl.BlockSpec((tk, D_MODEL), lambda s, j, kk: (kk, 0)),
               pl.BlockSpec((None, tk, FF_SHARD_P), lambda s, j, kk: (s, kk, j)),
               pl.BlockSpec((None, None, D_MODEL, FF_SHARD_P), lambda s, j, kk: (s, j, 0, 0)),
               (2, 4, D_MODEL, FF_SHARD_P), BF16, (D_MODEL, FF_SHARD_P), k_axis=2, nk=nk)


def _rms_fwd(name, h, g):
    t = h.shape[0]
    tm = _pick(t, (544, 384, 256, 128))

    def body(h_ref, g_ref, o_ref):
        hv = h_ref[...]
        r = lax.rsqrt(jnp.mean(hv * hv, axis=-1, keepdims=True) + EPS)
        o_ref[...] = ((hv * r) * g_ref[...]).astype(BF16)

    return pl.pallas_call(
        body, name=name, grid=(t // tm,),
        in_specs=[pl.BlockSpec((tm, D_MODEL), lambda i: (i, 0)), pl.BlockSpec((1, D_MODEL), lambda i: (0, 0))],
        out_specs=pl.BlockSpec((tm, D_MODEL), lambda i: (i, 0)),
        out_shape=jax.ShapeDtypeStruct((t, D_MODEL), BF16), compiler_params=_params(("parallel",)),
    )(h, g)


def _rms_bwd(name, h, g, dn, dres):
    t = h.shape[0]
    tm = _pick(t, (544, 384, 256, 128))

    def body(h_ref, g_ref, dn_ref, dres_ref, dh_ref, dg_ref):
        i = pl.program_id(0)
        hv = h_ref[...]
        dnv = dn_ref[...]
        r = lax.rsqrt(jnp.mean(hv * hv, axis=-1, keepdims=True) + EPS)
        tv = dnv * g_ref[...]
        dot = jnp.mean(tv * hv, axis=-1, keepdims=True)
        dh_ref[...] = dres_ref[...] + (r * tv - hv * (r * r * r * dot))
        part = jnp.sum(dnv * (hv * r), axis=0, keepdims=True)

        @pl.when(i == 0)
        def _():
            dg_ref[...] = part

        @pl.when(i > 0)
        def _():
            dg_ref[...] += part

    row = pl.BlockSpec((tm, D_MODEL), lambda i: (i, 0))
    vec = pl.BlockSpec((1, D_MODEL), lambda i: (0, 0))
    return pl.pallas_call(
        body, name=name, grid=(t // tm,), in_specs=[row, vec, row, row], out_specs=[row, vec],
        out_shape=[jax.ShapeDtypeStruct((t, D_MODEL), F32), jax.ShapeDtypeStruct((1, D_MODEL), F32)],
        compiler_params=_params(("arbitrary",)),
    )(h, g, dn, dres)


def _gate_fwd(name, gates, ya, yb):
    t = gates.shape[0]
    tm = _pick(t, (272, 256, 128))

    def body(g_ref, ya_ref, yb_ref, o_ref):
        sa = jax.nn.sigmoid(g_ref[:, 0:D_MODEL])
        sb = jax.nn.sigmoid(g_ref[:, D_MODEL:2 * D_MODEL])
        o_ref[...] = (sa * ya_ref[...] + sb * yb_ref[...]).astype(BF16)

    blk = pl.BlockSpec((tm, D_MODEL), lambda i: (i, 0))
    wide = pl.BlockSpec((tm, 2 * D_MODEL), lambda i: (i, 0))
    return pl.pallas_call(
        body, name=name, grid=(t // tm,), in_specs=[wide, blk, blk], out_specs=blk,
        out_shape=jax.ShapeDtypeStruct((t, D_MODEL), BF16), compiler_params=_params(("parallel",)),
    )(gates, ya, yb)


def _gate_bwd(name, gates, ya, yb, dmix):
    t = gates.shape[0]
    tm = _pick(t, (272, 256, 128))

    def body(g_ref, ya_ref, yb_ref, dm_ref, dya_ref, dyb_ref, dg_ref):
        dm = dm_ref[...]
        sa = jax.nn.sigmoid(g_ref[:, 0:D_MODEL])
        sb = jax.nn.sigmoid(g_ref[:, D_MODEL:2 * D_MODEL])
        dya_ref[...] = (dm * sa).astype(BF16)
        dyb_ref[...] = (dm * sb).astype(BF16)
        dg_ref[:, 0:D_MODEL] = (dm * ya_ref[...] * (sa * (1.0 - sa))).astype(BF16)
        dg_ref[:, D_MODEL:2 * D_MODEL] = (dm * yb_ref[...] * (sb * (1.0 - sb))).astype(BF16)

    blk = pl.BlockSpec((tm, D_MODEL), lambda i: (i, 0))
    wide = pl.BlockSpec((tm, 2 * D_MODEL), lambda i: (i, 0))
    out = jax.ShapeDtypeStruct((t, D_MODEL), BF16)
    return pl.pallas_call(
        body, name=name, grid=(t // tm,), in_specs=[wide, blk, blk, blk], out_specs=[blk, blk, wide],
        out_shape=[out, out, jax.ShapeDtypeStruct((t, 2 * D_MODEL), BF16)], compiler_params=_params(("parallel",)),
    )(gates, ya, yb, dmix)


def _loss_head(name, h3, gf, tgt):
    b, l, _ = h3.shape
    nb = l // BLOCK

    def body(h_ref, g_ref, t_ref, dh_ref, loss_ref, dg_ref):
        first = (pl.program_id(0) == 0) & (pl.program_id(1) == 0)
        real = (pl.program_id(1) > 0).astype(F32)
        hv = h_ref[...]
        g = g_ref[...]
        r = lax.rsqrt(jnp.mean(hv * hv, axis=-1, keepdims=True) + EPS)
        xn = hv * r
        err = (xn * g - t_ref[...]) * real
        lpart = 0.5 * jnp.sum(jnp.mean(err * err, axis=-1, keepdims=True), axis=0, keepdims=True)
        dy = err * (1.0 / D_MODEL)
        tv = dy * g
        dot = jnp.mean(tv * hv, axis=-1, keepdims=True)
        dh_ref[...] = r * tv - hv * (r * r * r * dot)
        gpart = jnp.sum(dy * xn, axis=0, keepdims=True)

        @pl.when(first)
        def _():
            loss_ref[...] = jnp.zeros_like(loss_ref)
            dg_ref[...] = jnp.zeros_like(dg_ref)

        loss_ref[...] += jnp.broadcast_to(lpart, loss_ref.shape)
        dg_ref[...] += gpart

    return pl.pallas_call(
        body, name=name, grid=(b, nb),
        in_specs=[pl.BlockSpec((None, BLOCK, D_MODEL), lambda bi, n: (bi, n, 0)),
                  pl.BlockSpec((1, D_MODEL), lambda bi, n: (0, 0)),
                  pl.BlockSpec((None, BLOCK, D_MODEL), lambda bi, n: (bi, jnp.maximum(n - 1, 0), 0))],
        out_specs=[pl.BlockSpec((None, BLOCK, D_MODEL), lambda bi, n: (bi, n, 0)),
                   pl.BlockSpec((8, 128), lambda bi, n: (0, 0)),
                   pl.BlockSpec((1, D_MODEL), lambda bi, n: (0, 0))],
        out_shape=[jax.ShapeDtypeStruct(h3.shape, F32), jax.ShapeDtypeStruct((8, 128), F32),
                   jax.ShapeDtypeStruct((1, D_MODEL), F32)],
        compiler_params=_params(("arbitrary", "arbitrary")),
    )(h3, gf, tgt)


def _fgate_fwd(name, f3, bf_row):
    b, l, _ = f3.shape
    nb = l // BLOCK

    def body(f_ref, b_ref, cc_ref, cr_ref):
        r_i = lax.broadcasted_iota(jnp.int32, (BLOCK, BLOCK), 0)
        c_i = lax.broadcasted_iota(jnp.int32, (BLOCK, BLOCK), 1)
        tri = (r_i >= c_i).astype(F32)
        carry = jnp.zeros((1, 128), F32)
        for blk in range(nb):
            rows = slice(blk * BLOCK, (blk + 1) * BLOCK)
            z = f_ref[rows, :] + b_ref[...]
            lf = jnp.minimum(z, 0.0) - jnp.log(1.0 + jnp.exp(-jnp.abs(z)))
            cb = jnp.dot(tri, lf, preferred_element_type=F32, precision=lax.Precision.HIGHEST) + carry
            carry = cb[BLOCK - 1:BLOCK, :]
            cbt = cb.T
            for hh in range(B_HEADS):
                cc_ref[hh, rows, :] = jnp.sum(jnp.where(c_i == hh, cb, 0.0), axis=1, keepdims=True)
                cr_ref[hh, :, rows] = cbt[hh:hh + 1, :]

    return pl.pallas_call(
        body, name=name, grid=(b,),
        in_specs=[pl.BlockSpec((None, l, 128), lambda bi: (bi, 0, 0)),
                  pl.BlockSpec((1, 128), lambda bi: (0, 0))],
        out_specs=[pl.BlockSpec((None, B_HEADS, l, 1), lambda bi: (bi, 0, 0, 0)),
                   pl.BlockSpec((None, B_HEADS, 1, l), lambda bi: (bi, 0, 0, 0))],
        out_shape=[jax.ShapeDtypeStruct((b, B_HEADS, l, 1), F32), jax.ShapeDtypeStruct((b, B_HEADS, 1, l), F32)],
        compiler_params=_params(("parallel",)),
    )(f3, bf_row)


def _fgate_bwd(name, f3, bf_row, dcq, dck):
    b, l, _ = f3.shape
    nb = l // BLOCK

    def body(f_ref, b_ref, dcq_ref, dck_ref, df_ref, db_ref):
        r_i = lax.broadcasted_iota(jnp.int32, (BLOCK, BLOCK), 0)
        c_i = lax.broadcasted_iota(jnp.int32, (BLOCK, BLOCK), 1)
        tri = (r_i <= c_i).astype(F32)
        carry = jnp.zeros((1, 128), F32)
        total = jnp.zeros((1, 128), F32)
        for blk in range(nb - 1, -1, -1):
            rows = slice(blk * BLOCK, (blk + 1) * BLOCK)
            krows = jnp.concatenate([dck_ref[hh, :, rows] for hh in range(B_HEADS)]
                                    + [jnp.zeros((BLOCK - B_HEADS, BLOCK), F32)], axis=0)
            dcb = krows.T
            for hh in range(B_HEADS):
                dcb = dcb + jnp.where(c_i == hh, dcq_ref[hh, rows, :], 0.0)
            rc = jnp.dot(tri, dcb, preferred_element_type=F32, precision=lax.Precision.HIGHEST) + carry
            carry = rc[0:1, :]
            z = f_ref[rows, :] + b_ref[...]
            df = rc * (1.0 / (1.0 + jnp.exp(z)))
            df_ref[rows, :] = df.astype(BF16)
            total = total + jnp.sum(df, axis=0, keepdims=True)

        @pl.when(pl.program_id(0) == 0)
        def _():
            db_ref[...] = total

        @pl.when(pl.program_id(0) > 0)
        def _():
            db_ref[...] += total

    return pl.pallas_call(
        body, name=name, grid=(b,),
        in_specs=[pl.BlockSpec((None, l, 128), lambda bi: (bi, 0, 0)),
                  pl.BlockSpec((1, 128), lambda bi: (0, 0)),
                  pl.BlockSpec((None, B_HEADS, l, 1), lambda bi: (bi, 0, 0, 0)),
                  pl.BlockSpec((None, B_HEADS, 1, l), lambda bi: (bi, 0, 0, 0))],
        out_specs=[pl.BlockSpec((None, l, 128), lambda bi: (bi, 0, 0)), pl.BlockSpec((1, 128), lambda bi: (0, 0))],
        out_shape=[jax.ShapeDtypeStruct((b, l, 128), BF16), jax.ShapeDtypeStruct((1, 128), F32)],
        compiler_params=_params(("arbitrary",)),
    )(f3, bf_row, dcq, dck)


A_Q_BLK = B_SEG // A_WIDTH
A_K_BLK = (B_SEG + A_WIDTH) // 128
A_V_BLK = A_K_BLK + 1
A_SEG_BLK = B_SEG // A_SEG
STACK = A_HEADS * BLOCK


def _lane_lo():
    return lax.broadcasted_iota(jnp.int32, (1, 128), 1) < HEAD_DIM


def _stack_heads(x, masked):
    lo = _lane_lo()
    blks = [x[:, 128 * j:128 * (j + 1)] for j in range(4)]
    if not masked:
        return jnp.concatenate(blks + blks, axis=0)
    zero = jnp.zeros_like(blks[0])
    return jnp.concatenate([jnp.where(lo, bk, zero) for bk in blks] + [jnp.where(lo, zero, bk) for bk in blks], axis=0)


def _unstack_heads(y):
    lo = _lane_lo()
    return jnp.concatenate([jnp.where(lo, y[128 * j:128 * (j + 1)], y[128 * (4 + j):128 * (5 + j)]) for j in range(4)], axis=1)


def _swa_scores(q, kcat, n, slope):
    rows = STACK
    r_i = lax.broadcasted_iota(jnp.int32, (rows, 3 * BLOCK), 0)
    c_i = lax.broadcasted_iota(jnp.int32, (rows, 3 * BLOCK), 1)
    qpos = n * BLOCK + (r_i & (BLOCK - 1))
    seg = c_i >> 7
    kpos = jnp.where(seg == 0, c_i, (n - 2) * BLOCK + c_i)
    dist = qpos - kpos
    s = lax.dot_general(q, kcat, _NT, preferred_element_type=F32) * SCALE
    s = s - slope * dist.astype(F32)
    is_meta = seg == 0
    band = jnp.logical_not(is_meta) & (dist < BLOCK) & (kpos >= PREFIX)
    meta = is_meta & (c_i >= N_PAD)
    return jnp.where((dist >= 0) & (band | meta), s, NEG)


def _swa_specs():
    def kv(col_blk):
        return [pl.BlockSpec((None, BLOCK, 128), lambda b, n: (b, 0, col_blk)),
                pl.BlockSpec((None, BLOCK, 128), lambda b, n: (b, jnp.maximum(n - 1, 0), col_blk)),
                pl.BlockSpec((None, BLOCK, 128), lambda b, n: (b, n, col_blk))]

    q_spec = pl.BlockSpec((None, BLOCK, A_WIDTH), lambda b, n: (b, n, A_Q_BLK))
    o_spec = pl.BlockSpec((None, BLOCK, A_WIDTH), lambda b, n: (b, n, 0))
    col = pl.BlockSpec((STACK, 1), lambda b, n: (0, 0))
    lse_spec = pl.BlockSpec((None, A_HEADS, BLOCK, 1), lambda b, n: (b, 0, n, 0))
    return q_spec, kv(A_K_BLK), kv(A_V_BLK), o_spec, col, lse_spec


def _swa_fwd(name, qkv, slopes, sinks):
    b, l, _ = qkv.shape
    nb = l // BLOCK

    def body(q_ref, k0_ref, kp_ref, kc_ref, v0_ref, vp_ref, vc_ref, sl_ref, sk_ref, o_ref, lse_ref):
        n = pl.program_id(1)
        qs = _stack_heads(q_ref[...], True)
        kcat = jnp.concatenate([k0_ref[...], kp_ref[...], kc_ref[...]], axis=0)
        vcat = jnp.concatenate([v0_ref[...], vp_ref[...], vc_ref[...]], axis=0)
        s = _swa_scores(qs, kcat, n, sl_ref[...])
        sink = sk_ref[...]
        m = jnp.maximum(jnp.max(s, axis=-1, keepdims=True), sink)
        p = jnp.exp(s - m)
        den = jnp.sum(p, axis=-1, keepdims=True) + jnp.exp(sink - m)
        o = lax.dot_general(p.astype(BF16), vcat, _NN, preferred_element_type=F32) / den
        o_ref[...] = _unstack_heads(o).astype(BF16)
        lse_ref[...] = (m + jnp.log(den)).reshape(A_HEADS, BLOCK, 1)

    q_spec, k_specs, v_specs, o_spec, col, lse_spec = _swa_specs()
    return pl.pallas_call(
        body, name=name, grid=(b, nb),
        in_specs=[q_spec] + k_specs + v_specs + [col, col], out_specs=[o_spec, lse_spec],
        out_shape=[jax.ShapeDtypeStruct((b, l, A_WIDTH), BF16), jax.ShapeDtypeStruct((b, A_HEADS, l, 1), F32)],
        compiler_params=_params(("parallel", "parallel")),
    )(qkv, qkv, qkv, qkv, qkv, qkv, qkv, slopes, sinks)


def _swa_bwd(name, qkv, o, lse, do, slopes, sinks, dqkv):
    b, l, _ = qkv.shape
    nb = l // BLOCK

    def body(q_ref, k0_ref, kp_ref, kc_ref, v0_ref, vp_ref, vc_ref, o_ref, lse_ref, do_ref, sl_ref, sk_ref, _,
             dx_ref, ds_ref, dk_acc, dv_acc):
        bi = pl.program_id(0)
        n = pl.program_id(1)
        qs = _stack_heads(q_ref[...], True)
        dos = _stack_heads(do_ref[...], True)
        os_ = _stack_heads(o_ref[...], False)
        lsev = lse_ref[...].reshape(STACK, 1)
        kcat = jnp.concatenate([k0_ref[...], kp_ref[...], kc_ref[...]], axis=0)
        vcat = jnp.concatenate([v0_ref[...], vp_ref[...], vc_ref[...]], axis=0)
        s = _swa_scores(qs, kcat, n, sl_ref[...])
        p = jnp.exp(s - lsev)
        dsum = jnp.sum(dos.astype(F32) * os_.astype(F32), axis=-1, keepdims=True)
        dp = lax.dot_general(dos, vcat, _NT, preferred_element_type=F32)
        dsc = (p * (dp - dsum)).astype(BF16)
        dq = lax.dot_general(dsc, kcat, _NN, preferred_element_type=F32) * SCALE
        row0 = pl.multiple_of(n * BLOCK, BLOCK)
        dx_ref[pl.ds(row0, BLOCK), 0:A_WIDTH] = _unstack_heads(dq).astype(BF16)
        dkc = lax.dot_general(dsc, qs, _TN, preferred_element_type=F32) * SCALE
        dvc = lax.dot_general(p.astype(BF16), dos, _TN, preferred_element_type=F32)

        @pl.when(n == 0)
        def _():
            dk_acc[...] = jnp.zeros_like(dk_acc)
            dv_acc[...] = jnp.zeros_like(dv_acc)

        starts = (0, pl.multiple_of(jnp.maximum(n - 1, 0) * BLOCK, BLOCK), row0)
        for t, st in enumerate(starts):
            dk_acc[pl.ds(st, BLOCK), :] += dkc[t * BLOCK:(t + 1) * BLOCK, :]
            dv_acc[pl.ds(st, BLOCK), :] += dvc[t * BLOCK:(t + 1) * BLOCK, :]

        @pl.when(n == nb - 1)
        def _():
            dx_ref[:, A_WIDTH:A_WIDTH + 128] = dk_acc[...].astype(BF16)
            dx_ref[:, A_WIDTH + 128:A_SEG] = dv_acc[...].astype(BF16)

        dsink = -(jnp.exp(sk_ref[...] - lsev) * dsum)
        r8 = lax.broadcasted_iota(jnp.int32, (8, 128), 0)
        acc = jnp.zeros((8, 128), F32)
        for hh in range(A_HEADS):
            acc = acc + jnp.where(r8 == hh, jnp.sum(dsink[hh * BLOCK:(hh + 1) * BLOCK, :]), 0.0)

        @pl.when((bi == 0) & (n == 0))
        def _():
            ds_ref[...] = jnp.zeros_like(ds_ref)

        ds_ref[...] += acc

    q_spec, k_specs, v_specs, o_spec, col, lse_spec = _swa_specs()
    return pl.pallas_call(
        body, name=name, grid=(b, nb),
        in_specs=[q_spec] + k_specs + v_specs + [o_spec, lse_spec, o_spec, col, col, pl.BlockSpec(memory_space=pl.ANY)],
        out_specs=[pl.BlockSpec((None, l, A_SEG), lambda bb, n: (bb, 0, A_SEG_BLK)),
                   pl.BlockSpec((8, 128), lambda bb, n: (0, 0))],
        out_shape=[jax.ShapeDtypeStruct(dqkv.shape, BF16), jax.ShapeDtypeStruct((8, 128), F32)],
        scratch_shapes=[pltpu.VMEM((l, 128), F32), pltpu.VMEM((l, 128), F32)],
        input_output_aliases={12: 0},
        compiler_params=_params(("arbitrary", "arbitrary")),
    )(qkv, qkv, qkv, qkv, qkv, qkv, qkv, o, lse, do, slopes, sinks, dqkv)


def _fox_scores(q, k, cq, ck, i):
    kh = k.shape[0]
    s = lax.dot_general(q, k, _NT, preferred_element_type=F32) * SCALE
    s = (s + cq) - ck
    qpos = i * BLOCK + lax.broadcasted_iota(jnp.int32, (BLOCK, kh), 0)
    kpos = lax.broadcasted_iota(jnp.int32, (BLOCK, kh), 1)
    return jnp.where((kpos <= qpos) & (kpos >= N_PAD), s, NEG)


def _pick_head(x, hh):
    lo = _lane_lo()
    return jnp.where(lo if hh == 0 else jnp.logical_not(lo), x, jnp.zeros_like(x))


def _fox_specs(l):
    pair = pl.BlockSpec((None, l, PAIR_W), lambda bi, hp: (bi, 0, hp))
    half = pl.BlockSpec((None, l, 128), lambda bi, hp: (bi, 0, hp))
    colv = pl.BlockSpec((None, 2, l, 1), lambda bi, hp: (bi, hp, 0, 0))
    rowv = pl.BlockSpec((None, 2, 1, l), lambda bi, hp: (bi, hp, 0, 0))
    return pair, half, colv, rowv


def _fox_fwd(name, qkv, c_col, c_row):
    b, l, _ = qkv.shape
    nb = l // BLOCK

    def body(x_ref, cc_ref, cr_ref, o_ref, lse_ref):
        for i in range(nb):
            rows = slice(i * BLOCK, (i + 1) * BLOCK)
            kh = (i + 1) * BLOCK
            qblk = x_ref[rows, 0:128]
            kv = x_ref[0:kh, 128:256]
            vv = x_ref[0:kh, 256:384]
            outs = []
            for hh in range(2):
                s = _fox_scores(_pick_head(qblk, hh), kv, cc_ref[hh, rows, :], cr_ref[hh, :, 0:kh], i)
                m = jnp.max(s, axis=-1, keepdims=True)
                p = jnp.exp(s - m)
                den = jnp.sum(p, axis=-1, keepdims=True)
                outs.append(lax.dot_general(p.astype(BF16), vv, _NN, preferred_element_type=F32) / den)
                lse_ref[hh, rows, :] = m + jnp.log(den)
            o_ref[rows, :] = jnp.where(_lane_lo(), outs[0], outs[1]).astype(BF16)

    pair, half, colv, rowv = _fox_specs(l)
    return pl.pallas_call(
        body, name=name, grid=(b, 4), in_specs=[pair, colv, rowv], out_specs=[half, colv],
        out_shape=[jax.ShapeDtypeStruct((b, l, B_WIDTH), BF16), jax.ShapeDtypeStruct((b, B_HEADS, l, 1), F32)],
        compiler_params=_params(("parallel", "parallel")),
    )(qkv, c_col, c_row)


def _fox_bwd(name, qkv, c_col, c_row, o, lse, do):
    b, l, _ = qkv.shape
    nb = l // BLOCK

    def body(x_ref, cc_ref, cr_ref, o_ref, lse_ref, do_ref, dx_ref, dcq_ref, dck_ref, dk_acc, dv_acc):
        dk_acc[...] = jnp.zeros_like(dk_acc)
        dv_acc[...] = jnp.zeros_like(dv_acc)
        dck_ref[...] = jnp.zeros_like(dck_ref)
        for i in range(nb):
            rows = slice(i * BLOCK, (i + 1) * BLOCK)
            kh = (i + 1) * BLOCK
            qblk = x_ref[rows, 0:128]
            kv = x_ref[0:kh, 128:256]
            vv = x_ref[0:kh, 256:384]
            doblk = do_ref[rows, :]
            ov = o_ref[rows, :].astype(F32)
            dqs = []
            for hh in range(2):
                qm = _pick_head(qblk, hh)
                dom = _pick_head(doblk, hh)
                s = _fox_scores(qm, kv, cc_ref[hh, rows, :], cr_ref[hh, :, 0:kh], i)
                p = jnp.exp(s - lse_ref[hh, rows, :])
                dsum = jnp.sum(dom.astype(F32) * ov, axis=-1, keepdims=True)
                dp = lax.dot_general(dom, vv, _NT, preferred_element_type=F32)
                ds = p * (dp - dsum)
                dsc = ds.astype(BF16)
                dqs.append(lax.dot_general(dsc, kv, _NN, preferred_element_type=F32) * SCALE)
                dk_acc[0:kh, :] += lax.dot_general(dsc, qm, _TN, preferred_element_type=F32) * SCALE
                dv_acc[0:kh, :] += lax.dot_general(p.astype(BF16), dom, _TN, preferred_element_type=F32)
                dcq_ref[hh, rows, :] = jnp.sum(ds, axis=-1, keepdims=True)
                dck_ref[hh, :, 0:kh] -= jnp.sum(ds, axis=0, keepdims=True)
            dx_ref[rows, 0:128] = jnp.where(_lane_lo(), dqs[0], dqs[1]).astype(BF16)
        dx_ref[:, 128:256] = dk_acc[...].astype(BF16)
        dx_ref[:, 256:384] = dv_acc[...].astype(BF16)

    pair, half, colv, rowv = _fox_specs(l)
    return pl.pallas_call(
        body, name=name, grid=(b, 4), in_specs=[pair, colv, rowv, half, colv, half],
        out_specs=[pair, colv, rowv],
        out_shape=[jax.ShapeDtypeStruct(qkv.shape, BF16), jax.ShapeDtypeStruct((b, B_HEADS, l, 1), F32),
                   jax.ShapeDtypeStruct((b, B_HEADS, 1, l), F32)],
        scratch_shapes=[pltpu.VMEM((l, 128), F32), pltpu.VMEM((l, 128), F32)],
        compiler_params=_params(("parallel", "parallel")),
    )(qkv, c_col, c_row, o, lse, do)


_FLIPS = ((0, 0, 1), (0, 1, 0), (0, 1, 1), (1, 0, 0), (1, 0, 1), (1, 1, 0), (1, 1, 1))


def _exchange(name, gather, scatter):
    ng, ns = len(gather), len(scatter)
    na = ng + ns
    npeer = len(_FLIPS)

    def body(*refs):
        ins = refs[:na]
        outs = refs[na:2 * na]
        send_sems, recv_sems, loc_sems = refs[2 * na:]
        x, y, c = lax.axis_index("x"), lax.axis_index("y"), lax.axis_index("c")
        me = 4 * x + 2 * y + c
        peers = []
        for fx, fy, fc in _FLIPS:
            px = 1 - x if fx else x
            py = 1 - y if fy else y
            pc = 1 - c if fc else c
            peers.append(((px, py, pc), 4 * px + 2 * py + pc))

        def remote(a, kk):
            dev, lin = peers[kk]
            src = ins[a] if a < ng else ins[a].at[lin]
            return pltpu.make_async_remote_copy(src_ref=src, dst_ref=outs[a].at[me], send_sem=send_sems.at[a * npeer + kk],
                                                recv_sem=recv_sems.at[a * npeer + kk], device_id=dev, device_id_type=MESH_ID)

        def arrival(a, kk):
            dev, lin = peers[kk]
            src = ins[a] if a < ng else ins[a].at[lin]
            return pltpu.make_async_remote_copy(src_ref=src, dst_ref=outs[a].at[lin], send_sem=send_sems.at[a * npeer + kk],
                                                recv_sem=recv_sems.at[a * npeer + kk], device_id=dev, device_id_type=MESH_ID)

        local = []
        for a in range(na):
            src = ins[a] if a < ng else ins[a].at[me]
            cp = pltpu.make_async_copy(src, outs[a].at[me], loc_sems.at[a])
            cp.start()
            local.append(cp)
        sent = [remote(a, kk) for kk in range(npeer) for a in range(na)]
        for cp in sent:
            cp.start()
        for kk in range(npeer):
            for a in range(na):
                arrival(a, kk).wait_recv()
        for cp in sent:
            cp.wait_send()
        for cp in local:
            cp.wait()

    arrs = list(gather) + list(scatter)
    out_shape = [jax.ShapeDtypeStruct((N_DEV,) + tuple(a.shape), a.dtype) for a in gather]
    out_shape += [jax.ShapeDtypeStruct(tuple(a.shape), a.dtype) for a in scatter]
    anyspec = pl.BlockSpec(memory_space=pl.ANY)
    return pl.pallas_call(
        body, name=name, in_specs=[anyspec] * na, out_specs=[anyspec] * na, out_shape=out_shape,
        scratch_shapes=[pltpu.SemaphoreType.DMA((na * npeer,)), pltpu.SemaphoreType.DMA((na * npeer,)),
                        pltpu.SemaphoreType.DMA((na,))],
        compiler_params=pltpu.CompilerParams(has_side_effects=True),
    )(*arrs)


def _peer_table():
    x, y, c = lax.axis_index("x"), lax.axis_index("y"), lax.axis_index("c")
    me = 4 * x + 2 * y + c
    peers = []
    for fx, fy, fc in _FLIPS:
        px = 1 - x if fx else x
        py = 1 - y if fy else y
        pc = 1 - c if fc else c
        peers.append(((px, py, pc), 4 * px + 2 * py + pc))
    return me, peers


_HBM = pl.BlockSpec(memory_space=pltpu.HBM)
_SEM = pl.BlockSpec(memory_space=pltpu.SEMAPHORE)
_ANY = pl.BlockSpec(memory_space=pl.ANY)
_EFFECT = pltpu.SideEffectType.DATAFLOW_SIDE_EFFECTING


def _split_copy(srcs_are_pieces, src_refs, land_refs, send_sem, recv_sem, a, kk, me, peers, arriving):
    dev, lin = peers[kk]
    npeer = len(_FLIPS)
    src = src_refs[a] if srcs_are_pieces[a] else src_refs[a].at[lin]
    dst = land_refs[a].at[lin] if arriving else land_refs[a].at[me]
    return pltpu.make_async_remote_copy(src_ref=src, dst_ref=dst, send_sem=send_sem.at[a * npeer + kk],
                                        recv_sem=recv_sem.at[a * npeer + kk], device_id=dev, device_id_type=MESH_ID)


def _xchg_start(name, gather, scatter, after=None):
    me_out = 4 * lax.axis_index("x") + 2 * lax.axis_index("y") + lax.axis_index("c")
    srcs = list(gather) + list(scatter)
    is_piece = [True] * len(gather) + [False] * len(scatter)
    lands = []
    for a, piece in zip(srcs, is_piece):
        own = a[None] if piece else lax.dynamic_slice_in_dim(a, me_out, 1, axis=0)
        shape = ((N_DEV,) + tuple(a.shape)) if piece else tuple(a.shape)
        start = (me_out,) + (0,) * (len(shape) - 1)
        lands.append(lax.dynamic_update_slice(jnp.zeros(shape, a.dtype), own, start))
    n = len(srcs)
    nsem = n * len(_FLIPS)
    has_after = after is not None

    def body(*refs):
        src_refs = refs[:n]
        land_refs = refs[n:2 * n]
        outs = refs[2 * n + (1 if has_after else 0):]
        send_sem, recv_sem = outs[0], outs[1]
        token = outs[-1]
        me, peers = _peer_table()
        for kk in range(len(_FLIPS)):
            for a in range(n):
                _split_copy(is_piece, src_refs, land_refs, send_sem, recv_sem, a, kk, me, peers, False).start()
        token[...] = jnp.zeros_like(token)

    out_shape = ([pltpu.SemaphoreType.DMA((nsem,)), pltpu.SemaphoreType.DMA((nsem,))]
                 + [pltpu.HBM(tuple(a.shape), a.dtype) for a in srcs] + [pltpu.HBM(tuple(a.shape), a.dtype) for a in lands]
                 + [jax.ShapeDtypeStruct((8, 128), F32)])
    args = [pltpu.with_memory_space_constraint(a, pltpu.HBM) for a in srcs + lands] + ([after] if has_after else [])
    res = pl.pallas_call(
        body, name=name, out_shape=out_shape,
        in_specs=[_HBM] * (2 * n) + ([_ANY] if has_after else []),
        out_specs=[_SEM, _SEM] + [_HBM] * (2 * n) + [pl.BlockSpec(memory_space=pltpu.VMEM)],
        input_output_aliases={i: 2 + i for i in range(2 * n)},
        compiler_params=pltpu.CompilerParams(has_side_effects=_EFFECT),
    )(*args)
    state = (res[0], res[1], list(res[2:2 + n]), list(res[2 + n:2 + 2 * n]), is_piece)
    return state, res[-1]


def _xchg_wait(name, state, after):
    send_sem, recv_sem, srcs, lands, is_piece = state
    n = len(srcs)

    def body(*refs):
        src_refs = refs[:n]
        land_refs = refs[n:2 * n]
        s_sem, r_sem = refs[2 * n], refs[2 * n + 1]
        me, peers = _peer_table()
        for kk in range(len(_FLIPS)):
            for a in range(n):
                cp = _split_copy(is_piece, src_refs, land_refs, s_sem, r_sem, a, kk, me, peers, True)
                cp.wait_send()
                cp.wait_recv()

    out_shape = [pltpu.HBM(tuple(a.shape), a.dtype) for a in srcs] + [pltpu.HBM(tuple(a.shape), a.dtype) for a in lands]
    res = pl.pallas_call(
        body, name=name, out_shape=out_shape,
        in_specs=[_HBM] * (2 * n) + [_SEM, _SEM, _ANY], out_specs=[_HBM] * (2 * n),
        input_output_aliases={i: i for i in range(2 * n)},
        compiler_params=pltpu.CompilerParams(has_side_effects=_EFFECT),
    )(*srcs, *lands, send_sem, recv_sem, after)
    return list(res[n:])


def _adam_math(w, g, m, v):
    m = ADAM_B1 * m + (1.0 - ADAM_B1) * g
    v = ADAM_B2 * v + (1.0 - ADAM_B2) * (g * g)
    m_hat = m / (1.0 - ADAM_B1 ** ADAM_STEP)
    v_hat = v / (1.0 - ADAM_B2 ** ADAM_STEP)
    delta = -ADAM_LR * (m_hat / (jnp.sqrt(v_hat) + ADAM_EPS) + ADAM_WD * w)
    return delta, m, v


def _adam(name, w, m, v, parts):
    r, c = w.shape
    npart, _, cp = parts.shape
    tr = _pick(r, (256, 176, 128, 64, 16, 8, 1))

    def body(w_ref, m_ref, v_ref, p_ref, g_ref, d_ref, mo_ref, vo_ref):
        g = p_ref[0].astype(F32)
        for pp in range(1, npart):
            g = g + p_ref[pp].astype(F32)
        g = g[:, 0:c]
        delta, mn, vn = _adam_math(w_ref[...], g, m_ref[...], v_ref[...])
        g_ref[...] = g
        d_ref[...] = delta
        mo_ref[...] = mn
        vo_ref[...] = vn

    blk = pl.BlockSpec((tr, c), lambda i: (i, 0))
    out = jax.ShapeDtypeStruct((r, c), F32)
    return pl.pallas_call(
        body, name=name, grid=(r // tr,),
        in_specs=[blk, blk, blk, pl.BlockSpec((npart, tr, cp), lambda i: (0, i, 0))],
        out_specs=[blk, blk, blk, blk], out_shape=[out, out, out, out], compiler_params=_params(("parallel",)),
    )(w, m, v, parts)


def _small_sum(name, packs):
    def body(p_ref, o_ref):
        tot = p_ref[0]
        for pp in range(1, N_DEV):
            tot = tot + p_ref[pp]
        o_ref[0:8, :] = tot[0:8, :]
        o_ref[8:24, :] = tot[8:24, :] + tot[24:40, :]

    return pl.pallas_call(body, name=name, out_shape=jax.ShapeDtypeStruct((24, D_MODEL), F32),
                          compiler_params=_params())(packs)


def _local_step(x, tgt, g1, gm, g2, gf, b_forget, sinks, weights, send):
    b, s, _ = x.shape
    l = s + PREFIX
    t = b * l
    w1i, meta = weights("ffn1_in", x)
    h0 = jnp.concatenate([jnp.zeros((b, N_PAD, D_MODEL), F32), jnp.broadcast_to(meta[None], (b, N_META, D_MODEL)), x],
                         axis=1).reshape(t, D_MODEL)

    n1 = _rms_fwd("rms1_fwd", h0, g1)
    gu1, a1 = _ffn_in_fwd("ffn1_in_fwd", n1, w1i)
    (w1o,) = weights("ffn1_out", a1)
    h1 = _mm_nn("ffn1_out_fwd", a1, w1o, alpha=0.5, res=h0)
    wi, wa, wb, wo = weights("mix", h1)
    um = _rms_fwd("rmsm_fwd", h1, gm)
    qkv = _mm_nn("proj_qkv_fwd", um, wi, out_dtype=BF16, tn_c=(768,), cols=(0, QKV_W))
    gates = _mm_nn("proj_gates_fwd", um, wi, tn_c=(1024,), cols=(P_GATES, 2 * D_MODEL))
    f2 = _mm_nn("proj_f_fwd", um, wi, tn_c=(128,), cols=(P_F, 128))
    qkv3 = qkv.reshape(b, l, QKV_W)
    f3 = f2.reshape(b, l, 128)
    bf_row = jnp.pad(b_forget, ((0, 0), (0, 128 - B_HEADS)))
    c_col, c_row = _fgate_fwd("fgate_fwd", f3, bf_row)
    head_of_row = jnp.arange(STACK) // BLOCK
    slopes = jnp.exp2(-8.0 * (head_of_row + 1).astype(F32) / A_HEADS).reshape(STACK, 1)
    sink_rows = jnp.repeat(sinks.reshape(A_HEADS), BLOCK).reshape(STACK, 1)
    oa3, lse_a = _swa_fwd("swa_fwd", qkv3, slopes, sink_rows)
    ob3, lse_b = _fox_fwd("fox_fwd", qkv3, c_col, c_row)
    oa = oa3.reshape(t, A_WIDTH)
    ob = ob3.reshape(t, B_WIDTH)
    ya = _mm_nn("branch_a_fwd", oa, wa)
    yb = _mm_nn("branch_b_fwd", ob, wb)
    mixed = _gate_fwd("gate_fwd", gates, ya, yb)
    h2 = _mm_nn("mix_out_fwd", mixed, wo, res=h1)
    w2i, w2o = weights("ffn2", h2)
    n2 = _rms_fwd("rms2_fwd", h2, g2)
    gu2, a2 = _ffn_in_fwd("ffn2_in_fwd", n2, w2i)
    h3 = _mm_nn("ffn2_out_fwd", a2, w2o, alpha=0.5, res=h2)

    dh3_3, loss_blk, dgf = _loss_head("loss_head", h3.reshape(b, l, D_MODEL), gf, tgt)
    dh3 = dh3_3.reshape(t, D_MODEL)

    def ffn_bwd(tag, dh, h_in, g_norm, n_in, gu, a, w_in_blk, w_out):
        dw_out = _mm_tn(tag + "_out_bwd_w", a, dh, alpha=0.5)
        dgu = _ffn_out_bwd_x(tag + "_out_bwd_x", dh, w_out, gu, dep=send(tag + "_out", (dw_out,)))
        dw_in = _ffn_in_bwd_w(tag + "_in_bwd_w", n_in, dgu)
        dn = _ffn_in_bwd_x(tag + "_in_bwd_x", dgu, w_in_blk, dep=send(tag + "_in", (dw_in,)))
        dh_in, dg = _rms_bwd("rms" + tag[-1] + "_bwd", h_in, g_norm, dn, dh)
        return dh_in, dg

    dh2, dg2 = ffn_bwd("ffn2", dh3, h2, g2, n2, gu2, a2, w2i, w2o)

    dmix = _mm_nt("mix_out_bwd_x", dh2, wo, tn_c=(512,))
    dwo = _mm_tn("mix_out_bwd_w", mixed, dh2)
    dya, dyb, dgates = _gate_bwd("gate_bwd", gates, ya, yb, dmix)
    doa = _mm_nt("branch_a_bwd_x", dya, wa, out_dtype=BF16, tn_c=(512,))
    dob = _mm_nt("branch_b_bwd_x", dyb, wb, out_dtype=BF16, tn_c=(512,))
    dwa = _mm_tn("branch_a_bwd_w", oa, dya, tm_c=(512,))
    dwb = _mm_tn("branch_b_bwd_w", ob, dyb, tm_c=(512,))
    dqkv3, dcq, dck = _fox_bwd("fox_bwd", qkv3, c_col, c_row, ob3, lse_b, dob.reshape(b, l, B_WIDTH))
    dqkv3, dsink = _swa_bwd("swa_bwd", qkv3, oa3, lse_a, doa.reshape(b, l, A_WIDTH), slopes, sink_rows, dqkv3)
    dqkv = dqkv3.reshape(t, QKV_W)
    df3, dbf = _fgate_bwd("fgate_bwd", f3, bf_row, dcq, dck)
    df = df3.reshape(t, 128)
    dwi_qkv = _mm_tn("proj_qkv_bwd_w", um, dqkv, tm_c=(512,), tn_c=(768,))
    dwi_g = _mm_tn("proj_gates_bwd_w", um, dgates, tm_c=(512,), tn_c=(512,))
    dwi_f = _mm_tn("proj_f_bwd_w", um, df, tm_c=(512,), tn_c=(128,))
    token = send("mix", (dwi_qkv, dwi_g, dwi_f, dwa, dwb, dwo))
    dum = _mm_nt("proj_gates_bwd_x", dgates, wi, tn_c=(512,), tk_c=(2 * D_MODEL,), kcols=(P_GATES, 2 * D_MODEL), dep=token)
    dum = _mm_nt("proj_qkv_bwd_x", dqkv, wi, tn_c=(512,), tk_c=(QKV_W,), kcols=(0, QKV_W), res=dum)
    dum = _mm_nt("proj_f_bwd_x", df, wi, tn_c=(512,), tk_c=(128,), kcols=(P_F, 128), res=dum)
    dh1, dgm = _rms_bwd("rmsm_bwd", h1, gm, dum, dh2)

    dh0, dg1 = ffn_bwd("ffn1", dh1, h0, g1, n1, gu1, a1, w1i, w1o)
    dh0_3 = dh0.reshape(b, l, D_MODEL)
    grad_x = dh0_3[:, PREFIX:, :]
    dmeta = dh0_3[:, N_PAD:PREFIX, :].reshape(b * N_META, D_MODEL)

    misc = jnp.concatenate([dbf[:, 0:B_HEADS], dsink[:, 0].reshape(1, A_HEADS), loss_blk[0:1, 0:1]], axis=1)
    misc = jnp.pad(misc, ((0, 0), (0, D_MODEL - misc.shape[1])))
    row = lax.broadcasted_iota(jnp.int32, (8, D_MODEL), 0)
    vec = jnp.zeros((8, D_MODEL), F32)
    for i, piece in enumerate((dg1, dgm, dg2, dgf, misc)):
        vec = jnp.where(row == i, piece, vec)
    small = jnp.concatenate([vec, dmeta], axis=0)
    return grad_x, small


def _pad_to(a, rows, cols):
    return jnp.pad(a, ((0, rows - a.shape[0]), (0, cols - a.shape[1])))


def _ffn_out_from_gathered(g):
    w = g.reshape(4, FF_SHARD, D_MODEL)
    return jnp.pad(w, ((0, 0), (0, FF_SHARD_P - FF_SHARD), (0, 0))).reshape(D_FF_P, D_MODEL)


def _ffn_out_to_scatter(dw):
    return dw.reshape(4, FF_SHARD_P, D_MODEL)[:, 0:FF_SHARD, :].reshape(N_DEV, FFO_SHARD, D_MODEL)


def _proj_segments():
    segs = [(HEAD_DIM * h, HEAD_DIM, 0, B_SEG + HEAD_DIM * A_HEAD_ORDER.index(h)) for h in range(A_HEADS)]
    segs += [(512, 128, 0, B_SEG + A_WIDTH), (640, 128, 0, B_SEG + A_WIDTH + 128)]
    for first, off in ((768, 0), (1280, 128), (1792, 256)):
        segs += [(first + 128 * hp, 128, 0, PAIR_W * hp + off) for hp in range(4)]
    segs += [(2304, B_HEADS, 2, 0), (2312, 2 * D_MODEL, 1, 0)]
    return segs


def _proj_from_gathered(g):
    def cols(first, width):
        out = []
        for p in range(N_DEV):
            lo, hi = max(first, WIN_SHARD * p), min(first + width, WIN_SHARD * (p + 1))
            if lo < hi:
                out.append(g[p, :, lo - WIN_SHARD * p:hi - WIN_SHARD * p])
        return out

    parts = []
    for arr in (0, 1, 2):
        for first, width, _, _ in sorted((s for s in _proj_segments() if s[2] == arr), key=lambda s: s[3]):
            parts += cols(first, width)
        if arr == 0:
            parts.append(jnp.zeros((D_MODEL, P_GATES - QKV_W), g.dtype))
    parts.append(jnp.zeros((D_MODEL, 128 - B_HEADS), g.dtype))
    return jnp.concatenate(parts, axis=1)


def _proj_to_scatter(dqkv_w, dg_w, df_w):
    arrays = (dqkv_w, dg_w, df_w)
    segs = sorted(_proj_segments())
    blocks = []
    for p in range(N_DEV):
        parts = []
        for first, width, arr, at in segs:
            lo, hi = max(first, WIN_SHARD * p), min(first + width, WIN_SHARD * (p + 1))
            if lo < hi:
                parts.append(arrays[arr][:, at + lo - first:at + hi - first])
        parts.append(jnp.zeros((D_MODEL, WIN_SHARD_P - WIN_SHARD), dqkv_w.dtype))
        blocks.append(jnp.concatenate(parts, axis=1))
    return jnp.stack(blocks, axis=0)


def _a_rows_from_natural(w):
    return jnp.concatenate([w[HEAD_DIM * h:HEAD_DIM * (h + 1)] for h in A_HEAD_ORDER], axis=0)


def _a_rows_to_natural(w):
    return jnp.concatenate([w[HEAD_DIM * A_HEAD_ORDER.index(h):HEAD_DIM * (A_HEAD_ORDER.index(h) + 1)]
                            for h in range(A_HEADS)], axis=0)


def kernel(x, meta_tokens, ffn1_norm, ffn1_w_in, ffn1_w_out, mix_norm, w_in, b_forget, attn_sinks, w_branch_a, w_branch_b, w_out, ffn2_norm, ffn2_w_in, ffn2_w_out, final_norm, loss_target, m_meta_tokens, m_ffn1_norm, m_ffn1_w_in, m_ffn1_w_out, m_mix_norm, m_w_in, m_b_forget, m_attn_sinks, m_w_branch_a, m_w_branch_b, m_w_out, m_ffn2_norm, m_ffn2_w_in, m_ffn2_w_out, m_final_norm, v_meta_tokens, v_ffn1_norm, v_ffn1_w_in, v_ffn1_w_out, v_mix_norm, v_w_in, v_b_forget, v_attn_sinks, v_w_branch_a, v_w_branch_b, v_w_out, v_ffn2_norm, v_ffn2_w_in, v_ffn2_w_out, v_final_norm):
    me = 4 * lax.axis_index("x") + 2 * lax.axis_index("y") + lax.axis_index("c")

    shards = (
        _pad_to(ffn1_w_in[0].astype(BF16), D_MODEL, FF_SHARD_P),
        ffn1_w_out[0].astype(BF16),
        _pad_to(w_in[0].astype(BF16), D_MODEL, WIN_SHARD_P),
        w_branch_a[0].astype(BF16), w_branch_b[0].astype(BF16), w_out[0].astype(BF16),
        _pad_to(ffn2_w_in[0].astype(BF16), D_MODEL, FF_SHARD_P),
        ffn2_w_out[0].astype(BF16),
        meta_tokens,
    )
    s1i, s1o, swi, swa, swb, swo, s2i, s2o, smeta = shards
    gather_state = {}
    gather_state["ffn1_in"], tok = _xchg_start("gather_ffn1_in_start", (s1i, smeta), ())
    gather_state["ffn1_out"], tok = _xchg_start("gather_ffn1_out_start", (s1o,), (), after=tok)
    gather_state["mix"], tok = _xchg_start("gather_mix_start", (swi, swa, swb, swo), (), after=tok)
    gather_state["ffn2"], tok = _xchg_start("gather_ffn2_start", (s2i, s2o), (), after=tok)
    started = {"tok": tok}

    def weights(group, after):
        if group == "ffn1_in":
            after = started["tok"]
        got = _xchg_wait("gather_" + group + "_wait", gather_state[group], after)
        if group == "mix":
            gwi, gwa, gwb, gwo = got
            return (_proj_from_gathered(gwi), _a_rows_from_natural(gwa.transpose(1, 0, 2).reshape(A_WIDTH, D_MODEL)),
                    gwb.transpose(1, 0, 2).reshape(B_WIDTH, D_MODEL), gwo.reshape(D_MODEL, D_MODEL))
        if group == "ffn1_in":
            return got[0].reshape(2, 4, D_MODEL, FF_SHARD_P), got[1].transpose(1, 0, 2).reshape(N_META, D_MODEL)
        if group == "ffn1_out":
            return (_ffn_out_from_gathered(got[0]),)
        return got[0].reshape(2, 4, D_MODEL, FF_SHARD_P), _ffn_out_from_gathered(got[1])

    scatter_state = {}

    def send(group, grads):
        if group == "mix":
            dwi_qkv, dwi_g, dwi_f, dwa, dwb, dwo = grads
            dwa = _a_rows_to_natural(dwa)
            blocks = (_proj_to_scatter(dwi_qkv, dwi_g, dwi_f), dwa.reshape(A_WIDTH, N_DEV, 128).transpose(1, 0, 2),
                      dwb.reshape(B_WIDTH, N_DEV, 128).transpose(1, 0, 2), dwo.reshape(N_DEV, 128, D_MODEL))
        elif group.endswith("_in"):
            blocks = (grads[0].reshape(N_DEV, D_MODEL, FF_SHARD_P),)
        else:
            blocks = (_ffn_out_to_scatter(grads[0]),)
        scatter_state[group], token = _xchg_start("scatter_" + group + "_start", (), blocks)
        return token

    gf = final_norm.reshape(1, D_MODEL)
    grad_x, small = _local_step(x, loss_target, ffn1_norm, mix_norm, ffn2_norm, gf, b_forget, attn_sinks, weights, send)

    (packs,) = _exchange("gather_small", (small,), ())
    recv = {}
    after = packs
    for group in ("ffn2_out", "ffn2_in", "mix", "ffn1_out", "ffn1_in"):
        recv[group] = _xchg_wait("scatter_" + group + "_wait", scatter_state[group], after)
        after = recv[group][0]
    (r2o,), (r2i,), (rwi, rwa, rwb, rwo), (r1o,), (r1i,) = (recv[g] for g in ("ffn2_out", "ffn2_in", "mix", "ffn1_out", "ffn1_in"))

    out = {}
    for nm, w, m, v, parts in (
        ("ffn1_w_in", ffn1_w_in, m_ffn1_w_in, v_ffn1_w_in, r1i), ("ffn1_w_out", ffn1_w_out, m_ffn1_w_out, v_ffn1_w_out, r1o),
        ("w_in", w_in, m_w_in, v_w_in, rwi), ("w_branch_a", w_branch_a, m_w_branch_a, v_w_branch_a, rwa),
        ("w_branch_b", w_branch_b, m_w_branch_b, v_w_branch_b, rwb), ("w_out", w_out, m_w_out, v_w_out, rwo),
        ("ffn2_w_in", ffn2_w_in, m_ffn2_w_in, v_ffn2_w_in, r2i), ("ffn2_w_out", ffn2_w_out, m_ffn2_w_out, v_ffn2_w_out, r2o),
    ):
        res4 = _adam("adam_" + nm, w[0], m[0], v[0], parts)
        out[nm] = tuple(r[None] for r in res4)

    tot = _small_sum("small_sum", packs)
    loss = tot[4, 2 * B_HEADS]
    g_meta = lax.dynamic_slice(tot[8:24, :], (0, me * 128), (N_META, 128))
    out["meta_tokens"] = tuple(_adam("adam_meta_tokens", meta_tokens, m_meta_tokens, v_meta_tokens, g_meta[None]))

    def pack_small(n1, nm, n2, nf, bfv, skv):
        misc = jnp.pad(jnp.concatenate([bfv, skv], axis=1), ((0, 0), (0, D_MODEL - 2 * B_HEADS)))
        row = lax.broadcasted_iota(jnp.int32, (8, D_MODEL), 0)
        vec = jnp.zeros((8, D_MODEL), F32)
        for i, piece in enumerate((n1, nm, n2, nf.reshape(1, D_MODEL), misc)):
            vec = jnp.where(row == i, piece, vec)
        return vec

    w_pack = pack_small(ffn1_norm, mix_norm, ffn2_norm, final_norm, b_forget, attn_sinks)
    m_pack = pack_small(m_ffn1_norm, m_mix_norm, m_ffn2_norm, m_final_norm, m_b_forget, m_attn_sinks)
    v_pack = pack_small(v_ffn1_norm, v_mix_norm, v_ffn2_norm, v_final_norm, v_b_forget, v_attn_sinks)
    small4 = _adam("adam_small", w_pack, m_pack, v_pack, tot[0:8][None])
    for i, nm in enumerate(("ffn1_norm", "mix_norm", "ffn2_norm")):
        out[nm] = tuple(r[i:i + 1] for r in small4)
    out["final_norm"] = tuple(r[3] for r in small4)
    out["b_forget"] = tuple(r[4:5, 0:B_HEADS] for r in small4)
    out["attn_sinks"] = tuple(r[4:5, B_HEADS:2 * B_HEADS] for r in small4)

    names = ("meta_tokens", "ffn1_norm", "ffn1_w_in", "ffn1_w_out", "mix_norm", "w_in", "b_forget", "attn_sinks",
             "w_branch_a", "w_branch_b", "w_out", "ffn2_norm", "ffn2_w_in", "ffn2_w_out", "final_norm")
    return (loss, grad_x) + tuple(out[nm][kind] for kind in range(4) for nm in names)
```

```python
import jax
import jax.numpy as jnp
from jax import lax
from jax.experimental import pallas as pl
from jax.experimental.pallas import tpu as pltpu

F32 = jnp.float32
BF16 = jnp.bfloat16

D_MODEL = 1024
N_META = 16
BLOCK = 128
PREFIX = 128
N_PAD = PREFIX - N_META
HEAD_DIM = 64
A_HEADS = 8
A_KV_HEADS = 2
A_GROUP = 4
B_HEADS = 8
A_WIDTH = 512
A_KV_WIDTH = 128
B_WIDTH = 512
D_FF = 2816
N_DEV = 8
FF_SHARD = 2 * D_FF // N_DEV
FF_SHARD_P = 768
FFO_SHARD = D_FF // N_DEV
FFO_SHARD_P = FF_SHARD_P // 2
D_FF_P = 4 * FF_SHARD_P
W_IN_COLS = 4360
WIN_SHARD = W_IN_COLS // N_DEV
WIN_SHARD_P = 640
PAIR_W = 3 * 128
B_SEG = 4 * PAIR_W
A_SEG = A_WIDTH + 2 * A_KV_WIDTH
QKV_W = B_SEG + A_SEG
P_GATES = 2 * (2 * D_MODEL)
P_F = P_GATES + 2 * D_MODEL
PROJ_P = P_F + 128
A_HEAD_ORDER = (0, 4, 1, 5, 2, 6, 3, 7)
EPS = 1e-6
NEG = -1e30
SCALE = HEAD_DIM ** -0.5
ADAM_LR = 0.001
ADAM_B1 = 0.9
ADAM_B2 = 0.999
ADAM_EPS = 1e-08
ADAM_WD = 0.01
ADAM_STEP = 10
VMEM_LIMIT = 56 * 1024 * 1024
MESH_ID = pl.DeviceIdType.MESH
SMALL_ROWS = 40

_NN = (((1,), (0,)), ((), ()))
_NT = (((1,), (1,)), ((), ()))
_TN = (((0,), (0,)), ((), ()))


def _params(sem=None):
    return pltpu.CompilerParams(dimension_semantics=sem, vmem_limit_bytes=VMEM_LIMIT)


def _pick(n, cands):
    for c in cands:
        if n % c == 0:
            return c
    raise ValueError(f"no tile for {n}")


def _bf(v):
    return v if v.dtype == BF16 else v.astype(BF16)


def _mm(name, a, b, dims, grid, a_spec, b_spec, o_spec, out_shape, out_dtype, acc_shape, k_axis=None, nk=1,
        alpha=1.0, res=None, res_spec=None, dep=None):
    has_res = res is not None
    has_dep = dep is not None

    def body(*refs):
        a_ref, b_ref = refs[0], refs[1]
        r_ref = refs[2] if has_res else None
        o_ref = refs[2 + has_res + has_dep]

        def finish(acc):
            if alpha != 1.0:
                acc = acc * alpha
            if has_res:
                acc = acc + r_ref[...]
            o_ref[...] = acc.astype(o_ref.dtype)

        part = lax.dot_general(_bf(a_ref[...]), _bf(b_ref[...]), dims, preferred_element_type=F32)
        if nk == 1:
            finish(part)
        else:
            acc_ref = refs[-1]
            k = pl.program_id(k_axis)

            @pl.when(k == 0)
            def _():
                acc_ref[...] = part

            @pl.when(k > 0)
            def _():
                acc_ref[...] += part

            @pl.when(k == nk - 1)
            def _():
                finish(acc_ref[...])

    in_specs = [a_spec, b_spec] + ([res_spec] if has_res else []) + ([pl.BlockSpec(memory_space=pl.ANY)] if has_dep else [])
    args = (a, b) + ((res,) if has_res else ()) + ((dep,) if has_dep else ())
    sem = tuple("arbitrary" if (nk > 1 and i == k_axis) else "parallel" for i in range(len(grid)))
    return pl.pallas_call(
        body, name=name, grid=grid, in_specs=in_specs, out_specs=o_spec,
        out_shape=jax.ShapeDtypeStruct(out_shape, out_dtype),
        scratch_shapes=[pltpu.VMEM(acc_shape, F32)] if nk > 1 else [],
        compiler_params=_params(sem),
    )(*args)


def _mm_nn(name, a, b, out_dtype=F32, alpha=1.0, res=None, tn_c=(512, 640, 256, 128), cols=None):
    t, k = a.shape
    c0, n = (0, b.shape[1]) if cols is None else cols
    tm = _pick(t, (1088, 768, 512, 256, 128))
    tn = _pick(n, tn_c)
    assert c0 % tn == 0
    jb = c0 // tn
    return _mm(name, a, b, _NN, (t // tm, n // tn),
               pl.BlockSpec((tm, k), lambda i, j: (i, 0)), pl.BlockSpec((k, tn), lambda i, j: (0, jb + j)),
               pl.BlockSpec((tm, tn), lambda i, j: (i, j)), (t, n), out_dtype, None,
               alpha=alpha, res=res, res_spec=pl.BlockSpec((tm, tn), lambda i, j: (i, j)))


def _mm_nt(name, a, b, out_dtype=F32, alpha=1.0, tn_c=(768, 512, 256, 128), tk_c=None, dep=None, res=None, kcols=None):
    t, k = a.shape
    n = b.shape[0]
    c0 = 0 if kcols is None else kcols[0]
    tm = _pick(t, (1088, 768, 512, 256, 128))
    tn = _pick(n, tn_c)
    tk = k if tk_c is None else _pick(k, tk_c)
    nk = k // tk
    assert c0 % tk == 0
    kb = c0 // tk
    return _mm(name, a, b, _NT, (t // tm, n // tn, nk),
               pl.BlockSpec((tm, tk), lambda i, j, kk: (i, kk)), pl.BlockSpec((tn, tk), lambda i, j, kk: (j, kb + kk)),
               pl.BlockSpec((tm, tn), lambda i, j, kk: (i, j)), (t, n), out_dtype, (tm, tn), k_axis=2, nk=nk, alpha=alpha,
               dep=dep, res=res, res_spec=pl.BlockSpec((tm, tn), lambda i, j, kk: (i, j)))


def _mm_tn(name, a, b, out_dtype=BF16, alpha=1.0, tm_c=(768, 512, 256, 128), tn_c=(512, 640, 256, 128)):
    t, m = a.shape
    n = b.shape[1]
    tm = _pick(m, tm_c)
    tn = _pick(n, tn_c)
    tk = t
    nk = 1
    return _mm(name, a, b, _TN, (m // tm, n // tn, nk),
               pl.BlockSpec((tk, tm), lambda i, j, kk: (kk, i)), pl.BlockSpec((tk, tn), lambda i, j, kk: (kk, j)),
               pl.BlockSpec((tm, tn), lambda i, j, kk: (i, j)), (m, n), out_dtype, (tm, tn), k_axis=2, nk=nk, alpha=alpha)


def _silu_grads(g, u, d):
    sg = jax.nn.sigmoid(g)
    return d * u * (sg * (1.0 + g * (1.0 - sg))), d * (g * sg)


def _ffn_in_fwd(name, n, wblk):
    t = n.shape[0]
    tm = _pick(t, (1088, 768, 512, 256, 128))

    def body(n_ref, w_ref, gu_ref, a_ref):
        nv = n_ref[...]
        g = lax.dot_general(nv, w_ref[0], _NN, preferred_element_type=F32)
        u = lax.dot_general(nv, w_ref[1], _NN, preferred_element_type=F32)
        gu_ref[0] = g.astype(BF16)
        gu_ref[1] = u.astype(BF16)
        a_ref[...] = ((g * jax.nn.sigmoid(g)) * u).astype(BF16)

    return pl.pallas_call(
        body, name=name, grid=(t // tm, 4),
        in_specs=[pl.BlockSpec((tm, D_MODEL), lambda i, j: (i, 0)),
                  pl.BlockSpec((2, None, D_MODEL, FF_SHARD_P), lambda i, j: (0, j, 0, 0))],
        out_specs=[pl.BlockSpec((2, tm, FF_SHARD_P), lambda i, j: (0, i, j)),
                   pl.BlockSpec((tm, FF_SHARD_P), lambda i, j: (i, j))],
        out_shape=[jax.ShapeDtypeStruct((2, t, D_FF_P), BF16), jax.ShapeDtypeStruct((t, D_FF_P), BF16)],
        compiler_params=_params(("parallel", "parallel")),
    )(n, wblk)


def _ffn_out_bwd_x(name, dh, w_out, gu, dep=None):
    t = dh.shape[0]
    tm = _pick(t, (1088, 768, 512, 256, 128))
    has_dep = dep is not None

    def body(dh_ref, w_ref, gu_ref, *rest):
        o_ref = rest[-1]
        da = lax.dot_general(_bf(dh_ref[...]), w_ref[...], _NT, preferred_element_type=F32) * 0.5
        dg, du = _silu_grads(gu_ref[0].astype(F32), gu_ref[1].astype(F32), da)
        o_ref[0] = dg.astype(BF16)
        o_ref[1] = du.astype(BF16)

    gu_spec = pl.BlockSpec((2, tm, FF_SHARD_P), lambda i, j: (0, i, j))
    return pl.pallas_call(
        body, name=name, grid=(t // tm, 4),
        in_specs=[pl.BlockSpec((tm, D_MODEL), lambda i, j: (i, 0)), pl.BlockSpec((FF_SHARD_P, D_MODEL), lambda i, j: (j, 0)),
                  gu_spec] + ([pl.BlockSpec(memory_space=pl.ANY)] if has_dep else []),
        out_specs=gu_spec, out_shape=jax.ShapeDtypeStruct((2, t, D_FF_P), BF16),
        compiler_params=_params(("parallel", "parallel")),
    )(*((dh, w_out, gu) + ((dep,) if has_dep else ())))


def _ffn_in_bwd_x(name, dgu, wblk, dep=None):
    t = dgu.shape[1]
    tm = _pick(t, (272, 256, 128))
    has_dep = dep is not None

    def body(d_ref, w_ref, *rest):
        o_ref = rest[-1]
        acc = None
        for s in range(2):
            for j in range(4):
                part = lax.dot_general(d_ref[s, :, FF_SHARD_P * j:FF_SHARD_P * (j + 1)], w_ref[s, j], _NT,
                                       preferred_element_type=F32)
                acc = part if acc is None else acc + part
        o_ref[...] = acc

    return pl.pallas_call(
        body, name=name, grid=(t // tm,),
        in_specs=[pl.BlockSpec((2, tm, D_FF_P), lambda i: (0, i, 0)),
                  pl.BlockSpec((2, 4, D_MODEL, FF_SHARD_P), lambda i: (0, 0, 0, 0))]
        + ([pl.BlockSpec(memory_space=pl.ANY)] if has_dep else []),
        out_specs=pl.BlockSpec((tm, D_MODEL), lambda i: (i, 0)),
        out_shape=jax.ShapeDtypeStruct((t, D_MODEL), F32), compiler_params=_params(("parallel",)),
    )(*((dgu, wblk) + ((dep,) if has_dep else ())))


def _ffn_in_bwd_w(name, n, dgu):
    t = n.shape[0]
    tk = t
    nk = 1
    return _mm(name, n, dgu, _TN, (2, 4, nk),
               pl.BlockSpec((tk, D_MODEL), lambda s, j, kk: (kk, 0)),
               pl.BlockSpec((None, tk, FF_SHARD_P), lambda s, j, kk: (s, kk, j)),
               pl.BlockSpec((None, None, D_MODEL, FF_SHARD_P), lambda s, j, kk: (s, j, 0, 0)),
               (2, 4, D_MODEL, FF_SHARD_P), BF16, (D_MODEL, FF_SHARD_P), k_axis=2, nk=nk)


def _rms_fwd(name, h, g):
    t = h.shape[0]
    tm = _pick(t, (544, 384, 256, 128))

    def body(h_ref, g_ref, o_ref):
        hv = h_ref[...]
        r = lax.rsqrt(jnp.mean(hv * hv, axis=-1, keepdims=True) + EPS)
        o_ref[...] = ((hv * r) * g_ref[...]).astype(BF16)

    return pl.pallas_call(
        body, name=name, grid=(t // tm,),
        in_specs=[pl.BlockSpec((tm, D_MODEL), lambda i: (i, 0)), pl.BlockSpec((1, D_MODEL), lambda i: (0, 0))],
        out_specs=pl.BlockSpec((tm, D_MODEL), lambda i: (i, 0)),
        out_shape=jax.ShapeDtypeStruct((t, D_MODEL), BF16), compiler_params=_params(("parallel",)),
    )(h, g)


def _rms_bwd(name, h, g, dn, dres):
    t = h.shape[0]
    tm = _pick(t, (544, 384, 256, 128))

    def body(h_ref, g_ref, dn_ref, dres_ref, dh_ref, dg_ref):
        i = pl.program_id(0)
        hv = h_ref[...]
        dnv = dn_ref[...]
        r = lax.rsqrt(jnp.mean(hv * hv, axis=-1, keepdims=True) + EPS)
        tv = dnv * g_ref[...]
        dot = jnp.mean(tv * hv, axis=-1, keepdims=True)
        dh_ref[...] = dres_ref[...] + (r * tv - hv * (r * r * r * dot))
        part = jnp.sum(dnv * (hv * r), axis=0, keepdims=True)

        @pl.when(i == 0)
        def _():
            dg_ref[...] = part

        @pl.when(i > 0)
        def _():
            dg_ref[...] += part

    row = pl.BlockSpec((tm, D_MODEL), lambda i: (i, 0))
    vec = pl.BlockSpec((1, D_MODEL), lambda i: (0, 0))
    return pl.pallas_call(
        body, name=name, grid=(t // tm,), in_specs=[row, vec, row, row], out_specs=[row, vec],
        out_shape=[jax.ShapeDtypeStruct((t, D_MODEL), F32), jax.ShapeDtypeStruct((1, D_MODEL), F32)],
        compiler_params=_params(("arbitrary",)),
    )(h, g, dn, dres)


def _gate_fwd(name, gates, ya, yb):
    t = gates.shape[0]
    tm = _pick(t, (272, 256, 128))

    def body(g_ref, ya_ref, yb_ref, o_ref):
        sa = jax.nn.sigmoid(g_ref[:, 0:D_MODEL])
        sb = jax.nn.sigmoid(g_ref[:, D_MODEL:2 * D_MODEL])
        o_ref[...] = (sa * ya_ref[...] + sb * yb_ref[...]).astype(BF16)

    blk = pl.BlockSpec((tm, D_MODEL), lambda i: (i, 0))
    wide = pl.BlockSpec((tm, 2 * D_MODEL), lambda i: (i, 0))
    return pl.pallas_call(
        body, name=name, grid=(t // tm,), in_specs=[wide, blk, blk], out_specs=blk,
        out_shape=jax.ShapeDtypeStruct((t, D_MODEL), BF16), compiler_params=_params(("parallel",)),
    )(gates, ya, yb)


def _gate_bwd(name, gates, ya, yb, dmix):
    t = gates.shape[0]
    tm = _pick(t, (272, 256, 128))

    def body(g_ref, ya_ref, yb_ref, dm_ref, dya_ref, dyb_ref, dg_ref):
        dm = dm_ref[...]
        sa = jax.nn.sigmoid(g_ref[:, 0:D_MODEL])
        sb = jax.nn.sigmoid(g_ref[:, D_MODEL:2 * D_MODEL])
        dya_ref[...] = (dm * sa).astype(BF16)
        dyb_ref[...] = (dm * sb).astype(BF16)
        dg_ref[:, 0:D_MODEL] = (dm * ya_ref[...] * (sa * (1.0 - sa))).astype(BF16)
        dg_ref[:, D_MODEL:2 * D_MODEL] = (dm * yb_ref[...] * (sb * (1.0 - sb))).astype(BF16)

    blk = pl.BlockSpec((tm, D_MODEL), lambda i: (i, 0))
    wide = pl.BlockSpec((tm, 2 * D_MODEL), lambda i: (i, 0))
    out = jax.ShapeDtypeStruct((t, D_MODEL), BF16)
    return pl.pallas_call(
        body, name=name, grid=(t // tm,), in_specs=[wide, blk, blk, blk], out_specs=[blk, blk, wide],
        out_shape=[out, out, jax.ShapeDtypeStruct((t, 2 * D_MODEL), BF16)], compiler_params=_params(("parallel",)),
    )(gates, ya, yb, dmix)


def _loss_head(name, h3, gf, tgt):
    b, l, _ = h3.shape
    nb = l // BLOCK

    def body(h_ref, g_ref, t_ref, dh_ref, loss_ref, dg_ref):
        first = (pl.program_id(0) == 0) & (pl.program_id(1) == 0)
        real = (pl.program_id(1) > 0).astype(F32)
        hv = h_ref[...]
        g = g_ref[...]
        r = lax.rsqrt(jnp.mean(hv * hv, axis=-1, keepdims=True) + EPS)
        xn = hv * r
        err = (xn * g - t_ref[...]) * real
        lpart = 0.5 * jnp.sum(jnp.mean(err * err, axis=-1, keepdims=True), axis=0, keepdims=True)
        dy = err * (1.0 / D_MODEL)
        tv = dy * g
        dot = jnp.mean(tv * hv, axis=-1, keepdims=True)
        dh_ref[...] = r * tv - hv * (r * r * r * dot)
        gpart = jnp.sum(dy * xn, axis=0, keepdims=True)

        @pl.when(first)
        def _():
            loss_ref[...] = jnp.zeros_like(loss_ref)
            dg_ref[...] = jnp.zeros_like(dg_ref)

        loss_ref[...] += jnp.broadcast_to(lpart, loss_ref.shape)
        dg_ref[...] += gpart

    return pl.pallas_call(
        body, name=name, grid=(b, nb),
        in_specs=[pl.BlockSpec((None, BLOCK, D_MODEL), lambda bi, n: (bi, n, 0)),
                  pl.BlockSpec((1, D_MODEL), lambda bi, n: (0, 0)),
                  pl.BlockSpec((None, BLOCK, D_MODEL), lambda bi, n: (bi, jnp.maximum(n - 1, 0), 0))],
        out_specs=[pl.BlockSpec((None, BLOCK, D_MODEL), lambda bi, n: (bi, n, 0)),
                   pl.BlockSpec((8, 128), lambda bi, n: (0, 0)),
                   pl.BlockSpec((1, D_MODEL), lambda bi, n: (0, 0))],
        out_shape=[jax.ShapeDtypeStruct(h3.shape, F32), jax.ShapeDtypeStruct((8, 128), F32),
                   jax.ShapeDtypeStruct((1, D_MODEL), F32)],
        compiler_params=_params(("arbitrary", "arbitrary")),
    )(h3, gf, tgt)


def _fgate_fwd(name, f3, bf_row):
    b, l, _ = f3.shape
    nb = l // BLOCK

    def body(f_ref, b_ref, cc_ref, cr_ref):
        r_i = lax.broadcasted_iota(jnp.int32, (BLOCK, BLOCK), 0)
        c_i = lax.broadcasted_iota(jnp.int32, (BLOCK, BLOCK), 1)
        tri = (r_i >= c_i).astype(F32)
        carry = jnp.zeros((1, 128), F32)
        for blk in range(nb):
            rows = slice(blk * BLOCK, (blk + 1) * BLOCK)
            z = f_ref[rows, :] + b_ref[...]
            lf = jnp.minimum(z, 0.0) - jnp.log(1.0 + jnp.exp(-jnp.abs(z)))
            cb = jnp.dot(tri, lf, preferred_element_type=F32, precision=lax.Precision.HIGHEST) + carry
            carry = cb[BLOCK - 1:BLOCK, :]
            cbt = cb.T
            for hh in range(B_HEADS):
                cc_ref[hh, rows, :] = jnp.sum(jnp.where(c_i == hh, cb, 0.0), axis=1, keepdims=True)
                cr_ref[hh, :, rows] = cbt[hh:hh + 1, :]

    return pl.pallas_call(
        body, name=name, grid=(b,),
        in_specs=[pl.BlockSpec((None, l, 128), lambda bi: (bi, 0, 0)),
                  pl.BlockSpec((1, 128), lambda bi: (0, 0))],
        out_specs=[pl.BlockSpec((None, B_HEADS, l, 1), lambda bi: (bi, 0, 0, 0)),
                   pl.BlockSpec((None, B_HEADS, 1, l), lambda bi: (bi, 0, 0, 0))],
        out_shape=[jax.ShapeDtypeStruct((b, B_HEADS, l, 1), F32), jax.ShapeDtypeStruct((b, B_HEADS, 1, l), F32)],
        compiler_params=_params(("parallel",)),
    )(f3, bf_row)


def _fgate_bwd(name, f3, bf_row, dcq, dck):
    b, l, _ = f3.shape
    nb = l // BLOCK

    def body(f_ref, b_ref, dcq_ref, dck_ref, df_ref, db_ref):
        r_i = lax.broadcasted_iota(jnp.int32, (BLOCK, BLOCK), 0)
        c_i = lax.broadcasted_iota(jnp.int32, (BLOCK, BLOCK), 1)
        tri = (r_i <= c_i).astype(F32)
        carry = jnp.zeros((1, 128), F32)
        total = jnp.zeros((1, 128), F32)
        for blk in range(nb - 1, -1, -1):
            rows = slice(blk * BLOCK, (blk + 1) * BLOCK)
            krows = jnp.concatenate([dck_ref[hh, :, rows] for hh in range(B_HEADS)]
                                    + [jnp.zeros((BLOCK - B_HEADS, BLOCK), F32)], axis=0)
            dcb = krows.T
            for hh in range(B_HEADS):
                dcb = dcb + jnp.where(c_i == hh, dcq_ref[hh, rows, :], 0.0)
            rc = jnp.dot(tri, dcb, preferred_element_type=F32, precision=lax.Precision.HIGHEST) + carry
            carry = rc[0:1, :]
            z = f_ref[rows, :] + b_ref[...]
            df = rc * (1.0 / (1.0 + jnp.exp(z)))
            df_ref[rows, :] = df.astype(BF16)
            total = total + jnp.sum(df, axis=0, keepdims=True)

        @pl.when(pl.program_id(0) == 0)
        def _():
            db_ref[...] = total

        @pl.when(pl.program_id(0) > 0)
        def _():
            db_ref[...] += total

    return pl.pallas_call(
        body, name=name, grid=(b,),
        in_specs=[pl.BlockSpec((None, l, 128), lambda bi: (bi, 0, 0)),
                  pl.BlockSpec((1, 128), lambda bi: (0, 0)),
                  pl.BlockSpec((None, B_HEADS, l, 1), lambda bi: (bi, 0, 0, 0)),
                  pl.BlockSpec((None, B_HEADS, 1, l), lambda bi: (bi, 0, 0, 0))],
        out_specs=[pl.BlockSpec((None, l, 128), lambda bi: (bi, 0, 0)), pl.BlockSpec((1, 128), lambda bi: (0, 0))],
        out_shape=[jax.ShapeDtypeStruct((b, l, 128), BF16), jax.ShapeDtypeStruct((1, 128), F32)],
        compiler_params=_params(("arbitrary",)),
    )(f3, bf_row, dcq, dck)


A_Q_BLK = B_SEG // A_WIDTH
A_K_BLK = (B_SEG + A_WIDTH) // 128
A_V_BLK = A_K_BLK + 1
A_SEG_BLK = B_SEG // A_SEG
STACK = A_HEADS * BLOCK


def _lane_lo():
    return lax.broadcasted_iota(jnp.int32, (1, 128), 1) < HEAD_DIM


def _stack_heads(x, masked):
    lo = _lane_lo()
    blks = [x[:, 128 * j:128 * (j + 1)] for j in range(4)]
    if not masked:
        return jnp.concatenate(blks + blks, axis=0)
    zero = jnp.zeros_like(blks[0])
    return jnp.concatenate([jnp.where(lo, bk, zero) for bk in blks] + [jnp.where(lo, zero, bk) for bk in blks], axis=0)


def _unstack_heads(y):
    lo = _lane_lo()
    return jnp.concatenate([jnp.where(lo, y[128 * j:128 * (j + 1)], y[128 * (4 + j):128 * (5 + j)]) for j in range(4)], axis=1)


def _swa_scores(q, kcat, n, slope):
    rows = STACK
    r_i = lax.broadcasted_iota(jnp.int32, (rows, 3 * BLOCK), 0)
    c_i = lax.broadcasted_iota(jnp.int32, (rows, 3 * BLOCK), 1)
    qpos = n * BLOCK + (r_i & (BLOCK - 1))
    seg = c_i >> 7
    kpos = jnp.where(seg == 0, c_i, (n - 2) * BLOCK + c_i)
    dist = qpos - kpos
    s = lax.dot_general(q, kcat, _NT, preferred_element_type=F32) * SCALE
    s = s - slope * dist.astype(F32)
    is_meta = seg == 0
    band = jnp.logical_not(is_meta) & (dist < BLOCK) & (kpos >= PREFIX)
    meta = is_meta & (c_i >= N_PAD)
    return jnp.where((dist >= 0) & (band | meta), s, NEG)


def _swa_specs():
    def kv(col_blk):
        return [pl.BlockSpec((None, BLOCK, 128), lambda b, n: (b, 0, col_blk)),
                pl.BlockSpec((None, BLOCK, 128), lambda b, n: (b, jnp.maximum(n - 1, 0), col_blk)),
                pl.BlockSpec((None, BLOCK, 128), lambda b, n: (b, n, col_blk))]

    q_spec = pl.BlockSpec((None, BLOCK, A_WIDTH), lambda b, n: (b, n, A_Q_BLK))
    o_spec = pl.BlockSpec((None, BLOCK, A_WIDTH), lambda b, n: (b, n, 0))
    col = pl.BlockSpec((STACK, 1), lambda b, n: (0, 0))
    lse_spec = pl.BlockSpec((None, A_HEADS, BLOCK, 1), lambda b, n: (b, 0, n, 0))
    return q_spec, kv(A_K_BLK), kv(A_V_BLK), o_spec, col, lse_spec


def _swa_fwd(name, qkv, slopes, sinks):
    b, l, _ = qkv.shape
    nb = l // BLOCK

    def body(q_ref, k0_ref, kp_ref, kc_ref, v0_ref, vp_ref, vc_ref, sl_ref, sk_ref, o_ref, lse_ref):
        n = pl.program_id(1)
        qs = _stack_heads(q_ref[...], True)
        kcat = jnp.concatenate([k0_ref[...], kp_ref[...], kc_ref[...]], axis=0)
        vcat = jnp.concatenate([v0_ref[...], vp_ref[...], vc_ref[...]], axis=0)
        s = _swa_scores(qs, kcat, n, sl_ref[...])
        sink = sk_ref[...]
        m = jnp.maximum(jnp.max(s, axis=-1, keepdims=True), sink)
        p = jnp.exp(s - m)
        den = jnp.sum(p, axis=-1, keepdims=True) + jnp.exp(sink - m)
        o = lax.dot_general(p.astype(BF16), vcat, _NN, preferred_element_type=F32) / den
        o_ref[...] = _unstack_heads(o)
        lse_ref[...] = (m + jnp.log(den)).reshape(A_HEADS, BLOCK, 1)

    q_spec, k_specs, v_specs, o_spec, col, lse_spec = _swa_specs()
    return pl.pallas_call(
        body, name=name, grid=(b, nb),
        in_specs=[q_spec] + k_specs + v_specs + [col, col], out_specs=[o_spec, lse_spec],
        out_shape=[jax.ShapeDtypeStruct((b, l, A_WIDTH), F32), jax.ShapeDtypeStruct((b, A_HEADS, l, 1), F32)],
        compiler_params=_params(("parallel", "parallel")),
    )(qkv, qkv, qkv, qkv, qkv, qkv, qkv, slopes, sinks)


def _swa_bwd(name, qkv, o, lse, do, slopes, sinks, dqkv):
    b, l, _ = qkv.shape
    nb = l // BLOCK

    def body(q_ref, k0_ref, kp_ref, kc_ref, v0_ref, vp_ref, vc_ref, o_ref, lse_ref, do_ref, sl_ref, sk_ref, _,
             dx_ref, ds_ref, dk_acc, dv_acc):
        bi = pl.program_id(0)
        n = pl.program_id(1)
        qs = _stack_heads(q_ref[...], True)
        dos32 = _stack_heads(do_ref[...], True)
        dos = dos32.astype(BF16)
        os_ = _stack_heads(o_ref[...], False)
        lsev = lse_ref[...].reshape(STACK, 1)
        kcat = jnp.concatenate([k0_ref[...], kp_ref[...], kc_ref[...]], axis=0)
        vcat = jnp.concatenate([v0_ref[...], vp_ref[...], vc_ref[...]], axis=0)
        s = _swa_scores(qs, kcat, n, sl_ref[...])
        p = jnp.exp(s - lsev)
        dsum = jnp.sum(dos32 * os_, axis=-1, keepdims=True)
        dp = lax.dot_general(dos, vcat, _NT, preferred_element_type=F32)
        dsc = (p * (dp - dsum)).astype(BF16)
        dq = lax.dot_general(dsc, kcat, _NN, preferred_element_type=F32) * SCALE
        row0 = pl.multiple_of(n * BLOCK, BLOCK)
        dx_ref[pl.ds(row0, BLOCK), 0:A_WIDTH] = _unstack_heads(dq).astype(BF16)
        dkc = lax.dot_general(dsc, qs, _TN, preferred_element_type=F32) * SCALE
        dvc = lax.dot_general(p.astype(BF16), dos, _TN, preferred_element_type=F32)

        @pl.when(n == 0)
        def _():
            dk_acc[...] = jnp.zeros_like(dk_acc)
            dv_acc[...] = jnp.zeros_like(dv_acc)

        starts = (0, pl.multiple_of(jnp.maximum(n - 1, 0) * BLOCK, BLOCK), row0)
        for t, st in enumerate(starts):
            dk_acc[pl.ds(st, BLOCK), :] += dkc[t * BLOCK:(t + 1) * BLOCK, :]
            dv_acc[pl.ds(st, BLOCK), :] += dvc[t * BLOCK:(t + 1) * BLOCK, :]

        @pl.when(n == nb - 1)
        def _():
            dx_ref[:, A_WIDTH:A_WIDTH + 128] = dk_acc[...].astype(BF16)
            dx_ref[:, A_WIDTH + 128:A_SEG] = dv_acc[...].astype(BF16)

        dsink = -(jnp.exp(sk_ref[...] - lsev) * dsum)
        r8 = lax.broadcasted_iota(jnp.int32, (8, 128), 0)
        acc = jnp.zeros((8, 128), F32)
        for hh in range(A_HEADS):
            acc = acc + jnp.where(r8 == hh, jnp.sum(dsink[hh * BLOCK:(hh + 1) * BLOCK, :]), 0.0)

        @pl.when((bi == 0) & (n == 0))
        def _():
            ds_ref[...] = jnp.zeros_like(ds_ref)

        ds_ref[...] += acc

    q_spec, k_specs, v_specs, o_spec, col, lse_spec = _swa_specs()
    return pl.pallas_call(
        body, name=name, grid=(b, nb),
        in_specs=[q_spec] + k_specs + v_specs + [o_spec, lse_spec, o_spec, col, col, pl.BlockSpec(memory_space=pl.ANY)],
        out_specs=[pl.BlockSpec((None, l, A_SEG), lambda bb, n: (bb, 0, A_SEG_BLK)),
                   pl.BlockSpec((8, 128), lambda bb, n: (0, 0))],
        out_shape=[jax.ShapeDtypeStruct(dqkv.shape, BF16), jax.ShapeDtypeStruct((8, 128), F32)],
        scratch_shapes=[pltpu.VMEM((l, 128), F32), pltpu.VMEM((l, 128), F32)],
        input_output_aliases={12: 0},
        compiler_params=_params(("arbitrary", "arbitrary")),
    )(qkv, qkv, qkv, qkv, qkv, qkv, qkv, o, lse, do, slopes, sinks, dqkv)


def _fox_scores(q, k, cq, ck, i):
    kh = k.shape[0]
    s = lax.dot_general(q, k, _NT, preferred_element_type=F32) * SCALE
    s = (s + cq) - ck
    qpos = i * BLOCK + lax.broadcasted_iota(jnp.int32, (BLOCK, kh), 0)
    kpos = lax.broadcasted_iota(jnp.int32, (BLOCK, kh), 1)
    return jnp.where((kpos <= qpos) & (kpos >= N_PAD), s, NEG)


def _pick_head(x, hh):
    lo = _lane_lo()
    return jnp.where(lo if hh == 0 else jnp.logical_not(lo), x, jnp.zeros_like(x))


def _fox_specs(l):
    pair = pl.BlockSpec((None, l, PAIR_W), lambda bi, hp: (bi, 0, hp))
    half = pl.BlockSpec((None, l, 128), lambda bi, hp: (bi, 0, hp))
    colv = pl.BlockSpec((None, 2, l, 1), lambda bi, hp: (bi, hp, 0, 0))
    rowv = pl.BlockSpec((None, 2, 1, l), lambda bi, hp: (bi, hp, 0, 0))
    return pair, half, colv, rowv


def _fox_fwd(name, qkv, c_col, c_row):
    b, l, _ = qkv.shape
    nb = l // BLOCK

    def body(x_ref, cc_ref, cr_ref, o_ref, lse_ref):
        for i in range(nb):
            rows = slice(i * BLOCK, (i + 1) * BLOCK)
            kh = (i + 1) * BLOCK
            qblk = x_ref[rows, 0:128]
            kv = x_ref[0:kh, 128:256]
            vv = x_ref[0:kh, 256:384]
            outs = []
            for hh in range(2):
                s = _fox_scores(_pick_head(qblk, hh), kv, cc_ref[hh, rows, :], cr_ref[hh, :, 0:kh], i)
                m = jnp.max(s, axis=-1, keepdims=True)
                p = jnp.exp(s - m)
                den = jnp.sum(p, axis=-1, keepdims=True)
                outs.append(lax.dot_general(p.astype(BF16), vv, _NN, preferred_element_type=F32) / den)
                lse_ref[hh, rows, :] = m + jnp.log(den)
            o_ref[rows, :] = jnp.where(_lane_lo(), outs[0], outs[1]).astype(BF16)

    pair, half, colv, rowv = _fox_specs(l)
    return pl.pallas_call(
        body, name=name, grid=(b, 4), in_specs=[pair, colv, rowv], out_specs=[half, colv],
        out_shape=[jax.ShapeDtypeStruct((b, l, B_WIDTH), BF16), jax.ShapeDtypeStruct((b, B_HEADS, l, 1), F32)],
        compiler_params=_params(("parallel", "parallel")),
    )(qkv, c_col, c_row)


def _fox_bwd(name, qkv, c_col, c_row, o, lse, do):
    b, l, _ = qkv.shape
    nb = l // BLOCK

    def body(x_ref, cc_ref, cr_ref, o_ref, lse_ref, do_ref, dx_ref, dcq_ref, dck_ref, dk_acc, dv_acc):
        dk_acc[...] = jnp.zeros_like(dk_acc)
        dv_acc[...] = jnp.zeros_like(dv_acc)
        dck_ref[...] = jnp.zeros_like(dck_ref)
        for i in range(nb):
            rows = slice(i * BLOCK, (i + 1) * BLOCK)
            kh = (i + 1) * BLOCK
            qblk = x_ref[rows, 0:128]
            kv = x_ref[0:kh, 128:256]
            vv = x_ref[0:kh, 256:384]
            doblk = do_ref[rows, :]
            ov = o_ref[rows, :].astype(F32)
            dqs = []
            for hh in range(2):
                qm = _pick_head(qblk, hh)
                dom = _pick_head(doblk, hh)
                s = _fox_scores(qm, kv, cc_ref[hh, rows, :], cr_ref[hh, :, 0:kh], i)
                p = jnp.exp(s - lse_ref[hh, rows, :])
                dsum = jnp.sum(dom.astype(F32) * ov, axis=-1, keepdims=True)
                dp = lax.dot_general(dom, vv, _NT, preferred_element_type=F32)
                ds = p * (dp - dsum)
                dsc = ds.astype(BF16)
                dqs.append(lax.dot_general(dsc, kv, _NN, preferred_element_type=F32) * SCALE)
                dk_acc[0:kh, :] += lax.dot_general(dsc, qm, _TN, preferred_element_type=F32) * SCALE
                dv_acc[0:kh, :] += lax.dot_general(p.astype(BF16), dom, _TN, preferred_element_type=F32)
                dcq_ref[hh, rows, :] = jnp.sum(ds, axis=-1, keepdims=True)
                dck_ref[hh, :, 0:kh] -= jnp.sum(ds, axis=0, keepdims=True)
            dx_ref[rows, 0:128] = jnp.where(_lane_lo(), dqs[0], dqs[1]).astype(BF16)
        dx_ref[:, 128:256] = dk_acc[...].astype(BF16)
        dx_ref[:, 256:384] = dv_acc[...].astype(BF16)

    pair, half, colv, rowv = _fox_specs(l)
    return pl.pallas_call(
        body, name=name, grid=(b, 4), in_specs=[pair, colv, rowv, half, colv, half],
        out_specs=[pair, colv, rowv],
        out_shape=[jax.ShapeDtypeStruct(qkv.shape, BF16), jax.ShapeDtypeStruct((b, B_HEADS, l, 1), F32),
                   jax.ShapeDtypeStruct((b, B_HEADS, 1, l), F32)],
        scratch_shapes=[pltpu.VMEM((l, 128), F32), pltpu.VMEM((l, 128), F32)],
        compiler_params=_params(("parallel", "parallel")),
    )(qkv, c_col, c_row, o, lse, do)


_FLIPS = ((0, 0, 1), (0, 1, 0), (0, 1, 1), (1, 0, 0), (1, 0, 1), (1, 1, 0), (1, 1, 1))


def _exchange(name, gather, scatter):
    ng, ns = len(gather), len(scatter)
    na = ng + ns
    npeer = len(_FLIPS)

    def body(*refs):
        ins = refs[:na]
        outs = refs[na:2 * na]
        send_sems, recv_sems, loc_sems = refs[2 * na:]
        x, y, c = lax.axis_index("x"), lax.axis_index("y"), lax.axis_index("c")
        me = 4 * x + 2 * y + c
        peers = []
        for fx, fy, fc in _FLIPS:
            px = 1 - x if fx else x
            py = 1 - y if fy else y
            pc = 1 - c if fc else c
            peers.append(((px, py, pc), 4 * px + 2 * py + pc))

        def remote(a, kk):
            dev, lin = peers[kk]
            src = ins[a] if a < ng else ins[a].at[lin]
            return pltpu.make_async_remote_copy(src_ref=src, dst_ref=outs[a].at[me], send_sem=send_sems.at[a * npeer + kk],
                                                recv_sem=recv_sems.at[a * npeer + kk], device_id=dev, device_id_type=MESH_ID)

        def arrival(a, kk):
            dev, lin = peers[kk]
            src = ins[a] if a < ng else ins[a].at[lin]
            return pltpu.make_async_remote_copy(src_ref=src, dst_ref=outs[a].at[lin], send_sem=send_sems.at[a * npeer + kk],
                                                recv_sem=recv_sems.at[a * npeer + kk], device_id=dev, device_id_type=MESH_ID)

        local = []
        for a in range(na):
            src = ins[a] if a < ng else ins[a].at[me]
            cp = pltpu.make_async_copy(src, outs[a].at[me], loc_sems.at[a])
            cp.start()
            local.append(cp)
        sent = [remote(a, kk) for kk in range(npeer) for a in range(na)]
        for cp in sent:
            cp.start()
        for kk in range(npeer):
            for a in range(na):
                arrival(a, kk).wait_recv()
        for cp in sent:
            cp.wait_send()
        for cp in local:
            cp.wait()

    arrs = list(gather) + list(scatter)
    out_shape = [jax.ShapeDtypeStruct((N_DEV,) + tuple(a.shape), a.dtype) for a in gather]
    out_shape += [jax.ShapeDtypeStruct(tuple(a.shape), a.dtype) for a in scatter]
    anyspec = pl.BlockSpec(memory_space=pl.ANY)
    return pl.pallas_call(
        body, name=name, in_specs=[anyspec] * na, out_specs=[anyspec] * na, out_shape=out_shape,
        scratch_shapes=[pltpu.SemaphoreType.DMA((na * npeer,)), pltpu.SemaphoreType.DMA((na * npeer,)),
                        pltpu.SemaphoreType.DMA((na,))],
        compiler_params=pltpu.CompilerParams(has_side_effects=True),
    )(*arrs)


def _peer_table():
    x, y, c = lax.axis_index("x"), lax.axis_index("y"), lax.axis_index("c")
    me = 4 * x + 2 * y + c
    peers = []
    for fx, fy, fc in _FLIPS:
        px = 1 - x if fx else x
        py = 1 - y if fy else y
        pc = 1 - c if fc else c
        peers.append(((px, py, pc), 4 * px + 2 * py + pc))
    return me, peers


_HBM = pl.BlockSpec(memory_space=pltpu.HBM)
_SEM = pl.BlockSpec(memory_space=pltpu.SEMAPHORE)
_ANY = pl.BlockSpec(memory_space=pl.ANY)
_EFFECT = pltpu.SideEffectType.DATAFLOW_SIDE_EFFECTING


def _split_copy(srcs_are_pieces, src_refs, land_refs, send_sem, recv_sem, a, kk, me, peers, arriving):
    dev, lin = peers[kk]
    npeer = len(_FLIPS)
    src = src_refs[a] if srcs_are_pieces[a] else src_refs[a].at[lin]
    dst = land_refs[a].at[lin] if arriving else land_refs[a].at[me]
    return pltpu.make_async_remote_copy(src_ref=src, dst_ref=dst, send_sem=send_sem.at[a * npeer + kk],
                                        recv_sem=recv_sem.at[a * npeer + kk], device_id=dev, device_id_type=MESH_ID)


def _xchg_start(name, gather, scatter, after=None):
    me_out = 4 * lax.axis_index("x") + 2 * lax.axis_index("y") + lax.axis_index("c")
    srcs = list(gather) + list(scatter)
    is_piece = [True] * len(gather) + [False] * len(scatter)
    lands = []
    for a, piece in zip(srcs, is_piece):
        own = a[None] if piece else lax.dynamic_slice_in_dim(a, me_out, 1, axis=0)
        shape = ((N_DEV,) + tuple(a.shape)) if piece else tuple(a.shape)
        start = (me_out,) + (0,) * (len(shape) - 1)
        lands.append(lax.dynamic_update_slice(jnp.zeros(shape, a.dtype), own, start))
    n = len(srcs)
    nsem = n * len(_FLIPS)
    has_after = after is not None

    def body(*refs):
        src_refs = refs[:n]
        land_refs = refs[n:2 * n]
        outs = refs[2 * n + (1 if has_after else 0):]
        send_sem, recv_sem = outs[0], outs[1]
        token = outs[-1]
        me, peers = _peer_table()
        for kk in range(len(_FLIPS)):
            for a in range(n):
                _split_copy(is_piece, src_refs, land_refs, send_sem, recv_sem, a, kk, me, peers, False).start()
        token[...] = jnp.zeros_like(token)

    out_shape = ([pltpu.SemaphoreType.DMA((nsem,)), pltpu.SemaphoreType.DMA((nsem,))]
                 + [pltpu.HBM(tuple(a.shape), a.dtype) for a in srcs] + [pltpu.HBM(tuple(a.shape), a.dtype) for a in lands]
                 + [jax.ShapeDtypeStruct((8, 128), F32)])
    args = [pltpu.with_memory_space_constraint(a, pltpu.HBM) for a in srcs + lands] + ([after] if has_after else [])
    res = pl.pallas_call(
        body, name=name, out_shape=out_shape,
        in_specs=[_HBM] * (2 * n) + ([_ANY] if has_after else []),
        out_specs=[_SEM, _SEM] + [_HBM] * (2 * n) + [pl.BlockSpec(memory_space=pltpu.VMEM)],
        input_output_aliases={i: 2 + i for i in range(2 * n)},
        compiler_params=pltpu.CompilerParams(has_side_effects=_EFFECT),
    )(*args)
    state = (res[0], res[1], list(res[2:2 + n]), list(res[2 + n:2 + 2 * n]), is_piece)
    return state, res[-1]


def _xchg_wait(name, state, after):
    send_sem, recv_sem, srcs, lands, is_piece = state
    n = len(srcs)

    def body(*refs):
        src_refs = refs[:n]
        land_refs = refs[n:2 * n]
        s_sem, r_sem = refs[2 * n], refs[2 * n + 1]
        me, peers = _peer_table()
        for kk in range(len(_FLIPS)):
            for a in range(n):
                cp = _split_copy(is_piece, src_refs, land_refs, s_sem, r_sem, a, kk, me, peers, True)
                cp.wait_send()
                cp.wait_recv()

    out_shape = [pltpu.HBM(tuple(a.shape), a.dtype) for a in srcs] + [pltpu.HBM(tuple(a.shape), a.dtype) for a in lands]
    res = pl.pallas_call(
        body, name=name, out_shape=out_shape,
        in_specs=[_HBM] * (2 * n) + [_SEM, _SEM, _ANY], out_specs=[_HBM] * (2 * n),
        input_output_aliases={i: i for i in range(2 * n)},
        compiler_params=pltpu.CompilerParams(has_side_effects=_EFFECT),
    )(*srcs, *lands, send_sem, recv_sem, after)
    return list(res[n:])


def _adam_math(w, g, m, v):
    m = ADAM_B1 * m + (1.0 - ADAM_B1) * g
    v = ADAM_B2 * v + (1.0 - ADAM_B2) * (g * g)
    m_hat = m / (1.0 - ADAM_B1 ** ADAM_STEP)
    v_hat = v / (1.0 - ADAM_B2 ** ADAM_STEP)
    delta = -ADAM_LR * (m_hat / (jnp.sqrt(v_hat) + ADAM_EPS) + ADAM_WD * w)
    return delta, m, v


def _adam(name, w, m, v, parts):
    r, c = w.shape
    npart, _, cp = parts.shape
    tr = _pick(r, (256, 176, 128, 64, 16, 8, 1))

    def body(w_ref, m_ref, v_ref, p_ref, g_ref, d_ref, mo_ref, vo_ref):
        g = p_ref[0].astype(F32)
        for pp in range(1, npart):
            g = g + p_ref[pp].astype(F32)
        g = g[:, 0:c]
        delta, mn, vn = _adam_math(w_ref[...], g, m_ref[...], v_ref[...])
        g_ref[...] = g
        d_ref[...] = delta
        mo_ref[...] = mn
        vo_ref[...] = vn

    blk = pl.BlockSpec((tr, c), lambda i: (i, 0))
    out = jax.ShapeDtypeStruct((r, c), F32)
    return pl.pallas_call(
        body, name=name, grid=(r // tr,),
        in_specs=[blk, blk, blk, pl.BlockSpec((npart, tr, cp), lambda i: (0, i, 0))],
        out_specs=[blk, blk, blk, blk], out_shape=[out, out, out, out], compiler_params=_params(("parallel",)),
    )(w, m, v, parts)


def _small_sum(name, packs):
    def body(p_ref, o_ref):
        tot = p_ref[0]
        for pp in range(1, N_DEV):
            tot = tot + p_ref[pp]
        o_ref[0:8, :] = tot[0:8, :]
        o_ref[8:24, :] = tot[8:24, :] + tot[24:40, :]

    return pl.pallas_call(body, name=name, out_shape=jax.ShapeDtypeStruct((24, D_MODEL), F32),
                          compiler_params=_params())(packs)


def _local_step(x, tgt, g1, gm, g2, gf, b_forget, sinks, weights, send):
    b, s, _ = x.shape
    l = s + PREFIX
    t = b * l
    w1i, meta = weights("ffn1_in", x)
    h0 = jnp.concatenate([jnp.zeros((b, N_PAD, D_MODEL), F32), jnp.broadcast_to(meta[None], (b, N_META, D_MODEL)), x],
                         axis=1).reshape(t, D_MODEL)

    n1 = _rms_fwd("rms1_fwd", h0, g1)
    gu1, a1 = _ffn_in_fwd("ffn1_in_fwd", n1, w1i)
    (w1o,) = weights("ffn1_out", a1)
    h1 = _mm_nn("ffn1_out_fwd", a1, w1o, alpha=0.5, res=h0)
    wi, wa, wb, wo = weights("mix", h1)
    um = _rms_fwd("rmsm_fwd", h1, gm)
    qkv = _mm_nn("proj_qkv_fwd", um, wi, out_dtype=BF16, tn_c=(768,), cols=(0, QKV_W))
    gates = _mm_nn("proj_gates_fwd", um, wi, tn_c=(1024,), cols=(P_GATES, 2 * D_MODEL))
    f2 = _mm_nn("proj_f_fwd", um, wi, tn_c=(128,), cols=(P_F, 128))
    qkv3 = qkv.reshape(b, l, QKV_W)
    f3 = f2.reshape(b, l, 128)
    bf_row = jnp.pad(b_forget, ((0, 0), (0, 128 - B_HEADS)))
    c_col, c_row = _fgate_fwd("fgate_fwd", f3, bf_row)
    head_of_row = jnp.arange(STACK) // BLOCK
    slopes = jnp.exp2(-8.0 * (head_of_row + 1).astype(F32) / A_HEADS).reshape(STACK, 1)
    sink_rows = jnp.repeat(sinks.reshape(A_HEADS), BLOCK).reshape(STACK, 1)
    oa3, lse_a = _swa_fwd("swa_fwd", qkv3, slopes, sink_rows)
    ob3, lse_b = _fox_fwd("fox_fwd", qkv3, c_col, c_row)
    oa = oa3.reshape(t, A_WIDTH)
    ob = ob3.reshape(t, B_WIDTH)
    ya = _mm_nn("branch_a_fwd", oa, wa)
    yb = _mm_nn("branch_b_fwd", ob, wb)
    mixed = _gate_fwd("gate_fwd", gates, ya, yb)
    h2 = _mm_nn("mix_out_fwd", mixed, wo, res=h1)
    w2i, w2o = weights("ffn2", h2)
    n2 = _rms_fwd("rms2_fwd", h2, g2)
    gu2, a2 = _ffn_in_fwd("ffn2_in_fwd", n2, w2i)
    h3 = _mm_nn("ffn2_out_fwd", a2, w2o, alpha=0.5, res=h2)

    dh3_3, loss_blk, dgf = _loss_head("loss_head", h3.reshape(b, l, D_MODEL), gf, tgt)
    dh3 = dh3_3.reshape(t, D_MODEL)

    def ffn_bwd(tag, dh, h_in, g_norm, n_in, gu, a, w_in_blk, w_out):
        dw_out = _mm_tn(tag + "_out_bwd_w", a, dh, alpha=0.5)
        dgu = _ffn_out_bwd_x(tag + "_out_bwd_x", dh, w_out, gu, dep=send(tag + "_out", (dw_out,)))
        dw_in = _ffn_in_bwd_w(tag + "_in_bwd_w", n_in, dgu)
        dn = _ffn_in_bwd_x(tag + "_in_bwd_x", dgu, w_in_blk, dep=send(tag + "_in", (dw_in,)))
        dh_in, dg = _rms_bwd("rms" + tag[-1] + "_bwd", h_in, g_norm, dn, dh)
        return dh_in, dg

    dh2, dg2 = ffn_bwd("ffn2", dh3, h2, g2, n2, gu2, a2, w2i, w2o)

    dmix = _mm_nt("mix_out_bwd_x", dh2, wo, tn_c=(512,))
    dwo = _mm_tn("mix_out_bwd_w", mixed, dh2)
    dya, dyb, dgates = _gate_bwd("gate_bwd", gates, ya, yb, dmix)
    doa = _mm_nt("branch_a_bwd_x", dya, wa, tn_c=(512,))
    dob = _mm_nt("branch_b_bwd_x", dyb, wb, out_dtype=BF16, tn_c=(512,))
    dwa = _mm_tn("branch_a_bwd_w", oa, dya, tm_c=(512,))
    dwb = _mm_tn("branch_b_bwd_w", ob, dyb, tm_c=(512,))
    dqkv3, dcq, dck = _fox_bwd("fox_bwd", qkv3, c_col, c_row, ob3, lse_b, dob.reshape(b, l, B_WIDTH))
    dqkv3, dsink = _swa_bwd("swa_bwd", qkv3, oa3, lse_a, doa.reshape(b, l, A_WIDTH), slopes, sink_rows, dqkv3)
    dqkv = dqkv3.reshape(t, QKV_W)
    df3, dbf = _fgate_bwd("fgate_bwd", f3, bf_row, dcq, dck)
    df = df3.reshape(t, 128)
    dwi_qkv = _mm_tn("proj_qkv_bwd_w", um, dqkv, tm_c=(512,), tn_c=(768,))
    dwi_g = _mm_tn("proj_gates_bwd_w", um, dgates, tm_c=(512,), tn_c=(512,))
    dwi_f = _mm_tn("proj_f_bwd_w", um, df, tm_c=(512,), tn_c=(128,))
    token = send("mix", (dwi_qkv, dwi_g, dwi_f, dwa, dwb, dwo))
    dum = _mm_nt("proj_gates_bwd_x", dgates, wi, tn_c=(512,), tk_c=(2 * D_MODEL,), kcols=(P_GATES, 2 * D_MODEL), dep=token)
    dum = _mm_nt("proj_qkv_bwd_x", dqkv, wi, tn_c=(512,), tk_c=(QKV_W,), kcols=(0, QKV_W), res=dum)
    dum = _mm_nt("proj_f_bwd_x", df, wi, tn_c=(512,), tk_c=(128,), kcols=(P_F, 128), res=dum)
    dh1, dgm = _rms_bwd("rmsm_bwd", h1, gm, dum, dh2)

    dh0, dg1 = ffn_bwd("ffn1", dh1, h0, g1, n1, gu1, a1, w1i, w1o)
    dh0_3 = dh0.reshape(b, l, D_MODEL)
    grad_x = dh0_3[:, PREFIX:, :]
    dmeta = dh0_3[:, N_PAD:PREFIX, :].reshape(b * N_META, D_MODEL)

    misc = jnp.concatenate([dbf[:, 0:B_HEADS], dsink[:, 0].reshape(1, A_HEADS), loss_blk[0:1, 0:1]], axis=1)
    misc = jnp.pad(misc, ((0, 0), (0, D_MODEL - misc.shape[1])))
    row = lax.broadcasted_iota(jnp.int32, (8, D_MODEL), 0)
    vec = jnp.zeros((8, D_MODEL), F32)
    for i, piece in enumerate((dg1, dgm, dg2, dgf, misc)):
        vec = jnp.where(row == i, piece, vec)
    small = jnp.concatenate([vec, dmeta], axis=0)
    return grad_x, small


def _pad_to(a, rows, cols):
    return jnp.pad(a, ((0, rows - a.shape[0]), (0, cols - a.shape[1])))


def _ffn_out_from_gathered(g):
    w = g.reshape(4, FF_SHARD, D_MODEL)
    return jnp.pad(w, ((0, 0), (0, FF_SHARD_P - FF_SHARD), (0, 0))).reshape(D_FF_P, D_MODEL)


def _ffn_out_to_scatter(dw):
    return dw.reshape(4, FF_SHARD_P, D_MODEL)[:, 0:FF_SHARD, :].reshape(N_DEV, FFO_SHARD, D_MODEL)


def _proj_segments():
    segs = [(HEAD_DIM * h, HEAD_DIM, 0, B_SEG + HEAD_DIM * A_HEAD_ORDER.index(h)) for h in range(A_HEADS)]
    segs += [(512, 128, 0, B_SEG + A_WIDTH), (640, 128, 0, B_SEG + A_WIDTH + 128)]
    for first, off in ((768, 0), (1280, 128), (1792, 256)):
        segs += [(first + 128 * hp, 128, 0, PAIR_W * hp + off) for hp in range(4)]
    segs += [(2304, B_HEADS, 2, 0), (2312, 2 * D_MODEL, 1, 0)]
    return segs


def _proj_from_gathered(g):
    def cols(first, width):
        out = []
        for p in range(N_DEV):
            lo, hi = max(first, WIN_SHARD * p), min(first + width, WIN_SHARD * (p + 1))
            if lo < hi:
                out.append(g[p, :, lo - WIN_SHARD * p:hi - WIN_SHARD * p])
        return out

    parts = []
    for arr in (0, 1, 2):
        for first, width, _, _ in sorted((s for s in _proj_segments() if s[2] == arr), key=lambda s: s[3]):
            parts += cols(first, width)
        if arr == 0:
            parts.append(jnp.zeros((D_MODEL, P_GATES - QKV_W), g.dtype))
    parts.append(jnp.zeros((D_MODEL, 128 - B_HEADS), g.dtype))
    return jnp.concatenate(parts, axis=1)


def _proj_to_scatter(dqkv_w, dg_w, df_w):
    arrays = (dqkv_w, dg_w, df_w)
    segs = sorted(_proj_segments())
    blocks = []
    for p in range(N_DEV):
        parts = []
        for first, width, arr, at in segs:
            lo, hi = max(first, WIN_SHARD * p), min(first + width, WIN_SHARD * (p + 1))
            if lo < hi:
                parts.append(arrays[arr][:, at + lo - first:at + hi - first])
        parts.append(jnp.zeros((D_MODEL, WIN_SHARD_P - WIN_SHARD), dqkv_w.dtype))
        blocks.append(jnp.concatenate(parts, axis=1))
    return jnp.stack(blocks, axis=0)


def _a_rows_from_natural(w):
    return jnp.concatenate([w[HEAD_DIM * h:HEAD_DIM * (h + 1)] for h in A_HEAD_ORDER], axis=0)


def _a_rows_to_natural(w):
    return jnp.concatenate([w[HEAD_DIM * A_HEAD_ORDER.index(h):HEAD_DIM * (A_HEAD_ORDER.index(h) + 1)]
                            for h in range(A_HEADS)], axis=0)


def kernel(x, meta_tokens, ffn1_norm, ffn1_w_in, ffn1_w_out, mix_norm, w_in, b_forget, attn_sinks, w_branch_a, w_branch_b, w_out, ffn2_norm, ffn2_w_in, ffn2_w_out, final_norm, loss_target, m_meta_tokens, m_ffn1_norm, m_ffn1_w_in, m_ffn1_w_out, m_mix_norm, m_w_in, m_b_forget, m_attn_sinks, m_w_branch_a, m_w_branch_b, m_w_out, m_ffn2_norm, m_ffn2_w_in, m_ffn2_w_out, m_final_norm, v_meta_tokens, v_ffn1_norm, v_ffn1_w_in, v_ffn1_w_out, v_mix_norm, v_w_in, v_b_forget, v_attn_sinks, v_w_branch_a, v_w_branch_b, v_w_out, v_ffn2_norm, v_ffn2_w_in, v_ffn2_w_out, v_final_norm):
    me = 4 * lax.axis_index("x") + 2 * lax.axis_index("y") + lax.axis_index("c")

    shards = (
        _pad_to(ffn1_w_in[0].astype(BF16), D_MODEL, FF_SHARD_P),
        ffn1_w_out[0].astype(BF16),
        _pad_to(w_in[0].astype(BF16), D_MODEL, WIN_SHARD_P),
        w_branch_a[0].astype(BF16), w_branch_b[0].astype(BF16), w_out[0].astype(BF16),
        _pad_to(ffn2_w_in[0].astype(BF16), D_MODEL, FF_SHARD_P),
        ffn2_w_out[0].astype(BF16),
        meta_tokens,
    )
    s1i, s1o, swi, swa, swb, swo, s2i, s2o, smeta = shards
    gather_state = {}
    gather_state["ffn1_in"], tok = _xchg_start("gather_ffn1_in_start", (s1i, smeta), ())
    gather_state["ffn1_out"], tok = _xchg_start("gather_ffn1_out_start", (s1o,), (), after=tok)
    gather_state["mix"], tok = _xchg_start("gather_mix_start", (swi, swa, swb, swo), (), after=tok)
    gather_state["ffn2"], tok = _xchg_start("gather_ffn2_start", (s2i, s2o), (), after=tok)
    started = {"tok": tok}

    def weights(group, after):
        if group == "ffn1_in":
            after = started["tok"]
        got = _xchg_wait("gather_" + group + "_wait", gather_state[group], after)
        if group == "mix":
            gwi, gwa, gwb, gwo = got
            return (_proj_from_gathered(gwi), _a_rows_from_natural(gwa.transpose(1, 0, 2).reshape(A_WIDTH, D_MODEL)),
                    gwb.transpose(1, 0, 2).reshape(B_WIDTH, D_MODEL), gwo.reshape(D_MODEL, D_MODEL))
        if group == "ffn1_in":
            return got[0].reshape(2, 4, D_MODEL, FF_SHARD_P), got[1].transpose(1, 0, 2).reshape(N_META, D_MODEL)
        if group == "ffn1_out":
            return (_ffn_out_from_gathered(got[0]),)
        return got[0].reshape(2, 4, D_MODEL, FF_SHARD_P), _ffn_out_from_gathered(got[1])

    scatter_state = {}

    def send(group, grads):
        if group == "mix":
            dwi_qkv, dwi_g, dwi_f, dwa, dwb, dwo = grads
            dwa = _a_rows_to_natural(dwa)
            blocks = (_proj_to_scatter(dwi_qkv, dwi_g, dwi_f), dwa.reshape(A_WIDTH, N_DEV, 128).transpose(1, 0, 2),
                      dwb.reshape(B_WIDTH, N_DEV, 128).transpose(1, 0, 2), dwo.reshape(N_DEV, 128, D_MODEL))
        elif group.endswith("_in"):
            blocks = (grads[0].reshape(N_DEV, D_MODEL, FF_SHARD_P),)
        else:
            blocks = (_ffn_out_to_scatter(grads[0]),)
        scatter_state[group], token = _xchg_start("scatter_" + group + "_start", (), blocks)
        return token

    gf = final_norm.reshape(1, D_MODEL)
    grad_x, small = _local_step(x, loss_target, ffn1_norm, mix_norm, ffn2_norm, gf, b_forget, attn_sinks, weights, send)

    (packs,) = _exchange("gather_small", (small,), ())
    recv = {}
    after = packs
    for group in ("ffn2_out", "ffn2_in", "mix", "ffn1_out", "ffn1_in"):
        recv[group] = _xchg_wait("scatter_" + group + "_wait", scatter_state[group], after)
        after = recv[group][0]
    (r2o,), (r2i,), (rwi, rwa, rwb, rwo), (r1o,), (r1i,) = (recv[g] for g in ("ffn2_out", "ffn2_in", "mix", "ffn1_out", "ffn1_in"))

    out = {}
    for nm, w, m, v, parts in (
        ("ffn1_w_in", ffn1_w_in, m_ffn1_w_in, v_ffn1_w_in, r1i), ("ffn1_w_out", ffn1_w_out, m_ffn1_w_out, v_ffn1_w_out, r1o),
        ("w_in", w_in, m_w_in, v_w_in, rwi), ("w_branch_a", w_branch_a, m_w_branch_a, v_w_branch_a, rwa),
        ("w_branch_b", w_branch_b, m_w_branch_b, v_w_branch_b, rwb), ("w_out", w_out, m_w_out, v_w_out, rwo),
        ("ffn2_w_in", ffn2_w_in, m_ffn2_w_in, v_ffn2_w_in, r2i), ("ffn2_w_out", ffn2_w_out, m_ffn2_w_out, v_ffn2_w_out, r2o),
    ):
        res4 = _adam("adam_" + nm, w[0], m[0], v[0], parts)
        out[nm] = tuple(r[None] for r in res4)

    tot = _small_sum("small_sum", packs)
    loss = tot[4, 2 * B_HEADS]
    g_meta = lax.dynamic_slice(tot[8:24, :], (0, me * 128), (N_META, 128))
    out["meta_tokens"] = tuple(_adam("adam_meta_tokens", meta_tokens, m_meta_tokens, v_meta_tokens, g_meta[None]))

    def pack_small(n1, nm, n2, nf, bfv, skv):
        misc = jnp.pad(jnp.concatenate([bfv, skv], axis=1), ((0, 0), (0, D_MODEL - 2 * B_HEADS)))
        row = lax.broadcasted_iota(jnp.int32, (8, D_MODEL), 0)
        vec = jnp.zeros((8, D_MODEL), F32)
        for i, piece in enumerate((n1, nm, n2, nf.reshape(1, D_MODEL), misc)):
            vec = jnp.where(row == i, piece, vec)
        return vec

    w_pack = pack_small(ffn1_norm, mix_norm, ffn2_norm, final_norm, b_forget, attn_sinks)
    m_pack = pack_small(m_ffn1_norm, m_mix_norm, m_ffn2_norm, m_final_norm, m_b_forget, m_attn_sinks)
    v_pack = pack_small(v_ffn1_norm, v_mix_norm, v_ffn2_norm, v_final_norm, v_b_forget, v_attn_sinks)
    small4 = _adam("adam_small", w_pack, m_pack, v_pack, tot[0:8][None])
    for i, nm in enumerate(("ffn1_norm", "mix_norm", "ffn2_norm")):
        out[nm] = tuple(r[i:i + 1] for r in small4)
    out["final_norm"] = tuple(r[3] for r in small4)
    out["b_forget"] = tuple(r[4:5, 0:B_HEADS] for r in small4)
    out["attn_sinks"] = tuple(r[4:5, B_HEADS:2 * B_HEADS] for r in small4)

    names = ("meta_tokens", "ffn1_norm", "ffn1_w_in", "ffn1_w_out", "mix_norm", "w_in", "b_forget", "attn_sinks",
             "w_branch_a", "w_branch_b", "w_out", "ffn2_norm", "ffn2_w_in", "ffn2_w_out", "final_norm")
    return (loss, grad_x) + tuple(out[nm][kind] for kind in range(4) for nm in names)
```

```python
import jax
import jax.numpy as jnp
from jax import lax
from jax.experimental import pallas as pl
from jax.experimental.pallas import tpu as pltpu

F32 = jnp.float32
BF16 = jnp.bfloat16

D_MODEL = 1024
N_META = 16
BLOCK = 128
PREFIX = 128
N_PAD = PREFIX - N_META
HEAD_DIM = 64
A_HEADS = 8
A_KV_HEADS = 2
A_GROUP = 4
B_HEADS = 8
A_WIDTH = 512
A_KV_WIDTH = 128
B_WIDTH = 512
D_FF = 2816
N_DEV = 8
FF_SHARD = 2 * D_FF // N_DEV
FF_SHARD_P = 768
FFO_SHARD = D_FF // N_DEV
FFO_SHARD_P = FF_SHARD_P // 2
D_FF_P = 4 * FF_SHARD_P
W_IN_COLS = 4360
WIN_SHARD = W_IN_COLS // N_DEV
WIN_SHARD_P = 640
PAIR_W = 3 * 128
B_SEG = 4 * PAIR_W
A_SEG = A_WIDTH + 2 * A_KV_WIDTH
QKV_W = B_SEG + A_SEG
P_GATES = 2 * (2 * D_MODEL)
P_F = P_GATES + 2 * D_MODEL
PROJ_P = P_F + 128
A_HEAD_ORDER = (0, 4, 1, 5, 2, 6, 3, 7)
EPS = 1e-6
NEG = -1e30
SCALE = HEAD_DIM ** -0.5
ADAM_LR = 0.001
ADAM_B1 = 0.9
ADAM_B2 = 0.999
ADAM_EPS = 1e-08
ADAM_WD = 0.01
ADAM_STEP = 10
VMEM_LIMIT = 56 * 1024 * 1024
MESH_ID = pl.DeviceIdType.MESH
SMALL_ROWS = 40

_NN = (((1,), (0,)), ((), ()))
_NT = (((1,), (1,)), ((), ()))
_TN = (((0,), (0,)), ((), ()))


def _params(sem=None):
    return pltpu.CompilerParams(dimension_semantics=sem, vmem_limit_bytes=VMEM_LIMIT)


def _pick(n, cands):
    for c in cands:
        if n % c == 0:
            return c
    raise ValueError(f"no tile for {n}")


def _bf(v):
    return v if v.dtype == BF16 else v.astype(BF16)


def _mm(name, a, b, dims, grid, a_spec, b_spec, o_spec, out_shape, out_dtype, acc_shape, k_axis=None, nk=1,
        alpha=1.0, res=None, res_spec=None, dep=None):
    has_res = res is not None
    has_dep = dep is not None

    def body(*refs):
        a_ref, b_ref = refs[0], refs[1]
        r_ref = refs[2] if has_res else None
        o_ref = refs[2 + has_res + has_dep]

        def finish(acc):
            if alpha != 1.0:
                acc = acc * alpha
            if has_res:
                acc = acc + r_ref[...]
            o_ref[...] = acc.astype(o_ref.dtype)

        part = lax.dot_general(_bf(a_ref[...]), _bf(b_ref[...]), dims, preferred_element_type=F32)
        if nk == 1:
            finish(part)
        else:
            acc_ref = refs[-1]
            k = pl.program_id(k_axis)

            @pl.when(k == 0)
            def _():
                acc_ref[...] = part

            @pl.when(k > 0)
            def _():
                acc_ref[...] += part

            @pl.when(k == nk - 1)
            def _():
                finish(acc_ref[...])

    in_specs = [a_spec, b_spec] + ([res_spec] if has_res else []) + ([pl.BlockSpec(memory_space=pl.ANY)] if has_dep else [])
    args = (a, b) + ((res,) if has_res else ()) + ((dep,) if has_dep else ())
    sem = tuple("arbitrary" if (nk > 1 and i == k_axis) else "parallel" for i in range(len(grid)))
    return pl.pallas_call(
        body, name=name, grid=grid, in_specs=in_specs, out_specs=o_spec,
        out_shape=jax.ShapeDtypeStruct(out_shape, out_dtype),
        scratch_shapes=[pltpu.VMEM(acc_shape, F32)] if nk > 1 else [],
        compiler_params=_params(sem),
    )(*args)


def _mm_nn(name, a, b, out_dtype=F32, alpha=1.0, res=None, tn_c=(512, 640, 256, 128), cols=None):
    t, k = a.shape
    c0, n = (0, b.shape[1]) if cols is None else cols
    tm = _pick(t, (1088, 768, 512, 256, 128))
    tn = _pick(n, tn_c)
    assert c0 % tn == 0
    jb = c0 // tn
    return _mm(name, a, b, _NN, (t // tm, n // tn),
               pl.BlockSpec((tm, k), lambda i, j: (i, 0)), pl.BlockSpec((k, tn), lambda i, j: (0, jb + j)),
               pl.BlockSpec((tm, tn), lambda i, j: (i, j)), (t, n), out_dtype, None,
               alpha=alpha, res=res, res_spec=pl.BlockSpec((tm, tn), lambda i, j: (i, j)))


def _mm_nt(name, a, b, out_dtype=F32, alpha=1.0, tn_c=(768, 512, 256, 128), tk_c=None, dep=None, res=None, kcols=None):
    t, k = a.shape
    n = b.shape[0]
    c0 = 0 if kcols is None else kcols[0]
    tm = _pick(t, (1088, 768, 512, 256, 128))
    tn = _pick(n, tn_c)
    tk = k if tk_c is None else _pick(k, tk_c)
    nk = k // tk
    assert c0 % tk == 0
    kb = c0 // tk
    return _mm(name, a, b, _NT, (t // tm, n // tn, nk),
               pl.BlockSpec((tm, tk), lambda i, j, kk: (i, kk)), pl.BlockSpec((tn, tk), lambda i, j, kk: (j, kb + kk)),
               pl.BlockSpec((tm, tn), lambda i, j, kk: (i, j)), (t, n), out_dtype, (tm, tn), k_axis=2, nk=nk, alpha=alpha,
               dep=dep, res=res, res_spec=pl.BlockSpec((tm, tn), lambda i, j, kk: (i, j)))


def _mm_tn(name, a, b, out_dtype=BF16, alpha=1.0, tm_c=(768, 512, 256, 128), tn_c=(512, 640, 256, 128)):
    t, m = a.shape
    n = b.shape[1]
    tm = _pick(m, tm_c)
    tn = _pick(n, tn_c)
    tk = t
    nk = 1
    return _mm(name, a, b, _TN, (m // tm, n // tn, nk),
               pl.BlockSpec((tk, tm), lambda i, j, kk: (kk, i)), pl.BlockSpec((tk, tn), lambda i, j, kk: (kk, j)),
               pl.BlockSpec((tm, tn), lambda i, j, kk: (i, j)), (m, n), out_dtype, (tm, tn), k_axis=2, nk=nk, alpha=alpha)


def _silu_grads(g, u, d):
    sg = jax.nn.sigmoid(g)
    return d * u * (sg * (1.0 + g * (1.0 - sg))), d * (g * sg)


def _ffn_in_fwd(name, n, wblk):
    t = n.shape[0]
    tm = _pick(t, (1088, 768, 512, 256, 128))

    half = tm // 2 if tm % 32 == 0 else tm

    def body(n_ref, w_ref, gu_ref, a_ref):
        for r0 in range(0, tm, half):
            rows = slice(r0, r0 + half)
            nv = n_ref[rows, :]
            g = lax.dot_general(nv, w_ref[0], _NN, preferred_element_type=F32)
            u = lax.dot_general(nv, w_ref[1], _NN, preferred_element_type=F32)
            gu_ref[0, rows, :] = g.astype(BF16)
            gu_ref[1, rows, :] = u.astype(BF16)
            a_ref[rows, :] = ((g * jax.nn.sigmoid(g)) * u).astype(BF16)

    return pl.pallas_call(
        body, name=name, grid=(t // tm, 4),
        in_specs=[pl.BlockSpec((tm, D_MODEL), lambda i, j: (i, 0)),
                  pl.BlockSpec((2, None, D_MODEL, FF_SHARD_P), lambda i, j: (0, j, 0, 0))],
        out_specs=[pl.BlockSpec((2, tm, FF_SHARD_P), lambda i, j: (0, i, j)),
                   pl.BlockSpec((tm, FF_SHARD_P), lambda i, j: (i, j))],
        out_shape=[jax.ShapeDtypeStruct((2, t, D_FF_P), BF16), jax.ShapeDtypeStruct((t, D_FF_P), BF16)],
        compiler_params=_params(("parallel", "parallel")),
    )(n, wblk)


def _ffn_out_bwd_x(name, dh, w_out, gu, dep=None):
    t = dh.shape[0]
    tm = _pick(t, (1088, 768, 512, 256, 128))
    has_dep = dep is not None

    half = tm // 2 if tm % 32 == 0 else tm

    def body(dh_ref, w_ref, gu_ref, *rest):
        o_ref = rest[-1]
        for r0 in range(0, tm, half):
            rows = slice(r0, r0 + half)
            da = lax.dot_general(_bf(dh_ref[rows, :]), w_ref[...], _NT, preferred_element_type=F32) * 0.5
            dg, du = _silu_grads(gu_ref[0, rows, :].astype(F32), gu_ref[1, rows, :].astype(F32), da)
            o_ref[0, rows, :] = dg.astype(BF16)
            o_ref[1, rows, :] = du.astype(BF16)

    gu_spec = pl.BlockSpec((2, tm, FF_SHARD_P), lambda i, j: (0, i, j))
    return pl.pallas_call(
        body, name=name, grid=(t // tm, 4),
        in_specs=[pl.BlockSpec((tm, D_MODEL), lambda i, j: (i, 0)), pl.BlockSpec((FF_SHARD_P, D_MODEL), lambda i, j: (j, 0)),
                  gu_spec] + ([pl.BlockSpec(memory_space=pl.ANY)] if has_dep else []),
        out_specs=gu_spec, out_shape=jax.ShapeDtypeStruct((2, t, D_FF_P), BF16),
        compiler_params=_params(("parallel", "parallel")),
    )(*((dh, w_out, gu) + ((dep,) if has_dep else ())))


def _rms_bwd_rows(dn, h, g, dres):
    r = lax.rsqrt(jnp.mean(h * h, axis=-1, keepdims=True) + EPS)
    tv = dn * g
    dot = jnp.mean(tv * h, axis=-1, keepdims=True)
    return dres + (r * tv - h * (r * r * r * dot)), jnp.sum(dn * (h * r), axis=0, keepdims=True)


def _accumulate_rows(ref, part, first):
    @pl.when(first)
    def _():
        ref[...] = part

    @pl.when(jnp.logical_not(first))
    def _():
        ref[...] += part


def _ffn_in_bwd_x(name, dgu, wblk, h_in, g_norm, dres, dep=None):
    t = dgu.shape[1]
    tm = _pick(t, (272, 256, 128))
    has_dep = dep is not None

    def body(d_ref, w_ref, h_ref, g_ref, r_ref, *rest):
        dh_ref, dg_ref = rest[-2], rest[-1]
        acc = None
        for s in range(2):
            for j in range(4):
                part = lax.dot_general(d_ref[s, :, FF_SHARD_P * j:FF_SHARD_P * (j + 1)], w_ref[s, j], _NT,
                                       preferred_element_type=F32)
                acc = part if acc is None else acc + part
        dh, dg = _rms_bwd_rows(acc, h_ref[...], g_ref[...], r_ref[...])
        dh_ref[...] = dh
        _accumulate_rows(dg_ref, dg, pl.program_id(0) == 0)

    row = pl.BlockSpec((tm, D_MODEL), lambda i: (i, 0))
    vec = pl.BlockSpec((1, D_MODEL), lambda i: (0, 0))
    return pl.pallas_call(
        body, name=name, grid=(t // tm,),
        in_specs=[pl.BlockSpec((2, tm, D_FF_P), lambda i: (0, i, 0)),
                  pl.BlockSpec((2, 4, D_MODEL, FF_SHARD_P), lambda i: (0, 0, 0, 0)), row, vec, row]
        + ([pl.BlockSpec(memory_space=pl.ANY)] if has_dep else []),
        out_specs=[row, vec],
        out_shape=[jax.ShapeDtypeStruct((t, D_MODEL), F32), jax.ShapeDtypeStruct((1, D_MODEL), F32)],
        compiler_params=_params(("arbitrary",)),
    )(*((dgu, wblk, h_in, g_norm, dres) + ((dep,) if has_dep else ())))


def _proj_bwd_x(name, dqkv, dgates, df, wi, h_in, g_norm, dres, dep=None):
    t = dqkv.shape[0]
    tm = _pick(t, (272, 256, 128))
    has_dep = dep is not None

    def body(q_ref, gt_ref, f_ref, w_ref, h_ref, g_ref, r_ref, *rest):
        dh_ref, dg_ref = rest[-2], rest[-1]
        acc = lax.dot_general(q_ref[...], w_ref[:, 0:QKV_W], _NT, preferred_element_type=F32)
        acc = acc + lax.dot_general(gt_ref[...], w_ref[:, P_GATES:P_F], _NT, preferred_element_type=F32)
        acc = acc + lax.dot_general(f_ref[...], w_ref[:, P_F:PROJ_P], _NT, preferred_element_type=F32)
        dh, dg = _rms_bwd_rows(acc, h_ref[...], g_ref[...], r_ref[...])
        dh_ref[...] = dh
        _accumulate_rows(dg_ref, dg, pl.program_id(0) == 0)

    row = pl.BlockSpec((tm, D_MODEL), lambda i: (i, 0))
    vec = pl.BlockSpec((1, D_MODEL), lambda i: (0, 0))
    return pl.pallas_call(
        body, name=name, grid=(t // tm,),
        in_specs=[pl.BlockSpec((tm, QKV_W), lambda i: (i, 0)), pl.BlockSpec((tm, 2 * D_MODEL), lambda i: (i, 0)),
                  pl.BlockSpec((tm, 128), lambda i: (i, 0)), pl.BlockSpec((D_MODEL, PROJ_P), lambda i: (0, 0)), row, vec, row]
        + ([pl.BlockSpec(memory_space=pl.ANY)] if has_dep else []),
        out_specs=[row, vec],
        out_shape=[jax.ShapeDtypeStruct((t, D_MODEL), F32), jax.ShapeDtypeStruct((1, D_MODEL), F32)],
        compiler_params=_params(("arbitrary",)),
    )(*((dqkv, dgates, df, wi, h_in, g_norm, dres) + ((dep,) if has_dep else ())))


def _ffn_in_bwd_w(name, n, dgu):
    t = n.shape[0]
    tk = t
    nk = 1
    return _mm(name, n, dgu, _TN, (2, 4, nk),
               pl.BlockSpec((tk, D_MODEL), lambda s, j, kk: (kk, 0)),
               pl.BlockSpec((None, tk, FF_SHARD_P), lambda s, j, kk: (s, kk, j)),
               pl.BlockSpec((None, None, D_MODEL, FF_SHARD_P), lambda s, j, kk: (s, j, 0, 0)),
               (2, 4, D_MODEL, FF_SHARD_P), BF16, (D_MODEL, FF_SHARD_P), k_axis=2, nk=nk)


def _rms_fwd(name, h, g):
    t = h.shape[0]
    tm = _pick(t, (544, 384, 256, 128))

    def body(h_ref, g_ref, o_ref):
        hv = h_ref[...]
        r = lax.rsqrt(jnp.mean(hv * hv, axis=-1, keepdims=True) + EPS)
        o_ref[...] = ((hv * r) * g_ref[...]).astype(BF16)

    return pl.pallas_call(
        body, name=name, grid=(t // tm,),
        in_specs=[pl.BlockSpec((tm, D_MODEL), lambda i: (i, 0)), pl.BlockSpec((1, D_MODEL), lambda i: (0, 0))],
        out_specs=pl.BlockSpec((tm, D_MODEL), lambda i: (i, 0)),
        out_shape=jax.ShapeDtypeStruct((t, D_MODEL), BF16), compiler_params=_params(("parallel",)),
    )(h, g)


def _rms_bwd(name, h, g, dn, dres):
    t = h.shape[0]
    tm = _pick(t, (544, 384, 256, 128))

    def body(h_ref, g_ref, dn_ref, dres_ref, dh_ref, dg_ref):
        i = pl.program_id(0)
        hv = h_ref[...]
        dnv = dn_ref[...]
        r = lax.rsqrt(jnp.mean(hv * hv, axis=-1, keepdims=True) + EPS)
        tv = dnv * g_ref[...]
        dot = jnp.mean(tv * hv, axis=-1, keepdims=True)
        dh_ref[...] = dres_ref[...] + (r * tv - hv * (r * r * r * dot))
        part = jnp.sum(dnv * (hv * r), axis=0, keepdims=True)

        @pl.when(i == 0)
        def _():
            dg_ref[...] = part

        @pl.when(i > 0)
        def _():
            dg_ref[...] += part

    row = pl.BlockSpec((tm, D_MODEL), lambda i: (i, 0))
    vec = pl.BlockSpec((1, D_MODEL), lambda i: (0, 0))
    return pl.pallas_call(
        body, name=name, grid=(t // tm,), in_specs=[row, vec, row, row], out_specs=[row, vec],
        out_shape=[jax.ShapeDtypeStruct((t, D_MODEL), F32), jax.ShapeDtypeStruct((1, D_MODEL), F32)],
        compiler_params=_params(("arbitrary",)),
    )(h, g, dn, dres)


def _gate_fwd(name, gates, ya, yb):
    t = gates.shape[0]
    tm = _pick(t, (272, 256, 128))

    def body(g_ref, ya_ref, yb_ref, o_ref):
        sa = jax.nn.sigmoid(g_ref[:, 0:D_MODEL])
        sb = jax.nn.sigmoid(g_ref[:, D_MODEL:2 * D_MODEL])
        o_ref[...] = (sa * ya_ref[...] + sb * yb_ref[...]).astype(BF16)

    blk = pl.BlockSpec((tm, D_MODEL), lambda i: (i, 0))
    wide = pl.BlockSpec((tm, 2 * D_MODEL), lambda i: (i, 0))
    return pl.pallas_call(
        body, name=name, grid=(t // tm,), in_specs=[wide, blk, blk], out_specs=blk,
        out_shape=jax.ShapeDtypeStruct((t, D_MODEL), BF16), compiler_params=_params(("parallel",)),
    )(gates, ya, yb)


def _gate_bwd(name, gates, ya, yb, dmix):
    t = gates.shape[0]
    tm = _pick(t, (272, 256, 128))

    def body(g_ref, ya_ref, yb_ref, dm_ref, dya_ref, dyb_ref, dg_ref):
        dm = dm_ref[...]
        sa = jax.nn.sigmoid(g_ref[:, 0:D_MODEL])
        sb = jax.nn.sigmoid(g_ref[:, D_MODEL:2 * D_MODEL])
        dya_ref[...] = (dm * sa).astype(BF16)
        dyb_ref[...] = (dm * sb).astype(BF16)
        dg_ref[:, 0:D_MODEL] = (dm * ya_ref[...] * (sa * (1.0 - sa))).astype(BF16)
        dg_ref[:, D_MODEL:2 * D_MODEL] = (dm * yb_ref[...] * (sb * (1.0 - sb))).astype(BF16)

    blk = pl.BlockSpec((tm, D_MODEL), lambda i: (i, 0))
    wide = pl.BlockSpec((tm, 2 * D_MODEL), lambda i: (i, 0))
    out = jax.ShapeDtypeStruct((t, D_MODEL), BF16)
    return pl.pallas_call(
        body, name=name, grid=(t // tm,), in_specs=[wide, blk, blk, blk], out_specs=[blk, blk, wide],
        out_shape=[out, out, jax.ShapeDtypeStruct((t, 2 * D_MODEL), BF16)], compiler_params=_params(("parallel",)),
    )(gates, ya, yb, dmix)


def _loss_head(name, h3, gf, tgt):
    b, l, _ = h3.shape
    nb = l // BLOCK

    def body(h_ref, g_ref, t_ref, dh_ref, loss_ref, dg_ref):
        first = (pl.program_id(0) == 0) & (pl.program_id(1) == 0)
        real = (pl.program_id(1) > 0).astype(F32)
        hv = h_ref[...]
        g = g_ref[...]
        r = lax.rsqrt(jnp.mean(hv * hv, axis=-1, keepdims=True) + EPS)
        xn = hv * r
        err = (xn * g - t_ref[...]) * real
        lpart = 0.5 * jnp.sum(jnp.mean(err * err, axis=-1, keepdims=True), axis=0, keepdims=True)
        dy = err * (1.0 / D_MODEL)
        tv = dy * g
        dot = jnp.mean(tv * hv, axis=-1, keepdims=True)
        dh_ref[...] = r * tv - hv * (r * r * r * dot)
        gpart = jnp.sum(dy * xn, axis=0, keepdims=True)

        @pl.when(first)
        def _():
            loss_ref[...] = jnp.zeros_like(loss_ref)
            dg_ref[...] = jnp.zeros_like(dg_ref)

        loss_ref[...] += jnp.broadcast_to(lpart, loss_ref.shape)
        dg_ref[...] += gpart

    return pl.pallas_call(
        body, name=name, grid=(b, nb),
        in_specs=[pl.BlockSpec((None, BLOCK, D_MODEL), lambda bi, n: (bi, n, 0)),
                  pl.BlockSpec((1, D_MODEL), lambda bi, n: (0, 0)),
                  pl.BlockSpec((None, BLOCK, D_MODEL), lambda bi, n: (bi, jnp.maximum(n - 1, 0), 0))],
        out_specs=[pl.BlockSpec((None, BLOCK, D_MODEL), lambda bi, n: (bi, n, 0)),
                   pl.BlockSpec((8, 128), lambda bi, n: (0, 0)),
                   pl.BlockSpec((1, D_MODEL), lambda bi, n: (0, 0))],
        out_shape=[jax.ShapeDtypeStruct(h3.shape, F32), jax.ShapeDtypeStruct((8, 128), F32),
                   jax.ShapeDtypeStruct((1, D_MODEL), F32)],
        compiler_params=_params(("arbitrary", "arbitrary")),
    )(h3, gf, tgt)


def _fgate_fwd(name, f3, bf_row):
    b, l, _ = f3.shape
    nb = l // BLOCK

    def body(f_ref, b_ref, cc_ref, cr_ref):
        r_i = lax.broadcasted_iota(jnp.int32, (BLOCK, BLOCK), 0)
        c_i = lax.broadcasted_iota(jnp.int32, (BLOCK, BLOCK), 1)
        tri = (r_i >= c_i).astype(F32)
        carry = jnp.zeros((1, 128), F32)
        for blk in range(nb):
            rows = slice(blk * BLOCK, (blk + 1) * BLOCK)
            z = f_ref[rows, :] + b_ref[...]
            lf = jnp.minimum(z, 0.0) - jnp.log(1.0 + jnp.exp(-jnp.abs(z)))
            cb = jnp.dot(tri, lf, preferred_element_type=F32, precision=lax.Precision.HIGHEST) + carry
            carry = cb[BLOCK - 1:BLOCK, :]
            cbt = cb.T
            for hh in range(B_HEADS):
                cc_ref[hh, rows, :] = jnp.sum(jnp.where(c_i == hh, cb, 0.0), axis=1, keepdims=True)
                cr_ref[hh, :, rows] = cbt[hh:hh + 1, :]

    return pl.pallas_call(
        body, name=name, grid=(b,),
        in_specs=[pl.BlockSpec((None, l, 128), lambda bi: (bi, 0, 0)),
                  pl.BlockSpec((1, 128), lambda bi: (0, 0))],
        out_specs=[pl.BlockSpec((None, B_HEADS, l, 1), lambda bi: (bi, 0, 0, 0)),
                   pl.BlockSpec((None, B_HEADS, 1, l), lambda bi: (bi, 0, 0, 0))],
        out_shape=[jax.ShapeDtypeStruct((b, B_HEADS, l, 1), F32), jax.ShapeDtypeStruct((b, B_HEADS, 1, l), F32)],
        compiler_params=_params(("parallel",)),
    )(f3, bf_row)


def _fgate_bwd(name, f3, bf_row, dcq, dck):
    b, l, _ = f3.shape
    nb = l // BLOCK

    def body(f_ref, b_ref, dcq_ref, dck_ref, df_ref, db_ref):
        r_i = lax.broadcasted_iota(jnp.int32, (BLOCK, BLOCK), 0)
        c_i = lax.broadcasted_iota(jnp.int32, (BLOCK, BLOCK), 1)
        tri = (r_i <= c_i).astype(F32)
        carry = jnp.zeros((1, 128), F32)
        total = jnp.zeros((1, 128), F32)
        for blk in range(nb - 1, -1, -1):
            rows = slice(blk * BLOCK, (blk + 1) * BLOCK)
            krows = jnp.concatenate([dck_ref[hh, :, rows] for hh in range(B_HEADS)]
                                    + [jnp.zeros((BLOCK - B_HEADS, BLOCK), F32)], axis=0)
            dcb = krows.T
            for hh in range(B_HEADS):
                dcb = dcb + jnp.where(c_i == hh, dcq_ref[hh, rows, :], 0.0)
            rc = jnp.dot(tri, dcb, preferred_element_type=F32, precision=lax.Precision.HIGHEST) + carry
            carry = rc[0:1, :]
            z = f_ref[rows, :] + b_ref[...]
            df = rc * (1.0 / (1.0 + jnp.exp(z)))
            df_ref[rows, :] = df.astype(BF16)
            total = total + jnp.sum(df, axis=0, keepdims=True)

        @pl.when(pl.program_id(0) == 0)
        def _():
            db_ref[...] = total

        @pl.when(pl.program_id(0) > 0)
        def _():
            db_ref[...] += total

    return pl.pallas_call(
        body, name=name, grid=(b,),
        in_specs=[pl.BlockSpec((None, l, 128), lambda bi: (bi, 0, 0)),
                  pl.BlockSpec((1, 128), lambda bi: (0, 0)),
                  pl.BlockSpec((None, B_HEADS, l, 1), lambda bi: (bi, 0, 0, 0)),
                  pl.BlockSpec((None, B_HEADS, 1, l), lambda bi: (bi, 0, 0, 0))],
        out_specs=[pl.BlockSpec((None, l, 128), lambda bi: (bi, 0, 0)), pl.BlockSpec((1, 128), lambda bi: (0, 0))],
        out_shape=[jax.ShapeDtypeStruct((b, l, 128), BF16), jax.ShapeDtypeStruct((1, 128), F32)],
        compiler_params=_params(("arbitrary",)),
    )(f3, bf_row, dcq, dck)


A_Q_BLK = B_SEG // A_WIDTH
A_K_BLK = (B_SEG + A_WIDTH) // 128
A_V_BLK = A_K_BLK + 1
A_SEG_BLK = B_SEG // A_SEG
STACK = A_HEADS * BLOCK


def _lane_lo():
    return lax.broadcasted_iota(jnp.int32, (1, 128), 1) < HEAD_DIM


def _stack_heads(x, masked):
    lo = _lane_lo()
    blks = [x[:, 128 * j:128 * (j + 1)] for j in range(4)]
    if not masked:
        return jnp.concatenate(blks + blks, axis=0)
    zero = jnp.zeros_like(blks[0])
    return jnp.concatenate([jnp.where(lo, bk, zero) for bk in blks] + [jnp.where(lo, zero, bk) for bk in blks], axis=0)


def _unstack_heads(y):
    lo = _lane_lo()
    return jnp.concatenate([jnp.where(lo, y[128 * j:128 * (j + 1)], y[128 * (4 + j):128 * (5 + j)]) for j in range(4)], axis=1)


def _swa_scores(q, kcat, n, slope):
    rows = STACK
    r_i = lax.broadcasted_iota(jnp.int32, (rows, 3 * BLOCK), 0)
    c_i = lax.broadcasted_iota(jnp.int32, (rows, 3 * BLOCK), 1)
    qpos = n * BLOCK + (r_i & (BLOCK - 1))
    seg = c_i >> 7
    kpos = jnp.where(seg == 0, c_i, (n - 2) * BLOCK + c_i)
    dist = qpos - kpos
    s = lax.dot_general(q, kcat, _NT, preferred_element_type=F32)
    s = s - slope * dist.astype(F32)
    is_meta = seg == 0
    band = jnp.logical_not(is_meta) & (dist < BLOCK) & (kpos >= PREFIX)
    meta = is_meta & (c_i >= N_PAD)
    return jnp.where((dist >= 0) & (band | meta), s, NEG)


def _swa_specs():
    def kv(col_blk):
        return [pl.BlockSpec((None, BLOCK, 128), lambda b, n: (b, 0, col_blk)),
                pl.BlockSpec((None, BLOCK, 128), lambda b, n: (b, jnp.maximum(n - 1, 0), col_blk)),
                pl.BlockSpec((None, BLOCK, 128), lambda b, n: (b, n, col_blk))]

    q_spec = pl.BlockSpec((None, BLOCK, A_WIDTH), lambda b, n: (b, n, A_Q_BLK))
    o_spec = pl.BlockSpec((None, BLOCK, A_WIDTH), lambda b, n: (b, n, 0))
    col = pl.BlockSpec((STACK, 1), lambda b, n: (0, 0))
    lse_spec = pl.BlockSpec((None, A_HEADS, BLOCK, 1), lambda b, n: (b, 0, n, 0))
    return q_spec, kv(A_K_BLK), kv(A_V_BLK), o_spec, col, lse_spec


def _swa_fwd(name, qkv, slopes, sinks):
    b, l, _ = qkv.shape
    nb = l // BLOCK

    def body(q_ref, k0_ref, kp_ref, kc_ref, v0_ref, vp_ref, vc_ref, sl_ref, sk_ref, o_ref, lse_ref):
        n = pl.program_id(1)
        qs = _stack_heads(q_ref[...], True) * SCALE
        kcat = jnp.concatenate([k0_ref[...], kp_ref[...], kc_ref[...]], axis=0)
        vcat = jnp.concatenate([v0_ref[...], vp_ref[...], vc_ref[...]], axis=0)
        s = _swa_scores(qs, kcat, n, sl_ref[...])
        sink = sk_ref[...]
        m = jnp.maximum(jnp.max(s, axis=-1, keepdims=True), sink)
        p = jnp.exp(s - m)
        den = jnp.sum(p, axis=-1, keepdims=True) + jnp.exp(sink - m)
        o = lax.dot_general(p.astype(BF16), vcat, _NN, preferred_element_type=F32) / den
        o_ref[...] = _unstack_heads(o)
        lse_ref[...] = (m + jnp.log(den)).reshape(A_HEADS, BLOCK, 1)

    q_spec, k_specs, v_specs, o_spec, col, lse_spec = _swa_specs()
    return pl.pallas_call(
        body, name=name, grid=(b, nb),
        in_specs=[q_spec] + k_specs + v_specs + [col, col], out_specs=[o_spec, lse_spec],
        out_shape=[jax.ShapeDtypeStruct((b, l, A_WIDTH), F32), jax.ShapeDtypeStruct((b, A_HEADS, l, 1), F32)],
        compiler_params=_params(("parallel", "parallel")),
    )(qkv, qkv, qkv, qkv, qkv, qkv, qkv, slopes, sinks)


def _swa_bwd(name, qkv, o, lse, do, slopes, sinks, dqkv):
    b, l, _ = qkv.shape
    nb = l // BLOCK

    def body(q_ref, k0_ref, kp_ref, kc_ref, v0_ref, vp_ref, vc_ref, o_ref, lse_ref, do_ref, sl_ref, sk_ref, _,
             dx_ref, ds_ref, dk_acc, dv_acc):
        bi = pl.program_id(0)
        n = pl.program_id(1)
        qs = _stack_heads(q_ref[...], True) * SCALE
        dos32 = _stack_heads(do_ref[...], True)
        dos = dos32.astype(BF16)
        os_ = _stack_heads(o_ref[...], False)
        lsev = lse_ref[...].reshape(STACK, 1)
        kcat = jnp.concatenate([k0_ref[...], kp_ref[...], kc_ref[...]], axis=0)
        vcat = jnp.concatenate([v0_ref[...], vp_ref[...], vc_ref[...]], axis=0)
        s = _swa_scores(qs, kcat, n, sl_ref[...])
        p = jnp.exp(s - lsev)
        dsum = jnp.sum(dos32 * os_, axis=-1, keepdims=True)
        dp = lax.dot_general(dos, vcat, _NT, preferred_element_type=F32)
        dsc = (p * (dp - dsum)).astype(BF16)
        dq = lax.dot_general(dsc, kcat, _NN, preferred_element_type=F32) * SCALE
        row0 = pl.multiple_of(n * BLOCK, BLOCK)
        dx_ref[pl.ds(row0, BLOCK), 0:A_WIDTH] = _unstack_heads(dq).astype(BF16)
        dkc = lax.dot_general(dsc, qs, _TN, preferred_element_type=F32)
        dvc = lax.dot_general(p.astype(BF16), dos, _TN, preferred_element_type=F32)

        @pl.when(n == 0)
        def _():
            dk_acc[...] = jnp.zeros_like(dk_acc)
            dv_acc[...] = jnp.zeros_like(dv_acc)

        starts = (0, pl.multiple_of(jnp.maximum(n - 1, 0) * BLOCK, BLOCK), row0)
        for t, st in enumerate(starts):
            dk_acc[pl.ds(st, BLOCK), :] += dkc[t * BLOCK:(t + 1) * BLOCK, :]
            dv_acc[pl.ds(st, BLOCK), :] += dvc[t * BLOCK:(t + 1) * BLOCK, :]

        @pl.when(n == nb - 1)
        def _():
            dx_ref[:, A_WIDTH:A_WIDTH + 128] = dk_acc[...].astype(BF16)
            dx_ref[:, A_WIDTH + 128:A_SEG] = dv_acc[...].astype(BF16)

        dsink = -(jnp.exp(sk_ref[...] - lsev) * dsum)
        r8 = lax.broadcasted_iota(jnp.int32, (8, 128), 0)
        acc = jnp.zeros((8, 128), F32)
        for hh in range(A_HEADS):
            acc = acc + jnp.where(r8 == hh, jnp.sum(dsink[hh * BLOCK:(hh + 1) * BLOCK, :]), 0.0)

        @pl.when((bi == 0) & (n == 0))
        def _():
            ds_ref[...] = jnp.zeros_like(ds_ref)

        ds_ref[...] += acc

    q_spec, k_specs, v_specs, o_spec, col, lse_spec = _swa_specs()
    return pl.pallas_call(
        body, name=name, grid=(b, nb),
        in_specs=[q_spec] + k_specs + v_specs + [o_spec, lse_spec, o_spec, col, col, pl.BlockSpec(memory_space=pl.ANY)],
        out_specs=[pl.BlockSpec((None, l, A_SEG), lambda bb, n: (bb, 0, A_SEG_BLK)),
                   pl.BlockSpec((8, 128), lambda bb, n: (0, 0))],
        out_shape=[jax.ShapeDtypeStruct(dqkv.shape, BF16), jax.ShapeDtypeStruct((8, 128), F32)],
        scratch_shapes=[pltpu.VMEM((l, 128), F32), pltpu.VMEM((l, 128), F32)],
        input_output_aliases={12: 0},
        compiler_params=_params(("arbitrary", "arbitrary")),
    )(qkv, qkv, qkv, qkv, qkv, qkv, qkv, o, lse, do, slopes, sinks, dqkv)


def _fox_scores(q, k, ck, i):
    kh = k.shape[0]
    s = lax.dot_general(q, k, _NT, preferred_element_type=F32) - ck
    qpos = i * BLOCK + lax.broadcasted_iota(jnp.int32, (BLOCK, kh), 0)
    kpos = lax.broadcasted_iota(jnp.int32, (BLOCK, kh), 1)
    return jnp.where((kpos <= qpos) & (kpos >= N_PAD), s, NEG)


def _pick_head(x, hh):
    lo = _lane_lo()
    return jnp.where(lo if hh == 0 else jnp.logical_not(lo), x, jnp.zeros_like(x))


def _fox_specs(l):
    pair = pl.BlockSpec((None, l, PAIR_W), lambda bi, hp: (bi, 0, hp))
    half = pl.BlockSpec((None, l, 128), lambda bi, hp: (bi, 0, hp))
    colv = pl.BlockSpec((None, 2, l, 1), lambda bi, hp: (bi, hp, 0, 0))
    rowv = pl.BlockSpec((None, 2, 1, l), lambda bi, hp: (bi, hp, 0, 0))
    return pair, half, colv, rowv


def _fox_fwd(name, qkv, c_col, c_row):
    b, l, _ = qkv.shape
    nb = l // BLOCK

    def body(x_ref, cc_ref, cr_ref, o_ref, lse_ref):
        for i in range(nb):
            rows = slice(i * BLOCK, (i + 1) * BLOCK)
            kh = (i + 1) * BLOCK
            qblk = x_ref[rows, 0:128]
            kv = x_ref[0:kh, 128:256]
            vv = x_ref[0:kh, 256:384]
            outs = []
            for hh in range(2):
                s = _fox_scores(_pick_head(qblk, hh) * SCALE, kv, cr_ref[hh, :, 0:kh], i)
                m = jnp.max(s, axis=-1, keepdims=True)
                p = jnp.exp(s - m)
                den = jnp.sum(p, axis=-1, keepdims=True)
                outs.append(lax.dot_general(p.astype(BF16), vv, _NN, preferred_element_type=F32) / den)
                lse_ref[hh, rows, :] = (m + jnp.log(den)) + cc_ref[hh, rows, :]
            o_ref[rows, :] = jnp.where(_lane_lo(), outs[0], outs[1]).astype(BF16)

    pair, half, colv, rowv = _fox_specs(l)
    return pl.pallas_call(
        body, name=name, grid=(b, 4), in_specs=[pair, colv, rowv], out_specs=[half, colv],
        out_shape=[jax.ShapeDtypeStruct((b, l, B_WIDTH), BF16), jax.ShapeDtypeStruct((b, B_HEADS, l, 1), F32)],
        compiler_params=_params(("parallel", "parallel")),
    )(qkv, c_col, c_row)


def _fox_bwd(name, qkv, c_col, c_row, o, lse, do):
    b, l, _ = qkv.shape
    nb = l // BLOCK

    def body(x_ref, cc_ref, cr_ref, o_ref, lse_ref, do_ref, dx_ref, dcq_ref, dck_ref, dk_acc, dv_acc):
        dk_acc[...] = jnp.zeros_like(dk_acc)
        dv_acc[...] = jnp.zeros_like(dv_acc)
        dck_ref[...] = jnp.zeros_like(dck_ref)
        for i in range(nb):
            rows = slice(i * BLOCK, (i + 1) * BLOCK)
            kh = (i + 1) * BLOCK
            qblk = x_ref[rows, 0:128]
            kv = x_ref[0:kh, 128:256]
            vv = x_ref[0:kh, 256:384]
            doblk = do_ref[rows, :]
            ov = o_ref[rows, :].astype(F32)
            dqs = []
            for hh in range(2):
                qm = _pick_head(qblk, hh) * SCALE
                dom = _pick_head(doblk, hh)
                s = _fox_scores(qm, kv, cr_ref[hh, :, 0:kh], i)
                p = jnp.exp(s - (lse_ref[hh, rows, :] - cc_ref[hh, rows, :]))
                dsum = jnp.sum(dom.astype(F32) * ov, axis=-1, keepdims=True)
                dp = lax.dot_general(dom, vv, _NT, preferred_element_type=F32)
                ds = p * (dp - dsum)
                dsc = ds.astype(BF16)
                dqs.append(lax.dot_general(dsc, kv, _NN, preferred_element_type=F32) * SCALE)
                dk_acc[0:kh, :] += lax.dot_general(dsc, qm, _TN, preferred_element_type=F32)
                dv_acc[0:kh, :] += lax.dot_general(p.astype(BF16), dom, _TN, preferred_element_type=F32)
                dcq_ref[hh, rows, :] = jnp.sum(ds, axis=-1, keepdims=True)
                dck_ref[hh, :, 0:kh] -= jnp.sum(ds, axis=0, keepdims=True)
            dx_ref[rows, 0:128] = jnp.where(_lane_lo(), dqs[0], dqs[1]).astype(BF16)
        dx_ref[:, 128:256] = dk_acc[...].astype(BF16)
        dx_ref[:, 256:384] = dv_acc[...].astype(BF16)

    pair, half, colv, rowv = _fox_specs(l)
    return pl.pallas_call(
        body, name=name, grid=(b, 4), in_specs=[pair, colv, rowv, half, colv, half],
        out_specs=[pair, colv, rowv],
        out_shape=[jax.ShapeDtypeStruct(qkv.shape, BF16), jax.ShapeDtypeStruct((b, B_HEADS, l, 1), F32),
                   jax.ShapeDtypeStruct((b, B_HEADS, 1, l), F32)],
        scratch_shapes=[pltpu.VMEM((l, 128), F32), pltpu.VMEM((l, 128), F32)],
        compiler_params=_params(("parallel", "parallel")),
    )(qkv, c_col, c_row, o, lse, do)


_FLIPS = ((0, 0, 1), (0, 1, 0), (0, 1, 1), (1, 0, 0), (1, 0, 1), (1, 1, 0), (1, 1, 1))


def _exchange(name, gather, scatter):
    ng, ns = len(gather), len(scatter)
    na = ng + ns
    npeer = len(_FLIPS)

    def body(*refs):
        ins = refs[:na]
        outs = refs[na:2 * na]
        send_sems, recv_sems, loc_sems = refs[2 * na:]
        x, y, c = lax.axis_index("x"), lax.axis_index("y"), lax.axis_index("c")
        me = 4 * x + 2 * y + c
        peers = []
        for fx, fy, fc in _FLIPS:
            px = 1 - x if fx else x
            py = 1 - y if fy else y
            pc = 1 - c if fc else c
            peers.append(((px, py, pc), 4 * px + 2 * py + pc))

        def remote(a, kk):
            dev, lin = peers[kk]
            src = ins[a] if a < ng else ins[a].at[lin]
            return pltpu.make_async_remote_copy(src_ref=src, dst_ref=outs[a].at[me], send_sem=send_sems.at[a * npeer + kk],
                                                recv_sem=recv_sems.at[a * npeer + kk], device_id=dev, device_id_type=MESH_ID)

        def arrival(a, kk):
            dev, lin = peers[kk]
            src = ins[a] if a < ng else ins[a].at[lin]
            return pltpu.make_async_remote_copy(src_ref=src, dst_ref=outs[a].at[lin], send_sem=send_sems.at[a * npeer + kk],
                                                recv_sem=recv_sems.at[a * npeer + kk], device_id=dev, device_id_type=MESH_ID)

        local = []
        for a in range(na):
            src = ins[a] if a < ng else ins[a].at[me]
            cp = pltpu.make_async_copy(src, outs[a].at[me], loc_sems.at[a])
            cp.start()
            local.append(cp)
        sent = [remote(a, kk) for kk in range(npeer) for a in range(na)]
        for cp in sent:
            cp.start()
        for kk in range(npeer):
            for a in range(na):
                arrival(a, kk).wait_recv()
        for cp in sent:
            cp.wait_send()
        for cp in local:
            cp.wait()

    arrs = list(gather) + list(scatter)
    out_shape = [jax.ShapeDtypeStruct((N_DEV,) + tuple(a.shape), a.dtype) for a in gather]
    out_shape += [jax.ShapeDtypeStruct(tuple(a.shape), a.dtype) for a in scatter]
    anyspec = pl.BlockSpec(memory_space=pl.ANY)
    return pl.pallas_call(
        body, name=name, in_specs=[anyspec] * na, out_specs=[anyspec] * na, out_shape=out_shape,
        scratch_shapes=[pltpu.SemaphoreType.DMA((na * npeer,)), pltpu.SemaphoreType.DMA((na * npeer,)),
                        pltpu.SemaphoreType.DMA((na,))],
        compiler_params=pltpu.CompilerParams(has_side_effects=True),
    )(*arrs)


def _peer_table():
    x, y, c = lax.axis_index("x"), lax.axis_index("y"), lax.axis_index("c")
    me = 4 * x + 2 * y + c
    peers = []
    for fx, fy, fc in _FLIPS:
        px = 1 - x if fx else x
        py = 1 - y if fy else y
        pc = 1 - c if fc else c
        peers.append(((px, py, pc), 4 * px + 2 * py + pc))
    return me, peers


_HBM = pl.BlockSpec(memory_space=pltpu.HBM)
_SEM = pl.BlockSpec(memory_space=pltpu.SEMAPHORE)
_ANY = pl.BlockSpec(memory_space=pl.ANY)
_EFFECT = pltpu.SideEffectType.DATAFLOW_SIDE_EFFECTING


def _split_copy(srcs_are_pieces, src_refs, land_refs, send_sem, recv_sem, a, kk, me, peers, arriving):
    dev, lin = peers[kk]
    npeer = len(_FLIPS)
    src = src_refs[a] if srcs_are_pieces[a] else src_refs[a].at[lin]
    dst = land_refs[a].at[lin] if arriving else land_refs[a].at[me]
    return pltpu.make_async_remote_copy(src_ref=src, dst_ref=dst, send_sem=send_sem.at[a * npeer + kk],
                                        recv_sem=recv_sem.at[a * npeer + kk], device_id=dev, device_id_type=MESH_ID)


def _xchg_start(name, gather, scatter, after=None):
    me_out = 4 * lax.axis_index("x") + 2 * lax.axis_index("y") + lax.axis_index("c")
    srcs = list(gather) + list(scatter)
    is_piece = [True] * len(gather) + [False] * len(scatter)
    lands = []
    for a, piece in zip(srcs, is_piece):
        own = a[None] if piece else lax.dynamic_slice_in_dim(a, me_out, 1, axis=0)
        shape = ((N_DEV,) + tuple(a.shape)) if piece else tuple(a.shape)
        start = (me_out,) + (0,) * (len(shape) - 1)
        lands.append(lax.dynamic_update_slice(jnp.zeros(shape, a.dtype), own, start))
    n = len(srcs)
    nsem = n * len(_FLIPS)
    has_after = after is not None

    def body(*refs):
        src_refs = refs[:n]
        land_refs = refs[n:2 * n]
        outs = refs[2 * n + (1 if has_after else 0):]
        send_sem, recv_sem = outs[0], outs[1]
        token = outs[-1]
        me, peers = _peer_table()
        for kk in range(len(_FLIPS)):
            for a in range(n):
                _split_copy(is_piece, src_refs, land_refs, send_sem, recv_sem, a, kk, me, peers, False).start()
        token[...] = jnp.zeros_like(token)

    out_shape = ([pltpu.SemaphoreType.DMA((nsem,)), pltpu.SemaphoreType.DMA((nsem,))]
                 + [pltpu.HBM(tuple(a.shape), a.dtype) for a in srcs] + [pltpu.HBM(tuple(a.shape), a.dtype) for a in lands]
                 + [jax.ShapeDtypeStruct((8, 128), F32)])
    args = [pltpu.with_memory_space_constraint(a, pltpu.HBM) for a in srcs + lands] + ([after] if has_after else [])
    res = pl.pallas_call(
        body, name=name, out_shape=out_shape,
        in_specs=[_HBM] * (2 * n) + ([_ANY] if has_after else []),
        out_specs=[_SEM, _SEM] + [_HBM] * (2 * n) + [pl.BlockSpec(memory_space=pltpu.VMEM)],
        input_output_aliases={i: 2 + i for i in range(2 * n)},
        compiler_params=pltpu.CompilerParams(has_side_effects=_EFFECT),
    )(*args)
    state = (res[0], res[1], list(res[2:2 + n]), list(res[2 + n:2 + 2 * n]), is_piece)
    return state, res[-1]


def _xchg_wait(name, state, after):
    send_sem, recv_sem, srcs, lands, is_piece = state
    n = len(srcs)

    def body(*refs):
        src_refs = refs[:n]
        land_refs = refs[n:2 * n]
        s_sem, r_sem = refs[2 * n], refs[2 * n + 1]
        me, peers = _peer_table()
        for kk in range(len(_FLIPS)):
            for a in range(n):
                cp = _split_copy(is_piece, src_refs, land_refs, s_sem, r_sem, a, kk, me, peers, True)
                cp.wait_send()
                cp.wait_recv()

    out_shape = [pltpu.HBM(tuple(a.shape), a.dtype) for a in srcs] + [pltpu.HBM(tuple(a.shape), a.dtype) for a in lands]
    res = pl.pallas_call(
        body, name=name, out_shape=out_shape,
        in_specs=[_HBM] * (2 * n) + [_SEM, _SEM, _ANY], out_specs=[_HBM] * (2 * n),
        input_output_aliases={i: i for i in range(2 * n)},
        compiler_params=pltpu.CompilerParams(has_side_effects=_EFFECT),
    )(*srcs, *lands, send_sem, recv_sem, after)
    return list(res[n:])


def _adam_math(w, g, m, v):
    m = ADAM_B1 * m + (1.0 - ADAM_B1) * g
    v = ADAM_B2 * v + (1.0 - ADAM_B2) * (g * g)
    m_hat = m / (1.0 - ADAM_B1 ** ADAM_STEP)
    v_hat = v / (1.0 - ADAM_B2 ** ADAM_STEP)
    delta = -ADAM_LR * (m_hat / (jnp.sqrt(v_hat) + ADAM_EPS) + ADAM_WD * w)
    return delta, m, v


def _adam(name, w, m, v, parts):
    r, c = w.shape
    npart, _, cp = parts.shape
    tr = _pick(r, (256, 176, 128, 64, 16, 8, 1))

    def body(w_ref, m_ref, v_ref, p_ref, g_ref, d_ref, mo_ref, vo_ref):
        g = p_ref[0].astype(F32)
        for pp in range(1, npart):
            g = g + p_ref[pp].astype(F32)
        g = g[:, 0:c]
        delta, mn, vn = _adam_math(w_ref[...], g, m_ref[...], v_ref[...])
        g_ref[...] = g
        d_ref[...] = delta
        mo_ref[...] = mn
        vo_ref[...] = vn

    blk = pl.BlockSpec((tr, c), lambda i: (i, 0))
    out = jax.ShapeDtypeStruct((r, c), F32)
    return pl.pallas_call(
        body, name=name, grid=(r // tr,),
        in_specs=[blk, blk, blk, pl.BlockSpec((npart, tr, cp), lambda i: (0, i, 0))],
        out_specs=[blk, blk, blk, blk], out_shape=[out, out, out, out], compiler_params=_params(("parallel",)),
    )(w, m, v, parts)


def _small_sum(name, packs):
    def body(p_ref, o_ref):
        tot = p_ref[0]
        for pp in range(1, N_DEV):
            tot = tot + p_ref[pp]
        o_ref[0:8, :] = tot[0:8, :]
        o_ref[8:24, :] = tot[8:24, :] + tot[24:40, :]

    return pl.pallas_call(body, name=name, out_shape=jax.ShapeDtypeStruct((24, D_MODEL), F32),
                          compiler_params=_params())(packs)


def _local_step(x, tgt, g1, gm, g2, gf, b_forget, sinks, weights, send):
    b, s, _ = x.shape
    l = s + PREFIX
    t = b * l
    w1i, meta = weights("ffn1_in", x)
    h0 = jnp.concatenate([jnp.zeros((b, N_PAD, D_MODEL), F32), jnp.broadcast_to(meta[None], (b, N_META, D_MODEL)), x],
                         axis=1).reshape(t, D_MODEL)

    n1 = _rms_fwd("rms1_fwd", h0, g1)
    gu1, a1 = _ffn_in_fwd("ffn1_in_fwd", n1, w1i)
    (w1o,) = weights("ffn1_out", a1)
    h1 = _mm_nn("ffn1_out_fwd", a1, w1o, alpha=0.5, res=h0)
    wi, wa, wb, wo = weights("mix", h1)
    um = _rms_fwd("rmsm_fwd", h1, gm)
    qkv = _mm_nn("proj_qkv_fwd", um, wi, out_dtype=BF16, tn_c=(768,), cols=(0, QKV_W))
    gates = _mm_nn("proj_gates_fwd", um, wi, tn_c=(1024,), cols=(P_GATES, 2 * D_MODEL))
    f2 = _mm_nn("proj_f_fwd", um, wi, tn_c=(128,), cols=(P_F, 128))
    qkv3 = qkv.reshape(b, l, QKV_W)
    f3 = f2.reshape(b, l, 128)
    bf_row = jnp.pad(b_forget, ((0, 0), (0, 128 - B_HEADS)))
    c_col, c_row = _fgate_fwd("fgate_fwd", f3, bf_row)
    head_of_row = jnp.arange(STACK) // BLOCK
    slopes = jnp.exp2(-8.0 * (head_of_row + 1).astype(F32) / A_HEADS).reshape(STACK, 1)
    sink_rows = jnp.repeat(sinks.reshape(A_HEADS), BLOCK).reshape(STACK, 1)
    oa3, lse_a = _swa_fwd("swa_fwd", qkv3, slopes, sink_rows)
    ob3, lse_b = _fox_fwd("fox_fwd", qkv3, c_col, c_row)
    oa = oa3.reshape(t, A_WIDTH)
    ob = ob3.reshape(t, B_WIDTH)
    ya = _mm_nn("branch_a_fwd", oa, wa)
    yb = _mm_nn("branch_b_fwd", ob, wb)
    mixed = _gate_fwd("gate_fwd", gates, ya, yb)
    h2 = _mm_nn("mix_out_fwd", mixed, wo, res=h1)
    w2i, w2o = weights("ffn2", h2)
    n2 = _rms_fwd("rms2_fwd", h2, g2)
    gu2, a2 = _ffn_in_fwd("ffn2_in_fwd", n2, w2i)
    h3 = _mm_nn("ffn2_out_fwd", a2, w2o, alpha=0.5, res=h2)

    dh3_3, loss_blk, dgf = _loss_head("loss_head", h3.reshape(b, l, D_MODEL), gf, tgt)
    dh3 = dh3_3.reshape(t, D_MODEL)

    def ffn_bwd(tag, dh, h_in, g_norm, n_in, gu, a, w_in_blk, w_out):
        dw_out = _mm_tn(tag + "_out_bwd_w", a, dh, alpha=0.5)
        dgu = _ffn_out_bwd_x(tag + "_out_bwd_x", dh, w_out, gu, dep=send(tag + "_out", (dw_out,)))
        dw_in = _ffn_in_bwd_w(tag + "_in_bwd_w", n_in, dgu)
        return _ffn_in_bwd_x(tag + "_in_bwd_x", dgu, w_in_blk, h_in, g_norm, dh, dep=send(tag + "_in", (dw_in,)))

    dh2, dg2 = ffn_bwd("ffn2", dh3, h2, g2, n2, gu2, a2, w2i, w2o)

    dmix = _mm_nt("mix_out_bwd_x", dh2, wo, tn_c=(512,))
    dwo = _mm_tn("mix_out_bwd_w", mixed, dh2)
    dya, dyb, dgates = _gate_bwd("gate_bwd", gates, ya, yb, dmix)
    doa = _mm_nt("branch_a_bwd_x", dya, wa, tn_c=(512,))
    dob = _mm_nt("branch_b_bwd_x", dyb, wb, out_dtype=BF16, tn_c=(512,))
    dwa = _mm_tn("branch_a_bwd_w", oa, dya, tm_c=(512,))
    dwb = _mm_tn("branch_b_bwd_w", ob, dyb, tm_c=(512,))
    dqkv3, dcq, dck = _fox_bwd("fox_bwd", qkv3, c_col, c_row, ob3, lse_b, dob.reshape(b, l, B_WIDTH))
    dqkv3, dsink = _swa_bwd("swa_bwd", qkv3, oa3, lse_a, doa.reshape(b, l, A_WIDTH), slopes, sink_rows, dqkv3)
    dqkv = dqkv3.reshape(t, QKV_W)
    df3, dbf = _fgate_bwd("fgate_bwd", f3, bf_row, dcq, dck)
    df = df3.reshape(t, 128)
    dwi_qkv = _mm_tn("proj_qkv_bwd_w", um, dqkv, tm_c=(512,), tn_c=(768,))
    dwi_g = _mm_tn("proj_gates_bwd_w", um, dgates, tm_c=(512,), tn_c=(512,))
    dwi_f = _mm_tn("proj_f_bwd_w", um, df, tm_c=(512,), tn_c=(128,))
    token = send("mix", (dwi_qkv, dwi_g, dwi_f, dwa, dwb, dwo))
    dh1, dgm = _proj_bwd_x("proj_bwd_x", dqkv, dgates, df, wi, h1, gm, dh2, dep=token)

    dh0, dg1 = ffn_bwd("ffn1", dh1, h0, g1, n1, gu1, a1, w1i, w1o)
    dh0_3 = dh0.reshape(b, l, D_MODEL)
    grad_x = dh0_3[:, PREFIX:, :]
    dmeta = dh0_3[:, N_PAD:PREFIX, :].reshape(b * N_META, D_MODEL)

    misc = jnp.concatenate([dbf[:, 0:B_HEADS], dsink[:, 0].reshape(1, A_HEADS), loss_blk[0:1, 0:1]], axis=1)
    misc = jnp.pad(misc, ((0, 0), (0, D_MODEL - misc.shape[1])))
    row = lax.broadcasted_iota(jnp.int32, (8, D_MODEL), 0)
    vec = jnp.zeros((8, D_MODEL), F32)
    for i, piece in enumerate((dg1, dgm, dg2, dgf, misc)):
        vec = jnp.where(row == i, piece, vec)
    small = jnp.concatenate([vec, dmeta], axis=0)
    return grad_x, small


def _pad_to(a, rows, cols):
    return jnp.pad(a, ((0, rows - a.shape[0]), (0, cols - a.shape[1])))


def _ffn_out_from_gathered(g):
    w = g.reshape(4, FF_SHARD, D_MODEL)
    return jnp.pad(w, ((0, 0), (0, FF_SHARD_P - FF_SHARD), (0, 0))).reshape(D_FF_P, D_MODEL)


def _ffn_out_to_scatter(dw):
    return dw.reshape(4, FF_SHARD_P, D_MODEL)[:, 0:FF_SHARD, :].reshape(N_DEV, FFO_SHARD, D_MODEL)


def _proj_segments():
    segs = [(HEAD_DIM * h, HEAD_DIM, 0, B_SEG + HEAD_DIM * A_HEAD_ORDER.index(h)) for h in range(A_HEADS)]
    segs += [(512, 128, 0, B_SEG + A_WIDTH), (640, 128, 0, B_SEG + A_WIDTH + 128)]
    for first, off in ((768, 0), (1280, 128), (1792, 256)):
        segs += [(first + 128 * hp, 128, 0, PAIR_W * hp + off) for hp in range(4)]
    segs += [(2304, B_HEADS, 2, 0), (2312, 2 * D_MODEL, 1, 0)]
    return segs


def _proj_from_gathered(g):
    def cols(first, width):
        out = []
        for p in range(N_DEV):
            lo, hi = max(first, WIN_SHARD * p), min(first + width, WIN_SHARD * (p + 1))
            if lo < hi:
                out.append(g[p, :, lo - WIN_SHARD * p:hi - WIN_SHARD * p])
        return out

    parts = []
    for arr in (0, 1, 2):
        for first, width, _, _ in sorted((s for s in _proj_segments() if s[2] == arr), key=lambda s: s[3]):
            parts += cols(first, width)
        if arr == 0:
            parts.append(jnp.zeros((D_MODEL, P_GATES - QKV_W), g.dtype))
    parts.append(jnp.zeros((D_MODEL, 128 - B_HEADS), g.dtype))
    return jnp.concatenate(parts, axis=1)


def _proj_to_scatter(dqkv_w, dg_w, df_w):
    arrays = (dqkv_w, dg_w, df_w)
    segs = sorted(_proj_segments())
    blocks = []
    for p in range(N_DEV):
        parts = []
        for first, width, arr, at in segs:
            lo, hi = max(first, WIN_SHARD * p), min(first + width, WIN_SHARD * (p + 1))
            if lo < hi:
                parts.append(arrays[arr][:, at + lo - first:at + hi - first])
        parts.append(jnp.zeros((D_MODEL, WIN_SHARD_P - WIN_SHARD), dqkv_w.dtype))
        blocks.append(jnp.concatenate(parts, axis=1))
    return jnp.stack(blocks, axis=0)


def _a_rows_from_natural(w):
    return jnp.concatenate([w[HEAD_DIM * h:HEAD_DIM * (h + 1)] for h in A_HEAD_ORDER], axis=0)


def _a_rows_to_natural(w):
    return jnp.concatenate([w[HEAD_DIM * A_HEAD_ORDER.index(h):HEAD_DIM * (A_HEAD_ORDER.index(h) + 1)]
                            for h in range(A_HEADS)], axis=0)


def kernel(x, meta_tokens, ffn1_norm, ffn1_w_in, ffn1_w_out, mix_norm, w_in, b_forget, attn_sinks, w_branch_a, w_branch_b, w_out, ffn2_norm, ffn2_w_in, ffn2_w_out, final_norm, loss_target, m_meta_tokens, m_ffn1_norm, m_ffn1_w_in, m_ffn1_w_out, m_mix_norm, m_w_in, m_b_forget, m_attn_sinks, m_w_branch_a, m_w_branch_b, m_w_out, m_ffn2_norm, m_ffn2_w_in, m_ffn2_w_out, m_final_norm, v_meta_tokens, v_ffn1_norm, v_ffn1_w_in, v_ffn1_w_out, v_mix_norm, v_w_in, v_b_forget, v_attn_sinks, v_w_branch_a, v_w_branch_b, v_w_out, v_ffn2_norm, v_ffn2_w_in, v_ffn2_w_out, v_final_norm):
    me = 4 * lax.axis_index("x") + 2 * lax.axis_index("y") + lax.axis_index("c")

    shards = (
        _pad_to(ffn1_w_in[0].astype(BF16), D_MODEL, FF_SHARD_P),
        ffn1_w_out[0].astype(BF16),
        _pad_to(w_in[0].astype(BF16), D_MODEL, WIN_SHARD_P),
        w_branch_a[0].astype(BF16), w_branch_b[0].astype(BF16), w_out[0].astype(BF16),
        _pad_to(ffn2_w_in[0].astype(BF16), D_MODEL, FF_SHARD_P),
        ffn2_w_out[0].astype(BF16),
        meta_tokens,
    )
    s1i, s1o, swi, swa, swb, swo, s2i, s2o, smeta = shards
    gather_state = {}
    gather_state["ffn1_in"], tok = _xchg_start("gather_ffn1_in_start", (s1i, smeta), ())
    gather_state["ffn1_out"], tok = _xchg_start("gather_ffn1_out_start", (s1o,), (), after=tok)
    gather_state["mix"], tok = _xchg_start("gather_mix_start", (swi, swa, swb, swo), (), after=tok)
    gather_state["ffn2"], tok = _xchg_start("gather_ffn2_start", (s2i, s2o), (), after=tok)
    started = {"tok": tok}

    def weights(group, after):
        if group == "ffn1_in":
            after = started["tok"]
        got = _xchg_wait("gather_" + group + "_wait", gather_state[group], after)
        if group == "mix":
            gwi, gwa, gwb, gwo = got
            return (_proj_from_gathered(gwi), _a_rows_from_natural(gwa.transpose(1, 0, 2).reshape(A_WIDTH, D_MODEL)),
                    gwb.transpose(1, 0, 2).reshape(B_WIDTH, D_MODEL), gwo.reshape(D_MODEL, D_MODEL))
        if group == "ffn1_in":
            return got[0].reshape(2, 4, D_MODEL, FF_SHARD_P), got[1].transpose(1, 0, 2).reshape(N_META, D_MODEL)
        if group == "ffn1_out":
            return (_ffn_out_from_gathered(got[0]),)
        return got[0].reshape(2, 4, D_MODEL, FF_SHARD_P), _ffn_out_from_gathered(got[1])

    scatter_state = {}

    def send(group, grads):
        if group == "mix":
            dwi_qkv, dwi_g, dwi_f, dwa, dwb, dwo = grads
            dwa = _a_rows_to_natural(dwa)
            blocks = (_proj_to_scatter(dwi_qkv, dwi_g, dwi_f), dwa.reshape(A_WIDTH, N_DEV, 128).transpose(1, 0, 2),
                      dwb.reshape(B_WIDTH, N_DEV, 128).transpose(1, 0, 2), dwo.reshape(N_DEV, 128, D_MODEL))
        elif group.endswith("_in"):
            blocks = (grads[0].reshape(N_DEV, D_MODEL, FF_SHARD_P),)
        else:
            blocks = (_ffn_out_to_scatter(grads[0]),)
        scatter_state[group], token = _xchg_start("scatter_" + group + "_start", (), blocks)
        return token

    gf = final_norm.reshape(1, D_MODEL)
    grad_x, small = _local_step(x, loss_target, ffn1_norm, mix_norm, ffn2_norm, gf, b_forget, attn_sinks, weights, send)

    small_state, after = _xchg_start("gather_small_start", (small,), ())
    out = {}
    updates = (
        ("ffn2_out", (("ffn2_w_out", ffn2_w_out, m_ffn2_w_out, v_ffn2_w_out),)),
        ("ffn2_in", (("ffn2_w_in", ffn2_w_in, m_ffn2_w_in, v_ffn2_w_in),)),
        ("mix", (("w_in", w_in, m_w_in, v_w_in), ("w_branch_a", w_branch_a, m_w_branch_a, v_w_branch_a),
                 ("w_branch_b", w_branch_b, m_w_branch_b, v_w_branch_b), ("w_out", w_out, m_w_out, v_w_out))),
        ("ffn1_out", (("ffn1_w_out", ffn1_w_out, m_ffn1_w_out, v_ffn1_w_out),)),
        ("ffn1_in", (("ffn1_w_in", ffn1_w_in, m_ffn1_w_in, v_ffn1_w_in),)),
    )
    for group, members in updates:
        parts_list = _xchg_wait("scatter_" + group + "_wait", scatter_state[group], after)
        for (nm, w, m, v), parts in zip(members, parts_list):
            res4 = _adam("adam_" + nm, w[0], m[0], v[0], parts)
            out[nm] = tuple(r[None] for r in res4)
            after = res4[0]
    (packs,) = _xchg_wait("gather_small_wait", small_state, after)

    tot = _small_sum("small_sum", packs)
    loss = tot[4, 2 * B_HEADS]
    g_meta = lax.dynamic_slice(tot[8:24, :], (0, me * 128), (N_META, 128))
    out["meta_tokens"] = tuple(_adam("adam_meta_tokens", meta_tokens, m_meta_tokens, v_meta_tokens, g_meta[None]))

    def pack_small(n1, nm, n2, nf, bfv, skv):
        misc = jnp.pad(jnp.concatenate([bfv, skv], axis=1), ((0, 0), (0, D_MODEL - 2 * B_HEADS)))
        row = lax.broadcasted_iota(jnp.int32, (8, D_MODEL), 0)
        vec = jnp.zeros((8, D_MODEL), F32)
        for i, piece in enumerate((n1, nm, n2, nf.reshape(1, D_MODEL), misc)):
            vec = jnp.where(row == i, piece, vec)
        return vec

    w_pack = pack_small(ffn1_norm, mix_norm, ffn2_norm, final_norm, b_forget, attn_sinks)
    m_pack = pack_small(m_ffn1_norm, m_mix_norm, m_ffn2_norm, m_final_norm, m_b_forget, m_attn_sinks)
    v_pack = pack_small(v_ffn1_norm, v_mix_norm, v_ffn2_norm, v_final_norm, v_b_forget, v_attn_sinks)
    small4 = _adam("adam_small", w_pack, m_pack, v_pack, tot[0:8][None])
    for i, nm in enumerate(("ffn1_norm", "mix_norm", "ffn2_norm")):
        out[nm] = tuple(r[i:i + 1] for r in small4)
    out["final_norm"] = tuple(r[3] for r in small4)
    out["b_forget"] = tuple(r[4:5, 0:B_HEADS] for r in small4)
    out["attn_sinks"] = tuple(r[4:5, B_HEADS:2 * B_HEADS] for r in small4)

    names = ("meta_tokens", "ffn1_norm", "ffn1_w_in", "ffn1_w_out", "mix_norm", "w_in", "b_forget", "attn_sinks",
             "w_branch_a", "w_branch_b", "w_out", "ffn2_norm", "ffn2_w_in", "ffn2_w_out", "final_norm")
    return (loss, grad_x) + tuple(out[nm][kind] for kind in range(4) for nm in names)
```

```python
import jax
import jax.numpy as jnp
from jax import lax
from jax.experimental import pallas as pl
from jax.experimental.pallas import tpu as pltpu

F32 = jnp.float32
BF16 = jnp.bfloat16

D_MODEL = 1024
N_META = 16
BLOCK = 128
PREFIX = 128
N_PAD = PREFIX - N_META
HEAD_DIM = 64
A_HEADS = 8
A_KV_HEADS = 2
A_GROUP = 4
B_HEADS = 8
A_WIDTH = 512
A_KV_WIDTH = 128
B_WIDTH = 512
D_FF = 2816
N_DEV = 8
FF_SHARD = 2 * D_FF // N_DEV
FF_SHARD_P = 768
FFO_SHARD = D_FF // N_DEV
FFO_SHARD_P = FF_SHARD_P // 2
D_FF_P = 4 * FF_SHARD_P
W_IN_COLS = 4360
WIN_SHARD = W_IN_COLS // N_DEV
WIN_SHARD_P = 640
PAIR_W = 3 * 128
B_SEG = 4 * PAIR_W
A_SEG = A_WIDTH + 2 * A_KV_WIDTH
QKV_W = B_SEG + A_SEG
P_GATES = 2 * (2 * D_MODEL)
P_F = P_GATES + 2 * D_MODEL
PROJ_P = P_F + 128
A_HEAD_ORDER = (0, 4, 1, 5, 2, 6, 3, 7)
EPS = 1e-6
NEG = -1e30
SCALE = HEAD_DIM ** -0.5
ADAM_LR = 0.001
ADAM_B1 = 0.9
ADAM_B2 = 0.999
ADAM_EPS = 1e-08
ADAM_WD = 0.01
ADAM_STEP = 10
VMEM_LIMIT = 56 * 1024 * 1024
MESH_ID = pl.DeviceIdType.MESH
SMALL_ROWS = 40

_NN = (((1,), (0,)), ((), ()))
_NT = (((1,), (1,)), ((), ()))
_TN = (((0,), (0,)), ((), ()))


def _params(sem=None):
    return pltpu.CompilerParams(dimension_semantics=sem, vmem_limit_bytes=VMEM_LIMIT)


def _pick(n, cands):
    for c in cands:
        if n % c == 0:
            return c
    raise ValueError(f"no tile for {n}")


def _bf(v):
    return v if v.dtype == BF16 else v.astype(BF16)


def _mm(name, a, b, dims, grid, a_spec, b_spec, o_spec, out_shape, out_dtype, acc_shape, k_axis=None, nk=1,
        alpha=1.0, res=None, res_spec=None, dep=None):
    has_res = res is not None
    has_dep = dep is not None

    def body(*refs):
        a_ref, b_ref = refs[0], refs[1]
        r_ref = refs[2] if has_res else None
        o_ref = refs[2 + has_res + has_dep]

        def finish(acc):
            if alpha != 1.0:
                acc = acc * alpha
            if has_res:
                acc = acc + r_ref[...]
            o_ref[...] = acc.astype(o_ref.dtype)

        part = lax.dot_general(_bf(a_ref[...]), _bf(b_ref[...]), dims, preferred_element_type=F32)
        if nk == 1:
            finish(part)
        else:
            acc_ref = refs[-1]
            k = pl.program_id(k_axis)

            @pl.when(k == 0)
            def _():
                acc_ref[...] = part

            @pl.when(k > 0)
            def _():
                acc_ref[...] += part

            @pl.when(k == nk - 1)
            def _():
                finish(acc_ref[...])

    in_specs = [a_spec, b_spec] + ([res_spec] if has_res else []) + ([pl.BlockSpec(memory_space=pl.ANY)] if has_dep else [])
    args = (a, b) + ((res,) if has_res else ()) + ((dep,) if has_dep else ())
    sem = tuple("arbitrary" if (nk > 1 and i == k_axis) else "parallel" for i in range(len(grid)))
    return pl.pallas_call(
        body, name=name, grid=grid, in_specs=in_specs, out_specs=o_spec,
        out_shape=jax.ShapeDtypeStruct(out_shape, out_dtype),
        scratch_shapes=[pltpu.VMEM(acc_shape, F32)] if nk > 1 else [],
        compiler_params=_params(sem),
    )(*args)


def _mm_nn(name, a, b, out_dtype=F32, alpha=1.0, res=None, tn_c=(512, 640, 256, 128), cols=None,
           tm_c=(1088, 768, 512, 256, 128)):
    t, k = a.shape
    c0, n = (0, b.shape[1]) if cols is None else cols
    tm = _pick(t, tm_c)
    tn = _pick(n, tn_c)
    assert c0 % tn == 0
    jb = c0 // tn
    return _mm(name, a, b, _NN, (t // tm, n // tn),
               pl.BlockSpec((tm, k), lambda i, j: (i, 0)), pl.BlockSpec((k, tn), lambda i, j: (0, jb + j)),
               pl.BlockSpec((tm, tn), lambda i, j: (i, j)), (t, n), out_dtype, None,
               alpha=alpha, res=res, res_spec=pl.BlockSpec((tm, tn), lambda i, j: (i, j)))


def _mm_nt(name, a, b, out_dtype=F32, alpha=1.0, tn_c=(768, 512, 256, 128), tk_c=None, dep=None, res=None, kcols=None):
    t, k = a.shape
    n = b.shape[0]
    c0 = 0 if kcols is None else kcols[0]
    tm = _pick(t, (1088, 768, 512, 256, 128))
    tn = _pick(n, tn_c)
    tk = k if tk_c is None else _pick(k, tk_c)
    nk = k // tk
    assert c0 % tk == 0
    kb = c0 // tk
    return _mm(name, a, b, _NT, (t // tm, n // tn, nk),
               pl.BlockSpec((tm, tk), lambda i, j, kk: (i, kk)), pl.BlockSpec((tn, tk), lambda i, j, kk: (j, kb + kk)),
               pl.BlockSpec((tm, tn), lambda i, j, kk: (i, j)), (t, n), out_dtype, (tm, tn), k_axis=2, nk=nk, alpha=alpha,
               dep=dep, res=res, res_spec=pl.BlockSpec((tm, tn), lambda i, j, kk: (i, j)))


def _mm_tn(name, a, b, out_dtype=BF16, alpha=1.0, tm_c=(768, 512, 256, 128), tn_c=(512, 640, 256, 128)):
    t, m = a.shape
    n = b.shape[1]
    tm = _pick(m, tm_c)
    tn = _pick(n, tn_c)
    tk = t
    nk = 1
    return _mm(name, a, b, _TN, (m // tm, n // tn, nk),
               pl.BlockSpec((tk, tm), lambda i, j, kk: (kk, i)), pl.BlockSpec((tk, tn), lambda i, j, kk: (kk, j)),
               pl.BlockSpec((tm, tn), lambda i, j, kk: (i, j)), (m, n), out_dtype, (tm, tn), k_axis=2, nk=nk, alpha=alpha)


def _silu_grads(g, u, d):
    sg = jax.nn.sigmoid(g)
    return d * u * (sg * (1.0 + g * (1.0 - sg))), d * (g * sg)


def _ffn_in_fwd(name, n, wblk):
    t = n.shape[0]
    tm = _pick(t, (1088, 768, 512, 256, 128))

    half = tm // 2 if tm % 32 == 0 else tm

    def body(n_ref, w_ref, gu_ref, a_ref):
        for r0 in range(0, tm, half):
            rows = slice(r0, r0 + half)
            nv = n_ref[rows, :]
            g = lax.dot_general(nv, w_ref[0], _NN, preferred_element_type=F32)
            u = lax.dot_general(nv, w_ref[1], _NN, preferred_element_type=F32)
            gu_ref[0, rows, :] = g.astype(BF16)
            gu_ref[1, rows, :] = u.astype(BF16)
            a_ref[rows, :] = ((g * jax.nn.sigmoid(g)) * u).astype(BF16)

    return pl.pallas_call(
        body, name=name, grid=(t // tm, 4),
        in_specs=[pl.BlockSpec((tm, D_MODEL), lambda i, j: (i, 0)),
                  pl.BlockSpec((2, None, D_MODEL, FF_SHARD_P), lambda i, j: (0, j, 0, 0))],
        out_specs=[pl.BlockSpec((2, tm, FF_SHARD_P), lambda i, j: (0, i, j)),
                   pl.BlockSpec((tm, FF_SHARD_P), lambda i, j: (i, j))],
        out_shape=[jax.ShapeDtypeStruct((2, t, D_FF_P), BF16), jax.ShapeDtypeStruct((t, D_FF_P), BF16)],
        compiler_params=_params(("parallel", "parallel")),
    )(n, wblk)


def _ffn_out_bwd_x(name, dh, w_out, gu, dep=None):
    t = dh.shape[0]
    tm = _pick(t, (1088, 768, 512, 256, 128))
    has_dep = dep is not None

    half = tm // 2 if tm % 32 == 0 else tm

    def body(dh_ref, w_ref, gu_ref, *rest):
        o_ref = rest[-1]
        for r0 in range(0, tm, half):
            rows = slice(r0, r0 + half)
            da = lax.dot_general(_bf(dh_ref[rows, :]), w_ref[...], _NT, preferred_element_type=F32) * 0.5
            dg, du = _silu_grads(gu_ref[0, rows, :].astype(F32), gu_ref[1, rows, :].astype(F32), da)
            o_ref[0, rows, :] = dg.astype(BF16)
            o_ref[1, rows, :] = du.astype(BF16)

    gu_spec = pl.BlockSpec((2, tm, FF_SHARD_P), lambda i, j: (0, i, j))
    return pl.pallas_call(
        body, name=name, grid=(t // tm, 4),
        in_specs=[pl.BlockSpec((tm, D_MODEL), lambda i, j: (i, 0)), pl.BlockSpec((FF_SHARD_P, D_MODEL), lambda i, j: (j, 0)),
                  gu_spec] + ([pl.BlockSpec(memory_space=pl.ANY)] if has_dep else []),
        out_specs=gu_spec, out_shape=jax.ShapeDtypeStruct((2, t, D_FF_P), BF16),
        compiler_params=_params(("parallel", "parallel")),
    )(*((dh, w_out, gu) + ((dep,) if has_dep else ())))


def _rms_bwd_rows(dn, h, g, dres):
    r = lax.rsqrt(jnp.mean(h * h, axis=-1, keepdims=True) + EPS)
    tv = dn * g
    dot = jnp.mean(tv * h, axis=-1, keepdims=True)
    return dres + (r * tv - h * (r * r * r * dot)), jnp.sum(dn * (h * r), axis=0, keepdims=True)


def _accumulate_rows(ref, part, first):
    @pl.when(first)
    def _():
        ref[...] = part

    @pl.when(jnp.logical_not(first))
    def _():
        ref[...] += part


def _ffn_in_bwd_x(name, dgu, wblk, h_in, g_norm, dres, dep=None):
    t = dgu.shape[1]
    tm = _pick(t, (272, 256, 128))
    has_dep = dep is not None

    def body(d_ref, w_ref, h_ref, g_ref, r_ref, *rest):
        dh_ref, dg_ref = rest[-2], rest[-1]
        acc = None
        for s in range(2):
            for j in range(4):
                part = lax.dot_general(d_ref[s, :, FF_SHARD_P * j:FF_SHARD_P * (j + 1)], w_ref[s, j], _NT,
                                       preferred_element_type=F32)
                acc = part if acc is None else acc + part
        dh, dg = _rms_bwd_rows(acc, h_ref[...], g_ref[...], r_ref[...])
        dh_ref[...] = dh
        _accumulate_rows(dg_ref, dg, pl.program_id(0) == 0)

    row = pl.BlockSpec((tm, D_MODEL), lambda i: (i, 0))
    vec = pl.BlockSpec((1, D_MODEL), lambda i: (0, 0))
    return pl.pallas_call(
        body, name=name, grid=(t // tm,),
        in_specs=[pl.BlockSpec((2, tm, D_FF_P), lambda i: (0, i, 0)),
                  pl.BlockSpec((2, 4, D_MODEL, FF_SHARD_P), lambda i: (0, 0, 0, 0)), row, vec, row]
        + ([pl.BlockSpec(memory_space=pl.ANY)] if has_dep else []),
        out_specs=[row, vec],
        out_shape=[jax.ShapeDtypeStruct((t, D_MODEL), F32), jax.ShapeDtypeStruct((1, D_MODEL), F32)],
        compiler_params=_params(("arbitrary",)),
    )(*((dgu, wblk, h_in, g_norm, dres) + ((dep,) if has_dep else ())))


def _proj_bwd_x(name, dqkv, dgates, df, wi, h_in, g_norm, dres, dep=None):
    t = dqkv.shape[0]
    tm = _pick(t, (272, 256, 128))
    has_dep = dep is not None

    def body(q_ref, gt_ref, f_ref, w_ref, h_ref, g_ref, r_ref, *rest):
        dh_ref, dg_ref = rest[-2], rest[-1]
        acc = lax.dot_general(q_ref[...], w_ref[:, 0:QKV_W], _NT, preferred_element_type=F32)
        acc = acc + lax.dot_general(gt_ref[...], w_ref[:, P_GATES:P_F], _NT, preferred_element_type=F32)
        acc = acc + lax.dot_general(f_ref[...], w_ref[:, P_F:PROJ_P], _NT, preferred_element_type=F32)
        dh, dg = _rms_bwd_rows(acc, h_ref[...], g_ref[...], r_ref[...])
        dh_ref[...] = dh
        _accumulate_rows(dg_ref, dg, pl.program_id(0) == 0)

    row = pl.BlockSpec((tm, D_MODEL), lambda i: (i, 0))
    vec = pl.BlockSpec((1, D_MODEL), lambda i: (0, 0))
    return pl.pallas_call(
        body, name=name, grid=(t // tm,),
        in_specs=[pl.BlockSpec((tm, QKV_W), lambda i: (i, 0)), pl.BlockSpec((tm, 2 * D_MODEL), lambda i: (i, 0)),
                  pl.BlockSpec((tm, 128), lambda i: (i, 0)), pl.BlockSpec((D_MODEL, PROJ_P), lambda i: (0, 0)), row, vec, row]
        + ([pl.BlockSpec(memory_space=pl.ANY)] if has_dep else []),
        out_specs=[row, vec],
        out_shape=[jax.ShapeDtypeStruct((t, D_MODEL), F32), jax.ShapeDtypeStruct((1, D_MODEL), F32)],
        compiler_params=_params(("arbitrary",)),
    )(*((dqkv, dgates, df, wi, h_in, g_norm, dres) + ((dep,) if has_dep else ())))


def _ffn_in_bwd_w(name, n, dgu):
    t = n.shape[0]
    tk = t
    nk = 1
    return _mm(name, n, dgu, _TN, (2, 4, nk),
               pl.BlockSpec((tk, D_MODEL), lambda s, j, kk: (kk, 0)),
               pl.BlockSpec((None, tk, FF_SHARD_P), lambda s, j, kk: (s, kk, j)),
               pl.BlockSpec((None, None, D_MODEL, FF_SHARD_P), lambda s, j, kk: (s, j, 0, 0)),
               (2, 4, D_MODEL, FF_SHARD_P), BF16, (D_MODEL, FF_SHARD_P), k_axis=2, nk=nk)


def _rms_fwd(name, h, g):
    t = h.shape[0]
    tm = _pick(t, (544, 384, 256, 128))

    def body(h_ref, g_ref, o_ref):
        hv = h_ref[...]
        r = lax.rsqrt(jnp.mean(hv * hv, axis=-1, keepdims=True) + EPS)
        o_ref[...] = ((hv * r) * g_ref[...]).astype(BF16)

    return pl.pallas_call(
        body, name=name, grid=(t // tm,),
        in_specs=[pl.BlockSpec((tm, D_MODEL), lambda i: (i, 0)), pl.BlockSpec((1, D_MODEL), lambda i: (0, 0))],
        out_specs=pl.BlockSpec((tm, D_MODEL), lambda i: (i, 0)),
        out_shape=jax.ShapeDtypeStruct((t, D_MODEL), BF16), compiler_params=_params(("parallel",)),
    )(h, g)


def _rms_bwd(name, h, g, dn, dres):
    t = h.shape[0]
    tm = _pick(t, (544, 384, 256, 128))

    def body(h_ref, g_ref, dn_ref, dres_ref, dh_ref, dg_ref):
        i = pl.program_id(0)
        hv = h_ref[...]
        dnv = dn_ref[...]
        r = lax.rsqrt(jnp.mean(hv * hv, axis=-1, keepdims=True) + EPS)
        tv = dnv * g_ref[...]
        dot = jnp.mean(tv * hv, axis=-1, keepdims=True)
        dh_ref[...] = dres_ref[...] + (r * tv - hv * (r * r * r * dot))
        part = jnp.sum(dnv * (hv * r), axis=0, keepdims=True)

        @pl.when(i == 0)
        def _():
            dg_ref[...] = part

        @pl.when(i > 0)
        def _():
            dg_ref[...] += part

    row = pl.BlockSpec((tm, D_MODEL), lambda i: (i, 0))
    vec = pl.BlockSpec((1, D_MODEL), lambda i: (0, 0))
    return pl.pallas_call(
        body, name=name, grid=(t // tm,), in_specs=[row, vec, row, row], out_specs=[row, vec],
        out_shape=[jax.ShapeDtypeStruct((t, D_MODEL), F32), jax.ShapeDtypeStruct((1, D_MODEL), F32)],
        compiler_params=_params(("arbitrary",)),
    )(h, g, dn, dres)


def _branch_gate_fwd(name, oa, ob, wa, wb, gates):
    t = gates.shape[0]
    tm = _pick(t, (544, 384, 256, 128))

    def body(oa_ref, ob_ref, wa_ref, wb_ref, g_ref, o_ref, ya_ref, yb_ref):
        ya = lax.dot_general(_bf(oa_ref[...]), wa_ref[...], _NN, preferred_element_type=F32)
        yb = lax.dot_general(_bf(ob_ref[...]), wb_ref[...], _NN, preferred_element_type=F32)
        sa = jax.nn.sigmoid(g_ref[:, 0:D_MODEL])
        sb = jax.nn.sigmoid(g_ref[:, D_MODEL:2 * D_MODEL])
        o_ref[...] = (sa * ya + sb * yb).astype(BF16)
        ya_ref[...] = ya.astype(BF16)
        yb_ref[...] = yb.astype(BF16)

    blk = pl.BlockSpec((tm, D_MODEL), lambda i: (i, 0))
    narrow = pl.BlockSpec((tm, A_WIDTH), lambda i: (i, 0))
    wide = pl.BlockSpec((tm, 2 * D_MODEL), lambda i: (i, 0))
    wspec = pl.BlockSpec((A_WIDTH, D_MODEL), lambda i: (0, 0))
    out = jax.ShapeDtypeStruct((t, D_MODEL), BF16)
    return pl.pallas_call(
        body, name=name, grid=(t // tm,), in_specs=[narrow, narrow, wspec, wspec, wide], out_specs=[blk, blk, blk],
        out_shape=[out, out, out], compiler_params=_params(("parallel",)),
    )(oa, ob, wa, wb, gates)


def _mix_out_gate_bwd(name, dh, wo, gates, ya, yb):
    t = gates.shape[0]
    tm = _pick(t, (544, 384, 256, 128))

    def body(dh_ref, w_ref, g_ref, ya_ref, yb_ref, dya_ref, dyb_ref, dg_ref):
        dm = lax.dot_general(_bf(dh_ref[...]), w_ref[...], _NT, preferred_element_type=F32)
        sa = jax.nn.sigmoid(g_ref[:, 0:D_MODEL])
        sb = jax.nn.sigmoid(g_ref[:, D_MODEL:2 * D_MODEL])
        dya_ref[...] = (dm * sa).astype(BF16)
        dyb_ref[...] = (dm * sb).astype(BF16)
        dg_ref[:, 0:D_MODEL] = (dm * ya_ref[...].astype(F32) * (sa * (1.0 - sa))).astype(BF16)
        dg_ref[:, D_MODEL:2 * D_MODEL] = (dm * yb_ref[...].astype(F32) * (sb * (1.0 - sb))).astype(BF16)

    blk = pl.BlockSpec((tm, D_MODEL), lambda i: (i, 0))
    wide = pl.BlockSpec((tm, 2 * D_MODEL), lambda i: (i, 0))
    out = jax.ShapeDtypeStruct((t, D_MODEL), BF16)
    return pl.pallas_call(
        body, name=name, grid=(t // tm,),
        in_specs=[blk, pl.BlockSpec((D_MODEL, D_MODEL), lambda i: (0, 0)), wide, blk, blk], out_specs=[blk, blk, wide],
        out_shape=[out, out, jax.ShapeDtypeStruct((t, 2 * D_MODEL), BF16)], compiler_params=_params(("parallel",)),
    )(dh, wo, gates, ya, yb)


def _loss_head(name, h3, gf, tgt):
    b, l, _ = h3.shape
    nb = l // BLOCK

    def body(h_ref, g_ref, t_ref, dh_ref, loss_ref, dg_ref):
        first = (pl.program_id(0) == 0) & (pl.program_id(1) == 0)
        real = (pl.program_id(1) > 0).astype(F32)
        hv = h_ref[...]
        g = g_ref[...]
        r = lax.rsqrt(jnp.mean(hv * hv, axis=-1, keepdims=True) + EPS)
        xn = hv * r
        err = (xn * g - t_ref[...]) * real
        lpart = 0.5 * jnp.sum(jnp.mean(err * err, axis=-1, keepdims=True), axis=0, keepdims=True)
        dy = err * (1.0 / D_MODEL)
        tv = dy * g
        dot = jnp.mean(tv * hv, axis=-1, keepdims=True)
        dh_ref[...] = r * tv - hv * (r * r * r * dot)
        gpart = jnp.sum(dy * xn, axis=0, keepdims=True)

        @pl.when(first)
        def _():
            loss_ref[...] = jnp.zeros_like(loss_ref)
            dg_ref[...] = jnp.zeros_like(dg_ref)

        loss_ref[...] += jnp.broadcast_to(lpart, loss_ref.shape)
        dg_ref[...] += gpart

    return pl.pallas_call(
        body, name=name, grid=(b, nb),
        in_specs=[pl.BlockSpec((None, BLOCK, D_MODEL), lambda bi, n: (bi, n, 0)),
                  pl.BlockSpec((1, D_MODEL), lambda bi, n: (0, 0)),
                  pl.BlockSpec((None, BLOCK, D_MODEL), lambda bi, n: (bi, jnp.maximum(n - 1, 0), 0))],
        out_specs=[pl.BlockSpec((None, BLOCK, D_MODEL), lambda bi, n: (bi, n, 0)),
                   pl.BlockSpec((8, 128), lambda bi, n: (0, 0)),
                   pl.BlockSpec((1, D_MODEL), lambda bi, n: (0, 0))],
        out_shape=[jax.ShapeDtypeStruct(h3.shape, F32), jax.ShapeDtypeStruct((8, 128), F32),
                   jax.ShapeDtypeStruct((1, D_MODEL), F32)],
        compiler_params=_params(("arbitrary", "arbitrary")),
    )(h3, gf, tgt)


def _fgate_fwd(name, f3, bf_row):
    b, l, _ = f3.shape
    nb = l // BLOCK

    def body(f_ref, b_ref, cc_ref, cr_ref):
        r_i = lax.broadcasted_iota(jnp.int32, (BLOCK, BLOCK), 0)
        c_i = lax.broadcasted_iota(jnp.int32, (BLOCK, BLOCK), 1)
        tri = (r_i >= c_i).astype(F32)
        carry = jnp.zeros((1, 128), F32)
        for blk in range(nb):
            rows = slice(blk * BLOCK, (blk + 1) * BLOCK)
            z = f_ref[rows, :] + b_ref[...]
            lf = jnp.minimum(z, 0.0) - jnp.log(1.0 + jnp.exp(-jnp.abs(z)))
            cb = jnp.dot(tri, lf, preferred_element_type=F32, precision=lax.Precision.HIGHEST) + carry
            carry = cb[BLOCK - 1:BLOCK, :]
            cbt = cb.T
            for hh in range(B_HEADS):
                cc_ref[hh, rows, :] = jnp.sum(jnp.where(c_i == hh, cb, 0.0), axis=1, keepdims=True)
                cr_ref[hh, :, rows] = cbt[hh:hh + 1, :]

    return pl.pallas_call(
        body, name=name, grid=(b,),
        in_specs=[pl.BlockSpec((None, l, 128), lambda bi: (bi, 0, 0)),
                  pl.BlockSpec((1, 128), lambda bi: (0, 0))],
        out_specs=[pl.BlockSpec((None, B_HEADS, l, 1), lambda bi: (bi, 0, 0, 0)),
                   pl.BlockSpec((None, B_HEADS, 1, l), lambda bi: (bi, 0, 0, 0))],
        out_shape=[jax.ShapeDtypeStruct((b, B_HEADS, l, 1), F32), jax.ShapeDtypeStruct((b, B_HEADS, 1, l), F32)],
        compiler_params=_params(("parallel",)),
    )(f3, bf_row)


def _fgate_bwd(name, f3, bf_row, dcq, dck):
    b, l, _ = f3.shape
    nb = l // BLOCK

    def body(f_ref, b_ref, dcq_ref, dck_ref, df_ref, db_ref):
        r_i = lax.broadcasted_iota(jnp.int32, (BLOCK, BLOCK), 0)
        c_i = lax.broadcasted_iota(jnp.int32, (BLOCK, BLOCK), 1)
        tri = (r_i <= c_i).astype(F32)
        carry = jnp.zeros((1, 128), F32)
        total = jnp.zeros((1, 128), F32)
        for blk in range(nb - 1, -1, -1):
            rows = slice(blk * BLOCK, (blk + 1) * BLOCK)
            krows = jnp.concatenate([dck_ref[hh, :, rows] for hh in range(B_HEADS)]
                                    + [jnp.zeros((BLOCK - B_HEADS, BLOCK), F32)], axis=0)
            dcb = krows.T
            for hh in range(B_HEADS):
                dcb = dcb + jnp.where(c_i == hh, dcq_ref[hh, rows, :], 0.0)
            rc = jnp.dot(tri, dcb, preferred_element_type=F32, precision=lax.Precision.HIGHEST) + carry
            carry = rc[0:1, :]
            z = f_ref[rows, :] + b_ref[...]
            df = rc * (1.0 / (1.0 + jnp.exp(z)))
            df_ref[rows, :] = df.astype(BF16)
            total = total + jnp.sum(df, axis=0, keepdims=True)

        @pl.when(pl.program_id(0) == 0)
        def _():
            db_ref[...] = total

        @pl.when(pl.program_id(0) > 0)
        def _():
            db_ref[...] += total

    return pl.pallas_call(
        body, name=name, grid=(b,),
        in_specs=[pl.BlockSpec((None, l, 128), lambda bi: (bi, 0, 0)),
                  pl.BlockSpec((1, 128), lambda bi: (0, 0)),
                  pl.BlockSpec((None, B_HEADS, l, 1), lambda bi: (bi, 0, 0, 0)),
                  pl.BlockSpec((None, B_HEADS, 1, l), lambda bi: (bi, 0, 0, 0))],
        out_specs=[pl.BlockSpec((None, l, 128), lambda bi: (bi, 0, 0)), pl.BlockSpec((1, 128), lambda bi: (0, 0))],
        out_shape=[jax.ShapeDtypeStruct((b, l, 128), BF16), jax.ShapeDtypeStruct((1, 128), F32)],
        compiler_params=_params(("arbitrary",)),
    )(f3, bf_row, dcq, dck)


A_Q_BLK = B_SEG // A_WIDTH
A_K_BLK = (B_SEG + A_WIDTH) // 128
A_V_BLK = A_K_BLK + 1
A_SEG_BLK = B_SEG // A_SEG
STACK = A_HEADS * BLOCK


def _lane_lo():
    return lax.broadcasted_iota(jnp.int32, (1, 128), 1) < HEAD_DIM


def _stack_heads(x, masked):
    lo = _lane_lo()
    blks = [x[:, 128 * j:128 * (j + 1)] for j in range(4)]
    if not masked:
        return jnp.concatenate(blks + blks, axis=0)
    zero = jnp.zeros_like(blks[0])
    return jnp.concatenate([jnp.where(lo, bk, zero) for bk in blks] + [jnp.where(lo, zero, bk) for bk in blks], axis=0)


def _unstack_heads(y):
    lo = _lane_lo()
    return jnp.concatenate([jnp.where(lo, y[128 * j:128 * (j + 1)], y[128 * (4 + j):128 * (5 + j)]) for j in range(4)], axis=1)


def _swa_scores(q, kcat, n, slope):
    rows = STACK
    r_i = lax.broadcasted_iota(jnp.int32, (rows, 3 * BLOCK), 0)
    c_i = lax.broadcasted_iota(jnp.int32, (rows, 3 * BLOCK), 1)
    qpos = n * BLOCK + (r_i & (BLOCK - 1))
    seg = c_i >> 7
    kpos = jnp.where(seg == 0, c_i, (n - 2) * BLOCK + c_i)
    dist = qpos - kpos
    s = lax.dot_general(q, kcat, _NT, preferred_element_type=F32)
    s = s - slope * dist.astype(F32)
    is_meta = seg == 0
    band = jnp.logical_not(is_meta) & (dist < BLOCK) & (kpos >= PREFIX)
    meta = is_meta & (c_i >= N_PAD)
    return jnp.where((dist >= 0) & (band | meta), s, NEG)


def _swa_specs():
    def kv(col_blk):
        return [pl.BlockSpec((None, BLOCK, 128), lambda b, n: (b, 0, col_blk)),
                pl.BlockSpec((None, BLOCK, 128), lambda b, n: (b, jnp.maximum(n - 1, 0), col_blk)),
                pl.BlockSpec((None, BLOCK, 128), lambda b, n: (b, n, col_blk))]

    q_spec = pl.BlockSpec((None, BLOCK, A_WIDTH), lambda b, n: (b, n, A_Q_BLK))
    o_spec = pl.BlockSpec((None, BLOCK, A_WIDTH), lambda b, n: (b, n, 0))
    col = pl.BlockSpec((STACK, 1), lambda b, n: (0, 0))
    lse_spec = pl.BlockSpec((None, A_HEADS, BLOCK, 1), lambda b, n: (b, 0, n, 0))
    return q_spec, kv(A_K_BLK), kv(A_V_BLK), o_spec, col, lse_spec


def _swa_fwd(name, qkv, slopes, sinks):
    b, l, _ = qkv.shape
    nb = l // BLOCK

    def body(q_ref, k0_ref, kp_ref, kc_ref, v0_ref, vp_ref, vc_ref, sl_ref, sk_ref, o_ref, lse_ref):
        n = pl.program_id(1)
        qs = _stack_heads(q_ref[...], True) * SCALE
        kcat = jnp.concatenate([k0_ref[...], kp_ref[...], kc_ref[...]], axis=0)
        vcat = jnp.concatenate([v0_ref[...], vp_ref[...], vc_ref[...]], axis=0)
        s = _swa_scores(qs, kcat, n, sl_ref[...])
        sink = sk_ref[...]
        m = jnp.maximum(jnp.max(s, axis=-1, keepdims=True), sink)
        p = jnp.exp(s - m)
        den = jnp.sum(p, axis=-1, keepdims=True) + jnp.exp(sink - m)
        o = lax.dot_general(p.astype(BF16), vcat, _NN, preferred_element_type=F32) / den
        o_ref[...] = _unstack_heads(o)
        lse_ref[...] = (m + jnp.log(den)).reshape(A_HEADS, BLOCK, 1)

    q_spec, k_specs, v_specs, o_spec, col, lse_spec = _swa_specs()
    return pl.pallas_call(
        body, name=name, grid=(b, nb),
        in_specs=[q_spec] + k_specs + v_specs + [col, col], out_specs=[o_spec, lse_spec],
        out_shape=[jax.ShapeDtypeStruct((b, l, A_WIDTH), F32), jax.ShapeDtypeStruct((b, A_HEADS, l, 1), F32)],
        compiler_params=_params(("parallel", "parallel")),
    )(qkv, qkv, qkv, qkv, qkv, qkv, qkv, slopes, sinks)


def _swa_bwd(name, qkv, o, lse, do, slopes, sinks, dqkv):
    b, l, _ = qkv.shape
    nb = l // BLOCK

    def body(q_ref, k0_ref, kp_ref, kc_ref, v0_ref, vp_ref, vc_ref, o_ref, lse_ref, do_ref, sl_ref, sk_ref, _,
             dx_ref, ds_ref, dk_acc, dv_acc):
        bi = pl.program_id(0)
        n = pl.program_id(1)
        qs = _stack_heads(q_ref[...], True) * SCALE
        dos32 = _stack_heads(do_ref[...], True)
        dos = dos32.astype(BF16)
        os_ = _stack_heads(o_ref[...], False)
        lsev = lse_ref[...].reshape(STACK, 1)
        kcat = jnp.concatenate([k0_ref[...], kp_ref[...], kc_ref[...]], axis=0)
        vcat = jnp.concatenate([v0_ref[...], vp_ref[...], vc_ref[...]], axis=0)
        s = _swa_scores(qs, kcat, n, sl_ref[...])
        p = jnp.exp(s - lsev)
        dsum = jnp.sum(dos32 * os_, axis=-1, keepdims=True)
        dp = lax.dot_general(dos, vcat, _NT, preferred_element_type=F32)
        dsc = (p * (dp - dsum)).astype(BF16)
        dq = lax.dot_general(dsc, kcat, _NN, preferred_element_type=F32) * SCALE
        row0 = pl.multiple_of(n * BLOCK, BLOCK)
        dx_ref[pl.ds(row0, BLOCK), 0:A_WIDTH] = _unstack_heads(dq).astype(BF16)
        dkc = lax.dot_general(dsc, qs, _TN, preferred_element_type=F32)
        dvc = lax.dot_general(p.astype(BF16), dos, _TN, preferred_element_type=F32)

        @pl.when(n == 0)
        def _():
            dk_acc[...] = jnp.zeros_like(dk_acc)
            dv_acc[...] = jnp.zeros_like(dv_acc)

        starts = (0, pl.multiple_of(jnp.maximum(n - 1, 0) * BLOCK, BLOCK), row0)
        for t, st in enumerate(starts):
            dk_acc[pl.ds(st, BLOCK), :] += dkc[t * BLOCK:(t + 1) * BLOCK, :]
            dv_acc[pl.ds(st, BLOCK), :] += dvc[t * BLOCK:(t + 1) * BLOCK, :]

        @pl.when(n == nb - 1)
        def _():
            dx_ref[:, A_WIDTH:A_WIDTH + 128] = dk_acc[...].astype(BF16)
            dx_ref[:, A_WIDTH + 128:A_SEG] = dv_acc[...].astype(BF16)

        dsink = -(jnp.exp(sk_ref[...] - lsev) * dsum)
        r8 = lax.broadcasted_iota(jnp.int32, (8, 128), 0)
        acc = jnp.zeros((8, 128), F32)
        for hh in range(A_HEADS):
            acc = acc + jnp.where(r8 == hh, jnp.sum(dsink[hh * BLOCK:(hh + 1) * BLOCK, :]), 0.0)

        @pl.when((bi == 0) & (n == 0))
        def _():
            ds_ref[...] = jnp.zeros_like(ds_ref)

        ds_ref[...] += acc

    q_spec, k_specs, v_specs, o_spec, col, lse_spec = _swa_specs()
    return pl.pallas_call(
        body, name=name, grid=(b, nb),
        in_specs=[q_spec] + k_specs + v_specs + [o_spec, lse_spec, o_spec, col, col, pl.BlockSpec(memory_space=pl.ANY)],
        out_specs=[pl.BlockSpec((None, l, A_SEG), lambda bb, n: (bb, 0, A_SEG_BLK)),
                   pl.BlockSpec((8, 128), lambda bb, n: (0, 0))],
        out_shape=[jax.ShapeDtypeStruct(dqkv.shape, BF16), jax.ShapeDtypeStruct((8, 128), F32)],
        scratch_shapes=[pltpu.VMEM((l, 128), F32), pltpu.VMEM((l, 128), F32)],
        input_output_aliases={12: 0},
        compiler_params=_params(("arbitrary", "arbitrary")),
    )(qkv, qkv, qkv, qkv, qkv, qkv, qkv, o, lse, do, slopes, sinks, dqkv)


def _fox_scores(q, k, ck, i):
    kh = k.shape[0]
    s = lax.dot_general(q, k, _NT, preferred_element_type=F32) - ck
    qpos = i * BLOCK + lax.broadcasted_iota(jnp.int32, (BLOCK, kh), 0)
    kpos = lax.broadcasted_iota(jnp.int32, (BLOCK, kh), 1)
    return jnp.where((kpos <= qpos) & (kpos >= N_PAD), s, NEG)


def _pick_head(x, hh):
    lo = _lane_lo()
    return jnp.where(lo if hh == 0 else jnp.logical_not(lo), x, jnp.zeros_like(x))


def _fox_specs(l):
    pair = pl.BlockSpec((None, l, PAIR_W), lambda bi, hp: (bi, 0, hp))
    half = pl.BlockSpec((None, l, 128), lambda bi, hp: (bi, 0, hp))
    colv = pl.BlockSpec((None, 2, l, 1), lambda bi, hp: (bi, hp, 0, 0))
    rowv = pl.BlockSpec((None, 2, 1, l), lambda bi, hp: (bi, hp, 0, 0))
    return pair, half, colv, rowv


def _fox_fwd(name, qkv, c_col, c_row):
    b, l, _ = qkv.shape
    nb = l // BLOCK

    def body(x_ref, cc_ref, cr_ref, o_ref, lse_ref):
        for i in range(nb):
            rows = slice(i * BLOCK, (i + 1) * BLOCK)
            kh = (i + 1) * BLOCK
            qblk = x_ref[rows, 0:128]
            kv = x_ref[0:kh, 128:256]
            vv = x_ref[0:kh, 256:384]
            outs = []
            for hh in range(2):
                s = _fox_scores(_pick_head(qblk, hh) * SCALE, kv, cr_ref[hh, :, 0:kh], i)
                m = jnp.max(s, axis=-1, keepdims=True)
                p = jnp.exp(s - m)
                den = jnp.sum(p, axis=-1, keepdims=True)
                outs.append(lax.dot_general(p.astype(BF16), vv, _NN, preferred_element_type=F32) / den)
                lse_ref[hh, rows, :] = (m + jnp.log(den)) + cc_ref[hh, rows, :]
            o_ref[rows, :] = jnp.where(_lane_lo(), outs[0], outs[1]).astype(BF16)

    pair, half, colv, rowv = _fox_specs(l)
    return pl.pallas_call(
        body, name=name, grid=(b, 4), in_specs=[pair, colv, rowv], out_specs=[half, colv],
        out_shape=[jax.ShapeDtypeStruct((b, l, B_WIDTH), BF16), jax.ShapeDtypeStruct((b, B_HEADS, l, 1), F32)],
        compiler_params=_params(("parallel", "parallel")),
    )(qkv, c_col, c_row)


def _fox_bwd(name, qkv, c_col, c_row, o, lse, do):
    b, l, _ = qkv.shape
    nb = l // BLOCK

    def body(x_ref, cc_ref, cr_ref, o_ref, lse_ref, do_ref, dx_ref, dcq_ref, dck_ref, dk_acc, dv_acc):
        dk_acc[...] = jnp.zeros_like(dk_acc)
        dv_acc[...] = jnp.zeros_like(dv_acc)
        dck_ref[...] = jnp.zeros_like(dck_ref)
        for i in range(nb):
            rows = slice(i * BLOCK, (i + 1) * BLOCK)
            kh = (i + 1) * BLOCK
            qblk = x_ref[rows, 0:128]
            kv = x_ref[0:kh, 128:256]
            vv = x_ref[0:kh, 256:384]
            doblk = do_ref[rows, :]
            ov = o_ref[rows, :].astype(F32)
            dqs = []
            for hh in range(2):
                qm = _pick_head(qblk, hh) * SCALE
                dom = _pick_head(doblk, hh)
                s = _fox_scores(qm, kv, cr_ref[hh, :, 0:kh], i)
                p = jnp.exp(s - (lse_ref[hh, rows, :] - cc_ref[hh, rows, :]))
                dsum = jnp.sum(dom.astype(F32) * ov, axis=-1, keepdims=True)
                dp = lax.dot_general(dom, vv, _NT, preferred_element_type=F32)
                ds = p * (dp - dsum)
                dsc = ds.astype(BF16)
                dqs.append(lax.dot_general(dsc, kv, _NN, preferred_element_type=F32) * SCALE)
                dk_acc[0:kh, :] += lax.dot_general(dsc, qm, _TN, preferred_element_type=F32)
                dv_acc[0:kh, :] += lax.dot_general(p.astype(BF16), dom, _TN, preferred_element_type=F32)
                dcq_ref[hh, rows, :] = jnp.sum(ds, axis=-1, keepdims=True)
                dck_ref[hh, :, 0:kh] -= jnp.sum(ds, axis=0, keepdims=True)
            dx_ref[rows, 0:128] = jnp.where(_lane_lo(), dqs[0], dqs[1]).astype(BF16)
        dx_ref[:, 128:256] = dk_acc[...].astype(BF16)
        dx_ref[:, 256:384] = dv_acc[...].astype(BF16)

    pair, half, colv, rowv = _fox_specs(l)
    return pl.pallas_call(
        body, name=name, grid=(b, 4), in_specs=[pair, colv, rowv, half, colv, half],
        out_specs=[pair, colv, rowv],
        out_shape=[jax.ShapeDtypeStruct(qkv.shape, BF16), jax.ShapeDtypeStruct((b, B_HEADS, l, 1), F32),
                   jax.ShapeDtypeStruct((b, B_HEADS, 1, l), F32)],
        scratch_shapes=[pltpu.VMEM((l, 128), F32), pltpu.VMEM((l, 128), F32)],
        compiler_params=_params(("parallel", "parallel")),
    )(qkv, c_col, c_row, o, lse, do)


_FLIPS = ((0, 0, 1), (0, 1, 0), (0, 1, 1), (1, 0, 0), (1, 0, 1), (1, 1, 0), (1, 1, 1))


def _exchange(name, gather, scatter):
    ng, ns = len(gather), len(scatter)
    na = ng + ns
    npeer = len(_FLIPS)

    def body(*refs):
        ins = refs[:na]
        outs = refs[na:2 * na]
        send_sems, recv_sems, loc_sems = refs[2 * na:]
        x, y, c = lax.axis_index("x"), lax.axis_index("y"), lax.axis_index("c")
        me = 4 * x + 2 * y + c
        peers = []
        for fx, fy, fc in _FLIPS:
            px = 1 - x if fx else x
            py = 1 - y if fy else y
            pc = 1 - c if fc else c
            peers.append(((px, py, pc), 4 * px + 2 * py + pc))

        def remote(a, kk):
            dev, lin = peers[kk]
            src = ins[a] if a < ng else ins[a].at[lin]
            return pltpu.make_async_remote_copy(src_ref=src, dst_ref=outs[a].at[me], send_sem=send_sems.at[a * npeer + kk],
                                                recv_sem=recv_sems.at[a * npeer + kk], device_id=dev, device_id_type=MESH_ID)

        def arrival(a, kk):
            dev, lin = peers[kk]
            src = ins[a] if a < ng else ins[a].at[lin]
            return pltpu.make_async_remote_copy(src_ref=src, dst_ref=outs[a].at[lin], send_sem=send_sems.at[a * npeer + kk],
                                                recv_sem=recv_sems.at[a * npeer + kk], device_id=dev, device_id_type=MESH_ID)

        local = []
        for a in range(na):
            src = ins[a] if a < ng else ins[a].at[me]
            cp = pltpu.make_async_copy(src, outs[a].at[me], loc_sems.at[a])
            cp.start()
            local.append(cp)
        sent = [remote(a, kk) for kk in range(npeer) for a in range(na)]
        for cp in sent:
            cp.start()
        for kk in range(npeer):
            for a in range(na):
                arrival(a, kk).wait_recv()
        for cp in sent:
            cp.wait_send()
        for cp in local:
            cp.wait()

    arrs = list(gather) + list(scatter)
    out_shape = [jax.ShapeDtypeStruct((N_DEV,) + tuple(a.shape), a.dtype) for a in gather]
    out_shape += [jax.ShapeDtypeStruct(tuple(a.shape), a.dtype) for a in scatter]
    anyspec = pl.BlockSpec(memory_space=pl.ANY)
    return pl.pallas_call(
        body, name=name, in_specs=[anyspec] * na, out_specs=[anyspec] * na, out_shape=out_shape,
        scratch_shapes=[pltpu.SemaphoreType.DMA((na * npeer,)), pltpu.SemaphoreType.DMA((na * npeer,)),
                        pltpu.SemaphoreType.DMA((na,))],
        compiler_params=pltpu.CompilerParams(has_side_effects=True),
    )(*arrs)


def _peer_table():
    x, y, c = lax.axis_index("x"), lax.axis_index("y"), lax.axis_index("c")
    me = 4 * x + 2 * y + c
    peers = []
    for fx, fy, fc in _FLIPS:
        px = 1 - x if fx else x
        py = 1 - y if fy else y
        pc = 1 - c if fc else c
        peers.append(((px, py, pc), 4 * px + 2 * py + pc))
    return me, peers


_HBM = pl.BlockSpec(memory_space=pltpu.HBM)
_SEM = pl.BlockSpec(memory_space=pltpu.SEMAPHORE)
_ANY = pl.BlockSpec(memory_space=pl.ANY)
_EFFECT = pltpu.SideEffectType.DATAFLOW_SIDE_EFFECTING


def _split_copy(srcs_are_pieces, src_refs, land_refs, send_sem, recv_sem, a, kk, me, peers, arriving):
    dev, lin = peers[kk]
    npeer = len(_FLIPS)
    src = src_refs[a] if srcs_are_pieces[a] else src_refs[a].at[lin]
    dst = land_refs[a].at[lin] if arriving else land_refs[a].at[me]
    return pltpu.make_async_remote_copy(src_ref=src, dst_ref=dst, send_sem=send_sem.at[a * npeer + kk],
                                        recv_sem=recv_sem.at[a * npeer + kk], device_id=dev, device_id_type=MESH_ID)


def _xchg_start(name, gather, scatter, after=None):
    me_out = 4 * lax.axis_index("x") + 2 * lax.axis_index("y") + lax.axis_index("c")
    srcs = list(gather) + list(scatter)
    is_piece = [True] * len(gather) + [False] * len(scatter)
    lands = []
    for a, piece in zip(srcs, is_piece):
        own = a[None] if piece else lax.dynamic_slice_in_dim(a, me_out, 1, axis=0)
        shape = ((N_DEV,) + tuple(a.shape)) if piece else tuple(a.shape)
        start = (me_out,) + (0,) * (len(shape) - 1)
        lands.append(lax.dynamic_update_slice(lax.empty(shape, a.dtype), own, start))
    n = len(srcs)
    nsem = n * len(_FLIPS)
    has_after = after is not None

    def body(*refs):
        src_refs = refs[:n]
        land_refs = refs[n:2 * n]
        outs = refs[2 * n + (1 if has_after else 0):]
        send_sem, recv_sem = outs[0], outs[1]
        token = outs[-1]
        me, peers = _peer_table()
        for kk in range(len(_FLIPS)):
            for a in range(n):
                _split_copy(is_piece, src_refs, land_refs, send_sem, recv_sem, a, kk, me, peers, False).start()
        token[...] = jnp.zeros_like(token)

    out_shape = ([pltpu.SemaphoreType.DMA((nsem,)), pltpu.SemaphoreType.DMA((nsem,))]
                 + [pltpu.HBM(tuple(a.shape), a.dtype) for a in srcs] + [pltpu.HBM(tuple(a.shape), a.dtype) for a in lands]
                 + [jax.ShapeDtypeStruct((8, 128), F32)])
    args = [pltpu.with_memory_space_constraint(a, pltpu.HBM) for a in srcs + lands] + ([after] if has_after else [])
    res = pl.pallas_call(
        body, name=name, out_shape=out_shape,
        in_specs=[_HBM] * (2 * n) + ([_ANY] if has_after else []),
        out_specs=[_SEM, _SEM] + [_HBM] * (2 * n) + [pl.BlockSpec(memory_space=pltpu.VMEM)],
        input_output_aliases={i: 2 + i for i in range(2 * n)},
        compiler_params=pltpu.CompilerParams(has_side_effects=_EFFECT),
    )(*args)
    state = (res[0], res[1], list(res[2:2 + n]), list(res[2 + n:2 + 2 * n]), is_piece)
    return state, res[-1]


def _xchg_wait(name, state, after):
    send_sem, recv_sem, srcs, lands, is_piece = state
    n = len(srcs)

    def body(*refs):
        src_refs = refs[:n]
        land_refs = refs[n:2 * n]
        s_sem, r_sem = refs[2 * n], refs[2 * n + 1]
        me, peers = _peer_table()
        for kk in range(len(_FLIPS)):
            for a in range(n):
                cp = _split_copy(is_piece, src_refs, land_refs, s_sem, r_sem, a, kk, me, peers, True)
                cp.wait_send()
                cp.wait_recv()

    out_shape = [pltpu.HBM(tuple(a.shape), a.dtype) for a in srcs] + [pltpu.HBM(tuple(a.shape), a.dtype) for a in lands]
    res = pl.pallas_call(
        body, name=name, out_shape=out_shape,
        in_specs=[_HBM] * (2 * n) + [_SEM, _SEM, _ANY], out_specs=[_HBM] * (2 * n),
        input_output_aliases={i: i for i in range(2 * n)},
        compiler_params=pltpu.CompilerParams(has_side_effects=_EFFECT),
    )(*srcs, *lands, send_sem, recv_sem, after)
    return list(res[n:])


def _adam_math(w, g, m, v):
    m = ADAM_B1 * m + (1.0 - ADAM_B1) * g
    v = ADAM_B2 * v + (1.0 - ADAM_B2) * (g * g)
    m_hat = m / (1.0 - ADAM_B1 ** ADAM_STEP)
    v_hat = v / (1.0 - ADAM_B2 ** ADAM_STEP)
    delta = -ADAM_LR * (m_hat / (jnp.sqrt(v_hat) + ADAM_EPS) + ADAM_WD * w)
    return delta, m, v


def _adam(name, w, m, v, parts):
    r, c = w.shape
    npart, _, cp = parts.shape
    tr = _pick(r, (256, 176, 128, 64, 16, 8, 1))

    def body(w_ref, m_ref, v_ref, p_ref, g_ref, d_ref, mo_ref, vo_ref):
        g = p_ref[0].astype(F32)
        for pp in range(1, npart):
            g = g + p_ref[pp].astype(F32)
        g = g[:, 0:c]
        delta, mn, vn = _adam_math(w_ref[...], g, m_ref[...], v_ref[...])
        g_ref[...] = g
        d_ref[...] = delta
        mo_ref[...] = mn
        vo_ref[...] = vn

    blk = pl.BlockSpec((tr, c), lambda i: (i, 0))
    out = jax.ShapeDtypeStruct((r, c), F32)
    return pl.pallas_call(
        body, name=name, grid=(r // tr,),
        in_specs=[blk, blk, blk, pl.BlockSpec((npart, tr, cp), lambda i: (0, i, 0))],
        out_specs=[blk, blk, blk, blk], out_shape=[out, out, out, out], compiler_params=_params(("parallel",)),
    )(w, m, v, parts)


def _small_sum(name, packs):
    def body(p_ref, o_ref):
        tot = p_ref[0]
        for pp in range(1, N_DEV):
            tot = tot + p_ref[pp]
        o_ref[0:8, :] = tot[0:8, :]
        o_ref[8:24, :] = tot[8:24, :] + tot[24:40, :]

    return pl.pallas_call(body, name=name, out_shape=jax.ShapeDtypeStruct((24, D_MODEL), F32),
                          compiler_params=_params())(packs)


def _local_step(x, tgt, g1, gm, g2, gf, b_forget, sinks, weights, send):
    b, s, _ = x.shape
    l = s + PREFIX
    t = b * l
    w1i, meta = weights("ffn1_in", x)
    h0 = jnp.concatenate([jnp.zeros((b, N_PAD, D_MODEL), F32), jnp.broadcast_to(meta[None], (b, N_META, D_MODEL)), x],
                         axis=1).reshape(t, D_MODEL)

    n1 = _rms_fwd("rms1_fwd", h0, g1)
    gu1, a1 = _ffn_in_fwd("ffn1_in_fwd", n1, w1i)
    (w1o,) = weights("ffn1_out", a1)
    h1 = _mm_nn("ffn1_out_fwd", a1, w1o, alpha=0.5, res=h0, tm_c=(544, 384, 256, 128), tn_c=(1024,))
    wi, wa, wb, wo = weights("mix", h1)
    um = _rms_fwd("rmsm_fwd", h1, gm)
    qkv = _mm_nn("proj_qkv_fwd", um, wi, out_dtype=BF16, tn_c=(768,), cols=(0, QKV_W))
    gates = _mm_nn("proj_gates_fwd", um, wi, tn_c=(1024,), cols=(P_GATES, 2 * D_MODEL))
    f2 = _mm_nn("proj_f_fwd", um, wi, tn_c=(128,), cols=(P_F, 128))
    qkv3 = qkv.reshape(b, l, QKV_W)
    f3 = f2.reshape(b, l, 128)
    bf_row = jnp.pad(b_forget, ((0, 0), (0, 128 - B_HEADS)))
    c_col, c_row = _fgate_fwd("fgate_fwd", f3, bf_row)
    head_of_row = jnp.arange(STACK) // BLOCK
    slopes = jnp.exp2(-8.0 * (head_of_row + 1).astype(F32) / A_HEADS).reshape(STACK, 1)
    sink_rows = jnp.repeat(sinks.reshape(A_HEADS), BLOCK).reshape(STACK, 1)
    oa3, lse_a = _swa_fwd("swa_fwd", qkv3, slopes, sink_rows)
    ob3, lse_b = _fox_fwd("fox_fwd", qkv3, c_col, c_row)
    oa = oa3.reshape(t, A_WIDTH)
    ob = ob3.reshape(t, B_WIDTH)
    mixed, ya, yb = _branch_gate_fwd("branch_gate_fwd", oa, ob, wa, wb, gates)
    h2 = _mm_nn("mix_out_fwd", mixed, wo, res=h1, tn_c=(1024,))
    w2i, w2o = weights("ffn2", h2)
    n2 = _rms_fwd("rms2_fwd", h2, g2)
    gu2, a2 = _ffn_in_fwd("ffn2_in_fwd", n2, w2i)
    h3 = _mm_nn("ffn2_out_fwd", a2, w2o, alpha=0.5, res=h2, tm_c=(544, 384, 256, 128), tn_c=(1024,))

    dh3_3, loss_blk, dgf = _loss_head("loss_head", h3.reshape(b, l, D_MODEL), gf, tgt)
    dh3 = dh3_3.reshape(t, D_MODEL)

    def ffn_bwd(tag, dh, h_in, g_norm, n_in, gu, a, w_in_blk, w_out):
        dw_out = _mm_tn(tag + "_out_bwd_w", a, dh, alpha=0.5)
        dgu = _ffn_out_bwd_x(tag + "_out_bwd_x", dh, w_out, gu, dep=send(tag + "_out", (dw_out,)))
        dw_in = _ffn_in_bwd_w(tag + "_in_bwd_w", n_in, dgu)
        return _ffn_in_bwd_x(tag + "_in_bwd_x", dgu, w_in_blk, h_in, g_norm, dh, dep=send(tag + "_in", (dw_in,)))

    dh2, dg2 = ffn_bwd("ffn2", dh3, h2, g2, n2, gu2, a2, w2i, w2o)

    dwo = _mm_tn("mix_out_bwd_w", mixed, dh2)
    dya, dyb, dgates = _mix_out_gate_bwd("mix_out_gate_bwd", dh2, wo, gates, ya, yb)
    doa = _mm_nt("branch_a_bwd_x", dya, wa, tn_c=(512,))
    dob = _mm_nt("branch_b_bwd_x", dyb, wb, out_dtype=BF16, tn_c=(512,))
    dwa = _mm_tn("branch_a_bwd_w", oa, dya, tm_c=(512,))
    dwb = _mm_tn("branch_b_bwd_w", ob, dyb, tm_c=(512,))
    dqkv3, dcq, dck = _fox_bwd("fox_bwd", qkv3, c_col, c_row, ob3, lse_b, dob.reshape(b, l, B_WIDTH))
    dqkv3, dsink = _swa_bwd("swa_bwd", qkv3, oa3, lse_a, doa.reshape(b, l, A_WIDTH), slopes, sink_rows, dqkv3)
    dqkv = dqkv3.reshape(t, QKV_W)
    df3, dbf = _fgate_bwd("fgate_bwd", f3, bf_row, dcq, dck)
    df = df3.reshape(t, 128)
    dwi_qkv = _mm_tn("proj_qkv_bwd_w", um, dqkv, tm_c=(512,), tn_c=(768,))
    dwi_g = _mm_tn("proj_gates_bwd_w", um, dgates, tm_c=(512,), tn_c=(512,))
    dwi_f = _mm_tn("proj_f_bwd_w", um, df, tm_c=(512,), tn_c=(128,))
    token = send("mix", (dwi_qkv, dwi_g, dwi_f, dwa, dwb, dwo))
    dh1, dgm = _proj_bwd_x("proj_bwd_x", dqkv, dgates, df, wi, h1, gm, dh2, dep=token)

    dh0, dg1 = ffn_bwd("ffn1", dh1, h0, g1, n1, gu1, a1, w1i, w1o)
    dh0_3 = dh0.reshape(b, l, D_MODEL)
    grad_x = dh0_3[:, PREFIX:, :]
    dmeta = dh0_3[:, N_PAD:PREFIX, :].reshape(b * N_META, D_MODEL)

    misc = jnp.concatenate([dbf[:, 0:B_HEADS], dsink[:, 0].reshape(1, A_HEADS), loss_blk[0:1, 0:1]], axis=1)
    misc = jnp.pad(misc, ((0, 0), (0, D_MODEL - misc.shape[1])))
    row = lax.broadcasted_iota(jnp.int32, (8, D_MODEL), 0)
    vec = jnp.zeros((8, D_MODEL), F32)
    for i, piece in enumerate((dg1, dgm, dg2, dgf, misc)):
        vec = jnp.where(row == i, piece, vec)
    small = jnp.concatenate([vec, dmeta], axis=0)
    return grad_x, small


def _pad_to(a, rows, cols):
    return jnp.pad(a, ((0, rows - a.shape[0]), (0, cols - a.shape[1])))


def _ffn_out_from_gathered(g):
    w = g.reshape(4, FF_SHARD, D_MODEL)
    return jnp.pad(w, ((0, 0), (0, FF_SHARD_P - FF_SHARD), (0, 0))).reshape(D_FF_P, D_MODEL)


def _ffn_out_to_scatter(dw):
    return dw.reshape(4, FF_SHARD_P, D_MODEL)[:, 0:FF_SHARD, :].reshape(N_DEV, FFO_SHARD, D_MODEL)


def _proj_segments():
    segs = [(HEAD_DIM * h, HEAD_DIM, 0, B_SEG + HEAD_DIM * A_HEAD_ORDER.index(h)) for h in range(A_HEADS)]
    segs += [(512, 128, 0, B_SEG + A_WIDTH), (640, 128, 0, B_SEG + A_WIDTH + 128)]
    for first, off in ((768, 0), (1280, 128), (1792, 256)):
        segs += [(first + 128 * hp, 128, 0, PAIR_W * hp + off) for hp in range(4)]
    segs += [(2304, B_HEADS, 2, 0), (2312, 2 * D_MODEL, 1, 0)]
    return segs


def _proj_from_gathered(g):
    def cols(first, width):
        out = []
        for p in range(N_DEV):
            lo, hi = max(first, WIN_SHARD * p), min(first + width, WIN_SHARD * (p + 1))
            if lo < hi:
                out.append(g[p, :, lo - WIN_SHARD * p:hi - WIN_SHARD * p])
        return out

    parts = []
    for arr in (0, 1, 2):
        for first, width, _, _ in sorted((s for s in _proj_segments() if s[2] == arr), key=lambda s: s[3]):
            parts += cols(first, width)
        if arr == 0:
            parts.append(jnp.zeros((D_MODEL, P_GATES - QKV_W), g.dtype))
    parts.append(jnp.zeros((D_MODEL, 128 - B_HEADS), g.dtype))
    return jnp.concatenate(parts, axis=1)


def _proj_to_scatter(dqkv_w, dg_w, df_w):
    arrays = (dqkv_w, dg_w, df_w)
    segs = sorted(_proj_segments())
    blocks = []
    for p in range(N_DEV):
        parts = []
        for first, width, arr, at in segs:
            lo, hi = max(first, WIN_SHARD * p), min(first + width, WIN_SHARD * (p + 1))
            if lo < hi:
                parts.append(arrays[arr][:, at + lo - first:at + hi - first])
        parts.append(jnp.zeros((D_MODEL, WIN_SHARD_P - WIN_SHARD), dqkv_w.dtype))
        blocks.append(jnp.concatenate(parts, axis=1))
    return jnp.stack(blocks, axis=0)


def _a_rows_from_natural(w):
    return jnp.concatenate([w[HEAD_DIM * h:HEAD_DIM * (h + 1)] for h in A_HEAD_ORDER], axis=0)


def _a_rows_to_natural(w):
    return jnp.concatenate([w[HEAD_DIM * A_HEAD_ORDER.index(h):HEAD_DIM * (A_HEAD_ORDER.index(h) + 1)]
                            for h in range(A_HEADS)], axis=0)


def kernel(x, meta_tokens, ffn1_norm, ffn1_w_in, ffn1_w_out, mix_norm, w_in, b_forget, attn_sinks, w_branch_a, w_branch_b, w_out, ffn2_norm, ffn2_w_in, ffn2_w_out, final_norm, loss_target, m_meta_tokens, m_ffn1_norm, m_ffn1_w_in, m_ffn1_w_out, m_mix_norm, m_w_in, m_b_forget, m_attn_sinks, m_w_branch_a, m_w_branch_b, m_w_out, m_ffn2_norm, m_ffn2_w_in, m_ffn2_w_out, m_final_norm, v_meta_tokens, v_ffn1_norm, v_ffn1_w_in, v_ffn1_w_out, v_mix_norm, v_w_in, v_b_forget, v_attn_sinks, v_w_branch_a, v_w_branch_b, v_w_out, v_ffn2_norm, v_ffn2_w_in, v_ffn2_w_out, v_final_norm):
    me = 4 * lax.axis_index("x") + 2 * lax.axis_index("y") + lax.axis_index("c")

    shards = (
        _pad_to(ffn1_w_in[0].astype(BF16), D_MODEL, FF_SHARD_P),
        ffn1_w_out[0].astype(BF16),
        _pad_to(w_in[0].astype(BF16), D_MODEL, WIN_SHARD_P),
        w_branch_a[0].astype(BF16), w_branch_b[0].astype(BF16), w_out[0].astype(BF16),
        _pad_to(ffn2_w_in[0].astype(BF16), D_MODEL, FF_SHARD_P),
        ffn2_w_out[0].astype(BF16),
        meta_tokens,
    )
    s1i, s1o, swi, swa, swb, swo, s2i, s2o, smeta = shards
    gather_state = {}
    gather_state["ffn1_in"], tok = _xchg_start("gather_ffn1_in_start", (s1i, smeta), ())
    gather_state["ffn1_out"], tok = _xchg_start("gather_ffn1_out_start", (s1o,), (), after=tok)
    gather_state["mix"], tok = _xchg_start("gather_mix_start", (swi, swa, swb, swo), (), after=tok)
    gather_state["ffn2"], tok = _xchg_start("gather_ffn2_start", (s2i, s2o), (), after=tok)
    started = {"tok": tok}

    def weights(group, after):
        if group == "ffn1_in":
            after = started["tok"]
        got = _xchg_wait("gather_" + group + "_wait", gather_state[group], after)
        if group == "mix":
            gwi, gwa, gwb, gwo = got
            return (_proj_from_gathered(gwi), _a_rows_from_natural(gwa.transpose(1, 0, 2).reshape(A_WIDTH, D_MODEL)),
                    gwb.transpose(1, 0, 2).reshape(B_WIDTH, D_MODEL), gwo.reshape(D_MODEL, D_MODEL))
        if group == "ffn1_in":
            return got[0].reshape(2, 4, D_MODEL, FF_SHARD_P), got[1].transpose(1, 0, 2).reshape(N_META, D_MODEL)
        if group == "ffn1_out":
            return (_ffn_out_from_gathered(got[0]),)
        return got[0].reshape(2, 4, D_MODEL, FF_SHARD_P), _ffn_out_from_gathered(got[1])

    scatter_state = {}

    def send(group, grads):
        if group == "mix":
            dwi_qkv, dwi_g, dwi_f, dwa, dwb, dwo = grads
            dwa = _a_rows_to_natural(dwa)
            blocks = (_proj_to_scatter(dwi_qkv, dwi_g, dwi_f), dwa.reshape(A_WIDTH, N_DEV, 128).transpose(1, 0, 2),
                      dwb.reshape(B_WIDTH, N_DEV, 128).transpose(1, 0, 2), dwo.reshape(N_DEV, 128, D_MODEL))
        elif group.endswith("_in"):
            blocks = (grads[0].reshape(N_DEV, D_MODEL, FF_SHARD_P),)
        else:
            blocks = (_ffn_out_to_scatter(grads[0]),)
        scatter_state[group], token = _xchg_start("scatter_" + group + "_start", (), blocks)
        return token

    gf = final_norm.reshape(1, D_MODEL)
    grad_x, small = _local_step(x, loss_target, ffn1_norm, mix_norm, ffn2_norm, gf, b_forget, attn_sinks, weights, send)

    small_state, after = _xchg_start("gather_small_start", (small,), ())
    out = {}
    updates = (
        ("ffn2_out", (("ffn2_w_out", ffn2_w_out, m_ffn2_w_out, v_ffn2_w_out),)),
        ("ffn2_in", (("ffn2_w_in", ffn2_w_in, m_ffn2_w_in, v_ffn2_w_in),)),
        ("mix", (("w_in", w_in, m_w_in, v_w_in), ("w_branch_a", w_branch_a, m_w_branch_a, v_w_branch_a),
                 ("w_branch_b", w_branch_b, m_w_branch_b, v_w_branch_b), ("w_out", w_out, m_w_out, v_w_out))),
        ("ffn1_out", (("ffn1_w_out", ffn1_w_out, m_ffn1_w_out, v_ffn1_w_out),)),
        ("ffn1_in", (("ffn1_w_in", ffn1_w_in, m_ffn1_w_in, v_ffn1_w_in),)),
    )
    for group, members in updates:
        parts_list = _xchg_wait("scatter_" + group + "_wait", scatter_state[group], after)
        for (nm, w, m, v), parts in zip(members, parts_list):
            res4 = _adam("adam_" + nm, w[0], m[0], v[0], parts)
            out[nm] = tuple(r[None] for r in res4)
            after = res4[0]
    (packs,) = _xchg_wait("gather_small_wait", small_state, after)

    tot = _small_sum("small_sum", packs)
    loss = tot[4, 2 * B_HEADS]
    g_meta = lax.dynamic_slice(tot[8:24, :], (0, me * 128), (N_META, 128))
    out["meta_tokens"] = tuple(_adam("adam_meta_tokens", meta_tokens, m_meta_tokens, v_meta_tokens, g_meta[None]))

    def pack_small(n1, nm, n2, nf, bfv, skv):
        misc = jnp.pad(jnp.concatenate([bfv, skv], axis=1), ((0, 0), (0, D_MODEL - 2 * B_HEADS)))
        row = lax.broadcasted_iota(jnp.int32, (8, D_MODEL), 0)
        vec = jnp.zeros((8, D_MODEL), F32)
        for i, piece in enumerate((n1, nm, n2, nf.reshape(1, D_MODEL), misc)):
            vec = jnp.where(row == i, piece, vec)
        return vec

    w_pack = pack_small(ffn1_norm, mix_norm, ffn2_norm, final_norm, b_forget, attn_sinks)
    m_pack = pack_small(m_ffn1_norm, m_mix_norm, m_ffn2_norm, m_final_norm, m_b_forget, m_attn_sinks)
    v_pack = pack_small(v_ffn1_norm, v_mix_norm, v_ffn2_norm, v_final_norm, v_b_forget, v_attn_sinks)
    small4 = _adam("adam_small", w_pack, m_pack, v_pack, tot[0:8][None])
    for i, nm in enumerate(("ffn1_norm", "mix_norm", "ffn2_norm")):
        out[nm] = tuple(r[i:i + 1] for r in small4)
    out["final_norm"] = tuple(r[3] for r in small4)
    out["b_forget"] = tuple(r[4:5, 0:B_HEADS] for r in small4)
    out["attn_sinks"] = tuple(r[4:5, B_HEADS:2 * B_HEADS] for r in small4)

    names = ("meta_tokens", "ffn1_norm", "ffn1_w_in", "ffn1_w_out", "mix_norm", "w_in", "b_forget", "attn_sinks",
             "w_branch_a", "w_branch_b", "w_out", "ffn2_norm", "ffn2_w_in", "ffn2_w_out", "final_norm")
    return (loss, grad_x) + tuple(out[nm][kind] for kind in range(4) for nm in names)
```

```python
import jax
import jax.numpy as jnp
from jax import lax
from jax.experimental import pallas as pl
from jax.experimental.pallas import tpu as pltpu

F32 = jnp.float32
BF16 = jnp.bfloat16

D_MODEL = 1024
N_META = 16
BLOCK = 128
PREFIX = 128
N_PAD = PREFIX - N_META
HEAD_DIM = 64
A_HEADS = 8
A_KV_HEADS = 2
A_GROUP = 4
B_HEADS = 8
A_WIDTH = 512
A_KV_WIDTH = 128
B_WIDTH = 512
D_FF = 2816
N_DEV = 8
FF_SHARD = 2 * D_FF // N_DEV
FF_SHARD_P = 768
FFO_SHARD = D_FF // N_DEV
FFO_SHARD_P = FF_SHARD_P // 2
D_FF_P = 4 * FF_SHARD_P
W_IN_COLS = 4360
WIN_SHARD = W_IN_COLS // N_DEV
WIN_SHARD_P = 640
PAIR_W = 3 * 128
B_SEG = 4 * PAIR_W
A_SEG = A_WIDTH + 2 * A_KV_WIDTH
QKV_W = B_SEG + A_SEG
P_GATES = 2 * (2 * D_MODEL)
P_F = P_GATES + 2 * D_MODEL
PROJ_P = P_F + 128
A_HEAD_ORDER = (0, 4, 1, 5, 2, 6, 3, 7)
EPS = 1e-6
NEG = -1e30
SCALE = HEAD_DIM ** -0.5
ADAM_LR = 0.001
ADAM_B1 = 0.9
ADAM_B2 = 0.999
ADAM_EPS = 1e-08
ADAM_WD = 0.01
ADAM_STEP = 10
VMEM_LIMIT = 56 * 1024 * 1024
MESH_ID = pl.DeviceIdType.MESH
SMALL_ROWS = 40

_NN = (((1,), (0,)), ((), ()))
_NT = (((1,), (1,)), ((), ()))
_TN = (((0,), (0,)), ((), ()))


def _params(sem=None):
    return pltpu.CompilerParams(dimension_semantics=sem, vmem_limit_bytes=VMEM_LIMIT)


def _pick(n, cands):
    for c in cands:
        if n % c == 0:
            return c
    raise ValueError(f"no tile for {n}")


def _bf(v):
    return v if v.dtype == BF16 else v.astype(BF16)


def _mm(name, a, b, dims, grid, a_spec, b_spec, o_spec, out_shape, out_dtype, acc_shape, k_axis=None, nk=1,
        alpha=1.0, res=None, res_spec=None, dep=None):
    has_res = res is not None
    has_dep = dep is not None

    def body(*refs):
        a_ref, b_ref = refs[0], refs[1]
        r_ref = refs[2] if has_res else None
        o_ref = refs[2 + has_res + has_dep]

        def finish(acc):
            if alpha != 1.0:
                acc = acc * alpha
            if has_res:
                acc = acc + r_ref[...]
            o_ref[...] = acc.astype(o_ref.dtype)

        part = lax.dot_general(_bf(a_ref[...]), _bf(b_ref[...]), dims, preferred_element_type=F32)
        if nk == 1:
            finish(part)
        else:
            acc_ref = refs[-1]
            k = pl.program_id(k_axis)

            @pl.when(k == 0)
            def _():
                acc_ref[...] = part

            @pl.when(k > 0)
            def _():
                acc_ref[...] += part

            @pl.when(k == nk - 1)
            def _():
                finish(acc_ref[...])

    in_specs = [a_spec, b_spec] + ([res_spec] if has_res else []) + ([pl.BlockSpec(memory_space=pl.ANY)] if has_dep else [])
    args = (a, b) + ((res,) if has_res else ()) + ((dep,) if has_dep else ())
    sem = tuple("arbitrary" if (nk > 1 and i == k_axis) else "parallel" for i in range(len(grid)))
    return pl.pallas_call(
        body, name=name, grid=grid, in_specs=in_specs, out_specs=o_spec,
        out_shape=jax.ShapeDtypeStruct(out_shape, out_dtype),
        scratch_shapes=[pltpu.VMEM(acc_shape, F32)] if nk > 1 else [],
        compiler_params=_params(sem),
    )(*args)


def _mm_nn(name, a, b, out_dtype=F32, alpha=1.0, res=None, tn_c=(512, 640, 256, 128), cols=None,
           tm_c=(1088, 768, 512, 256, 128), dep=None):
    t, k = a.shape
    c0, n = (0, b.shape[1]) if cols is None else cols
    tm = _pick(t, tm_c)
    tn = _pick(n, tn_c)
    assert c0 % tn == 0
    jb = c0 // tn
    return _mm(name, a, b, _NN, (t // tm, n // tn),
               pl.BlockSpec((tm, k), lambda i, j: (i, 0)), pl.BlockSpec((k, tn), lambda i, j: (0, jb + j)),
               pl.BlockSpec((tm, tn), lambda i, j: (i, j)), (t, n), out_dtype, None,
               alpha=alpha, res=res, res_spec=pl.BlockSpec((tm, tn), lambda i, j: (i, j)), dep=dep)


def _mm_nt(name, a, b, out_dtype=F32, alpha=1.0, tn_c=(768, 512, 256, 128), tk_c=None, dep=None, res=None, kcols=None):
    t, k = a.shape
    n = b.shape[0]
    c0 = 0 if kcols is None else kcols[0]
    tm = _pick(t, (1088, 768, 512, 256, 128))
    tn = _pick(n, tn_c)
    tk = k if tk_c is None else _pick(k, tk_c)
    nk = k // tk
    assert c0 % tk == 0
    kb = c0 // tk
    return _mm(name, a, b, _NT, (t // tm, n // tn, nk),
               pl.BlockSpec((tm, tk), lambda i, j, kk: (i, kk)), pl.BlockSpec((tn, tk), lambda i, j, kk: (j, kb + kk)),
               pl.BlockSpec((tm, tn), lambda i, j, kk: (i, j)), (t, n), out_dtype, (tm, tn), k_axis=2, nk=nk, alpha=alpha,
               dep=dep, res=res, res_spec=pl.BlockSpec((tm, tn), lambda i, j, kk: (i, j)))


def _mm_tn(name, a, b, out_dtype=BF16, alpha=1.0, tm_c=(768, 512, 256, 128), tn_c=(512, 640, 256, 128)):
    t, m = a.shape
    n = b.shape[1]
    tm = _pick(m, tm_c)
    tn = _pick(n, tn_c)
    tk = t
    nk = 1
    return _mm(name, a, b, _TN, (m // tm, n // tn, nk),
               pl.BlockSpec((tk, tm), lambda i, j, kk: (kk, i)), pl.BlockSpec((tk, tn), lambda i, j, kk: (kk, j)),
               pl.BlockSpec((tm, tn), lambda i, j, kk: (i, j)), (m, n), out_dtype, (tm, tn), k_axis=2, nk=nk, alpha=alpha)


def _silu_grads(g, u, d):
    sg = jax.nn.sigmoid(g)
    return d * u * (sg * (1.0 + g * (1.0 - sg))), d * (g * sg)


def _ffn_in_fwd(name, n, wblk, dep=None):
    t = n.shape[0]
    tm = _pick(t, (1088, 768, 512, 256, 128))
    has_dep = dep is not None

    def body(n_ref, w_ref, *rest):
        gu_ref, a_ref = rest[-2], rest[-1]
        nv = n_ref[...]
        g = lax.dot_general(nv, w_ref[0], _NN, preferred_element_type=F32)
        u = lax.dot_general(nv, w_ref[1], _NN, preferred_element_type=F32)
        gu_ref[0] = g.astype(BF16)
        gu_ref[1] = u.astype(BF16)
        a_ref[...] = ((g * jax.nn.sigmoid(g)) * u).astype(BF16)

    return pl.pallas_call(
        body, name=name, grid=(t // tm, 4),
        in_specs=[pl.BlockSpec((tm, D_MODEL), lambda i, j: (i, 0)),
                  pl.BlockSpec((2, None, D_MODEL, FF_SHARD_P), lambda i, j: (0, j, 0, 0))]
        + ([pl.BlockSpec(memory_space=pl.ANY)] if has_dep else []),
        out_specs=[pl.BlockSpec((2, tm, FF_SHARD_P), lambda i, j: (0, i, j)),
                   pl.BlockSpec((tm, FF_SHARD_P), lambda i, j: (i, j))],
        out_shape=[jax.ShapeDtypeStruct((2, t, D_FF_P), BF16), jax.ShapeDtypeStruct((t, D_FF_P), BF16)],
        compiler_params=_params(("parallel", "parallel")),
    )(*((n, wblk) + ((dep,) if has_dep else ())))


def _ffn_out_bwd_x(name, dh, w_out, gu, dep=None):
    t = dh.shape[0]
    tm = _pick(t, (1088, 768, 512, 256, 128))
    has_dep = dep is not None

    def body(dh_ref, w_ref, gu_ref, *rest):
        o_ref = rest[-1]
        da = lax.dot_general(_bf(dh_ref[...]), w_ref[...], _NT, preferred_element_type=F32) * 0.5
        dg, du = _silu_grads(gu_ref[0].astype(F32), gu_ref[1].astype(F32), da)
        o_ref[0] = dg.astype(BF16)
        o_ref[1] = du.astype(BF16)

    gu_spec = pl.BlockSpec((2, tm, FF_SHARD_P), lambda i, j: (0, i, j))
    return pl.pallas_call(
        body, name=name, grid=(t // tm, 4),
        in_specs=[pl.BlockSpec((tm, D_MODEL), lambda i, j: (i, 0)), pl.BlockSpec((FF_SHARD_P, D_MODEL), lambda i, j: (j, 0)),
                  gu_spec] + ([pl.BlockSpec(memory_space=pl.ANY)] if has_dep else []),
        out_specs=gu_spec, out_shape=jax.ShapeDtypeStruct((2, t, D_FF_P), BF16),
        compiler_params=_params(("parallel", "parallel")),
    )(*((dh, w_out, gu) + ((dep,) if has_dep else ())))


def _rms_bwd_rows(dn, h, g, dres):
    r = lax.rsqrt(jnp.mean(h * h, axis=-1, keepdims=True) + EPS)
    tv = dn * g
    dot = jnp.mean(tv * h, axis=-1, keepdims=True)
    return dres + (r * tv - h * (r * r * r * dot)), jnp.sum(dn * (h * r), axis=0, keepdims=True)


def _accumulate_rows(ref, part, first):
    @pl.when(first)
    def _():
        ref[...] = part

    @pl.when(jnp.logical_not(first))
    def _():
        ref[...] += part


def _ffn_in_bwd_x(name, dgu, wblk, h_in, g_norm, dres, dep=None):
    t = dgu.shape[1]
    tm = _pick(t, (272, 256, 128))
    has_dep = dep is not None

    def body(d_ref, w_ref, h_ref, g_ref, r_ref, *rest):
        dh_ref, dg_ref = rest[-2], rest[-1]
        acc = None
        for s in range(2):
            for j in range(4):
                part = lax.dot_general(d_ref[s, :, FF_SHARD_P * j:FF_SHARD_P * (j + 1)], w_ref[s, j], _NT,
                                       preferred_element_type=F32)
                acc = part if acc is None else acc + part
        dh, dg = _rms_bwd_rows(acc, h_ref[...], g_ref[...], r_ref[...])
        dh_ref[...] = dh
        _accumulate_rows(dg_ref, dg, pl.program_id(0) == 0)

    row = pl.BlockSpec((tm, D_MODEL), lambda i: (i, 0))
    vec = pl.BlockSpec((1, D_MODEL), lambda i: (0, 0))
    return pl.pallas_call(
        body, name=name, grid=(t // tm,),
        in_specs=[pl.BlockSpec((2, tm, D_FF_P), lambda i: (0, i, 0)),
                  pl.BlockSpec((2, 4, D_MODEL, FF_SHARD_P), lambda i: (0, 0, 0, 0)), row, vec, row]
        + ([pl.BlockSpec(memory_space=pl.ANY)] if has_dep else []),
        out_specs=[row, vec],
        out_shape=[jax.ShapeDtypeStruct((t, D_MODEL), F32), jax.ShapeDtypeStruct((1, D_MODEL), F32)],
        compiler_params=_params(("arbitrary",)),
    )(*((dgu, wblk, h_in, g_norm, dres) + ((dep,) if has_dep else ())))


def _proj_bwd_x(name, dqkv, dgates, df, wi, h_in, g_norm, dres, dep=None):
    t = dqkv.shape[0]
    tm = _pick(t, (272, 256, 128))
    has_dep = dep is not None

    def body(q_ref, gt_ref, f_ref, w_ref, h_ref, g_ref, r_ref, *rest):
        dh_ref, dg_ref = rest[-2], rest[-1]
        acc = lax.dot_general(q_ref[...], w_ref[:, 0:QKV_W], _NT, preferred_element_type=F32)
        acc = acc + lax.dot_general(gt_ref[...], w_ref[:, P_GATES:P_F], _NT, preferred_element_type=F32)
        acc = acc + lax.dot_general(f_ref[...], w_ref[:, P_F:PROJ_P], _NT, preferred_element_type=F32)
        dh, dg = _rms_bwd_rows(acc, h_ref[...], g_ref[...], r_ref[...])
        dh_ref[...] = dh
        _accumulate_rows(dg_ref, dg, pl.program_id(0) == 0)

    row = pl.BlockSpec((tm, D_MODEL), lambda i: (i, 0))
    vec = pl.BlockSpec((1, D_MODEL), lambda i: (0, 0))
    return pl.pallas_call(
        body, name=name, grid=(t // tm,),
        in_specs=[pl.BlockSpec((tm, QKV_W), lambda i: (i, 0)), pl.BlockSpec((tm, 2 * D_MODEL), lambda i: (i, 0)),
                  pl.BlockSpec((tm, 128), lambda i: (i, 0)), pl.BlockSpec((D_MODEL, PROJ_P), lambda i: (0, 0)), row, vec, row]
        + ([pl.BlockSpec(memory_space=pl.ANY)] if has_dep else []),
        out_specs=[row, vec],
        out_shape=[jax.ShapeDtypeStruct((t, D_MODEL), F32), jax.ShapeDtypeStruct((1, D_MODEL), F32)],
        compiler_params=_params(("arbitrary",)),
    )(*((dqkv, dgates, df, wi, h_in, g_norm, dres) + ((dep,) if has_dep else ())))


def _ffn_in_bwd_w(name, n, dgu):
    t = n.shape[0]
    tk = t
    nk = 1
    return _mm(name, n, dgu, _TN, (2, 4, nk),
               pl.BlockSpec((tk, D_MODEL), lambda s, j, kk: (kk, 0)),
               pl.BlockSpec((None, tk, FF_SHARD_P), lambda s, j, kk: (s, kk, j)),
               pl.BlockSpec((None, None, D_MODEL, FF_SHARD_P), lambda s, j, kk: (s, j, 0, 0)),
               (2, 4, D_MODEL, FF_SHARD_P), BF16, (D_MODEL, FF_SHARD_P), k_axis=2, nk=nk)


def _rms_fwd(name, h, g):
    t = h.shape[0]
    tm = _pick(t, (544, 384, 256, 128))

    def body(h_ref, g_ref, o_ref):
        hv = h_ref[...]
        r = lax.rsqrt(jnp.mean(hv * hv, axis=-1, keepdims=True) + EPS)
        o_ref[...] = ((hv * r) * g_ref[...]).astype(BF16)

    return pl.pallas_call(
        body, name=name, grid=(t // tm,),
        in_specs=[pl.BlockSpec((tm, D_MODEL), lambda i: (i, 0)), pl.BlockSpec((1, D_MODEL), lambda i: (0, 0))],
        out_specs=pl.BlockSpec((tm, D_MODEL), lambda i: (i, 0)),
        out_shape=jax.ShapeDtypeStruct((t, D_MODEL), BF16), compiler_params=_params(("parallel",)),
    )(h, g)


def _rms_bwd(name, h, g, dn, dres):
    t = h.shape[0]
    tm = _pick(t, (544, 384, 256, 128))

    def body(h_ref, g_ref, dn_ref, dres_ref, dh_ref, dg_ref):
        i = pl.program_id(0)
        hv = h_ref[...]
        dnv = dn_ref[...]
        r = lax.rsqrt(jnp.mean(hv * hv, axis=-1, keepdims=True) + EPS)
        tv = dnv * g_ref[...]
        dot = jnp.mean(tv * hv, axis=-1, keepdims=True)
        dh_ref[...] = dres_ref[...] + (r * tv - hv * (r * r * r * dot))
        part = jnp.sum(dnv * (hv * r), axis=0, keepdims=True)

        @pl.when(i == 0)
        def _():
            dg_ref[...] = part

        @pl.when(i > 0)
        def _():
            dg_ref[...] += part

    row = pl.BlockSpec((tm, D_MODEL), lambda i: (i, 0))
    vec = pl.BlockSpec((1, D_MODEL), lambda i: (0, 0))
    return pl.pallas_call(
        body, name=name, grid=(t // tm,), in_specs=[row, vec, row, row], out_specs=[row, vec],
        out_shape=[jax.ShapeDtypeStruct((t, D_MODEL), F32), jax.ShapeDtypeStruct((1, D_MODEL), F32)],
        compiler_params=_params(("arbitrary",)),
    )(h, g, dn, dres)


def _branch_gate_fwd(name, oa, ob, wa, wb, gates):
    t = gates.shape[0]
    tm = _pick(t, (544, 384, 256, 128))

    def body(oa_ref, ob_ref, wa_ref, wb_ref, g_ref, o_ref, ya_ref, yb_ref):
        ya = lax.dot_general(_bf(oa_ref[...]), wa_ref[...], _NN, preferred_element_type=F32)
        yb = lax.dot_general(_bf(ob_ref[...]), wb_ref[...], _NN, preferred_element_type=F32)
        sa = jax.nn.sigmoid(g_ref[:, 0:D_MODEL])
        sb = jax.nn.sigmoid(g_ref[:, D_MODEL:2 * D_MODEL])
        o_ref[...] = (sa * ya + sb * yb).astype(BF16)
        ya_ref[...] = ya.astype(BF16)
        yb_ref[...] = yb.astype(BF16)

    blk = pl.BlockSpec((tm, D_MODEL), lambda i: (i, 0))
    narrow = pl.BlockSpec((tm, A_WIDTH), lambda i: (i, 0))
    wide = pl.BlockSpec((tm, 2 * D_MODEL), lambda i: (i, 0))
    wspec = pl.BlockSpec((A_WIDTH, D_MODEL), lambda i: (0, 0))
    out = jax.ShapeDtypeStruct((t, D_MODEL), BF16)
    return pl.pallas_call(
        body, name=name, grid=(t // tm,), in_specs=[narrow, narrow, wspec, wspec, wide], out_specs=[blk, blk, blk],
        out_shape=[out, out, out], compiler_params=_params(("parallel",)),
    )(oa, ob, wa, wb, gates)


def _mix_out_gate_bwd(name, dh, wo, gates, ya, yb):
    t = gates.shape[0]
    tm = _pick(t, (544, 384, 256, 128))

    def body(dh_ref, w_ref, g_ref, ya_ref, yb_ref, dya_ref, dyb_ref, dg_ref):
        dm = lax.dot_general(_bf(dh_ref[...]), w_ref[...], _NT, preferred_element_type=F32)
        sa = jax.nn.sigmoid(g_ref[:, 0:D_MODEL])
        sb = jax.nn.sigmoid(g_ref[:, D_MODEL:2 * D_MODEL])
        dya_ref[...] = (dm * sa).astype(BF16)
        dyb_ref[...] = (dm * sb).astype(BF16)
        dg_ref[:, 0:D_MODEL] = (dm * ya_ref[...].astype(F32) * (sa * (1.0 - sa))).astype(BF16)
        dg_ref[:, D_MODEL:2 * D_MODEL] = (dm * yb_ref[...].astype(F32) * (sb * (1.0 - sb))).astype(BF16)

    blk = pl.BlockSpec((tm, D_MODEL), lambda i: (i, 0))
    wide = pl.BlockSpec((tm, 2 * D_MODEL), lambda i: (i, 0))
    out = jax.ShapeDtypeStruct((t, D_MODEL), BF16)
    return pl.pallas_call(
        body, name=name, grid=(t // tm,),
        in_specs=[blk, pl.BlockSpec((D_MODEL, D_MODEL), lambda i: (0, 0)), wide, blk, blk], out_specs=[blk, blk, wide],
        out_shape=[out, out, jax.ShapeDtypeStruct((t, 2 * D_MODEL), BF16)], compiler_params=_params(("parallel",)),
    )(dh, wo, gates, ya, yb)


def _loss_head(name, h3, gf, tgt):
    b, l, _ = h3.shape
    nb = l // BLOCK

    def body(h_ref, g_ref, t_ref, dh_ref, loss_ref, dg_ref):
        first = (pl.program_id(0) == 0) & (pl.program_id(1) == 0)
        real = (pl.program_id(1) > 0).astype(F32)
        hv = h_ref[...]
        g = g_ref[...]
        r = lax.rsqrt(jnp.mean(hv * hv, axis=-1, keepdims=True) + EPS)
        xn = hv * r
        err = (xn * g - t_ref[...]) * real
        lpart = 0.5 * jnp.sum(jnp.mean(err * err, axis=-1, keepdims=True), axis=0, keepdims=True)
        dy = err * (1.0 / D_MODEL)
        tv = dy * g
        dot = jnp.mean(tv * hv, axis=-1, keepdims=True)
        dh_ref[...] = r * tv - hv * (r * r * r * dot)
        gpart = jnp.sum(dy * xn, axis=0, keepdims=True)

        @pl.when(first)
        def _():
            loss_ref[...] = jnp.zeros_like(loss_ref)
            dg_ref[...] = jnp.zeros_like(dg_ref)

        loss_ref[...] += jnp.broadcast_to(lpart, loss_ref.shape)
        dg_ref[...] += gpart

    return pl.pallas_call(
        body, name=name, grid=(b, nb),
        in_specs=[pl.BlockSpec((None, BLOCK, D_MODEL), lambda bi, n: (bi, n, 0)),
                  pl.BlockSpec((1, D_MODEL), lambda bi, n: (0, 0)),
                  pl.BlockSpec((None, BLOCK, D_MODEL), lambda bi, n: (bi, jnp.maximum(n - 1, 0), 0))],
        out_specs=[pl.BlockSpec((None, BLOCK, D_MODEL), lambda bi, n: (bi, n, 0)),
                   pl.BlockSpec((8, 128), lambda bi, n: (0, 0)),
                   pl.BlockSpec((1, D_MODEL), lambda bi, n: (0, 0))],
        out_shape=[jax.ShapeDtypeStruct(h3.shape, F32), jax.ShapeDtypeStruct((8, 128), F32),
                   jax.ShapeDtypeStruct((1, D_MODEL), F32)],
        compiler_params=_params(("arbitrary", "arbitrary")),
    )(h3, gf, tgt)


def _fgate_fwd(name, f3, bf_row):
    b, l, _ = f3.shape
    nb = l // BLOCK

    def body(f_ref, b_ref, cc_ref, cr_ref):
        r_i = lax.broadcasted_iota(jnp.int32, (BLOCK, BLOCK), 0)
        c_i = lax.broadcasted_iota(jnp.int32, (BLOCK, BLOCK), 1)
        tri = (r_i >= c_i).astype(F32)
        carry = jnp.zeros((1, 128), F32)
        for blk in range(nb):
            rows = slice(blk * BLOCK, (blk + 1) * BLOCK)
            z = f_ref[rows, :] + b_ref[...]
            lf = jnp.minimum(z, 0.0) - jnp.log(1.0 + jnp.exp(-jnp.abs(z)))
            cb = jnp.dot(tri, lf, preferred_element_type=F32, precision=lax.Precision.HIGHEST) + carry
            carry = cb[BLOCK - 1:BLOCK, :]
            cbt = cb.T
            for hh in range(B_HEADS):
                cc_ref[hh, rows, :] = jnp.sum(jnp.where(c_i == hh, cb, 0.0), axis=1, keepdims=True)
                cr_ref[hh, :, rows] = cbt[hh:hh + 1, :]

    return pl.pallas_call(
        body, name=name, grid=(b,),
        in_specs=[pl.BlockSpec((None, l, 128), lambda bi: (bi, 0, 0)),
                  pl.BlockSpec((1, 128), lambda bi: (0, 0))],
        out_specs=[pl.BlockSpec((None, B_HEADS, l, 1), lambda bi: (bi, 0, 0, 0)),
                   pl.BlockSpec((None, B_HEADS, 1, l), lambda bi: (bi, 0, 0, 0))],
        out_shape=[jax.ShapeDtypeStruct((b, B_HEADS, l, 1), F32), jax.ShapeDtypeStruct((b, B_HEADS, 1, l), F32)],
        compiler_params=_params(("parallel",)),
    )(f3, bf_row)


def _fgate_bwd(name, f3, bf_row, dcq, dck):
    b, l, _ = f3.shape
    nb = l // BLOCK

    def body(f_ref, b_ref, dcq_ref, dck_ref, df_ref, db_ref):
        r_i = lax.broadcasted_iota(jnp.int32, (BLOCK, BLOCK), 0)
        c_i = lax.broadcasted_iota(jnp.int32, (BLOCK, BLOCK), 1)
        tri = (r_i <= c_i).astype(F32)
        carry = jnp.zeros((1, 128), F32)
        total = jnp.zeros((1, 128), F32)
        for blk in range(nb - 1, -1, -1):
            rows = slice(blk * BLOCK, (blk + 1) * BLOCK)
            krows = jnp.concatenate([dck_ref[hh, :, rows] for hh in range(B_HEADS)]
                                    + [jnp.zeros((BLOCK - B_HEADS, BLOCK), F32)], axis=0)
            dcb = krows.T
            for hh in range(B_HEADS):
                dcb = dcb + jnp.where(c_i == hh, dcq_ref[hh, rows, :], 0.0)
            rc = jnp.dot(tri, dcb, preferred_element_type=F32, precision=lax.Precision.HIGHEST) + carry
            carry = rc[0:1, :]
            z = f_ref[rows, :] + b_ref[...]
            df = rc * (1.0 / (1.0 + jnp.exp(z)))
            df_ref[rows, :] = df.astype(BF16)
            total = total + jnp.sum(df, axis=0, keepdims=True)

        @pl.when(pl.program_id(0) == 0)
        def _():
            db_ref[...] = total

        @pl.when(pl.program_id(0) > 0)
        def _():
            db_ref[...] += total

    return pl.pallas_call(
        body, name=name, grid=(b,),
        in_specs=[pl.BlockSpec((None, l, 128), lambda bi: (bi, 0, 0)),
                  pl.BlockSpec((1, 128), lambda bi: (0, 0)),
                  pl.BlockSpec((None, B_HEADS, l, 1), lambda bi: (bi, 0, 0, 0)),
                  pl.BlockSpec((None, B_HEADS, 1, l), lambda bi: (bi, 0, 0, 0))],
        out_specs=[pl.BlockSpec((None, l, 128), lambda bi: (bi, 0, 0)), pl.BlockSpec((1, 128), lambda bi: (0, 0))],
        out_shape=[jax.ShapeDtypeStruct((b, l, 128), BF16), jax.ShapeDtypeStruct((1, 128), F32)],
        compiler_params=_params(("arbitrary",)),
    )(f3, bf_row, dcq, dck)


A_Q_BLK = B_SEG // A_WIDTH
A_K_BLK = (B_SEG + A_WIDTH) // 128
A_V_BLK = A_K_BLK + 1
A_SEG_BLK = B_SEG // A_SEG
STACK = A_HEADS * BLOCK


def _lane_lo():
    return lax.broadcasted_iota(jnp.int32, (1, 128), 1) < HEAD_DIM


def _stack_heads(x, masked):
    lo = _lane_lo()
    blks = [x[:, 128 * j:128 * (j + 1)] for j in range(4)]
    if not masked:
        return jnp.concatenate(blks + blks, axis=0)
    zero = jnp.zeros_like(blks[0])
    return jnp.concatenate([jnp.where(lo, bk, zero) for bk in blks] + [jnp.where(lo, zero, bk) for bk in blks], axis=0)


def _unstack_heads(y):
    lo = _lane_lo()
    return jnp.concatenate([jnp.where(lo, y[128 * j:128 * (j + 1)], y[128 * (4 + j):128 * (5 + j)]) for j in range(4)], axis=1)


def _swa_scores(q, kcat, n, slope):
    rows = STACK
    r_i = lax.broadcasted_iota(jnp.int32, (rows, 3 * BLOCK), 0)
    c_i = lax.broadcasted_iota(jnp.int32, (rows, 3 * BLOCK), 1)
    qpos = n * BLOCK + (r_i & (BLOCK - 1))
    seg = c_i >> 7
    kpos = jnp.where(seg == 0, c_i, (n - 2) * BLOCK + c_i)
    dist = qpos - kpos
    s = lax.dot_general(q, kcat, _NT, preferred_element_type=F32)
    s = s - slope * dist.astype(F32)
    is_meta = seg == 0
    band = jnp.logical_not(is_meta) & (dist < BLOCK) & (kpos >= PREFIX)
    meta = is_meta & (c_i >= N_PAD)
    return jnp.where((dist >= 0) & (band | meta), s, NEG)


def _swa_specs():
    def kv(col_blk):
        return [pl.BlockSpec((None, BLOCK, 128), lambda b, n: (b, 0, col_blk)),
                pl.BlockSpec((None, BLOCK, 128), lambda b, n: (b, jnp.maximum(n - 1, 0), col_blk)),
                pl.BlockSpec((None, BLOCK, 128), lambda b, n: (b, n, col_blk))]

    q_spec = pl.BlockSpec((None, BLOCK, A_WIDTH), lambda b, n: (b, n, A_Q_BLK))
    o_spec = pl.BlockSpec((None, BLOCK, A_WIDTH), lambda b, n: (b, n, 0))
    col = pl.BlockSpec((STACK, 1), lambda b, n: (0, 0))
    lse_spec = pl.BlockSpec((None, A_HEADS, BLOCK, 1), lambda b, n: (b, 0, n, 0))
    return q_spec, kv(A_K_BLK), kv(A_V_BLK), o_spec, col, lse_spec


def _swa_fwd(name, qkv, slopes, sinks):
    b, l, _ = qkv.shape
    nb = l // BLOCK

    def body(q_ref, k0_ref, kp_ref, kc_ref, v0_ref, vp_ref, vc_ref, sl_ref, sk_ref, o_ref, lse_ref):
        n = pl.program_id(1)
        qs = _stack_heads(q_ref[...], True) * SCALE
        kcat = jnp.concatenate([k0_ref[...], kp_ref[...], kc_ref[...]], axis=0)
        vcat = jnp.concatenate([v0_ref[...], vp_ref[...], vc_ref[...]], axis=0)
        s = _swa_scores(qs, kcat, n, sl_ref[...])
        sink = sk_ref[...]
        m = jnp.maximum(jnp.max(s, axis=-1, keepdims=True), sink)
        p = jnp.exp(s - m)
        den = jnp.sum(p, axis=-1, keepdims=True) + jnp.exp(sink - m)
        o = lax.dot_general(p.astype(BF16), vcat, _NN, preferred_element_type=F32) / den
        o_ref[...] = _unstack_heads(o)
        lse_ref[...] = (m + jnp.log(den)).reshape(A_HEADS, BLOCK, 1)

    q_spec, k_specs, v_specs, o_spec, col, lse_spec = _swa_specs()
    return pl.pallas_call(
        body, name=name, grid=(b, nb),
        in_specs=[q_spec] + k_specs + v_specs + [col, col], out_specs=[o_spec, lse_spec],
        out_shape=[jax.ShapeDtypeStruct((b, l, A_WIDTH), F32), jax.ShapeDtypeStruct((b, A_HEADS, l, 1), F32)],
        compiler_params=_params(("parallel", "parallel")),
    )(qkv, qkv, qkv, qkv, qkv, qkv, qkv, slopes, sinks)


def _swa_bwd(name, qkv, o, lse, do, slopes, sinks, dqkv):
    b, l, _ = qkv.shape
    nb = l // BLOCK

    def body(q_ref, k0_ref, kp_ref, kc_ref, v0_ref, vp_ref, vc_ref, o_ref, lse_ref, do_ref, sl_ref, sk_ref, _,
             dx_ref, ds_ref, dk_acc, dv_acc):
        bi = pl.program_id(0)
        n = pl.program_id(1)
        qs = _stack_heads(q_ref[...], True) * SCALE
        dos32 = _stack_heads(do_ref[...], True)
        dos = dos32.astype(BF16)
        os_ = _stack_heads(o_ref[...], False)
        lsev = lse_ref[...].reshape(STACK, 1)
        kcat = jnp.concatenate([k0_ref[...], kp_ref[...], kc_ref[...]], axis=0)
        vcat = jnp.concatenate([v0_ref[...], vp_ref[...], vc_ref[...]], axis=0)
        s = _swa_scores(qs, kcat, n, sl_ref[...])
        p = jnp.exp(s - lsev)
        dsum = jnp.sum(dos32 * os_, axis=-1, keepdims=True)
        dp = lax.dot_general(dos, vcat, _NT, preferred_element_type=F32)
        dsc = (p * (dp - dsum)).astype(BF16)
        dq = lax.dot_general(dsc, kcat, _NN, preferred_element_type=F32) * SCALE
        row0 = pl.multiple_of(n * BLOCK, BLOCK)
        dx_ref[pl.ds(row0, BLOCK), 0:A_WIDTH] = _unstack_heads(dq).astype(BF16)
        dkc = lax.dot_general(dsc, qs, _TN, preferred_element_type=F32)
        dvc = lax.dot_general(p.astype(BF16), dos, _TN, preferred_element_type=F32)

        @pl.when(n == 0)
        def _():
            dk_acc[...] = jnp.zeros_like(dk_acc)
            dv_acc[...] = jnp.zeros_like(dv_acc)

        starts = (0, pl.multiple_of(jnp.maximum(n - 1, 0) * BLOCK, BLOCK), row0)
        for t, st in enumerate(starts):
            dk_acc[pl.ds(st, BLOCK), :] += dkc[t * BLOCK:(t + 1) * BLOCK, :]
            dv_acc[pl.ds(st, BLOCK), :] += dvc[t * BLOCK:(t + 1) * BLOCK, :]

        @pl.when(n == nb - 1)
        def _():
            dx_ref[:, A_WIDTH:A_WIDTH + 128] = dk_acc[...].astype(BF16)
            dx_ref[:, A_WIDTH + 128:A_SEG] = dv_acc[...].astype(BF16)

        dsink = -(jnp.exp(sk_ref[...] - lsev) * dsum)
        r8 = lax.broadcasted_iota(jnp.int32, (8, 128), 0)
        acc = jnp.zeros((8, 128), F32)
        for hh in range(A_HEADS):
            acc = acc + jnp.where(r8 == hh, jnp.sum(dsink[hh * BLOCK:(hh + 1) * BLOCK, :]), 0.0)

        @pl.when((bi == 0) & (n == 0))
        def _():
            ds_ref[...] = jnp.zeros_like(ds_ref)

        ds_ref[...] += acc

    q_spec, k_specs, v_specs, o_spec, col, lse_spec = _swa_specs()
    return pl.pallas_call(
        body, name=name, grid=(b, nb),
        in_specs=[q_spec] + k_specs + v_specs + [o_spec, lse_spec, o_spec, col, col, pl.BlockSpec(memory_space=pl.ANY)],
        out_specs=[pl.BlockSpec((None, l, A_SEG), lambda bb, n: (bb, 0, A_SEG_BLK)),
                   pl.BlockSpec((8, 128), lambda bb, n: (0, 0))],
        out_shape=[jax.ShapeDtypeStruct(dqkv.shape, BF16), jax.ShapeDtypeStruct((8, 128), F32)],
        scratch_shapes=[pltpu.VMEM((l, 128), F32), pltpu.VMEM((l, 128), F32)],
        input_output_aliases={12: 0},
        compiler_params=_params(("arbitrary", "arbitrary")),
    )(qkv, qkv, qkv, qkv, qkv, qkv, qkv, o, lse, do, slopes, sinks, dqkv)


def _fox_scores(q, k, ck, i):
    kh = k.shape[0]
    s = lax.dot_general(q, k, _NT, preferred_element_type=F32) - ck
    qpos = i * BLOCK + lax.broadcasted_iota(jnp.int32, (BLOCK, kh), 0)
    kpos = lax.broadcasted_iota(jnp.int32, (BLOCK, kh), 1)
    return jnp.where((kpos <= qpos) & (kpos >= N_PAD), s, NEG)


def _pick_head(x, hh):
    lo = _lane_lo()
    return jnp.where(lo if hh == 0 else jnp.logical_not(lo), x, jnp.zeros_like(x))


def _fox_specs(l):
    pair = pl.BlockSpec((None, l, PAIR_W), lambda bi, hp: (bi, 0, hp))
    half = pl.BlockSpec((None, l, 128), lambda bi, hp: (bi, 0, hp))
    colv = pl.BlockSpec((None, 2, l, 1), lambda bi, hp: (bi, hp, 0, 0))
    rowv = pl.BlockSpec((None, 2, 1, l), lambda bi, hp: (bi, hp, 0, 0))
    return pair, half, colv, rowv


def _fox_fwd(name, qkv, c_col, c_row):
    b, l, _ = qkv.shape
    nb = l // BLOCK

    def body(x_ref, cc_ref, cr_ref, o_ref, lse_ref):
        for i in range(nb):
            rows = slice(i * BLOCK, (i + 1) * BLOCK)
            kh = (i + 1) * BLOCK
            qblk = x_ref[rows, 0:128]
            kv = x_ref[0:kh, 128:256]
            vv = x_ref[0:kh, 256:384]
            outs = []
            for hh in range(2):
                s = _fox_scores(_pick_head(qblk, hh) * SCALE, kv, cr_ref[hh, :, 0:kh], i)
                m = jnp.max(s, axis=-1, keepdims=True)
                p = jnp.exp(s - m)
                den = jnp.sum(p, axis=-1, keepdims=True)
                outs.append(lax.dot_general(p.astype(BF16), vv, _NN, preferred_element_type=F32) / den)
                lse_ref[hh, rows, :] = (m + jnp.log(den)) + cc_ref[hh, rows, :]
            o_ref[rows, :] = jnp.where(_lane_lo(), outs[0], outs[1]).astype(BF16)

    pair, half, colv, rowv = _fox_specs(l)
    return pl.pallas_call(
        body, name=name, grid=(b, 4), in_specs=[pair, colv, rowv], out_specs=[half, colv],
        out_shape=[jax.ShapeDtypeStruct((b, l, B_WIDTH), BF16), jax.ShapeDtypeStruct((b, B_HEADS, l, 1), F32)],
        compiler_params=_params(("parallel", "parallel")),
    )(qkv, c_col, c_row)


def _fox_bwd(name, qkv, c_col, c_row, o, lse, do):
    b, l, _ = qkv.shape
    nb = l // BLOCK

    def body(x_ref, cc_ref, cr_ref, o_ref, lse_ref, do_ref, dx_ref, dcq_ref, dck_ref, dk_acc, dv_acc):
        dk_acc[...] = jnp.zeros_like(dk_acc)
        dv_acc[...] = jnp.zeros_like(dv_acc)
        dck_ref[...] = jnp.zeros_like(dck_ref)
        for i in range(nb):
            rows = slice(i * BLOCK, (i + 1) * BLOCK)
            kh = (i + 1) * BLOCK
            qblk = x_ref[rows, 0:128]
            kv = x_ref[0:kh, 128:256]
            vv = x_ref[0:kh, 256:384]
            doblk = do_ref[rows, :]
            ov = o_ref[rows, :].astype(F32)
            dqs = []
            for hh in range(2):
                qm = _pick_head(qblk, hh) * SCALE
                dom = _pick_head(doblk, hh)
                s = _fox_scores(qm, kv, cr_ref[hh, :, 0:kh], i)
                p = jnp.exp(s - (lse_ref[hh, rows, :] - cc_ref[hh, rows, :]))
                dsum = jnp.sum(dom.astype(F32) * ov, axis=-1, keepdims=True)
                dp = lax.dot_general(dom, vv, _NT, preferred_element_type=F32)
                ds = p * (dp - dsum)
                dsc = ds.astype(BF16)
                dqs.append(lax.dot_general(dsc, kv, _NN, preferred_element_type=F32) * SCALE)
                dk_acc[0:kh, :] += lax.dot_general(dsc, qm, _TN, preferred_element_type=F32)
                dv_acc[0:kh, :] += lax.dot_general(p.astype(BF16), dom, _TN, preferred_element_type=F32)
                dcq_ref[hh, rows, :] = jnp.sum(ds, axis=-1, keepdims=True)
                dck_ref[hh, :, 0:kh] -= jnp.sum(ds, axis=0, keepdims=True)
            dx_ref[rows, 0:128] = jnp.where(_lane_lo(), dqs[0], dqs[1]).astype(BF16)
        dx_ref[:, 128:256] = dk_acc[...].astype(BF16)
        dx_ref[:, 256:384] = dv_acc[...].astype(BF16)

    pair, half, colv, rowv = _fox_specs(l)
    return pl.pallas_call(
        body, name=name, grid=(b, 4), in_specs=[pair, colv, rowv, half, colv, half],
        out_specs=[pair, colv, rowv],
        out_shape=[jax.ShapeDtypeStruct(qkv.shape, BF16), jax.ShapeDtypeStruct((b, B_HEADS, l, 1), F32),
                   jax.ShapeDtypeStruct((b, B_HEADS, 1, l), F32)],
        scratch_shapes=[pltpu.VMEM((l, 128), F32), pltpu.VMEM((l, 128), F32)],
        compiler_params=_params(("parallel", "parallel")),
    )(qkv, c_col, c_row, o, lse, do)


_FLIPS = ((0, 0, 1), (0, 1, 0), (0, 1, 1), (1, 0, 0), (1, 0, 1), (1, 1, 0), (1, 1, 1))


def _exchange(name, gather, scatter):
    ng, ns = len(gather), len(scatter)
    na = ng + ns
    npeer = len(_FLIPS)

    def body(*refs):
        ins = refs[:na]
        outs = refs[na:2 * na]
        send_sems, recv_sems, loc_sems = refs[2 * na:]
        x, y, c = lax.axis_index("x"), lax.axis_index("y"), lax.axis_index("c")
        me = 4 * x + 2 * y + c
        peers = []
        for fx, fy, fc in _FLIPS:
            px = 1 - x if fx else x
            py = 1 - y if fy else y
            pc = 1 - c if fc else c
            peers.append(((px, py, pc), 4 * px + 2 * py + pc))

        def remote(a, kk):
            dev, lin = peers[kk]
            src = ins[a] if a < ng else ins[a].at[lin]
            return pltpu.make_async_remote_copy(src_ref=src, dst_ref=outs[a].at[me], send_sem=send_sems.at[a * npeer + kk],
                                                recv_sem=recv_sems.at[a * npeer + kk], device_id=dev, device_id_type=MESH_ID)

        def arrival(a, kk):
            dev, lin = peers[kk]
            src = ins[a] if a < ng else ins[a].at[lin]
            return pltpu.make_async_remote_copy(src_ref=src, dst_ref=outs[a].at[lin], send_sem=send_sems.at[a * npeer + kk],
                                                recv_sem=recv_sems.at[a * npeer + kk], device_id=dev, device_id_type=MESH_ID)

        local = []
        for a in range(na):
            src = ins[a] if a < ng else ins[a].at[me]
            cp = pltpu.make_async_copy(src, outs[a].at[me], loc_sems.at[a])
            cp.start()
            local.append(cp)
        sent = [remote(a, kk) for kk in range(npeer) for a in range(na)]
        for cp in sent:
            cp.start()
        for kk in range(npeer):
            for a in range(na):
                arrival(a, kk).wait_recv()
        for cp in sent:
            cp.wait_send()
        for cp in local:
            cp.wait()

    arrs = list(gather) + list(scatter)
    out_shape = [jax.ShapeDtypeStruct((N_DEV,) + tuple(a.shape), a.dtype) for a in gather]
    out_shape += [jax.ShapeDtypeStruct(tuple(a.shape), a.dtype) for a in scatter]
    anyspec = pl.BlockSpec(memory_space=pl.ANY)
    return pl.pallas_call(
        body, name=name, in_specs=[anyspec] * na, out_specs=[anyspec] * na, out_shape=out_shape,
        scratch_shapes=[pltpu.SemaphoreType.DMA((na * npeer,)), pltpu.SemaphoreType.DMA((na * npeer,)),
                        pltpu.SemaphoreType.DMA((na,))],
        compiler_params=pltpu.CompilerParams(has_side_effects=True),
    )(*arrs)


def _peer_table():
    x, y, c = lax.axis_index("x"), lax.axis_index("y"), lax.axis_index("c")
    me = 4 * x + 2 * y + c
    peers = []
    for fx, fy, fc in _FLIPS:
        px = 1 - x if fx else x
        py = 1 - y if fy else y
        pc = 1 - c if fc else c
        peers.append(((px, py, pc), 4 * px + 2 * py + pc))
    return me, peers


_HBM = pl.BlockSpec(memory_space=pltpu.HBM)
_SEM = pl.BlockSpec(memory_space=pltpu.SEMAPHORE)
_ANY = pl.BlockSpec(memory_space=pl.ANY)
_EFFECT = pltpu.SideEffectType.DATAFLOW_SIDE_EFFECTING


def _split_copy(srcs_are_pieces, src_refs, land_refs, send_sem, recv_sem, a, kk, me, peers, arriving):
    dev, lin = peers[kk]
    npeer = len(_FLIPS)
    src = src_refs[a] if srcs_are_pieces[a] else src_refs[a].at[lin]
    dst = land_refs[a].at[lin] if arriving else land_refs[a].at[me]
    return pltpu.make_async_remote_copy(src_ref=src, dst_ref=dst, send_sem=send_sem.at[a * npeer + kk],
                                        recv_sem=recv_sem.at[a * npeer + kk], device_id=dev, device_id_type=MESH_ID)


def _xchg_start(name, gather, scatter, after=None):
    me_out = 4 * lax.axis_index("x") + 2 * lax.axis_index("y") + lax.axis_index("c")
    srcs = list(gather) + list(scatter)
    is_piece = [True] * len(gather) + [False] * len(scatter)
    lands = []
    for a, piece in zip(srcs, is_piece):
        own = a[None] if piece else lax.dynamic_slice_in_dim(a, me_out, 1, axis=0)
        shape = ((N_DEV,) + tuple(a.shape)) if piece else tuple(a.shape)
        start = (me_out,) + (0,) * (len(shape) - 1)
        lands.append(lax.dynamic_update_slice(lax.empty(shape, a.dtype), own, start))
    n = len(srcs)
    nsem = n * len(_FLIPS)
    has_after = after is not None

    def body(*refs):
        src_refs = refs[:n]
        land_refs = refs[n:2 * n]
        outs = refs[2 * n + (1 if has_after else 0):]
        send_sem, recv_sem = outs[0], outs[1]
        token = outs[-1]
        me, peers = _peer_table()
        for kk in range(len(_FLIPS)):
            for a in range(n):
                _split_copy(is_piece, src_refs, land_refs, send_sem, recv_sem, a, kk, me, peers, False).start()
        token[...] = jnp.zeros_like(token)

    out_shape = ([pltpu.SemaphoreType.DMA((nsem,)), pltpu.SemaphoreType.DMA((nsem,))]
                 + [pltpu.HBM(tuple(a.shape), a.dtype) for a in srcs] + [pltpu.HBM(tuple(a.shape), a.dtype) for a in lands]
                 + [jax.ShapeDtypeStruct((8, 128), F32)])
    args = [pltpu.with_memory_space_constraint(a, pltpu.HBM) for a in srcs + lands] + ([after] if has_after else [])
    res = pl.pallas_call(
        body, name=name, out_shape=out_shape,
        in_specs=[_HBM] * (2 * n) + ([_ANY] if has_after else []),
        out_specs=[_SEM, _SEM] + [_HBM] * (2 * n) + [pl.BlockSpec(memory_space=pltpu.VMEM)],
        input_output_aliases={i: 2 + i for i in range(2 * n)},
        compiler_params=pltpu.CompilerParams(has_side_effects=_EFFECT),
    )(*args)
    state = (res[0], res[1], list(res[2:2 + n]), list(res[2 + n:2 + 2 * n]), is_piece)
    return state, res[-1]


def _xchg_wait(name, state, after):
    send_sem, recv_sem, srcs, lands, is_piece = state
    n = len(srcs)

    def body(*refs):
        src_refs = refs[:n]
        land_refs = refs[n:2 * n]
        s_sem, r_sem = refs[2 * n], refs[2 * n + 1]
        me, peers = _peer_table()
        for kk in range(len(_FLIPS)):
            for a in range(n):
                cp = _split_copy(is_piece, src_refs, land_refs, s_sem, r_sem, a, kk, me, peers, True)
                cp.wait_send()
                cp.wait_recv()

    out_shape = [pltpu.HBM(tuple(a.shape), a.dtype) for a in srcs] + [pltpu.HBM(tuple(a.shape), a.dtype) for a in lands]
    res = pl.pallas_call(
        body, name=name, out_shape=out_shape,
        in_specs=[_HBM] * (2 * n) + [_SEM, _SEM, _ANY], out_specs=[_HBM] * (2 * n),
        input_output_aliases={i: i for i in range(2 * n)},
        compiler_params=pltpu.CompilerParams(has_side_effects=_EFFECT),
    )(*srcs, *lands, send_sem, recv_sem, after)
    return list(res[n:])


_SIB = (0, 0, 1)
_ICI = ((0, 1, 0), (1, 0, 0), (1, 1, 0))


def _flip(fl):
    x, y, c = lax.axis_index("x"), lax.axis_index("y"), lax.axis_index("c")
    px = 1 - x if fl[0] else x
    py = 1 - y if fl[1] else y
    pc = 1 - c if fl[2] else c
    return (px, py, pc), 4 * px + 2 * py + pc


def _gather2_start(name, pieces, after=None):
    me_out = 4 * lax.axis_index("x") + 2 * lax.axis_index("y") + lax.axis_index("c")
    pieces = list(pieces)
    n = len(pieces)
    lands = [lax.dynamic_update_slice(lax.empty((N_DEV,) + tuple(a.shape), a.dtype), a[None],
                                      (me_out,) + (0,) * a.ndim) for a in pieces]
    first = (_SIB,) + _ICI
    has_after = after is not None

    def body(*refs):
        src_refs, land_refs = refs[:n], refs[n:2 * n]
        outs = refs[2 * n + (1 if has_after else 0):]
        send_sem, recv_sem, token = outs[0], outs[1], outs[-1]
        _, me = _flip((0, 0, 0))
        for kk, fl in enumerate(first):
            dev, _ = _flip(fl)
            for a in range(n):
                pltpu.make_async_remote_copy(src_ref=src_refs[a], dst_ref=land_refs[a].at[me],
                                             send_sem=send_sem.at[a * 4 + kk], recv_sem=recv_sem.at[a * 4 + kk],
                                             device_id=dev, device_id_type=MESH_ID).start()
        token[...] = jnp.zeros_like(token)

    hbm = [pltpu.HBM(tuple(a.shape), a.dtype) for a in pieces + lands]
    res = pl.pallas_call(
        body, name=name,
        out_shape=[pltpu.SemaphoreType.DMA((4 * n,)), pltpu.SemaphoreType.DMA((4 * n,))] + hbm
        + [jax.ShapeDtypeStruct((8, 128), F32)],
        in_specs=[_HBM] * (2 * n) + ([_ANY] if has_after else []),
        out_specs=[_SEM, _SEM] + [_HBM] * (2 * n) + [pl.BlockSpec(memory_space=pltpu.VMEM)],
        input_output_aliases={i: 2 + i for i in range(2 * n)},
        compiler_params=pltpu.CompilerParams(has_side_effects=_EFFECT),
    )(*([pltpu.with_memory_space_constraint(a, pltpu.HBM) for a in pieces + lands] + ([after] if has_after else [])))
    return (res[0], res[1], list(res[2:2 + n]), list(res[2 + n:2 + 2 * n])), res[-1]


def _gather2_forward(name, state, after):
    send_a, recv_a, pieces, lands = state
    n = len(pieces)
    first = (_SIB,) + _ICI

    def body(*refs):
        src_refs, land_refs = refs[:n], refs[n:2 * n]
        s_a, r_a = refs[2 * n], refs[2 * n + 1]
        outs = refs[2 * n + 3:]
        send_b, recv_b, token = outs[0], outs[1], outs[-1]
        for kk, fl in enumerate(first):
            dev, lin = _flip(fl)
            for a in range(n):
                cp = pltpu.make_async_remote_copy(src_ref=src_refs[a], dst_ref=land_refs[a].at[lin],
                                                  send_sem=s_a.at[a * 4 + kk], recv_sem=r_a.at[a * 4 + kk],
                                                  device_id=dev, device_id_type=MESH_ID)
                cp.wait_send()
                cp.wait_recv()
        sib, _ = _flip(_SIB)
        for j, fl in enumerate(_ICI):
            _, lin = _flip(fl)
            for a in range(n):
                pltpu.make_async_remote_copy(src_ref=land_refs[a].at[lin], dst_ref=land_refs[a].at[lin],
                                             send_sem=send_b.at[a * 3 + j], recv_sem=recv_b.at[a * 3 + j],
                                             device_id=sib, device_id_type=MESH_ID).start()
        token[...] = jnp.zeros_like(token)

    hbm = [pltpu.HBM(tuple(a.shape), a.dtype) for a in pieces + lands]
    res = pl.pallas_call(
        body, name=name,
        out_shape=[pltpu.SemaphoreType.DMA((3 * n,)), pltpu.SemaphoreType.DMA((3 * n,))] + hbm
        + [jax.ShapeDtypeStruct((8, 128), F32)],
        in_specs=[_HBM] * (2 * n) + [_SEM, _SEM, _ANY],
        out_specs=[_SEM, _SEM] + [_HBM] * (2 * n) + [pl.BlockSpec(memory_space=pltpu.VMEM)],
        input_output_aliases={i: 2 + i for i in range(2 * n)},
        compiler_params=pltpu.CompilerParams(has_side_effects=_EFFECT),
    )(*pieces, *lands, send_a, recv_a, after)
    return (res[0], res[1], list(res[2 + n:2 + 2 * n])), res[-1]


def _gather2_wait(name, state, after):
    send_b, recv_b, lands = state
    n = len(lands)

    def body(*refs):
        land_refs = refs[:n]
        s_b, r_b = refs[n], refs[n + 1]
        sib, _ = _flip(_SIB)
        for j, fl in enumerate(_ICI):
            _, sent = _flip(fl)
            _, arriving = _flip((fl[0], fl[1], 1))
            for a in range(n):
                cp = pltpu.make_async_remote_copy(src_ref=land_refs[a].at[sent], dst_ref=land_refs[a].at[arriving],
                                                  send_sem=s_b.at[a * 3 + j], recv_sem=r_b.at[a * 3 + j],
                                                  device_id=sib, device_id_type=MESH_ID)
                cp.wait_send()
                cp.wait_recv()

    res = pl.pallas_call(
        body, name=name, out_shape=[pltpu.HBM(tuple(a.shape), a.dtype) for a in lands],
        in_specs=[_HBM] * n + [_SEM, _SEM, _ANY], out_specs=[_HBM] * n,
        input_output_aliases={i: i for i in range(n)},
        compiler_params=pltpu.CompilerParams(has_side_effects=_EFFECT),
    )(*lands, send_b, recv_b, after)
    return list(res)


def _adam_math(w, g, m, v):
    m = ADAM_B1 * m + (1.0 - ADAM_B1) * g
    v = ADAM_B2 * v + (1.0 - ADAM_B2) * (g * g)
    m_hat = m / (1.0 - ADAM_B1 ** ADAM_STEP)
    v_hat = v / (1.0 - ADAM_B2 ** ADAM_STEP)
    delta = -ADAM_LR * (m_hat / (jnp.sqrt(v_hat) + ADAM_EPS) + ADAM_WD * w)
    return delta, m, v


def _adam(name, w, m, v, parts):
    r, c = w.shape
    npart, _, cp = parts.shape
    tr = _pick(r, (256, 176, 128, 64, 16, 8, 1))

    def body(w_ref, m_ref, v_ref, p_ref, g_ref, d_ref, mo_ref, vo_ref):
        g = p_ref[0].astype(F32)
        for pp in range(1, npart):
            g = g + p_ref[pp].astype(F32)
        g = g[:, 0:c]
        delta, mn, vn = _adam_math(w_ref[...], g, m_ref[...], v_ref[...])
        g_ref[...] = g
        d_ref[...] = delta
        mo_ref[...] = mn
        vo_ref[...] = vn

    blk = pl.BlockSpec((tr, c), lambda i: (i, 0))
    out = jax.ShapeDtypeStruct((r, c), F32)
    return pl.pallas_call(
        body, name=name, grid=(r // tr,),
        in_specs=[blk, blk, blk, pl.BlockSpec((npart, tr, cp), lambda i: (0, i, 0))],
        out_specs=[blk, blk, blk, blk], out_shape=[out, out, out, out], compiler_params=_params(("parallel",)),
    )(w, m, v, parts)


def _small_sum(name, packs):
    def body(p_ref, o_ref):
        tot = p_ref[0]
        for pp in range(1, N_DEV):
            tot = tot + p_ref[pp]
        o_ref[0:8, :] = tot[0:8, :]
        o_ref[8:24, :] = tot[8:24, :] + tot[24:40, :]

    return pl.pallas_call(body, name=name, out_shape=jax.ShapeDtypeStruct((24, D_MODEL), F32),
                          compiler_params=_params())(packs)


def _local_step(x, tgt, g1, gm, g2, gf, b_forget, sinks, weights, send):
    b, s, _ = x.shape
    l = s + PREFIX
    t = b * l
    w1i, meta = weights("ffn1_in", x)
    h0 = jnp.concatenate([jnp.zeros((b, N_PAD, D_MODEL), F32), jnp.broadcast_to(meta[None], (b, N_META, D_MODEL)), x],
                         axis=1).reshape(t, D_MODEL)

    n1 = _rms_fwd("rms1_fwd", h0, g1)
    gu1, a1 = _ffn_in_fwd("ffn1_in_fwd", n1, w1i, dep=weights("ffn1_out:forward", n1))
    (w1o,) = weights("ffn1_out", weights("mix:forward", a1))
    h1 = _mm_nn("ffn1_out_fwd", a1, w1o, alpha=0.5, res=h0, tm_c=(544, 384, 256, 128), tn_c=(1024,))
    wi, wa, wb, wo = weights("mix", h1)
    um = _rms_fwd("rmsm_fwd", h1, gm)
    qkv = _mm_nn("proj_qkv_fwd", um, wi, out_dtype=BF16, tn_c=(768,), cols=(0, QKV_W))
    gates = _mm_nn("proj_gates_fwd", um, wi, tn_c=(1024,), cols=(P_GATES, 2 * D_MODEL), dep=weights("ffn2:forward", qkv))
    f2 = _mm_nn("proj_f_fwd", um, wi, tn_c=(128,), cols=(P_F, 128))
    qkv3 = qkv.reshape(b, l, QKV_W)
    f3 = f2.reshape(b, l, 128)
    bf_row = jnp.pad(b_forget, ((0, 0), (0, 128 - B_HEADS)))
    c_col, c_row = _fgate_fwd("fgate_fwd", f3, bf_row)
    head_of_row = jnp.arange(STACK) // BLOCK
    slopes = jnp.exp2(-8.0 * (head_of_row + 1).astype(F32) / A_HEADS).reshape(STACK, 1)
    sink_rows = jnp.repeat(sinks.reshape(A_HEADS), BLOCK).reshape(STACK, 1)
    oa3, lse_a = _swa_fwd("swa_fwd", qkv3, slopes, sink_rows)
    ob3, lse_b = _fox_fwd("fox_fwd", qkv3, c_col, c_row)
    oa = oa3.reshape(t, A_WIDTH)
    ob = ob3.reshape(t, B_WIDTH)
    mixed, ya, yb = _branch_gate_fwd("branch_gate_fwd", oa, ob, wa, wb, gates)
    h2 = _mm_nn("mix_out_fwd", mixed, wo, res=h1, tn_c=(1024,))
    w2i, w2o = weights("ffn2", h2)
    n2 = _rms_fwd("rms2_fwd", h2, g2)
    gu2, a2 = _ffn_in_fwd("ffn2_in_fwd", n2, w2i)
    h3 = _mm_nn("ffn2_out_fwd", a2, w2o, alpha=0.5, res=h2, tm_c=(544, 384, 256, 128), tn_c=(1024,))

    dh3_3, loss_blk, dgf = _loss_head("loss_head", h3.reshape(b, l, D_MODEL), gf, tgt)
    dh3 = dh3_3.reshape(t, D_MODEL)

    def ffn_bwd(tag, dh, h_in, g_norm, n_in, gu, a, w_in_blk, w_out):
        dw_out = _mm_tn(tag + "_out_bwd_w", a, dh, alpha=0.5)
        dgu = _ffn_out_bwd_x(tag + "_out_bwd_x", dh, w_out, gu, dep=send(tag + "_out", (dw_out,)))
        dw_in = _ffn_in_bwd_w(tag + "_in_bwd_w", n_in, dgu)
        return _ffn_in_bwd_x(tag + "_in_bwd_x", dgu, w_in_blk, h_in, g_norm, dh, dep=send(tag + "_in", (dw_in,)))

    dh2, dg2 = ffn_bwd("ffn2", dh3, h2, g2, n2, gu2, a2, w2i, w2o)

    dwo = _mm_tn("mix_out_bwd_w", mixed, dh2)
    dya, dyb, dgates = _mix_out_gate_bwd("mix_out_gate_bwd", dh2, wo, gates, ya, yb)
    doa = _mm_nt("branch_a_bwd_x", dya, wa, tn_c=(512,))
    dob = _mm_nt("branch_b_bwd_x", dyb, wb, out_dtype=BF16, tn_c=(512,))
    dwa = _mm_tn("branch_a_bwd_w", oa, dya, tm_c=(512,))
    dwb = _mm_tn("branch_b_bwd_w", ob, dyb, tm_c=(512,))
    dqkv3, dcq, dck = _fox_bwd("fox_bwd", qkv3, c_col, c_row, ob3, lse_b, dob.reshape(b, l, B_WIDTH))
    dqkv3, dsink = _swa_bwd("swa_bwd", qkv3, oa3, lse_a, doa.reshape(b, l, A_WIDTH), slopes, sink_rows, dqkv3)
    dqkv = dqkv3.reshape(t, QKV_W)
    df3, dbf = _fgate_bwd("fgate_bwd", f3, bf_row, dcq, dck)
    df = df3.reshape(t, 128)
    dwi_qkv = _mm_tn("proj_qkv_bwd_w", um, dqkv, tm_c=(512,), tn_c=(768,))
    dwi_g = _mm_tn("proj_gates_bwd_w", um, dgates, tm_c=(512,), tn_c=(512,))
    dwi_f = _mm_tn("proj_f_bwd_w", um, df, tm_c=(512,), tn_c=(128,))
    token = send("mix", (dwi_qkv, dwi_g, dwi_f, dwa, dwb, dwo))
    dh1, dgm = _proj_bwd_x("proj_bwd_x", dqkv, dgates, df, wi, h1, gm, dh2, dep=token)

    dh0, dg1 = ffn_bwd("ffn1", dh1, h0, g1, n1, gu1, a1, w1i, w1o)
    dh0_3 = dh0.reshape(b, l, D_MODEL)
    grad_x = dh0_3[:, PREFIX:, :]
    dmeta = dh0_3[:, N_PAD:PREFIX, :].reshape(b * N_META, D_MODEL)

    misc = jnp.concatenate([dbf[:, 0:B_HEADS], dsink[:, 0].reshape(1, A_HEADS), loss_blk[0:1, 0:1]], axis=1)
    misc = jnp.pad(misc, ((0, 0), (0, D_MODEL - misc.shape[1])))
    row = lax.broadcasted_iota(jnp.int32, (8, D_MODEL), 0)
    vec = jnp.zeros((8, D_MODEL), F32)
    for i, piece in enumerate((dg1, dgm, dg2, dgf, misc)):
        vec = jnp.where(row == i, piece, vec)
    small = jnp.concatenate([vec, dmeta], axis=0)
    return grad_x, small


def _pad_to(a, rows, cols):
    return jnp.pad(a, ((0, rows - a.shape[0]), (0, cols - a.shape[1])))


def _ffn_out_from_gathered(g):
    w = g.reshape(4, FF_SHARD, D_MODEL)
    return jnp.pad(w, ((0, 0), (0, FF_SHARD_P - FF_SHARD), (0, 0))).reshape(D_FF_P, D_MODEL)


def _ffn_out_to_scatter(dw):
    return dw.reshape(4, FF_SHARD_P, D_MODEL)[:, 0:FF_SHARD, :].reshape(N_DEV, FFO_SHARD, D_MODEL)


def _proj_segments():
    segs = [(HEAD_DIM * h, HEAD_DIM, 0, B_SEG + HEAD_DIM * A_HEAD_ORDER.index(h)) for h in range(A_HEADS)]
    segs += [(512, 128, 0, B_SEG + A_WIDTH), (640, 128, 0, B_SEG + A_WIDTH + 128)]
    for first, off in ((768, 0), (1280, 128), (1792, 256)):
        segs += [(first + 128 * hp, 128, 0, PAIR_W * hp + off) for hp in range(4)]
    segs += [(2304, B_HEADS, 2, 0), (2312, 2 * D_MODEL, 1, 0)]
    return segs


def _proj_from_gathered(g):
    def cols(first, width):
        out = []
        for p in range(N_DEV):
            lo, hi = max(first, WIN_SHARD * p), min(first + width, WIN_SHARD * (p + 1))
            if lo < hi:
                out.append(g[p, :, lo - WIN_SHARD * p:hi - WIN_SHARD * p])
        return out

    parts = []
    for arr in (0, 1, 2):
        for first, width, _, _ in sorted((s for s in _proj_segments() if s[2] == arr), key=lambda s: s[3]):
            parts += cols(first, width)
        if arr == 0:
            parts.append(jnp.zeros((D_MODEL, P_GATES - QKV_W), g.dtype))
    parts.append(jnp.zeros((D_MODEL, 128 - B_HEADS), g.dtype))
    return jnp.concatenate(parts, axis=1)


def _proj_to_scatter(dqkv_w, dg_w, df_w):
    arrays = (dqkv_w, dg_w, df_w)
    segs = sorted(_proj_segments())
    blocks = []
    for p in range(N_DEV):
        parts = []
        for first, width, arr, at in segs:
            lo, hi = max(first, WIN_SHARD * p), min(first + width, WIN_SHARD * (p + 1))
            if lo < hi:
                parts.append(arrays[arr][:, at + lo - first:at + hi - first])
        parts.append(jnp.zeros((D_MODEL, WIN_SHARD_P - WIN_SHARD), dqkv_w.dtype))
        blocks.append(jnp.concatenate(parts, axis=1))
    return jnp.stack(blocks, axis=0)


def _a_rows_from_natural(w):
    return jnp.concatenate([w[HEAD_DIM * h:HEAD_DIM * (h + 1)] for h in A_HEAD_ORDER], axis=0)


def _a_rows_to_natural(w):
    return jnp.concatenate([w[HEAD_DIM * A_HEAD_ORDER.index(h):HEAD_DIM * (A_HEAD_ORDER.index(h) + 1)]
                            for h in range(A_HEADS)], axis=0)


def kernel(x, meta_tokens, ffn1_norm, ffn1_w_in, ffn1_w_out, mix_norm, w_in, b_forget, attn_sinks, w_branch_a, w_branch_b, w_out, ffn2_norm, ffn2_w_in, ffn2_w_out, final_norm, loss_target, m_meta_tokens, m_ffn1_norm, m_ffn1_w_in, m_ffn1_w_out, m_mix_norm, m_w_in, m_b_forget, m_attn_sinks, m_w_branch_a, m_w_branch_b, m_w_out, m_ffn2_norm, m_ffn2_w_in, m_ffn2_w_out, m_final_norm, v_meta_tokens, v_ffn1_norm, v_ffn1_w_in, v_ffn1_w_out, v_mix_norm, v_w_in, v_b_forget, v_attn_sinks, v_w_branch_a, v_w_branch_b, v_w_out, v_ffn2_norm, v_ffn2_w_in, v_ffn2_w_out, v_final_norm):
    me = 4 * lax.axis_index("x") + 2 * lax.axis_index("y") + lax.axis_index("c")

    shards = (
        _pad_to(ffn1_w_in[0].astype(BF16), D_MODEL, FF_SHARD_P),
        ffn1_w_out[0].astype(BF16),
        _pad_to(w_in[0].astype(BF16), D_MODEL, WIN_SHARD_P),
        w_branch_a[0].astype(BF16), w_branch_b[0].astype(BF16), w_out[0].astype(BF16),
        _pad_to(ffn2_w_in[0].astype(BF16), D_MODEL, FF_SHARD_P),
        ffn2_w_out[0].astype(BF16),
        meta_tokens,
    )
    s1i, s1o, swi, swa, swb, swo, s2i, s2o, smeta = shards
    first_level, second_level = {}, {}
    first_level["ffn1_in"], tok = _gather2_start("gather_ffn1_in_start", (s1i, smeta))
    first_level["ffn1_out"], tok = _gather2_start("gather_ffn1_out_start", (s1o,), after=tok)
    first_level["mix"], tok = _gather2_start("gather_mix_start", (swi, swa, swb, swo), after=tok)
    first_level["ffn2"], tok = _gather2_start("gather_ffn2_start", (s2i, s2o), after=tok)
    started = {"tok": tok}

    def weights(group, after):
        if group.endswith(":forward"):
            group = group[:-len(":forward")]
            second_level[group], token = _gather2_forward("gather_" + group + "_forward", first_level[group], after)
            return token
        if group == "ffn1_in":
            after = weights("ffn1_in:forward", started["tok"])
        got = _gather2_wait("gather_" + group + "_wait", second_level[group], after)
        if group == "mix":
            gwi, gwa, gwb, gwo = got
            return (_proj_from_gathered(gwi), _a_rows_from_natural(gwa.transpose(1, 0, 2).reshape(A_WIDTH, D_MODEL)),
                    gwb.transpose(1, 0, 2).reshape(B_WIDTH, D_MODEL), gwo.reshape(D_MODEL, D_MODEL))
        if group == "ffn1_in":
            return got[0].reshape(2, 4, D_MODEL, FF_SHARD_P), got[1].transpose(1, 0, 2).reshape(N_META, D_MODEL)
        if group == "ffn1_out":
            return (_ffn_out_from_gathered(got[0]),)
        return got[0].reshape(2, 4, D_MODEL, FF_SHARD_P), _ffn_out_from_gathered(got[1])

    scatter_state = {}

    def send(group, grads):
        if group == "mix":
            dwi_qkv, dwi_g, dwi_f, dwa, dwb, dwo = grads
            dwa = _a_rows_to_natural(dwa)
            blocks = (_proj_to_scatter(dwi_qkv, dwi_g, dwi_f), dwa.reshape(A_WIDTH, N_DEV, 128).transpose(1, 0, 2),
                      dwb.reshape(B_WIDTH, N_DEV, 128).transpose(1, 0, 2), dwo.reshape(N_DEV, 128, D_MODEL))
        elif group.endswith("_in"):
            blocks = (grads[0].reshape(N_DEV, D_MODEL, FF_SHARD_P),)
        else:
            blocks = (_ffn_out_to_scatter(grads[0]),)
        scatter_state[group], token = _xchg_start("scatter_" + group + "_start", (), blocks)
        return token

    gf = final_norm.reshape(1, D_MODEL)
    grad_x, small = _local_step(x, loss_target, ffn1_norm, mix_norm, ffn2_norm, gf, b_forget, attn_sinks, weights, send)

    small_state, after = _xchg_start("gather_small_start", (small,), ())
    out = {}
    updates = (
        ("ffn2_out", (("ffn2_w_out", ffn2_w_out, m_ffn2_w_out, v_ffn2_w_out),)),
        ("ffn2_in", (("ffn2_w_in", ffn2_w_in, m_ffn2_w_in, v_ffn2_w_in),)),
        ("mix", (("w_in", w_in, m_w_in, v_w_in), ("w_branch_a", w_branch_a, m_w_branch_a, v_w_branch_a),
                 ("w_branch_b", w_branch_b, m_w_branch_b, v_w_branch_b), ("w_out", w_out, m_w_out, v_w_out))),
        ("ffn1_out", (("ffn1_w_out", ffn1_w_out, m_ffn1_w_out, v_ffn1_w_out),)),
        ("ffn1_in", (("ffn1_w_in", ffn1_w_in, m_ffn1_w_in, v_ffn1_w_in),)),
    )
    for group, members in updates:
        parts_list = _xchg_wait("scatter_" + group + "_wait", scatter_state[group], after)
        for (nm, w, m, v), parts in zip(members, parts_list):
            res4 = _adam("adam_" + nm, w[0], m[0], v[0], parts)
            out[nm] = tuple(r[None] for r in res4)
            after = res4[0]
    (packs,) = _xchg_wait("gather_small_wait", small_state, after)

    tot = _small_sum("small_sum", packs)
    loss = tot[4, 2 * B_HEADS]
    g_meta = lax.dynamic_slice(tot[8:24, :], (0, me * 128), (N_META, 128))
    out["meta_tokens"] = tuple(_adam("adam_meta_tokens", meta_tokens, m_meta_tokens, v_meta_tokens, g_meta[None]))

    def pack_small(n1, nm, n2, nf, bfv, skv):
        misc = jnp.pad(jnp.concatenate([bfv, skv], axis=1), ((0, 0), (0, D_MODEL - 2 * B_HEADS)))
        row = lax.broadcasted_iota(jnp.int32, (8, D_MODEL), 0)
        vec = jnp.zeros((8, D_MODEL), F32)
        for i, piece in enumerate((n1, nm, n2, nf.reshape(1, D_MODEL), misc)):
            vec = jnp.where(row == i, piece, vec)
        return vec

    w_pack = pack_small(ffn1_norm, mix_norm, ffn2_norm, final_norm, b_forget, attn_sinks)
    m_pack = pack_small(m_ffn1_norm, m_mix_norm, m_ffn2_norm, m_final_norm, m_b_forget, m_attn_sinks)
    v_pack = pack_small(v_ffn1_norm, v_mix_norm, v_ffn2_norm, v_final_norm, v_b_forget, v_attn_sinks)
    small4 = _adam("adam_small", w_pack, m_pack, v_pack, tot[0:8][None])
    for i, nm in enumerate(("ffn1_norm", "mix_norm", "ffn2_norm")):
        out[nm] = tuple(r[i:i + 1] for r in small4)
    out["final_norm"] = tuple(r[3] for r in small4)
    out["b_forget"] = tuple(r[4:5, 0:B_HEADS] for r in small4)
    out["attn_sinks"] = tuple(r[4:5, B_HEADS:2 * B_HEADS] for r in small4)

    names = ("meta_tokens", "ffn1_norm", "ffn1_w_in", "ffn1_w_out", "mix_norm", "w_in", "b_forget", "attn_sinks",
             "w_branch_a", "w_branch_b", "w_out", "ffn2_norm", "ffn2_w_in", "ffn2_w_out", "final_norm")
    return (loss, grad_x) + tuple(out[nm][kind] for kind in range(4) for nm in names)
```

```python
import jax
import jax.numpy as jnp
from jax import lax
from jax.experimental import pallas as pl
from jax.experimental.pallas import tpu as pltpu

F32 = jnp.float32
BF16 = jnp.bfloat16

D_MODEL = 1024
N_META = 16
BLOCK = 128
PREFIX = 128
N_PAD = PREFIX - N_META
HEAD_DIM = 64
A_HEADS = 8
A_KV_HEADS = 2
A_GROUP = 4
B_HEADS = 8
A_WIDTH = 512
A_KV_WIDTH = 128
B_WIDTH = 512
D_FF = 2816
N_DEV = 8
FF_SHARD = 2 * D_FF // N_DEV
FF_SHARD_P = 768
FFO_SHARD = D_FF // N_DEV
FFO_SHARD_P = FF_SHARD_P // 2
D_FF_P = 4 * FF_SHARD_P
W_IN_COLS = 4360
WIN_SHARD = W_IN_COLS // N_DEV
WIN_SHARD_P = 640
PAIR_W = 3 * 128
B_SEG = 4 * PAIR_W
A_SEG = A_WIDTH + 2 * A_KV_WIDTH
QKV_W = B_SEG + A_SEG
P_GATES = 2 * (2 * D_MODEL)
P_F = P_GATES + 2 * D_MODEL
PROJ_P = P_F + 128
A_HEAD_ORDER = (0, 4, 1, 5, 2, 6, 3, 7)
EPS = 1e-6
NEG = -1e30
SCALE = HEAD_DIM ** -0.5
ADAM_LR = 0.001
ADAM_B1 = 0.9
ADAM_B2 = 0.999
ADAM_EPS = 1e-08
ADAM_WD = 0.01
ADAM_STEP = 10
VMEM_LIMIT = 56 * 1024 * 1024
MESH_ID = pl.DeviceIdType.MESH
SMALL_ROWS = 40

_NN = (((1,), (0,)), ((), ()))
_NT = (((1,), (1,)), ((), ()))
_TN = (((0,), (0,)), ((), ()))


def _params(sem=None):
    return pltpu.CompilerParams(dimension_semantics=sem, vmem_limit_bytes=VMEM_LIMIT)


def _pick(n, cands):
    for c in cands:
        if n % c == 0:
            return c
    raise ValueError(f"no tile for {n}")


def _bf(v):
    return v if v.dtype == BF16 else v.astype(BF16)


def _mm(name, a, b, dims, grid, a_spec, b_spec, o_spec, out_shape, out_dtype, acc_shape, k_axis=None, nk=1,
        alpha=1.0, res=None, res_spec=None, dep=None):
    has_res = res is not None
    has_dep = dep is not None

    def body(*refs):
        a_ref, b_ref = refs[0], refs[1]
        r_ref = refs[2] if has_res else None
        o_ref = refs[2 + has_res + has_dep]

        def finish(acc):
            if alpha != 1.0:
                acc = acc * alpha
            if has_res:
                acc = acc + r_ref[...]
            o_ref[...] = acc.astype(o_ref.dtype)

        part = lax.dot_general(_bf(a_ref[...]), _bf(b_ref[...]), dims, preferred_element_type=F32)
        if nk == 1:
            finish(part)
        else:
            acc_ref = refs[-1]
            k = pl.program_id(k_axis)

            @pl.when(k == 0)
            def _():
                acc_ref[...] = part

            @pl.when(k > 0)
            def _():
                acc_ref[...] += part

            @pl.when(k == nk - 1)
            def _():
                finish(acc_ref[...])

    in_specs = [a_spec, b_spec] + ([res_spec] if has_res else []) + ([pl.BlockSpec(memory_space=pl.ANY)] if has_dep else [])
    args = (a, b) + ((res,) if has_res else ()) + ((dep,) if has_dep else ())
    sem = tuple("arbitrary" if (nk > 1 and i == k_axis) else "parallel" for i in range(len(grid)))
    return pl.pallas_call(
        body, name=name, grid=grid, in_specs=in_specs, out_specs=o_spec,
        out_shape=jax.ShapeDtypeStruct(out_shape, out_dtype),
        scratch_shapes=[pltpu.VMEM(acc_shape, F32)] if nk > 1 else [],
        compiler_params=_params(sem),
    )(*args)


def _mm_nn(name, a, b, out_dtype=F32, alpha=1.0, res=None, tn_c=(512, 640, 256, 128), cols=None,
           tm_c=(1088, 768, 512, 256, 128), dep=None):
    t, k = a.shape
    c0, n = (0, b.shape[1]) if cols is None else cols
    tm = _pick(t, tm_c)
    tn = _pick(n, tn_c)
    assert c0 % tn == 0
    jb = c0 // tn
    return _mm(name, a, b, _NN, (t // tm, n // tn),
               pl.BlockSpec((tm, k), lambda i, j: (i, 0)), pl.BlockSpec((k, tn), lambda i, j: (0, jb + j)),
               pl.BlockSpec((tm, tn), lambda i, j: (i, j)), (t, n), out_dtype, None,
               alpha=alpha, res=res, res_spec=pl.BlockSpec((tm, tn), lambda i, j: (i, j)), dep=dep)


def _mm_nt(name, a, b, out_dtype=F32, alpha=1.0, tn_c=(768, 512, 256, 128), tk_c=None, dep=None, res=None, kcols=None):
    t, k = a.shape
    n = b.shape[0]
    c0 = 0 if kcols is None else kcols[0]
    tm = _pick(t, (1088, 768, 512, 256, 128))
    tn = _pick(n, tn_c)
    tk = k if tk_c is None else _pick(k, tk_c)
    nk = k // tk
    assert c0 % tk == 0
    kb = c0 // tk
    return _mm(name, a, b, _NT, (t // tm, n // tn, nk),
               pl.BlockSpec((tm, tk), lambda i, j, kk: (i, kk)), pl.BlockSpec((tn, tk), lambda i, j, kk: (j, kb + kk)),
               pl.BlockSpec((tm, tn), lambda i, j, kk: (i, j)), (t, n), out_dtype, (tm, tn), k_axis=2, nk=nk, alpha=alpha,
               dep=dep, res=res, res_spec=pl.BlockSpec((tm, tn), lambda i, j, kk: (i, j)))


def _mm_tn(name, a, b, out_dtype=BF16, alpha=1.0, tm_c=(768, 512, 256, 128), tn_c=(512, 640, 256, 128)):
    t, m = a.shape
    n = b.shape[1]
    tm = _pick(m, tm_c)
    tn = _pick(n, tn_c)
    tk = t
    nk = 1
    return _mm(name, a, b, _TN, (m // tm, n // tn, nk),
               pl.BlockSpec((tk, tm), lambda i, j, kk: (kk, i)), pl.BlockSpec((tk, tn), lambda i, j, kk: (kk, j)),
               pl.BlockSpec((tm, tn), lambda i, j, kk: (i, j)), (m, n), out_dtype, (tm, tn), k_axis=2, nk=nk, alpha=alpha)


def _silu_grads(g, u, d):
    sg = jax.nn.sigmoid(g)
    return d * u * (sg * (1.0 + g * (1.0 - sg))), d * (g * sg)


def _ffn_in_fwd(name, n, wblk, dep=None):
    t = n.shape[0]
    tm = _pick(t, (1088, 768, 512, 256, 128))
    has_dep = dep is not None

    def body(n_ref, w_ref, *rest):
        gu_ref, a_ref = rest[-2], rest[-1]
        nv = n_ref[...]
        g = lax.dot_general(nv, w_ref[0], _NN, preferred_element_type=F32)
        u = lax.dot_general(nv, w_ref[1], _NN, preferred_element_type=F32)
        gu_ref[0] = g.astype(BF16)
        gu_ref[1] = u.astype(BF16)
        a_ref[...] = ((g * jax.nn.sigmoid(g)) * u).astype(BF16)

    return pl.pallas_call(
        body, name=name, grid=(t // tm, 4),
        in_specs=[pl.BlockSpec((tm, D_MODEL), lambda i, j: (i, 0)),
                  pl.BlockSpec((2, None, D_MODEL, FF_SHARD_P), lambda i, j: (0, j, 0, 0))]
        + ([pl.BlockSpec(memory_space=pl.ANY)] if has_dep else []),
        out_specs=[pl.BlockSpec((2, tm, FF_SHARD_P), lambda i, j: (0, i, j)),
                   pl.BlockSpec((tm, FF_SHARD_P), lambda i, j: (i, j))],
        out_shape=[jax.ShapeDtypeStruct((2, t, D_FF_P), BF16), jax.ShapeDtypeStruct((t, D_FF_P), BF16)],
        compiler_params=_params(("parallel", "parallel")),
    )(*((n, wblk) + ((dep,) if has_dep else ())))


def _ffn_out_bwd_x(name, dh, w_out, gu, dep=None):
    t = dh.shape[0]
    tm = _pick(t, (1088, 768, 512, 256, 128))
    has_dep = dep is not None

    def body(dh_ref, w_ref, gu_ref, *rest):
        o_ref = rest[-1]
        da = lax.dot_general(_bf(dh_ref[...]), w_ref[...], _NT, preferred_element_type=F32) * 0.5
        dg, du = _silu_grads(gu_ref[0].astype(F32), gu_ref[1].astype(F32), da)
        o_ref[0] = dg.astype(BF16)
        o_ref[1] = du.astype(BF16)

    gu_spec = pl.BlockSpec((2, tm, FF_SHARD_P), lambda i, j: (0, i, j))
    return pl.pallas_call(
        body, name=name, grid=(t // tm, 4),
        in_specs=[pl.BlockSpec((tm, D_MODEL), lambda i, j: (i, 0)), pl.BlockSpec((FF_SHARD_P, D_MODEL), lambda i, j: (j, 0)),
                  gu_spec] + ([pl.BlockSpec(memory_space=pl.ANY)] if has_dep else []),
        out_specs=gu_spec, out_shape=jax.ShapeDtypeStruct((2, t, D_FF_P), BF16),
        compiler_params=_params(("parallel", "parallel")),
    )(*((dh, w_out, gu) + ((dep,) if has_dep else ())))


def _rms_bwd_rows(dn, h, g, dres):
    r = lax.rsqrt(jnp.mean(h * h, axis=-1, keepdims=True) + EPS)
    tv = dn * g
    dot = jnp.mean(tv * h, axis=-1, keepdims=True)
    return dres + (r * tv - h * (r * r * r * dot)), jnp.sum(dn * (h * r), axis=0, keepdims=True)


def _accumulate_rows(ref, part, first):
    @pl.when(first)
    def _():
        ref[...] = part

    @pl.when(jnp.logical_not(first))
    def _():
        ref[...] += part


def _ffn_in_bwd_x(name, dgu, wblk, h_in, g_norm, dres, dep=None):
    t = dgu.shape[1]
    tm = _pick(t, (272, 256, 128))
    has_dep = dep is not None

    def body(d_ref, w_ref, h_ref, g_ref, r_ref, *rest):
        dh_ref, dg_ref = rest[-2], rest[-1]
        acc = None
        for s in range(2):
            for j in range(4):
                part = lax.dot_general(d_ref[s, :, FF_SHARD_P * j:FF_SHARD_P * (j + 1)], w_ref[s, j], _NT,
                                       preferred_element_type=F32)
                acc = part if acc is None else acc + part
        dh, dg = _rms_bwd_rows(acc, h_ref[...], g_ref[...], r_ref[...])
        dh_ref[...] = dh
        _accumulate_rows(dg_ref, dg, pl.program_id(0) == 0)

    row = pl.BlockSpec((tm, D_MODEL), lambda i: (i, 0))
    vec = pl.BlockSpec((1, D_MODEL), lambda i: (0, 0))
    return pl.pallas_call(
        body, name=name, grid=(t // tm,),
        in_specs=[pl.BlockSpec((2, tm, D_FF_P), lambda i: (0, i, 0)),
                  pl.BlockSpec((2, 4, D_MODEL, FF_SHARD_P), lambda i: (0, 0, 0, 0)), row, vec, row]
        + ([pl.BlockSpec(memory_space=pl.ANY)] if has_dep else []),
        out_specs=[row, vec],
        out_shape=[jax.ShapeDtypeStruct((t, D_MODEL), F32), jax.ShapeDtypeStruct((1, D_MODEL), F32)],
        compiler_params=_params(("arbitrary",)),
    )(*((dgu, wblk, h_in, g_norm, dres) + ((dep,) if has_dep else ())))


def _proj_bwd_x(name, dqkv, dgates, df, wi, h_in, g_norm, dres, dep=None):
    t = dqkv.shape[0]
    tm = _pick(t, (272, 256, 128))
    has_dep = dep is not None

    def body(q_ref, gt_ref, f_ref, w_ref, h_ref, g_ref, r_ref, *rest):
        dh_ref, dg_ref = rest[-2], rest[-1]
        acc = lax.dot_general(q_ref[...], w_ref[:, 0:QKV_W], _NT, preferred_element_type=F32)
        acc = acc + lax.dot_general(gt_ref[...], w_ref[:, P_GATES:P_F], _NT, preferred_element_type=F32)
        acc = acc + lax.dot_general(f_ref[...], w_ref[:, P_F:PROJ_P], _NT, preferred_element_type=F32)
        dh, dg = _rms_bwd_rows(acc, h_ref[...], g_ref[...], r_ref[...])
        dh_ref[...] = dh
        _accumulate_rows(dg_ref, dg, pl.program_id(0) == 0)

    row = pl.BlockSpec((tm, D_MODEL), lambda i: (i, 0))
    vec = pl.BlockSpec((1, D_MODEL), lambda i: (0, 0))
    return pl.pallas_call(
        body, name=name, grid=(t // tm,),
        in_specs=[pl.BlockSpec((tm, QKV_W), lambda i: (i, 0)), pl.BlockSpec((tm, 2 * D_MODEL), lambda i: (i, 0)),
                  pl.BlockSpec((tm, 128), lambda i: (i, 0)), pl.BlockSpec((D_MODEL, PROJ_P), lambda i: (0, 0)), row, vec, row]
        + ([pl.BlockSpec(memory_space=pl.ANY)] if has_dep else []),
        out_specs=[row, vec],
        out_shape=[jax.ShapeDtypeStruct((t, D_MODEL), F32), jax.ShapeDtypeStruct((1, D_MODEL), F32)],
        compiler_params=_params(("arbitrary",)),
    )(*((dqkv, dgates, df, wi, h_in, g_norm, dres) + ((dep,) if has_dep else ())))


def _ffn_in_bwd_w(name, n, dgu):
    t = n.shape[0]
    tk = t
    nk = 1
    return _mm(name, n, dgu, _TN, (2, 4, nk),
               pl.BlockSpec((tk, D_MODEL), lambda s, j, kk: (kk, 0)),
               pl.BlockSpec((None, tk, FF_SHARD_P), lambda s, j, kk: (s, kk, j)),
               pl.BlockSpec((None, None, D_MODEL, FF_SHARD_P), lambda s, j, kk: (s, j, 0, 0)),
               (2, 4, D_MODEL, FF_SHARD_P), BF16, (D_MODEL, FF_SHARD_P), k_axis=2, nk=nk)


def _rms_fwd(name, h, g):
    t = h.shape[0]
    tm = _pick(t, (544, 384, 256, 128))

    def body(h_ref, g_ref, o_ref):
        hv = h_ref[...]
        r = lax.rsqrt(jnp.mean(hv * hv, axis=-1, keepdims=True) + EPS)
        o_ref[...] = ((hv * r) * g_ref[...]).astype(BF16)

    return pl.pallas_call(
        body, name=name, grid=(t // tm,),
        in_specs=[pl.BlockSpec((tm, D_MODEL), lambda i: (i, 0)), pl.BlockSpec((1, D_MODEL), lambda i: (0, 0))],
        out_specs=pl.BlockSpec((tm, D_MODEL), lambda i: (i, 0)),
        out_shape=jax.ShapeDtypeStruct((t, D_MODEL), BF16), compiler_params=_params(("parallel",)),
    )(h, g)


def _rms_bwd(name, h, g, dn, dres):
    t = h.shape[0]
    tm = _pick(t, (544, 384, 256, 128))

    def body(h_ref, g_ref, dn_ref, dres_ref, dh_ref, dg_ref):
        i = pl.program_id(0)
        hv = h_ref[...]
        dnv = dn_ref[...]
        r = lax.rsqrt(jnp.mean(hv * hv, axis=-1, keepdims=True) + EPS)
        tv = dnv * g_ref[...]
        dot = jnp.mean(tv * hv, axis=-1, keepdims=True)
        dh_ref[...] = dres_ref[...] + (r * tv - hv * (r * r * r * dot))
        part = jnp.sum(dnv * (hv * r), axis=0, keepdims=True)

        @pl.when(i == 0)
        def _():
            dg_ref[...] = part

        @pl.when(i > 0)
        def _():
            dg_ref[...] += part

    row = pl.BlockSpec((tm, D_MODEL), lambda i: (i, 0))
    vec = pl.BlockSpec((1, D_MODEL), lambda i: (0, 0))
    return pl.pallas_call(
        body, name=name, grid=(t // tm,), in_specs=[row, vec, row, row], out_specs=[row, vec],
        out_shape=[jax.ShapeDtypeStruct((t, D_MODEL), F32), jax.ShapeDtypeStruct((1, D_MODEL), F32)],
        compiler_params=_params(("arbitrary",)),
    )(h, g, dn, dres)


def _branch_gate_fwd(name, oa, ob, wa, wb, gates):
    t = gates.shape[0]
    tm = _pick(t, (544, 384, 256, 128))

    def body(oa_ref, ob_ref, wa_ref, wb_ref, g_ref, o_ref, ya_ref, yb_ref):
        ya = lax.dot_general(_bf(oa_ref[...]), wa_ref[...], _NN, preferred_element_type=F32)
        yb = lax.dot_general(_bf(ob_ref[...]), wb_ref[...], _NN, preferred_element_type=F32)
        sa = jax.nn.sigmoid(g_ref[:, 0:D_MODEL])
        sb = jax.nn.sigmoid(g_ref[:, D_MODEL:2 * D_MODEL])
        o_ref[...] = (sa * ya + sb * yb).astype(BF16)
        ya_ref[...] = ya.astype(BF16)
        yb_ref[...] = yb.astype(BF16)

    blk = pl.BlockSpec((tm, D_MODEL), lambda i: (i, 0))
    narrow = pl.BlockSpec((tm, A_WIDTH), lambda i: (i, 0))
    wide = pl.BlockSpec((tm, 2 * D_MODEL), lambda i: (i, 0))
    wspec = pl.BlockSpec((A_WIDTH, D_MODEL), lambda i: (0, 0))
    out = jax.ShapeDtypeStruct((t, D_MODEL), BF16)
    return pl.pallas_call(
        body, name=name, grid=(t // tm,), in_specs=[narrow, narrow, wspec, wspec, wide], out_specs=[blk, blk, blk],
        out_shape=[out, out, out], compiler_params=_params(("parallel",)),
    )(oa, ob, wa, wb, gates)


def _mix_out_gate_bwd(name, dh, wo, gates, ya, yb):
    t = gates.shape[0]
    tm = _pick(t, (544, 384, 256, 128))

    def body(dh_ref, w_ref, g_ref, ya_ref, yb_ref, dya_ref, dyb_ref, dg_ref):
        dm = lax.dot_general(_bf(dh_ref[...]), w_ref[...], _NT, preferred_element_type=F32)
        sa = jax.nn.sigmoid(g_ref[:, 0:D_MODEL])
        sb = jax.nn.sigmoid(g_ref[:, D_MODEL:2 * D_MODEL])
        dya_ref[...] = (dm * sa).astype(BF16)
        dyb_ref[...] = (dm * sb).astype(BF16)
        dg_ref[:, 0:D_MODEL] = (dm * ya_ref[...].astype(F32) * (sa * (1.0 - sa))).astype(BF16)
        dg_ref[:, D_MODEL:2 * D_MODEL] = (dm * yb_ref[...].astype(F32) * (sb * (1.0 - sb))).astype(BF16)

    blk = pl.BlockSpec((tm, D_MODEL), lambda i: (i, 0))
    wide = pl.BlockSpec((tm, 2 * D_MODEL), lambda i: (i, 0))
    out = jax.ShapeDtypeStruct((t, D_MODEL), BF16)
    return pl.pallas_call(
        body, name=name, grid=(t // tm,),
        in_specs=[blk, pl.BlockSpec((D_MODEL, D_MODEL), lambda i: (0, 0)), wide, blk, blk], out_specs=[blk, blk, wide],
        out_shape=[out, out, jax.ShapeDtypeStruct((t, 2 * D_MODEL), BF16)], compiler_params=_params(("parallel",)),
    )(dh, wo, gates, ya, yb)


def _loss_head(name, h3, gf, tgt):
    b, l, _ = h3.shape
    nb = l // BLOCK

    def body(h_ref, g_ref, t_ref, dh_ref, loss_ref, dg_ref):
        first = (pl.program_id(0) == 0) & (pl.program_id(1) == 0)
        real = (pl.program_id(1) > 0).astype(F32)
        hv = h_ref[...]
        g = g_ref[...]
        r = lax.rsqrt(jnp.mean(hv * hv, axis=-1, keepdims=True) + EPS)
        xn = hv * r
        err = (xn * g - t_ref[...]) * real
        lpart = 0.5 * jnp.sum(jnp.mean(err * err, axis=-1, keepdims=True), axis=0, keepdims=True)
        dy = err * (1.0 / D_MODEL)
        tv = dy * g
        dot = jnp.mean(tv * hv, axis=-1, keepdims=True)
        dh_ref[...] = r * tv - hv * (r * r * r * dot)
        gpart = jnp.sum(dy * xn, axis=0, keepdims=True)

        @pl.when(first)
        def _():
            loss_ref[...] = jnp.zeros_like(loss_ref)
            dg_ref[...] = jnp.zeros_like(dg_ref)

        loss_ref[...] += jnp.broadcast_to(lpart, loss_ref.shape)
        dg_ref[...] += gpart

    return pl.pallas_call(
        body, name=name, grid=(b, nb),
        in_specs=[pl.BlockSpec((None, BLOCK, D_MODEL), lambda bi, n: (bi, n, 0)),
                  pl.BlockSpec((1, D_MODEL), lambda bi, n: (0, 0)),
                  pl.BlockSpec((None, BLOCK, D_MODEL), lambda bi, n: (bi, jnp.maximum(n - 1, 0), 0))],
        out_specs=[pl.BlockSpec((None, BLOCK, D_MODEL), lambda bi, n: (bi, n, 0)),
                   pl.BlockSpec((8, 128), lambda bi, n: (0, 0)),
                   pl.BlockSpec((1, D_MODEL), lambda bi, n: (0, 0))],
        out_shape=[jax.ShapeDtypeStruct(h3.shape, F32), jax.ShapeDtypeStruct((8, 128), F32),
                   jax.ShapeDtypeStruct((1, D_MODEL), F32)],
        compiler_params=_params(("arbitrary", "arbitrary")),
    )(h3, gf, tgt)


def _fgate_fwd(name, f3, bf_row):
    b, l, _ = f3.shape
    nb = l // BLOCK

    def body(f_ref, b_ref, cc_ref, cr_ref):
        r_i = lax.broadcasted_iota(jnp.int32, (BLOCK, BLOCK), 0)
        c_i = lax.broadcasted_iota(jnp.int32, (BLOCK, BLOCK), 1)
        tri = (r_i >= c_i).astype(F32)
        carry = jnp.zeros((1, 128), F32)
        for blk in range(nb):
            rows = slice(blk * BLOCK, (blk + 1) * BLOCK)
            z = f_ref[rows, :] + b_ref[...]
            lf = jnp.minimum(z, 0.0) - jnp.log(1.0 + jnp.exp(-jnp.abs(z)))
            cb = jnp.dot(tri, lf, preferred_element_type=F32, precision=lax.Precision.HIGHEST) + carry
            carry = cb[BLOCK - 1:BLOCK, :]
            cbt = cb.T
            for hh in range(B_HEADS):
                cc_ref[hh, rows, :] = jnp.sum(jnp.where(c_i == hh, cb, 0.0), axis=1, keepdims=True)
                cr_ref[hh, :, rows] = cbt[hh:hh + 1, :]

    return pl.pallas_call(
        body, name=name, grid=(b,),
        in_specs=[pl.BlockSpec((None, l, 128), lambda bi: (bi, 0, 0)),
                  pl.BlockSpec((1, 128), lambda bi: (0, 0))],
        out_specs=[pl.BlockSpec((None, B_HEADS, l, 1), lambda bi: (bi, 0, 0, 0)),
                   pl.BlockSpec((None, B_HEADS, 1, l), lambda bi: (bi, 0, 0, 0))],
        out_shape=[jax.ShapeDtypeStruct((b, B_HEADS, l, 1), F32), jax.ShapeDtypeStruct((b, B_HEADS, 1, l), F32)],
        compiler_params=_params(("parallel",)),
    )(f3, bf_row)


def _fgate_bwd(name, f3, bf_row, dcq, dck):
    b, l, _ = f3.shape
    nb = l // BLOCK

    def body(f_ref, b_ref, dcq_ref, dck_ref, df_ref, db_ref):
        r_i = lax.broadcasted_iota(jnp.int32, (BLOCK, BLOCK), 0)
        c_i = lax.broadcasted_iota(jnp.int32, (BLOCK, BLOCK), 1)
        tri = (r_i <= c_i).astype(F32)
        carry = jnp.zeros((1, 128), F32)
        total = jnp.zeros((1, 128), F32)
        for blk in range(nb - 1, -1, -1):
            rows = slice(blk * BLOCK, (blk + 1) * BLOCK)
            krows = jnp.concatenate([dck_ref[hh, :, rows] for hh in range(B_HEADS)]
                                    + [jnp.zeros((BLOCK - B_HEADS, BLOCK), F32)], axis=0)
            dcb = krows.T
            for hh in range(B_HEADS):
                dcb = dcb + jnp.where(c_i == hh, dcq_ref[hh, rows, :], 0.0)
            rc = jnp.dot(tri, dcb, preferred_element_type=F32, precision=lax.Precision.HIGHEST) + carry
            carry = rc[0:1, :]
            z = f_ref[rows, :] + b_ref[...]
            df = rc * (1.0 / (1.0 + jnp.exp(z)))
            df_ref[rows, :] = df.astype(BF16)
            total = total + jnp.sum(df, axis=0, keepdims=True)

        @pl.when(pl.program_id(0) == 0)
        def _():
            db_ref[...] = total

        @pl.when(pl.program_id(0) > 0)
        def _():
            db_ref[...] += total

    return pl.pallas_call(
        body, name=name, grid=(b,),
        in_specs=[pl.BlockSpec((None, l, 128), lambda bi: (bi, 0, 0)),
                  pl.BlockSpec((1, 128), lambda bi: (0, 0)),
                  pl.BlockSpec((None, B_HEADS, l, 1), lambda bi: (bi, 0, 0, 0)),
                  pl.BlockSpec((None, B_HEADS, 1, l), lambda bi: (bi, 0, 0, 0))],
        out_specs=[pl.BlockSpec((None, l, 128), lambda bi: (bi, 0, 0)), pl.BlockSpec((1, 128), lambda bi: (0, 0))],
        out_shape=[jax.ShapeDtypeStruct((b, l, 128), BF16), jax.ShapeDtypeStruct((1, 128), F32)],
        compiler_params=_params(("arbitrary",)),
    )(f3, bf_row, dcq, dck)


A_Q_BLK = B_SEG // A_WIDTH
A_K_BLK = (B_SEG + A_WIDTH) // 128
A_V_BLK = A_K_BLK + 1
A_SEG_BLK = B_SEG // A_SEG
STACK = A_HEADS * BLOCK


def _lane_lo():
    return lax.broadcasted_iota(jnp.int32, (1, 128), 1) < HEAD_DIM


def _stack_heads(x, masked):
    lo = _lane_lo()
    blks = [x[:, 128 * j:128 * (j + 1)] for j in range(4)]
    if not masked:
        return jnp.concatenate(blks + blks, axis=0)
    zero = jnp.zeros_like(blks[0])
    return jnp.concatenate([jnp.where(lo, bk, zero) for bk in blks] + [jnp.where(lo, zero, bk) for bk in blks], axis=0)


def _unstack_heads(y):
    lo = _lane_lo()
    return jnp.concatenate([jnp.where(lo, y[128 * j:128 * (j + 1)], y[128 * (4 + j):128 * (5 + j)]) for j in range(4)], axis=1)


def _swa_bias(slopes):
    r_i = jnp.arange(STACK)[:, None]
    c_i = jnp.arange(3 * BLOCK)[None, :]
    seg = c_i >> 7
    out = []
    for n in range(3):
        qpos = n * BLOCK + (r_i & (BLOCK - 1))
        kpos = jnp.where(seg == 0, c_i, (n - 2) * BLOCK + c_i)
        dist = qpos - kpos
        band = (seg != 0) & (dist < BLOCK) & (kpos >= PREFIX)
        meta = (seg == 0) & (c_i >= N_PAD)
        out.append(jnp.where((dist >= 0) & (band | meta), -slopes * dist.astype(F32), NEG))
    return jnp.stack(out, axis=0)


def _swa_scores(q, kcat, n, slope, bias):
    s = lax.dot_general(q, kcat, _NT, preferred_element_type=F32) + bias
    further = slope * (-BLOCK * jnp.maximum(n - 2, 0)).astype(F32)
    return jnp.concatenate([s[:, 0:BLOCK] + further, s[:, BLOCK:]], axis=1)


def _swa_specs():
    def kv(col_blk):
        return [pl.BlockSpec((None, BLOCK, 128), lambda b, n: (b, 0, col_blk)),
                pl.BlockSpec((None, BLOCK, 128), lambda b, n: (b, jnp.maximum(n - 1, 0), col_blk)),
                pl.BlockSpec((None, BLOCK, 128), lambda b, n: (b, n, col_blk))]

    q_spec = pl.BlockSpec((None, BLOCK, A_WIDTH), lambda b, n: (b, n, A_Q_BLK))
    o_spec = pl.BlockSpec((None, BLOCK, A_WIDTH), lambda b, n: (b, n, 0))
    col = pl.BlockSpec((STACK, 1), lambda b, n: (0, 0))
    bias = pl.BlockSpec((None, STACK, 3 * BLOCK), lambda b, n: (jnp.minimum(n, 2), 0, 0))
    lse_spec = pl.BlockSpec((None, A_HEADS, BLOCK, 1), lambda b, n: (b, 0, n, 0))
    return q_spec, kv(A_K_BLK), kv(A_V_BLK), o_spec, [col, col, bias], lse_spec


def _swa_fwd(name, qkv, slopes, sinks, bias):
    b, l, _ = qkv.shape
    nb = l // BLOCK

    def body(q_ref, k0_ref, kp_ref, kc_ref, v0_ref, vp_ref, vc_ref, sl_ref, sk_ref, bias_ref, o_ref, lse_ref):
        n = pl.program_id(1)
        qs = _stack_heads(q_ref[...], True) * SCALE
        kcat = jnp.concatenate([k0_ref[...], kp_ref[...], kc_ref[...]], axis=0)
        vcat = jnp.concatenate([v0_ref[...], vp_ref[...], vc_ref[...]], axis=0)
        s = _swa_scores(qs, kcat, n, sl_ref[...], bias_ref[...])
        sink = sk_ref[...]
        m = jnp.maximum(jnp.max(s, axis=-1, keepdims=True), sink)
        p = jnp.exp(s - m)
        den = jnp.sum(p, axis=-1, keepdims=True) + jnp.exp(sink - m)
        o = lax.dot_general(p.astype(BF16), vcat, _NN, preferred_element_type=F32) / den
        o_ref[...] = _unstack_heads(o)
        lse_ref[...] = (m + jnp.log(den)).reshape(A_HEADS, BLOCK, 1)

    q_spec, k_specs, v_specs, o_spec, consts, lse_spec = _swa_specs()
    return pl.pallas_call(
        body, name=name, grid=(b, nb),
        in_specs=[q_spec] + k_specs + v_specs + consts, out_specs=[o_spec, lse_spec],
        out_shape=[jax.ShapeDtypeStruct((b, l, A_WIDTH), F32), jax.ShapeDtypeStruct((b, A_HEADS, l, 1), F32)],
        compiler_params=_params(("parallel", "parallel")),
    )(qkv, qkv, qkv, qkv, qkv, qkv, qkv, slopes, sinks, bias)


def _swa_bwd(name, qkv, o, lse, do, slopes, sinks, bias, dqkv):
    b, l, _ = qkv.shape
    nb = l // BLOCK

    def body(q_ref, k0_ref, kp_ref, kc_ref, v0_ref, vp_ref, vc_ref, o_ref, lse_ref, do_ref, sl_ref, sk_ref, bias_ref, _,
             dx_ref, ds_ref, dk_acc, dv_acc):
        bi = pl.program_id(0)
        n = pl.program_id(1)
        qs = _stack_heads(q_ref[...], True) * SCALE
        dos32 = _stack_heads(do_ref[...], True)
        dos = dos32.astype(BF16)
        os_ = _stack_heads(o_ref[...], False)
        lsev = lse_ref[...].reshape(STACK, 1)
        kcat = jnp.concatenate([k0_ref[...], kp_ref[...], kc_ref[...]], axis=0)
        vcat = jnp.concatenate([v0_ref[...], vp_ref[...], vc_ref[...]], axis=0)
        s = _swa_scores(qs, kcat, n, sl_ref[...], bias_ref[...])
        p = jnp.exp(s - lsev)
        dsum = jnp.sum(dos32 * os_, axis=-1, keepdims=True)
        dp = lax.dot_general(dos, vcat, _NT, preferred_element_type=F32)
        dsc = (p * (dp - dsum)).astype(BF16)
        dq = lax.dot_general(dsc, kcat, _NN, preferred_element_type=F32) * SCALE
        row0 = pl.multiple_of(n * BLOCK, BLOCK)
        dx_ref[pl.ds(row0, BLOCK), 0:A_WIDTH] = _unstack_heads(dq).astype(BF16)
        dkc = lax.dot_general(dsc, qs, _TN, preferred_element_type=F32)
        dvc = lax.dot_general(p.astype(BF16), dos, _TN, preferred_element_type=F32)

        @pl.when(n == 0)
        def _():
            dk_acc[...] = jnp.zeros_like(dk_acc)
            dv_acc[...] = jnp.zeros_like(dv_acc)

        starts = (0, pl.multiple_of(jnp.maximum(n - 1, 0) * BLOCK, BLOCK), row0)
        for t, st in enumerate(starts):
            dk_acc[pl.ds(st, BLOCK), :] += dkc[t * BLOCK:(t + 1) * BLOCK, :]
            dv_acc[pl.ds(st, BLOCK), :] += dvc[t * BLOCK:(t + 1) * BLOCK, :]

        @pl.when(n == nb - 1)
        def _():
            dx_ref[:, A_WIDTH:A_WIDTH + 128] = dk_acc[...].astype(BF16)
            dx_ref[:, A_WIDTH + 128:A_SEG] = dv_acc[...].astype(BF16)

        dsink = -(jnp.exp(sk_ref[...] - lsev) * dsum)
        r8 = lax.broadcasted_iota(jnp.int32, (8, 128), 0)
        acc = jnp.zeros((8, 128), F32)
        for hh in range(A_HEADS):
            acc = acc + jnp.where(r8 == hh, jnp.sum(dsink[hh * BLOCK:(hh + 1) * BLOCK, :]), 0.0)

        @pl.when((bi == 0) & (n == 0))
        def _():
            ds_ref[...] = jnp.zeros_like(ds_ref)

        ds_ref[...] += acc

    q_spec, k_specs, v_specs, o_spec, consts, lse_spec = _swa_specs()
    return pl.pallas_call(
        body, name=name, grid=(b, nb),
        in_specs=[q_spec] + k_specs + v_specs + [o_spec, lse_spec, o_spec] + consts + [pl.BlockSpec(memory_space=pl.ANY)],
        out_specs=[pl.BlockSpec((None, l, A_SEG), lambda bb, n: (bb, 0, A_SEG_BLK)),
                   pl.BlockSpec((8, 128), lambda bb, n: (0, 0))],
        out_shape=[jax.ShapeDtypeStruct(dqkv.shape, BF16), jax.ShapeDtypeStruct((8, 128), F32)],
        scratch_shapes=[pltpu.VMEM((l, 128), F32), pltpu.VMEM((l, 128), F32)],
        input_output_aliases={13: 0},
        compiler_params=_params(("arbitrary", "arbitrary")),
    )(qkv, qkv, qkv, qkv, qkv, qkv, qkv, o, lse, do, slopes, sinks, bias, dqkv)


def _fox_mask(qk, ck, i):
    kh = qk.shape[1]
    qpos = i * BLOCK + lax.broadcasted_iota(jnp.int32, (BLOCK, kh), 0)
    kpos = lax.broadcasted_iota(jnp.int32, (BLOCK, kh), 1)
    return jnp.where((kpos <= qpos) & (kpos >= N_PAD), qk - ck, NEG)


def _pick_head(x, hh):
    lo = _lane_lo()
    return jnp.where(lo if hh == 0 else jnp.logical_not(lo), x, jnp.zeros_like(x))


def _both_heads(x):
    return jnp.concatenate([_pick_head(x, 0), _pick_head(x, 1)], axis=0)


def _fox_specs(l):
    pair = pl.BlockSpec((None, l, PAIR_W), lambda bi, hp: (bi, 0, hp))
    half = pl.BlockSpec((None, l, 128), lambda bi, hp: (bi, 0, hp))
    colv = pl.BlockSpec((None, 2, l, 1), lambda bi, hp: (bi, hp, 0, 0))
    rowv = pl.BlockSpec((None, 2, 1, l), lambda bi, hp: (bi, hp, 0, 0))
    return pair, half, colv, rowv


def _fox_fwd(name, qkv, c_col, c_row):
    b, l, _ = qkv.shape
    nb = l // BLOCK

    def body(x_ref, cc_ref, cr_ref, o_ref, lse_ref):
        for i in range(nb):
            rows = slice(i * BLOCK, (i + 1) * BLOCK)
            kh = (i + 1) * BLOCK
            qblk = x_ref[rows, 0:128]
            kv = x_ref[0:kh, 128:256]
            vv = x_ref[0:kh, 256:384]
            qk = lax.dot_general(_both_heads(qblk) * SCALE, kv, _NT, preferred_element_type=F32)
            ps, dens = [], []
            for hh in range(2):
                s = _fox_mask(qk[hh * BLOCK:(hh + 1) * BLOCK], cr_ref[hh, :, 0:kh], i)
                m = jnp.max(s, axis=-1, keepdims=True)
                p = jnp.exp(s - m)
                den = jnp.sum(p, axis=-1, keepdims=True)
                ps.append(p.astype(BF16))
                dens.append(den)
                lse_ref[hh, rows, :] = (m + jnp.log(den)) + cc_ref[hh, rows, :]
            pv = lax.dot_general(jnp.concatenate(ps, axis=0), vv, _NN, preferred_element_type=F32)
            o_ref[rows, :] = jnp.where(_lane_lo(), pv[0:BLOCK] / dens[0], pv[BLOCK:2 * BLOCK] / dens[1]).astype(BF16)

    pair, half, colv, rowv = _fox_specs(l)
    return pl.pallas_call(
        body, name=name, grid=(b, 4), in_specs=[pair, colv, rowv], out_specs=[half, colv],
        out_shape=[jax.ShapeDtypeStruct((b, l, B_WIDTH), BF16), jax.ShapeDtypeStruct((b, B_HEADS, l, 1), F32)],
        compiler_params=_params(("parallel", "parallel")),
    )(qkv, c_col, c_row)


def _fox_bwd(name, qkv, c_col, c_row, o, lse, do):
    b, l, _ = qkv.shape
    nb = l // BLOCK

    def body(x_ref, cc_ref, cr_ref, o_ref, lse_ref, do_ref, dx_ref, dcq_ref, dck_ref, dk_acc, dv_acc):
        dk_acc[...] = jnp.zeros_like(dk_acc)
        dv_acc[...] = jnp.zeros_like(dv_acc)
        dck_ref[...] = jnp.zeros_like(dck_ref)
        for i in range(nb):
            rows = slice(i * BLOCK, (i + 1) * BLOCK)
            kh = (i + 1) * BLOCK
            qblk = x_ref[rows, 0:128]
            kv = x_ref[0:kh, 128:256]
            vv = x_ref[0:kh, 256:384]
            doblk = do_ref[rows, :]
            ov = o_ref[rows, :].astype(F32)
            q2 = _both_heads(qblk) * SCALE
            do2 = _both_heads(doblk)
            qk = lax.dot_general(q2, kv, _NT, preferred_element_type=F32)
            dp = lax.dot_general(do2, vv, _NT, preferred_element_type=F32)
            ps, dss = [], []
            for hh in range(2):
                half = slice(hh * BLOCK, (hh + 1) * BLOCK)
                s = _fox_mask(qk[half], cr_ref[hh, :, 0:kh], i)
                p = jnp.exp(s - (lse_ref[hh, rows, :] - cc_ref[hh, rows, :]))
                dsum = jnp.sum(do2[half].astype(F32) * ov, axis=-1, keepdims=True)
                ds = p * (dp[half] - dsum)
                ps.append(p.astype(BF16))
                dss.append(ds.astype(BF16))
                dcq_ref[hh, rows, :] = jnp.sum(ds, axis=-1, keepdims=True)
                dck_ref[hh, :, 0:kh] -= jnp.sum(ds, axis=0, keepdims=True)
            p2 = jnp.concatenate(ps, axis=0)
            ds2 = jnp.concatenate(dss, axis=0)
            dq = lax.dot_general(ds2, kv, _NN, preferred_element_type=F32) * SCALE
            dk_acc[0:kh, :] += lax.dot_general(ds2, q2, _TN, preferred_element_type=F32)
            dv_acc[0:kh, :] += lax.dot_general(p2, do2, _TN, preferred_element_type=F32)
            dx_ref[rows, 0:128] = jnp.where(_lane_lo(), dq[0:BLOCK], dq[BLOCK:2 * BLOCK]).astype(BF16)
        dx_ref[:, 128:256] = dk_acc[...].astype(BF16)
        dx_ref[:, 256:384] = dv_acc[...].astype(BF16)

    pair, half, colv, rowv = _fox_specs(l)
    return pl.pallas_call(
        body, name=name, grid=(b, 4), in_specs=[pair, colv, rowv, half, colv, half],
        out_specs=[pair, colv, rowv],
        out_shape=[jax.ShapeDtypeStruct(qkv.shape, BF16), jax.ShapeDtypeStruct((b, B_HEADS, l, 1), F32),
                   jax.ShapeDtypeStruct((b, B_HEADS, 1, l), F32)],
        scratch_shapes=[pltpu.VMEM((l, 128), F32), pltpu.VMEM((l, 128), F32)],
        compiler_params=_params(("parallel", "parallel")),
    )(qkv, c_col, c_row, o, lse, do)


_FLIPS = ((0, 0, 1), (0, 1, 0), (0, 1, 1), (1, 0, 0), (1, 0, 1), (1, 1, 0), (1, 1, 1))


def _exchange(name, gather, scatter):
    ng, ns = len(gather), len(scatter)
    na = ng + ns
    npeer = len(_FLIPS)

    def body(*refs):
        ins = refs[:na]
        outs = refs[na:2 * na]
        send_sems, recv_sems, loc_sems = refs[2 * na:]
        x, y, c = lax.axis_index("x"), lax.axis_index("y"), lax.axis_index("c")
        me = 4 * x + 2 * y + c
        peers = []
        for fx, fy, fc in _FLIPS:
            px = 1 - x if fx else x
            py = 1 - y if fy else y
            pc = 1 - c if fc else c
            peers.append(((px, py, pc), 4 * px + 2 * py + pc))

        def remote(a, kk):
            dev, lin = peers[kk]
            src = ins[a] if a < ng else ins[a].at[lin]
            return pltpu.make_async_remote_copy(src_ref=src, dst_ref=outs[a].at[me], send_sem=send_sems.at[a * npeer + kk],
                                                recv_sem=recv_sems.at[a * npeer + kk], device_id=dev, device_id_type=MESH_ID)

        def arrival(a, kk):
            dev, lin = peers[kk]
            src = ins[a] if a < ng else ins[a].at[lin]
            return pltpu.make_async_remote_copy(src_ref=src, dst_ref=outs[a].at[lin], send_sem=send_sems.at[a * npeer + kk],
                                                recv_sem=recv_sems.at[a * npeer + kk], device_id=dev, device_id_type=MESH_ID)

        local = []
        for a in range(na):
            src = ins[a] if a < ng else ins[a].at[me]
            cp = pltpu.make_async_copy(src, outs[a].at[me], loc_sems.at[a])
            cp.start()
            local.append(cp)
        sent = [remote(a, kk) for kk in range(npeer) for a in range(na)]
        for cp in sent:
            cp.start()
        for kk in range(npeer):
            for a in range(na):
                arrival(a, kk).wait_recv()
        for cp in sent:
            cp.wait_send()
        for cp in local:
            cp.wait()

    arrs = list(gather) + list(scatter)
    out_shape = [jax.ShapeDtypeStruct((N_DEV,) + tuple(a.shape), a.dtype) for a in gather]
    out_shape += [jax.ShapeDtypeStruct(tuple(a.shape), a.dtype) for a in scatter]
    anyspec = pl.BlockSpec(memory_space=pl.ANY)
    return pl.pallas_call(
        body, name=name, in_specs=[anyspec] * na, out_specs=[anyspec] * na, out_shape=out_shape,
        scratch_shapes=[pltpu.SemaphoreType.DMA((na * npeer,)), pltpu.SemaphoreType.DMA((na * npeer,)),
                        pltpu.SemaphoreType.DMA((na,))],
        compiler_params=pltpu.CompilerParams(has_side_effects=True),
    )(*arrs)


def _peer_table():
    x, y, c = lax.axis_index("x"), lax.axis_index("y"), lax.axis_index("c")
    me = 4 * x + 2 * y + c
    peers = []
    for fx, fy, fc in _FLIPS:
        px = 1 - x if fx else x
        py = 1 - y if fy else y
        pc = 1 - c if fc else c
        peers.append(((px, py, pc), 4 * px + 2 * py + pc))
    return me, peers


_HBM = pl.BlockSpec(memory_space=pltpu.HBM)
_SEM = pl.BlockSpec(memory_space=pltpu.SEMAPHORE)
_ANY = pl.BlockSpec(memory_space=pl.ANY)
_EFFECT = pltpu.SideEffectType.DATAFLOW_SIDE_EFFECTING


def _split_copy(srcs_are_pieces, src_refs, land_refs, send_sem, recv_sem, a, kk, me, peers, arriving):
    dev, lin = peers[kk]
    npeer = len(_FLIPS)
    src = src_refs[a] if srcs_are_pieces[a] else src_refs[a].at[lin]
    dst = land_refs[a].at[lin] if arriving else land_refs[a].at[me]
    return pltpu.make_async_remote_copy(src_ref=src, dst_ref=dst, send_sem=send_sem.at[a * npeer + kk],
                                        recv_sem=recv_sem.at[a * npeer + kk], device_id=dev, device_id_type=MESH_ID)


def _xchg_start(name, gather, scatter, after=None):
    me_out = 4 * lax.axis_index("x") + 2 * lax.axis_index("y") + lax.axis_index("c")
    srcs = list(gather) + list(scatter)
    is_piece = [True] * len(gather) + [False] * len(scatter)
    lands = []
    for a, piece in zip(srcs, is_piece):
        own = a[None] if piece else lax.dynamic_slice_in_dim(a, me_out, 1, axis=0)
        shape = ((N_DEV,) + tuple(a.shape)) if piece else tuple(a.shape)
        start = (me_out,) + (0,) * (len(shape) - 1)
        lands.append(lax.dynamic_update_slice(lax.empty(shape, a.dtype), own, start))
    n = len(srcs)
    nsem = n * len(_FLIPS)
    has_after = after is not None

    def body(*refs):
        src_refs = refs[:n]
        land_refs = refs[n:2 * n]
        outs = refs[2 * n + (1 if has_after else 0):]
        send_sem, recv_sem = outs[0], outs[1]
        token = outs[-1]
        me, peers = _peer_table()
        for kk in range(len(_FLIPS)):
            for a in range(n):
                _split_copy(is_piece, src_refs, land_refs, send_sem, recv_sem, a, kk, me, peers, False).start()
        token[...] = jnp.zeros_like(token)

    out_shape = ([pltpu.SemaphoreType.DMA((nsem,)), pltpu.SemaphoreType.DMA((nsem,))]
                 + [pltpu.HBM(tuple(a.shape), a.dtype) for a in srcs] + [pltpu.HBM(tuple(a.shape), a.dtype) for a in lands]
                 + [jax.ShapeDtypeStruct((8, 128), F32)])
    args = [pltpu.with_memory_space_constraint(a, pltpu.HBM) for a in srcs + lands] + ([after] if has_after else [])
    res = pl.pallas_call(
        body, name=name, out_shape=out_shape,
        in_specs=[_HBM] * (2 * n) + ([_ANY] if has_after else []),
        out_specs=[_SEM, _SEM] + [_HBM] * (2 * n) + [pl.BlockSpec(memory_space=pltpu.VMEM)],
        input_output_aliases={i: 2 + i for i in range(2 * n)},
        compiler_params=pltpu.CompilerParams(has_side_effects=_EFFECT),
    )(*args)
    state = (res[0], res[1], list(res[2:2 + n]), list(res[2 + n:2 + 2 * n]), is_piece)
    return state, res[-1]


def _xchg_wait(name, state, after):
    send_sem, recv_sem, srcs, lands, is_piece = state
    n = len(srcs)

    def body(*refs):
        src_refs = refs[:n]
        land_refs = refs[n:2 * n]
        s_sem, r_sem = refs[2 * n], refs[2 * n + 1]
        me, peers = _peer_table()
        for kk in range(len(_FLIPS)):
            for a in range(n):
                cp = _split_copy(is_piece, src_refs, land_refs, s_sem, r_sem, a, kk, me, peers, True)
                cp.wait_send()
                cp.wait_recv()

    out_shape = [pltpu.HBM(tuple(a.shape), a.dtype) for a in srcs] + [pltpu.HBM(tuple(a.shape), a.dtype) for a in lands]
    res = pl.pallas_call(
        body, name=name, out_shape=out_shape,
        in_specs=[_HBM] * (2 * n) + [_SEM, _SEM, _ANY], out_specs=[_HBM] * (2 * n),
        input_output_aliases={i: i for i in range(2 * n)},
        compiler_params=pltpu.CompilerParams(has_side_effects=_EFFECT),
    )(*srcs, *lands, send_sem, recv_sem, after)
    return list(res[n:])


_SIB = (0, 0, 1)
_ICI = ((0, 1, 0), (1, 0, 0), (1, 1, 0))


def _flip(fl):
    x, y, c = lax.axis_index("x"), lax.axis_index("y"), lax.axis_index("c")
    px = 1 - x if fl[0] else x
    py = 1 - y if fl[1] else y
    pc = 1 - c if fl[2] else c
    return (px, py, pc), 4 * px + 2 * py + pc


def _gather2_start(name, pieces, after=None):
    me_out = 4 * lax.axis_index("x") + 2 * lax.axis_index("y") + lax.axis_index("c")
    pieces = list(pieces)
    n = len(pieces)
    lands = [lax.dynamic_update_slice(lax.empty((N_DEV,) + tuple(a.shape), a.dtype), a[None],
                                      (me_out,) + (0,) * a.ndim) for a in pieces]
    first = (_SIB,) + _ICI
    has_after = after is not None

    def body(*refs):
        src_refs, land_refs = refs[:n], refs[n:2 * n]
        outs = refs[2 * n + (1 if has_after else 0):]
        send_sem, recv_sem, token = outs[0], outs[1], outs[-1]
        _, me = _flip((0, 0, 0))
        for kk, fl in enumerate(first):
            dev, _ = _flip(fl)
            for a in range(n):
                pltpu.make_async_remote_copy(src_ref=src_refs[a], dst_ref=land_refs[a].at[me],
                                             send_sem=send_sem.at[a * 4 + kk], recv_sem=recv_sem.at[a * 4 + kk],
                                             device_id=dev, device_id_type=MESH_ID).start()
        token[...] = jnp.zeros_like(token)

    hbm = [pltpu.HBM(tuple(a.shape), a.dtype) for a in pieces + lands]
    res = pl.pallas_call(
        body, name=name,
        out_shape=[pltpu.SemaphoreType.DMA((4 * n,)), pltpu.SemaphoreType.DMA((4 * n,))] + hbm
        + [jax.ShapeDtypeStruct((8, 128), F32)],
        in_specs=[_HBM] * (2 * n) + ([_ANY] if has_after else []),
        out_specs=[_SEM, _SEM] + [_HBM] * (2 * n) + [pl.BlockSpec(memory_space=pltpu.VMEM)],
        input_output_aliases={i: 2 + i for i in range(2 * n)},
        compiler_params=pltpu.CompilerParams(has_side_effects=_EFFECT),
    )(*([pltpu.with_memory_space_constraint(a, pltpu.HBM) for a in pieces + lands] + ([after] if has_after else [])))
    return (res[0], res[1], list(res[2:2 + n]), list(res[2 + n:2 + 2 * n])), res[-1]


def _gather2_forward(name, state, after):
    send_a, recv_a, pieces, lands = state
    n = len(pieces)
    first = (_SIB,) + _ICI

    def body(*refs):
        src_refs, land_refs = refs[:n], refs[n:2 * n]
        s_a, r_a = refs[2 * n], refs[2 * n + 1]
        outs = refs[2 * n + 3:]
        send_b, recv_b, token = outs[0], outs[1], outs[-1]
        for kk, fl in enumerate(first):
            dev, lin = _flip(fl)
            for a in range(n):
                cp = pltpu.make_async_remote_copy(src_ref=src_refs[a], dst_ref=land_refs[a].at[lin],
                                                  send_sem=s_a.at[a * 4 + kk], recv_sem=r_a.at[a * 4 + kk],
                                                  device_id=dev, device_id_type=MESH_ID)
                cp.wait_send()
                cp.wait_recv()
        sib, _ = _flip(_SIB)
        for j, fl in enumerate(_ICI):
            _, lin = _flip(fl)
            for a in range(n):
                pltpu.make_async_remote_copy(src_ref=land_refs[a].at[lin], dst_ref=land_refs[a].at[lin],
                                             send_sem=send_b.at[a * 3 + j], recv_sem=recv_b.at[a * 3 + j],
                                             device_id=sib, device_id_type=MESH_ID).start()
        token[...] = jnp.zeros_like(token)

    hbm = [pltpu.HBM(tuple(a.shape), a.dtype) for a in pieces + lands]
    res = pl.pallas_call(
        body, name=name,
        out_shape=[pltpu.SemaphoreType.DMA((3 * n,)), pltpu.SemaphoreType.DMA((3 * n,))] + hbm
        + [jax.ShapeDtypeStruct((8, 128), F32)],
        in_specs=[_HBM] * (2 * n) + [_SEM, _SEM, _ANY],
        out_specs=[_SEM, _SEM] + [_HBM] * (2 * n) + [pl.BlockSpec(memory_space=pltpu.VMEM)],
        input_output_aliases={i: 2 + i for i in range(2 * n)},
        compiler_params=pltpu.CompilerParams(has_side_effects=_EFFECT),
    )(*pieces, *lands, send_a, recv_a, after)
    return (res[0], res[1], list(res[2 + n:2 + 2 * n])), res[-1]


def _gather2_wait(name, state, after):
    send_b, recv_b, lands = state
    n = len(lands)

    def body(*refs):
        land_refs = refs[:n]
        s_b, r_b = refs[n], refs[n + 1]
        sib, _ = _flip(_SIB)
        for j, fl in enumerate(_ICI):
            _, sent = _flip(fl)
            _, arriving = _flip((fl[0], fl[1], 1))
            for a in range(n):
                cp = pltpu.make_async_remote_copy(src_ref=land_refs[a].at[sent], dst_ref=land_refs[a].at[arriving],
                                                  send_sem=s_b.at[a * 3 + j], recv_sem=r_b.at[a * 3 + j],
                                                  device_id=sib, device_id_type=MESH_ID)
                cp.wait_send()
                cp.wait_recv()

    res = pl.pallas_call(
        body, name=name, out_shape=[pltpu.HBM(tuple(a.shape), a.dtype) for a in lands],
        in_specs=[_HBM] * n + [_SEM, _SEM, _ANY], out_specs=[_HBM] * n,
        input_output_aliases={i: i for i in range(n)},
        compiler_params=pltpu.CompilerParams(has_side_effects=_EFFECT),
    )(*lands, send_b, recv_b, after)
    return list(res)


def _adam_math(w, g, m, v):
    m = ADAM_B1 * m + (1.0 - ADAM_B1) * g
    v = ADAM_B2 * v + (1.0 - ADAM_B2) * (g * g)
    m_hat = m / (1.0 - ADAM_B1 ** ADAM_STEP)
    v_hat = v / (1.0 - ADAM_B2 ** ADAM_STEP)
    delta = -ADAM_LR * (m_hat / (jnp.sqrt(v_hat) + ADAM_EPS) + ADAM_WD * w)
    return delta, m, v


def _adam(name, w, m, v, parts):
    r, c = w.shape
    npart, _, cp = parts.shape
    tr = _pick(r, (256, 176, 128, 64, 16, 8, 1))

    def body(w_ref, m_ref, v_ref, p_ref, g_ref, d_ref, mo_ref, vo_ref):
        g = p_ref[0].astype(F32)
        for pp in range(1, npart):
            g = g + p_ref[pp].astype(F32)
        g = g[:, 0:c]
        delta, mn, vn = _adam_math(w_ref[...], g, m_ref[...], v_ref[...])
        g_ref[...] = g
        d_ref[...] = delta
        mo_ref[...] = mn
        vo_ref[...] = vn

    blk = pl.BlockSpec((tr, c), lambda i: (i, 0))
    out = jax.ShapeDtypeStruct((r, c), F32)
    return pl.pallas_call(
        body, name=name, grid=(r // tr,),
        in_specs=[blk, blk, blk, pl.BlockSpec((npart, tr, cp), lambda i: (0, i, 0))],
        out_specs=[blk, blk, blk, blk], out_shape=[out, out, out, out], compiler_params=_params(("parallel",)),
    )(w, m, v, parts)


def _small_sum(name, packs):
    def body(p_ref, o_ref):
        tot = p_ref[0]
        for pp in range(1, N_DEV):
            tot = tot + p_ref[pp]
        o_ref[0:8, :] = tot[0:8, :]
        o_ref[8:24, :] = tot[8:24, :] + tot[24:40, :]

    return pl.pallas_call(body, name=name, out_shape=jax.ShapeDtypeStruct((24, D_MODEL), F32),
                          compiler_params=_params())(packs)


def _local_step(x, tgt, g1, gm, g2, gf, b_forget, sinks, weights, send):
    b, s, _ = x.shape
    l = s + PREFIX
    t = b * l
    w1i, meta = weights("ffn1_in", x)
    h0 = jnp.concatenate([jnp.zeros((b, N_PAD, D_MODEL), F32), jnp.broadcast_to(meta[None], (b, N_META, D_MODEL)), x],
                         axis=1).reshape(t, D_MODEL)

    n1 = _rms_fwd("rms1_fwd", h0, g1)
    gu1, a1 = _ffn_in_fwd("ffn1_in_fwd", n1, w1i, dep=weights("ffn1_out:forward", n1))
    (w1o,) = weights("ffn1_out", weights("mix:forward", a1))
    h1 = _mm_nn("ffn1_out_fwd", a1, w1o, alpha=0.5, res=h0, tm_c=(544, 384, 256, 128), tn_c=(1024,))
    wi, wa, wb, wo = weights("mix", h1)
    um = _rms_fwd("rmsm_fwd", h1, gm)
    qkv = _mm_nn("proj_qkv_fwd", um, wi, out_dtype=BF16, tn_c=(768,), cols=(0, QKV_W))
    gates = _mm_nn("proj_gates_fwd", um, wi, tn_c=(1024,), cols=(P_GATES, 2 * D_MODEL), dep=weights("ffn2:forward", qkv))
    f2 = _mm_nn("proj_f_fwd", um, wi, tn_c=(128,), cols=(P_F, 128))
    qkv3 = qkv.reshape(b, l, QKV_W)
    f3 = f2.reshape(b, l, 128)
    bf_row = jnp.pad(b_forget, ((0, 0), (0, 128 - B_HEADS)))
    c_col, c_row = _fgate_fwd("fgate_fwd", f3, bf_row)
    head_of_row = jnp.arange(STACK) // BLOCK
    slopes = jnp.exp2(-8.0 * (head_of_row + 1).astype(F32) / A_HEADS).reshape(STACK, 1)
    sink_rows = jnp.repeat(sinks.reshape(A_HEADS), BLOCK).reshape(STACK, 1)
    swa_bias = _swa_bias(slopes)
    oa3, lse_a = _swa_fwd("swa_fwd", qkv3, slopes, sink_rows, swa_bias)
    ob3, lse_b = _fox_fwd("fox_fwd", qkv3, c_col, c_row)
    oa = oa3.reshape(t, A_WIDTH)
    ob = ob3.reshape(t, B_WIDTH)
    mixed, ya, yb = _branch_gate_fwd("branch_gate_fwd", oa, ob, wa, wb, gates)
    h2 = _mm_nn("mix_out_fwd", mixed, wo, res=h1, tn_c=(1024,))
    w2i, w2o = weights("ffn2", h2)
    n2 = _rms_fwd("rms2_fwd", h2, g2)
    gu2, a2 = _ffn_in_fwd("ffn2_in_fwd", n2, w2i)
    h3 = _mm_nn("ffn2_out_fwd", a2, w2o, alpha=0.5, res=h2, tm_c=(544, 384, 256, 128), tn_c=(1024,))

    dh3_3, loss_blk, dgf = _loss_head("loss_head", h3.reshape(b, l, D_MODEL), gf, tgt)
    dh3 = dh3_3.reshape(t, D_MODEL)

    def ffn_bwd(tag, dh, h_in, g_norm, n_in, gu, a, w_in_blk, w_out):
        dw_out = _mm_tn(tag + "_out_bwd_w", a, dh, alpha=0.5)
        dgu = _ffn_out_bwd_x(tag + "_out_bwd_x", dh, w_out, gu, dep=send(tag + "_out", (dw_out,)))
        dw_in = _ffn_in_bwd_w(tag + "_in_bwd_w", n_in, dgu)
        return _ffn_in_bwd_x(tag + "_in_bwd_x", dgu, w_in_blk, h_in, g_norm, dh, dep=send(tag + "_in", (dw_in,)))

    dh2, dg2 = ffn_bwd("ffn2", dh3, h2, g2, n2, gu2, a2, w2i, w2o)

    dwo = _mm_tn("mix_out_bwd_w", mixed, dh2)
    dya, dyb, dgates = _mix_out_gate_bwd("mix_out_gate_bwd", dh2, wo, gates, ya, yb)
    doa = _mm_nt("branch_a_bwd_x", dya, wa, tn_c=(512,))
    dob = _mm_nt("branch_b_bwd_x", dyb, wb, out_dtype=BF16, tn_c=(512,))
    dwa = _mm_tn("branch_a_bwd_w", oa, dya, tm_c=(512,))
    dwb = _mm_tn("branch_b_bwd_w", ob, dyb, tm_c=(512,))
    dqkv3, dcq, dck = _fox_bwd("fox_bwd", qkv3, c_col, c_row, ob3, lse_b, dob.reshape(b, l, B_WIDTH))
    dqkv3, dsink = _swa_bwd("swa_bwd", qkv3, oa3, lse_a, doa.reshape(b, l, A_WIDTH), slopes, sink_rows, swa_bias, dqkv3)
    dqkv = dqkv3.reshape(t, QKV_W)
    df3, dbf = _fgate_bwd("fgate_bwd", f3, bf_row, dcq, dck)
    df = df3.reshape(t, 128)
    dwi_qkv = _mm_tn("proj_qkv_bwd_w", um, dqkv, tm_c=(512,), tn_c=(768,))
    dwi_g = _mm_tn("proj_gates_bwd_w", um, dgates, tm_c=(512,), tn_c=(512,))
    dwi_f = _mm_tn("proj_f_bwd_w", um, df, tm_c=(512,), tn_c=(128,))
    token = send("mix", (dwi_qkv, dwi_g, dwi_f, dwa, dwb, dwo))
    dh1, dgm = _proj_bwd_x("proj_bwd_x", dqkv, dgates, df, wi, h1, gm, dh2, dep=token)

    dh0, dg1 = ffn_bwd("ffn1", dh1, h0, g1, n1, gu1, a1, w1i, w1o)
    dh0_3 = dh0.reshape(b, l, D_MODEL)
    grad_x = dh0_3[:, PREFIX:, :]
    dmeta = dh0_3[:, N_PAD:PREFIX, :].reshape(b * N_META, D_MODEL)

    misc = jnp.concatenate([dbf[:, 0:B_HEADS], dsink[:, 0].reshape(1, A_HEADS), loss_blk[0:1, 0:1]], axis=1)
    misc = jnp.pad(misc, ((0, 0), (0, D_MODEL - misc.shape[1])))
    row = lax.broadcasted_iota(jnp.int32, (8, D_MODEL), 0)
    vec = jnp.zeros((8, D_MODEL), F32)
    for i, piece in enumerate((dg1, dgm, dg2, dgf, misc)):
        vec = jnp.where(row == i, piece, vec)
    small = jnp.concatenate([vec, dmeta], axis=0)
    return grad_x, small


def _pad_to(a, rows, cols):
    return jnp.pad(a, ((0, rows - a.shape[0]), (0, cols - a.shape[1])))


def _ffn_out_from_gathered(g):
    w = g.reshape(4, FF_SHARD, D_MODEL)
    return jnp.pad(w, ((0, 0), (0, FF_SHARD_P - FF_SHARD), (0, 0))).reshape(D_FF_P, D_MODEL)


def _ffn_out_to_scatter(dw):
    return dw.reshape(4, FF_SHARD_P, D_MODEL)[:, 0:FF_SHARD, :].reshape(N_DEV, FFO_SHARD, D_MODEL)


def _proj_segments():
    segs = [(HEAD_DIM * h, HEAD_DIM, 0, B_SEG + HEAD_DIM * A_HEAD_ORDER.index(h)) for h in range(A_HEADS)]
    segs += [(512, 128, 0, B_SEG + A_WIDTH), (640, 128, 0, B_SEG + A_WIDTH + 128)]
    for first, off in ((768, 0), (1280, 128), (1792, 256)):
        segs += [(first + 128 * hp, 128, 0, PAIR_W * hp + off) for hp in range(4)]
    segs += [(2304, B_HEADS, 2, 0), (2312, 2 * D_MODEL, 1, 0)]
    return segs


def _proj_from_gathered(g):
    def cols(first, width):
        out = []
        for p in range(N_DEV):
            lo, hi = max(first, WIN_SHARD * p), min(first + width, WIN_SHARD * (p + 1))
            if lo < hi:
                out.append(g[p, :, lo - WIN_SHARD * p:hi - WIN_SHARD * p])
        return out

    parts = []
    for arr in (0, 1, 2):
        for first, width, _, _ in sorted((s for s in _proj_segments() if s[2] == arr), key=lambda s: s[3]):
            parts += cols(first, width)
        if arr == 0:
            parts.append(jnp.zeros((D_MODEL, P_GATES - QKV_W), g.dtype))
    parts.append(jnp.zeros((D_MODEL, 128 - B_HEADS), g.dtype))
    return jnp.concatenate(parts, axis=1)


def _proj_to_scatter(dqkv_w, dg_w, df_w):
    arrays = (dqkv_w, dg_w, df_w)
    segs = sorted(_proj_segments())
    blocks = []
    for p in range(N_DEV):
        parts = []
        for first, width, arr, at in segs:
            lo, hi = max(first, WIN_SHARD * p), min(first + width, WIN_SHARD * (p + 1))
            if lo < hi:
                parts.append(arrays[arr][:, at + lo - first:at + hi - first])
        parts.append(jnp.zeros((D_MODEL, WIN_SHARD_P - WIN_SHARD), dqkv_w.dtype))
        blocks.append(jnp.concatenate(parts, axis=1))
    return jnp.stack(blocks, axis=0)


def _a_rows_from_natural(w):
    return jnp.concatenate([w[HEAD_DIM * h:HEAD_DIM * (h + 1)] for h in A_HEAD_ORDER], axis=0)


def _a_rows_to_natural(w):
    return jnp.concatenate([w[HEAD_DIM * A_HEAD_ORDER.index(h):HEAD_DIM * (A_HEAD_ORDER.index(h) + 1)]
                            for h in range(A_HEADS)], axis=0)


def kernel(x, meta_tokens, ffn1_norm, ffn1_w_in, ffn1_w_out, mix_norm, w_in, b_forget, attn_sinks, w_branch_a, w_branch_b, w_out, ffn2_norm, ffn2_w_in, ffn2_w_out, final_norm, loss_target, m_meta_tokens, m_ffn1_norm, m_ffn1_w_in, m_ffn1_w_out, m_mix_norm, m_w_in, m_b_forget, m_attn_sinks, m_w_branch_a, m_w_branch_b, m_w_out, m_ffn2_norm, m_ffn2_w_in, m_ffn2_w_out, m_final_norm, v_meta_tokens, v_ffn1_norm, v_ffn1_w_in, v_ffn1_w_out, v_mix_norm, v_w_in, v_b_forget, v_attn_sinks, v_w_branch_a, v_w_branch_b, v_w_out, v_ffn2_norm, v_ffn2_w_in, v_ffn2_w_out, v_final_norm):
    me = 4 * lax.axis_index("x") + 2 * lax.axis_index("y") + lax.axis_index("c")

    shards = (
        _pad_to(ffn1_w_in[0].astype(BF16), D_MODEL, FF_SHARD_P),
        ffn1_w_out[0].astype(BF16),
        _pad_to(w_in[0].astype(BF16), D_MODEL, WIN_SHARD_P),
        w_branch_a[0].astype(BF16), w_branch_b[0].astype(BF16), w_out[0].astype(BF16),
        _pad_to(ffn2_w_in[0].astype(BF16), D_MODEL, FF_SHARD_P),
        ffn2_w_out[0].astype(BF16),
        meta_tokens,
    )
    s1i, s1o, swi, swa, swb, swo, s2i, s2o, smeta = shards
    first_level, second_level = {}, {}
    first_level["ffn1_in"], tok = _gather2_start("gather_ffn1_in_start", (s1i, smeta))
    first_level["ffn1_out"], tok = _gather2_start("gather_ffn1_out_start", (s1o,), after=tok)
    first_level["mix"], tok = _gather2_start("gather_mix_start", (swi, swa, swb, swo), after=tok)
    first_level["ffn2"], tok = _gather2_start("gather_ffn2_start", (s2i, s2o), after=tok)
    started = {"tok": tok}

    def weights(group, after):
        if group.endswith(":forward"):
            group = group[:-len(":forward")]
            second_level[group], token = _gather2_forward("gather_" + group + "_forward", first_level[group], after)
            return token
        if group == "ffn1_in":
            after = weights("ffn1_in:forward", started["tok"])
        got = _gather2_wait("gather_" + group + "_wait", second_level[group], after)
        if group == "mix":
            gwi, gwa, gwb, gwo = got
            return (_proj_from_gathered(gwi), _a_rows_from_natural(gwa.transpose(1, 0, 2).reshape(A_WIDTH, D_MODEL)),
                    gwb.transpose(1, 0, 2).reshape(B_WIDTH, D_MODEL), gwo.reshape(D_MODEL, D_MODEL))
        if group == "ffn1_in":
            return got[0].reshape(2, 4, D_MODEL, FF_SHARD_P), got[1].transpose(1, 0, 2).reshape(N_META, D_MODEL)
        if group == "ffn1_out":
            return (_ffn_out_from_gathered(got[0]),)
        return got[0].reshape(2, 4, D_MODEL, FF_SHARD_P), _ffn_out_from_gathered(got[1])

    scatter_state = {}

    def send(group, grads):
        if group == "mix":
            dwi_qkv, dwi_g, dwi_f, dwa, dwb, dwo = grads
            dwa = _a_rows_to_natural(dwa)
            blocks = (_proj_to_scatter(dwi_qkv, dwi_g, dwi_f), dwa.reshape(A_WIDTH, N_DEV, 128).transpose(1, 0, 2),
                      dwb.reshape(B_WIDTH, N_DEV, 128).transpose(1, 0, 2), dwo.reshape(N_DEV, 128, D_MODEL))
        elif group.endswith("_in"):
            blocks = (grads[0].reshape(N_DEV, D_MODEL, FF_SHARD_P),)
        else:
            blocks = (_ffn_out_to_scatter(grads[0]),)
        scatter_state[group], token = _xchg_start("scatter_" + group + "_start", (), blocks)
        return token

    gf = final_norm.reshape(1, D_MODEL)
    grad_x, small = _local_step(x, loss_target, ffn1_norm, mix_norm, ffn2_norm, gf, b_forget, attn_sinks, weights, send)

    small_state, after = _xchg_start("gather_small_start", (small,), ())
    out = {}
    updates = (
        ("ffn2_out", (("ffn2_w_out", ffn2_w_out, m_ffn2_w_out, v_ffn2_w_out),)),
        ("ffn2_in", (("ffn2_w_in", ffn2_w_in, m_ffn2_w_in, v_ffn2_w_in),)),
        ("mix", (("w_in", w_in, m_w_in, v_w_in), ("w_branch_a", w_branch_a, m_w_branch_a, v_w_branch_a),
                 ("w_branch_b", w_branch_b, m_w_branch_b, v_w_branch_b), ("w_out", w_out, m_w_out, v_w_out))),
        ("ffn1_out", (("ffn1_w_out", ffn1_w_out, m_ffn1_w_out, v_ffn1_w_out),)),
        ("ffn1_in", (("ffn1_w_in", ffn1_w_in, m_ffn1_w_in, v_ffn1_w_in),)),
    )
    for group, members in updates:
        parts_list = _xchg_wait("scatter_" + group + "_wait", scatter_state[group], after)
        for (nm, w, m, v), parts in zip(members, parts_list):
            res4 = _adam("adam_" + nm, w[0], m[0], v[0], parts)
            out[nm] = tuple(r[None] for r in res4)
            after = res4[0]
    (packs,) = _xchg_wait("gather_small_wait", small_state, after)

    tot = _small_sum("small_sum", packs)
    loss = tot[4, 2 * B_HEADS]
    g_meta = lax.dynamic_slice(tot[8:24, :], (0, me * 128), (N_META, 128))
    out["meta_tokens"] = tuple(_adam("adam_meta_tokens", meta_tokens, m_meta_tokens, v_meta_tokens, g_meta[None]))

    def pack_small(n1, nm, n2, nf, bfv, skv):
        misc = jnp.pad(jnp.concatenate([bfv, skv], axis=1), ((0, 0), (0, D_MODEL - 2 * B_HEADS)))
        row = lax.broadcasted_iota(jnp.int32, (8, D_MODEL), 0)
        vec = jnp.zeros((8, D_MODEL), F32)
        for i, piece in enumerate((n1, nm, n2, nf.reshape(1, D_MODEL), misc)):
            vec = jnp.where(row == i, piece, vec)
        return vec

    w_pack = pack_small(ffn1_norm, mix_norm, ffn2_norm, final_norm, b_forget, attn_sinks)
    m_pack = pack_small(m_ffn1_norm, m_mix_norm, m_ffn2_norm, m_final_norm, m_b_forget, m_attn_sinks)
    v_pack = pack_small(v_ffn1_norm, v_mix_norm, v_ffn2_norm, v_final_norm, v_b_forget, v_attn_sinks)
    small4 = _adam("adam_small", w_pack, m_pack, v_pack, tot[0:8][None])
    for i, nm in enumerate(("ffn1_norm", "mix_norm", "ffn2_norm")):
        out[nm] = tuple(r[i:i + 1] for r in small4)
    out["final_norm"] = tuple(r[3] for r in small4)
    out["b_forget"] = tuple(r[4:5, 0:B_HEADS] for r in small4)
    out["attn_sinks"] = tuple(r[4:5, B_HEADS:2 * B_HEADS] for r in small4)

    names = ("meta_tokens", "ffn1_norm", "ffn1_w_in", "ffn1_w_out", "mix_norm", "w_in", "b_forget", "attn_sinks",
             "w_branch_a", "w_branch_b", "w_out", "ffn2_norm", "ffn2_w_in", "ffn2_w_out", "final_norm")
    return (loss, grad_x) + tuple(out[nm][kind] for kind in range(4) for nm in names)
```

```python
import jax
import jax.numpy as jnp
from jax import lax
from jax.experimental import pallas as pl
from jax.experimental.pallas import tpu as pltpu

F32 = jnp.float32
BF16 = jnp.bfloat16

D_MODEL = 1024
N_META = 16
BLOCK = 128
PREFIX = 128
N_PAD = PREFIX - N_META
HEAD_DIM = 64
A_HEADS = 8
A_KV_HEADS = 2
A_GROUP = 4
B_HEADS = 8
A_WIDTH = 512
A_KV_WIDTH = 128
B_WIDTH = 512
D_FF = 2816
N_DEV = 8
FF_SHARD = 2 * D_FF // N_DEV
FF_SHARD_P = 768
FFO_SHARD = D_FF // N_DEV
FFO_SHARD_P = FF_SHARD_P // 2
D_FF_P = 4 * FF_SHARD_P
W_IN_COLS = 4360
WIN_SHARD = W_IN_COLS // N_DEV
WIN_SHARD_P = 640
PAIR_W = 3 * 128
B_SEG = 4 * PAIR_W
A_SEG = A_WIDTH + 2 * A_KV_WIDTH
QKV_W = B_SEG + A_SEG
P_GATES = 2 * (2 * D_MODEL)
P_F = P_GATES + 2 * D_MODEL
PROJ_P = P_F + 128
A_HEAD_ORDER = (0, 4, 1, 5, 2, 6, 3, 7)
EPS = 1e-6
NEG = -1e30
SCALE = HEAD_DIM ** -0.5
ADAM_LR = 0.001
ADAM_B1 = 0.9
ADAM_B2 = 0.999
ADAM_EPS = 1e-08
ADAM_WD = 0.01
ADAM_STEP = 10
VMEM_LIMIT = 56 * 1024 * 1024
MESH_ID = pl.DeviceIdType.MESH
SMALL_ROWS = 40

_NN = (((1,), (0,)), ((), ()))
_NT = (((1,), (1,)), ((), ()))
_TN = (((0,), (0,)), ((), ()))


def _params(sem=None):
    return pltpu.CompilerParams(dimension_semantics=sem, vmem_limit_bytes=VMEM_LIMIT)


def _pick(n, cands):
    for c in cands:
        if n % c == 0:
            return c
    raise ValueError(f"no tile for {n}")


def _bf(v):
    return v if v.dtype == BF16 else v.astype(BF16)


def _mm(name, a, b, dims, grid, a_spec, b_spec, o_spec, out_shape, out_dtype, acc_shape, k_axis=None, nk=1,
        alpha=1.0, res=None, res_spec=None, dep=None):
    has_res = res is not None
    has_dep = dep is not None

    def body(*refs):
        a_ref, b_ref = refs[0], refs[1]
        r_ref = refs[2] if has_res else None
        o_ref = refs[2 + has_res + has_dep]

        def finish(acc):
            if alpha != 1.0:
                acc = acc * alpha
            if has_res:
                acc = acc + r_ref[...]
            o_ref[...] = acc.astype(o_ref.dtype)

        part = lax.dot_general(_bf(a_ref[...]), _bf(b_ref[...]), dims, preferred_element_type=F32)
        if nk == 1:
            finish(part)
        else:
            acc_ref = refs[-1]
            k = pl.program_id(k_axis)

            @pl.when(k == 0)
            def _():
                acc_ref[...] = part

            @pl.when(k > 0)
            def _():
                acc_ref[...] += part

            @pl.when(k == nk - 1)
            def _():
                finish(acc_ref[...])

    in_specs = [a_spec, b_spec] + ([res_spec] if has_res else []) + ([pl.BlockSpec(memory_space=pl.ANY)] if has_dep else [])
    args = (a, b) + ((res,) if has_res else ()) + ((dep,) if has_dep else ())
    sem = tuple("arbitrary" if (nk > 1 and i == k_axis) else "parallel" for i in range(len(grid)))
    return pl.pallas_call(
        body, name=name, grid=grid, in_specs=in_specs, out_specs=o_spec,
        out_shape=jax.ShapeDtypeStruct(out_shape, out_dtype),
        scratch_shapes=[pltpu.VMEM(acc_shape, F32)] if nk > 1 else [],
        compiler_params=_params(sem),
    )(*args)


def _mm_nn(name, a, b, out_dtype=F32, alpha=1.0, res=None, tn_c=(512, 640, 256, 128), cols=None,
           tm_c=(1088, 768, 512, 256, 128), dep=None):
    t, k = a.shape
    c0, n = (0, b.shape[1]) if cols is None else cols
    tm = _pick(t, tm_c)
    tn = _pick(n, tn_c)
    assert c0 % tn == 0
    jb = c0 // tn
    return _mm(name, a, b, _NN, (t // tm, n // tn),
               pl.BlockSpec((tm, k), lambda i, j: (i, 0)), pl.BlockSpec((k, tn), lambda i, j: (0, jb + j)),
               pl.BlockSpec((tm, tn), lambda i, j: (i, j)), (t, n), out_dtype, None,
               alpha=alpha, res=res, res_spec=pl.BlockSpec((tm, tn), lambda i, j: (i, j)), dep=dep)


def _mm_nt(name, a, b, out_dtype=F32, alpha=1.0, tn_c=(768, 512, 256, 128), tk_c=None, dep=None, res=None, kcols=None):
    t, k = a.shape
    n = b.shape[0]
    c0 = 0 if kcols is None else kcols[0]
    tm = _pick(t, (1088, 768, 512, 256, 128))
    tn = _pick(n, tn_c)
    tk = k if tk_c is None else _pick(k, tk_c)
    nk = k // tk
    assert c0 % tk == 0
    kb = c0 // tk
    return _mm(name, a, b, _NT, (t // tm, n // tn, nk),
               pl.BlockSpec((tm, tk), lambda i, j, kk: (i, kk)), pl.BlockSpec((tn, tk), lambda i, j, kk: (j, kb + kk)),
               pl.BlockSpec((tm, tn), lambda i, j, kk: (i, j)), (t, n), out_dtype, (tm, tn), k_axis=2, nk=nk, alpha=alpha,
               dep=dep, res=res, res_spec=pl.BlockSpec((tm, tn), lambda i, j, kk: (i, j)))


def _mm_tn(name, a, b, out_dtype=BF16, alpha=1.0, tm_c=(768, 512, 256, 128), tn_c=(512, 640, 256, 128)):
    t, m = a.shape
    n = b.shape[1]
    tm = _pick(m, tm_c)
    tn = _pick(n, tn_c)
    tk = t
    nk = 1
    return _mm(name, a, b, _TN, (m // tm, n // tn, nk),
               pl.BlockSpec((tk, tm), lambda i, j, kk: (kk, i)), pl.BlockSpec((tk, tn), lambda i, j, kk: (kk, j)),
               pl.BlockSpec((tm, tn), lambda i, j, kk: (i, j)), (m, n), out_dtype, (tm, tn), k_axis=2, nk=nk, alpha=alpha)


def _silu_grads(g, u, d):
    sg = jax.nn.sigmoid(g)
    return d * u * (sg * (1.0 + g * (1.0 - sg))), d * (g * sg)


def _ffn_in_fwd(name, n, wblk, dep=None):
    t = n.shape[0]
    tm = _pick(t, (1088, 768, 512, 256, 128))
    has_dep = dep is not None

    def body(n_ref, w_ref, *rest):
        gu_ref, a_ref = rest[-2], rest[-1]
        nv = n_ref[...]
        g = lax.dot_general(nv, w_ref[0], _NN, preferred_element_type=F32)
        u = lax.dot_general(nv, w_ref[1], _NN, preferred_element_type=F32)
        gu_ref[0] = g.astype(BF16)
        gu_ref[1] = u.astype(BF16)
        a_ref[...] = ((g * jax.nn.sigmoid(g)) * u).astype(BF16)

    return pl.pallas_call(
        body, name=name, grid=(t // tm, 4),
        in_specs=[pl.BlockSpec((tm, D_MODEL), lambda i, j: (i, 0)),
                  pl.BlockSpec((2, None, D_MODEL, FF_SHARD_P), lambda i, j: (0, j, 0, 0))]
        + ([pl.BlockSpec(memory_space=pl.ANY)] if has_dep else []),
        out_specs=[pl.BlockSpec((2, tm, FF_SHARD_P), lambda i, j: (0, i, j)),
                   pl.BlockSpec((tm, FF_SHARD_P), lambda i, j: (i, j))],
        out_shape=[jax.ShapeDtypeStruct((2, t, D_FF_P), BF16), jax.ShapeDtypeStruct((t, D_FF_P), BF16)],
        compiler_params=_params(("parallel", "parallel")),
    )(*((n, wblk) + ((dep,) if has_dep else ())))


def _ffn_out_bwd_x(name, dh, w_out, gu, dep=None):
    t = dh.shape[0]
    tm = _pick(t, (1088, 768, 512, 256, 128))
    has_dep = dep is not None

    def body(dh_ref, w_ref, gu_ref, *rest):
        o_ref = rest[-1]
        da = lax.dot_general(_bf(dh_ref[...]), w_ref[...], _NT, preferred_element_type=F32) * 0.5
        dg, du = _silu_grads(gu_ref[0].astype(F32), gu_ref[1].astype(F32), da)
        o_ref[0] = dg.astype(BF16)
        o_ref[1] = du.astype(BF16)

    gu_spec = pl.BlockSpec((2, tm, FF_SHARD_P), lambda i, j: (0, i, j))
    return pl.pallas_call(
        body, name=name, grid=(t // tm, 4),
        in_specs=[pl.BlockSpec((tm, D_MODEL), lambda i, j: (i, 0)), pl.BlockSpec((FF_SHARD_P, D_MODEL), lambda i, j: (j, 0)),
                  gu_spec] + ([pl.BlockSpec(memory_space=pl.ANY)] if has_dep else []),
        out_specs=gu_spec, out_shape=jax.ShapeDtypeStruct((2, t, D_FF_P), BF16),
        compiler_params=_params(("parallel", "parallel")),
    )(*((dh, w_out, gu) + ((dep,) if has_dep else ())))


def _rms_bwd_rows(dn, h, g, dres):
    r = lax.rsqrt(jnp.mean(h * h, axis=-1, keepdims=True) + EPS)
    tv = dn * g
    dot = jnp.mean(tv * h, axis=-1, keepdims=True)
    return dres + (r * tv - h * (r * r * r * dot)), jnp.sum(dn * (h * r), axis=0, keepdims=True)


def _accumulate_rows(ref, part, first):
    @pl.when(first)
    def _():
        ref[...] = part

    @pl.when(jnp.logical_not(first))
    def _():
        ref[...] += part


def _ffn_in_bwd_x(name, dgu, wblk, h_in, g_norm, dres, dep=None):
    t = dgu.shape[1]
    tm = _pick(t, (272, 256, 128))
    has_dep = dep is not None

    def body(d_ref, w_ref, h_ref, g_ref, r_ref, *rest):
        dh_ref, dg_ref = rest[-2], rest[-1]
        acc = None
        for s in range(2):
            for j in range(4):
                part = lax.dot_general(d_ref[s, :, FF_SHARD_P * j:FF_SHARD_P * (j + 1)], w_ref[s, j], _NT,
                                       preferred_element_type=F32)
                acc = part if acc is None else acc + part
        dh, dg = _rms_bwd_rows(acc, h_ref[...], g_ref[...], r_ref[...])
        dh_ref[...] = dh
        _accumulate_rows(dg_ref, dg, pl.program_id(0) == 0)

    row = pl.BlockSpec((tm, D_MODEL), lambda i: (i, 0))
    vec = pl.BlockSpec((1, D_MODEL), lambda i: (0, 0))
    return pl.pallas_call(
        body, name=name, grid=(t // tm,),
        in_specs=[pl.BlockSpec((2, tm, D_FF_P), lambda i: (0, i, 0)),
                  pl.BlockSpec((2, 4, D_MODEL, FF_SHARD_P), lambda i: (0, 0, 0, 0)), row, vec, row]
        + ([pl.BlockSpec(memory_space=pl.ANY)] if has_dep else []),
        out_specs=[row, vec],
        out_shape=[jax.ShapeDtypeStruct((t, D_MODEL), F32), jax.ShapeDtypeStruct((1, D_MODEL), F32)],
        compiler_params=_params(("arbitrary",)),
    )(*((dgu, wblk, h_in, g_norm, dres) + ((dep,) if has_dep else ())))


def _proj_bwd_x(name, dqkv, dgates, df, wi, h_in, g_norm, dres, dep=None):
    t = dqkv.shape[0]
    tm = _pick(t, (272, 256, 128))
    has_dep = dep is not None

    def body(q_ref, gt_ref, f_ref, w_ref, h_ref, g_ref, r_ref, *rest):
        dh_ref, dg_ref = rest[-2], rest[-1]
        acc = lax.dot_general(q_ref[...], w_ref[:, 0:QKV_W], _NT, preferred_element_type=F32)
        acc = acc + lax.dot_general(gt_ref[...], w_ref[:, P_GATES:P_F], _NT, preferred_element_type=F32)
        acc = acc + lax.dot_general(f_ref[...], w_ref[:, P_F:PROJ_P], _NT, preferred_element_type=F32)
        dh, dg = _rms_bwd_rows(acc, h_ref[...], g_ref[...], r_ref[...])
        dh_ref[...] = dh
        _accumulate_rows(dg_ref, dg, pl.program_id(0) == 0)

    row = pl.BlockSpec((tm, D_MODEL), lambda i: (i, 0))
    vec = pl.BlockSpec((1, D_MODEL), lambda i: (0, 0))
    return pl.pallas_call(
        body, name=name, grid=(t // tm,),
        in_specs=[pl.BlockSpec((tm, QKV_W), lambda i: (i, 0)), pl.BlockSpec((tm, 2 * D_MODEL), lambda i: (i, 0)),
                  pl.BlockSpec((tm, 128), lambda i: (i, 0)), pl.BlockSpec((D_MODEL, PROJ_P), lambda i: (0, 0)), row, vec, row]
        + ([pl.BlockSpec(memory_space=pl.ANY)] if has_dep else []),
        out_specs=[row, vec],
        out_shape=[jax.ShapeDtypeStruct((t, D_MODEL), F32), jax.ShapeDtypeStruct((1, D_MODEL), F32)],
        compiler_params=_params(("arbitrary",)),
    )(*((dqkv, dgates, df, wi, h_in, g_norm, dres) + ((dep,) if has_dep else ())))


def _ffn_in_bwd_w(name, n, dgu):
    t = n.shape[0]
    tk = t
    nk = 1
    return _mm(name, n, dgu, _TN, (2, 4, nk),
               pl.BlockSpec((tk, D_MODEL), lambda s, j, kk: (kk, 0)),
               pl.BlockSpec((None, tk, FF_SHARD_P), lambda s, j, kk: (s, kk, j)),
               pl.BlockSpec((None, None, D_MODEL, FF_SHARD_P), lambda s, j, kk: (s, j, 0, 0)),
               (2, 4, D_MODEL, FF_SHARD_P), BF16, (D_MODEL, FF_SHARD_P), k_axis=2, nk=nk)


def _rms_fwd(name, h, g):
    t = h.shape[0]
    tm = _pick(t, (544, 384, 256, 128))

    def body(h_ref, g_ref, o_ref):
        hv = h_ref[...]
        r = lax.rsqrt(jnp.mean(hv * hv, axis=-1, keepdims=True) + EPS)
        o_ref[...] = ((hv * r) * g_ref[...]).astype(BF16)

    return pl.pallas_call(
        body, name=name, grid=(t // tm,),
        in_specs=[pl.BlockSpec((tm, D_MODEL), lambda i: (i, 0)), pl.BlockSpec((1, D_MODEL), lambda i: (0, 0))],
        out_specs=pl.BlockSpec((tm, D_MODEL), lambda i: (i, 0)),
        out_shape=jax.ShapeDtypeStruct((t, D_MODEL), BF16), compiler_params=_params(("parallel",)),
    )(h, g)


def _rms_bwd(name, h, g, dn, dres):
    t = h.shape[0]
    tm = _pick(t, (544, 384, 256, 128))

    def body(h_ref, g_ref, dn_ref, dres_ref, dh_ref, dg_ref):
        i = pl.program_id(0)
        hv = h_ref[...]
        dnv = dn_ref[...]
        r = lax.rsqrt(jnp.mean(hv * hv, axis=-1, keepdims=True) + EPS)
        tv = dnv * g_ref[...]
        dot = jnp.mean(tv * hv, axis=-1, keepdims=True)
        dh_ref[...] = dres_ref[...] + (r * tv - hv * (r * r * r * dot))
        part = jnp.sum(dnv * (hv * r), axis=0, keepdims=True)

        @pl.when(i == 0)
        def _():
            dg_ref[...] = part

        @pl.when(i > 0)
        def _():
            dg_ref[...] += part

    row = pl.BlockSpec((tm, D_MODEL), lambda i: (i, 0))
    vec = pl.BlockSpec((1, D_MODEL), lambda i: (0, 0))
    return pl.pallas_call(
        body, name=name, grid=(t // tm,), in_specs=[row, vec, row, row], out_specs=[row, vec],
        out_shape=[jax.ShapeDtypeStruct((t, D_MODEL), F32), jax.ShapeDtypeStruct((1, D_MODEL), F32)],
        compiler_params=_params(("arbitrary",)),
    )(h, g, dn, dres)


def _branch_gate_fwd(name, oa, ob, wa, wb, gates):
    t = gates.shape[0]
    tm = _pick(t, (544, 384, 256, 128))

    def body(oa_ref, ob_ref, wa_ref, wb_ref, g_ref, o_ref, ya_ref, yb_ref):
        ya = lax.dot_general(_bf(oa_ref[...]), wa_ref[...], _NN, preferred_element_type=F32)
        yb = lax.dot_general(_bf(ob_ref[...]), wb_ref[...], _NN, preferred_element_type=F32)
        sa = jax.nn.sigmoid(g_ref[:, 0:D_MODEL])
        sb = jax.nn.sigmoid(g_ref[:, D_MODEL:2 * D_MODEL])
        o_ref[...] = (sa * ya + sb * yb).astype(BF16)
        ya_ref[...] = ya.astype(BF16)
        yb_ref[...] = yb.astype(BF16)

    blk = pl.BlockSpec((tm, D_MODEL), lambda i: (i, 0))
    narrow = pl.BlockSpec((tm, A_WIDTH), lambda i: (i, 0))
    wide = pl.BlockSpec((tm, 2 * D_MODEL), lambda i: (i, 0))
    wspec = pl.BlockSpec((A_WIDTH, D_MODEL), lambda i: (0, 0))
    out = jax.ShapeDtypeStruct((t, D_MODEL), BF16)
    return pl.pallas_call(
        body, name=name, grid=(t // tm,), in_specs=[narrow, narrow, wspec, wspec, wide], out_specs=[blk, blk, blk],
        out_shape=[out, out, out], compiler_params=_params(("parallel",)),
    )(oa, ob, wa, wb, gates)


def _mix_out_gate_bwd(name, dh, wo, gates, ya, yb):
    t = gates.shape[0]
    tm = _pick(t, (544, 384, 256, 128))

    def body(dh_ref, w_ref, g_ref, ya_ref, yb_ref, dya_ref, dyb_ref, dg_ref):
        dm = lax.dot_general(_bf(dh_ref[...]), w_ref[...], _NT, preferred_element_type=F32)
        sa = jax.nn.sigmoid(g_ref[:, 0:D_MODEL])
        sb = jax.nn.sigmoid(g_ref[:, D_MODEL:2 * D_MODEL])
        dya_ref[...] = (dm * sa).astype(BF16)
        dyb_ref[...] = (dm * sb).astype(BF16)
        dg_ref[:, 0:D_MODEL] = (dm * ya_ref[...].astype(F32) * (sa * (1.0 - sa))).astype(BF16)
        dg_ref[:, D_MODEL:2 * D_MODEL] = (dm * yb_ref[...].astype(F32) * (sb * (1.0 - sb))).astype(BF16)

    blk = pl.BlockSpec((tm, D_MODEL), lambda i: (i, 0))
    wide = pl.BlockSpec((tm, 2 * D_MODEL), lambda i: (i, 0))
    out = jax.ShapeDtypeStruct((t, D_MODEL), BF16)
    return pl.pallas_call(
        body, name=name, grid=(t // tm,),
        in_specs=[blk, pl.BlockSpec((D_MODEL, D_MODEL), lambda i: (0, 0)), wide, blk, blk], out_specs=[blk, blk, wide],
        out_shape=[out, out, jax.ShapeDtypeStruct((t, 2 * D_MODEL), BF16)], compiler_params=_params(("parallel",)),
    )(dh, wo, gates, ya, yb)


def _loss_head(name, h3, gf, tgt):
    b, l, _ = h3.shape
    nb = l // BLOCK

    def body(h_ref, g_ref, t_ref, dh_ref, loss_ref, dg_ref):
        first = (pl.program_id(0) == 0) & (pl.program_id(1) == 0)
        real = (pl.program_id(1) > 0).astype(F32)
        hv = h_ref[...]
        g = g_ref[...]
        r = lax.rsqrt(jnp.mean(hv * hv, axis=-1, keepdims=True) + EPS)
        xn = hv * r
        err = (xn * g - t_ref[...]) * real
        lpart = 0.5 * jnp.sum(jnp.mean(err * err, axis=-1, keepdims=True), axis=0, keepdims=True)
        dy = err * (1.0 / D_MODEL)
        tv = dy * g
        dot = jnp.mean(tv * hv, axis=-1, keepdims=True)
        dh_ref[...] = r * tv - hv * (r * r * r * dot)
        gpart = jnp.sum(dy * xn, axis=0, keepdims=True)

        @pl.when(first)
        def _():
            loss_ref[...] = jnp.zeros_like(loss_ref)
            dg_ref[...] = jnp.zeros_like(dg_ref)

        loss_ref[...] += jnp.broadcast_to(lpart, loss_ref.shape)
        dg_ref[...] += gpart

    return pl.pallas_call(
        body, name=name, grid=(b, nb),
        in_specs=[pl.BlockSpec((None, BLOCK, D_MODEL), lambda bi, n: (bi, n, 0)),
                  pl.BlockSpec((1, D_MODEL), lambda bi, n: (0, 0)),
                  pl.BlockSpec((None, BLOCK, D_MODEL), lambda bi, n: (bi, jnp.maximum(n - 1, 0), 0))],
        out_specs=[pl.BlockSpec((None, BLOCK, D_MODEL), lambda bi, n: (bi, n, 0)),
                   pl.BlockSpec((8, 128), lambda bi, n: (0, 0)),
                   pl.BlockSpec((1, D_MODEL), lambda bi, n: (0, 0))],
        out_shape=[jax.ShapeDtypeStruct(h3.shape, F32), jax.ShapeDtypeStruct((8, 128), F32),
                   jax.ShapeDtypeStruct((1, D_MODEL), F32)],
        compiler_params=_params(("arbitrary", "arbitrary")),
    )(h3, gf, tgt)


def _fgate_fwd(name, f3, bf_row):
    b, l, _ = f3.shape
    nb = l // BLOCK

    def body(f_ref, b_ref, cc_ref, cr_ref):
        r_i = lax.broadcasted_iota(jnp.int32, (BLOCK, BLOCK), 0)
        c_i = lax.broadcasted_iota(jnp.int32, (BLOCK, BLOCK), 1)
        tri = (r_i >= c_i).astype(F32)
        carry = jnp.zeros((1, 128), F32)
        for blk in range(nb):
            rows = slice(blk * BLOCK, (blk + 1) * BLOCK)
            z = f_ref[rows, :] + b_ref[...]
            lf = jnp.minimum(z, 0.0) - jnp.log(1.0 + jnp.exp(-jnp.abs(z)))
            cb = jnp.dot(tri, lf, preferred_element_type=F32, precision=lax.Precision.HIGHEST) + carry
            carry = cb[BLOCK - 1:BLOCK, :]
            cbt = cb.T
            for hh in range(B_HEADS):
                cc_ref[hh, rows, :] = jnp.sum(jnp.where(c_i == hh, cb, 0.0), axis=1, keepdims=True)
                cr_ref[hh, :, rows] = cbt[hh:hh + 1, :]

    return pl.pallas_call(
        body, name=name, grid=(b,),
        in_specs=[pl.BlockSpec((None, l, 128), lambda bi: (bi, 0, 0)),
                  pl.BlockSpec((1, 128), lambda bi: (0, 0))],
        out_specs=[pl.BlockSpec((None, B_HEADS, l, 1), lambda bi: (bi, 0, 0, 0)),
                   pl.BlockSpec((None, B_HEADS, 1, l), lambda bi: (bi, 0, 0, 0))],
        out_shape=[jax.ShapeDtypeStruct((b, B_HEADS, l, 1), F32), jax.ShapeDtypeStruct((b, B_HEADS, 1, l), F32)],
        compiler_params=_params(("parallel",)),
    )(f3, bf_row)


def _fgate_bwd(name, f3, bf_row, dcq, dck):
    b, l, _ = f3.shape
    nb = l // BLOCK

    def body(f_ref, b_ref, dcq_ref, dck_ref, df_ref, db_ref):
        r_i = lax.broadcasted_iota(jnp.int32, (BLOCK, BLOCK), 0)
        c_i = lax.broadcasted_iota(jnp.int32, (BLOCK, BLOCK), 1)
        tri = (r_i <= c_i).astype(F32)
        carry = jnp.zeros((1, 128), F32)
        total = jnp.zeros((1, 128), F32)
        for blk in range(nb - 1, -1, -1):
            rows = slice(blk * BLOCK, (blk + 1) * BLOCK)
            krows = jnp.concatenate([dck_ref[hh, :, rows] for hh in range(B_HEADS)]
                                    + [jnp.zeros((BLOCK - B_HEADS, BLOCK), F32)], axis=0)
            dcb = krows.T
            for hh in range(B_HEADS):
                dcb = dcb + jnp.where(c_i == hh, dcq_ref[hh, rows, :], 0.0)
            rc = jnp.dot(tri, dcb, preferred_element_type=F32, precision=lax.Precision.HIGHEST) + carry
            carry = rc[0:1, :]
            z = f_ref[rows, :] + b_ref[...]
            df = rc * (1.0 / (1.0 + jnp.exp(z)))
            df_ref[rows, :] = df.astype(BF16)
            total = total + jnp.sum(df, axis=0, keepdims=True)

        @pl.when(pl.program_id(0) == 0)
        def _():
            db_ref[...] = total

        @pl.when(pl.program_id(0) > 0)
        def _():
            db_ref[...] += total

    return pl.pallas_call(
        body, name=name, grid=(b,),
        in_specs=[pl.BlockSpec((None, l, 128), lambda bi: (bi, 0, 0)),
                  pl.BlockSpec((1, 128), lambda bi: (0, 0)),
                  pl.BlockSpec((None, B_HEADS, l, 1), lambda bi: (bi, 0, 0, 0)),
                  pl.BlockSpec((None, B_HEADS, 1, l), lambda bi: (bi, 0, 0, 0))],
        out_specs=[pl.BlockSpec((None, l, 128), lambda bi: (bi, 0, 0)), pl.BlockSpec((1, 128), lambda bi: (0, 0))],
        out_shape=[jax.ShapeDtypeStruct((b, l, 128), BF16), jax.ShapeDtypeStruct((1, 128), F32)],
        compiler_params=_params(("arbitrary",)),
    )(f3, bf_row, dcq, dck)


A_Q_BLK = B_SEG // A_WIDTH
A_K_BLK = (B_SEG + A_WIDTH) // 128
A_V_BLK = A_K_BLK + 1
A_SEG_BLK = B_SEG // A_SEG
STACK = A_HEADS * BLOCK


def _lane_lo():
    return lax.broadcasted_iota(jnp.int32, (1, 128), 1) < HEAD_DIM


def _stack_heads(x, masked):
    lo = _lane_lo()
    blks = [x[:, 128 * j:128 * (j + 1)] for j in range(4)]
    if not masked:
        return jnp.concatenate(blks + blks, axis=0)
    zero = jnp.zeros_like(blks[0])
    return jnp.concatenate([jnp.where(lo, bk, zero) for bk in blks] + [jnp.where(lo, zero, bk) for bk in blks], axis=0)


def _unstack_heads(y):
    lo = _lane_lo()
    return jnp.concatenate([jnp.where(lo, y[128 * j:128 * (j + 1)], y[128 * (4 + j):128 * (5 + j)]) for j in range(4)], axis=1)


def _swa_bias(slopes):
    r_i = jnp.arange(STACK)[:, None]
    c_i = jnp.arange(3 * BLOCK)[None, :]
    seg = c_i >> 7
    out = []
    for n in range(3):
        qpos = n * BLOCK + (r_i & (BLOCK - 1))
        kpos = jnp.where(seg == 0, c_i, (n - 2) * BLOCK + c_i)
        dist = qpos - kpos
        band = (seg != 0) & (dist < BLOCK) & (kpos >= PREFIX)
        meta = (seg == 0) & (c_i >= N_PAD)
        out.append(jnp.where((dist >= 0) & (band | meta), -slopes * dist.astype(F32), NEG))
    return jnp.stack(out, axis=0)


def _swa_scores(q, kcat, n, slope, bias):
    s = lax.dot_general(q, kcat, _NT, preferred_element_type=F32) + bias
    further = slope * (-BLOCK * jnp.maximum(n - 2, 0)).astype(F32)
    return jnp.concatenate([s[:, 0:BLOCK] + further, s[:, BLOCK:]], axis=1)


def _swa_specs():
    def kv(col_blk):
        return [pl.BlockSpec((None, BLOCK, 128), lambda b, n: (b, 0, col_blk)),
                pl.BlockSpec((None, BLOCK, 128), lambda b, n: (b, jnp.maximum(n - 1, 0), col_blk)),
                pl.BlockSpec((None, BLOCK, 128), lambda b, n: (b, n, col_blk))]

    q_spec = pl.BlockSpec((None, BLOCK, A_WIDTH), lambda b, n: (b, n, A_Q_BLK))
    o_spec = pl.BlockSpec((None, BLOCK, A_WIDTH), lambda b, n: (b, n, 0))
    col = pl.BlockSpec((STACK, 1), lambda b, n: (0, 0))
    bias = pl.BlockSpec((None, STACK, 3 * BLOCK), lambda b, n: (jnp.minimum(n, 2), 0, 0))
    lse_spec = pl.BlockSpec((None, A_HEADS, BLOCK, 1), lambda b, n: (b, 0, n, 0))
    return q_spec, kv(A_K_BLK), kv(A_V_BLK), o_spec, [col, col, bias], lse_spec


def _swa_fwd(name, qkv, slopes, sinks, bias):
    b, l, _ = qkv.shape
    nb = l // BLOCK

    def body(q_ref, k0_ref, kp_ref, kc_ref, v0_ref, vp_ref, vc_ref, sl_ref, sk_ref, bias_ref, o_ref, lse_ref):
        n = pl.program_id(1)
        qs = _stack_heads(q_ref[...], True) * SCALE
        kcat = jnp.concatenate([k0_ref[...], kp_ref[...], kc_ref[...]], axis=0)
        vcat = jnp.concatenate([v0_ref[...], vp_ref[...], vc_ref[...]], axis=0)
        s = _swa_scores(qs, kcat, n, sl_ref[...], bias_ref[...])
        sink = sk_ref[...]
        m = jnp.maximum(jnp.max(s, axis=-1, keepdims=True), sink)
        p = jnp.exp(s - m)
        den = jnp.sum(p, axis=-1, keepdims=True) + jnp.exp(sink - m)
        o = lax.dot_general(p.astype(BF16), vcat, _NN, preferred_element_type=F32) / den
        o_ref[...] = _unstack_heads(o)
        lse_ref[...] = (m + jnp.log(den)).reshape(A_HEADS, BLOCK, 1)

    q_spec, k_specs, v_specs, o_spec, consts, lse_spec = _swa_specs()
    return pl.pallas_call(
        body, name=name, grid=(b, nb),
        in_specs=[q_spec] + k_specs + v_specs + consts, out_specs=[o_spec, lse_spec],
        out_shape=[jax.ShapeDtypeStruct((b, l, A_WIDTH), F32), jax.ShapeDtypeStruct((b, A_HEADS, l, 1), F32)],
        compiler_params=_params(("parallel", "parallel")),
    )(qkv, qkv, qkv, qkv, qkv, qkv, qkv, slopes, sinks, bias)


def _swa_bwd(name, qkv, o, lse, do, slopes, sinks, bias, dqkv):
    b, l, _ = qkv.shape
    nb = l // BLOCK

    def body(q_ref, k0_ref, kp_ref, kc_ref, v0_ref, vp_ref, vc_ref, o_ref, lse_ref, do_ref, sl_ref, sk_ref, bias_ref, _,
             dx_ref, ds_ref, dk_acc, dv_acc):
        bi = pl.program_id(0)
        n = pl.program_id(1)
        qs = _stack_heads(q_ref[...], True) * SCALE
        dos32 = _stack_heads(do_ref[...], True)
        dos = dos32.astype(BF16)
        os_ = _stack_heads(o_ref[...], False)
        lsev = lse_ref[...].reshape(STACK, 1)
        kcat = jnp.concatenate([k0_ref[...], kp_ref[...], kc_ref[...]], axis=0)
        vcat = jnp.concatenate([v0_ref[...], vp_ref[...], vc_ref[...]], axis=0)
        s = _swa_scores(qs, kcat, n, sl_ref[...], bias_ref[...])
        p = jnp.exp(s - lsev)
        dsum = jnp.sum(dos32 * os_, axis=-1, keepdims=True)
        dp = lax.dot_general(dos, vcat, _NT, preferred_element_type=F32)
        dsc = (p * (dp - dsum)).astype(BF16)
        dq = lax.dot_general(dsc, kcat, _NN, preferred_element_type=F32) * SCALE
        row0 = pl.multiple_of(n * BLOCK, BLOCK)
        dx_ref[pl.ds(row0, BLOCK), 0:A_WIDTH] = _unstack_heads(dq).astype(BF16)
        dkc = lax.dot_general(dsc, qs, _TN, preferred_element_type=F32)
        dvc = lax.dot_general(p.astype(BF16), dos, _TN, preferred_element_type=F32)

        @pl.when(n == 0)
        def _():
            dk_acc[...] = jnp.zeros_like(dk_acc)
            dv_acc[...] = jnp.zeros_like(dv_acc)

        starts = (0, pl.multiple_of(jnp.maximum(n - 1, 0) * BLOCK, BLOCK), row0)
        for t, st in enumerate(starts):
            dk_acc[pl.ds(st, BLOCK), :] += dkc[t * BLOCK:(t + 1) * BLOCK, :]
            dv_acc[pl.ds(st, BLOCK), :] += dvc[t * BLOCK:(t + 1) * BLOCK, :]

        @pl.when(n == nb - 1)
        def _():
            dx_ref[:, A_WIDTH:A_WIDTH + 128] = dk_acc[...].astype(BF16)
            dx_ref[:, A_WIDTH + 128:A_SEG] = dv_acc[...].astype(BF16)

        dsink = -(jnp.exp(sk_ref[...] - lsev) * dsum)
        r8 = lax.broadcasted_iota(jnp.int32, (8, 128), 0)
        acc = jnp.zeros((8, 128), F32)
        for hh in range(A_HEADS):
            acc = acc + jnp.where(r8 == hh, jnp.sum(dsink[hh * BLOCK:(hh + 1) * BLOCK, :]), 0.0)

        @pl.when((bi == 0) & (n == 0))
        def _():
            ds_ref[...] = jnp.zeros_like(ds_ref)

        ds_ref[...] += acc

    q_spec, k_specs, v_specs, o_spec, consts, lse_spec = _swa_specs()
    return pl.pallas_call(
        body, name=name, grid=(b, nb),
        in_specs=[q_spec] + k_specs + v_specs + [o_spec, lse_spec, o_spec] + consts + [pl.BlockSpec(memory_space=pl.ANY)],
        out_specs=[pl.BlockSpec((None, l, A_SEG), lambda bb, n: (bb, 0, A_SEG_BLK)),
                   pl.BlockSpec((8, 128), lambda bb, n: (0, 0))],
        out_shape=[jax.ShapeDtypeStruct(dqkv.shape, BF16), jax.ShapeDtypeStruct((8, 128), F32)],
        scratch_shapes=[pltpu.VMEM((l, 128), F32), pltpu.VMEM((l, 128), F32)],
        input_output_aliases={13: 0},
        compiler_params=_params(("arbitrary", "arbitrary")),
    )(qkv, qkv, qkv, qkv, qkv, qkv, qkv, o, lse, do, slopes, sinks, bias, dqkv)


def _fox_mask(qk, ck, i):
    kh = qk.shape[1]
    qpos = i * BLOCK + lax.broadcasted_iota(jnp.int32, (BLOCK, kh), 0)
    kpos = lax.broadcasted_iota(jnp.int32, (BLOCK, kh), 1)
    return jnp.where((kpos <= qpos) & (kpos >= N_PAD), qk - ck, NEG)


def _pick_head(x, hh):
    lo = _lane_lo()
    return jnp.where(lo if hh == 0 else jnp.logical_not(lo), x, jnp.zeros_like(x))


def _both_heads(x):
    return jnp.concatenate([_pick_head(x, 0), _pick_head(x, 1)], axis=0)


def _fox_specs(l):
    pair = pl.BlockSpec((None, l, PAIR_W), lambda bi, hp: (bi, 0, hp))
    half = pl.BlockSpec((None, l, 128), lambda bi, hp: (bi, 0, hp))
    colv = pl.BlockSpec((None, 2, l, 1), lambda bi, hp: (bi, hp, 0, 0))
    rowv = pl.BlockSpec((None, 2, 1, l), lambda bi, hp: (bi, hp, 0, 0))
    return pair, half, colv, rowv


def _fox_fwd(name, qkv, c_col, c_row):
    b, l, _ = qkv.shape
    nb = l // BLOCK

    def body(x_ref, cc_ref, cr_ref, o_ref, lse_ref):
        for i in range(nb):
            rows = slice(i * BLOCK, (i + 1) * BLOCK)
            kh = (i + 1) * BLOCK
            qblk = x_ref[rows, 0:128]
            kv = x_ref[0:kh, 128:256]
            vv = x_ref[0:kh, 256:384]
            qk = lax.dot_general(_both_heads(qblk) * SCALE, kv, _NT, preferred_element_type=F32)
            ps, dens = [], []
            for hh in range(2):
                s = _fox_mask(qk[hh * BLOCK:(hh + 1) * BLOCK], cr_ref[hh, :, 0:kh], i)
                m = jnp.max(s, axis=-1, keepdims=True)
                p = jnp.exp(s - m)
                den = jnp.sum(p, axis=-1, keepdims=True)
                ps.append(p.astype(BF16))
                dens.append(den)
                lse_ref[hh, rows, :] = (m + jnp.log(den)) + cc_ref[hh, rows, :]
            pv = lax.dot_general(jnp.concatenate(ps, axis=0), vv, _NN, preferred_element_type=F32)
            o_ref[rows, :] = jnp.where(_lane_lo(), pv[0:BLOCK] / dens[0], pv[BLOCK:2 * BLOCK] / dens[1]).astype(BF16)

    pair, half, colv, rowv = _fox_specs(l)
    return pl.pallas_call(
        body, name=name, grid=(b, 4), in_specs=[pair, colv, rowv], out_specs=[half, colv],
        out_shape=[jax.ShapeDtypeStruct((b, l, B_WIDTH), BF16), jax.ShapeDtypeStruct((b, B_HEADS, l, 1), F32)],
        compiler_params=_params(("parallel", "parallel")),
    )(qkv, c_col, c_row)


def _fox_bwd(name, qkv, c_col, c_row, o, lse, do):
    b, l, _ = qkv.shape
    nb = l // BLOCK

    def body(x_ref, cc_ref, cr_ref, o_ref, lse_ref, do_ref, dx_ref, dcq_ref, dck_ref, dk_acc, dv_acc):
        dk_acc[...] = jnp.zeros_like(dk_acc)
        dv_acc[...] = jnp.zeros_like(dv_acc)
        dck_ref[...] = jnp.zeros_like(dck_ref)
        for i in range(nb):
            rows = slice(i * BLOCK, (i + 1) * BLOCK)
            kh = (i + 1) * BLOCK
            qblk = x_ref[rows, 0:128]
            kv = x_ref[0:kh, 128:256]
            vv = x_ref[0:kh, 256:384]
            doblk = do_ref[rows, :]
            ov = o_ref[rows, :].astype(F32)
            q2 = _both_heads(qblk) * SCALE
            do2 = _both_heads(doblk)
            qk = lax.dot_general(q2, kv, _NT, preferred_element_type=F32)
            dp = lax.dot_general(do2, vv, _NT, preferred_element_type=F32)
            ps, dss = [], []
            for hh in range(2):
                half = slice(hh * BLOCK, (hh + 1) * BLOCK)
                s = _fox_mask(qk[half], cr_ref[hh, :, 0:kh], i)
                p = jnp.exp(s - (lse_ref[hh, rows, :] - cc_ref[hh, rows, :]))
                dsum = jnp.sum(do2[half].astype(F32) * ov, axis=-1, keepdims=True)
                ds = p * (dp[half] - dsum)
                ps.append(p.astype(BF16))
                dss.append(ds.astype(BF16))
                dcq_ref[hh, rows, :] = jnp.sum(ds, axis=-1, keepdims=True)
                dck_ref[hh, :, 0:kh] -= jnp.sum(ds, axis=0, keepdims=True)
            p2 = jnp.concatenate(ps, axis=0)
            ds2 = jnp.concatenate(dss, axis=0)
            dq = lax.dot_general(ds2, kv, _NN, preferred_element_type=F32) * SCALE
            dk_acc[0:kh, :] += lax.dot_general(ds2, q2, _TN, preferred_element_type=F32)
            dv_acc[0:kh, :] += lax.dot_general(p2, do2, _TN, preferred_element_type=F32)
            dx_ref[rows, 0:128] = jnp.where(_lane_lo(), dq[0:BLOCK], dq[BLOCK:2 * BLOCK]).astype(BF16)
        dx_ref[:, 128:256] = dk_acc[...].astype(BF16)
        dx_ref[:, 256:384] = dv_acc[...].astype(BF16)

    pair, half, colv, rowv = _fox_specs(l)
    return pl.pallas_call(
        body, name=name, grid=(b, 4), in_specs=[pair, colv, rowv, half, colv, half],
        out_specs=[pair, colv, rowv],
        out_shape=[jax.ShapeDtypeStruct(qkv.shape, BF16), jax.ShapeDtypeStruct((b, B_HEADS, l, 1), F32),
                   jax.ShapeDtypeStruct((b, B_HEADS, 1, l), F32)],
        scratch_shapes=[pltpu.VMEM((l, 128), F32), pltpu.VMEM((l, 128), F32)],
        compiler_params=_params(("parallel", "parallel")),
    )(qkv, c_col, c_row, o, lse, do)


_FLIPS = ((0, 0, 1), (0, 1, 0), (0, 1, 1), (1, 0, 0), (1, 0, 1), (1, 1, 0), (1, 1, 1))


def _exchange(name, gather, scatter):
    ng, ns = len(gather), len(scatter)
    na = ng + ns
    npeer = len(_FLIPS)

    def body(*refs):
        ins = refs[:na]
        outs = refs[na:2 * na]
        send_sems, recv_sems, loc_sems = refs[2 * na:]
        x, y, c = lax.axis_index("x"), lax.axis_index("y"), lax.axis_index("c")
        me = 4 * x + 2 * y + c
        peers = []
        for fx, fy, fc in _FLIPS:
            px = 1 - x if fx else x
            py = 1 - y if fy else y
            pc = 1 - c if fc else c
            peers.append(((px, py, pc), 4 * px + 2 * py + pc))

        def remote(a, kk):
            dev, lin = peers[kk]
            src = ins[a] if a < ng else ins[a].at[lin]
            return pltpu.make_async_remote_copy(src_ref=src, dst_ref=outs[a].at[me], send_sem=send_sems.at[a * npeer + kk],
                                                recv_sem=recv_sems.at[a * npeer + kk], device_id=dev, device_id_type=MESH_ID)

        def arrival(a, kk):
            dev, lin = peers[kk]
            src = ins[a] if a < ng else ins[a].at[lin]
            return pltpu.make_async_remote_copy(src_ref=src, dst_ref=outs[a].at[lin], send_sem=send_sems.at[a * npeer + kk],
                                                recv_sem=recv_sems.at[a * npeer + kk], device_id=dev, device_id_type=MESH_ID)

        local = []
        for a in range(na):
            src = ins[a] if a < ng else ins[a].at[me]
            cp = pltpu.make_async_copy(src, outs[a].at[me], loc_sems.at[a])
            cp.start()
            local.append(cp)
        sent = [remote(a, kk) for kk in range(npeer) for a in range(na)]
        for cp in sent:
            cp.start()
        for kk in range(npeer):
            for a in range(na):
                arrival(a, kk).wait_recv()
        for cp in sent:
            cp.wait_send()
        for cp in local:
            cp.wait()

    arrs = list(gather) + list(scatter)
    out_shape = [jax.ShapeDtypeStruct((N_DEV,) + tuple(a.shape), a.dtype) for a in gather]
    out_shape += [jax.ShapeDtypeStruct(tuple(a.shape), a.dtype) for a in scatter]
    anyspec = pl.BlockSpec(memory_space=pl.ANY)
    return pl.pallas_call(
        body, name=name, in_specs=[anyspec] * na, out_specs=[anyspec] * na, out_shape=out_shape,
        scratch_shapes=[pltpu.SemaphoreType.DMA((na * npeer,)), pltpu.SemaphoreType.DMA((na * npeer,)),
                        pltpu.SemaphoreType.DMA((na,))],
        compiler_params=pltpu.CompilerParams(has_side_effects=True),
    )(*arrs)


def _peer_table():
    x, y, c = lax.axis_index("x"), lax.axis_index("y"), lax.axis_index("c")
    me = 4 * x + 2 * y + c
    peers = []
    for fx, fy, fc in _FLIPS:
        px = 1 - x if fx else x
        py = 1 - y if fy else y
        pc = 1 - c if fc else c
        peers.append(((px, py, pc), 4 * px + 2 * py + pc))
    return me, peers


_HBM = pl.BlockSpec(memory_space=pltpu.HBM)
_SEM = pl.BlockSpec(memory_space=pltpu.SEMAPHORE)
_ANY = pl.BlockSpec(memory_space=pl.ANY)
_EFFECT = pltpu.SideEffectType.DATAFLOW_SIDE_EFFECTING


def _split_copy(srcs_are_pieces, src_refs, land_refs, send_sem, recv_sem, a, kk, me, peers, arriving):
    dev, lin = peers[kk]
    npeer = len(_FLIPS)
    src = src_refs[a] if srcs_are_pieces[a] else src_refs[a].at[lin]
    dst = land_refs[a].at[lin] if arriving else land_refs[a].at[me]
    return pltpu.make_async_remote_copy(src_ref=src, dst_ref=dst, send_sem=send_sem.at[a * npeer + kk],
                                        recv_sem=recv_sem.at[a * npeer + kk], device_id=dev, device_id_type=MESH_ID)


def _xchg_start(name, gather, scatter, after=None):
    me_out = 4 * lax.axis_index("x") + 2 * lax.axis_index("y") + lax.axis_index("c")
    srcs = list(gather) + list(scatter)
    is_piece = [True] * len(gather) + [False] * len(scatter)
    lands = []
    for a, piece in zip(srcs, is_piece):
        own = a[None] if piece else lax.dynamic_slice_in_dim(a, me_out, 1, axis=0)
        shape = ((N_DEV,) + tuple(a.shape)) if piece else tuple(a.shape)
        start = (me_out,) + (0,) * (len(shape) - 1)
        lands.append(lax.dynamic_update_slice(lax.empty(shape, a.dtype), own, start))
    n = len(srcs)
    nsem = n * len(_FLIPS)
    has_after = after is not None

    def body(*refs):
        src_refs = refs[:n]
        land_refs = refs[n:2 * n]
        outs = refs[2 * n + (1 if has_after else 0):]
        send_sem, recv_sem = outs[0], outs[1]
        token = outs[-1]
        me, peers = _peer_table()
        for kk in range(len(_FLIPS)):
            for a in range(n):
                _split_copy(is_piece, src_refs, land_refs, send_sem, recv_sem, a, kk, me, peers, False).start()
        token[...] = jnp.zeros_like(token)

    out_shape = ([pltpu.SemaphoreType.DMA((nsem,)), pltpu.SemaphoreType.DMA((nsem,))]
                 + [pltpu.HBM(tuple(a.shape), a.dtype) for a in srcs] + [pltpu.HBM(tuple(a.shape), a.dtype) for a in lands]
                 + [jax.ShapeDtypeStruct((8, 128), F32)])
    args = [pltpu.with_memory_space_constraint(a, pltpu.HBM) for a in srcs + lands] + ([after] if has_after else [])
    res = pl.pallas_call(
        body, name=name, out_shape=out_shape,
        in_specs=[_HBM] * (2 * n) + ([_ANY] if has_after else []),
        out_specs=[_SEM, _SEM] + [_HBM] * (2 * n) + [pl.BlockSpec(memory_space=pltpu.VMEM)],
        input_output_aliases={i: 2 + i for i in range(2 * n)},
        compiler_params=pltpu.CompilerParams(has_side_effects=_EFFECT),
    )(*args)
    state = (res[0], res[1], list(res[2:2 + n]), list(res[2 + n:2 + 2 * n]), is_piece)
    return state, res[-1]


def _xchg_wait(name, state, after):
    send_sem, recv_sem, srcs, lands, is_piece = state
    n = len(srcs)

    def body(*refs):
        src_refs = refs[:n]
        land_refs = refs[n:2 * n]
        s_sem, r_sem = refs[2 * n], refs[2 * n + 1]
        me, peers = _peer_table()
        for kk in range(len(_FLIPS)):
            for a in range(n):
                cp = _split_copy(is_piece, src_refs, land_refs, s_sem, r_sem, a, kk, me, peers, True)
                cp.wait_send()
                cp.wait_recv()

    out_shape = [pltpu.HBM(tuple(a.shape), a.dtype) for a in srcs] + [pltpu.HBM(tuple(a.shape), a.dtype) for a in lands]
    res = pl.pallas_call(
        body, name=name, out_shape=out_shape,
        in_specs=[_HBM] * (2 * n) + [_SEM, _SEM, _ANY], out_specs=[_HBM] * (2 * n),
        input_output_aliases={i: i for i in range(2 * n)},
        compiler_params=pltpu.CompilerParams(has_side_effects=_EFFECT),
    )(*srcs, *lands, send_sem, recv_sem, after)
    return list(res[n:])


_SIB = (0, 0, 1)
_ICI = ((0, 1, 0), (1, 0, 0), (1, 1, 0))


def _flip(fl):
    x, y, c = lax.axis_index("x"), lax.axis_index("y"), lax.axis_index("c")
    px = 1 - x if fl[0] else x
    py = 1 - y if fl[1] else y
    pc = 1 - c if fl[2] else c
    return (px, py, pc), 4 * px + 2 * py + pc


def _gather2_start(name, pieces, after=None):
    me_out = 4 * lax.axis_index("x") + 2 * lax.axis_index("y") + lax.axis_index("c")
    pieces = list(pieces)
    n = len(pieces)
    lands = [lax.dynamic_update_slice(lax.empty((N_DEV,) + tuple(a.shape), a.dtype), a[None],
                                      (me_out,) + (0,) * a.ndim) for a in pieces]
    first = (_SIB,) + _ICI
    has_after = after is not None

    def body(*refs):
        src_refs, land_refs = refs[:n], refs[n:2 * n]
        outs = refs[2 * n + (1 if has_after else 0):]
        send_sem, recv_sem, token = outs[0], outs[1], outs[-1]
        _, me = _flip((0, 0, 0))
        for kk, fl in enumerate(first):
            dev, _ = _flip(fl)
            for a in range(n):
                pltpu.make_async_remote_copy(src_ref=src_refs[a], dst_ref=land_refs[a].at[me],
                                             send_sem=send_sem.at[a * 4 + kk], recv_sem=recv_sem.at[a * 4 + kk],
                                             device_id=dev, device_id_type=MESH_ID).start()
        token[...] = jnp.zeros_like(token)

    hbm = [pltpu.HBM(tuple(a.shape), a.dtype) for a in pieces + lands]
    res = pl.pallas_call(
        body, name=name,
        out_shape=[pltpu.SemaphoreType.DMA((4 * n,)), pltpu.SemaphoreType.DMA((4 * n,))] + hbm
        + [jax.ShapeDtypeStruct((8, 128), F32)],
        in_specs=[_HBM] * (2 * n) + ([_ANY] if has_after else []),
        out_specs=[_SEM, _SEM] + [_HBM] * (2 * n) + [pl.BlockSpec(memory_space=pltpu.VMEM)],
        input_output_aliases={i: 2 + i for i in range(2 * n)},
        compiler_params=pltpu.CompilerParams(has_side_effects=_EFFECT),
    )(*([pltpu.with_memory_space_constraint(a, pltpu.HBM) for a in pieces + lands] + ([after] if has_after else [])))
    return (res[0], res[1], list(res[2:2 + n]), list(res[2 + n:2 + 2 * n])), res[-1]


def _gather2_forward(name, state, after):
    send_a, recv_a, pieces, lands = state
    n = len(pieces)
    first = (_SIB,) + _ICI

    def body(*refs):
        src_refs, land_refs = refs[:n], refs[n:2 * n]
        s_a, r_a = refs[2 * n], refs[2 * n + 1]
        outs = refs[2 * n + 3:]
        send_b, recv_b, token = outs[0], outs[1], outs[-1]
        for kk, fl in enumerate(first):
            dev, lin = _flip(fl)
            for a in range(n):
                cp = pltpu.make_async_remote_copy(src_ref=src_refs[a], dst_ref=land_refs[a].at[lin],
                                                  send_sem=s_a.at[a * 4 + kk], recv_sem=r_a.at[a * 4 + kk],
                                                  device_id=dev, device_id_type=MESH_ID)
                cp.wait_send()
                cp.wait_recv()
        sib, _ = _flip(_SIB)
        for j, fl in enumerate(_ICI):
            _, lin = _flip(fl)
            for a in range(n):
                pltpu.make_async_remote_copy(src_ref=land_refs[a].at[lin], dst_ref=land_refs[a].at[lin],
                                             send_sem=send_b.at[a * 3 + j], recv_sem=recv_b.at[a * 3 + j],
                                             device_id=sib, device_id_type=MESH_ID).start()
        token[...] = jnp.zeros_like(token)

    hbm = [pltpu.HBM(tuple(a.shape), a.dtype) for a in pieces + lands]
    res = pl.pallas_call(
        body, name=name,
        out_shape=[pltpu.SemaphoreType.DMA((3 * n,)), pltpu.SemaphoreType.DMA((3 * n,))] + hbm
        + [jax.ShapeDtypeStruct((8, 128), F32)],
        in_specs=[_HBM] * (2 * n) + [_SEM, _SEM, _ANY],
        out_specs=[_SEM, _SEM] + [_HBM] * (2 * n) + [pl.BlockSpec(memory_space=pltpu.VMEM)],
        input_output_aliases={i: 2 + i for i in range(2 * n)},
        compiler_params=pltpu.CompilerParams(has_side_effects=_EFFECT),
    )(*pieces, *lands, send_a, recv_a, after)
    return (res[0], res[1], list(res[2 + n:2 + 2 * n])), res[-1]


def _gather2_wait(name, state, after):
    send_b, recv_b, lands = state
    n = len(lands)

    def body(*refs):
        land_refs = refs[:n]
        s_b, r_b = refs[n], refs[n + 1]
        sib, _ = _flip(_SIB)
        for j, fl in enumerate(_ICI):
            _, sent = _flip(fl)
            _, arriving = _flip((fl[0], fl[1], 1))
            for a in range(n):
                cp = pltpu.make_async_remote_copy(src_ref=land_refs[a].at[sent], dst_ref=land_refs[a].at[arriving],
                                                  send_sem=s_b.at[a * 3 + j], recv_sem=r_b.at[a * 3 + j],
                                                  device_id=sib, device_id_type=MESH_ID)
                cp.wait_send()
                cp.wait_recv()

    res = pl.pallas_call(
        body, name=name, out_shape=[pltpu.HBM(tuple(a.shape), a.dtype) for a in lands],
        in_specs=[_HBM] * n + [_SEM, _SEM, _ANY], out_specs=[_HBM] * n,
        input_output_aliases={i: i for i in range(n)},
        compiler_params=pltpu.CompilerParams(has_side_effects=_EFFECT),
    )(*lands, send_b, recv_b, after)
    return list(res)


def _adam_math(w, g, m, v):
    m = ADAM_B1 * m + (1.0 - ADAM_B1) * g
    v = ADAM_B2 * v + (1.0 - ADAM_B2) * (g * g)
    m_hat = m / (1.0 - ADAM_B1 ** ADAM_STEP)
    v_hat = v / (1.0 - ADAM_B2 ** ADAM_STEP)
    delta = -ADAM_LR * (m_hat / (jnp.sqrt(v_hat) + ADAM_EPS) + ADAM_WD * w)
    return delta, m, v


def _adam(name, w, m, v, parts):
    r, c = w.shape
    npart, _, cp = parts.shape
    tr = _pick(r, (256, 176, 128, 64, 16, 8, 1))

    def body(w_ref, m_ref, v_ref, p_ref, g_ref, d_ref, mo_ref, vo_ref):
        g = p_ref[0].astype(F32)
        for pp in range(1, npart):
            g = g + p_ref[pp].astype(F32)
        g = g[:, 0:c]
        delta, mn, vn = _adam_math(w_ref[...], g, m_ref[...], v_ref[...])
        g_ref[...] = g
        d_ref[...] = delta
        mo_ref[...] = mn
        vo_ref[...] = vn

    blk = pl.BlockSpec((tr, c), lambda i: (i, 0))
    out = jax.ShapeDtypeStruct((r, c), F32)
    return pl.pallas_call(
        body, name=name, grid=(r // tr,),
        in_specs=[blk, blk, blk, pl.BlockSpec((npart, tr, cp), lambda i: (0, i, 0))],
        out_specs=[blk, blk, blk, blk], out_shape=[out, out, out, out], compiler_params=_params(("parallel",)),
    )(w, m, v, parts)


def _adam_t(name, w_t, m_t, v_t, parts):
    c, r = w_t.shape
    npart, _, cp = parts.shape
    tr = _pick(r, (256, 128))

    def body(w_ref, m_ref, v_ref, p_ref, g_ref, d_ref, mo_ref, vo_ref):
        g = p_ref[0].astype(F32)
        for pp in range(1, npart):
            g = g + p_ref[pp].astype(F32)
        g = g.T[0:c, :]
        delta, mn, vn = _adam_math(w_ref[...], g, m_ref[...], v_ref[...])
        g_ref[...] = g
        d_ref[...] = delta
        mo_ref[...] = mn
        vo_ref[...] = vn

    blk = pl.BlockSpec((c, tr), lambda i: (0, i))
    out = jax.ShapeDtypeStruct((c, r), F32)
    return pl.pallas_call(
        body, name=name, grid=(r // tr,),
        in_specs=[blk, blk, blk, pl.BlockSpec((npart, tr, cp), lambda i: (0, i, 0))],
        out_specs=[blk, blk, blk, blk], out_shape=[out, out, out, out], compiler_params=_params(("parallel",)),
    )(w_t, m_t, v_t, parts)


def _small_sum(name, packs):
    def body(p_ref, o_ref):
        tot = p_ref[0]
        for pp in range(1, N_DEV):
            tot = tot + p_ref[pp]
        o_ref[0:8, :] = tot[0:8, :]
        o_ref[8:24, :] = tot[8:24, :] + tot[24:40, :]

    return pl.pallas_call(body, name=name, out_shape=jax.ShapeDtypeStruct((24, D_MODEL), F32),
                          compiler_params=_params())(packs)


def _local_step(x, tgt, g1, gm, g2, gf, b_forget, sinks, weights, send):
    b, s, _ = x.shape
    l = s + PREFIX
    t = b * l
    w1i, meta = weights("ffn1_in", x)
    h0 = jnp.concatenate([jnp.zeros((b, N_PAD, D_MODEL), F32), jnp.broadcast_to(meta[None], (b, N_META, D_MODEL)), x],
                         axis=1).reshape(t, D_MODEL)

    n1 = _rms_fwd("rms1_fwd", h0, g1)
    gu1, a1 = _ffn_in_fwd("ffn1_in_fwd", n1, w1i, dep=weights("ffn1_out:forward", n1))
    (w1o,) = weights("ffn1_out", weights("mix:forward", a1))
    h1 = _mm_nn("ffn1_out_fwd", a1, w1o, alpha=0.5, res=h0, tm_c=(544, 384, 256, 128), tn_c=(1024,))
    wi, wa, wb, wo = weights("mix", h1)
    um = _rms_fwd("rmsm_fwd", h1, gm)
    qkv = _mm_nn("proj_qkv_fwd", um, wi, out_dtype=BF16, tn_c=(768,), cols=(0, QKV_W))
    gates = _mm_nn("proj_gates_fwd", um, wi, tn_c=(1024,), cols=(P_GATES, 2 * D_MODEL), dep=weights("ffn2:forward", qkv))
    f2 = _mm_nn("proj_f_fwd", um, wi, tn_c=(128,), cols=(P_F, 128))
    qkv3 = qkv.reshape(b, l, QKV_W)
    f3 = f2.reshape(b, l, 128)
    bf_row = jnp.pad(b_forget, ((0, 0), (0, 128 - B_HEADS)))
    c_col, c_row = _fgate_fwd("fgate_fwd", f3, bf_row)
    head_of_row = jnp.arange(STACK) // BLOCK
    slopes = jnp.exp2(-8.0 * (head_of_row + 1).astype(F32) / A_HEADS).reshape(STACK, 1)
    sink_rows = jnp.repeat(sinks.reshape(A_HEADS), BLOCK).reshape(STACK, 1)
    swa_bias = _swa_bias(slopes)
    oa3, lse_a = _swa_fwd("swa_fwd", qkv3, slopes, sink_rows, swa_bias)
    ob3, lse_b = _fox_fwd("fox_fwd", qkv3, c_col, c_row)
    oa = oa3.reshape(t, A_WIDTH)
    ob = ob3.reshape(t, B_WIDTH)
    mixed, ya, yb = _branch_gate_fwd("branch_gate_fwd", oa, ob, wa, wb, gates)
    h2 = _mm_nn("mix_out_fwd", mixed, wo, res=h1, tn_c=(1024,))
    w2i, w2o = weights("ffn2", h2)
    n2 = _rms_fwd("rms2_fwd", h2, g2)
    gu2, a2 = _ffn_in_fwd("ffn2_in_fwd", n2, w2i)
    h3 = _mm_nn("ffn2_out_fwd", a2, w2o, alpha=0.5, res=h2, tm_c=(544, 384, 256, 128), tn_c=(1024,))

    dh3_3, loss_blk, dgf = _loss_head("loss_head", h3.reshape(b, l, D_MODEL), gf, tgt)
    dh3 = dh3_3.reshape(t, D_MODEL)

    def ffn_bwd(tag, dh, h_in, g_norm, n_in, gu, a, w_in_blk, w_out):
        dw_out = _mm_tn(tag + "_out_bwd_w", a, dh, alpha=0.5)
        dgu = _ffn_out_bwd_x(tag + "_out_bwd_x", dh, w_out, gu, dep=send(tag + "_out", (dw_out,)))
        dw_in = _ffn_in_bwd_w(tag + "_in_bwd_w", n_in, dgu)
        return _ffn_in_bwd_x(tag + "_in_bwd_x", dgu, w_in_blk, h_in, g_norm, dh, dep=send(tag + "_in", (dw_in,)))

    dh2, dg2 = ffn_bwd("ffn2", dh3, h2, g2, n2, gu2, a2, w2i, w2o)

    dwo = _mm_tn("mix_out_bwd_w", mixed, dh2)
    dya, dyb, dgates = _mix_out_gate_bwd("mix_out_gate_bwd", dh2, wo, gates, ya, yb)
    doa = _mm_nt("branch_a_bwd_x", dya, wa, tn_c=(512,))
    dob = _mm_nt("branch_b_bwd_x", dyb, wb, out_dtype=BF16, tn_c=(512,))
    dwa = _mm_tn("branch_a_bwd_w", oa, dya, tm_c=(512,))
    dwb = _mm_tn("branch_b_bwd_w", ob, dyb, tm_c=(512,))
    dqkv3, dcq, dck = _fox_bwd("fox_bwd", qkv3, c_col, c_row, ob3, lse_b, dob.reshape(b, l, B_WIDTH))
    dqkv3, dsink = _swa_bwd("swa_bwd", qkv3, oa3, lse_a, doa.reshape(b, l, A_WIDTH), slopes, sink_rows, swa_bias, dqkv3)
    dqkv = dqkv3.reshape(t, QKV_W)
    df3, dbf = _fgate_bwd("fgate_bwd", f3, bf_row, dcq, dck)
    df = df3.reshape(t, 128)
    dwi_qkv = _mm_tn("proj_qkv_bwd_w", um, dqkv, tm_c=(512,), tn_c=(768,))
    dwi_g = _mm_tn("proj_gates_bwd_w", um, dgates, tm_c=(512,), tn_c=(512,))
    dwi_f = _mm_tn("proj_f_bwd_w", um, df, tm_c=(512,), tn_c=(128,))
    token = send("mix", (dwi_qkv, dwi_g, dwi_f, dwa, dwb, dwo))
    dh1, dgm = _proj_bwd_x("proj_bwd_x", dqkv, dgates, df, wi, h1, gm, dh2, dep=token)

    dh0, dg1 = ffn_bwd("ffn1", dh1, h0, g1, n1, gu1, a1, w1i, w1o)
    dh0_3 = dh0.reshape(b, l, D_MODEL)
    grad_x = dh0_3[:, PREFIX:, :]
    dmeta = dh0_3[:, N_PAD:PREFIX, :].reshape(b * N_META, D_MODEL)

    misc = jnp.concatenate([dbf[:, 0:B_HEADS], dsink[:, 0].reshape(1, A_HEADS), loss_blk[0:1, 0:1]], axis=1)
    misc = jnp.pad(misc, ((0, 0), (0, D_MODEL - misc.shape[1])))
    row = lax.broadcasted_iota(jnp.int32, (8, D_MODEL), 0)
    vec = jnp.zeros((8, D_MODEL), F32)
    for i, piece in enumerate((dg1, dgm, dg2, dgf, misc)):
        vec = jnp.where(row == i, piece, vec)
    small = jnp.concatenate([vec, dmeta], axis=0)
    return grad_x, small


def _pad_to(a, rows, cols):
    return jnp.pad(a, ((0, rows - a.shape[0]), (0, cols - a.shape[1])))


def _ffn_out_from_gathered(g):
    w = g.reshape(4, FF_SHARD, D_MODEL)
    return jnp.pad(w, ((0, 0), (0, FF_SHARD_P - FF_SHARD), (0, 0))).reshape(D_FF_P, D_MODEL)


def _ffn_out_to_scatter(dw):
    return dw.reshape(4, FF_SHARD_P, D_MODEL)[:, 0:FF_SHARD, :].reshape(N_DEV, FFO_SHARD, D_MODEL)


def _proj_segments():
    segs = [(HEAD_DIM * h, HEAD_DIM, 0, B_SEG + HEAD_DIM * A_HEAD_ORDER.index(h)) for h in range(A_HEADS)]
    segs += [(512, 128, 0, B_SEG + A_WIDTH), (640, 128, 0, B_SEG + A_WIDTH + 128)]
    for first, off in ((768, 0), (1280, 128), (1792, 256)):
        segs += [(first + 128 * hp, 128, 0, PAIR_W * hp + off) for hp in range(4)]
    segs += [(2304, B_HEADS, 2, 0), (2312, 2 * D_MODEL, 1, 0)]
    return segs


def _proj_from_gathered(g):
    def cols(first, width):
        out = []
        for p in range(N_DEV):
            lo, hi = max(first, WIN_SHARD * p), min(first + width, WIN_SHARD * (p + 1))
            if lo < hi:
                out.append(g[p, :, lo - WIN_SHARD * p:hi - WIN_SHARD * p])
        return out

    parts = []
    for arr in (0, 1, 2):
        for first, width, _, _ in sorted((s for s in _proj_segments() if s[2] == arr), key=lambda s: s[3]):
            parts += cols(first, width)
        if arr == 0:
            parts.append(jnp.zeros((D_MODEL, P_GATES - QKV_W), g.dtype))
    parts.append(jnp.zeros((D_MODEL, 128 - B_HEADS), g.dtype))
    return jnp.concatenate(parts, axis=1)


def _proj_to_scatter(dqkv_w, dg_w, df_w):
    arrays = (dqkv_w, dg_w, df_w)
    segs = sorted(_proj_segments())
    blocks = []
    for p in range(N_DEV):
        parts = []
        for first, width, arr, at in segs:
            lo, hi = max(first, WIN_SHARD * p), min(first + width, WIN_SHARD * (p + 1))
            if lo < hi:
                parts.append(arrays[arr][:, at + lo - first:at + hi - first])
        parts.append(jnp.zeros((D_MODEL, WIN_SHARD_P - WIN_SHARD), dqkv_w.dtype))
        blocks.append(jnp.concatenate(parts, axis=1))
    return jnp.stack(blocks, axis=0)


def _a_rows_from_natural(w):
    return jnp.concatenate([w[HEAD_DIM * h:HEAD_DIM * (h + 1)] for h in A_HEAD_ORDER], axis=0)


def _a_rows_to_natural(w):
    return jnp.concatenate([w[HEAD_DIM * A_HEAD_ORDER.index(h):HEAD_DIM * (A_HEAD_ORDER.index(h) + 1)]
                            for h in range(A_HEADS)], axis=0)


def kernel(x, meta_tokens, ffn1_norm, ffn1_w_in, ffn1_w_out, mix_norm, w_in, b_forget, attn_sinks, w_branch_a, w_branch_b, w_out, ffn2_norm, ffn2_w_in, ffn2_w_out, final_norm, loss_target, m_meta_tokens, m_ffn1_norm, m_ffn1_w_in, m_ffn1_w_out, m_mix_norm, m_w_in, m_b_forget, m_attn_sinks, m_w_branch_a, m_w_branch_b, m_w_out, m_ffn2_norm, m_ffn2_w_in, m_ffn2_w_out, m_final_norm, v_meta_tokens, v_ffn1_norm, v_ffn1_w_in, v_ffn1_w_out, v_mix_norm, v_w_in, v_b_forget, v_attn_sinks, v_w_branch_a, v_w_branch_b, v_w_out, v_ffn2_norm, v_ffn2_w_in, v_ffn2_w_out, v_final_norm):
    me = 4 * lax.axis_index("x") + 2 * lax.axis_index("y") + lax.axis_index("c")

    shards = (
        _pad_to(ffn1_w_in[0].astype(BF16), D_MODEL, FF_SHARD_P),
        ffn1_w_out[0].astype(BF16),
        _pad_to(w_in[0].astype(BF16), D_MODEL, WIN_SHARD_P),
        w_branch_a[0].astype(BF16), w_branch_b[0].astype(BF16), w_out[0].astype(BF16),
        _pad_to(ffn2_w_in[0].astype(BF16), D_MODEL, FF_SHARD_P),
        ffn2_w_out[0].astype(BF16),
        meta_tokens,
    )
    s1i, s1o, swi, swa, swb, swo, s2i, s2o, smeta = shards
    first_level, second_level = {}, {}
    first_level["ffn1_in"], tok = _gather2_start("gather_ffn1_in_start", (s1i, smeta))
    first_level["ffn1_out"], tok = _gather2_start("gather_ffn1_out_start", (s1o,), after=tok)
    first_level["mix"], tok = _gather2_start("gather_mix_start", (swi, swa, swb, swo), after=tok)
    first_level["ffn2"], tok = _gather2_start("gather_ffn2_start", (s2i, s2o), after=tok)
    started = {"tok": tok}

    def weights(group, after):
        if group.endswith(":forward"):
            group = group[:-len(":forward")]
            second_level[group], token = _gather2_forward("gather_" + group + "_forward", first_level[group], after)
            return token
        if group == "ffn1_in":
            after = weights("ffn1_in:forward", started["tok"])
        got = _gather2_wait("gather_" + group + "_wait", second_level[group], after)
        if group == "mix":
            gwi, gwa, gwb, gwo = got
            return (_proj_from_gathered(gwi), _a_rows_from_natural(gwa.transpose(1, 0, 2).reshape(A_WIDTH, D_MODEL)),
                    gwb.transpose(1, 0, 2).reshape(B_WIDTH, D_MODEL), gwo.reshape(D_MODEL, D_MODEL))
        if group == "ffn1_in":
            return got[0].reshape(2, 4, D_MODEL, FF_SHARD_P), got[1].transpose(1, 0, 2).reshape(N_META, D_MODEL)
        if group == "ffn1_out":
            return (_ffn_out_from_gathered(got[0]),)
        return got[0].reshape(2, 4, D_MODEL, FF_SHARD_P), _ffn_out_from_gathered(got[1])

    scatter_state = {}

    def send(group, grads):
        if group == "mix":
            dwi_qkv, dwi_g, dwi_f, dwa, dwb, dwo = grads
            dwa = _a_rows_to_natural(dwa)
            blocks = (_proj_to_scatter(dwi_qkv, dwi_g, dwi_f), dwa.reshape(A_WIDTH, N_DEV, 128).transpose(1, 0, 2),
                      dwb.reshape(B_WIDTH, N_DEV, 128).transpose(1, 0, 2), dwo.reshape(N_DEV, 128, D_MODEL))
        elif group.endswith("_in"):
            blocks = (grads[0].reshape(N_DEV, D_MODEL, FF_SHARD_P),)
        else:
            blocks = (_ffn_out_to_scatter(grads[0]),)
        scatter_state[group], token = _xchg_start("scatter_" + group + "_start", (), blocks)
        return token

    gf = final_norm.reshape(1, D_MODEL)
    grad_x, small = _local_step(x, loss_target, ffn1_norm, mix_norm, ffn2_norm, gf, b_forget, attn_sinks, weights, send)

    small_state, after = _xchg_start("gather_small_start", (small,), ())
    out = {}
    updates = (
        ("ffn2_out", (("ffn2_w_out", ffn2_w_out, m_ffn2_w_out, v_ffn2_w_out),)),
        ("ffn2_in", (("ffn2_w_in", ffn2_w_in, m_ffn2_w_in, v_ffn2_w_in),)),
        ("mix", (("w_in", w_in, m_w_in, v_w_in), ("w_branch_a", w_branch_a, m_w_branch_a, v_w_branch_a),
                 ("w_branch_b", w_branch_b, m_w_branch_b, v_w_branch_b), ("w_out", w_out, m_w_out, v_w_out))),
        ("ffn1_out", (("ffn1_w_out", ffn1_w_out, m_ffn1_w_out, v_ffn1_w_out),)),
        ("ffn1_in", (("ffn1_w_in", ffn1_w_in, m_ffn1_w_in, v_ffn1_w_in),)),
    )
    for group, members in updates:
        parts_list = _xchg_wait("scatter_" + group + "_wait", scatter_state[group], after)
        for (nm, w, m, v), parts in zip(members, parts_list):
            if nm.endswith("w_in"):
                res4 = _adam_t("adam_" + nm, w[0].T, m[0].T, v[0].T, parts)
                out[nm] = tuple(r.T[None] for r in res4)
            else:
                res4 = _adam("adam_" + nm, w[0], m[0], v[0], parts)
                out[nm] = tuple(r[None] for r in res4)
            after = res4[0]
    (packs,) = _xchg_wait("gather_small_wait", small_state, after)

    tot = _small_sum("small_sum", packs)
    loss = tot[4, 2 * B_HEADS]
    g_meta = lax.dynamic_slice(tot[8:24, :], (0, me * 128), (N_META, 128))
    out["meta_tokens"] = tuple(_adam("adam_meta_tokens", meta_tokens, m_meta_tokens, v_meta_tokens, g_meta[None]))

    def pack_small(n1, nm, n2, nf, bfv, skv):
        misc = jnp.pad(jnp.concatenate([bfv, skv], axis=1), ((0, 0), (0, D_MODEL - 2 * B_HEADS)))
        row = lax.broadcasted_iota(jnp.int32, (8, D_MODEL), 0)
        vec = jnp.zeros((8, D_MODEL), F32)
        for i, piece in enumerate((n1, nm, n2, nf.reshape(1, D_MODEL), misc)):
            vec = jnp.where(row == i, piece, vec)
        return vec

    w_pack = pack_small(ffn1_norm, mix_norm, ffn2_norm, final_norm, b_forget, attn_sinks)
    m_pack = pack_small(m_ffn1_norm, m_mix_norm, m_ffn2_norm, m_final_norm, m_b_forget, m_attn_sinks)
    v_pack = pack_small(v_ffn1_norm, v_mix_norm, v_ffn2_norm, v_final_norm, v_b_forget, v_attn_sinks)
    small4 = _adam("adam_small", w_pack, m_pack, v_pack, tot[0:8][None])
    for i, nm in enumerate(("ffn1_norm", "mix_norm", "ffn2_norm")):
        out[nm] = tuple(r[i:i + 1] for r in small4)
    out["final_norm"] = tuple(r[3] for r in small4)
    out["b_forget"] = tuple(r[4:5, 0:B_HEADS] for r in small4)
    out["attn_sinks"] = tuple(r[4:5, B_HEADS:2 * B_HEADS] for r in small4)

    names = ("meta_tokens", "ffn1_norm", "ffn1_w_in", "ffn1_w_out", "mix_norm", "w_in", "b_forget", "attn_sinks",
             "w_branch_a", "w_branch_b", "w_out", "ffn2_norm", "ffn2_w_in", "ffn2_w_out", "final_norm")
    return (loss, grad_x) + tuple(out[nm][kind] for kind in range(4) for nm in names)
```

```python
import jax
import jax.numpy as jnp
from jax import lax
from jax.experimental import pallas as pl
from jax.experimental.pallas import tpu as pltpu

F32 = jnp.float32
BF16 = jnp.bfloat16

D_MODEL = 1024
N_META = 16
BLOCK = 128
PREFIX = 128
N_PAD = PREFIX - N_META
HEAD_DIM = 64
A_HEADS = 8
A_KV_HEADS = 2
A_GROUP = 4
B_HEADS = 8
A_WIDTH = 512
A_KV_WIDTH = 128
B_WIDTH = 512
D_FF = 2816
N_DEV = 8
FF_SHARD = 2 * D_FF // N_DEV
FF_SHARD_P = 768
FFO_SHARD = D_FF // N_DEV
FFO_SHARD_P = FF_SHARD_P // 2
D_FF_P = 4 * FF_SHARD_P
W_IN_COLS = 4360
WIN_SHARD = W_IN_COLS // N_DEV
WIN_SHARD_P = 640
PAIR_W = 3 * 128
B_SEG = 4 * PAIR_W
A_SEG = A_WIDTH + 2 * A_KV_WIDTH
QKV_W = B_SEG + A_SEG
P_GATES = 2 * (2 * D_MODEL)
P_F = P_GATES + 2 * D_MODEL
PROJ_P = P_F + 128
A_HEAD_ORDER = (0, 4, 1, 5, 2, 6, 3, 7)
EPS = 1e-6
NEG = -1e30
SCALE = HEAD_DIM ** -0.5
ADAM_LR = 0.001
ADAM_B1 = 0.9
ADAM_B2 = 0.999
ADAM_EPS = 1e-08
ADAM_WD = 0.01
ADAM_STEP = 10
VMEM_LIMIT = 56 * 1024 * 1024
MESH_ID = pl.DeviceIdType.MESH
SMALL_ROWS = 40

_NN = (((1,), (0,)), ((), ()))
_NT = (((1,), (1,)), ((), ()))
_TN = (((0,), (0,)), ((), ()))


def _params(sem=None):
    return pltpu.CompilerParams(dimension_semantics=sem, vmem_limit_bytes=VMEM_LIMIT)


def _pick(n, cands):
    for c in cands:
        if n % c == 0:
            return c
    raise ValueError(f"no tile for {n}")


def _bf(v):
    return v if v.dtype == BF16 else v.astype(BF16)


def _mm(name, a, b, dims, grid, a_spec, b_spec, o_spec, out_shape, out_dtype, acc_shape, k_axis=None, nk=1,
        alpha=1.0, res=None, res_spec=None, dep=None):
    has_res = res is not None
    has_dep = dep is not None

    def body(*refs):
        a_ref, b_ref = refs[0], refs[1]
        r_ref = refs[2] if has_res else None
        o_ref = refs[2 + has_res + has_dep]

        def finish(acc):
            if alpha != 1.0:
                acc = acc * alpha
            if has_res:
                acc = acc + r_ref[...]
            o_ref[...] = acc.astype(o_ref.dtype)

        part = lax.dot_general(_bf(a_ref[...]), _bf(b_ref[...]), dims, preferred_element_type=F32)
        if nk == 1:
            finish(part)
        else:
            acc_ref = refs[-1]
            k = pl.program_id(k_axis)

            @pl.when(k == 0)
            def _():
                acc_ref[...] = part

            @pl.when(k > 0)
            def _():
                acc_ref[...] += part

            @pl.when(k == nk - 1)
            def _():
                finish(acc_ref[...])

    in_specs = [a_spec, b_spec] + ([res_spec] if has_res else []) + ([pl.BlockSpec(memory_space=pl.ANY)] if has_dep else [])
    args = (a, b) + ((res,) if has_res else ()) + ((dep,) if has_dep else ())
    sem = tuple("arbitrary" if (nk > 1 and i == k_axis) else "parallel" for i in range(len(grid)))
    return pl.pallas_call(
        body, name=name, grid=grid, in_specs=in_specs, out_specs=o_spec,
        out_shape=jax.ShapeDtypeStruct(out_shape, out_dtype),
        scratch_shapes=[pltpu.VMEM(acc_shape, F32)] if nk > 1 else [],
        compiler_params=_params(sem),
    )(*args)


def _mm_nn(name, a, b, out_dtype=F32, alpha=1.0, res=None, tn_c=(512, 640, 256, 128), cols=None,
           tm_c=(1088, 768, 512, 256, 128), dep=None):
    t, k = a.shape
    c0, n = (0, b.shape[1]) if cols is None else cols
    tm = _pick(t, tm_c)
    tn = _pick(n, tn_c)
    assert c0 % tn == 0
    jb = c0 // tn
    return _mm(name, a, b, _NN, (t // tm, n // tn),
               pl.BlockSpec((tm, k), lambda i, j: (i, 0)), pl.BlockSpec((k, tn), lambda i, j: (0, jb + j)),
               pl.BlockSpec((tm, tn), lambda i, j: (i, j)), (t, n), out_dtype, None,
               alpha=alpha, res=res, res_spec=pl.BlockSpec((tm, tn), lambda i, j: (i, j)), dep=dep)


def _mm_nt(name, a, b, out_dtype=F32, alpha=1.0, tn_c=(768, 512, 256, 128), tk_c=None, dep=None, res=None, kcols=None):
    t, k = a.shape
    n = b.shape[0]
    c0 = 0 if kcols is None else kcols[0]
    tm = _pick(t, (1088, 768, 512, 256, 128))
    tn = _pick(n, tn_c)
    tk = k if tk_c is None else _pick(k, tk_c)
    nk = k // tk
    assert c0 % tk == 0
    kb = c0 // tk
    return _mm(name, a, b, _NT, (t // tm, n // tn, nk),
               pl.BlockSpec((tm, tk), lambda i, j, kk: (i, kk)), pl.BlockSpec((tn, tk), lambda i, j, kk: (j, kb + kk)),
               pl.BlockSpec((tm, tn), lambda i, j, kk: (i, j)), (t, n), out_dtype, (tm, tn), k_axis=2, nk=nk, alpha=alpha,
               dep=dep, res=res, res_spec=pl.BlockSpec((tm, tn), lambda i, j, kk: (i, j)))


def _mm_tn(name, a, b, out_dtype=BF16, alpha=1.0, tm_c=(768, 512, 256, 128), tn_c=(512, 640, 256, 128)):
    t, m = a.shape
    n = b.shape[1]
    tm = _pick(m, tm_c)
    tn = _pick(n, tn_c)
    tk = t
    nk = 1
    return _mm(name, a, b, _TN, (m // tm, n // tn, nk),
               pl.BlockSpec((tk, tm), lambda i, j, kk: (kk, i)), pl.BlockSpec((tk, tn), lambda i, j, kk: (kk, j)),
               pl.BlockSpec((tm, tn), lambda i, j, kk: (i, j)), (m, n), out_dtype, (tm, tn), k_axis=2, nk=nk, alpha=alpha)


def _silu_grads(g, u, d):
    sg = jax.nn.sigmoid(g)
    return d * u * (sg * (1.0 + g * (1.0 - sg))), d * (g * sg)


def _ffn_in_fwd(name, n, wblk, dep=None):
    t = n.shape[0]
    tm = _pick(t, (1088, 768, 512, 256, 128))
    has_dep = dep is not None

    def body(n_ref, w_ref, *rest):
        gu_ref, a_ref = rest[-2], rest[-1]
        nv = n_ref[...]
        g = lax.dot_general(nv, w_ref[0], _NN, preferred_element_type=F32)
        u = lax.dot_general(nv, w_ref[1], _NN, preferred_element_type=F32)
        gu_ref[0] = g.astype(BF16)
        gu_ref[1] = u.astype(BF16)
        a_ref[...] = ((g * jax.nn.sigmoid(g)) * u).astype(BF16)

    return pl.pallas_call(
        body, name=name, grid=(t // tm, 4),
        in_specs=[pl.BlockSpec((tm, D_MODEL), lambda i, j: (i, 0)),
                  pl.BlockSpec((2, None, D_MODEL, FF_SHARD_P), lambda i, j: (0, j, 0, 0))]
        + ([pl.BlockSpec(memory_space=pl.ANY)] if has_dep else []),
        out_specs=[pl.BlockSpec((2, tm, FF_SHARD_P), lambda i, j: (0, i, j)),
                   pl.BlockSpec((tm, FF_SHARD_P), lambda i, j: (i, j))],
        out_shape=[jax.ShapeDtypeStruct((2, t, D_FF_P), BF16), jax.ShapeDtypeStruct((t, D_FF_P), BF16)],
        compiler_params=_params(("parallel", "parallel")),
    )(*((n, wblk) + ((dep,) if has_dep else ())))


def _ffn_out_bwd_x(name, dh, w_out, gu, dep=None):
    t = dh.shape[0]
    tm = _pick(t, (1088, 768, 512, 256, 128))
    has_dep = dep is not None

    def body(dh_ref, w_ref, gu_ref, *rest):
        o_ref = rest[-1]
        da = lax.dot_general(_bf(dh_ref[...]), w_ref[...], _NT, preferred_element_type=F32) * 0.5
        dg, du = _silu_grads(gu_ref[0].astype(F32), gu_ref[1].astype(F32), da)
        o_ref[0] = dg.astype(BF16)
        o_ref[1] = du.astype(BF16)

    gu_spec = pl.BlockSpec((2, tm, FF_SHARD_P), lambda i, j: (0, i, j))
    return pl.pallas_call(
        body, name=name, grid=(t // tm, 4),
        in_specs=[pl.BlockSpec((tm, D_MODEL), lambda i, j: (i, 0)), pl.BlockSpec((FF_SHARD_P, D_MODEL), lambda i, j: (j, 0)),
                  gu_spec] + ([pl.BlockSpec(memory_space=pl.ANY)] if has_dep else []),
        out_specs=gu_spec, out_shape=jax.ShapeDtypeStruct((2, t, D_FF_P), BF16),
        compiler_params=_params(("parallel", "parallel")),
    )(*((dh, w_out, gu) + ((dep,) if has_dep else ())))


def _rms_bwd_rows(dn, h, g, dres):
    r = lax.rsqrt(jnp.mean(h * h, axis=-1, keepdims=True) + EPS)
    tv = dn * g
    dot = jnp.mean(tv * h, axis=-1, keepdims=True)
    return dres + (r * tv - h * (r * r * r * dot)), jnp.sum(dn * (h * r), axis=0, keepdims=True)


def _accumulate_rows(ref, part, first):
    @pl.when(first)
    def _():
        ref[...] = part

    @pl.when(jnp.logical_not(first))
    def _():
        ref[...] += part


def _ffn_in_bwd_x(name, dgu, wblk, h_in, g_norm, dres, dep=None):
    t = dgu.shape[1]
    tm = _pick(t, (272, 256, 128))
    has_dep = dep is not None

    def body(d_ref, w_ref, h_ref, g_ref, r_ref, *rest):
        dh_ref, dg_ref = rest[-2], rest[-1]
        acc = None
        for s in range(2):
            for j in range(4):
                part = lax.dot_general(d_ref[s, :, FF_SHARD_P * j:FF_SHARD_P * (j + 1)], w_ref[s, j], _NT,
                                       preferred_element_type=F32)
                acc = part if acc is None else acc + part
        dh, dg = _rms_bwd_rows(acc, h_ref[...], g_ref[...], r_ref[...])
        dh_ref[...] = dh
        _accumulate_rows(dg_ref, dg, pl.program_id(0) == 0)

    row = pl.BlockSpec((tm, D_MODEL), lambda i: (i, 0))
    vec = pl.BlockSpec((1, D_MODEL), lambda i: (0, 0))
    return pl.pallas_call(
        body, name=name, grid=(t // tm,),
        in_specs=[pl.BlockSpec((2, tm, D_FF_P), lambda i: (0, i, 0)),
                  pl.BlockSpec((2, 4, D_MODEL, FF_SHARD_P), lambda i: (0, 0, 0, 0)), row, vec, row]
        + ([pl.BlockSpec(memory_space=pl.ANY)] if has_dep else []),
        out_specs=[row, vec],
        out_shape=[jax.ShapeDtypeStruct((t, D_MODEL), F32), jax.ShapeDtypeStruct((1, D_MODEL), F32)],
        compiler_params=_params(("arbitrary",)),
    )(*((dgu, wblk, h_in, g_norm, dres) + ((dep,) if has_dep else ())))


def _proj_bwd_x(name, dqkv, dgates, df, wi, h_in, g_norm, dres, dep=None):
    t = dqkv.shape[0]
    tm = _pick(t, (272, 256, 128))
    has_dep = dep is not None

    def body(q_ref, gt_ref, f_ref, w_ref, h_ref, g_ref, r_ref, *rest):
        dh_ref, dg_ref = rest[-2], rest[-1]
        acc = lax.dot_general(q_ref[...], w_ref[:, 0:QKV_W], _NT, preferred_element_type=F32)
        acc = acc + lax.dot_general(gt_ref[...], w_ref[:, P_GATES:P_F], _NT, preferred_element_type=F32)
        acc = acc + lax.dot_general(f_ref[...], w_ref[:, P_F:PROJ_P], _NT, preferred_element_type=F32)
        dh, dg = _rms_bwd_rows(acc, h_ref[...], g_ref[...], r_ref[...])
        dh_ref[...] = dh
        _accumulate_rows(dg_ref, dg, pl.program_id(0) == 0)

    row = pl.BlockSpec((tm, D_MODEL), lambda i: (i, 0))
    vec = pl.BlockSpec((1, D_MODEL), lambda i: (0, 0))
    return pl.pallas_call(
        body, name=name, grid=(t // tm,),
        in_specs=[pl.BlockSpec((tm, QKV_W), lambda i: (i, 0)), pl.BlockSpec((tm, 2 * D_MODEL), lambda i: (i, 0)),
                  pl.BlockSpec((tm, 128), lambda i: (i, 0)), pl.BlockSpec((D_MODEL, PROJ_P), lambda i: (0, 0)), row, vec, row]
        + ([pl.BlockSpec(memory_space=pl.ANY)] if has_dep else []),
        out_specs=[row, vec],
        out_shape=[jax.ShapeDtypeStruct((t, D_MODEL), F32), jax.ShapeDtypeStruct((1, D_MODEL), F32)],
        compiler_params=_params(("arbitrary",)),
    )(*((dqkv, dgates, df, wi, h_in, g_norm, dres) + ((dep,) if has_dep else ())))


def _ffn_in_bwd_w(name, n, dgu):
    t = n.shape[0]
    tk = t
    nk = 1
    return _mm(name, n, dgu, _TN, (2, 4, nk),
               pl.BlockSpec((tk, D_MODEL), lambda s, j, kk: (kk, 0)),
               pl.BlockSpec((None, tk, FF_SHARD_P), lambda s, j, kk: (s, kk, j)),
               pl.BlockSpec((None, None, D_MODEL, FF_SHARD_P), lambda s, j, kk: (s, j, 0, 0)),
               (2, 4, D_MODEL, FF_SHARD_P), BF16, (D_MODEL, FF_SHARD_P), k_axis=2, nk=nk)


def _rms_fwd(name, h, g):
    t = h.shape[0]
    tm = _pick(t, (544, 384, 256, 128))

    def body(h_ref, g_ref, o_ref):
        hv = h_ref[...]
        r = lax.rsqrt(jnp.mean(hv * hv, axis=-1, keepdims=True) + EPS)
        o_ref[...] = ((hv * r) * g_ref[...]).astype(BF16)

    return pl.pallas_call(
        body, name=name, grid=(t // tm,),
        in_specs=[pl.BlockSpec((tm, D_MODEL), lambda i: (i, 0)), pl.BlockSpec((1, D_MODEL), lambda i: (0, 0))],
        out_specs=pl.BlockSpec((tm, D_MODEL), lambda i: (i, 0)),
        out_shape=jax.ShapeDtypeStruct((t, D_MODEL), BF16), compiler_params=_params(("parallel",)),
    )(h, g)


def _rms_bwd(name, h, g, dn, dres):
    t = h.shape[0]
    tm = _pick(t, (544, 384, 256, 128))

    def body(h_ref, g_ref, dn_ref, dres_ref, dh_ref, dg_ref):
        i = pl.program_id(0)
        hv = h_ref[...]
        dnv = dn_ref[...]
        r = lax.rsqrt(jnp.mean(hv * hv, axis=-1, keepdims=True) + EPS)
        tv = dnv * g_ref[...]
        dot = jnp.mean(tv * hv, axis=-1, keepdims=True)
        dh_ref[...] = dres_ref[...] + (r * tv - hv * (r * r * r * dot))
        part = jnp.sum(dnv * (hv * r), axis=0, keepdims=True)

        @pl.when(i == 0)
        def _():
            dg_ref[...] = part

        @pl.when(i > 0)
        def _():
            dg_ref[...] += part

    row = pl.BlockSpec((tm, D_MODEL), lambda i: (i, 0))
    vec = pl.BlockSpec((1, D_MODEL), lambda i: (0, 0))
    return pl.pallas_call(
        body, name=name, grid=(t // tm,), in_specs=[row, vec, row, row], out_specs=[row, vec],
        out_shape=[jax.ShapeDtypeStruct((t, D_MODEL), F32), jax.ShapeDtypeStruct((1, D_MODEL), F32)],
        compiler_params=_params(("arbitrary",)),
    )(h, g, dn, dres)


def _branch_gate_fwd(name, oa, ob, wa, wb, gates):
    t = gates.shape[0]
    tm = _pick(t, (544, 384, 256, 128))

    def body(oa_ref, ob_ref, wa_ref, wb_ref, g_ref, o_ref, ya_ref, yb_ref):
        ya = lax.dot_general(_bf(oa_ref[...]), wa_ref[...], _NN, preferred_element_type=F32)
        yb = lax.dot_general(_bf(ob_ref[...]), wb_ref[...], _NN, preferred_element_type=F32)
        sa = jax.nn.sigmoid(g_ref[:, 0:D_MODEL])
        sb = jax.nn.sigmoid(g_ref[:, D_MODEL:2 * D_MODEL])
        o_ref[...] = (sa * ya + sb * yb).astype(BF16)
        ya_ref[...] = ya.astype(BF16)
        yb_ref[...] = yb.astype(BF16)

    blk = pl.BlockSpec((tm, D_MODEL), lambda i: (i, 0))
    narrow = pl.BlockSpec((tm, A_WIDTH), lambda i: (i, 0))
    wide = pl.BlockSpec((tm, 2 * D_MODEL), lambda i: (i, 0))
    wspec = pl.BlockSpec((A_WIDTH, D_MODEL), lambda i: (0, 0))
    out = jax.ShapeDtypeStruct((t, D_MODEL), BF16)
    return pl.pallas_call(
        body, name=name, grid=(t // tm,), in_specs=[narrow, narrow, wspec, wspec, wide], out_specs=[blk, blk, blk],
        out_shape=[out, out, out], compiler_params=_params(("parallel",)),
    )(oa, ob, wa, wb, gates)


def _mix_out_gate_bwd(name, dh, wo, gates, ya, yb):
    t = gates.shape[0]
    tm = _pick(t, (544, 384, 256, 128))

    def body(dh_ref, w_ref, g_ref, ya_ref, yb_ref, dya_ref, dyb_ref, dg_ref):
        dm = lax.dot_general(_bf(dh_ref[...]), w_ref[...], _NT, preferred_element_type=F32)
        sa = jax.nn.sigmoid(g_ref[:, 0:D_MODEL])
        sb = jax.nn.sigmoid(g_ref[:, D_MODEL:2 * D_MODEL])
        dya_ref[...] = (dm * sa).astype(BF16)
        dyb_ref[...] = (dm * sb).astype(BF16)
        dg_ref[:, 0:D_MODEL] = (dm * ya_ref[...].astype(F32) * (sa * (1.0 - sa))).astype(BF16)
        dg_ref[:, D_MODEL:2 * D_MODEL] = (dm * yb_ref[...].astype(F32) * (sb * (1.0 - sb))).astype(BF16)

    blk = pl.BlockSpec((tm, D_MODEL), lambda i: (i, 0))
    wide = pl.BlockSpec((tm, 2 * D_MODEL), lambda i: (i, 0))
    out = jax.ShapeDtypeStruct((t, D_MODEL), BF16)
    return pl.pallas_call(
        body, name=name, grid=(t // tm,),
        in_specs=[blk, pl.BlockSpec((D_MODEL, D_MODEL), lambda i: (0, 0)), wide, blk, blk], out_specs=[blk, blk, wide],
        out_shape=[out, out, jax.ShapeDtypeStruct((t, 2 * D_MODEL), BF16)], compiler_params=_params(("parallel",)),
    )(dh, wo, gates, ya, yb)


def _loss_head(name, h3, gf, tgt):
    b, l, _ = h3.shape
    nb = l // BLOCK

    def body(h_ref, g_ref, t_ref, dh_ref, loss_ref, dg_ref):
        first = (pl.program_id(0) == 0) & (pl.program_id(1) == 0)
        real = (pl.program_id(1) > 0).astype(F32)
        hv = h_ref[...]
        g = g_ref[...]
        r = lax.rsqrt(jnp.mean(hv * hv, axis=-1, keepdims=True) + EPS)
        xn = hv * r
        err = (xn * g - t_ref[...]) * real
        lpart = 0.5 * jnp.sum(jnp.mean(err * err, axis=-1, keepdims=True), axis=0, keepdims=True)
        dy = err * (1.0 / D_MODEL)
        tv = dy * g
        dot = jnp.mean(tv * hv, axis=-1, keepdims=True)
        dh_ref[...] = r * tv - hv * (r * r * r * dot)
        gpart = jnp.sum(dy * xn, axis=0, keepdims=True)

        @pl.when(first)
        def _():
            loss_ref[...] = jnp.zeros_like(loss_ref)
            dg_ref[...] = jnp.zeros_like(dg_ref)

        loss_ref[...] += jnp.broadcast_to(lpart, loss_ref.shape)
        dg_ref[...] += gpart

    return pl.pallas_call(
        body, name=name, grid=(b, nb),
        in_specs=[pl.BlockSpec((None, BLOCK, D_MODEL), lambda bi, n: (bi, n, 0)),
                  pl.BlockSpec((1, D_MODEL), lambda bi, n: (0, 0)),
                  pl.BlockSpec((None, BLOCK, D_MODEL), lambda bi, n: (bi, jnp.maximum(n - 1, 0), 0))],
        out_specs=[pl.BlockSpec((None, BLOCK, D_MODEL), lambda bi, n: (bi, n, 0)),
                   pl.BlockSpec((8, 128), lambda bi, n: (0, 0)),
                   pl.BlockSpec((1, D_MODEL), lambda bi, n: (0, 0))],
        out_shape=[jax.ShapeDtypeStruct(h3.shape, F32), jax.ShapeDtypeStruct((8, 128), F32),
                   jax.ShapeDtypeStruct((1, D_MODEL), F32)],
        compiler_params=_params(("arbitrary", "arbitrary")),
    )(h3, gf, tgt)


def _fgate_fwd(name, f3, bf_row):
    b, l, _ = f3.shape
    nb = l // BLOCK

    def body(f_ref, b_ref, cc_ref, cr_ref):
        r_i = lax.broadcasted_iota(jnp.int32, (BLOCK, BLOCK), 0)
        c_i = lax.broadcasted_iota(jnp.int32, (BLOCK, BLOCK), 1)
        tri = (r_i >= c_i).astype(F32)
        carry = jnp.zeros((1, 128), F32)
        for blk in range(nb):
            rows = slice(blk * BLOCK, (blk + 1) * BLOCK)
            z = f_ref[rows, :] + b_ref[...]
            lf = jnp.minimum(z, 0.0) - jnp.log(1.0 + jnp.exp(-jnp.abs(z)))
            cb = jnp.dot(tri, lf, preferred_element_type=F32, precision=lax.Precision.HIGHEST) + carry
            carry = cb[BLOCK - 1:BLOCK, :]
            cbt = cb.T
            for hh in range(B_HEADS):
                cc_ref[hh, rows, :] = jnp.sum(jnp.where(c_i == hh, cb, 0.0), axis=1, keepdims=True)
                cr_ref[hh, :, rows] = cbt[hh:hh + 1, :]

    return pl.pallas_call(
        body, name=name, grid=(b,),
        in_specs=[pl.BlockSpec((None, l, 128), lambda bi: (bi, 0, 0)),
                  pl.BlockSpec((1, 128), lambda bi: (0, 0))],
        out_specs=[pl.BlockSpec((None, B_HEADS, l, 1), lambda bi: (bi, 0, 0, 0)),
                   pl.BlockSpec((None, B_HEADS, 1, l), lambda bi: (bi, 0, 0, 0))],
        out_shape=[jax.ShapeDtypeStruct((b, B_HEADS, l, 1), F32), jax.ShapeDtypeStruct((b, B_HEADS, 1, l), F32)],
        compiler_params=_params(("parallel",)),
    )(f3, bf_row)


def _fgate_bwd(name, f3, bf_row, dcq, dck):
    b, l, _ = f3.shape
    nb = l // BLOCK

    def body(f_ref, b_ref, dcq_ref, dck_ref, df_ref, db_ref):
        r_i = lax.broadcasted_iota(jnp.int32, (BLOCK, BLOCK), 0)
        c_i = lax.broadcasted_iota(jnp.int32, (BLOCK, BLOCK), 1)
        tri = (r_i <= c_i).astype(F32)
        carry = jnp.zeros((1, 128), F32)
        total = jnp.zeros((1, 128), F32)
        for blk in range(nb - 1, -1, -1):
            rows = slice(blk * BLOCK, (blk + 1) * BLOCK)
            krows = jnp.concatenate([dck_ref[hh, :, rows] for hh in range(B_HEADS)]
                                    + [jnp.zeros((BLOCK - B_HEADS, BLOCK), F32)], axis=0)
            dcb = krows.T
            for hh in range(B_HEADS):
                dcb = dcb + jnp.where(c_i == hh, dcq_ref[hh, rows, :], 0.0)
            rc = jnp.dot(tri, dcb, preferred_element_type=F32, precision=lax.Precision.HIGHEST) + carry
            carry = rc[0:1, :]
            z = f_ref[rows, :] + b_ref[...]
            df = rc * (1.0 / (1.0 + jnp.exp(z)))
            df_ref[rows, :] = df.astype(BF16)
            total = total + jnp.sum(df, axis=0, keepdims=True)

        @pl.when(pl.program_id(0) == 0)
        def _():
            db_ref[...] = total

        @pl.when(pl.program_id(0) > 0)
        def _():
            db_ref[...] += total

    return pl.pallas_call(
        body, name=name, grid=(b,),
        in_specs=[pl.BlockSpec((None, l, 128), lambda bi: (bi, 0, 0)),
                  pl.BlockSpec((1, 128), lambda bi: (0, 0)),
                  pl.BlockSpec((None, B_HEADS, l, 1), lambda bi: (bi, 0, 0, 0)),
                  pl.BlockSpec((None, B_HEADS, 1, l), lambda bi: (bi, 0, 0, 0))],
        out_specs=[pl.BlockSpec((None, l, 128), lambda bi: (bi, 0, 0)), pl.BlockSpec((1, 128), lambda bi: (0, 0))],
        out_shape=[jax.ShapeDtypeStruct((b, l, 128), BF16), jax.ShapeDtypeStruct((1, 128), F32)],
        compiler_params=_params(("arbitrary",)),
    )(f3, bf_row, dcq, dck)


A_Q_BLK = B_SEG // A_WIDTH
A_K_BLK = (B_SEG + A_WIDTH) // 128
A_V_BLK = A_K_BLK + 1
A_SEG_BLK = B_SEG // A_SEG
STACK = A_HEADS * BLOCK


def _lane_lo():
    return lax.broadcasted_iota(jnp.int32, (1, 128), 1) < HEAD_DIM


def _stack_heads(x, masked):
    lo = _lane_lo()
    blks = [x[:, 128 * j:128 * (j + 1)] for j in range(4)]
    if not masked:
        return jnp.concatenate(blks + blks, axis=0)
    zero = jnp.zeros_like(blks[0])
    return jnp.concatenate([jnp.where(lo, bk, zero) for bk in blks] + [jnp.where(lo, zero, bk) for bk in blks], axis=0)


def _unstack_heads(y):
    lo = _lane_lo()
    return jnp.concatenate([jnp.where(lo, y[128 * j:128 * (j + 1)], y[128 * (4 + j):128 * (5 + j)]) for j in range(4)], axis=1)


def _swa_bias(slopes):
    r_i = jnp.arange(STACK)[:, None]
    c_i = jnp.arange(3 * BLOCK)[None, :]
    seg = c_i >> 7
    out = []
    for n in range(3):
        qpos = n * BLOCK + (r_i & (BLOCK - 1))
        kpos = jnp.where(seg == 0, c_i, (n - 2) * BLOCK + c_i)
        dist = qpos - kpos
        band = (seg != 0) & (dist < BLOCK) & (kpos >= PREFIX)
        meta = (seg == 0) & (c_i >= N_PAD)
        out.append(jnp.where((dist >= 0) & (band | meta), -slopes * dist.astype(F32), NEG))
    return jnp.stack(out, axis=0)


def _swa_scores(q, kcat, n, slope, bias):
    s = lax.dot_general(q, kcat, _NT, preferred_element_type=F32) + bias
    further = slope * (-BLOCK * jnp.maximum(n - 2, 0)).astype(F32)
    return jnp.concatenate([s[:, 0:BLOCK] + further, s[:, BLOCK:]], axis=1)


def _swa_specs():
    def kv(col_blk):
        return [pl.BlockSpec((None, BLOCK, 128), lambda b, n: (b, 0, col_blk)),
                pl.BlockSpec((None, BLOCK, 128), lambda b, n: (b, jnp.maximum(n - 1, 0), col_blk)),
                pl.BlockSpec((None, BLOCK, 128), lambda b, n: (b, n, col_blk))]

    q_spec = pl.BlockSpec((None, BLOCK, A_WIDTH), lambda b, n: (b, n, A_Q_BLK))
    o_spec = pl.BlockSpec((None, BLOCK, A_WIDTH), lambda b, n: (b, n, 0))
    col = pl.BlockSpec((STACK, 1), lambda b, n: (0, 0))
    bias = pl.BlockSpec((None, STACK, 3 * BLOCK), lambda b, n: (jnp.minimum(n, 2), 0, 0))
    lse_spec = pl.BlockSpec((None, A_HEADS, BLOCK, 1), lambda b, n: (b, 0, n, 0))
    return q_spec, kv(A_K_BLK), kv(A_V_BLK), o_spec, [col, col, bias], lse_spec


def _swa_fwd(name, qkv, slopes, sinks, bias):
    b, l, _ = qkv.shape
    nb = l // BLOCK

    def body(q_ref, k0_ref, kp_ref, kc_ref, v0_ref, vp_ref, vc_ref, sl_ref, sk_ref, bias_ref, o_ref, lse_ref):
        n = pl.program_id(1)
        qs = _stack_heads(q_ref[...], True) * SCALE
        kcat = jnp.concatenate([k0_ref[...], kp_ref[...], kc_ref[...]], axis=0)
        vcat = jnp.concatenate([v0_ref[...], vp_ref[...], vc_ref[...]], axis=0)
        s = _swa_scores(qs, kcat, n, sl_ref[...], bias_ref[...])
        sink = sk_ref[...]
        m = jnp.maximum(jnp.max(s, axis=-1, keepdims=True), sink)
        p = jnp.exp(s - m)
        den = jnp.sum(p, axis=-1, keepdims=True) + jnp.exp(sink - m)
        o = lax.dot_general(p.astype(BF16), vcat, _NN, preferred_element_type=F32) / den
        o_ref[...] = _unstack_heads(o)
        lse_ref[...] = (m + jnp.log(den)).reshape(A_HEADS, BLOCK, 1)

    q_spec, k_specs, v_specs, o_spec, consts, lse_spec = _swa_specs()
    return pl.pallas_call(
        body, name=name, grid=(b, nb),
        in_specs=[q_spec] + k_specs + v_specs + consts, out_specs=[o_spec, lse_spec],
        out_shape=[jax.ShapeDtypeStruct((b, l, A_WIDTH), F32), jax.ShapeDtypeStruct((b, A_HEADS, l, 1), F32)],
        compiler_params=_params(("parallel", "parallel")),
    )(qkv, qkv, qkv, qkv, qkv, qkv, qkv, slopes, sinks, bias)


def _swa_bwd(name, qkv, o, lse, do, slopes, sinks, bias, dqkv):
    b, l, _ = qkv.shape
    nb = l // BLOCK

    def body(q_ref, k0_ref, kp_ref, kc_ref, v0_ref, vp_ref, vc_ref, o_ref, lse_ref, do_ref, sl_ref, sk_ref, bias_ref, _,
             dx_ref, ds_ref, dk_acc, dv_acc):
        bi = pl.program_id(0)
        n = pl.program_id(1)
        qs = _stack_heads(q_ref[...], True) * SCALE
        dos32 = _stack_heads(do_ref[...], True)
        dos = dos32.astype(BF16)
        os_ = _stack_heads(o_ref[...], False)
        lsev = lse_ref[...].reshape(STACK, 1)
        kcat = jnp.concatenate([k0_ref[...], kp_ref[...], kc_ref[...]], axis=0)
        vcat = jnp.concatenate([v0_ref[...], vp_ref[...], vc_ref[...]], axis=0)
        s = _swa_scores(qs, kcat, n, sl_ref[...], bias_ref[...])
        p = jnp.exp(s - lsev)
        dsum = jnp.sum(dos32 * os_, axis=-1, keepdims=True)
        dp = lax.dot_general(dos, vcat, _NT, preferred_element_type=F32)
        dsc = (p * (dp - dsum)).astype(BF16)
        dq = lax.dot_general(dsc, kcat, _NN, preferred_element_type=F32) * SCALE
        row0 = pl.multiple_of(n * BLOCK, BLOCK)
        dx_ref[pl.ds(row0, BLOCK), 0:A_WIDTH] = _unstack_heads(dq).astype(BF16)
        dkc = lax.dot_general(dsc, qs, _TN, preferred_element_type=F32)
        dvc = lax.dot_general(p.astype(BF16), dos, _TN, preferred_element_type=F32)

        @pl.when(n == 0)
        def _():
            dk_acc[...] = jnp.zeros_like(dk_acc)
            dv_acc[...] = jnp.zeros_like(dv_acc)

        starts = (0, pl.multiple_of(jnp.maximum(n - 1, 0) * BLOCK, BLOCK), row0)
        for t, st in enumerate(starts):
            dk_acc[pl.ds(st, BLOCK), :] += dkc[t * BLOCK:(t + 1) * BLOCK, :]
            dv_acc[pl.ds(st, BLOCK), :] += dvc[t * BLOCK:(t + 1) * BLOCK, :]

        @pl.when(n == nb - 1)
        def _():
            dx_ref[:, A_WIDTH:A_WIDTH + 128] = dk_acc[...].astype(BF16)
            dx_ref[:, A_WIDTH + 128:A_SEG] = dv_acc[...].astype(BF16)

        dsink = -(jnp.exp(sk_ref[...] - lsev) * dsum)
        r8 = lax.broadcasted_iota(jnp.int32, (8, 128), 0)
        acc = jnp.zeros((8, 128), F32)
        for hh in range(A_HEADS):
            acc = acc + jnp.where(r8 == hh, jnp.sum(dsink[hh * BLOCK:(hh + 1) * BLOCK, :]), 0.0)

        @pl.when((bi == 0) & (n == 0))
        def _():
            ds_ref[...] = jnp.zeros_like(ds_ref)

        ds_ref[...] += acc

    q_spec, k_specs, v_specs, o_spec, consts, lse_spec = _swa_specs()
    return pl.pallas_call(
        body, name=name, grid=(b, nb),
        in_specs=[q_spec] + k_specs + v_specs + [o_spec, lse_spec, o_spec] + consts + [pl.BlockSpec(memory_space=pl.ANY)],
        out_specs=[pl.BlockSpec((None, l, A_SEG), lambda bb, n: (bb, 0, A_SEG_BLK)),
                   pl.BlockSpec((8, 128), lambda bb, n: (0, 0))],
        out_shape=[jax.ShapeDtypeStruct(dqkv.shape, BF16), jax.ShapeDtypeStruct((8, 128), F32)],
        scratch_shapes=[pltpu.VMEM((l, 128), F32), pltpu.VMEM((l, 128), F32)],
        input_output_aliases={13: 0},
        compiler_params=_params(("arbitrary", "arbitrary")),
    )(qkv, qkv, qkv, qkv, qkv, qkv, qkv, o, lse, do, slopes, sinks, bias, dqkv)


def _fox_mask(qk, ck, i):
    kh = qk.shape[1]
    qpos = i * BLOCK + lax.broadcasted_iota(jnp.int32, (BLOCK, kh), 0)
    kpos = lax.broadcasted_iota(jnp.int32, (BLOCK, kh), 1)
    return jnp.where((kpos <= qpos) & (kpos >= N_PAD), qk - ck, NEG)


def _pick_head(x, hh):
    lo = _lane_lo()
    return jnp.where(lo if hh == 0 else jnp.logical_not(lo), x, jnp.zeros_like(x))


def _both_heads(x):
    return jnp.concatenate([_pick_head(x, 0), _pick_head(x, 1)], axis=0)


def _fox_specs(l):
    pair = pl.BlockSpec((None, l, PAIR_W), lambda bi, hp: (bi, 0, hp))
    half = pl.BlockSpec((None, l, 128), lambda bi, hp: (bi, 0, hp))
    colv = pl.BlockSpec((None, 2, l, 1), lambda bi, hp: (bi, hp, 0, 0))
    rowv = pl.BlockSpec((None, 2, 1, l), lambda bi, hp: (bi, hp, 0, 0))
    return pair, half, colv, rowv


def _fox_fwd(name, qkv, c_col, c_row):
    b, l, _ = qkv.shape
    nb = l // BLOCK

    def body(x_ref, cc_ref, cr_ref, o_ref, lse_ref):
        for i in range(nb):
            rows = slice(i * BLOCK, (i + 1) * BLOCK)
            kh = (i + 1) * BLOCK
            qblk = x_ref[rows, 0:128]
            kv = x_ref[0:kh, 128:256]
            vv = x_ref[0:kh, 256:384]
            qk = lax.dot_general(_both_heads(qblk) * SCALE, kv, _NT, preferred_element_type=F32)
            ps, dens = [], []
            for hh in range(2):
                s = _fox_mask(qk[hh * BLOCK:(hh + 1) * BLOCK], cr_ref[hh, :, 0:kh], i)
                m = jnp.max(s, axis=-1, keepdims=True)
                p = jnp.exp(s - m)
                den = jnp.sum(p, axis=-1, keepdims=True)
                ps.append(p.astype(BF16))
                dens.append(den)
                lse_ref[hh, rows, :] = (m + jnp.log(den)) + cc_ref[hh, rows, :]
            pv = lax.dot_general(jnp.concatenate(ps, axis=0), vv, _NN, preferred_element_type=F32)
            o_ref[rows, :] = jnp.where(_lane_lo(), pv[0:BLOCK] / dens[0], pv[BLOCK:2 * BLOCK] / dens[1]).astype(BF16)

    pair, half, colv, rowv = _fox_specs(l)
    return pl.pallas_call(
        body, name=name, grid=(b, 4), in_specs=[pair, colv, rowv], out_specs=[half, colv],
        out_shape=[jax.ShapeDtypeStruct((b, l, B_WIDTH), BF16), jax.ShapeDtypeStruct((b, B_HEADS, l, 1), F32)],
        compiler_params=_params(("parallel", "parallel")),
    )(qkv, c_col, c_row)


def _fox_bwd(name, qkv, c_col, c_row, o, lse, do):
    b, l, _ = qkv.shape
    nb = l // BLOCK

    def body(x_ref, cc_ref, cr_ref, o_ref, lse_ref, do_ref, dx_ref, dcq_ref, dck_ref, dk_acc, dv_acc):
        dk_acc[...] = jnp.zeros_like(dk_acc)
        dv_acc[...] = jnp.zeros_like(dv_acc)
        dck_ref[...] = jnp.zeros_like(dck_ref)
        for i in range(nb):
            rows = slice(i * BLOCK, (i + 1) * BLOCK)
            kh = (i + 1) * BLOCK
            qblk = x_ref[rows, 0:128]
            kv = x_ref[0:kh, 128:256]
            vv = x_ref[0:kh, 256:384]
            doblk = do_ref[rows, :]
            ov = o_ref[rows, :].astype(F32)
            q2 = _both_heads(qblk) * SCALE
            do2 = _both_heads(doblk)
            qk = lax.dot_general(q2, kv, _NT, preferred_element_type=F32)
            dp = lax.dot_general(do2, vv, _NT, preferred_element_type=F32)
            ps, dss = [], []
            for hh in range(2):
                half = slice(hh * BLOCK, (hh + 1) * BLOCK)
                s = _fox_mask(qk[half], cr_ref[hh, :, 0:kh], i)
                p = jnp.exp(s - (lse_ref[hh, rows, :] - cc_ref[hh, rows, :]))
                dsum = jnp.sum(do2[half].astype(F32) * ov, axis=-1, keepdims=True)
                ds = p * (dp[half] - dsum)
                ps.append(p.astype(BF16))
                dss.append(ds.astype(BF16))
                dcq_ref[hh, rows, :] = jnp.sum(ds, axis=-1, keepdims=True)
                dck_ref[hh, :, 0:kh] -= jnp.sum(ds, axis=0, keepdims=True)
            p2 = jnp.concatenate(ps, axis=0)
            ds2 = jnp.concatenate(dss, axis=0)
            dq = lax.dot_general(ds2, kv, _NN, preferred_element_type=F32) * SCALE
            dk_acc[0:kh, :] += lax.dot_general(ds2, q2, _TN, preferred_element_type=F32)
            dv_acc[0:kh, :] += lax.dot_general(p2, do2, _TN, preferred_element_type=F32)
            dx_ref[rows, 0:128] = jnp.where(_lane_lo(), dq[0:BLOCK], dq[BLOCK:2 * BLOCK]).astype(BF16)
        dx_ref[:, 128:256] = dk_acc[...].astype(BF16)
        dx_ref[:, 256:384] = dv_acc[...].astype(BF16)

    pair, half, colv, rowv = _fox_specs(l)
    return pl.pallas_call(
        body, name=name, grid=(b, 4), in_specs=[pair, colv, rowv, half, colv, half],
        out_specs=[pair, colv, rowv],
        out_shape=[jax.ShapeDtypeStruct(qkv.shape, BF16), jax.ShapeDtypeStruct((b, B_HEADS, l, 1), F32),
                   jax.ShapeDtypeStruct((b, B_HEADS, 1, l), F32)],
        scratch_shapes=[pltpu.VMEM((l, 128), F32), pltpu.VMEM((l, 128), F32)],
        compiler_params=_params(("parallel", "parallel")),
    )(qkv, c_col, c_row, o, lse, do)


_FLIPS = ((0, 0, 1), (0, 1, 0), (0, 1, 1), (1, 0, 0), (1, 0, 1), (1, 1, 0), (1, 1, 1))


def _exchange(name, gather, scatter):
    ng, ns = len(gather), len(scatter)
    na = ng + ns
    npeer = len(_FLIPS)

    def body(*refs):
        ins = refs[:na]
        outs = refs[na:2 * na]
        send_sems, recv_sems, loc_sems = refs[2 * na:]
        x, y, c = lax.axis_index("x"), lax.axis_index("y"), lax.axis_index("c")
        me = 4 * x + 2 * y + c
        peers = []
        for fx, fy, fc in _FLIPS:
            px = 1 - x if fx else x
            py = 1 - y if fy else y
            pc = 1 - c if fc else c
            peers.append(((px, py, pc), 4 * px + 2 * py + pc))

        def remote(a, kk):
            dev, lin = peers[kk]
            src = ins[a] if a < ng else ins[a].at[lin]
            return pltpu.make_async_remote_copy(src_ref=src, dst_ref=outs[a].at[me], send_sem=send_sems.at[a * npeer + kk],
                                                recv_sem=recv_sems.at[a * npeer + kk], device_id=dev, device_id_type=MESH_ID)

        def arrival(a, kk):
            dev, lin = peers[kk]
            src = ins[a] if a < ng else ins[a].at[lin]
            return pltpu.make_async_remote_copy(src_ref=src, dst_ref=outs[a].at[lin], send_sem=send_sems.at[a * npeer + kk],
                                                recv_sem=recv_sems.at[a * npeer + kk], device_id=dev, device_id_type=MESH_ID)

        local = []
        for a in range(na):
            src = ins[a] if a < ng else ins[a].at[me]
            cp = pltpu.make_async_copy(src, outs[a].at[me], loc_sems.at[a])
            cp.start()
            local.append(cp)
        sent = [remote(a, kk) for kk in range(npeer) for a in range(na)]
        for cp in sent:
            cp.start()
        for kk in range(npeer):
            for a in range(na):
                arrival(a, kk).wait_recv()
        for cp in sent:
            cp.wait_send()
        for cp in local:
            cp.wait()

    arrs = list(gather) + list(scatter)
    out_shape = [jax.ShapeDtypeStruct((N_DEV,) + tuple(a.shape), a.dtype) for a in gather]
    out_shape += [jax.ShapeDtypeStruct(tuple(a.shape), a.dtype) for a in scatter]
    anyspec = pl.BlockSpec(memory_space=pl.ANY)
    return pl.pallas_call(
        body, name=name, in_specs=[anyspec] * na, out_specs=[anyspec] * na, out_shape=out_shape,
        scratch_shapes=[pltpu.SemaphoreType.DMA((na * npeer,)), pltpu.SemaphoreType.DMA((na * npeer,)),
                        pltpu.SemaphoreType.DMA((na,))],
        compiler_params=pltpu.CompilerParams(has_side_effects=True),
    )(*arrs)


def _peer_table():
    x, y, c = lax.axis_index("x"), lax.axis_index("y"), lax.axis_index("c")
    me = 4 * x + 2 * y + c
    peers = []
    for fx, fy, fc in _FLIPS:
        px = 1 - x if fx else x
        py = 1 - y if fy else y
        pc = 1 - c if fc else c
        peers.append(((px, py, pc), 4 * px + 2 * py + pc))
    return me, peers


_HBM = pl.BlockSpec(memory_space=pltpu.HBM)
_SEM = pl.BlockSpec(memory_space=pltpu.SEMAPHORE)
_ANY = pl.BlockSpec(memory_space=pl.ANY)
_EFFECT = pltpu.SideEffectType.DATAFLOW_SIDE_EFFECTING


def _split_copy(srcs_are_pieces, src_refs, land_refs, send_sem, recv_sem, a, kk, me, peers, arriving):
    dev, lin = peers[kk]
    npeer = len(_FLIPS)
    src = src_refs[a] if srcs_are_pieces[a] else src_refs[a].at[lin]
    dst = land_refs[a].at[lin] if arriving else land_refs[a].at[me]
    return pltpu.make_async_remote_copy(src_ref=src, dst_ref=dst, send_sem=send_sem.at[a * npeer + kk],
                                        recv_sem=recv_sem.at[a * npeer + kk], device_id=dev, device_id_type=MESH_ID)


def _xchg_start(name, gather, scatter, after=None):
    me_out = 4 * lax.axis_index("x") + 2 * lax.axis_index("y") + lax.axis_index("c")
    srcs = list(gather) + list(scatter)
    is_piece = [True] * len(gather) + [False] * len(scatter)
    lands = []
    for a, piece in zip(srcs, is_piece):
        own = a[None] if piece else lax.dynamic_slice_in_dim(a, me_out, 1, axis=0)
        shape = ((N_DEV,) + tuple(a.shape)) if piece else tuple(a.shape)
        start = (me_out,) + (0,) * (len(shape) - 1)
        lands.append(lax.dynamic_update_slice(lax.empty(shape, a.dtype), own, start))
    n = len(srcs)
    nsem = n * len(_FLIPS)
    has_after = after is not None

    def body(*refs):
        src_refs = refs[:n]
        land_refs = refs[n:2 * n]
        outs = refs[2 * n + (1 if has_after else 0):]
        send_sem, recv_sem = outs[0], outs[1]
        token = outs[-1]
        me, peers = _peer_table()
        for kk in range(len(_FLIPS)):
            for a in range(n):
                _split_copy(is_piece, src_refs, land_refs, send_sem, recv_sem, a, kk, me, peers, False).start()
        token[...] = jnp.zeros_like(token)

    out_shape = ([pltpu.SemaphoreType.DMA((nsem,)), pltpu.SemaphoreType.DMA((nsem,))]
                 + [pltpu.HBM(tuple(a.shape), a.dtype) for a in srcs] + [pltpu.HBM(tuple(a.shape), a.dtype) for a in lands]
                 + [jax.ShapeDtypeStruct((8, 128), F32)])
    args = [pltpu.with_memory_space_constraint(a, pltpu.HBM) for a in srcs + lands] + ([after] if has_after else [])
    res = pl.pallas_call(
        body, name=name, out_shape=out_shape,
        in_specs=[_HBM] * (2 * n) + ([_ANY] if has_after else []),
        out_specs=[_SEM, _SEM] + [_HBM] * (2 * n) + [pl.BlockSpec(memory_space=pltpu.VMEM)],
        input_output_aliases={i: 2 + i for i in range(2 * n)},
        compiler_params=pltpu.CompilerParams(has_side_effects=_EFFECT),
    )(*args)
    state = (res[0], res[1], list(res[2:2 + n]), list(res[2 + n:2 + 2 * n]), is_piece)
    return state, res[-1]


def _xchg_wait(name, state, after):
    send_sem, recv_sem, srcs, lands, is_piece = state
    n = len(srcs)

    def body(*refs):
        src_refs = refs[:n]
        land_refs = refs[n:2 * n]
        s_sem, r_sem = refs[2 * n], refs[2 * n + 1]
        me, peers = _peer_table()
        for kk in range(len(_FLIPS)):
            for a in range(n):
                cp = _split_copy(is_piece, src_refs, land_refs, s_sem, r_sem, a, kk, me, peers, True)
                cp.wait_send()
                cp.wait_recv()

    out_shape = [pltpu.HBM(tuple(a.shape), a.dtype) for a in srcs] + [pltpu.HBM(tuple(a.shape), a.dtype) for a in lands]
    res = pl.pallas_call(
        body, name=name, out_shape=out_shape,
        in_specs=[_HBM] * (2 * n) + [_SEM, _SEM, _ANY], out_specs=[_HBM] * (2 * n),
        input_output_aliases={i: i for i in range(2 * n)},
        compiler_params=pltpu.CompilerParams(has_side_effects=_EFFECT),
    )(*srcs, *lands, send_sem, recv_sem, after)
    return list(res[n:])


_SIB = (0, 0, 1)
_ICI = ((0, 1, 0), (1, 0, 0), (1, 1, 0))


def _flip(fl):
    x, y, c = lax.axis_index("x"), lax.axis_index("y"), lax.axis_index("c")
    px = 1 - x if fl[0] else x
    py = 1 - y if fl[1] else y
    pc = 1 - c if fl[2] else c
    return (px, py, pc), 4 * px + 2 * py + pc


def _gather2_start(name, pieces, after=None):
    me_out = 4 * lax.axis_index("x") + 2 * lax.axis_index("y") + lax.axis_index("c")
    pieces = list(pieces)
    n = len(pieces)
    lands = [lax.dynamic_update_slice(lax.empty((N_DEV,) + tuple(a.shape), a.dtype), a[None],
                                      (me_out,) + (0,) * a.ndim) for a in pieces]
    first = (_SIB,) + _ICI
    has_after = after is not None

    def body(*refs):
        src_refs, land_refs = refs[:n], refs[n:2 * n]
        outs = refs[2 * n + (1 if has_after else 0):]
        send_sem, recv_sem, token = outs[0], outs[1], outs[-1]
        _, me = _flip((0, 0, 0))
        for kk, fl in enumerate(first):
            dev, _ = _flip(fl)
            for a in range(n):
                pltpu.make_async_remote_copy(src_ref=src_refs[a], dst_ref=land_refs[a].at[me],
                                             send_sem=send_sem.at[a * 4 + kk], recv_sem=recv_sem.at[a * 4 + kk],
                                             device_id=dev, device_id_type=MESH_ID).start()
        token[...] = jnp.zeros_like(token)

    hbm = [pltpu.HBM(tuple(a.shape), a.dtype) for a in pieces + lands]
    res = pl.pallas_call(
        body, name=name,
        out_shape=[pltpu.SemaphoreType.DMA((4 * n,)), pltpu.SemaphoreType.DMA((4 * n,))] + hbm
        + [jax.ShapeDtypeStruct((8, 128), F32)],
        in_specs=[_HBM] * (2 * n) + ([_ANY] if has_after else []),
        out_specs=[_SEM, _SEM] + [_HBM] * (2 * n) + [pl.BlockSpec(memory_space=pltpu.VMEM)],
        input_output_aliases={i: 2 + i for i in range(2 * n)},
        compiler_params=pltpu.CompilerParams(has_side_effects=_EFFECT),
    )(*([pltpu.with_memory_space_constraint(a, pltpu.HBM) for a in pieces + lands] + ([after] if has_after else [])))
    return (res[0], res[1], list(res[2:2 + n]), list(res[2 + n:2 + 2 * n])), res[-1]


def _gather2_forward(name, state, after):
    send_a, recv_a, pieces, lands = state
    n = len(pieces)
    first = (_SIB,) + _ICI

    def body(*refs):
        src_refs, land_refs = refs[:n], refs[n:2 * n]
        s_a, r_a = refs[2 * n], refs[2 * n + 1]
        outs = refs[2 * n + 3:]
        send_b, recv_b, token = outs[0], outs[1], outs[-1]
        for kk, fl in enumerate(first):
            dev, lin = _flip(fl)
            for a in range(n):
                cp = pltpu.make_async_remote_copy(src_ref=src_refs[a], dst_ref=land_refs[a].at[lin],
                                                  send_sem=s_a.at[a * 4 + kk], recv_sem=r_a.at[a * 4 + kk],
                                                  device_id=dev, device_id_type=MESH_ID)
                cp.wait_send()
                cp.wait_recv()
        sib, _ = _flip(_SIB)
        for j, fl in enumerate(_ICI):
            _, lin = _flip(fl)
            for a in range(n):
                pltpu.make_async_remote_copy(src_ref=land_refs[a].at[lin], dst_ref=land_refs[a].at[lin],
                                             send_sem=send_b.at[a * 3 + j], recv_sem=recv_b.at[a * 3 + j],
                                             device_id=sib, device_id_type=MESH_ID).start()
        token[...] = jnp.zeros_like(token)

    hbm = [pltpu.HBM(tuple(a.shape), a.dtype) for a in pieces + lands]
    res = pl.pallas_call(
        body, name=name,
        out_shape=[pltpu.SemaphoreType.DMA((3 * n,)), pltpu.SemaphoreType.DMA((3 * n,))] + hbm
        + [jax.ShapeDtypeStruct((8, 128), F32)],
        in_specs=[_HBM] * (2 * n) + [_SEM, _SEM, _ANY],
        out_specs=[_SEM, _SEM] + [_HBM] * (2 * n) + [pl.BlockSpec(memory_space=pltpu.VMEM)],
        input_output_aliases={i: 2 + i for i in range(2 * n)},
        compiler_params=pltpu.CompilerParams(has_side_effects=_EFFECT),
    )(*pieces, *lands, send_a, recv_a, after)
    return (res[0], res[1], list(res[2 + n:2 + 2 * n])), res[-1]


def _gather2_wait(name, state, after):
    send_b, recv_b, lands = state
    n = len(lands)

    def body(*refs):
        land_refs = refs[:n]
        s_b, r_b = refs[n], refs[n + 1]
        sib, _ = _flip(_SIB)
        for j, fl in enumerate(_ICI):
            _, sent = _flip(fl)
            _, arriving = _flip((fl[0], fl[1], 1))
            for a in range(n):
                cp = pltpu.make_async_remote_copy(src_ref=land_refs[a].at[sent], dst_ref=land_refs[a].at[arriving],
                                                  send_sem=s_b.at[a * 3 + j], recv_sem=r_b.at[a * 3 + j],
                                                  device_id=sib, device_id_type=MESH_ID)
                cp.wait_send()
                cp.wait_recv()

    res = pl.pallas_call(
        body, name=name, out_shape=[pltpu.HBM(tuple(a.shape), a.dtype) for a in lands],
        in_specs=[_HBM] * n + [_SEM, _SEM, _ANY], out_specs=[_HBM] * n,
        input_output_aliases={i: i for i in range(n)},
        compiler_params=pltpu.CompilerParams(has_side_effects=_EFFECT),
    )(*lands, send_b, recv_b, after)
    return list(res)


def _adam_math(w, g, m, v):
    m = ADAM_B1 * m + (1.0 - ADAM_B1) * g
    v = ADAM_B2 * v + (1.0 - ADAM_B2) * (g * g)
    m_hat = m / (1.0 - ADAM_B1 ** ADAM_STEP)
    v_hat = v / (1.0 - ADAM_B2 ** ADAM_STEP)
    delta = -ADAM_LR * (m_hat / (jnp.sqrt(v_hat) + ADAM_EPS) + ADAM_WD * w)
    return delta, m, v


def _adam(name, w, m, v, parts):
    r, c = w.shape
    npart, _, cp = parts.shape
    tr = _pick(r, (256, 176, 128, 64, 16, 8, 1))

    def body(w_ref, m_ref, v_ref, p_ref, g_ref, d_ref, mo_ref, vo_ref):
        g = p_ref[0].astype(F32)
        for pp in range(1, npart):
            g = g + p_ref[pp].astype(F32)
        g = g[:, 0:c]
        delta, mn, vn = _adam_math(w_ref[...], g, m_ref[...], v_ref[...])
        g_ref[...] = g
        d_ref[...] = delta
        mo_ref[...] = mn
        vo_ref[...] = vn

    blk = pl.BlockSpec((tr, c), lambda i: (i, 0))
    out = jax.ShapeDtypeStruct((r, c), F32)
    return pl.pallas_call(
        body, name=name, grid=(r // tr,),
        in_specs=[blk, blk, blk, pl.BlockSpec((npart, tr, cp), lambda i: (0, i, 0))],
        out_specs=[blk, blk, blk, blk], out_shape=[out, out, out, out], compiler_params=_params(("parallel",)),
    )(w, m, v, parts)


def _adam_t(name, w_t, m_t, v_t, parts):
    c, r = w_t.shape
    npart, _, cp = parts.shape
    tr = _pick(r, (256, 128))

    def body(w_ref, m_ref, v_ref, p_ref, g_ref, d_ref, mo_ref, vo_ref):
        g = p_ref[0].astype(F32)
        for pp in range(1, npart):
            g = g + p_ref[pp].astype(F32)
        g = g.T[0:c, :]
        delta, mn, vn = _adam_math(w_ref[...], g, m_ref[...], v_ref[...])
        g_ref[...] = g
        d_ref[...] = delta
        mo_ref[...] = mn
        vo_ref[...] = vn

    blk = pl.BlockSpec((c, tr), lambda i: (0, i))
    out = jax.ShapeDtypeStruct((c, r), F32)
    return pl.pallas_call(
        body, name=name, grid=(r // tr,),
        in_specs=[blk, blk, blk, pl.BlockSpec((npart, tr, cp), lambda i: (0, i, 0))],
        out_specs=[blk, blk, blk, blk], out_shape=[out, out, out, out], compiler_params=_params(("parallel",)),
    )(w_t, m_t, v_t, parts)


def _small_sum(name, packs):
    def body(p_ref, o_ref):
        tot = p_ref[0]
        for pp in range(1, N_DEV):
            tot = tot + p_ref[pp]
        o_ref[0:8, :] = tot[0:8, :]
        o_ref[8:24, :] = tot[8:24, :] + tot[24:40, :]

    return pl.pallas_call(body, name=name, out_shape=jax.ShapeDtypeStruct((24, D_MODEL), F32),
                          compiler_params=_params())(packs)


def _local_step(x, tgt, g1, gm, g2, gf, b_forget, sinks, weights, send):
    b, s, _ = x.shape
    l = s + PREFIX
    t = b * l
    (meta,) = weights("meta", x)
    h0 = jnp.concatenate([jnp.zeros((b, N_PAD, D_MODEL), F32), jnp.broadcast_to(meta[None], (b, N_META, D_MODEL)), x],
                         axis=1).reshape(t, D_MODEL)

    n1 = _rms_fwd("rms1_fwd", h0, g1)
    (w1i,) = weights("ffn1_in", n1)
    gu1, a1 = _ffn_in_fwd("ffn1_in_fwd", n1, w1i, dep=weights("ffn1_out:forward", n1))
    (w1o,) = weights("ffn1_out", weights("mix:forward", a1))
    h1 = _mm_nn("ffn1_out_fwd", a1, w1o, alpha=0.5, res=h0, tm_c=(544, 384, 256, 128), tn_c=(1024,))
    wi, wa, wb, wo = weights("mix", h1)
    um = _rms_fwd("rmsm_fwd", h1, gm)
    qkv = _mm_nn("proj_qkv_fwd", um, wi, out_dtype=BF16, tn_c=(768,), cols=(0, QKV_W))
    gates = _mm_nn("proj_gates_fwd", um, wi, tn_c=(1024,), cols=(P_GATES, 2 * D_MODEL), dep=weights("ffn2:forward", qkv))
    f2 = _mm_nn("proj_f_fwd", um, wi, tn_c=(128,), cols=(P_F, 128))
    qkv3 = qkv.reshape(b, l, QKV_W)
    f3 = f2.reshape(b, l, 128)
    bf_row = jnp.pad(b_forget, ((0, 0), (0, 128 - B_HEADS)))
    c_col, c_row = _fgate_fwd("fgate_fwd", f3, bf_row)
    head_of_row = jnp.arange(STACK) // BLOCK
    slopes = jnp.exp2(-8.0 * (head_of_row + 1).astype(F32) / A_HEADS).reshape(STACK, 1)
    sink_rows = jnp.repeat(sinks.reshape(A_HEADS), BLOCK).reshape(STACK, 1)
    swa_bias = _swa_bias(slopes)
    oa3, lse_a = _swa_fwd("swa_fwd", qkv3, slopes, sink_rows, swa_bias)
    ob3, lse_b = _fox_fwd("fox_fwd", qkv3, c_col, c_row)
    oa = oa3.reshape(t, A_WIDTH)
    ob = ob3.reshape(t, B_WIDTH)
    mixed, ya, yb = _branch_gate_fwd("branch_gate_fwd", oa, ob, wa, wb, gates)
    h2 = _mm_nn("mix_out_fwd", mixed, wo, res=h1, tn_c=(1024,))
    w2i, w2o = weights("ffn2", h2)
    n2 = _rms_fwd("rms2_fwd", h2, g2)
    gu2, a2 = _ffn_in_fwd("ffn2_in_fwd", n2, w2i)
    h3 = _mm_nn("ffn2_out_fwd", a2, w2o, alpha=0.5, res=h2, tm_c=(544, 384, 256, 128), tn_c=(1024,))

    dh3_3, loss_blk, dgf = _loss_head("loss_head", h3.reshape(b, l, D_MODEL), gf, tgt)
    dh3 = dh3_3.reshape(t, D_MODEL)

    def ffn_bwd(tag, dh, h_in, g_norm, n_in, gu, a, w_in_blk, w_out):
        dw_out = _mm_tn(tag + "_out_bwd_w", a, dh, alpha=0.5)
        dgu = _ffn_out_bwd_x(tag + "_out_bwd_x", dh, w_out, gu, dep=send(tag + "_out", (dw_out,)))
        dw_in = _ffn_in_bwd_w(tag + "_in_bwd_w", n_in, dgu)
        return _ffn_in_bwd_x(tag + "_in_bwd_x", dgu, w_in_blk, h_in, g_norm, dh, dep=send(tag + "_in", (dw_in,)))

    dh2, dg2 = ffn_bwd("ffn2", dh3, h2, g2, n2, gu2, a2, w2i, w2o)

    dwo = _mm_tn("mix_out_bwd_w", mixed, dh2)
    dya, dyb, dgates = _mix_out_gate_bwd("mix_out_gate_bwd", dh2, wo, gates, ya, yb)
    doa = _mm_nt("branch_a_bwd_x", dya, wa, tn_c=(512,))
    dob = _mm_nt("branch_b_bwd_x", dyb, wb, out_dtype=BF16, tn_c=(512,))
    dwa = _mm_tn("branch_a_bwd_w", oa, dya, tm_c=(512,))
    dwb = _mm_tn("branch_b_bwd_w", ob, dyb, tm_c=(512,))
    dqkv3, dcq, dck = _fox_bwd("fox_bwd", qkv3, c_col, c_row, ob3, lse_b, dob.reshape(b, l, B_WIDTH))
    dqkv3, dsink = _swa_bwd("swa_bwd", qkv3, oa3, lse_a, doa.reshape(b, l, A_WIDTH), slopes, sink_rows, swa_bias, dqkv3)
    dqkv = dqkv3.reshape(t, QKV_W)
    df3, dbf = _fgate_bwd("fgate_bwd", f3, bf_row, dcq, dck)
    df = df3.reshape(t, 128)
    dwi_qkv = _mm_tn("proj_qkv_bwd_w", um, dqkv, tm_c=(512,), tn_c=(768,))
    dwi_g = _mm_tn("proj_gates_bwd_w", um, dgates, tm_c=(512,), tn_c=(512,))
    dwi_f = _mm_tn("proj_f_bwd_w", um, df, tm_c=(512,), tn_c=(128,))
    token = send("mix", (dwi_qkv, dwi_g, dwi_f, dwa, dwb, dwo))
    dh1, dgm = _proj_bwd_x("proj_bwd_x", dqkv, dgates, df, wi, h1, gm, dh2, dep=token)

    dh0, dg1 = ffn_bwd("ffn1", dh1, h0, g1, n1, gu1, a1, w1i, w1o)
    dh0_3 = dh0.reshape(b, l, D_MODEL)
    grad_x = dh0_3[:, PREFIX:, :]
    dmeta = dh0_3[:, N_PAD:PREFIX, :].reshape(b * N_META, D_MODEL)

    misc = jnp.concatenate([dbf[:, 0:B_HEADS], dsink[:, 0].reshape(1, A_HEADS), loss_blk[0:1, 0:1]], axis=1)
    misc = jnp.pad(misc, ((0, 0), (0, D_MODEL - misc.shape[1])))
    row = lax.broadcasted_iota(jnp.int32, (8, D_MODEL), 0)
    vec = jnp.zeros((8, D_MODEL), F32)
    for i, piece in enumerate((dg1, dgm, dg2, dgf, misc)):
        vec = jnp.where(row == i, piece, vec)
    small = jnp.concatenate([vec, dmeta], axis=0)
    return grad_x, small


def _pad_to(a, rows, cols):
    return jnp.pad(a, ((0, rows - a.shape[0]), (0, cols - a.shape[1])))


def _ffn_out_from_gathered(g):
    w = g.reshape(4, FF_SHARD, D_MODEL)
    return jnp.pad(w, ((0, 0), (0, FF_SHARD_P - FF_SHARD), (0, 0))).reshape(D_FF_P, D_MODEL)


def _ffn_out_to_scatter(dw):
    return dw.reshape(4, FF_SHARD_P, D_MODEL)[:, 0:FF_SHARD, :].reshape(N_DEV, FFO_SHARD, D_MODEL)


def _proj_segments():
    segs = [(HEAD_DIM * h, HEAD_DIM, 0, B_SEG + HEAD_DIM * A_HEAD_ORDER.index(h)) for h in range(A_HEADS)]
    segs += [(512, 128, 0, B_SEG + A_WIDTH), (640, 128, 0, B_SEG + A_WIDTH + 128)]
    for first, off in ((768, 0), (1280, 128), (1792, 256)):
        segs += [(first + 128 * hp, 128, 0, PAIR_W * hp + off) for hp in range(4)]
    segs += [(2304, B_HEADS, 2, 0), (2312, 2 * D_MODEL, 1, 0)]
    return segs


def _proj_from_gathered(g):
    def cols(first, width):
        out = []
        for p in range(N_DEV):
            lo, hi = max(first, WIN_SHARD * p), min(first + width, WIN_SHARD * (p + 1))
            if lo < hi:
                out.append(g[p, :, lo - WIN_SHARD * p:hi - WIN_SHARD * p])
        return out

    parts = []
    for arr in (0, 1, 2):
        for first, width, _, _ in sorted((s for s in _proj_segments() if s[2] == arr), key=lambda s: s[3]):
            parts += cols(first, width)
        if arr == 0:
            parts.append(jnp.zeros((D_MODEL, P_GATES - QKV_W), g.dtype))
    parts.append(jnp.zeros((D_MODEL, 128 - B_HEADS), g.dtype))
    return jnp.concatenate(parts, axis=1)


def _proj_to_scatter(dqkv_w, dg_w, df_w):
    arrays = (dqkv_w, dg_w, df_w)
    segs = sorted(_proj_segments())
    blocks = []
    for p in range(N_DEV):
        parts = []
        for first, width, arr, at in segs:
            lo, hi = max(first, WIN_SHARD * p), min(first + width, WIN_SHARD * (p + 1))
            if lo < hi:
                parts.append(arrays[arr][:, at + lo - first:at + hi - first])
        parts.append(jnp.zeros((D_MODEL, WIN_SHARD_P - WIN_SHARD), dqkv_w.dtype))
        blocks.append(jnp.concatenate(parts, axis=1))
    return jnp.stack(blocks, axis=0)


def _a_rows_from_natural(w):
    return jnp.concatenate([w[HEAD_DIM * h:HEAD_DIM * (h + 1)] for h in A_HEAD_ORDER], axis=0)


def _a_rows_to_natural(w):
    return jnp.concatenate([w[HEAD_DIM * A_HEAD_ORDER.index(h):HEAD_DIM * (A_HEAD_ORDER.index(h) + 1)]
                            for h in range(A_HEADS)], axis=0)


def kernel(x, meta_tokens, ffn1_norm, ffn1_w_in, ffn1_w_out, mix_norm, w_in, b_forget, attn_sinks, w_branch_a, w_branch_b, w_out, ffn2_norm, ffn2_w_in, ffn2_w_out, final_norm, loss_target, m_meta_tokens, m_ffn1_norm, m_ffn1_w_in, m_ffn1_w_out, m_mix_norm, m_w_in, m_b_forget, m_attn_sinks, m_w_branch_a, m_w_branch_b, m_w_out, m_ffn2_norm, m_ffn2_w_in, m_ffn2_w_out, m_final_norm, v_meta_tokens, v_ffn1_norm, v_ffn1_w_in, v_ffn1_w_out, v_mix_norm, v_w_in, v_b_forget, v_attn_sinks, v_w_branch_a, v_w_branch_b, v_w_out, v_ffn2_norm, v_ffn2_w_in, v_ffn2_w_out, v_final_norm):
    me = 4 * lax.axis_index("x") + 2 * lax.axis_index("y") + lax.axis_index("c")

    shards = (
        _pad_to(ffn1_w_in[0].astype(BF16), D_MODEL, FF_SHARD_P),
        ffn1_w_out[0].astype(BF16),
        _pad_to(w_in[0].astype(BF16), D_MODEL, WIN_SHARD_P),
        w_branch_a[0].astype(BF16), w_branch_b[0].astype(BF16), w_out[0].astype(BF16),
        _pad_to(ffn2_w_in[0].astype(BF16), D_MODEL, FF_SHARD_P),
        ffn2_w_out[0].astype(BF16),
        meta_tokens,
    )
    s1i, s1o, swi, swa, swb, swo, s2i, s2o, smeta = shards
    first_level, second_level = {}, {}
    first_level["meta"], tok = _gather2_start("gather_meta_start", (smeta,))
    first_level["ffn1_in"], tok = _gather2_start("gather_ffn1_in_start", (s1i,), after=tok)
    first_level["ffn1_out"], tok = _gather2_start("gather_ffn1_out_start", (s1o,), after=tok)
    first_level["mix"], tok = _gather2_start("gather_mix_start", (swi, swa, swb, swo), after=tok)
    first_level["ffn2"], tok = _gather2_start("gather_ffn2_start", (s2i, s2o), after=tok)
    started = {"tok": tok}

    def weights(group, after):
        if group.endswith(":forward"):
            group = group[:-len(":forward")]
            second_level[group], token = _gather2_forward("gather_" + group + "_forward", first_level[group], after)
            return token
        if group == "meta":
            after = weights("meta:forward", started["tok"])
        if group == "ffn1_in":
            after = weights("ffn1_in:forward", after)
        got = _gather2_wait("gather_" + group + "_wait", second_level[group], after)
        if group == "mix":
            gwi, gwa, gwb, gwo = got
            return (_proj_from_gathered(gwi), _a_rows_from_natural(gwa.transpose(1, 0, 2).reshape(A_WIDTH, D_MODEL)),
                    gwb.transpose(1, 0, 2).reshape(B_WIDTH, D_MODEL), gwo.reshape(D_MODEL, D_MODEL))
        if group == "meta":
            return (got[0].transpose(1, 0, 2).reshape(N_META, D_MODEL),)
        if group == "ffn1_in":
            return (got[0].reshape(2, 4, D_MODEL, FF_SHARD_P),)
        if group == "ffn1_out":
            return (_ffn_out_from_gathered(got[0]),)
        return got[0].reshape(2, 4, D_MODEL, FF_SHARD_P), _ffn_out_from_gathered(got[1])

    scatter_state = {}

    def send(group, grads):
        if group == "mix":
            dwi_qkv, dwi_g, dwi_f, dwa, dwb, dwo = grads
            dwa = _a_rows_to_natural(dwa)
            blocks = (_proj_to_scatter(dwi_qkv, dwi_g, dwi_f), dwa.reshape(A_WIDTH, N_DEV, 128).transpose(1, 0, 2),
                      dwb.reshape(B_WIDTH, N_DEV, 128).transpose(1, 0, 2), dwo.reshape(N_DEV, 128, D_MODEL))
        elif group.endswith("_in"):
            blocks = (grads[0].reshape(N_DEV, D_MODEL, FF_SHARD_P),)
        else:
            blocks = (_ffn_out_to_scatter(grads[0]),)
        scatter_state[group], token = _xchg_start("scatter_" + group + "_start", (), blocks)
        return token

    gf = final_norm.reshape(1, D_MODEL)
    grad_x, small = _local_step(x, loss_target, ffn1_norm, mix_norm, ffn2_norm, gf, b_forget, attn_sinks, weights, send)

    small_state, after = _xchg_start("gather_small_start", (small,), ())
    out = {}
    updates = (
        ("ffn2_out", (("ffn2_w_out", ffn2_w_out, m_ffn2_w_out, v_ffn2_w_out),)),
        ("ffn2_in", (("ffn2_w_in", ffn2_w_in, m_ffn2_w_in, v_ffn2_w_in),)),
        ("ffn1_out", (("ffn1_w_out", ffn1_w_out, m_ffn1_w_out, v_ffn1_w_out),)),
        ("mix", (("w_in", w_in, m_w_in, v_w_in), ("w_branch_a", w_branch_a, m_w_branch_a, v_w_branch_a),
                 ("w_branch_b", w_branch_b, m_w_branch_b, v_w_branch_b), ("w_out", w_out, m_w_out, v_w_out))),
    )
    last_update = ("ffn1_in", (("ffn1_w_in", ffn1_w_in, m_ffn1_w_in, v_ffn1_w_in),))

    def update(group, members, after):
        parts_list = _xchg_wait("scatter_" + group + "_wait", scatter_state[group], after)
        for (nm, w, m, v), parts in zip(members, parts_list):
            if nm.endswith("w_in"):
                res4 = _adam_t("adam_" + nm, w[0].T, m[0].T, v[0].T, parts)
                out[nm] = tuple(r.T[None] for r in res4)
            else:
                res4 = _adam("adam_" + nm, w[0], m[0], v[0], parts)
                out[nm] = tuple(r[None] for r in res4)
            after = res4[0]
        return after

    for group, members in updates:
        after = update(group, members, after)
    (packs,) = _xchg_wait("gather_small_wait", small_state, after)

    tot = _small_sum("small_sum", packs)
    loss = tot[4, 2 * B_HEADS]
    g_meta = lax.dynamic_slice(tot[8:24, :], (0, me * 128), (N_META, 128))
    out["meta_tokens"] = tuple(_adam("adam_meta_tokens", meta_tokens, m_meta_tokens, v_meta_tokens, g_meta[None]))

    def pack_small(n1, nm, n2, nf, bfv, skv):
        misc = jnp.pad(jnp.concatenate([bfv, skv], axis=1), ((0, 0), (0, D_MODEL - 2 * B_HEADS)))
        row = lax.broadcasted_iota(jnp.int32, (8, D_MODEL), 0)
        vec = jnp.zeros((8, D_MODEL), F32)
        for i, piece in enumerate((n1, nm, n2, nf.reshape(1, D_MODEL), misc)):
            vec = jnp.where(row == i, piece, vec)
        return vec

    w_pack = pack_small(ffn1_norm, mix_norm, ffn2_norm, final_norm, b_forget, attn_sinks)
    m_pack = pack_small(m_ffn1_norm, m_mix_norm, m_ffn2_norm, m_final_norm, m_b_forget, m_attn_sinks)
    v_pack = pack_small(v_ffn1_norm, v_mix_norm, v_ffn2_norm, v_final_norm, v_b_forget, v_attn_sinks)
    small4 = _adam("adam_small", w_pack, m_pack, v_pack, tot[0:8][None])
    for i, nm in enumerate(("ffn1_norm", "mix_norm", "ffn2_norm")):
        out[nm] = tuple(r[i:i + 1] for r in small4)
    out["final_norm"] = tuple(r[3] for r in small4)
    out["b_forget"] = tuple(r[4:5, 0:B_HEADS] for r in small4)
    out["attn_sinks"] = tuple(r[4:5, B_HEADS:2 * B_HEADS] for r in small4)
    update(*last_update, small4[0])

    names = ("meta_tokens", "ffn1_norm", "ffn1_w_in", "ffn1_w_out", "mix_norm", "w_in", "b_forget", "attn_sinks",
             "w_branch_a", "w_branch_b", "w_out", "ffn2_norm", "ffn2_w_in", "ffn2_w_out", "final_norm")
    return (loss, grad_x) + tuple(out[nm][kind] for kind in range(4) for nm in names)
```

```python
import jax
import jax.numpy as jnp
from jax import lax
from jax.experimental import pallas as pl
from jax.experimental.pallas import tpu as pltpu

F32 = jnp.float32
BF16 = jnp.bfloat16

D_MODEL = 1024
N_META = 16
BLOCK = 128
PREFIX = 128
N_PAD = PREFIX - N_META
HEAD_DIM = 64
A_HEADS = 8
A_KV_HEADS = 2
A_GROUP = 4
B_HEADS = 8
A_WIDTH = 512
A_KV_WIDTH = 128
B_WIDTH = 512
D_FF = 2816
N_DEV = 8
FF_SHARD = 2 * D_FF // N_DEV
FF_SHARD_P = 768
FFO_SHARD = D_FF // N_DEV
FFO_SHARD_P = FF_SHARD_P // 2
D_FF_P = 4 * FF_SHARD_P
W_IN_COLS = 4360
WIN_SHARD = W_IN_COLS // N_DEV
WIN_SHARD_P = 640
PAIR_W = 3 * 128
B_SEG = 4 * PAIR_W
A_SEG = A_WIDTH + 2 * A_KV_WIDTH
QKV_W = B_SEG + A_SEG
P_GATES = 2 * (2 * D_MODEL)
P_F = P_GATES + 2 * D_MODEL
PROJ_P = P_F + 128
A_HEAD_ORDER = (0, 4, 1, 5, 2, 6, 3, 7)
EPS = 1e-6
NEG = -1e30
SCALE = HEAD_DIM ** -0.5
ADAM_LR = 0.001
ADAM_B1 = 0.9
ADAM_B2 = 0.999
ADAM_EPS = 1e-08
ADAM_WD = 0.01
ADAM_STEP = 10
VMEM_LIMIT = 56 * 1024 * 1024
MESH_ID = pl.DeviceIdType.MESH
SMALL_ROWS = 40

_NN = (((1,), (0,)), ((), ()))
_NT = (((1,), (1,)), ((), ()))
_TN = (((0,), (0,)), ((), ()))


def _params(sem=None):
    return pltpu.CompilerParams(dimension_semantics=sem, vmem_limit_bytes=VMEM_LIMIT)


def _pick(n, cands):
    for c in cands:
        if n % c == 0:
            return c
    raise ValueError(f"no tile for {n}")


def _bf(v):
    return v if v.dtype == BF16 else v.astype(BF16)


def _mm(name, a, b, dims, grid, a_spec, b_spec, o_spec, out_shape, out_dtype, acc_shape, k_axis=None, nk=1,
        alpha=1.0, res=None, res_spec=None, dep=None):
    has_res = res is not None
    has_dep = dep is not None

    def body(*refs):
        a_ref, b_ref = refs[0], refs[1]
        r_ref = refs[2] if has_res else None
        o_ref = refs[2 + has_res + has_dep]

        def finish(acc):
            if alpha != 1.0:
                acc = acc * alpha
            if has_res:
                acc = acc + r_ref[...]
            o_ref[...] = acc.astype(o_ref.dtype)

        part = lax.dot_general(_bf(a_ref[...]), _bf(b_ref[...]), dims, preferred_element_type=F32)
        if nk == 1:
            finish(part)
        else:
            acc_ref = refs[-1]
            k = pl.program_id(k_axis)

            @pl.when(k == 0)
            def _():
                acc_ref[...] = part

            @pl.when(k > 0)
            def _():
                acc_ref[...] += part

            @pl.when(k == nk - 1)
            def _():
                finish(acc_ref[...])

    in_specs = [a_spec, b_spec] + ([res_spec] if has_res else []) + ([pl.BlockSpec(memory_space=pl.ANY)] if has_dep else [])
    args = (a, b) + ((res,) if has_res else ()) + ((dep,) if has_dep else ())
    sem = tuple("arbitrary" if (nk > 1 and i == k_axis) else "parallel" for i in range(len(grid)))
    return pl.pallas_call(
        body, name=name, grid=grid, in_specs=in_specs, out_specs=o_spec,
        out_shape=jax.ShapeDtypeStruct(out_shape, out_dtype),
        scratch_shapes=[pltpu.VMEM(acc_shape, F32)] if nk > 1 else [],
        compiler_params=_params(sem),
    )(*args)


def _mm_nn(name, a, b, out_dtype=F32, alpha=1.0, res=None, tn_c=(512, 640, 256, 128), cols=None,
           tm_c=(1088, 768, 512, 256, 128), dep=None):
    t, k = a.shape
    c0, n = (0, b.shape[1]) if cols is None else cols
    tm = _pick(t, tm_c)
    tn = _pick(n, tn_c)
    assert c0 % tn == 0
    jb = c0 // tn
    return _mm(name, a, b, _NN, (t // tm, n // tn),
               pl.BlockSpec((tm, k), lambda i, j: (i, 0)), pl.BlockSpec((k, tn), lambda i, j: (0, jb + j)),
               pl.BlockSpec((tm, tn), lambda i, j: (i, j)), (t, n), out_dtype, None,
               alpha=alpha, res=res, res_spec=pl.BlockSpec((tm, tn), lambda i, j: (i, j)), dep=dep)


def _mm_nt(name, a, b, out_dtype=F32, alpha=1.0, tn_c=(768, 512, 256, 128), tk_c=None, dep=None, res=None, kcols=None):
    t, k = a.shape
    n = b.shape[0]
    c0 = 0 if kcols is None else kcols[0]
    tm = _pick(t, (1088, 768, 512, 256, 128))
    tn = _pick(n, tn_c)
    tk = k if tk_c is None else _pick(k, tk_c)
    nk = k // tk
    assert c0 % tk == 0
    kb = c0 // tk
    return _mm(name, a, b, _NT, (t // tm, n // tn, nk),
               pl.BlockSpec((tm, tk), lambda i, j, kk: (i, kk)), pl.BlockSpec((tn, tk), lambda i, j, kk: (j, kb + kk)),
               pl.BlockSpec((tm, tn), lambda i, j, kk: (i, j)), (t, n), out_dtype, (tm, tn), k_axis=2, nk=nk, alpha=alpha,
               dep=dep, res=res, res_spec=pl.BlockSpec((tm, tn), lambda i, j, kk: (i, j)))


def _mm_tn(name, a, b, out_dtype=BF16, alpha=1.0, tm_c=(768, 512, 256, 128), tn_c=(512, 640, 256, 128)):
    t, m = a.shape
    n = b.shape[1]
    tm = _pick(m, tm_c)
    tn = _pick(n, tn_c)
    bytes_a, bytes_b = a.size * a.dtype.itemsize, b.size * b.dtype.itemsize
    if bytes_a + bytes_b * (m // tm) <= bytes_b + bytes_a * (n // tn):
        return _mm(name, a, b, _TN, (m // tm, n // tn),
                   pl.BlockSpec((t, tm), lambda i, j: (0, i)), pl.BlockSpec((t, tn), lambda i, j: (0, j)),
                   pl.BlockSpec((tm, tn), lambda i, j: (i, j)), (m, n), out_dtype, None, alpha=alpha)
    return _mm(name, a, b, _TN, (n // tn, m // tm),
               pl.BlockSpec((t, tm), lambda j, i: (0, i)), pl.BlockSpec((t, tn), lambda j, i: (0, j)),
               pl.BlockSpec((tm, tn), lambda j, i: (i, j)), (m, n), out_dtype, None, alpha=alpha)


def _silu_grads(g, u, d):
    sg = jax.nn.sigmoid(g)
    return d * u * (sg * (1.0 + g * (1.0 - sg))), d * (g * sg)


def _ffn_in_fwd(name, n, wblk, dep=None):
    t = n.shape[0]
    tm = _pick(t, (1088, 768, 512, 256, 128))
    has_dep = dep is not None

    def body(n_ref, w_ref, *rest):
        gu_ref, a_ref = rest[-2], rest[-1]
        nv = n_ref[...]
        g = lax.dot_general(nv, w_ref[0], _NN, preferred_element_type=F32)
        u = lax.dot_general(nv, w_ref[1], _NN, preferred_element_type=F32)
        gu_ref[0] = g.astype(BF16)
        gu_ref[1] = u.astype(BF16)
        a_ref[...] = ((g * jax.nn.sigmoid(g)) * u).astype(BF16)

    return pl.pallas_call(
        body, name=name, grid=(t // tm, 4),
        in_specs=[pl.BlockSpec((tm, D_MODEL), lambda i, j: (i, 0)),
                  pl.BlockSpec((2, None, D_MODEL, FF_SHARD_P), lambda i, j: (0, j, 0, 0))]
        + ([pl.BlockSpec(memory_space=pl.ANY)] if has_dep else []),
        out_specs=[pl.BlockSpec((2, tm, FF_SHARD_P), lambda i, j: (0, i, j)),
                   pl.BlockSpec((tm, FF_SHARD_P), lambda i, j: (i, j))],
        out_shape=[jax.ShapeDtypeStruct((2, t, D_FF_P), BF16), jax.ShapeDtypeStruct((t, D_FF_P), BF16)],
        compiler_params=_params(("parallel", "parallel")),
    )(*((n, wblk) + ((dep,) if has_dep else ())))


def _ffn_out_bwd_x(name, dh, w_out, gu, dep=None):
    t = dh.shape[0]
    tm = _pick(t, (1088, 768, 512, 256, 128))
    has_dep = dep is not None

    def body(dh_ref, w_ref, gu_ref, *rest):
        o_ref = rest[-1]
        da = lax.dot_general(_bf(dh_ref[...]), w_ref[...], _NT, preferred_element_type=F32) * 0.5
        dg, du = _silu_grads(gu_ref[0].astype(F32), gu_ref[1].astype(F32), da)
        o_ref[0] = dg.astype(BF16)
        o_ref[1] = du.astype(BF16)

    gu_spec = pl.BlockSpec((2, tm, FF_SHARD_P), lambda i, j: (0, i, j))
    return pl.pallas_call(
        body, name=name, grid=(t // tm, 4),
        in_specs=[pl.BlockSpec((tm, D_MODEL), lambda i, j: (i, 0)), pl.BlockSpec((FF_SHARD_P, D_MODEL), lambda i, j: (j, 0)),
                  gu_spec] + ([pl.BlockSpec(memory_space=pl.ANY)] if has_dep else []),
        out_specs=gu_spec, out_shape=jax.ShapeDtypeStruct((2, t, D_FF_P), BF16),
        compiler_params=_params(("parallel", "parallel")),
    )(*((dh, w_out, gu) + ((dep,) if has_dep else ())))


def _rms_bwd_rows(dn, h, g, dres):
    r = lax.rsqrt(jnp.mean(h * h, axis=-1, keepdims=True) + EPS)
    tv = dn * g
    dot = jnp.mean(tv * h, axis=-1, keepdims=True)
    return dres + (r * tv - h * (r * r * r * dot)), jnp.sum(dn * (h * r), axis=0, keepdims=True)


def _accumulate_rows(ref, part, first):
    @pl.when(first)
    def _():
        ref[...] = part

    @pl.when(jnp.logical_not(first))
    def _():
        ref[...] += part


def _ffn_in_bwd_x(name, dgu, wblk, h_in, g_norm, dres, dep=None):
    t = dgu.shape[1]
    tm = _pick(t, (272, 256, 128))
    has_dep = dep is not None

    def body(d_ref, w_ref, h_ref, g_ref, r_ref, *rest):
        dh_ref, dg_ref = rest[-2], rest[-1]
        acc = None
        for s in range(2):
            for j in range(4):
                part = lax.dot_general(d_ref[s, :, FF_SHARD_P * j:FF_SHARD_P * (j + 1)], w_ref[s, j], _NT,
                                       preferred_element_type=F32)
                acc = part if acc is None else acc + part
        dh, dg = _rms_bwd_rows(acc, h_ref[...], g_ref[...], r_ref[...])
        dh_ref[...] = dh
        _accumulate_rows(dg_ref, dg, pl.program_id(0) == 0)

    row = pl.BlockSpec((tm, D_MODEL), lambda i: (i, 0))
    vec = pl.BlockSpec((1, D_MODEL), lambda i: (0, 0))
    return pl.pallas_call(
        body, name=name, grid=(t // tm,),
        in_specs=[pl.BlockSpec((2, tm, D_FF_P), lambda i: (0, i, 0)),
                  pl.BlockSpec((2, 4, D_MODEL, FF_SHARD_P), lambda i: (0, 0, 0, 0)), row, vec, row]
        + ([pl.BlockSpec(memory_space=pl.ANY)] if has_dep else []),
        out_specs=[row, vec],
        out_shape=[jax.ShapeDtypeStruct((t, D_MODEL), F32), jax.ShapeDtypeStruct((1, D_MODEL), F32)],
        compiler_params=_params(("arbitrary",)),
    )(*((dgu, wblk, h_in, g_norm, dres) + ((dep,) if has_dep else ())))


def _proj_bwd_x(name, dqkv, dgates, df, wi, h_in, g_norm, dres, dep=None):
    t = dqkv.shape[0]
    tm = _pick(t, (272, 256, 128))
    has_dep = dep is not None

    def body(q_ref, gt_ref, f_ref, w_ref, h_ref, g_ref, r_ref, *rest):
        dh_ref, dg_ref = rest[-2], rest[-1]
        acc = lax.dot_general(q_ref[...], w_ref[:, 0:QKV_W], _NT, preferred_element_type=F32)
        acc = acc + lax.dot_general(gt_ref[...], w_ref[:, P_GATES:P_F], _NT, preferred_element_type=F32)
        acc = acc + lax.dot_general(f_ref[...], w_ref[:, P_F:PROJ_P], _NT, preferred_element_type=F32)
        dh, dg = _rms_bwd_rows(acc, h_ref[...], g_ref[...], r_ref[...])
        dh_ref[...] = dh
        _accumulate_rows(dg_ref, dg, pl.program_id(0) == 0)

    row = pl.BlockSpec((tm, D_MODEL), lambda i: (i, 0))
    vec = pl.BlockSpec((1, D_MODEL), lambda i: (0, 0))
    return pl.pallas_call(
        body, name=name, grid=(t // tm,),
        in_specs=[pl.BlockSpec((tm, QKV_W), lambda i: (i, 0)), pl.BlockSpec((tm, 2 * D_MODEL), lambda i: (i, 0)),
                  pl.BlockSpec((tm, 128), lambda i: (i, 0)), pl.BlockSpec((D_MODEL, PROJ_P), lambda i: (0, 0)), row, vec, row]
        + ([pl.BlockSpec(memory_space=pl.ANY)] if has_dep else []),
        out_specs=[row, vec],
        out_shape=[jax.ShapeDtypeStruct((t, D_MODEL), F32), jax.ShapeDtypeStruct((1, D_MODEL), F32)],
        compiler_params=_params(("arbitrary",)),
    )(*((dqkv, dgates, df, wi, h_in, g_norm, dres) + ((dep,) if has_dep else ())))


def _ffn_in_bwd_w(name, n, dgu):
    t = n.shape[0]
    tk = t
    nk = 1
    return _mm(name, n, dgu, _TN, (2, 4, nk),
               pl.BlockSpec((tk, D_MODEL), lambda s, j, kk: (kk, 0)),
               pl.BlockSpec((None, tk, FF_SHARD_P), lambda s, j, kk: (s, kk, j)),
               pl.BlockSpec((None, None, D_MODEL, FF_SHARD_P), lambda s, j, kk: (s, j, 0, 0)),
               (2, 4, D_MODEL, FF_SHARD_P), BF16, (D_MODEL, FF_SHARD_P), k_axis=2, nk=nk)


def _rms_fwd(name, h, g):
    t = h.shape[0]
    tm = _pick(t, (544, 384, 256, 128))

    def body(h_ref, g_ref, o_ref):
        hv = h_ref[...]
        r = lax.rsqrt(jnp.mean(hv * hv, axis=-1, keepdims=True) + EPS)
        o_ref[...] = ((hv * r) * g_ref[...]).astype(BF16)

    return pl.pallas_call(
        body, name=name, grid=(t // tm,),
        in_specs=[pl.BlockSpec((tm, D_MODEL), lambda i: (i, 0)), pl.BlockSpec((1, D_MODEL), lambda i: (0, 0))],
        out_specs=pl.BlockSpec((tm, D_MODEL), lambda i: (i, 0)),
        out_shape=jax.ShapeDtypeStruct((t, D_MODEL), BF16), compiler_params=_params(("parallel",)),
    )(h, g)


def _rms_bwd(name, h, g, dn, dres):
    t = h.shape[0]
    tm = _pick(t, (544, 384, 256, 128))

    def body(h_ref, g_ref, dn_ref, dres_ref, dh_ref, dg_ref):
        i = pl.program_id(0)
        hv = h_ref[...]
        dnv = dn_ref[...]
        r = lax.rsqrt(jnp.mean(hv * hv, axis=-1, keepdims=True) + EPS)
        tv = dnv * g_ref[...]
        dot = jnp.mean(tv * hv, axis=-1, keepdims=True)
        dh_ref[...] = dres_ref[...] + (r * tv - hv * (r * r * r * dot))
        part = jnp.sum(dnv * (hv * r), axis=0, keepdims=True)

        @pl.when(i == 0)
        def _():
            dg_ref[...] = part

        @pl.when(i > 0)
        def _():
            dg_ref[...] += part

    row = pl.BlockSpec((tm, D_MODEL), lambda i: (i, 0))
    vec = pl.BlockSpec((1, D_MODEL), lambda i: (0, 0))
    return pl.pallas_call(
        body, name=name, grid=(t // tm,), in_specs=[row, vec, row, row], out_specs=[row, vec],
        out_shape=[jax.ShapeDtypeStruct((t, D_MODEL), F32), jax.ShapeDtypeStruct((1, D_MODEL), F32)],
        compiler_params=_params(("arbitrary",)),
    )(h, g, dn, dres)


def _branch_gate_fwd(name, oa, ob, wa, wb, gates):
    t = gates.shape[0]
    tm = _pick(t, (544, 384, 256, 128))

    def body(oa_ref, ob_ref, wa_ref, wb_ref, g_ref, o_ref, ya_ref, yb_ref):
        ya = lax.dot_general(_bf(oa_ref[...]), wa_ref[...], _NN, preferred_element_type=F32)
        yb = lax.dot_general(_bf(ob_ref[...]), wb_ref[...], _NN, preferred_element_type=F32)
        sa = jax.nn.sigmoid(g_ref[:, 0:D_MODEL])
        sb = jax.nn.sigmoid(g_ref[:, D_MODEL:2 * D_MODEL])
        o_ref[...] = (sa * ya + sb * yb).astype(BF16)
        ya_ref[...] = ya.astype(BF16)
        yb_ref[...] = yb.astype(BF16)

    blk = pl.BlockSpec((tm, D_MODEL), lambda i: (i, 0))
    narrow = pl.BlockSpec((tm, A_WIDTH), lambda i: (i, 0))
    wide = pl.BlockSpec((tm, 2 * D_MODEL), lambda i: (i, 0))
    wspec = pl.BlockSpec((A_WIDTH, D_MODEL), lambda i: (0, 0))
    out = jax.ShapeDtypeStruct((t, D_MODEL), BF16)
    return pl.pallas_call(
        body, name=name, grid=(t // tm,), in_specs=[narrow, narrow, wspec, wspec, wide], out_specs=[blk, blk, blk],
        out_shape=[out, out, out], compiler_params=_params(("parallel",)),
    )(oa, ob, wa, wb, gates)


def _mix_out_gate_bwd(name, dh, wo, gates, ya, yb):
    t = gates.shape[0]
    tm = _pick(t, (544, 384, 256, 128))

    def body(dh_ref, w_ref, g_ref, ya_ref, yb_ref, dya_ref, dyb_ref, dg_ref):
        dm = lax.dot_general(_bf(dh_ref[...]), w_ref[...], _NT, preferred_element_type=F32)
        sa = jax.nn.sigmoid(g_ref[:, 0:D_MODEL])
        sb = jax.nn.sigmoid(g_ref[:, D_MODEL:2 * D_MODEL])
        dya_ref[...] = (dm * sa).astype(BF16)
        dyb_ref[...] = (dm * sb).astype(BF16)
        dg_ref[:, 0:D_MODEL] = (dm * ya_ref[...].astype(F32) * (sa * (1.0 - sa))).astype(BF16)
        dg_ref[:, D_MODEL:2 * D_MODEL] = (dm * yb_ref[...].astype(F32) * (sb * (1.0 - sb))).astype(BF16)

    blk = pl.BlockSpec((tm, D_MODEL), lambda i: (i, 0))
    wide = pl.BlockSpec((tm, 2 * D_MODEL), lambda i: (i, 0))
    out = jax.ShapeDtypeStruct((t, D_MODEL), BF16)
    return pl.pallas_call(
        body, name=name, grid=(t // tm,),
        in_specs=[blk, pl.BlockSpec((D_MODEL, D_MODEL), lambda i: (0, 0)), wide, blk, blk], out_specs=[blk, blk, wide],
        out_shape=[out, out, jax.ShapeDtypeStruct((t, 2 * D_MODEL), BF16)], compiler_params=_params(("parallel",)),
    )(dh, wo, gates, ya, yb)


def _loss_head(name, h3, gf, tgt):
    b, l, _ = h3.shape
    nb = l // BLOCK

    def body(h_ref, g_ref, t_ref, dh_ref, loss_ref, dg_ref):
        first = (pl.program_id(0) == 0) & (pl.program_id(1) == 0)
        real = (pl.program_id(1) > 0).astype(F32)
        hv = h_ref[...]
        g = g_ref[...]
        r = lax.rsqrt(jnp.mean(hv * hv, axis=-1, keepdims=True) + EPS)
        xn = hv * r
        err = (xn * g - t_ref[...]) * real
        lpart = 0.5 * jnp.sum(jnp.mean(err * err, axis=-1, keepdims=True), axis=0, keepdims=True)
        dy = err * (1.0 / D_MODEL)
        tv = dy * g
        dot = jnp.mean(tv * hv, axis=-1, keepdims=True)
        dh_ref[...] = r * tv - hv * (r * r * r * dot)
        gpart = jnp.sum(dy * xn, axis=0, keepdims=True)

        @pl.when(first)
        def _():
            loss_ref[...] = jnp.zeros_like(loss_ref)
            dg_ref[...] = jnp.zeros_like(dg_ref)

        loss_ref[...] += jnp.broadcast_to(lpart, loss_ref.shape)
        dg_ref[...] += gpart

    return pl.pallas_call(
        body, name=name, grid=(b, nb),
        in_specs=[pl.BlockSpec((None, BLOCK, D_MODEL), lambda bi, n: (bi, n, 0)),
                  pl.BlockSpec((1, D_MODEL), lambda bi, n: (0, 0)),
                  pl.BlockSpec((None, BLOCK, D_MODEL), lambda bi, n: (bi, jnp.maximum(n - 1, 0), 0))],
        out_specs=[pl.BlockSpec((None, BLOCK, D_MODEL), lambda bi, n: (bi, n, 0)),
                   pl.BlockSpec((8, 128), lambda bi, n: (0, 0)),
                   pl.BlockSpec((1, D_MODEL), lambda bi, n: (0, 0))],
        out_shape=[jax.ShapeDtypeStruct(h3.shape, F32), jax.ShapeDtypeStruct((8, 128), F32),
                   jax.ShapeDtypeStruct((1, D_MODEL), F32)],
        compiler_params=_params(("arbitrary", "arbitrary")),
    )(h3, gf, tgt)


def _fgate_fwd(name, f3, bf_row):
    b, l, _ = f3.shape
    nb = l // BLOCK

    def body(f_ref, b_ref, cc_ref, cr_ref):
        r_i = lax.broadcasted_iota(jnp.int32, (BLOCK, BLOCK), 0)
        c_i = lax.broadcasted_iota(jnp.int32, (BLOCK, BLOCK), 1)
        tri = (r_i >= c_i).astype(F32)
        carry = jnp.zeros((1, 128), F32)
        for blk in range(nb):
            rows = slice(blk * BLOCK, (blk + 1) * BLOCK)
            z = f_ref[rows, :] + b_ref[...]
            lf = jnp.minimum(z, 0.0) - jnp.log(1.0 + jnp.exp(-jnp.abs(z)))
            cb = jnp.dot(tri, lf, preferred_element_type=F32, precision=lax.Precision.HIGHEST) + carry
            carry = cb[BLOCK - 1:BLOCK, :]
            cbt = cb.T
            for hh in range(B_HEADS):
                cc_ref[hh, rows, :] = jnp.sum(jnp.where(c_i == hh, cb, 0.0), axis=1, keepdims=True)
                cr_ref[hh, :, rows] = cbt[hh:hh + 1, :]

    return pl.pallas_call(
        body, name=name, grid=(b,),
        in_specs=[pl.BlockSpec((None, l, 128), lambda bi: (bi, 0, 0)),
                  pl.BlockSpec((1, 128), lambda bi: (0, 0))],
        out_specs=[pl.BlockSpec((None, B_HEADS, l, 1), lambda bi: (bi, 0, 0, 0)),
                   pl.BlockSpec((None, B_HEADS, 1, l), lambda bi: (bi, 0, 0, 0))],
        out_shape=[jax.ShapeDtypeStruct((b, B_HEADS, l, 1), F32), jax.ShapeDtypeStruct((b, B_HEADS, 1, l), F32)],
        compiler_params=_params(("parallel",)),
    )(f3, bf_row)


def _fgate_bwd(name, f3, bf_row, dcq, dck):
    b, l, _ = f3.shape
    nb = l // BLOCK

    def body(f_ref, b_ref, dcq_ref, dck_ref, df_ref, db_ref):
        r_i = lax.broadcasted_iota(jnp.int32, (BLOCK, BLOCK), 0)
        c_i = lax.broadcasted_iota(jnp.int32, (BLOCK, BLOCK), 1)
        tri = (r_i <= c_i).astype(F32)
        carry = jnp.zeros((1, 128), F32)
        total = jnp.zeros((1, 128), F32)
        for blk in range(nb - 1, -1, -1):
            rows = slice(blk * BLOCK, (blk + 1) * BLOCK)
            krows = jnp.concatenate([dck_ref[hh, :, rows] for hh in range(B_HEADS)]
                                    + [jnp.zeros((BLOCK - B_HEADS, BLOCK), F32)], axis=0)
            dcb = krows.T
            for hh in range(B_HEADS):
                dcb = dcb + jnp.where(c_i == hh, dcq_ref[hh, rows, :], 0.0)
            rc = jnp.dot(tri, dcb, preferred_element_type=F32, precision=lax.Precision.HIGHEST) + carry
            carry = rc[0:1, :]
            z = f_ref[rows, :] + b_ref[...]
            df = rc * (1.0 / (1.0 + jnp.exp(z)))
            df_ref[rows, :] = df.astype(BF16)
            total = total + jnp.sum(df, axis=0, keepdims=True)

        @pl.when(pl.program_id(0) == 0)
        def _():
            db_ref[...] = total

        @pl.when(pl.program_id(0) > 0)
        def _():
            db_ref[...] += total

    return pl.pallas_call(
        body, name=name, grid=(b,),
        in_specs=[pl.BlockSpec((None, l, 128), lambda bi: (bi, 0, 0)),
                  pl.BlockSpec((1, 128), lambda bi: (0, 0)),
                  pl.BlockSpec((None, B_HEADS, l, 1), lambda bi: (bi, 0, 0, 0)),
                  pl.BlockSpec((None, B_HEADS, 1, l), lambda bi: (bi, 0, 0, 0))],
        out_specs=[pl.BlockSpec((None, l, 128), lambda bi: (bi, 0, 0)), pl.BlockSpec((1, 128), lambda bi: (0, 0))],
        out_shape=[jax.ShapeDtypeStruct((b, l, 128), BF16), jax.ShapeDtypeStruct((1, 128), F32)],
        compiler_params=_params(("arbitrary",)),
    )(f3, bf_row, dcq, dck)


A_Q_BLK = B_SEG // A_WIDTH
A_K_BLK = (B_SEG + A_WIDTH) // 128
A_V_BLK = A_K_BLK + 1
A_SEG_BLK = B_SEG // A_SEG
STACK = A_HEADS * BLOCK


def _lane_lo():
    return lax.broadcasted_iota(jnp.int32, (1, 128), 1) < HEAD_DIM


def _stack_heads(x, masked):
    lo = _lane_lo()
    blks = [x[:, 128 * j:128 * (j + 1)] for j in range(4)]
    if not masked:
        return jnp.concatenate(blks + blks, axis=0)
    zero = jnp.zeros_like(blks[0])
    return jnp.concatenate([jnp.where(lo, bk, zero) for bk in blks] + [jnp.where(lo, zero, bk) for bk in blks], axis=0)


def _unstack_heads(y):
    lo = _lane_lo()
    return jnp.concatenate([jnp.where(lo, y[128 * j:128 * (j + 1)], y[128 * (4 + j):128 * (5 + j)]) for j in range(4)], axis=1)


def _swa_bias(slopes):
    r_i = jnp.arange(STACK)[:, None]
    c_i = jnp.arange(3 * BLOCK)[None, :]
    seg = c_i >> 7
    out = []
    for n in range(3):
        qpos = n * BLOCK + (r_i & (BLOCK - 1))
        kpos = jnp.where(seg == 0, c_i, (n - 2) * BLOCK + c_i)
        dist = qpos - kpos
        band = (seg != 0) & (dist < BLOCK) & (kpos >= PREFIX)
        meta = (seg == 0) & (c_i >= N_PAD)
        out.append(jnp.where((dist >= 0) & (band | meta), -slopes * dist.astype(F32), NEG))
    return jnp.stack(out, axis=0)


def _swa_scores(q, kcat, n, slope, bias):
    s = lax.dot_general(q, kcat, _NT, preferred_element_type=F32) + bias
    further = slope * (-BLOCK * jnp.maximum(n - 2, 0)).astype(F32)
    return jnp.concatenate([s[:, 0:BLOCK] + further, s[:, BLOCK:]], axis=1)


def _swa_specs():
    def kv(col_blk):
        return [pl.BlockSpec((None, BLOCK, 128), lambda b, n: (b, 0, col_blk)),
                pl.BlockSpec((None, BLOCK, 128), lambda b, n: (b, jnp.maximum(n - 1, 0), col_blk)),
                pl.BlockSpec((None, BLOCK, 128), lambda b, n: (b, n, col_blk))]

    q_spec = pl.BlockSpec((None, BLOCK, A_WIDTH), lambda b, n: (b, n, A_Q_BLK))
    o_spec = pl.BlockSpec((None, BLOCK, A_WIDTH), lambda b, n: (b, n, 0))
    col = pl.BlockSpec((STACK, 1), lambda b, n: (0, 0))
    bias = pl.BlockSpec((None, STACK, 3 * BLOCK), lambda b, n: (jnp.minimum(n, 2), 0, 0))
    lse_spec = pl.BlockSpec((None, A_HEADS, BLOCK, 1), lambda b, n: (b, 0, n, 0))
    return q_spec, kv(A_K_BLK), kv(A_V_BLK), o_spec, [col, col, bias], lse_spec


def _swa_fwd(name, qkv, slopes, sinks, bias):
    b, l, _ = qkv.shape
    nb = l // BLOCK

    def body(q_ref, k0_ref, kp_ref, kc_ref, v0_ref, vp_ref, vc_ref, sl_ref, sk_ref, bias_ref, o_ref, lse_ref):
        n = pl.program_id(1)
        qs = _stack_heads(q_ref[...], True) * SCALE
        kcat = jnp.concatenate([k0_ref[...], kp_ref[...], kc_ref[...]], axis=0)
        vcat = jnp.concatenate([v0_ref[...], vp_ref[...], vc_ref[...]], axis=0)
        s = _swa_scores(qs, kcat, n, sl_ref[...], bias_ref[...])
        sink = sk_ref[...]
        m = jnp.maximum(jnp.max(s, axis=-1, keepdims=True), sink)
        p = jnp.exp(s - m)
        den = jnp.sum(p, axis=-1, keepdims=True) + jnp.exp(sink - m)
        o = lax.dot_general(p.astype(BF16), vcat, _NN, preferred_element_type=F32) / den
        o_ref[...] = _unstack_heads(o)
        lse_ref[...] = (m + jnp.log(den)).reshape(A_HEADS, BLOCK, 1)

    q_spec, k_specs, v_specs, o_spec, consts, lse_spec = _swa_specs()
    return pl.pallas_call(
        body, name=name, grid=(b, nb),
        in_specs=[q_spec] + k_specs + v_specs + consts, out_specs=[o_spec, lse_spec],
        out_shape=[jax.ShapeDtypeStruct((b, l, A_WIDTH), F32), jax.ShapeDtypeStruct((b, A_HEADS, l, 1), F32)],
        compiler_params=_params(("parallel", "parallel")),
    )(qkv, qkv, qkv, qkv, qkv, qkv, qkv, slopes, sinks, bias)


def _swa_bwd(name, qkv, o, lse, do, slopes, sinks, bias, dqkv):
    b, l, _ = qkv.shape
    nb = l // BLOCK

    def body(q_ref, k0_ref, kp_ref, kc_ref, v0_ref, vp_ref, vc_ref, o_ref, lse_ref, do_ref, sl_ref, sk_ref, bias_ref, _,
             dx_ref, ds_ref, dk_acc, dv_acc):
        bi = pl.program_id(0)
        n = pl.program_id(1)
        qs = _stack_heads(q_ref[...], True) * SCALE
        dos32 = _stack_heads(do_ref[...], True)
        dos = dos32.astype(BF16)
        os_ = _stack_heads(o_ref[...], False)
        lsev = lse_ref[...].reshape(STACK, 1)
        kcat = jnp.concatenate([k0_ref[...], kp_ref[...], kc_ref[...]], axis=0)
        vcat = jnp.concatenate([v0_ref[...], vp_ref[...], vc_ref[...]], axis=0)
        s = _swa_scores(qs, kcat, n, sl_ref[...], bias_ref[...])
        p = jnp.exp(s - lsev)
        dsum = jnp.sum(dos32 * os_, axis=-1, keepdims=True)
        dp = lax.dot_general(dos, vcat, _NT, preferred_element_type=F32)
        dsc = (p * (dp - dsum)).astype(BF16)
        dq = lax.dot_general(dsc, kcat, _NN, preferred_element_type=F32) * SCALE
        row0 = pl.multiple_of(n * BLOCK, BLOCK)
        dx_ref[pl.ds(row0, BLOCK), 0:A_WIDTH] = _unstack_heads(dq).astype(BF16)
        dkc = lax.dot_general(dsc, qs, _TN, preferred_element_type=F32)
        dvc = lax.dot_general(p.astype(BF16), dos, _TN, preferred_element_type=F32)

        @pl.when(n == 0)
        def _():
            dk_acc[...] = jnp.zeros_like(dk_acc)
            dv_acc[...] = jnp.zeros_like(dv_acc)

        starts = (0, pl.multiple_of(jnp.maximum(n - 1, 0) * BLOCK, BLOCK), row0)
        for t, st in enumerate(starts):
            dk_acc[pl.ds(st, BLOCK), :] += dkc[t * BLOCK:(t + 1) * BLOCK, :]
            dv_acc[pl.ds(st, BLOCK), :] += dvc[t * BLOCK:(t + 1) * BLOCK, :]

        @pl.when(n == nb - 1)
        def _():
            dx_ref[:, A_WIDTH:A_WIDTH + 128] = dk_acc[...].astype(BF16)
            dx_ref[:, A_WIDTH + 128:A_SEG] = dv_acc[...].astype(BF16)

        dsink = -(jnp.exp(sk_ref[...] - lsev) * dsum)
        r8 = lax.broadcasted_iota(jnp.int32, (8, 128), 0)
        acc = jnp.zeros((8, 128), F32)
        for hh in range(A_HEADS):
            acc = acc + jnp.where(r8 == hh, jnp.sum(dsink[hh * BLOCK:(hh + 1) * BLOCK, :]), 0.0)

        @pl.when((bi == 0) & (n == 0))
        def _():
            ds_ref[...] = jnp.zeros_like(ds_ref)

        ds_ref[...] += acc

    q_spec, k_specs, v_specs, o_spec, consts, lse_spec = _swa_specs()
    return pl.pallas_call(
        body, name=name, grid=(b, nb),
        in_specs=[q_spec] + k_specs + v_specs + [o_spec, lse_spec, o_spec] + consts + [pl.BlockSpec(memory_space=pl.ANY)],
        out_specs=[pl.BlockSpec((None, l, A_SEG), lambda bb, n: (bb, 0, A_SEG_BLK)),
                   pl.BlockSpec((8, 128), lambda bb, n: (0, 0))],
        out_shape=[jax.ShapeDtypeStruct(dqkv.shape, BF16), jax.ShapeDtypeStruct((8, 128), F32)],
        scratch_shapes=[pltpu.VMEM((l, 128), F32), pltpu.VMEM((l, 128), F32)],
        input_output_aliases={13: 0},
        compiler_params=_params(("arbitrary", "arbitrary")),
    )(qkv, qkv, qkv, qkv, qkv, qkv, qkv, o, lse, do, slopes, sinks, bias, dqkv)


def _fox_mask(qk, ck, i):
    kh = qk.shape[1]
    qpos = i * BLOCK + lax.broadcasted_iota(jnp.int32, (BLOCK, kh), 0)
    kpos = lax.broadcasted_iota(jnp.int32, (BLOCK, kh), 1)
    return jnp.where((kpos <= qpos) & (kpos >= N_PAD), qk - ck, NEG)


def _pick_head(x, hh):
    lo = _lane_lo()
    return jnp.where(lo if hh == 0 else jnp.logical_not(lo), x, jnp.zeros_like(x))


def _both_heads(x):
    return jnp.concatenate([_pick_head(x, 0), _pick_head(x, 1)], axis=0)


def _fox_specs(l):
    pair = pl.BlockSpec((None, l, PAIR_W), lambda bi, hp: (bi, 0, hp))
    half = pl.BlockSpec((None, l, 128), lambda bi, hp: (bi, 0, hp))
    colv = pl.BlockSpec((None, 2, l, 1), lambda bi, hp: (bi, hp, 0, 0))
    rowv = pl.BlockSpec((None, 2, 1, l), lambda bi, hp: (bi, hp, 0, 0))
    return pair, half, colv, rowv


def _fox_fwd(name, qkv, c_col, c_row):
    b, l, _ = qkv.shape
    nb = l // BLOCK

    def body(x_ref, cc_ref, cr_ref, o_ref, lse_ref):
        for i in range(nb):
            rows = slice(i * BLOCK, (i + 1) * BLOCK)
            kh = (i + 1) * BLOCK
            qblk = x_ref[rows, 0:128]
            kv = x_ref[0:kh, 128:256]
            vv = x_ref[0:kh, 256:384]
            qk = lax.dot_general(_both_heads(qblk) * SCALE, kv, _NT, preferred_element_type=F32)
            ps, dens = [], []
            for hh in range(2):
                s = _fox_mask(qk[hh * BLOCK:(hh + 1) * BLOCK], cr_ref[hh, :, 0:kh], i)
                m = jnp.max(s, axis=-1, keepdims=True)
                p = jnp.exp(s - m)
                den = jnp.sum(p, axis=-1, keepdims=True)
                ps.append(p.astype(BF16))
                dens.append(den)
                lse_ref[hh, rows, :] = (m + jnp.log(den)) + cc_ref[hh, rows, :]
            pv = lax.dot_general(jnp.concatenate(ps, axis=0), vv, _NN, preferred_element_type=F32)
            o_ref[rows, :] = jnp.where(_lane_lo(), pv[0:BLOCK] / dens[0], pv[BLOCK:2 * BLOCK] / dens[1]).astype(BF16)

    pair, half, colv, rowv = _fox_specs(l)
    return pl.pallas_call(
        body, name=name, grid=(b, 4), in_specs=[pair, colv, rowv], out_specs=[half, colv],
        out_shape=[jax.ShapeDtypeStruct((b, l, B_WIDTH), BF16), jax.ShapeDtypeStruct((b, B_HEADS, l, 1), F32)],
        compiler_params=_params(("parallel", "parallel")),
    )(qkv, c_col, c_row)


def _fox_bwd(name, qkv, c_col, c_row, o, lse, do):
    b, l, _ = qkv.shape
    nb = l // BLOCK

    def body(x_ref, cc_ref, cr_ref, o_ref, lse_ref, do_ref, dx_ref, dcq_ref, dck_ref, dk_acc, dv_acc):
        dk_acc[...] = jnp.zeros_like(dk_acc)
        dv_acc[...] = jnp.zeros_like(dv_acc)
        dck_ref[...] = jnp.zeros_like(dck_ref)
        for i in range(nb):
            rows = slice(i * BLOCK, (i + 1) * BLOCK)
            kh = (i + 1) * BLOCK
            qblk = x_ref[rows, 0:128]
            kv = x_ref[0:kh, 128:256]
            vv = x_ref[0:kh, 256:384]
            doblk = do_ref[rows, :]
            ov = o_ref[rows, :].astype(F32)
            q2 = _both_heads(qblk) * SCALE
            do2 = _both_heads(doblk)
            qk = lax.dot_general(q2, kv, _NT, preferred_element_type=F32)
            dp = lax.dot_general(do2, vv, _NT, preferred_element_type=F32)
            ps, dss = [], []
            for hh in range(2):
                half = slice(hh * BLOCK, (hh + 1) * BLOCK)
                s = _fox_mask(qk[half], cr_ref[hh, :, 0:kh], i)
                p = jnp.exp(s - (lse_ref[hh, rows, :] - cc_ref[hh, rows, :]))
                dsum = jnp.sum(do2[half].astype(F32) * ov, axis=-1, keepdims=True)
                ds = p * (dp[half] - dsum)
                ps.append(p.astype(BF16))
                dss.append(ds.astype(BF16))
                dcq_ref[hh, rows, :] = jnp.sum(ds, axis=-1, keepdims=True)
                dck_ref[hh, :, 0:kh] -= jnp.sum(ds, axis=0, keepdims=True)
            p2 = jnp.concatenate(ps, axis=0)
            ds2 = jnp.concatenate(dss, axis=0)
            dq = lax.dot_general(ds2, kv, _NN, preferred_element_type=F32) * SCALE
            dk_acc[0:kh, :] += lax.dot_general(ds2, q2, _TN, preferred_element_type=F32)
            dv_acc[0:kh, :] += lax.dot_general(p2, do2, _TN, preferred_element_type=F32)
            dx_ref[rows, 0:128] = jnp.where(_lane_lo(), dq[0:BLOCK], dq[BLOCK:2 * BLOCK]).astype(BF16)
        dx_ref[:, 128:256] = dk_acc[...].astype(BF16)
        dx_ref[:, 256:384] = dv_acc[...].astype(BF16)

    pair, half, colv, rowv = _fox_specs(l)
    return pl.pallas_call(
        body, name=name, grid=(b, 4), in_specs=[pair, colv, rowv, half, colv, half],
        out_specs=[pair, colv, rowv],
        out_shape=[jax.ShapeDtypeStruct(qkv.shape, BF16), jax.ShapeDtypeStruct((b, B_HEADS, l, 1), F32),
                   jax.ShapeDtypeStruct((b, B_HEADS, 1, l), F32)],
        scratch_shapes=[pltpu.VMEM((l, 128), F32), pltpu.VMEM((l, 128), F32)],
        compiler_params=_params(("parallel", "parallel")),
    )(qkv, c_col, c_row, o, lse, do)


_FLIPS = ((0, 0, 1), (0, 1, 0), (0, 1, 1), (1, 0, 0), (1, 0, 1), (1, 1, 0), (1, 1, 1))


def _exchange(name, gather, scatter):
    ng, ns = len(gather), len(scatter)
    na = ng + ns
    npeer = len(_FLIPS)

    def body(*refs):
        ins = refs[:na]
        outs = refs[na:2 * na]
        send_sems, recv_sems, loc_sems = refs[2 * na:]
        x, y, c = lax.axis_index("x"), lax.axis_index("y"), lax.axis_index("c")
        me = 4 * x + 2 * y + c
        peers = []
        for fx, fy, fc in _FLIPS:
            px = 1 - x if fx else x
            py = 1 - y if fy else y
            pc = 1 - c if fc else c
            peers.append(((px, py, pc), 4 * px + 2 * py + pc))

        def remote(a, kk):
            dev, lin = peers[kk]
            src = ins[a] if a < ng else ins[a].at[lin]
            return pltpu.make_async_remote_copy(src_ref=src, dst_ref=outs[a].at[me], send_sem=send_sems.at[a * npeer + kk],
                                                recv_sem=recv_sems.at[a * npeer + kk], device_id=dev, device_id_type=MESH_ID)

        def arrival(a, kk):
            dev, lin = peers[kk]
            src = ins[a] if a < ng else ins[a].at[lin]
            return pltpu.make_async_remote_copy(src_ref=src, dst_ref=outs[a].at[lin], send_sem=send_sems.at[a * npeer + kk],
                                                recv_sem=recv_sems.at[a * npeer + kk], device_id=dev, device_id_type=MESH_ID)

        local = []
        for a in range(na):
            src = ins[a] if a < ng else ins[a].at[me]
            cp = pltpu.make_async_copy(src, outs[a].at[me], loc_sems.at[a])
            cp.start()
            local.append(cp)
        sent = [remote(a, kk) for kk in range(npeer) for a in range(na)]
        for cp in sent:
            cp.start()
        for kk in range(npeer):
            for a in range(na):
                arrival(a, kk).wait_recv()
        for cp in sent:
            cp.wait_send()
        for cp in local:
            cp.wait()

    arrs = list(gather) + list(scatter)
    out_shape = [jax.ShapeDtypeStruct((N_DEV,) + tuple(a.shape), a.dtype) for a in gather]
    out_shape += [jax.ShapeDtypeStruct(tuple(a.shape), a.dtype) for a in scatter]
    anyspec = pl.BlockSpec(memory_space=pl.ANY)
    return pl.pallas_call(
        body, name=name, in_specs=[anyspec] * na, out_specs=[anyspec] * na, out_shape=out_shape,
        scratch_shapes=[pltpu.SemaphoreType.DMA((na * npeer,)), pltpu.SemaphoreType.DMA((na * npeer,)),
                        pltpu.SemaphoreType.DMA((na,))],
        compiler_params=pltpu.CompilerParams(has_side_effects=True),
    )(*arrs)


def _peer_table():
    x, y, c = lax.axis_index("x"), lax.axis_index("y"), lax.axis_index("c")
    me = 4 * x + 2 * y + c
    peers = []
    for fx, fy, fc in _FLIPS:
        px = 1 - x if fx else x
        py = 1 - y if fy else y
        pc = 1 - c if fc else c
        peers.append(((px, py, pc), 4 * px + 2 * py + pc))
    return me, peers


_HBM = pl.BlockSpec(memory_space=pltpu.HBM)
_SEM = pl.BlockSpec(memory_space=pltpu.SEMAPHORE)
_ANY = pl.BlockSpec(memory_space=pl.ANY)
_EFFECT = pltpu.SideEffectType.DATAFLOW_SIDE_EFFECTING


def _split_copy(srcs_are_pieces, src_refs, land_refs, send_sem, recv_sem, a, kk, me, peers, arriving):
    dev, lin = peers[kk]
    npeer = len(_FLIPS)
    src = src_refs[a] if srcs_are_pieces[a] else src_refs[a].at[lin]
    dst = land_refs[a].at[lin] if arriving else land_refs[a].at[me]
    return pltpu.make_async_remote_copy(src_ref=src, dst_ref=dst, send_sem=send_sem.at[a * npeer + kk],
                                        recv_sem=recv_sem.at[a * npeer + kk], device_id=dev, device_id_type=MESH_ID)


def _xchg_start(name, gather, scatter, after=None):
    me_out = 4 * lax.axis_index("x") + 2 * lax.axis_index("y") + lax.axis_index("c")
    srcs = list(gather) + list(scatter)
    is_piece = [True] * len(gather) + [False] * len(scatter)
    lands = []
    for a, piece in zip(srcs, is_piece):
        own = a[None] if piece else lax.dynamic_slice_in_dim(a, me_out, 1, axis=0)
        shape = ((N_DEV,) + tuple(a.shape)) if piece else tuple(a.shape)
        start = (me_out,) + (0,) * (len(shape) - 1)
        lands.append(lax.dynamic_update_slice(lax.empty(shape, a.dtype), own, start))
    n = len(srcs)
    nsem = n * len(_FLIPS)
    has_after = after is not None

    def body(*refs):
        src_refs = refs[:n]
        land_refs = refs[n:2 * n]
        outs = refs[2 * n + (1 if has_after else 0):]
        send_sem, recv_sem = outs[0], outs[1]
        token = outs[-1]
        me, peers = _peer_table()
        for kk in range(len(_FLIPS)):
            for a in range(n):
                _split_copy(is_piece, src_refs, land_refs, send_sem, recv_sem, a, kk, me, peers, False).start()
        token[...] = jnp.zeros_like(token)

    out_shape = ([pltpu.SemaphoreType.DMA((nsem,)), pltpu.SemaphoreType.DMA((nsem,))]
                 + [pltpu.HBM(tuple(a.shape), a.dtype) for a in srcs] + [pltpu.HBM(tuple(a.shape), a.dtype) for a in lands]
                 + [jax.ShapeDtypeStruct((8, 128), F32)])
    args = [pltpu.with_memory_space_constraint(a, pltpu.HBM) for a in srcs + lands] + ([after] if has_after else [])
    res = pl.pallas_call(
        body, name=name, out_shape=out_shape,
        in_specs=[_HBM] * (2 * n) + ([_ANY] if has_after else []),
        out_specs=[_SEM, _SEM] + [_HBM] * (2 * n) + [pl.BlockSpec(memory_space=pltpu.VMEM)],
        input_output_aliases={i: 2 + i for i in range(2 * n)},
        compiler_params=pltpu.CompilerParams(has_side_effects=_EFFECT),
    )(*args)
    state = (res[0], res[1], list(res[2:2 + n]), list(res[2 + n:2 + 2 * n]), is_piece)
    return state, res[-1]


def _xchg_wait(name, state, after):
    send_sem, recv_sem, srcs, lands, is_piece = state
    n = len(srcs)

    def body(*refs):
        src_refs = refs[:n]
        land_refs = refs[n:2 * n]
        s_sem, r_sem = refs[2 * n], refs[2 * n + 1]
        me, peers = _peer_table()
        for kk in range(len(_FLIPS)):
            for a in range(n):
                cp = _split_copy(is_piece, src_refs, land_refs, s_sem, r_sem, a, kk, me, peers, True)
                cp.wait_send()
                cp.wait_recv()

    out_shape = [pltpu.HBM(tuple(a.shape), a.dtype) for a in srcs] + [pltpu.HBM(tuple(a.shape), a.dtype) for a in lands]
    res = pl.pallas_call(
        body, name=name, out_shape=out_shape,
        in_specs=[_HBM] * (2 * n) + [_SEM, _SEM, _ANY], out_specs=[_HBM] * (2 * n),
        input_output_aliases={i: i for i in range(2 * n)},
        compiler_params=pltpu.CompilerParams(has_side_effects=_EFFECT),
    )(*srcs, *lands, send_sem, recv_sem, after)
    return list(res[n:])


_SIB = (0, 0, 1)
_ICI = ((0, 1, 0), (1, 0, 0), (1, 1, 0))


def _flip(fl):
    x, y, c = lax.axis_index("x"), lax.axis_index("y"), lax.axis_index("c")
    px = 1 - x if fl[0] else x
    py = 1 - y if fl[1] else y
    pc = 1 - c if fl[2] else c
    return (px, py, pc), 4 * px + 2 * py + pc


def _gather2_start(name, pieces, after=None):
    me_out = 4 * lax.axis_index("x") + 2 * lax.axis_index("y") + lax.axis_index("c")
    pieces = list(pieces)
    n = len(pieces)
    lands = [lax.dynamic_update_slice(lax.empty((N_DEV,) + tuple(a.shape), a.dtype), a[None],
                                      (me_out,) + (0,) * a.ndim) for a in pieces]
    first = (_SIB,) + _ICI
    has_after = after is not None

    def body(*refs):
        src_refs, land_refs = refs[:n], refs[n:2 * n]
        outs = refs[2 * n + (1 if has_after else 0):]
        send_sem, recv_sem, token = outs[0], outs[1], outs[-1]
        _, me = _flip((0, 0, 0))
        for kk, fl in enumerate(first):
            dev, _ = _flip(fl)
            for a in range(n):
                pltpu.make_async_remote_copy(src_ref=src_refs[a], dst_ref=land_refs[a].at[me],
                                             send_sem=send_sem.at[a * 4 + kk], recv_sem=recv_sem.at[a * 4 + kk],
                                             device_id=dev, device_id_type=MESH_ID).start()
        token[...] = jnp.zeros_like(token)

    hbm = [pltpu.HBM(tuple(a.shape), a.dtype) for a in pieces + lands]
    res = pl.pallas_call(
        body, name=name,
        out_shape=[pltpu.SemaphoreType.DMA((4 * n,)), pltpu.SemaphoreType.DMA((4 * n,))] + hbm
        + [jax.ShapeDtypeStruct((8, 128), F32)],
        in_specs=[_HBM] * (2 * n) + ([_ANY] if has_after else []),
        out_specs=[_SEM, _SEM] + [_HBM] * (2 * n) + [pl.BlockSpec(memory_space=pltpu.VMEM)],
        input_output_aliases={i: 2 + i for i in range(2 * n)},
        compiler_params=pltpu.CompilerParams(has_side_effects=_EFFECT),
    )(*([pltpu.with_memory_space_constraint(a, pltpu.HBM) for a in pieces + lands] + ([after] if has_after else [])))
    return (res[0], res[1], list(res[2:2 + n]), list(res[2 + n:2 + 2 * n])), res[-1]


def _gather2_forward(name, state, after):
    send_a, recv_a, pieces, lands = state
    n = len(pieces)
    first = (_SIB,) + _ICI

    def body(*refs):
        src_refs, land_refs = refs[:n], refs[n:2 * n]
        s_a, r_a = refs[2 * n], refs[2 * n + 1]
        outs = refs[2 * n + 3:]
        send_b, recv_b, token = outs[0], outs[1], outs[-1]
        for kk, fl in enumerate(first):
            dev, lin = _flip(fl)
            for a in range(n):
                cp = pltpu.make_async_remote_copy(src_ref=src_refs[a], dst_ref=land_refs[a].at[lin],
                                                  send_sem=s_a.at[a * 4 + kk], recv_sem=r_a.at[a * 4 + kk],
                                                  device_id=dev, device_id_type=MESH_ID)
                cp.wait_send()
                cp.wait_recv()
        sib, _ = _flip(_SIB)
        for j, fl in enumerate(_ICI):
            _, lin = _flip(fl)
            for a in range(n):
                pltpu.make_async_remote_copy(src_ref=land_refs[a].at[lin], dst_ref=land_refs[a].at[lin],
                                             send_sem=send_b.at[a * 3 + j], recv_sem=recv_b.at[a * 3 + j],
                                             device_id=sib, device_id_type=MESH_ID).start()
        token[...] = jnp.zeros_like(token)

    hbm = [pltpu.HBM(tuple(a.shape), a.dtype) for a in pieces + lands]
    res = pl.pallas_call(
        body, name=name,
        out_shape=[pltpu.SemaphoreType.DMA((3 * n,)), pltpu.SemaphoreType.DMA((3 * n,))] + hbm
        + [jax.ShapeDtypeStruct((8, 128), F32)],
        in_specs=[_HBM] * (2 * n) + [_SEM, _SEM, _ANY],
        out_specs=[_SEM, _SEM] + [_HBM] * (2 * n) + [pl.BlockSpec(memory_space=pltpu.VMEM)],
        input_output_aliases={i: 2 + i for i in range(2 * n)},
        compiler_params=pltpu.CompilerParams(has_side_effects=_EFFECT),
    )(*pieces, *lands, send_a, recv_a, after)
    return (res[0], res[1], list(res[2 + n:2 + 2 * n])), res[-1]


def _gather2_wait(name, state, after):
    send_b, recv_b, lands = state
    n = len(lands)

    def body(*refs):
        land_refs = refs[:n]
        s_b, r_b = refs[n], refs[n + 1]
        sib, _ = _flip(_SIB)
        for j, fl in enumerate(_ICI):
            _, sent = _flip(fl)
            _, arriving = _flip((fl[0], fl[1], 1))
            for a in range(n):
                cp = pltpu.make_async_remote_copy(src_ref=land_refs[a].at[sent], dst_ref=land_refs[a].at[arriving],
                                                  send_sem=s_b.at[a * 3 + j], recv_sem=r_b.at[a * 3 + j],
                                                  device_id=sib, device_id_type=MESH_ID)
                cp.wait_send()
                cp.wait_recv()

    res = pl.pallas_call(
        body, name=name, out_shape=[pltpu.HBM(tuple(a.shape), a.dtype) for a in lands],
        in_specs=[_HBM] * n + [_SEM, _SEM, _ANY], out_specs=[_HBM] * n,
        input_output_aliases={i: i for i in range(n)},
        compiler_params=pltpu.CompilerParams(has_side_effects=_EFFECT),
    )(*lands, send_b, recv_b, after)
    return list(res)


def _adam_math(w, g, m, v):
    m = ADAM_B1 * m + (1.0 - ADAM_B1) * g
    v = ADAM_B2 * v + (1.0 - ADAM_B2) * (g * g)
    m_hat = m / (1.0 - ADAM_B1 ** ADAM_STEP)
    v_hat = v / (1.0 - ADAM_B2 ** ADAM_STEP)
    delta = -ADAM_LR * (m_hat / (jnp.sqrt(v_hat) + ADAM_EPS) + ADAM_WD * w)
    return delta, m, v


def _adam(name, w, m, v, parts):
    r, c = w.shape
    npart, _, cp = parts.shape
    tr = _pick(r, (256, 176, 128, 64, 16, 8, 1))

    def body(w_ref, m_ref, v_ref, p_ref, g_ref, d_ref, mo_ref, vo_ref):
        g = p_ref[0].astype(F32)
        for pp in range(1, npart):
            g = g + p_ref[pp].astype(F32)
        g = g[:, 0:c]
        delta, mn, vn = _adam_math(w_ref[...], g, m_ref[...], v_ref[...])
        g_ref[...] = g
        d_ref[...] = delta
        mo_ref[...] = mn
        vo_ref[...] = vn

    blk = pl.BlockSpec((tr, c), lambda i: (i, 0))
    out = jax.ShapeDtypeStruct((r, c), F32)
    return pl.pallas_call(
        body, name=name, grid=(r // tr,),
        in_specs=[blk, blk, blk, pl.BlockSpec((npart, tr, cp), lambda i: (0, i, 0))],
        out_specs=[blk, blk, blk, blk], out_shape=[out, out, out, out], compiler_params=_params(("parallel",)),
    )(w, m, v, parts)


def _adam_t(name, w_t, m_t, v_t, parts):
    c, r = w_t.shape
    npart, _, cp = parts.shape
    tr = _pick(r, (256, 128))

    def body(w_ref, m_ref, v_ref, p_ref, g_ref, d_ref, mo_ref, vo_ref):
        g = p_ref[0].astype(F32)
        for pp in range(1, npart):
            g = g + p_ref[pp].astype(F32)
        g = g.T[0:c, :]
        delta, mn, vn = _adam_math(w_ref[...], g, m_ref[...], v_ref[...])
        g_ref[...] = g
        d_ref[...] = delta
        mo_ref[...] = mn
        vo_ref[...] = vn

    blk = pl.BlockSpec((c, tr), lambda i: (0, i))
    out = jax.ShapeDtypeStruct((c, r), F32)
    return pl.pallas_call(
        body, name=name, grid=(r // tr,),
        in_specs=[blk, blk, blk, pl.BlockSpec((npart, tr, cp), lambda i: (0, i, 0))],
        out_specs=[blk, blk, blk, blk], out_shape=[out, out, out, out], compiler_params=_params(("parallel",)),
    )(w_t, m_t, v_t, parts)


def _small_sum(name, packs):
    def body(p_ref, o_ref):
        tot = p_ref[0]
        for pp in range(1, N_DEV):
            tot = tot + p_ref[pp]
        o_ref[0:8, :] = tot[0:8, :]
        o_ref[8:24, :] = tot[8:24, :] + tot[24:40, :]

    return pl.pallas_call(body, name=name, out_shape=jax.ShapeDtypeStruct((24, D_MODEL), F32),
                          compiler_params=_params())(packs)


def _local_step(x, tgt, g1, gm, g2, gf, b_forget, sinks, weights, send):
    b, s, _ = x.shape
    l = s + PREFIX
    t = b * l
    (meta,) = weights("meta", x)
    h0 = jnp.concatenate([jnp.zeros((b, N_PAD, D_MODEL), F32), jnp.broadcast_to(meta[None], (b, N_META, D_MODEL)), x],
                         axis=1).reshape(t, D_MODEL)

    n1 = _rms_fwd("rms1_fwd", h0, g1)
    (w1i,) = weights("ffn1_in", n1)
    gu1, a1 = _ffn_in_fwd("ffn1_in_fwd", n1, w1i)
    (w1o,) = weights("ffn1_out", weights("mix:forward", weights("ffn1_out:forward", a1)))
    h1 = _mm_nn("ffn1_out_fwd", a1, w1o, alpha=0.5, res=h0, tm_c=(544, 384, 256, 128), tn_c=(1024,))
    wi, wa, wb, wo = weights("mix", h1)
    um = _rms_fwd("rmsm_fwd", h1, gm)
    qkv = _mm_nn("proj_qkv_fwd", um, wi, out_dtype=BF16, tn_c=(768,), cols=(0, QKV_W))
    gates = _mm_nn("proj_gates_fwd", um, wi, tn_c=(1024,), cols=(P_GATES, 2 * D_MODEL), dep=weights("ffn2:forward", qkv))
    f2 = _mm_nn("proj_f_fwd", um, wi, tn_c=(128,), cols=(P_F, 128))
    qkv3 = qkv.reshape(b, l, QKV_W)
    f3 = f2.reshape(b, l, 128)
    bf_row = jnp.pad(b_forget, ((0, 0), (0, 128 - B_HEADS)))
    c_col, c_row = _fgate_fwd("fgate_fwd", f3, bf_row)
    head_of_row = jnp.arange(STACK) // BLOCK
    slopes = jnp.exp2(-8.0 * (head_of_row + 1).astype(F32) / A_HEADS).reshape(STACK, 1)
    sink_rows = jnp.repeat(sinks.reshape(A_HEADS), BLOCK).reshape(STACK, 1)
    swa_bias = _swa_bias(slopes)
    oa3, lse_a = _swa_fwd("swa_fwd", qkv3, slopes, sink_rows, swa_bias)
    ob3, lse_b = _fox_fwd("fox_fwd", qkv3, c_col, c_row)
    oa = oa3.reshape(t, A_WIDTH)
    ob = ob3.reshape(t, B_WIDTH)
    mixed, ya, yb = _branch_gate_fwd("branch_gate_fwd", oa, ob, wa, wb, gates)
    h2 = _mm_nn("mix_out_fwd", mixed, wo, res=h1, tn_c=(1024,))
    w2i, w2o = weights("ffn2", h2)
    n2 = _rms_fwd("rms2_fwd", h2, g2)
    gu2, a2 = _ffn_in_fwd("ffn2_in_fwd", n2, w2i)
    h3 = _mm_nn("ffn2_out_fwd", a2, w2o, alpha=0.5, res=h2, tm_c=(544, 384, 256, 128), tn_c=(1024,))

    dh3_3, loss_blk, dgf = _loss_head("loss_head", h3.reshape(b, l, D_MODEL), gf, tgt)
    dh3 = dh3_3.reshape(t, D_MODEL)

    def ffn_bwd(tag, dh, h_in, g_norm, n_in, gu, a, w_in_blk, w_out):
        dw_out = _mm_tn(tag + "_out_bwd_w", a, dh, alpha=0.5)
        dgu = _ffn_out_bwd_x(tag + "_out_bwd_x", dh, w_out, gu, dep=send(tag + "_out", (dw_out,)))
        dw_in = _ffn_in_bwd_w(tag + "_in_bwd_w", n_in, dgu)
        return _ffn_in_bwd_x(tag + "_in_bwd_x", dgu, w_in_blk, h_in, g_norm, dh, dep=send(tag + "_in", (dw_in,)))

    dh2, dg2 = ffn_bwd("ffn2", dh3, h2, g2, n2, gu2, a2, w2i, w2o)

    dwo = _mm_tn("mix_out_bwd_w", mixed, dh2)
    dya, dyb, dgates = _mix_out_gate_bwd("mix_out_gate_bwd", dh2, wo, gates, ya, yb)
    doa = _mm_nt("branch_a_bwd_x", dya, wa, tn_c=(512,))
    dob = _mm_nt("branch_b_bwd_x", dyb, wb, out_dtype=BF16, tn_c=(512,))
    dwa = _mm_tn("branch_a_bwd_w", oa, dya, tm_c=(512,))
    dwb = _mm_tn("branch_b_bwd_w", ob, dyb, tm_c=(512,))
    dqkv3, dcq, dck = _fox_bwd("fox_bwd", qkv3, c_col, c_row, ob3, lse_b, dob.reshape(b, l, B_WIDTH))
    dqkv3, dsink = _swa_bwd("swa_bwd", qkv3, oa3, lse_a, doa.reshape(b, l, A_WIDTH), slopes, sink_rows, swa_bias, dqkv3)
    dqkv = dqkv3.reshape(t, QKV_W)
    df3, dbf = _fgate_bwd("fgate_bwd", f3, bf_row, dcq, dck)
    df = df3.reshape(t, 128)
    dwi_qkv = _mm_tn("proj_qkv_bwd_w", um, dqkv, tm_c=(512,), tn_c=(768,))
    dwi_g = _mm_tn("proj_gates_bwd_w", um, dgates, tm_c=(512,), tn_c=(512,))
    dwi_f = _mm_tn("proj_f_bwd_w", um, df, tm_c=(512,), tn_c=(128,))
    token = send("mix", (dwi_qkv, dwi_g, dwi_f, dwa, dwb, dwo))
    dh1, dgm = _proj_bwd_x("proj_bwd_x", dqkv, dgates, df, wi, h1, gm, dh2, dep=token)

    dh0, dg1 = ffn_bwd("ffn1", dh1, h0, g1, n1, gu1, a1, w1i, w1o)
    dh0_3 = dh0.reshape(b, l, D_MODEL)
    grad_x = dh0_3[:, PREFIX:, :]
    dmeta = dh0_3[:, N_PAD:PREFIX, :].reshape(b * N_META, D_MODEL)

    misc = jnp.concatenate([dbf[:, 0:B_HEADS], dsink[:, 0].reshape(1, A_HEADS), loss_blk[0:1, 0:1]], axis=1)
    misc = jnp.pad(misc, ((0, 0), (0, D_MODEL - misc.shape[1])))
    row = lax.broadcasted_iota(jnp.int32, (8, D_MODEL), 0)
    vec = jnp.zeros((8, D_MODEL), F32)
    for i, piece in enumerate((dg1, dgm, dg2, dgf, misc)):
        vec = jnp.where(row == i, piece, vec)
    small = jnp.concatenate([vec, dmeta], axis=0)
    return grad_x, small


def _pad_to(a, rows, cols):
    return jnp.pad(a, ((0, rows - a.shape[0]), (0, cols - a.shape[1])))


def _ffn_out_from_gathered(g):
    w = g.reshape(4, FF_SHARD, D_MODEL)
    return jnp.pad(w, ((0, 0), (0, FF_SHARD_P - FF_SHARD), (0, 0))).reshape(D_FF_P, D_MODEL)


def _ffn_out_to_scatter(dw):
    return dw.reshape(4, FF_SHARD_P, D_MODEL)[:, 0:FF_SHARD, :].reshape(N_DEV, FFO_SHARD, D_MODEL)


def _proj_segments():
    segs = [(HEAD_DIM * h, HEAD_DIM, 0, B_SEG + HEAD_DIM * A_HEAD_ORDER.index(h)) for h in range(A_HEADS)]
    segs += [(512, 128, 0, B_SEG + A_WIDTH), (640, 128, 0, B_SEG + A_WIDTH + 128)]
    for first, off in ((768, 0), (1280, 128), (1792, 256)):
        segs += [(first + 128 * hp, 128, 0, PAIR_W * hp + off) for hp in range(4)]
    segs += [(2304, B_HEADS, 2, 0), (2312, 2 * D_MODEL, 1, 0)]
    return segs


def _proj_from_gathered(g):
    def cols(first, width):
        out = []
        for p in range(N_DEV):
            lo, hi = max(first, WIN_SHARD * p), min(first + width, WIN_SHARD * (p + 1))
            if lo < hi:
                out.append(g[p, :, lo - WIN_SHARD * p:hi - WIN_SHARD * p])
        return out

    parts = []
    for arr in (0, 1, 2):
        for first, width, _, _ in sorted((s for s in _proj_segments() if s[2] == arr), key=lambda s: s[3]):
            parts += cols(first, width)
        if arr == 0:
            parts.append(jnp.zeros((D_MODEL, P_GATES - QKV_W), g.dtype))
    parts.append(jnp.zeros((D_MODEL, 128 - B_HEADS), g.dtype))
    return jnp.concatenate(parts, axis=1)


def _proj_to_scatter(dqkv_w, dg_w, df_w):
    arrays = (dqkv_w, dg_w, df_w)
    segs = sorted(_proj_segments())
    blocks = []
    for p in range(N_DEV):
        parts = []
        for first, width, arr, at in segs:
            lo, hi = max(first, WIN_SHARD * p), min(first + width, WIN_SHARD * (p + 1))
            if lo < hi:
                parts.append(arrays[arr][:, at + lo - first:at + hi - first])
        parts.append(jnp.zeros((D_MODEL, WIN_SHARD_P - WIN_SHARD), dqkv_w.dtype))
        blocks.append(jnp.concatenate(parts, axis=1))
    return jnp.stack(blocks, axis=0)


def _a_rows_from_natural(w):
    return jnp.concatenate([w[HEAD_DIM * h:HEAD_DIM * (h + 1)] for h in A_HEAD_ORDER], axis=0)


def _a_rows_to_natural(w):
    return jnp.concatenate([w[HEAD_DIM * A_HEAD_ORDER.index(h):HEAD_DIM * (A_HEAD_ORDER.index(h) + 1)]
                            for h in range(A_HEADS)], axis=0)


def kernel(x, meta_tokens, ffn1_norm, ffn1_w_in, ffn1_w_out, mix_norm, w_in, b_forget, attn_sinks, w_branch_a, w_branch_b, w_out, ffn2_norm, ffn2_w_in, ffn2_w_out, final_norm, loss_target, m_meta_tokens, m_ffn1_norm, m_ffn1_w_in, m_ffn1_w_out, m_mix_norm, m_w_in, m_b_forget, m_attn_sinks, m_w_branch_a, m_w_branch_b, m_w_out, m_ffn2_norm, m_ffn2_w_in, m_ffn2_w_out, m_final_norm, v_meta_tokens, v_ffn1_norm, v_ffn1_w_in, v_ffn1_w_out, v_mix_norm, v_w_in, v_b_forget, v_attn_sinks, v_w_branch_a, v_w_branch_b, v_w_out, v_ffn2_norm, v_ffn2_w_in, v_ffn2_w_out, v_final_norm):
    me = 4 * lax.axis_index("x") + 2 * lax.axis_index("y") + lax.axis_index("c")

    shards = (
        _pad_to(ffn1_w_in[0].astype(BF16), D_MODEL, FF_SHARD_P),
        ffn1_w_out[0].astype(BF16),
        _pad_to(w_in[0].astype(BF16), D_MODEL, WIN_SHARD_P),
        w_branch_a[0].astype(BF16), w_branch_b[0].astype(BF16), w_out[0].astype(BF16),
        _pad_to(ffn2_w_in[0].astype(BF16), D_MODEL, FF_SHARD_P),
        ffn2_w_out[0].astype(BF16),
        meta_tokens,
    )
    s1i, s1o, swi, swa, swb, swo, s2i, s2o, smeta = shards
    first_level, second_level = {}, {}
    first_level["meta"], tok = _gather2_start("gather_meta_start", (smeta,))
    first_level["ffn1_in"], tok = _gather2_start("gather_ffn1_in_start", (s1i,), after=tok)
    first_level["ffn1_out"], tok = _gather2_start("gather_ffn1_out_start", (s1o,), after=tok)
    first_level["mix"], tok = _gather2_start("gather_mix_start", (swi, swa, swb, swo), after=tok)
    first_level["ffn2"], tok = _gather2_start("gather_ffn2_start", (s2i, s2o), after=tok)
    started = {"tok": tok}

    def weights(group, after):
        if group.endswith(":forward"):
            group = group[:-len(":forward")]
            second_level[group], token = _gather2_forward("gather_" + group + "_forward", first_level[group], after)
            return token
        if group == "meta":
            after = weights("meta:forward", started["tok"])
        if group == "ffn1_in":
            after = weights("ffn1_in:forward", after)
        got = _gather2_wait("gather_" + group + "_wait", second_level[group], after)
        if group == "mix":
            gwi, gwa, gwb, gwo = got
            return (_proj_from_gathered(gwi), _a_rows_from_natural(gwa.transpose(1, 0, 2).reshape(A_WIDTH, D_MODEL)),
                    gwb.transpose(1, 0, 2).reshape(B_WIDTH, D_MODEL), gwo.reshape(D_MODEL, D_MODEL))
        if group == "meta":
            return (got[0].transpose(1, 0, 2).reshape(N_META, D_MODEL),)
        if group == "ffn1_in":
            return (got[0].reshape(2, 4, D_MODEL, FF_SHARD_P),)
        if group == "ffn1_out":
            return (_ffn_out_from_gathered(got[0]),)
        return got[0].reshape(2, 4, D_MODEL, FF_SHARD_P), _ffn_out_from_gathered(got[1])

    scatter_state = {}

    def send(group, grads):
        if group == "mix":
            dwi_qkv, dwi_g, dwi_f, dwa, dwb, dwo = grads
            dwa = _a_rows_to_natural(dwa)
            blocks = (_proj_to_scatter(dwi_qkv, dwi_g, dwi_f), dwa.reshape(A_WIDTH, N_DEV, 128).transpose(1, 0, 2),
                      dwb.reshape(B_WIDTH, N_DEV, 128).transpose(1, 0, 2), dwo.reshape(N_DEV, 128, D_MODEL))
        elif group.endswith("_in"):
            blocks = (grads[0].reshape(N_DEV, D_MODEL, FF_SHARD_P),)
        else:
            blocks = (_ffn_out_to_scatter(grads[0]),)
        scatter_state[group], token = _xchg_start("scatter_" + group + "_start", (), blocks)
        return token

    gf = final_norm.reshape(1, D_MODEL)
    grad_x, small = _local_step(x, loss_target, ffn1_norm, mix_norm, ffn2_norm, gf, b_forget, attn_sinks, weights, send)

    small_state, after = _xchg_start("gather_small_start", (small,), ())
    out = {}
    updates = (
        ("ffn2_out", (("ffn2_w_out", ffn2_w_out, m_ffn2_w_out, v_ffn2_w_out),)),
        ("ffn2_in", (("ffn2_w_in", ffn2_w_in, m_ffn2_w_in, v_ffn2_w_in),)),
        ("ffn1_out", (("ffn1_w_out", ffn1_w_out, m_ffn1_w_out, v_ffn1_w_out),)),
        ("mix", (("w_in", w_in, m_w_in, v_w_in), ("w_branch_a", w_branch_a, m_w_branch_a, v_w_branch_a),
                 ("w_branch_b", w_branch_b, m_w_branch_b, v_w_branch_b), ("w_out", w_out, m_w_out, v_w_out))),
    )
    last_update = ("ffn1_in", (("ffn1_w_in", ffn1_w_in, m_ffn1_w_in, v_ffn1_w_in),))

    def update(group, members, after):
        parts_list = _xchg_wait("scatter_" + group + "_wait", scatter_state[group], after)
        for (nm, w, m, v), parts in zip(members, parts_list):
            if nm.endswith("w_in"):
                res4 = _adam_t("adam_" + nm, w[0].T, m[0].T, v[0].T, parts)
                out[nm] = tuple(r.T[None] for r in res4)
            else:
                res4 = _adam("adam_" + nm, w[0], m[0], v[0], parts)
                out[nm] = tuple(r[None] for r in res4)
            after = res4[0]
        return after

    for group, members in updates + (last_update,):
        after = update(group, members, after)
    (packs,) = _xchg_wait("gather_small_wait", small_state, after)

    tot = _small_sum("small_sum", packs)
    loss = tot[4, 2 * B_HEADS]
    g_meta = lax.dynamic_slice(tot[8:24, :], (0, me * 128), (N_META, 128))
    out["meta_tokens"] = tuple(_adam("adam_meta_tokens", meta_tokens, m_meta_tokens, v_meta_tokens, g_meta[None]))

    def pack_small(n1, nm, n2, nf, bfv, skv):
        misc = jnp.pad(jnp.concatenate([bfv, skv], axis=1), ((0, 0), (0, D_MODEL - 2 * B_HEADS)))
        row = lax.broadcasted_iota(jnp.int32, (8, D_MODEL), 0)
        vec = jnp.zeros((8, D_MODEL), F32)
        for i, piece in enumerate((n1, nm, n2, nf.reshape(1, D_MODEL), misc)):
            vec = jnp.where(row == i, piece, vec)
        return vec

    w_pack = pack_small(ffn1_norm, mix_norm, ffn2_norm, final_norm, b_forget, attn_sinks)
    m_pack = pack_small(m_ffn1_norm, m_mix_norm, m_ffn2_norm, m_final_norm, m_b_forget, m_attn_sinks)
    v_pack = pack_small(v_ffn1_norm, v_mix_norm, v_ffn2_norm, v_final_norm, v_b_forget, v_attn_sinks)
    small4 = _adam("adam_small", w_pack, m_pack, v_pack, tot[0:8][None])
    for i, nm in enumerate(("ffn1_norm", "mix_norm", "ffn2_norm")):
        out[nm] = tuple(r[i:i + 1] for r in small4)
    out["final_norm"] = tuple(r[3] for r in small4)
    out["b_forget"] = tuple(r[4:5, 0:B_HEADS] for r in small4)
    out["attn_sinks"] = tuple(r[4:5, B_HEADS:2 * B_HEADS] for r in small4)

    names = ("meta_tokens", "ffn1_norm", "ffn1_w_in", "ffn1_w_out", "mix_norm", "w_in", "b_forget", "attn_sinks",
             "w_branch_a", "w_branch_b", "w_out", "ffn2_norm", "ffn2_w_in", "ffn2_w_out", "final_norm")
    return (loss, grad_x) + tuple(out[nm][kind] for kind in range(4) for nm in names)
```

```python
import jax
import jax.numpy as jnp
from jax import lax
from jax.experimental import pallas as pl
from jax.experimental.pallas import tpu as pltpu

F32 = jnp.float32
BF16 = jnp.bfloat16

D_MODEL = 1024
N_META = 16
BLOCK = 128
PREFIX = 128
N_PAD = PREFIX - N_META
HEAD_DIM = 64
A_HEADS = 8
A_KV_HEADS = 2
A_GROUP = 4
B_HEADS = 8
A_WIDTH = 512
A_KV_WIDTH = 128
B_WIDTH = 512
D_FF = 2816
N_DEV = 8
FF_SHARD = 2 * D_FF // N_DEV
FF_SHARD_P = 768
FFO_SHARD = D_FF // N_DEV
FFO_SHARD_P = FF_SHARD_P // 2
D_FF_P = 4 * FF_SHARD_P
W_IN_COLS = 4360
WIN_SHARD = W_IN_COLS // N_DEV
WIN_SHARD_P = 640
PAIR_W = 3 * 128
B_SEG = 4 * PAIR_W
A_SEG = A_WIDTH + 2 * A_KV_WIDTH
QKV_W = B_SEG + A_SEG
P_GATES = 2 * (2 * D_MODEL)
P_F = P_GATES + 2 * D_MODEL
PROJ_P = P_F + 128
A_HEAD_ORDER = (0, 4, 1, 5, 2, 6, 3, 7)
EPS = 1e-6
NEG = -1e30
SCALE = HEAD_DIM ** -0.5
ADAM_LR = 0.001
ADAM_B1 = 0.9
ADAM_B2 = 0.999
ADAM_EPS = 1e-08
ADAM_WD = 0.01
ADAM_STEP = 10
VMEM_LIMIT = 56 * 1024 * 1024
MESH_ID = pl.DeviceIdType.MESH
SMALL_ROWS = 40

_NN = (((1,), (0,)), ((), ()))
_NT = (((1,), (1,)), ((), ()))
_TN = (((0,), (0,)), ((), ()))


def _params(sem=None):
    return pltpu.CompilerParams(dimension_semantics=sem, vmem_limit_bytes=VMEM_LIMIT)


def _pick(n, cands):
    for c in cands:
        if n % c == 0:
            return c
    raise ValueError(f"no tile for {n}")


def _bf(v):
    return v if v.dtype == BF16 else v.astype(BF16)


def _mm(name, a, b, dims, grid, a_spec, b_spec, o_spec, out_shape, out_dtype, acc_shape, k_axis=None, nk=1,
        alpha=1.0, res=None, res_spec=None, dep=None):
    has_res = res is not None
    has_dep = dep is not None

    def body(*refs):
        a_ref, b_ref = refs[0], refs[1]
        r_ref = refs[2] if has_res else None
        o_ref = refs[2 + has_res + has_dep]

        def finish(acc):
            if alpha != 1.0:
                acc = acc * alpha
            if has_res:
                acc = acc + r_ref[...]
            o_ref[...] = acc.astype(o_ref.dtype)

        part = lax.dot_general(_bf(a_ref[...]), _bf(b_ref[...]), dims, preferred_element_type=F32)
        if nk == 1:
            finish(part)
        else:
            acc_ref = refs[-1]
            k = pl.program_id(k_axis)

            @pl.when(k == 0)
            def _():
                acc_ref[...] = part

            @pl.when(k > 0)
            def _():
                acc_ref[...] += part

            @pl.when(k == nk - 1)
            def _():
                finish(acc_ref[...])

    in_specs = [a_spec, b_spec] + ([res_spec] if has_res else []) + ([pl.BlockSpec(memory_space=pl.ANY)] if has_dep else [])
    args = (a, b) + ((res,) if has_res else ()) + ((dep,) if has_dep else ())
    sem = tuple("arbitrary" if (nk > 1 and i == k_axis) else "parallel" for i in range(len(grid)))
    return pl.pallas_call(
        body, name=name, grid=grid, in_specs=in_specs, out_specs=o_spec,
        out_shape=jax.ShapeDtypeStruct(out_shape, out_dtype),
        scratch_shapes=[pltpu.VMEM(acc_shape, F32)] if nk > 1 else [],
        compiler_params=_params(sem),
    )(*args)


def _mm_res_norm(name, a, w, res, g_next, alpha=1.0):
    t, k = a.shape
    tm = _pick(t, (544, 384, 256, 128))

    def body(a_ref, w_ref, r_ref, g_ref, h_ref, n_ref):
        acc = lax.dot_general(_bf(a_ref[...]), w_ref[...], _NN, preferred_element_type=F32)
        if alpha != 1.0:
            acc = acc * alpha
        hv = acc + r_ref[...]
        h_ref[...] = hv
        r = lax.rsqrt(jnp.mean(hv * hv, axis=-1, keepdims=True) + EPS)
        n_ref[...] = ((hv * r) * g_ref[...]).astype(BF16)

    row = pl.BlockSpec((tm, D_MODEL), lambda i: (i, 0))
    return pl.pallas_call(
        body, name=name, grid=(t // tm,),
        in_specs=[pl.BlockSpec((tm, k), lambda i: (i, 0)), pl.BlockSpec((k, D_MODEL), lambda i: (0, 0)), row,
                  pl.BlockSpec((1, D_MODEL), lambda i: (0, 0))],
        out_specs=[row, row],
        out_shape=[jax.ShapeDtypeStruct((t, D_MODEL), F32), jax.ShapeDtypeStruct((t, D_MODEL), BF16)],
        compiler_params=_params(("parallel",)),
    )(a, w, res, g_next)


def _mm_nn(name, a, b, out_dtype=F32, alpha=1.0, res=None, tn_c=(512, 640, 256, 128), cols=None,
           tm_c=(1088, 768, 512, 256, 128), dep=None):
    t, k = a.shape
    c0, n = (0, b.shape[1]) if cols is None else cols
    tm = _pick(t, tm_c)
    tn = _pick(n, tn_c)
    assert c0 % tn == 0
    jb = c0 // tn
    return _mm(name, a, b, _NN, (t // tm, n // tn),
               pl.BlockSpec((tm, k), lambda i, j: (i, 0)), pl.BlockSpec((k, tn), lambda i, j: (0, jb + j)),
               pl.BlockSpec((tm, tn), lambda i, j: (i, j)), (t, n), out_dtype, None,
               alpha=alpha, res=res, res_spec=pl.BlockSpec((tm, tn), lambda i, j: (i, j)), dep=dep)


def _mm_nt(name, a, b, out_dtype=F32, alpha=1.0, tn_c=(768, 512, 256, 128), tk_c=None, dep=None, res=None, kcols=None):
    t, k = a.shape
    n = b.shape[0]
    c0 = 0 if kcols is None else kcols[0]
    tm = _pick(t, (1088, 768, 512, 256, 128))
    tn = _pick(n, tn_c)
    tk = k if tk_c is None else _pick(k, tk_c)
    nk = k // tk
    assert c0 % tk == 0
    kb = c0 // tk
    return _mm(name, a, b, _NT, (t // tm, n // tn, nk),
               pl.BlockSpec((tm, tk), lambda i, j, kk: (i, kk)), pl.BlockSpec((tn, tk), lambda i, j, kk: (j, kb + kk)),
               pl.BlockSpec((tm, tn), lambda i, j, kk: (i, j)), (t, n), out_dtype, (tm, tn), k_axis=2, nk=nk, alpha=alpha,
               dep=dep, res=res, res_spec=pl.BlockSpec((tm, tn), lambda i, j, kk: (i, j)))


def _mm_tn(name, a, b, out_dtype=BF16, alpha=1.0, tm_c=(768, 512, 256, 128), tn_c=(512, 640, 256, 128)):
    t, m = a.shape
    n = b.shape[1]
    tm = _pick(m, tm_c)
    tn = _pick(n, tn_c)
    bytes_a, bytes_b = a.size * a.dtype.itemsize, b.size * b.dtype.itemsize
    if bytes_a + bytes_b * (m // tm) <= bytes_b + bytes_a * (n // tn):
        return _mm(name, a, b, _TN, (m // tm, n // tn),
                   pl.BlockSpec((t, tm), lambda i, j: (0, i)), pl.BlockSpec((t, tn), lambda i, j: (0, j)),
                   pl.BlockSpec((tm, tn), lambda i, j: (i, j)), (m, n), out_dtype, None, alpha=alpha)
    return _mm(name, a, b, _TN, (n // tn, m // tm),
               pl.BlockSpec((t, tm), lambda j, i: (0, i)), pl.BlockSpec((t, tn), lambda j, i: (0, j)),
               pl.BlockSpec((tm, tn), lambda j, i: (i, j)), (m, n), out_dtype, None, alpha=alpha)


def _silu_grads(g, u, d):
    sg = jax.nn.sigmoid(g)
    return d * u * (sg * (1.0 + g * (1.0 - sg))), d * (g * sg)


def _ffn_in_fwd(name, n, wblk, dep=None):
    t = n.shape[0]
    tm = _pick(t, (1088, 768, 512, 256, 128))
    has_dep = dep is not None

    def body(n_ref, w_ref, *rest):
        gu_ref, a_ref = rest[-2], rest[-1]
        nv = n_ref[...]
        g = lax.dot_general(nv, w_ref[0], _NN, preferred_element_type=F32)
        u = lax.dot_general(nv, w_ref[1], _NN, preferred_element_type=F32)
        gu_ref[0] = g.astype(BF16)
        gu_ref[1] = u.astype(BF16)
        a_ref[...] = ((g * jax.nn.sigmoid(g)) * u).astype(BF16)

    return pl.pallas_call(
        body, name=name, grid=(t // tm, 4),
        in_specs=[pl.BlockSpec((tm, D_MODEL), lambda i, j: (i, 0)),
                  pl.BlockSpec((2, None, D_MODEL, FF_SHARD_P), lambda i, j: (0, j, 0, 0))]
        + ([pl.BlockSpec(memory_space=pl.ANY)] if has_dep else []),
        out_specs=[pl.BlockSpec((2, tm, FF_SHARD_P), lambda i, j: (0, i, j)),
                   pl.BlockSpec((tm, FF_SHARD_P), lambda i, j: (i, j))],
        out_shape=[jax.ShapeDtypeStruct((2, t, D_FF_P), BF16), jax.ShapeDtypeStruct((t, D_FF_P), BF16)],
        compiler_params=_params(("parallel", "parallel")),
    )(*((n, wblk) + ((dep,) if has_dep else ())))


def _ffn_out_bwd_x(name, dh, w_out, gu, dep=None):
    t = dh.shape[0]
    tm = _pick(t, (1088, 768, 512, 256, 128))
    has_dep = dep is not None

    def body(dh_ref, w_ref, gu_ref, *rest):
        o_ref = rest[-1]
        da = lax.dot_general(_bf(dh_ref[...]), w_ref[...], _NT, preferred_element_type=F32) * 0.5
        dg, du = _silu_grads(gu_ref[0].astype(F32), gu_ref[1].astype(F32), da)
        o_ref[0] = dg.astype(BF16)
        o_ref[1] = du.astype(BF16)

    gu_spec = pl.BlockSpec((2, tm, FF_SHARD_P), lambda i, j: (0, i, j))
    return pl.pallas_call(
        body, name=name, grid=(t // tm, 4),
        in_specs=[pl.BlockSpec((tm, D_MODEL), lambda i, j: (i, 0)), pl.BlockSpec((FF_SHARD_P, D_MODEL), lambda i, j: (j, 0)),
                  gu_spec] + ([pl.BlockSpec(memory_space=pl.ANY)] if has_dep else []),
        out_specs=gu_spec, out_shape=jax.ShapeDtypeStruct((2, t, D_FF_P), BF16),
        compiler_params=_params(("parallel", "parallel")),
    )(*((dh, w_out, gu) + ((dep,) if has_dep else ())))


def _rms_bwd_rows(dn, h, g, dres):
    r = lax.rsqrt(jnp.mean(h * h, axis=-1, keepdims=True) + EPS)
    tv = dn * g
    dot = jnp.mean(tv * h, axis=-1, keepdims=True)
    return dres + (r * tv - h * (r * r * r * dot)), jnp.sum(dn * (h * r), axis=0, keepdims=True)


def _accumulate_rows(ref, part, first):
    @pl.when(first)
    def _():
        ref[...] = part

    @pl.when(jnp.logical_not(first))
    def _():
        ref[...] += part


def _ffn_in_bwd_x(name, dgu, wblk, h_in, g_norm, dres, dep=None):
    t = dgu.shape[1]
    tm = _pick(t, (272, 256, 128))
    has_dep = dep is not None

    def body(d_ref, w_ref, h_ref, g_ref, r_ref, *rest):
        dh_ref, dg_ref = rest[-2], rest[-1]
        acc = None
        for s in range(2):
            for j in range(4):
                part = lax.dot_general(d_ref[s, :, FF_SHARD_P * j:FF_SHARD_P * (j + 1)], w_ref[s, j], _NT,
                                       preferred_element_type=F32)
                acc = part if acc is None else acc + part
        dh, dg = _rms_bwd_rows(acc, h_ref[...], g_ref[...], r_ref[...])
        dh_ref[...] = dh
        _accumulate_rows(dg_ref, dg, pl.program_id(0) == 0)

    row = pl.BlockSpec((tm, D_MODEL), lambda i: (i, 0))
    vec = pl.BlockSpec((1, D_MODEL), lambda i: (0, 0))
    return pl.pallas_call(
        body, name=name, grid=(t // tm,),
        in_specs=[pl.BlockSpec((2, tm, D_FF_P), lambda i: (0, i, 0)),
                  pl.BlockSpec((2, 4, D_MODEL, FF_SHARD_P), lambda i: (0, 0, 0, 0)), row, vec, row]
        + ([pl.BlockSpec(memory_space=pl.ANY)] if has_dep else []),
        out_specs=[row, vec],
        out_shape=[jax.ShapeDtypeStruct((t, D_MODEL), F32), jax.ShapeDtypeStruct((1, D_MODEL), F32)],
        compiler_params=_params(("arbitrary",)),
    )(*((dgu, wblk, h_in, g_norm, dres) + ((dep,) if has_dep else ())))


def _proj_bwd_x(name, dqkv, dgates, df, wi, h_in, g_norm, dres, dep=None):
    t = dqkv.shape[0]
    tm = _pick(t, (272, 256, 128))
    has_dep = dep is not None

    def body(q_ref, gt_ref, f_ref, w_ref, h_ref, g_ref, r_ref, *rest):
        dh_ref, dg_ref = rest[-2], rest[-1]
        acc = lax.dot_general(q_ref[...], w_ref[:, 0:QKV_W], _NT, preferred_element_type=F32)
        acc = acc + lax.dot_general(gt_ref[...], w_ref[:, P_GATES:P_F], _NT, preferred_element_type=F32)
        acc = acc + lax.dot_general(f_ref[...], w_ref[:, P_F:PROJ_P], _NT, preferred_element_type=F32)
        dh, dg = _rms_bwd_rows(acc, h_ref[...], g_ref[...], r_ref[...])
        dh_ref[...] = dh
        _accumulate_rows(dg_ref, dg, pl.program_id(0) == 0)

    row = pl.BlockSpec((tm, D_MODEL), lambda i: (i, 0))
    vec = pl.BlockSpec((1, D_MODEL), lambda i: (0, 0))
    return pl.pallas_call(
        body, name=name, grid=(t // tm,),
        in_specs=[pl.BlockSpec((tm, QKV_W), lambda i: (i, 0)), pl.BlockSpec((tm, 2 * D_MODEL), lambda i: (i, 0)),
                  pl.BlockSpec((tm, 128), lambda i: (i, 0)), pl.BlockSpec((D_MODEL, PROJ_P), lambda i: (0, 0)), row, vec, row]
        + ([pl.BlockSpec(memory_space=pl.ANY)] if has_dep else []),
        out_specs=[row, vec],
        out_shape=[jax.ShapeDtypeStruct((t, D_MODEL), F32), jax.ShapeDtypeStruct((1, D_MODEL), F32)],
        compiler_params=_params(("arbitrary",)),
    )(*((dqkv, dgates, df, wi, h_in, g_norm, dres) + ((dep,) if has_dep else ())))


def _ffn_in_bwd_w(name, n, dgu):
    t = n.shape[0]
    tk = t
    nk = 1
    return _mm(name, n, dgu, _TN, (2, 4, nk),
               pl.BlockSpec((tk, D_MODEL), lambda s, j, kk: (kk, 0)),
               pl.BlockSpec((None, tk, FF_SHARD_P), lambda s, j, kk: (s, kk, j)),
               pl.BlockSpec((None, None, D_MODEL, FF_SHARD_P), lambda s, j, kk: (s, j, 0, 0)),
               (2, 4, D_MODEL, FF_SHARD_P), BF16, (D_MODEL, FF_SHARD_P), k_axis=2, nk=nk)


def _rms_fwd(name, h, g):
    t = h.shape[0]
    tm = _pick(t, (544, 384, 256, 128))

    def body(h_ref, g_ref, o_ref):
        hv = h_ref[...]
        r = lax.rsqrt(jnp.mean(hv * hv, axis=-1, keepdims=True) + EPS)
        o_ref[...] = ((hv * r) * g_ref[...]).astype(BF16)

    return pl.pallas_call(
        body, name=name, grid=(t // tm,),
        in_specs=[pl.BlockSpec((tm, D_MODEL), lambda i: (i, 0)), pl.BlockSpec((1, D_MODEL), lambda i: (0, 0))],
        out_specs=pl.BlockSpec((tm, D_MODEL), lambda i: (i, 0)),
        out_shape=jax.ShapeDtypeStruct((t, D_MODEL), BF16), compiler_params=_params(("parallel",)),
    )(h, g)


def _rms_bwd(name, h, g, dn, dres):
    t = h.shape[0]
    tm = _pick(t, (544, 384, 256, 128))

    def body(h_ref, g_ref, dn_ref, dres_ref, dh_ref, dg_ref):
        i = pl.program_id(0)
        hv = h_ref[...]
        dnv = dn_ref[...]
        r = lax.rsqrt(jnp.mean(hv * hv, axis=-1, keepdims=True) + EPS)
        tv = dnv * g_ref[...]
        dot = jnp.mean(tv * hv, axis=-1, keepdims=True)
        dh_ref[...] = dres_ref[...] + (r * tv - hv * (r * r * r * dot))
        part = jnp.sum(dnv * (hv * r), axis=0, keepdims=True)

        @pl.when(i == 0)
        def _():
            dg_ref[...] = part

        @pl.when(i > 0)
        def _():
            dg_ref[...] += part

    row = pl.BlockSpec((tm, D_MODEL), lambda i: (i, 0))
    vec = pl.BlockSpec((1, D_MODEL), lambda i: (0, 0))
    return pl.pallas_call(
        body, name=name, grid=(t // tm,), in_specs=[row, vec, row, row], out_specs=[row, vec],
        out_shape=[jax.ShapeDtypeStruct((t, D_MODEL), F32), jax.ShapeDtypeStruct((1, D_MODEL), F32)],
        compiler_params=_params(("arbitrary",)),
    )(h, g, dn, dres)


def _branch_gate_fwd(name, oa, ob, wa, wb, gates):
    t = gates.shape[0]
    tm = _pick(t, (544, 384, 256, 128))

    def body(oa_ref, ob_ref, wa_ref, wb_ref, g_ref, o_ref, ya_ref, yb_ref):
        ya = lax.dot_general(_bf(oa_ref[...]), wa_ref[...], _NN, preferred_element_type=F32)
        yb = lax.dot_general(_bf(ob_ref[...]), wb_ref[...], _NN, preferred_element_type=F32)
        sa = jax.nn.sigmoid(g_ref[:, 0:D_MODEL])
        sb = jax.nn.sigmoid(g_ref[:, D_MODEL:2 * D_MODEL])
        o_ref[...] = (sa * ya + sb * yb).astype(BF16)
        ya_ref[...] = ya.astype(BF16)
        yb_ref[...] = yb.astype(BF16)

    blk = pl.BlockSpec((tm, D_MODEL), lambda i: (i, 0))
    narrow = pl.BlockSpec((tm, A_WIDTH), lambda i: (i, 0))
    wide = pl.BlockSpec((tm, 2 * D_MODEL), lambda i: (i, 0))
    wspec = pl.BlockSpec((A_WIDTH, D_MODEL), lambda i: (0, 0))
    out = jax.ShapeDtypeStruct((t, D_MODEL), BF16)
    return pl.pallas_call(
        body, name=name, grid=(t // tm,), in_specs=[narrow, narrow, wspec, wspec, wide], out_specs=[blk, blk, blk],
        out_shape=[out, out, out], compiler_params=_params(("parallel",)),
    )(oa, ob, wa, wb, gates)


def _mix_out_gate_bwd(name, dh, wo, gates, ya, yb):
    t = gates.shape[0]
    tm = _pick(t, (544, 384, 256, 128))

    def body(dh_ref, w_ref, g_ref, ya_ref, yb_ref, dya_ref, dyb_ref, dg_ref):
        dm = lax.dot_general(_bf(dh_ref[...]), w_ref[...], _NT, preferred_element_type=F32)
        sa = jax.nn.sigmoid(g_ref[:, 0:D_MODEL])
        sb = jax.nn.sigmoid(g_ref[:, D_MODEL:2 * D_MODEL])
        dya_ref[...] = (dm * sa).astype(BF16)
        dyb_ref[...] = (dm * sb).astype(BF16)
        dg_ref[:, 0:D_MODEL] = (dm * ya_ref[...].astype(F32) * (sa * (1.0 - sa))).astype(BF16)
        dg_ref[:, D_MODEL:2 * D_MODEL] = (dm * yb_ref[...].astype(F32) * (sb * (1.0 - sb))).astype(BF16)

    blk = pl.BlockSpec((tm, D_MODEL), lambda i: (i, 0))
    wide = pl.BlockSpec((tm, 2 * D_MODEL), lambda i: (i, 0))
    out = jax.ShapeDtypeStruct((t, D_MODEL), BF16)
    return pl.pallas_call(
        body, name=name, grid=(t // tm,),
        in_specs=[blk, pl.BlockSpec((D_MODEL, D_MODEL), lambda i: (0, 0)), wide, blk, blk], out_specs=[blk, blk, wide],
        out_shape=[out, out, jax.ShapeDtypeStruct((t, 2 * D_MODEL), BF16)], compiler_params=_params(("parallel",)),
    )(dh, wo, gates, ya, yb)


def _loss_head(name, h3, gf, tgt):
    b, l, _ = h3.shape
    nb = l // BLOCK

    def body(h_ref, g_ref, t_ref, dh_ref, loss_ref, dg_ref):
        first = (pl.program_id(0) == 0) & (pl.program_id(1) == 0)
        real = (pl.program_id(1) > 0).astype(F32)
        hv = h_ref[...]
        g = g_ref[...]
        r = lax.rsqrt(jnp.mean(hv * hv, axis=-1, keepdims=True) + EPS)
        xn = hv * r
        err = (xn * g - t_ref[...]) * real
        lpart = 0.5 * jnp.sum(jnp.mean(err * err, axis=-1, keepdims=True), axis=0, keepdims=True)
        dy = err * (1.0 / D_MODEL)
        tv = dy * g
        dot = jnp.mean(tv * hv, axis=-1, keepdims=True)
        dh_ref[...] = r * tv - hv * (r * r * r * dot)
        gpart = jnp.sum(dy * xn, axis=0, keepdims=True)

        @pl.when(first)
        def _():
            loss_ref[...] = jnp.zeros_like(loss_ref)
            dg_ref[...] = jnp.zeros_like(dg_ref)

        loss_ref[...] += jnp.broadcast_to(lpart, loss_ref.shape)
        dg_ref[...] += gpart

    return pl.pallas_call(
        body, name=name, grid=(b, nb),
        in_specs=[pl.BlockSpec((None, BLOCK, D_MODEL), lambda bi, n: (bi, n, 0)),
                  pl.BlockSpec((1, D_MODEL), lambda bi, n: (0, 0)),
                  pl.BlockSpec((None, BLOCK, D_MODEL), lambda bi, n: (bi, jnp.maximum(n - 1, 0), 0))],
        out_specs=[pl.BlockSpec((None, BLOCK, D_MODEL), lambda bi, n: (bi, n, 0)),
                   pl.BlockSpec((8, 128), lambda bi, n: (0, 0)),
                   pl.BlockSpec((1, D_MODEL), lambda bi, n: (0, 0))],
        out_shape=[jax.ShapeDtypeStruct(h3.shape, F32), jax.ShapeDtypeStruct((8, 128), F32),
                   jax.ShapeDtypeStruct((1, D_MODEL), F32)],
        compiler_params=_params(("arbitrary", "arbitrary")),
    )(h3, gf, tgt)


def _fgate_fwd(name, f3, bf_row):
    b, l, _ = f3.shape
    nb = l // BLOCK

    def body(f_ref, b_ref, cc_ref, cr_ref):
        r_i = lax.broadcasted_iota(jnp.int32, (BLOCK, BLOCK), 0)
        c_i = lax.broadcasted_iota(jnp.int32, (BLOCK, BLOCK), 1)
        tri = (r_i >= c_i).astype(F32)
        carry = jnp.zeros((1, 128), F32)
        for blk in range(nb):
            rows = slice(blk * BLOCK, (blk + 1) * BLOCK)
            z = f_ref[rows, :] + b_ref[...]
            lf = jnp.minimum(z, 0.0) - jnp.log(1.0 + jnp.exp(-jnp.abs(z)))
            cb = jnp.dot(tri, lf, preferred_element_type=F32, precision=lax.Precision.HIGHEST) + carry
            carry = cb[BLOCK - 1:BLOCK, :]
            cbt = cb.T
            for hh in range(B_HEADS):
                cc_ref[hh, rows, :] = jnp.sum(jnp.where(c_i == hh, cb, 0.0), axis=1, keepdims=True)
                cr_ref[hh, :, rows] = cbt[hh:hh + 1, :]

    return pl.pallas_call(
        body, name=name, grid=(b,),
        in_specs=[pl.BlockSpec((None, l, 128), lambda bi: (bi, 0, 0)),
                  pl.BlockSpec((1, 128), lambda bi: (0, 0))],
        out_specs=[pl.BlockSpec((None, B_HEADS, l, 1), lambda bi: (bi, 0, 0, 0)),
                   pl.BlockSpec((None, B_HEADS, 1, l), lambda bi: (bi, 0, 0, 0))],
        out_shape=[jax.ShapeDtypeStruct((b, B_HEADS, l, 1), F32), jax.ShapeDtypeStruct((b, B_HEADS, 1, l), F32)],
        compiler_params=_params(("parallel",)),
    )(f3, bf_row)


def _fgate_bwd(name, f3, bf_row, dcq, dck):
    b, l, _ = f3.shape
    nb = l // BLOCK

    def body(f_ref, b_ref, dcq_ref, dck_ref, df_ref, db_ref):
        r_i = lax.broadcasted_iota(jnp.int32, (BLOCK, BLOCK), 0)
        c_i = lax.broadcasted_iota(jnp.int32, (BLOCK, BLOCK), 1)
        tri = (r_i <= c_i).astype(F32)
        carry = jnp.zeros((1, 128), F32)
        total = jnp.zeros((1, 128), F32)
        for blk in range(nb - 1, -1, -1):
            rows = slice(blk * BLOCK, (blk + 1) * BLOCK)
            krows = jnp.concatenate([dck_ref[hh, :, rows] for hh in range(B_HEADS)]
                                    + [jnp.zeros((BLOCK - B_HEADS, BLOCK), F32)], axis=0)
            dcb = krows.T
            for hh in range(B_HEADS):
                dcb = dcb + jnp.where(c_i == hh, dcq_ref[hh, rows, :], 0.0)
            rc = jnp.dot(tri, dcb, preferred_element_type=F32, precision=lax.Precision.HIGHEST) + carry
            carry = rc[0:1, :]
            z = f_ref[rows, :] + b_ref[...]
            df = rc * (1.0 / (1.0 + jnp.exp(z)))
            df_ref[rows, :] = df.astype(BF16)
            total = total + jnp.sum(df, axis=0, keepdims=True)

        @pl.when(pl.program_id(0) == 0)
        def _():
            db_ref[...] = total

        @pl.when(pl.program_id(0) > 0)
        def _():
            db_ref[...] += total

    return pl.pallas_call(
        body, name=name, grid=(b,),
        in_specs=[pl.BlockSpec((None, l, 128), lambda bi: (bi, 0, 0)),
                  pl.BlockSpec((1, 128), lambda bi: (0, 0)),
                  pl.BlockSpec((None, B_HEADS, l, 1), lambda bi: (bi, 0, 0, 0)),
                  pl.BlockSpec((None, B_HEADS, 1, l), lambda bi: (bi, 0, 0, 0))],
        out_specs=[pl.BlockSpec((None, l, 128), lambda bi: (bi, 0, 0)), pl.BlockSpec((1, 128), lambda bi: (0, 0))],
        out_shape=[jax.ShapeDtypeStruct((b, l, 128), BF16), jax.ShapeDtypeStruct((1, 128), F32)],
        compiler_params=_params(("arbitrary",)),
    )(f3, bf_row, dcq, dck)


A_Q_BLK = B_SEG // A_WIDTH
A_K_BLK = (B_SEG + A_WIDTH) // 128
A_V_BLK = A_K_BLK + 1
A_SEG_BLK = B_SEG // A_SEG
STACK = A_HEADS * BLOCK


def _lane_lo():
    return lax.broadcasted_iota(jnp.int32, (1, 128), 1) < HEAD_DIM


def _stack_heads(x, masked):
    lo = _lane_lo()
    blks = [x[:, 128 * j:128 * (j + 1)] for j in range(4)]
    if not masked:
        return jnp.concatenate(blks + blks, axis=0)
    zero = jnp.zeros_like(blks[0])
    return jnp.concatenate([jnp.where(lo, bk, zero) for bk in blks] + [jnp.where(lo, zero, bk) for bk in blks], axis=0)


def _unstack_heads(y):
    lo = _lane_lo()
    return jnp.concatenate([jnp.where(lo, y[128 * j:128 * (j + 1)], y[128 * (4 + j):128 * (5 + j)]) for j in range(4)], axis=1)


def _swa_bias(slopes):
    r_i = jnp.arange(STACK)[:, None]
    c_i = jnp.arange(3 * BLOCK)[None, :]
    seg = c_i >> 7
    out = []
    for n in range(3):
        qpos = n * BLOCK + (r_i & (BLOCK - 1))
        kpos = jnp.where(seg == 0, c_i, (n - 2) * BLOCK + c_i)
        dist = qpos - kpos
        band = (seg != 0) & (dist < BLOCK) & (kpos >= PREFIX)
        meta = (seg == 0) & (c_i >= N_PAD)
        out.append(jnp.where((dist >= 0) & (band | meta), -slopes * dist.astype(F32), NEG))
    return jnp.stack(out, axis=0)


def _swa_scores(q, kcat, n, slope, bias):
    s = lax.dot_general(q, kcat, _NT, preferred_element_type=F32) + bias
    further = slope * (-BLOCK * jnp.maximum(n - 2, 0)).astype(F32)
    return jnp.concatenate([s[:, 0:BLOCK] + further, s[:, BLOCK:]], axis=1)


def _swa_specs():
    def kv(col_blk):
        return [pl.BlockSpec((None, BLOCK, 128), lambda b, n: (b, 0, col_blk)),
                pl.BlockSpec((None, BLOCK, 128), lambda b, n: (b, jnp.maximum(n - 1, 0), col_blk)),
                pl.BlockSpec((None, BLOCK, 128), lambda b, n: (b, n, col_blk))]

    q_spec = pl.BlockSpec((None, BLOCK, A_WIDTH), lambda b, n: (b, n, A_Q_BLK))
    o_spec = pl.BlockSpec((None, BLOCK, A_WIDTH), lambda b, n: (b, n, 0))
    col = pl.BlockSpec((STACK, 1), lambda b, n: (0, 0))
    bias = pl.BlockSpec((None, STACK, 3 * BLOCK), lambda b, n: (jnp.minimum(n, 2), 0, 0))
    lse_spec = pl.BlockSpec((None, A_HEADS, BLOCK, 1), lambda b, n: (b, 0, n, 0))
    return q_spec, kv(A_K_BLK), kv(A_V_BLK), o_spec, [col, col, bias], lse_spec


def _swa_fwd(name, qkv, slopes, sinks, bias):
    b, l, _ = qkv.shape
    nb = l // BLOCK

    def body(q_ref, k0_ref, kp_ref, kc_ref, v0_ref, vp_ref, vc_ref, sl_ref, sk_ref, bias_ref, o_ref, lse_ref):
        n = pl.program_id(1)
        qs = _stack_heads(q_ref[...], True) * SCALE
        kcat = jnp.concatenate([k0_ref[...], kp_ref[...], kc_ref[...]], axis=0)
        vcat = jnp.concatenate([v0_ref[...], vp_ref[...], vc_ref[...]], axis=0)
        s = _swa_scores(qs, kcat, n, sl_ref[...], bias_ref[...])
        sink = sk_ref[...]
        m = jnp.maximum(jnp.max(s, axis=-1, keepdims=True), sink)
        p = jnp.exp(s - m)
        den = jnp.sum(p, axis=-1, keepdims=True) + jnp.exp(sink - m)
        o = lax.dot_general(p.astype(BF16), vcat, _NN, preferred_element_type=F32) / den
        o_ref[...] = _unstack_heads(o)
        lse_ref[...] = (m + jnp.log(den)).reshape(A_HEADS, BLOCK, 1)

    q_spec, k_specs, v_specs, o_spec, consts, lse_spec = _swa_specs()
    return pl.pallas_call(
        body, name=name, grid=(b, nb),
        in_specs=[q_spec] + k_specs + v_specs + consts, out_specs=[o_spec, lse_spec],
        out_shape=[jax.ShapeDtypeStruct((b, l, A_WIDTH), F32), jax.ShapeDtypeStruct((b, A_HEADS, l, 1), F32)],
        compiler_params=_params(("parallel", "parallel")),
    )(qkv, qkv, qkv, qkv, qkv, qkv, qkv, slopes, sinks, bias)


def _swa_bwd(name, qkv, o, lse, do, slopes, sinks, bias, dqkv):
    b, l, _ = qkv.shape
    nb = l // BLOCK

    def body(q_ref, k0_ref, kp_ref, kc_ref, v0_ref, vp_ref, vc_ref, o_ref, lse_ref, do_ref, sl_ref, sk_ref, bias_ref, _,
             dx_ref, ds_ref, dk_acc, dv_acc):
        bi = pl.program_id(0)
        n = pl.program_id(1)
        qs = _stack_heads(q_ref[...], True) * SCALE
        dos32 = _stack_heads(do_ref[...], True)
        dos = dos32.astype(BF16)
        os_ = _stack_heads(o_ref[...], False)
        lsev = lse_ref[...].reshape(STACK, 1)
        kcat = jnp.concatenate([k0_ref[...], kp_ref[...], kc_ref[...]], axis=0)
        vcat = jnp.concatenate([v0_ref[...], vp_ref[...], vc_ref[...]], axis=0)
        s = _swa_scores(qs, kcat, n, sl_ref[...], bias_ref[...])
        p = jnp.exp(s - lsev)
        dsum = jnp.sum(dos32 * os_, axis=-1, keepdims=True)
        dp = lax.dot_general(dos, vcat, _NT, preferred_element_type=F32)
        dsc = (p * (dp - dsum)).astype(BF16)
        dq = lax.dot_general(dsc, kcat, _NN, preferred_element_type=F32) * SCALE
        row0 = pl.multiple_of(n * BLOCK, BLOCK)
        dx_ref[pl.ds(row0, BLOCK), 0:A_WIDTH] = _unstack_heads(dq).astype(BF16)
        dkc = lax.dot_general(dsc, qs, _TN, preferred_element_type=F32)
        dvc = lax.dot_general(p.astype(BF16), dos, _TN, preferred_element_type=F32)

        @pl.when(n == 0)
        def _():
            dk_acc[...] = jnp.zeros_like(dk_acc)
            dv_acc[...] = jnp.zeros_like(dv_acc)

        starts = (0, pl.multiple_of(jnp.maximum(n - 1, 0) * BLOCK, BLOCK), row0)
        for t, st in enumerate(starts):
            dk_acc[pl.ds(st, BLOCK), :] += dkc[t * BLOCK:(t + 1) * BLOCK, :]
            dv_acc[pl.ds(st, BLOCK), :] += dvc[t * BLOCK:(t + 1) * BLOCK, :]

        @pl.when(n == nb - 1)
        def _():
            dx_ref[:, A_WIDTH:A_WIDTH + 128] = dk_acc[...].astype(BF16)
            dx_ref[:, A_WIDTH + 128:A_SEG] = dv_acc[...].astype(BF16)

        dsink = -(jnp.exp(sk_ref[...] - lsev) * dsum)
        r8 = lax.broadcasted_iota(jnp.int32, (8, 128), 0)
        acc = jnp.zeros((8, 128), F32)
        for hh in range(A_HEADS):
            acc = acc + jnp.where(r8 == hh, jnp.sum(dsink[hh * BLOCK:(hh + 1) * BLOCK, :]), 0.0)

        @pl.when((bi == 0) & (n == 0))
        def _():
            ds_ref[...] = jnp.zeros_like(ds_ref)

        ds_ref[...] += acc

    q_spec, k_specs, v_specs, o_spec, consts, lse_spec = _swa_specs()
    return pl.pallas_call(
        body, name=name, grid=(b, nb),
        in_specs=[q_spec] + k_specs + v_specs + [o_spec, lse_spec, o_spec] + consts + [pl.BlockSpec(memory_space=pl.ANY)],
        out_specs=[pl.BlockSpec((None, l, A_SEG), lambda bb, n: (bb, 0, A_SEG_BLK)),
                   pl.BlockSpec((8, 128), lambda bb, n: (0, 0))],
        out_shape=[jax.ShapeDtypeStruct(dqkv.shape, BF16), jax.ShapeDtypeStruct((8, 128), F32)],
        scratch_shapes=[pltpu.VMEM((l, 128), F32), pltpu.VMEM((l, 128), F32)],
        input_output_aliases={13: 0},
        compiler_params=_params(("arbitrary", "arbitrary")),
    )(qkv, qkv, qkv, qkv, qkv, qkv, qkv, o, lse, do, slopes, sinks, bias, dqkv)


def _fox_mask(qk, ck, i):
    kh = qk.shape[1]
    qpos = i * BLOCK + lax.broadcasted_iota(jnp.int32, (BLOCK, kh), 0)
    kpos = lax.broadcasted_iota(jnp.int32, (BLOCK, kh), 1)
    return jnp.where((kpos <= qpos) & (kpos >= N_PAD), qk - ck, NEG)


def _pick_head(x, hh):
    lo = _lane_lo()
    return jnp.where(lo if hh == 0 else jnp.logical_not(lo), x, jnp.zeros_like(x))


def _both_heads(x):
    return jnp.concatenate([_pick_head(x, 0), _pick_head(x, 1)], axis=0)


def _fox_specs(l):
    pair = pl.BlockSpec((None, l, PAIR_W), lambda bi, hp: (bi, 0, hp))
    half = pl.BlockSpec((None, l, 128), lambda bi, hp: (bi, 0, hp))
    colv = pl.BlockSpec((None, 2, l, 1), lambda bi, hp: (bi, hp, 0, 0))
    rowv = pl.BlockSpec((None, 2, 1, l), lambda bi, hp: (bi, hp, 0, 0))
    return pair, half, colv, rowv


def _fox_fwd(name, qkv, c_col, c_row):
    b, l, _ = qkv.shape
    nb = l // BLOCK

    def body(x_ref, cc_ref, cr_ref, o_ref, lse_ref):
        for i in range(nb):
            rows = slice(i * BLOCK, (i + 1) * BLOCK)
            kh = (i + 1) * BLOCK
            qblk = x_ref[rows, 0:128]
            kv = x_ref[0:kh, 128:256]
            vv = x_ref[0:kh, 256:384]
            qk = lax.dot_general(_both_heads(qblk) * SCALE, kv, _NT, preferred_element_type=F32)
            ps, dens = [], []
            for hh in range(2):
                s = _fox_mask(qk[hh * BLOCK:(hh + 1) * BLOCK], cr_ref[hh, :, 0:kh], i)
                m = jnp.max(s, axis=-1, keepdims=True)
                p = jnp.exp(s - m)
                den = jnp.sum(p, axis=-1, keepdims=True)
                ps.append(p.astype(BF16))
                dens.append(den)
                lse_ref[hh, rows, :] = (m + jnp.log(den)) + cc_ref[hh, rows, :]
            pv = lax.dot_general(jnp.concatenate(ps, axis=0), vv, _NN, preferred_element_type=F32)
            o_ref[rows, :] = jnp.where(_lane_lo(), pv[0:BLOCK] / dens[0], pv[BLOCK:2 * BLOCK] / dens[1]).astype(BF16)

    pair, half, colv, rowv = _fox_specs(l)
    return pl.pallas_call(
        body, name=name, grid=(b, 4), in_specs=[pair, colv, rowv], out_specs=[half, colv],
        out_shape=[jax.ShapeDtypeStruct((b, l, B_WIDTH), BF16), jax.ShapeDtypeStruct((b, B_HEADS, l, 1), F32)],
        compiler_params=_params(("parallel", "parallel")),
    )(qkv, c_col, c_row)


def _fox_bwd(name, qkv, c_col, c_row, o, lse, do):
    b, l, _ = qkv.shape
    nb = l // BLOCK

    def body(x_ref, cc_ref, cr_ref, o_ref, lse_ref, do_ref, dx_ref, dcq_ref, dck_ref, dk_acc, dv_acc):
        dk_acc[...] = jnp.zeros_like(dk_acc)
        dv_acc[...] = jnp.zeros_like(dv_acc)
        dck_ref[...] = jnp.zeros_like(dck_ref)
        for i in range(nb):
            rows = slice(i * BLOCK, (i + 1) * BLOCK)
            kh = (i + 1) * BLOCK
            qblk = x_ref[rows, 0:128]
            kv = x_ref[0:kh, 128:256]
            vv = x_ref[0:kh, 256:384]
            doblk = do_ref[rows, :]
            ov = o_ref[rows, :].astype(F32)
            q2 = _both_heads(qblk) * SCALE
            do2 = _both_heads(doblk)
            qk = lax.dot_general(q2, kv, _NT, preferred_element_type=F32)
            dp = lax.dot_general(do2, vv, _NT, preferred_element_type=F32)
            ps, dss = [], []
            for hh in range(2):
                half = slice(hh * BLOCK, (hh + 1) * BLOCK)
                s = _fox_mask(qk[half], cr_ref[hh, :, 0:kh], i)
                p = jnp.exp(s - (lse_ref[hh, rows, :] - cc_ref[hh, rows, :]))
                dsum = jnp.sum(do2[half].astype(F32) * ov, axis=-1, keepdims=True)
                ds = p * (dp[half] - dsum)
                ps.append(p.astype(BF16))
                dss.append(ds.astype(BF16))
                dcq_ref[hh, rows, :] = jnp.sum(ds, axis=-1, keepdims=True)
                dck_ref[hh, :, 0:kh] -= jnp.sum(ds, axis=0, keepdims=True)
            p2 = jnp.concatenate(ps, axis=0)
            ds2 = jnp.concatenate(dss, axis=0)
            dq = lax.dot_general(ds2, kv, _NN, preferred_element_type=F32) * SCALE
            dk_acc[0:kh, :] += lax.dot_general(ds2, q2, _TN, preferred_element_type=F32)
            dv_acc[0:kh, :] += lax.dot_general(p2, do2, _TN, preferred_element_type=F32)
            dx_ref[rows, 0:128] = jnp.where(_lane_lo(), dq[0:BLOCK], dq[BLOCK:2 * BLOCK]).astype(BF16)
        dx_ref[:, 128:256] = dk_acc[...].astype(BF16)
        dx_ref[:, 256:384] = dv_acc[...].astype(BF16)

    pair, half, colv, rowv = _fox_specs(l)
    return pl.pallas_call(
        body, name=name, grid=(b, 4), in_specs=[pair, colv, rowv, half, colv, half],
        out_specs=[pair, colv, rowv],
        out_shape=[jax.ShapeDtypeStruct(qkv.shape, BF16), jax.ShapeDtypeStruct((b, B_HEADS, l, 1), F32),
                   jax.ShapeDtypeStruct((b, B_HEADS, 1, l), F32)],
        scratch_shapes=[pltpu.VMEM((l, 128), F32), pltpu.VMEM((l, 128), F32)],
        compiler_params=_params(("parallel", "parallel")),
    )(qkv, c_col, c_row, o, lse, do)


_FLIPS = ((0, 0, 1), (0, 1, 0), (0, 1, 1), (1, 0, 0), (1, 0, 1), (1, 1, 0), (1, 1, 1))


def _exchange(name, gather, scatter):
    ng, ns = len(gather), len(scatter)
    na = ng + ns
    npeer = len(_FLIPS)

    def body(*refs):
        ins = refs[:na]
        outs = refs[na:2 * na]
        send_sems, recv_sems, loc_sems = refs[2 * na:]
        x, y, c = lax.axis_index("x"), lax.axis_index("y"), lax.axis_index("c")
        me = 4 * x + 2 * y + c
        peers = []
        for fx, fy, fc in _FLIPS:
            px = 1 - x if fx else x
            py = 1 - y if fy else y
            pc = 1 - c if fc else c
            peers.append(((px, py, pc), 4 * px + 2 * py + pc))

        def remote(a, kk):
            dev, lin = peers[kk]
            src = ins[a] if a < ng else ins[a].at[lin]
            return pltpu.make_async_remote_copy(src_ref=src, dst_ref=outs[a].at[me], send_sem=send_sems.at[a * npeer + kk],
                                                recv_sem=recv_sems.at[a * npeer + kk], device_id=dev, device_id_type=MESH_ID)

        def arrival(a, kk):
            dev, lin = peers[kk]
            src = ins[a] if a < ng else ins[a].at[lin]
            return pltpu.make_async_remote_copy(src_ref=src, dst_ref=outs[a].at[lin], send_sem=send_sems.at[a * npeer + kk],
                                                recv_sem=recv_sems.at[a * npeer + kk], device_id=dev, device_id_type=MESH_ID)

        local = []
        for a in range(na):
            src = ins[a] if a < ng else ins[a].at[me]
            cp = pltpu.make_async_copy(src, outs[a].at[me], loc_sems.at[a])
            cp.start()
            local.append(cp)
        sent = [remote(a, kk) for kk in range(npeer) for a in range(na)]
        for cp in sent:
            cp.start()
        for kk in range(npeer):
            for a in range(na):
                arrival(a, kk).wait_recv()
        for cp in sent:
            cp.wait_send()
        for cp in local:
            cp.wait()

    arrs = list(gather) + list(scatter)
    out_shape = [jax.ShapeDtypeStruct((N_DEV,) + tuple(a.shape), a.dtype) for a in gather]
    out_shape += [jax.ShapeDtypeStruct(tuple(a.shape), a.dtype) for a in scatter]
    anyspec = pl.BlockSpec(memory_space=pl.ANY)
    return pl.pallas_call(
        body, name=name, in_specs=[anyspec] * na, out_specs=[anyspec] * na, out_shape=out_shape,
        scratch_shapes=[pltpu.SemaphoreType.DMA((na * npeer,)), pltpu.SemaphoreType.DMA((na * npeer,)),
                        pltpu.SemaphoreType.DMA((na,))],
        compiler_params=pltpu.CompilerParams(has_side_effects=True),
    )(*arrs)


def _peer_table():
    x, y, c = lax.axis_index("x"), lax.axis_index("y"), lax.axis_index("c")
    me = 4 * x + 2 * y + c
    peers = []
    for fx, fy, fc in _FLIPS:
        px = 1 - x if fx else x
        py = 1 - y if fy else y
        pc = 1 - c if fc else c
        peers.append(((px, py, pc), 4 * px + 2 * py + pc))
    return me, peers


_HBM = pl.BlockSpec(memory_space=pltpu.HBM)
_SEM = pl.BlockSpec(memory_space=pltpu.SEMAPHORE)
_ANY = pl.BlockSpec(memory_space=pl.ANY)
_EFFECT = pltpu.SideEffectType.DATAFLOW_SIDE_EFFECTING


def _split_copy(srcs_are_pieces, src_refs, land_refs, send_sem, recv_sem, a, kk, me, peers, arriving):
    dev, lin = peers[kk]
    npeer = len(_FLIPS)
    src = src_refs[a] if srcs_are_pieces[a] else src_refs[a].at[lin]
    dst = land_refs[a].at[lin] if arriving else land_refs[a].at[me]
    return pltpu.make_async_remote_copy(src_ref=src, dst_ref=dst, send_sem=send_sem.at[a * npeer + kk],
                                        recv_sem=recv_sem.at[a * npeer + kk], device_id=dev, device_id_type=MESH_ID)


def _xchg_start(name, gather, scatter, after=None):
    me_out = 4 * lax.axis_index("x") + 2 * lax.axis_index("y") + lax.axis_index("c")
    srcs = list(gather) + list(scatter)
    is_piece = [True] * len(gather) + [False] * len(scatter)
    lands = []
    for a, piece in zip(srcs, is_piece):
        own = a[None] if piece else lax.dynamic_slice_in_dim(a, me_out, 1, axis=0)
        shape = ((N_DEV,) + tuple(a.shape)) if piece else tuple(a.shape)
        start = (me_out,) + (0,) * (len(shape) - 1)
        lands.append(lax.dynamic_update_slice(lax.empty(shape, a.dtype), own, start))
    n = len(srcs)
    nsem = n * len(_FLIPS)
    has_after = after is not None

    def body(*refs):
        src_refs = refs[:n]
        land_refs = refs[n:2 * n]
        outs = refs[2 * n + (1 if has_after else 0):]
        send_sem, recv_sem = outs[0], outs[1]
        token = outs[-1]
        me, peers = _peer_table()
        for kk in range(len(_FLIPS)):
            for a in range(n):
                _split_copy(is_piece, src_refs, land_refs, send_sem, recv_sem, a, kk, me, peers, False).start()
        token[...] = jnp.zeros_like(token)

    out_shape = ([pltpu.SemaphoreType.DMA((nsem,)), pltpu.SemaphoreType.DMA((nsem,))]
                 + [pltpu.HBM(tuple(a.shape), a.dtype) for a in srcs] + [pltpu.HBM(tuple(a.shape), a.dtype) for a in lands]
                 + [jax.ShapeDtypeStruct((8, 128), F32)])
    args = [pltpu.with_memory_space_constraint(a, pltpu.HBM) for a in srcs + lands] + ([after] if has_after else [])
    res = pl.pallas_call(
        body, name=name, out_shape=out_shape,
        in_specs=[_HBM] * (2 * n) + ([_ANY] if has_after else []),
        out_specs=[_SEM, _SEM] + [_HBM] * (2 * n) + [pl.BlockSpec(memory_space=pltpu.VMEM)],
        input_output_aliases={i: 2 + i for i in range(2 * n)},
        compiler_params=pltpu.CompilerParams(has_side_effects=_EFFECT),
    )(*args)
    state = (res[0], res[1], list(res[2:2 + n]), list(res[2 + n:2 + 2 * n]), is_piece)
    return state, res[-1]


def _xchg_wait(name, state, after):
    send_sem, recv_sem, srcs, lands, is_piece = state
    n = len(srcs)

    def body(*refs):
        src_refs = refs[:n]
        land_refs = refs[n:2 * n]
        s_sem, r_sem = refs[2 * n], refs[2 * n + 1]
        me, peers = _peer_table()
        for kk in range(len(_FLIPS)):
            for a in range(n):
                cp = _split_copy(is_piece, src_refs, land_refs, s_sem, r_sem, a, kk, me, peers, True)
                cp.wait_send()
                cp.wait_recv()

    out_shape = [pltpu.HBM(tuple(a.shape), a.dtype) for a in srcs] + [pltpu.HBM(tuple(a.shape), a.dtype) for a in lands]
    res = pl.pallas_call(
        body, name=name, out_shape=out_shape,
        in_specs=[_HBM] * (2 * n) + [_SEM, _SEM, _ANY], out_specs=[_HBM] * (2 * n),
        input_output_aliases={i: i for i in range(2 * n)},
        compiler_params=pltpu.CompilerParams(has_side_effects=_EFFECT),
    )(*srcs, *lands, send_sem, recv_sem, after)
    return list(res[n:])


_SIB = (0, 0, 1)
_ICI = ((0, 1, 0), (1, 0, 0), (1, 1, 0))


def _flip(fl):
    x, y, c = lax.axis_index("x"), lax.axis_index("y"), lax.axis_index("c")
    px = 1 - x if fl[0] else x
    py = 1 - y if fl[1] else y
    pc = 1 - c if fl[2] else c
    return (px, py, pc), 4 * px + 2 * py + pc


def _gather2_start(name, pieces, after=None):
    me_out = 4 * lax.axis_index("x") + 2 * lax.axis_index("y") + lax.axis_index("c")
    pieces = list(pieces)
    n = len(pieces)
    lands = [lax.dynamic_update_slice(lax.empty((N_DEV,) + tuple(a.shape), a.dtype), a[None],
                                      (me_out,) + (0,) * a.ndim) for a in pieces]
    first = (_SIB,) + _ICI
    has_after = after is not None

    def body(*refs):
        src_refs, land_refs = refs[:n], refs[n:2 * n]
        outs = refs[2 * n + (1 if has_after else 0):]
        send_sem, recv_sem, token = outs[0], outs[1], outs[-1]
        _, me = _flip((0, 0, 0))
        for kk, fl in enumerate(first):
            dev, _ = _flip(fl)
            for a in range(n):
                pltpu.make_async_remote_copy(src_ref=src_refs[a], dst_ref=land_refs[a].at[me],
                                             send_sem=send_sem.at[a * 4 + kk], recv_sem=recv_sem.at[a * 4 + kk],
                                             device_id=dev, device_id_type=MESH_ID).start()
        token[...] = jnp.zeros_like(token)

    hbm = [pltpu.HBM(tuple(a.shape), a.dtype) for a in pieces + lands]
    res = pl.pallas_call(
        body, name=name,
        out_shape=[pltpu.SemaphoreType.DMA((4 * n,)), pltpu.SemaphoreType.DMA((4 * n,))] + hbm
        + [jax.ShapeDtypeStruct((8, 128), F32)],
        in_specs=[_HBM] * (2 * n) + ([_ANY] if has_after else []),
        out_specs=[_SEM, _SEM] + [_HBM] * (2 * n) + [pl.BlockSpec(memory_space=pltpu.VMEM)],
        input_output_aliases={i: 2 + i for i in range(2 * n)},
        compiler_params=pltpu.CompilerParams(has_side_effects=_EFFECT),
    )(*([pltpu.with_memory_space_constraint(a, pltpu.HBM) for a in pieces + lands] + ([after] if has_after else [])))
    return (res[0], res[1], list(res[2:2 + n]), list(res[2 + n:2 + 2 * n])), res[-1]


def _gather2_forward(name, state, after):
    send_a, recv_a, pieces, lands = state
    n = len(pieces)
    first = (_SIB,) + _ICI

    def body(*refs):
        src_refs, land_refs = refs[:n], refs[n:2 * n]
        s_a, r_a = refs[2 * n], refs[2 * n + 1]
        outs = refs[2 * n + 3:]
        send_b, recv_b, token = outs[0], outs[1], outs[-1]
        for kk, fl in enumerate(first):
            dev, lin = _flip(fl)
            for a in range(n):
                cp = pltpu.make_async_remote_copy(src_ref=src_refs[a], dst_ref=land_refs[a].at[lin],
                                                  send_sem=s_a.at[a * 4 + kk], recv_sem=r_a.at[a * 4 + kk],
                                                  device_id=dev, device_id_type=MESH_ID)
                cp.wait_send()
                cp.wait_recv()
        sib, _ = _flip(_SIB)
        for j, fl in enumerate(_ICI):
            _, lin = _flip(fl)
            for a in range(n):
                pltpu.make_async_remote_copy(src_ref=land_refs[a].at[lin], dst_ref=land_refs[a].at[lin],
                                             send_sem=send_b.at[a * 3 + j], recv_sem=recv_b.at[a * 3 + j],
                                             device_id=sib, device_id_type=MESH_ID).start()
        token[...] = jnp.zeros_like(token)

    hbm = [pltpu.HBM(tuple(a.shape), a.dtype) for a in pieces + lands]
    res = pl.pallas_call(
        body, name=name,
        out_shape=[pltpu.SemaphoreType.DMA((3 * n,)), pltpu.SemaphoreType.DMA((3 * n,))] + hbm
        + [jax.ShapeDtypeStruct((8, 128), F32)],
        in_specs=[_HBM] * (2 * n) + [_SEM, _SEM, _ANY],
        out_specs=[_SEM, _SEM] + [_HBM] * (2 * n) + [pl.BlockSpec(memory_space=pltpu.VMEM)],
        input_output_aliases={i: 2 + i for i in range(2 * n)},
        compiler_params=pltpu.CompilerParams(has_side_effects=_EFFECT),
    )(*pieces, *lands, send_a, recv_a, after)
    return (res[0], res[1], list(res[2 + n:2 + 2 * n])), res[-1]


def _gather2_wait(name, state, after):
    send_b, recv_b, lands = state
    n = len(lands)

    def body(*refs):
        land_refs = refs[:n]
        s_b, r_b = refs[n], refs[n + 1]
        sib, _ = _flip(_SIB)
        for j, fl in enumerate(_ICI):
            _, sent = _flip(fl)
            _, arriving = _flip((fl[0], fl[1], 1))
            for a in range(n):
                cp = pltpu.make_async_remote_copy(src_ref=land_refs[a].at[sent], dst_ref=land_refs[a].at[arriving],
                                                  send_sem=s_b.at[a * 3 + j], recv_sem=r_b.at[a * 3 + j],
                                                  device_id=sib, device_id_type=MESH_ID)
                cp.wait_send()
                cp.wait_recv()

    res = pl.pallas_call(
        body, name=name, out_shape=[pltpu.HBM(tuple(a.shape), a.dtype) for a in lands],
        in_specs=[_HBM] * n + [_SEM, _SEM, _ANY], out_specs=[_HBM] * n,
        input_output_aliases={i: i for i in range(n)},
        compiler_params=pltpu.CompilerParams(has_side_effects=_EFFECT),
    )(*lands, send_b, recv_b, after)
    return list(res)


def _adam_math(w, g, m, v):
    m = ADAM_B1 * m + (1.0 - ADAM_B1) * g
    v = ADAM_B2 * v + (1.0 - ADAM_B2) * (g * g)
    m_hat = m / (1.0 - ADAM_B1 ** ADAM_STEP)
    v_hat = v / (1.0 - ADAM_B2 ** ADAM_STEP)
    delta = -ADAM_LR * (m_hat / (jnp.sqrt(v_hat) + ADAM_EPS) + ADAM_WD * w)
    return delta, m, v


def _adam(name, w, m, v, parts, transposed=False, dep=None):
    npart, _, cp = parts.shape
    has_dep = dep is not None
    if transposed:
        c, r = w.shape
        tr = _pick(r, (256, 128))
        blk = pl.BlockSpec((c, tr), lambda i: (0, i))
    else:
        r, c = w.shape
        tr = _pick(r, (256, 176, 128, 64, 16, 8, 1))
        blk = pl.BlockSpec((tr, c), lambda i: (i, 0))

    def body(w_ref, m_ref, v_ref, p_ref, *rest):
        g_ref, d_ref, mo_ref, vo_ref = rest[-4:]
        g = p_ref[0].astype(F32)
        for pp in range(1, npart):
            g = g + p_ref[pp].astype(F32)
        g = g.T[0:c, :] if transposed else g[:, 0:c]
        delta, mn, vn = _adam_math(w_ref[...], g, m_ref[...], v_ref[...])
        g_ref[...] = g
        d_ref[...] = delta
        mo_ref[...] = mn
        vo_ref[...] = vn

    out = jax.ShapeDtypeStruct(w.shape, F32)
    return pl.pallas_call(
        body, name=name, grid=(r // tr,),
        in_specs=[blk, blk, blk, pl.BlockSpec((npart, tr, cp), lambda i: (0, i, 0))]
        + ([pl.BlockSpec(memory_space=pl.ANY)] if has_dep else []),
        out_specs=[blk, blk, blk, blk], out_shape=[out, out, out, out], compiler_params=_params(("parallel",)),
    )(*((w, m, v, parts) + ((dep,) if has_dep else ())))


def _small_sum(name, packs):
    def body(p_ref, o_ref):
        tot = p_ref[0]
        for pp in range(1, N_DEV):
            tot = tot + p_ref[pp]
        o_ref[0:8, :] = tot[0:8, :]
        o_ref[8:24, :] = tot[8:24, :] + tot[24:40, :]

    return pl.pallas_call(body, name=name, out_shape=jax.ShapeDtypeStruct((24, D_MODEL), F32),
                          compiler_params=_params())(packs)


def _local_step(x, tgt, g1, gm, g2, gf, b_forget, sinks, weights, send):
    b, s, _ = x.shape
    l = s + PREFIX
    t = b * l
    (meta,) = weights("meta", x)
    h0 = jnp.concatenate([jnp.zeros((b, N_PAD, D_MODEL), F32), jnp.broadcast_to(meta[None], (b, N_META, D_MODEL)), x],
                         axis=1).reshape(t, D_MODEL)

    n1 = _rms_fwd("rms1_fwd", h0, g1)
    (w1i,) = weights("ffn1_in", n1)
    gu1, a1 = _ffn_in_fwd("ffn1_in_fwd", n1, w1i)
    (w1o,) = weights("ffn1_out", weights("mix:forward", weights("ffn1_out:forward", a1)))
    h1, um = _mm_res_norm("ffn1_out_fwd", a1, w1o, h0, gm, alpha=0.5)
    wi, wa, wb, wo = weights("mix", h1)
    qkv = _mm_nn("proj_qkv_fwd", um, wi, out_dtype=BF16, tn_c=(768,), cols=(0, QKV_W))
    gates = _mm_nn("proj_gates_fwd", um, wi, tn_c=(1024,), cols=(P_GATES, 2 * D_MODEL), dep=weights("ffn2:forward", qkv))
    f2 = _mm_nn("proj_f_fwd", um, wi, tn_c=(128,), cols=(P_F, 128))
    qkv3 = qkv.reshape(b, l, QKV_W)
    f3 = f2.reshape(b, l, 128)
    bf_row = jnp.pad(b_forget, ((0, 0), (0, 128 - B_HEADS)))
    c_col, c_row = _fgate_fwd("fgate_fwd", f3, bf_row)
    head_of_row = jnp.arange(STACK) // BLOCK
    slopes = jnp.exp2(-8.0 * (head_of_row + 1).astype(F32) / A_HEADS).reshape(STACK, 1)
    sink_rows = jnp.repeat(sinks.reshape(A_HEADS), BLOCK).reshape(STACK, 1)
    swa_bias = _swa_bias(slopes)
    oa3, lse_a = _swa_fwd("swa_fwd", qkv3, slopes, sink_rows, swa_bias)
    ob3, lse_b = _fox_fwd("fox_fwd", qkv3, c_col, c_row)
    oa = oa3.reshape(t, A_WIDTH)
    ob = ob3.reshape(t, B_WIDTH)
    mixed, ya, yb = _branch_gate_fwd("branch_gate_fwd", oa, ob, wa, wb, gates)
    h2, n2 = _mm_res_norm("mix_out_fwd", mixed, wo, h1, g2)
    w2i, w2o = weights("ffn2", h2)
    gu2, a2 = _ffn_in_fwd("ffn2_in_fwd", n2, w2i)
    h3 = _mm_nn("ffn2_out_fwd", a2, w2o, alpha=0.5, res=h2, tm_c=(544, 384, 256, 128), tn_c=(1024,))

    dh3_3, loss_blk, dgf = _loss_head("loss_head", h3.reshape(b, l, D_MODEL), gf, tgt)
    dh3 = dh3_3.reshape(t, D_MODEL)

    def ffn_bwd(tag, dh, h_in, g_norm, n_in, gu, a, w_in_blk, w_out, one_send):
        dw_out = _mm_tn(tag + "_out_bwd_w", a, dh, alpha=0.5)
        dgu = _ffn_out_bwd_x(tag + "_out_bwd_x", dh, w_out, gu, dep=None if one_send else send(tag + "_out", (dw_out,)))
        dw_in = _ffn_in_bwd_w(tag + "_in_bwd_w", n_in, dgu)
        token = send(tag, (dw_in, dw_out)) if one_send else send(tag + "_in", (dw_in,))
        return _ffn_in_bwd_x(tag + "_in_bwd_x", dgu, w_in_blk, h_in, g_norm, dh, dep=token)

    dh2, dg2 = ffn_bwd("ffn2", dh3, h2, g2, n2, gu2, a2, w2i, w2o, True)

    dwo = _mm_tn("mix_out_bwd_w", mixed, dh2)
    dya, dyb, dgates = _mix_out_gate_bwd("mix_out_gate_bwd", dh2, wo, gates, ya, yb)
    doa = _mm_nt("branch_a_bwd_x", dya, wa, tn_c=(512,))
    dob = _mm_nt("branch_b_bwd_x", dyb, wb, out_dtype=BF16, tn_c=(512,))
    dwa = _mm_tn("branch_a_bwd_w", oa, dya, tm_c=(512,))
    dwb = _mm_tn("branch_b_bwd_w", ob, dyb, tm_c=(512,))
    dqkv3, dcq, dck = _fox_bwd("fox_bwd", qkv3, c_col, c_row, ob3, lse_b, dob.reshape(b, l, B_WIDTH))
    dqkv3, dsink = _swa_bwd("swa_bwd", qkv3, oa3, lse_a, doa.reshape(b, l, A_WIDTH), slopes, sink_rows, swa_bias, dqkv3)
    dqkv = dqkv3.reshape(t, QKV_W)
    df3, dbf = _fgate_bwd("fgate_bwd", f3, bf_row, dcq, dck)
    df = df3.reshape(t, 128)
    dwi_qkv = _mm_tn("proj_qkv_bwd_w", um, dqkv, tm_c=(512,), tn_c=(768,))
    dwi_g = _mm_tn("proj_gates_bwd_w", um, dgates, tm_c=(512,), tn_c=(512,))
    dwi_f = _mm_tn("proj_f_bwd_w", um, df, tm_c=(512,), tn_c=(128,))
    token = send("mix", (dwi_qkv, dwi_g, dwi_f, dwa, dwb, dwo))
    dh1, dgm = _proj_bwd_x("proj_bwd_x", dqkv, dgates, df, wi, h1, gm, dh2, dep=token)

    dh0, dg1 = ffn_bwd("ffn1", dh1, h0, g1, n1, gu1, a1, w1i, w1o, False)
    dh0_3 = dh0.reshape(b, l, D_MODEL)
    grad_x = dh0_3[:, PREFIX:, :]
    dmeta = dh0_3[:, N_PAD:PREFIX, :].reshape(b * N_META, D_MODEL)

    misc = jnp.concatenate([dbf[:, 0:B_HEADS], dsink[:, 0].reshape(1, A_HEADS), loss_blk[0:1, 0:1]], axis=1)
    misc = jnp.pad(misc, ((0, 0), (0, D_MODEL - misc.shape[1])))
    row = lax.broadcasted_iota(jnp.int32, (8, D_MODEL), 0)
    vec = jnp.zeros((8, D_MODEL), F32)
    for i, piece in enumerate((dg1, dgm, dg2, dgf, misc)):
        vec = jnp.where(row == i, piece, vec)
    small = jnp.concatenate([vec, dmeta], axis=0)
    return grad_x, small


def _pad_to(a, rows, cols):
    return jnp.pad(a, ((0, rows - a.shape[0]), (0, cols - a.shape[1])))


def _ffn_out_from_gathered(g):
    w = g.reshape(4, FF_SHARD, D_MODEL)
    return jnp.pad(w, ((0, 0), (0, FF_SHARD_P - FF_SHARD), (0, 0))).reshape(D_FF_P, D_MODEL)


def _ffn_out_to_scatter(dw):
    return dw.reshape(4, FF_SHARD_P, D_MODEL)[:, 0:FF_SHARD, :].reshape(N_DEV, FFO_SHARD, D_MODEL)


def _proj_segments():
    segs = [(HEAD_DIM * h, HEAD_DIM, 0, B_SEG + HEAD_DIM * A_HEAD_ORDER.index(h)) for h in range(A_HEADS)]
    segs += [(512, 128, 0, B_SEG + A_WIDTH), (640, 128, 0, B_SEG + A_WIDTH + 128)]
    for first, off in ((768, 0), (1280, 128), (1792, 256)):
        segs += [(first + 128 * hp, 128, 0, PAIR_W * hp + off) for hp in range(4)]
    segs += [(2304, B_HEADS, 2, 0), (2312, 2 * D_MODEL, 1, 0)]
    return segs


def _proj_from_gathered(g):
    def cols(first, width):
        out = []
        for p in range(N_DEV):
            lo, hi = max(first, WIN_SHARD * p), min(first + width, WIN_SHARD * (p + 1))
            if lo < hi:
                out.append(g[p, :, lo - WIN_SHARD * p:hi - WIN_SHARD * p])
        return out

    parts = []
    for arr in (0, 1, 2):
        for first, width, _, _ in sorted((s for s in _proj_segments() if s[2] == arr), key=lambda s: s[3]):
            parts += cols(first, width)
        if arr == 0:
            parts.append(jnp.zeros((D_MODEL, P_GATES - QKV_W), g.dtype))
    parts.append(jnp.zeros((D_MODEL, 128 - B_HEADS), g.dtype))
    return jnp.concatenate(parts, axis=1)


def _proj_to_scatter(dqkv_w, dg_w, df_w):
    arrays = (dqkv_w, dg_w, df_w)
    segs = sorted(_proj_segments())
    blocks = []
    for p in range(N_DEV):
        parts = []
        for first, width, arr, at in segs:
            lo, hi = max(first, WIN_SHARD * p), min(first + width, WIN_SHARD * (p + 1))
            if lo < hi:
                parts.append(arrays[arr][:, at + lo - first:at + hi - first])
        parts.append(jnp.zeros((D_MODEL, WIN_SHARD_P - WIN_SHARD), dqkv_w.dtype))
        blocks.append(jnp.concatenate(parts, axis=1))
    return jnp.stack(blocks, axis=0)


def _a_rows_from_natural(w):
    return jnp.concatenate([w[HEAD_DIM * h:HEAD_DIM * (h + 1)] for h in A_HEAD_ORDER], axis=0)


def _a_rows_to_natural(w):
    return jnp.concatenate([w[HEAD_DIM * A_HEAD_ORDER.index(h):HEAD_DIM * (A_HEAD_ORDER.index(h) + 1)]
                            for h in range(A_HEADS)], axis=0)


def kernel(x, meta_tokens, ffn1_norm, ffn1_w_in, ffn1_w_out, mix_norm, w_in, b_forget, attn_sinks, w_branch_a, w_branch_b, w_out, ffn2_norm, ffn2_w_in, ffn2_w_out, final_norm, loss_target, m_meta_tokens, m_ffn1_norm, m_ffn1_w_in, m_ffn1_w_out, m_mix_norm, m_w_in, m_b_forget, m_attn_sinks, m_w_branch_a, m_w_branch_b, m_w_out, m_ffn2_norm, m_ffn2_w_in, m_ffn2_w_out, m_final_norm, v_meta_tokens, v_ffn1_norm, v_ffn1_w_in, v_ffn1_w_out, v_mix_norm, v_w_in, v_b_forget, v_attn_sinks, v_w_branch_a, v_w_branch_b, v_w_out, v_ffn2_norm, v_ffn2_w_in, v_ffn2_w_out, v_final_norm):
    me = 4 * lax.axis_index("x") + 2 * lax.axis_index("y") + lax.axis_index("c")

    shards = (
        _pad_to(ffn1_w_in[0].astype(BF16), D_MODEL, FF_SHARD_P),
        ffn1_w_out[0].astype(BF16),
        _pad_to(w_in[0].astype(BF16), D_MODEL, WIN_SHARD_P),
        w_branch_a[0].astype(BF16), w_branch_b[0].astype(BF16), w_out[0].astype(BF16),
        _pad_to(ffn2_w_in[0].astype(BF16), D_MODEL, FF_SHARD_P),
        ffn2_w_out[0].astype(BF16),
        meta_tokens,
    )
    s1i, s1o, swi, swa, swb, swo, s2i, s2o, smeta = shards
    first_level, second_level = {}, {}
    first_level["meta"], tok = _gather2_start("gather_meta_start", (smeta,))
    first_level["ffn1_in"], tok = _gather2_start("gather_ffn1_in_start", (s1i,), after=tok)
    first_level["ffn1_out"], tok = _gather2_start("gather_ffn1_out_start", (s1o,), after=tok)
    first_level["mix"], tok = _gather2_start("gather_mix_start", (swi, swa, swb, swo), after=tok)
    first_level["ffn2"], tok = _gather2_start("gather_ffn2_start", (s2i, s2o), after=tok)
    started = {"tok": tok}

    def weights(group, after):
        if group.endswith(":forward"):
            group = group[:-len(":forward")]
            second_level[group], token = _gather2_forward("gather_" + group + "_forward", first_level[group], after)
            return token
        if group == "meta":
            after = weights("meta:forward", started["tok"])
        if group == "ffn1_in":
            after = weights("ffn1_in:forward", after)
        got = _gather2_wait("gather_" + group + "_wait", second_level[group], after)
        if group == "mix":
            gwi, gwa, gwb, gwo = got
            return (_proj_from_gathered(gwi), _a_rows_from_natural(gwa.transpose(1, 0, 2).reshape(A_WIDTH, D_MODEL)),
                    gwb.transpose(1, 0, 2).reshape(B_WIDTH, D_MODEL), gwo.reshape(D_MODEL, D_MODEL))
        if group == "meta":
            return (got[0].transpose(1, 0, 2).reshape(N_META, D_MODEL),)
        if group == "ffn1_in":
            return (got[0].reshape(2, 4, D_MODEL, FF_SHARD_P),)
        if group == "ffn1_out":
            return (_ffn_out_from_gathered(got[0]),)
        return got[0].reshape(2, 4, D_MODEL, FF_SHARD_P), _ffn_out_from_gathered(got[1])

    scatter_state = {}

    def send(group, grads):
        if group == "mix":
            dwi_qkv, dwi_g, dwi_f, dwa, dwb, dwo = grads
            dwa = _a_rows_to_natural(dwa)
            blocks = (_proj_to_scatter(dwi_qkv, dwi_g, dwi_f), dwa.reshape(A_WIDTH, N_DEV, 128).transpose(1, 0, 2),
                      dwb.reshape(B_WIDTH, N_DEV, 128).transpose(1, 0, 2), dwo.reshape(N_DEV, 128, D_MODEL))
        elif group.endswith("_in"):
            blocks = (grads[0].reshape(N_DEV, D_MODEL, FF_SHARD_P),)
        elif group.endswith("_out"):
            blocks = (_ffn_out_to_scatter(grads[0]),)
        else:
            blocks = (grads[0].reshape(N_DEV, D_MODEL, FF_SHARD_P), _ffn_out_to_scatter(grads[1]))
        scatter_state[group], token = _xchg_start("scatter_" + group + "_start", (), blocks)
        return token

    gf = final_norm.reshape(1, D_MODEL)
    grad_x, small = _local_step(x, loss_target, ffn1_norm, mix_norm, ffn2_norm, gf, b_forget, attn_sinks, weights, send)

    small_state, after = _xchg_start("gather_small_start", (small,), ())
    out = {}
    updates = (
        ("ffn2", (("ffn2_w_in", ffn2_w_in, m_ffn2_w_in, v_ffn2_w_in), ("ffn2_w_out", ffn2_w_out, m_ffn2_w_out, v_ffn2_w_out))),
        ("ffn1_out", (("ffn1_w_out", ffn1_w_out, m_ffn1_w_out, v_ffn1_w_out),)),
        ("mix", (("w_in", w_in, m_w_in, v_w_in), ("w_branch_a", w_branch_a, m_w_branch_a, v_w_branch_a),
                 ("w_branch_b", w_branch_b, m_w_branch_b, v_w_branch_b), ("w_out", w_out, m_w_out, v_w_out))),
    )
    last_update = ("ffn1_in", (("ffn1_w_in", ffn1_w_in, m_ffn1_w_in, v_ffn1_w_in),))

    def update(group, members, after):
        parts_list = _xchg_wait("scatter_" + group + "_wait", scatter_state[group], after)
        prev = None
        for (nm, w, m, v), parts in zip(members, parts_list):
            if nm.endswith("w_in"):
                res4 = _adam("adam_" + nm, w[0].T, m[0].T, v[0].T, parts, transposed=True, dep=prev)
                out[nm] = tuple(r.T[None] for r in res4)
            else:
                res4 = _adam("adam_" + nm, w[0], m[0], v[0], parts, dep=prev)
                out[nm] = tuple(r[None] for r in res4)
            prev = res4[0]
        return prev

    for group, members in updates + (last_update,):
        after = update(group, members, after)
    (packs,) = _xchg_wait("gather_small_wait", small_state, after)

    tot = _small_sum("small_sum", packs)
    loss = tot[4, 2 * B_HEADS]
    g_meta = lax.dynamic_slice(tot[8:24, :], (0, me * 128), (N_META, 128))
    out["meta_tokens"] = tuple(_adam("adam_meta_tokens", meta_tokens, m_meta_tokens, v_meta_tokens, g_meta[None]))

    def pack_small(n1, nm, n2, nf, bfv, skv):
        misc = jnp.pad(jnp.concatenate([bfv, skv], axis=1), ((0, 0), (0, D_MODEL - 2 * B_HEADS)))
        row = lax.broadcasted_iota(jnp.int32, (8, D_MODEL), 0)
        vec = jnp.zeros((8, D_MODEL), F32)
        for i, piece in enumerate((n1, nm, n2, nf.reshape(1, D_MODEL), misc)):
            vec = jnp.where(row == i, piece, vec)
        return vec

    w_pack = pack_small(ffn1_norm, mix_norm, ffn2_norm, final_norm, b_forget, attn_sinks)
    m_pack = pack_small(m_ffn1_norm, m_mix_norm, m_ffn2_norm, m_final_norm, m_b_forget, m_attn_sinks)
    v_pack = pack_small(v_ffn1_norm, v_mix_norm, v_ffn2_norm, v_final_norm, v_b_forget, v_attn_sinks)
    small4 = _adam("adam_small", w_pack, m_pack, v_pack, tot[0:8][None])
    for i, nm in enumerate(("ffn1_norm", "mix_norm", "ffn2_norm")):
        out[nm] = tuple(r[i:i + 1] for r in small4)
    out["final_norm"] = tuple(r[3] for r in small4)
    out["b_forget"] = tuple(r[4:5, 0:B_HEADS] for r in small4)
    out["attn_sinks"] = tuple(r[4:5, B_HEADS:2 * B_HEADS] for r in small4)

    names = ("meta_tokens", "ffn1_norm", "ffn1_w_in", "ffn1_w_out", "mix_norm", "w_in", "b_forget", "attn_sinks",
             "w_branch_a", "w_branch_b", "w_out", "ffn2_norm", "ffn2_w_in", "ffn2_w_out", "final_norm")
    return (loss, grad_x) + tuple(out[nm][kind] for kind in range(4) for nm in names)
```

```python
import jax
import jax.numpy as jnp
from jax import lax
from jax.experimental import pallas as pl
from jax.experimental.pallas import tpu as pltpu

F32 = jnp.float32
BF16 = jnp.bfloat16

D_MODEL = 1024
N_META = 16
BLOCK = 128
PREFIX = 128
N_PAD = PREFIX - N_META
HEAD_DIM = 64
A_HEADS = 8
A_KV_HEADS = 2
A_GROUP = 4
B_HEADS = 8
A_WIDTH = 512
A_KV_WIDTH = 128
B_WIDTH = 512
D_FF = 2816
N_DEV = 8
FF_SHARD = 2 * D_FF // N_DEV
FF_SHARD_P = 768
FFO_SHARD = D_FF // N_DEV
FFO_SHARD_P = FF_SHARD_P // 2
D_FF_P = 4 * FF_SHARD_P
W_IN_COLS = 4360
WIN_SHARD = W_IN_COLS // N_DEV
WIN_SHARD_P = 640
PAIR_W = 3 * 128
B_SEG = 4 * PAIR_W
A_SEG = A_WIDTH + 2 * A_KV_WIDTH
QKV_W = B_SEG + A_SEG
P_GATES = 2 * (2 * D_MODEL)
P_F = P_GATES + 2 * D_MODEL
PROJ_P = P_F + 128
A_HEAD_ORDER = (0, 4, 1, 5, 2, 6, 3, 7)
EPS = 1e-6
NEG = -1e30
SCALE = HEAD_DIM ** -0.5
ADAM_LR = 0.001
ADAM_B1 = 0.9
ADAM_B2 = 0.999
ADAM_EPS = 1e-08
ADAM_WD = 0.01
ADAM_STEP = 10
VMEM_LIMIT = 56 * 1024 * 1024
MESH_ID = pl.DeviceIdType.MESH
SMALL_ROWS = 40

_NN = (((1,), (0,)), ((), ()))
_NT = (((1,), (1,)), ((), ()))
_TN = (((0,), (0,)), ((), ()))


def _params(sem=None):
    return pltpu.CompilerParams(dimension_semantics=sem, vmem_limit_bytes=VMEM_LIMIT)


def _pick(n, cands):
    for c in cands:
        if n % c == 0:
            return c
    raise ValueError(f"no tile for {n}")


def _bf(v):
    return v if v.dtype == BF16 else v.astype(BF16)


def _mm(name, a, b, dims, grid, a_spec, b_spec, o_spec, out_shape, out_dtype, acc_shape, k_axis=None, nk=1,
        alpha=1.0, res=None, res_spec=None, dep=None):
    has_res = res is not None
    has_dep = dep is not None

    def body(*refs):
        a_ref, b_ref = refs[0], refs[1]
        r_ref = refs[2] if has_res else None
        o_ref = refs[2 + has_res + has_dep]

        def finish(acc):
            if alpha != 1.0:
                acc = acc * alpha
            if has_res:
                acc = acc + r_ref[...]
            o_ref[...] = acc.astype(o_ref.dtype)

        part = lax.dot_general(_bf(a_ref[...]), _bf(b_ref[...]), dims, preferred_element_type=F32)
        if nk == 1:
            finish(part)
        else:
            acc_ref = refs[-1]
            k = pl.program_id(k_axis)

            @pl.when(k == 0)
            def _():
                acc_ref[...] = part

            @pl.when(k > 0)
            def _():
                acc_ref[...] += part

            @pl.when(k == nk - 1)
            def _():
                finish(acc_ref[...])

    in_specs = [a_spec, b_spec] + ([res_spec] if has_res else []) + ([pl.BlockSpec(memory_space=pl.ANY)] if has_dep else [])
    args = (a, b) + ((res,) if has_res else ()) + ((dep,) if has_dep else ())
    sem = tuple("arbitrary" if (nk > 1 and i == k_axis) else "parallel" for i in range(len(grid)))
    return pl.pallas_call(
        body, name=name, grid=grid, in_specs=in_specs, out_specs=o_spec,
        out_shape=jax.ShapeDtypeStruct(out_shape, out_dtype),
        scratch_shapes=[pltpu.VMEM(acc_shape, F32)] if nk > 1 else [],
        compiler_params=_params(sem),
    )(*args)


def _mm_res_norm(name, a, w, res, g_next, alpha=1.0):
    t, k = a.shape
    tm = _pick(t, (544, 384, 256, 128))

    def body(a_ref, w_ref, r_ref, g_ref, h_ref, n_ref):
        acc = lax.dot_general(_bf(a_ref[...]), w_ref[...], _NN, preferred_element_type=F32)
        if alpha != 1.0:
            acc = acc * alpha
        hv = acc + r_ref[...]
        h_ref[...] = hv
        r = lax.rsqrt(jnp.mean(hv * hv, axis=-1, keepdims=True) + EPS)
        n_ref[...] = ((hv * r) * g_ref[...]).astype(BF16)

    row = pl.BlockSpec((tm, D_MODEL), lambda i: (i, 0))
    return pl.pallas_call(
        body, name=name, grid=(t // tm,),
        in_specs=[pl.BlockSpec((tm, k), lambda i: (i, 0)), pl.BlockSpec((k, D_MODEL), lambda i: (0, 0)), row,
                  pl.BlockSpec((1, D_MODEL), lambda i: (0, 0))],
        out_specs=[row, row],
        out_shape=[jax.ShapeDtypeStruct((t, D_MODEL), F32), jax.ShapeDtypeStruct((t, D_MODEL), BF16)],
        compiler_params=_params(("parallel",)),
    )(a, w, res, g_next)


def _mm_nn(name, a, b, out_dtype=F32, alpha=1.0, res=None, tn_c=(512, 640, 256, 128), cols=None,
           tm_c=(1088, 768, 512, 256, 128), dep=None):
    t, k = a.shape
    c0, n = (0, b.shape[1]) if cols is None else cols
    tm = _pick(t, tm_c)
    tn = _pick(n, tn_c)
    assert c0 % tn == 0
    jb = c0 // tn
    return _mm(name, a, b, _NN, (t // tm, n // tn),
               pl.BlockSpec((tm, k), lambda i, j: (i, 0)), pl.BlockSpec((k, tn), lambda i, j: (0, jb + j)),
               pl.BlockSpec((tm, tn), lambda i, j: (i, j)), (t, n), out_dtype, None,
               alpha=alpha, res=res, res_spec=pl.BlockSpec((tm, tn), lambda i, j: (i, j)), dep=dep)


def _mm_nt(name, a, b, out_dtype=F32, alpha=1.0, tn_c=(768, 512, 256, 128), tk_c=None, dep=None, res=None, kcols=None):
    t, k = a.shape
    n = b.shape[0]
    c0 = 0 if kcols is None else kcols[0]
    tm = _pick(t, (1088, 768, 512, 256, 128))
    tn = _pick(n, tn_c)
    tk = k if tk_c is None else _pick(k, tk_c)
    nk = k // tk
    assert c0 % tk == 0
    kb = c0 // tk
    return _mm(name, a, b, _NT, (t // tm, n // tn, nk),
               pl.BlockSpec((tm, tk), lambda i, j, kk: (i, kk)), pl.BlockSpec((tn, tk), lambda i, j, kk: (j, kb + kk)),
               pl.BlockSpec((tm, tn), lambda i, j, kk: (i, j)), (t, n), out_dtype, (tm, tn), k_axis=2, nk=nk, alpha=alpha,
               dep=dep, res=res, res_spec=pl.BlockSpec((tm, tn), lambda i, j, kk: (i, j)))


def _mm_tn(name, a, b, out_dtype=BF16, alpha=1.0, tm_c=(768, 512, 256, 128), tn_c=(512, 640, 256, 128)):
    t, m = a.shape
    n = b.shape[1]
    tm = _pick(m, tm_c)
    tn = _pick(n, tn_c)
    bytes_a, bytes_b = a.size * a.dtype.itemsize, b.size * b.dtype.itemsize
    if bytes_a + bytes_b * (m // tm) <= bytes_b + bytes_a * (n // tn):
        return _mm(name, a, b, _TN, (m // tm, n // tn),
                   pl.BlockSpec((t, tm), lambda i, j: (0, i)), pl.BlockSpec((t, tn), lambda i, j: (0, j)),
                   pl.BlockSpec((tm, tn), lambda i, j: (i, j)), (m, n), out_dtype, None, alpha=alpha)
    return _mm(name, a, b, _TN, (n // tn, m // tm),
               pl.BlockSpec((t, tm), lambda j, i: (0, i)), pl.BlockSpec((t, tn), lambda j, i: (0, j)),
               pl.BlockSpec((tm, tn), lambda j, i: (i, j)), (m, n), out_dtype, None, alpha=alpha)


def _ffn_in_fwd(name, n, wblk, dep=None):
    t = n.shape[0]
    tm = _pick(t, (1088, 768, 512, 256, 128))
    has_dep = dep is not None

    def body(n_ref, w_ref, *rest):
        gu_ref, a_ref = rest[-2], rest[-1]
        nv = n_ref[...]
        g = lax.dot_general(nv, w_ref[0], _NN, preferred_element_type=F32)
        u = lax.dot_general(nv, w_ref[1], _NN, preferred_element_type=F32)
        sg = jax.nn.sigmoid(g)
        silu = g * sg
        a_ref[...] = (silu * u).astype(BF16)
        gu_ref[0] = ((0.5 * u) * (sg + silu * (1.0 - sg))).astype(BF16)
        gu_ref[1] = (0.5 * silu).astype(BF16)

    return pl.pallas_call(
        body, name=name, grid=(t // tm, 4),
        in_specs=[pl.BlockSpec((tm, D_MODEL), lambda i, j: (i, 0)),
                  pl.BlockSpec((2, None, D_MODEL, FF_SHARD_P), lambda i, j: (0, j, 0, 0))]
        + ([pl.BlockSpec(memory_space=pl.ANY)] if has_dep else []),
        out_specs=[pl.BlockSpec((2, tm, FF_SHARD_P), lambda i, j: (0, i, j)),
                   pl.BlockSpec((tm, FF_SHARD_P), lambda i, j: (i, j))],
        out_shape=[jax.ShapeDtypeStruct((2, t, D_FF_P), BF16), jax.ShapeDtypeStruct((t, D_FF_P), BF16)],
        compiler_params=_params(("parallel", "parallel")),
    )(*((n, wblk) + ((dep,) if has_dep else ())))


def _ffn_out_bwd_x(name, dh, w_out, gu, dep=None):
    t = dh.shape[0]
    tm = _pick(t, (1088, 768, 512, 256, 128))
    has_dep = dep is not None

    def body(dh_ref, w_ref, gu_ref, *rest):
        o_ref = rest[-1]
        da = lax.dot_general(_bf(dh_ref[...]), w_ref[...], _NT, preferred_element_type=F32)
        o_ref[0] = (da * gu_ref[0].astype(F32)).astype(BF16)
        o_ref[1] = (da * gu_ref[1].astype(F32)).astype(BF16)

    gu_spec = pl.BlockSpec((2, tm, FF_SHARD_P), lambda i, j: (0, i, j))
    return pl.pallas_call(
        body, name=name, grid=(t // tm, 4),
        in_specs=[pl.BlockSpec((tm, D_MODEL), lambda i, j: (i, 0)), pl.BlockSpec((FF_SHARD_P, D_MODEL), lambda i, j: (j, 0)),
                  gu_spec] + ([pl.BlockSpec(memory_space=pl.ANY)] if has_dep else []),
        out_specs=gu_spec, out_shape=jax.ShapeDtypeStruct((2, t, D_FF_P), BF16),
        compiler_params=_params(("parallel", "parallel")),
    )(*((dh, w_out, gu) + ((dep,) if has_dep else ())))


def _rms_bwd_rows(dn, h, g, dres):
    r = lax.rsqrt(jnp.mean(h * h, axis=-1, keepdims=True) + EPS)
    tv = dn * g
    dot = jnp.mean(tv * h, axis=-1, keepdims=True)
    return dres + (r * tv - h * (r * r * r * dot)), jnp.sum(dn * (h * r), axis=0, keepdims=True)


def _accumulate_rows(ref, part, first):
    @pl.when(first)
    def _():
        ref[...] = part

    @pl.when(jnp.logical_not(first))
    def _():
        ref[...] += part


def _ffn_in_bwd_x(name, dgu, wblk, h_in, g_norm, dres, dep=None):
    t = dgu.shape[1]
    tm = _pick(t, (272, 256, 128))
    has_dep = dep is not None

    def body(d_ref, w_ref, h_ref, g_ref, r_ref, *rest):
        dh_ref, dg_ref = rest[-2], rest[-1]
        acc = None
        for s in range(2):
            for j in range(4):
                part = lax.dot_general(d_ref[s, :, FF_SHARD_P * j:FF_SHARD_P * (j + 1)], w_ref[s, j], _NT,
                                       preferred_element_type=F32)
                acc = part if acc is None else acc + part
        dh, dg = _rms_bwd_rows(acc, h_ref[...], g_ref[...], r_ref[...])
        dh_ref[...] = dh
        _accumulate_rows(dg_ref, dg, pl.program_id(0) == 0)

    row = pl.BlockSpec((tm, D_MODEL), lambda i: (i, 0))
    vec = pl.BlockSpec((1, D_MODEL), lambda i: (0, 0))
    return pl.pallas_call(
        body, name=name, grid=(t // tm,),
        in_specs=[pl.BlockSpec((2, tm, D_FF_P), lambda i: (0, i, 0)),
                  pl.BlockSpec((2, 4, D_MODEL, FF_SHARD_P), lambda i: (0, 0, 0, 0)), row, vec, row]
        + ([pl.BlockSpec(memory_space=pl.ANY)] if has_dep else []),
        out_specs=[row, vec],
        out_shape=[jax.ShapeDtypeStruct((t, D_MODEL), F32), jax.ShapeDtypeStruct((1, D_MODEL), F32)],
        compiler_params=_params(("arbitrary",)),
    )(*((dgu, wblk, h_in, g_norm, dres) + ((dep,) if has_dep else ())))


def _proj_bwd_x(name, dqkv, dgates, df, wi, h_in, g_norm, dres, dep=None):
    t = dqkv.shape[0]
    tm = _pick(t, (272, 256, 128))
    has_dep = dep is not None

    def body(q_ref, gt_ref, f_ref, w_ref, h_ref, g_ref, r_ref, *rest):
        dh_ref, dg_ref = rest[-2], rest[-1]
        acc = lax.dot_general(q_ref[...], w_ref[:, 0:QKV_W], _NT, preferred_element_type=F32)
        acc = acc + lax.dot_general(gt_ref[...], w_ref[:, P_GATES:P_F], _NT, preferred_element_type=F32)
        acc = acc + lax.dot_general(f_ref[...], w_ref[:, P_F:PROJ_P], _NT, preferred_element_type=F32)
        dh, dg = _rms_bwd_rows(acc, h_ref[...], g_ref[...], r_ref[...])
        dh_ref[...] = dh
        _accumulate_rows(dg_ref, dg, pl.program_id(0) == 0)

    row = pl.BlockSpec((tm, D_MODEL), lambda i: (i, 0))
    vec = pl.BlockSpec((1, D_MODEL), lambda i: (0, 0))
    return pl.pallas_call(
        body, name=name, grid=(t // tm,),
        in_specs=[pl.BlockSpec((tm, QKV_W), lambda i: (i, 0)), pl.BlockSpec((tm, 2 * D_MODEL), lambda i: (i, 0)),
                  pl.BlockSpec((tm, 128), lambda i: (i, 0)), pl.BlockSpec((D_MODEL, PROJ_P), lambda i: (0, 0)), row, vec, row]
        + ([pl.BlockSpec(memory_space=pl.ANY)] if has_dep else []),
        out_specs=[row, vec],
        out_shape=[jax.ShapeDtypeStruct((t, D_MODEL), F32), jax.ShapeDtypeStruct((1, D_MODEL), F32)],
        compiler_params=_params(("arbitrary",)),
    )(*((dqkv, dgates, df, wi, h_in, g_norm, dres) + ((dep,) if has_dep else ())))


def _ffn_in_bwd_w(name, n, dgu):
    t = n.shape[0]
    tk = t
    nk = 1
    return _mm(name, n, dgu, _TN, (2, 4, nk),
               pl.BlockSpec((tk, D_MODEL), lambda s, j, kk: (kk, 0)),
               pl.BlockSpec((None, tk, FF_SHARD_P), lambda s, j, kk: (s, kk, j)),
               pl.BlockSpec((None, None, D_MODEL, FF_SHARD_P), lambda s, j, kk: (s, j, 0, 0)),
               (2, 4, D_MODEL, FF_SHARD_P), BF16, (D_MODEL, FF_SHARD_P), k_axis=2, nk=nk)


def _rms_fwd(name, h, g):
    t = h.shape[0]
    tm = _pick(t, (544, 384, 256, 128))

    def body(h_ref, g_ref, o_ref):
        hv = h_ref[...]
        r = lax.rsqrt(jnp.mean(hv * hv, axis=-1, keepdims=True) + EPS)
        o_ref[...] = ((hv * r) * g_ref[...]).astype(BF16)

    return pl.pallas_call(
        body, name=name, grid=(t // tm,),
        in_specs=[pl.BlockSpec((tm, D_MODEL), lambda i: (i, 0)), pl.BlockSpec((1, D_MODEL), lambda i: (0, 0))],
        out_specs=pl.BlockSpec((tm, D_MODEL), lambda i: (i, 0)),
        out_shape=jax.ShapeDtypeStruct((t, D_MODEL), BF16), compiler_params=_params(("parallel",)),
    )(h, g)


def _rms_bwd(name, h, g, dn, dres):
    t = h.shape[0]
    tm = _pick(t, (544, 384, 256, 128))

    def body(h_ref, g_ref, dn_ref, dres_ref, dh_ref, dg_ref):
        i = pl.program_id(0)
        hv = h_ref[...]
        dnv = dn_ref[...]
        r = lax.rsqrt(jnp.mean(hv * hv, axis=-1, keepdims=True) + EPS)
        tv = dnv * g_ref[...]
        dot = jnp.mean(tv * hv, axis=-1, keepdims=True)
        dh_ref[...] = dres_ref[...] + (r * tv - hv * (r * r * r * dot))
        part = jnp.sum(dnv * (hv * r), axis=0, keepdims=True)

        @pl.when(i == 0)
        def _():
            dg_ref[...] = part

        @pl.when(i > 0)
        def _():
            dg_ref[...] += part

    row = pl.BlockSpec((tm, D_MODEL), lambda i: (i, 0))
    vec = pl.BlockSpec((1, D_MODEL), lambda i: (0, 0))
    return pl.pallas_call(
        body, name=name, grid=(t // tm,), in_specs=[row, vec, row, row], out_specs=[row, vec],
        out_shape=[jax.ShapeDtypeStruct((t, D_MODEL), F32), jax.ShapeDtypeStruct((1, D_MODEL), F32)],
        compiler_params=_params(("arbitrary",)),
    )(h, g, dn, dres)


def _branch_gate_fwd(name, oa, ob, wa, wb, gates):
    t = gates.shape[0]
    tm = _pick(t, (544, 384, 256, 128))

    def body(oa_ref, ob_ref, wa_ref, wb_ref, g_ref, o_ref, ya_ref, yb_ref):
        ya = lax.dot_general(_bf(oa_ref[...]), wa_ref[...], _NN, preferred_element_type=F32)
        yb = lax.dot_general(_bf(ob_ref[...]), wb_ref[...], _NN, preferred_element_type=F32)
        sa = jax.nn.sigmoid(g_ref[:, 0:D_MODEL])
        sb = jax.nn.sigmoid(g_ref[:, D_MODEL:2 * D_MODEL])
        o_ref[...] = (sa * ya + sb * yb).astype(BF16)
        ya_ref[...] = ya.astype(BF16)
        yb_ref[...] = yb.astype(BF16)

    blk = pl.BlockSpec((tm, D_MODEL), lambda i: (i, 0))
    narrow = pl.BlockSpec((tm, A_WIDTH), lambda i: (i, 0))
    wide = pl.BlockSpec((tm, 2 * D_MODEL), lambda i: (i, 0))
    wspec = pl.BlockSpec((A_WIDTH, D_MODEL), lambda i: (0, 0))
    out = jax.ShapeDtypeStruct((t, D_MODEL), BF16)
    return pl.pallas_call(
        body, name=name, grid=(t // tm,), in_specs=[narrow, narrow, wspec, wspec, wide], out_specs=[blk, blk, blk],
        out_shape=[out, out, out], compiler_params=_params(("parallel",)),
    )(oa, ob, wa, wb, gates)


def _mix_out_gate_bwd(name, dh, wo, gates, ya, yb):
    t = gates.shape[0]
    tm = _pick(t, (544, 384, 256, 128))

    def body(dh_ref, w_ref, g_ref, ya_ref, yb_ref, dya_ref, dyb_ref, dg_ref):
        dm = lax.dot_general(_bf(dh_ref[...]), w_ref[...], _NT, preferred_element_type=F32)
        sa = jax.nn.sigmoid(g_ref[:, 0:D_MODEL])
        sb = jax.nn.sigmoid(g_ref[:, D_MODEL:2 * D_MODEL])
        dya_ref[...] = (dm * sa).astype(BF16)
        dyb_ref[...] = (dm * sb).astype(BF16)
        dg_ref[:, 0:D_MODEL] = (dm * ya_ref[...].astype(F32) * (sa * (1.0 - sa))).astype(BF16)
        dg_ref[:, D_MODEL:2 * D_MODEL] = (dm * yb_ref[...].astype(F32) * (sb * (1.0 - sb))).astype(BF16)

    blk = pl.BlockSpec((tm, D_MODEL), lambda i: (i, 0))
    wide = pl.BlockSpec((tm, 2 * D_MODEL), lambda i: (i, 0))
    out = jax.ShapeDtypeStruct((t, D_MODEL), BF16)
    return pl.pallas_call(
        body, name=name, grid=(t // tm,),
        in_specs=[blk, pl.BlockSpec((D_MODEL, D_MODEL), lambda i: (0, 0)), wide, blk, blk], out_specs=[blk, blk, wide],
        out_shape=[out, out, jax.ShapeDtypeStruct((t, 2 * D_MODEL), BF16)], compiler_params=_params(("parallel",)),
    )(dh, wo, gates, ya, yb)


def _loss_head(name, h3, gf, tgt):
    b, l, _ = h3.shape
    nb = l // BLOCK

    def body(h_ref, g_ref, t_ref, dh_ref, loss_ref, dg_ref):
        first = (pl.program_id(0) == 0) & (pl.program_id(1) == 0)
        real = (pl.program_id(1) > 0).astype(F32)
        hv = h_ref[...]
        g = g_ref[...]
        r = lax.rsqrt(jnp.mean(hv * hv, axis=-1, keepdims=True) + EPS)
        xn = hv * r
        err = (xn * g - t_ref[...]) * real
        lpart = 0.5 * jnp.sum(jnp.mean(err * err, axis=-1, keepdims=True), axis=0, keepdims=True)
        dy = err * (1.0 / D_MODEL)
        tv = dy * g
        dot = jnp.mean(tv * hv, axis=-1, keepdims=True)
        dh_ref[...] = r * tv - hv * (r * r * r * dot)
        gpart = jnp.sum(dy * xn, axis=0, keepdims=True)

        @pl.when(first)
        def _():
            loss_ref[...] = jnp.zeros_like(loss_ref)
            dg_ref[...] = jnp.zeros_like(dg_ref)

        loss_ref[...] += jnp.broadcast_to(lpart, loss_ref.shape)
        dg_ref[...] += gpart

    return pl.pallas_call(
        body, name=name, grid=(b, nb),
        in_specs=[pl.BlockSpec((None, BLOCK, D_MODEL), lambda bi, n: (bi, n, 0)),
                  pl.BlockSpec((1, D_MODEL), lambda bi, n: (0, 0)),
                  pl.BlockSpec((None, BLOCK, D_MODEL), lambda bi, n: (bi, jnp.maximum(n - 1, 0), 0))],
        out_specs=[pl.BlockSpec((None, BLOCK, D_MODEL), lambda bi, n: (bi, n, 0)),
                   pl.BlockSpec((8, 128), lambda bi, n: (0, 0)),
                   pl.BlockSpec((1, D_MODEL), lambda bi, n: (0, 0))],
        out_shape=[jax.ShapeDtypeStruct(h3.shape, F32), jax.ShapeDtypeStruct((8, 128), F32),
                   jax.ShapeDtypeStruct((1, D_MODEL), F32)],
        compiler_params=_params(("arbitrary", "arbitrary")),
    )(h3, gf, tgt)


def _fgate_fwd(name, f3, bf_row):
    b, l, _ = f3.shape
    nb = l // BLOCK

    def body(f_ref, b_ref, cc_ref, cr_ref):
        r_i = lax.broadcasted_iota(jnp.int32, (BLOCK, BLOCK), 0)
        c_i = lax.broadcasted_iota(jnp.int32, (BLOCK, BLOCK), 1)
        tri = (r_i >= c_i).astype(F32)
        carry = jnp.zeros((1, 128), F32)
        for blk in range(nb):
            rows = slice(blk * BLOCK, (blk + 1) * BLOCK)
            z = f_ref[rows, :] + b_ref[...]
            lf = jnp.minimum(z, 0.0) - jnp.log(1.0 + jnp.exp(-jnp.abs(z)))
            cb = jnp.dot(tri, lf, preferred_element_type=F32, precision=lax.Precision.HIGHEST) + carry
            carry = cb[BLOCK - 1:BLOCK, :]
            cbt = cb.T
            for hh in range(B_HEADS):
                cc_ref[hh, rows, :] = jnp.sum(jnp.where(c_i == hh, cb, 0.0), axis=1, keepdims=True)
                cr_ref[hh, :, rows] = cbt[hh:hh + 1, :]

    return pl.pallas_call(
        body, name=name, grid=(b,),
        in_specs=[pl.BlockSpec((None, l, 128), lambda bi: (bi, 0, 0)),
                  pl.BlockSpec((1, 128), lambda bi: (0, 0))],
        out_specs=[pl.BlockSpec((None, B_HEADS, l, 1), lambda bi: (bi, 0, 0, 0)),
                   pl.BlockSpec((None, B_HEADS, 1, l), lambda bi: (bi, 0, 0, 0))],
        out_shape=[jax.ShapeDtypeStruct((b, B_HEADS, l, 1), F32), jax.ShapeDtypeStruct((b, B_HEADS, 1, l), F32)],
        compiler_params=_params(("parallel",)),
    )(f3, bf_row)


def _fgate_bwd(name, f3, bf_row, dcq, dck):
    b, l, _ = f3.shape
    nb = l // BLOCK

    def body(f_ref, b_ref, dcq_ref, dck_ref, df_ref, db_ref):
        r_i = lax.broadcasted_iota(jnp.int32, (BLOCK, BLOCK), 0)
        c_i = lax.broadcasted_iota(jnp.int32, (BLOCK, BLOCK), 1)
        tri = (r_i <= c_i).astype(F32)
        carry = jnp.zeros((1, 128), F32)
        total = jnp.zeros((1, 128), F32)
        for blk in range(nb - 1, -1, -1):
            rows = slice(blk * BLOCK, (blk + 1) * BLOCK)
            krows = jnp.concatenate([dck_ref[hh, :, rows] for hh in range(B_HEADS)]
                                    + [jnp.zeros((BLOCK - B_HEADS, BLOCK), F32)], axis=0)
            dcb = krows.T
            for hh in range(B_HEADS):
                dcb = dcb + jnp.where(c_i == hh, dcq_ref[hh, rows, :], 0.0)
            rc = jnp.dot(tri, dcb, preferred_element_type=F32, precision=lax.Precision.HIGHEST) + carry
            carry = rc[0:1, :]
            z = f_ref[rows, :] + b_ref[...]
            df = rc * (1.0 / (1.0 + jnp.exp(z)))
            df_ref[rows, :] = df.astype(BF16)
            total = total + jnp.sum(df, axis=0, keepdims=True)

        @pl.when(pl.program_id(0) == 0)
        def _():
            db_ref[...] = total

        @pl.when(pl.program_id(0) > 0)
        def _():
            db_ref[...] += total

    return pl.pallas_call(
        body, name=name, grid=(b,),
        in_specs=[pl.BlockSpec((None, l, 128), lambda bi: (bi, 0, 0)),
                  pl.BlockSpec((1, 128), lambda bi: (0, 0)),
                  pl.BlockSpec((None, B_HEADS, l, 1), lambda bi: (bi, 0, 0, 0)),
                  pl.BlockSpec((None, B_HEADS, 1, l), lambda bi: (bi, 0, 0, 0))],
        out_specs=[pl.BlockSpec((None, l, 128), lambda bi: (bi, 0, 0)), pl.BlockSpec((1, 128), lambda bi: (0, 0))],
        out_shape=[jax.ShapeDtypeStruct((b, l, 128), BF16), jax.ShapeDtypeStruct((1, 128), F32)],
        compiler_params=_params(("arbitrary",)),
    )(f3, bf_row, dcq, dck)


A_Q_BLK = B_SEG // A_WIDTH
A_K_BLK = (B_SEG + A_WIDTH) // 128
A_V_BLK = A_K_BLK + 1
A_SEG_BLK = B_SEG // A_SEG
STACK = A_HEADS * BLOCK


def _lane_lo():
    return lax.broadcasted_iota(jnp.int32, (1, 128), 1) < HEAD_DIM


def _stack_heads(x, masked):
    lo = _lane_lo()
    blks = [x[:, 128 * j:128 * (j + 1)] for j in range(4)]
    if not masked:
        return jnp.concatenate(blks + blks, axis=0)
    zero = jnp.zeros_like(blks[0])
    return jnp.concatenate([jnp.where(lo, bk, zero) for bk in blks] + [jnp.where(lo, zero, bk) for bk in blks], axis=0)


def _unstack_heads(y):
    lo = _lane_lo()
    return jnp.concatenate([jnp.where(lo, y[128 * j:128 * (j + 1)], y[128 * (4 + j):128 * (5 + j)]) for j in range(4)], axis=1)


def _swa_bias(slopes):
    r_i = jnp.arange(STACK)[:, None]
    c_i = jnp.arange(3 * BLOCK)[None, :]
    seg = c_i >> 7
    out = []
    for n in range(3):
        qpos = n * BLOCK + (r_i & (BLOCK - 1))
        kpos = jnp.where(seg == 0, c_i, (n - 2) * BLOCK + c_i)
        dist = qpos - kpos
        band = (seg != 0) & (dist < BLOCK) & (kpos >= PREFIX)
        meta = (seg == 0) & (c_i >= N_PAD)
        out.append(jnp.where((dist >= 0) & (band | meta), -slopes * dist.astype(F32), NEG))
    return jnp.stack(out, axis=0)


def _swa_scores(q, kcat, n, slope, bias):
    s = lax.dot_general(q, kcat, _NT, preferred_element_type=F32) + bias
    further = slope * (-BLOCK * jnp.maximum(n - 2, 0)).astype(F32)
    return jnp.concatenate([s[:, 0:BLOCK] + further, s[:, BLOCK:]], axis=1)


def _swa_specs():
    def kv(col_blk):
        return [pl.BlockSpec((None, BLOCK, 128), lambda b, n: (b, 0, col_blk)),
                pl.BlockSpec((None, BLOCK, 128), lambda b, n: (b, jnp.maximum(n - 1, 0), col_blk)),
                pl.BlockSpec((None, BLOCK, 128), lambda b, n: (b, n, col_blk))]

    q_spec = pl.BlockSpec((None, BLOCK, A_WIDTH), lambda b, n: (b, n, A_Q_BLK))
    o_spec = pl.BlockSpec((None, BLOCK, A_WIDTH), lambda b, n: (b, n, 0))
    col = pl.BlockSpec((STACK, 1), lambda b, n: (0, 0))
    bias = pl.BlockSpec((None, STACK, 3 * BLOCK), lambda b, n: (jnp.minimum(n, 2), 0, 0))
    lse_spec = pl.BlockSpec((None, A_HEADS, BLOCK, 1), lambda b, n: (b, 0, n, 0))
    return q_spec, kv(A_K_BLK), kv(A_V_BLK), o_spec, [col, col, bias], lse_spec


def _swa_fwd(name, qkv, slopes, sinks, bias):
    b, l, _ = qkv.shape
    nb = l // BLOCK

    def body(q_ref, k0_ref, kp_ref, kc_ref, v0_ref, vp_ref, vc_ref, sl_ref, sk_ref, bias_ref, o_ref, lse_ref):
        n = pl.program_id(1)
        qs = _stack_heads(q_ref[...], True) * SCALE
        kcat = jnp.concatenate([k0_ref[...], kp_ref[...], kc_ref[...]], axis=0)
        vcat = jnp.concatenate([v0_ref[...], vp_ref[...], vc_ref[...]], axis=0)
        s = _swa_scores(qs, kcat, n, sl_ref[...], bias_ref[...])
        sink = sk_ref[...]
        m = jnp.maximum(jnp.max(s, axis=-1, keepdims=True), sink)
        p = jnp.exp(s - m)
        den = jnp.sum(p, axis=-1, keepdims=True) + jnp.exp(sink - m)
        o = lax.dot_general(p.astype(BF16), vcat, _NN, preferred_element_type=F32) / den
        o_ref[...] = _unstack_heads(o)
        lse_ref[...] = (m + jnp.log(den)).reshape(A_HEADS, BLOCK, 1)

    q_spec, k_specs, v_specs, o_spec, consts, lse_spec = _swa_specs()
    return pl.pallas_call(
        body, name=name, grid=(b, nb),
        in_specs=[q_spec] + k_specs + v_specs + consts, out_specs=[o_spec, lse_spec],
        out_shape=[jax.ShapeDtypeStruct((b, l, A_WIDTH), F32), jax.ShapeDtypeStruct((b, A_HEADS, l, 1), F32)],
        compiler_params=_params(("parallel", "parallel")),
    )(qkv, qkv, qkv, qkv, qkv, qkv, qkv, slopes, sinks, bias)


def _swa_bwd(name, qkv, o, lse, do, slopes, sinks, bias, dqkv):
    b, l, _ = qkv.shape
    nb = l // BLOCK

    def body(q_ref, k0_ref, kp_ref, kc_ref, v0_ref, vp_ref, vc_ref, o_ref, lse_ref, do_ref, sl_ref, sk_ref, bias_ref, _,
             dx_ref, ds_ref, dk_acc, dv_acc):
        bi = pl.program_id(0)
        n = pl.program_id(1)
        qs = _stack_heads(q_ref[...], True) * SCALE
        dos32 = _stack_heads(do_ref[...], True)
        dos = dos32.astype(BF16)
        os_ = _stack_heads(o_ref[...], False)
        lsev = lse_ref[...].reshape(STACK, 1)
        kcat = jnp.concatenate([k0_ref[...], kp_ref[...], kc_ref[...]], axis=0)
        vcat = jnp.concatenate([v0_ref[...], vp_ref[...], vc_ref[...]], axis=0)
        s = _swa_scores(qs, kcat, n, sl_ref[...], bias_ref[...])
        p = jnp.exp(s - lsev)
        dsum = jnp.sum(dos32 * os_, axis=-1, keepdims=True)
        dp = lax.dot_general(dos, vcat, _NT, preferred_element_type=F32)
        dsc = (p * (dp - dsum)).astype(BF16)
        dq = lax.dot_general(dsc, kcat, _NN, preferred_element_type=F32) * SCALE
        row0 = pl.multiple_of(n * BLOCK, BLOCK)
        dx_ref[pl.ds(row0, BLOCK), 0:A_WIDTH] = _unstack_heads(dq).astype(BF16)
        dkc = lax.dot_general(dsc, qs, _TN, preferred_element_type=F32)
        dvc = lax.dot_general(p.astype(BF16), dos, _TN, preferred_element_type=F32)

        @pl.when(n == 0)
        def _():
            dk_acc[...] = jnp.zeros_like(dk_acc)
            dv_acc[...] = jnp.zeros_like(dv_acc)

        starts = (0, pl.multiple_of(jnp.maximum(n - 1, 0) * BLOCK, BLOCK), row0)
        for t, st in enumerate(starts):
            dk_acc[pl.ds(st, BLOCK), :] += dkc[t * BLOCK:(t + 1) * BLOCK, :]
            dv_acc[pl.ds(st, BLOCK), :] += dvc[t * BLOCK:(t + 1) * BLOCK, :]

        @pl.when(n == nb - 1)
        def _():
            dx_ref[:, A_WIDTH:A_WIDTH + 128] = dk_acc[...].astype(BF16)
            dx_ref[:, A_WIDTH + 128:A_SEG] = dv_acc[...].astype(BF16)

        dsink = -(jnp.exp(sk_ref[...] - lsev) * dsum)
        r8 = lax.broadcasted_iota(jnp.int32, (8, 128), 0)
        acc = jnp.zeros((8, 128), F32)
        for hh in range(A_HEADS):
            acc = acc + jnp.where(r8 == hh, jnp.sum(dsink[hh * BLOCK:(hh + 1) * BLOCK, :]), 0.0)

        @pl.when((bi == 0) & (n == 0))
        def _():
            ds_ref[...] = jnp.zeros_like(ds_ref)

        ds_ref[...] += acc

    q_spec, k_specs, v_specs, o_spec, consts, lse_spec = _swa_specs()
    return pl.pallas_call(
        body, name=name, grid=(b, nb),
        in_specs=[q_spec] + k_specs + v_specs + [o_spec, lse_spec, o_spec] + consts + [pl.BlockSpec(memory_space=pl.ANY)],
        out_specs=[pl.BlockSpec((None, l, A_SEG), lambda bb, n: (bb, 0, A_SEG_BLK)),
                   pl.BlockSpec((8, 128), lambda bb, n: (0, 0))],
        out_shape=[jax.ShapeDtypeStruct(dqkv.shape, BF16), jax.ShapeDtypeStruct((8, 128), F32)],
        scratch_shapes=[pltpu.VMEM((l, 128), F32), pltpu.VMEM((l, 128), F32)],
        input_output_aliases={13: 0},
        compiler_params=_params(("arbitrary", "arbitrary")),
    )(qkv, qkv, qkv, qkv, qkv, qkv, qkv, o, lse, do, slopes, sinks, bias, dqkv)


def _fox_mask(qk, ck, i):
    kh = qk.shape[1]
    qpos = i * BLOCK + lax.broadcasted_iota(jnp.int32, (BLOCK, kh), 0)
    kpos = lax.broadcasted_iota(jnp.int32, (BLOCK, kh), 1)
    return jnp.where((kpos <= qpos) & (kpos >= N_PAD), qk - ck, NEG)


def _pick_head(x, hh):
    lo = _lane_lo()
    return jnp.where(lo if hh == 0 else jnp.logical_not(lo), x, jnp.zeros_like(x))


def _both_heads(x):
    return jnp.concatenate([_pick_head(x, 0), _pick_head(x, 1)], axis=0)


def _fox_specs(l):
    pair = pl.BlockSpec((None, l, PAIR_W), lambda bi, hp: (bi, 0, hp))
    half = pl.BlockSpec((None, l, 128), lambda bi, hp: (bi, 0, hp))
    colv = pl.BlockSpec((None, 2, l, 1), lambda bi, hp: (bi, hp, 0, 0))
    rowv = pl.BlockSpec((None, 2, 1, l), lambda bi, hp: (bi, hp, 0, 0))
    return pair, half, colv, rowv


def _fox_fwd(name, qkv, c_col, c_row):
    b, l, _ = qkv.shape
    nb = l // BLOCK

    def body(x_ref, cc_ref, cr_ref, o_ref, lse_ref):
        for i in range(nb):
            rows = slice(i * BLOCK, (i + 1) * BLOCK)
            kh = (i + 1) * BLOCK
            qblk = x_ref[rows, 0:128]
            kv = x_ref[0:kh, 128:256]
            vv = x_ref[0:kh, 256:384]
            qk = lax.dot_general(_both_heads(qblk) * SCALE, kv, _NT, preferred_element_type=F32)
            ps, dens = [], []
            for hh in range(2):
                s = _fox_mask(qk[hh * BLOCK:(hh + 1) * BLOCK], cr_ref[hh, :, 0:kh], i)
                m = jnp.max(s, axis=-1, keepdims=True)
                p = jnp.exp(s - m)
                den = jnp.sum(p, axis=-1, keepdims=True)
                ps.append(p.astype(BF16))
                dens.append(den)
                lse_ref[hh, rows, :] = (m + jnp.log(den)) + cc_ref[hh, rows, :]
            pv = lax.dot_general(jnp.concatenate(ps, axis=0), vv, _NN, preferred_element_type=F32)
            o_ref[rows, :] = jnp.where(_lane_lo(), pv[0:BLOCK] / dens[0], pv[BLOCK:2 * BLOCK] / dens[1]).astype(BF16)

    pair, half, colv, rowv = _fox_specs(l)
    return pl.pallas_call(
        body, name=name, grid=(b, 4), in_specs=[pair, colv, rowv], out_specs=[half, colv],
        out_shape=[jax.ShapeDtypeStruct((b, l, B_WIDTH), BF16), jax.ShapeDtypeStruct((b, B_HEADS, l, 1), F32)],
        compiler_params=_params(("parallel", "parallel")),
    )(qkv, c_col, c_row)


def _fox_bwd(name, qkv, c_col, c_row, o, lse, do):
    b, l, _ = qkv.shape
    nb = l // BLOCK

    def body(x_ref, cc_ref, cr_ref, o_ref, lse_ref, do_ref, dx_ref, dcq_ref, dck_ref, dk_acc, dv_acc):
        dk_acc[...] = jnp.zeros_like(dk_acc)
        dv_acc[...] = jnp.zeros_like(dv_acc)
        dck_ref[...] = jnp.zeros_like(dck_ref)
        for i in range(nb):
            rows = slice(i * BLOCK, (i + 1) * BLOCK)
            kh = (i + 1) * BLOCK
            qblk = x_ref[rows, 0:128]
            kv = x_ref[0:kh, 128:256]
            vv = x_ref[0:kh, 256:384]
            doblk = do_ref[rows, :]
            ov = o_ref[rows, :].astype(F32)
            q2 = _both_heads(qblk) * SCALE
            do2 = _both_heads(doblk)
            qk = lax.dot_general(q2, kv, _NT, preferred_element_type=F32)
            dp = lax.dot_general(do2, vv, _NT, preferred_element_type=F32)
            ps, dss = [], []
            for hh in range(2):
                half = slice(hh * BLOCK, (hh + 1) * BLOCK)
                s = _fox_mask(qk[half], cr_ref[hh, :, 0:kh], i)
                p = jnp.exp(s - (lse_ref[hh, rows, :] - cc_ref[hh, rows, :]))
                dsum = jnp.sum(do2[half].astype(F32) * ov, axis=-1, keepdims=True)
                ds = p * (dp[half] - dsum)
                ps.append(p.astype(BF16))
                dss.append(ds.astype(BF16))
                dcq_ref[hh, rows, :] = jnp.sum(ds, axis=-1, keepdims=True)
                dck_ref[hh, :, 0:kh] -= jnp.sum(ds, axis=0, keepdims=True)
            p2 = jnp.concatenate(ps, axis=0)
            ds2 = jnp.concatenate(dss, axis=0)
            dq = lax.dot_general(ds2, kv, _NN, preferred_element_type=F32) * SCALE
            dk_acc[0:kh, :] += lax.dot_general(ds2, q2, _TN, preferred_element_type=F32)
            dv_acc[0:kh, :] += lax.dot_general(p2, do2, _TN, preferred_element_type=F32)
            dx_ref[rows, 0:128] = jnp.where(_lane_lo(), dq[0:BLOCK], dq[BLOCK:2 * BLOCK]).astype(BF16)
        dx_ref[:, 128:256] = dk_acc[...].astype(BF16)
        dx_ref[:, 256:384] = dv_acc[...].astype(BF16)

    pair, half, colv, rowv = _fox_specs(l)
    return pl.pallas_call(
        body, name=name, grid=(b, 4), in_specs=[pair, colv, rowv, half, colv, half],
        out_specs=[pair, colv, rowv],
        out_shape=[jax.ShapeDtypeStruct(qkv.shape, BF16), jax.ShapeDtypeStruct((b, B_HEADS, l, 1), F32),
                   jax.ShapeDtypeStruct((b, B_HEADS, 1, l), F32)],
        scratch_shapes=[pltpu.VMEM((l, 128), F32), pltpu.VMEM((l, 128), F32)],
        compiler_params=_params(("parallel", "parallel")),
    )(qkv, c_col, c_row, o, lse, do)


_FLIPS = ((0, 0, 1), (0, 1, 0), (0, 1, 1), (1, 0, 0), (1, 0, 1), (1, 1, 0), (1, 1, 1))


def _exchange(name, gather, scatter):
    ng, ns = len(gather), len(scatter)
    na = ng + ns
    npeer = len(_FLIPS)

    def body(*refs):
        ins = refs[:na]
        outs = refs[na:2 * na]
        send_sems, recv_sems, loc_sems = refs[2 * na:]
        x, y, c = lax.axis_index("x"), lax.axis_index("y"), lax.axis_index("c")
        me = 4 * x + 2 * y + c
        peers = []
        for fx, fy, fc in _FLIPS:
            px = 1 - x if fx else x
            py = 1 - y if fy else y
            pc = 1 - c if fc else c
            peers.append(((px, py, pc), 4 * px + 2 * py + pc))

        def remote(a, kk):
            dev, lin = peers[kk]
            src = ins[a] if a < ng else ins[a].at[lin]
            return pltpu.make_async_remote_copy(src_ref=src, dst_ref=outs[a].at[me], send_sem=send_sems.at[a * npeer + kk],
                                                recv_sem=recv_sems.at[a * npeer + kk], device_id=dev, device_id_type=MESH_ID)

        def arrival(a, kk):
            dev, lin = peers[kk]
            src = ins[a] if a < ng else ins[a].at[lin]
            return pltpu.make_async_remote_copy(src_ref=src, dst_ref=outs[a].at[lin], send_sem=send_sems.at[a * npeer + kk],
                                                recv_sem=recv_sems.at[a * npeer + kk], device_id=dev, device_id_type=MESH_ID)

        local = []
        for a in range(na):
            src = ins[a] if a < ng else ins[a].at[me]
            cp = pltpu.make_async_copy(src, outs[a].at[me], loc_sems.at[a])
            cp.start()
            local.append(cp)
        sent = [remote(a, kk) for kk in range(npeer) for a in range(na)]
        for cp in sent:
            cp.start()
        for kk in range(npeer):
            for a in range(na):
                arrival(a, kk).wait_recv()
        for cp in sent:
            cp.wait_send()
        for cp in local:
            cp.wait()

    arrs = list(gather) + list(scatter)
    out_shape = [jax.ShapeDtypeStruct((N_DEV,) + tuple(a.shape), a.dtype) for a in gather]
    out_shape += [jax.ShapeDtypeStruct(tuple(a.shape), a.dtype) for a in scatter]
    anyspec = pl.BlockSpec(memory_space=pl.ANY)
    return pl.pallas_call(
        body, name=name, in_specs=[anyspec] * na, out_specs=[anyspec] * na, out_shape=out_shape,
        scratch_shapes=[pltpu.SemaphoreType.DMA((na * npeer,)), pltpu.SemaphoreType.DMA((na * npeer,)),
                        pltpu.SemaphoreType.DMA((na,))],
        compiler_params=pltpu.CompilerParams(has_side_effects=True),
    )(*arrs)


def _peer_table():
    x, y, c = lax.axis_index("x"), lax.axis_index("y"), lax.axis_index("c")
    me = 4 * x + 2 * y + c
    peers = []
    for fx, fy, fc in _FLIPS:
        px = 1 - x if fx else x
        py = 1 - y if fy else y
        pc = 1 - c if fc else c
        peers.append(((px, py, pc), 4 * px + 2 * py + pc))
    return me, peers


_HBM = pl.BlockSpec(memory_space=pltpu.HBM)
_SEM = pl.BlockSpec(memory_space=pltpu.SEMAPHORE)
_ANY = pl.BlockSpec(memory_space=pl.ANY)
_EFFECT = pltpu.SideEffectType.DATAFLOW_SIDE_EFFECTING


def _split_copy(srcs_are_pieces, src_refs, land_refs, send_sem, recv_sem, a, kk, me, peers, arriving):
    dev, lin = peers[kk]
    npeer = len(_FLIPS)
    src = src_refs[a] if srcs_are_pieces[a] else src_refs[a].at[lin]
    dst = land_refs[a].at[lin] if arriving else land_refs[a].at[me]
    return pltpu.make_async_remote_copy(src_ref=src, dst_ref=dst, send_sem=send_sem.at[a * npeer + kk],
                                        recv_sem=recv_sem.at[a * npeer + kk], device_id=dev, device_id_type=MESH_ID)


def _xchg_start(name, gather, scatter, after=None):
    me_out = 4 * lax.axis_index("x") + 2 * lax.axis_index("y") + lax.axis_index("c")
    srcs = list(gather) + list(scatter)
    is_piece = [True] * len(gather) + [False] * len(scatter)
    lands = []
    for a, piece in zip(srcs, is_piece):
        own = a[None] if piece else lax.dynamic_slice_in_dim(a, me_out, 1, axis=0)
        shape = ((N_DEV,) + tuple(a.shape)) if piece else tuple(a.shape)
        start = (me_out,) + (0,) * (len(shape) - 1)
        lands.append(lax.dynamic_update_slice(lax.empty(shape, a.dtype), own, start))
    n = len(srcs)
    nsem = n * len(_FLIPS)
    has_after = after is not None

    def body(*refs):
        src_refs = refs[:n]
        land_refs = refs[n:2 * n]
        outs = refs[2 * n + (1 if has_after else 0):]
        send_sem, recv_sem = outs[0], outs[1]
        token = outs[-1]
        me, peers = _peer_table()
        for kk in range(len(_FLIPS)):
            for a in range(n):
                _split_copy(is_piece, src_refs, land_refs, send_sem, recv_sem, a, kk, me, peers, False).start()
        token[...] = jnp.zeros_like(token)

    out_shape = ([pltpu.SemaphoreType.DMA((nsem,)), pltpu.SemaphoreType.DMA((nsem,))]
                 + [pltpu.HBM(tuple(a.shape), a.dtype) for a in srcs] + [pltpu.HBM(tuple(a.shape), a.dtype) for a in lands]
                 + [jax.ShapeDtypeStruct((8, 128), F32)])
    args = [pltpu.with_memory_space_constraint(a, pltpu.HBM) for a in srcs + lands] + ([after] if has_after else [])
    res = pl.pallas_call(
        body, name=name, out_shape=out_shape,
        in_specs=[_HBM] * (2 * n) + ([_ANY] if has_after else []),
        out_specs=[_SEM, _SEM] + [_HBM] * (2 * n) + [pl.BlockSpec(memory_space=pltpu.VMEM)],
        input_output_aliases={i: 2 + i for i in range(2 * n)},
        compiler_params=pltpu.CompilerParams(has_side_effects=_EFFECT),
    )(*args)
    state = (res[0], res[1], list(res[2:2 + n]), list(res[2 + n:2 + 2 * n]), is_piece)
    return state, res[-1]


def _xchg_wait(name, state, after):
    send_sem, recv_sem, srcs, lands, is_piece = state
    n = len(srcs)

    def body(*refs):
        src_refs = refs[:n]
        land_refs = refs[n:2 * n]
        s_sem, r_sem = refs[2 * n], refs[2 * n + 1]
        me, peers = _peer_table()
        for kk in range(len(_FLIPS)):
            for a in range(n):
                cp = _split_copy(is_piece, src_refs, land_refs, s_sem, r_sem, a, kk, me, peers, True)
                cp.wait_send()
                cp.wait_recv()

    out_shape = [pltpu.HBM(tuple(a.shape), a.dtype) for a in srcs] + [pltpu.HBM(tuple(a.shape), a.dtype) for a in lands]
    res = pl.pallas_call(
        body, name=name, out_shape=out_shape,
        in_specs=[_HBM] * (2 * n) + [_SEM, _SEM, _ANY], out_specs=[_HBM] * (2 * n),
        input_output_aliases={i: i for i in range(2 * n)},
        compiler_params=pltpu.CompilerParams(has_side_effects=_EFFECT),
    )(*srcs, *lands, send_sem, recv_sem, after)
    return list(res[n:])


_SIB = (0, 0, 1)
_ICI = ((0, 1, 0), (1, 0, 0), (1, 1, 0))


def _flip(fl):
    x, y, c = lax.axis_index("x"), lax.axis_index("y"), lax.axis_index("c")
    px = 1 - x if fl[0] else x
    py = 1 - y if fl[1] else y
    pc = 1 - c if fl[2] else c
    return (px, py, pc), 4 * px + 2 * py + pc


def _gather2_start(name, pieces, after=None):
    me_out = 4 * lax.axis_index("x") + 2 * lax.axis_index("y") + lax.axis_index("c")
    pieces = list(pieces)
    n = len(pieces)
    lands = [lax.dynamic_update_slice(lax.empty((N_DEV,) + tuple(a.shape), a.dtype), a[None],
                                      (me_out,) + (0,) * a.ndim) for a in pieces]
    first = (_SIB,) + _ICI
    has_after = after is not None

    def body(*refs):
        src_refs, land_refs = refs[:n], refs[n:2 * n]
        outs = refs[2 * n + (1 if has_after else 0):]
        send_sem, recv_sem, token = outs[0], outs[1], outs[-1]
        _, me = _flip((0, 0, 0))
        for kk, fl in enumerate(first):
            dev, _ = _flip(fl)
            for a in range(n):
                pltpu.make_async_remote_copy(src_ref=src_refs[a], dst_ref=land_refs[a].at[me],
                                             send_sem=send_sem.at[a * 4 + kk], recv_sem=recv_sem.at[a * 4 + kk],
                                             device_id=dev, device_id_type=MESH_ID).start()
        token[...] = jnp.zeros_like(token)

    hbm = [pltpu.HBM(tuple(a.shape), a.dtype) for a in pieces + lands]
    res = pl.pallas_call(
        body, name=name,
        out_shape=[pltpu.SemaphoreType.DMA((4 * n,)), pltpu.SemaphoreType.DMA((4 * n,))] + hbm
        + [jax.ShapeDtypeStruct((8, 128), F32)],
        in_specs=[_HBM] * (2 * n) + ([_ANY] if has_after else []),
        out_specs=[_SEM, _SEM] + [_HBM] * (2 * n) + [pl.BlockSpec(memory_space=pltpu.VMEM)],
        input_output_aliases={i: 2 + i for i in range(2 * n)},
        compiler_params=pltpu.CompilerParams(has_side_effects=_EFFECT),
    )(*([pltpu.with_memory_space_constraint(a, pltpu.HBM) for a in pieces + lands] + ([after] if has_after else [])))
    return (res[0], res[1], list(res[2:2 + n]), list(res[2 + n:2 + 2 * n])), res[-1]


def _gather2_forward(name, state, after):
    send_a, recv_a, pieces, lands = state
    n = len(pieces)
    first = (_SIB,) + _ICI

    def body(*refs):
        src_refs, land_refs = refs[:n], refs[n:2 * n]
        s_a, r_a = refs[2 * n], refs[2 * n + 1]
        outs = refs[2 * n + 3:]
        send_b, recv_b, token = outs[0], outs[1], outs[-1]
        for kk, fl in enumerate(first):
            dev, lin = _flip(fl)
            for a in range(n):
                cp = pltpu.make_async_remote_copy(src_ref=src_refs[a], dst_ref=land_refs[a].at[lin],
                                                  send_sem=s_a.at[a * 4 + kk], recv_sem=r_a.at[a * 4 + kk],
                                                  device_id=dev, device_id_type=MESH_ID)
                cp.wait_send()
                cp.wait_recv()
        sib, _ = _flip(_SIB)
        for j, fl in enumerate(_ICI):
            _, lin = _flip(fl)
            for a in range(n):
                pltpu.make_async_remote_copy(src_ref=land_refs[a].at[lin], dst_ref=land_refs[a].at[lin],
                                             send_sem=send_b.at[a * 3 + j], recv_sem=recv_b.at[a * 3 + j],
                                             device_id=sib, device_id_type=MESH_ID).start()
        token[...] = jnp.zeros_like(token)

    hbm = [pltpu.HBM(tuple(a.shape), a.dtype) for a in pieces + lands]
    res = pl.pallas_call(
        body, name=name,
        out_shape=[pltpu.SemaphoreType.DMA((3 * n,)), pltpu.SemaphoreType.DMA((3 * n,))] + hbm
        + [jax.ShapeDtypeStruct((8, 128), F32)],
        in_specs=[_HBM] * (2 * n) + [_SEM, _SEM, _ANY],
        out_specs=[_SEM, _SEM] + [_HBM] * (2 * n) + [pl.BlockSpec(memory_space=pltpu.VMEM)],
        input_output_aliases={i: 2 + i for i in range(2 * n)},
        compiler_params=pltpu.CompilerParams(has_side_effects=_EFFECT),
    )(*pieces, *lands, send_a, recv_a, after)
    return (res[0], res[1], list(res[2 + n:2 + 2 * n])), res[-1]


def _gather2_wait(name, state, after):
    send_b, recv_b, lands = state
    n = len(lands)

    def body(*refs):
        land_refs = refs[:n]
        s_b, r_b = refs[n], refs[n + 1]
        sib, _ = _flip(_SIB)
        for j, fl in enumerate(_ICI):
            _, sent = _flip(fl)
            _, arriving = _flip((fl[0], fl[1], 1))
            for a in range(n):
                cp = pltpu.make_async_remote_copy(src_ref=land_refs[a].at[sent], dst_ref=land_refs[a].at[arriving],
                                                  send_sem=s_b.at[a * 3 + j], recv_sem=r_b.at[a * 3 + j],
                                                  device_id=sib, device_id_type=MESH_ID)
                cp.wait_send()
                cp.wait_recv()

    res = pl.pallas_call(
        body, name=name, out_shape=[pltpu.HBM(tuple(a.shape), a.dtype) for a in lands],
        in_specs=[_HBM] * n + [_SEM, _SEM, _ANY], out_specs=[_HBM] * n,
        input_output_aliases={i: i for i in range(n)},
        compiler_params=pltpu.CompilerParams(has_side_effects=_EFFECT),
    )(*lands, send_b, recv_b, after)
    return list(res)


def _adam_math(w, g, m, v):
    m = ADAM_B1 * m + (1.0 - ADAM_B1) * g
    v = ADAM_B2 * v + (1.0 - ADAM_B2) * (g * g)
    m_hat = m / (1.0 - ADAM_B1 ** ADAM_STEP)
    v_hat = v / (1.0 - ADAM_B2 ** ADAM_STEP)
    delta = -ADAM_LR * (m_hat / (jnp.sqrt(v_hat) + ADAM_EPS) + ADAM_WD * w)
    return delta, m, v


def _adam(name, w, m, v, parts, transposed=False, dep=None):
    npart, _, cp = parts.shape
    has_dep = dep is not None
    if transposed:
        c, r = w.shape
        tr = _pick(r, (256, 128))
        blk = pl.BlockSpec((c, tr), lambda i: (0, i))
    else:
        r, c = w.shape
        tr = _pick(r, (256, 176, 128, 64, 16, 8, 1))
        blk = pl.BlockSpec((tr, c), lambda i: (i, 0))

    def body(w_ref, m_ref, v_ref, p_ref, *rest):
        g_ref, d_ref, mo_ref, vo_ref = rest[-4:]
        g = p_ref[0].astype(F32)
        for pp in range(1, npart):
            g = g + p_ref[pp].astype(F32)
        g = g.T[0:c, :] if transposed else g[:, 0:c]
        delta, mn, vn = _adam_math(w_ref[...], g, m_ref[...], v_ref[...])
        g_ref[...] = g
        d_ref[...] = delta
        mo_ref[...] = mn
        vo_ref[...] = vn

    out = jax.ShapeDtypeStruct(w.shape, F32)
    return pl.pallas_call(
        body, name=name, grid=(r // tr,),
        in_specs=[blk, blk, blk, pl.BlockSpec((npart, tr, cp), lambda i: (0, i, 0))]
        + ([pl.BlockSpec(memory_space=pl.ANY)] if has_dep else []),
        out_specs=[blk, blk, blk, blk], out_shape=[out, out, out, out], compiler_params=_params(("parallel",)),
    )(*((w, m, v, parts) + ((dep,) if has_dep else ())))


def _small_sum(name, packs):
    def body(p_ref, o_ref):
        tot = p_ref[0]
        for pp in range(1, N_DEV):
            tot = tot + p_ref[pp]
        o_ref[0:8, :] = tot[0:8, :]
        o_ref[8:24, :] = tot[8:24, :] + tot[24:40, :]

    return pl.pallas_call(body, name=name, out_shape=jax.ShapeDtypeStruct((24, D_MODEL), F32),
                          compiler_params=_params())(packs)


def _local_step(x, tgt, g1, gm, g2, gf, b_forget, sinks, weights, send):
    b, s, _ = x.shape
    l = s + PREFIX
    t = b * l
    (meta,) = weights("meta", x)
    h0 = jnp.concatenate([jnp.zeros((b, N_PAD, D_MODEL), F32), jnp.broadcast_to(meta[None], (b, N_META, D_MODEL)), x],
                         axis=1).reshape(t, D_MODEL)

    n1 = _rms_fwd("rms1_fwd", h0, g1)
    (w1i,) = weights("ffn1_in", n1)
    gu1, a1 = _ffn_in_fwd("ffn1_in_fwd", n1, w1i)
    w1o, wi, wa, wb, wo = weights("mix", weights("mix:forward", a1))
    h1, um = _mm_res_norm("ffn1_out_fwd", a1, w1o, h0, gm, alpha=0.5)
    qkv = _mm_nn("proj_qkv_fwd", um, wi, out_dtype=BF16, tn_c=(768,), cols=(0, QKV_W))
    gates = _mm_nn("proj_gates_fwd", um, wi, tn_c=(1024,), cols=(P_GATES, 2 * D_MODEL), dep=weights("ffn2:forward", qkv))
    f2 = _mm_nn("proj_f_fwd", um, wi, tn_c=(128,), cols=(P_F, 128))
    qkv3 = qkv.reshape(b, l, QKV_W)
    f3 = f2.reshape(b, l, 128)
    bf_row = jnp.pad(b_forget, ((0, 0), (0, 128 - B_HEADS)))
    c_col, c_row = _fgate_fwd("fgate_fwd", f3, bf_row)
    head_of_row = jnp.arange(STACK) // BLOCK
    slopes = jnp.exp2(-8.0 * (head_of_row + 1).astype(F32) / A_HEADS).reshape(STACK, 1)
    sink_rows = jnp.repeat(sinks.reshape(A_HEADS), BLOCK).reshape(STACK, 1)
    swa_bias = _swa_bias(slopes)
    oa3, lse_a = _swa_fwd("swa_fwd", qkv3, slopes, sink_rows, swa_bias)
    ob3, lse_b = _fox_fwd("fox_fwd", qkv3, c_col, c_row)
    oa = oa3.reshape(t, A_WIDTH)
    ob = ob3.reshape(t, B_WIDTH)
    mixed, ya, yb = _branch_gate_fwd("branch_gate_fwd", oa, ob, wa, wb, gates)
    h2, n2 = _mm_res_norm("mix_out_fwd", mixed, wo, h1, g2)
    w2i, w2o = weights("ffn2", h2)
    gu2, a2 = _ffn_in_fwd("ffn2_in_fwd", n2, w2i)
    h3 = _mm_nn("ffn2_out_fwd", a2, w2o, alpha=0.5, res=h2, tm_c=(544, 384, 256, 128), tn_c=(1024,))

    dh3_3, loss_blk, dgf = _loss_head("loss_head", h3.reshape(b, l, D_MODEL), gf, tgt)
    dh3 = dh3_3.reshape(t, D_MODEL)

    def ffn_bwd(tag, dh, h_in, g_norm, n_in, gu, a, w_in_blk, w_out, one_send):
        dw_out = _mm_tn(tag + "_out_bwd_w", a, dh, alpha=0.5)
        dgu = _ffn_out_bwd_x(tag + "_out_bwd_x", dh, w_out, gu, dep=None if one_send else send(tag + "_out", (dw_out,)))
        dw_in = _ffn_in_bwd_w(tag + "_in_bwd_w", n_in, dgu)
        token = send(tag, (dw_in, dw_out)) if one_send else send(tag + "_in", (dw_in,))
        return _ffn_in_bwd_x(tag + "_in_bwd_x", dgu, w_in_blk, h_in, g_norm, dh, dep=token)

    dh2, dg2 = ffn_bwd("ffn2", dh3, h2, g2, n2, gu2, a2, w2i, w2o, True)

    dwo = _mm_tn("mix_out_bwd_w", mixed, dh2)
    dya, dyb, dgates = _mix_out_gate_bwd("mix_out_gate_bwd", dh2, wo, gates, ya, yb)
    doa = _mm_nt("branch_a_bwd_x", dya, wa, tn_c=(512,))
    dob = _mm_nt("branch_b_bwd_x", dyb, wb, out_dtype=BF16, tn_c=(512,))
    dwa = _mm_tn("branch_a_bwd_w", oa, dya, tm_c=(512,))
    dwb = _mm_tn("branch_b_bwd_w", ob, dyb, tm_c=(512,))
    dqkv3, dcq, dck = _fox_bwd("fox_bwd", qkv3, c_col, c_row, ob3, lse_b, dob.reshape(b, l, B_WIDTH))
    dqkv3, dsink = _swa_bwd("swa_bwd", qkv3, oa3, lse_a, doa.reshape(b, l, A_WIDTH), slopes, sink_rows, swa_bias, dqkv3)
    dqkv = dqkv3.reshape(t, QKV_W)
    df3, dbf = _fgate_bwd("fgate_bwd", f3, bf_row, dcq, dck)
    df = df3.reshape(t, 128)
    dwi_qkv = _mm_tn("proj_qkv_bwd_w", um, dqkv, tm_c=(512,), tn_c=(768,))
    dwi_g = _mm_tn("proj_gates_bwd_w", um, dgates, tm_c=(512,), tn_c=(512,))
    dwi_f = _mm_tn("proj_f_bwd_w", um, df, tm_c=(512,), tn_c=(128,))
    token = send("mix", (dwi_qkv, dwi_g, dwi_f, dwa, dwb, dwo))
    dh1, dgm = _proj_bwd_x("proj_bwd_x", dqkv, dgates, df, wi, h1, gm, dh2, dep=token)

    dh0, dg1 = ffn_bwd("ffn1", dh1, h0, g1, n1, gu1, a1, w1i, w1o, False)
    dh0_3 = dh0.reshape(b, l, D_MODEL)
    grad_x = dh0_3[:, PREFIX:, :]
    dmeta = dh0_3[:, N_PAD:PREFIX, :].reshape(b * N_META, D_MODEL)

    misc = jnp.concatenate([dbf[:, 0:B_HEADS], dsink[:, 0].reshape(1, A_HEADS), loss_blk[0:1, 0:1]], axis=1)
    misc = jnp.pad(misc, ((0, 0), (0, D_MODEL - misc.shape[1])))
    row = lax.broadcasted_iota(jnp.int32, (8, D_MODEL), 0)
    vec = jnp.zeros((8, D_MODEL), F32)
    for i, piece in enumerate((dg1, dgm, dg2, dgf, misc)):
        vec = jnp.where(row == i, piece, vec)
    small = jnp.concatenate([vec, dmeta], axis=0)
    return grad_x, small


def _pad_to(a, rows, cols):
    return jnp.pad(a, ((0, rows - a.shape[0]), (0, cols - a.shape[1])))


def _ffn_out_from_gathered(g):
    w = g.reshape(4, FF_SHARD, D_MODEL)
    return jnp.pad(w, ((0, 0), (0, FF_SHARD_P - FF_SHARD), (0, 0))).reshape(D_FF_P, D_MODEL)


def _ffn_out_to_scatter(dw):
    return dw.reshape(4, FF_SHARD_P, D_MODEL)[:, 0:FF_SHARD, :].reshape(N_DEV, FFO_SHARD, D_MODEL)


def _proj_segments():
    segs = [(HEAD_DIM * h, HEAD_DIM, 0, B_SEG + HEAD_DIM * A_HEAD_ORDER.index(h)) for h in range(A_HEADS)]
    segs += [(512, 128, 0, B_SEG + A_WIDTH), (640, 128, 0, B_SEG + A_WIDTH + 128)]
    for first, off in ((768, 0), (1280, 128), (1792, 256)):
        segs += [(first + 128 * hp, 128, 0, PAIR_W * hp + off) for hp in range(4)]
    segs += [(2304, B_HEADS, 2, 0), (2312, 2 * D_MODEL, 1, 0)]
    return segs


def _proj_from_gathered(g):
    def cols(first, width):
        out = []
        for p in range(N_DEV):
            lo, hi = max(first, WIN_SHARD * p), min(first + width, WIN_SHARD * (p + 1))
            if lo < hi:
                out.append(g[p, :, lo - WIN_SHARD * p:hi - WIN_SHARD * p])
        return out

    parts = []
    for arr in (0, 1, 2):
        for first, width, _, _ in sorted((s for s in _proj_segments() if s[2] == arr), key=lambda s: s[3]):
            parts += cols(first, width)
        if arr == 0:
            parts.append(jnp.zeros((D_MODEL, P_GATES - QKV_W), g.dtype))
    parts.append(jnp.zeros((D_MODEL, 128 - B_HEADS), g.dtype))
    return jnp.concatenate(parts, axis=1)


def _proj_to_scatter(dqkv_w, dg_w, df_w):
    arrays = (dqkv_w, dg_w, df_w)
    segs = sorted(_proj_segments())
    blocks = []
    for p in range(N_DEV):
        parts = []
        for first, width, arr, at in segs:
            lo, hi = max(first, WIN_SHARD * p), min(first + width, WIN_SHARD * (p + 1))
            if lo < hi:
                parts.append(arrays[arr][:, at + lo - first:at + hi - first])
        parts.append(jnp.zeros((D_MODEL, WIN_SHARD_P - WIN_SHARD), dqkv_w.dtype))
        blocks.append(jnp.concatenate(parts, axis=1))
    return jnp.stack(blocks, axis=0)


def _a_rows_from_natural(w):
    return jnp.concatenate([w[HEAD_DIM * h:HEAD_DIM * (h + 1)] for h in A_HEAD_ORDER], axis=0)


def _a_rows_to_natural(w):
    return jnp.concatenate([w[HEAD_DIM * A_HEAD_ORDER.index(h):HEAD_DIM * (A_HEAD_ORDER.index(h) + 1)]
                            for h in range(A_HEADS)], axis=0)


def kernel(x, meta_tokens, ffn1_norm, ffn1_w_in, ffn1_w_out, mix_norm, w_in, b_forget, attn_sinks, w_branch_a, w_branch_b, w_out, ffn2_norm, ffn2_w_in, ffn2_w_out, final_norm, loss_target, m_meta_tokens, m_ffn1_norm, m_ffn1_w_in, m_ffn1_w_out, m_mix_norm, m_w_in, m_b_forget, m_attn_sinks, m_w_branch_a, m_w_branch_b, m_w_out, m_ffn2_norm, m_ffn2_w_in, m_ffn2_w_out, m_final_norm, v_meta_tokens, v_ffn1_norm, v_ffn1_w_in, v_ffn1_w_out, v_mix_norm, v_w_in, v_b_forget, v_attn_sinks, v_w_branch_a, v_w_branch_b, v_w_out, v_ffn2_norm, v_ffn2_w_in, v_ffn2_w_out, v_final_norm):
    me = 4 * lax.axis_index("x") + 2 * lax.axis_index("y") + lax.axis_index("c")

    shards = (
        _pad_to(ffn1_w_in[0].astype(BF16), D_MODEL, FF_SHARD_P),
        ffn1_w_out[0].astype(BF16),
        _pad_to(w_in[0].astype(BF16), D_MODEL, WIN_SHARD_P),
        w_branch_a[0].astype(BF16), w_branch_b[0].astype(BF16), w_out[0].astype(BF16),
        _pad_to(ffn2_w_in[0].astype(BF16), D_MODEL, FF_SHARD_P),
        ffn2_w_out[0].astype(BF16),
        meta_tokens,
    )
    s1i, s1o, swi, swa, swb, swo, s2i, s2o, smeta = shards
    first_level, second_level = {}, {}
    first_level["meta"], tok = _gather2_start("gather_meta_start", (smeta,))
    first_level["ffn1_in"], tok = _gather2_start("gather_ffn1_in_start", (s1i,), after=tok)
    first_level["mix"], tok = _gather2_start("gather_mix_start", (s1o, swi, swa, swb, swo), after=tok)
    first_level["ffn2"], tok = _gather2_start("gather_ffn2_start", (s2i, s2o), after=tok)
    started = {"tok": tok}

    def weights(group, after):
        if group.endswith(":forward"):
            group = group[:-len(":forward")]
            second_level[group], token = _gather2_forward("gather_" + group + "_forward", first_level[group], after)
            return token
        if group == "meta":
            after = weights("meta:forward", started["tok"])
        if group == "ffn1_in":
            after = weights("ffn1_in:forward", after)
        got = _gather2_wait("gather_" + group + "_wait", second_level[group], after)
        if group == "mix":
            g1o, gwi, gwa, gwb, gwo = got
            return (_ffn_out_from_gathered(g1o), _proj_from_gathered(gwi),
                    _a_rows_from_natural(gwa.transpose(1, 0, 2).reshape(A_WIDTH, D_MODEL)),
                    gwb.transpose(1, 0, 2).reshape(B_WIDTH, D_MODEL), gwo.reshape(D_MODEL, D_MODEL))
        if group == "meta":
            return (got[0].transpose(1, 0, 2).reshape(N_META, D_MODEL),)
        if group == "ffn1_in":
            return (got[0].reshape(2, 4, D_MODEL, FF_SHARD_P),)
        return got[0].reshape(2, 4, D_MODEL, FF_SHARD_P), _ffn_out_from_gathered(got[1])

    scatter_state = {}

    def send(group, grads):
        if group == "mix":
            dwi_qkv, dwi_g, dwi_f, dwa, dwb, dwo = grads
            dwa = _a_rows_to_natural(dwa)
            blocks = (_proj_to_scatter(dwi_qkv, dwi_g, dwi_f), dwa.reshape(A_WIDTH, N_DEV, 128).transpose(1, 0, 2),
                      dwb.reshape(B_WIDTH, N_DEV, 128).transpose(1, 0, 2), dwo.reshape(N_DEV, 128, D_MODEL))
        elif group.endswith("_in"):
            blocks = (grads[0].reshape(N_DEV, D_MODEL, FF_SHARD_P),)
        elif group.endswith("_out"):
            blocks = (_ffn_out_to_scatter(grads[0]),)
        else:
            blocks = (grads[0].reshape(N_DEV, D_MODEL, FF_SHARD_P), _ffn_out_to_scatter(grads[1]))
        scatter_state[group], token = _xchg_start("scatter_" + group + "_start", (), blocks)
        return token

    gf = final_norm.reshape(1, D_MODEL)
    grad_x, small = _local_step(x, loss_target, ffn1_norm, mix_norm, ffn2_norm, gf, b_forget, attn_sinks, weights, send)

    small_state, after = _xchg_start("gather_small_start", (small,), ())
    out = {}
    updates = (
        ("ffn2", (("ffn2_w_in", ffn2_w_in, m_ffn2_w_in, v_ffn2_w_in), ("ffn2_w_out", ffn2_w_out, m_ffn2_w_out, v_ffn2_w_out))),
        ("ffn1_out", (("ffn1_w_out", ffn1_w_out, m_ffn1_w_out, v_ffn1_w_out),)),
        ("mix", (("w_in", w_in, m_w_in, v_w_in), ("w_branch_a", w_branch_a, m_w_branch_a, v_w_branch_a),
                 ("w_branch_b", w_branch_b, m_w_branch_b, v_w_branch_b), ("w_out", w_out, m_w_out, v_w_out))),
    )
    last_update = ("ffn1_in", (("ffn1_w_in", ffn1_w_in, m_ffn1_w_in, v_ffn1_w_in),))

    def update(group, members, after):
        parts_list = _xchg_wait("scatter_" + group + "_wait", scatter_state[group], after)
        prev = None
        for (nm, w, m, v), parts in zip(members, parts_list):
            if nm.endswith("w_in"):
                res4 = _adam("adam_" + nm, w[0].T, m[0].T, v[0].T, parts, transposed=True, dep=prev)
                out[nm] = tuple(r.T[None] for r in res4)
            else:
                res4 = _adam("adam_" + nm, w[0], m[0], v[0], parts, dep=prev)
                out[nm] = tuple(r[None] for r in res4)
            prev = res4[0]
        return prev

    for group, members in updates + (last_update,):
        after = update(group, members, after)
    (packs,) = _xchg_wait("gather_small_wait", small_state, after)

    tot = _small_sum("small_sum", packs)
    loss = tot[4, 2 * B_HEADS]
    g_meta = lax.dynamic_slice(tot[8:24, :], (0, me * 128), (N_META, 128))
    out["meta_tokens"] = tuple(_adam("adam_meta_tokens", meta_tokens, m_meta_tokens, v_meta_tokens, g_meta[None]))

    def pack_small(n1, nm, n2, nf, bfv, skv):
        misc = jnp.pad(jnp.concatenate([bfv, skv], axis=1), ((0, 0), (0, D_MODEL - 2 * B_HEADS)))
        row = lax.broadcasted_iota(jnp.int32, (8, D_MODEL), 0)
        vec = jnp.zeros((8, D_MODEL), F32)
        for i, piece in enumerate((n1, nm, n2, nf.reshape(1, D_MODEL), misc)):
            vec = jnp.where(row == i, piece, vec)
        return vec

    w_pack = pack_small(ffn1_norm, mix_norm, ffn2_norm, final_norm, b_forget, attn_sinks)
    m_pack = pack_small(m_ffn1_norm, m_mix_norm, m_ffn2_norm, m_final_norm, m_b_forget, m_attn_sinks)
    v_pack = pack_small(v_ffn1_norm, v_mix_norm, v_ffn2_norm, v_final_norm, v_b_forget, v_attn_sinks)
    small4 = _adam("adam_small", w_pack, m_pack, v_pack, tot[0:8][None])
    for i, nm in enumerate(("ffn1_norm", "mix_norm", "ffn2_norm")):
        out[nm] = tuple(r[i:i + 1] for r in small4)
    out["final_norm"] = tuple(r[3] for r in small4)
    out["b_forget"] = tuple(r[4:5, 0:B_HEADS] for r in small4)
    out["attn_sinks"] = tuple(r[4:5, B_HEADS:2 * B_HEADS] for r in small4)

    names = ("meta_tokens", "ffn1_norm", "ffn1_w_in", "ffn1_w_out", "mix_norm", "w_in", "b_forget", "attn_sinks",
             "w_branch_a", "w_branch_b", "w_out", "ffn2_norm", "ffn2_w_in", "ffn2_w_out", "final_norm")
    return (loss, grad_x) + tuple(out[nm][kind] for kind in range(4) for nm in names)
```

```python
import jax
import jax.numpy as jnp
from jax import lax
from jax.experimental import pallas as pl
from jax.experimental.pallas import tpu as pltpu

F32 = jnp.float32
BF16 = jnp.bfloat16

D_MODEL = 1024
N_META = 16
BLOCK = 128
PREFIX = 128
N_PAD = PREFIX - N_META
HEAD_DIM = 64
A_HEADS = 8
A_KV_HEADS = 2
A_GROUP = 4
B_HEADS = 8
A_WIDTH = 512
A_KV_WIDTH = 128
B_WIDTH = 512
D_FF = 2816
N_DEV = 8
FF_SHARD = 2 * D_FF // N_DEV
FF_SHARD_P = 768
FFO_SHARD = D_FF // N_DEV
FFO_SHARD_P = FF_SHARD_P // 2
D_FF_P = 4 * FF_SHARD_P
W_IN_COLS = 4360
WIN_SHARD = W_IN_COLS // N_DEV
WIN_SHARD_P = 640
PAIR_W = 3 * 128
B_SEG = 4 * PAIR_W
A_SEG = A_WIDTH + 2 * A_KV_WIDTH
QKV_W = B_SEG + A_SEG
P_GATES = 2 * (2 * D_MODEL)
P_F = P_GATES + 2 * D_MODEL
PROJ_P = P_F + 128
A_HEAD_ORDER = (0, 4, 1, 5, 2, 6, 3, 7)
EPS = 1e-6
NEG = -1e30
SCALE = HEAD_DIM ** -0.5
ADAM_LR = 0.001
ADAM_B1 = 0.9
ADAM_B2 = 0.999
ADAM_EPS = 1e-08
ADAM_WD = 0.01
ADAM_STEP = 10
VMEM_LIMIT = 56 * 1024 * 1024
MESH_ID = pl.DeviceIdType.MESH
SMALL_ROWS = 40

_NN = (((1,), (0,)), ((), ()))
_NT = (((1,), (1,)), ((), ()))
_TN = (((0,), (0,)), ((), ()))


def _params(sem=None):
    return pltpu.CompilerParams(dimension_semantics=sem, vmem_limit_bytes=VMEM_LIMIT)


def _pick(n, cands):
    for c in cands:
        if n % c == 0:
            return c
    raise ValueError(f"no tile for {n}")


def _bf(v):
    return v if v.dtype == BF16 else v.astype(BF16)


def _mm(name, a, b, dims, grid, a_spec, b_spec, o_spec, out_shape, out_dtype, acc_shape, k_axis=None, nk=1,
        alpha=1.0, res=None, res_spec=None, dep=None):
    has_res = res is not None
    has_dep = dep is not None

    def body(*refs):
        a_ref, b_ref = refs[0], refs[1]
        r_ref = refs[2] if has_res else None
        o_ref = refs[2 + has_res + has_dep]

        def finish(acc):
            if alpha != 1.0:
                acc = acc * alpha
            if has_res:
                acc = acc + r_ref[...]
            o_ref[...] = acc.astype(o_ref.dtype)

        part = lax.dot_general(_bf(a_ref[...]), _bf(b_ref[...]), dims, preferred_element_type=F32)
        if nk == 1:
            finish(part)
        else:
            acc_ref = refs[-1]
            k = pl.program_id(k_axis)

            @pl.when(k == 0)
            def _():
                acc_ref[...] = part

            @pl.when(k > 0)
            def _():
                acc_ref[...] += part

            @pl.when(k == nk - 1)
            def _():
                finish(acc_ref[...])

    in_specs = [a_spec, b_spec] + ([res_spec] if has_res else []) + ([pl.BlockSpec(memory_space=pl.ANY)] if has_dep else [])
    args = (a, b) + ((res,) if has_res else ()) + ((dep,) if has_dep else ())
    sem = tuple("arbitrary" if (nk > 1 and i == k_axis) else "parallel" for i in range(len(grid)))
    return pl.pallas_call(
        body, name=name, grid=grid, in_specs=in_specs, out_specs=o_spec,
        out_shape=jax.ShapeDtypeStruct(out_shape, out_dtype),
        scratch_shapes=[pltpu.VMEM(acc_shape, F32)] if nk > 1 else [],
        compiler_params=_params(sem),
    )(*args)


def _mm_res_norm(name, a, w, res, g_next, alpha=1.0):
    t, k = a.shape
    tm = _pick(t, (544, 384, 256, 128))

    def body(a_ref, w_ref, r_ref, g_ref, h_ref, n_ref):
        acc = lax.dot_general(_bf(a_ref[...]), w_ref[...], _NN, preferred_element_type=F32)
        if alpha != 1.0:
            acc = acc * alpha
        hv = acc + r_ref[...]
        h_ref[...] = hv
        r = lax.rsqrt(jnp.mean(hv * hv, axis=-1, keepdims=True) + EPS)
        n_ref[...] = ((hv * r) * g_ref[...]).astype(BF16)

    row = pl.BlockSpec((tm, D_MODEL), lambda i: (i, 0))
    return pl.pallas_call(
        body, name=name, grid=(t // tm,),
        in_specs=[pl.BlockSpec((tm, k), lambda i: (i, 0)), pl.BlockSpec((k, D_MODEL), lambda i: (0, 0)), row,
                  pl.BlockSpec((1, D_MODEL), lambda i: (0, 0))],
        out_specs=[row, row],
        out_shape=[jax.ShapeDtypeStruct((t, D_MODEL), F32), jax.ShapeDtypeStruct((t, D_MODEL), BF16)],
        compiler_params=_params(("parallel",)),
    )(a, w, res, g_next)


def _mm_nn(name, a, b, out_dtype=F32, alpha=1.0, res=None, tn_c=(512, 640, 256, 128), cols=None,
           tm_c=(1088, 768, 512, 256, 128), dep=None):
    t, k = a.shape
    c0, n = (0, b.shape[1]) if cols is None else cols
    tm = _pick(t, tm_c)
    tn = _pick(n, tn_c)
    assert c0 % tn == 0
    jb = c0 // tn
    return _mm(name, a, b, _NN, (t // tm, n // tn),
               pl.BlockSpec((tm, k), lambda i, j: (i, 0)), pl.BlockSpec((k, tn), lambda i, j: (0, jb + j)),
               pl.BlockSpec((tm, tn), lambda i, j: (i, j)), (t, n), out_dtype, None,
               alpha=alpha, res=res, res_spec=pl.BlockSpec((tm, tn), lambda i, j: (i, j)), dep=dep)


def _mm_nt(name, a, b, out_dtype=F32, alpha=1.0, tn_c=(768, 512, 256, 128), tk_c=None, dep=None, res=None, kcols=None):
    t, k = a.shape
    n = b.shape[0]
    c0 = 0 if kcols is None else kcols[0]
    tm = _pick(t, (1088, 768, 512, 256, 128))
    tn = _pick(n, tn_c)
    tk = k if tk_c is None else _pick(k, tk_c)
    nk = k // tk
    assert c0 % tk == 0
    kb = c0 // tk
    return _mm(name, a, b, _NT, (t // tm, n // tn, nk),
               pl.BlockSpec((tm, tk), lambda i, j, kk: (i, kk)), pl.BlockSpec((tn, tk), lambda i, j, kk: (j, kb + kk)),
               pl.BlockSpec((tm, tn), lambda i, j, kk: (i, j)), (t, n), out_dtype, (tm, tn), k_axis=2, nk=nk, alpha=alpha,
               dep=dep, res=res, res_spec=pl.BlockSpec((tm, tn), lambda i, j, kk: (i, j)))


def _mm_tn(name, a, b, out_dtype=BF16, alpha=1.0, tm_c=(768, 512, 256, 128), tn_c=(512, 640, 256, 128)):
    t, m = a.shape
    n = b.shape[1]
    tm = _pick(m, tm_c)
    tn = _pick(n, tn_c)
    bytes_a, bytes_b = a.size * a.dtype.itemsize, b.size * b.dtype.itemsize
    if bytes_a + bytes_b * (m // tm) <= bytes_b + bytes_a * (n // tn):
        return _mm(name, a, b, _TN, (m // tm, n // tn),
                   pl.BlockSpec((t, tm), lambda i, j: (0, i)), pl.BlockSpec((t, tn), lambda i, j: (0, j)),
                   pl.BlockSpec((tm, tn), lambda i, j: (i, j)), (m, n), out_dtype, None, alpha=alpha)
    return _mm(name, a, b, _TN, (n // tn, m // tm),
               pl.BlockSpec((t, tm), lambda j, i: (0, i)), pl.BlockSpec((t, tn), lambda j, i: (0, j)),
               pl.BlockSpec((tm, tn), lambda j, i: (i, j)), (m, n), out_dtype, None, alpha=alpha)


def _ffn_in_fwd(name, n, wblk, dep=None):
    t = n.shape[0]
    tm = _pick(t, (1088, 768, 512, 256, 128))
    has_dep = dep is not None

    def body(n_ref, w_ref, *rest):
        gu_ref, a_ref = rest[-2], rest[-1]
        nv = n_ref[...]
        g = lax.dot_general(nv, w_ref[0], _NN, preferred_element_type=F32)
        u = lax.dot_general(nv, w_ref[1], _NN, preferred_element_type=F32)
        sg = jax.nn.sigmoid(g)
        silu = g * sg
        a_ref[...] = (silu * u).astype(BF16)
        gu_ref[0] = ((0.5 * u) * (sg + silu * (1.0 - sg))).astype(BF16)
        gu_ref[1] = (0.5 * silu).astype(BF16)

    return pl.pallas_call(
        body, name=name, grid=(t // tm, 4),
        in_specs=[pl.BlockSpec((tm, D_MODEL), lambda i, j: (i, 0)),
                  pl.BlockSpec((2, None, D_MODEL, FF_SHARD_P), lambda i, j: (0, j, 0, 0))]
        + ([pl.BlockSpec(memory_space=pl.ANY)] if has_dep else []),
        out_specs=[pl.BlockSpec((2, tm, FF_SHARD_P), lambda i, j: (0, i, j)),
                   pl.BlockSpec((tm, FF_SHARD_P), lambda i, j: (i, j))],
        out_shape=[jax.ShapeDtypeStruct((2, t, D_FF_P), BF16), jax.ShapeDtypeStruct((t, D_FF_P), BF16)],
        compiler_params=_params(("parallel", "parallel")),
    )(*((n, wblk) + ((dep,) if has_dep else ())))


def _ffn_out_bwd_x(name, dh, w_out, gu, dep=None):
    t = dh.shape[0]
    tm = _pick(t, (1088, 768, 512, 256, 128))
    has_dep = dep is not None

    def body(dh_ref, w_ref, gu_ref, *rest):
        o_ref = rest[-1]
        da = lax.dot_general(_bf(dh_ref[...]), w_ref[...], _NT, preferred_element_type=F32)
        o_ref[0] = (da * gu_ref[0].astype(F32)).astype(BF16)
        o_ref[1] = (da * gu_ref[1].astype(F32)).astype(BF16)

    gu_spec = pl.BlockSpec((2, tm, FF_SHARD_P), lambda i, j: (0, i, j))
    return pl.pallas_call(
        body, name=name, grid=(t // tm, 4),
        in_specs=[pl.BlockSpec((tm, D_MODEL), lambda i, j: (i, 0)), pl.BlockSpec((FF_SHARD_P, D_MODEL), lambda i, j: (j, 0)),
                  gu_spec] + ([pl.BlockSpec(memory_space=pl.ANY)] if has_dep else []),
        out_specs=gu_spec, out_shape=jax.ShapeDtypeStruct((2, t, D_FF_P), BF16),
        compiler_params=_params(("parallel", "parallel")),
    )(*((dh, w_out, gu) + ((dep,) if has_dep else ())))


def _rms_bwd_rows(dn, h, g, dres):
    r = lax.rsqrt(jnp.mean(h * h, axis=-1, keepdims=True) + EPS)
    tv = dn * g
    dot = jnp.mean(tv * h, axis=-1, keepdims=True)
    return dres + (r * tv - h * (r * r * r * dot)), jnp.sum(dn * (h * r), axis=0, keepdims=True)


def _accumulate_rows(ref, part, first):
    @pl.when(first)
    def _():
        ref[...] = part

    @pl.when(jnp.logical_not(first))
    def _():
        ref[...] += part


def _ffn_in_bwd_x(name, dgu, wblk, h_in, g_norm, dres, dep=None):
    t = dgu.shape[1]
    tm = _pick(t, (544, 384, 256, 128))
    has_dep = dep is not None

    def body(d_ref, w_ref, h_ref, g_ref, r_ref, *rest):
        dh_ref, dg_ref = rest[-2], rest[-1]
        acc = None
        for s in range(2):
            for j in range(4):
                part = lax.dot_general(d_ref[s, :, FF_SHARD_P * j:FF_SHARD_P * (j + 1)], w_ref[s, j], _NT,
                                       preferred_element_type=F32)
                acc = part if acc is None else acc + part
        dh, dg = _rms_bwd_rows(acc, h_ref[...], g_ref[...], r_ref[...])
        dh_ref[...] = dh
        _accumulate_rows(dg_ref, dg, pl.program_id(0) == 0)

    row = pl.BlockSpec((tm, D_MODEL), lambda i: (i, 0))
    vec = pl.BlockSpec((1, D_MODEL), lambda i: (0, 0))
    return pl.pallas_call(
        body, name=name, grid=(t // tm,),
        in_specs=[pl.BlockSpec((2, tm, D_FF_P), lambda i: (0, i, 0)),
                  pl.BlockSpec((2, 4, D_MODEL, FF_SHARD_P), lambda i: (0, 0, 0, 0), pipeline_mode=pl.Buffered(1)),
                  row, vec, row]
        + ([pl.BlockSpec(memory_space=pl.ANY)] if has_dep else []),
        out_specs=[row, vec],
        out_shape=[jax.ShapeDtypeStruct((t, D_MODEL), F32), jax.ShapeDtypeStruct((1, D_MODEL), F32)],
        compiler_params=_params(("arbitrary",)),
    )(*((dgu, wblk, h_in, g_norm, dres) + ((dep,) if has_dep else ())))


def _proj_fwd(name, um, wi):
    t = um.shape[0]
    tm = _pick(t, (544, 384, 256, 128))

    def body(u_ref, w_ref, q_ref, g_ref, f_ref):
        uv = u_ref[...]
        q_ref[...] = lax.dot_general(uv, w_ref[:, 0:QKV_W], _NN, preferred_element_type=F32).astype(BF16)
        g_ref[...] = lax.dot_general(uv, w_ref[:, P_GATES:P_F], _NN, preferred_element_type=F32)
        f_ref[...] = lax.dot_general(uv, w_ref[:, P_F:PROJ_P], _NN, preferred_element_type=F32)

    return pl.pallas_call(
        body, name=name, grid=(t // tm,),
        in_specs=[pl.BlockSpec((tm, D_MODEL), lambda i: (i, 0)),
                  pl.BlockSpec((D_MODEL, PROJ_P), lambda i: (0, 0), pipeline_mode=pl.Buffered(1))],
        out_specs=[pl.BlockSpec((tm, QKV_W), lambda i: (i, 0)), pl.BlockSpec((tm, 2 * D_MODEL), lambda i: (i, 0)),
                   pl.BlockSpec((tm, 128), lambda i: (i, 0))],
        out_shape=[jax.ShapeDtypeStruct((t, QKV_W), BF16), jax.ShapeDtypeStruct((t, 2 * D_MODEL), F32),
                   jax.ShapeDtypeStruct((t, 128), F32)],
        compiler_params=_params(("parallel",)),
    )(um, wi)


def _proj_bwd_x(name, dqkv, dgates, df, wi, h_in, g_norm, dres, dep=None):
    t = dqkv.shape[0]
    tm = _pick(t, (544, 384, 256, 128))
    has_dep = dep is not None

    def body(q_ref, gt_ref, f_ref, w_ref, h_ref, g_ref, r_ref, *rest):
        dh_ref, dg_ref = rest[-2], rest[-1]
        acc = lax.dot_general(q_ref[...], w_ref[:, 0:QKV_W], _NT, preferred_element_type=F32)
        acc = acc + lax.dot_general(gt_ref[...], w_ref[:, P_GATES:P_F], _NT, preferred_element_type=F32)
        acc = acc + lax.dot_general(f_ref[...], w_ref[:, P_F:PROJ_P], _NT, preferred_element_type=F32)
        dh, dg = _rms_bwd_rows(acc, h_ref[...], g_ref[...], r_ref[...])
        dh_ref[...] = dh
        _accumulate_rows(dg_ref, dg, pl.program_id(0) == 0)

    row = pl.BlockSpec((tm, D_MODEL), lambda i: (i, 0))
    vec = pl.BlockSpec((1, D_MODEL), lambda i: (0, 0))
    return pl.pallas_call(
        body, name=name, grid=(t // tm,),
        in_specs=[pl.BlockSpec((tm, QKV_W), lambda i: (i, 0)), pl.BlockSpec((tm, 2 * D_MODEL), lambda i: (i, 0)),
                  pl.BlockSpec((tm, 128), lambda i: (i, 0)),
                  pl.BlockSpec((D_MODEL, PROJ_P), lambda i: (0, 0), pipeline_mode=pl.Buffered(1)), row, vec, row]
        + ([pl.BlockSpec(memory_space=pl.ANY)] if has_dep else []),
        out_specs=[row, vec],
        out_shape=[jax.ShapeDtypeStruct((t, D_MODEL), F32), jax.ShapeDtypeStruct((1, D_MODEL), F32)],
        compiler_params=_params(("arbitrary",)),
    )(*((dqkv, dgates, df, wi, h_in, g_norm, dres) + ((dep,) if has_dep else ())))


def _ffn_in_bwd_w(name, n, dgu):
    t = n.shape[0]
    tk = t
    nk = 1
    return _mm(name, n, dgu, _TN, (2, 4, nk),
               pl.BlockSpec((tk, D_MODEL), lambda s, j, kk: (kk, 0)),
               pl.BlockSpec((None, tk, FF_SHARD_P), lambda s, j, kk: (s, kk, j)),
               pl.BlockSpec((None, None, D_MODEL, FF_SHARD_P), lambda s, j, kk: (s, j, 0, 0)),
               (2, 4, D_MODEL, FF_SHARD_P), BF16, (D_MODEL, FF_SHARD_P), k_axis=2, nk=nk)


def _rms_fwd(name, h, g):
    t = h.shape[0]
    tm = _pick(t, (544, 384, 256, 128))

    def body(h_ref, g_ref, o_ref):
        hv = h_ref[...]
        r = lax.rsqrt(jnp.mean(hv * hv, axis=-1, keepdims=True) + EPS)
        o_ref[...] = ((hv * r) * g_ref[...]).astype(BF16)

    return pl.pallas_call(
        body, name=name, grid=(t // tm,),
        in_specs=[pl.BlockSpec((tm, D_MODEL), lambda i: (i, 0)), pl.BlockSpec((1, D_MODEL), lambda i: (0, 0))],
        out_specs=pl.BlockSpec((tm, D_MODEL), lambda i: (i, 0)),
        out_shape=jax.ShapeDtypeStruct((t, D_MODEL), BF16), compiler_params=_params(("parallel",)),
    )(h, g)


def _rms_bwd(name, h, g, dn, dres):
    t = h.shape[0]
    tm = _pick(t, (544, 384, 256, 128))

    def body(h_ref, g_ref, dn_ref, dres_ref, dh_ref, dg_ref):
        i = pl.program_id(0)
        hv = h_ref[...]
        dnv = dn_ref[...]
        r = lax.rsqrt(jnp.mean(hv * hv, axis=-1, keepdims=True) + EPS)
        tv = dnv * g_ref[...]
        dot = jnp.mean(tv * hv, axis=-1, keepdims=True)
        dh_ref[...] = dres_ref[...] + (r * tv - hv * (r * r * r * dot))
        part = jnp.sum(dnv * (hv * r), axis=0, keepdims=True)

        @pl.when(i == 0)
        def _():
            dg_ref[...] = part

        @pl.when(i > 0)
        def _():
            dg_ref[...] += part

    row = pl.BlockSpec((tm, D_MODEL), lambda i: (i, 0))
    vec = pl.BlockSpec((1, D_MODEL), lambda i: (0, 0))
    return pl.pallas_call(
        body, name=name, grid=(t // tm,), in_specs=[row, vec, row, row], out_specs=[row, vec],
        out_shape=[jax.ShapeDtypeStruct((t, D_MODEL), F32), jax.ShapeDtypeStruct((1, D_MODEL), F32)],
        compiler_params=_params(("arbitrary",)),
    )(h, g, dn, dres)


def _branch_gate_fwd(name, oa, ob, wa, wb, gates, dep=None):
    t = gates.shape[0]
    tm = _pick(t, (544, 384, 256, 128))
    has_dep = dep is not None

    def body(oa_ref, ob_ref, wa_ref, wb_ref, g_ref, *rest):
        o_ref, ya_ref, yb_ref = rest[-3:]
        ya = lax.dot_general(_bf(oa_ref[...]), wa_ref[...], _NN, preferred_element_type=F32)
        yb = lax.dot_general(_bf(ob_ref[...]), wb_ref[...], _NN, preferred_element_type=F32)
        sa = jax.nn.sigmoid(g_ref[:, 0:D_MODEL])
        sb = jax.nn.sigmoid(g_ref[:, D_MODEL:2 * D_MODEL])
        o_ref[...] = (sa * ya + sb * yb).astype(BF16)
        ya_ref[...] = ya.astype(BF16)
        yb_ref[...] = yb.astype(BF16)

    blk = pl.BlockSpec((tm, D_MODEL), lambda i: (i, 0))
    narrow = pl.BlockSpec((tm, A_WIDTH), lambda i: (i, 0))
    wide = pl.BlockSpec((tm, 2 * D_MODEL), lambda i: (i, 0))
    wspec = pl.BlockSpec((A_WIDTH, D_MODEL), lambda i: (0, 0))
    out = jax.ShapeDtypeStruct((t, D_MODEL), BF16)
    return pl.pallas_call(
        body, name=name, grid=(t // tm,),
        in_specs=[narrow, narrow, wspec, wspec, wide] + ([pl.BlockSpec(memory_space=pl.ANY)] if has_dep else []),
        out_specs=[blk, blk, blk], out_shape=[out, out, out], compiler_params=_params(("parallel",)),
    )(*((oa, ob, wa, wb, gates) + ((dep,) if has_dep else ())))


def _mix_out_gate_bwd(name, dh, wo, gates, ya, yb):
    t = gates.shape[0]
    tm = _pick(t, (544, 384, 256, 128))

    def body(dh_ref, w_ref, g_ref, ya_ref, yb_ref, dya_ref, dyb_ref, dg_ref):
        dm = lax.dot_general(_bf(dh_ref[...]), w_ref[...], _NT, preferred_element_type=F32)
        sa = jax.nn.sigmoid(g_ref[:, 0:D_MODEL])
        sb = jax.nn.sigmoid(g_ref[:, D_MODEL:2 * D_MODEL])
        dya_ref[...] = (dm * sa).astype(BF16)
        dyb_ref[...] = (dm * sb).astype(BF16)
        dg_ref[:, 0:D_MODEL] = (dm * ya_ref[...].astype(F32) * (sa * (1.0 - sa))).astype(BF16)
        dg_ref[:, D_MODEL:2 * D_MODEL] = (dm * yb_ref[...].astype(F32) * (sb * (1.0 - sb))).astype(BF16)

    blk = pl.BlockSpec((tm, D_MODEL), lambda i: (i, 0))
    wide = pl.BlockSpec((tm, 2 * D_MODEL), lambda i: (i, 0))
    out = jax.ShapeDtypeStruct((t, D_MODEL), BF16)
    return pl.pallas_call(
        body, name=name, grid=(t // tm,),
        in_specs=[blk, pl.BlockSpec((D_MODEL, D_MODEL), lambda i: (0, 0)), wide, blk, blk], out_specs=[blk, blk, wide],
        out_shape=[out, out, jax.ShapeDtypeStruct((t, 2 * D_MODEL), BF16)], compiler_params=_params(("parallel",)),
    )(dh, wo, gates, ya, yb)


def _loss_head(name, h3, gf, tgt):
    b, l, _ = h3.shape
    nb = l // BLOCK

    def body(h_ref, g_ref, t_ref, dh_ref, loss_ref, dg_ref):
        first = (pl.program_id(0) == 0) & (pl.program_id(1) == 0)
        real = (pl.program_id(1) > 0).astype(F32)
        hv = h_ref[...]
        g = g_ref[...]
        r = lax.rsqrt(jnp.mean(hv * hv, axis=-1, keepdims=True) + EPS)
        xn = hv * r
        err = (xn * g - t_ref[...]) * real
        lpart = 0.5 * jnp.sum(jnp.mean(err * err, axis=-1, keepdims=True), axis=0, keepdims=True)
        dy = err * (1.0 / D_MODEL)
        tv = dy * g
        dot = jnp.mean(tv * hv, axis=-1, keepdims=True)
        dh_ref[...] = r * tv - hv * (r * r * r * dot)
        gpart = jnp.sum(dy * xn, axis=0, keepdims=True)

        @pl.when(first)
        def _():
            loss_ref[...] = jnp.zeros_like(loss_ref)
            dg_ref[...] = jnp.zeros_like(dg_ref)

        loss_ref[...] += jnp.broadcast_to(lpart, loss_ref.shape)
        dg_ref[...] += gpart

    return pl.pallas_call(
        body, name=name, grid=(b, nb),
        in_specs=[pl.BlockSpec((None, BLOCK, D_MODEL), lambda bi, n: (bi, n, 0)),
                  pl.BlockSpec((1, D_MODEL), lambda bi, n: (0, 0)),
                  pl.BlockSpec((None, BLOCK, D_MODEL), lambda bi, n: (bi, jnp.maximum(n - 1, 0), 0))],
        out_specs=[pl.BlockSpec((None, BLOCK, D_MODEL), lambda bi, n: (bi, n, 0)),
                   pl.BlockSpec((8, 128), lambda bi, n: (0, 0)),
                   pl.BlockSpec((1, D_MODEL), lambda bi, n: (0, 0))],
        out_shape=[jax.ShapeDtypeStruct(h3.shape, F32), jax.ShapeDtypeStruct((8, 128), F32),
                   jax.ShapeDtypeStruct((1, D_MODEL), F32)],
        compiler_params=_params(("arbitrary", "arbitrary")),
    )(h3, gf, tgt)


def _fgate_fwd(name, f3, bf_row):
    b, l, _ = f3.shape
    nb = l // BLOCK

    def body(f_ref, b_ref, cc_ref, cr_ref):
        r_i = lax.broadcasted_iota(jnp.int32, (BLOCK, BLOCK), 0)
        c_i = lax.broadcasted_iota(jnp.int32, (BLOCK, BLOCK), 1)
        tri = (r_i >= c_i).astype(F32)
        carry = jnp.zeros((1, 128), F32)
        for blk in range(nb):
            rows = slice(blk * BLOCK, (blk + 1) * BLOCK)
            z = f_ref[rows, :] + b_ref[...]
            lf = jnp.minimum(z, 0.0) - jnp.log(1.0 + jnp.exp(-jnp.abs(z)))
            cb = jnp.dot(tri, lf, preferred_element_type=F32, precision=lax.Precision.HIGHEST) + carry
            carry = cb[BLOCK - 1:BLOCK, :]
            cbt = cb.T
            for hh in range(B_HEADS):
                cc_ref[hh, rows, :] = jnp.sum(jnp.where(c_i == hh, cb, 0.0), axis=1, keepdims=True)
                cr_ref[hh, :, rows] = cbt[hh:hh + 1, :]

    return pl.pallas_call(
        body, name=name, grid=(b,),
        in_specs=[pl.BlockSpec((None, l, 128), lambda bi: (bi, 0, 0)),
                  pl.BlockSpec((1, 128), lambda bi: (0, 0))],
        out_specs=[pl.BlockSpec((None, B_HEADS, l, 1), lambda bi: (bi, 0, 0, 0)),
                   pl.BlockSpec((None, B_HEADS, 1, l), lambda bi: (bi, 0, 0, 0))],
        out_shape=[jax.ShapeDtypeStruct((b, B_HEADS, l, 1), F32), jax.ShapeDtypeStruct((b, B_HEADS, 1, l), F32)],
        compiler_params=_params(("parallel",)),
    )(f3, bf_row)


def _fgate_bwd(name, f3, bf_row, dcq, dck):
    b, l, _ = f3.shape
    nb = l // BLOCK

    def body(f_ref, b_ref, dcq_ref, dck_ref, df_ref, db_ref):
        r_i = lax.broadcasted_iota(jnp.int32, (BLOCK, BLOCK), 0)
        c_i = lax.broadcasted_iota(jnp.int32, (BLOCK, BLOCK), 1)
        tri = (r_i <= c_i).astype(F32)
        carry = jnp.zeros((1, 128), F32)
        total = jnp.zeros((1, 128), F32)
        for blk in range(nb - 1, -1, -1):
            rows = slice(blk * BLOCK, (blk + 1) * BLOCK)
            krows = jnp.concatenate([dck_ref[hh, :, rows] for hh in range(B_HEADS)]
                                    + [jnp.zeros((BLOCK - B_HEADS, BLOCK), F32)], axis=0)
            dcb = krows.T
            for hh in range(B_HEADS):
                dcb = dcb + jnp.where(c_i == hh, dcq_ref[hh, rows, :], 0.0)
            rc = jnp.dot(tri, dcb, preferred_element_type=F32, precision=lax.Precision.HIGHEST) + carry
            carry = rc[0:1, :]
            z = f_ref[rows, :] + b_ref[...]
            df = rc * (1.0 / (1.0 + jnp.exp(z)))
            df_ref[rows, :] = df.astype(BF16)
            total = total + jnp.sum(df, axis=0, keepdims=True)

        @pl.when(pl.program_id(0) == 0)
        def _():
            db_ref[...] = total

        @pl.when(pl.program_id(0) > 0)
        def _():
            db_ref[...] += total

    return pl.pallas_call(
        body, name=name, grid=(b,),
        in_specs=[pl.BlockSpec((None, l, 128), lambda bi: (bi, 0, 0)),
                  pl.BlockSpec((1, 128), lambda bi: (0, 0)),
                  pl.BlockSpec((None, B_HEADS, l, 1), lambda bi: (bi, 0, 0, 0)),
                  pl.BlockSpec((None, B_HEADS, 1, l), lambda bi: (bi, 0, 0, 0))],
        out_specs=[pl.BlockSpec((None, l, 128), lambda bi: (bi, 0, 0)), pl.BlockSpec((1, 128), lambda bi: (0, 0))],
        out_shape=[jax.ShapeDtypeStruct((b, l, 128), BF16), jax.ShapeDtypeStruct((1, 128), F32)],
        compiler_params=_params(("arbitrary",)),
    )(f3, bf_row, dcq, dck)


A_Q_BLK = B_SEG // A_WIDTH
A_K_BLK = (B_SEG + A_WIDTH) // 128
A_V_BLK = A_K_BLK + 1
A_SEG_BLK = B_SEG // A_SEG
STACK = A_HEADS * BLOCK


def _lane_lo():
    return lax.broadcasted_iota(jnp.int32, (1, 128), 1) < HEAD_DIM


def _stack_heads(x, masked):
    lo = _lane_lo()
    blks = [x[:, 128 * j:128 * (j + 1)] for j in range(4)]
    if not masked:
        return jnp.concatenate(blks + blks, axis=0)
    zero = jnp.zeros_like(blks[0])
    return jnp.concatenate([jnp.where(lo, bk, zero) for bk in blks] + [jnp.where(lo, zero, bk) for bk in blks], axis=0)


def _unstack_heads(y):
    lo = _lane_lo()
    return jnp.concatenate([jnp.where(lo, y[128 * j:128 * (j + 1)], y[128 * (4 + j):128 * (5 + j)]) for j in range(4)], axis=1)


def _swa_bias(slopes):
    r_i = jnp.arange(STACK)[:, None]
    c_i = jnp.arange(3 * BLOCK)[None, :]
    seg = c_i >> 7
    out = []
    for n in range(3):
        qpos = n * BLOCK + (r_i & (BLOCK - 1))
        kpos = jnp.where(seg == 0, c_i, (n - 2) * BLOCK + c_i)
        dist = qpos - kpos
        band = (seg != 0) & (dist < BLOCK) & (kpos >= PREFIX)
        meta = (seg == 0) & (c_i >= N_PAD)
        out.append(jnp.where((dist >= 0) & (band | meta), -slopes * dist.astype(F32), NEG))
    return jnp.stack(out, axis=0)


def _swa_scores(q, kcat, n, slope, bias):
    s = lax.dot_general(q, kcat, _NT, preferred_element_type=F32) + bias
    further = slope * (-BLOCK * jnp.maximum(n - 2, 0)).astype(F32)
    return jnp.concatenate([s[:, 0:BLOCK] + further, s[:, BLOCK:]], axis=1)


def _swa_specs():
    def kv(col_blk):
        return [pl.BlockSpec((None, BLOCK, 128), lambda b, n: (b, 0, col_blk)),
                pl.BlockSpec((None, BLOCK, 128), lambda b, n: (b, jnp.maximum(n - 1, 0), col_blk)),
                pl.BlockSpec((None, BLOCK, 128), lambda b, n: (b, n, col_blk))]

    q_spec = pl.BlockSpec((None, BLOCK, A_WIDTH), lambda b, n: (b, n, A_Q_BLK))
    o_spec = pl.BlockSpec((None, BLOCK, A_WIDTH), lambda b, n: (b, n, 0))
    col = pl.BlockSpec((STACK, 1), lambda b, n: (0, 0))
    bias = pl.BlockSpec((None, STACK, 3 * BLOCK), lambda b, n: (jnp.minimum(n, 2), 0, 0))
    lse_spec = pl.BlockSpec((None, A_HEADS, BLOCK, 1), lambda b, n: (b, 0, n, 0))
    return q_spec, kv(A_K_BLK), kv(A_V_BLK), o_spec, [col, col, bias], lse_spec


def _swa_fwd(name, qkv, slopes, sinks, bias):
    b, l, _ = qkv.shape
    nb = l // BLOCK

    def body(q_ref, k0_ref, kp_ref, kc_ref, v0_ref, vp_ref, vc_ref, sl_ref, sk_ref, bias_ref, o_ref, lse_ref):
        n = pl.program_id(1)
        qs = _stack_heads(q_ref[...], True) * SCALE
        kcat = jnp.concatenate([k0_ref[...], kp_ref[...], kc_ref[...]], axis=0)
        vcat = jnp.concatenate([v0_ref[...], vp_ref[...], vc_ref[...]], axis=0)
        s = _swa_scores(qs, kcat, n, sl_ref[...], bias_ref[...])
        sink = sk_ref[...]
        m = jnp.maximum(jnp.max(s, axis=-1, keepdims=True), sink)
        p = jnp.exp(s - m)
        den = jnp.sum(p, axis=-1, keepdims=True) + jnp.exp(sink - m)
        o = lax.dot_general(p.astype(BF16), vcat, _NN, preferred_element_type=F32) / den
        o_ref[...] = _unstack_heads(o)
        lse_ref[...] = (m + jnp.log(den)).reshape(A_HEADS, BLOCK, 1)

    q_spec, k_specs, v_specs, o_spec, consts, lse_spec = _swa_specs()
    return pl.pallas_call(
        body, name=name, grid=(b, nb),
        in_specs=[q_spec] + k_specs + v_specs + consts, out_specs=[o_spec, lse_spec],
        out_shape=[jax.ShapeDtypeStruct((b, l, A_WIDTH), F32), jax.ShapeDtypeStruct((b, A_HEADS, l, 1), F32)],
        compiler_params=_params(("parallel", "parallel")),
    )(qkv, qkv, qkv, qkv, qkv, qkv, qkv, slopes, sinks, bias)


def _swa_bwd(name, qkv, o, lse, do, slopes, sinks, bias, dqkv):
    b, l, _ = qkv.shape
    nb = l // BLOCK

    def body(q_ref, k0_ref, kp_ref, kc_ref, v0_ref, vp_ref, vc_ref, o_ref, lse_ref, do_ref, sl_ref, sk_ref, bias_ref, _,
             dx_ref, ds_ref, dk_acc, dv_acc):
        bi = pl.program_id(0)
        n = pl.program_id(1)
        qs = _stack_heads(q_ref[...], True) * SCALE
        dos32 = _stack_heads(do_ref[...], True)
        dos = dos32.astype(BF16)
        os_ = _stack_heads(o_ref[...], False)
        lsev = lse_ref[...].reshape(STACK, 1)
        kcat = jnp.concatenate([k0_ref[...], kp_ref[...], kc_ref[...]], axis=0)
        vcat = jnp.concatenate([v0_ref[...], vp_ref[...], vc_ref[...]], axis=0)
        s = _swa_scores(qs, kcat, n, sl_ref[...], bias_ref[...])
        p = jnp.exp(s - lsev)
        dsum = jnp.sum(dos32 * os_, axis=-1, keepdims=True)
        dp = lax.dot_general(dos, vcat, _NT, preferred_element_type=F32)
        dsc = (p * (dp - dsum)).astype(BF16)
        dq = lax.dot_general(dsc, kcat, _NN, preferred_element_type=F32) * SCALE
        row0 = pl.multiple_of(n * BLOCK, BLOCK)
        dx_ref[pl.ds(row0, BLOCK), 0:A_WIDTH] = _unstack_heads(dq).astype(BF16)
        dkc = lax.dot_general(dsc, qs, _TN, preferred_element_type=F32)
        dvc = lax.dot_general(p.astype(BF16), dos, _TN, preferred_element_type=F32)

        @pl.when(n == 0)
        def _():
            dk_acc[...] = jnp.zeros_like(dk_acc)
            dv_acc[...] = jnp.zeros_like(dv_acc)

        starts = (0, pl.multiple_of(jnp.maximum(n - 1, 0) * BLOCK, BLOCK), row0)
        for t, st in enumerate(starts):
            dk_acc[pl.ds(st, BLOCK), :] += dkc[t * BLOCK:(t + 1) * BLOCK, :]
            dv_acc[pl.ds(st, BLOCK), :] += dvc[t * BLOCK:(t + 1) * BLOCK, :]

        @pl.when(n == nb - 1)
        def _():
            dx_ref[:, A_WIDTH:A_WIDTH + 128] = dk_acc[...].astype(BF16)
            dx_ref[:, A_WIDTH + 128:A_SEG] = dv_acc[...].astype(BF16)

        dsink = -(jnp.exp(sk_ref[...] - lsev) * dsum)
        r8 = lax.broadcasted_iota(jnp.int32, (8, 128), 0)
        acc = jnp.zeros((8, 128), F32)
        for hh in range(A_HEADS):
            acc = acc + jnp.where(r8 == hh, jnp.sum(dsink[hh * BLOCK:(hh + 1) * BLOCK, :]), 0.0)

        @pl.when((bi == 0) & (n == 0))
        def _():
            ds_ref[...] = jnp.zeros_like(ds_ref)

        ds_ref[...] += acc

    q_spec, k_specs, v_specs, o_spec, consts, lse_spec = _swa_specs()
    return pl.pallas_call(
        body, name=name, grid=(b, nb),
        in_specs=[q_spec] + k_specs + v_specs + [o_spec, lse_spec, o_spec] + consts + [pl.BlockSpec(memory_space=pl.ANY)],
        out_specs=[pl.BlockSpec((None, l, A_SEG), lambda bb, n: (bb, 0, A_SEG_BLK)),
                   pl.BlockSpec((8, 128), lambda bb, n: (0, 0))],
        out_shape=[jax.ShapeDtypeStruct(dqkv.shape, BF16), jax.ShapeDtypeStruct((8, 128), F32)],
        scratch_shapes=[pltpu.VMEM((l, 128), F32), pltpu.VMEM((l, 128), F32)],
        input_output_aliases={13: 0},
        compiler_params=_params(("arbitrary", "arbitrary")),
    )(qkv, qkv, qkv, qkv, qkv, qkv, qkv, o, lse, do, slopes, sinks, bias, dqkv)


def _fox_mask(qk, ck, i):
    kh = qk.shape[1]
    qpos = i * BLOCK + lax.broadcasted_iota(jnp.int32, (BLOCK, kh), 0)
    kpos = lax.broadcasted_iota(jnp.int32, (BLOCK, kh), 1)
    return jnp.where((kpos <= qpos) & (kpos >= N_PAD), qk - ck, NEG)


def _pick_head(x, hh):
    lo = _lane_lo()
    return jnp.where(lo if hh == 0 else jnp.logical_not(lo), x, jnp.zeros_like(x))


def _both_heads(x):
    return jnp.concatenate([_pick_head(x, 0), _pick_head(x, 1)], axis=0)


def _fox_specs(l):
    pair = pl.BlockSpec((None, l, PAIR_W), lambda bi, hp: (bi, 0, hp))
    half = pl.BlockSpec((None, l, 128), lambda bi, hp: (bi, 0, hp))
    colv = pl.BlockSpec((None, 2, l, 1), lambda bi, hp: (bi, hp, 0, 0))
    rowv = pl.BlockSpec((None, 2, 1, l), lambda bi, hp: (bi, hp, 0, 0))
    return pair, half, colv, rowv


def _fox_fwd(name, qkv, c_col, c_row):
    b, l, _ = qkv.shape
    nb = l // BLOCK

    def body(x_ref, cc_ref, cr_ref, o_ref, lse_ref):
        for i in range(nb):
            rows = slice(i * BLOCK, (i + 1) * BLOCK)
            kh = (i + 1) * BLOCK
            qblk = x_ref[rows, 0:128]
            kv = x_ref[0:kh, 128:256]
            vv = x_ref[0:kh, 256:384]
            qk = lax.dot_general(_both_heads(qblk) * SCALE, kv, _NT, preferred_element_type=F32)
            ps, dens = [], []
            for hh in range(2):
                s = _fox_mask(qk[hh * BLOCK:(hh + 1) * BLOCK], cr_ref[hh, :, 0:kh], i)
                m = jnp.max(s, axis=-1, keepdims=True)
                p = jnp.exp(s - m)
                den = jnp.sum(p, axis=-1, keepdims=True)
                ps.append(p.astype(BF16))
                dens.append(den)
                lse_ref[hh, rows, :] = (m + jnp.log(den)) + cc_ref[hh, rows, :]
            pv = lax.dot_general(jnp.concatenate(ps, axis=0), vv, _NN, preferred_element_type=F32)
            o_ref[rows, :] = jnp.where(_lane_lo(), pv[0:BLOCK] / dens[0], pv[BLOCK:2 * BLOCK] / dens[1]).astype(BF16)

    pair, half, colv, rowv = _fox_specs(l)
    return pl.pallas_call(
        body, name=name, grid=(b, 4), in_specs=[pair, colv, rowv], out_specs=[half, colv],
        out_shape=[jax.ShapeDtypeStruct((b, l, B_WIDTH), BF16), jax.ShapeDtypeStruct((b, B_HEADS, l, 1), F32)],
        compiler_params=_params(("parallel", "parallel")),
    )(qkv, c_col, c_row)


def _fox_bwd(name, qkv, c_col, c_row, o, lse, do):
    b, l, _ = qkv.shape
    nb = l // BLOCK

    def body(x_ref, cc_ref, cr_ref, o_ref, lse_ref, do_ref, dx_ref, dcq_ref, dck_ref, dk_acc, dv_acc):
        dk_acc[...] = jnp.zeros_like(dk_acc)
        dv_acc[...] = jnp.zeros_like(dv_acc)
        dck_ref[...] = jnp.zeros_like(dck_ref)
        for i in range(nb):
            rows = slice(i * BLOCK, (i + 1) * BLOCK)
            kh = (i + 1) * BLOCK
            qblk = x_ref[rows, 0:128]
            kv = x_ref[0:kh, 128:256]
            vv = x_ref[0:kh, 256:384]
            doblk = do_ref[rows, :]
            ov = o_ref[rows, :].astype(F32)
            q2 = _both_heads(qblk) * SCALE
            do2 = _both_heads(doblk)
            qk = lax.dot_general(q2, kv, _NT, preferred_element_type=F32)
            dp = lax.dot_general(do2, vv, _NT, preferred_element_type=F32)
            ps, dss = [], []
            for hh in range(2):
                half = slice(hh * BLOCK, (hh + 1) * BLOCK)
                s = _fox_mask(qk[half], cr_ref[hh, :, 0:kh], i)
                p = jnp.exp(s - (lse_ref[hh, rows, :] - cc_ref[hh, rows, :]))
                dsum = jnp.sum(do2[half].astype(F32) * ov, axis=-1, keepdims=True)
                ds = p * (dp[half] - dsum)
                ps.append(p.astype(BF16))
                dss.append(ds.astype(BF16))
                dcq_ref[hh, rows, :] = jnp.sum(ds, axis=-1, keepdims=True)
                dck_ref[hh, :, 0:kh] -= jnp.sum(ds, axis=0, keepdims=True)
            p2 = jnp.concatenate(ps, axis=0)
            ds2 = jnp.concatenate(dss, axis=0)
            dq = lax.dot_general(ds2, kv, _NN, preferred_element_type=F32) * SCALE
            dk_acc[0:kh, :] += lax.dot_general(ds2, q2, _TN, preferred_element_type=F32)
            dv_acc[0:kh, :] += lax.dot_general(p2, do2, _TN, preferred_element_type=F32)
            dx_ref[rows, 0:128] = jnp.where(_lane_lo(), dq[0:BLOCK], dq[BLOCK:2 * BLOCK]).astype(BF16)
        dx_ref[:, 128:256] = dk_acc[...].astype(BF16)
        dx_ref[:, 256:384] = dv_acc[...].astype(BF16)

    pair, half, colv, rowv = _fox_specs(l)
    return pl.pallas_call(
        body, name=name, grid=(b, 4), in_specs=[pair, colv, rowv, half, colv, half],
        out_specs=[pair, colv, rowv],
        out_shape=[jax.ShapeDtypeStruct(qkv.shape, BF16), jax.ShapeDtypeStruct((b, B_HEADS, l, 1), F32),
                   jax.ShapeDtypeStruct((b, B_HEADS, 1, l), F32)],
        scratch_shapes=[pltpu.VMEM((l, 128), F32), pltpu.VMEM((l, 128), F32)],
        compiler_params=_params(("parallel", "parallel")),
    )(qkv, c_col, c_row, o, lse, do)


_FLIPS = ((0, 0, 1), (0, 1, 0), (0, 1, 1), (1, 0, 0), (1, 0, 1), (1, 1, 0), (1, 1, 1))


def _exchange(name, gather, scatter):
    ng, ns = len(gather), len(scatter)
    na = ng + ns
    npeer = len(_FLIPS)

    def body(*refs):
        ins = refs[:na]
        outs = refs[na:2 * na]
        send_sems, recv_sems, loc_sems = refs[2 * na:]
        x, y, c = lax.axis_index("x"), lax.axis_index("y"), lax.axis_index("c")
        me = 4 * x + 2 * y + c
        peers = []
        for fx, fy, fc in _FLIPS:
            px = 1 - x if fx else x
            py = 1 - y if fy else y
            pc = 1 - c if fc else c
            peers.append(((px, py, pc), 4 * px + 2 * py + pc))

        def remote(a, kk):
            dev, lin = peers[kk]
            src = ins[a] if a < ng else ins[a].at[lin]
            return pltpu.make_async_remote_copy(src_ref=src, dst_ref=outs[a].at[me], send_sem=send_sems.at[a * npeer + kk],
                                                recv_sem=recv_sems.at[a * npeer + kk], device_id=dev, device_id_type=MESH_ID)

        def arrival(a, kk):
            dev, lin = peers[kk]
            src = ins[a] if a < ng else ins[a].at[lin]
            return pltpu.make_async_remote_copy(src_ref=src, dst_ref=outs[a].at[lin], send_sem=send_sems.at[a * npeer + kk],
                                                recv_sem=recv_sems.at[a * npeer + kk], device_id=dev, device_id_type=MESH_ID)

        local = []
        for a in range(na):
            src = ins[a] if a < ng else ins[a].at[me]
            cp = pltpu.make_async_copy(src, outs[a].at[me], loc_sems.at[a])
            cp.start()
            local.append(cp)
        sent = [remote(a, kk) for kk in range(npeer) for a in range(na)]
        for cp in sent:
            cp.start()
        for kk in range(npeer):
            for a in range(na):
                arrival(a, kk).wait_recv()
        for cp in sent:
            cp.wait_send()
        for cp in local:
            cp.wait()

    arrs = list(gather) + list(scatter)
    out_shape = [jax.ShapeDtypeStruct((N_DEV,) + tuple(a.shape), a.dtype) for a in gather]
    out_shape += [jax.ShapeDtypeStruct(tuple(a.shape), a.dtype) for a in scatter]
    anyspec = pl.BlockSpec(memory_space=pl.ANY)
    return pl.pallas_call(
        body, name=name, in_specs=[anyspec] * na, out_specs=[anyspec] * na, out_shape=out_shape,
        scratch_shapes=[pltpu.SemaphoreType.DMA((na * npeer,)), pltpu.SemaphoreType.DMA((na * npeer,)),
                        pltpu.SemaphoreType.DMA((na,))],
        compiler_params=pltpu.CompilerParams(has_side_effects=True),
    )(*arrs)


def _peer_table():
    x, y, c = lax.axis_index("x"), lax.axis_index("y"), lax.axis_index("c")
    me = 4 * x + 2 * y + c
    peers = []
    for fx, fy, fc in _FLIPS:
        px = 1 - x if fx else x
        py = 1 - y if fy else y
        pc = 1 - c if fc else c
        peers.append(((px, py, pc), 4 * px + 2 * py + pc))
    return me, peers


_HBM = pl.BlockSpec(memory_space=pltpu.HBM)
_SEM = pl.BlockSpec(memory_space=pltpu.SEMAPHORE)
_ANY = pl.BlockSpec(memory_space=pl.ANY)
_EFFECT = pltpu.SideEffectType.DATAFLOW_SIDE_EFFECTING


def _split_copy(srcs_are_pieces, src_refs, land_refs, send_sem, recv_sem, a, kk, me, peers, arriving):
    dev, lin = peers[kk]
    npeer = len(_FLIPS)
    src = src_refs[a] if srcs_are_pieces[a] else src_refs[a].at[lin]
    dst = land_refs[a].at[lin] if arriving else land_refs[a].at[me]
    return pltpu.make_async_remote_copy(src_ref=src, dst_ref=dst, send_sem=send_sem.at[a * npeer + kk],
                                        recv_sem=recv_sem.at[a * npeer + kk], device_id=dev, device_id_type=MESH_ID)


def _xchg_start(name, gather, scatter, after=None):
    me_out = 4 * lax.axis_index("x") + 2 * lax.axis_index("y") + lax.axis_index("c")
    srcs = list(gather) + list(scatter)
    is_piece = [True] * len(gather) + [False] * len(scatter)
    lands = []
    for a, piece in zip(srcs, is_piece):
        own = a[None] if piece else lax.dynamic_slice_in_dim(a, me_out, 1, axis=0)
        shape = ((N_DEV,) + tuple(a.shape)) if piece else tuple(a.shape)
        start = (me_out,) + (0,) * (len(shape) - 1)
        lands.append(lax.dynamic_update_slice(lax.empty(shape, a.dtype), own, start))
    n = len(srcs)
    nsem = n * len(_FLIPS)
    has_after = after is not None

    def body(*refs):
        src_refs = refs[:n]
        land_refs = refs[n:2 * n]
        outs = refs[2 * n + (1 if has_after else 0):]
        send_sem, recv_sem = outs[0], outs[1]
        token = outs[-1]
        me, peers = _peer_table()
        for kk in range(len(_FLIPS)):
            for a in range(n):
                _split_copy(is_piece, src_refs, land_refs, send_sem, recv_sem, a, kk, me, peers, False).start()
        token[...] = jnp.zeros_like(token)

    out_shape = ([pltpu.SemaphoreType.DMA((nsem,)), pltpu.SemaphoreType.DMA((nsem,))]
                 + [pltpu.HBM(tuple(a.shape), a.dtype) for a in srcs] + [pltpu.HBM(tuple(a.shape), a.dtype) for a in lands]
                 + [jax.ShapeDtypeStruct((8, 128), F32)])
    args = [pltpu.with_memory_space_constraint(a, pltpu.HBM) for a in srcs + lands] + ([after] if has_after else [])
    res = pl.pallas_call(
        body, name=name, out_shape=out_shape,
        in_specs=[_HBM] * (2 * n) + ([_ANY] if has_after else []),
        out_specs=[_SEM, _SEM] + [_HBM] * (2 * n) + [pl.BlockSpec(memory_space=pltpu.VMEM)],
        input_output_aliases={i: 2 + i for i in range(2 * n)},
        compiler_params=pltpu.CompilerParams(has_side_effects=_EFFECT),
    )(*args)
    state = (res[0], res[1], list(res[2:2 + n]), list(res[2 + n:2 + 2 * n]), is_piece)
    return state, res[-1]


def _xchg_wait(name, state, after):
    send_sem, recv_sem, srcs, lands, is_piece = state
    n = len(srcs)

    def body(*refs):
        src_refs = refs[:n]
        land_refs = refs[n:2 * n]
        s_sem, r_sem = refs[2 * n], refs[2 * n + 1]
        me, peers = _peer_table()
        for kk in range(len(_FLIPS)):
            for a in range(n):
                cp = _split_copy(is_piece, src_refs, land_refs, s_sem, r_sem, a, kk, me, peers, True)
                cp.wait_send()
                cp.wait_recv()

    out_shape = [pltpu.HBM(tuple(a.shape), a.dtype) for a in srcs] + [pltpu.HBM(tuple(a.shape), a.dtype) for a in lands]
    res = pl.pallas_call(
        body, name=name, out_shape=out_shape,
        in_specs=[_HBM] * (2 * n) + [_SEM, _SEM, _ANY], out_specs=[_HBM] * (2 * n),
        input_output_aliases={i: i for i in range(2 * n)},
        compiler_params=pltpu.CompilerParams(has_side_effects=_EFFECT),
    )(*srcs, *lands, send_sem, recv_sem, after)
    return list(res[n:])


_SIB = (0, 0, 1)
_ICI = ((0, 1, 0), (1, 0, 0), (1, 1, 0))


def _flip(fl):
    x, y, c = lax.axis_index("x"), lax.axis_index("y"), lax.axis_index("c")
    px = 1 - x if fl[0] else x
    py = 1 - y if fl[1] else y
    pc = 1 - c if fl[2] else c
    return (px, py, pc), 4 * px + 2 * py + pc


def _gather2_start(name, pieces, after=None):
    me_out = 4 * lax.axis_index("x") + 2 * lax.axis_index("y") + lax.axis_index("c")
    pieces = list(pieces)
    n = len(pieces)
    lands = [lax.dynamic_update_slice(lax.empty((N_DEV,) + tuple(a.shape), a.dtype), a[None],
                                      (me_out,) + (0,) * a.ndim) for a in pieces]
    first = (_SIB,) + _ICI
    has_after = after is not None

    def body(*refs):
        src_refs, land_refs = refs[:n], refs[n:2 * n]
        outs = refs[2 * n + (1 if has_after else 0):]
        send_sem, recv_sem, token = outs[0], outs[1], outs[-1]
        _, me = _flip((0, 0, 0))
        for kk, fl in enumerate(first):
            dev, _ = _flip(fl)
            for a in range(n):
                pltpu.make_async_remote_copy(src_ref=src_refs[a], dst_ref=land_refs[a].at[me],
                                             send_sem=send_sem.at[a * 4 + kk], recv_sem=recv_sem.at[a * 4 + kk],
                                             device_id=dev, device_id_type=MESH_ID).start()
        token[...] = jnp.zeros_like(token)

    hbm = [pltpu.HBM(tuple(a.shape), a.dtype) for a in pieces + lands]
    res = pl.pallas_call(
        body, name=name,
        out_shape=[pltpu.SemaphoreType.DMA((4 * n,)), pltpu.SemaphoreType.DMA((4 * n,))] + hbm
        + [jax.ShapeDtypeStruct((8, 128), F32)],
        in_specs=[_HBM] * (2 * n) + ([_ANY] if has_after else []),
        out_specs=[_SEM, _SEM] + [_HBM] * (2 * n) + [pl.BlockSpec(memory_space=pltpu.VMEM)],
        input_output_aliases={i: 2 + i for i in range(2 * n)},
        compiler_params=pltpu.CompilerParams(has_side_effects=_EFFECT),
    )(*([pltpu.with_memory_space_constraint(a, pltpu.HBM) for a in pieces + lands] + ([after] if has_after else [])))
    return (res[0], res[1], list(res[2:2 + n]), list(res[2 + n:2 + 2 * n])), res[-1]


def _gather2_forward(name, state, after):
    send_a, recv_a, pieces, lands = state
    n = len(pieces)
    first = (_SIB,) + _ICI

    def body(*refs):
        src_refs, land_refs = refs[:n], refs[n:2 * n]
        s_a, r_a = refs[2 * n], refs[2 * n + 1]
        outs = refs[2 * n + 3:]
        send_b, recv_b, token = outs[0], outs[1], outs[-1]
        for kk, fl in enumerate(first):
            dev, lin = _flip(fl)
            for a in range(n):
                cp = pltpu.make_async_remote_copy(src_ref=src_refs[a], dst_ref=land_refs[a].at[lin],
                                                  send_sem=s_a.at[a * 4 + kk], recv_sem=r_a.at[a * 4 + kk],
                                                  device_id=dev, device_id_type=MESH_ID)
                cp.wait_send()
                cp.wait_recv()
        sib, _ = _flip(_SIB)
        for j, fl in enumerate(_ICI):
            _, lin = _flip(fl)
            for a in range(n):
                pltpu.make_async_remote_copy(src_ref=land_refs[a].at[lin], dst_ref=land_refs[a].at[lin],
                                             send_sem=send_b.at[a * 3 + j], recv_sem=recv_b.at[a * 3 + j],
                                             device_id=sib, device_id_type=MESH_ID).start()
        token[...] = jnp.zeros_like(token)

    hbm = [pltpu.HBM(tuple(a.shape), a.dtype) for a in pieces + lands]
    res = pl.pallas_call(
        body, name=name,
        out_shape=[pltpu.SemaphoreType.DMA((3 * n,)), pltpu.SemaphoreType.DMA((3 * n,))] + hbm
        + [jax.ShapeDtypeStruct((8, 128), F32)],
        in_specs=[_HBM] * (2 * n) + [_SEM, _SEM, _ANY],
        out_specs=[_SEM, _SEM] + [_HBM] * (2 * n) + [pl.BlockSpec(memory_space=pltpu.VMEM)],
        input_output_aliases={i: 2 + i for i in range(2 * n)},
        compiler_params=pltpu.CompilerParams(has_side_effects=_EFFECT),
    )(*pieces, *lands, send_a, recv_a, after)
    return (res[0], res[1], list(res[2 + n:2 + 2 * n])), res[-1]


def _gather2_wait(name, state, after):
    send_b, recv_b, lands = state
    n = len(lands)

    def body(*refs):
        land_refs = refs[:n]
        s_b, r_b = refs[n], refs[n + 1]
        sib, _ = _flip(_SIB)
        for j, fl in enumerate(_ICI):
            _, sent = _flip(fl)
            _, arriving = _flip((fl[0], fl[1], 1))
            for a in range(n):
                cp = pltpu.make_async_remote_copy(src_ref=land_refs[a].at[sent], dst_ref=land_refs[a].at[arriving],
                                                  send_sem=s_b.at[a * 3 + j], recv_sem=r_b.at[a * 3 + j],
                                                  device_id=sib, device_id_type=MESH_ID)
                cp.wait_send()
                cp.wait_recv()

    res = pl.pallas_call(
        body, name=name, out_shape=[pltpu.HBM(tuple(a.shape), a.dtype) for a in lands],
        in_specs=[_HBM] * n + [_SEM, _SEM, _ANY], out_specs=[_HBM] * n,
        input_output_aliases={i: i for i in range(n)},
        compiler_params=pltpu.CompilerParams(has_side_effects=_EFFECT),
    )(*lands, send_b, recv_b, after)
    return list(res)


def _adam_math(w, g, m, v):
    m = ADAM_B1 * m + (1.0 - ADAM_B1) * g
    v = ADAM_B2 * v + (1.0 - ADAM_B2) * (g * g)
    m_hat = m / (1.0 - ADAM_B1 ** ADAM_STEP)
    v_hat = v / (1.0 - ADAM_B2 ** ADAM_STEP)
    delta = -ADAM_LR * (m_hat / (jnp.sqrt(v_hat) + ADAM_EPS) + ADAM_WD * w)
    return delta, m, v


def _adam(name, w, m, v, parts, transposed=False, dep=None):
    npart, _, cp = parts.shape
    has_dep = dep is not None
    if transposed:
        c, r = w.shape
        tr = _pick(r, (256, 128))
        blk = pl.BlockSpec((c, tr), lambda i: (0, i))
    else:
        r, c = w.shape
        tr = _pick(r, (256, 176, 128, 64, 16, 8, 1))
        blk = pl.BlockSpec((tr, c), lambda i: (i, 0))

    def body(w_ref, m_ref, v_ref, p_ref, *rest):
        g_ref, d_ref, mo_ref, vo_ref = rest[-4:]
        g = p_ref[0].astype(F32)
        for pp in range(1, npart):
            g = g + p_ref[pp].astype(F32)
        g = g.T[0:c, :] if transposed else g[:, 0:c]
        delta, mn, vn = _adam_math(w_ref[...], g, m_ref[...], v_ref[...])
        g_ref[...] = g
        d_ref[...] = delta
        mo_ref[...] = mn
        vo_ref[...] = vn

    out = jax.ShapeDtypeStruct(w.shape, F32)
    return pl.pallas_call(
        body, name=name, grid=(r // tr,),
        in_specs=[blk, blk, blk, pl.BlockSpec((npart, tr, cp), lambda i: (0, i, 0))]
        + ([pl.BlockSpec(memory_space=pl.ANY)] if has_dep else []),
        out_specs=[blk, blk, blk, blk], out_shape=[out, out, out, out], compiler_params=_params(("parallel",)),
    )(*((w, m, v, parts) + ((dep,) if has_dep else ())))


def _small_sum(name, packs):
    def body(p_ref, o_ref):
        tot = p_ref[0]
        for pp in range(1, N_DEV):
            tot = tot + p_ref[pp]
        o_ref[0:8, :] = tot[0:8, :]
        o_ref[8:24, :] = tot[8:24, :] + tot[24:40, :]

    return pl.pallas_call(body, name=name, out_shape=jax.ShapeDtypeStruct((24, D_MODEL), F32),
                          compiler_params=_params())(packs)


def _local_step(x, tgt, g1, gm, g2, gf, b_forget, sinks, weights, send):
    b, s, _ = x.shape
    l = s + PREFIX
    t = b * l
    (meta,) = weights("meta", x)
    h0 = jnp.concatenate([jnp.zeros((b, N_PAD, D_MODEL), F32), jnp.broadcast_to(meta[None], (b, N_META, D_MODEL)), x],
                         axis=1).reshape(t, D_MODEL)

    n1 = _rms_fwd("rms1_fwd", h0, g1)
    (w1i,) = weights("ffn1_in", n1)
    gu1, a1 = _ffn_in_fwd("ffn1_in_fwd", n1, w1i)
    w1o, wi, wa, wb, wo = weights("mix", weights("mix:forward", a1))
    h1, um = _mm_res_norm("ffn1_out_fwd", a1, w1o, h0, gm, alpha=0.5)
    qkv, gates, f2 = _proj_fwd("proj_fwd", um, wi)
    qkv3 = qkv.reshape(b, l, QKV_W)
    f3 = f2.reshape(b, l, 128)
    bf_row = jnp.pad(b_forget, ((0, 0), (0, 128 - B_HEADS)))
    c_col, c_row = _fgate_fwd("fgate_fwd", f3, bf_row)
    head_of_row = jnp.arange(STACK) // BLOCK
    slopes = jnp.exp2(-8.0 * (head_of_row + 1).astype(F32) / A_HEADS).reshape(STACK, 1)
    sink_rows = jnp.repeat(sinks.reshape(A_HEADS), BLOCK).reshape(STACK, 1)
    swa_bias = _swa_bias(slopes)
    oa3, lse_a = _swa_fwd("swa_fwd", qkv3, slopes, sink_rows, swa_bias)
    ob3, lse_b = _fox_fwd("fox_fwd", qkv3, c_col, c_row)
    oa = oa3.reshape(t, A_WIDTH)
    ob = ob3.reshape(t, B_WIDTH)
    mixed, ya, yb = _branch_gate_fwd("branch_gate_fwd", oa, ob, wa, wb, gates, dep=weights("ffn2:forward", ob))
    h2, n2 = _mm_res_norm("mix_out_fwd", mixed, wo, h1, g2)
    w2i, w2o = weights("ffn2", h2)
    gu2, a2 = _ffn_in_fwd("ffn2_in_fwd", n2, w2i)
    h3 = _mm_nn("ffn2_out_fwd", a2, w2o, alpha=0.5, res=h2, tm_c=(544, 384, 256, 128), tn_c=(1024,))

    dh3_3, loss_blk, dgf = _loss_head("loss_head", h3.reshape(b, l, D_MODEL), gf, tgt)
    dh3 = dh3_3.reshape(t, D_MODEL)

    def ffn_bwd(tag, dh, h_in, g_norm, n_in, gu, a, w_in_blk, w_out, one_send):
        dw_out = _mm_tn(tag + "_out_bwd_w", a, dh, alpha=0.5)
        dgu = _ffn_out_bwd_x(tag + "_out_bwd_x", dh, w_out, gu, dep=None if one_send else send(tag + "_out", (dw_out,)))
        dw_in = _ffn_in_bwd_w(tag + "_in_bwd_w", n_in, dgu)
        token = send(tag, (dw_in, dw_out)) if one_send else send(tag + "_in", (dw_in,))
        return _ffn_in_bwd_x(tag + "_in_bwd_x", dgu, w_in_blk, h_in, g_norm, dh, dep=token)

    dh2, dg2 = ffn_bwd("ffn2", dh3, h2, g2, n2, gu2, a2, w2i, w2o, True)

    dwo = _mm_tn("mix_out_bwd_w", mixed, dh2)
    dya, dyb, dgates = _mix_out_gate_bwd("mix_out_gate_bwd", dh2, wo, gates, ya, yb)
    doa = _mm_nt("branch_a_bwd_x", dya, wa, tn_c=(512,))
    dob = _mm_nt("branch_b_bwd_x", dyb, wb, out_dtype=BF16, tn_c=(512,))
    dwa = _mm_tn("branch_a_bwd_w", oa, dya, tm_c=(512,))
    dwb = _mm_tn("branch_b_bwd_w", ob, dyb, tm_c=(512,))
    dqkv3, dcq, dck = _fox_bwd("fox_bwd", qkv3, c_col, c_row, ob3, lse_b, dob.reshape(b, l, B_WIDTH))
    dqkv3, dsink = _swa_bwd("swa_bwd", qkv3, oa3, lse_a, doa.reshape(b, l, A_WIDTH), slopes, sink_rows, swa_bias, dqkv3)
    dqkv = dqkv3.reshape(t, QKV_W)
    df3, dbf = _fgate_bwd("fgate_bwd", f3, bf_row, dcq, dck)
    df = df3.reshape(t, 128)
    dwi_qkv = _mm_tn("proj_qkv_bwd_w", um, dqkv, tm_c=(512,), tn_c=(768,))
    dwi_g = _mm_tn("proj_gates_bwd_w", um, dgates, tm_c=(512,), tn_c=(512,))
    dwi_f = _mm_tn("proj_f_bwd_w", um, df, tm_c=(512,), tn_c=(128,))
    token = send("mix", (dwi_qkv, dwi_g, dwi_f, dwa, dwb, dwo))
    dh1, dgm = _proj_bwd_x("proj_bwd_x", dqkv, dgates, df, wi, h1, gm, dh2, dep=token)

    dh0, dg1 = ffn_bwd("ffn1", dh1, h0, g1, n1, gu1, a1, w1i, w1o, False)
    dh0_3 = dh0.reshape(b, l, D_MODEL)
    grad_x = dh0_3[:, PREFIX:, :]
    dmeta = dh0_3[:, N_PAD:PREFIX, :].reshape(b * N_META, D_MODEL)

    misc = jnp.concatenate([dbf[:, 0:B_HEADS], dsink[:, 0].reshape(1, A_HEADS), loss_blk[0:1, 0:1]], axis=1)
    misc = jnp.pad(misc, ((0, 0), (0, D_MODEL - misc.shape[1])))
    row = lax.broadcasted_iota(jnp.int32, (8, D_MODEL), 0)
    vec = jnp.zeros((8, D_MODEL), F32)
    for i, piece in enumerate((dg1, dgm, dg2, dgf, misc)):
        vec = jnp.where(row == i, piece, vec)
    small = jnp.concatenate([vec, dmeta], axis=0)
    return grad_x, small


def _pad_to(a, rows, cols):
    return jnp.pad(a, ((0, rows - a.shape[0]), (0, cols - a.shape[1])))


def _ffn_out_from_gathered(g):
    w = g.reshape(4, FF_SHARD, D_MODEL)
    return jnp.pad(w, ((0, 0), (0, FF_SHARD_P - FF_SHARD), (0, 0))).reshape(D_FF_P, D_MODEL)


def _ffn_out_to_scatter(dw):
    return dw.reshape(4, FF_SHARD_P, D_MODEL)[:, 0:FF_SHARD, :].reshape(N_DEV, FFO_SHARD, D_MODEL)


def _proj_segments():
    segs = [(HEAD_DIM * h, HEAD_DIM, 0, B_SEG + HEAD_DIM * A_HEAD_ORDER.index(h)) for h in range(A_HEADS)]
    segs += [(512, 128, 0, B_SEG + A_WIDTH), (640, 128, 0, B_SEG + A_WIDTH + 128)]
    for first, off in ((768, 0), (1280, 128), (1792, 256)):
        segs += [(first + 128 * hp, 128, 0, PAIR_W * hp + off) for hp in range(4)]
    segs += [(2304, B_HEADS, 2, 0), (2312, 2 * D_MODEL, 1, 0)]
    return segs


def _proj_from_gathered(g):
    def cols(first, width):
        out = []
        for p in range(N_DEV):
            lo, hi = max(first, WIN_SHARD * p), min(first + width, WIN_SHARD * (p + 1))
            if lo < hi:
                out.append(g[p, :, lo - WIN_SHARD * p:hi - WIN_SHARD * p])
        return out

    parts = []
    for arr in (0, 1, 2):
        for first, width, _, _ in sorted((s for s in _proj_segments() if s[2] == arr), key=lambda s: s[3]):
            parts += cols(first, width)
        if arr == 0:
            parts.append(jnp.zeros((D_MODEL, P_GATES - QKV_W), g.dtype))
    parts.append(jnp.zeros((D_MODEL, 128 - B_HEADS), g.dtype))
    return jnp.concatenate(parts, axis=1)


def _proj_to_scatter(dqkv_w, dg_w, df_w):
    arrays = (dqkv_w, dg_w, df_w)
    segs = sorted(_proj_segments())
    blocks = []
    for p in range(N_DEV):
        parts = []
        for first, width, arr, at in segs:
            lo, hi = max(first, WIN_SHARD * p), min(first + width, WIN_SHARD * (p + 1))
            if lo < hi:
                parts.append(arrays[arr][:, at + lo - first:at + hi - first])
        parts.append(jnp.zeros((D_MODEL, WIN_SHARD_P - WIN_SHARD), dqkv_w.dtype))
        blocks.append(jnp.concatenate(parts, axis=1))
    return jnp.stack(blocks, axis=0)


def _a_rows_from_natural(w):
    return jnp.concatenate([w[HEAD_DIM * h:HEAD_DIM * (h + 1)] for h in A_HEAD_ORDER], axis=0)


def _a_rows_to_natural(w):
    return jnp.concatenate([w[HEAD_DIM * A_HEAD_ORDER.index(h):HEAD_DIM * (A_HEAD_ORDER.index(h) + 1)]
                            for h in range(A_HEADS)], axis=0)


def kernel(x, meta_tokens, ffn1_norm, ffn1_w_in, ffn1_w_out, mix_norm, w_in, b_forget, attn_sinks, w_branch_a, w_branch_b, w_out, ffn2_norm, ffn2_w_in, ffn2_w_out, final_norm, loss_target, m_meta_tokens, m_ffn1_norm, m_ffn1_w_in, m_ffn1_w_out, m_mix_norm, m_w_in, m_b_forget, m_attn_sinks, m_w_branch_a, m_w_branch_b, m_w_out, m_ffn2_norm, m_ffn2_w_in, m_ffn2_w_out, m_final_norm, v_meta_tokens, v_ffn1_norm, v_ffn1_w_in, v_ffn1_w_out, v_mix_norm, v_w_in, v_b_forget, v_attn_sinks, v_w_branch_a, v_w_branch_b, v_w_out, v_ffn2_norm, v_ffn2_w_in, v_ffn2_w_out, v_final_norm):
    me = 4 * lax.axis_index("x") + 2 * lax.axis_index("y") + lax.axis_index("c")

    shards = (
        _pad_to(ffn1_w_in[0].astype(BF16), D_MODEL, FF_SHARD_P),
        ffn1_w_out[0].astype(BF16),
        _pad_to(w_in[0].astype(BF16), D_MODEL, WIN_SHARD_P),
        w_branch_a[0].astype(BF16), w_branch_b[0].astype(BF16), w_out[0].astype(BF16),
        _pad_to(ffn2_w_in[0].astype(BF16), D_MODEL, FF_SHARD_P),
        ffn2_w_out[0].astype(BF16),
        meta_tokens,
    )
    s1i, s1o, swi, swa, swb, swo, s2i, s2o, smeta = shards
    first_level, second_level = {}, {}
    first_level["meta"], tok = _gather2_start("gather_meta_start", (smeta,))
    first_level["ffn1_in"], tok = _gather2_start("gather_ffn1_in_start", (s1i,), after=tok)
    first_level["mix"], tok = _gather2_start("gather_mix_start", (s1o, swi, swa, swb, swo), after=tok)
    first_level["ffn2"], tok = _gather2_start("gather_ffn2_start", (s2i, s2o), after=tok)
    started = {"tok": tok}

    def weights(group, after):
        if group.endswith(":forward"):
            group = group[:-len(":forward")]
            second_level[group], token = _gather2_forward("gather_" + group + "_forward", first_level[group], after)
            return token
        if group == "meta":
            after = weights("meta:forward", started["tok"])
        if group == "ffn1_in":
            after = weights("ffn1_in:forward", after)
        got = _gather2_wait("gather_" + group + "_wait", second_level[group], after)
        if group == "mix":
            g1o, gwi, gwa, gwb, gwo = got
            return (_ffn_out_from_gathered(g1o), _proj_from_gathered(gwi),
                    _a_rows_from_natural(gwa.transpose(1, 0, 2).reshape(A_WIDTH, D_MODEL)),
                    gwb.transpose(1, 0, 2).reshape(B_WIDTH, D_MODEL), gwo.reshape(D_MODEL, D_MODEL))
        if group == "meta":
            return (got[0].transpose(1, 0, 2).reshape(N_META, D_MODEL),)
        if group == "ffn1_in":
            return (got[0].reshape(2, 4, D_MODEL, FF_SHARD_P),)
        return got[0].reshape(2, 4, D_MODEL, FF_SHARD_P), _ffn_out_from_gathered(got[1])

    scatter_state = {}

    def send(group, grads):
        if group == "mix":
            dwi_qkv, dwi_g, dwi_f, dwa, dwb, dwo = grads
            dwa = _a_rows_to_natural(dwa)
            blocks = (_proj_to_scatter(dwi_qkv, dwi_g, dwi_f), dwa.reshape(A_WIDTH, N_DEV, 128).transpose(1, 0, 2),
                      dwb.reshape(B_WIDTH, N_DEV, 128).transpose(1, 0, 2), dwo.reshape(N_DEV, 128, D_MODEL))
        elif group.endswith("_in"):
            blocks = (grads[0].reshape(N_DEV, D_MODEL, FF_SHARD_P),)
        elif group.endswith("_out"):
            blocks = (_ffn_out_to_scatter(grads[0]),)
        else:
            blocks = (grads[0].reshape(N_DEV, D_MODEL, FF_SHARD_P), _ffn_out_to_scatter(grads[1]))
        scatter_state[group], token = _xchg_start("scatter_" + group + "_start", (), blocks)
        return token

    gf = final_norm.reshape(1, D_MODEL)
    grad_x, small = _local_step(x, loss_target, ffn1_norm, mix_norm, ffn2_norm, gf, b_forget, attn_sinks, weights, send)

    small_state, after = _xchg_start("gather_small_start", (small,), ())
    out = {}
    updates = (
        ("ffn2", (("ffn2_w_in", ffn2_w_in, m_ffn2_w_in, v_ffn2_w_in), ("ffn2_w_out", ffn2_w_out, m_ffn2_w_out, v_ffn2_w_out))),
        ("ffn1_out", (("ffn1_w_out", ffn1_w_out, m_ffn1_w_out, v_ffn1_w_out),)),
        ("mix", (("w_in", w_in, m_w_in, v_w_in), ("w_branch_a", w_branch_a, m_w_branch_a, v_w_branch_a),
                 ("w_branch_b", w_branch_b, m_w_branch_b, v_w_branch_b), ("w_out", w_out, m_w_out, v_w_out))),
    )
    last_update = ("ffn1_in", (("ffn1_w_in", ffn1_w_in, m_ffn1_w_in, v_ffn1_w_in),))

    def update(group, members, after):
        parts_list = _xchg_wait("scatter_" + group + "_wait", scatter_state[group], after)
        prev = None
        for (nm, w, m, v), parts in zip(members, parts_list):
            if nm.endswith("w_in"):
                res4 = _adam("adam_" + nm, w[0].T, m[0].T, v[0].T, parts, transposed=True, dep=prev)
                out[nm] = tuple(r.T[None] for r in res4)
            else:
                res4 = _adam("adam_" + nm, w[0], m[0], v[0], parts, dep=prev)
                out[nm] = tuple(r[None] for r in res4)
            prev = res4[0]
        return prev

    for group, members in updates + (last_update,):
        after = update(group, members, after)
    (packs,) = _xchg_wait("gather_small_wait", small_state, after)

    tot = _small_sum("small_sum", packs)
    loss = tot[4, 2 * B_HEADS]
    g_meta = lax.dynamic_slice(tot[8:24, :], (0, me * 128), (N_META, 128))
    out["meta_tokens"] = tuple(_adam("adam_meta_tokens", meta_tokens, m_meta_tokens, v_meta_tokens, g_meta[None]))

    def pack_small(n1, nm, n2, nf, bfv, skv):
        misc = jnp.pad(jnp.concatenate([bfv, skv], axis=1), ((0, 0), (0, D_MODEL - 2 * B_HEADS)))
        row = lax.broadcasted_iota(jnp.int32, (8, D_MODEL), 0)
        vec = jnp.zeros((8, D_MODEL), F32)
        for i, piece in enumerate((n1, nm, n2, nf.reshape(1, D_MODEL), misc)):
            vec = jnp.where(row == i, piece, vec)
        return vec

    w_pack = pack_small(ffn1_norm, mix_norm, ffn2_norm, final_norm, b_forget, attn_sinks)
    m_pack = pack_small(m_ffn1_norm, m_mix_norm, m_ffn2_norm, m_final_norm, m_b_forget, m_attn_sinks)
    v_pack = pack_small(v_ffn1_norm, v_mix_norm, v_ffn2_norm, v_final_norm, v_b_forget, v_attn_sinks)
    small4 = _adam("adam_small", w_pack, m_pack, v_pack, tot[0:8][None])
    for i, nm in enumerate(("ffn1_norm", "mix_norm", "ffn2_norm")):
        out[nm] = tuple(r[i:i + 1] for r in small4)
    out["final_norm"] = tuple(r[3] for r in small4)
    out["b_forget"] = tuple(r[4:5, 0:B_HEADS] for r in small4)
    out["attn_sinks"] = tuple(r[4:5, B_HEADS:2 * B_HEADS] for r in small4)

    names = ("meta_tokens", "ffn1_norm", "ffn1_w_in", "ffn1_w_out", "mix_norm", "w_in", "b_forget", "attn_sinks",
             "w_branch_a", "w_branch_b", "w_out", "ffn2_norm", "ffn2_w_in", "ffn2_w_out", "final_norm")
    return (loss, grad_x) + tuple(out[nm][kind] for kind in range(4) for nm in names)
```

```python
import jax
import jax.numpy as jnp
from jax import lax
from jax.experimental import pallas as pl
from jax.experimental.pallas import tpu as pltpu

F32 = jnp.float32
BF16 = jnp.bfloat16

D_MODEL = 1024
N_META = 16
BLOCK = 128
PREFIX = 128
N_PAD = PREFIX - N_META
HEAD_DIM = 64
A_HEADS = 8
A_KV_HEADS = 2
A_GROUP = 4
B_HEADS = 8
A_WIDTH = 512
A_KV_WIDTH = 128
B_WIDTH = 512
D_FF = 2816
N_DEV = 8
FF_SHARD = 2 * D_FF // N_DEV
FF_SHARD_P = 768
FFO_SHARD = D_FF // N_DEV
FFO_SHARD_P = FF_SHARD_P // 2
D_FF_P = 4 * FF_SHARD_P
W_IN_COLS = 4360
WIN_SHARD = W_IN_COLS // N_DEV
WIN_SHARD_P = 640
PAIR_W = 3 * 128
B_SEG = 4 * PAIR_W
A_SEG = A_WIDTH + 2 * A_KV_WIDTH
QKV_W = B_SEG + A_SEG
P_GATES = 2 * (2 * D_MODEL)
P_F = P_GATES + 2 * D_MODEL
PROJ_P = P_F + 128
A_HEAD_ORDER = (0, 4, 1, 5, 2, 6, 3, 7)
EPS = 1e-6
NEG = -1e30
SCALE = HEAD_DIM ** -0.5
ADAM_LR = 0.001
ADAM_B1 = 0.9
ADAM_B2 = 0.999
ADAM_EPS = 1e-08
ADAM_WD = 0.01
ADAM_STEP = 10
VMEM_LIMIT = 56 * 1024 * 1024
MESH_ID = pl.DeviceIdType.MESH
SMALL_ROWS = 40

_NN = (((1,), (0,)), ((), ()))
_NT = (((1,), (1,)), ((), ()))
_TN = (((0,), (0,)), ((), ()))


def _params(sem=None):
    return pltpu.CompilerParams(dimension_semantics=sem, vmem_limit_bytes=VMEM_LIMIT)


def _pick(n, cands):
    for c in cands:
        if n % c == 0:
            return c
    raise ValueError(f"no tile for {n}")


def _bf(v):
    return v if v.dtype == BF16 else v.astype(BF16)


def _mm(name, a, b, dims, grid, a_spec, b_spec, o_spec, out_shape, out_dtype, acc_shape, k_axis=None, nk=1,
        alpha=1.0, res=None, res_spec=None, dep=None):
    has_res = res is not None
    has_dep = dep is not None

    def body(*refs):
        a_ref, b_ref = refs[0], refs[1]
        r_ref = refs[2] if has_res else None
        o_ref = refs[2 + has_res + has_dep]

        def finish(acc):
            if alpha != 1.0:
                acc = acc * alpha
            if has_res:
                acc = acc + r_ref[...]
            o_ref[...] = acc.astype(o_ref.dtype)

        part = lax.dot_general(_bf(a_ref[...]), _bf(b_ref[...]), dims, preferred_element_type=F32)
        if nk == 1:
            finish(part)
        else:
            acc_ref = refs[-1]
            k = pl.program_id(k_axis)

            @pl.when(k == 0)
            def _():
                acc_ref[...] = part

            @pl.when(k > 0)
            def _():
                acc_ref[...] += part

            @pl.when(k == nk - 1)
            def _():
                finish(acc_ref[...])

    in_specs = [a_spec, b_spec] + ([res_spec] if has_res else []) + ([pl.BlockSpec(memory_space=pl.ANY)] if has_dep else [])
    args = (a, b) + ((res,) if has_res else ()) + ((dep,) if has_dep else ())
    sem = tuple("arbitrary" if (nk > 1 and i == k_axis) else "parallel" for i in range(len(grid)))
    return pl.pallas_call(
        body, name=name, grid=grid, in_specs=in_specs, out_specs=o_spec,
        out_shape=jax.ShapeDtypeStruct(out_shape, out_dtype),
        scratch_shapes=[pltpu.VMEM(acc_shape, F32)] if nk > 1 else [],
        compiler_params=_params(sem),
    )(*args)


def _mm_res_norm(name, a, w, res, g_next, alpha=1.0):
    t, k = a.shape
    tm = _pick(t, (544, 384, 256, 128))

    def body(a_ref, w_ref, r_ref, g_ref, h_ref, n_ref):
        acc = lax.dot_general(_bf(a_ref[...]), w_ref[...], _NN, preferred_element_type=F32)
        if alpha != 1.0:
            acc = acc * alpha
        hv = acc + r_ref[...]
        h_ref[...] = hv
        r = lax.rsqrt(jnp.mean(hv * hv, axis=-1, keepdims=True) + EPS)
        n_ref[...] = ((hv * r) * g_ref[...]).astype(BF16)

    row = pl.BlockSpec((tm, D_MODEL), lambda i: (i, 0))
    return pl.pallas_call(
        body, name=name, grid=(t // tm,),
        in_specs=[pl.BlockSpec((tm, k), lambda i: (i, 0)), pl.BlockSpec((k, D_MODEL), lambda i: (0, 0)), row,
                  pl.BlockSpec((1, D_MODEL), lambda i: (0, 0))],
        out_specs=[row, row],
        out_shape=[jax.ShapeDtypeStruct((t, D_MODEL), F32), jax.ShapeDtypeStruct((t, D_MODEL), BF16)],
        compiler_params=_params(("parallel",)),
    )(a, w, res, g_next)


def _mm_nn(name, a, b, out_dtype=F32, alpha=1.0, res=None, tn_c=(512, 640, 256, 128), cols=None,
           tm_c=(1088, 768, 512, 256, 128), dep=None):
    t, k = a.shape
    c0, n = (0, b.shape[1]) if cols is None else cols
    tm = _pick(t, tm_c)
    tn = _pick(n, tn_c)
    assert c0 % tn == 0
    jb = c0 // tn
    return _mm(name, a, b, _NN, (t // tm, n // tn),
               pl.BlockSpec((tm, k), lambda i, j: (i, 0)), pl.BlockSpec((k, tn), lambda i, j: (0, jb + j)),
               pl.BlockSpec((tm, tn), lambda i, j: (i, j)), (t, n), out_dtype, None,
               alpha=alpha, res=res, res_spec=pl.BlockSpec((tm, tn), lambda i, j: (i, j)), dep=dep)


def _mm_nt(name, a, b, out_dtype=F32, alpha=1.0, tn_c=(768, 512, 256, 128), tk_c=None, dep=None, res=None, kcols=None):
    t, k = a.shape
    n = b.shape[0]
    c0 = 0 if kcols is None else kcols[0]
    tm = _pick(t, (1088, 768, 512, 256, 128))
    tn = _pick(n, tn_c)
    tk = k if tk_c is None else _pick(k, tk_c)
    nk = k // tk
    assert c0 % tk == 0
    kb = c0 // tk
    return _mm(name, a, b, _NT, (t // tm, n // tn, nk),
               pl.BlockSpec((tm, tk), lambda i, j, kk: (i, kk)), pl.BlockSpec((tn, tk), lambda i, j, kk: (j, kb + kk)),
               pl.BlockSpec((tm, tn), lambda i, j, kk: (i, j)), (t, n), out_dtype, (tm, tn), k_axis=2, nk=nk, alpha=alpha,
               dep=dep, res=res, res_spec=pl.BlockSpec((tm, tn), lambda i, j, kk: (i, j)))


def _mm_tn(name, a, b, out_dtype=BF16, alpha=1.0, tm_c=(768, 512, 256, 128), tn_c=(512, 640, 256, 128)):
    t, m = a.shape
    n = b.shape[1]
    tm = _pick(m, tm_c)
    tn = _pick(n, tn_c)
    bytes_a, bytes_b = a.size * a.dtype.itemsize, b.size * b.dtype.itemsize
    if bytes_a + bytes_b * (m // tm) <= bytes_b + bytes_a * (n // tn):
        return _mm(name, a, b, _TN, (m // tm, n // tn),
                   pl.BlockSpec((t, tm), lambda i, j: (0, i)), pl.BlockSpec((t, tn), lambda i, j: (0, j)),
                   pl.BlockSpec((tm, tn), lambda i, j: (i, j)), (m, n), out_dtype, None, alpha=alpha)
    return _mm(name, a, b, _TN, (n // tn, m // tm),
               pl.BlockSpec((t, tm), lambda j, i: (0, i)), pl.BlockSpec((t, tn), lambda j, i: (0, j)),
               pl.BlockSpec((tm, tn), lambda j, i: (i, j)), (m, n), out_dtype, None, alpha=alpha)


def _ffn_in_fwd(name, n, wblk, dep=None):
    t = n.shape[0]
    tm = _pick(t, (1088, 768, 512, 256, 128))
    has_dep = dep is not None

    def body(n_ref, w_ref, *rest):
        gu_ref, a_ref = rest[-2], rest[-1]
        nv = n_ref[...]
        g = lax.dot_general(nv, w_ref[0], _NN, preferred_element_type=F32)
        u = lax.dot_general(nv, w_ref[1], _NN, preferred_element_type=F32)
        sg = jax.nn.sigmoid(g)
        silu = g * sg
        a_ref[...] = (silu * u).astype(BF16)
        gu_ref[0] = ((0.5 * u) * (sg + silu * (1.0 - sg))).astype(BF16)
        gu_ref[1] = (0.5 * silu).astype(BF16)

    return pl.pallas_call(
        body, name=name, grid=(t // tm, 4),
        in_specs=[pl.BlockSpec((tm, D_MODEL), lambda i, j: (i, 0)),
                  pl.BlockSpec((2, None, D_MODEL, FF_SHARD_P), lambda i, j: (0, j, 0, 0))]
        + ([pl.BlockSpec(memory_space=pl.ANY)] if has_dep else []),
        out_specs=[pl.BlockSpec((2, tm, FF_SHARD_P), lambda i, j: (0, i, j)),
                   pl.BlockSpec((tm, FF_SHARD_P), lambda i, j: (i, j))],
        out_shape=[jax.ShapeDtypeStruct((2, t, D_FF_P), BF16), jax.ShapeDtypeStruct((t, D_FF_P), BF16)],
        compiler_params=_params(("parallel", "parallel")),
    )(*((n, wblk) + ((dep,) if has_dep else ())))


def _ffn_out_bwd_x(name, dh, w_out, gu, dep=None):
    t = dh.shape[0]
    tm = _pick(t, (1088, 768, 512, 256, 128))
    has_dep = dep is not None

    def body(dh_ref, w_ref, gu_ref, *rest):
        o_ref = rest[-1]
        da = lax.dot_general(_bf(dh_ref[...]), w_ref[...], _NT, preferred_element_type=F32)
        o_ref[0] = (da * gu_ref[0].astype(F32)).astype(BF16)
        o_ref[1] = (da * gu_ref[1].astype(F32)).astype(BF16)

    gu_spec = pl.BlockSpec((2, tm, FF_SHARD_P), lambda i, j: (0, i, j))
    return pl.pallas_call(
        body, name=name, grid=(t // tm, 4),
        in_specs=[pl.BlockSpec((tm, D_MODEL), lambda i, j: (i, 0)), pl.BlockSpec((FF_SHARD_P, D_MODEL), lambda i, j: (j, 0)),
                  gu_spec] + ([pl.BlockSpec(memory_space=pl.ANY)] if has_dep else []),
        out_specs=gu_spec, out_shape=jax.ShapeDtypeStruct((2, t, D_FF_P), BF16),
        compiler_params=_params(("parallel", "parallel")),
    )(*((dh, w_out, gu) + ((dep,) if has_dep else ())))


def _rms_bwd_rows(dn, h, g, dres):
    r = lax.rsqrt(jnp.mean(h * h, axis=-1, keepdims=True) + EPS)
    tv = dn * g
    dot = jnp.mean(tv * h, axis=-1, keepdims=True)
    return dres + (r * tv - h * (r * r * r * dot)), jnp.sum(dn * (h * r), axis=0, keepdims=True)


def _accumulate_rows(ref, part, first):
    @pl.when(first)
    def _():
        ref[...] = part

    @pl.when(jnp.logical_not(first))
    def _():
        ref[...] += part


def _ffn_in_bwd_x(name, dgu, wblk, h_in, g_norm, dres, dep=None):
    t = dgu.shape[1]
    tm = _pick(t, (544, 384, 256, 128))
    has_dep = dep is not None

    def body(d_ref, w_ref, h_ref, g_ref, r_ref, *rest):
        dh_ref, dg_ref = rest[-2], rest[-1]
        acc = None
        for s in range(2):
            for j in range(4):
                part = lax.dot_general(d_ref[s, :, FF_SHARD_P * j:FF_SHARD_P * (j + 1)], w_ref[s, j], _NT,
                                       preferred_element_type=F32)
                acc = part if acc is None else acc + part
        dh, dg = _rms_bwd_rows(acc, h_ref[...], g_ref[...], r_ref[...])
        dh_ref[...] = dh
        _accumulate_rows(dg_ref, dg, pl.program_id(0) == 0)

    row = pl.BlockSpec((tm, D_MODEL), lambda i: (i, 0))
    vec = pl.BlockSpec((1, D_MODEL), lambda i: (0, 0))
    return pl.pallas_call(
        body, name=name, grid=(t // tm,),
        in_specs=[pl.BlockSpec((2, tm, D_FF_P), lambda i: (0, i, 0)),
                  pl.BlockSpec((2, 4, D_MODEL, FF_SHARD_P), lambda i: (0, 0, 0, 0), pipeline_mode=pl.Buffered(1)),
                  row, vec, row]
        + ([pl.BlockSpec(memory_space=pl.ANY)] if has_dep else []),
        out_specs=[row, vec],
        out_shape=[jax.ShapeDtypeStruct((t, D_MODEL), F32), jax.ShapeDtypeStruct((1, D_MODEL), F32)],
        compiler_params=_params(("arbitrary",)),
    )(*((dgu, wblk, h_in, g_norm, dres) + ((dep,) if has_dep else ())))


def _proj_fwd(name, um, wi):
    t = um.shape[0]
    tm = _pick(t, (544, 384, 256, 128))

    def body(u_ref, w_ref, q_ref, g_ref, f_ref):
        uv = u_ref[...]
        q_ref[...] = lax.dot_general(uv, w_ref[:, 0:QKV_W], _NN, preferred_element_type=F32).astype(BF16)
        g_ref[...] = lax.dot_general(uv, w_ref[:, P_GATES:P_F], _NN, preferred_element_type=F32)
        f_ref[...] = lax.dot_general(uv, w_ref[:, P_F:PROJ_P], _NN, preferred_element_type=F32)

    return pl.pallas_call(
        body, name=name, grid=(t // tm,),
        in_specs=[pl.BlockSpec((tm, D_MODEL), lambda i: (i, 0)),
                  pl.BlockSpec((D_MODEL, PROJ_P), lambda i: (0, 0), pipeline_mode=pl.Buffered(1))],
        out_specs=[pl.BlockSpec((tm, QKV_W), lambda i: (i, 0)), pl.BlockSpec((tm, 2 * D_MODEL), lambda i: (i, 0)),
                   pl.BlockSpec((tm, 128), lambda i: (i, 0))],
        out_shape=[jax.ShapeDtypeStruct((t, QKV_W), BF16), jax.ShapeDtypeStruct((t, 2 * D_MODEL), F32),
                   jax.ShapeDtypeStruct((t, 128), F32)],
        compiler_params=_params(("parallel",)),
    )(um, wi)


def _proj_bwd_x(name, dqkv, dgates, df, wi, h_in, g_norm, dres, dep=None):
    t = dqkv.shape[0]
    tm = _pick(t, (544, 384, 256, 128))
    has_dep = dep is not None

    def body(q_ref, gt_ref, f_ref, w_ref, h_ref, g_ref, r_ref, *rest):
        dh_ref, dg_ref = rest[-2], rest[-1]
        acc = lax.dot_general(q_ref[...], w_ref[:, 0:QKV_W], _NT, preferred_element_type=F32)
        acc = acc + lax.dot_general(gt_ref[...], w_ref[:, P_GATES:P_F], _NT, preferred_element_type=F32)
        acc = acc + lax.dot_general(f_ref[...], w_ref[:, P_F:PROJ_P], _NT, preferred_element_type=F32)
        dh, dg = _rms_bwd_rows(acc, h_ref[...], g_ref[...], r_ref[...])
        dh_ref[...] = dh
        _accumulate_rows(dg_ref, dg, pl.program_id(0) == 0)

    row = pl.BlockSpec((tm, D_MODEL), lambda i: (i, 0))
    vec = pl.BlockSpec((1, D_MODEL), lambda i: (0, 0))
    return pl.pallas_call(
        body, name=name, grid=(t // tm,),
        in_specs=[pl.BlockSpec((tm, QKV_W), lambda i: (i, 0)), pl.BlockSpec((tm, 2 * D_MODEL), lambda i: (i, 0)),
                  pl.BlockSpec((tm, 128), lambda i: (i, 0)),
                  pl.BlockSpec((D_MODEL, PROJ_P), lambda i: (0, 0), pipeline_mode=pl.Buffered(1)), row, vec, row]
        + ([pl.BlockSpec(memory_space=pl.ANY)] if has_dep else []),
        out_specs=[row, vec],
        out_shape=[jax.ShapeDtypeStruct((t, D_MODEL), F32), jax.ShapeDtypeStruct((1, D_MODEL), F32)],
        compiler_params=_params(("arbitrary",)),
    )(*((dqkv, dgates, df, wi, h_in, g_norm, dres) + ((dep,) if has_dep else ())))


def _ffn_in_bwd_w(name, n, dgu):
    t = n.shape[0]
    tk = t
    nk = 1
    return _mm(name, n, dgu, _TN, (2, 4, nk),
               pl.BlockSpec((tk, D_MODEL), lambda s, j, kk: (kk, 0)),
               pl.BlockSpec((None, tk, FF_SHARD_P), lambda s, j, kk: (s, kk, j)),
               pl.BlockSpec((None, None, D_MODEL, FF_SHARD_P), lambda s, j, kk: (s, j, 0, 0)),
               (2, 4, D_MODEL, FF_SHARD_P), BF16, (D_MODEL, FF_SHARD_P), k_axis=2, nk=nk)


def _rms_fwd(name, h, g):
    t = h.shape[0]
    tm = _pick(t, (544, 384, 256, 128))

    def body(h_ref, g_ref, o_ref):
        hv = h_ref[...]
        r = lax.rsqrt(jnp.mean(hv * hv, axis=-1, keepdims=True) + EPS)
        o_ref[...] = ((hv * r) * g_ref[...]).astype(BF16)

    return pl.pallas_call(
        body, name=name, grid=(t // tm,),
        in_specs=[pl.BlockSpec((tm, D_MODEL), lambda i: (i, 0)), pl.BlockSpec((1, D_MODEL), lambda i: (0, 0))],
        out_specs=pl.BlockSpec((tm, D_MODEL), lambda i: (i, 0)),
        out_shape=jax.ShapeDtypeStruct((t, D_MODEL), BF16), compiler_params=_params(("parallel",)),
    )(h, g)


def _rms_bwd(name, h, g, dn, dres):
    t = h.shape[0]
    tm = _pick(t, (544, 384, 256, 128))

    def body(h_ref, g_ref, dn_ref, dres_ref, dh_ref, dg_ref):
        i = pl.program_id(0)
        hv = h_ref[...]
        dnv = dn_ref[...]
        r = lax.rsqrt(jnp.mean(hv * hv, axis=-1, keepdims=True) + EPS)
        tv = dnv * g_ref[...]
        dot = jnp.mean(tv * hv, axis=-1, keepdims=True)
        dh_ref[...] = dres_ref[...] + (r * tv - hv * (r * r * r * dot))
        part = jnp.sum(dnv * (hv * r), axis=0, keepdims=True)

        @pl.when(i == 0)
        def _():
            dg_ref[...] = part

        @pl.when(i > 0)
        def _():
            dg_ref[...] += part

    row = pl.BlockSpec((tm, D_MODEL), lambda i: (i, 0))
    vec = pl.BlockSpec((1, D_MODEL), lambda i: (0, 0))
    return pl.pallas_call(
        body, name=name, grid=(t // tm,), in_specs=[row, vec, row, row], out_specs=[row, vec],
        out_shape=[jax.ShapeDtypeStruct((t, D_MODEL), F32), jax.ShapeDtypeStruct((1, D_MODEL), F32)],
        compiler_params=_params(("arbitrary",)),
    )(h, g, dn, dres)


def _branch_gate_fwd(name, oa, ob, wa, wb, gates, dep=None):
    t = gates.shape[0]
    tm = _pick(t, (544, 384, 256, 128))
    has_dep = dep is not None

    def body(oa_ref, ob_ref, wa_ref, wb_ref, g_ref, *rest):
        o_ref, ya_ref, yb_ref = rest[-3:]
        ya = lax.dot_general(_bf(oa_ref[...]), wa_ref[...], _NN, preferred_element_type=F32)
        yb = lax.dot_general(_bf(ob_ref[...]), wb_ref[...], _NN, preferred_element_type=F32)
        sa = jax.nn.sigmoid(g_ref[:, 0:D_MODEL])
        sb = jax.nn.sigmoid(g_ref[:, D_MODEL:2 * D_MODEL])
        o_ref[...] = (sa * ya + sb * yb).astype(BF16)
        ya_ref[...] = ya.astype(BF16)
        yb_ref[...] = yb.astype(BF16)

    blk = pl.BlockSpec((tm, D_MODEL), lambda i: (i, 0))
    narrow = pl.BlockSpec((tm, A_WIDTH), lambda i: (i, 0))
    wide = pl.BlockSpec((tm, 2 * D_MODEL), lambda i: (i, 0))
    wspec = pl.BlockSpec((A_WIDTH, D_MODEL), lambda i: (0, 0))
    out = jax.ShapeDtypeStruct((t, D_MODEL), BF16)
    return pl.pallas_call(
        body, name=name, grid=(t // tm,),
        in_specs=[narrow, narrow, wspec, wspec, wide] + ([pl.BlockSpec(memory_space=pl.ANY)] if has_dep else []),
        out_specs=[blk, blk, blk], out_shape=[out, out, out], compiler_params=_params(("parallel",)),
    )(*((oa, ob, wa, wb, gates) + ((dep,) if has_dep else ())))


def _branch_bwd_x(name, dya, dyb, wa, wb):
    t = dya.shape[0]
    tm = _pick(t, (1088, 768, 512, 256, 128))

    def body(da_ref, db_ref, wa_ref, wb_ref, oa_ref, ob_ref):
        oa_ref[...] = lax.dot_general(da_ref[...], wa_ref[...], _NT, preferred_element_type=F32)
        ob_ref[...] = lax.dot_general(db_ref[...], wb_ref[...], _NT, preferred_element_type=F32).astype(BF16)

    blk = pl.BlockSpec((tm, D_MODEL), lambda i: (i, 0))
    narrow = pl.BlockSpec((tm, A_WIDTH), lambda i: (i, 0))
    wspec = pl.BlockSpec((A_WIDTH, D_MODEL), lambda i: (0, 0), pipeline_mode=pl.Buffered(1))
    return pl.pallas_call(
        body, name=name, grid=(t // tm,), in_specs=[blk, blk, wspec, wspec], out_specs=[narrow, narrow],
        out_shape=[jax.ShapeDtypeStruct((t, A_WIDTH), F32), jax.ShapeDtypeStruct((t, B_WIDTH), BF16)],
        compiler_params=_params(("parallel",)),
    )(dya, dyb, wa, wb)


def _branch_bwd_w(name, oa, ob, dya, dyb):
    t = oa.shape[0]
    tn = 512

    def body(oa_ref, ob_ref, da_ref, db_ref, wa_ref, wb_ref):
        wa_ref[...] = lax.dot_general(_bf(oa_ref[...]), da_ref[...], _TN, preferred_element_type=F32).astype(BF16)
        wb_ref[...] = lax.dot_general(_bf(ob_ref[...]), db_ref[...], _TN, preferred_element_type=F32).astype(BF16)

    whole = pl.BlockSpec((t, A_WIDTH), lambda j: (0, 0), pipeline_mode=pl.Buffered(1))
    cols = pl.BlockSpec((t, tn), lambda j: (0, j))
    out_spec = pl.BlockSpec((A_WIDTH, tn), lambda j: (0, j))
    out = jax.ShapeDtypeStruct((A_WIDTH, D_MODEL), BF16)
    return pl.pallas_call(
        body, name=name, grid=(D_MODEL // tn,), in_specs=[whole, whole, cols, cols], out_specs=[out_spec, out_spec],
        out_shape=[out, out], compiler_params=_params(("parallel",)),
    )(oa, ob, dya, dyb)


def _mix_out_gate_bwd(name, dh, wo, gates, ya, yb):
    t = gates.shape[0]
    tm = _pick(t, (544, 384, 256, 128))

    def body(dh_ref, w_ref, g_ref, ya_ref, yb_ref, dya_ref, dyb_ref, dg_ref):
        dm = lax.dot_general(_bf(dh_ref[...]), w_ref[...], _NT, preferred_element_type=F32)
        sa = jax.nn.sigmoid(g_ref[:, 0:D_MODEL])
        sb = jax.nn.sigmoid(g_ref[:, D_MODEL:2 * D_MODEL])
        dya_ref[...] = (dm * sa).astype(BF16)
        dyb_ref[...] = (dm * sb).astype(BF16)
        dg_ref[:, 0:D_MODEL] = (dm * ya_ref[...].astype(F32) * (sa * (1.0 - sa))).astype(BF16)
        dg_ref[:, D_MODEL:2 * D_MODEL] = (dm * yb_ref[...].astype(F32) * (sb * (1.0 - sb))).astype(BF16)

    blk = pl.BlockSpec((tm, D_MODEL), lambda i: (i, 0))
    wide = pl.BlockSpec((tm, 2 * D_MODEL), lambda i: (i, 0))
    out = jax.ShapeDtypeStruct((t, D_MODEL), BF16)
    return pl.pallas_call(
        body, name=name, grid=(t // tm,),
        in_specs=[blk, pl.BlockSpec((D_MODEL, D_MODEL), lambda i: (0, 0)), wide, blk, blk], out_specs=[blk, blk, wide],
        out_shape=[out, out, jax.ShapeDtypeStruct((t, 2 * D_MODEL), BF16)], compiler_params=_params(("parallel",)),
    )(dh, wo, gates, ya, yb)


def _loss_head(name, h3, gf, tgt):
    b, l, _ = h3.shape
    nb = l // BLOCK

    def body(h_ref, g_ref, t_ref, dh_ref, loss_ref, dg_ref):
        first = (pl.program_id(0) == 0) & (pl.program_id(1) == 0)
        real = (pl.program_id(1) > 0).astype(F32)
        hv = h_ref[...]
        g = g_ref[...]
        r = lax.rsqrt(jnp.mean(hv * hv, axis=-1, keepdims=True) + EPS)
        xn = hv * r
        err = (xn * g - t_ref[...]) * real
        lpart = 0.5 * jnp.sum(jnp.mean(err * err, axis=-1, keepdims=True), axis=0, keepdims=True)
        dy = err * (1.0 / D_MODEL)
        tv = dy * g
        dot = jnp.mean(tv * hv, axis=-1, keepdims=True)
        dh_ref[...] = r * tv - hv * (r * r * r * dot)
        gpart = jnp.sum(dy * xn, axis=0, keepdims=True)

        @pl.when(first)
        def _():
            loss_ref[...] = jnp.zeros_like(loss_ref)
            dg_ref[...] = jnp.zeros_like(dg_ref)

        loss_ref[...] += jnp.broadcast_to(lpart, loss_ref.shape)
        dg_ref[...] += gpart

    return pl.pallas_call(
        body, name=name, grid=(b, nb),
        in_specs=[pl.BlockSpec((None, BLOCK, D_MODEL), lambda bi, n: (bi, n, 0)),
                  pl.BlockSpec((1, D_MODEL), lambda bi, n: (0, 0)),
                  pl.BlockSpec((None, BLOCK, D_MODEL), lambda bi, n: (bi, jnp.maximum(n - 1, 0), 0))],
        out_specs=[pl.BlockSpec((None, BLOCK, D_MODEL), lambda bi, n: (bi, n, 0)),
                   pl.BlockSpec((8, 128), lambda bi, n: (0, 0)),
                   pl.BlockSpec((1, D_MODEL), lambda bi, n: (0, 0))],
        out_shape=[jax.ShapeDtypeStruct(h3.shape, F32), jax.ShapeDtypeStruct((8, 128), F32),
                   jax.ShapeDtypeStruct((1, D_MODEL), F32)],
        compiler_params=_params(("arbitrary", "arbitrary")),
    )(h3, gf, tgt)


def _fgate_fwd(name, f3, bf_row):
    b, l, _ = f3.shape
    nb = l // BLOCK

    def body(f_ref, b_ref, cc_ref, cr_ref):
        r_i = lax.broadcasted_iota(jnp.int32, (BLOCK, BLOCK), 0)
        c_i = lax.broadcasted_iota(jnp.int32, (BLOCK, BLOCK), 1)
        tri = (r_i >= c_i).astype(F32)
        carry = jnp.zeros((1, 128), F32)
        for blk in range(nb):
            rows = slice(blk * BLOCK, (blk + 1) * BLOCK)
            z = f_ref[rows, :] + b_ref[...]
            lf = jnp.minimum(z, 0.0) - jnp.log(1.0 + jnp.exp(-jnp.abs(z)))
            cb = jnp.dot(tri, lf, preferred_element_type=F32, precision=lax.Precision.HIGHEST) + carry
            carry = cb[BLOCK - 1:BLOCK, :]
            cbt = cb.T
            for hh in range(B_HEADS):
                cc_ref[hh, rows, :] = jnp.sum(jnp.where(c_i == hh, cb, 0.0), axis=1, keepdims=True)
                cr_ref[hh, :, rows] = cbt[hh:hh + 1, :]

    return pl.pallas_call(
        body, name=name, grid=(b,),
        in_specs=[pl.BlockSpec((None, l, 128), lambda bi: (bi, 0, 0)),
                  pl.BlockSpec((1, 128), lambda bi: (0, 0))],
        out_specs=[pl.BlockSpec((None, B_HEADS, l, 1), lambda bi: (bi, 0, 0, 0)),
                   pl.BlockSpec((None, B_HEADS, 1, l), lambda bi: (bi, 0, 0, 0))],
        out_shape=[jax.ShapeDtypeStruct((b, B_HEADS, l, 1), F32), jax.ShapeDtypeStruct((b, B_HEADS, 1, l), F32)],
        compiler_params=_params(("parallel",)),
    )(f3, bf_row)


def _fgate_bwd(name, f3, bf_row, dcq, dck):
    b, l, _ = f3.shape
    nb = l // BLOCK

    def body(f_ref, b_ref, dcq_ref, dck_ref, df_ref, db_ref):
        r_i = lax.broadcasted_iota(jnp.int32, (BLOCK, BLOCK), 0)
        c_i = lax.broadcasted_iota(jnp.int32, (BLOCK, BLOCK), 1)
        tri = (r_i <= c_i).astype(F32)
        carry = jnp.zeros((1, 128), F32)
        total = jnp.zeros((1, 128), F32)
        for blk in range(nb - 1, -1, -1):
            rows = slice(blk * BLOCK, (blk + 1) * BLOCK)
            krows = jnp.concatenate([dck_ref[hh, :, rows] for hh in range(B_HEADS)]
                                    + [jnp.zeros((BLOCK - B_HEADS, BLOCK), F32)], axis=0)
            dcb = krows.T
            for hh in range(B_HEADS):
                dcb = dcb + jnp.where(c_i == hh, dcq_ref[hh, rows, :], 0.0)
            rc = jnp.dot(tri, dcb, preferred_element_type=F32, precision=lax.Precision.HIGHEST) + carry
            carry = rc[0:1, :]
            z = f_ref[rows, :] + b_ref[...]
            df = rc * (1.0 / (1.0 + jnp.exp(z)))
            df_ref[rows, :] = df.astype(BF16)
            total = total + jnp.sum(df, axis=0, keepdims=True)

        @pl.when(pl.program_id(0) == 0)
        def _():
            db_ref[...] = total

        @pl.when(pl.program_id(0) > 0)
        def _():
            db_ref[...] += total

    return pl.pallas_call(
        body, name=name, grid=(b,),
        in_specs=[pl.BlockSpec((None, l, 128), lambda bi: (bi, 0, 0)),
                  pl.BlockSpec((1, 128), lambda bi: (0, 0)),
                  pl.BlockSpec((None, B_HEADS, l, 1), lambda bi: (bi, 0, 0, 0)),
                  pl.BlockSpec((None, B_HEADS, 1, l), lambda bi: (bi, 0, 0, 0))],
        out_specs=[pl.BlockSpec((None, l, 128), lambda bi: (bi, 0, 0)), pl.BlockSpec((1, 128), lambda bi: (0, 0))],
        out_shape=[jax.ShapeDtypeStruct((b, l, 128), BF16), jax.ShapeDtypeStruct((1, 128), F32)],
        compiler_params=_params(("arbitrary",)),
    )(f3, bf_row, dcq, dck)


A_Q_BLK = B_SEG // A_WIDTH
A_K_BLK = (B_SEG + A_WIDTH) // 128
A_V_BLK = A_K_BLK + 1
A_SEG_BLK = B_SEG // A_SEG
STACK = A_HEADS * BLOCK


def _lane_lo():
    return lax.broadcasted_iota(jnp.int32, (1, 128), 1) < HEAD_DIM


def _stack_heads(x, masked):
    lo = _lane_lo()
    blks = [x[:, 128 * j:128 * (j + 1)] for j in range(4)]
    if not masked:
        return jnp.concatenate(blks + blks, axis=0)
    zero = jnp.zeros_like(blks[0])
    return jnp.concatenate([jnp.where(lo, bk, zero) for bk in blks] + [jnp.where(lo, zero, bk) for bk in blks], axis=0)


def _unstack_heads(y):
    lo = _lane_lo()
    return jnp.concatenate([jnp.where(lo, y[128 * j:128 * (j + 1)], y[128 * (4 + j):128 * (5 + j)]) for j in range(4)], axis=1)


def _swa_bias(slopes):
    r_i = jnp.arange(STACK)[:, None]
    c_i = jnp.arange(3 * BLOCK)[None, :]
    seg = c_i >> 7
    out = []
    for n in range(3):
        qpos = n * BLOCK + (r_i & (BLOCK - 1))
        kpos = jnp.where(seg == 0, c_i, (n - 2) * BLOCK + c_i)
        dist = qpos - kpos
        band = (seg != 0) & (dist < BLOCK) & (kpos >= PREFIX)
        meta = (seg == 0) & (c_i >= N_PAD)
        out.append(jnp.where((dist >= 0) & (band | meta), -slopes * dist.astype(F32), NEG))
    return jnp.stack(out, axis=0)


def _swa_scores(q, kcat, n, slope, bias):
    s = lax.dot_general(q, kcat, _NT, preferred_element_type=F32) + bias
    further = slope * (-BLOCK * jnp.maximum(n - 2, 0)).astype(F32)
    return jnp.concatenate([s[:, 0:BLOCK] + further, s[:, BLOCK:]], axis=1)


def _swa_specs():
    def kv(col_blk):
        return [pl.BlockSpec((None, BLOCK, 128), lambda b, n: (b, 0, col_blk)),
                pl.BlockSpec((None, BLOCK, 128), lambda b, n: (b, jnp.maximum(n - 1, 0), col_blk)),
                pl.BlockSpec((None, BLOCK, 128), lambda b, n: (b, n, col_blk))]

    q_spec = pl.BlockSpec((None, BLOCK, A_WIDTH), lambda b, n: (b, n, A_Q_BLK))
    o_spec = pl.BlockSpec((None, BLOCK, A_WIDTH), lambda b, n: (b, n, 0))
    col = pl.BlockSpec((STACK, 1), lambda b, n: (0, 0))
    bias = pl.BlockSpec((None, STACK, 3 * BLOCK), lambda b, n: (jnp.minimum(n, 2), 0, 0))
    lse_spec = pl.BlockSpec((None, A_HEADS, BLOCK, 1), lambda b, n: (b, 0, n, 0))
    return q_spec, kv(A_K_BLK), kv(A_V_BLK), o_spec, [col, col, bias], lse_spec


def _swa_fwd(name, qkv, slopes, sinks, bias):
    b, l, _ = qkv.shape
    nb = l // BLOCK

    def body(q_ref, k0_ref, kp_ref, kc_ref, v0_ref, vp_ref, vc_ref, sl_ref, sk_ref, bias_ref, o_ref, lse_ref):
        n = pl.program_id(1)
        qs = _stack_heads(q_ref[...], True) * SCALE
        kcat = jnp.concatenate([k0_ref[...], kp_ref[...], kc_ref[...]], axis=0)
        vcat = jnp.concatenate([v0_ref[...], vp_ref[...], vc_ref[...]], axis=0)
        s = _swa_scores(qs, kcat, n, sl_ref[...], bias_ref[...])
        sink = sk_ref[...]
        m = jnp.maximum(jnp.max(s, axis=-1, keepdims=True), sink)
        p = jnp.exp(s - m)
        den = jnp.sum(p, axis=-1, keepdims=True) + jnp.exp(sink - m)
        o = lax.dot_general(p.astype(BF16), vcat, _NN, preferred_element_type=F32) / den
        o_ref[...] = _unstack_heads(o)
        lse_ref[...] = (m + jnp.log(den)).reshape(A_HEADS, BLOCK, 1)

    q_spec, k_specs, v_specs, o_spec, consts, lse_spec = _swa_specs()
    return pl.pallas_call(
        body, name=name, grid=(b, nb),
        in_specs=[q_spec] + k_specs + v_specs + consts, out_specs=[o_spec, lse_spec],
        out_shape=[jax.ShapeDtypeStruct((b, l, A_WIDTH), F32), jax.ShapeDtypeStruct((b, A_HEADS, l, 1), F32)],
        compiler_params=_params(("parallel", "parallel")),
    )(qkv, qkv, qkv, qkv, qkv, qkv, qkv, slopes, sinks, bias)


def _swa_bwd(name, qkv, o, lse, do, slopes, sinks, bias, dqkv):
    b, l, _ = qkv.shape
    nb = l // BLOCK

    def body(q_ref, k0_ref, kp_ref, kc_ref, v0_ref, vp_ref, vc_ref, o_ref, lse_ref, do_ref, sl_ref, sk_ref, bias_ref, _,
             dx_ref, ds_ref, dk_acc, dv_acc):
        bi = pl.program_id(0)
        n = pl.program_id(1)
        qs = _stack_heads(q_ref[...], True) * SCALE
        dos32 = _stack_heads(do_ref[...], True)
        dos = dos32.astype(BF16)
        os_ = _stack_heads(o_ref[...], False)
        lsev = lse_ref[...].reshape(STACK, 1)
        kcat = jnp.concatenate([k0_ref[...], kp_ref[...], kc_ref[...]], axis=0)
        vcat = jnp.concatenate([v0_ref[...], vp_ref[...], vc_ref[...]], axis=0)
        s = _swa_scores(qs, kcat, n, sl_ref[...], bias_ref[...])
        p = jnp.exp(s - lsev)
        dsum = jnp.sum(dos32 * os_, axis=-1, keepdims=True)
        dp = lax.dot_general(dos, vcat, _NT, preferred_element_type=F32)
        dsc = (p * (dp - dsum)).astype(BF16)
        dq = lax.dot_general(dsc, kcat, _NN, preferred_element_type=F32) * SCALE
        row0 = pl.multiple_of(n * BLOCK, BLOCK)
        dx_ref[pl.ds(row0, BLOCK), 0:A_WIDTH] = _unstack_heads(dq).astype(BF16)
        dkc = lax.dot_general(dsc, qs, _TN, preferred_element_type=F32)
        dvc = lax.dot_general(p.astype(BF16), dos, _TN, preferred_element_type=F32)

        @pl.when(n == 0)
        def _():
            dk_acc[...] = jnp.zeros_like(dk_acc)
            dv_acc[...] = jnp.zeros_like(dv_acc)

        starts = (0, pl.multiple_of(jnp.maximum(n - 1, 0) * BLOCK, BLOCK), row0)
        for t, st in enumerate(starts):
            dk_acc[pl.ds(st, BLOCK), :] += dkc[t * BLOCK:(t + 1) * BLOCK, :]
            dv_acc[pl.ds(st, BLOCK), :] += dvc[t * BLOCK:(t + 1) * BLOCK, :]

        @pl.when(n == nb - 1)
        def _():
            dx_ref[:, A_WIDTH:A_WIDTH + 128] = dk_acc[...].astype(BF16)
            dx_ref[:, A_WIDTH + 128:A_SEG] = dv_acc[...].astype(BF16)

        dsink = -(jnp.exp(sk_ref[...] - lsev) * dsum)
        r8 = lax.broadcasted_iota(jnp.int32, (8, 128), 0)
        acc = jnp.zeros((8, 128), F32)
        for hh in range(A_HEADS):
            acc = acc + jnp.where(r8 == hh, jnp.sum(dsink[hh * BLOCK:(hh + 1) * BLOCK, :]), 0.0)

        @pl.when((bi == 0) & (n == 0))
        def _():
            ds_ref[...] = jnp.zeros_like(ds_ref)

        ds_ref[...] += acc

    q_spec, k_specs, v_specs, o_spec, consts, lse_spec = _swa_specs()
    return pl.pallas_call(
        body, name=name, grid=(b, nb),
        in_specs=[q_spec] + k_specs + v_specs + [o_spec, lse_spec, o_spec] + consts + [pl.BlockSpec(memory_space=pl.ANY)],
        out_specs=[pl.BlockSpec((None, l, A_SEG), lambda bb, n: (bb, 0, A_SEG_BLK)),
                   pl.BlockSpec((8, 128), lambda bb, n: (0, 0))],
        out_shape=[jax.ShapeDtypeStruct(dqkv.shape, BF16), jax.ShapeDtypeStruct((8, 128), F32)],
        scratch_shapes=[pltpu.VMEM((l, 128), F32), pltpu.VMEM((l, 128), F32)],
        input_output_aliases={13: 0},
        compiler_params=_params(("arbitrary", "arbitrary")),
    )(qkv, qkv, qkv, qkv, qkv, qkv, qkv, o, lse, do, slopes, sinks, bias, dqkv)


def _fox_mask(qk, ck, i):
    kh = qk.shape[1]
    qpos = i * BLOCK + lax.broadcasted_iota(jnp.int32, (BLOCK, kh), 0)
    kpos = lax.broadcasted_iota(jnp.int32, (BLOCK, kh), 1)
    return jnp.where((kpos <= qpos) & (kpos >= N_PAD), qk - ck, NEG)


def _pick_head(x, hh):
    lo = _lane_lo()
    return jnp.where(lo if hh == 0 else jnp.logical_not(lo), x, jnp.zeros_like(x))


def _both_heads(x):
    return jnp.concatenate([_pick_head(x, 0), _pick_head(x, 1)], axis=0)


def _fox_specs(l):
    pair = pl.BlockSpec((None, l, PAIR_W), lambda bi, hp: (bi, 0, hp))
    half = pl.BlockSpec((None, l, 128), lambda bi, hp: (bi, 0, hp))
    colv = pl.BlockSpec((None, 2, l, 1), lambda bi, hp: (bi, hp, 0, 0))
    rowv = pl.BlockSpec((None, 2, 1, l), lambda bi, hp: (bi, hp, 0, 0))
    return pair, half, colv, rowv


def _fox_fwd(name, qkv, c_col, c_row):
    b, l, _ = qkv.shape
    nb = l // BLOCK

    def body(x_ref, cc_ref, cr_ref, o_ref, lse_ref):
        for i in range(nb):
            rows = slice(i * BLOCK, (i + 1) * BLOCK)
            kh = (i + 1) * BLOCK
            qblk = x_ref[rows, 0:128]
            kv = x_ref[0:kh, 128:256]
            vv = x_ref[0:kh, 256:384]
            qk = lax.dot_general(_both_heads(qblk) * SCALE, kv, _NT, preferred_element_type=F32)
            ps, dens = [], []
            for hh in range(2):
                s = _fox_mask(qk[hh * BLOCK:(hh + 1) * BLOCK], cr_ref[hh, :, 0:kh], i)
                m = jnp.max(s, axis=-1, keepdims=True)
                p = jnp.exp(s - m)
                den = jnp.sum(p, axis=-1, keepdims=True)
                ps.append(p.astype(BF16))
                dens.append(den)
                lse_ref[hh, rows, :] = (m + jnp.log(den)) + cc_ref[hh, rows, :]
            pv = lax.dot_general(jnp.concatenate(ps, axis=0), vv, _NN, preferred_element_type=F32)
            o_ref[rows, :] = jnp.where(_lane_lo(), pv[0:BLOCK] / dens[0], pv[BLOCK:2 * BLOCK] / dens[1]).astype(BF16)

    pair, half, colv, rowv = _fox_specs(l)
    return pl.pallas_call(
        body, name=name, grid=(b, 4), in_specs=[pair, colv, rowv], out_specs=[half, colv],
        out_shape=[jax.ShapeDtypeStruct((b, l, B_WIDTH), BF16), jax.ShapeDtypeStruct((b, B_HEADS, l, 1), F32)],
        compiler_params=_params(("parallel", "parallel")),
    )(qkv, c_col, c_row)


def _fox_bwd(name, qkv, c_col, c_row, o, lse, do):
    b, l, _ = qkv.shape
    nb = l // BLOCK

    def body(x_ref, cc_ref, cr_ref, o_ref, lse_ref, do_ref, dx_ref, dcq_ref, dck_ref, dk_acc, dv_acc):
        dk_acc[...] = jnp.zeros_like(dk_acc)
        dv_acc[...] = jnp.zeros_like(dv_acc)
        dck_ref[...] = jnp.zeros_like(dck_ref)
        for i in range(nb):
            rows = slice(i * BLOCK, (i + 1) * BLOCK)
            kh = (i + 1) * BLOCK
            qblk = x_ref[rows, 0:128]
            kv = x_ref[0:kh, 128:256]
            vv = x_ref[0:kh, 256:384]
            doblk = do_ref[rows, :]
            ov = o_ref[rows, :].astype(F32)
            q2 = _both_heads(qblk) * SCALE
            do2 = _both_heads(doblk)
            qk = lax.dot_general(q2, kv, _NT, preferred_element_type=F32)
            dp = lax.dot_general(do2, vv, _NT, preferred_element_type=F32)
            ps, dss = [], []
            for hh in range(2):
                half = slice(hh * BLOCK, (hh + 1) * BLOCK)
                s = _fox_mask(qk[half], cr_ref[hh, :, 0:kh], i)
                p = jnp.exp(s - (lse_ref[hh, rows, :] - cc_ref[hh, rows, :]))
                dsum = jnp.sum(do2[half].astype(F32) * ov, axis=-1, keepdims=True)
                ds = p * (dp[half] - dsum)
                ps.append(p.astype(BF16))
                dss.append(ds.astype(BF16))
                dcq_ref[hh, rows, :] = jnp.sum(ds, axis=-1, keepdims=True)
                dck_ref[hh, :, 0:kh] -= jnp.sum(ds, axis=0, keepdims=True)
            p2 = jnp.concatenate(ps, axis=0)
            ds2 = jnp.concatenate(dss, axis=0)
            dq = lax.dot_general(ds2, kv, _NN, preferred_element_type=F32) * SCALE
            dk_acc[0:kh, :] += lax.dot_general(ds2, q2, _TN, preferred_element_type=F32)
            dv_acc[0:kh, :] += lax.dot_general(p2, do2, _TN, preferred_element_type=F32)
            dx_ref[rows, 0:128] = jnp.where(_lane_lo(), dq[0:BLOCK], dq[BLOCK:2 * BLOCK]).astype(BF16)
        dx_ref[:, 128:256] = dk_acc[...].astype(BF16)
        dx_ref[:, 256:384] = dv_acc[...].astype(BF16)

    pair, half, colv, rowv = _fox_specs(l)
    return pl.pallas_call(
        body, name=name, grid=(b, 4), in_specs=[pair, colv, rowv, half, colv, half],
        out_specs=[pair, colv, rowv],
        out_shape=[jax.ShapeDtypeStruct(qkv.shape, BF16), jax.ShapeDtypeStruct((b, B_HEADS, l, 1), F32),
                   jax.ShapeDtypeStruct((b, B_HEADS, 1, l), F32)],
        scratch_shapes=[pltpu.VMEM((l, 128), F32), pltpu.VMEM((l, 128), F32)],
        compiler_params=_params(("parallel", "parallel")),
    )(qkv, c_col, c_row, o, lse, do)


_FLIPS = ((0, 0, 1), (0, 1, 0), (0, 1, 1), (1, 0, 0), (1, 0, 1), (1, 1, 0), (1, 1, 1))


def _exchange(name, gather, scatter):
    ng, ns = len(gather), len(scatter)
    na = ng + ns
    npeer = len(_FLIPS)

    def body(*refs):
        ins = refs[:na]
        outs = refs[na:2 * na]
        send_sems, recv_sems, loc_sems = refs[2 * na:]
        x, y, c = lax.axis_index("x"), lax.axis_index("y"), lax.axis_index("c")
        me = 4 * x + 2 * y + c
        peers = []
        for fx, fy, fc in _FLIPS:
            px = 1 - x if fx else x
            py = 1 - y if fy else y
            pc = 1 - c if fc else c
            peers.append(((px, py, pc), 4 * px + 2 * py + pc))

        def remote(a, kk):
            dev, lin = peers[kk]
            src = ins[a] if a < ng else ins[a].at[lin]
            return pltpu.make_async_remote_copy(src_ref=src, dst_ref=outs[a].at[me], send_sem=send_sems.at[a * npeer + kk],
                                                recv_sem=recv_sems.at[a * npeer + kk], device_id=dev, device_id_type=MESH_ID)

        def arrival(a, kk):
            dev, lin = peers[kk]
            src = ins[a] if a < ng else ins[a].at[lin]
            return pltpu.make_async_remote_copy(src_ref=src, dst_ref=outs[a].at[lin], send_sem=send_sems.at[a * npeer + kk],
                                                recv_sem=recv_sems.at[a * npeer + kk], device_id=dev, device_id_type=MESH_ID)

        local = []
        for a in range(na):
            src = ins[a] if a < ng else ins[a].at[me]
            cp = pltpu.make_async_copy(src, outs[a].at[me], loc_sems.at[a])
            cp.start()
            local.append(cp)
        sent = [remote(a, kk) for kk in range(npeer) for a in range(na)]
        for cp in sent:
            cp.start()
        for kk in range(npeer):
            for a in range(na):
                arrival(a, kk).wait_recv()
        for cp in sent:
            cp.wait_send()
        for cp in local:
            cp.wait()

    arrs = list(gather) + list(scatter)
    out_shape = [jax.ShapeDtypeStruct((N_DEV,) + tuple(a.shape), a.dtype) for a in gather]
    out_shape += [jax.ShapeDtypeStruct(tuple(a.shape), a.dtype) for a in scatter]
    anyspec = pl.BlockSpec(memory_space=pl.ANY)
    return pl.pallas_call(
        body, name=name, in_specs=[anyspec] * na, out_specs=[anyspec] * na, out_shape=out_shape,
        scratch_shapes=[pltpu.SemaphoreType.DMA((na * npeer,)), pltpu.SemaphoreType.DMA((na * npeer,)),
                        pltpu.SemaphoreType.DMA((na,))],
        compiler_params=pltpu.CompilerParams(has_side_effects=True),
    )(*arrs)


def _peer_table():
    x, y, c = lax.axis_index("x"), lax.axis_index("y"), lax.axis_index("c")
    me = 4 * x + 2 * y + c
    peers = []
    for fx, fy, fc in _FLIPS:
        px = 1 - x if fx else x
        py = 1 - y if fy else y
        pc = 1 - c if fc else c
        peers.append(((px, py, pc), 4 * px + 2 * py + pc))
    return me, peers


_HBM = pl.BlockSpec(memory_space=pltpu.HBM)
_SEM = pl.BlockSpec(memory_space=pltpu.SEMAPHORE)
_ANY = pl.BlockSpec(memory_space=pl.ANY)
_EFFECT = pltpu.SideEffectType.DATAFLOW_SIDE_EFFECTING


def _split_copy(srcs_are_pieces, src_refs, land_refs, send_sem, recv_sem, a, kk, me, peers, arriving):
    dev, lin = peers[kk]
    npeer = len(_FLIPS)
    src = src_refs[a] if srcs_are_pieces[a] else src_refs[a].at[lin]
    dst = land_refs[a].at[lin] if arriving else land_refs[a].at[me]
    return pltpu.make_async_remote_copy(src_ref=src, dst_ref=dst, send_sem=send_sem.at[a * npeer + kk],
                                        recv_sem=recv_sem.at[a * npeer + kk], device_id=dev, device_id_type=MESH_ID)


def _xchg_start(name, gather, scatter, after=None):
    me_out = 4 * lax.axis_index("x") + 2 * lax.axis_index("y") + lax.axis_index("c")
    srcs = list(gather) + list(scatter)
    is_piece = [True] * len(gather) + [False] * len(scatter)
    lands = []
    for a, piece in zip(srcs, is_piece):
        own = a[None] if piece else lax.dynamic_slice_in_dim(a, me_out, 1, axis=0)
        shape = ((N_DEV,) + tuple(a.shape)) if piece else tuple(a.shape)
        start = (me_out,) + (0,) * (len(shape) - 1)
        lands.append(lax.dynamic_update_slice(lax.empty(shape, a.dtype), own, start))
    n = len(srcs)
    nsem = n * len(_FLIPS)
    has_after = after is not None

    def body(*refs):
        src_refs = refs[:n]
        land_refs = refs[n:2 * n]
        outs = refs[2 * n + (1 if has_after else 0):]
        send_sem, recv_sem = outs[0], outs[1]
        token = outs[-1]
        me, peers = _peer_table()
        for kk in range(len(_FLIPS)):
            for a in range(n):
                _split_copy(is_piece, src_refs, land_refs, send_sem, recv_sem, a, kk, me, peers, False).start()
        token[...] = jnp.zeros_like(token)

    out_shape = ([pltpu.SemaphoreType.DMA((nsem,)), pltpu.SemaphoreType.DMA((nsem,))]
                 + [pltpu.HBM(tuple(a.shape), a.dtype) for a in srcs] + [pltpu.HBM(tuple(a.shape), a.dtype) for a in lands]
                 + [jax.ShapeDtypeStruct((8, 128), F32)])
    args = [pltpu.with_memory_space_constraint(a, pltpu.HBM) for a in srcs + lands] + ([after] if has_after else [])
    res = pl.pallas_call(
        body, name=name, out_shape=out_shape,
        in_specs=[_HBM] * (2 * n) + ([_ANY] if has_after else []),
        out_specs=[_SEM, _SEM] + [_HBM] * (2 * n) + [pl.BlockSpec(memory_space=pltpu.VMEM)],
        input_output_aliases={i: 2 + i for i in range(2 * n)},
        compiler_params=pltpu.CompilerParams(has_side_effects=_EFFECT),
    )(*args)
    state = (res[0], res[1], list(res[2:2 + n]), list(res[2 + n:2 + 2 * n]), is_piece)
    return state, res[-1]


def _xchg_wait(name, state, after):
    send_sem, recv_sem, srcs, lands, is_piece = state
    n = len(srcs)

    def body(*refs):
        src_refs = refs[:n]
        land_refs = refs[n:2 * n]
        s_sem, r_sem = refs[2 * n], refs[2 * n + 1]
        me, peers = _peer_table()
        for kk in range(len(_FLIPS)):
            for a in range(n):
                cp = _split_copy(is_piece, src_refs, land_refs, s_sem, r_sem, a, kk, me, peers, True)
                cp.wait_send()
                cp.wait_recv()

    out_shape = [pltpu.HBM(tuple(a.shape), a.dtype) for a in srcs] + [pltpu.HBM(tuple(a.shape), a.dtype) for a in lands]
    res = pl.pallas_call(
        body, name=name, out_shape=out_shape,
        in_specs=[_HBM] * (2 * n) + [_SEM, _SEM, _ANY], out_specs=[_HBM] * (2 * n),
        input_output_aliases={i: i for i in range(2 * n)},
        compiler_params=pltpu.CompilerParams(has_side_effects=_EFFECT),
    )(*srcs, *lands, send_sem, recv_sem, after)
    return list(res[n:])


_SIB = (0, 0, 1)
_ICI = ((0, 1, 0), (1, 0, 0), (1, 1, 0))


def _flip(fl):
    x, y, c = lax.axis_index("x"), lax.axis_index("y"), lax.axis_index("c")
    px = 1 - x if fl[0] else x
    py = 1 - y if fl[1] else y
    pc = 1 - c if fl[2] else c
    return (px, py, pc), 4 * px + 2 * py + pc


def _gather2_start(name, pieces, after=None):
    me_out = 4 * lax.axis_index("x") + 2 * lax.axis_index("y") + lax.axis_index("c")
    pieces = list(pieces)
    n = len(pieces)
    lands = [lax.dynamic_update_slice(lax.empty((N_DEV,) + tuple(a.shape), a.dtype), a[None],
                                      (me_out,) + (0,) * a.ndim) for a in pieces]
    first = (_SIB,) + _ICI
    has_after = after is not None

    def body(*refs):
        src_refs, land_refs = refs[:n], refs[n:2 * n]
        outs = refs[2 * n + (1 if has_after else 0):]
        send_sem, recv_sem, token = outs[0], outs[1], outs[-1]
        _, me = _flip((0, 0, 0))
        for kk, fl in enumerate(first):
            dev, _ = _flip(fl)
            for a in range(n):
                pltpu.make_async_remote_copy(src_ref=src_refs[a], dst_ref=land_refs[a].at[me],
                                             send_sem=send_sem.at[a * 4 + kk], recv_sem=recv_sem.at[a * 4 + kk],
                                             device_id=dev, device_id_type=MESH_ID).start()
        token[...] = jnp.zeros_like(token)

    hbm = [pltpu.HBM(tuple(a.shape), a.dtype) for a in pieces + lands]
    res = pl.pallas_call(
        body, name=name,
        out_shape=[pltpu.SemaphoreType.DMA((4 * n,)), pltpu.SemaphoreType.DMA((4 * n,))] + hbm
        + [jax.ShapeDtypeStruct((8, 128), F32)],
        in_specs=[_HBM] * (2 * n) + ([_ANY] if has_after else []),
        out_specs=[_SEM, _SEM] + [_HBM] * (2 * n) + [pl.BlockSpec(memory_space=pltpu.VMEM)],
        input_output_aliases={i: 2 + i for i in range(2 * n)},
        compiler_params=pltpu.CompilerParams(has_side_effects=_EFFECT),
    )(*([pltpu.with_memory_space_constraint(a, pltpu.HBM) for a in pieces + lands] + ([after] if has_after else [])))
    return (res[0], res[1], list(res[2:2 + n]), list(res[2 + n:2 + 2 * n])), res[-1]


def _gather2_forward(name, state, after):
    send_a, recv_a, pieces, lands = state
    n = len(pieces)
    first = (_SIB,) + _ICI

    def body(*refs):
        src_refs, land_refs = refs[:n], refs[n:2 * n]
        s_a, r_a = refs[2 * n], refs[2 * n + 1]
        outs = refs[2 * n + 3:]
        send_b, recv_b, token = outs[0], outs[1], outs[-1]
        for kk, fl in enumerate(first):
            dev, lin = _flip(fl)
            for a in range(n):
                cp = pltpu.make_async_remote_copy(src_ref=src_refs[a], dst_ref=land_refs[a].at[lin],
                                                  send_sem=s_a.at[a * 4 + kk], recv_sem=r_a.at[a * 4 + kk],
                                                  device_id=dev, device_id_type=MESH_ID)
                cp.wait_send()
                cp.wait_recv()
        sib, _ = _flip(_SIB)
        for j, fl in enumerate(_ICI):
            _, lin = _flip(fl)
            for a in range(n):
                pltpu.make_async_remote_copy(src_ref=land_refs[a].at[lin], dst_ref=land_refs[a].at[lin],
                                             send_sem=send_b.at[a * 3 + j], recv_sem=recv_b.at[a * 3 + j],
                                             device_id=sib, device_id_type=MESH_ID).start()
        token[...] = jnp.zeros_like(token)

    hbm = [pltpu.HBM(tuple(a.shape), a.dtype) for a in pieces + lands]
    res = pl.pallas_call(
        body, name=name,
        out_shape=[pltpu.SemaphoreType.DMA((3 * n,)), pltpu.SemaphoreType.DMA((3 * n,))] + hbm
        + [jax.ShapeDtypeStruct((8, 128), F32)],
        in_specs=[_HBM] * (2 * n) + [_SEM, _SEM, _ANY],
        out_specs=[_SEM, _SEM] + [_HBM] * (2 * n) + [pl.BlockSpec(memory_space=pltpu.VMEM)],
        input_output_aliases={i: 2 + i for i in range(2 * n)},
        compiler_params=pltpu.CompilerParams(has_side_effects=_EFFECT),
    )(*pieces, *lands, send_a, recv_a, after)
    return (res[0], res[1], list(res[2 + n:2 + 2 * n])), res[-1]


def _gather2_wait(name, state, after):
    send_b, recv_b, lands = state
    n = len(lands)

    def body(*refs):
        land_refs = refs[:n]
        s_b, r_b = refs[n], refs[n + 1]
        sib, _ = _flip(_SIB)
        for j, fl in enumerate(_ICI):
            _, sent = _flip(fl)
            _, arriving = _flip((fl[0], fl[1], 1))
            for a in range(n):
                cp = pltpu.make_async_remote_copy(src_ref=land_refs[a].at[sent], dst_ref=land_refs[a].at[arriving],
                                                  send_sem=s_b.at[a * 3 + j], recv_sem=r_b.at[a * 3 + j],
                                                  device_id=sib, device_id_type=MESH_ID)
                cp.wait_send()
                cp.wait_recv()

    res = pl.pallas_call(
        body, name=name, out_shape=[pltpu.HBM(tuple(a.shape), a.dtype) for a in lands],
        in_specs=[_HBM] * n + [_SEM, _SEM, _ANY], out_specs=[_HBM] * n,
        input_output_aliases={i: i for i in range(n)},
        compiler_params=pltpu.CompilerParams(has_side_effects=_EFFECT),
    )(*lands, send_b, recv_b, after)
    return list(res)


def _adam_math(w, g, m, v):
    m = ADAM_B1 * m + (1.0 - ADAM_B1) * g
    v = ADAM_B2 * v + (1.0 - ADAM_B2) * (g * g)
    m_hat = m / (1.0 - ADAM_B1 ** ADAM_STEP)
    v_hat = v / (1.0 - ADAM_B2 ** ADAM_STEP)
    delta = -ADAM_LR * (m_hat / (jnp.sqrt(v_hat) + ADAM_EPS) + ADAM_WD * w)
    return delta, m, v


def _adam(name, w, m, v, parts, transposed=False, dep=None):
    npart, _, cp = parts.shape
    has_dep = dep is not None
    if transposed:
        c, r = w.shape
        tr = _pick(r, (256, 128))
        blk = pl.BlockSpec((c, tr), lambda i: (0, i))
    else:
        r, c = w.shape
        tr = _pick(r, (256, 176, 128, 64, 16, 8, 1))
        blk = pl.BlockSpec((tr, c), lambda i: (i, 0))

    def body(w_ref, m_ref, v_ref, p_ref, *rest):
        g_ref, d_ref, mo_ref, vo_ref = rest[-4:]
        g = p_ref[0].astype(F32)
        for pp in range(1, npart):
            g = g + p_ref[pp].astype(F32)
        g = g.T[0:c, :] if transposed else g[:, 0:c]
        delta, mn, vn = _adam_math(w_ref[...], g, m_ref[...], v_ref[...])
        g_ref[...] = g
        d_ref[...] = delta
        mo_ref[...] = mn
        vo_ref[...] = vn

    out = jax.ShapeDtypeStruct(w.shape, F32)
    return pl.pallas_call(
        body, name=name, grid=(r // tr,),
        in_specs=[blk, blk, blk, pl.BlockSpec((npart, tr, cp), lambda i: (0, i, 0))]
        + ([pl.BlockSpec(memory_space=pl.ANY)] if has_dep else []),
        out_specs=[blk, blk, blk, blk], out_shape=[out, out, out, out], compiler_params=_params(("parallel",)),
    )(*((w, m, v, parts) + ((dep,) if has_dep else ())))


def _small_sum(name, packs):
    def body(p_ref, o_ref):
        tot = p_ref[0]
        for pp in range(1, N_DEV):
            tot = tot + p_ref[pp]
        o_ref[0:8, :] = tot[0:8, :]
        o_ref[8:24, :] = tot[8:24, :] + tot[24:40, :]

    return pl.pallas_call(body, name=name, out_shape=jax.ShapeDtypeStruct((24, D_MODEL), F32),
                          compiler_params=_params())(packs)


def _local_step(x, tgt, g1, gm, g2, gf, b_forget, sinks, weights, send):
    b, s, _ = x.shape
    l = s + PREFIX
    t = b * l
    (meta,) = weights("meta", x)
    h0 = jnp.concatenate([jnp.zeros((b, N_PAD, D_MODEL), F32), jnp.broadcast_to(meta[None], (b, N_META, D_MODEL)), x],
                         axis=1).reshape(t, D_MODEL)

    n1 = _rms_fwd("rms1_fwd", h0, g1)
    (w1i,) = weights("ffn1_in", n1)
    gu1, a1 = _ffn_in_fwd("ffn1_in_fwd", n1, w1i)
    (w1o,) = weights("ffn1_out", weights("ffn1_out:forward", a1))
    h1, um = _mm_res_norm("ffn1_out_fwd", a1, w1o, h0, gm, alpha=0.5)
    wi, wa, wb, wo = weights("mix", weights("mix:forward", um))
    qkv, gates, f2 = _proj_fwd("proj_fwd", um, wi)
    qkv3 = qkv.reshape(b, l, QKV_W)
    f3 = f2.reshape(b, l, 128)
    bf_row = jnp.pad(b_forget, ((0, 0), (0, 128 - B_HEADS)))
    c_col, c_row = _fgate_fwd("fgate_fwd", f3, bf_row)
    head_of_row = jnp.arange(STACK) // BLOCK
    slopes = jnp.exp2(-8.0 * (head_of_row + 1).astype(F32) / A_HEADS).reshape(STACK, 1)
    sink_rows = jnp.repeat(sinks.reshape(A_HEADS), BLOCK).reshape(STACK, 1)
    swa_bias = _swa_bias(slopes)
    oa3, lse_a = _swa_fwd("swa_fwd", qkv3, slopes, sink_rows, swa_bias)
    ob3, lse_b = _fox_fwd("fox_fwd", qkv3, c_col, c_row)
    oa = oa3.reshape(t, A_WIDTH)
    ob = ob3.reshape(t, B_WIDTH)
    mixed, ya, yb = _branch_gate_fwd("branch_gate_fwd", oa, ob, wa, wb, gates, dep=weights("ffn2:forward", ob))
    h2, n2 = _mm_res_norm("mix_out_fwd", mixed, wo, h1, g2)
    w2i, w2o = weights("ffn2", h2)
    gu2, a2 = _ffn_in_fwd("ffn2_in_fwd", n2, w2i)
    h3 = _mm_nn("ffn2_out_fwd", a2, w2o, alpha=0.5, res=h2, tm_c=(544, 384, 256, 128), tn_c=(1024,))

    dh3_3, loss_blk, dgf = _loss_head("loss_head", h3.reshape(b, l, D_MODEL), gf, tgt)
    dh3 = dh3_3.reshape(t, D_MODEL)

    def ffn_bwd(tag, dh, h_in, g_norm, n_in, gu, a, w_in_blk, w_out, one_send):
        dw_out = _mm_tn(tag + "_out_bwd_w", a, dh, alpha=0.5)
        dgu = _ffn_out_bwd_x(tag + "_out_bwd_x", dh, w_out, gu, dep=None if one_send else send(tag + "_out", (dw_out,)))
        dw_in = _ffn_in_bwd_w(tag + "_in_bwd_w", n_in, dgu)
        token = send(tag, (dw_in, dw_out)) if one_send else send(tag + "_in", (dw_in,))
        return _ffn_in_bwd_x(tag + "_in_bwd_x", dgu, w_in_blk, h_in, g_norm, dh, dep=token)

    dh2, dg2 = ffn_bwd("ffn2", dh3, h2, g2, n2, gu2, a2, w2i, w2o, True)

    dwo = _mm_tn("mix_out_bwd_w", mixed, dh2)
    dya, dyb, dgates = _mix_out_gate_bwd("mix_out_gate_bwd", dh2, wo, gates, ya, yb)
    doa, dob = _branch_bwd_x("branch_bwd_x", dya, dyb, wa, wb)
    dwa, dwb = _branch_bwd_w("branch_bwd_w", oa, ob, dya, dyb)
    dqkv3, dcq, dck = _fox_bwd("fox_bwd", qkv3, c_col, c_row, ob3, lse_b, dob.reshape(b, l, B_WIDTH))
    dqkv3, dsink = _swa_bwd("swa_bwd", qkv3, oa3, lse_a, doa.reshape(b, l, A_WIDTH), slopes, sink_rows, swa_bias, dqkv3)
    dqkv = dqkv3.reshape(t, QKV_W)
    df3, dbf = _fgate_bwd("fgate_bwd", f3, bf_row, dcq, dck)
    df = df3.reshape(t, 128)
    dwi_qkv = _mm_tn("proj_qkv_bwd_w", um, dqkv, tm_c=(512,), tn_c=(768,))
    dwi_g = _mm_tn("proj_gates_bwd_w", um, dgates, tm_c=(512,), tn_c=(512,))
    dwi_f = _mm_tn("proj_f_bwd_w", um, df, tm_c=(512,), tn_c=(128,))
    token = send("mix", (dwi_qkv, dwi_g, dwi_f, dwa, dwb, dwo))
    dh1, dgm = _proj_bwd_x("proj_bwd_x", dqkv, dgates, df, wi, h1, gm, dh2, dep=token)

    dh0, dg1 = ffn_bwd("ffn1", dh1, h0, g1, n1, gu1, a1, w1i, w1o, False)
    dh0_3 = dh0.reshape(b, l, D_MODEL)
    grad_x = dh0_3[:, PREFIX:, :]
    dmeta = dh0_3[:, N_PAD:PREFIX, :].reshape(b * N_META, D_MODEL)

    misc = jnp.concatenate([dbf[:, 0:B_HEADS], dsink[:, 0].reshape(1, A_HEADS), loss_blk[0:1, 0:1]], axis=1)
    misc = jnp.pad(misc, ((0, 0), (0, D_MODEL - misc.shape[1])))
    row = lax.broadcasted_iota(jnp.int32, (8, D_MODEL), 0)
    vec = jnp.zeros((8, D_MODEL), F32)
    for i, piece in enumerate((dg1, dgm, dg2, dgf, misc)):
        vec = jnp.where(row == i, piece, vec)
    small = jnp.concatenate([vec, dmeta], axis=0)
    return grad_x, small


def _pad_to(a, rows, cols):
    return jnp.pad(a, ((0, rows - a.shape[0]), (0, cols - a.shape[1])))


def _ffn_out_from_gathered(g):
    w = g.reshape(4, FF_SHARD, D_MODEL)
    return jnp.pad(w, ((0, 0), (0, FF_SHARD_P - FF_SHARD), (0, 0))).reshape(D_FF_P, D_MODEL)


def _ffn_out_to_scatter(dw):
    return dw.reshape(4, FF_SHARD_P, D_MODEL)[:, 0:FF_SHARD, :].reshape(N_DEV, FFO_SHARD, D_MODEL)


def _proj_segments():
    segs = [(HEAD_DIM * h, HEAD_DIM, 0, B_SEG + HEAD_DIM * A_HEAD_ORDER.index(h)) for h in range(A_HEADS)]
    segs += [(512, 128, 0, B_SEG + A_WIDTH), (640, 128, 0, B_SEG + A_WIDTH + 128)]
    for first, off in ((768, 0), (1280, 128), (1792, 256)):
        segs += [(first + 128 * hp, 128, 0, PAIR_W * hp + off) for hp in range(4)]
    segs += [(2304, B_HEADS, 2, 0), (2312, 2 * D_MODEL, 1, 0)]
    return segs


def _proj_from_gathered(g):
    def cols(first, width):
        out = []
        for p in range(N_DEV):
            lo, hi = max(first, WIN_SHARD * p), min(first + width, WIN_SHARD * (p + 1))
            if lo < hi:
                out.append(g[p, :, lo - WIN_SHARD * p:hi - WIN_SHARD * p])
        return out

    parts = []
    for arr in (0, 1, 2):
        for first, width, _, _ in sorted((s for s in _proj_segments() if s[2] == arr), key=lambda s: s[3]):
            parts += cols(first, width)
        if arr == 0:
            parts.append(jnp.zeros((D_MODEL, P_GATES - QKV_W), g.dtype))
    parts.append(jnp.zeros((D_MODEL, 128 - B_HEADS), g.dtype))
    return jnp.concatenate(parts, axis=1)


def _proj_to_scatter(dqkv_w, dg_w, df_w):
    arrays = (dqkv_w, dg_w, df_w)
    segs = sorted(_proj_segments())
    blocks = []
    for p in range(N_DEV):
        parts = []
        for first, width, arr, at in segs:
            lo, hi = max(first, WIN_SHARD * p), min(first + width, WIN_SHARD * (p + 1))
            if lo < hi:
                parts.append(arrays[arr][:, at + lo - first:at + hi - first])
        parts.append(jnp.zeros((D_MODEL, WIN_SHARD_P - WIN_SHARD), dqkv_w.dtype))
        blocks.append(jnp.concatenate(parts, axis=1))
    return jnp.stack(blocks, axis=0)


def _a_rows_from_natural(w):
    return jnp.concatenate([w[HEAD_DIM * h:HEAD_DIM * (h + 1)] for h in A_HEAD_ORDER], axis=0)


def _a_rows_to_natural(w):
    return jnp.concatenate([w[HEAD_DIM * A_HEAD_ORDER.index(h):HEAD_DIM * (A_HEAD_ORDER.index(h) + 1)]
                            for h in range(A_HEADS)], axis=0)


def kernel(x, meta_tokens, ffn1_norm, ffn1_w_in, ffn1_w_out, mix_norm, w_in, b_forget, attn_sinks, w_branch_a, w_branch_b, w_out, ffn2_norm, ffn2_w_in, ffn2_w_out, final_norm, loss_target, m_meta_tokens, m_ffn1_norm, m_ffn1_w_in, m_ffn1_w_out, m_mix_norm, m_w_in, m_b_forget, m_attn_sinks, m_w_branch_a, m_w_branch_b, m_w_out, m_ffn2_norm, m_ffn2_w_in, m_ffn2_w_out, m_final_norm, v_meta_tokens, v_ffn1_norm, v_ffn1_w_in, v_ffn1_w_out, v_mix_norm, v_w_in, v_b_forget, v_attn_sinks, v_w_branch_a, v_w_branch_b, v_w_out, v_ffn2_norm, v_ffn2_w_in, v_ffn2_w_out, v_final_norm):
    me = 4 * lax.axis_index("x") + 2 * lax.axis_index("y") + lax.axis_index("c")

    shards = (
        _pad_to(ffn1_w_in[0].astype(BF16), D_MODEL, FF_SHARD_P),
        ffn1_w_out[0].astype(BF16),
        _pad_to(w_in[0].astype(BF16), D_MODEL, WIN_SHARD_P),
        w_branch_a[0].astype(BF16), w_branch_b[0].astype(BF16), w_out[0].astype(BF16),
        _pad_to(ffn2_w_in[0].astype(BF16), D_MODEL, FF_SHARD_P),
        ffn2_w_out[0].astype(BF16),
        meta_tokens,
    )
    s1i, s1o, swi, swa, swb, swo, s2i, s2o, smeta = shards
    first_level, second_level = {}, {}
    first_level["meta"], tok = _gather2_start("gather_meta_start", (smeta,))
    first_level["ffn1_in"], tok = _gather2_start("gather_ffn1_in_start", (s1i,), after=tok)
    first_level["ffn1_out"], tok = _gather2_start("gather_ffn1_out_start", (s1o,), after=tok)
    first_level["mix"], tok = _gather2_start("gather_mix_start", (swi, swa, swb, swo), after=tok)
    first_level["ffn2"], tok = _gather2_start("gather_ffn2_start", (s2i, s2o), after=tok)
    started = {"tok": tok}

    def weights(group, after):
        if group.endswith(":forward"):
            group = group[:-len(":forward")]
            second_level[group], token = _gather2_forward("gather_" + group + "_forward", first_level[group], after)
            return token
        if group == "meta":
            after = weights("meta:forward", started["tok"])
        if group == "ffn1_in":
            after = weights("ffn1_in:forward", after)
        got = _gather2_wait("gather_" + group + "_wait", second_level[group], after)
        if group == "mix":
            gwi, gwa, gwb, gwo = got
            return (_proj_from_gathered(gwi), _a_rows_from_natural(gwa.transpose(1, 0, 2).reshape(A_WIDTH, D_MODEL)),
                    gwb.transpose(1, 0, 2).reshape(B_WIDTH, D_MODEL), gwo.reshape(D_MODEL, D_MODEL))
        if group == "ffn1_out":
            return (_ffn_out_from_gathered(got[0]),)
        if group == "meta":
            return (got[0].transpose(1, 0, 2).reshape(N_META, D_MODEL),)
        if group == "ffn1_in":
            return (got[0].reshape(2, 4, D_MODEL, FF_SHARD_P),)
        return got[0].reshape(2, 4, D_MODEL, FF_SHARD_P), _ffn_out_from_gathered(got[1])

    scatter_state = {}

    def send(group, grads):
        if group == "mix":
            dwi_qkv, dwi_g, dwi_f, dwa, dwb, dwo = grads
            dwa = _a_rows_to_natural(dwa)
            blocks = (_proj_to_scatter(dwi_qkv, dwi_g, dwi_f), dwa.reshape(A_WIDTH, N_DEV, 128).transpose(1, 0, 2),
                      dwb.reshape(B_WIDTH, N_DEV, 128).transpose(1, 0, 2), dwo.reshape(N_DEV, 128, D_MODEL))
        elif group.endswith("_in"):
            blocks = (grads[0].reshape(N_DEV, D_MODEL, FF_SHARD_P),)
        elif group.endswith("_out"):
            blocks = (_ffn_out_to_scatter(grads[0]),)
        else:
            blocks = (grads[0].reshape(N_DEV, D_MODEL, FF_SHARD_P), _ffn_out_to_scatter(grads[1]))
        scatter_state[group], token = _xchg_start("scatter_" + group + "_start", (), blocks)
        return token

    gf = final_norm.reshape(1, D_MODEL)
    grad_x, small = _local_step(x, loss_target, ffn1_norm, mix_norm, ffn2_norm, gf, b_forget, attn_sinks, weights, send)

    small_state, after = _xchg_start("gather_small_start", (small,), ())
    out = {}
    updates = (
        ("ffn2", (("ffn2_w_in", ffn2_w_in, m_ffn2_w_in, v_ffn2_w_in), ("ffn2_w_out", ffn2_w_out, m_ffn2_w_out, v_ffn2_w_out))),
        ("ffn1_out", (("ffn1_w_out", ffn1_w_out, m_ffn1_w_out, v_ffn1_w_out),)),
        ("mix", (("w_in", w_in, m_w_in, v_w_in), ("w_branch_a", w_branch_a, m_w_branch_a, v_w_branch_a),
                 ("w_branch_b", w_branch_b, m_w_branch_b, v_w_branch_b), ("w_out", w_out, m_w_out, v_w_out))),
    )
    last_update = ("ffn1_in", (("ffn1_w_in", ffn1_w_in, m_ffn1_w_in, v_ffn1_w_in),))

    def update(group, members, after):
        parts_list = _xchg_wait("scatter_" + group + "_wait", scatter_state[group], after)
        prev = None
        for (nm, w, m, v), parts in zip(members, parts_list):
            if nm.endswith("w_in"):
                res4 = _adam("adam_" + nm, w[0].T, m[0].T, v[0].T, parts, transposed=True, dep=prev)
                out[nm] = tuple(r.T[None] for r in res4)
            else:
                res4 = _adam("adam_" + nm, w[0], m[0], v[0], parts, dep=prev)
                out[nm] = tuple(r[None] for r in res4)
            prev = res4[0]
        return prev

    for group, members in updates + (last_update,):
        after = update(group, members, after)
    (packs,) = _xchg_wait("gather_small_wait", small_state, after)

    tot = _small_sum("small_sum", packs)
    loss = tot[4, 2 * B_HEADS]
    g_meta = lax.dynamic_slice(tot[8:24, :], (0, me * 128), (N_META, 128))
    out["meta_tokens"] = tuple(_adam("adam_meta_tokens", meta_tokens, m_meta_tokens, v_meta_tokens, g_meta[None]))

    def pack_small(n1, nm, n2, nf, bfv, skv):
        misc = jnp.pad(jnp.concatenate([bfv, skv], axis=1), ((0, 0), (0, D_MODEL - 2 * B_HEADS)))
        row = lax.broadcasted_iota(jnp.int32, (8, D_MODEL), 0)
        vec = jnp.zeros((8, D_MODEL), F32)
        for i, piece in enumerate((n1, nm, n2, nf.reshape(1, D_MODEL), misc)):
            vec = jnp.where(row == i, piece, vec)
        return vec

    w_pack = pack_small(ffn1_norm, mix_norm, ffn2_norm, final_norm, b_forget, attn_sinks)
    m_pack = pack_small(m_ffn1_norm, m_mix_norm, m_ffn2_norm, m_final_norm, m_b_forget, m_attn_sinks)
    v_pack = pack_small(v_ffn1_norm, v_mix_norm, v_ffn2_norm, v_final_norm, v_b_forget, v_attn_sinks)
    small4 = _adam("adam_small", w_pack, m_pack, v_pack, tot[0:8][None])
    for i, nm in enumerate(("ffn1_norm", "mix_norm", "ffn2_norm")):
        out[nm] = tuple(r[i:i + 1] for r in small4)
    out["final_norm"] = tuple(r[3] for r in small4)
    out["b_forget"] = tuple(r[4:5, 0:B_HEADS] for r in small4)
    out["attn_sinks"] = tuple(r[4:5, B_HEADS:2 * B_HEADS] for r in small4)

    names = ("meta_tokens", "ffn1_norm", "ffn1_w_in", "ffn1_w_out", "mix_norm", "w_in", "b_forget", "attn_sinks",
             "w_branch_a", "w_branch_b", "w_out", "ffn2_norm", "ffn2_w_in", "ffn2_w_out", "final_norm")
    return (loss, grad_x) + tuple(out[nm][kind] for kind in range(4) for nm in names)
```

```python
import jax
import jax.numpy as jnp
from jax import lax
from jax.experimental import pallas as pl
from jax.experimental.pallas import tpu as pltpu

F32 = jnp.float32
BF16 = jnp.bfloat16

D_MODEL = 1024
N_META = 16
BLOCK = 128
PREFIX = 128
N_PAD = PREFIX - N_META
HEAD_DIM = 64
A_HEADS = 8
A_KV_HEADS = 2
A_GROUP = 4
B_HEADS = 8
A_WIDTH = 512
A_KV_WIDTH = 128
B_WIDTH = 512
D_FF = 2816
N_DEV = 8
FF_SHARD = 2 * D_FF // N_DEV
FF_SHARD_P = 768
FFO_SHARD = D_FF // N_DEV
FFO_SHARD_P = FF_SHARD_P // 2
D_FF_P = 4 * FF_SHARD_P
W_IN_COLS = 4360
WIN_SHARD = W_IN_COLS // N_DEV
WIN_SHARD_P = 640
PAIR_W = 3 * 128
B_SEG = 4 * PAIR_W
A_SEG = A_WIDTH + 2 * A_KV_WIDTH
QKV_W = B_SEG + A_SEG
P_GATES = 2 * (2 * D_MODEL)
P_F = P_GATES + 2 * D_MODEL
PROJ_P = P_F + 128
A_HEAD_ORDER = (0, 4, 1, 5, 2, 6, 3, 7)
EPS = 1e-6
NEG = -1e30
SCALE = HEAD_DIM ** -0.5
ADAM_LR = 0.001
ADAM_B1 = 0.9
ADAM_B2 = 0.999
ADAM_EPS = 1e-08
ADAM_WD = 0.01
ADAM_STEP = 10
VMEM_LIMIT = 56 * 1024 * 1024
MESH_ID = pl.DeviceIdType.MESH
SMALL_ROWS = 40

_NN = (((1,), (0,)), ((), ()))
_NT = (((1,), (1,)), ((), ()))
_TN = (((0,), (0,)), ((), ()))


def _params(sem=None):
    return pltpu.CompilerParams(dimension_semantics=sem, vmem_limit_bytes=VMEM_LIMIT)


def _pick(n, cands):
    for c in cands:
        if n % c == 0:
            return c
    raise ValueError(f"no tile for {n}")


def _bf(v):
    return v if v.dtype == BF16 else v.astype(BF16)


def _mm(name, a, b, dims, grid, a_spec, b_spec, o_spec, out_shape, out_dtype, acc_shape, k_axis=None, nk=1,
        alpha=1.0, res=None, res_spec=None, dep=None):
    has_res = res is not None
    has_dep = dep is not None

    def body(*refs):
        a_ref, b_ref = refs[0], refs[1]
        r_ref = refs[2] if has_res else None
        o_ref = refs[2 + has_res + has_dep]

        def finish(acc):
            if alpha != 1.0:
                acc = acc * alpha
            if has_res:
                acc = acc + r_ref[...]
            o_ref[...] = acc.astype(o_ref.dtype)

        part = lax.dot_general(_bf(a_ref[...]), _bf(b_ref[...]), dims, preferred_element_type=F32)
        if nk == 1:
            finish(part)
        else:
            acc_ref = refs[-1]
            k = pl.program_id(k_axis)

            @pl.when(k == 0)
            def _():
                acc_ref[...] = part

            @pl.when(k > 0)
            def _():
                acc_ref[...] += part

            @pl.when(k == nk - 1)
            def _():
                finish(acc_ref[...])

    in_specs = [a_spec, b_spec] + ([res_spec] if has_res else []) + ([pl.BlockSpec(memory_space=pl.ANY)] if has_dep else [])
    args = (a, b) + ((res,) if has_res else ()) + ((dep,) if has_dep else ())
    sem = tuple("arbitrary" if (nk > 1 and i == k_axis) else "parallel" for i in range(len(grid)))
    return pl.pallas_call(
        body, name=name, grid=grid, in_specs=in_specs, out_specs=o_spec,
        out_shape=jax.ShapeDtypeStruct(out_shape, out_dtype),
        scratch_shapes=[pltpu.VMEM(acc_shape, F32)] if nk > 1 else [],
        compiler_params=_params(sem),
    )(*args)


def _mm_res_norm(name, a, w, res, g_next, alpha=1.0):
    t, k = a.shape
    tm = _pick(t, (544, 384, 256, 128))

    def body(a_ref, w_ref, r_ref, g_ref, h_ref, n_ref):
        acc = lax.dot_general(_bf(a_ref[...]), w_ref[...], _NN, preferred_element_type=F32)
        if alpha != 1.0:
            acc = acc * alpha
        hv = acc + r_ref[...]
        h_ref[...] = hv
        r = lax.rsqrt(jnp.mean(hv * hv, axis=-1, keepdims=True) + EPS)
        n_ref[...] = ((hv * r) * g_ref[...]).astype(BF16)

    row = pl.BlockSpec((tm, D_MODEL), lambda i: (i, 0))
    return pl.pallas_call(
        body, name=name, grid=(t // tm,),
        in_specs=[pl.BlockSpec((tm, k), lambda i: (i, 0)), pl.BlockSpec((k, D_MODEL), lambda i: (0, 0)), row,
                  pl.BlockSpec((1, D_MODEL), lambda i: (0, 0))],
        out_specs=[row, row],
        out_shape=[jax.ShapeDtypeStruct((t, D_MODEL), F32), jax.ShapeDtypeStruct((t, D_MODEL), BF16)],
        compiler_params=_params(("parallel",)),
    )(a, w, res, g_next)


def _mm_nn(name, a, b, out_dtype=F32, alpha=1.0, res=None, tn_c=(512, 640, 256, 128), cols=None,
           tm_c=(1088, 768, 512, 256, 128), dep=None):
    t, k = a.shape
    c0, n = (0, b.shape[1]) if cols is None else cols
    tm = _pick(t, tm_c)
    tn = _pick(n, tn_c)
    assert c0 % tn == 0
    jb = c0 // tn
    return _mm(name, a, b, _NN, (t // tm, n // tn),
               pl.BlockSpec((tm, k), lambda i, j: (i, 0)), pl.BlockSpec((k, tn), lambda i, j: (0, jb + j)),
               pl.BlockSpec((tm, tn), lambda i, j: (i, j)), (t, n), out_dtype, None,
               alpha=alpha, res=res, res_spec=pl.BlockSpec((tm, tn), lambda i, j: (i, j)), dep=dep)


def _mm_nt(name, a, b, out_dtype=F32, alpha=1.0, tn_c=(768, 512, 256, 128), tk_c=None, dep=None, res=None, kcols=None):
    t, k = a.shape
    n = b.shape[0]
    c0 = 0 if kcols is None else kcols[0]
    tm = _pick(t, (1088, 768, 512, 256, 128))
    tn = _pick(n, tn_c)
    tk = k if tk_c is None else _pick(k, tk_c)
    nk = k // tk
    assert c0 % tk == 0
    kb = c0 // tk
    return _mm(name, a, b, _NT, (t // tm, n // tn, nk),
               pl.BlockSpec((tm, tk), lambda i, j, kk: (i, kk)), pl.BlockSpec((tn, tk), lambda i, j, kk: (j, kb + kk)),
               pl.BlockSpec((tm, tn), lambda i, j, kk: (i, j)), (t, n), out_dtype, (tm, tn), k_axis=2, nk=nk, alpha=alpha,
               dep=dep, res=res, res_spec=pl.BlockSpec((tm, tn), lambda i, j, kk: (i, j)))


def _mm_tn(name, a, b, out_dtype=BF16, alpha=1.0, tm_c=(768, 512, 256, 128), tn_c=(512, 640, 256, 128)):
    t, m = a.shape
    n = b.shape[1]
    tm = _pick(m, tm_c)
    tn = _pick(n, tn_c)
    bytes_a, bytes_b = a.size * a.dtype.itemsize, b.size * b.dtype.itemsize
    if bytes_a + bytes_b * (m // tm) <= bytes_b + bytes_a * (n // tn):
        return _mm(name, a, b, _TN, (m // tm, n // tn),
                   pl.BlockSpec((t, tm), lambda i, j: (0, i)), pl.BlockSpec((t, tn), lambda i, j: (0, j)),
                   pl.BlockSpec((tm, tn), lambda i, j: (i, j)), (m, n), out_dtype, None, alpha=alpha)
    return _mm(name, a, b, _TN, (n // tn, m // tm),
               pl.BlockSpec((t, tm), lambda j, i: (0, i)), pl.BlockSpec((t, tn), lambda j, i: (0, j)),
               pl.BlockSpec((tm, tn), lambda j, i: (i, j)), (m, n), out_dtype, None, alpha=alpha)


def _ffn_in_fwd(name, n, wblk, dep=None):
    t = n.shape[0]
    tm = _pick(t, (1088, 768, 512, 256, 128))
    has_dep = dep is not None

    def body(n_ref, w_ref, *rest):
        gu_ref, a_ref = rest[-2], rest[-1]
        nv = n_ref[...]
        g = lax.dot_general(nv, w_ref[0], _NN, preferred_element_type=F32)
        u = lax.dot_general(nv, w_ref[1], _NN, preferred_element_type=F32)
        sg = jax.nn.sigmoid(g)
        silu = g * sg
        a_ref[...] = (silu * u).astype(BF16)
        gu_ref[0] = ((0.5 * u) * (sg + silu * (1.0 - sg))).astype(BF16)
        gu_ref[1] = (0.5 * silu).astype(BF16)

    return pl.pallas_call(
        body, name=name, grid=(t // tm, 4),
        in_specs=[pl.BlockSpec((tm, D_MODEL), lambda i, j: (i, 0)),
                  pl.BlockSpec((2, None, D_MODEL, FF_SHARD_P), lambda i, j: (0, j, 0, 0))]
        + ([pl.BlockSpec(memory_space=pl.ANY)] if has_dep else []),
        out_specs=[pl.BlockSpec((2, tm, FF_SHARD_P), lambda i, j: (0, i, j)),
                   pl.BlockSpec((tm, FF_SHARD_P), lambda i, j: (i, j))],
        out_shape=[jax.ShapeDtypeStruct((2, t, D_FF_P), BF16), jax.ShapeDtypeStruct((t, D_FF_P), BF16)],
        compiler_params=_params(("parallel", "parallel")),
    )(*((n, wblk) + ((dep,) if has_dep else ())))


def _ffn_out_bwd_x(name, dh, w_out, gu, dep=None):
    t = dh.shape[0]
    tm = _pick(t, (1088, 768, 512, 256, 128))
    has_dep = dep is not None

    def body(dh_ref, w_ref, gu_ref, *rest):
        o_ref = rest[-1]
        da = lax.dot_general(_bf(dh_ref[...]), w_ref[...], _NT, preferred_element_type=F32)
        o_ref[0] = (da * gu_ref[0].astype(F32)).astype(BF16)
        o_ref[1] = (da * gu_ref[1].astype(F32)).astype(BF16)

    gu_spec = pl.BlockSpec((2, tm, FF_SHARD_P), lambda i, j: (0, i, j))
    return pl.pallas_call(
        body, name=name, grid=(t // tm, 4),
        in_specs=[pl.BlockSpec((tm, D_MODEL), lambda i, j: (i, 0)), pl.BlockSpec((FF_SHARD_P, D_MODEL), lambda i, j: (j, 0)),
                  gu_spec] + ([pl.BlockSpec(memory_space=pl.ANY)] if has_dep else []),
        out_specs=gu_spec, out_shape=jax.ShapeDtypeStruct((2, t, D_FF_P), BF16),
        compiler_params=_params(("parallel", "parallel")),
    )(*((dh, w_out, gu) + ((dep,) if has_dep else ())))


def _rms_bwd_rows(dn, h, g, dres):
    r = lax.rsqrt(jnp.mean(h * h, axis=-1, keepdims=True) + EPS)
    tv = dn * g
    dot = jnp.mean(tv * h, axis=-1, keepdims=True)
    return dres + (r * tv - h * (r * r * r * dot)), jnp.sum(dn * (h * r), axis=0, keepdims=True)


def _accumulate_rows(ref, part, first):
    @pl.when(first)
    def _():
        ref[...] = part

    @pl.when(jnp.logical_not(first))
    def _():
        ref[...] += part


def _ffn_in_bwd_x(name, dgu, wblk, h_in, g_norm, dres, dep=None, examples=None):
    t = dgu.shape[1]
    tm = _pick(t, (544, 384, 256, 128))
    has_dep = dep is not None
    split = examples is not None
    if split:
        l = t // examples
        per = l // tm
        assert per * tm == l and tm > PREFIX

    def body(d_ref, w_ref, h_ref, g_ref, r_ref, *rest):
        acc = None
        for s in range(2):
            for j in range(4):
                part = lax.dot_general(d_ref[s, :, FF_SHARD_P * j:FF_SHARD_P * (j + 1)], w_ref[s, j], _NT,
                                       preferred_element_type=F32)
                acc = part if acc is None else acc + part
        dh, dg = _rms_bwd_rows(acc, h_ref[...], g_ref[...], r_ref[...])
        i = pl.program_id(0)
        if not split:
            dh_ref, dg_ref = rest[-2], rest[-1]
            dh_ref[...] = dh
        else:
            gx_ref, meta_ref, dg_ref, buf, sem = rest[-5:]
            bi, r = i // per, i % per
            buf[...] = dh

            @pl.when(r == 0)
            def _():
                meta_ref[...] = dh[N_PAD:PREFIX]
                cp = pltpu.make_async_copy(buf.at[pl.ds(PREFIX, tm - PREFIX)], gx_ref.at[bi, pl.ds(0, tm - PREFIX)], sem)
                cp.start()
                cp.wait()

            if per > 1:
                @pl.when(r > 0)
                def _():
                    cp = pltpu.make_async_copy(buf, gx_ref.at[bi, pl.ds(pl.multiple_of(r * tm - PREFIX, 8), tm)], sem)
                    cp.start()
                    cp.wait()

        _accumulate_rows(dg_ref, dg, i == 0)

    row = pl.BlockSpec((tm, D_MODEL), lambda i: (i, 0))
    vec = pl.BlockSpec((1, D_MODEL), lambda i: (0, 0))
    if split:
        out_specs = [pl.BlockSpec(memory_space=pl.ANY), pl.BlockSpec((None, N_META, D_MODEL), lambda i: (i // per, 0, 0)), vec]
        out_shape = [jax.ShapeDtypeStruct((examples, l - PREFIX, D_MODEL), F32),
                     jax.ShapeDtypeStruct((examples, N_META, D_MODEL), F32), jax.ShapeDtypeStruct((1, D_MODEL), F32)]
        scratch = [pltpu.VMEM((tm, D_MODEL), F32), pltpu.SemaphoreType.DMA(())]
    else:
        out_specs = [row, vec]
        out_shape = [jax.ShapeDtypeStruct((t, D_MODEL), F32), jax.ShapeDtypeStruct((1, D_MODEL), F32)]
        scratch = []
    return pl.pallas_call(
        body, name=name, grid=(t // tm,),
        in_specs=[pl.BlockSpec((2, tm, D_FF_P), lambda i: (0, i, 0)),
                  pl.BlockSpec((2, 4, D_MODEL, FF_SHARD_P), lambda i: (0, 0, 0, 0), pipeline_mode=pl.Buffered(1)),
                  row, vec, row]
        + ([pl.BlockSpec(memory_space=pl.ANY)] if has_dep else []),
        out_specs=out_specs, out_shape=out_shape, scratch_shapes=scratch,
        compiler_params=_params(("arbitrary",)),
    )(*((dgu, wblk, h_in, g_norm, dres) + ((dep,) if has_dep else ())))


def _proj_fwd(name, um, wi):
    t = um.shape[0]
    tm = _pick(t, (544, 384, 256, 128))

    def body(u_ref, w_ref, q_ref, g_ref, f_ref):
        uv = u_ref[...]
        q_ref[...] = lax.dot_general(uv, w_ref[:, 0:QKV_W], _NN, preferred_element_type=F32).astype(BF16)
        g_ref[...] = lax.dot_general(uv, w_ref[:, P_GATES:P_F], _NN, preferred_element_type=F32)
        f_ref[...] = lax.dot_general(uv, w_ref[:, P_F:PROJ_P], _NN, preferred_element_type=F32)

    return pl.pallas_call(
        body, name=name, grid=(t // tm,),
        in_specs=[pl.BlockSpec((tm, D_MODEL), lambda i: (i, 0)),
                  pl.BlockSpec((D_MODEL, PROJ_P), lambda i: (0, 0), pipeline_mode=pl.Buffered(1))],
        out_specs=[pl.BlockSpec((tm, QKV_W), lambda i: (i, 0)), pl.BlockSpec((tm, 2 * D_MODEL), lambda i: (i, 0)),
                   pl.BlockSpec((tm, 128), lambda i: (i, 0))],
        out_shape=[jax.ShapeDtypeStruct((t, QKV_W), BF16), jax.ShapeDtypeStruct((t, 2 * D_MODEL), F32),
                   jax.ShapeDtypeStruct((t, 128), F32)],
        compiler_params=_params(("parallel",)),
    )(um, wi)


def _proj_bwd_x(name, dqkv, dgates, df, wi, h_in, g_norm, dres, dep=None):
    t = dqkv.shape[0]
    tm = _pick(t, (544, 384, 256, 128))
    has_dep = dep is not None

    def body(q_ref, gt_ref, f_ref, w_ref, h_ref, g_ref, r_ref, *rest):
        dh_ref, dg_ref = rest[-2], rest[-1]
        acc = lax.dot_general(q_ref[...], w_ref[:, 0:QKV_W], _NT, preferred_element_type=F32)
        acc = acc + lax.dot_general(gt_ref[...], w_ref[:, P_GATES:P_F], _NT, preferred_element_type=F32)
        acc = acc + lax.dot_general(f_ref[...], w_ref[:, P_F:PROJ_P], _NT, preferred_element_type=F32)
        dh, dg = _rms_bwd_rows(acc, h_ref[...], g_ref[...], r_ref[...])
        dh_ref[...] = dh
        _accumulate_rows(dg_ref, dg, pl.program_id(0) == 0)

    row = pl.BlockSpec((tm, D_MODEL), lambda i: (i, 0))
    vec = pl.BlockSpec((1, D_MODEL), lambda i: (0, 0))
    return pl.pallas_call(
        body, name=name, grid=(t // tm,),
        in_specs=[pl.BlockSpec((tm, QKV_W), lambda i: (i, 0)), pl.BlockSpec((tm, 2 * D_MODEL), lambda i: (i, 0)),
                  pl.BlockSpec((tm, 128), lambda i: (i, 0)),
                  pl.BlockSpec((D_MODEL, PROJ_P), lambda i: (0, 0), pipeline_mode=pl.Buffered(1)), row, vec, row]
        + ([pl.BlockSpec(memory_space=pl.ANY)] if has_dep else []),
        out_specs=[row, vec],
        out_shape=[jax.ShapeDtypeStruct((t, D_MODEL), F32), jax.ShapeDtypeStruct((1, D_MODEL), F32)],
        compiler_params=_params(("arbitrary",)),
    )(*((dqkv, dgates, df, wi, h_in, g_norm, dres) + ((dep,) if has_dep else ())))


def _ffn_in_bwd_w(name, n, dgu):
    t = n.shape[0]
    tk = t
    nk = 1
    return _mm(name, n, dgu, _TN, (2, 4, nk),
               pl.BlockSpec((tk, D_MODEL), lambda s, j, kk: (kk, 0)),
               pl.BlockSpec((None, tk, FF_SHARD_P), lambda s, j, kk: (s, kk, j)),
               pl.BlockSpec((None, None, D_MODEL, FF_SHARD_P), lambda s, j, kk: (s, j, 0, 0)),
               (2, 4, D_MODEL, FF_SHARD_P), BF16, (D_MODEL, FF_SHARD_P), k_axis=2, nk=nk)


def _rms_fwd(name, h, g):
    t = h.shape[0]
    tm = _pick(t, (544, 384, 256, 128))

    def body(h_ref, g_ref, o_ref):
        hv = h_ref[...]
        r = lax.rsqrt(jnp.mean(hv * hv, axis=-1, keepdims=True) + EPS)
        o_ref[...] = ((hv * r) * g_ref[...]).astype(BF16)

    return pl.pallas_call(
        body, name=name, grid=(t // tm,),
        in_specs=[pl.BlockSpec((tm, D_MODEL), lambda i: (i, 0)), pl.BlockSpec((1, D_MODEL), lambda i: (0, 0))],
        out_specs=pl.BlockSpec((tm, D_MODEL), lambda i: (i, 0)),
        out_shape=jax.ShapeDtypeStruct((t, D_MODEL), BF16), compiler_params=_params(("parallel",)),
    )(h, g)


def _embed_norm(name, x, meta, g):
    b, s, _ = x.shape
    half = (s + PREFIX) // 2
    first = half - PREFIX
    assert first > 0 and half % 16 == 0

    def body(x_ref, m_ref, g_ref, h_ref, n_ref, buf, sem):
        bi, k = pl.program_id(0), pl.program_id(1)

        @pl.when(k == 0)
        def _():
            buf[0:N_PAD, :] = jnp.zeros((N_PAD, D_MODEL), F32)
            buf[N_PAD:PREFIX, :] = m_ref[...]
            cp = pltpu.make_async_copy(x_ref.at[bi, pl.ds(0, first)], buf.at[pl.ds(PREFIX, first)], sem)
            cp.start()
            cp.wait()

        @pl.when(k == 1)
        def _():
            cp = pltpu.make_async_copy(x_ref.at[bi, pl.ds(first, half)], buf, sem)
            cp.start()
            cp.wait()

        hv = buf[...]
        h_ref[...] = hv
        r = lax.rsqrt(jnp.mean(hv * hv, axis=-1, keepdims=True) + EPS)
        n_ref[...] = ((hv * r) * g_ref[...]).astype(BF16)

    rows = pl.BlockSpec((half, D_MODEL), lambda bi, k: (2 * bi + k, 0))
    return pl.pallas_call(
        body, name=name, grid=(b, 2),
        in_specs=[pl.BlockSpec(memory_space=pl.ANY), pl.BlockSpec((N_META, D_MODEL), lambda bi, k: (0, 0)),
                  pl.BlockSpec((1, D_MODEL), lambda bi, k: (0, 0))],
        out_specs=[rows, rows],
        out_shape=[jax.ShapeDtypeStruct((2 * b * half, D_MODEL), F32), jax.ShapeDtypeStruct((2 * b * half, D_MODEL), BF16)],
        scratch_shapes=[pltpu.VMEM((half, D_MODEL), F32), pltpu.SemaphoreType.DMA(())],
        compiler_params=_params(("arbitrary", "arbitrary")),
    )(x, meta, g)


def _rms_bwd(name, h, g, dn, dres):
    t = h.shape[0]
    tm = _pick(t, (544, 384, 256, 128))

    def body(h_ref, g_ref, dn_ref, dres_ref, dh_ref, dg_ref):
        i = pl.program_id(0)
        hv = h_ref[...]
        dnv = dn_ref[...]
        r = lax.rsqrt(jnp.mean(hv * hv, axis=-1, keepdims=True) + EPS)
        tv = dnv * g_ref[...]
        dot = jnp.mean(tv * hv, axis=-1, keepdims=True)
        dh_ref[...] = dres_ref[...] + (r * tv - hv * (r * r * r * dot))
        part = jnp.sum(dnv * (hv * r), axis=0, keepdims=True)

        @pl.when(i == 0)
        def _():
            dg_ref[...] = part

        @pl.when(i > 0)
        def _():
            dg_ref[...] += part

    row = pl.BlockSpec((tm, D_MODEL), lambda i: (i, 0))
    vec = pl.BlockSpec((1, D_MODEL), lambda i: (0, 0))
    return pl.pallas_call(
        body, name=name, grid=(t // tm,), in_specs=[row, vec, row, row], out_specs=[row, vec],
        out_shape=[jax.ShapeDtypeStruct((t, D_MODEL), F32), jax.ShapeDtypeStruct((1, D_MODEL), F32)],
        compiler_params=_params(("arbitrary",)),
    )(h, g, dn, dres)


def _branch_gate_fwd(name, oa, ob, wa, wb, gates, dep=None):
    t = gates.shape[0]
    tm = _pick(t, (544, 384, 256, 128))
    has_dep = dep is not None

    def body(oa_ref, ob_ref, wa_ref, wb_ref, g_ref, *rest):
        o_ref, ya_ref, yb_ref = rest[-3:]
        ya = lax.dot_general(_bf(oa_ref[...]), wa_ref[...], _NN, preferred_element_type=F32)
        yb = lax.dot_general(_bf(ob_ref[...]), wb_ref[...], _NN, preferred_element_type=F32)
        sa = jax.nn.sigmoid(g_ref[:, 0:D_MODEL])
        sb = jax.nn.sigmoid(g_ref[:, D_MODEL:2 * D_MODEL])
        o_ref[...] = (sa * ya + sb * yb).astype(BF16)
        ya_ref[...] = ya.astype(BF16)
        yb_ref[...] = yb.astype(BF16)

    blk = pl.BlockSpec((tm, D_MODEL), lambda i: (i, 0))
    narrow = pl.BlockSpec((tm, A_WIDTH), lambda i: (i, 0))
    wide = pl.BlockSpec((tm, 2 * D_MODEL), lambda i: (i, 0))
    wspec = pl.BlockSpec((A_WIDTH, D_MODEL), lambda i: (0, 0))
    out = jax.ShapeDtypeStruct((t, D_MODEL), BF16)
    return pl.pallas_call(
        body, name=name, grid=(t // tm,),
        in_specs=[narrow, narrow, wspec, wspec, wide] + ([pl.BlockSpec(memory_space=pl.ANY)] if has_dep else []),
        out_specs=[blk, blk, blk], out_shape=[out, out, out], compiler_params=_params(("parallel",)),
    )(*((oa, ob, wa, wb, gates) + ((dep,) if has_dep else ())))


def _branch_bwd_x(name, dya, dyb, wa, wb):
    t = dya.shape[0]
    tm = _pick(t, (1088, 768, 512, 256, 128))

    def body(da_ref, db_ref, wa_ref, wb_ref, oa_ref, ob_ref):
        oa_ref[...] = lax.dot_general(da_ref[...], wa_ref[...], _NT, preferred_element_type=F32)
        ob_ref[...] = lax.dot_general(db_ref[...], wb_ref[...], _NT, preferred_element_type=F32).astype(BF16)

    blk = pl.BlockSpec((tm, D_MODEL), lambda i: (i, 0))
    narrow = pl.BlockSpec((tm, A_WIDTH), lambda i: (i, 0))
    wspec = pl.BlockSpec((A_WIDTH, D_MODEL), lambda i: (0, 0), pipeline_mode=pl.Buffered(1))
    return pl.pallas_call(
        body, name=name, grid=(t // tm,), in_specs=[blk, blk, wspec, wspec], out_specs=[narrow, narrow],
        out_shape=[jax.ShapeDtypeStruct((t, A_WIDTH), F32), jax.ShapeDtypeStruct((t, B_WIDTH), BF16)],
        compiler_params=_params(("parallel",)),
    )(dya, dyb, wa, wb)


def _branch_bwd_w(name, oa, ob, dya, dyb):
    t = oa.shape[0]
    tn = 512

    def body(oa_ref, ob_ref, da_ref, db_ref, wa_ref, wb_ref):
        wa_ref[...] = lax.dot_general(_bf(oa_ref[...]), da_ref[...], _TN, preferred_element_type=F32).astype(BF16)
        wb_ref[...] = lax.dot_general(_bf(ob_ref[...]), db_ref[...], _TN, preferred_element_type=F32).astype(BF16)

    whole = pl.BlockSpec((t, A_WIDTH), lambda j: (0, 0), pipeline_mode=pl.Buffered(1))
    cols = pl.BlockSpec((t, tn), lambda j: (0, j))
    out_spec = pl.BlockSpec((A_WIDTH, tn), lambda j: (0, j))
    out = jax.ShapeDtypeStruct((A_WIDTH, D_MODEL), BF16)
    return pl.pallas_call(
        body, name=name, grid=(D_MODEL // tn,), in_specs=[whole, whole, cols, cols], out_specs=[out_spec, out_spec],
        out_shape=[out, out], compiler_params=_params(("parallel",)),
    )(oa, ob, dya, dyb)


def _mix_out_gate_bwd(name, dh, wo, gates, ya, yb):
    t = gates.shape[0]
    tm = _pick(t, (544, 384, 256, 128))

    def body(dh_ref, w_ref, g_ref, ya_ref, yb_ref, dya_ref, dyb_ref, dg_ref):
        dm = lax.dot_general(_bf(dh_ref[...]), w_ref[...], _NT, preferred_element_type=F32)
        sa = jax.nn.sigmoid(g_ref[:, 0:D_MODEL])
        sb = jax.nn.sigmoid(g_ref[:, D_MODEL:2 * D_MODEL])
        dya_ref[...] = (dm * sa).astype(BF16)
        dyb_ref[...] = (dm * sb).astype(BF16)
        dg_ref[:, 0:D_MODEL] = (dm * ya_ref[...].astype(F32) * (sa * (1.0 - sa))).astype(BF16)
        dg_ref[:, D_MODEL:2 * D_MODEL] = (dm * yb_ref[...].astype(F32) * (sb * (1.0 - sb))).astype(BF16)

    blk = pl.BlockSpec((tm, D_MODEL), lambda i: (i, 0))
    wide = pl.BlockSpec((tm, 2 * D_MODEL), lambda i: (i, 0))
    out = jax.ShapeDtypeStruct((t, D_MODEL), BF16)
    return pl.pallas_call(
        body, name=name, grid=(t // tm,),
        in_specs=[blk, pl.BlockSpec((D_MODEL, D_MODEL), lambda i: (0, 0)), wide, blk, blk], out_specs=[blk, blk, wide],
        out_shape=[out, out, jax.ShapeDtypeStruct((t, 2 * D_MODEL), BF16)], compiler_params=_params(("parallel",)),
    )(dh, wo, gates, ya, yb)


def _loss_head(name, h3, gf, tgt):
    b, l, _ = h3.shape
    nb = l // BLOCK

    def body(h_ref, g_ref, t_ref, dh_ref, loss_ref, dg_ref):
        first = (pl.program_id(0) == 0) & (pl.program_id(1) == 0)
        real = (pl.program_id(1) > 0).astype(F32)
        hv = h_ref[...]
        g = g_ref[...]
        r = lax.rsqrt(jnp.mean(hv * hv, axis=-1, keepdims=True) + EPS)
        xn = hv * r
        err = (xn * g - t_ref[...]) * real
        lpart = 0.5 * jnp.sum(jnp.mean(err * err, axis=-1, keepdims=True), axis=0, keepdims=True)
        dy = err * (1.0 / D_MODEL)
        tv = dy * g
        dot = jnp.mean(tv * hv, axis=-1, keepdims=True)
        dh_ref[...] = r * tv - hv * (r * r * r * dot)
        gpart = jnp.sum(dy * xn, axis=0, keepdims=True)

        @pl.when(first)
        def _():
            loss_ref[...] = jnp.zeros_like(loss_ref)
            dg_ref[...] = jnp.zeros_like(dg_ref)

        loss_ref[...] += jnp.broadcast_to(lpart, loss_ref.shape)
        dg_ref[...] += gpart

    return pl.pallas_call(
        body, name=name, grid=(b, nb),
        in_specs=[pl.BlockSpec((None, BLOCK, D_MODEL), lambda bi, n: (bi, n, 0)),
                  pl.BlockSpec((1, D_MODEL), lambda bi, n: (0, 0)),
                  pl.BlockSpec((None, BLOCK, D_MODEL), lambda bi, n: (bi, jnp.maximum(n - 1, 0), 0))],
        out_specs=[pl.BlockSpec((None, BLOCK, D_MODEL), lambda bi, n: (bi, n, 0)),
                   pl.BlockSpec((8, 128), lambda bi, n: (0, 0)),
                   pl.BlockSpec((1, D_MODEL), lambda bi, n: (0, 0))],
        out_shape=[jax.ShapeDtypeStruct(h3.shape, F32), jax.ShapeDtypeStruct((8, 128), F32),
                   jax.ShapeDtypeStruct((1, D_MODEL), F32)],
        compiler_params=_params(("arbitrary", "arbitrary")),
    )(h3, gf, tgt)


def _fgate_fwd(name, f3, bf_row):
    b, l, _ = f3.shape
    nb = l // BLOCK

    def body(f_ref, b_ref, cc_ref, cr_ref):
        r_i = lax.broadcasted_iota(jnp.int32, (BLOCK, BLOCK), 0)
        c_i = lax.broadcasted_iota(jnp.int32, (BLOCK, BLOCK), 1)
        tri = (r_i >= c_i).astype(F32)
        carry = jnp.zeros((1, 128), F32)
        for blk in range(nb):
            rows = slice(blk * BLOCK, (blk + 1) * BLOCK)
            z = f_ref[rows, :] + b_ref[...]
            lf = jnp.minimum(z, 0.0) - jnp.log(1.0 + jnp.exp(-jnp.abs(z)))
            cb = jnp.dot(tri, lf, preferred_element_type=F32, precision=lax.Precision.HIGHEST) + carry
            carry = cb[BLOCK - 1:BLOCK, :]
            cbt = cb.T
            for hh in range(B_HEADS):
                cc_ref[hh, rows, :] = jnp.sum(jnp.where(c_i == hh, cb, 0.0), axis=1, keepdims=True)
                cr_ref[hh, :, rows] = cbt[hh:hh + 1, :]

    return pl.pallas_call(
        body, name=name, grid=(b,),
        in_specs=[pl.BlockSpec((None, l, 128), lambda bi: (bi, 0, 0)),
                  pl.BlockSpec((1, 128), lambda bi: (0, 0))],
        out_specs=[pl.BlockSpec((None, B_HEADS, l, 1), lambda bi: (bi, 0, 0, 0)),
                   pl.BlockSpec((None, B_HEADS, 1, l), lambda bi: (bi, 0, 0, 0))],
        out_shape=[jax.ShapeDtypeStruct((b, B_HEADS, l, 1), F32), jax.ShapeDtypeStruct((b, B_HEADS, 1, l), F32)],
        compiler_params=_params(("parallel",)),
    )(f3, bf_row)


def _fgate_bwd(name, f3, bf_row, dcq, dck):
    b, l, _ = f3.shape
    nb = l // BLOCK

    def body(f_ref, b_ref, dcq_ref, dck_ref, df_ref, db_ref):
        r_i = lax.broadcasted_iota(jnp.int32, (BLOCK, BLOCK), 0)
        c_i = lax.broadcasted_iota(jnp.int32, (BLOCK, BLOCK), 1)
        tri = (r_i <= c_i).astype(F32)
        carry = jnp.zeros((1, 128), F32)
        total = jnp.zeros((1, 128), F32)
        for blk in range(nb - 1, -1, -1):
            rows = slice(blk * BLOCK, (blk + 1) * BLOCK)
            krows = jnp.concatenate([dck_ref[hh, :, rows] for hh in range(B_HEADS)]
                                    + [jnp.zeros((BLOCK - B_HEADS, BLOCK), F32)], axis=0)
            dcb = krows.T
            for hh in range(B_HEADS):
                dcb = dcb + jnp.where(c_i == hh, dcq_ref[hh, rows, :], 0.0)
            rc = jnp.dot(tri, dcb, preferred_element_type=F32, precision=lax.Precision.HIGHEST) + carry
            carry = rc[0:1, :]
            z = f_ref[rows, :] + b_ref[...]
            df = rc * (1.0 / (1.0 + jnp.exp(z)))
            df_ref[rows, :] = df.astype(BF16)
            total = total + jnp.sum(df, axis=0, keepdims=True)

        @pl.when(pl.program_id(0) == 0)
        def _():
            db_ref[...] = total

        @pl.when(pl.program_id(0) > 0)
        def _():
            db_ref[...] += total

    return pl.pallas_call(
        body, name=name, grid=(b,),
        in_specs=[pl.BlockSpec((None, l, 128), lambda bi: (bi, 0, 0)),
                  pl.BlockSpec((1, 128), lambda bi: (0, 0)),
                  pl.BlockSpec((None, B_HEADS, l, 1), lambda bi: (bi, 0, 0, 0)),
                  pl.BlockSpec((None, B_HEADS, 1, l), lambda bi: (bi, 0, 0, 0))],
        out_specs=[pl.BlockSpec((None, l, 128), lambda bi: (bi, 0, 0)), pl.BlockSpec((1, 128), lambda bi: (0, 0))],
        out_shape=[jax.ShapeDtypeStruct((b, l, 128), BF16), jax.ShapeDtypeStruct((1, 128), F32)],
        compiler_params=_params(("arbitrary",)),
    )(f3, bf_row, dcq, dck)


A_Q_BLK = B_SEG // A_WIDTH
A_K_BLK = (B_SEG + A_WIDTH) // 128
A_V_BLK = A_K_BLK + 1
A_SEG_BLK = B_SEG // A_SEG
STACK = A_HEADS * BLOCK


def _lane_lo():
    return lax.broadcasted_iota(jnp.int32, (1, 128), 1) < HEAD_DIM


def _stack_heads(x, masked):
    lo = _lane_lo()
    blks = [x[:, 128 * j:128 * (j + 1)] for j in range(4)]
    if not masked:
        return jnp.concatenate(blks + blks, axis=0)
    zero = jnp.zeros_like(blks[0])
    return jnp.concatenate([jnp.where(lo, bk, zero) for bk in blks] + [jnp.where(lo, zero, bk) for bk in blks], axis=0)


def _unstack_heads(y):
    lo = _lane_lo()
    return jnp.concatenate([jnp.where(lo, y[128 * j:128 * (j + 1)], y[128 * (4 + j):128 * (5 + j)]) for j in range(4)], axis=1)


def _swa_bias(slopes):
    r_i = jnp.arange(STACK)[:, None]
    c_i = jnp.arange(3 * BLOCK)[None, :]
    seg = c_i >> 7
    out = []
    for n in range(3):
        qpos = n * BLOCK + (r_i & (BLOCK - 1))
        kpos = jnp.where(seg == 0, c_i, (n - 2) * BLOCK + c_i)
        dist = qpos - kpos
        band = (seg != 0) & (dist < BLOCK) & (kpos >= PREFIX)
        meta = (seg == 0) & (c_i >= N_PAD)
        out.append(jnp.where((dist >= 0) & (band | meta), -slopes * dist.astype(F32), NEG))
    return jnp.stack(out, axis=0)


def _swa_scores(q, kcat, n, slope, bias):
    s = lax.dot_general(q, kcat, _NT, preferred_element_type=F32) + bias
    further = slope * (-BLOCK * jnp.maximum(n - 2, 0)).astype(F32)
    return jnp.concatenate([s[:, 0:BLOCK] + further, s[:, BLOCK:]], axis=1)


def _swa_specs():
    def kv(col_blk):
        return [pl.BlockSpec((None, BLOCK, 128), lambda b, n: (b, 0, col_blk)),
                pl.BlockSpec((None, BLOCK, 128), lambda b, n: (b, jnp.maximum(n - 1, 0), col_blk)),
                pl.BlockSpec((None, BLOCK, 128), lambda b, n: (b, n, col_blk))]

    q_spec = pl.BlockSpec((None, BLOCK, A_WIDTH), lambda b, n: (b, n, A_Q_BLK))
    o_spec = pl.BlockSpec((None, BLOCK, A_WIDTH), lambda b, n: (b, n, 0))
    col = pl.BlockSpec((STACK, 1), lambda b, n: (0, 0))
    bias = pl.BlockSpec((None, STACK, 3 * BLOCK), lambda b, n: (jnp.minimum(n, 2), 0, 0))
    lse_spec = pl.BlockSpec((None, A_HEADS, BLOCK, 1), lambda b, n: (b, 0, n, 0))
    return q_spec, kv(A_K_BLK), kv(A_V_BLK), o_spec, [col, col, bias], lse_spec


def _swa_fwd(name, qkv, slopes, sinks, bias):
    b, l, _ = qkv.shape
    nb = l // BLOCK

    def body(q_ref, k0_ref, kp_ref, kc_ref, v0_ref, vp_ref, vc_ref, sl_ref, sk_ref, bias_ref, o_ref, lse_ref):
        n = pl.program_id(1)
        qs = _stack_heads(q_ref[...], True) * SCALE
        kcat = jnp.concatenate([k0_ref[...], kp_ref[...], kc_ref[...]], axis=0)
        vcat = jnp.concatenate([v0_ref[...], vp_ref[...], vc_ref[...]], axis=0)
        s = _swa_scores(qs, kcat, n, sl_ref[...], bias_ref[...])
        sink = sk_ref[...]
        m = jnp.maximum(jnp.max(s, axis=-1, keepdims=True), sink)
        p = jnp.exp(s - m)
        den = jnp.sum(p, axis=-1, keepdims=True) + jnp.exp(sink - m)
        o = lax.dot_general(p.astype(BF16), vcat, _NN, preferred_element_type=F32) / den
        o_ref[...] = _unstack_heads(o)
        lse_ref[...] = (m + jnp.log(den)).reshape(A_HEADS, BLOCK, 1)

    q_spec, k_specs, v_specs, o_spec, consts, lse_spec = _swa_specs()
    return pl.pallas_call(
        body, name=name, grid=(b, nb),
        in_specs=[q_spec] + k_specs + v_specs + consts, out_specs=[o_spec, lse_spec],
        out_shape=[jax.ShapeDtypeStruct((b, l, A_WIDTH), F32), jax.ShapeDtypeStruct((b, A_HEADS, l, 1), F32)],
        compiler_params=_params(("parallel", "parallel")),
    )(qkv, qkv, qkv, qkv, qkv, qkv, qkv, slopes, sinks, bias)


def _swa_bwd(name, qkv, o, lse, do, slopes, sinks, bias, dqkv):
    b, l, _ = qkv.shape
    nb = l // BLOCK

    def body(q_ref, k0_ref, kp_ref, kc_ref, v0_ref, vp_ref, vc_ref, o_ref, lse_ref, do_ref, sl_ref, sk_ref, bias_ref, _,
             dx_ref, ds_ref, dk_acc, dv_acc):
        bi = pl.program_id(0)
        n = pl.program_id(1)
        qs = _stack_heads(q_ref[...], True) * SCALE
        dos32 = _stack_heads(do_ref[...], True)
        dos = dos32.astype(BF16)
        os_ = _stack_heads(o_ref[...], False)
        lsev = lse_ref[...].reshape(STACK, 1)
        kcat = jnp.concatenate([k0_ref[...], kp_ref[...], kc_ref[...]], axis=0)
        vcat = jnp.concatenate([v0_ref[...], vp_ref[...], vc_ref[...]], axis=0)
        s = _swa_scores(qs, kcat, n, sl_ref[...], bias_ref[...])
        p = jnp.exp(s - lsev)
        dsum = jnp.sum(dos32 * os_, axis=-1, keepdims=True)
        dp = lax.dot_general(dos, vcat, _NT, preferred_element_type=F32)
        dsc = (p * (dp - dsum)).astype(BF16)
        dq = lax.dot_general(dsc, kcat, _NN, preferred_element_type=F32) * SCALE
        row0 = pl.multiple_of(n * BLOCK, BLOCK)
        dx_ref[pl.ds(row0, BLOCK), 0:A_WIDTH] = _unstack_heads(dq).astype(BF16)
        dkc = lax.dot_general(dsc, qs, _TN, preferred_element_type=F32)
        dvc = lax.dot_general(p.astype(BF16), dos, _TN, preferred_element_type=F32)

        @pl.when(n == 0)
        def _():
            dk_acc[...] = jnp.zeros_like(dk_acc)
            dv_acc[...] = jnp.zeros_like(dv_acc)

        starts = (0, pl.multiple_of(jnp.maximum(n - 1, 0) * BLOCK, BLOCK), row0)
        for t, st in enumerate(starts):
            dk_acc[pl.ds(st, BLOCK), :] += dkc[t * BLOCK:(t + 1) * BLOCK, :]
            dv_acc[pl.ds(st, BLOCK), :] += dvc[t * BLOCK:(t + 1) * BLOCK, :]

        @pl.when(n == nb - 1)
        def _():
            dx_ref[:, A_WIDTH:A_WIDTH + 128] = dk_acc[...].astype(BF16)
            dx_ref[:, A_WIDTH + 128:A_SEG] = dv_acc[...].astype(BF16)

        dsink = -(jnp.exp(sk_ref[...] - lsev) * dsum)
        r8 = lax.broadcasted_iota(jnp.int32, (8, 128), 0)
        acc = jnp.zeros((8, 128), F32)
        for hh in range(A_HEADS):
            acc = acc + jnp.where(r8 == hh, jnp.sum(dsink[hh * BLOCK:(hh + 1) * BLOCK, :]), 0.0)

        @pl.when((bi == 0) & (n == 0))
        def _():
            ds_ref[...] = jnp.zeros_like(ds_ref)

        ds_ref[...] += acc

    q_spec, k_specs, v_specs, o_spec, consts, lse_spec = _swa_specs()
    return pl.pallas_call(
        body, name=name, grid=(b, nb),
        in_specs=[q_spec] + k_specs + v_specs + [o_spec, lse_spec, o_spec] + consts + [pl.BlockSpec(memory_space=pl.ANY)],
        out_specs=[pl.BlockSpec((None, l, A_SEG), lambda bb, n: (bb, 0, A_SEG_BLK)),
                   pl.BlockSpec((8, 128), lambda bb, n: (0, 0))],
        out_shape=[jax.ShapeDtypeStruct(dqkv.shape, BF16), jax.ShapeDtypeStruct((8, 128), F32)],
        scratch_shapes=[pltpu.VMEM((l, 128), F32), pltpu.VMEM((l, 128), F32)],
        input_output_aliases={13: 0},
        compiler_params=_params(("arbitrary", "arbitrary")),
    )(qkv, qkv, qkv, qkv, qkv, qkv, qkv, o, lse, do, slopes, sinks, bias, dqkv)


def _fox_mask(qk, ck, i):
    kh = qk.shape[1]
    qpos = i * BLOCK + lax.broadcasted_iota(jnp.int32, (BLOCK, kh), 0)
    kpos = lax.broadcasted_iota(jnp.int32, (BLOCK, kh), 1)
    return jnp.where((kpos <= qpos) & (kpos >= N_PAD), qk - ck, NEG)


def _pick_head(x, hh):
    lo = _lane_lo()
    return jnp.where(lo if hh == 0 else jnp.logical_not(lo), x, jnp.zeros_like(x))


def _both_heads(x):
    return jnp.concatenate([_pick_head(x, 0), _pick_head(x, 1)], axis=0)


def _fox_specs(l):
    pair = pl.BlockSpec((None, l, PAIR_W), lambda bi, hp: (bi, 0, hp))
    half = pl.BlockSpec((None, l, 128), lambda bi, hp: (bi, 0, hp))
    colv = pl.BlockSpec((None, 2, l, 1), lambda bi, hp: (bi, hp, 0, 0))
    rowv = pl.BlockSpec((None, 2, 1, l), lambda bi, hp: (bi, hp, 0, 0))
    return pair, half, colv, rowv


def _fox_fwd(name, qkv, c_col, c_row):
    b, l, _ = qkv.shape
    nb = l // BLOCK

    def body(x_ref, cc_ref, cr_ref, o_ref, lse_ref):
        for i in range(nb):
            rows = slice(i * BLOCK, (i + 1) * BLOCK)
            kh = (i + 1) * BLOCK
            qblk = x_ref[rows, 0:128]
            kv = x_ref[0:kh, 128:256]
            vv = x_ref[0:kh, 256:384]
            qk = lax.dot_general(_both_heads(qblk) * SCALE, kv, _NT, preferred_element_type=F32)
            ps, dens = [], []
            for hh in range(2):
                s = _fox_mask(qk[hh * BLOCK:(hh + 1) * BLOCK], cr_ref[hh, :, 0:kh], i)
                m = jnp.max(s, axis=-1, keepdims=True)
                p = jnp.exp(s - m)
                den = jnp.sum(p, axis=-1, keepdims=True)
                ps.append(p.astype(BF16))
                dens.append(den)
                lse_ref[hh, rows, :] = (m + jnp.log(den)) + cc_ref[hh, rows, :]
            pv = lax.dot_general(jnp.concatenate(ps, axis=0), vv, _NN, preferred_element_type=F32)
            o_ref[rows, :] = jnp.where(_lane_lo(), pv[0:BLOCK] / dens[0], pv[BLOCK:2 * BLOCK] / dens[1]).astype(BF16)

    pair, half, colv, rowv = _fox_specs(l)
    return pl.pallas_call(
        body, name=name, grid=(b, 4), in_specs=[pair, colv, rowv], out_specs=[half, colv],
        out_shape=[jax.ShapeDtypeStruct((b, l, B_WIDTH), BF16), jax.ShapeDtypeStruct((b, B_HEADS, l, 1), F32)],
        compiler_params=_params(("parallel", "parallel")),
    )(qkv, c_col, c_row)


def _fox_bwd(name, qkv, c_col, c_row, o, lse, do):
    b, l, _ = qkv.shape
    nb = l // BLOCK

    def body(x_ref, cc_ref, cr_ref, o_ref, lse_ref, do_ref, dx_ref, dcq_ref, dck_ref, dk_acc, dv_acc):
        dk_acc[...] = jnp.zeros_like(dk_acc)
        dv_acc[...] = jnp.zeros_like(dv_acc)
        dck_ref[...] = jnp.zeros_like(dck_ref)
        for i in range(nb):
            rows = slice(i * BLOCK, (i + 1) * BLOCK)
            kh = (i + 1) * BLOCK
            qblk = x_ref[rows, 0:128]
            kv = x_ref[0:kh, 128:256]
            vv = x_ref[0:kh, 256:384]
            doblk = do_ref[rows, :]
            ov = o_ref[rows, :].astype(F32)
            q2 = _both_heads(qblk) * SCALE
            do2 = _both_heads(doblk)
            qk = lax.dot_general(q2, kv, _NT, preferred_element_type=F32)
            dp = lax.dot_general(do2, vv, _NT, preferred_element_type=F32)
            ps, dss = [], []
            for hh in range(2):
                half = slice(hh * BLOCK, (hh + 1) * BLOCK)
                s = _fox_mask(qk[half], cr_ref[hh, :, 0:kh], i)
                p = jnp.exp(s - (lse_ref[hh, rows, :] - cc_ref[hh, rows, :]))
                dsum = jnp.sum(do2[half].astype(F32) * ov, axis=-1, keepdims=True)
                ds = p * (dp[half] - dsum)
                ps.append(p.astype(BF16))
                dss.append(ds.astype(BF16))
                dcq_ref[hh, rows, :] = jnp.sum(ds, axis=-1, keepdims=True)
                dck_ref[hh, :, 0:kh] -= jnp.sum(ds, axis=0, keepdims=True)
            p2 = jnp.concatenate(ps, axis=0)
            ds2 = jnp.concatenate(dss, axis=0)
            dq = lax.dot_general(ds2, kv, _NN, preferred_element_type=F32) * SCALE
            dk_acc[0:kh, :] += lax.dot_general(ds2, q2, _TN, preferred_element_type=F32)
            dv_acc[0:kh, :] += lax.dot_general(p2, do2, _TN, preferred_element_type=F32)
            dx_ref[rows, 0:128] = jnp.where(_lane_lo(), dq[0:BLOCK], dq[BLOCK:2 * BLOCK]).astype(BF16)
        dx_ref[:, 128:256] = dk_acc[...].astype(BF16)
        dx_ref[:, 256:384] = dv_acc[...].astype(BF16)

    pair, half, colv, rowv = _fox_specs(l)
    return pl.pallas_call(
        body, name=name, grid=(b, 4), in_specs=[pair, colv, rowv, half, colv, half],
        out_specs=[pair, colv, rowv],
        out_shape=[jax.ShapeDtypeStruct(qkv.shape, BF16), jax.ShapeDtypeStruct((b, B_HEADS, l, 1), F32),
                   jax.ShapeDtypeStruct((b, B_HEADS, 1, l), F32)],
        scratch_shapes=[pltpu.VMEM((l, 128), F32), pltpu.VMEM((l, 128), F32)],
        compiler_params=_params(("parallel", "parallel")),
    )(qkv, c_col, c_row, o, lse, do)


_FLIPS = ((0, 0, 1), (0, 1, 0), (0, 1, 1), (1, 0, 0), (1, 0, 1), (1, 1, 0), (1, 1, 1))


def _exchange(name, gather, scatter):
    ng, ns = len(gather), len(scatter)
    na = ng + ns
    npeer = len(_FLIPS)

    def body(*refs):
        ins = refs[:na]
        outs = refs[na:2 * na]
        send_sems, recv_sems, loc_sems = refs[2 * na:]
        x, y, c = lax.axis_index("x"), lax.axis_index("y"), lax.axis_index("c")
        me = 4 * x + 2 * y + c
        peers = []
        for fx, fy, fc in _FLIPS:
            px = 1 - x if fx else x
            py = 1 - y if fy else y
            pc = 1 - c if fc else c
            peers.append(((px, py, pc), 4 * px + 2 * py + pc))

        def remote(a, kk):
            dev, lin = peers[kk]
            src = ins[a] if a < ng else ins[a].at[lin]
            return pltpu.make_async_remote_copy(src_ref=src, dst_ref=outs[a].at[me], send_sem=send_sems.at[a * npeer + kk],
                                                recv_sem=recv_sems.at[a * npeer + kk], device_id=dev, device_id_type=MESH_ID)

        def arrival(a, kk):
            dev, lin = peers[kk]
            src = ins[a] if a < ng else ins[a].at[lin]
            return pltpu.make_async_remote_copy(src_ref=src, dst_ref=outs[a].at[lin], send_sem=send_sems.at[a * npeer + kk],
                                                recv_sem=recv_sems.at[a * npeer + kk], device_id=dev, device_id_type=MESH_ID)

        local = []
        for a in range(na):
            src = ins[a] if a < ng else ins[a].at[me]
            cp = pltpu.make_async_copy(src, outs[a].at[me], loc_sems.at[a])
            cp.start()
            local.append(cp)
        sent = [remote(a, kk) for kk in range(npeer) for a in range(na)]
        for cp in sent:
            cp.start()
        for kk in range(npeer):
            for a in range(na):
                arrival(a, kk).wait_recv()
        for cp in sent:
            cp.wait_send()
        for cp in local:
            cp.wait()

    arrs = list(gather) + list(scatter)
    out_shape = [jax.ShapeDtypeStruct((N_DEV,) + tuple(a.shape), a.dtype) for a in gather]
    out_shape += [jax.ShapeDtypeStruct(tuple(a.shape), a.dtype) for a in scatter]
    anyspec = pl.BlockSpec(memory_space=pl.ANY)
    return pl.pallas_call(
        body, name=name, in_specs=[anyspec] * na, out_specs=[anyspec] * na, out_shape=out_shape,
        scratch_shapes=[pltpu.SemaphoreType.DMA((na * npeer,)), pltpu.SemaphoreType.DMA((na * npeer,)),
                        pltpu.SemaphoreType.DMA((na,))],
        compiler_params=pltpu.CompilerParams(has_side_effects=True),
    )(*arrs)


def _peer_table():
    x, y, c = lax.axis_index("x"), lax.axis_index("y"), lax.axis_index("c")
    me = 4 * x + 2 * y + c
    peers = []
    for fx, fy, fc in _FLIPS:
        px = 1 - x if fx else x
        py = 1 - y if fy else y
        pc = 1 - c if fc else c
        peers.append(((px, py, pc), 4 * px + 2 * py + pc))
    return me, peers


_HBM = pl.BlockSpec(memory_space=pltpu.HBM)
_SEM = pl.BlockSpec(memory_space=pltpu.SEMAPHORE)
_ANY = pl.BlockSpec(memory_space=pl.ANY)
_EFFECT = pltpu.SideEffectType.DATAFLOW_SIDE_EFFECTING


def _split_copy(srcs_are_pieces, src_refs, land_refs, send_sem, recv_sem, a, kk, me, peers, arriving):
    dev, lin = peers[kk]
    npeer = len(_FLIPS)
    src = src_refs[a] if srcs_are_pieces[a] else src_refs[a].at[lin]
    dst = land_refs[a].at[lin] if arriving else land_refs[a].at[me]
    return pltpu.make_async_remote_copy(src_ref=src, dst_ref=dst, send_sem=send_sem.at[a * npeer + kk],
                                        recv_sem=recv_sem.at[a * npeer + kk], device_id=dev, device_id_type=MESH_ID)


def _xchg_start(name, gather, scatter, after=None):
    me_out = 4 * lax.axis_index("x") + 2 * lax.axis_index("y") + lax.axis_index("c")
    srcs = list(gather) + list(scatter)
    is_piece = [True] * len(gather) + [False] * len(scatter)
    lands = []
    for a, piece in zip(srcs, is_piece):
        own = a[None] if piece else lax.dynamic_slice_in_dim(a, me_out, 1, axis=0)
        shape = ((N_DEV,) + tuple(a.shape)) if piece else tuple(a.shape)
        start = (me_out,) + (0,) * (len(shape) - 1)
        lands.append(lax.dynamic_update_slice(lax.empty(shape, a.dtype), own, start))
    n = len(srcs)
    nsem = n * len(_FLIPS)
    has_after = after is not None

    def body(*refs):
        src_refs = refs[:n]
        land_refs = refs[n:2 * n]
        outs = refs[2 * n + (1 if has_after else 0):]
        send_sem, recv_sem = outs[0], outs[1]
        token = outs[-1]
        me, peers = _peer_table()
        for kk in range(len(_FLIPS)):
            for a in range(n):
                _split_copy(is_piece, src_refs, land_refs, send_sem, recv_sem, a, kk, me, peers, False).start()
        token[...] = jnp.zeros_like(token)

    out_shape = ([pltpu.SemaphoreType.DMA((nsem,)), pltpu.SemaphoreType.DMA((nsem,))]
                 + [pltpu.HBM(tuple(a.shape), a.dtype) for a in srcs] + [pltpu.HBM(tuple(a.shape), a.dtype) for a in lands]
                 + [jax.ShapeDtypeStruct((8, 128), F32)])
    args = [pltpu.with_memory_space_constraint(a, pltpu.HBM) for a in srcs + lands] + ([after] if has_after else [])
    res = pl.pallas_call(
        body, name=name, out_shape=out_shape,
        in_specs=[_HBM] * (2 * n) + ([_ANY] if has_after else []),
        out_specs=[_SEM, _SEM] + [_HBM] * (2 * n) + [pl.BlockSpec(memory_space=pltpu.VMEM)],
        input_output_aliases={i: 2 + i for i in range(2 * n)},
        compiler_params=pltpu.CompilerParams(has_side_effects=_EFFECT),
    )(*args)
    state = (res[0], res[1], list(res[2:2 + n]), list(res[2 + n:2 + 2 * n]), is_piece)
    return state, res[-1]


def _xchg_wait(name, state, after):
    send_sem, recv_sem, srcs, lands, is_piece = state
    n = len(srcs)

    def body(*refs):
        src_refs = refs[:n]
        land_refs = refs[n:2 * n]
        s_sem, r_sem = refs[2 * n], refs[2 * n + 1]
        me, peers = _peer_table()
        for kk in range(len(_FLIPS)):
            for a in range(n):
                cp = _split_copy(is_piece, src_refs, land_refs, s_sem, r_sem, a, kk, me, peers, True)
                cp.wait_send()
                cp.wait_recv()

    out_shape = [pltpu.HBM(tuple(a.shape), a.dtype) for a in srcs] + [pltpu.HBM(tuple(a.shape), a.dtype) for a in lands]
    res = pl.pallas_call(
        body, name=name, out_shape=out_shape,
        in_specs=[_HBM] * (2 * n) + [_SEM, _SEM, _ANY], out_specs=[_HBM] * (2 * n),
        input_output_aliases={i: i for i in range(2 * n)},
        compiler_params=pltpu.CompilerParams(has_side_effects=_EFFECT),
    )(*srcs, *lands, send_sem, recv_sem, after)
    return list(res[n:])


_SIB = (0, 0, 1)
_ICI = ((0, 1, 0), (1, 0, 0), (1, 1, 0))


def _flip(fl):
    x, y, c = lax.axis_index("x"), lax.axis_index("y"), lax.axis_index("c")
    px = 1 - x if fl[0] else x
    py = 1 - y if fl[1] else y
    pc = 1 - c if fl[2] else c
    return (px, py, pc), 4 * px + 2 * py + pc


def _gather2_start(name, pieces, after=None):
    me_out = 4 * lax.axis_index("x") + 2 * lax.axis_index("y") + lax.axis_index("c")
    pieces = list(pieces)
    n = len(pieces)
    lands = [lax.dynamic_update_slice(lax.empty((N_DEV,) + tuple(a.shape), a.dtype), a[None],
                                      (me_out,) + (0,) * a.ndim) for a in pieces]
    first = (_SIB,) + _ICI
    has_after = after is not None

    def body(*refs):
        src_refs, land_refs = refs[:n], refs[n:2 * n]
        outs = refs[2 * n + (1 if has_after else 0):]
        send_sem, recv_sem, token = outs[0], outs[1], outs[-1]
        _, me = _flip((0, 0, 0))
        for kk, fl in enumerate(first):
            dev, _ = _flip(fl)
            for a in range(n):
                pltpu.make_async_remote_copy(src_ref=src_refs[a], dst_ref=land_refs[a].at[me],
                                             send_sem=send_sem.at[a * 4 + kk], recv_sem=recv_sem.at[a * 4 + kk],
                                             device_id=dev, device_id_type=MESH_ID).start()
        token[...] = jnp.zeros_like(token)

    hbm = [pltpu.HBM(tuple(a.shape), a.dtype) for a in pieces + lands]
    res = pl.pallas_call(
        body, name=name,
        out_shape=[pltpu.SemaphoreType.DMA((4 * n,)), pltpu.SemaphoreType.DMA((4 * n,))] + hbm
        + [jax.ShapeDtypeStruct((8, 128), F32)],
        in_specs=[_HBM] * (2 * n) + ([_ANY] if has_after else []),
        out_specs=[_SEM, _SEM] + [_HBM] * (2 * n) + [pl.BlockSpec(memory_space=pltpu.VMEM)],
        input_output_aliases={i: 2 + i for i in range(2 * n)},
        compiler_params=pltpu.CompilerParams(has_side_effects=_EFFECT),
    )(*([pltpu.with_memory_space_constraint(a, pltpu.HBM) for a in pieces + lands] + ([after] if has_after else [])))
    return (res[0], res[1], list(res[2:2 + n]), list(res[2 + n:2 + 2 * n])), res[-1]


def _gather2_forward(name, state, after):
    send_a, recv_a, pieces, lands = state
    n = len(pieces)
    first = (_SIB,) + _ICI

    def body(*refs):
        src_refs, land_refs = refs[:n], refs[n:2 * n]
        s_a, r_a = refs[2 * n], refs[2 * n + 1]
        outs = refs[2 * n + 3:]
        send_b, recv_b, token = outs[0], outs[1], outs[-1]
        for kk, fl in enumerate(first):
            dev, lin = _flip(fl)
            for a in range(n):
                cp = pltpu.make_async_remote_copy(src_ref=src_refs[a], dst_ref=land_refs[a].at[lin],
                                                  send_sem=s_a.at[a * 4 + kk], recv_sem=r_a.at[a * 4 + kk],
                                                  device_id=dev, device_id_type=MESH_ID)
                cp.wait_send()
                cp.wait_recv()
        sib, _ = _flip(_SIB)
        for j, fl in enumerate(_ICI):
            _, lin = _flip(fl)
            for a in range(n):
                pltpu.make_async_remote_copy(src_ref=land_refs[a].at[lin], dst_ref=land_refs[a].at[lin],
                                             send_sem=send_b.at[a * 3 + j], recv_sem=recv_b.at[a * 3 + j],
                                             device_id=sib, device_id_type=MESH_ID).start()
        token[...] = jnp.zeros_like(token)

    hbm = [pltpu.HBM(tuple(a.shape), a.dtype) for a in pieces + lands]
    res = pl.pallas_call(
        body, name=name,
        out_shape=[pltpu.SemaphoreType.DMA((3 * n,)), pltpu.SemaphoreType.DMA((3 * n,))] + hbm
        + [jax.ShapeDtypeStruct((8, 128), F32)],
        in_specs=[_HBM] * (2 * n) + [_SEM, _SEM, _ANY],
        out_specs=[_SEM, _SEM] + [_HBM] * (2 * n) + [pl.BlockSpec(memory_space=pltpu.VMEM)],
        input_output_aliases={i: 2 + i for i in range(2 * n)},
        compiler_params=pltpu.CompilerParams(has_side_effects=_EFFECT),
    )(*pieces, *lands, send_a, recv_a, after)
    return (res[0], res[1], list(res[2 + n:2 + 2 * n])), res[-1]


def _gather2_wait(name, state, after):
    send_b, recv_b, lands = state
    n = len(lands)

    def body(*refs):
        land_refs = refs[:n]
        s_b, r_b = refs[n], refs[n + 1]
        sib, _ = _flip(_SIB)
        for j, fl in enumerate(_ICI):
            _, sent = _flip(fl)
            _, arriving = _flip((fl[0], fl[1], 1))
            for a in range(n):
                cp = pltpu.make_async_remote_copy(src_ref=land_refs[a].at[sent], dst_ref=land_refs[a].at[arriving],
                                                  send_sem=s_b.at[a * 3 + j], recv_sem=r_b.at[a * 3 + j],
                                                  device_id=sib, device_id_type=MESH_ID)
                cp.wait_send()
                cp.wait_recv()

    res = pl.pallas_call(
        body, name=name, out_shape=[pltpu.HBM(tuple(a.shape), a.dtype) for a in lands],
        in_specs=[_HBM] * n + [_SEM, _SEM, _ANY], out_specs=[_HBM] * n,
        input_output_aliases={i: i for i in range(n)},
        compiler_params=pltpu.CompilerParams(has_side_effects=_EFFECT),
    )(*lands, send_b, recv_b, after)
    return list(res)


def _adam_math(w, g, m, v):
    m = ADAM_B1 * m + (1.0 - ADAM_B1) * g
    v = ADAM_B2 * v + (1.0 - ADAM_B2) * (g * g)
    m_hat = m / (1.0 - ADAM_B1 ** ADAM_STEP)
    v_hat = v / (1.0 - ADAM_B2 ** ADAM_STEP)
    delta = -ADAM_LR * (m_hat / (jnp.sqrt(v_hat) + ADAM_EPS) + ADAM_WD * w)
    return delta, m, v


def _adam(name, w, m, v, parts, transposed=False, dep=None):
    npart, _, cp = parts.shape
    has_dep = dep is not None
    if transposed:
        c, r = w.shape
        tr = _pick(r, (256, 128))
        blk = pl.BlockSpec((c, tr), lambda i: (0, i))
    else:
        r, c = w.shape
        tr = _pick(r, (256, 176, 128, 64, 16, 8, 1))
        blk = pl.BlockSpec((tr, c), lambda i: (i, 0))

    def body(w_ref, m_ref, v_ref, p_ref, *rest):
        g_ref, d_ref, mo_ref, vo_ref = rest[-4:]
        g = p_ref[0].astype(F32)
        for pp in range(1, npart):
            g = g + p_ref[pp].astype(F32)
        g = g.T[0:c, :] if transposed else g[:, 0:c]
        delta, mn, vn = _adam_math(w_ref[...], g, m_ref[...], v_ref[...])
        g_ref[...] = g
        d_ref[...] = delta
        mo_ref[...] = mn
        vo_ref[...] = vn

    out = jax.ShapeDtypeStruct(w.shape, F32)
    return pl.pallas_call(
        body, name=name, grid=(r // tr,),
        in_specs=[blk, blk, blk, pl.BlockSpec((npart, tr, cp), lambda i: (0, i, 0))]
        + ([pl.BlockSpec(memory_space=pl.ANY)] if has_dep else []),
        out_specs=[blk, blk, blk, blk], out_shape=[out, out, out, out], compiler_params=_params(("parallel",)),
    )(*((w, m, v, parts) + ((dep,) if has_dep else ())))


def _small_sum(name, packs):
    def body(p_ref, o_ref):
        tot = p_ref[0]
        for pp in range(1, N_DEV):
            tot = tot + p_ref[pp]
        o_ref[0:8, :] = tot[0:8, :]
        o_ref[8:24, :] = tot[8:24, :] + tot[24:40, :]

    return pl.pallas_call(body, name=name, out_shape=jax.ShapeDtypeStruct((24, D_MODEL), F32),
                          compiler_params=_params())(packs)


def _local_step(x, tgt, g1, gm, g2, gf, b_forget, sinks, weights, send):
    b, s, _ = x.shape
    l = s + PREFIX
    t = b * l
    (meta,) = weights("meta", x)

    h0, n1 = _embed_norm("embed_rms1_fwd", x, meta, g1)
    (w1i,) = weights("ffn1_in", n1)
    gu1, a1 = _ffn_in_fwd("ffn1_in_fwd", n1, w1i)
    (w1o,) = weights("ffn1_out", weights("ffn1_out:forward", a1))
    h1, um = _mm_res_norm("ffn1_out_fwd", a1, w1o, h0, gm, alpha=0.5)
    wi, wa, wb, wo = weights("mix", weights("mix:forward", um))
    qkv, gates, f2 = _proj_fwd("proj_fwd", um, wi)
    qkv3 = qkv.reshape(b, l, QKV_W)
    f3 = f2.reshape(b, l, 128)
    bf_row = jnp.pad(b_forget, ((0, 0), (0, 128 - B_HEADS)))
    c_col, c_row = _fgate_fwd("fgate_fwd", f3, bf_row)
    head_of_row = jnp.arange(STACK) // BLOCK
    slopes = jnp.exp2(-8.0 * (head_of_row + 1).astype(F32) / A_HEADS).reshape(STACK, 1)
    sink_rows = jnp.repeat(sinks.reshape(A_HEADS), BLOCK).reshape(STACK, 1)
    swa_bias = _swa_bias(slopes)
    oa3, lse_a = _swa_fwd("swa_fwd", qkv3, slopes, sink_rows, swa_bias)
    ob3, lse_b = _fox_fwd("fox_fwd", qkv3, c_col, c_row)
    oa = oa3.reshape(t, A_WIDTH)
    ob = ob3.reshape(t, B_WIDTH)
    mixed, ya, yb = _branch_gate_fwd("branch_gate_fwd", oa, ob, wa, wb, gates, dep=weights("ffn2:forward", ob))
    h2, n2 = _mm_res_norm("mix_out_fwd", mixed, wo, h1, g2)
    w2i, w2o = weights("ffn2", h2)
    gu2, a2 = _ffn_in_fwd("ffn2_in_fwd", n2, w2i)
    h3 = _mm_nn("ffn2_out_fwd", a2, w2o, alpha=0.5, res=h2, tm_c=(544, 384, 256, 128), tn_c=(1024,))

    dh3_3, loss_blk, dgf = _loss_head("loss_head", h3.reshape(b, l, D_MODEL), gf, tgt)
    dh3 = dh3_3.reshape(t, D_MODEL)

    def ffn_bwd(tag, dh, h_in, g_norm, n_in, gu, a, w_in_blk, w_out, one_send, examples=None):
        dw_out = _mm_tn(tag + "_out_bwd_w", a, dh, alpha=0.5)
        dgu = _ffn_out_bwd_x(tag + "_out_bwd_x", dh, w_out, gu, dep=None if one_send else send(tag + "_out", (dw_out,)))
        dw_in = _ffn_in_bwd_w(tag + "_in_bwd_w", n_in, dgu)
        token = send(tag, (dw_in, dw_out)) if one_send else send(tag + "_in", (dw_in,))
        return _ffn_in_bwd_x(tag + "_in_bwd_x", dgu, w_in_blk, h_in, g_norm, dh, dep=token, examples=examples)

    dh2, dg2 = ffn_bwd("ffn2", dh3, h2, g2, n2, gu2, a2, w2i, w2o, True)

    dwo = _mm_tn("mix_out_bwd_w", mixed, dh2)
    dya, dyb, dgates = _mix_out_gate_bwd("mix_out_gate_bwd", dh2, wo, gates, ya, yb)
    doa, dob = _branch_bwd_x("branch_bwd_x", dya, dyb, wa, wb)
    dwa, dwb = _branch_bwd_w("branch_bwd_w", oa, ob, dya, dyb)
    dqkv3, dcq, dck = _fox_bwd("fox_bwd", qkv3, c_col, c_row, ob3, lse_b, dob.reshape(b, l, B_WIDTH))
    dqkv3, dsink = _swa_bwd("swa_bwd", qkv3, oa3, lse_a, doa.reshape(b, l, A_WIDTH), slopes, sink_rows, swa_bias, dqkv3)
    dqkv = dqkv3.reshape(t, QKV_W)
    df3, dbf = _fgate_bwd("fgate_bwd", f3, bf_row, dcq, dck)
    df = df3.reshape(t, 128)
    dwi_qkv = _mm_tn("proj_qkv_bwd_w", um, dqkv, tm_c=(512,), tn_c=(768,))
    dwi_g = _mm_tn("proj_gates_bwd_w", um, dgates, tm_c=(512,), tn_c=(512,))
    dwi_f = _mm_tn("proj_f_bwd_w", um, df, tm_c=(512,), tn_c=(128,))
    token = send("mix", (dwi_qkv, dwi_g, dwi_f, dwa, dwb, dwo))
    dh1, dgm = _proj_bwd_x("proj_bwd_x", dqkv, dgates, df, wi, h1, gm, dh2, dep=token)

    grad_x, dmeta3, dg1 = ffn_bwd("ffn1", dh1, h0, g1, n1, gu1, a1, w1i, w1o, False, examples=b)
    dmeta = dmeta3.reshape(b * N_META, D_MODEL)

    misc = jnp.concatenate([dbf[:, 0:B_HEADS], dsink[:, 0].reshape(1, A_HEADS), loss_blk[0:1, 0:1]], axis=1)
    misc = jnp.pad(misc, ((0, 0), (0, D_MODEL - misc.shape[1])))
    row = lax.broadcasted_iota(jnp.int32, (8, D_MODEL), 0)
    vec = jnp.zeros((8, D_MODEL), F32)
    for i, piece in enumerate((dg1, dgm, dg2, dgf, misc)):
        vec = jnp.where(row == i, piece, vec)
    small = jnp.concatenate([vec, dmeta], axis=0)
    return grad_x, small


def _pad_to(a, rows, cols):
    return jnp.pad(a, ((0, rows - a.shape[0]), (0, cols - a.shape[1])))


def _ffn_out_from_gathered(g):
    w = g.reshape(4, FF_SHARD, D_MODEL)
    return jnp.pad(w, ((0, 0), (0, FF_SHARD_P - FF_SHARD), (0, 0))).reshape(D_FF_P, D_MODEL)


def _ffn_out_to_scatter(dw):
    return dw.reshape(4, FF_SHARD_P, D_MODEL)[:, 0:FF_SHARD, :].reshape(N_DEV, FFO_SHARD, D_MODEL)


def _proj_segments():
    segs = [(HEAD_DIM * h, HEAD_DIM, 0, B_SEG + HEAD_DIM * A_HEAD_ORDER.index(h)) for h in range(A_HEADS)]
    segs += [(512, 128, 0, B_SEG + A_WIDTH), (640, 128, 0, B_SEG + A_WIDTH + 128)]
    for first, off in ((768, 0), (1280, 128), (1792, 256)):
        segs += [(first + 128 * hp, 128, 0, PAIR_W * hp + off) for hp in range(4)]
    segs += [(2304, B_HEADS, 2, 0), (2312, 2 * D_MODEL, 1, 0)]
    return segs


def _proj_from_gathered(g):
    def cols(first, width):
        out = []
        for p in range(N_DEV):
            lo, hi = max(first, WIN_SHARD * p), min(first + width, WIN_SHARD * (p + 1))
            if lo < hi:
                out.append(g[p, :, lo - WIN_SHARD * p:hi - WIN_SHARD * p])
        return out

    parts = []
    for arr in (0, 1, 2):
        for first, width, _, _ in sorted((s for s in _proj_segments() if s[2] == arr), key=lambda s: s[3]):
            parts += cols(first, width)
        if arr == 0:
            parts.append(jnp.zeros((D_MODEL, P_GATES - QKV_W), g.dtype))
    parts.append(jnp.zeros((D_MODEL, 128 - B_HEADS), g.dtype))
    return jnp.concatenate(parts, axis=1)


def _proj_to_scatter(dqkv_w, dg_w, df_w):
    arrays = (dqkv_w, dg_w, df_w)
    segs = sorted(_proj_segments())
    blocks = []
    for p in range(N_DEV):
        parts = []
        for first, width, arr, at in segs:
            lo, hi = max(first, WIN_SHARD * p), min(first + width, WIN_SHARD * (p + 1))
            if lo < hi:
                parts.append(arrays[arr][:, at + lo - first:at + hi - first])
        parts.append(jnp.zeros((D_MODEL, WIN_SHARD_P - WIN_SHARD), dqkv_w.dtype))
        blocks.append(jnp.concatenate(parts, axis=1))
    return jnp.stack(blocks, axis=0)


def _a_rows_from_natural(w):
    return jnp.concatenate([w[HEAD_DIM * h:HEAD_DIM * (h + 1)] for h in A_HEAD_ORDER], axis=0)


def _a_rows_to_natural(w):
    return jnp.concatenate([w[HEAD_DIM * A_HEAD_ORDER.index(h):HEAD_DIM * (A_HEAD_ORDER.index(h) + 1)]
                            for h in range(A_HEADS)], axis=0)


def kernel(x, meta_tokens, ffn1_norm, ffn1_w_in, ffn1_w_out, mix_norm, w_in, b_forget, attn_sinks, w_branch_a, w_branch_b, w_out, ffn2_norm, ffn2_w_in, ffn2_w_out, final_norm, loss_target, m_meta_tokens, m_ffn1_norm, m_ffn1_w_in, m_ffn1_w_out, m_mix_norm, m_w_in, m_b_forget, m_attn_sinks, m_w_branch_a, m_w_branch_b, m_w_out, m_ffn2_norm, m_ffn2_w_in, m_ffn2_w_out, m_final_norm, v_meta_tokens, v_ffn1_norm, v_ffn1_w_in, v_ffn1_w_out, v_mix_norm, v_w_in, v_b_forget, v_attn_sinks, v_w_branch_a, v_w_branch_b, v_w_out, v_ffn2_norm, v_ffn2_w_in, v_ffn2_w_out, v_final_norm):
    me = 4 * lax.axis_index("x") + 2 * lax.axis_index("y") + lax.axis_index("c")

    shards = (
        _pad_to(ffn1_w_in[0].astype(BF16), D_MODEL, FF_SHARD_P),
        ffn1_w_out[0].astype(BF16),
        _pad_to(w_in[0].astype(BF16), D_MODEL, WIN_SHARD_P),
        w_branch_a[0].astype(BF16), w_branch_b[0].astype(BF16), w_out[0].astype(BF16),
        _pad_to(ffn2_w_in[0].astype(BF16), D_MODEL, FF_SHARD_P),
        ffn2_w_out[0].astype(BF16),
        meta_tokens,
    )
    s1i, s1o, swi, swa, swb, swo, s2i, s2o, smeta = shards
    first_level, second_level = {}, {}
    first_level["meta"], tok = _gather2_start("gather_meta_start", (smeta,))
    first_level["ffn1_in"], tok = _gather2_start("gather_ffn1_in_start", (s1i,), after=tok)
    first_level["ffn1_out"], tok = _gather2_start("gather_ffn1_out_start", (s1o,), after=tok)
    first_level["mix"], tok = _gather2_start("gather_mix_start", (swi, swa, swb, swo), after=tok)
    first_level["ffn2"], tok = _gather2_start("gather_ffn2_start", (s2i, s2o), after=tok)
    started = {"tok": tok}

    def weights(group, after):
        if group.endswith(":forward"):
            group = group[:-len(":forward")]
            second_level[group], token = _gather2_forward("gather_" + group + "_forward", first_level[group], after)
            return token
        if group == "meta":
            after = weights("meta:forward", started["tok"])
        if group == "ffn1_in":
            after = weights("ffn1_in:forward", after)
        got = _gather2_wait("gather_" + group + "_wait", second_level[group], after)
        if group == "mix":
            gwi, gwa, gwb, gwo = got
            return (_proj_from_gathered(gwi), _a_rows_from_natural(gwa.transpose(1, 0, 2).reshape(A_WIDTH, D_MODEL)),
                    gwb.transpose(1, 0, 2).reshape(B_WIDTH, D_MODEL), gwo.reshape(D_MODEL, D_MODEL))
        if group == "ffn1_out":
            return (_ffn_out_from_gathered(got[0]),)
        if group == "meta":
            return (got[0].transpose(1, 0, 2).reshape(N_META, D_MODEL),)
        if group == "ffn1_in":
            return (got[0].reshape(2, 4, D_MODEL, FF_SHARD_P),)
        return got[0].reshape(2, 4, D_MODEL, FF_SHARD_P), _ffn_out_from_gathered(got[1])

    scatter_state = {}

    def send(group, grads):
        if group == "mix":
            dwi_qkv, dwi_g, dwi_f, dwa, dwb, dwo = grads
            dwa = _a_rows_to_natural(dwa)
            blocks = (_proj_to_scatter(dwi_qkv, dwi_g, dwi_f), dwa.reshape(A_WIDTH, N_DEV, 128).transpose(1, 0, 2),
                      dwb.reshape(B_WIDTH, N_DEV, 128).transpose(1, 0, 2), dwo.reshape(N_DEV, 128, D_MODEL))
        elif group.endswith("_in"):
            blocks = (grads[0].reshape(N_DEV, D_MODEL, FF_SHARD_P),)
        elif group.endswith("_out"):
            blocks = (_ffn_out_to_scatter(grads[0]),)
        else:
            blocks = (grads[0].reshape(N_DEV, D_MODEL, FF_SHARD_P), _ffn_out_to_scatter(grads[1]))
        scatter_state[group], token = _xchg_start("scatter_" + group + "_start", (), blocks)
        return token

    gf = final_norm.reshape(1, D_MODEL)
    grad_x, small = _local_step(x, loss_target, ffn1_norm, mix_norm, ffn2_norm, gf, b_forget, attn_sinks, weights, send)

    small_state, after = _xchg_start("gather_small_start", (small,), ())
    out = {}
    updates = (
        ("ffn2", (("ffn2_w_in", ffn2_w_in, m_ffn2_w_in, v_ffn2_w_in), ("ffn2_w_out", ffn2_w_out, m_ffn2_w_out, v_ffn2_w_out))),
        ("ffn1_out", (("ffn1_w_out", ffn1_w_out, m_ffn1_w_out, v_ffn1_w_out),)),
        ("mix", (("w_in", w_in, m_w_in, v_w_in), ("w_branch_a", w_branch_a, m_w_branch_a, v_w_branch_a),
                 ("w_branch_b", w_branch_b, m_w_branch_b, v_w_branch_b), ("w_out", w_out, m_w_out, v_w_out))),
    )
    last_update = ("ffn1_in", (("ffn1_w_in", ffn1_w_in, m_ffn1_w_in, v_ffn1_w_in),))

    def update(group, members, after):
        parts_list = _xchg_wait("scatter_" + group + "_wait", scatter_state[group], after)
        prev = None
        for (nm, w, m, v), parts in zip(members, parts_list):
            if nm.endswith("w_in"):
                res4 = _adam("adam_" + nm, w[0].T, m[0].T, v[0].T, parts, transposed=True, dep=prev)
                out[nm] = tuple(r.T[None] for r in res4)
            else:
                res4 = _adam("adam_" + nm, w[0], m[0], v[0], parts, dep=prev)
                out[nm] = tuple(r[None] for r in res4)
            prev = res4[0]
        return prev

    for group, members in updates + (last_update,):
        after = update(group, members, after)
    (packs,) = _xchg_wait("gather_small_wait", small_state, after)

    tot = _small_sum("small_sum", packs)
    loss = tot[4, 2 * B_HEADS]
    g_meta = lax.dynamic_slice(tot[8:24, :], (0, me * 128), (N_META, 128))
    out["meta_tokens"] = tuple(_adam("adam_meta_tokens", meta_tokens, m_meta_tokens, v_meta_tokens, g_meta[None]))

    def pack_small(n1, nm, n2, nf, bfv, skv):
        misc = jnp.pad(jnp.concatenate([bfv, skv], axis=1), ((0, 0), (0, D_MODEL - 2 * B_HEADS)))
        row = lax.broadcasted_iota(jnp.int32, (8, D_MODEL), 0)
        vec = jnp.zeros((8, D_MODEL), F32)
        for i, piece in enumerate((n1, nm, n2, nf.reshape(1, D_MODEL), misc)):
            vec = jnp.where(row == i, piece, vec)
        return vec

    w_pack = pack_small(ffn1_norm, mix_norm, ffn2_norm, final_norm, b_forget, attn_sinks)
    m_pack = pack_small(m_ffn1_norm, m_mix_norm, m_ffn2_norm, m_final_norm, m_b_forget, m_attn_sinks)
    v_pack = pack_small(v_ffn1_norm, v_mix_norm, v_ffn2_norm, v_final_norm, v_b_forget, v_attn_sinks)
    small4 = _adam("adam_small", w_pack, m_pack, v_pack, tot[0:8][None])
    for i, nm in enumerate(("ffn1_norm", "mix_norm", "ffn2_norm")):
        out[nm] = tuple(r[i:i + 1] for r in small4)
    out["final_norm"] = tuple(r[3] for r in small4)
    out["b_forget"] = tuple(r[4:5, 0:B_HEADS] for r in small4)
    out["attn_sinks"] = tuple(r[4:5, B_HEADS:2 * B_HEADS] for r in small4)

    names = ("meta_tokens", "ffn1_norm", "ffn1_w_in", "ffn1_w_out", "mix_norm", "w_in", "b_forget", "attn_sinks",
             "w_branch_a", "w_branch_b", "w_out", "ffn2_norm", "ffn2_w_in", "ffn2_w_out", "final_norm")
    return (loss, grad_x) + tuple(out[nm][kind] for kind in range(4) for nm in names)
```

```python
import jax
import jax.numpy as jnp
from jax import lax
from jax.experimental import pallas as pl
from jax.experimental.pallas import tpu as pltpu

F32 = jnp.float32
BF16 = jnp.bfloat16

D_MODEL = 1024
N_META = 16
BLOCK = 128
PREFIX = 128
N_PAD = PREFIX - N_META
HEAD_DIM = 64
A_HEADS = 8
A_KV_HEADS = 2
A_GROUP = 4
B_HEADS = 8
A_WIDTH = 512
A_KV_WIDTH = 128
B_WIDTH = 512
D_FF = 2816
N_DEV = 8
FF_SHARD = 2 * D_FF // N_DEV
FF_SHARD_P = 768
FFO_SHARD = D_FF // N_DEV
FFO_SHARD_P = FF_SHARD_P // 2
D_FF_P = 4 * FF_SHARD_P
W_IN_COLS = 4360
WIN_SHARD = W_IN_COLS // N_DEV
WIN_SHARD_P = 640
PAIR_W = 3 * 128
B_SEG = 4 * PAIR_W
A_SEG = A_WIDTH + 2 * A_KV_WIDTH
QKV_W = B_SEG + A_SEG
P_GATES = 2 * (2 * D_MODEL)
P_F = P_GATES + 2 * D_MODEL
PROJ_P = P_F + 128
A_HEAD_ORDER = (0, 4, 1, 5, 2, 6, 3, 7)
EPS = 1e-6
NEG = -1e30
SCALE = HEAD_DIM ** -0.5
ADAM_LR = 0.001
ADAM_B1 = 0.9
ADAM_B2 = 0.999
ADAM_EPS = 1e-08
ADAM_WD = 0.01
ADAM_STEP = 10
VMEM_LIMIT = 56 * 1024 * 1024
MESH_ID = pl.DeviceIdType.MESH
SMALL_ROWS = 40

_NN = (((1,), (0,)), ((), ()))
_NT = (((1,), (1,)), ((), ()))
_TN = (((0,), (0,)), ((), ()))


def _params(sem=None):
    return pltpu.CompilerParams(dimension_semantics=sem, vmem_limit_bytes=VMEM_LIMIT)


def _pick(n, cands):
    for c in cands:
        if n % c == 0:
            return c
    raise ValueError(f"no tile for {n}")


def _bf(v):
    return v if v.dtype == BF16 else v.astype(BF16)


def _mm(name, a, b, dims, grid, a_spec, b_spec, o_spec, out_shape, out_dtype, acc_shape, k_axis=None, nk=1,
        alpha=1.0, res=None, res_spec=None, dep=None):
    has_res = res is not None
    has_dep = dep is not None

    def body(*refs):
        a_ref, b_ref = refs[0], refs[1]
        r_ref = refs[2] if has_res else None
        o_ref = refs[2 + has_res + has_dep]

        def finish(acc):
            if alpha != 1.0:
                acc = acc * alpha
            if has_res:
                acc = acc + r_ref[...]
            o_ref[...] = acc.astype(o_ref.dtype)

        part = lax.dot_general(_bf(a_ref[...]), _bf(b_ref[...]), dims, preferred_element_type=F32)
        if nk == 1:
            finish(part)
        else:
            acc_ref = refs[-1]
            k = pl.program_id(k_axis)

            @pl.when(k == 0)
            def _():
                acc_ref[...] = part

            @pl.when(k > 0)
            def _():
                acc_ref[...] += part

            @pl.when(k == nk - 1)
            def _():
                finish(acc_ref[...])

    in_specs = [a_spec, b_spec] + ([res_spec] if has_res else []) + ([pl.BlockSpec(memory_space=pl.ANY)] if has_dep else [])
    args = (a, b) + ((res,) if has_res else ()) + ((dep,) if has_dep else ())
    sem = tuple("arbitrary" if (nk > 1 and i == k_axis) else "parallel" for i in range(len(grid)))
    return pl.pallas_call(
        body, name=name, grid=grid, in_specs=in_specs, out_specs=o_spec,
        out_shape=jax.ShapeDtypeStruct(out_shape, out_dtype),
        scratch_shapes=[pltpu.VMEM(acc_shape, F32)] if nk > 1 else [],
        compiler_params=_params(sem),
    )(*args)


def _mm_res_norm(name, a, w, res, g_next, alpha=1.0):
    t, k = a.shape
    tm = _pick(t, (544, 384, 256, 128))

    def body(a_ref, w_ref, r_ref, g_ref, h_ref, n_ref):
        acc = lax.dot_general(_bf(a_ref[...]), w_ref[...], _NN, preferred_element_type=F32)
        if alpha != 1.0:
            acc = acc * alpha
        hv = acc + r_ref[...]
        h_ref[...] = hv
        r = lax.rsqrt(jnp.mean(hv * hv, axis=-1, keepdims=True) + EPS)
        n_ref[...] = ((hv * r) * g_ref[...]).astype(BF16)

    row = pl.BlockSpec((tm, D_MODEL), lambda i: (i, 0))
    return pl.pallas_call(
        body, name=name, grid=(t // tm,),
        in_specs=[pl.BlockSpec((tm, k), lambda i: (i, 0)), pl.BlockSpec((k, D_MODEL), lambda i: (0, 0)), row,
                  pl.BlockSpec((1, D_MODEL), lambda i: (0, 0))],
        out_specs=[row, row],
        out_shape=[jax.ShapeDtypeStruct((t, D_MODEL), F32), jax.ShapeDtypeStruct((t, D_MODEL), BF16)],
        compiler_params=_params(("parallel",)),
    )(a, w, res, g_next)


def _mm_nn(name, a, b, out_dtype=F32, alpha=1.0, res=None, tn_c=(512, 640, 256, 128), cols=None,
           tm_c=(1088, 768, 512, 256, 128), dep=None):
    t, k = a.shape
    c0, n = (0, b.shape[1]) if cols is None else cols
    tm = _pick(t, tm_c)
    tn = _pick(n, tn_c)
    assert c0 % tn == 0
    jb = c0 // tn
    return _mm(name, a, b, _NN, (t // tm, n // tn),
               pl.BlockSpec((tm, k), lambda i, j: (i, 0)), pl.BlockSpec((k, tn), lambda i, j: (0, jb + j)),
               pl.BlockSpec((tm, tn), lambda i, j: (i, j)), (t, n), out_dtype, None,
               alpha=alpha, res=res, res_spec=pl.BlockSpec((tm, tn), lambda i, j: (i, j)), dep=dep)


def _mm_nt(name, a, b, out_dtype=F32, alpha=1.0, tn_c=(768, 512, 256, 128), tk_c=None, dep=None, res=None, kcols=None):
    t, k = a.shape
    n = b.shape[0]
    c0 = 0 if kcols is None else kcols[0]
    tm = _pick(t, (1088, 768, 512, 256, 128))
    tn = _pick(n, tn_c)
    tk = k if tk_c is None else _pick(k, tk_c)
    nk = k // tk
    assert c0 % tk == 0
    kb = c0 // tk
    return _mm(name, a, b, _NT, (t // tm, n // tn, nk),
               pl.BlockSpec((tm, tk), lambda i, j, kk: (i, kk)), pl.BlockSpec((tn, tk), lambda i, j, kk: (j, kb + kk)),
               pl.BlockSpec((tm, tn), lambda i, j, kk: (i, j)), (t, n), out_dtype, (tm, tn), k_axis=2, nk=nk, alpha=alpha,
               dep=dep, res=res, res_spec=pl.BlockSpec((tm, tn), lambda i, j, kk: (i, j)))


def _mm_tn(name, a, b, out_dtype=BF16, alpha=1.0, tm_c=(768, 512, 256, 128), tn_c=(512, 640, 256, 128)):
    t, m = a.shape
    n = b.shape[1]
    tm = _pick(m, tm_c)
    tn = _pick(n, tn_c)
    bytes_a, bytes_b = a.size * a.dtype.itemsize, b.size * b.dtype.itemsize
    if bytes_a + bytes_b * (m // tm) <= bytes_b + bytes_a * (n // tn):
        return _mm(name, a, b, _TN, (m // tm, n // tn),
                   pl.BlockSpec((t, tm), lambda i, j: (0, i)), pl.BlockSpec((t, tn), lambda i, j: (0, j)),
                   pl.BlockSpec((tm, tn), lambda i, j: (i, j)), (m, n), out_dtype, None, alpha=alpha)
    return _mm(name, a, b, _TN, (n // tn, m // tm),
               pl.BlockSpec((t, tm), lambda j, i: (0, i)), pl.BlockSpec((t, tn), lambda j, i: (0, j)),
               pl.BlockSpec((tm, tn), lambda j, i: (i, j)), (m, n), out_dtype, None, alpha=alpha)


def _ffn_in_fwd(name, n, wblk, dep=None):
    t = n.shape[0]
    tm = _pick(t, (1088, 768, 512, 256, 128))
    has_dep = dep is not None

    def body(n_ref, w_ref, *rest):
        gu_ref, a_ref = rest[-2], rest[-1]
        nv = n_ref[...]
        g = lax.dot_general(nv, w_ref[0], _NN, preferred_element_type=F32)
        u = lax.dot_general(nv, w_ref[1], _NN, preferred_element_type=F32)
        sg = jax.nn.sigmoid(g)
        silu = g * sg
        a_ref[...] = (silu * u).astype(BF16)
        gu_ref[0] = ((0.5 * u) * (sg + silu * (1.0 - sg))).astype(BF16)
        gu_ref[1] = (0.5 * silu).astype(BF16)

    return pl.pallas_call(
        body, name=name, grid=(t // tm, 4),
        in_specs=[pl.BlockSpec((tm, D_MODEL), lambda i, j: (i, 0)),
                  pl.BlockSpec((2, None, D_MODEL, FF_SHARD_P), lambda i, j: (0, j, 0, 0))]
        + ([pl.BlockSpec(memory_space=pl.ANY)] if has_dep else []),
        out_specs=[pl.BlockSpec((2, tm, FF_SHARD_P), lambda i, j: (0, i, j)),
                   pl.BlockSpec((tm, FF_SHARD_P), lambda i, j: (i, j))],
        out_shape=[jax.ShapeDtypeStruct((2, t, D_FF_P), BF16), jax.ShapeDtypeStruct((t, D_FF_P), BF16)],
        compiler_params=_params(("parallel", "parallel")),
    )(*((n, wblk) + ((dep,) if has_dep else ())))


def _ffn_out_bwd_x(name, dh, w_out, gu, dep=None):
    t = dh.shape[0]
    tm = _pick(t, (1088, 768, 512, 256, 128))
    has_dep = dep is not None

    def body(dh_ref, w_ref, gu_ref, *rest):
        o_ref = rest[-1]
        da = lax.dot_general(_bf(dh_ref[...]), w_ref[...], _NT, preferred_element_type=F32)
        o_ref[0] = (da * gu_ref[0].astype(F32)).astype(BF16)
        o_ref[1] = (da * gu_ref[1].astype(F32)).astype(BF16)

    gu_spec = pl.BlockSpec((2, tm, FF_SHARD_P), lambda i, j: (0, i, j))
    return pl.pallas_call(
        body, name=name, grid=(t // tm, 4),
        in_specs=[pl.BlockSpec((tm, D_MODEL), lambda i, j: (i, 0)), pl.BlockSpec((FF_SHARD_P, D_MODEL), lambda i, j: (j, 0)),
                  gu_spec] + ([pl.BlockSpec(memory_space=pl.ANY)] if has_dep else []),
        out_specs=gu_spec, out_shape=jax.ShapeDtypeStruct((2, t, D_FF_P), BF16),
        compiler_params=_params(("parallel", "parallel")),
    )(*((dh, w_out, gu) + ((dep,) if has_dep else ())))


def _rms_bwd_rows(dn, h, g, dres):
    r = lax.rsqrt(jnp.mean(h * h, axis=-1, keepdims=True) + EPS)
    tv = dn * g
    dot = jnp.mean(tv * h, axis=-1, keepdims=True)
    return dres + (r * tv - h * (r * r * r * dot)), jnp.sum(dn * (h * r), axis=0, keepdims=True)


def _accumulate_rows(ref, part, first):
    @pl.when(first)
    def _():
        ref[...] = part

    @pl.when(jnp.logical_not(first))
    def _():
        ref[...] += part


def _ffn_in_bwd_x(name, dgu, wblk, h_in, g_norm, dres, dep=None, examples=None):
    t = dgu.shape[1]
    tm = _pick(t, (544, 384, 256, 128))
    has_dep = dep is not None
    split = examples is not None
    if split:
        l = t // examples
        per = l // tm
        assert per * tm == l and tm > PREFIX

    def body(d_ref, w_ref, h_ref, g_ref, r_ref, *rest):
        acc = None
        for s in range(2):
            for j in range(4):
                part = lax.dot_general(d_ref[s, :, FF_SHARD_P * j:FF_SHARD_P * (j + 1)], w_ref[s, j], _NT,
                                       preferred_element_type=F32)
                acc = part if acc is None else acc + part
        dh, dg = _rms_bwd_rows(acc, h_ref[...], g_ref[...], r_ref[...])
        i = pl.program_id(0)
        if not split:
            dh_ref, dg_ref = rest[-2], rest[-1]
            dh_ref[...] = dh
        else:
            gx_ref, meta_ref, dg_ref, buf, sem = rest[-5:]
            bi, r = i // per, i % per
            buf[...] = dh

            @pl.when(r == 0)
            def _():
                meta_ref[...] = dh[N_PAD:PREFIX]
                cp = pltpu.make_async_copy(buf.at[pl.ds(PREFIX, tm - PREFIX)], gx_ref.at[bi, pl.ds(0, tm - PREFIX)], sem)
                cp.start()
                cp.wait()

            if per > 1:
                @pl.when(r > 0)
                def _():
                    cp = pltpu.make_async_copy(buf, gx_ref.at[bi, pl.ds(pl.multiple_of(r * tm - PREFIX, 8), tm)], sem)
                    cp.start()
                    cp.wait()

        _accumulate_rows(dg_ref, dg, i == 0)

    row = pl.BlockSpec((tm, D_MODEL), lambda i: (i, 0))
    vec = pl.BlockSpec((1, D_MODEL), lambda i: (0, 0))
    if split:
        out_specs = [pl.BlockSpec(memory_space=pl.ANY), pl.BlockSpec((None, N_META, D_MODEL), lambda i: (i // per, 0, 0)), vec]
        out_shape = [jax.ShapeDtypeStruct((examples, l - PREFIX, D_MODEL), F32),
                     jax.ShapeDtypeStruct((examples, N_META, D_MODEL), F32), jax.ShapeDtypeStruct((1, D_MODEL), F32)]
        scratch = [pltpu.VMEM((tm, D_MODEL), F32), pltpu.SemaphoreType.DMA(())]
    else:
        out_specs = [row, vec]
        out_shape = [jax.ShapeDtypeStruct((t, D_MODEL), F32), jax.ShapeDtypeStruct((1, D_MODEL), F32)]
        scratch = []
    return pl.pallas_call(
        body, name=name, grid=(t // tm,),
        in_specs=[pl.BlockSpec((2, tm, D_FF_P), lambda i: (0, i, 0)),
                  pl.BlockSpec((2, 4, D_MODEL, FF_SHARD_P), lambda i: (0, 0, 0, 0), pipeline_mode=pl.Buffered(1)),
                  row, vec, row]
        + ([pl.BlockSpec(memory_space=pl.ANY)] if has_dep else []),
        out_specs=out_specs, out_shape=out_shape, scratch_shapes=scratch,
        compiler_params=_params(("arbitrary",)),
    )(*((dgu, wblk, h_in, g_norm, dres) + ((dep,) if has_dep else ())))


def _proj_fwd(name, um, wi):
    t = um.shape[0]
    tm = _pick(t, (544, 384, 256, 128))

    def body(u_ref, w_ref, q_ref, g_ref, f_ref):
        uv = u_ref[...]
        q_ref[...] = lax.dot_general(uv, w_ref[:, 0:QKV_W], _NN, preferred_element_type=F32).astype(BF16)
        g_ref[...] = lax.dot_general(uv, w_ref[:, P_GATES:P_F], _NN, preferred_element_type=F32)
        f_ref[...] = lax.dot_general(uv, w_ref[:, P_F:PROJ_P], _NN, preferred_element_type=F32)

    return pl.pallas_call(
        body, name=name, grid=(t // tm,),
        in_specs=[pl.BlockSpec((tm, D_MODEL), lambda i: (i, 0)),
                  pl.BlockSpec((D_MODEL, PROJ_P), lambda i: (0, 0), pipeline_mode=pl.Buffered(1))],
        out_specs=[pl.BlockSpec((tm, QKV_W), lambda i: (i, 0)), pl.BlockSpec((tm, 2 * D_MODEL), lambda i: (i, 0)),
                   pl.BlockSpec((tm, 128), lambda i: (i, 0))],
        out_shape=[jax.ShapeDtypeStruct((t, QKV_W), BF16), jax.ShapeDtypeStruct((t, 2 * D_MODEL), F32),
                   jax.ShapeDtypeStruct((t, 128), F32)],
        compiler_params=_params(("parallel",)),
    )(um, wi)


def _proj_bwd_x(name, dqkv, dgates, df, wi, h_in, g_norm, dres, dep=None):
    t = dqkv.shape[0]
    tm = _pick(t, (544, 384, 256, 128))
    has_dep = dep is not None

    def body(q_ref, gt_ref, f_ref, w_ref, h_ref, g_ref, r_ref, *rest):
        dh_ref, dg_ref = rest[-2], rest[-1]
        acc = lax.dot_general(q_ref[...], w_ref[:, 0:QKV_W], _NT, preferred_element_type=F32)
        acc = acc + lax.dot_general(gt_ref[...], w_ref[:, P_GATES:P_F], _NT, preferred_element_type=F32)
        acc = acc + lax.dot_general(f_ref[...], w_ref[:, P_F:PROJ_P], _NT, preferred_element_type=F32)
        dh, dg = _rms_bwd_rows(acc, h_ref[...], g_ref[...], r_ref[...])
        dh_ref[...] = dh
        _accumulate_rows(dg_ref, dg, pl.program_id(0) == 0)

    row = pl.BlockSpec((tm, D_MODEL), lambda i: (i, 0))
    vec = pl.BlockSpec((1, D_MODEL), lambda i: (0, 0))
    return pl.pallas_call(
        body, name=name, grid=(t // tm,),
        in_specs=[pl.BlockSpec((tm, QKV_W), lambda i: (i, 0)), pl.BlockSpec((tm, 2 * D_MODEL), lambda i: (i, 0)),
                  pl.BlockSpec((tm, 128), lambda i: (i, 0)),
                  pl.BlockSpec((D_MODEL, PROJ_P), lambda i: (0, 0), pipeline_mode=pl.Buffered(1)), row, vec, row]
        + ([pl.BlockSpec(memory_space=pl.ANY)] if has_dep else []),
        out_specs=[row, vec],
        out_shape=[jax.ShapeDtypeStruct((t, D_MODEL), F32), jax.ShapeDtypeStruct((1, D_MODEL), F32)],
        compiler_params=_params(("arbitrary",)),
    )(*((dqkv, dgates, df, wi, h_in, g_norm, dres) + ((dep,) if has_dep else ())))


def _ffn_in_bwd_w(name, n, dgu):
    t = n.shape[0]
    tk = t
    nk = 1
    return _mm(name, n, dgu, _TN, (2, 4, nk),
               pl.BlockSpec((tk, D_MODEL), lambda s, j, kk: (kk, 0)),
               pl.BlockSpec((None, tk, FF_SHARD_P), lambda s, j, kk: (s, kk, j)),
               pl.BlockSpec((None, None, D_MODEL, FF_SHARD_P), lambda s, j, kk: (s, j, 0, 0)),
               (2, 4, D_MODEL, FF_SHARD_P), BF16, (D_MODEL, FF_SHARD_P), k_axis=2, nk=nk)


def _rms_fwd(name, h, g):
    t = h.shape[0]
    tm = _pick(t, (544, 384, 256, 128))

    def body(h_ref, g_ref, o_ref):
        hv = h_ref[...]
        r = lax.rsqrt(jnp.mean(hv * hv, axis=-1, keepdims=True) + EPS)
        o_ref[...] = ((hv * r) * g_ref[...]).astype(BF16)

    return pl.pallas_call(
        body, name=name, grid=(t // tm,),
        in_specs=[pl.BlockSpec((tm, D_MODEL), lambda i: (i, 0)), pl.BlockSpec((1, D_MODEL), lambda i: (0, 0))],
        out_specs=pl.BlockSpec((tm, D_MODEL), lambda i: (i, 0)),
        out_shape=jax.ShapeDtypeStruct((t, D_MODEL), BF16), compiler_params=_params(("parallel",)),
    )(h, g)


def _embed_norm(name, x, meta, g):
    b, s, _ = x.shape
    half = (s + PREFIX) // 2
    first = half - PREFIX
    assert first > 0 and half % 16 == 0

    def body(x_ref, m_ref, g_ref, h_ref, n_ref, buf, sem):
        bi, k = pl.program_id(0), pl.program_id(1)

        @pl.when(k == 0)
        def _():
            buf[0:N_PAD, :] = jnp.zeros((N_PAD, D_MODEL), F32)
            buf[N_PAD:PREFIX, :] = m_ref[...]
            cp = pltpu.make_async_copy(x_ref.at[bi, pl.ds(0, first)], buf.at[pl.ds(PREFIX, first)], sem)
            cp.start()
            cp.wait()

        @pl.when(k == 1)
        def _():
            cp = pltpu.make_async_copy(x_ref.at[bi, pl.ds(first, half)], buf, sem)
            cp.start()
            cp.wait()

        hv = buf[...]
        h_ref[...] = hv
        r = lax.rsqrt(jnp.mean(hv * hv, axis=-1, keepdims=True) + EPS)
        n_ref[...] = ((hv * r) * g_ref[...]).astype(BF16)

    rows = pl.BlockSpec((half, D_MODEL), lambda bi, k: (2 * bi + k, 0))
    return pl.pallas_call(
        body, name=name, grid=(b, 2),
        in_specs=[pl.BlockSpec(memory_space=pl.ANY), pl.BlockSpec((N_META, D_MODEL), lambda bi, k: (0, 0)),
                  pl.BlockSpec((1, D_MODEL), lambda bi, k: (0, 0))],
        out_specs=[rows, rows],
        out_shape=[jax.ShapeDtypeStruct((2 * b * half, D_MODEL), F32), jax.ShapeDtypeStruct((2 * b * half, D_MODEL), BF16)],
        scratch_shapes=[pltpu.VMEM((half, D_MODEL), F32), pltpu.SemaphoreType.DMA(())],
        compiler_params=_params(("arbitrary", "arbitrary")),
    )(x, meta, g)


def _rms_bwd(name, h, g, dn, dres):
    t = h.shape[0]
    tm = _pick(t, (544, 384, 256, 128))

    def body(h_ref, g_ref, dn_ref, dres_ref, dh_ref, dg_ref):
        i = pl.program_id(0)
        hv = h_ref[...]
        dnv = dn_ref[...]
        r = lax.rsqrt(jnp.mean(hv * hv, axis=-1, keepdims=True) + EPS)
        tv = dnv * g_ref[...]
        dot = jnp.mean(tv * hv, axis=-1, keepdims=True)
        dh_ref[...] = dres_ref[...] + (r * tv - hv * (r * r * r * dot))
        part = jnp.sum(dnv * (hv * r), axis=0, keepdims=True)

        @pl.when(i == 0)
        def _():
            dg_ref[...] = part

        @pl.when(i > 0)
        def _():
            dg_ref[...] += part

    row = pl.BlockSpec((tm, D_MODEL), lambda i: (i, 0))
    vec = pl.BlockSpec((1, D_MODEL), lambda i: (0, 0))
    return pl.pallas_call(
        body, name=name, grid=(t // tm,), in_specs=[row, vec, row, row], out_specs=[row, vec],
        out_shape=[jax.ShapeDtypeStruct((t, D_MODEL), F32), jax.ShapeDtypeStruct((1, D_MODEL), F32)],
        compiler_params=_params(("arbitrary",)),
    )(h, g, dn, dres)


def _branch_gate_fwd(name, oa, ob, wa, wb, gates, dep=None):
    t = gates.shape[0]
    tm = _pick(t, (544, 384, 256, 128))
    has_dep = dep is not None

    def body(oa_ref, ob_ref, wa_ref, wb_ref, g_ref, *rest):
        o_ref, ya_ref, yb_ref = rest[-3:]
        ya = lax.dot_general(_bf(oa_ref[...]), wa_ref[...], _NN, preferred_element_type=F32)
        yb = lax.dot_general(_bf(ob_ref[...]), wb_ref[...], _NN, preferred_element_type=F32)
        sa = jax.nn.sigmoid(g_ref[:, 0:D_MODEL])
        sb = jax.nn.sigmoid(g_ref[:, D_MODEL:2 * D_MODEL])
        o_ref[...] = (sa * ya + sb * yb).astype(BF16)
        ya_ref[...] = ya.astype(BF16)
        yb_ref[...] = yb.astype(BF16)

    blk = pl.BlockSpec((tm, D_MODEL), lambda i: (i, 0))
    narrow = pl.BlockSpec((tm, A_WIDTH), lambda i: (i, 0))
    wide = pl.BlockSpec((tm, 2 * D_MODEL), lambda i: (i, 0))
    wspec = pl.BlockSpec((A_WIDTH, D_MODEL), lambda i: (0, 0))
    out = jax.ShapeDtypeStruct((t, D_MODEL), BF16)
    return pl.pallas_call(
        body, name=name, grid=(t // tm,),
        in_specs=[narrow, narrow, wspec, wspec, wide] + ([pl.BlockSpec(memory_space=pl.ANY)] if has_dep else []),
        out_specs=[blk, blk, blk], out_shape=[out, out, out], compiler_params=_params(("parallel",)),
    )(*((oa, ob, wa, wb, gates) + ((dep,) if has_dep else ())))


def _branch_bwd_x(name, dya, dyb, wa, wb):
    t = dya.shape[0]
    tm = _pick(t, (1088, 768, 512, 256, 128))

    def body(da_ref, db_ref, wa_ref, wb_ref, oa_ref, ob_ref):
        oa_ref[...] = lax.dot_general(da_ref[...], wa_ref[...], _NT, preferred_element_type=F32)
        ob_ref[...] = lax.dot_general(db_ref[...], wb_ref[...], _NT, preferred_element_type=F32).astype(BF16)

    blk = pl.BlockSpec((tm, D_MODEL), lambda i: (i, 0))
    narrow = pl.BlockSpec((tm, A_WIDTH), lambda i: (i, 0))
    wspec = pl.BlockSpec((A_WIDTH, D_MODEL), lambda i: (0, 0), pipeline_mode=pl.Buffered(1))
    return pl.pallas_call(
        body, name=name, grid=(t // tm,), in_specs=[blk, blk, wspec, wspec], out_specs=[narrow, narrow],
        out_shape=[jax.ShapeDtypeStruct((t, A_WIDTH), F32), jax.ShapeDtypeStruct((t, B_WIDTH), BF16)],
        compiler_params=_params(("parallel",)),
    )(dya, dyb, wa, wb)


def _branch_bwd_w(name, oa, ob, dya, dyb):
    t = oa.shape[0]
    tn = 512

    def body(oa_ref, ob_ref, da_ref, db_ref, wa_ref, wb_ref):
        wa_ref[...] = lax.dot_general(_bf(oa_ref[...]), da_ref[...], _TN, preferred_element_type=F32).astype(BF16)
        wb_ref[...] = lax.dot_general(_bf(ob_ref[...]), db_ref[...], _TN, preferred_element_type=F32).astype(BF16)

    whole = pl.BlockSpec((t, A_WIDTH), lambda j: (0, 0), pipeline_mode=pl.Buffered(1))
    cols = pl.BlockSpec((t, tn), lambda j: (0, j))
    out_spec = pl.BlockSpec((A_WIDTH, tn), lambda j: (0, j))
    out = jax.ShapeDtypeStruct((A_WIDTH, D_MODEL), BF16)
    return pl.pallas_call(
        body, name=name, grid=(D_MODEL // tn,), in_specs=[whole, whole, cols, cols], out_specs=[out_spec, out_spec],
        out_shape=[out, out], compiler_params=_params(("parallel",)),
    )(oa, ob, dya, dyb)


def _mix_out_gate_bwd(name, dh, wo, gates, ya, yb):
    t = gates.shape[0]
    tm = _pick(t, (544, 384, 256, 128))

    def body(dh_ref, w_ref, g_ref, ya_ref, yb_ref, dya_ref, dyb_ref, dg_ref):
        dm = lax.dot_general(_bf(dh_ref[...]), w_ref[...], _NT, preferred_element_type=F32)
        sa = jax.nn.sigmoid(g_ref[:, 0:D_MODEL])
        sb = jax.nn.sigmoid(g_ref[:, D_MODEL:2 * D_MODEL])
        dya_ref[...] = (dm * sa).astype(BF16)
        dyb_ref[...] = (dm * sb).astype(BF16)
        dg_ref[:, 0:D_MODEL] = (dm * ya_ref[...].astype(F32) * (sa * (1.0 - sa))).astype(BF16)
        dg_ref[:, D_MODEL:2 * D_MODEL] = (dm * yb_ref[...].astype(F32) * (sb * (1.0 - sb))).astype(BF16)

    blk = pl.BlockSpec((tm, D_MODEL), lambda i: (i, 0))
    wide = pl.BlockSpec((tm, 2 * D_MODEL), lambda i: (i, 0))
    out = jax.ShapeDtypeStruct((t, D_MODEL), BF16)
    return pl.pallas_call(
        body, name=name, grid=(t // tm,),
        in_specs=[blk, pl.BlockSpec((D_MODEL, D_MODEL), lambda i: (0, 0)), wide, blk, blk], out_specs=[blk, blk, wide],
        out_shape=[out, out, jax.ShapeDtypeStruct((t, 2 * D_MODEL), BF16)], compiler_params=_params(("parallel",)),
    )(dh, wo, gates, ya, yb)


def _ffn_out_loss(name, a, w, res, gf, tgt, alpha):
    t, k = a.shape
    b, s, _ = tgt.shape
    l = t // b
    tm = _pick(t, (544, 384, 256, 128))
    per = l // tm
    assert per * tm == l and tm > PREFIX and l - PREFIX == s

    def body(a_ref, w_ref, r_ref, g_ref, t_ref, dh_ref, loss_ref, dg_ref, buf, sem):
        i = pl.program_id(0)
        bi, r = i // per, i % per

        @pl.when(r == 0)
        def _():
            buf[0:PREFIX, :] = jnp.zeros((PREFIX, D_MODEL), F32)
            cp = pltpu.make_async_copy(t_ref.at[bi, pl.ds(0, tm - PREFIX)], buf.at[pl.ds(PREFIX, tm - PREFIX)], sem)
            cp.start()
            cp.wait()

        if per > 1:
            @pl.when(r > 0)
            def _():
                cp = pltpu.make_async_copy(t_ref.at[bi, pl.ds(pl.multiple_of(r * tm - PREFIX, 8), tm)], buf, sem)
                cp.start()
                cp.wait()

        acc = lax.dot_general(a_ref[...], w_ref[...], _NN, preferred_element_type=F32)
        hv = acc * alpha + r_ref[...]
        row = lax.broadcasted_iota(jnp.int32, (tm, 1), 0)
        real = ((r > 0) | (row >= PREFIX)).astype(F32)
        g = g_ref[...]
        rn = lax.rsqrt(jnp.mean(hv * hv, axis=-1, keepdims=True) + EPS)
        xn = hv * rn
        err = (xn * g - buf[...]) * real
        lpart = 0.5 * jnp.sum(jnp.mean(err * err, axis=-1, keepdims=True), axis=0, keepdims=True)
        dy = err * (1.0 / D_MODEL)
        tv = dy * g
        dot = jnp.mean(tv * hv, axis=-1, keepdims=True)
        dh_ref[...] = rn * tv - hv * (rn * rn * rn * dot)
        gpart = jnp.sum(dy * xn, axis=0, keepdims=True)

        @pl.when(i == 0)
        def _():
            loss_ref[...] = jnp.zeros_like(loss_ref)
            dg_ref[...] = jnp.zeros_like(dg_ref)

        loss_ref[...] += jnp.broadcast_to(lpart, loss_ref.shape)
        dg_ref[...] += gpart

    row_spec = pl.BlockSpec((tm, D_MODEL), lambda i: (i, 0))
    vec = pl.BlockSpec((1, D_MODEL), lambda i: (0, 0))
    return pl.pallas_call(
        body, name=name, grid=(t // tm,),
        in_specs=[pl.BlockSpec((tm, k), lambda i: (i, 0)),
                  pl.BlockSpec((k, D_MODEL), lambda i: (0, 0), pipeline_mode=pl.Buffered(1)), row_spec, vec,
                  pl.BlockSpec(memory_space=pl.ANY)],
        out_specs=[row_spec, pl.BlockSpec((8, 128), lambda i: (0, 0)), vec],
        out_shape=[jax.ShapeDtypeStruct((t, D_MODEL), F32), jax.ShapeDtypeStruct((8, 128), F32),
                   jax.ShapeDtypeStruct((1, D_MODEL), F32)],
        scratch_shapes=[pltpu.VMEM((tm, D_MODEL), F32), pltpu.SemaphoreType.DMA(())],
        compiler_params=_params(("arbitrary",)),
    )(a, w, res, gf, tgt)


def _fgate_fwd(name, f3, bf_row):
    b, l, _ = f3.shape
    nb = l // BLOCK

    def body(f_ref, b_ref, cc_ref, cr_ref):
        r_i = lax.broadcasted_iota(jnp.int32, (BLOCK, BLOCK), 0)
        c_i = lax.broadcasted_iota(jnp.int32, (BLOCK, BLOCK), 1)
        tri = (r_i >= c_i).astype(F32)
        carry = jnp.zeros((1, 128), F32)
        for blk in range(nb):
            rows = slice(blk * BLOCK, (blk + 1) * BLOCK)
            z = f_ref[rows, :] + b_ref[...]
            lf = jnp.minimum(z, 0.0) - jnp.log(1.0 + jnp.exp(-jnp.abs(z)))
            cb = jnp.dot(tri, lf, preferred_element_type=F32, precision=lax.Precision.HIGHEST) + carry
            carry = cb[BLOCK - 1:BLOCK, :]
            cbt = cb.T
            for hh in range(B_HEADS):
                cc_ref[hh, rows, :] = jnp.sum(jnp.where(c_i == hh, cb, 0.0), axis=1, keepdims=True)
                cr_ref[hh, :, rows] = cbt[hh:hh + 1, :]

    return pl.pallas_call(
        body, name=name, grid=(b,),
        in_specs=[pl.BlockSpec((None, l, 128), lambda bi: (bi, 0, 0)),
                  pl.BlockSpec((1, 128), lambda bi: (0, 0))],
        out_specs=[pl.BlockSpec((None, B_HEADS, l, 1), lambda bi: (bi, 0, 0, 0)),
                   pl.BlockSpec((None, B_HEADS, 1, l), lambda bi: (bi, 0, 0, 0))],
        out_shape=[jax.ShapeDtypeStruct((b, B_HEADS, l, 1), F32), jax.ShapeDtypeStruct((b, B_HEADS, 1, l), F32)],
        compiler_params=_params(("parallel",)),
    )(f3, bf_row)


def _fgate_bwd(name, f3, bf_row, dcq, dck):
    b, l, _ = f3.shape
    nb = l // BLOCK

    def body(f_ref, b_ref, dcq_ref, dck_ref, df_ref, db_ref):
        r_i = lax.broadcasted_iota(jnp.int32, (BLOCK, BLOCK), 0)
        c_i = lax.broadcasted_iota(jnp.int32, (BLOCK, BLOCK), 1)
        tri = (r_i <= c_i).astype(F32)
        carry = jnp.zeros((1, 128), F32)
        total = jnp.zeros((1, 128), F32)
        for blk in range(nb - 1, -1, -1):
            rows = slice(blk * BLOCK, (blk + 1) * BLOCK)
            krows = jnp.concatenate([dck_ref[hh, :, rows] for hh in range(B_HEADS)]
                                    + [jnp.zeros((BLOCK - B_HEADS, BLOCK), F32)], axis=0)
            dcb = krows.T
            for hh in range(B_HEADS):
                dcb = dcb + jnp.where(c_i == hh, dcq_ref[hh, rows, :], 0.0)
            rc = jnp.dot(tri, dcb, preferred_element_type=F32, precision=lax.Precision.HIGHEST) + carry
            carry = rc[0:1, :]
            z = f_ref[rows, :] + b_ref[...]
            df = rc * (1.0 / (1.0 + jnp.exp(z)))
            df_ref[rows, :] = df.astype(BF16)
            total = total + jnp.sum(df, axis=0, keepdims=True)

        @pl.when(pl.program_id(0) == 0)
        def _():
            db_ref[...] = total

        @pl.when(pl.program_id(0) > 0)
        def _():
            db_ref[...] += total

    return pl.pallas_call(
        body, name=name, grid=(b,),
        in_specs=[pl.BlockSpec((None, l, 128), lambda bi: (bi, 0, 0)),
                  pl.BlockSpec((1, 128), lambda bi: (0, 0)),
                  pl.BlockSpec((None, B_HEADS, l, 1), lambda bi: (bi, 0, 0, 0)),
                  pl.BlockSpec((None, B_HEADS, 1, l), lambda bi: (bi, 0, 0, 0))],
        out_specs=[pl.BlockSpec((None, l, 128), lambda bi: (bi, 0, 0)), pl.BlockSpec((1, 128), lambda bi: (0, 0))],
        out_shape=[jax.ShapeDtypeStruct((b, l, 128), BF16), jax.ShapeDtypeStruct((1, 128), F32)],
        compiler_params=_params(("arbitrary",)),
    )(f3, bf_row, dcq, dck)


A_Q_BLK = B_SEG // A_WIDTH
A_K_BLK = (B_SEG + A_WIDTH) // 128
A_V_BLK = A_K_BLK + 1
A_SEG_BLK = B_SEG // A_SEG
STACK = A_HEADS * BLOCK


def _lane_lo():
    return lax.broadcasted_iota(jnp.int32, (1, 128), 1) < HEAD_DIM


def _stack_heads(x, masked):
    lo = _lane_lo()
    blks = [x[:, 128 * j:128 * (j + 1)] for j in range(4)]
    if not masked:
        return jnp.concatenate(blks + blks, axis=0)
    zero = jnp.zeros_like(blks[0])
    return jnp.concatenate([jnp.where(lo, bk, zero) for bk in blks] + [jnp.where(lo, zero, bk) for bk in blks], axis=0)


def _unstack_heads(y):
    lo = _lane_lo()
    return jnp.concatenate([jnp.where(lo, y[128 * j:128 * (j + 1)], y[128 * (4 + j):128 * (5 + j)]) for j in range(4)], axis=1)


def _swa_bias(slopes):
    r_i = jnp.arange(STACK)[:, None]
    c_i = jnp.arange(3 * BLOCK)[None, :]
    seg = c_i >> 7
    out = []
    for n in range(3):
        qpos = n * BLOCK + (r_i & (BLOCK - 1))
        kpos = jnp.where(seg == 0, c_i, (n - 2) * BLOCK + c_i)
        dist = qpos - kpos
        band = (seg != 0) & (dist < BLOCK) & (kpos >= PREFIX)
        meta = (seg == 0) & (c_i >= N_PAD)
        out.append(jnp.where((dist >= 0) & (band | meta), -slopes * dist.astype(F32), NEG))
    return jnp.stack(out, axis=0)


def _swa_scores(q, kcat, n, slope, bias):
    s = lax.dot_general(q, kcat, _NT, preferred_element_type=F32) + bias
    further = slope * (-BLOCK * jnp.maximum(n - 2, 0)).astype(F32)
    return jnp.concatenate([s[:, 0:BLOCK] + further, s[:, BLOCK:]], axis=1)


def _swa_specs():
    def kv(col_blk):
        return [pl.BlockSpec((None, BLOCK, 128), lambda b, n: (b, 0, col_blk)),
                pl.BlockSpec((None, BLOCK, 128), lambda b, n: (b, jnp.maximum(n - 1, 0), col_blk)),
                pl.BlockSpec((None, BLOCK, 128), lambda b, n: (b, n, col_blk))]

    q_spec = pl.BlockSpec((None, BLOCK, A_WIDTH), lambda b, n: (b, n, A_Q_BLK))
    o_spec = pl.BlockSpec((None, BLOCK, A_WIDTH), lambda b, n: (b, n, 0))
    col = pl.BlockSpec((STACK, 1), lambda b, n: (0, 0))
    bias = pl.BlockSpec((None, STACK, 3 * BLOCK), lambda b, n: (jnp.minimum(n, 2), 0, 0))
    lse_spec = pl.BlockSpec((None, A_HEADS, BLOCK, 1), lambda b, n: (b, 0, n, 0))
    return q_spec, kv(A_K_BLK), kv(A_V_BLK), o_spec, [col, col, bias], lse_spec


def _swa_fwd(name, qkv, slopes, sinks, bias):
    b, l, _ = qkv.shape
    nb = l // BLOCK

    def body(q_ref, k0_ref, kp_ref, kc_ref, v0_ref, vp_ref, vc_ref, sl_ref, sk_ref, bias_ref, o_ref, lse_ref):
        n = pl.program_id(1)
        qs = _stack_heads(q_ref[...], True) * SCALE
        kcat = jnp.concatenate([k0_ref[...], kp_ref[...], kc_ref[...]], axis=0)
        vcat = jnp.concatenate([v0_ref[...], vp_ref[...], vc_ref[...]], axis=0)
        s = _swa_scores(qs, kcat, n, sl_ref[...], bias_ref[...])
        sink = sk_ref[...]
        m = jnp.maximum(jnp.max(s, axis=-1, keepdims=True), sink)
        p = jnp.exp(s - m)
        den = jnp.sum(p, axis=-1, keepdims=True) + jnp.exp(sink - m)
        o = lax.dot_general(p.astype(BF16), vcat, _NN, preferred_element_type=F32) / den
        o_ref[...] = _unstack_heads(o)
        lse_ref[...] = (m + jnp.log(den)).reshape(A_HEADS, BLOCK, 1)

    q_spec, k_specs, v_specs, o_spec, consts, lse_spec = _swa_specs()
    return pl.pallas_call(
        body, name=name, grid=(b, nb),
        in_specs=[q_spec] + k_specs + v_specs + consts, out_specs=[o_spec, lse_spec],
        out_shape=[jax.ShapeDtypeStruct((b, l, A_WIDTH), F32), jax.ShapeDtypeStruct((b, A_HEADS, l, 1), F32)],
        compiler_params=_params(("parallel", "parallel")),
    )(qkv, qkv, qkv, qkv, qkv, qkv, qkv, slopes, sinks, bias)


def _swa_bwd(name, qkv, o, lse, do, slopes, sinks, bias, dqkv):
    b, l, _ = qkv.shape
    nb = l // BLOCK

    def body(q_ref, k0_ref, kp_ref, kc_ref, v0_ref, vp_ref, vc_ref, o_ref, lse_ref, do_ref, sl_ref, sk_ref, bias_ref, _,
             dx_ref, ds_ref, dk_acc, dv_acc):
        bi = pl.program_id(0)
        n = pl.program_id(1)
        qs = _stack_heads(q_ref[...], True) * SCALE
        dos32 = _stack_heads(do_ref[...], True)
        dos = dos32.astype(BF16)
        os_ = _stack_heads(o_ref[...], False)
        lsev = lse_ref[...].reshape(STACK, 1)
        kcat = jnp.concatenate([k0_ref[...], kp_ref[...], kc_ref[...]], axis=0)
        vcat = jnp.concatenate([v0_ref[...], vp_ref[...], vc_ref[...]], axis=0)
        s = _swa_scores(qs, kcat, n, sl_ref[...], bias_ref[...])
        p = jnp.exp(s - lsev)
        dsum = jnp.sum(dos32 * os_, axis=-1, keepdims=True)
        dp = lax.dot_general(dos, vcat, _NT, preferred_element_type=F32)
        dsc = (p * (dp - dsum)).astype(BF16)
        dq = lax.dot_general(dsc, kcat, _NN, preferred_element_type=F32) * SCALE
        row0 = pl.multiple_of(n * BLOCK, BLOCK)
        dx_ref[pl.ds(row0, BLOCK), 0:A_WIDTH] = _unstack_heads(dq).astype(BF16)
        dkc = lax.dot_general(dsc, qs, _TN, preferred_element_type=F32)
        dvc = lax.dot_general(p.astype(BF16), dos, _TN, preferred_element_type=F32)

        @pl.when(n == 0)
        def _():
            dk_acc[...] = jnp.zeros_like(dk_acc)
            dv_acc[...] = jnp.zeros_like(dv_acc)

        starts = (0, pl.multiple_of(jnp.maximum(n - 1, 0) * BLOCK, BLOCK), row0)
        for t, st in enumerate(starts):
            dk_acc[pl.ds(st, BLOCK), :] += dkc[t * BLOCK:(t + 1) * BLOCK, :]
            dv_acc[pl.ds(st, BLOCK), :] += dvc[t * BLOCK:(t + 1) * BLOCK, :]

        @pl.when(n == nb - 1)
        def _():
            dx_ref[:, A_WIDTH:A_WIDTH + 128] = dk_acc[...].astype(BF16)
            dx_ref[:, A_WIDTH + 128:A_SEG] = dv_acc[...].astype(BF16)

        dsink = -(jnp.exp(sk_ref[...] - lsev) * dsum)
        r8 = lax.broadcasted_iota(jnp.int32, (8, 128), 0)
        acc = jnp.zeros((8, 128), F32)
        for hh in range(A_HEADS):
            acc = acc + jnp.where(r8 == hh, jnp.sum(dsink[hh * BLOCK:(hh + 1) * BLOCK, :]), 0.0)

        @pl.when((bi == 0) & (n == 0))
        def _():
            ds_ref[...] = jnp.zeros_like(ds_ref)

        ds_ref[...] += acc

    q_spec, k_specs, v_specs, o_spec, consts, lse_spec = _swa_specs()
    return pl.pallas_call(
        body, name=name, grid=(b, nb),
        in_specs=[q_spec] + k_specs + v_specs + [o_spec, lse_spec, o_spec] + consts + [pl.BlockSpec(memory_space=pl.ANY)],
        out_specs=[pl.BlockSpec((None, l, A_SEG), lambda bb, n: (bb, 0, A_SEG_BLK)),
                   pl.BlockSpec((8, 128), lambda bb, n: (0, 0))],
        out_shape=[jax.ShapeDtypeStruct(dqkv.shape, BF16), jax.ShapeDtypeStruct((8, 128), F32)],
        scratch_shapes=[pltpu.VMEM((l, 128), F32), pltpu.VMEM((l, 128), F32)],
        input_output_aliases={13: 0},
        compiler_params=_params(("arbitrary", "arbitrary")),
    )(qkv, qkv, qkv, qkv, qkv, qkv, qkv, o, lse, do, slopes, sinks, bias, dqkv)


def _fox_mask(qk, ck, i):
    kh = qk.shape[1]
    qpos = i * BLOCK + lax.broadcasted_iota(jnp.int32, (BLOCK, kh), 0)
    kpos = lax.broadcasted_iota(jnp.int32, (BLOCK, kh), 1)
    return jnp.where((kpos <= qpos) & (kpos >= N_PAD), qk - ck, NEG)


def _pick_head(x, hh):
    lo = _lane_lo()
    return jnp.where(lo if hh == 0 else jnp.logical_not(lo), x, jnp.zeros_like(x))


def _both_heads(x):
    return jnp.concatenate([_pick_head(x, 0), _pick_head(x, 1)], axis=0)


def _fox_specs(l):
    pair = pl.BlockSpec((None, l, PAIR_W), lambda bi, hp: (bi, 0, hp))
    half = pl.BlockSpec((None, l, 128), lambda bi, hp: (bi, 0, hp))
    colv = pl.BlockSpec((None, 2, l, 1), lambda bi, hp: (bi, hp, 0, 0))
    rowv = pl.BlockSpec((None, 2, 1, l), lambda bi, hp: (bi, hp, 0, 0))
    return pair, half, colv, rowv


def _fox_fwd(name, qkv, c_col, c_row):
    b, l, _ = qkv.shape
    nb = l // BLOCK

    def body(x_ref, cc_ref, cr_ref, o_ref, lse_ref):
        for i in range(nb):
            rows = slice(i * BLOCK, (i + 1) * BLOCK)
            kh = (i + 1) * BLOCK
            qblk = x_ref[rows, 0:128]
            kv = x_ref[0:kh, 128:256]
            vv = x_ref[0:kh, 256:384]
            qk = lax.dot_general(_both_heads(qblk) * SCALE, kv, _NT, preferred_element_type=F32)
            ps, dens = [], []
            for hh in range(2):
                s = _fox_mask(qk[hh * BLOCK:(hh + 1) * BLOCK], cr_ref[hh, :, 0:kh], i)
                m = jnp.max(s, axis=-1, keepdims=True)
                p = jnp.exp(s - m)
                den = jnp.sum(p, axis=-1, keepdims=True)
                ps.append(p.astype(BF16))
                dens.append(den)
                lse_ref[hh, rows, :] = (m + jnp.log(den)) + cc_ref[hh, rows, :]
            pv = lax.dot_general(jnp.concatenate(ps, axis=0), vv, _NN, preferred_element_type=F32)
            o_ref[rows, :] = jnp.where(_lane_lo(), pv[0:BLOCK] / dens[0], pv[BLOCK:2 * BLOCK] / dens[1]).astype(BF16)

    pair, half, colv, rowv = _fox_specs(l)
    return pl.pallas_call(
        body, name=name, grid=(b, 4), in_specs=[pair, colv, rowv], out_specs=[half, colv],
        out_shape=[jax.ShapeDtypeStruct((b, l, B_WIDTH), BF16), jax.ShapeDtypeStruct((b, B_HEADS, l, 1), F32)],
        compiler_params=_params(("parallel", "parallel")),
    )(qkv, c_col, c_row)


def _fox_bwd(name, qkv, c_col, c_row, o, lse, do):
    b, l, _ = qkv.shape
    nb = l // BLOCK

    def body(x_ref, cc_ref, cr_ref, o_ref, lse_ref, do_ref, dx_ref, dcq_ref, dck_ref, dk_acc, dv_acc):
        dk_acc[...] = jnp.zeros_like(dk_acc)
        dv_acc[...] = jnp.zeros_like(dv_acc)
        dck_ref[...] = jnp.zeros_like(dck_ref)
        for i in range(nb):
            rows = slice(i * BLOCK, (i + 1) * BLOCK)
            kh = (i + 1) * BLOCK
            qblk = x_ref[rows, 0:128]
            kv = x_ref[0:kh, 128:256]
            vv = x_ref[0:kh, 256:384]
            doblk = do_ref[rows, :]
            ov = o_ref[rows, :].astype(F32)
            q2 = _both_heads(qblk) * SCALE
            do2 = _both_heads(doblk)
            qk = lax.dot_general(q2, kv, _NT, preferred_element_type=F32)
            dp = lax.dot_general(do2, vv, _NT, preferred_element_type=F32)
            ps, dss = [], []
            for hh in range(2):
                half = slice(hh * BLOCK, (hh + 1) * BLOCK)
                s = _fox_mask(qk[half], cr_ref[hh, :, 0:kh], i)
                p = jnp.exp(s - (lse_ref[hh, rows, :] - cc_ref[hh, rows, :]))
                dsum = jnp.sum(do2[half].astype(F32) * ov, axis=-1, keepdims=True)
                ds = p * (dp[half] - dsum)
                ps.append(p.astype(BF16))
                dss.append(ds.astype(BF16))
                dcq_ref[hh, rows, :] = jnp.sum(ds, axis=-1, keepdims=True)
                dck_ref[hh, :, 0:kh] -= jnp.sum(ds, axis=0, keepdims=True)
            p2 = jnp.concatenate(ps, axis=0)
            ds2 = jnp.concatenate(dss, axis=0)
            dq = lax.dot_general(ds2, kv, _NN, preferred_element_type=F32) * SCALE
            dk_acc[0:kh, :] += lax.dot_general(ds2, q2, _TN, preferred_element_type=F32)
            dv_acc[0:kh, :] += lax.dot_general(p2, do2, _TN, preferred_element_type=F32)
            dx_ref[rows, 0:128] = jnp.where(_lane_lo(), dq[0:BLOCK], dq[BLOCK:2 * BLOCK]).astype(BF16)
        dx_ref[:, 128:256] = dk_acc[...].astype(BF16)
        dx_ref[:, 256:384] = dv_acc[...].astype(BF16)

    pair, half, colv, rowv = _fox_specs(l)
    return pl.pallas_call(
        body, name=name, grid=(b, 4), in_specs=[pair, colv, rowv, half, colv, half],
        out_specs=[pair, colv, rowv],
        out_shape=[jax.ShapeDtypeStruct(qkv.shape, BF16), jax.ShapeDtypeStruct((b, B_HEADS, l, 1), F32),
                   jax.ShapeDtypeStruct((b, B_HEADS, 1, l), F32)],
        scratch_shapes=[pltpu.VMEM((l, 128), F32), pltpu.VMEM((l, 128), F32)],
        compiler_params=_params(("parallel", "parallel")),
    )(qkv, c_col, c_row, o, lse, do)


_FLIPS = ((0, 0, 1), (0, 1, 0), (0, 1, 1), (1, 0, 0), (1, 0, 1), (1, 1, 0), (1, 1, 1))


def _exchange(name, gather, scatter):
    ng, ns = len(gather), len(scatter)
    na = ng + ns
    npeer = len(_FLIPS)

    def body(*refs):
        ins = refs[:na]
        outs = refs[na:2 * na]
        send_sems, recv_sems, loc_sems = refs[2 * na:]
        x, y, c = lax.axis_index("x"), lax.axis_index("y"), lax.axis_index("c")
        me = 4 * x + 2 * y + c
        peers = []
        for fx, fy, fc in _FLIPS:
            px = 1 - x if fx else x
            py = 1 - y if fy else y
            pc = 1 - c if fc else c
            peers.append(((px, py, pc), 4 * px + 2 * py + pc))

        def remote(a, kk):
            dev, lin = peers[kk]
            src = ins[a] if a < ng else ins[a].at[lin]
            return pltpu.make_async_remote_copy(src_ref=src, dst_ref=outs[a].at[me], send_sem=send_sems.at[a * npeer + kk],
                                                recv_sem=recv_sems.at[a * npeer + kk], device_id=dev, device_id_type=MESH_ID)

        def arrival(a, kk):
            dev, lin = peers[kk]
            src = ins[a] if a < ng else ins[a].at[lin]
            return pltpu.make_async_remote_copy(src_ref=src, dst_ref=outs[a].at[lin], send_sem=send_sems.at[a * npeer + kk],
                                                recv_sem=recv_sems.at[a * npeer + kk], device_id=dev, device_id_type=MESH_ID)

        local = []
        for a in range(na):
            src = ins[a] if a < ng else ins[a].at[me]
            cp = pltpu.make_async_copy(src, outs[a].at[me], loc_sems.at[a])
            cp.start()
            local.append(cp)
        sent = [remote(a, kk) for kk in range(npeer) for a in range(na)]
        for cp in sent:
            cp.start()
        for kk in range(npeer):
            for a in range(na):
                arrival(a, kk).wait_recv()
        for cp in sent:
            cp.wait_send()
        for cp in local:
            cp.wait()

    arrs = list(gather) + list(scatter)
    out_shape = [jax.ShapeDtypeStruct((N_DEV,) + tuple(a.shape), a.dtype) for a in gather]
    out_shape += [jax.ShapeDtypeStruct(tuple(a.shape), a.dtype) for a in scatter]
    anyspec = pl.BlockSpec(memory_space=pl.ANY)
    return pl.pallas_call(
        body, name=name, in_specs=[anyspec] * na, out_specs=[anyspec] * na, out_shape=out_shape,
        scratch_shapes=[pltpu.SemaphoreType.DMA((na * npeer,)), pltpu.SemaphoreType.DMA((na * npeer,)),
                        pltpu.SemaphoreType.DMA((na,))],
        compiler_params=pltpu.CompilerParams(has_side_effects=True),
    )(*arrs)


def _peer_table():
    x, y, c = lax.axis_index("x"), lax.axis_index("y"), lax.axis_index("c")
    me = 4 * x + 2 * y + c
    peers = []
    for fx, fy, fc in _FLIPS:
        px = 1 - x if fx else x
        py = 1 - y if fy else y
        pc = 1 - c if fc else c
        peers.append(((px, py, pc), 4 * px + 2 * py + pc))
    return me, peers


_HBM = pl.BlockSpec(memory_space=pltpu.HBM)
_SEM = pl.BlockSpec(memory_space=pltpu.SEMAPHORE)
_ANY = pl.BlockSpec(memory_space=pl.ANY)
_EFFECT = pltpu.SideEffectType.DATAFLOW_SIDE_EFFECTING


def _split_copy(srcs_are_pieces, src_refs, land_refs, send_sem, recv_sem, a, kk, me, peers, arriving):
    dev, lin = peers[kk]
    npeer = len(_FLIPS)
    src = src_refs[a] if srcs_are_pieces[a] else src_refs[a].at[lin]
    dst = land_refs[a].at[lin] if arriving else land_refs[a].at[me]
    return pltpu.make_async_remote_copy(src_ref=src, dst_ref=dst, send_sem=send_sem.at[a * npeer + kk],
                                        recv_sem=recv_sem.at[a * npeer + kk], device_id=dev, device_id_type=MESH_ID)


def _xchg_start(name, gather, scatter, after=None):
    me_out = 4 * lax.axis_index("x") + 2 * lax.axis_index("y") + lax.axis_index("c")
    srcs = list(gather) + list(scatter)
    is_piece = [True] * len(gather) + [False] * len(scatter)
    lands = []
    for a, piece in zip(srcs, is_piece):
        own = a[None] if piece else lax.dynamic_slice_in_dim(a, me_out, 1, axis=0)
        shape = ((N_DEV,) + tuple(a.shape)) if piece else tuple(a.shape)
        start = (me_out,) + (0,) * (len(shape) - 1)
        lands.append(lax.dynamic_update_slice(lax.empty(shape, a.dtype), own, start))
    n = len(srcs)
    nsem = n * len(_FLIPS)
    has_after = after is not None

    def body(*refs):
        src_refs = refs[:n]
        land_refs = refs[n:2 * n]
        outs = refs[2 * n + (1 if has_after else 0):]
        send_sem, recv_sem = outs[0], outs[1]
        token = outs[-1]
        me, peers = _peer_table()
        for kk in range(len(_FLIPS)):
            for a in range(n):
                _split_copy(is_piece, src_refs, land_refs, send_sem, recv_sem, a, kk, me, peers, False).start()
        token[...] = jnp.zeros_like(token)

    out_shape = ([pltpu.SemaphoreType.DMA((nsem,)), pltpu.SemaphoreType.DMA((nsem,))]
                 + [pltpu.HBM(tuple(a.shape), a.dtype) for a in srcs] + [pltpu.HBM(tuple(a.shape), a.dtype) for a in lands]
                 + [jax.ShapeDtypeStruct((8, 128), F32)])
    args = [pltpu.with_memory_space_constraint(a, pltpu.HBM) for a in srcs + lands] + ([after] if has_after else [])
    res = pl.pallas_call(
        body, name=name, out_shape=out_shape,
        in_specs=[_HBM] * (2 * n) + ([_ANY] if has_after else []),
        out_specs=[_SEM, _SEM] + [_HBM] * (2 * n) + [pl.BlockSpec(memory_space=pltpu.VMEM)],
        input_output_aliases={i: 2 + i for i in range(2 * n)},
        compiler_params=pltpu.CompilerParams(has_side_effects=_EFFECT),
    )(*args)
    state = (res[0], res[1], list(res[2:2 + n]), list(res[2 + n:2 + 2 * n]), is_piece)
    return state, res[-1]


def _xchg_wait(name, state, after):
    send_sem, recv_sem, srcs, lands, is_piece = state
    n = len(srcs)

    def body(*refs):
        src_refs = refs[:n]
        land_refs = refs[n:2 * n]
        s_sem, r_sem = refs[2 * n], refs[2 * n + 1]
        me, peers = _peer_table()
        for kk in range(len(_FLIPS)):
            for a in range(n):
                cp = _split_copy(is_piece, src_refs, land_refs, s_sem, r_sem, a, kk, me, peers, True)
                cp.wait_send()
                cp.wait_recv()

    out_shape = [pltpu.HBM(tuple(a.shape), a.dtype) for a in srcs] + [pltpu.HBM(tuple(a.shape), a.dtype) for a in lands]
    res = pl.pallas_call(
        body, name=name, out_shape=out_shape,
        in_specs=[_HBM] * (2 * n) + [_SEM, _SEM, _ANY], out_specs=[_HBM] * (2 * n),
        input_output_aliases={i: i for i in range(2 * n)},
        compiler_params=pltpu.CompilerParams(has_side_effects=_EFFECT),
    )(*srcs, *lands, send_sem, recv_sem, after)
    return list(res[n:])


_SIB = (0, 0, 1)
_ICI = ((0, 1, 0), (1, 0, 0), (1, 1, 0))


def _flip(fl):
    x, y, c = lax.axis_index("x"), lax.axis_index("y"), lax.axis_index("c")
    px = 1 - x if fl[0] else x
    py = 1 - y if fl[1] else y
    pc = 1 - c if fl[2] else c
    return (px, py, pc), 4 * px + 2 * py + pc


def _gather2_start(name, pieces, after=None):
    me_out = 4 * lax.axis_index("x") + 2 * lax.axis_index("y") + lax.axis_index("c")
    pieces = list(pieces)
    n = len(pieces)
    lands = [lax.dynamic_update_slice(lax.empty((N_DEV,) + tuple(a.shape), a.dtype), a[None],
                                      (me_out,) + (0,) * a.ndim) for a in pieces]
    first = (_SIB,) + _ICI
    has_after = after is not None

    def body(*refs):
        src_refs, land_refs = refs[:n], refs[n:2 * n]
        outs = refs[2 * n + (1 if has_after else 0):]
        send_sem, recv_sem, token = outs[0], outs[1], outs[-1]
        _, me = _flip((0, 0, 0))
        for kk, fl in enumerate(first):
            dev, _ = _flip(fl)
            for a in range(n):
                pltpu.make_async_remote_copy(src_ref=src_refs[a], dst_ref=land_refs[a].at[me],
                                             send_sem=send_sem.at[a * 4 + kk], recv_sem=recv_sem.at[a * 4 + kk],
                                             device_id=dev, device_id_type=MESH_ID).start()
        token[...] = jnp.zeros_like(token)

    hbm = [pltpu.HBM(tuple(a.shape), a.dtype) for a in pieces + lands]
    res = pl.pallas_call(
        body, name=name,
        out_shape=[pltpu.SemaphoreType.DMA((4 * n,)), pltpu.SemaphoreType.DMA((4 * n,))] + hbm
        + [jax.ShapeDtypeStruct((8, 128), F32)],
        in_specs=[_HBM] * (2 * n) + ([_ANY] if has_after else []),
        out_specs=[_SEM, _SEM] + [_HBM] * (2 * n) + [pl.BlockSpec(memory_space=pltpu.VMEM)],
        input_output_aliases={i: 2 + i for i in range(2 * n)},
        compiler_params=pltpu.CompilerParams(has_side_effects=_EFFECT),
    )(*([pltpu.with_memory_space_constraint(a, pltpu.HBM) for a in pieces + lands] + ([after] if has_after else [])))
    return (res[0], res[1], list(res[2:2 + n]), list(res[2 + n:2 + 2 * n])), res[-1]


def _gather2_forward(name, state, after):
    send_a, recv_a, pieces, lands = state
    n = len(pieces)
    first = (_SIB,) + _ICI

    def body(*refs):
        src_refs, land_refs = refs[:n], refs[n:2 * n]
        s_a, r_a = refs[2 * n], refs[2 * n + 1]
        outs = refs[2 * n + 3:]
        send_b, recv_b, token = outs[0], outs[1], outs[-1]
        for kk, fl in enumerate(first):
            dev, lin = _flip(fl)
            for a in range(n):
                cp = pltpu.make_async_remote_copy(src_ref=src_refs[a], dst_ref=land_refs[a].at[lin],
                                                  send_sem=s_a.at[a * 4 + kk], recv_sem=r_a.at[a * 4 + kk],
                                                  device_id=dev, device_id_type=MESH_ID)
                cp.wait_send()
                cp.wait_recv()
        sib, _ = _flip(_SIB)
        for j, fl in enumerate(_ICI):
            _, lin = _flip(fl)
            for a in range(n):
                pltpu.make_async_remote_copy(src_ref=land_refs[a].at[lin], dst_ref=land_refs[a].at[lin],
                                             send_sem=send_b.at[a * 3 + j], recv_sem=recv_b.at[a * 3 + j],
                                             device_id=sib, device_id_type=MESH_ID).start()
        token[...] = jnp.zeros_like(token)

    hbm = [pltpu.HBM(tuple(a.shape), a.dtype) for a in pieces + lands]
    res = pl.pallas_call(
        body, name=name,
        out_shape=[pltpu.SemaphoreType.DMA((3 * n,)), pltpu.SemaphoreType.DMA((3 * n,))] + hbm
        + [jax.ShapeDtypeStruct((8, 128), F32)],
        in_specs=[_HBM] * (2 * n) + [_SEM, _SEM, _ANY],
        out_specs=[_SEM, _SEM] + [_HBM] * (2 * n) + [pl.BlockSpec(memory_space=pltpu.VMEM)],
        input_output_aliases={i: 2 + i for i in range(2 * n)},
        compiler_params=pltpu.CompilerParams(has_side_effects=_EFFECT),
    )(*pieces, *lands, send_a, recv_a, after)
    return (res[0], res[1], list(res[2 + n:2 + 2 * n])), res[-1]


def _gather2_wait(name, state, after):
    send_b, recv_b, lands = state
    n = len(lands)

    def body(*refs):
        land_refs = refs[:n]
        s_b, r_b = refs[n], refs[n + 1]
        sib, _ = _flip(_SIB)
        for j, fl in enumerate(_ICI):
            _, sent = _flip(fl)
            _, arriving = _flip((fl[0], fl[1], 1))
            for a in range(n):
                cp = pltpu.make_async_remote_copy(src_ref=land_refs[a].at[sent], dst_ref=land_refs[a].at[arriving],
                                                  send_sem=s_b.at[a * 3 + j], recv_sem=r_b.at[a * 3 + j],
                                                  device_id=sib, device_id_type=MESH_ID)
                cp.wait_send()
                cp.wait_recv()

    res = pl.pallas_call(
        body, name=name, out_shape=[pltpu.HBM(tuple(a.shape), a.dtype) for a in lands],
        in_specs=[_HBM] * n + [_SEM, _SEM, _ANY], out_specs=[_HBM] * n,
        input_output_aliases={i: i for i in range(n)},
        compiler_params=pltpu.CompilerParams(has_side_effects=_EFFECT),
    )(*lands, send_b, recv_b, after)
    return list(res)


def _adam_math(w, g, m, v):
    m = ADAM_B1 * m + (1.0 - ADAM_B1) * g
    v = ADAM_B2 * v + (1.0 - ADAM_B2) * (g * g)
    m_hat = m / (1.0 - ADAM_B1 ** ADAM_STEP)
    v_hat = v / (1.0 - ADAM_B2 ** ADAM_STEP)
    delta = -ADAM_LR * (m_hat / (jnp.sqrt(v_hat) + ADAM_EPS) + ADAM_WD * w)
    return delta, m, v


def _adam(name, w, m, v, parts, transposed=False, dep=None):
    npart, _, cp = parts.shape
    has_dep = dep is not None
    if transposed:
        c, r = w.shape
        tr = _pick(r, (256, 128))
        blk = pl.BlockSpec((c, tr), lambda i: (0, i))
    else:
        r, c = w.shape
        tr = _pick(r, (256, 176, 128, 64, 16, 8, 1))
        blk = pl.BlockSpec((tr, c), lambda i: (i, 0))

    def body(w_ref, m_ref, v_ref, p_ref, *rest):
        g_ref, d_ref, mo_ref, vo_ref = rest[-4:]
        g = p_ref[0].astype(F32)
        for pp in range(1, npart):
            g = g + p_ref[pp].astype(F32)
        g = g.T[0:c, :] if transposed else g[:, 0:c]
        delta, mn, vn = _adam_math(w_ref[...], g, m_ref[...], v_ref[...])
        g_ref[...] = g
        d_ref[...] = delta
        mo_ref[...] = mn
        vo_ref[...] = vn

    out = jax.ShapeDtypeStruct(w.shape, F32)
    return pl.pallas_call(
        body, name=name, grid=(r // tr,),
        in_specs=[blk, blk, blk, pl.BlockSpec((npart, tr, cp), lambda i: (0, i, 0))]
        + ([pl.BlockSpec(memory_space=pl.ANY)] if has_dep else []),
        out_specs=[blk, blk, blk, blk], out_shape=[out, out, out, out], compiler_params=_params(("parallel",)),
    )(*((w, m, v, parts) + ((dep,) if has_dep else ())))


def _small_sum(name, packs):
    def body(p_ref, o_ref):
        tot = p_ref[0]
        for pp in range(1, N_DEV):
            tot = tot + p_ref[pp]
        o_ref[0:8, :] = tot[0:8, :]
        o_ref[8:24, :] = tot[8:24, :] + tot[24:40, :]

    return pl.pallas_call(body, name=name, out_shape=jax.ShapeDtypeStruct((24, D_MODEL), F32),
                          compiler_params=_params())(packs)


def _local_step(x, tgt, g1, gm, g2, gf, b_forget, sinks, weights, send):
    b, s, _ = x.shape
    l = s + PREFIX
    t = b * l
    (meta,) = weights("meta", x)

    h0, n1 = _embed_norm("embed_rms1_fwd", x, meta, g1)
    (w1i,) = weights("ffn1_in", n1)
    gu1, a1 = _ffn_in_fwd("ffn1_in_fwd", n1, w1i)
    (w1o,) = weights("ffn1_out", weights("ffn1_out:forward", a1))
    h1, um = _mm_res_norm("ffn1_out_fwd", a1, w1o, h0, gm, alpha=0.5)
    wi, wa, wb, wo = weights("mix", weights("mix:forward", um))
    qkv, gates, f2 = _proj_fwd("proj_fwd", um, wi)
    qkv3 = qkv.reshape(b, l, QKV_W)
    f3 = f2.reshape(b, l, 128)
    bf_row = jnp.pad(b_forget, ((0, 0), (0, 128 - B_HEADS)))
    c_col, c_row = _fgate_fwd("fgate_fwd", f3, bf_row)
    head_of_row = jnp.arange(STACK) // BLOCK
    slopes = jnp.exp2(-8.0 * (head_of_row + 1).astype(F32) / A_HEADS).reshape(STACK, 1)
    sink_rows = jnp.repeat(sinks.reshape(A_HEADS), BLOCK).reshape(STACK, 1)
    swa_bias = _swa_bias(slopes)
    oa3, lse_a = _swa_fwd("swa_fwd", qkv3, slopes, sink_rows, swa_bias)
    ob3, lse_b = _fox_fwd("fox_fwd", qkv3, c_col, c_row)
    oa = oa3.reshape(t, A_WIDTH)
    ob = ob3.reshape(t, B_WIDTH)
    mixed, ya, yb = _branch_gate_fwd("branch_gate_fwd", oa, ob, wa, wb, gates, dep=weights("ffn2:forward", ob))
    h2, n2 = _mm_res_norm("mix_out_fwd", mixed, wo, h1, g2)
    w2i, w2o = weights("ffn2", h2)
    gu2, a2 = _ffn_in_fwd("ffn2_in_fwd", n2, w2i)

    dh3, loss_blk, dgf = _ffn_out_loss("ffn2_out_loss", a2, w2o, h2, gf, tgt, 0.5)

    def ffn_bwd(tag, dh, h_in, g_norm, n_in, gu, a, w_in_blk, w_out, one_send, examples=None):
        dw_out = _mm_tn(tag + "_out_bwd_w", a, dh, alpha=0.5)
        dgu = _ffn_out_bwd_x(tag + "_out_bwd_x", dh, w_out, gu, dep=None if one_send else send(tag + "_out", (dw_out,)))
        dw_in = _ffn_in_bwd_w(tag + "_in_bwd_w", n_in, dgu)
        token = send(tag, (dw_in, dw_out)) if one_send else send(tag + "_in", (dw_in,))
        return _ffn_in_bwd_x(tag + "_in_bwd_x", dgu, w_in_blk, h_in, g_norm, dh, dep=token, examples=examples)

    dh2, dg2 = ffn_bwd("ffn2", dh3, h2, g2, n2, gu2, a2, w2i, w2o, True)

    dwo = _mm_tn("mix_out_bwd_w", mixed, dh2)
    dya, dyb, dgates = _mix_out_gate_bwd("mix_out_gate_bwd", dh2, wo, gates, ya, yb)
    doa, dob = _branch_bwd_x("branch_bwd_x", dya, dyb, wa, wb)
    dwa, dwb = _branch_bwd_w("branch_bwd_w", oa, ob, dya, dyb)
    dqkv3, dcq, dck = _fox_bwd("fox_bwd", qkv3, c_col, c_row, ob3, lse_b, dob.reshape(b, l, B_WIDTH))
    dqkv3, dsink = _swa_bwd("swa_bwd", qkv3, oa3, lse_a, doa.reshape(b, l, A_WIDTH), slopes, sink_rows, swa_bias, dqkv3)
    dqkv = dqkv3.reshape(t, QKV_W)
    df3, dbf = _fgate_bwd("fgate_bwd", f3, bf_row, dcq, dck)
    df = df3.reshape(t, 128)
    dwi_qkv = _mm_tn("proj_qkv_bwd_w", um, dqkv, tm_c=(512,), tn_c=(768,))
    dwi_g = _mm_tn("proj_gates_bwd_w", um, dgates, tm_c=(512,), tn_c=(512,))
    dwi_f = _mm_tn("proj_f_bwd_w", um, df, tm_c=(512,), tn_c=(128,))
    token = send("mix", (dwi_qkv, dwi_g, dwi_f, dwa, dwb, dwo))
    dh1, dgm = _proj_bwd_x("proj_bwd_x", dqkv, dgates, df, wi, h1, gm, dh2, dep=token)

    grad_x, dmeta3, dg1 = ffn_bwd("ffn1", dh1, h0, g1, n1, gu1, a1, w1i, w1o, False, examples=b)
    dmeta = dmeta3.reshape(b * N_META, D_MODEL)

    misc = jnp.concatenate([dbf[:, 0:B_HEADS], dsink[:, 0].reshape(1, A_HEADS), loss_blk[0:1, 0:1]], axis=1)
    misc = jnp.pad(misc, ((0, 0), (0, D_MODEL - misc.shape[1])))
    row = lax.broadcasted_iota(jnp.int32, (8, D_MODEL), 0)
    vec = jnp.zeros((8, D_MODEL), F32)
    for i, piece in enumerate((dg1, dgm, dg2, dgf, misc)):
        vec = jnp.where(row == i, piece, vec)
    small = jnp.concatenate([vec, dmeta], axis=0)
    return grad_x, small


def _pad_to(a, rows, cols):
    return jnp.pad(a, ((0, rows - a.shape[0]), (0, cols - a.shape[1])))


def _ffn_out_from_gathered(g):
    w = g.reshape(4, FF_SHARD, D_MODEL)
    return jnp.pad(w, ((0, 0), (0, FF_SHARD_P - FF_SHARD), (0, 0))).reshape(D_FF_P, D_MODEL)


def _ffn_out_to_scatter(dw):
    return dw.reshape(4, FF_SHARD_P, D_MODEL)[:, 0:FF_SHARD, :].reshape(N_DEV, FFO_SHARD, D_MODEL)


def _proj_segments():
    segs = [(HEAD_DIM * h, HEAD_DIM, 0, B_SEG + HEAD_DIM * A_HEAD_ORDER.index(h)) for h in range(A_HEADS)]
    segs += [(512, 128, 0, B_SEG + A_WIDTH), (640, 128, 0, B_SEG + A_WIDTH + 128)]
    for first, off in ((768, 0), (1280, 128), (1792, 256)):
        segs += [(first + 128 * hp, 128, 0, PAIR_W * hp + off) for hp in range(4)]
    segs += [(2304, B_HEADS, 2, 0), (2312, 2 * D_MODEL, 1, 0)]
    return segs


def _proj_from_gathered(g):
    def cols(first, width):
        out = []
        for p in range(N_DEV):
            lo, hi = max(first, WIN_SHARD * p), min(first + width, WIN_SHARD * (p + 1))
            if lo < hi:
                out.append(g[p, :, lo - WIN_SHARD * p:hi - WIN_SHARD * p])
        return out

    parts = []
    for arr in (0, 1, 2):
        for first, width, _, _ in sorted((s for s in _proj_segments() if s[2] == arr), key=lambda s: s[3]):
            parts += cols(first, width)
        if arr == 0:
            parts.append(jnp.zeros((D_MODEL, P_GATES - QKV_W), g.dtype))
    parts.append(jnp.zeros((D_MODEL, 128 - B_HEADS), g.dtype))
    return jnp.concatenate(parts, axis=1)


def _proj_to_scatter(dqkv_w, dg_w, df_w):
    arrays = (dqkv_w, dg_w, df_w)
    segs = sorted(_proj_segments())
    blocks = []
    for p in range(N_DEV):
        parts = []
        for first, width, arr, at in segs:
            lo, hi = max(first, WIN_SHARD * p), min(first + width, WIN_SHARD * (p + 1))
            if lo < hi:
                parts.append(arrays[arr][:, at + lo - first:at + hi - first])
        parts.append(jnp.zeros((D_MODEL, WIN_SHARD_P - WIN_SHARD), dqkv_w.dtype))
        blocks.append(jnp.concatenate(parts, axis=1))
    return jnp.stack(blocks, axis=0)


def _a_rows_from_natural(w):
    return jnp.concatenate([w[HEAD_DIM * h:HEAD_DIM * (h + 1)] for h in A_HEAD_ORDER], axis=0)


def _a_rows_to_natural(w):
    return jnp.concatenate([w[HEAD_DIM * A_HEAD_ORDER.index(h):HEAD_DIM * (A_HEAD_ORDER.index(h) + 1)]
                            for h in range(A_HEADS)], axis=0)


def kernel(x, meta_tokens, ffn1_norm, ffn1_w_in, ffn1_w_out, mix_norm, w_in, b_forget, attn_sinks, w_branch_a, w_branch_b, w_out, ffn2_norm, ffn2_w_in, ffn2_w_out, final_norm, loss_target, m_meta_tokens, m_ffn1_norm, m_ffn1_w_in, m_ffn1_w_out, m_mix_norm, m_w_in, m_b_forget, m_attn_sinks, m_w_branch_a, m_w_branch_b, m_w_out, m_ffn2_norm, m_ffn2_w_in, m_ffn2_w_out, m_final_norm, v_meta_tokens, v_ffn1_norm, v_ffn1_w_in, v_ffn1_w_out, v_mix_norm, v_w_in, v_b_forget, v_attn_sinks, v_w_branch_a, v_w_branch_b, v_w_out, v_ffn2_norm, v_ffn2_w_in, v_ffn2_w_out, v_final_norm):
    me = 4 * lax.axis_index("x") + 2 * lax.axis_index("y") + lax.axis_index("c")

    shards = (
        _pad_to(ffn1_w_in[0].astype(BF16), D_MODEL, FF_SHARD_P),
        ffn1_w_out[0].astype(BF16),
        _pad_to(w_in[0].astype(BF16), D_MODEL, WIN_SHARD_P),
        w_branch_a[0].astype(BF16), w_branch_b[0].astype(BF16), w_out[0].astype(BF16),
        _pad_to(ffn2_w_in[0].astype(BF16), D_MODEL, FF_SHARD_P),
        ffn2_w_out[0].astype(BF16),
        meta_tokens,
    )
    s1i, s1o, swi, swa, swb, swo, s2i, s2o, smeta = shards
    first_level, second_level = {}, {}
    first_level["meta"], tok = _gather2_start("gather_meta_start", (smeta,))
    first_level["ffn1_in"], tok = _gather2_start("gather_ffn1_in_start", (s1i,), after=tok)
    first_level["ffn1_out"], tok = _gather2_start("gather_ffn1_out_start", (s1o,), after=tok)
    first_level["mix"], tok = _gather2_start("gather_mix_start", (swi, swa, swb, swo), after=tok)
    first_level["ffn2"], tok = _gather2_start("gather_ffn2_start", (s2i, s2o), after=tok)
    started = {"tok": tok}

    def weights(group, after):
        if group.endswith(":forward"):
            group = group[:-len(":forward")]
            second_level[group], token = _gather2_forward("gather_" + group + "_forward", first_level[group], after)
            return token
        if group == "meta":
            after = weights("meta:forward", started["tok"])
        if group == "ffn1_in":
            after = weights("ffn1_in:forward", after)
        got = _gather2_wait("gather_" + group + "_wait", second_level[group], after)
        if group == "mix":
            gwi, gwa, gwb, gwo = got
            return (_proj_from_gathered(gwi), _a_rows_from_natural(gwa.transpose(1, 0, 2).reshape(A_WIDTH, D_MODEL)),
                    gwb.transpose(1, 0, 2).reshape(B_WIDTH, D_MODEL), gwo.reshape(D_MODEL, D_MODEL))
        if group == "ffn1_out":
            return (_ffn_out_from_gathered(got[0]),)
        if group == "meta":
            return (got[0].transpose(1, 0, 2).reshape(N_META, D_MODEL),)
        if group == "ffn1_in":
            return (got[0].reshape(2, 4, D_MODEL, FF_SHARD_P),)
        return got[0].reshape(2, 4, D_MODEL, FF_SHARD_P), _ffn_out_from_gathered(got[1])

    scatter_state = {}

    def send(group, grads):
        if group == "mix":
            dwi_qkv, dwi_g, dwi_f, dwa, dwb, dwo = grads
            dwa = _a_rows_to_natural(dwa)
            blocks = (_proj_to_scatter(dwi_qkv, dwi_g, dwi_f), dwa.reshape(A_WIDTH, N_DEV, 128).transpose(1, 0, 2),
                      dwb.reshape(B_WIDTH, N_DEV, 128).transpose(1, 0, 2), dwo.reshape(N_DEV, 128, D_MODEL))
        elif group.endswith("_in"):
            blocks = (grads[0].reshape(N_DEV, D_MODEL, FF_SHARD_P),)
        elif group.endswith("_out"):
            blocks = (_ffn_out_to_scatter(grads[0]),)
        else:
            blocks = (grads[0].reshape(N_DEV, D_MODEL, FF_SHARD_P), _ffn_out_to_scatter(grads[1]))
        scatter_state[group], token = _xchg_start("scatter_" + group + "_start", (), blocks)
        return token

    gf = final_norm.reshape(1, D_MODEL)
    grad_x, small = _local_step(x, loss_target, ffn1_norm, mix_norm, ffn2_norm, gf, b_forget, attn_sinks, weights, send)

    small_state, after = _xchg_start("gather_small_start", (small,), ())
    out = {}
    updates = (
        ("ffn2", (("ffn2_w_in", ffn2_w_in, m_ffn2_w_in, v_ffn2_w_in), ("ffn2_w_out", ffn2_w_out, m_ffn2_w_out, v_ffn2_w_out))),
        ("ffn1_out", (("ffn1_w_out", ffn1_w_out, m_ffn1_w_out, v_ffn1_w_out),)),
        ("mix", (("w_in", w_in, m_w_in, v_w_in), ("w_branch_a", w_branch_a, m_w_branch_a, v_w_branch_a),
                 ("w_branch_b", w_branch_b, m_w_branch_b, v_w_branch_b), ("w_out", w_out, m_w_out, v_w_out))),
    )
    last_update = ("ffn1_in", (("ffn1_w_in", ffn1_w_in, m_ffn1_w_in, v_ffn1_w_in),))

    def update(group, members, after):
        parts_list = _xchg_wait("scatter_" + group + "_wait", scatter_state[group], after)
        prev = None
        for (nm, w, m, v), parts in zip(members, parts_list):
            if nm.endswith("w_in"):
                res4 = _adam("adam_" + nm, w[0].T, m[0].T, v[0].T, parts, transposed=True, dep=prev)
                out[nm] = tuple(r.T[None] for r in res4)
            else:
                res4 = _adam("adam_" + nm, w[0], m[0], v[0], parts, dep=prev)
                out[nm] = tuple(r[None] for r in res4)
            prev = res4[0]
        return prev

    for group, members in updates + (last_update,):
        after = update(group, members, after)
    (packs,) = _xchg_wait("gather_small_wait", small_state, after)

    tot = _small_sum("small_sum", packs)
    loss = tot[4, 2 * B_HEADS]
    g_meta = lax.dynamic_slice(tot[8:24, :], (0, me * 128), (N_META, 128))
    out["meta_tokens"] = tuple(_adam("adam_meta_tokens", meta_tokens, m_meta_tokens, v_meta_tokens, g_meta[None]))

    def pack_small(n1, nm, n2, nf, bfv, skv):
        misc = jnp.pad(jnp.concatenate([bfv, skv], axis=1), ((0, 0), (0, D_MODEL - 2 * B_HEADS)))
        row = lax.broadcasted_iota(jnp.int32, (8, D_MODEL), 0)
        vec = jnp.zeros((8, D_MODEL), F32)
        for i, piece in enumerate((n1, nm, n2, nf.reshape(1, D_MODEL), misc)):
            vec = jnp.where(row == i, piece, vec)
        return vec

    w_pack = pack_small(ffn1_norm, mix_norm, ffn2_norm, final_norm, b_forget, attn_sinks)
    m_pack = pack_small(m_ffn1_norm, m_mix_norm, m_ffn2_norm, m_final_norm, m_b_forget, m_attn_sinks)
    v_pack = pack_small(v_ffn1_norm, v_mix_norm, v_ffn2_norm, v_final_norm, v_b_forget, v_attn_sinks)
    small4 = _adam("adam_small", w_pack, m_pack, v_pack, tot[0:8][None])
    for i, nm in enumerate(("ffn1_norm", "mix_norm", "ffn2_norm")):
        out[nm] = tuple(r[i:i + 1] for r in small4)
    out["final_norm"] = tuple(r[3] for r in small4)
    out["b_forget"] = tuple(r[4:5, 0:B_HEADS] for r in small4)
    out["attn_sinks"] = tuple(r[4:5, B_HEADS:2 * B_HEADS] for r in small4)

    names = ("meta_tokens", "ffn1_norm", "ffn1_w_in", "ffn1_w_out", "mix_norm", "w_in", "b_forget", "attn_sinks",
             "w_branch_a", "w_branch_b", "w_out", "ffn2_norm", "ffn2_w_in", "ffn2_w_out", "final_norm")
    return (loss, grad_x) + tuple(out[nm][kind] for kind in range(4) for nm in names)
```

```python
import jax
import jax.numpy as jnp
from jax import lax
from jax.experimental import pallas as pl
from jax.experimental.pallas import tpu as pltpu

F32 = jnp.float32
BF16 = jnp.bfloat16

D_MODEL = 1024
N_META = 16
BLOCK = 128
PREFIX = 128
N_PAD = PREFIX - N_META
HEAD_DIM = 64
A_HEADS = 8
B_HEADS = 8
A_WIDTH = 512
A_KV_WIDTH = 128
B_WIDTH = 512
D_FF = 2816
N_DEV = 8
FF_SHARD = 2 * D_FF // N_DEV
FF_SHARD_P = 768
FFO_SHARD = D_FF // N_DEV
D_FF_P = 4 * FF_SHARD_P
W_IN_COLS = 4360
WIN_SHARD = W_IN_COLS // N_DEV
WIN_SHARD_P = 640
PAIR_W = 3 * 128
B_SEG = 4 * PAIR_W
A_SEG = A_WIDTH + 2 * A_KV_WIDTH
QKV_W = B_SEG + A_SEG
P_GATES = 2 * (2 * D_MODEL)
P_F = P_GATES + 2 * D_MODEL
PROJ_P = P_F + 128
A_HEAD_ORDER = (0, 4, 1, 5, 2, 6, 3, 7)
EPS = 1e-6
NEG = -1e30
SCALE = HEAD_DIM ** -0.5
ADAM_LR = 0.001
ADAM_B1 = 0.9
ADAM_B2 = 0.999
ADAM_EPS = 1e-08
ADAM_WD = 0.01
ADAM_STEP = 10
VMEM_LIMIT = 56 * 1024 * 1024
MESH_ID = pl.DeviceIdType.MESH
SMALL_ROWS = 40

_NN = (((1,), (0,)), ((), ()))
_NT = (((1,), (1,)), ((), ()))
_TN = (((0,), (0,)), ((), ()))


def _params(sem=None):
    return pltpu.CompilerParams(dimension_semantics=sem, vmem_limit_bytes=VMEM_LIMIT)


def _pick(n, cands):
    for c in cands:
        if n % c == 0:
            return c
    raise ValueError(f"no tile for {n}")


def _bf(v):
    return v if v.dtype == BF16 else v.astype(BF16)


def _mm(name, a, b, dims, grid, a_spec, b_spec, o_spec, out_shape, out_dtype, alpha=1.0):
    def body(a_ref, b_ref, o_ref):
        acc = lax.dot_general(_bf(a_ref[...]), _bf(b_ref[...]), dims, preferred_element_type=F32)
        if alpha != 1.0:
            acc = acc * alpha
        o_ref[...] = acc.astype(o_ref.dtype)

    return pl.pallas_call(
        body, name=name, grid=grid, in_specs=[a_spec, b_spec], out_specs=o_spec,
        out_shape=jax.ShapeDtypeStruct(out_shape, out_dtype),
        compiler_params=_params(("parallel",) * len(grid)),
    )(a, b)


def _mm_res_norm(name, a, w, res, g_next, alpha=1.0):
    t, k = a.shape
    tm = _pick(t, (544, 384, 256, 128))

    def body(a_ref, w_ref, r_ref, g_ref, h_ref, n_ref):
        acc = lax.dot_general(_bf(a_ref[...]), w_ref[...], _NN, preferred_element_type=F32)
        if alpha != 1.0:
            acc = acc * alpha
        hv = acc + r_ref[...]
        h_ref[...] = hv
        r = lax.rsqrt(jnp.mean(hv * hv, axis=-1, keepdims=True) + EPS)
        n_ref[...] = ((hv * r) * g_ref[...]).astype(BF16)

    row = pl.BlockSpec((tm, D_MODEL), lambda i: (i, 0))
    return pl.pallas_call(
        body, name=name, grid=(t // tm,),
        in_specs=[pl.BlockSpec((tm, k), lambda i: (i, 0)), pl.BlockSpec((k, D_MODEL), lambda i: (0, 0)), row,
                  pl.BlockSpec((1, D_MODEL), lambda i: (0, 0))],
        out_specs=[row, row],
        out_shape=[jax.ShapeDtypeStruct((t, D_MODEL), F32), jax.ShapeDtypeStruct((t, D_MODEL), BF16)],
        compiler_params=_params(("parallel",)),
    )(a, w, res, g_next)


def _mm_tn(name, a, b, out_dtype=BF16, alpha=1.0, tm_c=(768, 512, 256, 128), tn_c=(512, 640, 256, 128)):
    t, m = a.shape
    n = b.shape[1]
    tm = _pick(m, tm_c)
    tn = _pick(n, tn_c)
    bytes_a, bytes_b = a.size * a.dtype.itemsize, b.size * b.dtype.itemsize
    if bytes_a + bytes_b * (m // tm) <= bytes_b + bytes_a * (n // tn):
        return _mm(name, a, b, _TN, (m // tm, n // tn),
                   pl.BlockSpec((t, tm), lambda i, j: (0, i)), pl.BlockSpec((t, tn), lambda i, j: (0, j)),
                   pl.BlockSpec((tm, tn), lambda i, j: (i, j)), (m, n), out_dtype, alpha=alpha)
    return _mm(name, a, b, _TN, (n // tn, m // tm),
               pl.BlockSpec((t, tm), lambda j, i: (0, i)), pl.BlockSpec((t, tn), lambda j, i: (0, j)),
               pl.BlockSpec((tm, tn), lambda j, i: (i, j)), (m, n), out_dtype, alpha=alpha)


def _ffn_in_fwd(name, n, wblk, dep=None):
    t = n.shape[0]
    tm = _pick(t, (1088, 768, 512, 256, 128))
    has_dep = dep is not None

    def body(n_ref, w_ref, *rest):
        gu_ref, a_ref = rest[-2], rest[-1]
        nv = n_ref[...]
        g = lax.dot_general(nv, w_ref[0], _NN, preferred_element_type=F32)
        u = lax.dot_general(nv, w_ref[1], _NN, preferred_element_type=F32)
        sg = jax.nn.sigmoid(g)
        silu = g * sg
        a_ref[...] = (silu * u).astype(BF16)
        gu_ref[0] = ((0.5 * u) * (sg + silu * (1.0 - sg))).astype(BF16)
        gu_ref[1] = (0.5 * silu).astype(BF16)

    return pl.pallas_call(
        body, name=name, grid=(t // tm, 4),
        in_specs=[pl.BlockSpec((tm, D_MODEL), lambda i, j: (i, 0)),
                  pl.BlockSpec((2, None, D_MODEL, FF_SHARD_P), lambda i, j: (0, j, 0, 0))]
        + ([pl.BlockSpec(memory_space=pl.ANY)] if has_dep else []),
        out_specs=[pl.BlockSpec((2, tm, FF_SHARD_P), lambda i, j: (0, i, j)),
                   pl.BlockSpec((tm, FF_SHARD_P), lambda i, j: (i, j))],
        out_shape=[jax.ShapeDtypeStruct((2, t, D_FF_P), BF16), jax.ShapeDtypeStruct((t, D_FF_P), BF16)],
        compiler_params=_params(("parallel", "parallel")),
    )(*((n, wblk) + ((dep,) if has_dep else ())))


def _ffn_out_bwd_x(name, dh, w_out, gu, dep=None):
    t = dh.shape[0]
    tm = _pick(t, (1088, 768, 512, 256, 128))
    has_dep = dep is not None

    def body(dh_ref, w_ref, gu_ref, *rest):
        o_ref = rest[-1]
        da = lax.dot_general(_bf(dh_ref[...]), w_ref[...], _NT, preferred_element_type=F32)
        o_ref[0] = (da * gu_ref[0].astype(F32)).astype(BF16)
        o_ref[1] = (da * gu_ref[1].astype(F32)).astype(BF16)

    gu_spec = pl.BlockSpec((2, tm, FF_SHARD_P), lambda i, j: (0, i, j))
    return pl.pallas_call(
        body, name=name, grid=(t // tm, 4),
        in_specs=[pl.BlockSpec((tm, D_MODEL), lambda i, j: (i, 0)), pl.BlockSpec((FF_SHARD_P, D_MODEL), lambda i, j: (j, 0)),
                  gu_spec] + ([pl.BlockSpec(memory_space=pl.ANY)] if has_dep else []),
        out_specs=gu_spec, out_shape=jax.ShapeDtypeStruct((2, t, D_FF_P), BF16),
        compiler_params=_params(("parallel", "parallel")),
    )(*((dh, w_out, gu) + ((dep,) if has_dep else ())))


def _rms_bwd_rows(dn, h, g, dres):
    r = lax.rsqrt(jnp.mean(h * h, axis=-1, keepdims=True) + EPS)
    tv = dn * g
    dot = jnp.mean(tv * h, axis=-1, keepdims=True)
    return dres + (r * tv - h * (r * r * r * dot)), jnp.sum(dn * (h * r), axis=0, keepdims=True)


def _accumulate_rows(ref, part, first):
    @pl.when(first)
    def _():
        ref[...] = part

    @pl.when(jnp.logical_not(first))
    def _():
        ref[...] += part


def _ffn_in_bwd_x(name, dgu, wblk, h_in, g_norm, dres, dep=None, examples=None):
    t = dgu.shape[1]
    tm = _pick(t, (544, 384, 256, 128))
    has_dep = dep is not None
    split = examples is not None
    if split:
        l = t // examples
        per = l // tm
        assert per * tm == l and tm > PREFIX

    def body(d_ref, w_ref, h_ref, g_ref, r_ref, *rest):
        acc = None
        for s in range(2):
            for j in range(4):
                part = lax.dot_general(d_ref[s, :, FF_SHARD_P * j:FF_SHARD_P * (j + 1)], w_ref[s, j], _NT,
                                       preferred_element_type=F32)
                acc = part if acc is None else acc + part
        dh, dg = _rms_bwd_rows(acc, h_ref[...], g_ref[...], r_ref[...])
        i = pl.program_id(0)
        if not split:
            dh_ref, dg_ref = rest[-2], rest[-1]
            dh_ref[...] = dh
        else:
            gx_ref, meta_ref, dg_ref, buf, sem = rest[-5:]
            bi, r = i // per, i % per
            buf[...] = dh

            @pl.when(r == 0)
            def _():
                meta_ref[...] = dh[N_PAD:PREFIX]
                cp = pltpu.make_async_copy(buf.at[pl.ds(PREFIX, tm - PREFIX)], gx_ref.at[bi, pl.ds(0, tm - PREFIX)], sem)
                cp.start()
                cp.wait()

            if per > 1:
                @pl.when(r > 0)
                def _():
                    cp = pltpu.make_async_copy(buf, gx_ref.at[bi, pl.ds(pl.multiple_of(r * tm - PREFIX, 8), tm)], sem)
                    cp.start()
                    cp.wait()

        _accumulate_rows(dg_ref, dg, i == 0)

    row = pl.BlockSpec((tm, D_MODEL), lambda i: (i, 0))
    vec = pl.BlockSpec((1, D_MODEL), lambda i: (0, 0))
    if split:
        out_specs = [pl.BlockSpec(memory_space=pl.ANY), pl.BlockSpec((None, N_META, D_MODEL), lambda i: (i // per, 0, 0)), vec]
        out_shape = [jax.ShapeDtypeStruct((examples, l - PREFIX, D_MODEL), F32),
                     jax.ShapeDtypeStruct((examples, N_META, D_MODEL), F32), jax.ShapeDtypeStruct((1, D_MODEL), F32)]
        scratch = [pltpu.VMEM((tm, D_MODEL), F32), pltpu.SemaphoreType.DMA(())]
    else:
        out_specs = [row, vec]
        out_shape = [jax.ShapeDtypeStruct((t, D_MODEL), F32), jax.ShapeDtypeStruct((1, D_MODEL), F32)]
        scratch = []
    return pl.pallas_call(
        body, name=name, grid=(t // tm,),
        in_specs=[pl.BlockSpec((2, tm, D_FF_P), lambda i: (0, i, 0)),
                  pl.BlockSpec((2, 4, D_MODEL, FF_SHARD_P), lambda i: (0, 0, 0, 0), pipeline_mode=pl.Buffered(1)),
                  row, vec, row]
        + ([pl.BlockSpec(memory_space=pl.ANY)] if has_dep else []),
        out_specs=out_specs, out_shape=out_shape, scratch_shapes=scratch,
        compiler_params=_params(("arbitrary",)),
    )(*((dgu, wblk, h_in, g_norm, dres) + ((dep,) if has_dep else ())))


def _proj_fwd(name, um, wi):
    t = um.shape[0]
    tm = _pick(t, (544, 384, 256, 128))

    def body(u_ref, w_ref, q_ref, g_ref, f_ref):
        uv = u_ref[...]
        q_ref[...] = lax.dot_general(uv, w_ref[:, 0:QKV_W], _NN, preferred_element_type=F32).astype(BF16)
        g_ref[...] = lax.dot_general(uv, w_ref[:, P_GATES:P_F], _NN, preferred_element_type=F32)
        f_ref[...] = lax.dot_general(uv, w_ref[:, P_F:PROJ_P], _NN, preferred_element_type=F32)

    return pl.pallas_call(
        body, name=name, grid=(t // tm,),
        in_specs=[pl.BlockSpec((tm, D_MODEL), lambda i: (i, 0)),
                  pl.BlockSpec((D_MODEL, PROJ_P), lambda i: (0, 0), pipeline_mode=pl.Buffered(1))],
        out_specs=[pl.BlockSpec((tm, QKV_W), lambda i: (i, 0)), pl.BlockSpec((tm, 2 * D_MODEL), lambda i: (i, 0)),
                   pl.BlockSpec((tm, 128), lambda i: (i, 0))],
        out_shape=[jax.ShapeDtypeStruct((t, QKV_W), BF16), jax.ShapeDtypeStruct((t, 2 * D_MODEL), F32),
                   jax.ShapeDtypeStruct((t, 128), F32)],
        compiler_params=_params(("parallel",)),
    )(um, wi)


def _proj_bwd_x(name, dqkv, dgates, df, wi, h_in, g_norm, dres, dep=None):
    t = dqkv.shape[0]
    tm = _pick(t, (544, 384, 256, 128))
    has_dep = dep is not None

    def body(q_ref, gt_ref, f_ref, w_ref, h_ref, g_ref, r_ref, *rest):
        dh_ref, dg_ref = rest[-2], rest[-1]
        acc = lax.dot_general(q_ref[...], w_ref[:, 0:QKV_W], _NT, preferred_element_type=F32)
        acc = acc + lax.dot_general(gt_ref[...], w_ref[:, P_GATES:P_F], _NT, preferred_element_type=F32)
        acc = acc + lax.dot_general(f_ref[...], w_ref[:, P_F:PROJ_P], _NT, preferred_element_type=F32)
        dh, dg = _rms_bwd_rows(acc, h_ref[...], g_ref[...], r_ref[...])
        dh_ref[...] = dh
        _accumulate_rows(dg_ref, dg, pl.program_id(0) == 0)

    row = pl.BlockSpec((tm, D_MODEL), lambda i: (i, 0))
    vec = pl.BlockSpec((1, D_MODEL), lambda i: (0, 0))
    return pl.pallas_call(
        body, name=name, grid=(t // tm,),
        in_specs=[pl.BlockSpec((tm, QKV_W), lambda i: (i, 0)), pl.BlockSpec((tm, 2 * D_MODEL), lambda i: (i, 0)),
                  pl.BlockSpec((tm, 128), lambda i: (i, 0)),
                  pl.BlockSpec((D_MODEL, PROJ_P), lambda i: (0, 0), pipeline_mode=pl.Buffered(1)), row, vec, row]
        + ([pl.BlockSpec(memory_space=pl.ANY)] if has_dep else []),
        out_specs=[row, vec],
        out_shape=[jax.ShapeDtypeStruct((t, D_MODEL), F32), jax.ShapeDtypeStruct((1, D_MODEL), F32)],
        compiler_params=_params(("arbitrary",)),
    )(*((dqkv, dgates, df, wi, h_in, g_norm, dres) + ((dep,) if has_dep else ())))


def _ffn_in_bwd_w(name, n, dgu):
    t = n.shape[0]
    return _mm(name, n, dgu, _TN, (2, 4),
               pl.BlockSpec((t, D_MODEL), lambda s, j: (0, 0)),
               pl.BlockSpec((None, t, FF_SHARD_P), lambda s, j: (s, 0, j)),
               pl.BlockSpec((None, None, D_MODEL, FF_SHARD_P), lambda s, j: (s, j, 0, 0)),
               (2, 4, D_MODEL, FF_SHARD_P), BF16)


def _embed_norm(name, x, meta, g):
    b, s, _ = x.shape
    half = (s + PREFIX) // 2
    first = half - PREFIX
    assert first > 0 and half % 16 == 0

    def body(x_ref, m_ref, g_ref, h_ref, n_ref, buf, sem):
        bi, k = pl.program_id(0), pl.program_id(1)

        @pl.when(k == 0)
        def _():
            buf[0:N_PAD, :] = jnp.zeros((N_PAD, D_MODEL), F32)
            buf[N_PAD:PREFIX, :] = m_ref[...]
            cp = pltpu.make_async_copy(x_ref.at[bi, pl.ds(0, first)], buf.at[pl.ds(PREFIX, first)], sem)
            cp.start()
            cp.wait()

        @pl.when(k == 1)
        def _():
            cp = pltpu.make_async_copy(x_ref.at[bi, pl.ds(first, half)], buf, sem)
            cp.start()
            cp.wait()

        hv = buf[...]
        h_ref[...] = hv
        r = lax.rsqrt(jnp.mean(hv * hv, axis=-1, keepdims=True) + EPS)
        n_ref[...] = ((hv * r) * g_ref[...]).astype(BF16)

    rows = pl.BlockSpec((half, D_MODEL), lambda bi, k: (2 * bi + k, 0))
    return pl.pallas_call(
        body, name=name, grid=(b, 2),
        in_specs=[pl.BlockSpec(memory_space=pl.ANY), pl.BlockSpec((N_META, D_MODEL), lambda bi, k: (0, 0)),
                  pl.BlockSpec((1, D_MODEL), lambda bi, k: (0, 0))],
        out_specs=[rows, rows],
        out_shape=[jax.ShapeDtypeStruct((2 * b * half, D_MODEL), F32), jax.ShapeDtypeStruct((2 * b * half, D_MODEL), BF16)],
        scratch_shapes=[pltpu.VMEM((half, D_MODEL), F32), pltpu.SemaphoreType.DMA(())],
        compiler_params=_params(("arbitrary", "arbitrary")),
    )(x, meta, g)


def _branch_gate_fwd(name, oa, ob, wa, wb, gates, dep=None):
    t = gates.shape[0]
    tm = _pick(t, (544, 384, 256, 128))
    has_dep = dep is not None

    def body(oa_ref, ob_ref, wa_ref, wb_ref, g_ref, *rest):
        o_ref, ya_ref, yb_ref = rest[-3:]
        ya = lax.dot_general(_bf(oa_ref[...]), wa_ref[...], _NN, preferred_element_type=F32)
        yb = lax.dot_general(_bf(ob_ref[...]), wb_ref[...], _NN, preferred_element_type=F32)
        sa = jax.nn.sigmoid(g_ref[:, 0:D_MODEL])
        sb = jax.nn.sigmoid(g_ref[:, D_MODEL:2 * D_MODEL])
        o_ref[...] = (sa * ya + sb * yb).astype(BF16)
        ya_ref[...] = ya.astype(BF16)
        yb_ref[...] = yb.astype(BF16)

    blk = pl.BlockSpec((tm, D_MODEL), lambda i: (i, 0))
    narrow = pl.BlockSpec((tm, A_WIDTH), lambda i: (i, 0))
    wide = pl.BlockSpec((tm, 2 * D_MODEL), lambda i: (i, 0))
    wspec = pl.BlockSpec((A_WIDTH, D_MODEL), lambda i: (0, 0))
    out = jax.ShapeDtypeStruct((t, D_MODEL), BF16)
    return pl.pallas_call(
        body, name=name, grid=(t // tm,),
        in_specs=[narrow, narrow, wspec, wspec, wide] + ([pl.BlockSpec(memory_space=pl.ANY)] if has_dep else []),
        out_specs=[blk, blk, blk], out_shape=[out, out, out], compiler_params=_params(("parallel",)),
    )(*((oa, ob, wa, wb, gates) + ((dep,) if has_dep else ())))


def _branch_bwd_x(name, dya, dyb, wa, wb):
    t = dya.shape[0]
    tm = _pick(t, (1088, 768, 512, 256, 128))

    def body(da_ref, db_ref, wa_ref, wb_ref, oa_ref, ob_ref):
        oa_ref[...] = lax.dot_general(da_ref[...], wa_ref[...], _NT, preferred_element_type=F32)
        ob_ref[...] = lax.dot_general(db_ref[...], wb_ref[...], _NT, preferred_element_type=F32).astype(BF16)

    blk = pl.BlockSpec((tm, D_MODEL), lambda i: (i, 0))
    narrow = pl.BlockSpec((tm, A_WIDTH), lambda i: (i, 0))
    wspec = pl.BlockSpec((A_WIDTH, D_MODEL), lambda i: (0, 0), pipeline_mode=pl.Buffered(1))
    return pl.pallas_call(
        body, name=name, grid=(t // tm,), in_specs=[blk, blk, wspec, wspec], out_specs=[narrow, narrow],
        out_shape=[jax.ShapeDtypeStruct((t, A_WIDTH), F32), jax.ShapeDtypeStruct((t, B_WIDTH), BF16)],
        compiler_params=_params(("parallel",)),
    )(dya, dyb, wa, wb)


def _branch_bwd_w(name, oa, ob, dya, dyb):
    t = oa.shape[0]
    tn = 512

    def body(oa_ref, ob_ref, da_ref, db_ref, wa_ref, wb_ref):
        wa_ref[...] = lax.dot_general(_bf(oa_ref[...]), da_ref[...], _TN, preferred_element_type=F32).astype(BF16)
        wb_ref[...] = lax.dot_general(_bf(ob_ref[...]), db_ref[...], _TN, preferred_element_type=F32).astype(BF16)

    whole = pl.BlockSpec((t, A_WIDTH), lambda j: (0, 0), pipeline_mode=pl.Buffered(1))
    cols = pl.BlockSpec((t, tn), lambda j: (0, j))
    out_spec = pl.BlockSpec((A_WIDTH, tn), lambda j: (0, j))
    out = jax.ShapeDtypeStruct((A_WIDTH, D_MODEL), BF16)
    return pl.pallas_call(
        body, name=name, grid=(D_MODEL // tn,), in_specs=[whole, whole, cols, cols], out_specs=[out_spec, out_spec],
        out_shape=[out, out], compiler_params=_params(("parallel",)),
    )(oa, ob, dya, dyb)


def _mix_out_gate_bwd(name, dh, wo, gates, ya, yb):
    t = gates.shape[0]
    tm = _pick(t, (544, 384, 256, 128))

    def body(dh_ref, w_ref, g_ref, ya_ref, yb_ref, dya_ref, dyb_ref, dg_ref):
        dm = lax.dot_general(_bf(dh_ref[...]), w_ref[...], _NT, preferred_element_type=F32)
        sa = jax.nn.sigmoid(g_ref[:, 0:D_MODEL])
        sb = jax.nn.sigmoid(g_ref[:, D_MODEL:2 * D_MODEL])
        dya_ref[...] = (dm * sa).astype(BF16)
        dyb_ref[...] = (dm * sb).astype(BF16)
        dg_ref[:, 0:D_MODEL] = (dm * ya_ref[...].astype(F32) * (sa * (1.0 - sa))).astype(BF16)
        dg_ref[:, D_MODEL:2 * D_MODEL] = (dm * yb_ref[...].astype(F32) * (sb * (1.0 - sb))).astype(BF16)

    blk = pl.BlockSpec((tm, D_MODEL), lambda i: (i, 0))
    wide = pl.BlockSpec((tm, 2 * D_MODEL), lambda i: (i, 0))
    out = jax.ShapeDtypeStruct((t, D_MODEL), BF16)
    return pl.pallas_call(
        body, name=name, grid=(t // tm,),
        in_specs=[blk, pl.BlockSpec((D_MODEL, D_MODEL), lambda i: (0, 0)), wide, blk, blk], out_specs=[blk, blk, wide],
        out_shape=[out, out, jax.ShapeDtypeStruct((t, 2 * D_MODEL), BF16)], compiler_params=_params(("parallel",)),
    )(dh, wo, gates, ya, yb)


def _ffn_out_loss(name, a, w, res, gf, tgt, alpha):
    t, k = a.shape
    b, s, _ = tgt.shape
    l = t // b
    tm = _pick(t, (544, 384, 256, 128))
    per = l // tm
    assert per * tm == l and tm > PREFIX and l - PREFIX == s

    def body(a_ref, w_ref, r_ref, g_ref, t_ref, dh_ref, loss_ref, dg_ref, buf, sem):
        i = pl.program_id(0)
        bi, r = i // per, i % per

        def first_rows():
            return pltpu.make_async_copy(t_ref.at[bi, pl.ds(0, tm - PREFIX)], buf.at[pl.ds(PREFIX, tm - PREFIX)], sem)

        def later_rows():
            return pltpu.make_async_copy(t_ref.at[bi, pl.ds(pl.multiple_of(r * tm - PREFIX, 8), tm)], buf, sem)

        @pl.when(r == 0)
        def _():
            buf[0:PREFIX, :] = jnp.zeros((PREFIX, D_MODEL), F32)
            first_rows().start()

        if per > 1:
            @pl.when(r > 0)
            def _():
                later_rows().start()

        acc = lax.dot_general(a_ref[...], w_ref[...], _NN, preferred_element_type=F32)

        @pl.when(r == 0)
        def _():
            first_rows().wait()

        if per > 1:
            @pl.when(r > 0)
            def _():
                later_rows().wait()

        hv = acc * alpha + r_ref[...]
        row = lax.broadcasted_iota(jnp.int32, (tm, 1), 0)
        real = ((r > 0) | (row >= PREFIX)).astype(F32)
        g = g_ref[...]
        rn = lax.rsqrt(jnp.mean(hv * hv, axis=-1, keepdims=True) + EPS)
        xn = hv * rn
        err = (xn * g - buf[...]) * real
        lpart = 0.5 * jnp.sum(jnp.mean(err * err, axis=-1, keepdims=True), axis=0, keepdims=True)
        dy = err * (1.0 / D_MODEL)
        tv = dy * g
        dot = jnp.mean(tv * hv, axis=-1, keepdims=True)
        dh_ref[...] = rn * tv - hv * (rn * rn * rn * dot)
        gpart = jnp.sum(dy * xn, axis=0, keepdims=True)

        @pl.when(i == 0)
        def _():
            loss_ref[...] = jnp.zeros_like(loss_ref)
            dg_ref[...] = jnp.zeros_like(dg_ref)

        loss_ref[...] += jnp.broadcast_to(lpart, loss_ref.shape)
        dg_ref[...] += gpart

    row_spec = pl.BlockSpec((tm, D_MODEL), lambda i: (i, 0))
    vec = pl.BlockSpec((1, D_MODEL), lambda i: (0, 0))
    return pl.pallas_call(
        body, name=name, grid=(t // tm,),
        in_specs=[pl.BlockSpec((tm, k), lambda i: (i, 0)),
                  pl.BlockSpec((k, D_MODEL), lambda i: (0, 0), pipeline_mode=pl.Buffered(1)), row_spec, vec,
                  pl.BlockSpec(memory_space=pl.ANY)],
        out_specs=[row_spec, pl.BlockSpec((8, 128), lambda i: (0, 0)), vec],
        out_shape=[jax.ShapeDtypeStruct((t, D_MODEL), F32), jax.ShapeDtypeStruct((8, 128), F32),
                   jax.ShapeDtypeStruct((1, D_MODEL), F32)],
        scratch_shapes=[pltpu.VMEM((tm, D_MODEL), F32), pltpu.SemaphoreType.DMA(())],
        compiler_params=_params(("arbitrary",)),
    )(a, w, res, gf, tgt)


def _fgate_fwd(name, f3, bf_row):
    b, l, _ = f3.shape
    nb = l // BLOCK

    def body(f_ref, b_ref, cc_ref, cr_ref):
        r_i = lax.broadcasted_iota(jnp.int32, (BLOCK, BLOCK), 0)
        c_i = lax.broadcasted_iota(jnp.int32, (BLOCK, BLOCK), 1)
        tri = (r_i >= c_i).astype(F32)
        carry = jnp.zeros((1, 128), F32)
        for blk in range(nb):
            rows = slice(blk * BLOCK, (blk + 1) * BLOCK)
            z = f_ref[rows, :] + b_ref[...]
            lf = jnp.minimum(z, 0.0) - jnp.log(1.0 + jnp.exp(-jnp.abs(z)))
            cb = jnp.dot(tri, lf, preferred_element_type=F32, precision=lax.Precision.HIGHEST) + carry
            carry = cb[BLOCK - 1:BLOCK, :]
            cbt = cb.T
            for hh in range(B_HEADS):
                cc_ref[hh, rows, :] = jnp.sum(jnp.where(c_i == hh, cb, 0.0), axis=1, keepdims=True)
                cr_ref[hh, :, rows] = cbt[hh:hh + 1, :]

    return pl.pallas_call(
        body, name=name, grid=(b,),
        in_specs=[pl.BlockSpec((None, l, 128), lambda bi: (bi, 0, 0)),
                  pl.BlockSpec((1, 128), lambda bi: (0, 0))],
        out_specs=[pl.BlockSpec((None, B_HEADS, l, 1), lambda bi: (bi, 0, 0, 0)),
                   pl.BlockSpec((None, B_HEADS, 1, l), lambda bi: (bi, 0, 0, 0))],
        out_shape=[jax.ShapeDtypeStruct((b, B_HEADS, l, 1), F32), jax.ShapeDtypeStruct((b, B_HEADS, 1, l), F32)],
        compiler_params=_params(("parallel",)),
    )(f3, bf_row)


def _fgate_bwd(name, f3, bf_row, dcq, dck):
    b, l, _ = f3.shape
    nb = l // BLOCK

    def body(f_ref, b_ref, dcq_ref, dck_ref, df_ref, db_ref):
        r_i = lax.broadcasted_iota(jnp.int32, (BLOCK, BLOCK), 0)
        c_i = lax.broadcasted_iota(jnp.int32, (BLOCK, BLOCK), 1)
        tri = (r_i <= c_i).astype(F32)
        carry = jnp.zeros((1, 128), F32)
        total = jnp.zeros((1, 128), F32)
        for blk in range(nb - 1, -1, -1):
            rows = slice(blk * BLOCK, (blk + 1) * BLOCK)
            krows = jnp.concatenate([dck_ref[hh, :, rows] for hh in range(B_HEADS)]
                                    + [jnp.zeros((BLOCK - B_HEADS, BLOCK), F32)], axis=0)
            dcb = krows.T
            for hh in range(B_HEADS):
                dcb = dcb + jnp.where(c_i == hh, dcq_ref[hh, rows, :], 0.0)
            rc = jnp.dot(tri, dcb, preferred_element_type=F32, precision=lax.Precision.HIGHEST) + carry
            carry = rc[0:1, :]
            z = f_ref[rows, :] + b_ref[...]
            df = rc * (1.0 / (1.0 + jnp.exp(z)))
            df_ref[rows, :] = df.astype(BF16)
            total = total + jnp.sum(df, axis=0, keepdims=True)

        @pl.when(pl.program_id(0) == 0)
        def _():
            db_ref[...] = total

        @pl.when(pl.program_id(0) > 0)
        def _():
            db_ref[...] += total

    return pl.pallas_call(
        body, name=name, grid=(b,),
        in_specs=[pl.BlockSpec((None, l, 128), lambda bi: (bi, 0, 0)),
                  pl.BlockSpec((1, 128), lambda bi: (0, 0)),
                  pl.BlockSpec((None, B_HEADS, l, 1), lambda bi: (bi, 0, 0, 0)),
                  pl.BlockSpec((None, B_HEADS, 1, l), lambda bi: (bi, 0, 0, 0))],
        out_specs=[pl.BlockSpec((None, l, 128), lambda bi: (bi, 0, 0)), pl.BlockSpec((1, 128), lambda bi: (0, 0))],
        out_shape=[jax.ShapeDtypeStruct((b, l, 128), BF16), jax.ShapeDtypeStruct((1, 128), F32)],
        compiler_params=_params(("arbitrary",)),
    )(f3, bf_row, dcq, dck)


A_Q_BLK = B_SEG // A_WIDTH
A_K_BLK = (B_SEG + A_WIDTH) // 128
A_V_BLK = A_K_BLK + 1
A_SEG_BLK = B_SEG // A_SEG
STACK = A_HEADS * BLOCK


def _lane_lo():
    return lax.broadcasted_iota(jnp.int32, (1, 128), 1) < HEAD_DIM


def _stack_heads(x, masked):
    lo = _lane_lo()
    blks = [x[:, 128 * j:128 * (j + 1)] for j in range(4)]
    if not masked:
        return jnp.concatenate(blks + blks, axis=0)
    zero = jnp.zeros_like(blks[0])
    return jnp.concatenate([jnp.where(lo, bk, zero) for bk in blks] + [jnp.where(lo, zero, bk) for bk in blks], axis=0)


def _unstack_heads(y):
    lo = _lane_lo()
    return jnp.concatenate([jnp.where(lo, y[128 * j:128 * (j + 1)], y[128 * (4 + j):128 * (5 + j)]) for j in range(4)], axis=1)


def _swa_bias(slopes):
    r_i = jnp.arange(STACK)[:, None]
    c_i = jnp.arange(3 * BLOCK)[None, :]
    seg = c_i >> 7
    out = []
    for n in range(3):
        qpos = n * BLOCK + (r_i & (BLOCK - 1))
        kpos = jnp.where(seg == 0, c_i, (n - 2) * BLOCK + c_i)
        dist = qpos - kpos
        band = (seg != 0) & (dist < BLOCK) & (kpos >= PREFIX)
        meta = (seg == 0) & (c_i >= N_PAD)
        out.append(jnp.where((dist >= 0) & (band | meta), -slopes * dist.astype(F32), NEG))
    return jnp.stack(out, axis=0)


def _swa_scores(q, kcat, n, slope, bias):
    s = lax.dot_general(q, kcat, _NT, preferred_element_type=F32) + bias
    further = slope * (-BLOCK * jnp.maximum(n - 2, 0)).astype(F32)
    return jnp.concatenate([s[:, 0:BLOCK] + further, s[:, BLOCK:]], axis=1)


def _swa_specs():
    def kv(col_blk):
        return [pl.BlockSpec((None, BLOCK, 128), lambda b, n: (b, 0, col_blk)),
                pl.BlockSpec((None, BLOCK, 128), lambda b, n: (b, jnp.maximum(n - 1, 0), col_blk)),
                pl.BlockSpec((None, BLOCK, 128), lambda b, n: (b, n, col_blk))]

    q_spec = pl.BlockSpec((None, BLOCK, A_WIDTH), lambda b, n: (b, n, A_Q_BLK))
    o_spec = pl.BlockSpec((None, BLOCK, A_WIDTH), lambda b, n: (b, n, 0))
    col = pl.BlockSpec((STACK, 1), lambda b, n: (0, 0))
    bias = pl.BlockSpec((None, STACK, 3 * BLOCK), lambda b, n: (jnp.minimum(n, 2), 0, 0))
    lse_spec = pl.BlockSpec((None, A_HEADS, BLOCK, 1), lambda b, n: (b, 0, n, 0))
    return q_spec, kv(A_K_BLK), kv(A_V_BLK), o_spec, [col, col, bias], lse_spec


def _swa_fwd(name, qkv, slopes, sinks, bias):
    b, l, _ = qkv.shape
    nb = l // BLOCK

    def body(q_ref, k0_ref, kp_ref, kc_ref, v0_ref, vp_ref, vc_ref, sl_ref, sk_ref, bias_ref, o_ref, lse_ref):
        n = pl.program_id(1)
        qs = _stack_heads(q_ref[...], True) * SCALE
        kcat = jnp.concatenate([k0_ref[...], kp_ref[...], kc_ref[...]], axis=0)
        vcat = jnp.concatenate([v0_ref[...], vp_ref[...], vc_ref[...]], axis=0)
        s = _swa_scores(qs, kcat, n, sl_ref[...], bias_ref[...])
        sink = sk_ref[...]
        m = jnp.maximum(jnp.max(s, axis=-1, keepdims=True), sink)
        p = jnp.exp(s - m)
        den = jnp.sum(p, axis=-1, keepdims=True) + jnp.exp(sink - m)
        o = lax.dot_general(p.astype(BF16), vcat, _NN, preferred_element_type=F32) / den
        o_ref[...] = _unstack_heads(o)
        lse_ref[...] = (m + jnp.log(den)).reshape(A_HEADS, BLOCK, 1)

    q_spec, k_specs, v_specs, o_spec, consts, lse_spec = _swa_specs()
    return pl.pallas_call(
        body, name=name, grid=(b, nb),
        in_specs=[q_spec] + k_specs + v_specs + consts, out_specs=[o_spec, lse_spec],
        out_shape=[jax.ShapeDtypeStruct((b, l, A_WIDTH), F32), jax.ShapeDtypeStruct((b, A_HEADS, l, 1), F32)],
        compiler_params=_params(("parallel", "parallel")),
    )(qkv, qkv, qkv, qkv, qkv, qkv, qkv, slopes, sinks, bias)


def _swa_bwd(name, qkv, o, lse, do, slopes, sinks, bias, dqkv):
    b, l, _ = qkv.shape
    nb = l // BLOCK

    def body(q_ref, k0_ref, kp_ref, kc_ref, v0_ref, vp_ref, vc_ref, o_ref, lse_ref, do_ref, sl_ref, sk_ref, bias_ref, _,
             dx_ref, ds_ref, dk_acc, dv_acc):
        bi = pl.program_id(0)
        n = pl.program_id(1)
        qs = _stack_heads(q_ref[...], True) * SCALE
        dos32 = _stack_heads(do_ref[...], True)
        dos = dos32.astype(BF16)
        os_ = _stack_heads(o_ref[...], False)
        lsev = lse_ref[...].reshape(STACK, 1)
        kcat = jnp.concatenate([k0_ref[...], kp_ref[...], kc_ref[...]], axis=0)
        vcat = jnp.concatenate([v0_ref[...], vp_ref[...], vc_ref[...]], axis=0)
        s = _swa_scores(qs, kcat, n, sl_ref[...], bias_ref[...])
        p = jnp.exp(s - lsev)
        dsum = jnp.sum(dos32 * os_, axis=-1, keepdims=True)
        dp = lax.dot_general(dos, vcat, _NT, preferred_element_type=F32)
        dsc = (p * (dp - dsum)).astype(BF16)
        dq = lax.dot_general(dsc, kcat, _NN, preferred_element_type=F32) * SCALE
        row0 = pl.multiple_of(n * BLOCK, BLOCK)
        dx_ref[pl.ds(row0, BLOCK), 0:A_WIDTH] = _unstack_heads(dq).astype(BF16)
        dkc = lax.dot_general(dsc, qs, _TN, preferred_element_type=F32)
        dvc = lax.dot_general(p.astype(BF16), dos, _TN, preferred_element_type=F32)

        @pl.when(n == 0)
        def _():
            dk_acc[...] = jnp.zeros_like(dk_acc)
            dv_acc[...] = jnp.zeros_like(dv_acc)

        starts = (0, pl.multiple_of(jnp.maximum(n - 1, 0) * BLOCK, BLOCK), row0)
        for t, st in enumerate(starts):
            dk_acc[pl.ds(st, BLOCK), :] += dkc[t * BLOCK:(t + 1) * BLOCK, :]
            dv_acc[pl.ds(st, BLOCK), :] += dvc[t * BLOCK:(t + 1) * BLOCK, :]

        @pl.when(n == nb - 1)
        def _():
            dx_ref[:, A_WIDTH:A_WIDTH + 128] = dk_acc[...].astype(BF16)
            dx_ref[:, A_WIDTH + 128:A_SEG] = dv_acc[...].astype(BF16)

        dsink = -(jnp.exp(sk_ref[...] - lsev) * dsum)
        r8 = lax.broadcasted_iota(jnp.int32, (8, 128), 0)
        acc = jnp.zeros((8, 128), F32)
        for hh in range(A_HEADS):
            acc = acc + jnp.where(r8 == hh, jnp.sum(dsink[hh * BLOCK:(hh + 1) * BLOCK, :]), 0.0)

        @pl.when((bi == 0) & (n == 0))
        def _():
            ds_ref[...] = jnp.zeros_like(ds_ref)

        ds_ref[...] += acc

    q_spec, k_specs, v_specs, o_spec, consts, lse_spec = _swa_specs()
    return pl.pallas_call(
        body, name=name, grid=(b, nb),
        in_specs=[q_spec] + k_specs + v_specs + [o_spec, lse_spec, o_spec] + consts + [pl.BlockSpec(memory_space=pl.ANY)],
        out_specs=[pl.BlockSpec((None, l, A_SEG), lambda bb, n: (bb, 0, A_SEG_BLK)),
                   pl.BlockSpec((8, 128), lambda bb, n: (0, 0))],
        out_shape=[jax.ShapeDtypeStruct(dqkv.shape, BF16), jax.ShapeDtypeStruct((8, 128), F32)],
        scratch_shapes=[pltpu.VMEM((l, 128), F32), pltpu.VMEM((l, 128), F32)],
        input_output_aliases={13: 0},
        compiler_params=_params(("arbitrary", "arbitrary")),
    )(qkv, qkv, qkv, qkv, qkv, qkv, qkv, o, lse, do, slopes, sinks, bias, dqkv)


def _fox_mask(qk, ck, i):
    kh = qk.shape[1]
    qpos = i * BLOCK + lax.broadcasted_iota(jnp.int32, (BLOCK, kh), 0)
    kpos = lax.broadcasted_iota(jnp.int32, (BLOCK, kh), 1)
    return jnp.where((kpos <= qpos) & (kpos >= N_PAD), qk - ck, NEG)


def _pick_head(x, hh):
    lo = _lane_lo()
    return jnp.where(lo if hh == 0 else jnp.logical_not(lo), x, jnp.zeros_like(x))


def _both_heads(x):
    return jnp.concatenate([_pick_head(x, 0), _pick_head(x, 1)], axis=0)


def _fox_specs(l):
    pair = pl.BlockSpec((None, l, PAIR_W), lambda bi, hp: (bi, 0, hp))
    half = pl.BlockSpec((None, l, 128), lambda bi, hp: (bi, 0, hp))
    colv = pl.BlockSpec((None, 2, l, 1), lambda bi, hp: (bi, hp, 0, 0))
    rowv = pl.BlockSpec((None, 2, 1, l), lambda bi, hp: (bi, hp, 0, 0))
    return pair, half, colv, rowv


def _fox_fwd(name, qkv, c_col, c_row):
    b, l, _ = qkv.shape
    nb = l // BLOCK

    def body(x_ref, cc_ref, cr_ref, o_ref, lse_ref):
        for i in range(nb):
            rows = slice(i * BLOCK, (i + 1) * BLOCK)
            kh = (i + 1) * BLOCK
            qblk = x_ref[rows, 0:128]
            kv = x_ref[0:kh, 128:256]
            vv = x_ref[0:kh, 256:384]
            qk = lax.dot_general(_both_heads(qblk) * SCALE, kv, _NT, preferred_element_type=F32)
            ps, dens = [], []
            for hh in range(2):
                s = _fox_mask(qk[hh * BLOCK:(hh + 1) * BLOCK], cr_ref[hh, :, 0:kh], i)
                m = jnp.max(s, axis=-1, keepdims=True)
                p = jnp.exp(s - m)
                den = jnp.sum(p, axis=-1, keepdims=True)
                ps.append(p.astype(BF16))
                dens.append(den)
                lse_ref[hh, rows, :] = (m + jnp.log(den)) + cc_ref[hh, rows, :]
            pv = lax.dot_general(jnp.concatenate(ps, axis=0), vv, _NN, preferred_element_type=F32)
            o_ref[rows, :] = jnp.where(_lane_lo(), pv[0:BLOCK] / dens[0], pv[BLOCK:2 * BLOCK] / dens[1]).astype(BF16)

    pair, half, colv, rowv = _fox_specs(l)
    return pl.pallas_call(
        body, name=name, grid=(b, 4), in_specs=[pair, colv, rowv], out_specs=[half, colv],
        out_shape=[jax.ShapeDtypeStruct((b, l, B_WIDTH), BF16), jax.ShapeDtypeStruct((b, B_HEADS, l, 1), F32)],
        compiler_params=_params(("parallel", "parallel")),
    )(qkv, c_col, c_row)


def _fox_bwd(name, qkv, c_col, c_row, o, lse, do):
    b, l, _ = qkv.shape
    nb = l // BLOCK

    def body(x_ref, cc_ref, cr_ref, o_ref, lse_ref, do_ref, dx_ref, dcq_ref, dck_ref, dk_acc, dv_acc):
        dk_acc[...] = jnp.zeros_like(dk_acc)
        dv_acc[...] = jnp.zeros_like(dv_acc)
        dck_ref[...] = jnp.zeros_like(dck_ref)
        for i in range(nb):
            rows = slice(i * BLOCK, (i + 1) * BLOCK)
            kh = (i + 1) * BLOCK
            qblk = x_ref[rows, 0:128]
            kv = x_ref[0:kh, 128:256]
            vv = x_ref[0:kh, 256:384]
            doblk = do_ref[rows, :]
            ov = o_ref[rows, :].astype(F32)
            q2 = _both_heads(qblk) * SCALE
            do2 = _both_heads(doblk)
            qk = lax.dot_general(q2, kv, _NT, preferred_element_type=F32)
            dp = lax.dot_general(do2, vv, _NT, preferred_element_type=F32)
            ps, dss = [], []
            for hh in range(2):
                half = slice(hh * BLOCK, (hh + 1) * BLOCK)
                s = _fox_mask(qk[half], cr_ref[hh, :, 0:kh], i)
                p = jnp.exp(s - (lse_ref[hh, rows, :] - cc_ref[hh, rows, :]))
                dsum = jnp.sum(do2[half].astype(F32) * ov, axis=-1, keepdims=True)
                ds = p * (dp[half] - dsum)
                ps.append(p.astype(BF16))
                dss.append(ds.astype(BF16))
                dcq_ref[hh, rows, :] = jnp.sum(ds, axis=-1, keepdims=True)
                dck_ref[hh, :, 0:kh] -= jnp.sum(ds, axis=0, keepdims=True)
            p2 = jnp.concatenate(ps, axis=0)
            ds2 = jnp.concatenate(dss, axis=0)
            dq = lax.dot_general(ds2, kv, _NN, preferred_element_type=F32) * SCALE
            dk_acc[0:kh, :] += lax.dot_general(ds2, q2, _TN, preferred_element_type=F32)
            dv_acc[0:kh, :] += lax.dot_general(p2, do2, _TN, preferred_element_type=F32)
            dx_ref[rows, 0:128] = jnp.where(_lane_lo(), dq[0:BLOCK], dq[BLOCK:2 * BLOCK]).astype(BF16)
        dx_ref[:, 128:256] = dk_acc[...].astype(BF16)
        dx_ref[:, 256:384] = dv_acc[...].astype(BF16)

    pair, half, colv, rowv = _fox_specs(l)
    return pl.pallas_call(
        body, name=name, grid=(b, 4), in_specs=[pair, colv, rowv, half, colv, half],
        out_specs=[pair, colv, rowv],
        out_shape=[jax.ShapeDtypeStruct(qkv.shape, BF16), jax.ShapeDtypeStruct((b, B_HEADS, l, 1), F32),
                   jax.ShapeDtypeStruct((b, B_HEADS, 1, l), F32)],
        scratch_shapes=[pltpu.VMEM((l, 128), F32), pltpu.VMEM((l, 128), F32)],
        compiler_params=_params(("parallel", "parallel")),
    )(qkv, c_col, c_row, o, lse, do)


_FLIPS = ((0, 0, 1), (0, 1, 0), (0, 1, 1), (1, 0, 0), (1, 0, 1), (1, 1, 0), (1, 1, 1))


def _peer_table():
    x, y, c = lax.axis_index("x"), lax.axis_index("y"), lax.axis_index("c")
    me = 4 * x + 2 * y + c
    peers = []
    for fx, fy, fc in _FLIPS:
        px = 1 - x if fx else x
        py = 1 - y if fy else y
        pc = 1 - c if fc else c
        peers.append(((px, py, pc), 4 * px + 2 * py + pc))
    return me, peers


_HBM = pl.BlockSpec(memory_space=pltpu.HBM)
_SEM = pl.BlockSpec(memory_space=pltpu.SEMAPHORE)
_ANY = pl.BlockSpec(memory_space=pl.ANY)
_EFFECT = pltpu.SideEffectType.DATAFLOW_SIDE_EFFECTING


def _split_copy(srcs_are_pieces, src_refs, land_refs, send_sem, recv_sem, a, kk, me, peers, arriving):
    dev, lin = peers[kk]
    npeer = len(_FLIPS)
    src = src_refs[a] if srcs_are_pieces[a] else src_refs[a].at[lin]
    dst = land_refs[a].at[lin] if arriving else land_refs[a].at[me]
    return pltpu.make_async_remote_copy(src_ref=src, dst_ref=dst, send_sem=send_sem.at[a * npeer + kk],
                                        recv_sem=recv_sem.at[a * npeer + kk], device_id=dev, device_id_type=MESH_ID)


def _xchg_start(name, gather, scatter, after=None):
    me_out = 4 * lax.axis_index("x") + 2 * lax.axis_index("y") + lax.axis_index("c")
    srcs = list(gather) + list(scatter)
    is_piece = [True] * len(gather) + [False] * len(scatter)
    lands = []
    for a, piece in zip(srcs, is_piece):
        own = a[None] if piece else lax.dynamic_slice_in_dim(a, me_out, 1, axis=0)
        shape = ((N_DEV,) + tuple(a.shape)) if piece else tuple(a.shape)
        start = (me_out,) + (0,) * (len(shape) - 1)
        lands.append(lax.dynamic_update_slice(lax.empty(shape, a.dtype), own, start))
    n = len(srcs)
    nsem = n * len(_FLIPS)
    has_after = after is not None

    def body(*refs):
        src_refs = refs[:n]
        land_refs = refs[n:2 * n]
        outs = refs[2 * n + (1 if has_after else 0):]
        send_sem, recv_sem = outs[0], outs[1]
        token = outs[-1]
        me, peers = _peer_table()
        for kk in range(len(_FLIPS)):
            for a in range(n):
                _split_copy(is_piece, src_refs, land_refs, send_sem, recv_sem, a, kk, me, peers, False).start()
        token[...] = jnp.zeros_like(token)

    out_shape = ([pltpu.SemaphoreType.DMA((nsem,)), pltpu.SemaphoreType.DMA((nsem,))]
                 + [pltpu.HBM(tuple(a.shape), a.dtype) for a in srcs] + [pltpu.HBM(tuple(a.shape), a.dtype) for a in lands]
                 + [jax.ShapeDtypeStruct((8, 128), F32)])
    args = [pltpu.with_memory_space_constraint(a, pltpu.HBM) for a in srcs + lands] + ([after] if has_after else [])
    res = pl.pallas_call(
        body, name=name, out_shape=out_shape,
        in_specs=[_HBM] * (2 * n) + ([_ANY] if has_after else []),
        out_specs=[_SEM, _SEM] + [_HBM] * (2 * n) + [pl.BlockSpec(memory_space=pltpu.VMEM)],
        input_output_aliases={i: 2 + i for i in range(2 * n)},
        compiler_params=pltpu.CompilerParams(has_side_effects=_EFFECT),
    )(*args)
    state = (res[0], res[1], list(res[2:2 + n]), list(res[2 + n:2 + 2 * n]), is_piece)
    return state, res[-1]


def _xchg_wait(name, state, after):
    send_sem, recv_sem, srcs, lands, is_piece = state
    n = len(srcs)

    def body(*refs):
        src_refs = refs[:n]
        land_refs = refs[n:2 * n]
        s_sem, r_sem = refs[2 * n], refs[2 * n + 1]
        me, peers = _peer_table()
        for kk in range(len(_FLIPS)):
            for a in range(n):
                cp = _split_copy(is_piece, src_refs, land_refs, s_sem, r_sem, a, kk, me, peers, True)
                cp.wait_send()
                cp.wait_recv()

    out_shape = [pltpu.HBM(tuple(a.shape), a.dtype) for a in srcs] + [pltpu.HBM(tuple(a.shape), a.dtype) for a in lands]
    res = pl.pallas_call(
        body, name=name, out_shape=out_shape,
        in_specs=[_HBM] * (2 * n) + [_SEM, _SEM, _ANY], out_specs=[_HBM] * (2 * n),
        input_output_aliases={i: i for i in range(2 * n)},
        compiler_params=pltpu.CompilerParams(has_side_effects=_EFFECT),
    )(*srcs, *lands, send_sem, recv_sem, after)
    return list(res[n:])


_SIB = (0, 0, 1)
_ICI = ((0, 1, 0), (1, 0, 0), (1, 1, 0))


def _flip(fl):
    x, y, c = lax.axis_index("x"), lax.axis_index("y"), lax.axis_index("c")
    px = 1 - x if fl[0] else x
    py = 1 - y if fl[1] else y
    pc = 1 - c if fl[2] else c
    return (px, py, pc), 4 * px + 2 * py + pc


def _gather2_start(name, pieces, after=None):
    me_out = 4 * lax.axis_index("x") + 2 * lax.axis_index("y") + lax.axis_index("c")
    pieces = list(pieces)
    n = len(pieces)
    lands = [lax.dynamic_update_slice(lax.empty((N_DEV,) + tuple(a.shape), a.dtype), a[None],
                                      (me_out,) + (0,) * a.ndim) for a in pieces]
    first = (_SIB,) + _ICI
    has_after = after is not None

    def body(*refs):
        src_refs, land_refs = refs[:n], refs[n:2 * n]
        outs = refs[2 * n + (1 if has_after else 0):]
        send_sem, recv_sem, token = outs[0], outs[1], outs[-1]
        _, me = _flip((0, 0, 0))
        for kk, fl in enumerate(first):
            dev, _ = _flip(fl)
            for a in range(n):
                pltpu.make_async_remote_copy(src_ref=src_refs[a], dst_ref=land_refs[a].at[me],
                                             send_sem=send_sem.at[a * 4 + kk], recv_sem=recv_sem.at[a * 4 + kk],
                                             device_id=dev, device_id_type=MESH_ID).start()
        token[...] = jnp.zeros_like(token)

    hbm = [pltpu.HBM(tuple(a.shape), a.dtype) for a in pieces + lands]
    res = pl.pallas_call(
        body, name=name,
        out_shape=[pltpu.SemaphoreType.DMA((4 * n,)), pltpu.SemaphoreType.DMA((4 * n,))] + hbm
        + [jax.ShapeDtypeStruct((8, 128), F32)],
        in_specs=[_HBM] * (2 * n) + ([_ANY] if has_after else []),
        out_specs=[_SEM, _SEM] + [_HBM] * (2 * n) + [pl.BlockSpec(memory_space=pltpu.VMEM)],
        input_output_aliases={i: 2 + i for i in range(2 * n)},
        compiler_params=pltpu.CompilerParams(has_side_effects=_EFFECT),
    )(*([pltpu.with_memory_space_constraint(a, pltpu.HBM) for a in pieces + lands] + ([after] if has_after else [])))
    return (res[0], res[1], list(res[2:2 + n]), list(res[2 + n:2 + 2 * n])), res[-1]


def _gather2_forward(name, state, after):
    send_a, recv_a, pieces, lands = state
    n = len(pieces)
    first = (_SIB,) + _ICI

    def body(*refs):
        src_refs, land_refs = refs[:n], refs[n:2 * n]
        s_a, r_a = refs[2 * n], refs[2 * n + 1]
        outs = refs[2 * n + 3:]
        send_b, recv_b, token = outs[0], outs[1], outs[-1]
        for kk, fl in enumerate(first):
            dev, lin = _flip(fl)
            for a in range(n):
                cp = pltpu.make_async_remote_copy(src_ref=src_refs[a], dst_ref=land_refs[a].at[lin],
                                                  send_sem=s_a.at[a * 4 + kk], recv_sem=r_a.at[a * 4 + kk],
                                                  device_id=dev, device_id_type=MESH_ID)
                cp.wait_send()
                cp.wait_recv()
        sib, _ = _flip(_SIB)
        for j, fl in enumerate(_ICI):
            _, lin = _flip(fl)
            for a in range(n):
                pltpu.make_async_remote_copy(src_ref=land_refs[a].at[lin], dst_ref=land_refs[a].at[lin],
                                             send_sem=send_b.at[a * 3 + j], recv_sem=recv_b.at[a * 3 + j],
                                             device_id=sib, device_id_type=MESH_ID).start()
        token[...] = jnp.zeros_like(token)

    hbm = [pltpu.HBM(tuple(a.shape), a.dtype) for a in pieces + lands]
    res = pl.pallas_call(
        body, name=name,
        out_shape=[pltpu.SemaphoreType.DMA((3 * n,)), pltpu.SemaphoreType.DMA((3 * n,))] + hbm
        + [jax.ShapeDtypeStruct((8, 128), F32)],
        in_specs=[_HBM] * (2 * n) + [_SEM, _SEM, _ANY],
        out_specs=[_SEM, _SEM] + [_HBM] * (2 * n) + [pl.BlockSpec(memory_space=pltpu.VMEM)],
        input_output_aliases={i: 2 + i for i in range(2 * n)},
        compiler_params=pltpu.CompilerParams(has_side_effects=_EFFECT),
    )(*pieces, *lands, send_a, recv_a, after)
    return (res[0], res[1], list(res[2 + n:2 + 2 * n])), res[-1]


def _gather2_wait(name, state, after):
    send_b, recv_b, lands = state
    n = len(lands)

    def body(*refs):
        land_refs = refs[:n]
        s_b, r_b = refs[n], refs[n + 1]
        sib, _ = _flip(_SIB)
        for j, fl in enumerate(_ICI):
            _, sent = _flip(fl)
            _, arriving = _flip((fl[0], fl[1], 1))
            for a in range(n):
                cp = pltpu.make_async_remote_copy(src_ref=land_refs[a].at[sent], dst_ref=land_refs[a].at[arriving],
                                                  send_sem=s_b.at[a * 3 + j], recv_sem=r_b.at[a * 3 + j],
                                                  device_id=sib, device_id_type=MESH_ID)
                cp.wait_send()
                cp.wait_recv()

    res = pl.pallas_call(
        body, name=name, out_shape=[pltpu.HBM(tuple(a.shape), a.dtype) for a in lands],
        in_specs=[_HBM] * n + [_SEM, _SEM, _ANY], out_specs=[_HBM] * n,
        input_output_aliases={i: i for i in range(n)},
        compiler_params=pltpu.CompilerParams(has_side_effects=_EFFECT),
    )(*lands, send_b, recv_b, after)
    return list(res)


def _adam_math(w, g, m, v):
    m = ADAM_B1 * m + (1.0 - ADAM_B1) * g
    v = ADAM_B2 * v + (1.0 - ADAM_B2) * (g * g)
    m_hat = m / (1.0 - ADAM_B1 ** ADAM_STEP)
    v_hat = v / (1.0 - ADAM_B2 ** ADAM_STEP)
    delta = -ADAM_LR * (m_hat / (jnp.sqrt(v_hat) + ADAM_EPS) + ADAM_WD * w)
    return delta, m, v


def _adam(name, w, m, v, parts, transposed=False, dep=None):
    npart, _, cp = parts.shape
    has_dep = dep is not None
    if transposed:
        c, r = w.shape
        tr = _pick(r, (256, 128))
        blk = pl.BlockSpec((c, tr), lambda i: (0, i))
    else:
        r, c = w.shape
        tr = _pick(r, (256, 176, 128, 64, 16, 8, 1))
        blk = pl.BlockSpec((tr, c), lambda i: (i, 0))

    def body(w_ref, m_ref, v_ref, p_ref, *rest):
        g_ref, d_ref, mo_ref, vo_ref = rest[-4:]
        g = p_ref[0].astype(F32)
        for pp in range(1, npart):
            g = g + p_ref[pp].astype(F32)
        g = g.T[0:c, :] if transposed else g[:, 0:c]
        delta, mn, vn = _adam_math(w_ref[...], g, m_ref[...], v_ref[...])
        g_ref[...] = g
        d_ref[...] = delta
        mo_ref[...] = mn
        vo_ref[...] = vn

    out = jax.ShapeDtypeStruct(w.shape, F32)
    return pl.pallas_call(
        body, name=name, grid=(r // tr,),
        in_specs=[blk, blk, blk, pl.BlockSpec((npart, tr, cp), lambda i: (0, i, 0))]
        + ([pl.BlockSpec(memory_space=pl.ANY)] if has_dep else []),
        out_specs=[blk, blk, blk, blk], out_shape=[out, out, out, out], compiler_params=_params(("parallel",)),
    )(*((w, m, v, parts) + ((dep,) if has_dep else ())))


def _small_sum(name, packs):
    def body(p_ref, o_ref):
        tot = p_ref[0]
        for pp in range(1, N_DEV):
            tot = tot + p_ref[pp]
        o_ref[0:8, :] = tot[0:8, :]
        o_ref[8:24, :] = tot[8:24, :] + tot[24:40, :]

    return pl.pallas_call(body, name=name, out_shape=jax.ShapeDtypeStruct((24, D_MODEL), F32),
                          compiler_params=_params())(packs)


def _local_step(x, tgt, g1, gm, g2, gf, b_forget, sinks, weights, send):
    b, s, _ = x.shape
    l = s + PREFIX
    t = b * l
    (meta,) = weights("meta", x)

    h0, n1 = _embed_norm("embed_rms1_fwd", x, meta, g1)
    (w1i,) = weights("ffn1_in", n1)
    gu1, a1 = _ffn_in_fwd("ffn1_in_fwd", n1, w1i)
    (w1o,) = weights("ffn1_out", weights("ffn1_out:forward", a1))
    h1, um = _mm_res_norm("ffn1_out_fwd", a1, w1o, h0, gm, alpha=0.5)
    wi, wa, wb, wo = weights("mix", weights("mix:forward", um))
    qkv, gates, f2 = _proj_fwd("proj_fwd", um, wi)
    qkv3 = qkv.reshape(b, l, QKV_W)
    f3 = f2.reshape(b, l, 128)
    bf_row = jnp.pad(b_forget, ((0, 0), (0, 128 - B_HEADS)))
    c_col, c_row = _fgate_fwd("fgate_fwd", f3, bf_row)
    head_of_row = jnp.arange(STACK) // BLOCK
    slopes = jnp.exp2(-8.0 * (head_of_row + 1).astype(F32) / A_HEADS).reshape(STACK, 1)
    sink_rows = jnp.repeat(sinks.reshape(A_HEADS), BLOCK).reshape(STACK, 1)
    swa_bias = _swa_bias(slopes)
    oa3, lse_a = _swa_fwd("swa_fwd", qkv3, slopes, sink_rows, swa_bias)
    ob3, lse_b = _fox_fwd("fox_fwd", qkv3, c_col, c_row)
    oa = oa3.reshape(t, A_WIDTH)
    ob = ob3.reshape(t, B_WIDTH)
    mixed, ya, yb = _branch_gate_fwd("branch_gate_fwd", oa, ob, wa, wb, gates, dep=weights("ffn2:forward", ob))
    h2, n2 = _mm_res_norm("mix_out_fwd", mixed, wo, h1, g2)
    w2i, w2o = weights("ffn2", h2)
    gu2, a2 = _ffn_in_fwd("ffn2_in_fwd", n2, w2i)

    dh3, loss_blk, dgf = _ffn_out_loss("ffn2_out_loss", a2, w2o, h2, gf, tgt, 0.5)

    def ffn_bwd(tag, dh, h_in, g_norm, n_in, gu, a, w_in_blk, w_out, one_send, examples=None):
        dw_out = _mm_tn(tag + "_out_bwd_w", a, dh, alpha=0.5)
        dgu = _ffn_out_bwd_x(tag + "_out_bwd_x", dh, w_out, gu, dep=None if one_send else send(tag + "_out", (dw_out,)))
        dw_in = _ffn_in_bwd_w(tag + "_in_bwd_w", n_in, dgu)
        token = send(tag, (dw_in, dw_out)) if one_send else send(tag + "_in", (dw_in,))
        return _ffn_in_bwd_x(tag + "_in_bwd_x", dgu, w_in_blk, h_in, g_norm, dh, dep=token, examples=examples)

    dh2, dg2 = ffn_bwd("ffn2", dh3, h2, g2, n2, gu2, a2, w2i, w2o, True)

    dwo = _mm_tn("mix_out_bwd_w", mixed, dh2)
    dya, dyb, dgates = _mix_out_gate_bwd("mix_out_gate_bwd", dh2, wo, gates, ya, yb)
    doa, dob = _branch_bwd_x("branch_bwd_x", dya, dyb, wa, wb)
    dwa, dwb = _branch_bwd_w("branch_bwd_w", oa, ob, dya, dyb)
    dqkv3, dcq, dck = _fox_bwd("fox_bwd", qkv3, c_col, c_row, ob3, lse_b, dob.reshape(b, l, B_WIDTH))
    dqkv3, dsink = _swa_bwd("swa_bwd", qkv3, oa3, lse_a, doa.reshape(b, l, A_WIDTH), slopes, sink_rows, swa_bias, dqkv3)
    dqkv = dqkv3.reshape(t, QKV_W)
    df3, dbf = _fgate_bwd("fgate_bwd", f3, bf_row, dcq, dck)
    df = df3.reshape(t, 128)
    dwi_qkv = _mm_tn("proj_qkv_bwd_w", um, dqkv, tm_c=(512,), tn_c=(768,))
    dwi_g = _mm_tn("proj_gates_bwd_w", um, dgates, tm_c=(512,), tn_c=(512,))
    dwi_f = _mm_tn("proj_f_bwd_w", um, df, tm_c=(512,), tn_c=(128,))
    token = send("mix", (dwi_qkv, dwi_g, dwi_f, dwa, dwb, dwo))
    dh1, dgm = _proj_bwd_x("proj_bwd_x", dqkv, dgates, df, wi, h1, gm, dh2, dep=token)

    grad_x, dmeta3, dg1 = ffn_bwd("ffn1", dh1, h0, g1, n1, gu1, a1, w1i, w1o, False, examples=b)
    dmeta = dmeta3.reshape(b * N_META, D_MODEL)

    misc = jnp.concatenate([dbf[:, 0:B_HEADS], dsink[:, 0].reshape(1, A_HEADS), loss_blk[0:1, 0:1]], axis=1)
    misc = jnp.pad(misc, ((0, 0), (0, D_MODEL - misc.shape[1])))
    row = lax.broadcasted_iota(jnp.int32, (8, D_MODEL), 0)
    vec = jnp.zeros((8, D_MODEL), F32)
    for i, piece in enumerate((dg1, dgm, dg2, dgf, misc)):
        vec = jnp.where(row == i, piece, vec)
    small = jnp.concatenate([vec, dmeta], axis=0)
    return grad_x, small


def _pad_to(a, rows, cols):
    return jnp.pad(a, ((0, rows - a.shape[0]), (0, cols - a.shape[1])))


def _ffn_out_from_gathered(g):
    w = g.reshape(4, FF_SHARD, D_MODEL)
    return jnp.pad(w, ((0, 0), (0, FF_SHARD_P - FF_SHARD), (0, 0))).reshape(D_FF_P, D_MODEL)


def _ffn_out_to_scatter(dw):
    return dw.reshape(4, FF_SHARD_P, D_MODEL)[:, 0:FF_SHARD, :].reshape(N_DEV, FFO_SHARD, D_MODEL)


def _proj_segments():
    segs = [(HEAD_DIM * h, HEAD_DIM, 0, B_SEG + HEAD_DIM * A_HEAD_ORDER.index(h)) for h in range(A_HEADS)]
    segs += [(512, 128, 0, B_SEG + A_WIDTH), (640, 128, 0, B_SEG + A_WIDTH + 128)]
    for first, off in ((768, 0), (1280, 128), (1792, 256)):
        segs += [(first + 128 * hp, 128, 0, PAIR_W * hp + off) for hp in range(4)]
    segs += [(2304, B_HEADS, 2, 0), (2312, 2 * D_MODEL, 1, 0)]
    return segs


def _proj_from_gathered(g):
    def cols(first, width):
        out = []
        for p in range(N_DEV):
            lo, hi = max(first, WIN_SHARD * p), min(first + width, WIN_SHARD * (p + 1))
            if lo < hi:
                out.append(g[p, :, lo - WIN_SHARD * p:hi - WIN_SHARD * p])
        return out

    parts = []
    for arr in (0, 1, 2):
        for first, width, _, _ in sorted((s for s in _proj_segments() if s[2] == arr), key=lambda s: s[3]):
            parts += cols(first, width)
        if arr == 0:
            parts.append(jnp.zeros((D_MODEL, P_GATES - QKV_W), g.dtype))
    parts.append(jnp.zeros((D_MODEL, 128 - B_HEADS), g.dtype))
    return jnp.concatenate(parts, axis=1)


def _proj_to_scatter(dqkv_w, dg_w, df_w):
    arrays = (dqkv_w, dg_w, df_w)
    segs = sorted(_proj_segments())
    blocks = []
    for p in range(N_DEV):
        parts = []
        for first, width, arr, at in segs:
            lo, hi = max(first, WIN_SHARD * p), min(first + width, WIN_SHARD * (p + 1))
            if lo < hi:
                parts.append(arrays[arr][:, at + lo - first:at + hi - first])
        parts.append(jnp.zeros((D_MODEL, WIN_SHARD_P - WIN_SHARD), dqkv_w.dtype))
        blocks.append(jnp.concatenate(parts, axis=1))
    return jnp.stack(blocks, axis=0)


def _a_rows_from_natural(w):
    return jnp.concatenate([w[HEAD_DIM * h:HEAD_DIM * (h + 1)] for h in A_HEAD_ORDER], axis=0)


def _a_rows_to_natural(w):
    return jnp.concatenate([w[HEAD_DIM * A_HEAD_ORDER.index(h):HEAD_DIM * (A_HEAD_ORDER.index(h) + 1)]
                            for h in range(A_HEADS)], axis=0)


def kernel(x, meta_tokens, ffn1_norm, ffn1_w_in, ffn1_w_out, mix_norm, w_in, b_forget, attn_sinks, w_branch_a, w_branch_b, w_out, ffn2_norm, ffn2_w_in, ffn2_w_out, final_norm, loss_target, m_meta_tokens, m_ffn1_norm, m_ffn1_w_in, m_ffn1_w_out, m_mix_norm, m_w_in, m_b_forget, m_attn_sinks, m_w_branch_a, m_w_branch_b, m_w_out, m_ffn2_norm, m_ffn2_w_in, m_ffn2_w_out, m_final_norm, v_meta_tokens, v_ffn1_norm, v_ffn1_w_in, v_ffn1_w_out, v_mix_norm, v_w_in, v_b_forget, v_attn_sinks, v_w_branch_a, v_w_branch_b, v_w_out, v_ffn2_norm, v_ffn2_w_in, v_ffn2_w_out, v_final_norm):
    me = 4 * lax.axis_index("x") + 2 * lax.axis_index("y") + lax.axis_index("c")

    shards = (
        _pad_to(ffn1_w_in[0].astype(BF16), D_MODEL, FF_SHARD_P),
        ffn1_w_out[0].astype(BF16),
        _pad_to(w_in[0].astype(BF16), D_MODEL, WIN_SHARD_P),
        w_branch_a[0].astype(BF16), w_branch_b[0].astype(BF16), w_out[0].astype(BF16),
        _pad_to(ffn2_w_in[0].astype(BF16), D_MODEL, FF_SHARD_P),
        ffn2_w_out[0].astype(BF16),
        meta_tokens,
    )
    s1i, s1o, swi, swa, swb, swo, s2i, s2o, smeta = shards
    first_level, second_level = {}, {}
    first_level["meta"], tok = _gather2_start("gather_meta_start", (smeta,))
    first_level["ffn1_in"], tok = _gather2_start("gather_ffn1_in_start", (s1i,), after=tok)
    first_level["ffn1_out"], tok = _gather2_start("gather_ffn1_out_start", (s1o,), after=tok)
    first_level["mix"], tok = _gather2_start("gather_mix_start", (swi, swa, swb, swo), after=tok)
    first_level["ffn2"], tok = _gather2_start("gather_ffn2_start", (s2i, s2o), after=tok)
    started = {"tok": tok}

    def weights(group, after):
        if group.endswith(":forward"):
            group = group[:-len(":forward")]
            second_level[group], token = _gather2_forward("gather_" + group + "_forward", first_level[group], after)
            return token
        if group == "meta":
            after = weights("meta:forward", started["tok"])
        if group == "ffn1_in":
            after = weights("ffn1_in:forward", after)
        got = _gather2_wait("gather_" + group + "_wait", second_level[group], after)
        if group == "mix":
            gwi, gwa, gwb, gwo = got
            return (_proj_from_gathered(gwi), _a_rows_from_natural(gwa.transpose(1, 0, 2).reshape(A_WIDTH, D_MODEL)),
                    gwb.transpose(1, 0, 2).reshape(B_WIDTH, D_MODEL), gwo.reshape(D_MODEL, D_MODEL))
        if group == "ffn1_out":
            return (_ffn_out_from_gathered(got[0]),)
        if group == "meta":
            return (got[0].transpose(1, 0, 2).reshape(N_META, D_MODEL),)
        if group == "ffn1_in":
            return (got[0].reshape(2, 4, D_MODEL, FF_SHARD_P),)
        return got[0].reshape(2, 4, D_MODEL, FF_SHARD_P), _ffn_out_from_gathered(got[1])

    scatter_state = {}

    def send(group, grads):
        if group == "mix":
            dwi_qkv, dwi_g, dwi_f, dwa, dwb, dwo = grads
            dwa = _a_rows_to_natural(dwa)
            blocks = (_proj_to_scatter(dwi_qkv, dwi_g, dwi_f), dwa.reshape(A_WIDTH, N_DEV, 128).transpose(1, 0, 2),
                      dwb.reshape(B_WIDTH, N_DEV, 128).transpose(1, 0, 2), dwo.reshape(N_DEV, 128, D_MODEL))
        elif group.endswith("_in"):
            blocks = (grads[0].reshape(N_DEV, D_MODEL, FF_SHARD_P),)
        elif group.endswith("_out"):
            blocks = (_ffn_out_to_scatter(grads[0]),)
        else:
            blocks = (grads[0].reshape(N_DEV, D_MODEL, FF_SHARD_P), _ffn_out_to_scatter(grads[1]))
        scatter_state[group], token = _xchg_start("scatter_" + group + "_start", (), blocks)
        return token

    gf = final_norm.reshape(1, D_MODEL)
    grad_x, small = _local_step(x, loss_target, ffn1_norm, mix_norm, ffn2_norm, gf, b_forget, attn_sinks, weights, send)

    small_state, after = _xchg_start("gather_small_start", (small,), ())
    out = {}
    updates = (
        ("ffn2", (("ffn2_w_in", ffn2_w_in, m_ffn2_w_in, v_ffn2_w_in), ("ffn2_w_out", ffn2_w_out, m_ffn2_w_out, v_ffn2_w_out))),
        ("ffn1_out", (("ffn1_w_out", ffn1_w_out, m_ffn1_w_out, v_ffn1_w_out),)),
        ("mix", (("w_in", w_in, m_w_in, v_w_in), ("w_branch_a", w_branch_a, m_w_branch_a, v_w_branch_a),
                 ("w_branch_b", w_branch_b, m_w_branch_b, v_w_branch_b), ("w_out", w_out, m_w_out, v_w_out))),
    )
    last_update = ("ffn1_in", (("ffn1_w_in", ffn1_w_in, m_ffn1_w_in, v_ffn1_w_in),))

    def update(group, members, after):
        parts_list = _xchg_wait("scatter_" + group + "_wait", scatter_state[group], after)
        prev = None
        for (nm, w, m, v), parts in zip(members, parts_list):
            if nm.endswith("w_in"):
                res4 = _adam("adam_" + nm, w[0].T, m[0].T, v[0].T, parts, transposed=True, dep=prev)
                out[nm] = tuple(r.T[None] for r in res4)
            else:
                res4 = _adam("adam_" + nm, w[0], m[0], v[0], parts, dep=prev)
                out[nm] = tuple(r[None] for r in res4)
            prev = res4[0]
        return prev

    for group, members in updates + (last_update,):
        after = update(group, members, after)
    (packs,) = _xchg_wait("gather_small_wait", small_state, after)

    tot = _small_sum("small_sum", packs)
    loss = tot[4, 2 * B_HEADS]
    g_meta = lax.dynamic_slice(tot[8:24, :], (0, me * 128), (N_META, 128))
    out["meta_tokens"] = tuple(_adam("adam_meta_tokens", meta_tokens, m_meta_tokens, v_meta_tokens, g_meta[None]))

    def pack_small(n1, nm, n2, nf, bfv, skv):
        misc = jnp.pad(jnp.concatenate([bfv, skv], axis=1), ((0, 0), (0, D_MODEL - 2 * B_HEADS)))
        row = lax.broadcasted_iota(jnp.int32, (8, D_MODEL), 0)
        vec = jnp.zeros((8, D_MODEL), F32)
        for i, piece in enumerate((n1, nm, n2, nf.reshape(1, D_MODEL), misc)):
            vec = jnp.where(row == i, piece, vec)
        return vec

    w_pack = pack_small(ffn1_norm, mix_norm, ffn2_norm, final_norm, b_forget, attn_sinks)
    m_pack = pack_small(m_ffn1_norm, m_mix_norm, m_ffn2_norm, m_final_norm, m_b_forget, m_attn_sinks)
    v_pack = pack_small(v_ffn1_norm, v_mix_norm, v_ffn2_norm, v_final_norm, v_b_forget, v_attn_sinks)
    small4 = _adam("adam_small", w_pack, m_pack, v_pack, tot[0:8][None])
    for i, nm in enumerate(("ffn1_norm", "mix_norm", "ffn2_norm")):
        out[nm] = tuple(r[i:i + 1] for r in small4)
    out["final_norm"] = tuple(r[3] for r in small4)
    out["b_forget"] = tuple(r[4:5, 0:B_HEADS] for r in small4)
    out["attn_sinks"] = tuple(r[4:5, B_HEADS:2 * B_HEADS] for r in small4)

    names = ("meta_tokens", "ffn1_norm", "ffn1_w_in", "ffn1_w_out", "mix_norm", "w_in", "b_forget", "attn_sinks",
             "w_branch_a", "w_branch_b", "w_out", "ffn2_norm", "ffn2_w_in", "ffn2_w_out", "final_norm")
    return (loss, grad_x) + tuple(out[nm][kind] for kind in range(4) for nm in names)
```

```python
import jax
import jax.numpy as jnp
from jax import lax
from jax.experimental import pallas as pl
from jax.experimental.pallas import tpu as pltpu

F32 = jnp.float32
BF16 = jnp.bfloat16

D_MODEL = 1024
N_META = 16
BLOCK = 128
PREFIX = 128
N_PAD = PREFIX - N_META
HEAD_DIM = 64
A_HEADS = 8
B_HEADS = 8
A_WIDTH = 512
A_KV_WIDTH = 128
B_WIDTH = 512
D_FF = 2816
N_DEV = 8
FF_SHARD = 2 * D_FF // N_DEV
FF_SHARD_P = 768
FFO_SHARD = D_FF // N_DEV
D_FF_P = 4 * FF_SHARD_P
W_IN_COLS = 4360
WIN_SHARD = W_IN_COLS // N_DEV
WIN_SHARD_P = 640
PAIR_W = 3 * 128
B_SEG = 4 * PAIR_W
A_SEG = A_WIDTH + 2 * A_KV_WIDTH
QKV_W = B_SEG + A_SEG
P_GATES = 2 * (2 * D_MODEL)
P_F = P_GATES + 2 * D_MODEL
PROJ_P = P_F + 128
A_HEAD_ORDER = (0, 4, 1, 5, 2, 6, 3, 7)
EPS = 1e-6
NEG = -1e30
SCALE = HEAD_DIM ** -0.5
ADAM_LR = 0.001
ADAM_B1 = 0.9
ADAM_B2 = 0.999
ADAM_EPS = 1e-08
ADAM_WD = 0.01
ADAM_STEP = 10
VMEM_LIMIT = 56 * 1024 * 1024
MESH_ID = pl.DeviceIdType.MESH
SMALL_ROWS = 40

_NN = (((1,), (0,)), ((), ()))
_NT = (((1,), (1,)), ((), ()))
_TN = (((0,), (0,)), ((), ()))


def _params(sem=None):
    return pltpu.CompilerParams(dimension_semantics=sem, vmem_limit_bytes=VMEM_LIMIT)


def _pick(n, cands):
    for c in cands:
        if n % c == 0:
            return c
    raise ValueError(f"no tile for {n}")


def _bf(v):
    return v if v.dtype == BF16 else v.astype(BF16)


def _mm(name, a, b, dims, grid, a_spec, b_spec, o_spec, out_shape, out_dtype, alpha=1.0):
    def body(a_ref, b_ref, o_ref):
        acc = lax.dot_general(_bf(a_ref[...]), _bf(b_ref[...]), dims, preferred_element_type=F32)
        if alpha != 1.0:
            acc = acc * alpha
        o_ref[...] = acc.astype(o_ref.dtype)

    return pl.pallas_call(
        body, name=name, grid=grid, in_specs=[a_spec, b_spec], out_specs=o_spec,
        out_shape=jax.ShapeDtypeStruct(out_shape, out_dtype),
        compiler_params=_params(("parallel",) * len(grid)),
    )(a, b)


def _mm_res_norm(name, a, w, res, g_next, alpha=1.0):
    t, k = a.shape
    tm = _pick(t, (544, 384, 256, 128))

    def body(a_ref, w_ref, r_ref, g_ref, h_ref, n_ref):
        acc = lax.dot_general(_bf(a_ref[...]), w_ref[...], _NN, preferred_element_type=F32)
        if alpha != 1.0:
            acc = acc * alpha
        hv = acc + r_ref[...]
        h_ref[...] = hv
        r = lax.rsqrt(jnp.mean(hv * hv, axis=-1, keepdims=True) + EPS)
        n_ref[...] = ((hv * r) * g_ref[...]).astype(BF16)

    row = pl.BlockSpec((tm, D_MODEL), lambda i: (i, 0))
    return pl.pallas_call(
        body, name=name, grid=(t // tm,),
        in_specs=[pl.BlockSpec((tm, k), lambda i: (i, 0)), pl.BlockSpec((k, D_MODEL), lambda i: (0, 0)), row,
                  pl.BlockSpec((1, D_MODEL), lambda i: (0, 0))],
        out_specs=[row, row],
        out_shape=[jax.ShapeDtypeStruct((t, D_MODEL), F32), jax.ShapeDtypeStruct((t, D_MODEL), BF16)],
        compiler_params=_params(("parallel",)),
    )(a, w, res, g_next)


def _mm_tn(name, a, b, out_dtype=BF16, alpha=1.0, tm_c=(768, 512, 256, 128), tn_c=(512, 640, 256, 128)):
    t, m = a.shape
    n = b.shape[1]
    tm = _pick(m, tm_c)
    tn = _pick(n, tn_c)
    bytes_a, bytes_b = a.size * a.dtype.itemsize, b.size * b.dtype.itemsize
    if bytes_a + bytes_b * (m // tm) <= bytes_b + bytes_a * (n // tn):
        return _mm(name, a, b, _TN, (m // tm, n // tn),
                   pl.BlockSpec((t, tm), lambda i, j: (0, i)), pl.BlockSpec((t, tn), lambda i, j: (0, j)),
                   pl.BlockSpec((tm, tn), lambda i, j: (i, j)), (m, n), out_dtype, alpha=alpha)
    return _mm(name, a, b, _TN, (n // tn, m // tm),
               pl.BlockSpec((t, tm), lambda j, i: (0, i)), pl.BlockSpec((t, tn), lambda j, i: (0, j)),
               pl.BlockSpec((tm, tn), lambda j, i: (i, j)), (m, n), out_dtype, alpha=alpha)


def _ffn_in_fwd(name, n, wblk, dep=None):
    t = n.shape[0]
    tm = _pick(t, (1088, 768, 512, 256, 128))
    has_dep = dep is not None

    def body(n_ref, w_ref, *rest):
        gu_ref, a_ref = rest[-2], rest[-1]
        nv = n_ref[...]
        g = lax.dot_general(nv, w_ref[0], _NN, preferred_element_type=F32)
        u = lax.dot_general(nv, w_ref[1], _NN, preferred_element_type=F32)
        sg = jax.nn.sigmoid(g)
        silu = g * sg
        a_ref[...] = (silu * u).astype(BF16)
        gu_ref[0] = ((0.5 * u) * (sg + silu * (1.0 - sg))).astype(BF16)
        gu_ref[1] = (0.5 * silu).astype(BF16)

    return pl.pallas_call(
        body, name=name, grid=(t // tm, 4),
        in_specs=[pl.BlockSpec((tm, D_MODEL), lambda i, j: (i, 0)),
                  pl.BlockSpec((2, None, D_MODEL, FF_SHARD_P), lambda i, j: (0, j, 0, 0))]
        + ([pl.BlockSpec(memory_space=pl.ANY)] if has_dep else []),
        out_specs=[pl.BlockSpec((2, tm, FF_SHARD_P), lambda i, j: (0, i, j)),
                   pl.BlockSpec((tm, FF_SHARD_P), lambda i, j: (i, j))],
        out_shape=[jax.ShapeDtypeStruct((2, t, D_FF_P), BF16), jax.ShapeDtypeStruct((t, D_FF_P), BF16)],
        compiler_params=_params(("parallel", "parallel")),
    )(*((n, wblk) + ((dep,) if has_dep else ())))


def _ffn_out_bwd_x(name, dh, w_out, gu, dep=None):
    t = dh.shape[0]
    tm = _pick(t, (1088, 768, 512, 256, 128))
    has_dep = dep is not None

    def body(dh_ref, w_ref, gu_ref, *rest):
        o_ref = rest[-1]
        da = lax.dot_general(_bf(dh_ref[...]), w_ref[...], _NT, preferred_element_type=F32)
        o_ref[0] = (da * gu_ref[0].astype(F32)).astype(BF16)
        o_ref[1] = (da * gu_ref[1].astype(F32)).astype(BF16)

    gu_spec = pl.BlockSpec((2, tm, FF_SHARD_P), lambda i, j: (0, i, j))
    return pl.pallas_call(
        body, name=name, grid=(t // tm, 4),
        in_specs=[pl.BlockSpec((tm, D_MODEL), lambda i, j: (i, 0)), pl.BlockSpec((FF_SHARD_P, D_MODEL), lambda i, j: (j, 0)),
                  gu_spec] + ([pl.BlockSpec(memory_space=pl.ANY)] if has_dep else []),
        out_specs=gu_spec, out_shape=jax.ShapeDtypeStruct((2, t, D_FF_P), BF16),
        compiler_params=_params(("parallel", "parallel")),
    )(*((dh, w_out, gu) + ((dep,) if has_dep else ())))


def _rms_bwd_rows(dn, h, g, dres):
    r = lax.rsqrt(jnp.mean(h * h, axis=-1, keepdims=True) + EPS)
    tv = dn * g
    dot = jnp.mean(tv * h, axis=-1, keepdims=True)
    return dres + (r * tv - h * (r * r * r * dot)), jnp.sum(dn * (h * r), axis=0, keepdims=True)


def _accumulate_rows(ref, part, first):
    @pl.when(first)
    def _():
        ref[...] = part

    @pl.when(jnp.logical_not(first))
    def _():
        ref[...] += part


def _ffn_in_bwd_x(name, dgu, wblk, h_in, g_norm, dres, dep=None, examples=None):
    t = dgu.shape[1]
    tm = _pick(t, (544, 384, 256, 128))
    has_dep = dep is not None
    split = examples is not None
    if split:
        l = t // examples
        per = l // tm
        assert per * tm == l and tm > PREFIX

    def body(d_ref, w_ref, h_ref, g_ref, r_ref, *rest):
        acc = None
        for s in range(2):
            for j in range(4):
                part = lax.dot_general(d_ref[s, :, FF_SHARD_P * j:FF_SHARD_P * (j + 1)], w_ref[s, j], _NT,
                                       preferred_element_type=F32)
                acc = part if acc is None else acc + part
        dh, dg = _rms_bwd_rows(acc, h_ref[...], g_ref[...], r_ref[...])
        i = pl.program_id(0)
        if not split:
            dh_ref, dg_ref = rest[-2], rest[-1]
            dh_ref[...] = dh
        else:
            gx_ref, meta_ref, dg_ref, buf, sem = rest[-5:]

            def out_copy(step):
                bi, r = step // per, step % per
                head = pltpu.make_async_copy(buf.at[pl.ds(PREFIX, tm - PREFIX)], gx_ref.at[bi, pl.ds(0, tm - PREFIX)], sem)
                if per == 1:
                    return r == 0, head, None
                return r == 0, head, pltpu.make_async_copy(
                    buf, gx_ref.at[bi, pl.ds(pl.multiple_of(jnp.maximum(r, 1) * tm - PREFIX, 8), tm)], sem)

            def run(step, method):
                is_head, head, later = out_copy(step)

                @pl.when(is_head)
                def _():
                    getattr(head, method)()

                if later is not None:
                    @pl.when(jnp.logical_not(is_head))
                    def _():
                        getattr(later, method)()

            @pl.when(i > 0)
            def _():
                run(i - 1, "wait")

            buf[...] = dh

            @pl.when(i % per == 0)
            def _():
                meta_ref[...] = dh[N_PAD:PREFIX]

            run(i, "start")

            @pl.when(i == pl.num_programs(0) - 1)
            def _():
                run(i, "wait")

        _accumulate_rows(dg_ref, dg, i == 0)

    row = pl.BlockSpec((tm, D_MODEL), lambda i: (i, 0))
    vec = pl.BlockSpec((1, D_MODEL), lambda i: (0, 0))
    if split:
        out_specs = [pl.BlockSpec(memory_space=pl.ANY), pl.BlockSpec((None, N_META, D_MODEL), lambda i: (i // per, 0, 0)), vec]
        out_shape = [jax.ShapeDtypeStruct((examples, l - PREFIX, D_MODEL), F32),
                     jax.ShapeDtypeStruct((examples, N_META, D_MODEL), F32), jax.ShapeDtypeStruct((1, D_MODEL), F32)]
        scratch = [pltpu.VMEM((tm, D_MODEL), F32), pltpu.SemaphoreType.DMA(())]
    else:
        out_specs = [row, vec]
        out_shape = [jax.ShapeDtypeStruct((t, D_MODEL), F32), jax.ShapeDtypeStruct((1, D_MODEL), F32)]
        scratch = []
    return pl.pallas_call(
        body, name=name, grid=(t // tm,),
        in_specs=[pl.BlockSpec((2, tm, D_FF_P), lambda i: (0, i, 0)),
                  pl.BlockSpec((2, 4, D_MODEL, FF_SHARD_P), lambda i: (0, 0, 0, 0), pipeline_mode=pl.Buffered(1)),
                  row, vec, row]
        + ([pl.BlockSpec(memory_space=pl.ANY)] if has_dep else []),
        out_specs=out_specs, out_shape=out_shape, scratch_shapes=scratch,
        compiler_params=_params(("arbitrary",)),
    )(*((dgu, wblk, h_in, g_norm, dres) + ((dep,) if has_dep else ())))


def _proj_fwd(name, um, wi):
    t = um.shape[0]
    tm = _pick(t, (544, 384, 256, 128))

    def body(u_ref, w_ref, q_ref, g_ref, f_ref):
        uv = u_ref[...]
        q_ref[...] = lax.dot_general(uv, w_ref[:, 0:QKV_W], _NN, preferred_element_type=F32).astype(BF16)
        g_ref[...] = lax.dot_general(uv, w_ref[:, P_GATES:P_F], _NN, preferred_element_type=F32)
        f_ref[...] = lax.dot_general(uv, w_ref[:, P_F:PROJ_P], _NN, preferred_element_type=F32)

    return pl.pallas_call(
        body, name=name, grid=(t // tm,),
        in_specs=[pl.BlockSpec((tm, D_MODEL), lambda i: (i, 0)),
                  pl.BlockSpec((D_MODEL, PROJ_P), lambda i: (0, 0), pipeline_mode=pl.Buffered(1))],
        out_specs=[pl.BlockSpec((tm, QKV_W), lambda i: (i, 0)), pl.BlockSpec((tm, 2 * D_MODEL), lambda i: (i, 0)),
                   pl.BlockSpec((tm, 128), lambda i: (i, 0))],
        out_shape=[jax.ShapeDtypeStruct((t, QKV_W), BF16), jax.ShapeDtypeStruct((t, 2 * D_MODEL), F32),
                   jax.ShapeDtypeStruct((t, 128), F32)],
        compiler_params=_params(("parallel",)),
    )(um, wi)


def _proj_bwd_x(name, dqkv, dgates, df, wi, h_in, g_norm, dres, dep=None):
    t = dqkv.shape[0]
    tm = _pick(t, (544, 384, 256, 128))
    has_dep = dep is not None

    def body(q_ref, gt_ref, f_ref, w_ref, h_ref, g_ref, r_ref, *rest):
        dh_ref, dg_ref = rest[-2], rest[-1]
        acc = lax.dot_general(q_ref[...], w_ref[:, 0:QKV_W], _NT, preferred_element_type=F32)
        acc = acc + lax.dot_general(gt_ref[...], w_ref[:, P_GATES:P_F], _NT, preferred_element_type=F32)
        acc = acc + lax.dot_general(f_ref[...], w_ref[:, P_F:PROJ_P], _NT, preferred_element_type=F32)
        dh, dg = _rms_bwd_rows(acc, h_ref[...], g_ref[...], r_ref[...])
        dh_ref[...] = dh
        _accumulate_rows(dg_ref, dg, pl.program_id(0) == 0)

    row = pl.BlockSpec((tm, D_MODEL), lambda i: (i, 0))
    vec = pl.BlockSpec((1, D_MODEL), lambda i: (0, 0))
    return pl.pallas_call(
        body, name=name, grid=(t // tm,),
        in_specs=[pl.BlockSpec((tm, QKV_W), lambda i: (i, 0)), pl.BlockSpec((tm, 2 * D_MODEL), lambda i: (i, 0)),
                  pl.BlockSpec((tm, 128), lambda i: (i, 0)),
                  pl.BlockSpec((D_MODEL, PROJ_P), lambda i: (0, 0), pipeline_mode=pl.Buffered(1)), row, vec, row]
        + ([pl.BlockSpec(memory_space=pl.ANY)] if has_dep else []),
        out_specs=[row, vec],
        out_shape=[jax.ShapeDtypeStruct((t, D_MODEL), F32), jax.ShapeDtypeStruct((1, D_MODEL), F32)],
        compiler_params=_params(("arbitrary",)),
    )(*((dqkv, dgates, df, wi, h_in, g_norm, dres) + ((dep,) if has_dep else ())))


def _ffn_in_bwd_w(name, n, dgu):
    t = n.shape[0]
    return _mm(name, n, dgu, _TN, (2, 4),
               pl.BlockSpec((t, D_MODEL), lambda s, j: (0, 0)),
               pl.BlockSpec((None, t, FF_SHARD_P), lambda s, j: (s, 0, j)),
               pl.BlockSpec((None, None, D_MODEL, FF_SHARD_P), lambda s, j: (s, j, 0, 0)),
               (2, 4, D_MODEL, FF_SHARD_P), BF16)


def _embed_norm(name, x, meta, g):
    b, s, _ = x.shape
    half = (s + PREFIX) // 2
    first = half - PREFIX
    assert first > 0 and half % 16 == 0

    def body(x_ref, m_ref, g_ref, h_ref, n_ref, buf, sem):
        bi, k = pl.program_id(0), pl.program_id(1)

        @pl.when(k == 0)
        def _():
            buf[0:N_PAD, :] = jnp.zeros((N_PAD, D_MODEL), F32)
            buf[N_PAD:PREFIX, :] = m_ref[...]
            cp = pltpu.make_async_copy(x_ref.at[bi, pl.ds(0, first)], buf.at[pl.ds(PREFIX, first)], sem)
            cp.start()
            cp.wait()

        @pl.when(k == 1)
        def _():
            cp = pltpu.make_async_copy(x_ref.at[bi, pl.ds(first, half)], buf, sem)
            cp.start()
            cp.wait()

        hv = buf[...]
        h_ref[...] = hv
        r = lax.rsqrt(jnp.mean(hv * hv, axis=-1, keepdims=True) + EPS)
        n_ref[...] = ((hv * r) * g_ref[...]).astype(BF16)

    rows = pl.BlockSpec((half, D_MODEL), lambda bi, k: (2 * bi + k, 0))
    return pl.pallas_call(
        body, name=name, grid=(b, 2),
        in_specs=[pl.BlockSpec(memory_space=pl.ANY), pl.BlockSpec((N_META, D_MODEL), lambda bi, k: (0, 0)),
                  pl.BlockSpec((1, D_MODEL), lambda bi, k: (0, 0))],
        out_specs=[rows, rows],
        out_shape=[jax.ShapeDtypeStruct((2 * b * half, D_MODEL), F32), jax.ShapeDtypeStruct((2 * b * half, D_MODEL), BF16)],
        scratch_shapes=[pltpu.VMEM((half, D_MODEL), F32), pltpu.SemaphoreType.DMA(())],
        compiler_params=_params(("arbitrary", "arbitrary")),
    )(x, meta, g)


def _branch_gate_fwd(name, oa, ob, wa, wb, gates, dep=None):
    t = gates.shape[0]
    tm = _pick(t, (544, 384, 256, 128))
    has_dep = dep is not None

    def body(oa_ref, ob_ref, wa_ref, wb_ref, g_ref, *rest):
        o_ref, ya_ref, yb_ref = rest[-3:]
        ya = lax.dot_general(_bf(oa_ref[...]), wa_ref[...], _NN, preferred_element_type=F32)
        yb = lax.dot_general(_bf(ob_ref[...]), wb_ref[...], _NN, preferred_element_type=F32)
        sa = jax.nn.sigmoid(g_ref[:, 0:D_MODEL])
        sb = jax.nn.sigmoid(g_ref[:, D_MODEL:2 * D_MODEL])
        o_ref[...] = (sa * ya + sb * yb).astype(BF16)
        ya_ref[...] = ya.astype(BF16)
        yb_ref[...] = yb.astype(BF16)

    blk = pl.BlockSpec((tm, D_MODEL), lambda i: (i, 0))
    narrow = pl.BlockSpec((tm, A_WIDTH), lambda i: (i, 0))
    wide = pl.BlockSpec((tm, 2 * D_MODEL), lambda i: (i, 0))
    wspec = pl.BlockSpec((A_WIDTH, D_MODEL), lambda i: (0, 0))
    out = jax.ShapeDtypeStruct((t, D_MODEL), BF16)
    return pl.pallas_call(
        body, name=name, grid=(t // tm,),
        in_specs=[narrow, narrow, wspec, wspec, wide] + ([pl.BlockSpec(memory_space=pl.ANY)] if has_dep else []),
        out_specs=[blk, blk, blk], out_shape=[out, out, out], compiler_params=_params(("parallel",)),
    )(*((oa, ob, wa, wb, gates) + ((dep,) if has_dep else ())))


def _branch_bwd_x(name, dya, dyb, wa, wb):
    t = dya.shape[0]
    tm = _pick(t, (1088, 768, 512, 256, 128))

    def body(da_ref, db_ref, wa_ref, wb_ref, oa_ref, ob_ref):
        oa_ref[...] = lax.dot_general(da_ref[...], wa_ref[...], _NT, preferred_element_type=F32)
        ob_ref[...] = lax.dot_general(db_ref[...], wb_ref[...], _NT, preferred_element_type=F32).astype(BF16)

    blk = pl.BlockSpec((tm, D_MODEL), lambda i: (i, 0))
    narrow = pl.BlockSpec((tm, A_WIDTH), lambda i: (i, 0))
    wspec = pl.BlockSpec((A_WIDTH, D_MODEL), lambda i: (0, 0), pipeline_mode=pl.Buffered(1))
    return pl.pallas_call(
        body, name=name, grid=(t // tm,), in_specs=[blk, blk, wspec, wspec], out_specs=[narrow, narrow],
        out_shape=[jax.ShapeDtypeStruct((t, A_WIDTH), F32), jax.ShapeDtypeStruct((t, B_WIDTH), BF16)],
        compiler_params=_params(("parallel",)),
    )(dya, dyb, wa, wb)


def _branch_bwd_w(name, oa, ob, dya, dyb):
    t = oa.shape[0]
    tn = 512

    def body(oa_ref, ob_ref, da_ref, db_ref, wa_ref, wb_ref):
        wa_ref[...] = lax.dot_general(_bf(oa_ref[...]), da_ref[...], _TN, preferred_element_type=F32).astype(BF16)
        wb_ref[...] = lax.dot_general(_bf(ob_ref[...]), db_ref[...], _TN, preferred_element_type=F32).astype(BF16)

    whole = pl.BlockSpec((t, A_WIDTH), lambda j: (0, 0), pipeline_mode=pl.Buffered(1))
    cols = pl.BlockSpec((t, tn), lambda j: (0, j))
    out_spec = pl.BlockSpec((A_WIDTH, tn), lambda j: (0, j))
    out = jax.ShapeDtypeStruct((A_WIDTH, D_MODEL), BF16)
    return pl.pallas_call(
        body, name=name, grid=(D_MODEL // tn,), in_specs=[whole, whole, cols, cols], out_specs=[out_spec, out_spec],
        out_shape=[out, out], compiler_params=_params(("parallel",)),
    )(oa, ob, dya, dyb)


def _mix_out_gate_bwd(name, dh, wo, gates, ya, yb):
    t = gates.shape[0]
    tm = _pick(t, (544, 384, 256, 128))

    def body(dh_ref, w_ref, g_ref, ya_ref, yb_ref, dya_ref, dyb_ref, dg_ref):
        dm = lax.dot_general(_bf(dh_ref[...]), w_ref[...], _NT, preferred_element_type=F32)
        sa = jax.nn.sigmoid(g_ref[:, 0:D_MODEL])
        sb = jax.nn.sigmoid(g_ref[:, D_MODEL:2 * D_MODEL])
        dya_ref[...] = (dm * sa).astype(BF16)
        dyb_ref[...] = (dm * sb).astype(BF16)
        dg_ref[:, 0:D_MODEL] = (dm * ya_ref[...].astype(F32) * (sa * (1.0 - sa))).astype(BF16)
        dg_ref[:, D_MODEL:2 * D_MODEL] = (dm * yb_ref[...].astype(F32) * (sb * (1.0 - sb))).astype(BF16)

    blk = pl.BlockSpec((tm, D_MODEL), lambda i: (i, 0))
    wide = pl.BlockSpec((tm, 2 * D_MODEL), lambda i: (i, 0))
    out = jax.ShapeDtypeStruct((t, D_MODEL), BF16)
    return pl.pallas_call(
        body, name=name, grid=(t // tm,),
        in_specs=[blk, pl.BlockSpec((D_MODEL, D_MODEL), lambda i: (0, 0)), wide, blk, blk], out_specs=[blk, blk, wide],
        out_shape=[out, out, jax.ShapeDtypeStruct((t, 2 * D_MODEL), BF16)], compiler_params=_params(("parallel",)),
    )(dh, wo, gates, ya, yb)


def _ffn_out_loss(name, a, w, res, gf, tgt, alpha):
    t, k = a.shape
    b, s, _ = tgt.shape
    l = t // b
    tm = _pick(t, (544, 384, 256, 128))
    per = l // tm
    assert per * tm == l and tm > PREFIX and l - PREFIX == s

    def body(a_ref, w_ref, r_ref, g_ref, t_ref, dh_ref, loss_ref, dg_ref, buf, sem):
        i = pl.program_id(0)
        bi, r = i // per, i % per

        def first_rows():
            return pltpu.make_async_copy(t_ref.at[bi, pl.ds(0, tm - PREFIX)], buf.at[pl.ds(PREFIX, tm - PREFIX)], sem)

        def later_rows():
            return pltpu.make_async_copy(t_ref.at[bi, pl.ds(pl.multiple_of(r * tm - PREFIX, 8), tm)], buf, sem)

        @pl.when(r == 0)
        def _():
            buf[0:PREFIX, :] = jnp.zeros((PREFIX, D_MODEL), F32)
            first_rows().start()

        if per > 1:
            @pl.when(r > 0)
            def _():
                later_rows().start()

        acc = lax.dot_general(a_ref[...], w_ref[...], _NN, preferred_element_type=F32)

        @pl.when(r == 0)
        def _():
            first_rows().wait()

        if per > 1:
            @pl.when(r > 0)
            def _():
                later_rows().wait()

        hv = acc * alpha + r_ref[...]
        row = lax.broadcasted_iota(jnp.int32, (tm, 1), 0)
        real = ((r > 0) | (row >= PREFIX)).astype(F32)
        g = g_ref[...]
        rn = lax.rsqrt(jnp.mean(hv * hv, axis=-1, keepdims=True) + EPS)
        xn = hv * rn
        err = (xn * g - buf[...]) * real
        lpart = 0.5 * jnp.sum(jnp.mean(err * err, axis=-1, keepdims=True), axis=0, keepdims=True)
        dy = err * (1.0 / D_MODEL)
        tv = dy * g
        dot = jnp.mean(tv * hv, axis=-1, keepdims=True)
        dh_ref[...] = rn * tv - hv * (rn * rn * rn * dot)
        gpart = jnp.sum(dy * xn, axis=0, keepdims=True)

        @pl.when(i == 0)
        def _():
            loss_ref[...] = jnp.zeros_like(loss_ref)
            dg_ref[...] = jnp.zeros_like(dg_ref)

        loss_ref[...] += jnp.broadcast_to(lpart, loss_ref.shape)
        dg_ref[...] += gpart

    row_spec = pl.BlockSpec((tm, D_MODEL), lambda i: (i, 0))
    vec = pl.BlockSpec((1, D_MODEL), lambda i: (0, 0))
    return pl.pallas_call(
        body, name=name, grid=(t // tm,),
        in_specs=[pl.BlockSpec((tm, k), lambda i: (i, 0)),
                  pl.BlockSpec((k, D_MODEL), lambda i: (0, 0), pipeline_mode=pl.Buffered(1)), row_spec, vec,
                  pl.BlockSpec(memory_space=pl.ANY)],
        out_specs=[row_spec, pl.BlockSpec((8, 128), lambda i: (0, 0)), vec],
        out_shape=[jax.ShapeDtypeStruct((t, D_MODEL), F32), jax.ShapeDtypeStruct((8, 128), F32),
                   jax.ShapeDtypeStruct((1, D_MODEL), F32)],
        scratch_shapes=[pltpu.VMEM((tm, D_MODEL), F32), pltpu.SemaphoreType.DMA(())],
        compiler_params=_params(("arbitrary",)),
    )(a, w, res, gf, tgt)


def _fgate_fwd(name, f3, bf_row):
    b, l, _ = f3.shape
    nb = l // BLOCK

    def body(f_ref, b_ref, cc_ref, cr_ref):
        r_i = lax.broadcasted_iota(jnp.int32, (BLOCK, BLOCK), 0)
        c_i = lax.broadcasted_iota(jnp.int32, (BLOCK, BLOCK), 1)
        tri = (r_i >= c_i).astype(F32)
        carry = jnp.zeros((1, 128), F32)
        for blk in range(nb):
            rows = slice(blk * BLOCK, (blk + 1) * BLOCK)
            z = f_ref[rows, :] + b_ref[...]
            lf = jnp.minimum(z, 0.0) - jnp.log(1.0 + jnp.exp(-jnp.abs(z)))
            cb = jnp.dot(tri, lf, preferred_element_type=F32, precision=lax.Precision.HIGHEST) + carry
            carry = cb[BLOCK - 1:BLOCK, :]
            cbt = cb.T
            for hh in range(B_HEADS):
                cc_ref[hh, rows, :] = jnp.sum(jnp.where(c_i == hh, cb, 0.0), axis=1, keepdims=True)
                cr_ref[hh, :, rows] = cbt[hh:hh + 1, :]

    return pl.pallas_call(
        body, name=name, grid=(b,),
        in_specs=[pl.BlockSpec((None, l, 128), lambda bi: (bi, 0, 0)),
                  pl.BlockSpec((1, 128), lambda bi: (0, 0))],
        out_specs=[pl.BlockSpec((None, B_HEADS, l, 1), lambda bi: (bi, 0, 0, 0)),
                   pl.BlockSpec((None, B_HEADS, 1, l), lambda bi: (bi, 0, 0, 0))],
        out_shape=[jax.ShapeDtypeStruct((b, B_HEADS, l, 1), F32), jax.ShapeDtypeStruct((b, B_HEADS, 1, l), F32)],
        compiler_params=_params(("parallel",)),
    )(f3, bf_row)


def _fgate_bwd(name, f3, bf_row, dcq, dck):
    b, l, _ = f3.shape
    nb = l // BLOCK

    def body(f_ref, b_ref, dcq_ref, dck_ref, df_ref, db_ref):
        r_i = lax.broadcasted_iota(jnp.int32, (BLOCK, BLOCK), 0)
        c_i = lax.broadcasted_iota(jnp.int32, (BLOCK, BLOCK), 1)
        tri = (r_i <= c_i).astype(F32)
        carry = jnp.zeros((1, 128), F32)
        total = jnp.zeros((1, 128), F32)
        for blk in range(nb - 1, -1, -1):
            rows = slice(blk * BLOCK, (blk + 1) * BLOCK)
            krows = jnp.concatenate([dck_ref[hh, :, rows] for hh in range(B_HEADS)]
                                    + [jnp.zeros((BLOCK - B_HEADS, BLOCK), F32)], axis=0)
            dcb = krows.T
            for hh in range(B_HEADS):
                dcb = dcb + jnp.where(c_i == hh, dcq_ref[hh, rows, :], 0.0)
            rc = jnp.dot(tri, dcb, preferred_element_type=F32, precision=lax.Precision.HIGHEST) + carry
            carry = rc[0:1, :]
            z = f_ref[rows, :] + b_ref[...]
            df = rc * (1.0 / (1.0 + jnp.exp(z)))
            df_ref[rows, :] = df.astype(BF16)
            total = total + jnp.sum(df, axis=0, keepdims=True)

        @pl.when(pl.program_id(0) == 0)
        def _():
            db_ref[...] = total

        @pl.when(pl.program_id(0) > 0)
        def _():
            db_ref[...] += total

    return pl.pallas_call(
        body, name=name, grid=(b,),
        in_specs=[pl.BlockSpec((None, l, 128), lambda bi: (bi, 0, 0)),
                  pl.BlockSpec((1, 128), lambda bi: (0, 0)),
                  pl.BlockSpec((None, B_HEADS, l, 1), lambda bi: (bi, 0, 0, 0)),
                  pl.BlockSpec((None, B_HEADS, 1, l), lambda bi: (bi, 0, 0, 0))],
        out_specs=[pl.BlockSpec((None, l, 128), lambda bi: (bi, 0, 0)), pl.BlockSpec((1, 128), lambda bi: (0, 0))],
        out_shape=[jax.ShapeDtypeStruct((b, l, 128), BF16), jax.ShapeDtypeStruct((1, 128), F32)],
        compiler_params=_params(("arbitrary",)),
    )(f3, bf_row, dcq, dck)


A_Q_BLK = B_SEG // A_WIDTH
A_K_BLK = (B_SEG + A_WIDTH) // 128
A_V_BLK = A_K_BLK + 1
A_SEG_BLK = B_SEG // A_SEG
STACK = A_HEADS * BLOCK


def _lane_lo():
    return lax.broadcasted_iota(jnp.int32, (1, 128), 1) < HEAD_DIM


def _stack_heads(x, masked):
    lo = _lane_lo()
    blks = [x[:, 128 * j:128 * (j + 1)] for j in range(4)]
    if not masked:
        return jnp.concatenate(blks + blks, axis=0)
    zero = jnp.zeros_like(blks[0])
    return jnp.concatenate([jnp.where(lo, bk, zero) for bk in blks] + [jnp.where(lo, zero, bk) for bk in blks], axis=0)


def _unstack_heads(y):
    lo = _lane_lo()
    return jnp.concatenate([jnp.where(lo, y[128 * j:128 * (j + 1)], y[128 * (4 + j):128 * (5 + j)]) for j in range(4)], axis=1)


def _swa_bias(slopes):
    r_i = jnp.arange(STACK)[:, None]
    c_i = jnp.arange(3 * BLOCK)[None, :]
    seg = c_i >> 7
    out = []
    for n in range(3):
        qpos = n * BLOCK + (r_i & (BLOCK - 1))
        kpos = jnp.where(seg == 0, c_i, (n - 2) * BLOCK + c_i)
        dist = qpos - kpos
        band = (seg != 0) & (dist < BLOCK) & (kpos >= PREFIX)
        meta = (seg == 0) & (c_i >= N_PAD)
        out.append(jnp.where((dist >= 0) & (band | meta), -slopes * dist.astype(F32), NEG))
    return jnp.stack(out, axis=0)


def _swa_scores(q, kcat, n, slope, bias):
    s = lax.dot_general(q, kcat, _NT, preferred_element_type=F32) + bias
    further = slope * (-BLOCK * jnp.maximum(n - 2, 0)).astype(F32)
    return jnp.concatenate([s[:, 0:BLOCK] + further, s[:, BLOCK:]], axis=1)


def _swa_specs():
    def kv(col_blk):
        return [pl.BlockSpec((None, BLOCK, 128), lambda b, n: (b, 0, col_blk)),
                pl.BlockSpec((None, BLOCK, 128), lambda b, n: (b, jnp.maximum(n - 1, 0), col_blk)),
                pl.BlockSpec((None, BLOCK, 128), lambda b, n: (b, n, col_blk))]

    q_spec = pl.BlockSpec((None, BLOCK, A_WIDTH), lambda b, n: (b, n, A_Q_BLK))
    o_spec = pl.BlockSpec((None, BLOCK, A_WIDTH), lambda b, n: (b, n, 0))
    col = pl.BlockSpec((STACK, 1), lambda b, n: (0, 0))
    bias = pl.BlockSpec((None, STACK, 3 * BLOCK), lambda b, n: (jnp.minimum(n, 2), 0, 0))
    lse_spec = pl.BlockSpec((None, A_HEADS, BLOCK, 1), lambda b, n: (b, 0, n, 0))
    return q_spec, kv(A_K_BLK), kv(A_V_BLK), o_spec, [col, col, bias], lse_spec


def _swa_fwd(name, qkv, slopes, sinks, bias):
    b, l, _ = qkv.shape
    nb = l // BLOCK

    def body(q_ref, k0_ref, kp_ref, kc_ref, v0_ref, vp_ref, vc_ref, sl_ref, sk_ref, bias_ref, o_ref, lse_ref):
        n = pl.program_id(1)
        qs = _stack_heads(q_ref[...], True) * SCALE
        kcat = jnp.concatenate([k0_ref[...], kp_ref[...], kc_ref[...]], axis=0)
        vcat = jnp.concatenate([v0_ref[...], vp_ref[...], vc_ref[...]], axis=0)
        s = _swa_scores(qs, kcat, n, sl_ref[...], bias_ref[...])
        sink = sk_ref[...]
        m = jnp.maximum(jnp.max(s, axis=-1, keepdims=True), sink)
        p = jnp.exp(s - m)
        den = jnp.sum(p, axis=-1, keepdims=True) + jnp.exp(sink - m)
        o = lax.dot_general(p.astype(BF16), vcat, _NN, preferred_element_type=F32) / den
        o_ref[...] = _unstack_heads(o)
        lse_ref[...] = (m + jnp.log(den)).reshape(A_HEADS, BLOCK, 1)

    q_spec, k_specs, v_specs, o_spec, consts, lse_spec = _swa_specs()
    return pl.pallas_call(
        body, name=name, grid=(b, nb),
        in_specs=[q_spec] + k_specs + v_specs + consts, out_specs=[o_spec, lse_spec],
        out_shape=[jax.ShapeDtypeStruct((b, l, A_WIDTH), F32), jax.ShapeDtypeStruct((b, A_HEADS, l, 1), F32)],
        compiler_params=_params(("parallel", "parallel")),
    )(qkv, qkv, qkv, qkv, qkv, qkv, qkv, slopes, sinks, bias)


def _swa_bwd(name, qkv, o, lse, do, slopes, sinks, bias, dqkv):
    b, l, _ = qkv.shape
    nb = l // BLOCK

    def body(q_ref, k0_ref, kp_ref, kc_ref, v0_ref, vp_ref, vc_ref, o_ref, lse_ref, do_ref, sl_ref, sk_ref, bias_ref, _,
             dx_ref, ds_ref, dk_acc, dv_acc):
        bi = pl.program_id(0)
        n = pl.program_id(1)
        qs = _stack_heads(q_ref[...], True) * SCALE
        dos32 = _stack_heads(do_ref[...], True)
        dos = dos32.astype(BF16)
        os_ = _stack_heads(o_ref[...], False)
        lsev = lse_ref[...].reshape(STACK, 1)
        kcat = jnp.concatenate([k0_ref[...], kp_ref[...], kc_ref[...]], axis=0)
        vcat = jnp.concatenate([v0_ref[...], vp_ref[...], vc_ref[...]], axis=0)
        s = _swa_scores(qs, kcat, n, sl_ref[...], bias_ref[...])
        p = jnp.exp(s - lsev)
        dsum = jnp.sum(dos32 * os_, axis=-1, keepdims=True)
        dp = lax.dot_general(dos, vcat, _NT, preferred_element_type=F32)
        dsc = (p * (dp - dsum)).astype(BF16)
        dq = lax.dot_general(dsc, kcat, _NN, preferred_element_type=F32) * SCALE
        row0 = pl.multiple_of(n * BLOCK, BLOCK)
        dx_ref[pl.ds(row0, BLOCK), 0:A_WIDTH] = _unstack_heads(dq).astype(BF16)
        dkc = lax.dot_general(dsc, qs, _TN, preferred_element_type=F32)
        dvc = lax.dot_general(p.astype(BF16), dos, _TN, preferred_element_type=F32)

        @pl.when(n == 0)
        def _():
            dk_acc[...] = jnp.zeros_like(dk_acc)
            dv_acc[...] = jnp.zeros_like(dv_acc)

        starts = (0, pl.multiple_of(jnp.maximum(n - 1, 0) * BLOCK, BLOCK), row0)
        for t, st in enumerate(starts):
            dk_acc[pl.ds(st, BLOCK), :] += dkc[t * BLOCK:(t + 1) * BLOCK, :]
            dv_acc[pl.ds(st, BLOCK), :] += dvc[t * BLOCK:(t + 1) * BLOCK, :]

        @pl.when(n == nb - 1)
        def _():
            dx_ref[:, A_WIDTH:A_WIDTH + 128] = dk_acc[...].astype(BF16)
            dx_ref[:, A_WIDTH + 128:A_SEG] = dv_acc[...].astype(BF16)

        dsink = -(jnp.exp(sk_ref[...] - lsev) * dsum)
        r8 = lax.broadcasted_iota(jnp.int32, (8, 128), 0)
        acc = jnp.zeros((8, 128), F32)
        for hh in range(A_HEADS):
            acc = acc + jnp.where(r8 == hh, jnp.sum(dsink[hh * BLOCK:(hh + 1) * BLOCK, :]), 0.0)

        @pl.when((bi == 0) & (n == 0))
        def _():
            ds_ref[...] = jnp.zeros_like(ds_ref)

        ds_ref[...] += acc

    q_spec, k_specs, v_specs, o_spec, consts, lse_spec = _swa_specs()
    return pl.pallas_call(
        body, name=name, grid=(b, nb),
        in_specs=[q_spec] + k_specs + v_specs + [o_spec, lse_spec, o_spec] + consts + [pl.BlockSpec(memory_space=pl.ANY)],
        out_specs=[pl.BlockSpec((None, l, A_SEG), lambda bb, n: (bb, 0, A_SEG_BLK)),
                   pl.BlockSpec((8, 128), lambda bb, n: (0, 0))],
        out_shape=[jax.ShapeDtypeStruct(dqkv.shape, BF16), jax.ShapeDtypeStruct((8, 128), F32)],
        scratch_shapes=[pltpu.VMEM((l, 128), F32), pltpu.VMEM((l, 128), F32)],
        input_output_aliases={13: 0},
        compiler_params=_params(("arbitrary", "arbitrary")),
    )(qkv, qkv, qkv, qkv, qkv, qkv, qkv, o, lse, do, slopes, sinks, bias, dqkv)


def _fox_mask(qk, ck, i):
    kh = qk.shape[1]
    qpos = i * BLOCK + lax.broadcasted_iota(jnp.int32, (BLOCK, kh), 0)
    kpos = lax.broadcasted_iota(jnp.int32, (BLOCK, kh), 1)
    return jnp.where((kpos <= qpos) & (kpos >= N_PAD), qk - ck, NEG)


def _pick_head(x, hh):
    lo = _lane_lo()
    return jnp.where(lo if hh == 0 else jnp.logical_not(lo), x, jnp.zeros_like(x))


def _both_heads(x):
    return jnp.concatenate([_pick_head(x, 0), _pick_head(x, 1)], axis=0)


def _fox_specs(l):
    pair = pl.BlockSpec((None, l, PAIR_W), lambda bi, hp: (bi, 0, hp))
    half = pl.BlockSpec((None, l, 128), lambda bi, hp: (bi, 0, hp))
    colv = pl.BlockSpec((None, 2, l, 1), lambda bi, hp: (bi, hp, 0, 0))
    rowv = pl.BlockSpec((None, 2, 1, l), lambda bi, hp: (bi, hp, 0, 0))
    return pair, half, colv, rowv


def _fox_fwd(name, qkv, c_col, c_row):
    b, l, _ = qkv.shape
    nb = l // BLOCK

    def body(x_ref, cc_ref, cr_ref, o_ref, lse_ref):
        for i in range(nb):
            rows = slice(i * BLOCK, (i + 1) * BLOCK)
            kh = (i + 1) * BLOCK
            qblk = x_ref[rows, 0:128]
            kv = x_ref[0:kh, 128:256]
            vv = x_ref[0:kh, 256:384]
            qk = lax.dot_general(_both_heads(qblk) * SCALE, kv, _NT, preferred_element_type=F32)
            ps, dens = [], []
            for hh in range(2):
                s = _fox_mask(qk[hh * BLOCK:(hh + 1) * BLOCK], cr_ref[hh, :, 0:kh], i)
                m = jnp.max(s, axis=-1, keepdims=True)
                p = jnp.exp(s - m)
                den = jnp.sum(p, axis=-1, keepdims=True)
                ps.append(p.astype(BF16))
                dens.append(den)
                lse_ref[hh, rows, :] = (m + jnp.log(den)) + cc_ref[hh, rows, :]
            pv = lax.dot_general(jnp.concatenate(ps, axis=0), vv, _NN, preferred_element_type=F32)
            o_ref[rows, :] = jnp.where(_lane_lo(), pv[0:BLOCK] / dens[0], pv[BLOCK:2 * BLOCK] / dens[1]).astype(BF16)

    pair, half, colv, rowv = _fox_specs(l)
    return pl.pallas_call(
        body, name=name, grid=(b, 4), in_specs=[pair, colv, rowv], out_specs=[half, colv],
        out_shape=[jax.ShapeDtypeStruct((b, l, B_WIDTH), BF16), jax.ShapeDtypeStruct((b, B_HEADS, l, 1), F32)],
        compiler_params=_params(("parallel", "parallel")),
    )(qkv, c_col, c_row)


def _fox_bwd(name, qkv, c_col, c_row, o, lse, do):
    b, l, _ = qkv.shape
    nb = l // BLOCK

    def body(x_ref, cc_ref, cr_ref, o_ref, lse_ref, do_ref, dx_ref, dcq_ref, dck_ref, dk_acc, dv_acc):
        dk_acc[...] = jnp.zeros_like(dk_acc)
        dv_acc[...] = jnp.zeros_like(dv_acc)
        dck_ref[...] = jnp.zeros_like(dck_ref)
        for i in range(nb):
            rows = slice(i * BLOCK, (i + 1) * BLOCK)
            kh = (i + 1) * BLOCK
            qblk = x_ref[rows, 0:128]
            kv = x_ref[0:kh, 128:256]
            vv = x_ref[0:kh, 256:384]
            doblk = do_ref[rows, :]
            ov = o_ref[rows, :].astype(F32)
            q2 = _both_heads(qblk) * SCALE
            do2 = _both_heads(doblk)
            qk = lax.dot_general(q2, kv, _NT, preferred_element_type=F32)
            dp = lax.dot_general(do2, vv, _NT, preferred_element_type=F32)
            ps, dss = [], []
            for hh in range(2):
                half = slice(hh * BLOCK, (hh + 1) * BLOCK)
                s = _fox_mask(qk[half], cr_ref[hh, :, 0:kh], i)
                p = jnp.exp(s - (lse_ref[hh, rows, :] - cc_ref[hh, rows, :]))
                dsum = jnp.sum(do2[half].astype(F32) * ov, axis=-1, keepdims=True)
                ds = p * (dp[half] - dsum)
                ps.append(p.astype(BF16))
                dss.append(ds.astype(BF16))
                dcq_ref[hh, rows, :] = jnp.sum(ds, axis=-1, keepdims=True)
                dck_ref[hh, :, 0:kh] -= jnp.sum(ds, axis=0, keepdims=True)
            p2 = jnp.concatenate(ps, axis=0)
            ds2 = jnp.concatenate(dss, axis=0)
            dq = lax.dot_general(ds2, kv, _NN, preferred_element_type=F32) * SCALE
            dk_acc[0:kh, :] += lax.dot_general(ds2, q2, _TN, preferred_element_type=F32)
            dv_acc[0:kh, :] += lax.dot_general(p2, do2, _TN, preferred_element_type=F32)
            dx_ref[rows, 0:128] = jnp.where(_lane_lo(), dq[0:BLOCK], dq[BLOCK:2 * BLOCK]).astype(BF16)
        dx_ref[:, 128:256] = dk_acc[...].astype(BF16)
        dx_ref[:, 256:384] = dv_acc[...].astype(BF16)

    pair, half, colv, rowv = _fox_specs(l)
    return pl.pallas_call(
        body, name=name, grid=(b, 4), in_specs=[pair, colv, rowv, half, colv, half],
        out_specs=[pair, colv, rowv],
        out_shape=[jax.ShapeDtypeStruct(qkv.shape, BF16), jax.ShapeDtypeStruct((b, B_HEADS, l, 1), F32),
                   jax.ShapeDtypeStruct((b, B_HEADS, 1, l), F32)],
        scratch_shapes=[pltpu.VMEM((l, 128), F32), pltpu.VMEM((l, 128), F32)],
        compiler_params=_params(("parallel", "parallel")),
    )(qkv, c_col, c_row, o, lse, do)


_FLIPS = ((0, 0, 1), (0, 1, 0), (0, 1, 1), (1, 0, 0), (1, 0, 1), (1, 1, 0), (1, 1, 1))


def _peer_table():
    x, y, c = lax.axis_index("x"), lax.axis_index("y"), lax.axis_index("c")
    me = 4 * x + 2 * y + c
    peers = []
    for fx, fy, fc in _FLIPS:
        px = 1 - x if fx else x
        py = 1 - y if fy else y
        pc = 1 - c if fc else c
        peers.append(((px, py, pc), 4 * px + 2 * py + pc))
    return me, peers


_HBM = pl.BlockSpec(memory_space=pltpu.HBM)
_SEM = pl.BlockSpec(memory_space=pltpu.SEMAPHORE)
_ANY = pl.BlockSpec(memory_space=pl.ANY)
_EFFECT = pltpu.SideEffectType.DATAFLOW_SIDE_EFFECTING


def _split_copy(srcs_are_pieces, src_refs, land_refs, send_sem, recv_sem, a, kk, me, peers, arriving):
    dev, lin = peers[kk]
    npeer = len(_FLIPS)
    src = src_refs[a] if srcs_are_pieces[a] else src_refs[a].at[lin]
    dst = land_refs[a].at[lin] if arriving else land_refs[a].at[me]
    return pltpu.make_async_remote_copy(src_ref=src, dst_ref=dst, send_sem=send_sem.at[a * npeer + kk],
                                        recv_sem=recv_sem.at[a * npeer + kk], device_id=dev, device_id_type=MESH_ID)


def _xchg_start(name, gather, scatter, after=None):
    me_out = 4 * lax.axis_index("x") + 2 * lax.axis_index("y") + lax.axis_index("c")
    srcs = list(gather) + list(scatter)
    is_piece = [True] * len(gather) + [False] * len(scatter)
    lands = []
    for a, piece in zip(srcs, is_piece):
        own = a[None] if piece else lax.dynamic_slice_in_dim(a, me_out, 1, axis=0)
        shape = ((N_DEV,) + tuple(a.shape)) if piece else tuple(a.shape)
        start = (me_out,) + (0,) * (len(shape) - 1)
        lands.append(lax.dynamic_update_slice(lax.empty(shape, a.dtype), own, start))
    n = len(srcs)
    nsem = n * len(_FLIPS)
    has_after = after is not None

    def body(*refs):
        src_refs = refs[:n]
        land_refs = refs[n:2 * n]
        outs = refs[2 * n + (1 if has_after else 0):]
        send_sem, recv_sem = outs[0], outs[1]
        token = outs[-1]
        me, peers = _peer_table()
        for kk in range(len(_FLIPS)):
            for a in range(n):
                _split_copy(is_piece, src_refs, land_refs, send_sem, recv_sem, a, kk, me, peers, False).start()
        token[...] = jnp.zeros_like(token)

    out_shape = ([pltpu.SemaphoreType.DMA((nsem,)), pltpu.SemaphoreType.DMA((nsem,))]
                 + [pltpu.HBM(tuple(a.shape), a.dtype) for a in srcs] + [pltpu.HBM(tuple(a.shape), a.dtype) for a in lands]
                 + [jax.ShapeDtypeStruct((8, 128), F32)])
    args = [pltpu.with_memory_space_constraint(a, pltpu.HBM) for a in srcs + lands] + ([after] if has_after else [])
    res = pl.pallas_call(
        body, name=name, out_shape=out_shape,
        in_specs=[_HBM] * (2 * n) + ([_ANY] if has_after else []),
        out_specs=[_SEM, _SEM] + [_HBM] * (2 * n) + [pl.BlockSpec(memory_space=pltpu.VMEM)],
        input_output_aliases={i: 2 + i for i in range(2 * n)},
        compiler_params=pltpu.CompilerParams(has_side_effects=_EFFECT),
    )(*args)
    state = (res[0], res[1], list(res[2:2 + n]), list(res[2 + n:2 + 2 * n]), is_piece)
    return state, res[-1]


def _xchg_wait(name, state, after):
    send_sem, recv_sem, srcs, lands, is_piece = state
    n = len(srcs)

    def body(*refs):
        src_refs = refs[:n]
        land_refs = refs[n:2 * n]
        s_sem, r_sem = refs[2 * n], refs[2 * n + 1]
        me, peers = _peer_table()
        for kk in range(len(_FLIPS)):
            for a in range(n):
                cp = _split_copy(is_piece, src_refs, land_refs, s_sem, r_sem, a, kk, me, peers, True)
                cp.wait_send()
                cp.wait_recv()

    out_shape = [pltpu.HBM(tuple(a.shape), a.dtype) for a in srcs] + [pltpu.HBM(tuple(a.shape), a.dtype) for a in lands]
    res = pl.pallas_call(
        body, name=name, out_shape=out_shape,
        in_specs=[_HBM] * (2 * n) + [_SEM, _SEM, _ANY], out_specs=[_HBM] * (2 * n),
        input_output_aliases={i: i for i in range(2 * n)},
        compiler_params=pltpu.CompilerParams(has_side_effects=_EFFECT),
    )(*srcs, *lands, send_sem, recv_sem, after)
    return list(res[n:])


_SIB = (0, 0, 1)
_ICI = ((0, 1, 0), (1, 0, 0), (1, 1, 0))


def _flip(fl):
    x, y, c = lax.axis_index("x"), lax.axis_index("y"), lax.axis_index("c")
    px = 1 - x if fl[0] else x
    py = 1 - y if fl[1] else y
    pc = 1 - c if fl[2] else c
    return (px, py, pc), 4 * px + 2 * py + pc


def _gather2_start(name, pieces, after=None):
    me_out = 4 * lax.axis_index("x") + 2 * lax.axis_index("y") + lax.axis_index("c")
    pieces = list(pieces)
    n = len(pieces)
    lands = [lax.dynamic_update_slice(lax.empty((N_DEV,) + tuple(a.shape), a.dtype), a[None],
                                      (me_out,) + (0,) * a.ndim) for a in pieces]
    first = (_SIB,) + _ICI
    has_after = after is not None

    def body(*refs):
        src_refs, land_refs = refs[:n], refs[n:2 * n]
        outs = refs[2 * n + (1 if has_after else 0):]
        send_sem, recv_sem, token = outs[0], outs[1], outs[-1]
        _, me = _flip((0, 0, 0))
        for kk, fl in enumerate(first):
            dev, _ = _flip(fl)
            for a in range(n):
                pltpu.make_async_remote_copy(src_ref=src_refs[a], dst_ref=land_refs[a].at[me],
                                             send_sem=send_sem.at[a * 4 + kk], recv_sem=recv_sem.at[a * 4 + kk],
                                             device_id=dev, device_id_type=MESH_ID).start()
        token[...] = jnp.zeros_like(token)

    hbm = [pltpu.HBM(tuple(a.shape), a.dtype) for a in pieces + lands]
    res = pl.pallas_call(
        body, name=name,
        out_shape=[pltpu.SemaphoreType.DMA((4 * n,)), pltpu.SemaphoreType.DMA((4 * n,))] + hbm
        + [jax.ShapeDtypeStruct((8, 128), F32)],
        in_specs=[_HBM] * (2 * n) + ([_ANY] if has_after else []),
        out_specs=[_SEM, _SEM] + [_HBM] * (2 * n) + [pl.BlockSpec(memory_space=pltpu.VMEM)],
        input_output_aliases={i: 2 + i for i in range(2 * n)},
        compiler_params=pltpu.CompilerParams(has_side_effects=_EFFECT),
    )(*([pltpu.with_memory_space_constraint(a, pltpu.HBM) for a in pieces + lands] + ([after] if has_after else [])))
    return (res[0], res[1], list(res[2:2 + n]), list(res[2 + n:2 + 2 * n])), res[-1]


def _gather2_forward(name, state, after):
    send_a, recv_a, pieces, lands = state
    n = len(pieces)
    first = (_SIB,) + _ICI

    def body(*refs):
        src_refs, land_refs = refs[:n], refs[n:2 * n]
        s_a, r_a = refs[2 * n], refs[2 * n + 1]
        outs = refs[2 * n + 3:]
        send_b, recv_b, token = outs[0], outs[1], outs[-1]
        for kk, fl in enumerate(first):
            dev, lin = _flip(fl)
            for a in range(n):
                cp = pltpu.make_async_remote_copy(src_ref=src_refs[a], dst_ref=land_refs[a].at[lin],
                                                  send_sem=s_a.at[a * 4 + kk], recv_sem=r_a.at[a * 4 + kk],
                                                  device_id=dev, device_id_type=MESH_ID)
                cp.wait_send()
                cp.wait_recv()
        sib, _ = _flip(_SIB)
        for j, fl in enumerate(_ICI):
            _, lin = _flip(fl)
            for a in range(n):
                pltpu.make_async_remote_copy(src_ref=land_refs[a].at[lin], dst_ref=land_refs[a].at[lin],
                                             send_sem=send_b.at[a * 3 + j], recv_sem=recv_b.at[a * 3 + j],
                                             device_id=sib, device_id_type=MESH_ID).start()
        token[...] = jnp.zeros_like(token)

    hbm = [pltpu.HBM(tuple(a.shape), a.dtype) for a in pieces + lands]
    res = pl.pallas_call(
        body, name=name,
        out_shape=[pltpu.SemaphoreType.DMA((3 * n,)), pltpu.SemaphoreType.DMA((3 * n,))] + hbm
        + [jax.ShapeDtypeStruct((8, 128), F32)],
        in_specs=[_HBM] * (2 * n) + [_SEM, _SEM, _ANY],
        out_specs=[_SEM, _SEM] + [_HBM] * (2 * n) + [pl.BlockSpec(memory_space=pltpu.VMEM)],
        input_output_aliases={i: 2 + i for i in range(2 * n)},
        compiler_params=pltpu.CompilerParams(has_side_effects=_EFFECT),
    )(*pieces, *lands, send_a, recv_a, after)
    return (res[0], res[1], list(res[2 + n:2 + 2 * n])), res[-1]


def _gather2_wait(name, state, after):
    send_b, recv_b, lands = state
    n = len(lands)

    def body(*refs):
        land_refs = refs[:n]
        s_b, r_b = refs[n], refs[n + 1]
        sib, _ = _flip(_SIB)
        for j, fl in enumerate(_ICI):
            _, sent = _flip(fl)
            _, arriving = _flip((fl[0], fl[1], 1))
            for a in range(n):
                cp = pltpu.make_async_remote_copy(src_ref=land_refs[a].at[sent], dst_ref=land_refs[a].at[arriving],
                                                  send_sem=s_b.at[a * 3 + j], recv_sem=r_b.at[a * 3 + j],
                                                  device_id=sib, device_id_type=MESH_ID)
                cp.wait_send()
                cp.wait_recv()

    res = pl.pallas_call(
        body, name=name, out_shape=[pltpu.HBM(tuple(a.shape), a.dtype) for a in lands],
        in_specs=[_HBM] * n + [_SEM, _SEM, _ANY], out_specs=[_HBM] * n,
        input_output_aliases={i: i for i in range(n)},
        compiler_params=pltpu.CompilerParams(has_side_effects=_EFFECT),
    )(*lands, send_b, recv_b, after)
    return list(res)


def _adam_math(w, g, m, v):
    m = ADAM_B1 * m + (1.0 - ADAM_B1) * g
    v = ADAM_B2 * v + (1.0 - ADAM_B2) * (g * g)
    m_hat = m / (1.0 - ADAM_B1 ** ADAM_STEP)
    v_hat = v / (1.0 - ADAM_B2 ** ADAM_STEP)
    delta = -ADAM_LR * (m_hat / (jnp.sqrt(v_hat) + ADAM_EPS) + ADAM_WD * w)
    return delta, m, v


def _adam(name, w, m, v, parts, transposed=False, dep=None):
    npart, _, cp = parts.shape
    has_dep = dep is not None
    if transposed:
        c, r = w.shape
        tr = _pick(r, (256, 128))
        blk = pl.BlockSpec((c, tr), lambda i: (0, i))
    else:
        r, c = w.shape
        tr = _pick(r, (256, 176, 128, 64, 16, 8, 1))
        blk = pl.BlockSpec((tr, c), lambda i: (i, 0))

    def body(w_ref, m_ref, v_ref, p_ref, *rest):
        g_ref, d_ref, mo_ref, vo_ref = rest[-4:]
        g = p_ref[0].astype(F32)
        for pp in range(1, npart):
            g = g + p_ref[pp].astype(F32)
        g = g.T[0:c, :] if transposed else g[:, 0:c]
        delta, mn, vn = _adam_math(w_ref[...], g, m_ref[...], v_ref[...])
        g_ref[...] = g
        d_ref[...] = delta
        mo_ref[...] = mn
        vo_ref[...] = vn

    out = jax.ShapeDtypeStruct(w.shape, F32)
    return pl.pallas_call(
        body, name=name, grid=(r // tr,),
        in_specs=[blk, blk, blk, pl.BlockSpec((npart, tr, cp), lambda i: (0, i, 0))]
        + ([pl.BlockSpec(memory_space=pl.ANY)] if has_dep else []),
        out_specs=[blk, blk, blk, blk], out_shape=[out, out, out, out], compiler_params=_params(("parallel",)),
    )(*((w, m, v, parts) + ((dep,) if has_dep else ())))


def _small_sum(name, packs):
    def body(p_ref, o_ref):
        tot = p_ref[0]
        for pp in range(1, N_DEV):
            tot = tot + p_ref[pp]
        o_ref[0:8, :] = tot[0:8, :]
        o_ref[8:24, :] = tot[8:24, :] + tot[24:40, :]

    return pl.pallas_call(body, name=name, out_shape=jax.ShapeDtypeStruct((24, D_MODEL), F32),
                          compiler_params=_params())(packs)


def _local_step(x, tgt, g1, gm, g2, gf, b_forget, sinks, weights, send):
    b, s, _ = x.shape
    l = s + PREFIX
    t = b * l
    (meta,) = weights("meta", x)

    h0, n1 = _embed_norm("embed_rms1_fwd", x, meta, g1)
    (w1i,) = weights("ffn1_in", n1)
    gu1, a1 = _ffn_in_fwd("ffn1_in_fwd", n1, w1i)
    (w1o,) = weights("ffn1_out", weights("ffn1_out:forward", a1))
    h1, um = _mm_res_norm("ffn1_out_fwd", a1, w1o, h0, gm, alpha=0.5)
    wi, wa, wb, wo = weights("mix", weights("mix:forward", um))
    qkv, gates, f2 = _proj_fwd("proj_fwd", um, wi)
    qkv3 = qkv.reshape(b, l, QKV_W)
    f3 = f2.reshape(b, l, 128)
    bf_row = jnp.pad(b_forget, ((0, 0), (0, 128 - B_HEADS)))
    c_col, c_row = _fgate_fwd("fgate_fwd", f3, bf_row)
    head_of_row = jnp.arange(STACK) // BLOCK
    slopes = jnp.exp2(-8.0 * (head_of_row + 1).astype(F32) / A_HEADS).reshape(STACK, 1)
    sink_rows = jnp.repeat(sinks.reshape(A_HEADS), BLOCK).reshape(STACK, 1)
    swa_bias = _swa_bias(slopes)
    oa3, lse_a = _swa_fwd("swa_fwd", qkv3, slopes, sink_rows, swa_bias)
    ob3, lse_b = _fox_fwd("fox_fwd", qkv3, c_col, c_row)
    oa = oa3.reshape(t, A_WIDTH)
    ob = ob3.reshape(t, B_WIDTH)
    mixed, ya, yb = _branch_gate_fwd("branch_gate_fwd", oa, ob, wa, wb, gates, dep=weights("ffn2:forward", ob))
    h2, n2 = _mm_res_norm("mix_out_fwd", mixed, wo, h1, g2)
    w2i, w2o = weights("ffn2", h2)
    gu2, a2 = _ffn_in_fwd("ffn2_in_fwd", n2, w2i)

    dh3, loss_blk, dgf = _ffn_out_loss("ffn2_out_loss", a2, w2o, h2, gf, tgt, 0.5)

    def ffn_bwd(tag, dh, h_in, g_norm, n_in, gu, a, w_in_blk, w_out, one_send, examples=None):
        dw_out = _mm_tn(tag + "_out_bwd_w", a, dh, alpha=0.5)
        dgu = _ffn_out_bwd_x(tag + "_out_bwd_x", dh, w_out, gu, dep=None if one_send else send(tag + "_out", (dw_out,)))
        dw_in = _ffn_in_bwd_w(tag + "_in_bwd_w", n_in, dgu)
        token = send(tag, (dw_in, dw_out)) if one_send else send(tag + "_in", (dw_in,))
        return _ffn_in_bwd_x(tag + "_in_bwd_x", dgu, w_in_blk, h_in, g_norm, dh, dep=token, examples=examples)

    dh2, dg2 = ffn_bwd("ffn2", dh3, h2, g2, n2, gu2, a2, w2i, w2o, True)

    dwo = _mm_tn("mix_out_bwd_w", mixed, dh2)
    dya, dyb, dgates = _mix_out_gate_bwd("mix_out_gate_bwd", dh2, wo, gates, ya, yb)
    doa, dob = _branch_bwd_x("branch_bwd_x", dya, dyb, wa, wb)
    dwa, dwb = _branch_bwd_w("branch_bwd_w", oa, ob, dya, dyb)
    dqkv3, dcq, dck = _fox_bwd("fox_bwd", qkv3, c_col, c_row, ob3, lse_b, dob.reshape(b, l, B_WIDTH))
    dqkv3, dsink = _swa_bwd("swa_bwd", qkv3, oa3, lse_a, doa.reshape(b, l, A_WIDTH), slopes, sink_rows, swa_bias, dqkv3)
    dqkv = dqkv3.reshape(t, QKV_W)
    df3, dbf = _fgate_bwd("fgate_bwd", f3, bf_row, dcq, dck)
    df = df3.reshape(t, 128)
    dwi_qkv = _mm_tn("proj_qkv_bwd_w", um, dqkv, tm_c=(512,), tn_c=(768,))
    dwi_g = _mm_tn("proj_gates_bwd_w", um, dgates, tm_c=(512,), tn_c=(512,))
    dwi_f = _mm_tn("proj_f_bwd_w", um, df, tm_c=(512,), tn_c=(128,))
    token = send("mix", (dwi_qkv, dwi_g, dwi_f, dwa, dwb, dwo))
    dh1, dgm = _proj_bwd_x("proj_bwd_x", dqkv, dgates, df, wi, h1, gm, dh2, dep=token)

    grad_x, dmeta3, dg1 = ffn_bwd("ffn1", dh1, h0, g1, n1, gu1, a1, w1i, w1o, False, examples=b)
    dmeta = dmeta3.reshape(b * N_META, D_MODEL)

    misc = jnp.concatenate([dbf[:, 0:B_HEADS], dsink[:, 0].reshape(1, A_HEADS), loss_blk[0:1, 0:1]], axis=1)
    misc = jnp.pad(misc, ((0, 0), (0, D_MODEL - misc.shape[1])))
    row = lax.broadcasted_iota(jnp.int32, (8, D_MODEL), 0)
    vec = jnp.zeros((8, D_MODEL), F32)
    for i, piece in enumerate((dg1, dgm, dg2, dgf, misc)):
        vec = jnp.where(row == i, piece, vec)
    small = jnp.concatenate([vec, dmeta], axis=0)
    return grad_x, small


def _pad_to(a, rows, cols):
    return jnp.pad(a, ((0, rows - a.shape[0]), (0, cols - a.shape[1])))


def _ffn_out_from_gathered(g):
    w = g.reshape(4, FF_SHARD, D_MODEL)
    return jnp.pad(w, ((0, 0), (0, FF_SHARD_P - FF_SHARD), (0, 0))).reshape(D_FF_P, D_MODEL)


def _ffn_out_to_scatter(dw):
    return dw.reshape(4, FF_SHARD_P, D_MODEL)[:, 0:FF_SHARD, :].reshape(N_DEV, FFO_SHARD, D_MODEL)


def _proj_segments():
    segs = [(HEAD_DIM * h, HEAD_DIM, 0, B_SEG + HEAD_DIM * A_HEAD_ORDER.index(h)) for h in range(A_HEADS)]
    segs += [(512, 128, 0, B_SEG + A_WIDTH), (640, 128, 0, B_SEG + A_WIDTH + 128)]
    for first, off in ((768, 0), (1280, 128), (1792, 256)):
        segs += [(first + 128 * hp, 128, 0, PAIR_W * hp + off) for hp in range(4)]
    segs += [(2304, B_HEADS, 2, 0), (2312, 2 * D_MODEL, 1, 0)]
    return segs


def _proj_from_gathered(g):
    def cols(first, width):
        out = []
        for p in range(N_DEV):
            lo, hi = max(first, WIN_SHARD * p), min(first + width, WIN_SHARD * (p + 1))
            if lo < hi:
                out.append(g[p, :, lo - WIN_SHARD * p:hi - WIN_SHARD * p])
        return out

    parts = []
    for arr in (0, 1, 2):
        for first, width, _, _ in sorted((s for s in _proj_segments() if s[2] == arr), key=lambda s: s[3]):
            parts += cols(first, width)
        if arr == 0:
            parts.append(jnp.zeros((D_MODEL, P_GATES - QKV_W), g.dtype))
    parts.append(jnp.zeros((D_MODEL, 128 - B_HEADS), g.dtype))
    return jnp.concatenate(parts, axis=1)


def _proj_to_scatter(dqkv_w, dg_w, df_w):
    arrays = (dqkv_w, dg_w, df_w)
    segs = sorted(_proj_segments())
    blocks = []
    for p in range(N_DEV):
        parts = []
        for first, width, arr, at in segs:
            lo, hi = max(first, WIN_SHARD * p), min(first + width, WIN_SHARD * (p + 1))
            if lo < hi:
                parts.append(arrays[arr][:, at + lo - first:at + hi - first])
        parts.append(jnp.zeros((D_MODEL, WIN_SHARD_P - WIN_SHARD), dqkv_w.dtype))
        blocks.append(jnp.concatenate(parts, axis=1))
    return jnp.stack(blocks, axis=0)


def _a_rows_from_natural(w):
    return jnp.concatenate([w[HEAD_DIM * h:HEAD_DIM * (h + 1)] for h in A_HEAD_ORDER], axis=0)


def _a_rows_to_natural(w):
    return jnp.concatenate([w[HEAD_DIM * A_HEAD_ORDER.index(h):HEAD_DIM * (A_HEAD_ORDER.index(h) + 1)]
                            for h in range(A_HEADS)], axis=0)


def kernel(x, meta_tokens, ffn1_norm, ffn1_w_in, ffn1_w_out, mix_norm, w_in, b_forget, attn_sinks, w_branch_a, w_branch_b, w_out, ffn2_norm, ffn2_w_in, ffn2_w_out, final_norm, loss_target, m_meta_tokens, m_ffn1_norm, m_ffn1_w_in, m_ffn1_w_out, m_mix_norm, m_w_in, m_b_forget, m_attn_sinks, m_w_branch_a, m_w_branch_b, m_w_out, m_ffn2_norm, m_ffn2_w_in, m_ffn2_w_out, m_final_norm, v_meta_tokens, v_ffn1_norm, v_ffn1_w_in, v_ffn1_w_out, v_mix_norm, v_w_in, v_b_forget, v_attn_sinks, v_w_branch_a, v_w_branch_b, v_w_out, v_ffn2_norm, v_ffn2_w_in, v_ffn2_w_out, v_final_norm):
    me = 4 * lax.axis_index("x") + 2 * lax.axis_index("y") + lax.axis_index("c")

    shards = (
        _pad_to(ffn1_w_in[0].astype(BF16), D_MODEL, FF_SHARD_P),
        ffn1_w_out[0].astype(BF16),
        _pad_to(w_in[0].astype(BF16), D_MODEL, WIN_SHARD_P),
        w_branch_a[0].astype(BF16), w_branch_b[0].astype(BF16), w_out[0].astype(BF16),
        _pad_to(ffn2_w_in[0].astype(BF16), D_MODEL, FF_SHARD_P),
        ffn2_w_out[0].astype(BF16),
        meta_tokens,
    )
    s1i, s1o, swi, swa, swb, swo, s2i, s2o, smeta = shards
    first_level, second_level = {}, {}
    first_level["meta"], tok = _gather2_start("gather_meta_start", (smeta,))
    first_level["ffn1_in"], tok = _gather2_start("gather_ffn1_in_start", (s1i,), after=tok)
    first_level["ffn1_out"], tok = _gather2_start("gather_ffn1_out_start", (s1o,), after=tok)
    first_level["mix"], tok = _gather2_start("gather_mix_start", (swi, swa, swb, swo), after=tok)
    first_level["ffn2"], tok = _gather2_start("gather_ffn2_start", (s2i, s2o), after=tok)
    started = {"tok": tok}

    def weights(group, after):
        if group.endswith(":forward"):
            group = group[:-len(":forward")]
            second_level[group], token = _gather2_forward("gather_" + group + "_forward", first_level[group], after)
            return token
        if group == "meta":
            after = weights("meta:forward", started["tok"])
        if group == "ffn1_in":
            after = weights("ffn1_in:forward", after)
        got = _gather2_wait("gather_" + group + "_wait", second_level[group], after)
        if group == "mix":
            gwi, gwa, gwb, gwo = got
            return (_proj_from_gathered(gwi), _a_rows_from_natural(gwa.transpose(1, 0, 2).reshape(A_WIDTH, D_MODEL)),
                    gwb.transpose(1, 0, 2).reshape(B_WIDTH, D_MODEL), gwo.reshape(D_MODEL, D_MODEL))
        if group == "ffn1_out":
            return (_ffn_out_from_gathered(got[0]),)
        if group == "meta":
            return (got[0].transpose(1, 0, 2).reshape(N_META, D_MODEL),)
        if group == "ffn1_in":
            return (got[0].reshape(2, 4, D_MODEL, FF_SHARD_P),)
        return got[0].reshape(2, 4, D_MODEL, FF_SHARD_P), _ffn_out_from_gathered(got[1])

    scatter_state = {}

    def send(group, grads):
        if group == "mix":
            dwi_qkv, dwi_g, dwi_f, dwa, dwb, dwo = grads
            dwa = _a_rows_to_natural(dwa)
            blocks = (_proj_to_scatter(dwi_qkv, dwi_g, dwi_f), dwa.reshape(A_WIDTH, N_DEV, 128).transpose(1, 0, 2),
                      dwb.reshape(B_WIDTH, N_DEV, 128).transpose(1, 0, 2), dwo.reshape(N_DEV, 128, D_MODEL))
        elif group.endswith("_in"):
            blocks = (grads[0].reshape(N_DEV, D_MODEL, FF_SHARD_P),)
        elif group.endswith("_out"):
            blocks = (_ffn_out_to_scatter(grads[0]),)
        else:
            blocks = (grads[0].reshape(N_DEV, D_MODEL, FF_SHARD_P), _ffn_out_to_scatter(grads[1]))
        scatter_state[group], token = _xchg_start("scatter_" + group + "_start", (), blocks)
        return token

    gf = final_norm.reshape(1, D_MODEL)
    grad_x, small = _local_step(x, loss_target, ffn1_norm, mix_norm, ffn2_norm, gf, b_forget, attn_sinks, weights, send)

    small_state, after = _xchg_start("gather_small_start", (small,), ())
    out = {}
    updates = (
        ("ffn2", (("ffn2_w_in", ffn2_w_in, m_ffn2_w_in, v_ffn2_w_in), ("ffn2_w_out", ffn2_w_out, m_ffn2_w_out, v_ffn2_w_out))),
        ("ffn1_out", (("ffn1_w_out", ffn1_w_out, m_ffn1_w_out, v_ffn1_w_out),)),
        ("mix", (("w_in", w_in, m_w_in, v_w_in), ("w_branch_a", w_branch_a, m_w_branch_a, v_w_branch_a),
                 ("w_branch_b", w_branch_b, m_w_branch_b, v_w_branch_b), ("w_out", w_out, m_w_out, v_w_out))),
    )
    last_update = ("ffn1_in", (("ffn1_w_in", ffn1_w_in, m_ffn1_w_in, v_ffn1_w_in),))

    def update(group, members, after):
        parts_list = _xchg_wait("scatter_" + group + "_wait", scatter_state[group], after)
        prev = None
        for (nm, w, m, v), parts in zip(members, parts_list):
            if nm.endswith("w_in"):
                res4 = _adam("adam_" + nm, w[0].T, m[0].T, v[0].T, parts, transposed=True, dep=prev)
                out[nm] = tuple(r.T[None] for r in res4)
            else:
                res4 = _adam("adam_" + nm, w[0], m[0], v[0], parts, dep=prev)
                out[nm] = tuple(r[None] for r in res4)
            prev = res4[0]
        return prev

    for group, members in updates + (last_update,):
        after = update(group, members, after)
    (packs,) = _xchg_wait("gather_small_wait", small_state, after)

    tot = _small_sum("small_sum", packs)
    loss = tot[4, 2 * B_HEADS]
    g_meta = lax.dynamic_slice(tot[8:24, :], (0, me * 128), (N_META, 128))
    out["meta_tokens"] = tuple(_adam("adam_meta_tokens", meta_tokens, m_meta_tokens, v_meta_tokens, g_meta[None]))

    def pack_small(n1, nm, n2, nf, bfv, skv):
        misc = jnp.pad(jnp.concatenate([bfv, skv], axis=1), ((0, 0), (0, D_MODEL - 2 * B_HEADS)))
        row = lax.broadcasted_iota(jnp.int32, (8, D_MODEL), 0)
        vec = jnp.zeros((8, D_MODEL), F32)
        for i, piece in enumerate((n1, nm, n2, nf.reshape(1, D_MODEL), misc)):
            vec = jnp.where(row == i, piece, vec)
        return vec

    w_pack = pack_small(ffn1_norm, mix_norm, ffn2_norm, final_norm, b_forget, attn_sinks)
    m_pack = pack_small(m_ffn1_norm, m_mix_norm, m_ffn2_norm, m_final_norm, m_b_forget, m_attn_sinks)
    v_pack = pack_small(v_ffn1_norm, v_mix_norm, v_ffn2_norm, v_final_norm, v_b_forget, v_attn_sinks)
    small4 = _adam("adam_small", w_pack, m_pack, v_pack, tot[0:8][None])
    for i, nm in enumerate(("ffn1_norm", "mix_norm", "ffn2_norm")):
        out[nm] = tuple(r[i:i + 1] for r in small4)
    out["final_norm"] = tuple(r[3] for r in small4)
    out["b_forget"] = tuple(r[4:5, 0:B_HEADS] for r in small4)
    out["attn_sinks"] = tuple(r[4:5, B_HEADS:2 * B_HEADS] for r in small4)

    names = ("meta_tokens", "ffn1_norm", "ffn1_w_in", "ffn1_w_out", "mix_norm", "w_in", "b_forget", "attn_sinks",
             "w_branch_a", "w_branch_b", "w_out", "ffn2_norm", "ffn2_w_in", "ffn2_w_out", "final_norm")
    return (loss, grad_x) + tuple(out[nm][kind] for kind in range(4) for nm in names)
```

```python
import jax
import jax.numpy as jnp
from jax import lax
from jax.experimental import pallas as pl
from jax.experimental.pallas import tpu as pltpu

F32 = jnp.float32
BF16 = jnp.bfloat16

D_MODEL = 1024
N_META = 16
BLOCK = 128
PREFIX = 128
N_PAD = PREFIX - N_META
HEAD_DIM = 64
A_HEADS = 8
B_HEADS = 8
A_WIDTH = 512
A_KV_WIDTH = 128
B_WIDTH = 512
D_FF = 2816
N_DEV = 8
FF_SHARD = 2 * D_FF // N_DEV
FF_SHARD_P = 768
FFO_SHARD = D_FF // N_DEV
D_FF_P = 4 * FF_SHARD_P
W_IN_COLS = 4360
WIN_SHARD = W_IN_COLS // N_DEV
WIN_SHARD_P = 640
PAIR_W = 3 * 128
B_SEG = 4 * PAIR_W
A_SEG = A_WIDTH + 2 * A_KV_WIDTH
QKV_W = B_SEG + A_SEG
P_GATES = 2 * (2 * D_MODEL)
P_F = P_GATES + 2 * D_MODEL
PROJ_P = P_F + 128
A_HEAD_ORDER = (0, 4, 1, 5, 2, 6, 3, 7)
EPS = 1e-6
NEG = -1e30
SCALE = HEAD_DIM ** -0.5
ADAM_LR = 0.001
ADAM_B1 = 0.9
ADAM_B2 = 0.999
ADAM_EPS = 1e-08
ADAM_WD = 0.01
ADAM_STEP = 10
VMEM_LIMIT = 56 * 1024 * 1024
MESH_ID = pl.DeviceIdType.MESH
SMALL_ROWS = 40

_NN = (((1,), (0,)), ((), ()))
_NT = (((1,), (1,)), ((), ()))
_TN = (((0,), (0,)), ((), ()))


def _params(sem=None):
    return pltpu.CompilerParams(dimension_semantics=sem, vmem_limit_bytes=VMEM_LIMIT)


def _pick(n, cands):
    for c in cands:
        if n % c == 0:
            return c
    raise ValueError(f"no tile for {n}")


def _bf(v):
    return v if v.dtype == BF16 else v.astype(BF16)


def _mm(name, a, b, dims, grid, a_spec, b_spec, o_spec, out_shape, out_dtype, alpha=1.0):
    def body(a_ref, b_ref, o_ref):
        acc = lax.dot_general(_bf(a_ref[...]), _bf(b_ref[...]), dims, preferred_element_type=F32)
        if alpha != 1.0:
            acc = acc * alpha
        o_ref[...] = acc.astype(o_ref.dtype)

    return pl.pallas_call(
        body, name=name, grid=grid, in_specs=[a_spec, b_spec], out_specs=o_spec,
        out_shape=jax.ShapeDtypeStruct(out_shape, out_dtype),
        compiler_params=_params(("parallel",) * len(grid)),
    )(a, b)


def _mm_res_norm(name, a, w, res, g_next, alpha=1.0):
    t, k = a.shape
    tm = _pick(t, (544, 384, 256, 128))

    def body(a_ref, w_ref, r_ref, g_ref, h_ref, n_ref):
        acc = lax.dot_general(_bf(a_ref[...]), w_ref[...], _NN, preferred_element_type=F32)
        if alpha != 1.0:
            acc = acc * alpha
        hv = acc + r_ref[...]
        h_ref[...] = hv
        r = lax.rsqrt(jnp.mean(hv * hv, axis=-1, keepdims=True) + EPS)
        n_ref[...] = ((hv * r) * g_ref[...]).astype(BF16)

    row = pl.BlockSpec((tm, D_MODEL), lambda i: (i, 0))
    return pl.pallas_call(
        body, name=name, grid=(t // tm,),
        in_specs=[pl.BlockSpec((tm, k), lambda i: (i, 0)), pl.BlockSpec((k, D_MODEL), lambda i: (0, 0)), row,
                  pl.BlockSpec((1, D_MODEL), lambda i: (0, 0))],
        out_specs=[row, row],
        out_shape=[jax.ShapeDtypeStruct((t, D_MODEL), F32), jax.ShapeDtypeStruct((t, D_MODEL), BF16)],
        compiler_params=_params(("parallel",)),
    )(a, w, res, g_next)


def _mm_tn(name, a, b, out_dtype=BF16, alpha=1.0, tm_c=(768, 512, 256, 128), tn_c=(512, 640, 256, 128)):
    t, m = a.shape
    n = b.shape[1]
    tm = _pick(m, tm_c)
    tn = _pick(n, tn_c)
    bytes_a, bytes_b = a.size * a.dtype.itemsize, b.size * b.dtype.itemsize
    if bytes_a + bytes_b * (m // tm) <= bytes_b + bytes_a * (n // tn):
        return _mm(name, a, b, _TN, (m // tm, n // tn),
                   pl.BlockSpec((t, tm), lambda i, j: (0, i)), pl.BlockSpec((t, tn), lambda i, j: (0, j)),
                   pl.BlockSpec((tm, tn), lambda i, j: (i, j)), (m, n), out_dtype, alpha=alpha)
    return _mm(name, a, b, _TN, (n // tn, m // tm),
               pl.BlockSpec((t, tm), lambda j, i: (0, i)), pl.BlockSpec((t, tn), lambda j, i: (0, j)),
               pl.BlockSpec((tm, tn), lambda j, i: (i, j)), (m, n), out_dtype, alpha=alpha)


def _ffn_in_fwd(name, n, wblk, dep=None):
    t = n.shape[0]
    tm = _pick(t, (1088, 768, 512, 256, 128))
    has_dep = dep is not None

    def body(n_ref, w_ref, *rest):
        gu_ref, a_ref = rest[-2], rest[-1]
        nv = n_ref[...]
        g = lax.dot_general(nv, w_ref[0], _NN, preferred_element_type=F32)
        u = lax.dot_general(nv, w_ref[1], _NN, preferred_element_type=F32)
        sg = jax.nn.sigmoid(g)
        silu = g * sg
        a_ref[...] = (silu * u).astype(BF16)
        gu_ref[0] = ((0.5 * u) * (sg + silu * (1.0 - sg))).astype(BF16)
        gu_ref[1] = (0.5 * silu).astype(BF16)

    return pl.pallas_call(
        body, name=name, grid=(t // tm, 4),
        in_specs=[pl.BlockSpec((tm, D_MODEL), lambda i, j: (i, 0)),
                  pl.BlockSpec((2, None, D_MODEL, FF_SHARD_P), lambda i, j: (0, j, 0, 0))]
        + ([pl.BlockSpec(memory_space=pl.ANY)] if has_dep else []),
        out_specs=[pl.BlockSpec((2, tm, FF_SHARD_P), lambda i, j: (0, i, j)),
                   pl.BlockSpec((tm, FF_SHARD_P), lambda i, j: (i, j))],
        out_shape=[jax.ShapeDtypeStruct((2, t, D_FF_P), BF16), jax.ShapeDtypeStruct((t, D_FF_P), BF16)],
        compiler_params=_params(("parallel", "parallel")),
    )(*((n, wblk) + ((dep,) if has_dep else ())))


def _ffn_out_bwd_x(name, dh, w_out, gu, dep=None):
    t = dh.shape[0]
    tm = _pick(t, (1088, 768, 512, 256, 128))
    has_dep = dep is not None

    def body(dh_ref, w_ref, gu_ref, *rest):
        o_ref = rest[-1]
        da = lax.dot_general(_bf(dh_ref[...]), w_ref[...], _NT, preferred_element_type=F32)
        o_ref[0] = (da * gu_ref[0].astype(F32)).astype(BF16)
        o_ref[1] = (da * gu_ref[1].astype(F32)).astype(BF16)

    gu_spec = pl.BlockSpec((2, tm, FF_SHARD_P), lambda i, j: (0, i, j))
    return pl.pallas_call(
        body, name=name, grid=(t // tm, 4),
        in_specs=[pl.BlockSpec((tm, D_MODEL), lambda i, j: (i, 0)), pl.BlockSpec((FF_SHARD_P, D_MODEL), lambda i, j: (j, 0)),
                  gu_spec] + ([pl.BlockSpec(memory_space=pl.ANY)] if has_dep else []),
        out_specs=gu_spec, out_shape=jax.ShapeDtypeStruct((2, t, D_FF_P), BF16),
        compiler_params=_params(("parallel", "parallel")),
    )(*((dh, w_out, gu) + ((dep,) if has_dep else ())))


def _rms_bwd_rows(dn, h, g, dres):
    r = lax.rsqrt(jnp.mean(h * h, axis=-1, keepdims=True) + EPS)
    tv = dn * g
    dot = jnp.mean(tv * h, axis=-1, keepdims=True)
    return dres + (r * tv - h * (r * r * r * dot)), jnp.sum(dn * (h * r), axis=0, keepdims=True)


def _accumulate_rows(ref, part, first):
    @pl.when(first)
    def _():
        ref[...] = part

    @pl.when(jnp.logical_not(first))
    def _():
        ref[...] += part


def _ffn_in_bwd_x(name, dgu, wblk, h_in, g_norm, dres, dep=None, examples=None):
    t = dgu.shape[1]
    tm = _pick(t, (544, 384, 256, 128))
    has_dep = dep is not None
    split = examples is not None
    if split:
        l = t // examples
        per = l // tm
        assert per * tm == l and tm > PREFIX

    def body(d_ref, w_ref, h_ref, g_ref, r_ref, *rest):
        acc = None
        for s in range(2):
            for j in range(4):
                part = lax.dot_general(d_ref[s, :, FF_SHARD_P * j:FF_SHARD_P * (j + 1)], w_ref[s, j], _NT,
                                       preferred_element_type=F32)
                acc = part if acc is None else acc + part
        dh, dg = _rms_bwd_rows(acc, h_ref[...], g_ref[...], r_ref[...])
        i = pl.program_id(0)
        if not split:
            dh_ref, dg_ref = rest[-2], rest[-1]
            dh_ref[...] = dh
        else:
            gx_ref, meta_ref, dg_ref, buf, sem = rest[-5:]

            def out_copy(step):
                bi, r = step // per, step % per
                head = pltpu.make_async_copy(buf.at[pl.ds(PREFIX, tm - PREFIX)], gx_ref.at[bi, pl.ds(0, tm - PREFIX)], sem)
                if per == 1:
                    return r == 0, head, None
                return r == 0, head, pltpu.make_async_copy(
                    buf, gx_ref.at[bi, pl.ds(pl.multiple_of(jnp.maximum(r, 1) * tm - PREFIX, 8), tm)], sem)

            def run(step, method):
                is_head, head, later = out_copy(step)

                @pl.when(is_head)
                def _():
                    getattr(head, method)()

                if later is not None:
                    @pl.when(jnp.logical_not(is_head))
                    def _():
                        getattr(later, method)()

            @pl.when(i > 0)
            def _():
                run(i - 1, "wait")

            buf[...] = dh

            @pl.when(i % per == 0)
            def _():
                meta_ref[...] = dh[N_PAD:PREFIX]

            run(i, "start")

            @pl.when(i == pl.num_programs(0) - 1)
            def _():
                run(i, "wait")

        _accumulate_rows(dg_ref, dg, i == 0)

    row = pl.BlockSpec((tm, D_MODEL), lambda i: (i, 0))
    vec = pl.BlockSpec((1, D_MODEL), lambda i: (0, 0))
    if split:
        out_specs = [pl.BlockSpec(memory_space=pl.ANY), pl.BlockSpec((None, N_META, D_MODEL), lambda i: (i // per, 0, 0)), vec]
        out_shape = [jax.ShapeDtypeStruct((examples, l - PREFIX, D_MODEL), F32),
                     jax.ShapeDtypeStruct((examples, N_META, D_MODEL), F32), jax.ShapeDtypeStruct((1, D_MODEL), F32)]
        scratch = [pltpu.VMEM((tm, D_MODEL), F32), pltpu.SemaphoreType.DMA(())]
    else:
        out_specs = [row, vec]
        out_shape = [jax.ShapeDtypeStruct((t, D_MODEL), F32), jax.ShapeDtypeStruct((1, D_MODEL), F32)]
        scratch = []
    return pl.pallas_call(
        body, name=name, grid=(t // tm,),
        in_specs=[pl.BlockSpec((2, tm, D_FF_P), lambda i: (0, i, 0)),
                  pl.BlockSpec((2, 4, D_MODEL, FF_SHARD_P), lambda i: (0, 0, 0, 0), pipeline_mode=pl.Buffered(1)),
                  row, vec, row]
        + ([pl.BlockSpec(memory_space=pl.ANY)] if has_dep else []),
        out_specs=out_specs, out_shape=out_shape, scratch_shapes=scratch,
        compiler_params=_params(("arbitrary",)),
    )(*((dgu, wblk, h_in, g_norm, dres) + ((dep,) if has_dep else ())))


def _proj_fwd(name, um, wi):
    t = um.shape[0]
    tm = _pick(t, (544, 384, 256, 128))

    def body(u_ref, w_ref, q_ref, g_ref, f_ref):
        uv = u_ref[...]
        q_ref[...] = lax.dot_general(uv, w_ref[:, 0:QKV_W], _NN, preferred_element_type=F32).astype(BF16)
        g_ref[...] = lax.dot_general(uv, w_ref[:, P_GATES:P_F], _NN, preferred_element_type=F32)
        f_ref[...] = lax.dot_general(uv, w_ref[:, P_F:PROJ_P], _NN, preferred_element_type=F32)

    return pl.pallas_call(
        body, name=name, grid=(t // tm,),
        in_specs=[pl.BlockSpec((tm, D_MODEL), lambda i: (i, 0)),
                  pl.BlockSpec((D_MODEL, PROJ_P), lambda i: (0, 0), pipeline_mode=pl.Buffered(1))],
        out_specs=[pl.BlockSpec((tm, QKV_W), lambda i: (i, 0)), pl.BlockSpec((tm, 2 * D_MODEL), lambda i: (i, 0)),
                   pl.BlockSpec((tm, 128), lambda i: (i, 0))],
        out_shape=[jax.ShapeDtypeStruct((t, QKV_W), BF16), jax.ShapeDtypeStruct((t, 2 * D_MODEL), F32),
                   jax.ShapeDtypeStruct((t, 128), F32)],
        compiler_params=_params(("parallel",)),
    )(um, wi)


def _proj_bwd_x(name, dqkv, dgates, df, wi, h_in, g_norm, dres, dep=None):
    t = dqkv.shape[0]
    tm = _pick(t, (544, 384, 256, 128))
    has_dep = dep is not None

    def body(q_ref, gt_ref, f_ref, w_ref, h_ref, g_ref, r_ref, *rest):
        dh_ref, dg_ref = rest[-2], rest[-1]
        acc = lax.dot_general(q_ref[...], w_ref[:, 0:QKV_W], _NT, preferred_element_type=F32)
        acc = acc + lax.dot_general(gt_ref[...], w_ref[:, P_GATES:P_F], _NT, preferred_element_type=F32)
        acc = acc + lax.dot_general(f_ref[...], w_ref[:, P_F:PROJ_P], _NT, preferred_element_type=F32)
        dh, dg = _rms_bwd_rows(acc, h_ref[...], g_ref[...], r_ref[...])
        dh_ref[...] = dh
        _accumulate_rows(dg_ref, dg, pl.program_id(0) == 0)

    row = pl.BlockSpec((tm, D_MODEL), lambda i: (i, 0))
    vec = pl.BlockSpec((1, D_MODEL), lambda i: (0, 0))
    return pl.pallas_call(
        body, name=name, grid=(t // tm,),
        in_specs=[pl.BlockSpec((tm, QKV_W), lambda i: (i, 0)), pl.BlockSpec((tm, 2 * D_MODEL), lambda i: (i, 0)),
                  pl.BlockSpec((tm, 128), lambda i: (i, 0)),
                  pl.BlockSpec((D_MODEL, PROJ_P), lambda i: (0, 0), pipeline_mode=pl.Buffered(1)), row, vec, row]
        + ([pl.BlockSpec(memory_space=pl.ANY)] if has_dep else []),
        out_specs=[row, vec],
        out_shape=[jax.ShapeDtypeStruct((t, D_MODEL), F32), jax.ShapeDtypeStruct((1, D_MODEL), F32)],
        compiler_params=_params(("arbitrary",)),
    )(*((dqkv, dgates, df, wi, h_in, g_norm, dres) + ((dep,) if has_dep else ())))


def _ffn_in_bwd_w(name, n, dgu):
    t = n.shape[0]
    return _mm(name, n, dgu, _TN, (2, 4),
               pl.BlockSpec((t, D_MODEL), lambda s, j: (0, 0)),
               pl.BlockSpec((None, t, FF_SHARD_P), lambda s, j: (s, 0, j)),
               pl.BlockSpec((None, None, D_MODEL, FF_SHARD_P), lambda s, j: (s, j, 0, 0)),
               (2, 4, D_MODEL, FF_SHARD_P), BF16)


def _embed_norm(name, x, meta, g):
    b, s, _ = x.shape
    half = (s + PREFIX) // 2
    first = half - PREFIX
    assert first > 0 and half % 16 == 0

    def body(x_ref, m_ref, g_ref, h_ref, n_ref, buf, sem):
        bi, k = pl.program_id(0), pl.program_id(1)

        @pl.when(k == 0)
        def _():
            buf[0:N_PAD, :] = jnp.zeros((N_PAD, D_MODEL), F32)
            buf[N_PAD:PREFIX, :] = m_ref[...]
            cp = pltpu.make_async_copy(x_ref.at[bi, pl.ds(0, first)], buf.at[pl.ds(PREFIX, first)], sem)
            cp.start()
            cp.wait()

        @pl.when(k == 1)
        def _():
            cp = pltpu.make_async_copy(x_ref.at[bi, pl.ds(first, half)], buf, sem)
            cp.start()
            cp.wait()

        hv = buf[...]
        h_ref[...] = hv
        r = lax.rsqrt(jnp.mean(hv * hv, axis=-1, keepdims=True) + EPS)
        n_ref[...] = ((hv * r) * g_ref[...]).astype(BF16)

    rows = pl.BlockSpec((half, D_MODEL), lambda bi, k: (2 * bi + k, 0))
    return pl.pallas_call(
        body, name=name, grid=(b, 2),
        in_specs=[pl.BlockSpec(memory_space=pl.ANY), pl.BlockSpec((N_META, D_MODEL), lambda bi, k: (0, 0)),
                  pl.BlockSpec((1, D_MODEL), lambda bi, k: (0, 0))],
        out_specs=[rows, rows],
        out_shape=[jax.ShapeDtypeStruct((2 * b * half, D_MODEL), F32), jax.ShapeDtypeStruct((2 * b * half, D_MODEL), BF16)],
        scratch_shapes=[pltpu.VMEM((half, D_MODEL), F32), pltpu.SemaphoreType.DMA(())],
        compiler_params=_params(("arbitrary", "arbitrary")),
    )(x, meta, g)


def _branch_gate_fwd(name, oa, ob, wa, wb, gates, dep=None):
    t = gates.shape[0]
    tm = _pick(t, (544, 384, 256, 128))
    has_dep = dep is not None

    def body(oa_ref, ob_ref, wa_ref, wb_ref, g_ref, *rest):
        o_ref, ya_ref, yb_ref = rest[-3:]
        ya = lax.dot_general(_bf(oa_ref[...]), wa_ref[...], _NN, preferred_element_type=F32)
        yb = lax.dot_general(_bf(ob_ref[...]), wb_ref[...], _NN, preferred_element_type=F32)
        sa = jax.nn.sigmoid(g_ref[:, 0:D_MODEL])
        sb = jax.nn.sigmoid(g_ref[:, D_MODEL:2 * D_MODEL])
        o_ref[...] = (sa * ya + sb * yb).astype(BF16)
        ya_ref[...] = ya.astype(BF16)
        yb_ref[...] = yb.astype(BF16)

    blk = pl.BlockSpec((tm, D_MODEL), lambda i: (i, 0))
    narrow = pl.BlockSpec((tm, A_WIDTH), lambda i: (i, 0))
    wide = pl.BlockSpec((tm, 2 * D_MODEL), lambda i: (i, 0))
    wspec = pl.BlockSpec((A_WIDTH, D_MODEL), lambda i: (0, 0))
    out = jax.ShapeDtypeStruct((t, D_MODEL), BF16)
    return pl.pallas_call(
        body, name=name, grid=(t // tm,),
        in_specs=[narrow, narrow, wspec, wspec, wide] + ([pl.BlockSpec(memory_space=pl.ANY)] if has_dep else []),
        out_specs=[blk, blk, blk], out_shape=[out, out, out], compiler_params=_params(("parallel",)),
    )(*((oa, ob, wa, wb, gates) + ((dep,) if has_dep else ())))


def _branch_bwd_x(name, dya, dyb, wa, wb):
    t = dya.shape[0]
    tm = _pick(t, (1088, 768, 512, 256, 128))

    def body(da_ref, db_ref, wa_ref, wb_ref, oa_ref, ob_ref):
        oa_ref[...] = lax.dot_general(da_ref[...], wa_ref[...], _NT, preferred_element_type=F32)
        ob_ref[...] = lax.dot_general(db_ref[...], wb_ref[...], _NT, preferred_element_type=F32).astype(BF16)

    blk = pl.BlockSpec((tm, D_MODEL), lambda i: (i, 0))
    narrow = pl.BlockSpec((tm, A_WIDTH), lambda i: (i, 0))
    wspec = pl.BlockSpec((A_WIDTH, D_MODEL), lambda i: (0, 0), pipeline_mode=pl.Buffered(1))
    return pl.pallas_call(
        body, name=name, grid=(t // tm,), in_specs=[blk, blk, wspec, wspec], out_specs=[narrow, narrow],
        out_shape=[jax.ShapeDtypeStruct((t, A_WIDTH), F32), jax.ShapeDtypeStruct((t, B_WIDTH), BF16)],
        compiler_params=_params(("parallel",)),
    )(dya, dyb, wa, wb)


def _branch_bwd_w(name, oa, ob, dya, dyb):
    t = oa.shape[0]
    tn = 512

    def body(oa_ref, ob_ref, da_ref, db_ref, wa_ref, wb_ref):
        wa_ref[...] = lax.dot_general(_bf(oa_ref[...]), da_ref[...], _TN, preferred_element_type=F32).astype(BF16)
        wb_ref[...] = lax.dot_general(_bf(ob_ref[...]), db_ref[...], _TN, preferred_element_type=F32).astype(BF16)

    whole = pl.BlockSpec((t, A_WIDTH), lambda j: (0, 0), pipeline_mode=pl.Buffered(1))
    cols = pl.BlockSpec((t, tn), lambda j: (0, j))
    out_spec = pl.BlockSpec((A_WIDTH, tn), lambda j: (0, j))
    out = jax.ShapeDtypeStruct((A_WIDTH, D_MODEL), BF16)
    return pl.pallas_call(
        body, name=name, grid=(D_MODEL // tn,), in_specs=[whole, whole, cols, cols], out_specs=[out_spec, out_spec],
        out_shape=[out, out], compiler_params=_params(("parallel",)),
    )(oa, ob, dya, dyb)


def _mix_out_gate_bwd(name, dh, wo, gates, ya, yb):
    t = gates.shape[0]
    tm = _pick(t, (544, 384, 256, 128))

    def body(dh_ref, w_ref, g_ref, ya_ref, yb_ref, dya_ref, dyb_ref, dg_ref):
        dm = lax.dot_general(_bf(dh_ref[...]), w_ref[...], _NT, preferred_element_type=F32)
        sa = jax.nn.sigmoid(g_ref[:, 0:D_MODEL])
        sb = jax.nn.sigmoid(g_ref[:, D_MODEL:2 * D_MODEL])
        dya_ref[...] = (dm * sa).astype(BF16)
        dyb_ref[...] = (dm * sb).astype(BF16)
        dg_ref[:, 0:D_MODEL] = (dm * ya_ref[...].astype(F32) * (sa * (1.0 - sa))).astype(BF16)
        dg_ref[:, D_MODEL:2 * D_MODEL] = (dm * yb_ref[...].astype(F32) * (sb * (1.0 - sb))).astype(BF16)

    blk = pl.BlockSpec((tm, D_MODEL), lambda i: (i, 0))
    wide = pl.BlockSpec((tm, 2 * D_MODEL), lambda i: (i, 0))
    out = jax.ShapeDtypeStruct((t, D_MODEL), BF16)
    return pl.pallas_call(
        body, name=name, grid=(t // tm,),
        in_specs=[blk, pl.BlockSpec((D_MODEL, D_MODEL), lambda i: (0, 0)), wide, blk, blk], out_specs=[blk, blk, wide],
        out_shape=[out, out, jax.ShapeDtypeStruct((t, 2 * D_MODEL), BF16)], compiler_params=_params(("parallel",)),
    )(dh, wo, gates, ya, yb)


def _ffn_out_loss(name, a, w, res, gf, tgt, alpha):
    t, k = a.shape
    b, s, _ = tgt.shape
    l = t // b
    tm = _pick(t, (544, 384, 256, 128))
    per = l // tm
    assert per * tm == l and tm > PREFIX and l - PREFIX == s

    def body(a_ref, w_ref, r_ref, g_ref, t_ref, dh_ref, loss_ref, dg_ref, buf, sem):
        i = pl.program_id(0)
        bi, r = i // per, i % per

        def first_rows():
            return pltpu.make_async_copy(t_ref.at[bi, pl.ds(0, tm - PREFIX)], buf.at[pl.ds(PREFIX, tm - PREFIX)], sem)

        def later_rows():
            return pltpu.make_async_copy(t_ref.at[bi, pl.ds(pl.multiple_of(r * tm - PREFIX, 8), tm)], buf, sem)

        @pl.when(r == 0)
        def _():
            buf[0:PREFIX, :] = jnp.zeros((PREFIX, D_MODEL), F32)
            first_rows().start()

        if per > 1:
            @pl.when(r > 0)
            def _():
                later_rows().start()

        acc = lax.dot_general(a_ref[...], w_ref[...], _NN, preferred_element_type=F32)

        @pl.when(r == 0)
        def _():
            first_rows().wait()

        if per > 1:
            @pl.when(r > 0)
            def _():
                later_rows().wait()

        hv = acc * alpha + r_ref[...]
        row = lax.broadcasted_iota(jnp.int32, (tm, 1), 0)
        real = ((r > 0) | (row >= PREFIX)).astype(F32)
        g = g_ref[...]
        rn = lax.rsqrt(jnp.mean(hv * hv, axis=-1, keepdims=True) + EPS)
        xn = hv * rn
        err = (xn * g - buf[...]) * real
        lpart = 0.5 * jnp.sum(jnp.mean(err * err, axis=-1, keepdims=True), axis=0, keepdims=True)
        dy = err * (1.0 / D_MODEL)
        tv = dy * g
        dot = jnp.mean(tv * hv, axis=-1, keepdims=True)
        dh_ref[...] = rn * tv - hv * (rn * rn * rn * dot)
        gpart = jnp.sum(dy * xn, axis=0, keepdims=True)

        @pl.when(i == 0)
        def _():
            loss_ref[...] = jnp.zeros_like(loss_ref)
            dg_ref[...] = jnp.zeros_like(dg_ref)

        loss_ref[...] += jnp.broadcast_to(lpart, loss_ref.shape)
        dg_ref[...] += gpart

    row_spec = pl.BlockSpec((tm, D_MODEL), lambda i: (i, 0))
    vec = pl.BlockSpec((1, D_MODEL), lambda i: (0, 0))
    return pl.pallas_call(
        body, name=name, grid=(t // tm,),
        in_specs=[pl.BlockSpec((tm, k), lambda i: (i, 0)),
                  pl.BlockSpec((k, D_MODEL), lambda i: (0, 0), pipeline_mode=pl.Buffered(1)), row_spec, vec,
                  pl.BlockSpec(memory_space=pl.ANY)],
        out_specs=[row_spec, pl.BlockSpec((8, 128), lambda i: (0, 0)), vec],
        out_shape=[jax.ShapeDtypeStruct((t, D_MODEL), F32), jax.ShapeDtypeStruct((8, 128), F32),
                   jax.ShapeDtypeStruct((1, D_MODEL), F32)],
        scratch_shapes=[pltpu.VMEM((tm, D_MODEL), F32), pltpu.SemaphoreType.DMA(())],
        compiler_params=_params(("arbitrary",)),
    )(a, w, res, gf, tgt)


def _fgate_fwd(name, f3, bf_row):
    b, l, _ = f3.shape
    nb = l // BLOCK

    def body(f_ref, b_ref, cc_ref, cr_ref):
        r_i = lax.broadcasted_iota(jnp.int32, (BLOCK, BLOCK), 0)
        c_i = lax.broadcasted_iota(jnp.int32, (BLOCK, BLOCK), 1)
        tri = (r_i >= c_i).astype(F32)
        carry = jnp.zeros((1, 128), F32)
        for blk in range(nb):
            rows = slice(blk * BLOCK, (blk + 1) * BLOCK)
            z = f_ref[rows, :] + b_ref[...]
            lf = jnp.minimum(z, 0.0) - jnp.log(1.0 + jnp.exp(-jnp.abs(z)))
            cb = jnp.dot(tri, lf, preferred_element_type=F32, precision=lax.Precision.HIGHEST) + carry
            carry = cb[BLOCK - 1:BLOCK, :]
            cbt = cb.T
            for hh in range(B_HEADS):
                cc_ref[hh, rows, :] = jnp.sum(jnp.where(c_i == hh, cb, 0.0), axis=1, keepdims=True)
                cr_ref[hh, :, rows] = cbt[hh:hh + 1, :]

    return pl.pallas_call(
        body, name=name, grid=(b,),
        in_specs=[pl.BlockSpec((None, l, 128), lambda bi: (bi, 0, 0)),
                  pl.BlockSpec((1, 128), lambda bi: (0, 0))],
        out_specs=[pl.BlockSpec((None, B_HEADS, l, 1), lambda bi: (bi, 0, 0, 0)),
                   pl.BlockSpec((None, B_HEADS, 1, l), lambda bi: (bi, 0, 0, 0))],
        out_shape=[jax.ShapeDtypeStruct((b, B_HEADS, l, 1), F32), jax.ShapeDtypeStruct((b, B_HEADS, 1, l), F32)],
        compiler_params=_params(("parallel",)),
    )(f3, bf_row)


def _fgate_bwd(name, f3, bf_row, dcq, dck):
    b, l, _ = f3.shape
    nb = l // BLOCK

    def body(f_ref, b_ref, dcq_ref, dck_ref, df_ref, db_ref):
        r_i = lax.broadcasted_iota(jnp.int32, (BLOCK, BLOCK), 0)
        c_i = lax.broadcasted_iota(jnp.int32, (BLOCK, BLOCK), 1)
        tri = (r_i <= c_i).astype(F32)
        carry = jnp.zeros((1, 128), F32)
        total = jnp.zeros((1, 128), F32)
        for blk in range(nb - 1, -1, -1):
            rows = slice(blk * BLOCK, (blk + 1) * BLOCK)
            krows = jnp.concatenate([dck_ref[hh, :, rows] for hh in range(B_HEADS)]
                                    + [jnp.zeros((BLOCK - B_HEADS, BLOCK), F32)], axis=0)
            dcb = krows.T
            for hh in range(B_HEADS):
                dcb = dcb + jnp.where(c_i == hh, dcq_ref[hh, rows, :], 0.0)
            rc = jnp.dot(tri, dcb, preferred_element_type=F32, precision=lax.Precision.HIGHEST) + carry
            carry = rc[0:1, :]
            z = f_ref[rows, :] + b_ref[...]
            df = rc * (1.0 / (1.0 + jnp.exp(z)))
            df_ref[rows, :] = df.astype(BF16)
            total = total + jnp.sum(df, axis=0, keepdims=True)

        @pl.when(pl.program_id(0) == 0)
        def _():
            db_ref[...] = total

        @pl.when(pl.program_id(0) > 0)
        def _():
            db_ref[...] += total

    return pl.pallas_call(
        body, name=name, grid=(b,),
        in_specs=[pl.BlockSpec((None, l, 128), lambda bi: (bi, 0, 0)),
                  pl.BlockSpec((1, 128), lambda bi: (0, 0)),
                  pl.BlockSpec((None, B_HEADS, l, 1), lambda bi: (bi, 0, 0, 0)),
                  pl.BlockSpec((None, B_HEADS, 1, l), lambda bi: (bi, 0, 0, 0))],
        out_specs=[pl.BlockSpec((None, l, 128), lambda bi: (bi, 0, 0)), pl.BlockSpec((1, 128), lambda bi: (0, 0))],
        out_shape=[jax.ShapeDtypeStruct((b, l, 128), BF16), jax.ShapeDtypeStruct((1, 128), F32)],
        compiler_params=_params(("arbitrary",)),
    )(f3, bf_row, dcq, dck)


A_Q_BLK = B_SEG // A_WIDTH
A_K_BLK = (B_SEG + A_WIDTH) // 128
A_V_BLK = A_K_BLK + 1
A_SEG_BLK = B_SEG // A_SEG
STACK = A_HEADS * BLOCK


def _lane_lo():
    return lax.broadcasted_iota(jnp.int32, (1, 128), 1) < HEAD_DIM


def _stack_heads(x, masked):
    lo = _lane_lo()
    blks = [x[:, 128 * j:128 * (j + 1)] for j in range(4)]
    if not masked:
        return jnp.concatenate(blks + blks, axis=0)
    zero = jnp.zeros_like(blks[0])
    return jnp.concatenate([jnp.where(lo, bk, zero) for bk in blks] + [jnp.where(lo, zero, bk) for bk in blks], axis=0)


def _unstack_heads(y):
    lo = _lane_lo()
    return jnp.concatenate([jnp.where(lo, y[128 * j:128 * (j + 1)], y[128 * (4 + j):128 * (5 + j)]) for j in range(4)], axis=1)


def _swa_bias(slopes):
    r_i = jnp.arange(STACK)[:, None]
    c_i = jnp.arange(3 * BLOCK)[None, :]
    seg = c_i >> 7
    out = []
    for n in range(3):
        qpos = n * BLOCK + (r_i & (BLOCK - 1))
        kpos = jnp.where(seg == 0, c_i, (n - 2) * BLOCK + c_i)
        dist = qpos - kpos
        band = (seg != 0) & (dist < BLOCK) & (kpos >= PREFIX)
        meta = (seg == 0) & (c_i >= N_PAD)
        out.append(jnp.where((dist >= 0) & (band | meta), -slopes * dist.astype(F32), NEG))
    return jnp.stack(out, axis=0)


def _swa_scores(q, kcat, n, slope, bias):
    s = lax.dot_general(q, kcat, _NT, preferred_element_type=F32) + bias
    further = slope * (-BLOCK * jnp.maximum(n - 2, 0)).astype(F32)
    return jnp.concatenate([s[:, 0:BLOCK] + further, s[:, BLOCK:]], axis=1)


def _swa_specs():
    def kv(col_blk):
        return [pl.BlockSpec((None, BLOCK, 128), lambda b, n: (b, 0, col_blk)),
                pl.BlockSpec((None, BLOCK, 128), lambda b, n: (b, jnp.maximum(n - 1, 0), col_blk)),
                pl.BlockSpec((None, BLOCK, 128), lambda b, n: (b, n, col_blk))]

    q_spec = pl.BlockSpec((None, BLOCK, A_WIDTH), lambda b, n: (b, n, A_Q_BLK))
    o_spec = pl.BlockSpec((None, BLOCK, A_WIDTH), lambda b, n: (b, n, 0))
    col = pl.BlockSpec((STACK, 1), lambda b, n: (0, 0))
    bias = pl.BlockSpec((None, STACK, 3 * BLOCK), lambda b, n: (jnp.minimum(n, 2), 0, 0))
    lse_spec = pl.BlockSpec((None, A_HEADS, BLOCK, 1), lambda b, n: (b, 0, n, 0))
    return q_spec, kv(A_K_BLK), kv(A_V_BLK), o_spec, [col, col, bias], lse_spec


def _swa_fwd(name, qkv, slopes, sinks, bias):
    b, l, _ = qkv.shape
    nb = l // BLOCK

    def body(q_ref, k0_ref, kp_ref, kc_ref, v0_ref, vp_ref, vc_ref, sl_ref, sk_ref, bias_ref, o_ref, lse_ref):
        n = pl.program_id(1)
        qs = _stack_heads(q_ref[...], True) * SCALE
        kcat = jnp.concatenate([k0_ref[...], kp_ref[...], kc_ref[...]], axis=0)
        vcat = jnp.concatenate([v0_ref[...], vp_ref[...], vc_ref[...]], axis=0)
        s = _swa_scores(qs, kcat, n, sl_ref[...], bias_ref[...])
        sink = sk_ref[...]
        m = jnp.maximum(jnp.max(s, axis=-1, keepdims=True), sink)
        p = jnp.exp(s - m)
        den = jnp.sum(p, axis=-1, keepdims=True) + jnp.exp(sink - m)
        o = lax.dot_general(p.astype(BF16), vcat, _NN, preferred_element_type=F32) / den
        o_ref[...] = _unstack_heads(o)
        lse_ref[...] = (m + jnp.log(den)).reshape(A_HEADS, BLOCK, 1)

    q_spec, k_specs, v_specs, o_spec, consts, lse_spec = _swa_specs()
    return pl.pallas_call(
        body, name=name, grid=(b, nb),
        in_specs=[q_spec] + k_specs + v_specs + consts, out_specs=[o_spec, lse_spec],
        out_shape=[jax.ShapeDtypeStruct((b, l, A_WIDTH), F32), jax.ShapeDtypeStruct((b, A_HEADS, l, 1), F32)],
        compiler_params=_params(("parallel", "parallel")),
    )(qkv, qkv, qkv, qkv, qkv, qkv, qkv, slopes, sinks, bias)


def _swa_bwd(name, qkv, o, lse, do, slopes, sinks, bias, dqkv):
    b, l, _ = qkv.shape
    nb = l // BLOCK

    def body(q_ref, k0_ref, kp_ref, kc_ref, v0_ref, vp_ref, vc_ref, o_ref, lse_ref, do_ref, sl_ref, sk_ref, bias_ref, _,
             dx_ref, ds_ref, dk_acc, dv_acc):
        bi = pl.program_id(0)
        n = pl.program_id(1)
        qs = _stack_heads(q_ref[...], True) * SCALE
        dos32 = _stack_heads(do_ref[...], True)
        dos = dos32.astype(BF16)
        os_ = _stack_heads(o_ref[...], False)
        lsev = lse_ref[...].reshape(STACK, 1)
        kcat = jnp.concatenate([k0_ref[...], kp_ref[...], kc_ref[...]], axis=0)
        vcat = jnp.concatenate([v0_ref[...], vp_ref[...], vc_ref[...]], axis=0)
        s = _swa_scores(qs, kcat, n, sl_ref[...], bias_ref[...])
        p = jnp.exp(s - lsev)
        dsum = jnp.sum(dos32 * os_, axis=-1, keepdims=True)
        dp = lax.dot_general(dos, vcat, _NT, preferred_element_type=F32)
        dsc = (p * (dp - dsum)).astype(BF16)
        dq = lax.dot_general(dsc, kcat, _NN, preferred_element_type=F32) * SCALE
        row0 = pl.multiple_of(n * BLOCK, BLOCK)
        dx_ref[pl.ds(row0, BLOCK), 0:A_WIDTH] = _unstack_heads(dq).astype(BF16)
        dkc = lax.dot_general(dsc, qs, _TN, preferred_element_type=F32)
        dvc = lax.dot_general(p.astype(BF16), dos, _TN, preferred_element_type=F32)

        @pl.when(n == 0)
        def _():
            dk_acc[...] = jnp.zeros_like(dk_acc)
            dv_acc[...] = jnp.zeros_like(dv_acc)

        starts = (0, pl.multiple_of(jnp.maximum(n - 1, 0) * BLOCK, BLOCK), row0)
        for t, st in enumerate(starts):
            dk_acc[pl.ds(st, BLOCK), :] += dkc[t * BLOCK:(t + 1) * BLOCK, :]
            dv_acc[pl.ds(st, BLOCK), :] += dvc[t * BLOCK:(t + 1) * BLOCK, :]

        @pl.when(n == nb - 1)
        def _():
            dx_ref[:, A_WIDTH:A_WIDTH + 128] = dk_acc[...].astype(BF16)
            dx_ref[:, A_WIDTH + 128:A_SEG] = dv_acc[...].astype(BF16)

        dsink = -(jnp.exp(sk_ref[...] - lsev) * dsum)
        r8 = lax.broadcasted_iota(jnp.int32, (8, 128), 0)
        acc = jnp.zeros((8, 128), F32)
        for hh in range(A_HEADS):
            acc = acc + jnp.where(r8 == hh, jnp.sum(dsink[hh * BLOCK:(hh + 1) * BLOCK, :]), 0.0)

        @pl.when((bi == 0) & (n == 0))
        def _():
            ds_ref[...] = jnp.zeros_like(ds_ref)

        ds_ref[...] += acc

    q_spec, k_specs, v_specs, o_spec, consts, lse_spec = _swa_specs()
    return pl.pallas_call(
        body, name=name, grid=(b, nb),
        in_specs=[q_spec] + k_specs + v_specs + [o_spec, lse_spec, o_spec] + consts + [pl.BlockSpec(memory_space=pl.ANY)],
        out_specs=[pl.BlockSpec((None, l, A_SEG), lambda bb, n: (bb, 0, A_SEG_BLK)),
                   pl.BlockSpec((8, 128), lambda bb, n: (0, 0))],
        out_shape=[jax.ShapeDtypeStruct(dqkv.shape, BF16), jax.ShapeDtypeStruct((8, 128), F32)],
        scratch_shapes=[pltpu.VMEM((l, 128), F32), pltpu.VMEM((l, 128), F32)],
        input_output_aliases={13: 0},
        compiler_params=_params(("arbitrary", "arbitrary")),
    )(qkv, qkv, qkv, qkv, qkv, qkv, qkv, o, lse, do, slopes, sinks, bias, dqkv)


def _fox_mask(qk, ck, i):
    kh = qk.shape[1]
    qpos = i * BLOCK + lax.broadcasted_iota(jnp.int32, (BLOCK, kh), 0)
    kpos = lax.broadcasted_iota(jnp.int32, (BLOCK, kh), 1)
    return jnp.where((kpos <= qpos) & (kpos >= N_PAD), qk - ck, NEG)


def _pick_head(x, hh):
    lo = _lane_lo()
    return jnp.where(lo if hh == 0 else jnp.logical_not(lo), x, jnp.zeros_like(x))


def _both_heads(x):
    return jnp.concatenate([_pick_head(x, 0), _pick_head(x, 1)], axis=0)


def _fox_specs(l):
    pair = pl.BlockSpec((None, l, PAIR_W), lambda bi, hp: (bi, 0, hp))
    half = pl.BlockSpec((None, l, 128), lambda bi, hp: (bi, 0, hp))
    colv = pl.BlockSpec((None, 2, l, 1), lambda bi, hp: (bi, hp, 0, 0))
    rowv = pl.BlockSpec((None, 2, 1, l), lambda bi, hp: (bi, hp, 0, 0))
    return pair, half, colv, rowv


def _fox_fwd(name, qkv, c_col, c_row):
    b, l, _ = qkv.shape
    nb = l // BLOCK

    def body(x_ref, cc_ref, cr_ref, o_ref, lse_ref):
        for i in range(nb):
            rows = slice(i * BLOCK, (i + 1) * BLOCK)
            kh = (i + 1) * BLOCK
            qblk = x_ref[rows, 0:128]
            kv = x_ref[0:kh, 128:256]
            vv = x_ref[0:kh, 256:384]
            qk = lax.dot_general(_both_heads(qblk) * SCALE, kv, _NT, preferred_element_type=F32)
            ps, dens = [], []
            for hh in range(2):
                s = _fox_mask(qk[hh * BLOCK:(hh + 1) * BLOCK], cr_ref[hh, :, 0:kh], i)
                m = jnp.max(s, axis=-1, keepdims=True)
                p = jnp.exp(s - m)
                den = jnp.sum(p, axis=-1, keepdims=True)
                ps.append(p.astype(BF16))
                dens.append(den)
                lse_ref[hh, rows, :] = (m + jnp.log(den)) + cc_ref[hh, rows, :]
            pv = lax.dot_general(jnp.concatenate(ps, axis=0), vv, _NN, preferred_element_type=F32)
            o_ref[rows, :] = jnp.where(_lane_lo(), pv[0:BLOCK] / dens[0], pv[BLOCK:2 * BLOCK] / dens[1]).astype(BF16)

    pair, half, colv, rowv = _fox_specs(l)
    return pl.pallas_call(
        body, name=name, grid=(b, 4), in_specs=[pair, colv, rowv], out_specs=[half, colv],
        out_shape=[jax.ShapeDtypeStruct((b, l, B_WIDTH), BF16), jax.ShapeDtypeStruct((b, B_HEADS, l, 1), F32)],
        compiler_params=_params(("parallel", "parallel")),
    )(qkv, c_col, c_row)


def _fox_bwd(name, qkv, c_col, c_row, o, lse, do):
    b, l, _ = qkv.shape
    nb = l // BLOCK

    def body(x_ref, cc_ref, cr_ref, o_ref, lse_ref, do_ref, dx_ref, dcq_ref, dck_ref, dk_acc, dv_acc):
        dk_acc[...] = jnp.zeros_like(dk_acc)
        dv_acc[...] = jnp.zeros_like(dv_acc)
        dck_ref[...] = jnp.zeros_like(dck_ref)
        for i in range(nb):
            rows = slice(i * BLOCK, (i + 1) * BLOCK)
            kh = (i + 1) * BLOCK
            qblk = x_ref[rows, 0:128]
            kv = x_ref[0:kh, 128:256]
            vv = x_ref[0:kh, 256:384]
            doblk = do_ref[rows, :]
            ov = o_ref[rows, :].astype(F32)
            q2 = _both_heads(qblk) * SCALE
            do2 = _both_heads(doblk)
            qk = lax.dot_general(q2, kv, _NT, preferred_element_type=F32)
            dp = lax.dot_general(do2, vv, _NT, preferred_element_type=F32)
            ps, dss = [], []
            for hh in range(2):
                half = slice(hh * BLOCK, (hh + 1) * BLOCK)
                s = _fox_mask(qk[half], cr_ref[hh, :, 0:kh], i)
                p = jnp.exp(s - (lse_ref[hh, rows, :] - cc_ref[hh, rows, :]))
                dsum = jnp.sum(do2[half].astype(F32) * ov, axis=-1, keepdims=True)
                ds = p * (dp[half] - dsum)
                ps.append(p.astype(BF16))
                dss.append(ds.astype(BF16))
                dcq_ref[hh, rows, :] = jnp.sum(ds, axis=-1, keepdims=True)
                dck_ref[hh, :, 0:kh] -= jnp.sum(ds, axis=0, keepdims=True)
            p2 = jnp.concatenate(ps, axis=0)
            ds2 = jnp.concatenate(dss, axis=0)
            dq = lax.dot_general(ds2, kv, _NN, preferred_element_type=F32) * SCALE
            dk_acc[0:kh, :] += lax.dot_general(ds2, q2, _TN, preferred_element_type=F32)
            dv_acc[0:kh, :] += lax.dot_general(p2, do2, _TN, preferred_element_type=F32)
            dx_ref[rows, 0:128] = jnp.where(_lane_lo(), dq[0:BLOCK], dq[BLOCK:2 * BLOCK]).astype(BF16)
        dx_ref[:, 128:256] = dk_acc[...].astype(BF16)
        dx_ref[:, 256:384] = dv_acc[...].astype(BF16)

    pair, half, colv, rowv = _fox_specs(l)
    return pl.pallas_call(
        body, name=name, grid=(b, 4), in_specs=[pair, colv, rowv, half, colv, half],
        out_specs=[pair, colv, rowv],
        out_shape=[jax.ShapeDtypeStruct(qkv.shape, BF16), jax.ShapeDtypeStruct((b, B_HEADS, l, 1), F32),
                   jax.ShapeDtypeStruct((b, B_HEADS, 1, l), F32)],
        scratch_shapes=[pltpu.VMEM((l, 128), F32), pltpu.VMEM((l, 128), F32)],
        compiler_params=_params(("parallel", "parallel")),
    )(qkv, c_col, c_row, o, lse, do)


_FLIPS = ((0, 0, 1), (0, 1, 0), (0, 1, 1), (1, 0, 0), (1, 0, 1), (1, 1, 0), (1, 1, 1))


def _peer_table():
    x, y, c = lax.axis_index("x"), lax.axis_index("y"), lax.axis_index("c")
    me = 4 * x + 2 * y + c
    peers = []
    for fx, fy, fc in _FLIPS:
        px = 1 - x if fx else x
        py = 1 - y if fy else y
        pc = 1 - c if fc else c
        peers.append(((px, py, pc), 4 * px + 2 * py + pc))
    return me, peers


_HBM = pl.BlockSpec(memory_space=pltpu.HBM)
_SEM = pl.BlockSpec(memory_space=pltpu.SEMAPHORE)
_ANY = pl.BlockSpec(memory_space=pl.ANY)
_EFFECT = pltpu.SideEffectType.DATAFLOW_SIDE_EFFECTING


def _split_copy(srcs_are_pieces, src_refs, land_refs, send_sem, recv_sem, a, kk, me, peers, arriving):
    dev, lin = peers[kk]
    npeer = len(_FLIPS)
    src = src_refs[a] if srcs_are_pieces[a] else src_refs[a].at[lin]
    dst = land_refs[a].at[lin] if arriving else land_refs[a].at[me]
    return pltpu.make_async_remote_copy(src_ref=src, dst_ref=dst, send_sem=send_sem.at[a * npeer + kk],
                                        recv_sem=recv_sem.at[a * npeer + kk], device_id=dev, device_id_type=MESH_ID)


def _xchg_start(name, gather, scatter, after=None):
    me_out = 4 * lax.axis_index("x") + 2 * lax.axis_index("y") + lax.axis_index("c")
    srcs = list(gather) + list(scatter)
    is_piece = [True] * len(gather) + [False] * len(scatter)
    lands = []
    for a, piece in zip(srcs, is_piece):
        own = a[None] if piece else lax.dynamic_slice_in_dim(a, me_out, 1, axis=0)
        shape = ((N_DEV,) + tuple(a.shape)) if piece else tuple(a.shape)
        start = (me_out,) + (0,) * (len(shape) - 1)
        lands.append(lax.dynamic_update_slice(lax.empty(shape, a.dtype), own, start))
    n = len(srcs)
    nsem = n * len(_FLIPS)
    has_after = after is not None

    def body(*refs):
        src_refs = refs[:n]
        land_refs = refs[n:2 * n]
        outs = refs[2 * n + (1 if has_after else 0):]
        send_sem, recv_sem = outs[0], outs[1]
        token = outs[-1]
        me, peers = _peer_table()
        for kk in range(len(_FLIPS)):
            for a in range(n):
                _split_copy(is_piece, src_refs, land_refs, send_sem, recv_sem, a, kk, me, peers, False).start()
        token[...] = jnp.zeros_like(token)

    out_shape = ([pltpu.SemaphoreType.DMA((nsem,)), pltpu.SemaphoreType.DMA((nsem,))]
                 + [pltpu.HBM(tuple(a.shape), a.dtype) for a in srcs] + [pltpu.HBM(tuple(a.shape), a.dtype) for a in lands]
                 + [jax.ShapeDtypeStruct((8, 128), F32)])
    args = [pltpu.with_memory_space_constraint(a, pltpu.HBM) for a in srcs + lands] + ([after] if has_after else [])
    res = pl.pallas_call(
        body, name=name, out_shape=out_shape,
        in_specs=[_HBM] * (2 * n) + ([_ANY] if has_after else []),
        out_specs=[_SEM, _SEM] + [_HBM] * (2 * n) + [pl.BlockSpec(memory_space=pltpu.VMEM)],
        input_output_aliases={i: 2 + i for i in range(2 * n)},
        compiler_params=pltpu.CompilerParams(has_side_effects=_EFFECT),
    )(*args)
    state = (res[0], res[1], list(res[2:2 + n]), list(res[2 + n:2 + 2 * n]), is_piece)
    return state, res[-1]


def _xchg_wait(name, state, after):
    send_sem, recv_sem, srcs, lands, is_piece = state
    n = len(srcs)

    def body(*refs):
        src_refs = refs[:n]
        land_refs = refs[n:2 * n]
        s_sem, r_sem = refs[2 * n], refs[2 * n + 1]
        me, peers = _peer_table()
        for kk in range(len(_FLIPS)):
            for a in range(n):
                cp = _split_copy(is_piece, src_refs, land_refs, s_sem, r_sem, a, kk, me, peers, True)
                cp.wait_send()
                cp.wait_recv()

    out_shape = [pltpu.HBM(tuple(a.shape), a.dtype) for a in srcs] + [pltpu.HBM(tuple(a.shape), a.dtype) for a in lands]
    res = pl.pallas_call(
        body, name=name, out_shape=out_shape,
        in_specs=[_HBM] * (2 * n) + [_SEM, _SEM, _ANY], out_specs=[_HBM] * (2 * n),
        input_output_aliases={i: i for i in range(2 * n)},
        compiler_params=pltpu.CompilerParams(has_side_effects=_EFFECT),
    )(*srcs, *lands, send_sem, recv_sem, after)
    return list(res[n:])


_SIB = (0, 0, 1)
_ICI = ((0, 1, 0), (1, 0, 0), (1, 1, 0))


def _flip(fl):
    x, y, c = lax.axis_index("x"), lax.axis_index("y"), lax.axis_index("c")
    px = 1 - x if fl[0] else x
    py = 1 - y if fl[1] else y
    pc = 1 - c if fl[2] else c
    return (px, py, pc), 4 * px + 2 * py + pc


def _gather2_start(name, pieces, after=None):
    me_out = 4 * lax.axis_index("x") + 2 * lax.axis_index("y") + lax.axis_index("c")
    pieces = list(pieces)
    n = len(pieces)
    lands = [lax.dynamic_update_slice(lax.empty((N_DEV,) + tuple(a.shape), a.dtype), a[None],
                                      (me_out,) + (0,) * a.ndim) for a in pieces]
    first = (_SIB,) + _ICI
    has_after = after is not None

    def body(*refs):
        src_refs, land_refs = refs[:n], refs[n:2 * n]
        outs = refs[2 * n + (1 if has_after else 0):]
        send_sem, recv_sem, token = outs[0], outs[1], outs[-1]
        _, me = _flip((0, 0, 0))
        for kk, fl in enumerate(first):
            dev, _ = _flip(fl)
            for a in range(n):
                pltpu.make_async_remote_copy(src_ref=src_refs[a], dst_ref=land_refs[a].at[me],
                                             send_sem=send_sem.at[a * 4 + kk], recv_sem=recv_sem.at[a * 4 + kk],
                                             device_id=dev, device_id_type=MESH_ID).start()
        token[...] = jnp.zeros_like(token)

    hbm = [pltpu.HBM(tuple(a.shape), a.dtype) for a in pieces + lands]
    res = pl.pallas_call(
        body, name=name,
        out_shape=[pltpu.SemaphoreType.DMA((4 * n,)), pltpu.SemaphoreType.DMA((4 * n,))] + hbm
        + [jax.ShapeDtypeStruct((8, 128), F32)],
        in_specs=[_HBM] * (2 * n) + ([_ANY] if has_after else []),
        out_specs=[_SEM, _SEM] + [_HBM] * (2 * n) + [pl.BlockSpec(memory_space=pltpu.VMEM)],
        input_output_aliases={i: 2 + i for i in range(2 * n)},
        compiler_params=pltpu.CompilerParams(has_side_effects=_EFFECT),
    )(*([pltpu.with_memory_space_constraint(a, pltpu.HBM) for a in pieces + lands] + ([after] if has_after else [])))
    return (res[0], res[1], list(res[2:2 + n]), list(res[2 + n:2 + 2 * n])), res[-1]


def _gather2_forward(name, state, after):
    send_a, recv_a, pieces, lands = state
    n = len(pieces)
    first = (_SIB,) + _ICI

    def body(*refs):
        src_refs, land_refs = refs[:n], refs[n:2 * n]
        s_a, r_a = refs[2 * n], refs[2 * n + 1]
        outs = refs[2 * n + 3:]
        send_b, recv_b, token = outs[0], outs[1], outs[-1]
        for kk, fl in enumerate(first):
            dev, lin = _flip(fl)
            for a in range(n):
                cp = pltpu.make_async_remote_copy(src_ref=src_refs[a], dst_ref=land_refs[a].at[lin],
                                                  send_sem=s_a.at[a * 4 + kk], recv_sem=r_a.at[a * 4 + kk],
                                                  device_id=dev, device_id_type=MESH_ID)
                cp.wait_send()
                cp.wait_recv()
        sib, _ = _flip(_SIB)
        for j, fl in enumerate(_ICI):
            _, lin = _flip(fl)
            for a in range(n):
                pltpu.make_async_remote_copy(src_ref=land_refs[a].at[lin], dst_ref=land_refs[a].at[lin],
                                             send_sem=send_b.at[a * 3 + j], recv_sem=recv_b.at[a * 3 + j],
                                             device_id=sib, device_id_type=MESH_ID).start()
        token[...] = jnp.zeros_like(token)

    hbm = [pltpu.HBM(tuple(a.shape), a.dtype) for a in pieces + lands]
    res = pl.pallas_call(
        body, name=name,
        out_shape=[pltpu.SemaphoreType.DMA((3 * n,)), pltpu.SemaphoreType.DMA((3 * n,))] + hbm
        + [jax.ShapeDtypeStruct((8, 128), F32)],
        in_specs=[_HBM] * (2 * n) + [_SEM, _SEM, _ANY],
        out_specs=[_SEM, _SEM] + [_HBM] * (2 * n) + [pl.BlockSpec(memory_space=pltpu.VMEM)],
        input_output_aliases={i: 2 + i for i in range(2 * n)},
        compiler_params=pltpu.CompilerParams(has_side_effects=_EFFECT),
    )(*pieces, *lands, send_a, recv_a, after)
    return (res[0], res[1], list(res[2 + n:2 + 2 * n])), res[-1]


def _gather2_wait(name, state, after):
    send_b, recv_b, lands = state
    n = len(lands)

    def body(*refs):
        land_refs = refs[:n]
        s_b, r_b = refs[n], refs[n + 1]
        sib, _ = _flip(_SIB)
        for j, fl in enumerate(_ICI):
            _, sent = _flip(fl)
            _, arriving = _flip((fl[0], fl[1], 1))
            for a in range(n):
                cp = pltpu.make_async_remote_copy(src_ref=land_refs[a].at[sent], dst_ref=land_refs[a].at[arriving],
                                                  send_sem=s_b.at[a * 3 + j], recv_sem=r_b.at[a * 3 + j],
                                                  device_id=sib, device_id_type=MESH_ID)
                cp.wait_send()
                cp.wait_recv()

    res = pl.pallas_call(
        body, name=name, out_shape=[pltpu.HBM(tuple(a.shape), a.dtype) for a in lands],
        in_specs=[_HBM] * n + [_SEM, _SEM, _ANY], out_specs=[_HBM] * n,
        input_output_aliases={i: i for i in range(n)},
        compiler_params=pltpu.CompilerParams(has_side_effects=_EFFECT),
    )(*lands, send_b, recv_b, after)
    return list(res)


def _adam_math(w, g, m, v):
    m = ADAM_B1 * m + (1.0 - ADAM_B1) * g
    v = ADAM_B2 * v + (1.0 - ADAM_B2) * (g * g)
    m_hat = m / (1.0 - ADAM_B1 ** ADAM_STEP)
    v_hat = v / (1.0 - ADAM_B2 ** ADAM_STEP)
    delta = -ADAM_LR * (m_hat / (jnp.sqrt(v_hat) + ADAM_EPS) + ADAM_WD * w)
    return delta, m, v


def _adam(name, w, m, v, parts, transposed=False, dep=None):
    npart, _, cp = parts.shape
    has_dep = dep is not None
    if transposed:
        c, r = w.shape
        tr = _pick(r, (256, 128))
        blk = pl.BlockSpec((c, tr), lambda i: (0, i))
    else:
        r, c = w.shape
        tr = _pick(r, (256, 176, 128, 64, 16, 8, 1))
        blk = pl.BlockSpec((tr, c), lambda i: (i, 0))

    def body(w_ref, m_ref, v_ref, p_ref, *rest):
        g_ref, d_ref, mo_ref, vo_ref = rest[-4:]
        g = p_ref[0].astype(F32)
        for pp in range(1, npart):
            g = g + p_ref[pp].astype(F32)
        g = g.T[0:c, :] if transposed else g[:, 0:c]
        delta, mn, vn = _adam_math(w_ref[...], g, m_ref[...], v_ref[...])
        g_ref[...] = g
        d_ref[...] = delta
        mo_ref[...] = mn
        vo_ref[...] = vn

    out = jax.ShapeDtypeStruct(w.shape, F32)
    return pl.pallas_call(
        body, name=name, grid=(r // tr,),
        in_specs=[blk, blk, blk, pl.BlockSpec((npart, tr, cp), lambda i: (0, i, 0))]
        + ([pl.BlockSpec(memory_space=pl.ANY)] if has_dep else []),
        out_specs=[blk, blk, blk, blk], out_shape=[out, out, out, out], compiler_params=_params(("parallel",)),
    )(*((w, m, v, parts) + ((dep,) if has_dep else ())))


def _small_sum(name, packs):
    def body(p_ref, o_ref):
        tot = p_ref[0]
        for pp in range(1, N_DEV):
            tot = tot + p_ref[pp]
        o_ref[0:8, :] = tot[0:8, :]
        o_ref[8:24, :] = tot[8:24, :] + tot[24:40, :]

    return pl.pallas_call(body, name=name, out_shape=jax.ShapeDtypeStruct((24, D_MODEL), F32),
                          compiler_params=_params())(packs)


def _local_step(x, tgt, g1, gm, g2, gf, b_forget, sinks, weights, send):
    b, s, _ = x.shape
    l = s + PREFIX
    t = b * l
    (meta,) = weights("meta", x)

    h0, n1 = _embed_norm("embed_rms1_fwd", x, meta, g1)
    (w1i,) = weights("ffn1_in", n1)
    gu1, a1 = _ffn_in_fwd("ffn1_in_fwd", n1, w1i)
    (w1o,) = weights("ffn1_out", weights("ffn1_out:forward", a1))
    h1, um = _mm_res_norm("ffn1_out_fwd", a1, w1o, h0, gm, alpha=0.5)
    wi, wa, wb, wo = weights("mix", weights("mix:forward", um))
    qkv, gates, f2 = _proj_fwd("proj_fwd", um, wi)
    qkv3 = qkv.reshape(b, l, QKV_W)
    f3 = f2.reshape(b, l, 128)
    bf_row = jnp.pad(b_forget, ((0, 0), (0, 128 - B_HEADS)))
    c_col, c_row = _fgate_fwd("fgate_fwd", f3, bf_row)
    head_of_row = jnp.arange(STACK) // BLOCK
    slopes = jnp.exp2(-8.0 * (head_of_row + 1).astype(F32) / A_HEADS).reshape(STACK, 1)
    sink_rows = jnp.repeat(sinks.reshape(A_HEADS), BLOCK).reshape(STACK, 1)
    swa_bias = _swa_bias(slopes)
    oa3, lse_a = _swa_fwd("swa_fwd", qkv3, slopes, sink_rows, swa_bias)
    ob3, lse_b = _fox_fwd("fox_fwd", qkv3, c_col, c_row)
    oa = oa3.reshape(t, A_WIDTH)
    ob = ob3.reshape(t, B_WIDTH)
    mixed, ya, yb = _branch_gate_fwd("branch_gate_fwd", oa, ob, wa, wb, gates, dep=weights("ffn2:forward", ob))
    h2, n2 = _mm_res_norm("mix_out_fwd", mixed, wo, h1, g2)
    w2i, w2o = weights("ffn2", h2)
    gu2, a2 = _ffn_in_fwd("ffn2_in_fwd", n2, w2i)

    dh3, loss_blk, dgf = _ffn_out_loss("ffn2_out_loss", a2, w2o, h2, gf, tgt, 0.5)

    def ffn_bwd(tag, dh, h_in, g_norm, n_in, gu, a, w_in_blk, w_out, one_send, examples=None):
        dw_out = _mm_tn(tag + "_out_bwd_w", a, dh, alpha=0.5)
        dgu = _ffn_out_bwd_x(tag + "_out_bwd_x", dh, w_out, gu, dep=None if one_send else send(tag + "_out", (dw_out,)))
        dw_in = _ffn_in_bwd_w(tag + "_in_bwd_w", n_in, dgu)
        token = send(tag, (dw_in, dw_out)) if one_send else send(tag + "_in", (dw_in,))
        return _ffn_in_bwd_x(tag + "_in_bwd_x", dgu, w_in_blk, h_in, g_norm, dh, dep=token, examples=examples)

    dh2, dg2 = ffn_bwd("ffn2", dh3, h2, g2, n2, gu2, a2, w2i, w2o, True)

    dwo = _mm_tn("mix_out_bwd_w", mixed, dh2)
    dya, dyb, dgates = _mix_out_gate_bwd("mix_out_gate_bwd", dh2, wo, gates, ya, yb)
    doa, dob = _branch_bwd_x("branch_bwd_x", dya, dyb, wa, wb)
    dwa, dwb = _branch_bwd_w("branch_bwd_w", oa, ob, dya, dyb)
    dqkv3, dcq, dck = _fox_bwd("fox_bwd", qkv3, c_col, c_row, ob3, lse_b, dob.reshape(b, l, B_WIDTH))
    dqkv3, dsink = _swa_bwd("swa_bwd", qkv3, oa3, lse_a, doa.reshape(b, l, A_WIDTH), slopes, sink_rows, swa_bias, dqkv3)
    dqkv = dqkv3.reshape(t, QKV_W)
    df3, dbf = _fgate_bwd("fgate_bwd", f3, bf_row, dcq, dck)
    df = df3.reshape(t, 128)
    dwi_qkv = _mm_tn("proj_qkv_bwd_w", um, dqkv, tm_c=(512,), tn_c=(768,))
    dwi_g = _mm_tn("proj_gates_bwd_w", um, dgates, tm_c=(512,), tn_c=(512,))
    dwi_f = _mm_tn("proj_f_bwd_w", um, df, tm_c=(512,), tn_c=(128,))
    token = send("mix", (dwi_qkv, dwi_g, dwi_f, dwa, dwb, dwo))
    dh1, dgm = _proj_bwd_x("proj_bwd_x", dqkv, dgates, df, wi, h1, gm, dh2, dep=token)

    grad_x, dmeta3, dg1 = ffn_bwd("ffn1", dh1, h0, g1, n1, gu1, a1, w1i, w1o, False, examples=b)
    dmeta = dmeta3.reshape(b * N_META, D_MODEL)

    misc = jnp.concatenate([dbf[:, 0:B_HEADS], dsink[:, 0].reshape(1, A_HEADS), loss_blk[0:1, 0:1]], axis=1)
    misc = jnp.pad(misc, ((0, 0), (0, D_MODEL - misc.shape[1])))
    row = lax.broadcasted_iota(jnp.int32, (8, D_MODEL), 0)
    vec = jnp.zeros((8, D_MODEL), F32)
    for i, piece in enumerate((dg1, dgm, dg2, dgf, misc)):
        vec = jnp.where(row == i, piece, vec)
    small = jnp.concatenate([vec, dmeta], axis=0)
    return grad_x, small


def _pad_to(a, rows, cols):
    return jnp.pad(a, ((0, rows - a.shape[0]), (0, cols - a.shape[1])))


def _ffn_out_from_gathered(g):
    w = g.reshape(4, FF_SHARD, D_MODEL)
    return jnp.pad(w, ((0, 0), (0, FF_SHARD_P - FF_SHARD), (0, 0))).reshape(D_FF_P, D_MODEL)


def _ffn_out_to_scatter(dw):
    return dw.reshape(4, FF_SHARD_P, D_MODEL)[:, 0:FF_SHARD, :].reshape(N_DEV, FFO_SHARD, D_MODEL)


def _proj_segments():
    segs = [(HEAD_DIM * h, HEAD_DIM, 0, B_SEG + HEAD_DIM * A_HEAD_ORDER.index(h)) for h in range(A_HEADS)]
    segs += [(512, 128, 0, B_SEG + A_WIDTH), (640, 128, 0, B_SEG + A_WIDTH + 128)]
    for first, off in ((768, 0), (1280, 128), (1792, 256)):
        segs += [(first + 128 * hp, 128, 0, PAIR_W * hp + off) for hp in range(4)]
    segs += [(2304, B_HEADS, 2, 0), (2312, 2 * D_MODEL, 1, 0)]
    return segs


_RELAYOUT_ROWS = 256


def _proj_from_gathered(name, g):
    rows = _RELAYOUT_ROWS

    def body(g_ref, o_ref):
        def cols(first, width):
            out = []
            for p in range(N_DEV):
                lo, hi = max(first, WIN_SHARD * p), min(first + width, WIN_SHARD * (p + 1))
                if lo < hi:
                    out.append(g_ref[p, :, lo - WIN_SHARD * p:hi - WIN_SHARD * p])
            return out

        parts = []
        for arr in (0, 1, 2):
            for first, width, _, _ in sorted((s for s in _proj_segments() if s[2] == arr), key=lambda s: s[3]):
                parts += cols(first, width)
            if arr == 0:
                parts.append(jnp.zeros((rows, P_GATES - QKV_W), BF16))
        parts.append(jnp.zeros((rows, 128 - B_HEADS), BF16))
        o_ref[...] = jnp.concatenate(parts, axis=1)

    return pl.pallas_call(
        body, name=name, grid=(D_MODEL // rows,),
        in_specs=[pl.BlockSpec((N_DEV, rows, WIN_SHARD_P), lambda i: (0, i, 0))],
        out_specs=pl.BlockSpec((rows, PROJ_P), lambda i: (i, 0)),
        out_shape=jax.ShapeDtypeStruct((D_MODEL, PROJ_P), BF16), compiler_params=_params(("parallel",)),
    )(g)


def _proj_to_scatter(name, dqkv_w, dg_w, df_w):
    rows = _RELAYOUT_ROWS
    segs = sorted(_proj_segments())

    def body(q_ref, g_ref, f_ref, o_ref):
        arrays = (q_ref, g_ref, f_ref)
        for p in range(N_DEV):
            parts = []
            for first, width, arr, at in segs:
                lo, hi = max(first, WIN_SHARD * p), min(first + width, WIN_SHARD * (p + 1))
                if lo < hi:
                    parts.append(arrays[arr][:, at + lo - first:at + hi - first])
            parts.append(jnp.zeros((rows, WIN_SHARD_P - WIN_SHARD), BF16))
            o_ref[p] = jnp.concatenate(parts, axis=1)

    return pl.pallas_call(
        body, name=name, grid=(D_MODEL // rows,),
        in_specs=[pl.BlockSpec((rows, QKV_W), lambda i: (i, 0)), pl.BlockSpec((rows, 2 * D_MODEL), lambda i: (i, 0)),
                  pl.BlockSpec((rows, 128), lambda i: (i, 0))],
        out_specs=pl.BlockSpec((N_DEV, rows, WIN_SHARD_P), lambda i: (0, i, 0)),
        out_shape=jax.ShapeDtypeStruct((N_DEV, D_MODEL, WIN_SHARD_P), BF16), compiler_params=_params(("parallel",)),
    )(dqkv_w, dg_w, df_w)


def _a_rows_from_natural(w):
    return jnp.concatenate([w[HEAD_DIM * h:HEAD_DIM * (h + 1)] for h in A_HEAD_ORDER], axis=0)


def _a_rows_to_natural(w):
    return jnp.concatenate([w[HEAD_DIM * A_HEAD_ORDER.index(h):HEAD_DIM * (A_HEAD_ORDER.index(h) + 1)]
                            for h in range(A_HEADS)], axis=0)


def kernel(x, meta_tokens, ffn1_norm, ffn1_w_in, ffn1_w_out, mix_norm, w_in, b_forget, attn_sinks, w_branch_a, w_branch_b, w_out, ffn2_norm, ffn2_w_in, ffn2_w_out, final_norm, loss_target, m_meta_tokens, m_ffn1_norm, m_ffn1_w_in, m_ffn1_w_out, m_mix_norm, m_w_in, m_b_forget, m_attn_sinks, m_w_branch_a, m_w_branch_b, m_w_out, m_ffn2_norm, m_ffn2_w_in, m_ffn2_w_out, m_final_norm, v_meta_tokens, v_ffn1_norm, v_ffn1_w_in, v_ffn1_w_out, v_mix_norm, v_w_in, v_b_forget, v_attn_sinks, v_w_branch_a, v_w_branch_b, v_w_out, v_ffn2_norm, v_ffn2_w_in, v_ffn2_w_out, v_final_norm):
    me = 4 * lax.axis_index("x") + 2 * lax.axis_index("y") + lax.axis_index("c")

    shards = (
        _pad_to(ffn1_w_in[0].astype(BF16), D_MODEL, FF_SHARD_P),
        ffn1_w_out[0].astype(BF16),
        _pad_to(w_in[0].astype(BF16), D_MODEL, WIN_SHARD_P),
        w_branch_a[0].astype(BF16), w_branch_b[0].astype(BF16), w_out[0].astype(BF16),
        _pad_to(ffn2_w_in[0].astype(BF16), D_MODEL, FF_SHARD_P),
        ffn2_w_out[0].astype(BF16),
        meta_tokens,
    )
    s1i, s1o, swi, swa, swb, swo, s2i, s2o, smeta = shards
    first_level, second_level = {}, {}
    first_level["meta"], tok = _gather2_start("gather_meta_start", (smeta,))
    first_level["ffn1_in"], tok = _gather2_start("gather_ffn1_in_start", (s1i,), after=tok)
    first_level["ffn1_out"], tok = _gather2_start("gather_ffn1_out_start", (s1o,), after=tok)
    first_level["mix"], tok = _gather2_start("gather_mix_start", (swi, swa, swb, swo), after=tok)
    first_level["ffn2"], tok = _gather2_start("gather_ffn2_start", (s2i, s2o), after=tok)
    started = {"tok": tok}

    def weights(group, after):
        if group.endswith(":forward"):
            group = group[:-len(":forward")]
            second_level[group], token = _gather2_forward("gather_" + group + "_forward", first_level[group], after)
            return token
        if group == "meta":
            after = weights("meta:forward", started["tok"])
        if group == "ffn1_in":
            after = weights("ffn1_in:forward", after)
        got = _gather2_wait("gather_" + group + "_wait", second_level[group], after)
        if group == "mix":
            gwi, gwa, gwb, gwo = got
            return (_proj_from_gathered("proj_w_relayout", gwi), _a_rows_from_natural(gwa.transpose(1, 0, 2).reshape(A_WIDTH, D_MODEL)),
                    gwb.transpose(1, 0, 2).reshape(B_WIDTH, D_MODEL), gwo.reshape(D_MODEL, D_MODEL))
        if group == "ffn1_out":
            return (_ffn_out_from_gathered(got[0]),)
        if group == "meta":
            return (got[0].transpose(1, 0, 2).reshape(N_META, D_MODEL),)
        if group == "ffn1_in":
            return (got[0].reshape(2, 4, D_MODEL, FF_SHARD_P),)
        return got[0].reshape(2, 4, D_MODEL, FF_SHARD_P), _ffn_out_from_gathered(got[1])

    scatter_state = {}

    def send(group, grads):
        if group == "mix":
            dwi_qkv, dwi_g, dwi_f, dwa, dwb, dwo = grads
            dwa = _a_rows_to_natural(dwa)
            blocks = (_proj_to_scatter("proj_dw_relayout", dwi_qkv, dwi_g, dwi_f), dwa.reshape(A_WIDTH, N_DEV, 128).transpose(1, 0, 2),
                      dwb.reshape(B_WIDTH, N_DEV, 128).transpose(1, 0, 2), dwo.reshape(N_DEV, 128, D_MODEL))
        elif group.endswith("_in"):
            blocks = (grads[0].reshape(N_DEV, D_MODEL, FF_SHARD_P),)
        elif group.endswith("_out"):
            blocks = (_ffn_out_to_scatter(grads[0]),)
        else:
            blocks = (grads[0].reshape(N_DEV, D_MODEL, FF_SHARD_P), _ffn_out_to_scatter(grads[1]))
        scatter_state[group], token = _xchg_start("scatter_" + group + "_start", (), blocks)
        return token

    gf = final_norm.reshape(1, D_MODEL)
    grad_x, small = _local_step(x, loss_target, ffn1_norm, mix_norm, ffn2_norm, gf, b_forget, attn_sinks, weights, send)

    small_state, after = _xchg_start("gather_small_start", (small,), ())
    out = {}
    updates = (
        ("ffn2", (("ffn2_w_in", ffn2_w_in, m_ffn2_w_in, v_ffn2_w_in), ("ffn2_w_out", ffn2_w_out, m_ffn2_w_out, v_ffn2_w_out))),
        ("ffn1_out", (("ffn1_w_out", ffn1_w_out, m_ffn1_w_out, v_ffn1_w_out),)),
        ("mix", (("w_in", w_in, m_w_in, v_w_in), ("w_branch_a", w_branch_a, m_w_branch_a, v_w_branch_a),
                 ("w_branch_b", w_branch_b, m_w_branch_b, v_w_branch_b), ("w_out", w_out, m_w_out, v_w_out))),
    )
    last_update = ("ffn1_in", (("ffn1_w_in", ffn1_w_in, m_ffn1_w_in, v_ffn1_w_in),))

    def update(group, members, after):
        parts_list = _xchg_wait("scatter_" + group + "_wait", scatter_state[group], after)
        prev = None
        for (nm, w, m, v), parts in zip(members, parts_list):
            if nm.endswith("w_in"):
                res4 = _adam("adam_" + nm, w[0].T, m[0].T, v[0].T, parts, transposed=True, dep=prev)
                out[nm] = tuple(r.T[None] for r in res4)
            else:
                res4 = _adam("adam_" + nm, w[0], m[0], v[0], parts, dep=prev)
                out[nm] = tuple(r[None] for r in res4)
            prev = res4[0]
        return prev

    for group, members in updates + (last_update,):
        after = update(group, members, after)
    (packs,) = _xchg_wait("gather_small_wait", small_state, after)

    tot = _small_sum("small_sum", packs)
    loss = tot[4, 2 * B_HEADS]
    g_meta = lax.dynamic_slice(tot[8:24, :], (0, me * 128), (N_META, 128))
    out["meta_tokens"] = tuple(_adam("adam_meta_tokens", meta_tokens, m_meta_tokens, v_meta_tokens, g_meta[None]))

    def pack_small(n1, nm, n2, nf, bfv, skv):
        misc = jnp.pad(jnp.concatenate([bfv, skv], axis=1), ((0, 0), (0, D_MODEL - 2 * B_HEADS)))
        row = lax.broadcasted_iota(jnp.int32, (8, D_MODEL), 0)
        vec = jnp.zeros((8, D_MODEL), F32)
        for i, piece in enumerate((n1, nm, n2, nf.reshape(1, D_MODEL), misc)):
            vec = jnp.where(row == i, piece, vec)
        return vec

    w_pack = pack_small(ffn1_norm, mix_norm, ffn2_norm, final_norm, b_forget, attn_sinks)
    m_pack = pack_small(m_ffn1_norm, m_mix_norm, m_ffn2_norm, m_final_norm, m_b_forget, m_attn_sinks)
    v_pack = pack_small(v_ffn1_norm, v_mix_norm, v_ffn2_norm, v_final_norm, v_b_forget, v_attn_sinks)
    small4 = _adam("adam_small", w_pack, m_pack, v_pack, tot[0:8][None])
    for i, nm in enumerate(("ffn1_norm", "mix_norm", "ffn2_norm")):
        out[nm] = tuple(r[i:i + 1] for r in small4)
    out["final_norm"] = tuple(r[3] for r in small4)
    out["b_forget"] = tuple(r[4:5, 0:B_HEADS] for r in small4)
    out["attn_sinks"] = tuple(r[4:5, B_HEADS:2 * B_HEADS] for r in small4)

    names = ("meta_tokens", "ffn1_norm", "ffn1_w_in", "ffn1_w_out", "mix_norm", "w_in", "b_forget", "attn_sinks",
             "w_branch_a", "w_branch_b", "w_out", "ffn2_norm", "ffn2_w_in", "ffn2_w_out", "final_norm")
    return (loss, grad_x) + tuple(out[nm][kind] for kind in range(4) for nm in names)
```

```python
import jax
import jax.numpy as jnp
from jax import lax
from jax.experimental import pallas as pl
from jax.experimental.pallas import tpu as pltpu

F32 = jnp.float32
BF16 = jnp.bfloat16

D_MODEL = 1024
N_META = 16
BLOCK = 128
PREFIX = 128
N_PAD = PREFIX - N_META
HEAD_DIM = 64
A_HEADS = 8
B_HEADS = 8
A_WIDTH = 512
A_KV_WIDTH = 128
B_WIDTH = 512
D_FF = 2816
N_DEV = 8
FF_SHARD = 2 * D_FF // N_DEV
FF_SHARD_P = 768
FFO_SHARD = D_FF // N_DEV
D_FF_P = 4 * FF_SHARD_P
W_IN_COLS = 4360
WIN_SHARD = W_IN_COLS // N_DEV
WIN_SHARD_P = 640
PAIR_W = 3 * 128
B_SEG = 4 * PAIR_W
A_SEG = A_WIDTH + 2 * A_KV_WIDTH
QKV_W = B_SEG + A_SEG
P_GATES = 2 * (2 * D_MODEL)
P_F = P_GATES + 2 * D_MODEL
PROJ_P = P_F + 128
A_HEAD_ORDER = (0, 4, 1, 5, 2, 6, 3, 7)
EPS = 1e-6
NEG = -1e30
SCALE = HEAD_DIM ** -0.5
ADAM_LR = 0.001
ADAM_B1 = 0.9
ADAM_B2 = 0.999
ADAM_EPS = 1e-08
ADAM_WD = 0.01
ADAM_STEP = 10
VMEM_LIMIT = 56 * 1024 * 1024
MESH_ID = pl.DeviceIdType.MESH
SMALL_ROWS = 40

_NN = (((1,), (0,)), ((), ()))
_NT = (((1,), (1,)), ((), ()))
_TN = (((0,), (0,)), ((), ()))


def _params(sem=None):
    return pltpu.CompilerParams(dimension_semantics=sem, vmem_limit_bytes=VMEM_LIMIT)


def _pick(n, cands):
    for c in cands:
        if n % c == 0:
            return c
    raise ValueError(f"no tile for {n}")


def _bf(v):
    return v if v.dtype == BF16 else v.astype(BF16)


def _mm(name, a, b, dims, grid, a_spec, b_spec, o_spec, out_shape, out_dtype, alpha=1.0):
    def body(a_ref, b_ref, o_ref):
        acc = lax.dot_general(_bf(a_ref[...]), _bf(b_ref[...]), dims, preferred_element_type=F32)
        if alpha != 1.0:
            acc = acc * alpha
        o_ref[...] = acc.astype(o_ref.dtype)

    return pl.pallas_call(
        body, name=name, grid=grid, in_specs=[a_spec, b_spec], out_specs=o_spec,
        out_shape=jax.ShapeDtypeStruct(out_shape, out_dtype),
        compiler_params=_params(("parallel",) * len(grid)),
    )(a, b)


def _mm_res_norm(name, a, w, res, g_next, alpha=1.0):
    t, k = a.shape
    tm = _pick(t, (544, 384, 256, 128))

    def body(a_ref, w_ref, r_ref, g_ref, h_ref, n_ref):
        acc = lax.dot_general(_bf(a_ref[...]), w_ref[...], _NN, preferred_element_type=F32)
        if alpha != 1.0:
            acc = acc * alpha
        hv = acc + r_ref[...]
        h_ref[...] = hv
        r = lax.rsqrt(jnp.mean(hv * hv, axis=-1, keepdims=True) + EPS)
        n_ref[...] = ((hv * r) * g_ref[...]).astype(BF16)

    row = pl.BlockSpec((tm, D_MODEL), lambda i: (i, 0))
    return pl.pallas_call(
        body, name=name, grid=(t // tm,),
        in_specs=[pl.BlockSpec((tm, k), lambda i: (i, 0)), pl.BlockSpec((k, D_MODEL), lambda i: (0, 0)), row,
                  pl.BlockSpec((1, D_MODEL), lambda i: (0, 0))],
        out_specs=[row, row],
        out_shape=[jax.ShapeDtypeStruct((t, D_MODEL), F32), jax.ShapeDtypeStruct((t, D_MODEL), BF16)],
        compiler_params=_params(("parallel",)),
    )(a, w, res, g_next)


def _mm_tn(name, a, b, out_dtype=BF16, alpha=1.0, tm_c=(768, 512, 256, 128), tn_c=(512, 640, 256, 128)):
    t, m = a.shape
    n = b.shape[1]
    tm = _pick(m, tm_c)
    tn = _pick(n, tn_c)
    bytes_a, bytes_b = a.size * a.dtype.itemsize, b.size * b.dtype.itemsize
    if bytes_a + bytes_b * (m // tm) <= bytes_b + bytes_a * (n // tn):
        return _mm(name, a, b, _TN, (m // tm, n // tn),
                   pl.BlockSpec((t, tm), lambda i, j: (0, i)), pl.BlockSpec((t, tn), lambda i, j: (0, j)),
                   pl.BlockSpec((tm, tn), lambda i, j: (i, j)), (m, n), out_dtype, alpha=alpha)
    return _mm(name, a, b, _TN, (n // tn, m // tm),
               pl.BlockSpec((t, tm), lambda j, i: (0, i)), pl.BlockSpec((t, tn), lambda j, i: (0, j)),
               pl.BlockSpec((tm, tn), lambda j, i: (i, j)), (m, n), out_dtype, alpha=alpha)


def _ffn_in_fwd(name, n, wblk, dep=None):
    t = n.shape[0]
    tm = _pick(t, (1088, 768, 512, 256, 128))
    has_dep = dep is not None

    def body(n_ref, w_ref, *rest):
        gu_ref, a_ref = rest[-2], rest[-1]
        nv = n_ref[...]
        g = lax.dot_general(nv, w_ref[0], _NN, preferred_element_type=F32)
        u = lax.dot_general(nv, w_ref[1], _NN, preferred_element_type=F32)
        sg = jax.nn.sigmoid(g)
        silu = g * sg
        a_ref[...] = (silu * u).astype(BF16)
        gu_ref[0] = ((0.5 * u) * (sg + silu * (1.0 - sg))).astype(BF16)
        gu_ref[1] = (0.5 * silu).astype(BF16)

    return pl.pallas_call(
        body, name=name, grid=(t // tm, 4),
        in_specs=[pl.BlockSpec((tm, D_MODEL), lambda i, j: (i, 0)),
                  pl.BlockSpec((2, None, D_MODEL, FF_SHARD_P), lambda i, j: (0, j, 0, 0))]
        + ([pl.BlockSpec(memory_space=pl.ANY)] if has_dep else []),
        out_specs=[pl.BlockSpec((2, tm, FF_SHARD_P), lambda i, j: (0, i, j)),
                   pl.BlockSpec((tm, FF_SHARD_P), lambda i, j: (i, j))],
        out_shape=[jax.ShapeDtypeStruct((2, t, D_FF_P), BF16), jax.ShapeDtypeStruct((t, D_FF_P), BF16)],
        compiler_params=_params(("parallel", "parallel")),
    )(*((n, wblk) + ((dep,) if has_dep else ())))


def _ffn_out_bwd_x(name, dh, w_out, gu, dep=None):
    t = dh.shape[0]
    tm = _pick(t, (1088, 768, 512, 256, 128))
    has_dep = dep is not None

    def body(dh_ref, w_ref, gu_ref, *rest):
        o_ref = rest[-1]
        da = lax.dot_general(_bf(dh_ref[...]), w_ref[...], _NT, preferred_element_type=F32)
        o_ref[0] = (da * gu_ref[0].astype(F32)).astype(BF16)
        o_ref[1] = (da * gu_ref[1].astype(F32)).astype(BF16)

    gu_spec = pl.BlockSpec((2, tm, FF_SHARD_P), lambda i, j: (0, i, j))
    return pl.pallas_call(
        body, name=name, grid=(t // tm, 4),
        in_specs=[pl.BlockSpec((tm, D_MODEL), lambda i, j: (i, 0)), pl.BlockSpec((FF_SHARD_P, D_MODEL), lambda i, j: (j, 0)),
                  gu_spec] + ([pl.BlockSpec(memory_space=pl.ANY)] if has_dep else []),
        out_specs=gu_spec, out_shape=jax.ShapeDtypeStruct((2, t, D_FF_P), BF16),
        compiler_params=_params(("parallel", "parallel")),
    )(*((dh, w_out, gu) + ((dep,) if has_dep else ())))


def _rms_bwd_rows(dn, h, g, dres):
    r = lax.rsqrt(jnp.mean(h * h, axis=-1, keepdims=True) + EPS)
    tv = dn * g
    dot = jnp.mean(tv * h, axis=-1, keepdims=True)
    return dres + (r * tv - h * (r * r * r * dot)), jnp.sum(dn * (h * r), axis=0, keepdims=True)


def _accumulate_rows(ref, part, first):
    @pl.when(first)
    def _():
        ref[...] = part

    @pl.when(jnp.logical_not(first))
    def _():
        ref[...] += part


def _ffn_in_bwd_x(name, dgu, wblk, h_in, g_norm, dres, dep=None, examples=None):
    t = dgu.shape[1]
    tm = _pick(t, (544, 384, 256, 128))
    has_dep = dep is not None
    split = examples is not None
    if split:
        l = t // examples
        per = l // tm
        assert per * tm == l and tm > PREFIX

    def body(d_ref, w_ref, h_ref, g_ref, r_ref, *rest):
        acc = None
        for s in range(2):
            for j in range(4):
                part = lax.dot_general(d_ref[s, :, FF_SHARD_P * j:FF_SHARD_P * (j + 1)], w_ref[s, j], _NT,
                                       preferred_element_type=F32)
                acc = part if acc is None else acc + part
        dh, dg = _rms_bwd_rows(acc, h_ref[...], g_ref[...], r_ref[...])
        i = pl.program_id(0)
        if not split:
            dh_ref, dg_ref = rest[-2], rest[-1]
            dh_ref[...] = dh
        else:
            gx_ref, meta_ref, dg_ref, buf, sem = rest[-5:]

            def out_copy(step):
                bi, r = step // per, step % per
                head = pltpu.make_async_copy(buf.at[pl.ds(PREFIX, tm - PREFIX)], gx_ref.at[bi, pl.ds(0, tm - PREFIX)], sem)
                if per == 1:
                    return r == 0, head, None
                return r == 0, head, pltpu.make_async_copy(
                    buf, gx_ref.at[bi, pl.ds(pl.multiple_of(jnp.maximum(r, 1) * tm - PREFIX, 8), tm)], sem)

            def run(step, method):
                is_head, head, later = out_copy(step)

                @pl.when(is_head)
                def _():
                    getattr(head, method)()

                if later is not None:
                    @pl.when(jnp.logical_not(is_head))
                    def _():
                        getattr(later, method)()

            @pl.when(i > 0)
            def _():
                run(i - 1, "wait")

            buf[...] = dh

            @pl.when(i % per == 0)
            def _():
                meta_ref[...] = dh[N_PAD:PREFIX]

            run(i, "start")

            @pl.when(i == pl.num_programs(0) - 1)
            def _():
                run(i, "wait")

        _accumulate_rows(dg_ref, dg, i == 0)

    row = pl.BlockSpec((tm, D_MODEL), lambda i: (i, 0))
    vec = pl.BlockSpec((1, D_MODEL), lambda i: (0, 0))
    if split:
        out_specs = [pl.BlockSpec(memory_space=pl.ANY), pl.BlockSpec((None, N_META, D_MODEL), lambda i: (i // per, 0, 0)), vec]
        out_shape = [jax.ShapeDtypeStruct((examples, l - PREFIX, D_MODEL), F32),
                     jax.ShapeDtypeStruct((examples, N_META, D_MODEL), F32), jax.ShapeDtypeStruct((1, D_MODEL), F32)]
        scratch = [pltpu.VMEM((tm, D_MODEL), F32), pltpu.SemaphoreType.DMA(())]
    else:
        out_specs = [row, vec]
        out_shape = [jax.ShapeDtypeStruct((t, D_MODEL), F32), jax.ShapeDtypeStruct((1, D_MODEL), F32)]
        scratch = []
    return pl.pallas_call(
        body, name=name, grid=(t // tm,),
        in_specs=[pl.BlockSpec((2, tm, D_FF_P), lambda i: (0, i, 0)),
                  pl.BlockSpec((2, 4, D_MODEL, FF_SHARD_P), lambda i: (0, 0, 0, 0), pipeline_mode=pl.Buffered(1)),
                  row, vec, row]
        + ([pl.BlockSpec(memory_space=pl.ANY)] if has_dep else []),
        out_specs=out_specs, out_shape=out_shape, scratch_shapes=scratch,
        compiler_params=_params(("arbitrary",)),
    )(*((dgu, wblk, h_in, g_norm, dres) + ((dep,) if has_dep else ())))


def _proj_fwd(name, um, wi):
    t = um.shape[0]
    tm = _pick(t, (544, 384, 256, 128))

    def body(u_ref, w_ref, q_ref, g_ref, f_ref):
        uv = u_ref[...]
        q_ref[...] = lax.dot_general(uv, w_ref[:, 0:QKV_W], _NN, preferred_element_type=F32).astype(BF16)
        g_ref[...] = lax.dot_general(uv, w_ref[:, P_GATES:P_F], _NN, preferred_element_type=F32)
        f_ref[...] = lax.dot_general(uv, w_ref[:, P_F:PROJ_P], _NN, preferred_element_type=F32)

    return pl.pallas_call(
        body, name=name, grid=(t // tm,),
        in_specs=[pl.BlockSpec((tm, D_MODEL), lambda i: (i, 0)),
                  pl.BlockSpec((D_MODEL, PROJ_P), lambda i: (0, 0), pipeline_mode=pl.Buffered(1))],
        out_specs=[pl.BlockSpec((tm, QKV_W), lambda i: (i, 0)), pl.BlockSpec((tm, 2 * D_MODEL), lambda i: (i, 0)),
                   pl.BlockSpec((tm, 128), lambda i: (i, 0))],
        out_shape=[jax.ShapeDtypeStruct((t, QKV_W), BF16), jax.ShapeDtypeStruct((t, 2 * D_MODEL), F32),
                   jax.ShapeDtypeStruct((t, 128), F32)],
        compiler_params=_params(("parallel",)),
    )(um, wi)


def _proj_bwd_x(name, dqkv, dgates, df, wi, h_in, g_norm, dres, dep=None):
    t = dqkv.shape[0]
    tm = _pick(t, (544, 384, 256, 128))
    has_dep = dep is not None

    def body(q_ref, gt_ref, f_ref, w_ref, h_ref, g_ref, r_ref, *rest):
        dh_ref, dg_ref = rest[-2], rest[-1]
        acc = lax.dot_general(q_ref[...], w_ref[:, 0:QKV_W], _NT, preferred_element_type=F32)
        acc = acc + lax.dot_general(gt_ref[...], w_ref[:, P_GATES:P_F], _NT, preferred_element_type=F32)
        acc = acc + lax.dot_general(f_ref[...], w_ref[:, P_F:PROJ_P], _NT, preferred_element_type=F32)
        dh, dg = _rms_bwd_rows(acc, h_ref[...], g_ref[...], r_ref[...])
        dh_ref[...] = dh
        _accumulate_rows(dg_ref, dg, pl.program_id(0) == 0)

    row = pl.BlockSpec((tm, D_MODEL), lambda i: (i, 0))
    vec = pl.BlockSpec((1, D_MODEL), lambda i: (0, 0))
    return pl.pallas_call(
        body, name=name, grid=(t // tm,),
        in_specs=[pl.BlockSpec((tm, QKV_W), lambda i: (i, 0)), pl.BlockSpec((tm, 2 * D_MODEL), lambda i: (i, 0)),
                  pl.BlockSpec((tm, 128), lambda i: (i, 0)),
                  pl.BlockSpec((D_MODEL, PROJ_P), lambda i: (0, 0), pipeline_mode=pl.Buffered(1)), row, vec, row]
        + ([pl.BlockSpec(memory_space=pl.ANY)] if has_dep else []),
        out_specs=[row, vec],
        out_shape=[jax.ShapeDtypeStruct((t, D_MODEL), F32), jax.ShapeDtypeStruct((1, D_MODEL), F32)],
        compiler_params=_params(("arbitrary",)),
    )(*((dqkv, dgates, df, wi, h_in, g_norm, dres) + ((dep,) if has_dep else ())))


def _ffn_in_bwd_w(name, n, dgu):
    t = n.shape[0]
    return _mm(name, n, dgu, _TN, (2, 4),
               pl.BlockSpec((t, D_MODEL), lambda s, j: (0, 0)),
               pl.BlockSpec((None, t, FF_SHARD_P), lambda s, j: (s, 0, j)),
               pl.BlockSpec((None, None, D_MODEL, FF_SHARD_P), lambda s, j: (s, j, 0, 0)),
               (2, 4, D_MODEL, FF_SHARD_P), BF16)


def _embed_norm(name, x, meta, g):
    b, s, _ = x.shape
    half = (s + PREFIX) // 2
    first = half - PREFIX
    assert first > 0 and half % 16 == 0

    def body(x_ref, m_ref, g_ref, h_ref, n_ref, buf, sem):
        bi, k = pl.program_id(0), pl.program_id(1)

        def tokens(example, second, method):
            if second:
                cp = pltpu.make_async_copy(x_ref.at[example, pl.ds(first, half)], buf.at[1], sem.at[1])
            else:
                cp = pltpu.make_async_copy(x_ref.at[example, pl.ds(0, first)], buf.at[0, pl.ds(PREFIX, first)], sem.at[0])
            getattr(cp, method)()

        @pl.when((bi == 0) & (k == 0))
        def _():
            tokens(0, False, "start")

        @pl.when(k == 0)
        def _():
            tokens(bi, True, "start")
            tokens(bi, False, "wait")
            buf[0, 0:N_PAD, :] = jnp.zeros((N_PAD, D_MODEL), F32)
            buf[0, N_PAD:PREFIX, :] = m_ref[...]

        @pl.when(k == 1)
        def _():
            @pl.when(bi + 1 < b)
            def _():
                tokens(bi + 1, False, "start")

            tokens(bi, True, "wait")

        hv = buf[k]
        h_ref[...] = hv
        r = lax.rsqrt(jnp.mean(hv * hv, axis=-1, keepdims=True) + EPS)
        n_ref[...] = ((hv * r) * g_ref[...]).astype(BF16)

    rows = pl.BlockSpec((half, D_MODEL), lambda bi, k: (2 * bi + k, 0))
    return pl.pallas_call(
        body, name=name, grid=(b, 2),
        in_specs=[pl.BlockSpec(memory_space=pl.ANY), pl.BlockSpec((N_META, D_MODEL), lambda bi, k: (0, 0)),
                  pl.BlockSpec((1, D_MODEL), lambda bi, k: (0, 0))],
        out_specs=[rows, rows],
        out_shape=[jax.ShapeDtypeStruct((2 * b * half, D_MODEL), F32), jax.ShapeDtypeStruct((2 * b * half, D_MODEL), BF16)],
        scratch_shapes=[pltpu.VMEM((2, half, D_MODEL), F32), pltpu.SemaphoreType.DMA((2,))],
        compiler_params=_params(("arbitrary", "arbitrary")),
    )(x, meta, g)


def _branch_gate_fwd(name, oa, ob, wa, wb, gates, dep=None):
    t = gates.shape[0]
    tm = _pick(t, (544, 384, 256, 128))
    has_dep = dep is not None

    def body(oa_ref, ob_ref, wa_ref, wb_ref, g_ref, *rest):
        o_ref, ya_ref, yb_ref = rest[-3:]
        ya = lax.dot_general(_bf(oa_ref[...]), wa_ref[...], _NN, preferred_element_type=F32)
        yb = lax.dot_general(_bf(ob_ref[...]), wb_ref[...], _NN, preferred_element_type=F32)
        sa = jax.nn.sigmoid(g_ref[:, 0:D_MODEL])
        sb = jax.nn.sigmoid(g_ref[:, D_MODEL:2 * D_MODEL])
        o_ref[...] = (sa * ya + sb * yb).astype(BF16)
        ya_ref[...] = ya.astype(BF16)
        yb_ref[...] = yb.astype(BF16)

    blk = pl.BlockSpec((tm, D_MODEL), lambda i: (i, 0))
    narrow = pl.BlockSpec((tm, A_WIDTH), lambda i: (i, 0))
    wide = pl.BlockSpec((tm, 2 * D_MODEL), lambda i: (i, 0))
    wspec = pl.BlockSpec((A_WIDTH, D_MODEL), lambda i: (0, 0))
    out = jax.ShapeDtypeStruct((t, D_MODEL), BF16)
    return pl.pallas_call(
        body, name=name, grid=(t // tm,),
        in_specs=[narrow, narrow, wspec, wspec, wide] + ([pl.BlockSpec(memory_space=pl.ANY)] if has_dep else []),
        out_specs=[blk, blk, blk], out_shape=[out, out, out], compiler_params=_params(("parallel",)),
    )(*((oa, ob, wa, wb, gates) + ((dep,) if has_dep else ())))


def _branch_bwd_x(name, dya, dyb, wa, wb):
    t = dya.shape[0]
    tm = _pick(t, (1088, 768, 512, 256, 128))

    def body(da_ref, db_ref, wa_ref, wb_ref, oa_ref, ob_ref):
        oa_ref[...] = lax.dot_general(da_ref[...], wa_ref[...], _NT, preferred_element_type=F32)
        ob_ref[...] = lax.dot_general(db_ref[...], wb_ref[...], _NT, preferred_element_type=F32).astype(BF16)

    blk = pl.BlockSpec((tm, D_MODEL), lambda i: (i, 0))
    narrow = pl.BlockSpec((tm, A_WIDTH), lambda i: (i, 0))
    wspec = pl.BlockSpec((A_WIDTH, D_MODEL), lambda i: (0, 0), pipeline_mode=pl.Buffered(1))
    return pl.pallas_call(
        body, name=name, grid=(t // tm,), in_specs=[blk, blk, wspec, wspec], out_specs=[narrow, narrow],
        out_shape=[jax.ShapeDtypeStruct((t, A_WIDTH), F32), jax.ShapeDtypeStruct((t, B_WIDTH), BF16)],
        compiler_params=_params(("parallel",)),
    )(dya, dyb, wa, wb)


def _branch_bwd_w(name, oa, ob, dya, dyb):
    t = oa.shape[0]
    tn = 512

    def body(oa_ref, ob_ref, da_ref, db_ref, wa_ref, wb_ref):
        wa_ref[...] = lax.dot_general(_bf(oa_ref[...]), da_ref[...], _TN, preferred_element_type=F32).astype(BF16)
        wb_ref[...] = lax.dot_general(_bf(ob_ref[...]), db_ref[...], _TN, preferred_element_type=F32).astype(BF16)

    whole = pl.BlockSpec((t, A_WIDTH), lambda j: (0, 0), pipeline_mode=pl.Buffered(1))
    cols = pl.BlockSpec((t, tn), lambda j: (0, j))
    out_spec = pl.BlockSpec((A_WIDTH, tn), lambda j: (0, j))
    out = jax.ShapeDtypeStruct((A_WIDTH, D_MODEL), BF16)
    return pl.pallas_call(
        body, name=name, grid=(D_MODEL // tn,), in_specs=[whole, whole, cols, cols], out_specs=[out_spec, out_spec],
        out_shape=[out, out], compiler_params=_params(("parallel",)),
    )(oa, ob, dya, dyb)


def _mix_out_gate_bwd(name, dh, wo, gates, ya, yb):
    t = gates.shape[0]
    tm = _pick(t, (544, 384, 256, 128))

    def body(dh_ref, w_ref, g_ref, ya_ref, yb_ref, dya_ref, dyb_ref, dg_ref):
        dm = lax.dot_general(_bf(dh_ref[...]), w_ref[...], _NT, preferred_element_type=F32)
        sa = jax.nn.sigmoid(g_ref[:, 0:D_MODEL])
        sb = jax.nn.sigmoid(g_ref[:, D_MODEL:2 * D_MODEL])
        dya_ref[...] = (dm * sa).astype(BF16)
        dyb_ref[...] = (dm * sb).astype(BF16)
        dg_ref[:, 0:D_MODEL] = (dm * ya_ref[...].astype(F32) * (sa * (1.0 - sa))).astype(BF16)
        dg_ref[:, D_MODEL:2 * D_MODEL] = (dm * yb_ref[...].astype(F32) * (sb * (1.0 - sb))).astype(BF16)

    blk = pl.BlockSpec((tm, D_MODEL), lambda i: (i, 0))
    wide = pl.BlockSpec((tm, 2 * D_MODEL), lambda i: (i, 0))
    out = jax.ShapeDtypeStruct((t, D_MODEL), BF16)
    return pl.pallas_call(
        body, name=name, grid=(t // tm,),
        in_specs=[blk, pl.BlockSpec((D_MODEL, D_MODEL), lambda i: (0, 0)), wide, blk, blk], out_specs=[blk, blk, wide],
        out_shape=[out, out, jax.ShapeDtypeStruct((t, 2 * D_MODEL), BF16)], compiler_params=_params(("parallel",)),
    )(dh, wo, gates, ya, yb)


def _ffn_out_loss(name, a, w, res, gf, tgt, alpha):
    t, k = a.shape
    b, s, _ = tgt.shape
    l = t // b
    tm = _pick(t, (544, 384, 256, 128))
    per = l // tm
    assert per * tm == l and tm > PREFIX and l - PREFIX == s

    def body(a_ref, w_ref, r_ref, g_ref, t_ref, dh_ref, loss_ref, dg_ref, buf, sem):
        i = pl.program_id(0)
        bi, r = i // per, i % per

        def first_rows():
            return pltpu.make_async_copy(t_ref.at[bi, pl.ds(0, tm - PREFIX)], buf.at[pl.ds(PREFIX, tm - PREFIX)], sem)

        def later_rows():
            return pltpu.make_async_copy(t_ref.at[bi, pl.ds(pl.multiple_of(r * tm - PREFIX, 8), tm)], buf, sem)

        @pl.when(r == 0)
        def _():
            buf[0:PREFIX, :] = jnp.zeros((PREFIX, D_MODEL), F32)
            first_rows().start()

        if per > 1:
            @pl.when(r > 0)
            def _():
                later_rows().start()

        acc = lax.dot_general(a_ref[...], w_ref[...], _NN, preferred_element_type=F32)

        @pl.when(r == 0)
        def _():
            first_rows().wait()

        if per > 1:
            @pl.when(r > 0)
            def _():
                later_rows().wait()

        hv = acc * alpha + r_ref[...]
        row = lax.broadcasted_iota(jnp.int32, (tm, 1), 0)
        real = ((r > 0) | (row >= PREFIX)).astype(F32)
        g = g_ref[...]
        rn = lax.rsqrt(jnp.mean(hv * hv, axis=-1, keepdims=True) + EPS)
        xn = hv * rn
        err = (xn * g - buf[...]) * real
        lpart = 0.5 * jnp.sum(jnp.mean(err * err, axis=-1, keepdims=True), axis=0, keepdims=True)
        dy = err * (1.0 / D_MODEL)
        tv = dy * g
        dot = jnp.mean(tv * hv, axis=-1, keepdims=True)
        dh_ref[...] = rn * tv - hv * (rn * rn * rn * dot)
        gpart = jnp.sum(dy * xn, axis=0, keepdims=True)

        @pl.when(i == 0)
        def _():
            loss_ref[...] = jnp.zeros_like(loss_ref)
            dg_ref[...] = jnp.zeros_like(dg_ref)

        loss_ref[...] += jnp.broadcast_to(lpart, loss_ref.shape)
        dg_ref[...] += gpart

    row_spec = pl.BlockSpec((tm, D_MODEL), lambda i: (i, 0))
    vec = pl.BlockSpec((1, D_MODEL), lambda i: (0, 0))
    return pl.pallas_call(
        body, name=name, grid=(t // tm,),
        in_specs=[pl.BlockSpec((tm, k), lambda i: (i, 0)),
                  pl.BlockSpec((k, D_MODEL), lambda i: (0, 0), pipeline_mode=pl.Buffered(1)), row_spec, vec,
                  pl.BlockSpec(memory_space=pl.ANY)],
        out_specs=[row_spec, pl.BlockSpec((8, 128), lambda i: (0, 0)), vec],
        out_shape=[jax.ShapeDtypeStruct((t, D_MODEL), F32), jax.ShapeDtypeStruct((8, 128), F32),
                   jax.ShapeDtypeStruct((1, D_MODEL), F32)],
        scratch_shapes=[pltpu.VMEM((tm, D_MODEL), F32), pltpu.SemaphoreType.DMA(())],
        compiler_params=_params(("arbitrary",)),
    )(a, w, res, gf, tgt)


def _fgate_fwd(name, f3, bf_row):
    b, l, _ = f3.shape
    nb = l // BLOCK

    def body(f_ref, b_ref, cc_ref, cr_ref):
        r_i = lax.broadcasted_iota(jnp.int32, (BLOCK, BLOCK), 0)
        c_i = lax.broadcasted_iota(jnp.int32, (BLOCK, BLOCK), 1)
        tri = (r_i >= c_i).astype(F32)
        carry = jnp.zeros((1, 128), F32)
        for blk in range(nb):
            rows = slice(blk * BLOCK, (blk + 1) * BLOCK)
            z = f_ref[rows, :] + b_ref[...]
            lf = jnp.minimum(z, 0.0) - jnp.log(1.0 + jnp.exp(-jnp.abs(z)))
            cb = jnp.dot(tri, lf, preferred_element_type=F32, precision=lax.Precision.HIGHEST) + carry
            carry = cb[BLOCK - 1:BLOCK, :]
            cbt = cb.T
            for hh in range(B_HEADS):
                cc_ref[hh, rows, :] = jnp.sum(jnp.where(c_i == hh, cb, 0.0), axis=1, keepdims=True)
                cr_ref[hh, :, rows] = cbt[hh:hh + 1, :]

    return pl.pallas_call(
        body, name=name, grid=(b,),
        in_specs=[pl.BlockSpec((None, l, 128), lambda bi: (bi, 0, 0)),
                  pl.BlockSpec((1, 128), lambda bi: (0, 0))],
        out_specs=[pl.BlockSpec((None, B_HEADS, l, 1), lambda bi: (bi, 0, 0, 0)),
                   pl.BlockSpec((None, B_HEADS, 1, l), lambda bi: (bi, 0, 0, 0))],
        out_shape=[jax.ShapeDtypeStruct((b, B_HEADS, l, 1), F32), jax.ShapeDtypeStruct((b, B_HEADS, 1, l), F32)],
        compiler_params=_params(("parallel",)),
    )(f3, bf_row)


def _fgate_bwd(name, f3, bf_row, dcq, dck):
    b, l, _ = f3.shape
    nb = l // BLOCK

    def body(f_ref, b_ref, dcq_ref, dck_ref, df_ref, db_ref):
        r_i = lax.broadcasted_iota(jnp.int32, (BLOCK, BLOCK), 0)
        c_i = lax.broadcasted_iota(jnp.int32, (BLOCK, BLOCK), 1)
        tri = (r_i <= c_i).astype(F32)
        carry = jnp.zeros((1, 128), F32)
        total = jnp.zeros((1, 128), F32)
        for blk in range(nb - 1, -1, -1):
            rows = slice(blk * BLOCK, (blk + 1) * BLOCK)
            krows = jnp.concatenate([dck_ref[hh, :, rows] for hh in range(B_HEADS)]
                                    + [jnp.zeros((BLOCK - B_HEADS, BLOCK), F32)], axis=0)
            dcb = krows.T
            for hh in range(B_HEADS):
                dcb = dcb + jnp.where(c_i == hh, dcq_ref[hh, rows, :], 0.0)
            rc = jnp.dot(tri, dcb, preferred_element_type=F32, precision=lax.Precision.HIGHEST) + carry
            carry = rc[0:1, :]
            z = f_ref[rows, :] + b_ref[...]
            df = rc * (1.0 / (1.0 + jnp.exp(z)))
            df_ref[rows, :] = df.astype(BF16)
            total = total + jnp.sum(df, axis=0, keepdims=True)

        @pl.when(pl.program_id(0) == 0)
        def _():
            db_ref[...] = total

        @pl.when(pl.program_id(0) > 0)
        def _():
            db_ref[...] += total

    return pl.pallas_call(
        body, name=name, grid=(b,),
        in_specs=[pl.BlockSpec((None, l, 128), lambda bi: (bi, 0, 0)),
                  pl.BlockSpec((1, 128), lambda bi: (0, 0)),
                  pl.BlockSpec((None, B_HEADS, l, 1), lambda bi: (bi, 0, 0, 0)),
                  pl.BlockSpec((None, B_HEADS, 1, l), lambda bi: (bi, 0, 0, 0))],
        out_specs=[pl.BlockSpec((None, l, 128), lambda bi: (bi, 0, 0)), pl.BlockSpec((1, 128), lambda bi: (0, 0))],
        out_shape=[jax.ShapeDtypeStruct((b, l, 128), BF16), jax.ShapeDtypeStruct((1, 128), F32)],
        compiler_params=_params(("arbitrary",)),
    )(f3, bf_row, dcq, dck)


A_Q_BLK = B_SEG // A_WIDTH
A_K_BLK = (B_SEG + A_WIDTH) // 128
A_V_BLK = A_K_BLK + 1
A_SEG_BLK = B_SEG // A_SEG
STACK = A_HEADS * BLOCK


def _lane_lo():
    return lax.broadcasted_iota(jnp.int32, (1, 128), 1) < HEAD_DIM


def _stack_heads(x, masked):
    lo = _lane_lo()
    blks = [x[:, 128 * j:128 * (j + 1)] for j in range(4)]
    if not masked:
        return jnp.concatenate(blks + blks, axis=0)
    zero = jnp.zeros_like(blks[0])
    return jnp.concatenate([jnp.where(lo, bk, zero) for bk in blks] + [jnp.where(lo, zero, bk) for bk in blks], axis=0)


def _unstack_heads(y):
    lo = _lane_lo()
    return jnp.concatenate([jnp.where(lo, y[128 * j:128 * (j + 1)], y[128 * (4 + j):128 * (5 + j)]) for j in range(4)], axis=1)


def _swa_bias(slopes):
    r_i = jnp.arange(STACK)[:, None]
    c_i = jnp.arange(3 * BLOCK)[None, :]
    seg = c_i >> 7
    out = []
    for n in range(3):
        qpos = n * BLOCK + (r_i & (BLOCK - 1))
        kpos = jnp.where(seg == 0, c_i, (n - 2) * BLOCK + c_i)
        dist = qpos - kpos
        band = (seg != 0) & (dist < BLOCK) & (kpos >= PREFIX)
        meta = (seg == 0) & (c_i >= N_PAD)
        out.append(jnp.where((dist >= 0) & (band | meta), -slopes * dist.astype(F32), NEG))
    return jnp.stack(out, axis=0)


def _swa_scores(q, kcat, n, slope, bias):
    s = lax.dot_general(q, kcat, _NT, preferred_element_type=F32) + bias
    further = slope * (-BLOCK * jnp.maximum(n - 2, 0)).astype(F32)
    return jnp.concatenate([s[:, 0:BLOCK] + further, s[:, BLOCK:]], axis=1)


def _swa_specs():
    def kv(col_blk):
        return [pl.BlockSpec((None, BLOCK, 128), lambda b, n: (b, 0, col_blk)),
                pl.BlockSpec((None, BLOCK, 128), lambda b, n: (b, jnp.maximum(n - 1, 0), col_blk)),
                pl.BlockSpec((None, BLOCK, 128), lambda b, n: (b, n, col_blk))]

    q_spec = pl.BlockSpec((None, BLOCK, A_WIDTH), lambda b, n: (b, n, A_Q_BLK))
    o_spec = pl.BlockSpec((None, BLOCK, A_WIDTH), lambda b, n: (b, n, 0))
    col = pl.BlockSpec((STACK, 1), lambda b, n: (0, 0))
    bias = pl.BlockSpec((None, STACK, 3 * BLOCK), lambda b, n: (jnp.minimum(n, 2), 0, 0))
    lse_spec = pl.BlockSpec((None, A_HEADS, BLOCK, 1), lambda b, n: (b, 0, n, 0))
    return q_spec, kv(A_K_BLK), kv(A_V_BLK), o_spec, [col, col, bias], lse_spec


def _swa_fwd(name, qkv, slopes, sinks, bias):
    b, l, _ = qkv.shape
    nb = l // BLOCK

    def body(q_ref, k0_ref, kp_ref, kc_ref, v0_ref, vp_ref, vc_ref, sl_ref, sk_ref, bias_ref, o_ref, lse_ref):
        n = pl.program_id(1)
        qs = _stack_heads(q_ref[...], True) * SCALE
        kcat = jnp.concatenate([k0_ref[...], kp_ref[...], kc_ref[...]], axis=0)
        vcat = jnp.concatenate([v0_ref[...], vp_ref[...], vc_ref[...]], axis=0)
        s = _swa_scores(qs, kcat, n, sl_ref[...], bias_ref[...])
        sink = sk_ref[...]
        m = jnp.maximum(jnp.max(s, axis=-1, keepdims=True), sink)
        p = jnp.exp(s - m)
        den = jnp.sum(p, axis=-1, keepdims=True) + jnp.exp(sink - m)
        o = lax.dot_general(p.astype(BF16), vcat, _NN, preferred_element_type=F32) / den
        o_ref[...] = _unstack_heads(o)
        lse_ref[...] = (m + jnp.log(den)).reshape(A_HEADS, BLOCK, 1)

    q_spec, k_specs, v_specs, o_spec, consts, lse_spec = _swa_specs()
    return pl.pallas_call(
        body, name=name, grid=(b, nb),
        in_specs=[q_spec] + k_specs + v_specs + consts, out_specs=[o_spec, lse_spec],
        out_shape=[jax.ShapeDtypeStruct((b, l, A_WIDTH), F32), jax.ShapeDtypeStruct((b, A_HEADS, l, 1), F32)],
        compiler_params=_params(("parallel", "parallel")),
    )(qkv, qkv, qkv, qkv, qkv, qkv, qkv, slopes, sinks, bias)


def _swa_bwd(name, qkv, o, lse, do, slopes, sinks, bias, dqkv):
    b, l, _ = qkv.shape
    nb = l // BLOCK

    def body(q_ref, k0_ref, kp_ref, kc_ref, v0_ref, vp_ref, vc_ref, o_ref, lse_ref, do_ref, sl_ref, sk_ref, bias_ref, _,
             dx_ref, ds_ref, dk_acc, dv_acc):
        bi = pl.program_id(0)
        n = pl.program_id(1)
        qs = _stack_heads(q_ref[...], True) * SCALE
        dos32 = _stack_heads(do_ref[...], True)
        dos = dos32.astype(BF16)
        os_ = _stack_heads(o_ref[...], False)
        lsev = lse_ref[...].reshape(STACK, 1)
        kcat = jnp.concatenate([k0_ref[...], kp_ref[...], kc_ref[...]], axis=0)
        vcat = jnp.concatenate([v0_ref[...], vp_ref[...], vc_ref[...]], axis=0)
        s = _swa_scores(qs, kcat, n, sl_ref[...], bias_ref[...])
        p = jnp.exp(s - lsev)
        dsum = jnp.sum(dos32 * os_, axis=-1, keepdims=True)
        dp = lax.dot_general(dos, vcat, _NT, preferred_element_type=F32)
        dsc = (p * (dp - dsum)).astype(BF16)
        dq = lax.dot_general(dsc, kcat, _NN, preferred_element_type=F32) * SCALE
        row0 = pl.multiple_of(n * BLOCK, BLOCK)
        dx_ref[pl.ds(row0, BLOCK), 0:A_WIDTH] = _unstack_heads(dq).astype(BF16)
        dkc = lax.dot_general(dsc, qs, _TN, preferred_element_type=F32)
        dvc = lax.dot_general(p.astype(BF16), dos, _TN, preferred_element_type=F32)

        @pl.when(n == 0)
        def _():
            dk_acc[...] = jnp.zeros_like(dk_acc)
            dv_acc[...] = jnp.zeros_like(dv_acc)

        starts = (0, pl.multiple_of(jnp.maximum(n - 1, 0) * BLOCK, BLOCK), row0)
        for t, st in enumerate(starts):
            dk_acc[pl.ds(st, BLOCK), :] += dkc[t * BLOCK:(t + 1) * BLOCK, :]
            dv_acc[pl.ds(st, BLOCK), :] += dvc[t * BLOCK:(t + 1) * BLOCK, :]

        @pl.when(n == nb - 1)
        def _():
            dx_ref[:, A_WIDTH:A_WIDTH + 128] = dk_acc[...].astype(BF16)
            dx_ref[:, A_WIDTH + 128:A_SEG] = dv_acc[...].astype(BF16)

        dsink = -(jnp.exp(sk_ref[...] - lsev) * dsum)
        r8 = lax.broadcasted_iota(jnp.int32, (8, 128), 0)
        acc = jnp.zeros((8, 128), F32)
        for hh in range(A_HEADS):
            acc = acc + jnp.where(r8 == hh, jnp.sum(dsink[hh * BLOCK:(hh + 1) * BLOCK, :]), 0.0)

        @pl.when((bi == 0) & (n == 0))
        def _():
            ds_ref[...] = jnp.zeros_like(ds_ref)

        ds_ref[...] += acc

    q_spec, k_specs, v_specs, o_spec, consts, lse_spec = _swa_specs()
    return pl.pallas_call(
        body, name=name, grid=(b, nb),
        in_specs=[q_spec] + k_specs + v_specs + [o_spec, lse_spec, o_spec] + consts + [pl.BlockSpec(memory_space=pl.ANY)],
        out_specs=[pl.BlockSpec((None, l, A_SEG), lambda bb, n: (bb, 0, A_SEG_BLK)),
                   pl.BlockSpec((8, 128), lambda bb, n: (0, 0))],
        out_shape=[jax.ShapeDtypeStruct(dqkv.shape, BF16), jax.ShapeDtypeStruct((8, 128), F32)],
        scratch_shapes=[pltpu.VMEM((l, 128), F32), pltpu.VMEM((l, 128), F32)],
        input_output_aliases={13: 0},
        compiler_params=_params(("arbitrary", "arbitrary")),
    )(qkv, qkv, qkv, qkv, qkv, qkv, qkv, o, lse, do, slopes, sinks, bias, dqkv)


def _fox_mask(qk, ck, i):
    kh = qk.shape[1]
    qpos = i * BLOCK + lax.broadcasted_iota(jnp.int32, (BLOCK, kh), 0)
    kpos = lax.broadcasted_iota(jnp.int32, (BLOCK, kh), 1)
    return jnp.where((kpos <= qpos) & (kpos >= N_PAD), qk - ck, NEG)


def _pick_head(x, hh):
    lo = _lane_lo()
    return jnp.where(lo if hh == 0 else jnp.logical_not(lo), x, jnp.zeros_like(x))


def _both_heads(x):
    return jnp.concatenate([_pick_head(x, 0), _pick_head(x, 1)], axis=0)


def _fox_specs(l):
    pair = pl.BlockSpec((None, l, PAIR_W), lambda bi, hp: (bi, 0, hp))
    half = pl.BlockSpec((None, l, 128), lambda bi, hp: (bi, 0, hp))
    colv = pl.BlockSpec((None, 2, l, 1), lambda bi, hp: (bi, hp, 0, 0))
    rowv = pl.BlockSpec((None, 2, 1, l), lambda bi, hp: (bi, hp, 0, 0))
    return pair, half, colv, rowv


def _fox_fwd(name, qkv, c_col, c_row):
    b, l, _ = qkv.shape
    nb = l // BLOCK

    def body(x_ref, cc_ref, cr_ref, o_ref, lse_ref):
        for i in range(nb):
            rows = slice(i * BLOCK, (i + 1) * BLOCK)
            kh = (i + 1) * BLOCK
            qblk = x_ref[rows, 0:128]
            kv = x_ref[0:kh, 128:256]
            vv = x_ref[0:kh, 256:384]
            qk = lax.dot_general(_both_heads(qblk) * SCALE, kv, _NT, preferred_element_type=F32)
            ps, dens = [], []
            for hh in range(2):
                s = _fox_mask(qk[hh * BLOCK:(hh + 1) * BLOCK], cr_ref[hh, :, 0:kh], i)
                m = jnp.max(s, axis=-1, keepdims=True)
                p = jnp.exp(s - m)
                den = jnp.sum(p, axis=-1, keepdims=True)
                ps.append(p.astype(BF16))
                dens.append(den)
                lse_ref[hh, rows, :] = (m + jnp.log(den)) + cc_ref[hh, rows, :]
            pv = lax.dot_general(jnp.concatenate(ps, axis=0), vv, _NN, preferred_element_type=F32)
            o_ref[rows, :] = jnp.where(_lane_lo(), pv[0:BLOCK] / dens[0], pv[BLOCK:2 * BLOCK] / dens[1]).astype(BF16)

    pair, half, colv, rowv = _fox_specs(l)
    return pl.pallas_call(
        body, name=name, grid=(b, 4), in_specs=[pair, colv, rowv], out_specs=[half, colv],
        out_shape=[jax.ShapeDtypeStruct((b, l, B_WIDTH), BF16), jax.ShapeDtypeStruct((b, B_HEADS, l, 1), F32)],
        compiler_params=_params(("parallel", "parallel")),
    )(qkv, c_col, c_row)


def _fox_bwd(name, qkv, c_col, c_row, o, lse, do):
    b, l, _ = qkv.shape
    nb = l // BLOCK

    def body(x_ref, cc_ref, cr_ref, o_ref, lse_ref, do_ref, dx_ref, dcq_ref, dck_ref, dk_acc, dv_acc):
        dk_acc[...] = jnp.zeros_like(dk_acc)
        dv_acc[...] = jnp.zeros_like(dv_acc)
        dck_ref[...] = jnp.zeros_like(dck_ref)
        for i in range(nb):
            rows = slice(i * BLOCK, (i + 1) * BLOCK)
            kh = (i + 1) * BLOCK
            qblk = x_ref[rows, 0:128]
            kv = x_ref[0:kh, 128:256]
            vv = x_ref[0:kh, 256:384]
            doblk = do_ref[rows, :]
            ov = o_ref[rows, :].astype(F32)
            q2 = _both_heads(qblk) * SCALE
            do2 = _both_heads(doblk)
            qk = lax.dot_general(q2, kv, _NT, preferred_element_type=F32)
            dp = lax.dot_general(do2, vv, _NT, preferred_element_type=F32)
            ps, dss = [], []
            for hh in range(2):
                half = slice(hh * BLOCK, (hh + 1) * BLOCK)
                s = _fox_mask(qk[half], cr_ref[hh, :, 0:kh], i)
                p = jnp.exp(s - (lse_ref[hh, rows, :] - cc_ref[hh, rows, :]))
                dsum = jnp.sum(do2[half].astype(F32) * ov, axis=-1, keepdims=True)
                ds = p * (dp[half] - dsum)
                ps.append(p.astype(BF16))
                dss.append(ds.astype(BF16))
                dcq_ref[hh, rows, :] = jnp.sum(ds, axis=-1, keepdims=True)
                dck_ref[hh, :, 0:kh] -= jnp.sum(ds, axis=0, keepdims=True)
            p2 = jnp.concatenate(ps, axis=0)
            ds2 = jnp.concatenate(dss, axis=0)
            dq = lax.dot_general(ds2, kv, _NN, preferred_element_type=F32) * SCALE
            dk_acc[0:kh, :] += lax.dot_general(ds2, q2, _TN, preferred_element_type=F32)
            dv_acc[0:kh, :] += lax.dot_general(p2, do2, _TN, preferred_element_type=F32)
            dx_ref[rows, 0:128] = jnp.where(_lane_lo(), dq[0:BLOCK], dq[BLOCK:2 * BLOCK]).astype(BF16)
        dx_ref[:, 128:256] = dk_acc[...].astype(BF16)
        dx_ref[:, 256:384] = dv_acc[...].astype(BF16)

    pair, half, colv, rowv = _fox_specs(l)
    return pl.pallas_call(
        body, name=name, grid=(b, 4), in_specs=[pair, colv, rowv, half, colv, half],
        out_specs=[pair, colv, rowv],
        out_shape=[jax.ShapeDtypeStruct(qkv.shape, BF16), jax.ShapeDtypeStruct((b, B_HEADS, l, 1), F32),
                   jax.ShapeDtypeStruct((b, B_HEADS, 1, l), F32)],
        scratch_shapes=[pltpu.VMEM((l, 128), F32), pltpu.VMEM((l, 128), F32)],
        compiler_params=_params(("parallel", "parallel")),
    )(qkv, c_col, c_row, o, lse, do)


_FLIPS = ((0, 0, 1), (0, 1, 0), (0, 1, 1), (1, 0, 0), (1, 0, 1), (1, 1, 0), (1, 1, 1))


def _peer_table():
    x, y, c = lax.axis_index("x"), lax.axis_index("y"), lax.axis_index("c")
    me = 4 * x + 2 * y + c
    peers = []
    for fx, fy, fc in _FLIPS:
        px = 1 - x if fx else x
        py = 1 - y if fy else y
        pc = 1 - c if fc else c
        peers.append(((px, py, pc), 4 * px + 2 * py + pc))
    return me, peers


_HBM = pl.BlockSpec(memory_space=pltpu.HBM)
_SEM = pl.BlockSpec(memory_space=pltpu.SEMAPHORE)
_ANY = pl.BlockSpec(memory_space=pl.ANY)
_EFFECT = pltpu.SideEffectType.DATAFLOW_SIDE_EFFECTING


def _split_copy(srcs_are_pieces, src_refs, land_refs, send_sem, recv_sem, a, kk, me, peers, arriving):
    dev, lin = peers[kk]
    npeer = len(_FLIPS)
    src = src_refs[a] if srcs_are_pieces[a] else src_refs[a].at[lin]
    dst = land_refs[a].at[lin] if arriving else land_refs[a].at[me]
    return pltpu.make_async_remote_copy(src_ref=src, dst_ref=dst, send_sem=send_sem.at[a * npeer + kk],
                                        recv_sem=recv_sem.at[a * npeer + kk], device_id=dev, device_id_type=MESH_ID)


def _xchg_start(name, gather, scatter, after=None):
    me_out = 4 * lax.axis_index("x") + 2 * lax.axis_index("y") + lax.axis_index("c")
    srcs = list(gather) + list(scatter)
    is_piece = [True] * len(gather) + [False] * len(scatter)
    lands = []
    for a, piece in zip(srcs, is_piece):
        own = a[None] if piece else lax.dynamic_slice_in_dim(a, me_out, 1, axis=0)
        shape = ((N_DEV,) + tuple(a.shape)) if piece else tuple(a.shape)
        start = (me_out,) + (0,) * (len(shape) - 1)
        lands.append(lax.dynamic_update_slice(lax.empty(shape, a.dtype), own, start))
    n = len(srcs)
    nsem = n * len(_FLIPS)
    has_after = after is not None

    def body(*refs):
        src_refs = refs[:n]
        land_refs = refs[n:2 * n]
        outs = refs[2 * n + (1 if has_after else 0):]
        send_sem, recv_sem = outs[0], outs[1]
        token = outs[-1]
        me, peers = _peer_table()
        for kk in range(len(_FLIPS)):
            for a in range(n):
                _split_copy(is_piece, src_refs, land_refs, send_sem, recv_sem, a, kk, me, peers, False).start()
        token[...] = jnp.zeros_like(token)

    out_shape = ([pltpu.SemaphoreType.DMA((nsem,)), pltpu.SemaphoreType.DMA((nsem,))]
                 + [pltpu.HBM(tuple(a.shape), a.dtype) for a in srcs] + [pltpu.HBM(tuple(a.shape), a.dtype) for a in lands]
                 + [jax.ShapeDtypeStruct((8, 128), F32)])
    args = [pltpu.with_memory_space_constraint(a, pltpu.HBM) for a in srcs + lands] + ([after] if has_after else [])
    res = pl.pallas_call(
        body, name=name, out_shape=out_shape,
        in_specs=[_HBM] * (2 * n) + ([_ANY] if has_after else []),
        out_specs=[_SEM, _SEM] + [_HBM] * (2 * n) + [pl.BlockSpec(memory_space=pltpu.VMEM)],
        input_output_aliases={i: 2 + i for i in range(2 * n)},
        compiler_params=pltpu.CompilerParams(has_side_effects=_EFFECT),
    )(*args)
    state = (res[0], res[1], list(res[2:2 + n]), list(res[2 + n:2 + 2 * n]), is_piece)
    return state, res[-1]


def _xchg_wait(name, state, after):
    send_sem, recv_sem, srcs, lands, is_piece = state
    n = len(srcs)

    def body(*refs):
        src_refs = refs[:n]
        land_refs = refs[n:2 * n]
        s_sem, r_sem = refs[2 * n], refs[2 * n + 1]
        me, peers = _peer_table()
        for kk in range(len(_FLIPS)):
            for a in range(n):
                cp = _split_copy(is_piece, src_refs, land_refs, s_sem, r_sem, a, kk, me, peers, True)
                cp.wait_send()
                cp.wait_recv()

    out_shape = [pltpu.HBM(tuple(a.shape), a.dtype) for a in srcs] + [pltpu.HBM(tuple(a.shape), a.dtype) for a in lands]
    res = pl.pallas_call(
        body, name=name, out_shape=out_shape,
        in_specs=[_HBM] * (2 * n) + [_SEM, _SEM, _ANY], out_specs=[_HBM] * (2 * n),
        input_output_aliases={i: i for i in range(2 * n)},
        compiler_params=pltpu.CompilerParams(has_side_effects=_EFFECT),
    )(*srcs, *lands, send_sem, recv_sem, after)
    return list(res[n:])


_SIB = (0, 0, 1)
_ICI = ((0, 1, 0), (1, 0, 0), (1, 1, 0))


def _flip(fl):
    x, y, c = lax.axis_index("x"), lax.axis_index("y"), lax.axis_index("c")
    px = 1 - x if fl[0] else x
    py = 1 - y if fl[1] else y
    pc = 1 - c if fl[2] else c
    return (px, py, pc), 4 * px + 2 * py + pc


def _gather2_start(name, pieces, after=None):
    me_out = 4 * lax.axis_index("x") + 2 * lax.axis_index("y") + lax.axis_index("c")
    pieces = list(pieces)
    n = len(pieces)
    lands = [lax.dynamic_update_slice(lax.empty((N_DEV,) + tuple(a.shape), a.dtype), a[None],
                                      (me_out,) + (0,) * a.ndim) for a in pieces]
    first = (_SIB,) + _ICI
    has_after = after is not None

    def body(*refs):
        src_refs, land_refs = refs[:n], refs[n:2 * n]
        outs = refs[2 * n + (1 if has_after else 0):]
        send_sem, recv_sem, token = outs[0], outs[1], outs[-1]
        _, me = _flip((0, 0, 0))
        for kk, fl in enumerate(first):
            dev, _ = _flip(fl)
            for a in range(n):
                pltpu.make_async_remote_copy(src_ref=src_refs[a], dst_ref=land_refs[a].at[me],
                                             send_sem=send_sem.at[a * 4 + kk], recv_sem=recv_sem.at[a * 4 + kk],
                                             device_id=dev, device_id_type=MESH_ID).start()
        token[...] = jnp.zeros_like(token)

    hbm = [pltpu.HBM(tuple(a.shape), a.dtype) for a in pieces + lands]
    res = pl.pallas_call(
        body, name=name,
        out_shape=[pltpu.SemaphoreType.DMA((4 * n,)), pltpu.SemaphoreType.DMA((4 * n,))] + hbm
        + [jax.ShapeDtypeStruct((8, 128), F32)],
        in_specs=[_HBM] * (2 * n) + ([_ANY] if has_after else []),
        out_specs=[_SEM, _SEM] + [_HBM] * (2 * n) + [pl.BlockSpec(memory_space=pltpu.VMEM)],
        input_output_aliases={i: 2 + i for i in range(2 * n)},
        compiler_params=pltpu.CompilerParams(has_side_effects=_EFFECT),
    )(*([pltpu.with_memory_space_constraint(a, pltpu.HBM) for a in pieces + lands] + ([after] if has_after else [])))
    return (res[0], res[1], list(res[2:2 + n]), list(res[2 + n:2 + 2 * n])), res[-1]


def _gather2_forward(name, state, after):
    send_a, recv_a, pieces, lands = state
    n = len(pieces)
    first = (_SIB,) + _ICI

    def body(*refs):
        src_refs, land_refs = refs[:n], refs[n:2 * n]
        s_a, r_a = refs[2 * n], refs[2 * n + 1]
        outs = refs[2 * n + 3:]
        send_b, recv_b, token = outs[0], outs[1], outs[-1]
        for kk, fl in enumerate(first):
            dev, lin = _flip(fl)
            for a in range(n):
                cp = pltpu.make_async_remote_copy(src_ref=src_refs[a], dst_ref=land_refs[a].at[lin],
                                                  send_sem=s_a.at[a * 4 + kk], recv_sem=r_a.at[a * 4 + kk],
                                                  device_id=dev, device_id_type=MESH_ID)
                cp.wait_send()
                cp.wait_recv()
        sib, _ = _flip(_SIB)
        for j, fl in enumerate(_ICI):
            _, lin = _flip(fl)
            for a in range(n):
                pltpu.make_async_remote_copy(src_ref=land_refs[a].at[lin], dst_ref=land_refs[a].at[lin],
                                             send_sem=send_b.at[a * 3 + j], recv_sem=recv_b.at[a * 3 + j],
                                             device_id=sib, device_id_type=MESH_ID).start()
        token[...] = jnp.zeros_like(token)

    hbm = [pltpu.HBM(tuple(a.shape), a.dtype) for a in pieces + lands]
    res = pl.pallas_call(
        body, name=name,
        out_shape=[pltpu.SemaphoreType.DMA((3 * n,)), pltpu.SemaphoreType.DMA((3 * n,))] + hbm
        + [jax.ShapeDtypeStruct((8, 128), F32)],
        in_specs=[_HBM] * (2 * n) + [_SEM, _SEM, _ANY],
        out_specs=[_SEM, _SEM] + [_HBM] * (2 * n) + [pl.BlockSpec(memory_space=pltpu.VMEM)],
        input_output_aliases={i: 2 + i for i in range(2 * n)},
        compiler_params=pltpu.CompilerParams(has_side_effects=_EFFECT),
    )(*pieces, *lands, send_a, recv_a, after)
    return (res[0], res[1], list(res[2 + n:2 + 2 * n])), res[-1]


def _gather2_wait(name, state, after):
    send_b, recv_b, lands = state
    n = len(lands)

    def body(*refs):
        land_refs = refs[:n]
        s_b, r_b = refs[n], refs[n + 1]
        sib, _ = _flip(_SIB)
        for j, fl in enumerate(_ICI):
            _, sent = _flip(fl)
            _, arriving = _flip((fl[0], fl[1], 1))
            for a in range(n):
                cp = pltpu.make_async_remote_copy(src_ref=land_refs[a].at[sent], dst_ref=land_refs[a].at[arriving],
                                                  send_sem=s_b.at[a * 3 + j], recv_sem=r_b.at[a * 3 + j],
                                                  device_id=sib, device_id_type=MESH_ID)
                cp.wait_send()
                cp.wait_recv()

    res = pl.pallas_call(
        body, name=name, out_shape=[pltpu.HBM(tuple(a.shape), a.dtype) for a in lands],
        in_specs=[_HBM] * n + [_SEM, _SEM, _ANY], out_specs=[_HBM] * n,
        input_output_aliases={i: i for i in range(n)},
        compiler_params=pltpu.CompilerParams(has_side_effects=_EFFECT),
    )(*lands, send_b, recv_b, after)
    return list(res)


def _adam_math(w, g, m, v):
    m = ADAM_B1 * m + (1.0 - ADAM_B1) * g
    v = ADAM_B2 * v + (1.0 - ADAM_B2) * (g * g)
    m_hat = m / (1.0 - ADAM_B1 ** ADAM_STEP)
    v_hat = v / (1.0 - ADAM_B2 ** ADAM_STEP)
    delta = -ADAM_LR * (m_hat / (jnp.sqrt(v_hat) + ADAM_EPS) + ADAM_WD * w)
    return delta, m, v


def _adam(name, w, m, v, parts, transposed=False, dep=None):
    npart, _, cp = parts.shape
    has_dep = dep is not None
    if transposed:
        c, r = w.shape
        tr = _pick(r, (256, 128))
        blk = pl.BlockSpec((c, tr), lambda i: (0, i))
    else:
        r, c = w.shape
        tr = _pick(r, (256, 176, 128, 64, 16, 8, 1))
        blk = pl.BlockSpec((tr, c), lambda i: (i, 0))

    def body(w_ref, m_ref, v_ref, p_ref, *rest):
        g_ref, d_ref, mo_ref, vo_ref = rest[-4:]
        g = p_ref[0].astype(F32)
        for pp in range(1, npart):
            g = g + p_ref[pp].astype(F32)
        g = g.T[0:c, :] if transposed else g[:, 0:c]
        delta, mn, vn = _adam_math(w_ref[...], g, m_ref[...], v_ref[...])
        g_ref[...] = g
        d_ref[...] = delta
        mo_ref[...] = mn
        vo_ref[...] = vn

    out = jax.ShapeDtypeStruct(w.shape, F32)
    return pl.pallas_call(
        body, name=name, grid=(r // tr,),
        in_specs=[blk, blk, blk, pl.BlockSpec((npart, tr, cp), lambda i: (0, i, 0))]
        + ([pl.BlockSpec(memory_space=pl.ANY)] if has_dep else []),
        out_specs=[blk, blk, blk, blk], out_shape=[out, out, out, out], compiler_params=_params(("parallel",)),
    )(*((w, m, v, parts) + ((dep,) if has_dep else ())))


def _small_update(name, packs, wmv):
    nparam = len(wmv)

    def body(p_ref, *refs):
        ins, outs = refs[:3 * nparam], refs[3 * nparam:]
        tot = p_ref[0]
        for pp in range(1, N_DEV):
            tot = tot + p_ref[pp]
        outs[0][0:8, :] = tot[0:8, :]
        outs[0][8:24, :] = tot[8:24, :] + tot[24:40, :]
        grads = [tot[i:i + 1, :] for i in range(4)] + [tot[4:5, 0:B_HEADS], tot[4:5, B_HEADS:2 * B_HEADS]]
        for i, g in enumerate(grads):
            w_ref, m_ref, v_ref = ins[3 * i:3 * i + 3]
            delta, mn, vn = _adam_math(w_ref[...], g, m_ref[...], v_ref[...])
            for o_ref, val in zip(outs[1 + 4 * i:5 + 4 * i], (g, delta, mn, vn)):
                o_ref[...] = val

    flat = [a for trio in wmv for a in trio]
    out_shape = [jax.ShapeDtypeStruct((24, D_MODEL), F32)]
    for w, _, _ in wmv:
        out_shape += [jax.ShapeDtypeStruct(w.shape, F32)] * 4
    res = pl.pallas_call(body, name=name, out_shape=out_shape, compiler_params=_params())(packs, *flat)
    return res[0], [tuple(res[1 + 4 * i:5 + 4 * i]) for i in range(nparam)]


def _local_step(x, tgt, g1, gm, g2, gf, b_forget, sinks, weights, send):
    b, s, _ = x.shape
    l = s + PREFIX
    t = b * l
    (meta,) = weights("meta", x)

    h0, n1 = _embed_norm("embed_rms1_fwd", x, meta, g1)
    (w1i,) = weights("ffn1_in", n1)
    gu1, a1 = _ffn_in_fwd("ffn1_in_fwd", n1, w1i)
    (w1o,) = weights("ffn1_out", weights("ffn1_out:forward", a1))
    h1, um = _mm_res_norm("ffn1_out_fwd", a1, w1o, h0, gm, alpha=0.5)
    wi, wa, wb, wo = weights("mix", weights("mix:forward", um))
    qkv, gates, f2 = _proj_fwd("proj_fwd", um, wi)
    qkv3 = qkv.reshape(b, l, QKV_W)
    f3 = f2.reshape(b, l, 128)
    bf_row = jnp.pad(b_forget, ((0, 0), (0, 128 - B_HEADS)))
    c_col, c_row = _fgate_fwd("fgate_fwd", f3, bf_row)
    head_of_row = jnp.arange(STACK) // BLOCK
    slopes = jnp.exp2(-8.0 * (head_of_row + 1).astype(F32) / A_HEADS).reshape(STACK, 1)
    sink_rows = jnp.repeat(sinks.reshape(A_HEADS), BLOCK).reshape(STACK, 1)
    swa_bias = _swa_bias(slopes)
    oa3, lse_a = _swa_fwd("swa_fwd", qkv3, slopes, sink_rows, swa_bias)
    ob3, lse_b = _fox_fwd("fox_fwd", qkv3, c_col, c_row)
    oa = oa3.reshape(t, A_WIDTH)
    ob = ob3.reshape(t, B_WIDTH)
    mixed, ya, yb = _branch_gate_fwd("branch_gate_fwd", oa, ob, wa, wb, gates, dep=weights("ffn2:forward", ob))
    h2, n2 = _mm_res_norm("mix_out_fwd", mixed, wo, h1, g2)
    w2i, w2o = weights("ffn2", h2)
    gu2, a2 = _ffn_in_fwd("ffn2_in_fwd", n2, w2i)

    dh3, loss_blk, dgf = _ffn_out_loss("ffn2_out_loss", a2, w2o, h2, gf, tgt, 0.5)

    def ffn_bwd(tag, dh, h_in, g_norm, n_in, gu, a, w_in_blk, w_out, one_send, examples=None):
        dw_out = _mm_tn(tag + "_out_bwd_w", a, dh, alpha=0.5)
        dgu = _ffn_out_bwd_x(tag + "_out_bwd_x", dh, w_out, gu, dep=None if one_send else send(tag + "_out", (dw_out,)))
        dw_in = _ffn_in_bwd_w(tag + "_in_bwd_w", n_in, dgu)
        token = send(tag, (dw_in, dw_out)) if one_send else send(tag + "_in", (dw_in,))
        return _ffn_in_bwd_x(tag + "_in_bwd_x", dgu, w_in_blk, h_in, g_norm, dh, dep=token, examples=examples)

    dh2, dg2 = ffn_bwd("ffn2", dh3, h2, g2, n2, gu2, a2, w2i, w2o, True)

    dwo = _mm_tn("mix_out_bwd_w", mixed, dh2)
    dya, dyb, dgates = _mix_out_gate_bwd("mix_out_gate_bwd", dh2, wo, gates, ya, yb)
    doa, dob = _branch_bwd_x("branch_bwd_x", dya, dyb, wa, wb)
    dwa, dwb = _branch_bwd_w("branch_bwd_w", oa, ob, dya, dyb)
    dqkv3, dcq, dck = _fox_bwd("fox_bwd", qkv3, c_col, c_row, ob3, lse_b, dob.reshape(b, l, B_WIDTH))
    dqkv3, dsink = _swa_bwd("swa_bwd", qkv3, oa3, lse_a, doa.reshape(b, l, A_WIDTH), slopes, sink_rows, swa_bias, dqkv3)
    dqkv = dqkv3.reshape(t, QKV_W)
    df3, dbf = _fgate_bwd("fgate_bwd", f3, bf_row, dcq, dck)
    df = df3.reshape(t, 128)
    dwi_qkv = _mm_tn("proj_qkv_bwd_w", um, dqkv, tm_c=(512,), tn_c=(768,))
    dwi_g = _mm_tn("proj_gates_bwd_w", um, dgates, tm_c=(512,), tn_c=(512,))
    dwi_f = _mm_tn("proj_f_bwd_w", um, df, tm_c=(512,), tn_c=(128,))
    token = send("mix", (dwi_qkv, dwi_g, dwi_f, dwa, dwb, dwo))
    dh1, dgm = _proj_bwd_x("proj_bwd_x", dqkv, dgates, df, wi, h1, gm, dh2, dep=token)

    grad_x, dmeta3, dg1 = ffn_bwd("ffn1", dh1, h0, g1, n1, gu1, a1, w1i, w1o, False, examples=b)
    dmeta = dmeta3.reshape(b * N_META, D_MODEL)

    misc = jnp.concatenate([dbf[:, 0:B_HEADS], dsink[:, 0].reshape(1, A_HEADS), loss_blk[0:1, 0:1]], axis=1)
    misc = jnp.pad(misc, ((0, 0), (0, D_MODEL - misc.shape[1])))
    row = lax.broadcasted_iota(jnp.int32, (8, D_MODEL), 0)
    vec = jnp.zeros((8, D_MODEL), F32)
    for i, piece in enumerate((dg1, dgm, dg2, dgf, misc)):
        vec = jnp.where(row == i, piece, vec)
    small = jnp.concatenate([vec, dmeta], axis=0)
    return grad_x, small


def _pad_to(a, rows, cols):
    return jnp.pad(a, ((0, rows - a.shape[0]), (0, cols - a.shape[1])))


def _ffn_out_from_gathered(g):
    w = g.reshape(4, FF_SHARD, D_MODEL)
    return jnp.pad(w, ((0, 0), (0, FF_SHARD_P - FF_SHARD), (0, 0))).reshape(D_FF_P, D_MODEL)


def _ffn_out_to_scatter(dw):
    return dw.reshape(4, FF_SHARD_P, D_MODEL)[:, 0:FF_SHARD, :].reshape(N_DEV, FFO_SHARD, D_MODEL)


def _proj_segments():
    segs = [(HEAD_DIM * h, HEAD_DIM, 0, B_SEG + HEAD_DIM * A_HEAD_ORDER.index(h)) for h in range(A_HEADS)]
    segs += [(512, 128, 0, B_SEG + A_WIDTH), (640, 128, 0, B_SEG + A_WIDTH + 128)]
    for first, off in ((768, 0), (1280, 128), (1792, 256)):
        segs += [(first + 128 * hp, 128, 0, PAIR_W * hp + off) for hp in range(4)]
    segs += [(2304, B_HEADS, 2, 0), (2312, 2 * D_MODEL, 1, 0)]
    return segs


_RELAYOUT_ROWS = 256


def _proj_from_gathered(name, g):
    rows = _RELAYOUT_ROWS

    def body(g_ref, o_ref):
        def cols(first, width):
            out = []
            for p in range(N_DEV):
                lo, hi = max(first, WIN_SHARD * p), min(first + width, WIN_SHARD * (p + 1))
                if lo < hi:
                    out.append(g_ref[p, :, lo - WIN_SHARD * p:hi - WIN_SHARD * p])
            return out

        parts = []
        for arr in (0, 1, 2):
            for first, width, _, _ in sorted((s for s in _proj_segments() if s[2] == arr), key=lambda s: s[3]):
                parts += cols(first, width)
            if arr == 0:
                parts.append(jnp.zeros((rows, P_GATES - QKV_W), BF16))
        parts.append(jnp.zeros((rows, 128 - B_HEADS), BF16))
        o_ref[...] = jnp.concatenate(parts, axis=1)

    return pl.pallas_call(
        body, name=name, grid=(D_MODEL // rows,),
        in_specs=[pl.BlockSpec((N_DEV, rows, WIN_SHARD_P), lambda i: (0, i, 0))],
        out_specs=pl.BlockSpec((rows, PROJ_P), lambda i: (i, 0)),
        out_shape=jax.ShapeDtypeStruct((D_MODEL, PROJ_P), BF16), compiler_params=_params(("parallel",)),
    )(g)


def _proj_to_scatter(name, dqkv_w, dg_w, df_w):
    rows = _RELAYOUT_ROWS
    segs = sorted(_proj_segments())

    def body(q_ref, g_ref, f_ref, o_ref):
        arrays = (q_ref, g_ref, f_ref)
        for p in range(N_DEV):
            parts = []
            for first, width, arr, at in segs:
                lo, hi = max(first, WIN_SHARD * p), min(first + width, WIN_SHARD * (p + 1))
                if lo < hi:
                    parts.append(arrays[arr][:, at + lo - first:at + hi - first])
            parts.append(jnp.zeros((rows, WIN_SHARD_P - WIN_SHARD), BF16))
            o_ref[p] = jnp.concatenate(parts, axis=1)

    return pl.pallas_call(
        body, name=name, grid=(D_MODEL // rows,),
        in_specs=[pl.BlockSpec((rows, QKV_W), lambda i: (i, 0)), pl.BlockSpec((rows, 2 * D_MODEL), lambda i: (i, 0)),
                  pl.BlockSpec((rows, 128), lambda i: (i, 0))],
        out_specs=pl.BlockSpec((N_DEV, rows, WIN_SHARD_P), lambda i: (0, i, 0)),
        out_shape=jax.ShapeDtypeStruct((N_DEV, D_MODEL, WIN_SHARD_P), BF16), compiler_params=_params(("parallel",)),
    )(dqkv_w, dg_w, df_w)


def _a_rows_from_natural(w):
    return jnp.concatenate([w[HEAD_DIM * h:HEAD_DIM * (h + 1)] for h in A_HEAD_ORDER], axis=0)


def _a_rows_to_natural(w):
    return jnp.concatenate([w[HEAD_DIM * A_HEAD_ORDER.index(h):HEAD_DIM * (A_HEAD_ORDER.index(h) + 1)]
                            for h in range(A_HEADS)], axis=0)


def kernel(x, meta_tokens, ffn1_norm, ffn1_w_in, ffn1_w_out, mix_norm, w_in, b_forget, attn_sinks, w_branch_a, w_branch_b, w_out, ffn2_norm, ffn2_w_in, ffn2_w_out, final_norm, loss_target, m_meta_tokens, m_ffn1_norm, m_ffn1_w_in, m_ffn1_w_out, m_mix_norm, m_w_in, m_b_forget, m_attn_sinks, m_w_branch_a, m_w_branch_b, m_w_out, m_ffn2_norm, m_ffn2_w_in, m_ffn2_w_out, m_final_norm, v_meta_tokens, v_ffn1_norm, v_ffn1_w_in, v_ffn1_w_out, v_mix_norm, v_w_in, v_b_forget, v_attn_sinks, v_w_branch_a, v_w_branch_b, v_w_out, v_ffn2_norm, v_ffn2_w_in, v_ffn2_w_out, v_final_norm):
    me = 4 * lax.axis_index("x") + 2 * lax.axis_index("y") + lax.axis_index("c")

    shards = (
        _pad_to(ffn1_w_in[0].astype(BF16), D_MODEL, FF_SHARD_P),
        ffn1_w_out[0].astype(BF16),
        _pad_to(w_in[0].astype(BF16), D_MODEL, WIN_SHARD_P),
        w_branch_a[0].astype(BF16), w_branch_b[0].astype(BF16), w_out[0].astype(BF16),
        _pad_to(ffn2_w_in[0].astype(BF16), D_MODEL, FF_SHARD_P),
        ffn2_w_out[0].astype(BF16),
        meta_tokens,
    )
    s1i, s1o, swi, swa, swb, swo, s2i, s2o, smeta = shards
    first_level, second_level = {}, {}
    first_level["meta"], tok = _gather2_start("gather_meta_start", (smeta,))
    first_level["ffn1_in"], tok = _gather2_start("gather_ffn1_in_start", (s1i,), after=tok)
    first_level["ffn1_out"], tok = _gather2_start("gather_ffn1_out_start", (s1o,), after=tok)
    first_level["mix"], tok = _gather2_start("gather_mix_start", (swi, swa, swb, swo), after=tok)
    first_level["ffn2"], tok = _gather2_start("gather_ffn2_start", (s2i, s2o), after=tok)
    started = {"tok": tok}

    def weights(group, after):
        if group.endswith(":forward"):
            group = group[:-len(":forward")]
            second_level[group], token = _gather2_forward("gather_" + group + "_forward", first_level[group], after)
            return token
        if group == "meta":
            after = weights("meta:forward", started["tok"])
        if group == "ffn1_in":
            after = weights("ffn1_in:forward", after)
        got = _gather2_wait("gather_" + group + "_wait", second_level[group], after)
        if group == "mix":
            gwi, gwa, gwb, gwo = got
            return (_proj_from_gathered("proj_w_relayout", gwi), _a_rows_from_natural(gwa.transpose(1, 0, 2).reshape(A_WIDTH, D_MODEL)),
                    gwb.transpose(1, 0, 2).reshape(B_WIDTH, D_MODEL), gwo.reshape(D_MODEL, D_MODEL))
        if group == "ffn1_out":
            return (_ffn_out_from_gathered(got[0]),)
        if group == "meta":
            return (got[0].transpose(1, 0, 2).reshape(N_META, D_MODEL),)
        if group == "ffn1_in":
            return (got[0].reshape(2, 4, D_MODEL, FF_SHARD_P),)
        return got[0].reshape(2, 4, D_MODEL, FF_SHARD_P), _ffn_out_from_gathered(got[1])

    scatter_state = {}

    def send(group, grads):
        if group == "mix":
            dwi_qkv, dwi_g, dwi_f, dwa, dwb, dwo = grads
            dwa = _a_rows_to_natural(dwa)
            blocks = (_proj_to_scatter("proj_dw_relayout", dwi_qkv, dwi_g, dwi_f), dwa.reshape(A_WIDTH, N_DEV, 128).transpose(1, 0, 2),
                      dwb.reshape(B_WIDTH, N_DEV, 128).transpose(1, 0, 2), dwo.reshape(N_DEV, 128, D_MODEL))
        elif group.endswith("_in"):
            blocks = (grads[0].reshape(N_DEV, D_MODEL, FF_SHARD_P),)
        elif group.endswith("_out"):
            blocks = (_ffn_out_to_scatter(grads[0]),)
        else:
            blocks = (grads[0].reshape(N_DEV, D_MODEL, FF_SHARD_P), _ffn_out_to_scatter(grads[1]))
        scatter_state[group], token = _xchg_start("scatter_" + group + "_start", (), blocks)
        return token

    gf = final_norm.reshape(1, D_MODEL)
    grad_x, small = _local_step(x, loss_target, ffn1_norm, mix_norm, ffn2_norm, gf, b_forget, attn_sinks, weights, send)

    small_state, after = _xchg_start("gather_small_start", (small,), ())
    out = {}
    updates = (
        ("ffn2", (("ffn2_w_in", ffn2_w_in, m_ffn2_w_in, v_ffn2_w_in), ("ffn2_w_out", ffn2_w_out, m_ffn2_w_out, v_ffn2_w_out))),
        ("ffn1_out", (("ffn1_w_out", ffn1_w_out, m_ffn1_w_out, v_ffn1_w_out),)),
        ("mix", (("w_in", w_in, m_w_in, v_w_in), ("w_branch_a", w_branch_a, m_w_branch_a, v_w_branch_a),
                 ("w_branch_b", w_branch_b, m_w_branch_b, v_w_branch_b), ("w_out", w_out, m_w_out, v_w_out))),
    )
    last_update = ("ffn1_in", (("ffn1_w_in", ffn1_w_in, m_ffn1_w_in, v_ffn1_w_in),))

    def update(group, members, after):
        parts_list = _xchg_wait("scatter_" + group + "_wait", scatter_state[group], after)
        prev = None
        for (nm, w, m, v), parts in zip(members, parts_list):
            if nm.endswith("w_in"):
                res4 = _adam("adam_" + nm, w[0].T, m[0].T, v[0].T, parts, transposed=True, dep=prev)
                out[nm] = tuple(r.T[None] for r in res4)
            else:
                res4 = _adam("adam_" + nm, w[0], m[0], v[0], parts, dep=prev)
                out[nm] = tuple(r[None] for r in res4)
            prev = res4[0]
        return prev

    for group, members in updates + (last_update,):
        after = update(group, members, after)
    (packs,) = _xchg_wait("gather_small_wait", small_state, after)

    row = lambda a: a.reshape(1, D_MODEL)
    tot, small_res = _small_update("small_update", packs, (
        (ffn1_norm, m_ffn1_norm, v_ffn1_norm), (mix_norm, m_mix_norm, v_mix_norm), (ffn2_norm, m_ffn2_norm, v_ffn2_norm),
        (row(final_norm), row(m_final_norm), row(v_final_norm)), (b_forget, m_b_forget, v_b_forget),
        (attn_sinks, m_attn_sinks, v_attn_sinks)))
    for nm, res4 in zip(("ffn1_norm", "mix_norm", "ffn2_norm", "final_norm", "b_forget", "attn_sinks"), small_res):
        out[nm] = tuple(r.reshape(D_MODEL) for r in res4) if nm == "final_norm" else res4
    loss = tot[4, 2 * B_HEADS]
    g_meta = lax.dynamic_slice(tot[8:24, :], (0, me * 128), (N_META, 128))
    out["meta_tokens"] = tuple(_adam("adam_meta_tokens", meta_tokens, m_meta_tokens, v_meta_tokens, g_meta[None]))

    names = ("meta_tokens", "ffn1_norm", "ffn1_w_in", "ffn1_w_out", "mix_norm", "w_in", "b_forget", "attn_sinks",
             "w_branch_a", "w_branch_b", "w_out", "ffn2_norm", "ffn2_w_in", "ffn2_w_out", "final_norm")
    return (loss, grad_x) + tuple(out[nm][kind] for kind in range(4) for nm in names)
```

```python
import jax
import jax.numpy as jnp
from jax import lax
from jax.experimental import pallas as pl
from jax.experimental.pallas import tpu as pltpu

F32 = jnp.float32
BF16 = jnp.bfloat16

D_MODEL = 1024
N_META = 16
BLOCK = 128
PREFIX = 128
N_PAD = PREFIX - N_META
HEAD_DIM = 64
A_HEADS = 8
B_HEADS = 8
A_WIDTH = 512
A_KV_WIDTH = 128
B_WIDTH = 512
D_FF = 2816
N_DEV = 8
FF_SHARD = 2 * D_FF // N_DEV
FF_SHARD_P = 768
FFO_SHARD = D_FF // N_DEV
D_FF_P = 4 * FF_SHARD_P
W_IN_COLS = 4360
WIN_SHARD = W_IN_COLS // N_DEV
WIN_SHARD_P = 640
PAIR_W = 3 * 128
B_SEG = 4 * PAIR_W
A_SEG = A_WIDTH + 2 * A_KV_WIDTH
QKV_W = B_SEG + A_SEG
P_GATES = 2 * (2 * D_MODEL)
P_F = P_GATES + 2 * D_MODEL
PROJ_P = P_F + 128
A_HEAD_ORDER = (0, 4, 1, 5, 2, 6, 3, 7)
EPS = 1e-6
NEG = -1e30
SCALE = HEAD_DIM ** -0.5
ADAM_LR = 0.001
ADAM_B1 = 0.9
ADAM_B2 = 0.999
ADAM_EPS = 1e-08
ADAM_WD = 0.01
ADAM_STEP = 10
VMEM_LIMIT = 56 * 1024 * 1024
MESH_ID = pl.DeviceIdType.MESH
SMALL_ROWS = 40

_NN = (((1,), (0,)), ((), ()))
_NT = (((1,), (1,)), ((), ()))
_TN = (((0,), (0,)), ((), ()))


def _params(sem=None):
    return pltpu.CompilerParams(dimension_semantics=sem, vmem_limit_bytes=VMEM_LIMIT)


def _pick(n, cands):
    for c in cands:
        if n % c == 0:
            return c
    raise ValueError(f"no tile for {n}")


def _bf(v):
    return v if v.dtype == BF16 else v.astype(BF16)


def _mm(name, a, b, dims, grid, a_spec, b_spec, o_spec, out_shape, out_dtype, alpha=1.0):
    def body(a_ref, b_ref, o_ref):
        acc = lax.dot_general(_bf(a_ref[...]), _bf(b_ref[...]), dims, preferred_element_type=F32)
        if alpha != 1.0:
            acc = acc * alpha
        o_ref[...] = acc.astype(o_ref.dtype)

    return pl.pallas_call(
        body, name=name, grid=grid, in_specs=[a_spec, b_spec], out_specs=o_spec,
        out_shape=jax.ShapeDtypeStruct(out_shape, out_dtype),
        compiler_params=_params(("parallel",) * len(grid)),
    )(a, b)


def _mm_res_norm(name, a, w, res, g_next, alpha=1.0):
    t, k = a.shape
    tm = _pick(t, (544, 384, 256, 128))

    def body(a_ref, w_ref, r_ref, g_ref, h_ref, n_ref):
        acc = lax.dot_general(_bf(a_ref[...]), w_ref[...], _NN, preferred_element_type=F32)
        if alpha != 1.0:
            acc = acc * alpha
        hv = acc + r_ref[...]
        h_ref[...] = hv
        r = lax.rsqrt(jnp.mean(hv * hv, axis=-1, keepdims=True) + EPS)
        n_ref[...] = ((hv * r) * g_ref[...]).astype(BF16)

    row = pl.BlockSpec((tm, D_MODEL), lambda i: (i, 0))
    return pl.pallas_call(
        body, name=name, grid=(t // tm,),
        in_specs=[pl.BlockSpec((tm, k), lambda i: (i, 0)), pl.BlockSpec((k, D_MODEL), lambda i: (0, 0)), row,
                  pl.BlockSpec((1, D_MODEL), lambda i: (0, 0))],
        out_specs=[row, row],
        out_shape=[jax.ShapeDtypeStruct((t, D_MODEL), F32), jax.ShapeDtypeStruct((t, D_MODEL), BF16)],
        compiler_params=_params(("parallel",)),
    )(a, w, res, g_next)


def _mm_tn(name, a, b, out_dtype=BF16, alpha=1.0, tm_c=(768, 512, 256, 128), tn_c=(512, 640, 256, 128)):
    t, m = a.shape
    n = b.shape[1]
    tm = _pick(m, tm_c)
    tn = _pick(n, tn_c)
    bytes_a, bytes_b = a.size * a.dtype.itemsize, b.size * b.dtype.itemsize
    if bytes_a + bytes_b * (m // tm) <= bytes_b + bytes_a * (n // tn):
        return _mm(name, a, b, _TN, (m // tm, n // tn),
                   pl.BlockSpec((t, tm), lambda i, j: (0, i)), pl.BlockSpec((t, tn), lambda i, j: (0, j)),
                   pl.BlockSpec((tm, tn), lambda i, j: (i, j)), (m, n), out_dtype, alpha=alpha)
    return _mm(name, a, b, _TN, (n // tn, m // tm),
               pl.BlockSpec((t, tm), lambda j, i: (0, i)), pl.BlockSpec((t, tn), lambda j, i: (0, j)),
               pl.BlockSpec((tm, tn), lambda j, i: (i, j)), (m, n), out_dtype, alpha=alpha)


def _ffn_in_fwd(name, n, wblk, dep=None):
    t = n.shape[0]
    tm = _pick(t, (1088, 768, 512, 256, 128))
    has_dep = dep is not None

    def body(n_ref, w_ref, *rest):
        gu_ref, a_ref = rest[-2], rest[-1]
        nv = n_ref[...]
        g = lax.dot_general(nv, w_ref[0], _NN, preferred_element_type=F32)
        u = lax.dot_general(nv, w_ref[1], _NN, preferred_element_type=F32)
        sg = jax.nn.sigmoid(g)
        silu = g * sg
        a_ref[...] = (silu * u).astype(BF16)
        gu_ref[0] = ((0.5 * u) * (sg + silu * (1.0 - sg))).astype(BF16)
        gu_ref[1] = (0.5 * silu).astype(BF16)

    return pl.pallas_call(
        body, name=name, grid=(t // tm, 4),
        in_specs=[pl.BlockSpec((tm, D_MODEL), lambda i, j: (i, 0)),
                  pl.BlockSpec((2, None, D_MODEL, FF_SHARD_P), lambda i, j: (0, j, 0, 0))]
        + ([pl.BlockSpec(memory_space=pl.ANY)] if has_dep else []),
        out_specs=[pl.BlockSpec((2, tm, FF_SHARD_P), lambda i, j: (0, i, j)),
                   pl.BlockSpec((tm, FF_SHARD_P), lambda i, j: (i, j))],
        out_shape=[jax.ShapeDtypeStruct((2, t, D_FF_P), BF16), jax.ShapeDtypeStruct((t, D_FF_P), BF16)],
        compiler_params=_params(("parallel", "parallel")),
    )(*((n, wblk) + ((dep,) if has_dep else ())))


def _ffn_out_bwd_x(name, dh, w_out, gu, dep=None):
    t = dh.shape[0]
    tm = _pick(t, (1088, 768, 512, 256, 128))
    has_dep = dep is not None

    def body(dh_ref, w_ref, gu_ref, *rest):
        o_ref = rest[-1]
        da = lax.dot_general(_bf(dh_ref[...]), w_ref[...], _NT, preferred_element_type=F32)
        o_ref[0] = (da * gu_ref[0].astype(F32)).astype(BF16)
        o_ref[1] = (da * gu_ref[1].astype(F32)).astype(BF16)

    gu_spec = pl.BlockSpec((2, tm, FF_SHARD_P), lambda i, j: (0, i, j))
    return pl.pallas_call(
        body, name=name, grid=(t // tm, 4),
        in_specs=[pl.BlockSpec((tm, D_MODEL), lambda i, j: (i, 0)), pl.BlockSpec((FF_SHARD_P, D_MODEL), lambda i, j: (j, 0)),
                  gu_spec] + ([pl.BlockSpec(memory_space=pl.ANY)] if has_dep else []),
        out_specs=gu_spec, out_shape=jax.ShapeDtypeStruct((2, t, D_FF_P), BF16),
        compiler_params=_params(("parallel", "parallel")),
    )(*((dh, w_out, gu) + ((dep,) if has_dep else ())))


def _rms_bwd_rows(dn, h, g, dres):
    r = lax.rsqrt(jnp.mean(h * h, axis=-1, keepdims=True) + EPS)
    tv = dn * g
    dot = jnp.mean(tv * h, axis=-1, keepdims=True)
    return dres + (r * tv - h * (r * r * r * dot)), jnp.sum(dn * (h * r), axis=0, keepdims=True)


def _accumulate_rows(ref, part, first):
    @pl.when(first)
    def _():
        ref[...] = part

    @pl.when(jnp.logical_not(first))
    def _():
        ref[...] += part


def _ffn_in_bwd_x(name, dgu, wblk, h_in, g_norm, dres, dep=None, examples=None):
    t = dgu.shape[1]
    tm = _pick(t, (544, 384, 256, 128))
    has_dep = dep is not None
    split = examples is not None
    if split:
        l = t // examples
        per = l // tm
        assert per * tm == l and tm > PREFIX

    def body(d_ref, w_ref, h_ref, g_ref, r_ref, *rest):
        acc = None
        for s in range(2):
            for j in range(4):
                part = lax.dot_general(d_ref[s, :, FF_SHARD_P * j:FF_SHARD_P * (j + 1)], w_ref[s, j], _NT,
                                       preferred_element_type=F32)
                acc = part if acc is None else acc + part
        dh, dg = _rms_bwd_rows(acc, h_ref[...], g_ref[...], r_ref[...])
        i = pl.program_id(0)
        if not split:
            dh_ref, dg_ref = rest[-2], rest[-1]
            dh_ref[...] = dh
        else:
            gx_ref, meta_ref, dg_ref, buf, sem = rest[-5:]

            def out_copy(step):
                bi, r = step // per, step % per
                head = pltpu.make_async_copy(buf.at[pl.ds(PREFIX, tm - PREFIX)], gx_ref.at[bi, pl.ds(0, tm - PREFIX)], sem)
                if per == 1:
                    return r == 0, head, None
                return r == 0, head, pltpu.make_async_copy(
                    buf, gx_ref.at[bi, pl.ds(pl.multiple_of(jnp.maximum(r, 1) * tm - PREFIX, 8), tm)], sem)

            def run(step, method):
                is_head, head, later = out_copy(step)

                @pl.when(is_head)
                def _():
                    getattr(head, method)()

                if later is not None:
                    @pl.when(jnp.logical_not(is_head))
                    def _():
                        getattr(later, method)()

            @pl.when(i > 0)
            def _():
                run(i - 1, "wait")

            buf[...] = dh

            @pl.when(i % per == 0)
            def _():
                meta_ref[...] = dh[N_PAD:PREFIX]

            run(i, "start")

            @pl.when(i == pl.num_programs(0) - 1)
            def _():
                run(i, "wait")

        _accumulate_rows(dg_ref, dg, i == 0)

    row = pl.BlockSpec((tm, D_MODEL), lambda i: (i, 0))
    vec = pl.BlockSpec((1, D_MODEL), lambda i: (0, 0))
    if split:
        out_specs = [pl.BlockSpec(memory_space=pl.ANY), pl.BlockSpec((None, N_META, D_MODEL), lambda i: (i // per, 0, 0)), vec]
        out_shape = [jax.ShapeDtypeStruct((examples, l - PREFIX, D_MODEL), F32),
                     jax.ShapeDtypeStruct((examples, N_META, D_MODEL), F32), jax.ShapeDtypeStruct((1, D_MODEL), F32)]
        scratch = [pltpu.VMEM((tm, D_MODEL), F32), pltpu.SemaphoreType.DMA(())]
    else:
        out_specs = [row, vec]
        out_shape = [jax.ShapeDtypeStruct((t, D_MODEL), F32), jax.ShapeDtypeStruct((1, D_MODEL), F32)]
        scratch = []
    return pl.pallas_call(
        body, name=name, grid=(t // tm,),
        in_specs=[pl.BlockSpec((2, tm, D_FF_P), lambda i: (0, i, 0)),
                  pl.BlockSpec((2, 4, D_MODEL, FF_SHARD_P), lambda i: (0, 0, 0, 0), pipeline_mode=pl.Buffered(1)),
                  row, vec, row]
        + ([pl.BlockSpec(memory_space=pl.ANY)] if has_dep else []),
        out_specs=out_specs, out_shape=out_shape, scratch_shapes=scratch,
        compiler_params=_params(("arbitrary",)),
    )(*((dgu, wblk, h_in, g_norm, dres) + ((dep,) if has_dep else ())))


def _proj_fwd(name, um, wi):
    t = um.shape[0]
    tm = _pick(t, (544, 384, 256, 128))

    def body(u_ref, w_ref, q_ref, g_ref, f_ref):
        uv = u_ref[...]
        q_ref[...] = lax.dot_general(uv, w_ref[:, 0:QKV_W], _NN, preferred_element_type=F32).astype(BF16)
        g_ref[...] = lax.dot_general(uv, w_ref[:, P_GATES:P_F], _NN, preferred_element_type=F32)
        f_ref[...] = lax.dot_general(uv, w_ref[:, P_F:PROJ_P], _NN, preferred_element_type=F32)

    return pl.pallas_call(
        body, name=name, grid=(t // tm,),
        in_specs=[pl.BlockSpec((tm, D_MODEL), lambda i: (i, 0)),
                  pl.BlockSpec((D_MODEL, PROJ_P), lambda i: (0, 0), pipeline_mode=pl.Buffered(1))],
        out_specs=[pl.BlockSpec((tm, QKV_W), lambda i: (i, 0)), pl.BlockSpec((tm, 2 * D_MODEL), lambda i: (i, 0)),
                   pl.BlockSpec((tm, 128), lambda i: (i, 0))],
        out_shape=[jax.ShapeDtypeStruct((t, QKV_W), BF16), jax.ShapeDtypeStruct((t, 2 * D_MODEL), F32),
                   jax.ShapeDtypeStruct((t, 128), F32)],
        compiler_params=_params(("parallel",)),
    )(um, wi)


def _proj_bwd_x(name, dqkv, dgates, df, wi, h_in, g_norm, dres, dep=None):
    t = dqkv.shape[0]
    tm = _pick(t, (544, 384, 256, 128))
    has_dep = dep is not None

    def body(q_ref, gt_ref, f_ref, w_ref, h_ref, g_ref, r_ref, *rest):
        dh_ref, dg_ref = rest[-2], rest[-1]
        acc = lax.dot_general(q_ref[...], w_ref[:, 0:QKV_W], _NT, preferred_element_type=F32)
        acc = acc + lax.dot_general(gt_ref[...], w_ref[:, P_GATES:P_F], _NT, preferred_element_type=F32)
        acc = acc + lax.dot_general(f_ref[...], w_ref[:, P_F:PROJ_P], _NT, preferred_element_type=F32)
        dh, dg = _rms_bwd_rows(acc, h_ref[...], g_ref[...], r_ref[...])
        dh_ref[...] = dh
        _accumulate_rows(dg_ref, dg, pl.program_id(0) == 0)

    row = pl.BlockSpec((tm, D_MODEL), lambda i: (i, 0))
    vec = pl.BlockSpec((1, D_MODEL), lambda i: (0, 0))
    return pl.pallas_call(
        body, name=name, grid=(t // tm,),
        in_specs=[pl.BlockSpec((tm, QKV_W), lambda i: (i, 0)), pl.BlockSpec((tm, 2 * D_MODEL), lambda i: (i, 0)),
                  pl.BlockSpec((tm, 128), lambda i: (i, 0)),
                  pl.BlockSpec((D_MODEL, PROJ_P), lambda i: (0, 0), pipeline_mode=pl.Buffered(1)), row, vec, row]
        + ([pl.BlockSpec(memory_space=pl.ANY)] if has_dep else []),
        out_specs=[row, vec],
        out_shape=[jax.ShapeDtypeStruct((t, D_MODEL), F32), jax.ShapeDtypeStruct((1, D_MODEL), F32)],
        compiler_params=_params(("arbitrary",)),
    )(*((dqkv, dgates, df, wi, h_in, g_norm, dres) + ((dep,) if has_dep else ())))


def _ffn_in_bwd_w(name, n, dgu):
    t = n.shape[0]
    return _mm(name, n, dgu, _TN, (2, 4),
               pl.BlockSpec((t, D_MODEL), lambda s, j: (0, 0)),
               pl.BlockSpec((None, t, FF_SHARD_P), lambda s, j: (s, 0, j)),
               pl.BlockSpec((None, None, D_MODEL, FF_SHARD_P), lambda s, j: (s, j, 0, 0)),
               (2, 4, D_MODEL, FF_SHARD_P), BF16)


def _embed_norm(name, x, meta, g):
    b, s, _ = x.shape
    half = (s + PREFIX) // 2
    first = half - PREFIX
    assert first > 0 and half % 16 == 0

    def body(x_ref, m_ref, g_ref, h_ref, n_ref, buf, sem):
        bi, k = pl.program_id(0), pl.program_id(1)

        def tokens(example, second, method):
            if second:
                cp = pltpu.make_async_copy(x_ref.at[example, pl.ds(first, half)], buf.at[1], sem.at[1])
            else:
                cp = pltpu.make_async_copy(x_ref.at[example, pl.ds(0, first)], buf.at[0, pl.ds(PREFIX, first)], sem.at[0])
            getattr(cp, method)()

        @pl.when((bi == 0) & (k == 0))
        def _():
            tokens(0, False, "start")

        @pl.when(k == 0)
        def _():
            tokens(bi, True, "start")
            tokens(bi, False, "wait")
            buf[0, 0:N_PAD, :] = jnp.zeros((N_PAD, D_MODEL), F32)
            buf[0, N_PAD:PREFIX, :] = m_ref[...]

        @pl.when(k == 1)
        def _():
            @pl.when(bi + 1 < b)
            def _():
                tokens(bi + 1, False, "start")

            tokens(bi, True, "wait")

        hv = buf[k]
        h_ref[...] = hv
        r = lax.rsqrt(jnp.mean(hv * hv, axis=-1, keepdims=True) + EPS)
        n_ref[...] = ((hv * r) * g_ref[...]).astype(BF16)

    rows = pl.BlockSpec((half, D_MODEL), lambda bi, k: (2 * bi + k, 0))
    return pl.pallas_call(
        body, name=name, grid=(b, 2),
        in_specs=[pl.BlockSpec(memory_space=pl.ANY), pl.BlockSpec((N_META, D_MODEL), lambda bi, k: (0, 0)),
                  pl.BlockSpec((1, D_MODEL), lambda bi, k: (0, 0))],
        out_specs=[rows, rows],
        out_shape=[jax.ShapeDtypeStruct((2 * b * half, D_MODEL), F32), jax.ShapeDtypeStruct((2 * b * half, D_MODEL), BF16)],
        scratch_shapes=[pltpu.VMEM((2, half, D_MODEL), F32), pltpu.SemaphoreType.DMA((2,))],
        compiler_params=_params(("arbitrary", "arbitrary")),
    )(x, meta, g)


def _branch_gate_fwd(name, oa, ob, wa, wb, gates, dep=None):
    t = gates.shape[0]
    tm = _pick(t, (544, 384, 256, 128))
    has_dep = dep is not None

    def body(oa_ref, ob_ref, wa_ref, wb_ref, g_ref, *rest):
        o_ref, ya_ref, yb_ref = rest[-3:]
        ya = lax.dot_general(_bf(oa_ref[...]), wa_ref[...], _NN, preferred_element_type=F32)
        yb = lax.dot_general(_bf(ob_ref[...]), wb_ref[...], _NN, preferred_element_type=F32)
        sa = jax.nn.sigmoid(g_ref[:, 0:D_MODEL])
        sb = jax.nn.sigmoid(g_ref[:, D_MODEL:2 * D_MODEL])
        o_ref[...] = (sa * ya + sb * yb).astype(BF16)
        ya_ref[...] = ya.astype(BF16)
        yb_ref[...] = yb.astype(BF16)

    blk = pl.BlockSpec((tm, D_MODEL), lambda i: (i, 0))
    narrow = pl.BlockSpec((tm, A_WIDTH), lambda i: (i, 0))
    wide = pl.BlockSpec((tm, 2 * D_MODEL), lambda i: (i, 0))
    wspec = pl.BlockSpec((A_WIDTH, D_MODEL), lambda i: (0, 0))
    out = jax.ShapeDtypeStruct((t, D_MODEL), BF16)
    return pl.pallas_call(
        body, name=name, grid=(t // tm,),
        in_specs=[narrow, narrow, wspec, wspec, wide] + ([pl.BlockSpec(memory_space=pl.ANY)] if has_dep else []),
        out_specs=[blk, blk, blk], out_shape=[out, out, out], compiler_params=_params(("parallel",)),
    )(*((oa, ob, wa, wb, gates) + ((dep,) if has_dep else ())))


def _branch_bwd_x(name, dya, dyb, wa, wb):
    t = dya.shape[0]
    tm = _pick(t, (1088, 768, 512, 256, 128))

    def body(da_ref, db_ref, wa_ref, wb_ref, oa_ref, ob_ref):
        oa_ref[...] = lax.dot_general(da_ref[...], wa_ref[...], _NT, preferred_element_type=F32)
        ob_ref[...] = lax.dot_general(db_ref[...], wb_ref[...], _NT, preferred_element_type=F32).astype(BF16)

    blk = pl.BlockSpec((tm, D_MODEL), lambda i: (i, 0))
    narrow = pl.BlockSpec((tm, A_WIDTH), lambda i: (i, 0))
    wspec = pl.BlockSpec((A_WIDTH, D_MODEL), lambda i: (0, 0), pipeline_mode=pl.Buffered(1))
    return pl.pallas_call(
        body, name=name, grid=(t // tm,), in_specs=[blk, blk, wspec, wspec], out_specs=[narrow, narrow],
        out_shape=[jax.ShapeDtypeStruct((t, A_WIDTH), F32), jax.ShapeDtypeStruct((t, B_WIDTH), BF16)],
        compiler_params=_params(("parallel",)),
    )(dya, dyb, wa, wb)


def _branch_bwd_w(name, oa, ob, dya, dyb):
    t = oa.shape[0]
    tn = 512

    def body(oa_ref, ob_ref, da_ref, db_ref, wa_ref, wb_ref):
        wa_ref[...] = lax.dot_general(_bf(oa_ref[...]), da_ref[...], _TN, preferred_element_type=F32).astype(BF16)
        wb_ref[...] = lax.dot_general(_bf(ob_ref[...]), db_ref[...], _TN, preferred_element_type=F32).astype(BF16)

    whole = pl.BlockSpec((t, A_WIDTH), lambda j: (0, 0), pipeline_mode=pl.Buffered(1))
    cols = pl.BlockSpec((t, tn), lambda j: (0, j))
    out_spec = pl.BlockSpec((A_WIDTH, tn), lambda j: (0, j))
    out = jax.ShapeDtypeStruct((A_WIDTH, D_MODEL), BF16)
    return pl.pallas_call(
        body, name=name, grid=(D_MODEL // tn,), in_specs=[whole, whole, cols, cols], out_specs=[out_spec, out_spec],
        out_shape=[out, out], compiler_params=_params(("parallel",)),
    )(oa, ob, dya, dyb)


def _mix_out_gate_bwd(name, dh, wo, gates, ya, yb):
    t = gates.shape[0]
    tm = _pick(t, (544, 384, 256, 128))

    def body(dh_ref, w_ref, g_ref, ya_ref, yb_ref, dya_ref, dyb_ref, dg_ref):
        dm = lax.dot_general(_bf(dh_ref[...]), w_ref[...], _NT, preferred_element_type=F32)
        sa = jax.nn.sigmoid(g_ref[:, 0:D_MODEL])
        sb = jax.nn.sigmoid(g_ref[:, D_MODEL:2 * D_MODEL])
        dya_ref[...] = (dm * sa).astype(BF16)
        dyb_ref[...] = (dm * sb).astype(BF16)
        dg_ref[:, 0:D_MODEL] = (dm * ya_ref[...].astype(F32) * (sa * (1.0 - sa))).astype(BF16)
        dg_ref[:, D_MODEL:2 * D_MODEL] = (dm * yb_ref[...].astype(F32) * (sb * (1.0 - sb))).astype(BF16)

    blk = pl.BlockSpec((tm, D_MODEL), lambda i: (i, 0))
    wide = pl.BlockSpec((tm, 2 * D_MODEL), lambda i: (i, 0))
    out = jax.ShapeDtypeStruct((t, D_MODEL), BF16)
    return pl.pallas_call(
        body, name=name, grid=(t // tm,),
        in_specs=[blk, pl.BlockSpec((D_MODEL, D_MODEL), lambda i: (0, 0)), wide, blk, blk], out_specs=[blk, blk, wide],
        out_shape=[out, out, jax.ShapeDtypeStruct((t, 2 * D_MODEL), BF16)], compiler_params=_params(("parallel",)),
    )(dh, wo, gates, ya, yb)


def _ffn_out_loss(name, a, w, res, gf, tgt, alpha):
    t, k = a.shape
    b, s, _ = tgt.shape
    l = t // b
    tm = _pick(t, (544, 384, 256, 128))
    per = l // tm
    assert per * tm == l and tm > PREFIX and l - PREFIX == s

    def body(a_ref, w_ref, r_ref, g_ref, t_ref, dh_ref, loss_ref, dg_ref, buf, sem):
        i = pl.program_id(0)
        bi, r = i // per, i % per

        def first_rows():
            return pltpu.make_async_copy(t_ref.at[bi, pl.ds(0, tm - PREFIX)], buf.at[pl.ds(PREFIX, tm - PREFIX)], sem)

        def later_rows():
            return pltpu.make_async_copy(t_ref.at[bi, pl.ds(pl.multiple_of(r * tm - PREFIX, 8), tm)], buf, sem)

        @pl.when(r == 0)
        def _():
            buf[0:PREFIX, :] = jnp.zeros((PREFIX, D_MODEL), F32)
            first_rows().start()

        if per > 1:
            @pl.when(r > 0)
            def _():
                later_rows().start()

        acc = lax.dot_general(a_ref[...], w_ref[...], _NN, preferred_element_type=F32)

        @pl.when(r == 0)
        def _():
            first_rows().wait()

        if per > 1:
            @pl.when(r > 0)
            def _():
                later_rows().wait()

        hv = acc * alpha + r_ref[...]
        row = lax.broadcasted_iota(jnp.int32, (tm, 1), 0)
        real = ((r > 0) | (row >= PREFIX)).astype(F32)
        g = g_ref[...]
        rn = lax.rsqrt(jnp.mean(hv * hv, axis=-1, keepdims=True) + EPS)
        xn = hv * rn
        err = (xn * g - buf[...]) * real
        lpart = 0.5 * jnp.sum(jnp.mean(err * err, axis=-1, keepdims=True), axis=0, keepdims=True)
        dy = err * (1.0 / D_MODEL)
        tv = dy * g
        dot = jnp.mean(tv * hv, axis=-1, keepdims=True)
        dh_ref[...] = rn * tv - hv * (rn * rn * rn * dot)
        gpart = jnp.sum(dy * xn, axis=0, keepdims=True)

        @pl.when(i == 0)
        def _():
            loss_ref[...] = jnp.zeros_like(loss_ref)
            dg_ref[...] = jnp.zeros_like(dg_ref)

        loss_ref[...] += jnp.broadcast_to(lpart, loss_ref.shape)
        dg_ref[...] += gpart

    row_spec = pl.BlockSpec((tm, D_MODEL), lambda i: (i, 0))
    vec = pl.BlockSpec((1, D_MODEL), lambda i: (0, 0))
    return pl.pallas_call(
        body, name=name, grid=(t // tm,),
        in_specs=[pl.BlockSpec((tm, k), lambda i: (i, 0)),
                  pl.BlockSpec((k, D_MODEL), lambda i: (0, 0), pipeline_mode=pl.Buffered(1)), row_spec, vec,
                  pl.BlockSpec(memory_space=pl.ANY)],
        out_specs=[row_spec, pl.BlockSpec((8, 128), lambda i: (0, 0)), vec],
        out_shape=[jax.ShapeDtypeStruct((t, D_MODEL), F32), jax.ShapeDtypeStruct((8, 128), F32),
                   jax.ShapeDtypeStruct((1, D_MODEL), F32)],
        scratch_shapes=[pltpu.VMEM((tm, D_MODEL), F32), pltpu.SemaphoreType.DMA(())],
        compiler_params=_params(("arbitrary",)),
    )(a, w, res, gf, tgt)


def _fgate_fwd(name, f3, bf_row):
    b, l, _ = f3.shape
    nb = l // BLOCK

    def body(f_ref, b_ref, cc_ref, cr_ref):
        r_i = lax.broadcasted_iota(jnp.int32, (BLOCK, BLOCK), 0)
        c_i = lax.broadcasted_iota(jnp.int32, (BLOCK, BLOCK), 1)
        tri = (r_i >= c_i).astype(F32)
        carry = jnp.zeros((1, 128), F32)
        for blk in range(nb):
            rows = slice(blk * BLOCK, (blk + 1) * BLOCK)
            z = f_ref[rows, :] + b_ref[...]
            lf = jnp.minimum(z, 0.0) - jnp.log(1.0 + jnp.exp(-jnp.abs(z)))
            cb = jnp.dot(tri, lf, preferred_element_type=F32, precision=lax.Precision.HIGHEST) + carry
            carry = cb[BLOCK - 1:BLOCK, :]
            cbt = cb.T
            for hh in range(B_HEADS):
                cc_ref[hh, rows, :] = jnp.sum(jnp.where(c_i == hh, cb, 0.0), axis=1, keepdims=True)
                cr_ref[hh, :, rows] = cbt[hh:hh + 1, :]

    return pl.pallas_call(
        body, name=name, grid=(b,),
        in_specs=[pl.BlockSpec((None, l, 128), lambda bi: (bi, 0, 0)),
                  pl.BlockSpec((1, 128), lambda bi: (0, 0))],
        out_specs=[pl.BlockSpec((None, B_HEADS, l, 1), lambda bi: (bi, 0, 0, 0)),
                   pl.BlockSpec((None, B_HEADS, 1, l), lambda bi: (bi, 0, 0, 0))],
        out_shape=[jax.ShapeDtypeStruct((b, B_HEADS, l, 1), F32), jax.ShapeDtypeStruct((b, B_HEADS, 1, l), F32)],
        compiler_params=_params(("parallel",)),
    )(f3, bf_row)


def _fgate_bwd(name, f3, bf_row, dcq, dck):
    b, l, _ = f3.shape
    nb = l // BLOCK

    def body(f_ref, b_ref, dcq_ref, dck_ref, df_ref, db_ref):
        r_i = lax.broadcasted_iota(jnp.int32, (BLOCK, BLOCK), 0)
        c_i = lax.broadcasted_iota(jnp.int32, (BLOCK, BLOCK), 1)
        tri = (r_i <= c_i).astype(F32)
        carry = jnp.zeros((1, 128), F32)
        total = jnp.zeros((1, 128), F32)
        for blk in range(nb - 1, -1, -1):
            rows = slice(blk * BLOCK, (blk + 1) * BLOCK)
            krows = jnp.concatenate([dck_ref[hh, :, rows] for hh in range(B_HEADS)]
                                    + [jnp.zeros((BLOCK - B_HEADS, BLOCK), F32)], axis=0)
            dcb = krows.T
            for hh in range(B_HEADS):
                dcb = dcb + jnp.where(c_i == hh, dcq_ref[hh, rows, :], 0.0)
            rc = jnp.dot(tri, dcb, preferred_element_type=F32, precision=lax.Precision.HIGHEST) + carry
            carry = rc[0:1, :]
            z = f_ref[rows, :] + b_ref[...]
            df = rc * (1.0 / (1.0 + jnp.exp(z)))
            df_ref[rows, :] = df.astype(BF16)
            total = total + jnp.sum(df, axis=0, keepdims=True)

        @pl.when(pl.program_id(0) == 0)
        def _():
            db_ref[...] = total

        @pl.when(pl.program_id(0) > 0)
        def _():
            db_ref[...] += total

    return pl.pallas_call(
        body, name=name, grid=(b,),
        in_specs=[pl.BlockSpec((None, l, 128), lambda bi: (bi, 0, 0)),
                  pl.BlockSpec((1, 128), lambda bi: (0, 0)),
                  pl.BlockSpec((None, B_HEADS, l, 1), lambda bi: (bi, 0, 0, 0)),
                  pl.BlockSpec((None, B_HEADS, 1, l), lambda bi: (bi, 0, 0, 0))],
        out_specs=[pl.BlockSpec((None, l, 128), lambda bi: (bi, 0, 0)), pl.BlockSpec((1, 128), lambda bi: (0, 0))],
        out_shape=[jax.ShapeDtypeStruct((b, l, 128), BF16), jax.ShapeDtypeStruct((1, 128), F32)],
        compiler_params=_params(("arbitrary",)),
    )(f3, bf_row, dcq, dck)


A_Q_BLK = B_SEG // A_WIDTH
A_K_BLK = (B_SEG + A_WIDTH) // 128
A_V_BLK = A_K_BLK + 1
A_SEG_BLK = B_SEG // A_SEG
STACK = A_HEADS * BLOCK


def _lane_lo():
    return lax.broadcasted_iota(jnp.int32, (1, 128), 1) < HEAD_DIM


def _stack_heads(x, masked):
    lo = _lane_lo()
    blks = [x[:, 128 * j:128 * (j + 1)] for j in range(4)]
    if not masked:
        return jnp.concatenate(blks + blks, axis=0)
    zero = jnp.zeros_like(blks[0])
    return jnp.concatenate([jnp.where(lo, bk, zero) for bk in blks] + [jnp.where(lo, zero, bk) for bk in blks], axis=0)


def _unstack_heads(y):
    lo = _lane_lo()
    return jnp.concatenate([jnp.where(lo, y[128 * j:128 * (j + 1)], y[128 * (4 + j):128 * (5 + j)]) for j in range(4)], axis=1)


def _swa_bias(slopes):
    r_i = jnp.arange(STACK)[:, None]
    c_i = jnp.arange(3 * BLOCK)[None, :]
    seg = c_i >> 7
    out = []
    for n in range(3):
        qpos = n * BLOCK + (r_i & (BLOCK - 1))
        kpos = jnp.where(seg == 0, c_i, (n - 2) * BLOCK + c_i)
        dist = qpos - kpos
        band = (seg != 0) & (dist < BLOCK) & (kpos >= PREFIX)
        meta = (seg == 0) & (c_i >= N_PAD)
        out.append(jnp.where((dist >= 0) & (band | meta), -slopes * dist.astype(F32), NEG))
    return jnp.stack(out, axis=0)


def _swa_scores(q, kcat, n, slope, bias):
    s = lax.dot_general(q, kcat, _NT, preferred_element_type=F32) + bias
    further = slope * (-BLOCK * jnp.maximum(n - 2, 0)).astype(F32)
    return jnp.concatenate([s[:, 0:BLOCK] + further, s[:, BLOCK:]], axis=1)


def _swa_specs():
    def kv(col_blk):
        return [pl.BlockSpec((None, BLOCK, 128), lambda b, n: (b, 0, col_blk)),
                pl.BlockSpec((None, BLOCK, 128), lambda b, n: (b, jnp.maximum(n - 1, 0), col_blk)),
                pl.BlockSpec((None, BLOCK, 128), lambda b, n: (b, n, col_blk))]

    q_spec = pl.BlockSpec((None, BLOCK, A_WIDTH), lambda b, n: (b, n, A_Q_BLK))
    o_spec = pl.BlockSpec((None, BLOCK, A_WIDTH), lambda b, n: (b, n, 0))
    col = pl.BlockSpec((STACK, 1), lambda b, n: (0, 0))
    bias = pl.BlockSpec((None, STACK, 3 * BLOCK), lambda b, n: (jnp.minimum(n, 2), 0, 0))
    lse_spec = pl.BlockSpec((None, A_HEADS, BLOCK, 1), lambda b, n: (b, 0, n, 0))
    return q_spec, kv(A_K_BLK), kv(A_V_BLK), o_spec, [col, col, bias], lse_spec


def _swa_fwd(name, qkv, slopes, sinks, bias):
    b, l, _ = qkv.shape
    nb = l // BLOCK

    def body(q_ref, k0_ref, kp_ref, kc_ref, v0_ref, vp_ref, vc_ref, sl_ref, sk_ref, bias_ref, o_ref, lse_ref):
        n = pl.program_id(1)
        qs = _stack_heads(q_ref[...], True) * SCALE
        kcat = jnp.concatenate([k0_ref[...], kp_ref[...], kc_ref[...]], axis=0)
        vcat = jnp.concatenate([v0_ref[...], vp_ref[...], vc_ref[...]], axis=0)
        s = _swa_scores(qs, kcat, n, sl_ref[...], bias_ref[...])
        sink = sk_ref[...]
        m = jnp.maximum(jnp.max(s, axis=-1, keepdims=True), sink)
        p = jnp.exp(s - m)
        den = jnp.sum(p, axis=-1, keepdims=True) + jnp.exp(sink - m)
        o = lax.dot_general(p.astype(BF16), vcat, _NN, preferred_element_type=F32) / den
        o_ref[...] = _unstack_heads(o)
        lse_ref[...] = (m + jnp.log(den)).reshape(A_HEADS, BLOCK, 1)

    q_spec, k_specs, v_specs, o_spec, consts, lse_spec = _swa_specs()
    return pl.pallas_call(
        body, name=name, grid=(b, nb),
        in_specs=[q_spec] + k_specs + v_specs + consts, out_specs=[o_spec, lse_spec],
        out_shape=[jax.ShapeDtypeStruct((b, l, A_WIDTH), F32), jax.ShapeDtypeStruct((b, A_HEADS, l, 1), F32)],
        compiler_params=_params(("parallel", "parallel")),
    )(qkv, qkv, qkv, qkv, qkv, qkv, qkv, slopes, sinks, bias)


def _swa_bwd(name, qkv, o, lse, do, slopes, sinks, bias, dqkv):
    b, l, _ = qkv.shape
    nb = l // BLOCK

    def body(q_ref, k0_ref, kp_ref, kc_ref, v0_ref, vp_ref, vc_ref, o_ref, lse_ref, do_ref, sl_ref, sk_ref, bias_ref, _,
             dx_ref, ds_ref, dk_acc, dv_acc):
        bi = pl.program_id(0)
        n = pl.program_id(1)
        qs = _stack_heads(q_ref[...], True) * SCALE
        dos32 = _stack_heads(do_ref[...], True)
        dos = dos32.astype(BF16)
        os_ = _stack_heads(o_ref[...], False)
        lsev = lse_ref[...].reshape(STACK, 1)
        kcat = jnp.concatenate([k0_ref[...], kp_ref[...], kc_ref[...]], axis=0)
        vcat = jnp.concatenate([v0_ref[...], vp_ref[...], vc_ref[...]], axis=0)
        s = _swa_scores(qs, kcat, n, sl_ref[...], bias_ref[...])
        p = jnp.exp(s - lsev)
        dsum = jnp.sum(dos32 * os_, axis=-1, keepdims=True)
        dp = lax.dot_general(dos, vcat, _NT, preferred_element_type=F32)
        dsc = (p * (dp - dsum)).astype(BF16)
        dq = lax.dot_general(dsc, kcat, _NN, preferred_element_type=F32) * SCALE
        row0 = pl.multiple_of(n * BLOCK, BLOCK)
        dx_ref[pl.ds(row0, BLOCK), 0:A_WIDTH] = _unstack_heads(dq).astype(BF16)
        dkc = lax.dot_general(dsc, qs, _TN, preferred_element_type=F32)
        dvc = lax.dot_general(p.astype(BF16), dos, _TN, preferred_element_type=F32)

        @pl.when(n == 0)
        def _():
            dk_acc[...] = jnp.zeros_like(dk_acc)
            dv_acc[...] = jnp.zeros_like(dv_acc)

        starts = (0, pl.multiple_of(jnp.maximum(n - 1, 0) * BLOCK, BLOCK), row0)
        for t, st in enumerate(starts):
            dk_acc[pl.ds(st, BLOCK), :] += dkc[t * BLOCK:(t + 1) * BLOCK, :]
            dv_acc[pl.ds(st, BLOCK), :] += dvc[t * BLOCK:(t + 1) * BLOCK, :]

        @pl.when(n == nb - 1)
        def _():
            dx_ref[:, A_WIDTH:A_WIDTH + 128] = dk_acc[...].astype(BF16)
            dx_ref[:, A_WIDTH + 128:A_SEG] = dv_acc[...].astype(BF16)

        dsink = -(jnp.exp(sk_ref[...] - lsev) * dsum)
        r8 = lax.broadcasted_iota(jnp.int32, (8, 128), 0)
        acc = jnp.zeros((8, 128), F32)
        for hh in range(A_HEADS):
            acc = acc + jnp.where(r8 == hh, jnp.sum(dsink[hh * BLOCK:(hh + 1) * BLOCK, :]), 0.0)

        @pl.when((bi == 0) & (n == 0))
        def _():
            ds_ref[...] = jnp.zeros_like(ds_ref)

        ds_ref[...] += acc

    q_spec, k_specs, v_specs, o_spec, consts, lse_spec = _swa_specs()
    return pl.pallas_call(
        body, name=name, grid=(b, nb),
        in_specs=[q_spec] + k_specs + v_specs + [o_spec, lse_spec, o_spec] + consts + [pl.BlockSpec(memory_space=pl.ANY)],
        out_specs=[pl.BlockSpec((None, l, A_SEG), lambda bb, n: (bb, 0, A_SEG_BLK)),
                   pl.BlockSpec((8, 128), lambda bb, n: (0, 0))],
        out_shape=[jax.ShapeDtypeStruct(dqkv.shape, BF16), jax.ShapeDtypeStruct((8, 128), F32)],
        scratch_shapes=[pltpu.VMEM((l, 128), F32), pltpu.VMEM((l, 128), F32)],
        input_output_aliases={13: 0},
        compiler_params=_params(("arbitrary", "arbitrary")),
    )(qkv, qkv, qkv, qkv, qkv, qkv, qkv, o, lse, do, slopes, sinks, bias, dqkv)


def _fox_mask(qk, ck, i):
    kh = qk.shape[1]
    qpos = i * BLOCK + lax.broadcasted_iota(jnp.int32, (BLOCK, kh), 0)
    kpos = lax.broadcasted_iota(jnp.int32, (BLOCK, kh), 1)
    return jnp.where((kpos <= qpos) & (kpos >= N_PAD), qk - ck, NEG)


def _pick_head(x, hh):
    lo = _lane_lo()
    return jnp.where(lo if hh == 0 else jnp.logical_not(lo), x, jnp.zeros_like(x))


def _both_heads(x):
    return jnp.concatenate([_pick_head(x, 0), _pick_head(x, 1)], axis=0)


def _fox_specs(l):
    pair = pl.BlockSpec((None, l, PAIR_W), lambda bi, hp: (bi, 0, hp))
    half = pl.BlockSpec((None, l, 128), lambda bi, hp: (bi, 0, hp))
    colv = pl.BlockSpec((None, 2, l, 1), lambda bi, hp: (bi, hp, 0, 0))
    rowv = pl.BlockSpec((None, 2, 1, l), lambda bi, hp: (bi, hp, 0, 0))
    return pair, half, colv, rowv


def _fox_fwd(name, qkv, c_col, c_row):
    b, l, _ = qkv.shape
    nb = l // BLOCK

    def body(x_ref, cc_ref, cr_ref, o_ref, lse_ref):
        for i in range(nb):
            rows = slice(i * BLOCK, (i + 1) * BLOCK)
            kh = (i + 1) * BLOCK
            qblk = x_ref[rows, 0:128]
            kv = x_ref[0:kh, 128:256]
            vv = x_ref[0:kh, 256:384]
            qk = lax.dot_general(_both_heads(qblk) * SCALE, kv, _NT, preferred_element_type=F32)
            ps, dens = [], []
            for hh in range(2):
                s = _fox_mask(qk[hh * BLOCK:(hh + 1) * BLOCK], cr_ref[hh, :, 0:kh], i)
                m = jnp.max(s, axis=-1, keepdims=True)
                p = jnp.exp(s - m)
                den = jnp.sum(p, axis=-1, keepdims=True)
                ps.append(p.astype(BF16))
                dens.append(den)
                lse_ref[hh, rows, :] = (m + jnp.log(den)) + cc_ref[hh, rows, :]
            pv = lax.dot_general(jnp.concatenate(ps, axis=0), vv, _NN, preferred_element_type=F32)
            o_ref[rows, :] = jnp.where(_lane_lo(), pv[0:BLOCK] / dens[0], pv[BLOCK:2 * BLOCK] / dens[1]).astype(BF16)

    pair, half, colv, rowv = _fox_specs(l)
    return pl.pallas_call(
        body, name=name, grid=(b, 4), in_specs=[pair, colv, rowv], out_specs=[half, colv],
        out_shape=[jax.ShapeDtypeStruct((b, l, B_WIDTH), BF16), jax.ShapeDtypeStruct((b, B_HEADS, l, 1), F32)],
        compiler_params=_params(("parallel", "parallel")),
    )(qkv, c_col, c_row)


def _fox_bwd(name, qkv, c_col, c_row, o, lse, do):
    b, l, _ = qkv.shape
    nb = l // BLOCK

    def body(x_ref, cc_ref, cr_ref, o_ref, lse_ref, do_ref, dx_ref, dcq_ref, dck_ref, dk_acc, dv_acc):
        dk_acc[...] = jnp.zeros_like(dk_acc)
        dv_acc[...] = jnp.zeros_like(dv_acc)
        dck_ref[...] = jnp.zeros_like(dck_ref)
        for i in range(nb):
            rows = slice(i * BLOCK, (i + 1) * BLOCK)
            kh = (i + 1) * BLOCK
            qblk = x_ref[rows, 0:128]
            kv = x_ref[0:kh, 128:256]
            vv = x_ref[0:kh, 256:384]
            doblk = do_ref[rows, :]
            ov = o_ref[rows, :].astype(F32)
            q2 = _both_heads(qblk) * SCALE
            do2 = _both_heads(doblk)
            qk = lax.dot_general(q2, kv, _NT, preferred_element_type=F32)
            dp = lax.dot_general(do2, vv, _NT, preferred_element_type=F32)
            ps, dss = [], []
            for hh in range(2):
                half = slice(hh * BLOCK, (hh + 1) * BLOCK)
                s = _fox_mask(qk[half], cr_ref[hh, :, 0:kh], i)
                p = jnp.exp(s - (lse_ref[hh, rows, :] - cc_ref[hh, rows, :]))
                dsum = jnp.sum(do2[half].astype(F32) * ov, axis=-1, keepdims=True)
                ds = p * (dp[half] - dsum)
                ps.append(p.astype(BF16))
                dss.append(ds.astype(BF16))
                dcq_ref[hh, rows, :] = jnp.sum(ds, axis=-1, keepdims=True)
                dck_ref[hh, :, 0:kh] -= jnp.sum(ds, axis=0, keepdims=True)
            p2 = jnp.concatenate(ps, axis=0)
            ds2 = jnp.concatenate(dss, axis=0)
            dq = lax.dot_general(ds2, kv, _NN, preferred_element_type=F32) * SCALE
            dk_acc[0:kh, :] += lax.dot_general(ds2, q2, _TN, preferred_element_type=F32)
            dv_acc[0:kh, :] += lax.dot_general(p2, do2, _TN, preferred_element_type=F32)
            dx_ref[rows, 0:128] = jnp.where(_lane_lo(), dq[0:BLOCK], dq[BLOCK:2 * BLOCK]).astype(BF16)
        dx_ref[:, 128:256] = dk_acc[...].astype(BF16)
        dx_ref[:, 256:384] = dv_acc[...].astype(BF16)

    pair, half, colv, rowv = _fox_specs(l)
    return pl.pallas_call(
        body, name=name, grid=(b, 4), in_specs=[pair, colv, rowv, half, colv, half],
        out_specs=[pair, colv, rowv],
        out_shape=[jax.ShapeDtypeStruct(qkv.shape, BF16), jax.ShapeDtypeStruct((b, B_HEADS, l, 1), F32),
                   jax.ShapeDtypeStruct((b, B_HEADS, 1, l), F32)],
        scratch_shapes=[pltpu.VMEM((l, 128), F32), pltpu.VMEM((l, 128), F32)],
        compiler_params=_params(("parallel", "parallel")),
    )(qkv, c_col, c_row, o, lse, do)


_FLIPS = ((0, 0, 1), (0, 1, 0), (0, 1, 1), (1, 0, 0), (1, 0, 1), (1, 1, 0), (1, 1, 1))


def _peer_table():
    x, y, c = lax.axis_index("x"), lax.axis_index("y"), lax.axis_index("c")
    me = 4 * x + 2 * y + c
    peers = []
    for fx, fy, fc in _FLIPS:
        px = 1 - x if fx else x
        py = 1 - y if fy else y
        pc = 1 - c if fc else c
        peers.append(((px, py, pc), 4 * px + 2 * py + pc))
    return me, peers


_HBM = pl.BlockSpec(memory_space=pltpu.HBM)
_SEM = pl.BlockSpec(memory_space=pltpu.SEMAPHORE)
_ANY = pl.BlockSpec(memory_space=pl.ANY)
_EFFECT = pltpu.SideEffectType.DATAFLOW_SIDE_EFFECTING


def _split_copy(srcs_are_pieces, src_refs, land_refs, send_sem, recv_sem, a, kk, me, peers, arriving):
    dev, lin = peers[kk]
    npeer = len(_FLIPS)
    src = src_refs[a] if srcs_are_pieces[a] else src_refs[a].at[lin]
    dst = land_refs[a].at[lin] if arriving else land_refs[a].at[me]
    return pltpu.make_async_remote_copy(src_ref=src, dst_ref=dst, send_sem=send_sem.at[a * npeer + kk],
                                        recv_sem=recv_sem.at[a * npeer + kk], device_id=dev, device_id_type=MESH_ID)


def _xchg_start(name, gather, scatter, after=None):
    me_out = 4 * lax.axis_index("x") + 2 * lax.axis_index("y") + lax.axis_index("c")
    srcs = list(gather) + list(scatter)
    is_piece = [True] * len(gather) + [False] * len(scatter)
    lands = []
    for a, piece in zip(srcs, is_piece):
        own = a[None] if piece else lax.dynamic_slice_in_dim(a, me_out, 1, axis=0)
        shape = ((N_DEV,) + tuple(a.shape)) if piece else tuple(a.shape)
        start = (me_out,) + (0,) * (len(shape) - 1)
        lands.append(lax.dynamic_update_slice(lax.empty(shape, a.dtype), own, start))
    n = len(srcs)
    nsem = n * len(_FLIPS)
    has_after = after is not None

    def body(*refs):
        src_refs = refs[:n]
        land_refs = refs[n:2 * n]
        outs = refs[2 * n + (1 if has_after else 0):]
        send_sem, recv_sem = outs[0], outs[1]
        token = outs[-1]
        me, peers = _peer_table()
        for kk in range(len(_FLIPS)):
            for a in range(n):
                _split_copy(is_piece, src_refs, land_refs, send_sem, recv_sem, a, kk, me, peers, False).start()
        token[...] = jnp.zeros_like(token)

    out_shape = ([pltpu.SemaphoreType.DMA((nsem,)), pltpu.SemaphoreType.DMA((nsem,))]
                 + [pltpu.HBM(tuple(a.shape), a.dtype) for a in srcs] + [pltpu.HBM(tuple(a.shape), a.dtype) for a in lands]
                 + [jax.ShapeDtypeStruct((8, 128), F32)])
    args = [pltpu.with_memory_space_constraint(a, pltpu.HBM) for a in srcs + lands] + ([after] if has_after else [])
    res = pl.pallas_call(
        body, name=name, out_shape=out_shape,
        in_specs=[_HBM] * (2 * n) + ([_ANY] if has_after else []),
        out_specs=[_SEM, _SEM] + [_HBM] * (2 * n) + [pl.BlockSpec(memory_space=pltpu.VMEM)],
        input_output_aliases={i: 2 + i for i in range(2 * n)},
        compiler_params=pltpu.CompilerParams(has_side_effects=_EFFECT),
    )(*args)
    state = (res[0], res[1], list(res[2:2 + n]), list(res[2 + n:2 + 2 * n]), is_piece)
    return state, res[-1]


def _xchg_wait(name, state, after):
    send_sem, recv_sem, srcs, lands, is_piece = state
    n = len(srcs)

    def body(*refs):
        src_refs = refs[:n]
        land_refs = refs[n:2 * n]
        s_sem, r_sem = refs[2 * n], refs[2 * n + 1]
        me, peers = _peer_table()
        for kk in range(len(_FLIPS)):
            for a in range(n):
                cp = _split_copy(is_piece, src_refs, land_refs, s_sem, r_sem, a, kk, me, peers, True)
                cp.wait_send()
                cp.wait_recv()

    out_shape = [pltpu.HBM(tuple(a.shape), a.dtype) for a in srcs] + [pltpu.HBM(tuple(a.shape), a.dtype) for a in lands]
    res = pl.pallas_call(
        body, name=name, out_shape=out_shape,
        in_specs=[_HBM] * (2 * n) + [_SEM, _SEM, _ANY], out_specs=[_HBM] * (2 * n),
        input_output_aliases={i: i for i in range(2 * n)},
        compiler_params=pltpu.CompilerParams(has_side_effects=_EFFECT),
    )(*srcs, *lands, send_sem, recv_sem, after)
    return list(res[n:])


_SIB = (0, 0, 1)
_ICI = ((0, 1, 0), (1, 0, 0), (1, 1, 0))


def _flip(fl):
    x, y, c = lax.axis_index("x"), lax.axis_index("y"), lax.axis_index("c")
    px = 1 - x if fl[0] else x
    py = 1 - y if fl[1] else y
    pc = 1 - c if fl[2] else c
    return (px, py, pc), 4 * px + 2 * py + pc


def _gather2_start(name, pieces, after=None):
    me_out = 4 * lax.axis_index("x") + 2 * lax.axis_index("y") + lax.axis_index("c")
    pieces = list(pieces)
    n = len(pieces)
    lands = [lax.dynamic_update_slice(lax.empty((N_DEV,) + tuple(a.shape), a.dtype), a[None],
                                      (me_out,) + (0,) * a.ndim) for a in pieces]
    first = (_SIB,) + _ICI
    has_after = after is not None

    def body(*refs):
        src_refs, land_refs = refs[:n], refs[n:2 * n]
        outs = refs[2 * n + (1 if has_after else 0):]
        send_sem, recv_sem, token = outs[0], outs[1], outs[-1]
        _, me = _flip((0, 0, 0))
        for kk, fl in enumerate(first):
            dev, _ = _flip(fl)
            for a in range(n):
                pltpu.make_async_remote_copy(src_ref=src_refs[a], dst_ref=land_refs[a].at[me],
                                             send_sem=send_sem.at[a * 4 + kk], recv_sem=recv_sem.at[a * 4 + kk],
                                             device_id=dev, device_id_type=MESH_ID).start()
        token[...] = jnp.zeros_like(token)

    hbm = [pltpu.HBM(tuple(a.shape), a.dtype) for a in pieces + lands]
    res = pl.pallas_call(
        body, name=name,
        out_shape=[pltpu.SemaphoreType.DMA((4 * n,)), pltpu.SemaphoreType.DMA((4 * n,))] + hbm
        + [jax.ShapeDtypeStruct((8, 128), F32)],
        in_specs=[_HBM] * (2 * n) + ([_ANY] if has_after else []),
        out_specs=[_SEM, _SEM] + [_HBM] * (2 * n) + [pl.BlockSpec(memory_space=pltpu.VMEM)],
        input_output_aliases={i: 2 + i for i in range(2 * n)},
        compiler_params=pltpu.CompilerParams(has_side_effects=_EFFECT),
    )(*([pltpu.with_memory_space_constraint(a, pltpu.HBM) for a in pieces + lands] + ([after] if has_after else [])))
    return (res[0], res[1], list(res[2:2 + n]), list(res[2 + n:2 + 2 * n])), res[-1]


def _gather2_forward(name, state, after):
    send_a, recv_a, pieces, lands = state
    n = len(pieces)
    first = (_SIB,) + _ICI

    def body(*refs):
        src_refs, land_refs = refs[:n], refs[n:2 * n]
        s_a, r_a = refs[2 * n], refs[2 * n + 1]
        outs = refs[2 * n + 3:]
        send_b, recv_b, token = outs[0], outs[1], outs[-1]
        for kk, fl in enumerate(first):
            dev, lin = _flip(fl)
            for a in range(n):
                cp = pltpu.make_async_remote_copy(src_ref=src_refs[a], dst_ref=land_refs[a].at[lin],
                                                  send_sem=s_a.at[a * 4 + kk], recv_sem=r_a.at[a * 4 + kk],
                                                  device_id=dev, device_id_type=MESH_ID)
                cp.wait_send()
                cp.wait_recv()
        sib, _ = _flip(_SIB)
        for j, fl in enumerate(_ICI):
            _, lin = _flip(fl)
            for a in range(n):
                pltpu.make_async_remote_copy(src_ref=land_refs[a].at[lin], dst_ref=land_refs[a].at[lin],
                                             send_sem=send_b.at[a * 3 + j], recv_sem=recv_b.at[a * 3 + j],
                                             device_id=sib, device_id_type=MESH_ID).start()
        token[...] = jnp.zeros_like(token)

    hbm = [pltpu.HBM(tuple(a.shape), a.dtype) for a in pieces + lands]
    res = pl.pallas_call(
        body, name=name,
        out_shape=[pltpu.SemaphoreType.DMA((3 * n,)), pltpu.SemaphoreType.DMA((3 * n,))] + hbm
        + [jax.ShapeDtypeStruct((8, 128), F32)],
        in_specs=[_HBM] * (2 * n) + [_SEM, _SEM, _ANY],
        out_specs=[_SEM, _SEM] + [_HBM] * (2 * n) + [pl.BlockSpec(memory_space=pltpu.VMEM)],
        input_output_aliases={i: 2 + i for i in range(2 * n)},
        compiler_params=pltpu.CompilerParams(has_side_effects=_EFFECT),
    )(*pieces, *lands, send_a, recv_a, after)
    return (res[0], res[1], list(res[2 + n:2 + 2 * n])), res[-1]


def _gather2_wait(name, state, after):
    send_b, recv_b, lands = state
    n = len(lands)

    def body(*refs):
        land_refs = refs[:n]
        s_b, r_b = refs[n], refs[n + 1]
        sib, _ = _flip(_SIB)
        for j, fl in enumerate(_ICI):
            _, sent = _flip(fl)
            _, arriving = _flip((fl[0], fl[1], 1))
            for a in range(n):
                cp = pltpu.make_async_remote_copy(src_ref=land_refs[a].at[sent], dst_ref=land_refs[a].at[arriving],
                                                  send_sem=s_b.at[a * 3 + j], recv_sem=r_b.at[a * 3 + j],
                                                  device_id=sib, device_id_type=MESH_ID)
                cp.wait_send()
                cp.wait_recv()

    res = pl.pallas_call(
        body, name=name, out_shape=[pltpu.HBM(tuple(a.shape), a.dtype) for a in lands],
        in_specs=[_HBM] * n + [_SEM, _SEM, _ANY], out_specs=[_HBM] * n,
        input_output_aliases={i: i for i in range(n)},
        compiler_params=pltpu.CompilerParams(has_side_effects=_EFFECT),
    )(*lands, send_b, recv_b, after)
    return list(res)


def _adam_math(w, g, m, v):
    m = ADAM_B1 * m + (1.0 - ADAM_B1) * g
    v = ADAM_B2 * v + (1.0 - ADAM_B2) * (g * g)
    m_hat = m / (1.0 - ADAM_B1 ** ADAM_STEP)
    v_hat = v / (1.0 - ADAM_B2 ** ADAM_STEP)
    delta = -ADAM_LR * (m_hat / (jnp.sqrt(v_hat) + ADAM_EPS) + ADAM_WD * w)
    return delta, m, v


def _adam(name, w, m, v, parts, transposed=False, dep=None):
    npart, _, cp = parts.shape
    has_dep = dep is not None
    if transposed:
        c, r = w.shape
        tr = _pick(r, (256, 128))
        blk = pl.BlockSpec((c, tr), lambda i: (0, i))
    else:
        r, c = w.shape
        tr = _pick(r, (256, 176, 128, 64, 16, 8, 1))
        blk = pl.BlockSpec((tr, c), lambda i: (i, 0))

    def body(w_ref, m_ref, v_ref, p_ref, *rest):
        g_ref, d_ref, mo_ref, vo_ref = rest[-4:]
        g = p_ref[0].astype(F32)
        for pp in range(1, npart):
            g = g + p_ref[pp].astype(F32)
        g = g.T[0:c, :] if transposed else g[:, 0:c]
        delta, mn, vn = _adam_math(w_ref[...], g, m_ref[...], v_ref[...])
        g_ref[...] = g
        d_ref[...] = delta
        mo_ref[...] = mn
        vo_ref[...] = vn

    out = jax.ShapeDtypeStruct(w.shape, F32)
    return pl.pallas_call(
        body, name=name, grid=(r // tr,),
        in_specs=[blk, blk, blk, pl.BlockSpec((npart, tr, cp), lambda i: (0, i, 0))]
        + ([pl.BlockSpec(memory_space=pl.ANY)] if has_dep else []),
        out_specs=[blk, blk, blk, blk], out_shape=[out, out, out, out], compiler_params=_params(("parallel",)),
    )(*((w, m, v, parts) + ((dep,) if has_dep else ())))


def _small_update(name, packs, wmv):
    nparam = len(wmv)

    def body(p_ref, *refs):
        ins, outs = refs[:3 * nparam], refs[3 * nparam:]
        tot = p_ref[0]
        for pp in range(1, N_DEV):
            tot = tot + p_ref[pp]
        outs[0][0:8, :] = tot[0:8, :]
        outs[0][8:24, :] = tot[8:24, :] + tot[24:40, :]
        grads = [tot[i:i + 1, :] for i in range(4)] + [tot[4:5, 0:B_HEADS], tot[4:5, B_HEADS:2 * B_HEADS]]
        for i, g in enumerate(grads):
            w_ref, m_ref, v_ref = ins[3 * i:3 * i + 3]
            delta, mn, vn = _adam_math(w_ref[...], g, m_ref[...], v_ref[...])
            for o_ref, val in zip(outs[1 + 4 * i:5 + 4 * i], (g, delta, mn, vn)):
                o_ref[...] = val

    flat = [a for trio in wmv for a in trio]
    out_shape = [jax.ShapeDtypeStruct((24, D_MODEL), F32)]
    for w, _, _ in wmv:
        out_shape += [jax.ShapeDtypeStruct(w.shape, F32)] * 4
    res = pl.pallas_call(body, name=name, out_shape=out_shape, compiler_params=_params())(packs, *flat)
    return res[0], [tuple(res[1 + 4 * i:5 + 4 * i]) for i in range(nparam)]


def _local_step(x, tgt, g1, gm, g2, gf, b_forget, sinks, weights, send):
    b, s, _ = x.shape
    l = s + PREFIX
    t = b * l
    (meta,) = weights("meta", x)

    h0, n1 = _embed_norm("embed_rms1_fwd", x, meta, g1)
    (w1i,) = weights("ffn1_in", n1)
    gu1, a1 = _ffn_in_fwd("ffn1_in_fwd", n1, w1i)
    (w1o,) = weights("ffn1_out", weights("ffn1_out:forward", a1))
    h1, um = _mm_res_norm("ffn1_out_fwd", a1, w1o, h0, gm, alpha=0.5)
    wi, wa, wb, wo = weights("mix", weights("mix:forward", um))
    qkv, gates, f2 = _proj_fwd("proj_fwd", um, wi)
    qkv3 = qkv.reshape(b, l, QKV_W)
    f3 = f2.reshape(b, l, 128)
    bf_row = jnp.pad(b_forget, ((0, 0), (0, 128 - B_HEADS)))
    c_col, c_row = _fgate_fwd("fgate_fwd", f3, bf_row)
    head_of_row = jnp.arange(STACK) // BLOCK
    slopes = jnp.exp2(-8.0 * (head_of_row + 1).astype(F32) / A_HEADS).reshape(STACK, 1)
    sink_rows = jnp.repeat(sinks.reshape(A_HEADS), BLOCK).reshape(STACK, 1)
    swa_bias = _swa_bias(slopes)
    oa3, lse_a = _swa_fwd("swa_fwd", qkv3, slopes, sink_rows, swa_bias)
    ob3, lse_b = _fox_fwd("fox_fwd", qkv3, c_col, c_row)
    oa = oa3.reshape(t, A_WIDTH)
    ob = ob3.reshape(t, B_WIDTH)
    mixed, ya, yb = _branch_gate_fwd("branch_gate_fwd", oa, ob, wa, wb, gates, dep=weights("ffn2:forward", ob))
    h2, n2 = _mm_res_norm("mix_out_fwd", mixed, wo, h1, g2)
    w2i, w2o = weights("ffn2", h2)
    gu2, a2 = _ffn_in_fwd("ffn2_in_fwd", n2, w2i)

    dh3, loss_blk, dgf = _ffn_out_loss("ffn2_out_loss", a2, w2o, h2, gf, tgt, 0.5)

    def ffn_bwd(tag, dh, h_in, g_norm, n_in, gu, a, w_in_blk, w_out, one_send, examples=None):
        dw_out = _mm_tn(tag + "_out_bwd_w", a, dh, alpha=0.5)
        dgu = _ffn_out_bwd_x(tag + "_out_bwd_x", dh, w_out, gu, dep=None if one_send else send(tag + "_out", (dw_out,)))
        dw_in = _ffn_in_bwd_w(tag + "_in_bwd_w", n_in, dgu)
        token = send(tag, (dw_in, dw_out)) if one_send else send(tag + "_in", (dw_in,))
        return _ffn_in_bwd_x(tag + "_in_bwd_x", dgu, w_in_blk, h_in, g_norm, dh, dep=token, examples=examples)

    dh2, dg2 = ffn_bwd("ffn2", dh3, h2, g2, n2, gu2, a2, w2i, w2o, True)

    dwo = _mm_tn("mix_out_bwd_w", mixed, dh2)
    dya, dyb, dgates = _mix_out_gate_bwd("mix_out_gate_bwd", dh2, wo, gates, ya, yb)
    doa, dob = _branch_bwd_x("branch_bwd_x", dya, dyb, wa, wb)
    dwa, dwb = _branch_bwd_w("branch_bwd_w", oa, ob, dya, dyb)
    dqkv3, dcq, dck = _fox_bwd("fox_bwd", qkv3, c_col, c_row, ob3, lse_b, dob.reshape(b, l, B_WIDTH))
    dqkv3, dsink = _swa_bwd("swa_bwd", qkv3, oa3, lse_a, doa.reshape(b, l, A_WIDTH), slopes, sink_rows, swa_bias, dqkv3)
    dqkv = dqkv3.reshape(t, QKV_W)
    df3, dbf = _fgate_bwd("fgate_bwd", f3, bf_row, dcq, dck)
    df = df3.reshape(t, 128)
    dwi_qkv = _mm_tn("proj_qkv_bwd_w", um, dqkv, tm_c=(512,), tn_c=(768,))
    dwi_g = _mm_tn("proj_gates_bwd_w", um, dgates, tm_c=(512,), tn_c=(512,))
    dwi_f = _mm_tn("proj_f_bwd_w", um, df, tm_c=(512,), tn_c=(128,))
    token = send("mix", (dwi_qkv, dwi_g, dwi_f, dwa, dwb, dwo))
    dh1, dgm = _proj_bwd_x("proj_bwd_x", dqkv, dgates, df, wi, h1, gm, dh2, dep=token)

    grad_x, dmeta3, dg1 = ffn_bwd("ffn1", dh1, h0, g1, n1, gu1, a1, w1i, w1o, False, examples=b)
    dmeta = dmeta3.reshape(b * N_META, D_MODEL)

    misc = jnp.concatenate([dbf[:, 0:B_HEADS], dsink[:, 0].reshape(1, A_HEADS), loss_blk[0:1, 0:1]], axis=1)
    misc = jnp.pad(misc, ((0, 0), (0, D_MODEL - misc.shape[1])))
    row = lax.broadcasted_iota(jnp.int32, (8, D_MODEL), 0)
    vec = jnp.zeros((8, D_MODEL), F32)
    for i, piece in enumerate((dg1, dgm, dg2, dgf, misc)):
        vec = jnp.where(row == i, piece, vec)
    small = jnp.concatenate([vec, dmeta], axis=0)
    return grad_x, small


def _pad_to(a, rows, cols):
    return jnp.pad(a, ((0, rows - a.shape[0]), (0, cols - a.shape[1])))


def _ffn_out_from_gathered(name, g):
    def body(g_ref, o_ref):
        o_ref[0:FFO_SHARD, :] = g_ref[0]
        o_ref[FFO_SHARD:FF_SHARD, :] = g_ref[1]
        o_ref[FF_SHARD:FF_SHARD_P, :] = jnp.zeros((FF_SHARD_P - FF_SHARD, D_MODEL), BF16)

    return pl.pallas_call(
        body, name=name, grid=(4,),
        in_specs=[pl.BlockSpec((2, FFO_SHARD, D_MODEL), lambda j: (j, 0, 0))],
        out_specs=pl.BlockSpec((FF_SHARD_P, D_MODEL), lambda j: (j, 0)),
        out_shape=jax.ShapeDtypeStruct((D_FF_P, D_MODEL), BF16), compiler_params=_params(("parallel",)),
    )(g)


def _ffn_out_to_scatter(dw):
    return dw.reshape(4, FF_SHARD_P, D_MODEL)[:, 0:FF_SHARD, :].reshape(N_DEV, FFO_SHARD, D_MODEL)


def _proj_segments():
    segs = [(HEAD_DIM * h, HEAD_DIM, 0, B_SEG + HEAD_DIM * A_HEAD_ORDER.index(h)) for h in range(A_HEADS)]
    segs += [(512, 128, 0, B_SEG + A_WIDTH), (640, 128, 0, B_SEG + A_WIDTH + 128)]
    for first, off in ((768, 0), (1280, 128), (1792, 256)):
        segs += [(first + 128 * hp, 128, 0, PAIR_W * hp + off) for hp in range(4)]
    segs += [(2304, B_HEADS, 2, 0), (2312, 2 * D_MODEL, 1, 0)]
    return segs


_RELAYOUT_ROWS = 256


def _proj_from_gathered(name, g):
    rows = _RELAYOUT_ROWS

    def body(g_ref, o_ref):
        def cols(first, width):
            out = []
            for p in range(N_DEV):
                lo, hi = max(first, WIN_SHARD * p), min(first + width, WIN_SHARD * (p + 1))
                if lo < hi:
                    out.append(g_ref[p, :, lo - WIN_SHARD * p:hi - WIN_SHARD * p])
            return out

        parts = []
        for arr in (0, 1, 2):
            for first, width, _, _ in sorted((s for s in _proj_segments() if s[2] == arr), key=lambda s: s[3]):
                parts += cols(first, width)
            if arr == 0:
                parts.append(jnp.zeros((rows, P_GATES - QKV_W), BF16))
        parts.append(jnp.zeros((rows, 128 - B_HEADS), BF16))
        o_ref[...] = jnp.concatenate(parts, axis=1)

    return pl.pallas_call(
        body, name=name, grid=(D_MODEL // rows,),
        in_specs=[pl.BlockSpec((N_DEV, rows, WIN_SHARD_P), lambda i: (0, i, 0))],
        out_specs=pl.BlockSpec((rows, PROJ_P), lambda i: (i, 0)),
        out_shape=jax.ShapeDtypeStruct((D_MODEL, PROJ_P), BF16), compiler_params=_params(("parallel",)),
    )(g)


def _proj_to_scatter(name, dqkv_w, dg_w, df_w):
    rows = _RELAYOUT_ROWS
    segs = sorted(_proj_segments())

    def body(q_ref, g_ref, f_ref, o_ref):
        arrays = (q_ref, g_ref, f_ref)
        for p in range(N_DEV):
            parts = []
            for first, width, arr, at in segs:
                lo, hi = max(first, WIN_SHARD * p), min(first + width, WIN_SHARD * (p + 1))
                if lo < hi:
                    parts.append(arrays[arr][:, at + lo - first:at + hi - first])
            parts.append(jnp.zeros((rows, WIN_SHARD_P - WIN_SHARD), BF16))
            o_ref[p] = jnp.concatenate(parts, axis=1)

    return pl.pallas_call(
        body, name=name, grid=(D_MODEL // rows,),
        in_specs=[pl.BlockSpec((rows, QKV_W), lambda i: (i, 0)), pl.BlockSpec((rows, 2 * D_MODEL), lambda i: (i, 0)),
                  pl.BlockSpec((rows, 128), lambda i: (i, 0))],
        out_specs=pl.BlockSpec((N_DEV, rows, WIN_SHARD_P), lambda i: (0, i, 0)),
        out_shape=jax.ShapeDtypeStruct((N_DEV, D_MODEL, WIN_SHARD_P), BF16), compiler_params=_params(("parallel",)),
    )(dqkv_w, dg_w, df_w)


def _a_rows_from_natural(w):
    return jnp.concatenate([w[HEAD_DIM * h:HEAD_DIM * (h + 1)] for h in A_HEAD_ORDER], axis=0)


def _a_rows_to_natural(w):
    return jnp.concatenate([w[HEAD_DIM * A_HEAD_ORDER.index(h):HEAD_DIM * (A_HEAD_ORDER.index(h) + 1)]
                            for h in range(A_HEADS)], axis=0)


def kernel(x, meta_tokens, ffn1_norm, ffn1_w_in, ffn1_w_out, mix_norm, w_in, b_forget, attn_sinks, w_branch_a, w_branch_b, w_out, ffn2_norm, ffn2_w_in, ffn2_w_out, final_norm, loss_target, m_meta_tokens, m_ffn1_norm, m_ffn1_w_in, m_ffn1_w_out, m_mix_norm, m_w_in, m_b_forget, m_attn_sinks, m_w_branch_a, m_w_branch_b, m_w_out, m_ffn2_norm, m_ffn2_w_in, m_ffn2_w_out, m_final_norm, v_meta_tokens, v_ffn1_norm, v_ffn1_w_in, v_ffn1_w_out, v_mix_norm, v_w_in, v_b_forget, v_attn_sinks, v_w_branch_a, v_w_branch_b, v_w_out, v_ffn2_norm, v_ffn2_w_in, v_ffn2_w_out, v_final_norm):
    me = 4 * lax.axis_index("x") + 2 * lax.axis_index("y") + lax.axis_index("c")

    def shard(w, tok, rows=None, cols=None):
        piece = (w[0] if tok is None else w[0] + tok[0, 0]).astype(BF16)
        return piece if rows is None else _pad_to(piece, rows, cols)

    first_level, second_level = {}, {}
    first_level["meta"], tok = _gather2_start("gather_meta_start", (meta_tokens,))
    first_level["ffn1_in"], tok = _gather2_start(
        "gather_ffn1_in_start", (shard(ffn1_w_in, None, D_MODEL, FF_SHARD_P),), after=tok)
    first_level["ffn1_out"], tok = _gather2_start("gather_ffn1_out_start", (shard(ffn1_w_out, tok),), after=tok)
    first_level["mix"], tok = _gather2_start(
        "gather_mix_start", (shard(w_in, tok, D_MODEL, WIN_SHARD_P), shard(w_branch_a, tok), shard(w_branch_b, tok),
                             shard(w_out, tok)), after=tok)
    first_level["ffn2"], tok = _gather2_start(
        "gather_ffn2_start", (shard(ffn2_w_in, tok, D_MODEL, FF_SHARD_P), shard(ffn2_w_out, tok)), after=tok)
    started = {"tok": tok}

    def weights(group, after):
        if group.endswith(":forward"):
            group = group[:-len(":forward")]
            second_level[group], token = _gather2_forward("gather_" + group + "_forward", first_level[group], after)
            return token
        if group == "meta":
            after = weights("meta:forward", started["tok"])
        if group == "ffn1_in":
            after = weights("ffn1_in:forward", after)
        got = _gather2_wait("gather_" + group + "_wait", second_level[group], after)
        if group == "mix":
            gwi, gwa, gwb, gwo = got
            return (_proj_from_gathered("proj_w_relayout", gwi), _a_rows_from_natural(gwa.transpose(1, 0, 2).reshape(A_WIDTH, D_MODEL)),
                    gwb.transpose(1, 0, 2).reshape(B_WIDTH, D_MODEL), gwo.reshape(D_MODEL, D_MODEL))
        if group == "ffn1_out":
            return (_ffn_out_from_gathered("ffn1_w_out_relayout", got[0]),)
        if group == "meta":
            return (got[0].transpose(1, 0, 2).reshape(N_META, D_MODEL),)
        if group == "ffn1_in":
            return (got[0].reshape(2, 4, D_MODEL, FF_SHARD_P),)
        return got[0].reshape(2, 4, D_MODEL, FF_SHARD_P), _ffn_out_from_gathered("ffn2_w_out_relayout", got[1])

    scatter_state = {}

    def send(group, grads):
        if group == "mix":
            dwi_qkv, dwi_g, dwi_f, dwa, dwb, dwo = grads
            dwa = _a_rows_to_natural(dwa)
            blocks = (_proj_to_scatter("proj_dw_relayout", dwi_qkv, dwi_g, dwi_f), dwa.reshape(A_WIDTH, N_DEV, 128).transpose(1, 0, 2),
                      dwb.reshape(B_WIDTH, N_DEV, 128).transpose(1, 0, 2), dwo.reshape(N_DEV, 128, D_MODEL))
        elif group.endswith("_in"):
            blocks = (grads[0].reshape(N_DEV, D_MODEL, FF_SHARD_P),)
        elif group.endswith("_out"):
            blocks = (_ffn_out_to_scatter(grads[0]),)
        else:
            blocks = (grads[0].reshape(N_DEV, D_MODEL, FF_SHARD_P), _ffn_out_to_scatter(grads[1]))
        scatter_state[group], token = _xchg_start("scatter_" + group + "_start", (), blocks)
        return token

    gf = final_norm.reshape(1, D_MODEL)
    grad_x, small = _local_step(x, loss_target, ffn1_norm, mix_norm, ffn2_norm, gf, b_forget, attn_sinks, weights, send)

    small_state, after = _xchg_start("gather_small_start", (small,), ())
    out = {}
    updates = (
        ("ffn2", (("ffn2_w_in", ffn2_w_in, m_ffn2_w_in, v_ffn2_w_in), ("ffn2_w_out", ffn2_w_out, m_ffn2_w_out, v_ffn2_w_out))),
        ("ffn1_out", (("ffn1_w_out", ffn1_w_out, m_ffn1_w_out, v_ffn1_w_out),)),
        ("mix", (("w_in", w_in, m_w_in, v_w_in), ("w_branch_a", w_branch_a, m_w_branch_a, v_w_branch_a),
                 ("w_branch_b", w_branch_b, m_w_branch_b, v_w_branch_b), ("w_out", w_out, m_w_out, v_w_out))),
    )
    last_update = ("ffn1_in", (("ffn1_w_in", ffn1_w_in, m_ffn1_w_in, v_ffn1_w_in),))

    def update(group, members, after):
        parts_list = _xchg_wait("scatter_" + group + "_wait", scatter_state[group], after)
        prev = None
        for (nm, w, m, v), parts in zip(members, parts_list):
            if nm.endswith("w_in"):
                res4 = _adam("adam_" + nm, w[0].T, m[0].T, v[0].T, parts, transposed=True, dep=prev)
                out[nm] = tuple(r.T[None] for r in res4)
            else:
                res4 = _adam("adam_" + nm, w[0], m[0], v[0], parts, dep=prev)
                out[nm] = tuple(r[None] for r in res4)
            prev = res4[0]
        return prev

    for group, members in updates + (last_update,):
        after = update(group, members, after)
    (packs,) = _xchg_wait("gather_small_wait", small_state, after)

    row = lambda a: a.reshape(1, D_MODEL)
    tot, small_res = _small_update("small_update", packs, (
        (ffn1_norm, m_ffn1_norm, v_ffn1_norm), (mix_norm, m_mix_norm, v_mix_norm), (ffn2_norm, m_ffn2_norm, v_ffn2_norm),
        (row(final_norm), row(m_final_norm), row(v_final_norm)), (b_forget, m_b_forget, v_b_forget),
        (attn_sinks, m_attn_sinks, v_attn_sinks)))
    for nm, res4 in zip(("ffn1_norm", "mix_norm", "ffn2_norm", "final_norm", "b_forget", "attn_sinks"), small_res):
        out[nm] = tuple(r.reshape(D_MODEL) for r in res4) if nm == "final_norm" else res4
    loss = tot[4, 2 * B_HEADS]
    g_meta = lax.dynamic_slice(tot[8:24, :], (0, me * 128), (N_META, 128))
    out["meta_tokens"] = tuple(_adam("adam_meta_tokens", meta_tokens, m_meta_tokens, v_meta_tokens, g_meta[None]))

    names = ("meta_tokens", "ffn1_norm", "ffn1_w_in", "ffn1_w_out", "mix_norm", "w_in", "b_forget", "attn_sinks",
             "w_branch_a", "w_branch_b", "w_out", "ffn2_norm", "ffn2_w_in", "ffn2_w_out", "final_norm")
    return (loss, grad_x) + tuple(out[nm][kind] for kind in range(4) for nm in names)
```

```python
import jax
import jax.numpy as jnp
from jax import lax
from jax.experimental import pallas as pl
from jax.experimental.pallas import tpu as pltpu

F32 = jnp.float32
BF16 = jnp.bfloat16

D_MODEL = 1024
N_META = 16
BLOCK = 128
PREFIX = 128
N_PAD = PREFIX - N_META
HEAD_DIM = 64
A_HEADS = 8
B_HEADS = 8
A_WIDTH = 512
A_KV_WIDTH = 128
B_WIDTH = 512
D_FF = 2816
N_DEV = 8
FF_SHARD = 2 * D_FF // N_DEV
FF_SHARD_P = 768
FFO_SHARD = D_FF // N_DEV
D_FF_P = 4 * FF_SHARD_P
W_IN_COLS = 4360
WIN_SHARD = W_IN_COLS // N_DEV
WIN_SHARD_P = 640
PAIR_W = 3 * 128
B_SEG = 4 * PAIR_W
A_SEG = A_WIDTH + 2 * A_KV_WIDTH
QKV_W = B_SEG + A_SEG
P_GATES = 2 * (2 * D_MODEL)
P_F = P_GATES + 2 * D_MODEL
PROJ_P = P_F + 128
A_HEAD_ORDER = (0, 4, 1, 5, 2, 6, 3, 7)
EPS = 1e-6
NEG = -1e30
SCALE = HEAD_DIM ** -0.5
ADAM_LR = 0.001
ADAM_B1 = 0.9
ADAM_B2 = 0.999
ADAM_EPS = 1e-08
ADAM_WD = 0.01
ADAM_STEP = 10
VMEM_LIMIT = 56 * 1024 * 1024
MESH_ID = pl.DeviceIdType.MESH
SMALL_ROWS = 40

_NN = (((1,), (0,)), ((), ()))
_NT = (((1,), (1,)), ((), ()))
_TN = (((0,), (0,)), ((), ()))


def _params(sem=None):
    return pltpu.CompilerParams(dimension_semantics=sem, vmem_limit_bytes=VMEM_LIMIT)


def _pick(n, cands):
    for c in cands:
        if n % c == 0:
            return c
    raise ValueError(f"no tile for {n}")


def _bf(v):
    return v if v.dtype == BF16 else v.astype(BF16)


def _mm(name, a, b, dims, grid, a_spec, b_spec, o_spec, out_shape, out_dtype, alpha=1.0):
    def body(a_ref, b_ref, o_ref):
        acc = lax.dot_general(_bf(a_ref[...]), _bf(b_ref[...]), dims, preferred_element_type=F32)
        if alpha != 1.0:
            acc = acc * alpha
        o_ref[...] = acc.astype(o_ref.dtype)

    return pl.pallas_call(
        body, name=name, grid=grid, in_specs=[a_spec, b_spec], out_specs=o_spec,
        out_shape=jax.ShapeDtypeStruct(out_shape, out_dtype),
        compiler_params=_params(("parallel",) * len(grid)),
    )(a, b)


def _mm_res_norm(name, a, w, res, g_next, alpha=1.0):
    t, k = a.shape
    tm = _pick(t, (544, 384, 256, 128))

    def body(a_ref, w_ref, r_ref, g_ref, h_ref, n_ref):
        acc = lax.dot_general(_bf(a_ref[...]), w_ref[...], _NN, preferred_element_type=F32)
        if alpha != 1.0:
            acc = acc * alpha
        hv = acc + r_ref[...]
        h_ref[...] = hv
        r = lax.rsqrt(jnp.mean(hv * hv, axis=-1, keepdims=True) + EPS)
        n_ref[...] = ((hv * r) * g_ref[...]).astype(BF16)

    row = pl.BlockSpec((tm, D_MODEL), lambda i: (i, 0))
    return pl.pallas_call(
        body, name=name, grid=(t // tm,),
        in_specs=[pl.BlockSpec((tm, k), lambda i: (i, 0)), pl.BlockSpec((k, D_MODEL), lambda i: (0, 0)), row,
                  pl.BlockSpec((1, D_MODEL), lambda i: (0, 0))],
        out_specs=[row, row],
        out_shape=[jax.ShapeDtypeStruct((t, D_MODEL), F32), jax.ShapeDtypeStruct((t, D_MODEL), BF16)],
        compiler_params=_params(("parallel",)),
    )(a, w, res, g_next)


def _mm_tn(name, a, b, out_dtype=BF16, alpha=1.0, tm_c=(768, 512, 256, 128), tn_c=(512, 640, 256, 128)):
    t, m = a.shape
    n = b.shape[1]
    tm = _pick(m, tm_c)
    tn = _pick(n, tn_c)
    bytes_a, bytes_b = a.size * a.dtype.itemsize, b.size * b.dtype.itemsize
    if bytes_a + bytes_b * (m // tm) <= bytes_b + bytes_a * (n // tn):
        return _mm(name, a, b, _TN, (m // tm, n // tn),
                   pl.BlockSpec((t, tm), lambda i, j: (0, i)), pl.BlockSpec((t, tn), lambda i, j: (0, j)),
                   pl.BlockSpec((tm, tn), lambda i, j: (i, j)), (m, n), out_dtype, alpha=alpha)
    return _mm(name, a, b, _TN, (n // tn, m // tm),
               pl.BlockSpec((t, tm), lambda j, i: (0, i)), pl.BlockSpec((t, tn), lambda j, i: (0, j)),
               pl.BlockSpec((tm, tn), lambda j, i: (i, j)), (m, n), out_dtype, alpha=alpha)


def _ffn_in_fwd(name, n, wblk, dep=None):
    t = n.shape[0]
    tm = _pick(t, (1088, 768, 512, 256, 128))
    has_dep = dep is not None

    def body(n_ref, w_ref, *rest):
        gu_ref, a_ref = rest[-2], rest[-1]
        nv = n_ref[...]
        g = lax.dot_general(nv, w_ref[0], _NN, preferred_element_type=F32)
        u = lax.dot_general(nv, w_ref[1], _NN, preferred_element_type=F32)
        sg = jax.nn.sigmoid(g)
        silu = g * sg
        a_ref[...] = (silu * u).astype(BF16)
        gu_ref[0] = ((0.5 * u) * (sg + silu * (1.0 - sg))).astype(BF16)
        gu_ref[1] = (0.5 * silu).astype(BF16)

    return pl.pallas_call(
        body, name=name, grid=(t // tm, 4),
        in_specs=[pl.BlockSpec((tm, D_MODEL), lambda i, j: (i, 0)),
                  pl.BlockSpec((2, None, D_MODEL, FF_SHARD_P), lambda i, j: (0, j, 0, 0))]
        + ([pl.BlockSpec(memory_space=pl.ANY)] if has_dep else []),
        out_specs=[pl.BlockSpec((2, tm, FF_SHARD_P), lambda i, j: (0, i, j)),
                   pl.BlockSpec((tm, FF_SHARD_P), lambda i, j: (i, j))],
        out_shape=[jax.ShapeDtypeStruct((2, t, D_FF_P), BF16), jax.ShapeDtypeStruct((t, D_FF_P), BF16)],
        compiler_params=_params(("parallel", "parallel")),
    )(*((n, wblk) + ((dep,) if has_dep else ())))


def _ffn_out_bwd_x(name, dh, w_out, gu, dep=None):
    t = dh.shape[0]
    tm = _pick(t, (1088, 768, 512, 256, 128))
    has_dep = dep is not None

    def body(dh_ref, w_ref, gu_ref, *rest):
        o_ref = rest[-1]
        da = lax.dot_general(_bf(dh_ref[...]), w_ref[...], _NT, preferred_element_type=F32)
        o_ref[0] = (da * gu_ref[0].astype(F32)).astype(BF16)
        o_ref[1] = (da * gu_ref[1].astype(F32)).astype(BF16)

    gu_spec = pl.BlockSpec((2, tm, FF_SHARD_P), lambda i, j: (0, i, j))
    return pl.pallas_call(
        body, name=name, grid=(t // tm, 4),
        in_specs=[pl.BlockSpec((tm, D_MODEL), lambda i, j: (i, 0)), pl.BlockSpec((FF_SHARD_P, D_MODEL), lambda i, j: (j, 0)),
                  gu_spec] + ([pl.BlockSpec(memory_space=pl.ANY)] if has_dep else []),
        out_specs=gu_spec, out_shape=jax.ShapeDtypeStruct((2, t, D_FF_P), BF16),
        compiler_params=_params(("parallel", "parallel")),
    )(*((dh, w_out, gu) + ((dep,) if has_dep else ())))


def _rms_bwd_rows(dn, h, g, dres):
    r = lax.rsqrt(jnp.mean(h * h, axis=-1, keepdims=True) + EPS)
    tv = dn * g
    dot = jnp.mean(tv * h, axis=-1, keepdims=True)
    return dres + (r * tv - h * (r * r * r * dot)), jnp.sum(dn * (h * r), axis=0, keepdims=True)


def _accumulate_rows(ref, part, first):
    @pl.when(first)
    def _():
        ref[...] = part

    @pl.when(jnp.logical_not(first))
    def _():
        ref[...] += part


def _ffn_in_bwd_x(name, dgu, wblk, h_in, g_norm, dres, dep=None, examples=None):
    t = dgu.shape[1]
    tm = _pick(t, (544, 384, 256, 128))
    has_dep = dep is not None
    split = examples is not None
    if split:
        l = t // examples
        per = l // tm
        assert per * tm == l and tm > PREFIX

    def body(d_ref, w_ref, h_ref, g_ref, r_ref, *rest):
        acc = None
        for s in range(2):
            for j in range(4):
                part = lax.dot_general(d_ref[s, :, FF_SHARD_P * j:FF_SHARD_P * (j + 1)], w_ref[s, j], _NT,
                                       preferred_element_type=F32)
                acc = part if acc is None else acc + part
        dh, dg = _rms_bwd_rows(acc, h_ref[...], g_ref[...], r_ref[...])
        i = pl.program_id(0)
        if not split:
            dh_ref, dg_ref = rest[-2], rest[-1]
            dh_ref[...] = dh
        else:
            gx_ref, meta_ref, dg_ref, buf, sem = rest[-5:]

            def out_copy(step):
                bi, r = step // per, step % per
                head = pltpu.make_async_copy(buf.at[pl.ds(PREFIX, tm - PREFIX)], gx_ref.at[bi, pl.ds(0, tm - PREFIX)], sem)
                if per == 1:
                    return r == 0, head, None
                return r == 0, head, pltpu.make_async_copy(
                    buf, gx_ref.at[bi, pl.ds(pl.multiple_of(jnp.maximum(r, 1) * tm - PREFIX, 8), tm)], sem)

            def run(step, method):
                is_head, head, later = out_copy(step)

                @pl.when(is_head)
                def _():
                    getattr(head, method)()

                if later is not None:
                    @pl.when(jnp.logical_not(is_head))
                    def _():
                        getattr(later, method)()

            @pl.when(i > 0)
            def _():
                run(i - 1, "wait")

            buf[...] = dh

            @pl.when(i % per == 0)
            def _():
                meta_ref[...] = dh[N_PAD:PREFIX]

            run(i, "start")

            @pl.when(i == pl.num_programs(0) - 1)
            def _():
                run(i, "wait")

        _accumulate_rows(dg_ref, dg, i == 0)

    row = pl.BlockSpec((tm, D_MODEL), lambda i: (i, 0))
    vec = pl.BlockSpec((1, D_MODEL), lambda i: (0, 0))
    if split:
        out_specs = [pl.BlockSpec(memory_space=pl.ANY), pl.BlockSpec((None, N_META, D_MODEL), lambda i: (i // per, 0, 0)), vec]
        out_shape = [jax.ShapeDtypeStruct((examples, l - PREFIX, D_MODEL), F32),
                     jax.ShapeDtypeStruct((examples, N_META, D_MODEL), F32), jax.ShapeDtypeStruct((1, D_MODEL), F32)]
        scratch = [pltpu.VMEM((tm, D_MODEL), F32), pltpu.SemaphoreType.DMA(())]
    else:
        out_specs = [row, vec]
        out_shape = [jax.ShapeDtypeStruct((t, D_MODEL), F32), jax.ShapeDtypeStruct((1, D_MODEL), F32)]
        scratch = []
    return pl.pallas_call(
        body, name=name, grid=(t // tm,),
        in_specs=[pl.BlockSpec((2, tm, D_FF_P), lambda i: (0, i, 0)),
                  pl.BlockSpec((2, 4, D_MODEL, FF_SHARD_P), lambda i: (0, 0, 0, 0), pipeline_mode=pl.Buffered(1)),
                  row, vec, row]
        + ([pl.BlockSpec(memory_space=pl.ANY)] if has_dep else []),
        out_specs=out_specs, out_shape=out_shape, scratch_shapes=scratch,
        compiler_params=_params(("arbitrary",)),
    )(*((dgu, wblk, h_in, g_norm, dres) + ((dep,) if has_dep else ())))


def _proj_fwd(name, um, wi):
    t = um.shape[0]
    tm = _pick(t, (544, 384, 256, 128))

    def body(u_ref, w_ref, q_ref, g_ref, f_ref):
        uv = u_ref[...]
        q_ref[...] = lax.dot_general(uv, w_ref[:, 0:QKV_W], _NN, preferred_element_type=F32).astype(BF16)
        g_ref[...] = lax.dot_general(uv, w_ref[:, P_GATES:P_F], _NN, preferred_element_type=F32)
        f_ref[...] = lax.dot_general(uv, w_ref[:, P_F:PROJ_P], _NN, preferred_element_type=F32)

    return pl.pallas_call(
        body, name=name, grid=(t // tm,),
        in_specs=[pl.BlockSpec((tm, D_MODEL), lambda i: (i, 0)),
                  pl.BlockSpec((D_MODEL, PROJ_P), lambda i: (0, 0), pipeline_mode=pl.Buffered(1))],
        out_specs=[pl.BlockSpec((tm, QKV_W), lambda i: (i, 0)), pl.BlockSpec((tm, 2 * D_MODEL), lambda i: (i, 0)),
                   pl.BlockSpec((tm, 128), lambda i: (i, 0))],
        out_shape=[jax.ShapeDtypeStruct((t, QKV_W), BF16), jax.ShapeDtypeStruct((t, 2 * D_MODEL), F32),
                   jax.ShapeDtypeStruct((t, 128), F32)],
        compiler_params=_params(("parallel",)),
    )(um, wi)


def _proj_bwd_x(name, dqkv, dgates, df, wi, h_in, g_norm, dres, dep=None):
    t = dqkv.shape[0]
    tm = _pick(t, (544, 384, 256, 128))
    has_dep = dep is not None

    def body(q_ref, gt_ref, f_ref, w_ref, h_ref, g_ref, r_ref, *rest):
        dh_ref, dg_ref = rest[-2], rest[-1]
        acc = lax.dot_general(q_ref[...], w_ref[:, 0:QKV_W], _NT, preferred_element_type=F32)
        acc = acc + lax.dot_general(gt_ref[...], w_ref[:, P_GATES:P_F], _NT, preferred_element_type=F32)
        acc = acc + lax.dot_general(f_ref[...], w_ref[:, P_F:PROJ_P], _NT, preferred_element_type=F32)
        dh, dg = _rms_bwd_rows(acc, h_ref[...], g_ref[...], r_ref[...])
        dh_ref[...] = dh
        _accumulate_rows(dg_ref, dg, pl.program_id(0) == 0)

    row = pl.BlockSpec((tm, D_MODEL), lambda i: (i, 0))
    vec = pl.BlockSpec((1, D_MODEL), lambda i: (0, 0))
    return pl.pallas_call(
        body, name=name, grid=(t // tm,),
        in_specs=[pl.BlockSpec((tm, QKV_W), lambda i: (i, 0)), pl.BlockSpec((tm, 2 * D_MODEL), lambda i: (i, 0)),
                  pl.BlockSpec((tm, 128), lambda i: (i, 0)),
                  pl.BlockSpec((D_MODEL, PROJ_P), lambda i: (0, 0), pipeline_mode=pl.Buffered(1)), row, vec, row]
        + ([pl.BlockSpec(memory_space=pl.ANY)] if has_dep else []),
        out_specs=[row, vec],
        out_shape=[jax.ShapeDtypeStruct((t, D_MODEL), F32), jax.ShapeDtypeStruct((1, D_MODEL), F32)],
        compiler_params=_params(("arbitrary",)),
    )(*((dqkv, dgates, df, wi, h_in, g_norm, dres) + ((dep,) if has_dep else ())))


def _ffn_in_bwd_w(name, n, dgu):
    t = n.shape[0]
    return _mm(name, n, dgu, _TN, (2, 4),
               pl.BlockSpec((t, D_MODEL), lambda s, j: (0, 0)),
               pl.BlockSpec((None, t, FF_SHARD_P), lambda s, j: (s, 0, j)),
               pl.BlockSpec((None, None, D_MODEL, FF_SHARD_P), lambda s, j: (s, j, 0, 0)),
               (2, 4, D_MODEL, FF_SHARD_P), BF16)


def _embed_norm(name, x, meta, g):
    b, s, _ = x.shape
    half = (s + PREFIX) // 2
    first = half - PREFIX
    assert first > 0 and half % 16 == 0

    def body(x_ref, m_ref, g_ref, h_ref, n_ref, buf, sem):
        bi, k = pl.program_id(0), pl.program_id(1)

        def tokens(example, second, method):
            if second:
                cp = pltpu.make_async_copy(x_ref.at[example, pl.ds(first, half)], buf.at[1], sem.at[1])
            else:
                cp = pltpu.make_async_copy(x_ref.at[example, pl.ds(0, first)], buf.at[0, pl.ds(PREFIX, first)], sem.at[0])
            getattr(cp, method)()

        @pl.when((bi == 0) & (k == 0))
        def _():
            tokens(0, False, "start")

        @pl.when(k == 0)
        def _():
            tokens(bi, True, "start")
            tokens(bi, False, "wait")
            buf[0, 0:N_PAD, :] = jnp.zeros((N_PAD, D_MODEL), F32)
            buf[0, N_PAD:PREFIX, :] = m_ref[...]

        @pl.when(k == 1)
        def _():
            @pl.when(bi + 1 < b)
            def _():
                tokens(bi + 1, False, "start")

            tokens(bi, True, "wait")

        hv = buf[k]
        h_ref[...] = hv
        r = lax.rsqrt(jnp.mean(hv * hv, axis=-1, keepdims=True) + EPS)
        n_ref[...] = ((hv * r) * g_ref[...]).astype(BF16)

    rows = pl.BlockSpec((half, D_MODEL), lambda bi, k: (2 * bi + k, 0))
    return pl.pallas_call(
        body, name=name, grid=(b, 2),
        in_specs=[pl.BlockSpec(memory_space=pl.ANY), pl.BlockSpec((N_META, D_MODEL), lambda bi, k: (0, 0)),
                  pl.BlockSpec((1, D_MODEL), lambda bi, k: (0, 0))],
        out_specs=[rows, rows],
        out_shape=[jax.ShapeDtypeStruct((2 * b * half, D_MODEL), F32), jax.ShapeDtypeStruct((2 * b * half, D_MODEL), BF16)],
        scratch_shapes=[pltpu.VMEM((2, half, D_MODEL), F32), pltpu.SemaphoreType.DMA((2,))],
        compiler_params=_params(("arbitrary", "arbitrary")),
    )(x, meta, g)


def _branch_gate_fwd(name, oa, ob, wa, wb, gates, dep=None):
    t = gates.shape[0]
    tm = _pick(t, (544, 384, 256, 128))
    has_dep = dep is not None

    def body(oa_ref, ob_ref, wa_ref, wb_ref, g_ref, *rest):
        o_ref, ya_ref, yb_ref = rest[-3:]
        ya = lax.dot_general(_bf(oa_ref[...]), wa_ref[...], _NN, preferred_element_type=F32)
        yb = lax.dot_general(_bf(ob_ref[...]), wb_ref[...], _NN, preferred_element_type=F32)
        sa = jax.nn.sigmoid(g_ref[:, 0:D_MODEL])
        sb = jax.nn.sigmoid(g_ref[:, D_MODEL:2 * D_MODEL])
        o_ref[...] = (sa * ya + sb * yb).astype(BF16)
        ya_ref[...] = ya.astype(BF16)
        yb_ref[...] = yb.astype(BF16)

    blk = pl.BlockSpec((tm, D_MODEL), lambda i: (i, 0))
    narrow = pl.BlockSpec((tm, A_WIDTH), lambda i: (i, 0))
    wide = pl.BlockSpec((tm, 2 * D_MODEL), lambda i: (i, 0))
    wspec = pl.BlockSpec((A_WIDTH, D_MODEL), lambda i: (0, 0))
    out = jax.ShapeDtypeStruct((t, D_MODEL), BF16)
    return pl.pallas_call(
        body, name=name, grid=(t // tm,),
        in_specs=[narrow, narrow, wspec, wspec, wide] + ([pl.BlockSpec(memory_space=pl.ANY)] if has_dep else []),
        out_specs=[blk, blk, blk], out_shape=[out, out, out], compiler_params=_params(("parallel",)),
    )(*((oa, ob, wa, wb, gates) + ((dep,) if has_dep else ())))


def _branch_bwd_x(name, dya, dyb, wa, wb):
    t = dya.shape[0]
    tm = _pick(t, (1088, 768, 512, 256, 128))

    def body(da_ref, db_ref, wa_ref, wb_ref, oa_ref, ob_ref):
        oa_ref[...] = lax.dot_general(da_ref[...], wa_ref[...], _NT, preferred_element_type=F32)
        ob_ref[...] = lax.dot_general(db_ref[...], wb_ref[...], _NT, preferred_element_type=F32).astype(BF16)

    blk = pl.BlockSpec((tm, D_MODEL), lambda i: (i, 0))
    narrow = pl.BlockSpec((tm, A_WIDTH), lambda i: (i, 0))
    wspec = pl.BlockSpec((A_WIDTH, D_MODEL), lambda i: (0, 0), pipeline_mode=pl.Buffered(1))
    return pl.pallas_call(
        body, name=name, grid=(t // tm,), in_specs=[blk, blk, wspec, wspec], out_specs=[narrow, narrow],
        out_shape=[jax.ShapeDtypeStruct((t, A_WIDTH), F32), jax.ShapeDtypeStruct((t, B_WIDTH), BF16)],
        compiler_params=_params(("parallel",)),
    )(dya, dyb, wa, wb)


def _branch_bwd_w(name, oa, ob, dya, dyb):
    t = oa.shape[0]
    tn = 512

    def body(oa_ref, ob_ref, da_ref, db_ref, wa_ref, wb_ref):
        wa_ref[...] = lax.dot_general(_bf(oa_ref[...]), da_ref[...], _TN, preferred_element_type=F32).astype(BF16)
        wb_ref[...] = lax.dot_general(_bf(ob_ref[...]), db_ref[...], _TN, preferred_element_type=F32).astype(BF16)

    whole = pl.BlockSpec((t, A_WIDTH), lambda j: (0, 0), pipeline_mode=pl.Buffered(1))
    cols = pl.BlockSpec((t, tn), lambda j: (0, j))
    out_spec = pl.BlockSpec((A_WIDTH, tn), lambda j: (0, j))
    out = jax.ShapeDtypeStruct((A_WIDTH, D_MODEL), BF16)
    return pl.pallas_call(
        body, name=name, grid=(D_MODEL // tn,), in_specs=[whole, whole, cols, cols], out_specs=[out_spec, out_spec],
        out_shape=[out, out], compiler_params=_params(("parallel",)),
    )(oa, ob, dya, dyb)


def _mix_out_gate_bwd(name, dh, wo, gates, ya, yb):
    t = gates.shape[0]
    tm = _pick(t, (544, 384, 256, 128))

    def body(dh_ref, w_ref, g_ref, ya_ref, yb_ref, dya_ref, dyb_ref, dg_ref):
        dm = lax.dot_general(_bf(dh_ref[...]), w_ref[...], _NT, preferred_element_type=F32)
        sa = jax.nn.sigmoid(g_ref[:, 0:D_MODEL])
        sb = jax.nn.sigmoid(g_ref[:, D_MODEL:2 * D_MODEL])
        dya_ref[...] = (dm * sa).astype(BF16)
        dyb_ref[...] = (dm * sb).astype(BF16)
        dg_ref[:, 0:D_MODEL] = (dm * ya_ref[...].astype(F32) * (sa * (1.0 - sa))).astype(BF16)
        dg_ref[:, D_MODEL:2 * D_MODEL] = (dm * yb_ref[...].astype(F32) * (sb * (1.0 - sb))).astype(BF16)

    blk = pl.BlockSpec((tm, D_MODEL), lambda i: (i, 0))
    wide = pl.BlockSpec((tm, 2 * D_MODEL), lambda i: (i, 0))
    out = jax.ShapeDtypeStruct((t, D_MODEL), BF16)
    return pl.pallas_call(
        body, name=name, grid=(t // tm,),
        in_specs=[blk, pl.BlockSpec((D_MODEL, D_MODEL), lambda i: (0, 0)), wide, blk, blk], out_specs=[blk, blk, wide],
        out_shape=[out, out, jax.ShapeDtypeStruct((t, 2 * D_MODEL), BF16)], compiler_params=_params(("parallel",)),
    )(dh, wo, gates, ya, yb)


def _ffn_out_loss(name, a, w, res, gf, tgt, alpha):
    t, k = a.shape
    b, s, _ = tgt.shape
    l = t // b
    tm = _pick(t, (544, 384, 256, 128))
    per = l // tm
    assert per * tm == l and tm > PREFIX and l - PREFIX == s

    def body(a_ref, w_ref, r_ref, g_ref, t_ref, dh_ref, loss_ref, dg_ref, buf, sem):
        i = pl.program_id(0)
        bi, r = i // per, i % per

        def first_rows():
            return pltpu.make_async_copy(t_ref.at[bi, pl.ds(0, tm - PREFIX)], buf.at[pl.ds(PREFIX, tm - PREFIX)], sem)

        def later_rows():
            return pltpu.make_async_copy(t_ref.at[bi, pl.ds(pl.multiple_of(r * tm - PREFIX, 8), tm)], buf, sem)

        @pl.when(r == 0)
        def _():
            buf[0:PREFIX, :] = jnp.zeros((PREFIX, D_MODEL), F32)
            first_rows().start()

        if per > 1:
            @pl.when(r > 0)
            def _():
                later_rows().start()

        acc = lax.dot_general(a_ref[...], w_ref[...], _NN, preferred_element_type=F32)

        @pl.when(r == 0)
        def _():
            first_rows().wait()

        if per > 1:
            @pl.when(r > 0)
            def _():
                later_rows().wait()

        hv = acc * alpha + r_ref[...]
        row = lax.broadcasted_iota(jnp.int32, (tm, 1), 0)
        real = ((r > 0) | (row >= PREFIX)).astype(F32)
        g = g_ref[...]
        rn = lax.rsqrt(jnp.mean(hv * hv, axis=-1, keepdims=True) + EPS)
        xn = hv * rn
        err = (xn * g - buf[...]) * real
        lpart = 0.5 * jnp.sum(jnp.mean(err * err, axis=-1, keepdims=True), axis=0, keepdims=True)
        dy = err * (1.0 / D_MODEL)
        tv = dy * g
        dot = jnp.mean(tv * hv, axis=-1, keepdims=True)
        dh_ref[...] = rn * tv - hv * (rn * rn * rn * dot)
        gpart = jnp.sum(dy * xn, axis=0, keepdims=True)

        @pl.when(i == 0)
        def _():
            loss_ref[...] = jnp.zeros_like(loss_ref)
            dg_ref[...] = jnp.zeros_like(dg_ref)

        loss_ref[...] += jnp.broadcast_to(lpart, loss_ref.shape)
        dg_ref[...] += gpart

    row_spec = pl.BlockSpec((tm, D_MODEL), lambda i: (i, 0))
    vec = pl.BlockSpec((1, D_MODEL), lambda i: (0, 0))
    return pl.pallas_call(
        body, name=name, grid=(t // tm,),
        in_specs=[pl.BlockSpec((tm, k), lambda i: (i, 0)),
                  pl.BlockSpec((k, D_MODEL), lambda i: (0, 0), pipeline_mode=pl.Buffered(1)), row_spec, vec,
                  pl.BlockSpec(memory_space=pl.ANY)],
        out_specs=[row_spec, pl.BlockSpec((8, 128), lambda i: (0, 0)), vec],
        out_shape=[jax.ShapeDtypeStruct((t, D_MODEL), F32), jax.ShapeDtypeStruct((8, 128), F32),
                   jax.ShapeDtypeStruct((1, D_MODEL), F32)],
        scratch_shapes=[pltpu.VMEM((tm, D_MODEL), F32), pltpu.SemaphoreType.DMA(())],
        compiler_params=_params(("arbitrary",)),
    )(a, w, res, gf, tgt)


def _fgate_fwd(name, f3, bf_row):
    b, l, _ = f3.shape
    nb = l // BLOCK

    def body(f_ref, b_ref, cc_ref, cr_ref):
        r_i = lax.broadcasted_iota(jnp.int32, (BLOCK, BLOCK), 0)
        c_i = lax.broadcasted_iota(jnp.int32, (BLOCK, BLOCK), 1)
        tri = (r_i >= c_i).astype(F32)
        carry = jnp.zeros((1, 128), F32)
        for blk in range(nb):
            rows = slice(blk * BLOCK, (blk + 1) * BLOCK)
            z = f_ref[rows, :] + b_ref[...]
            lf = jnp.minimum(z, 0.0) - jnp.log(1.0 + jnp.exp(-jnp.abs(z)))
            cb = jnp.dot(tri, lf, preferred_element_type=F32, precision=lax.Precision.HIGHEST) + carry
            carry = cb[BLOCK - 1:BLOCK, :]
            cbt = cb.T
            for hh in range(B_HEADS):
                cc_ref[hh, rows, :] = jnp.sum(jnp.where(c_i == hh, cb, 0.0), axis=1, keepdims=True)
                cr_ref[hh, :, rows] = cbt[hh:hh + 1, :]

    return pl.pallas_call(
        body, name=name, grid=(b,),
        in_specs=[pl.BlockSpec((None, l, 128), lambda bi: (bi, 0, 0)),
                  pl.BlockSpec((1, 128), lambda bi: (0, 0))],
        out_specs=[pl.BlockSpec((None, B_HEADS, l, 1), lambda bi: (bi, 0, 0, 0)),
                   pl.BlockSpec((None, B_HEADS, 1, l), lambda bi: (bi, 0, 0, 0))],
        out_shape=[jax.ShapeDtypeStruct((b, B_HEADS, l, 1), F32), jax.ShapeDtypeStruct((b, B_HEADS, 1, l), F32)],
        compiler_params=_params(("parallel",)),
    )(f3, bf_row)


def _fgate_bwd(name, f3, bf_row, dcq, dck):
    b, l, _ = f3.shape
    nb = l // BLOCK

    def body(f_ref, b_ref, dcq_ref, dck_ref, df_ref, db_ref):
        r_i = lax.broadcasted_iota(jnp.int32, (BLOCK, BLOCK), 0)
        c_i = lax.broadcasted_iota(jnp.int32, (BLOCK, BLOCK), 1)
        tri = (r_i <= c_i).astype(F32)
        carry = jnp.zeros((1, 128), F32)
        total = jnp.zeros((1, 128), F32)
        for blk in range(nb - 1, -1, -1):
            rows = slice(blk * BLOCK, (blk + 1) * BLOCK)
            krows = jnp.concatenate([dck_ref[hh, :, rows] for hh in range(B_HEADS)]
                                    + [jnp.zeros((BLOCK - B_HEADS, BLOCK), F32)], axis=0)
            dcb = krows.T
            for hh in range(B_HEADS):
                dcb = dcb + jnp.where(c_i == hh, dcq_ref[hh, rows, :], 0.0)
            rc = jnp.dot(tri, dcb, preferred_element_type=F32, precision=lax.Precision.HIGHEST) + carry
            carry = rc[0:1, :]
            z = f_ref[rows, :] + b_ref[...]
            df = rc * (1.0 / (1.0 + jnp.exp(z)))
            df_ref[rows, :] = df.astype(BF16)
            total = total + jnp.sum(df, axis=0, keepdims=True)

        @pl.when(pl.program_id(0) == 0)
        def _():
            db_ref[...] = total

        @pl.when(pl.program_id(0) > 0)
        def _():
            db_ref[...] += total

    return pl.pallas_call(
        body, name=name, grid=(b,),
        in_specs=[pl.BlockSpec((None, l, 128), lambda bi: (bi, 0, 0)),
                  pl.BlockSpec((1, 128), lambda bi: (0, 0)),
                  pl.BlockSpec((None, B_HEADS, l, 1), lambda bi: (bi, 0, 0, 0)),
                  pl.BlockSpec((None, B_HEADS, 1, l), lambda bi: (bi, 0, 0, 0))],
        out_specs=[pl.BlockSpec((None, l, 128), lambda bi: (bi, 0, 0)), pl.BlockSpec((1, 128), lambda bi: (0, 0))],
        out_shape=[jax.ShapeDtypeStruct((b, l, 128), BF16), jax.ShapeDtypeStruct((1, 128), F32)],
        compiler_params=_params(("arbitrary",)),
    )(f3, bf_row, dcq, dck)


A_Q_BLK = B_SEG // A_WIDTH
A_K_BLK = (B_SEG + A_WIDTH) // 128
A_V_BLK = A_K_BLK + 1
A_SEG_BLK = B_SEG // A_SEG
STACK = A_HEADS * BLOCK


def _lane_lo():
    return lax.broadcasted_iota(jnp.int32, (1, 128), 1) < HEAD_DIM


def _stack_heads(x, masked):
    lo = _lane_lo()
    blks = [x[:, 128 * j:128 * (j + 1)] for j in range(4)]
    if not masked:
        return jnp.concatenate(blks + blks, axis=0)
    zero = jnp.zeros_like(blks[0])
    return jnp.concatenate([jnp.where(lo, bk, zero) for bk in blks] + [jnp.where(lo, zero, bk) for bk in blks], axis=0)


def _unstack_heads(y):
    lo = _lane_lo()
    return jnp.concatenate([jnp.where(lo, y[128 * j:128 * (j + 1)], y[128 * (4 + j):128 * (5 + j)]) for j in range(4)], axis=1)


def _swa_bias(slopes):
    r_i = jnp.arange(STACK)[:, None]
    c_i = jnp.arange(3 * BLOCK)[None, :]
    seg = c_i >> 7
    out = []
    for n in range(3):
        qpos = n * BLOCK + (r_i & (BLOCK - 1))
        kpos = jnp.where(seg == 0, c_i, (n - 2) * BLOCK + c_i)
        dist = qpos - kpos
        band = (seg != 0) & (dist < BLOCK) & (kpos >= PREFIX)
        meta = (seg == 0) & (c_i >= N_PAD)
        out.append(jnp.where((dist >= 0) & (band | meta), -slopes * dist.astype(F32), NEG))
    return jnp.stack(out, axis=0)


def _swa_scores(q, kcat, n, slope, bias):
    s = lax.dot_general(q, kcat, _NT, preferred_element_type=F32) + bias
    further = slope * (-BLOCK * jnp.maximum(n - 2, 0)).astype(F32)
    return jnp.concatenate([s[:, 0:BLOCK] + further, s[:, BLOCK:]], axis=1)


def _swa_specs():
    def kv(col_blk):
        return [pl.BlockSpec((None, BLOCK, 128), lambda b, n: (b, 0, col_blk)),
                pl.BlockSpec((None, BLOCK, 128), lambda b, n: (b, jnp.maximum(n - 1, 0), col_blk)),
                pl.BlockSpec((None, BLOCK, 128), lambda b, n: (b, n, col_blk))]

    q_spec = pl.BlockSpec((None, BLOCK, A_WIDTH), lambda b, n: (b, n, A_Q_BLK))
    o_spec = pl.BlockSpec((None, BLOCK, A_WIDTH), lambda b, n: (b, n, 0))
    col = pl.BlockSpec((STACK, 1), lambda b, n: (0, 0))
    bias = pl.BlockSpec((None, STACK, 3 * BLOCK), lambda b, n: (jnp.minimum(n, 2), 0, 0))
    lse_spec = pl.BlockSpec((None, A_HEADS, BLOCK, 1), lambda b, n: (b, 0, n, 0))
    return q_spec, kv(A_K_BLK), kv(A_V_BLK), o_spec, [col, col, bias], lse_spec


def _swa_fwd(name, qkv, slopes, sinks, bias):
    b, l, _ = qkv.shape
    nb = l // BLOCK

    def body(q_ref, k0_ref, kp_ref, kc_ref, v0_ref, vp_ref, vc_ref, sl_ref, sk_ref, bias_ref, o_ref, lse_ref):
        n = pl.program_id(1)
        qs = _stack_heads(q_ref[...], True) * SCALE
        kcat = jnp.concatenate([k0_ref[...], kp_ref[...], kc_ref[...]], axis=0)
        vcat = jnp.concatenate([v0_ref[...], vp_ref[...], vc_ref[...]], axis=0)
        s = _swa_scores(qs, kcat, n, sl_ref[...], bias_ref[...])
        sink = sk_ref[...]
        m = jnp.maximum(jnp.max(s, axis=-1, keepdims=True), sink)
        p = jnp.exp(s - m)
        den = jnp.sum(p, axis=-1, keepdims=True) + jnp.exp(sink - m)
        o = lax.dot_general(p.astype(BF16), vcat, _NN, preferred_element_type=F32) / den
        o_ref[...] = _unstack_heads(o)
        lse_ref[...] = (m + jnp.log(den)).reshape(A_HEADS, BLOCK, 1)

    q_spec, k_specs, v_specs, o_spec, consts, lse_spec = _swa_specs()
    return pl.pallas_call(
        body, name=name, grid=(b, nb),
        in_specs=[q_spec] + k_specs + v_specs + consts, out_specs=[o_spec, lse_spec],
        out_shape=[jax.ShapeDtypeStruct((b, l, A_WIDTH), F32), jax.ShapeDtypeStruct((b, A_HEADS, l, 1), F32)],
        compiler_params=_params(("parallel", "parallel")),
    )(qkv, qkv, qkv, qkv, qkv, qkv, qkv, slopes, sinks, bias)


def _swa_bwd(name, qkv, o, lse, do, slopes, sinks, bias, dqkv):
    b, l, _ = qkv.shape
    nb = l // BLOCK

    def body(q_ref, k0_ref, kp_ref, kc_ref, v0_ref, vp_ref, vc_ref, o_ref, lse_ref, do_ref, sl_ref, sk_ref, bias_ref, _,
             dx_ref, ds_ref, dk_acc, dv_acc):
        bi = pl.program_id(0)
        n = pl.program_id(1)
        qs = _stack_heads(q_ref[...], True) * SCALE
        dos32 = _stack_heads(do_ref[...], True)
        dos = dos32.astype(BF16)
        os_ = _stack_heads(o_ref[...], False)
        lsev = lse_ref[...].reshape(STACK, 1)
        kcat = jnp.concatenate([k0_ref[...], kp_ref[...], kc_ref[...]], axis=0)
        vcat = jnp.concatenate([v0_ref[...], vp_ref[...], vc_ref[...]], axis=0)
        s = _swa_scores(qs, kcat, n, sl_ref[...], bias_ref[...])
        p = jnp.exp(s - lsev)
        dsum = jnp.sum(dos32 * os_, axis=-1, keepdims=True)
        dp = lax.dot_general(dos, vcat, _NT, preferred_element_type=F32)
        dsc = (p * (dp - dsum)).astype(BF16)
        dq = lax.dot_general(dsc, kcat, _NN, preferred_element_type=F32) * SCALE
        row0 = pl.multiple_of(n * BLOCK, BLOCK)
        dx_ref[pl.ds(row0, BLOCK), 0:A_WIDTH] = _unstack_heads(dq).astype(BF16)
        dkc = lax.dot_general(dsc, qs, _TN, preferred_element_type=F32)
        dvc = lax.dot_general(p.astype(BF16), dos, _TN, preferred_element_type=F32)

        @pl.when(n == 0)
        def _():
            dk_acc[...] = jnp.zeros_like(dk_acc)
            dv_acc[...] = jnp.zeros_like(dv_acc)

        starts = (0, pl.multiple_of(jnp.maximum(n - 1, 0) * BLOCK, BLOCK), row0)
        for t, st in enumerate(starts):
            dk_acc[pl.ds(st, BLOCK), :] += dkc[t * BLOCK:(t + 1) * BLOCK, :]
            dv_acc[pl.ds(st, BLOCK), :] += dvc[t * BLOCK:(t + 1) * BLOCK, :]

        @pl.when(n == nb - 1)
        def _():
            dx_ref[:, A_WIDTH:A_WIDTH + 128] = dk_acc[...].astype(BF16)
            dx_ref[:, A_WIDTH + 128:A_SEG] = dv_acc[...].astype(BF16)

        dsink = -(jnp.exp(sk_ref[...] - lsev) * dsum)
        r8 = lax.broadcasted_iota(jnp.int32, (8, 128), 0)
        acc = jnp.zeros((8, 128), F32)
        for hh in range(A_HEADS):
            acc = acc + jnp.where(r8 == hh, jnp.sum(dsink[hh * BLOCK:(hh + 1) * BLOCK, :]), 0.0)

        @pl.when((bi == 0) & (n == 0))
        def _():
            ds_ref[...] = jnp.zeros_like(ds_ref)

        ds_ref[...] += acc

    q_spec, k_specs, v_specs, o_spec, consts, lse_spec = _swa_specs()
    return pl.pallas_call(
        body, name=name, grid=(b, nb),
        in_specs=[q_spec] + k_specs + v_specs + [o_spec, lse_spec, o_spec] + consts + [pl.BlockSpec(memory_space=pl.ANY)],
        out_specs=[pl.BlockSpec((None, l, A_SEG), lambda bb, n: (bb, 0, A_SEG_BLK)),
                   pl.BlockSpec((8, 128), lambda bb, n: (0, 0))],
        out_shape=[jax.ShapeDtypeStruct(dqkv.shape, BF16), jax.ShapeDtypeStruct((8, 128), F32)],
        scratch_shapes=[pltpu.VMEM((l, 128), F32), pltpu.VMEM((l, 128), F32)],
        input_output_aliases={13: 0},
        compiler_params=_params(("arbitrary", "arbitrary")),
    )(qkv, qkv, qkv, qkv, qkv, qkv, qkv, o, lse, do, slopes, sinks, bias, dqkv)


def _fox_mask(qk, ck, i):
    kh = qk.shape[1]
    qpos = i * BLOCK + lax.broadcasted_iota(jnp.int32, (BLOCK, kh), 0)
    kpos = lax.broadcasted_iota(jnp.int32, (BLOCK, kh), 1)
    return jnp.where((kpos <= qpos) & (kpos >= N_PAD), qk - ck, NEG)


def _pick_head(x, hh):
    lo = _lane_lo()
    return jnp.where(lo if hh == 0 else jnp.logical_not(lo), x, jnp.zeros_like(x))


def _both_heads(x):
    return jnp.concatenate([_pick_head(x, 0), _pick_head(x, 1)], axis=0)


def _fox_specs(l):
    pair = pl.BlockSpec((None, l, PAIR_W), lambda bi, hp: (bi, 0, hp))
    half = pl.BlockSpec((None, l, 128), lambda bi, hp: (bi, 0, hp))
    colv = pl.BlockSpec((None, 2, l, 1), lambda bi, hp: (bi, hp, 0, 0))
    rowv = pl.BlockSpec((None, 2, 1, l), lambda bi, hp: (bi, hp, 0, 0))
    return pair, half, colv, rowv


def _fox_fwd(name, qkv, c_col, c_row):
    b, l, _ = qkv.shape
    nb = l // BLOCK

    def body(x_ref, cc_ref, cr_ref, o_ref, lse_ref):
        for i in range(nb):
            rows = slice(i * BLOCK, (i + 1) * BLOCK)
            kh = (i + 1) * BLOCK
            qblk = x_ref[rows, 0:128]
            kv = x_ref[0:kh, 128:256]
            vv = x_ref[0:kh, 256:384]
            qk = lax.dot_general(_both_heads(qblk) * SCALE, kv, _NT, preferred_element_type=F32)
            ps, dens = [], []
            for hh in range(2):
                s = _fox_mask(qk[hh * BLOCK:(hh + 1) * BLOCK], cr_ref[hh, :, 0:kh], i)
                m = jnp.max(s, axis=-1, keepdims=True)
                p = jnp.exp(s - m)
                den = jnp.sum(p, axis=-1, keepdims=True)
                ps.append(p.astype(BF16))
                dens.append(den)
                lse_ref[hh, rows, :] = (m + jnp.log(den)) + cc_ref[hh, rows, :]
            pv = lax.dot_general(jnp.concatenate(ps, axis=0), vv, _NN, preferred_element_type=F32)
            o_ref[rows, :] = jnp.where(_lane_lo(), pv[0:BLOCK] / dens[0], pv[BLOCK:2 * BLOCK] / dens[1]).astype(BF16)

    pair, half, colv, rowv = _fox_specs(l)
    return pl.pallas_call(
        body, name=name, grid=(b, 4), in_specs=[pair, colv, rowv], out_specs=[half, colv],
        out_shape=[jax.ShapeDtypeStruct((b, l, B_WIDTH), BF16), jax.ShapeDtypeStruct((b, B_HEADS, l, 1), F32)],
        compiler_params=_params(("parallel", "parallel")),
    )(qkv, c_col, c_row)


def _fox_bwd(name, qkv, c_col, c_row, o, lse, do):
    b, l, _ = qkv.shape
    nb = l // BLOCK

    def body(x_ref, cc_ref, cr_ref, o_ref, lse_ref, do_ref, dx_ref, dcq_ref, dck_ref, dk_acc, dv_acc):
        dk_acc[...] = jnp.zeros_like(dk_acc)
        dv_acc[...] = jnp.zeros_like(dv_acc)
        dck_ref[...] = jnp.zeros_like(dck_ref)
        for i in range(nb):
            rows = slice(i * BLOCK, (i + 1) * BLOCK)
            kh = (i + 1) * BLOCK
            qblk = x_ref[rows, 0:128]
            kv = x_ref[0:kh, 128:256]
            vv = x_ref[0:kh, 256:384]
            doblk = do_ref[rows, :]
            ov = o_ref[rows, :].astype(F32)
            q2 = _both_heads(qblk) * SCALE
            do2 = _both_heads(doblk)
            qk = lax.dot_general(q2, kv, _NT, preferred_element_type=F32)
            dp = lax.dot_general(do2, vv, _NT, preferred_element_type=F32)
            ps, dss = [], []
            for hh in range(2):
                half = slice(hh * BLOCK, (hh + 1) * BLOCK)
                s = _fox_mask(qk[half], cr_ref[hh, :, 0:kh], i)
                p = jnp.exp(s - (lse_ref[hh, rows, :] - cc_ref[hh, rows, :]))
                dsum = jnp.sum(do2[half].astype(F32) * ov, axis=-1, keepdims=True)
                ds = p * (dp[half] - dsum)
                ps.append(p.astype(BF16))
                dss.append(ds.astype(BF16))
                dcq_ref[hh, rows, :] = jnp.sum(ds, axis=-1, keepdims=True)
                dck_ref[hh, :, 0:kh] -= jnp.sum(ds, axis=0, keepdims=True)
            p2 = jnp.concatenate(ps, axis=0)
            ds2 = jnp.concatenate(dss, axis=0)
            dq = lax.dot_general(ds2, kv, _NN, preferred_element_type=F32) * SCALE
            dk_acc[0:kh, :] += lax.dot_general(ds2, q2, _TN, preferred_element_type=F32)
            dv_acc[0:kh, :] += lax.dot_general(p2, do2, _TN, preferred_element_type=F32)
            dx_ref[rows, 0:128] = jnp.where(_lane_lo(), dq[0:BLOCK], dq[BLOCK:2 * BLOCK]).astype(BF16)
        dx_ref[:, 128:256] = dk_acc[...].astype(BF16)
        dx_ref[:, 256:384] = dv_acc[...].astype(BF16)

    pair, half, colv, rowv = _fox_specs(l)
    return pl.pallas_call(
        body, name=name, grid=(b, 4), in_specs=[pair, colv, rowv, half, colv, half],
        out_specs=[pair, colv, rowv],
        out_shape=[jax.ShapeDtypeStruct(qkv.shape, BF16), jax.ShapeDtypeStruct((b, B_HEADS, l, 1), F32),
                   jax.ShapeDtypeStruct((b, B_HEADS, 1, l), F32)],
        scratch_shapes=[pltpu.VMEM((l, 128), F32), pltpu.VMEM((l, 128), F32)],
        compiler_params=_params(("parallel", "parallel")),
    )(qkv, c_col, c_row, o, lse, do)


_FLIPS = ((0, 0, 1), (0, 1, 0), (0, 1, 1), (1, 0, 0), (1, 0, 1), (1, 1, 0), (1, 1, 1))


def _peer_table():
    x, y, c = lax.axis_index("x"), lax.axis_index("y"), lax.axis_index("c")
    me = 4 * x + 2 * y + c
    peers = []
    for fx, fy, fc in _FLIPS:
        px = 1 - x if fx else x
        py = 1 - y if fy else y
        pc = 1 - c if fc else c
        peers.append(((px, py, pc), 4 * px + 2 * py + pc))
    return me, peers


_HBM = pl.BlockSpec(memory_space=pltpu.HBM)
_SEM = pl.BlockSpec(memory_space=pltpu.SEMAPHORE)
_ANY = pl.BlockSpec(memory_space=pl.ANY)
_EFFECT = pltpu.SideEffectType.DATAFLOW_SIDE_EFFECTING


def _split_copy(srcs_are_pieces, src_refs, land_refs, send_sem, recv_sem, a, kk, me, peers, arriving):
    dev, lin = peers[kk]
    npeer = len(_FLIPS)
    src = src_refs[a] if srcs_are_pieces[a] else src_refs[a].at[lin]
    dst = land_refs[a].at[lin] if arriving else land_refs[a].at[me]
    return pltpu.make_async_remote_copy(src_ref=src, dst_ref=dst, send_sem=send_sem.at[a * npeer + kk],
                                        recv_sem=recv_sem.at[a * npeer + kk], device_id=dev, device_id_type=MESH_ID)


def _xchg_start(name, gather, scatter, after=None):
    me_out = 4 * lax.axis_index("x") + 2 * lax.axis_index("y") + lax.axis_index("c")
    srcs = list(gather) + list(scatter)
    is_piece = [True] * len(gather) + [False] * len(scatter)
    lands = []
    for a, piece in zip(srcs, is_piece):
        if piece:
            lands.append(lax.dynamic_update_slice(lax.empty((N_DEV,) + tuple(a.shape), a.dtype), a[None],
                                                  (me_out,) + (0,) * a.ndim))
        else:
            lands.append(lax.empty(tuple(a.shape), a.dtype))
    n = len(srcs)
    nsem = n * len(_FLIPS)
    has_after = after is not None

    def body(*refs):
        src_refs = refs[:n]
        land_refs = refs[n:2 * n]
        outs = refs[2 * n + (1 if has_after else 0):]
        send_sem, recv_sem = outs[0], outs[1]
        token = outs[-1]
        me, peers = _peer_table()
        for kk in range(len(_FLIPS)):
            for a in range(n):
                _split_copy(is_piece, src_refs, land_refs, send_sem, recv_sem, a, kk, me, peers, False).start()
        token[...] = jnp.zeros_like(token)

    out_shape = ([pltpu.SemaphoreType.DMA((nsem,)), pltpu.SemaphoreType.DMA((nsem,))]
                 + [pltpu.HBM(tuple(a.shape), a.dtype) for a in srcs] + [pltpu.HBM(tuple(a.shape), a.dtype) for a in lands]
                 + [jax.ShapeDtypeStruct((8, 128), F32)])
    args = [pltpu.with_memory_space_constraint(a, pltpu.HBM) for a in srcs + lands] + ([after] if has_after else [])
    res = pl.pallas_call(
        body, name=name, out_shape=out_shape,
        in_specs=[_HBM] * (2 * n) + ([_ANY] if has_after else []),
        out_specs=[_SEM, _SEM] + [_HBM] * (2 * n) + [pl.BlockSpec(memory_space=pltpu.VMEM)],
        input_output_aliases={i: 2 + i for i in range(2 * n)},
        compiler_params=pltpu.CompilerParams(has_side_effects=_EFFECT),
    )(*args)
    state = (res[0], res[1], list(res[2:2 + n]), list(res[2 + n:2 + 2 * n]), is_piece)
    return state, res[-1]


def _xchg_wait(name, state, after):
    send_sem, recv_sem, srcs, lands, is_piece = state
    n = len(srcs)

    def body(*refs):
        src_refs = refs[:n]
        land_refs = refs[n:2 * n]
        s_sem, r_sem = refs[2 * n], refs[2 * n + 1]
        me, peers = _peer_table()
        for kk in range(len(_FLIPS)):
            for a in range(n):
                cp = _split_copy(is_piece, src_refs, land_refs, s_sem, r_sem, a, kk, me, peers, True)
                cp.wait_send()
                cp.wait_recv()

    out_shape = [pltpu.HBM(tuple(a.shape), a.dtype) for a in srcs] + [pltpu.HBM(tuple(a.shape), a.dtype) for a in lands]
    res = pl.pallas_call(
        body, name=name, out_shape=out_shape,
        in_specs=[_HBM] * (2 * n) + [_SEM, _SEM, _ANY], out_specs=[_HBM] * (2 * n),
        input_output_aliases={i: i for i in range(2 * n)},
        compiler_params=pltpu.CompilerParams(has_side_effects=_EFFECT),
    )(*srcs, *lands, send_sem, recv_sem, after)
    return list(res[n:]), list(res[:n])


_SIB = (0, 0, 1)
_ICI = ((0, 1, 0), (1, 0, 0), (1, 1, 0))


def _flip(fl):
    x, y, c = lax.axis_index("x"), lax.axis_index("y"), lax.axis_index("c")
    px = 1 - x if fl[0] else x
    py = 1 - y if fl[1] else y
    pc = 1 - c if fl[2] else c
    return (px, py, pc), 4 * px + 2 * py + pc


def _gather2_start(name, pieces, after=None):
    me_out = 4 * lax.axis_index("x") + 2 * lax.axis_index("y") + lax.axis_index("c")
    pieces = list(pieces)
    n = len(pieces)
    lands = [lax.dynamic_update_slice(lax.empty((N_DEV,) + tuple(a.shape), a.dtype), a[None],
                                      (me_out,) + (0,) * a.ndim) for a in pieces]
    first = (_SIB,) + _ICI
    has_after = after is not None

    def body(*refs):
        src_refs, land_refs = refs[:n], refs[n:2 * n]
        outs = refs[2 * n + (1 if has_after else 0):]
        send_sem, recv_sem, token = outs[0], outs[1], outs[-1]
        _, me = _flip((0, 0, 0))
        for kk, fl in enumerate(first):
            dev, _ = _flip(fl)
            for a in range(n):
                pltpu.make_async_remote_copy(src_ref=src_refs[a], dst_ref=land_refs[a].at[me],
                                             send_sem=send_sem.at[a * 4 + kk], recv_sem=recv_sem.at[a * 4 + kk],
                                             device_id=dev, device_id_type=MESH_ID).start()
        token[...] = jnp.zeros_like(token)

    hbm = [pltpu.HBM(tuple(a.shape), a.dtype) for a in pieces + lands]
    res = pl.pallas_call(
        body, name=name,
        out_shape=[pltpu.SemaphoreType.DMA((4 * n,)), pltpu.SemaphoreType.DMA((4 * n,))] + hbm
        + [jax.ShapeDtypeStruct((8, 128), F32)],
        in_specs=[_HBM] * (2 * n) + ([_ANY] if has_after else []),
        out_specs=[_SEM, _SEM] + [_HBM] * (2 * n) + [pl.BlockSpec(memory_space=pltpu.VMEM)],
        input_output_aliases={i: 2 + i for i in range(2 * n)},
        compiler_params=pltpu.CompilerParams(has_side_effects=_EFFECT),
    )(*([pltpu.with_memory_space_constraint(a, pltpu.HBM) for a in pieces + lands] + ([after] if has_after else [])))
    return (res[0], res[1], list(res[2:2 + n]), list(res[2 + n:2 + 2 * n])), res[-1]


def _gather2_forward(name, state, after):
    send_a, recv_a, pieces, lands = state
    n = len(pieces)
    first = (_SIB,) + _ICI

    def body(*refs):
        src_refs, land_refs = refs[:n], refs[n:2 * n]
        s_a, r_a = refs[2 * n], refs[2 * n + 1]
        outs = refs[2 * n + 3:]
        send_b, recv_b, token = outs[0], outs[1], outs[-1]
        for kk, fl in enumerate(first):
            dev, lin = _flip(fl)
            for a in range(n):
                cp = pltpu.make_async_remote_copy(src_ref=src_refs[a], dst_ref=land_refs[a].at[lin],
                                                  send_sem=s_a.at[a * 4 + kk], recv_sem=r_a.at[a * 4 + kk],
                                                  device_id=dev, device_id_type=MESH_ID)
                cp.wait_send()
                cp.wait_recv()
        sib, _ = _flip(_SIB)
        for j, fl in enumerate(_ICI):
            _, lin = _flip(fl)
            for a in range(n):
                pltpu.make_async_remote_copy(src_ref=land_refs[a].at[lin], dst_ref=land_refs[a].at[lin],
                                             send_sem=send_b.at[a * 3 + j], recv_sem=recv_b.at[a * 3 + j],
                                             device_id=sib, device_id_type=MESH_ID).start()
        token[...] = jnp.zeros_like(token)

    hbm = [pltpu.HBM(tuple(a.shape), a.dtype) for a in pieces + lands]
    res = pl.pallas_call(
        body, name=name,
        out_shape=[pltpu.SemaphoreType.DMA((3 * n,)), pltpu.SemaphoreType.DMA((3 * n,))] + hbm
        + [jax.ShapeDtypeStruct((8, 128), F32)],
        in_specs=[_HBM] * (2 * n) + [_SEM, _SEM, _ANY],
        out_specs=[_SEM, _SEM] + [_HBM] * (2 * n) + [pl.BlockSpec(memory_space=pltpu.VMEM)],
        input_output_aliases={i: 2 + i for i in range(2 * n)},
        compiler_params=pltpu.CompilerParams(has_side_effects=_EFFECT),
    )(*pieces, *lands, send_a, recv_a, after)
    return (res[0], res[1], list(res[2 + n:2 + 2 * n])), res[-1]


def _gather2_wait(name, state, after):
    send_b, recv_b, lands = state
    n = len(lands)

    def body(*refs):
        land_refs = refs[:n]
        s_b, r_b = refs[n], refs[n + 1]
        sib, _ = _flip(_SIB)
        for j, fl in enumerate(_ICI):
            _, sent = _flip(fl)
            _, arriving = _flip((fl[0], fl[1], 1))
            for a in range(n):
                cp = pltpu.make_async_remote_copy(src_ref=land_refs[a].at[sent], dst_ref=land_refs[a].at[arriving],
                                                  send_sem=s_b.at[a * 3 + j], recv_sem=r_b.at[a * 3 + j],
                                                  device_id=sib, device_id_type=MESH_ID)
                cp.wait_send()
                cp.wait_recv()

    res = pl.pallas_call(
        body, name=name, out_shape=[pltpu.HBM(tuple(a.shape), a.dtype) for a in lands],
        in_specs=[_HBM] * n + [_SEM, _SEM, _ANY], out_specs=[_HBM] * n,
        input_output_aliases={i: i for i in range(n)},
        compiler_params=pltpu.CompilerParams(has_side_effects=_EFFECT),
    )(*lands, send_b, recv_b, after)
    return list(res)


def _adam_math(w, g, m, v):
    m = ADAM_B1 * m + (1.0 - ADAM_B1) * g
    v = ADAM_B2 * v + (1.0 - ADAM_B2) * (g * g)
    m_hat = m / (1.0 - ADAM_B1 ** ADAM_STEP)
    v_hat = v / (1.0 - ADAM_B2 ** ADAM_STEP)
    delta = -ADAM_LR * (m_hat / (jnp.sqrt(v_hat) + ADAM_EPS) + ADAM_WD * w)
    return delta, m, v


def _adam(name, w, m, v, parts, transposed=False, dep=None, own=None):
    npart, _, cp = parts.shape
    has_dep = dep is not None
    has_own = own is not None
    if transposed:
        c, r = w.shape
        tr = _pick(r, (256, 128))
        blk = pl.BlockSpec((c, tr), lambda i, *_: (0, i))
    else:
        r, c = w.shape
        tr = _pick(r, (256, 176, 128, 64, 16, 8, 1))
        blk = pl.BlockSpec((tr, c), lambda i, *_: (i, 0))

    def body(*refs):
        refs = list(refs)
        me = refs.pop(0)[0] if has_own else None
        w_ref, m_ref, v_ref, p_ref = refs[:4]
        own_ref = refs[4] if has_own else None
        g_ref, d_ref, mo_ref, vo_ref = refs[-4:]
        g = None
        for pp in range(npart):
            part = p_ref[pp] if not has_own else jnp.where(me == pp, own_ref[...], p_ref[pp])
            g = part.astype(F32) if g is None else g + part.astype(F32)
        g = g.T[0:c, :] if transposed else g[:, 0:c]
        delta, mn, vn = _adam_math(w_ref[...], g, m_ref[...], v_ref[...])
        g_ref[...] = g
        d_ref[...] = delta
        mo_ref[...] = mn
        vo_ref[...] = vn

    in_specs = [blk, blk, blk, pl.BlockSpec((npart, tr, cp), lambda i, *_: (0, i, 0))]
    args = [w, m, v, parts]
    if has_own:
        in_specs.append(pl.BlockSpec((None, tr, cp), lambda i, me_ref: (me_ref[0], i, 0)))
        args.append(own[0])
    if has_dep:
        in_specs.append(pl.BlockSpec(memory_space=pl.ANY))
        args.append(dep)
    out = jax.ShapeDtypeStruct(w.shape, F32)
    grid_spec = pltpu.PrefetchScalarGridSpec(num_scalar_prefetch=1 if has_own else 0, grid=(r // tr,),
                                             in_specs=in_specs, out_specs=[blk, blk, blk, blk])
    return pl.pallas_call(
        body, name=name, grid_spec=grid_spec, out_shape=[out, out, out, out], compiler_params=_params(("parallel",)),
    )(*(([own[1]] if has_own else []) + args))


def _small_update(name, packs, wmv):
    nparam = len(wmv)

    def body(p_ref, *refs):
        ins, outs = refs[:3 * nparam], refs[3 * nparam:]
        tot = p_ref[0]
        for pp in range(1, N_DEV):
            tot = tot + p_ref[pp]
        outs[0][0:8, :] = tot[0:8, :]
        outs[0][8:24, :] = tot[8:24, :] + tot[24:40, :]
        grads = [tot[i:i + 1, :] for i in range(4)] + [tot[4:5, 0:B_HEADS], tot[4:5, B_HEADS:2 * B_HEADS]]
        for i, g in enumerate(grads):
            w_ref, m_ref, v_ref = ins[3 * i:3 * i + 3]
            delta, mn, vn = _adam_math(w_ref[...], g, m_ref[...], v_ref[...])
            for o_ref, val in zip(outs[1 + 4 * i:5 + 4 * i], (g, delta, mn, vn)):
                o_ref[...] = val

    flat = [a for trio in wmv for a in trio]
    out_shape = [jax.ShapeDtypeStruct((24, D_MODEL), F32)]
    for w, _, _ in wmv:
        out_shape += [jax.ShapeDtypeStruct(w.shape, F32)] * 4
    res = pl.pallas_call(body, name=name, out_shape=out_shape, compiler_params=_params())(packs, *flat)
    return res[0], [tuple(res[1 + 4 * i:5 + 4 * i]) for i in range(nparam)]


def _local_step(x, tgt, g1, gm, g2, gf, b_forget, sinks, weights, send):
    b, s, _ = x.shape
    l = s + PREFIX
    t = b * l
    (meta,) = weights("meta", x)

    h0, n1 = _embed_norm("embed_rms1_fwd", x, meta, g1)
    (w1i,) = weights("ffn1_in", n1)
    gu1, a1 = _ffn_in_fwd("ffn1_in_fwd", n1, w1i)
    (w1o,) = weights("ffn1_out", weights("ffn1_out:forward", a1))
    h1, um = _mm_res_norm("ffn1_out_fwd", a1, w1o, h0, gm, alpha=0.5)
    wi, wa, wb, wo = weights("mix", weights("mix:forward", um))
    qkv, gates, f2 = _proj_fwd("proj_fwd", um, wi)
    qkv3 = qkv.reshape(b, l, QKV_W)
    f3 = f2.reshape(b, l, 128)
    bf_row = jnp.pad(b_forget, ((0, 0), (0, 128 - B_HEADS)))
    c_col, c_row = _fgate_fwd("fgate_fwd", f3, bf_row)
    head_of_row = jnp.arange(STACK) // BLOCK
    slopes = jnp.exp2(-8.0 * (head_of_row + 1).astype(F32) / A_HEADS).reshape(STACK, 1)
    sink_rows = jnp.repeat(sinks.reshape(A_HEADS), BLOCK).reshape(STACK, 1)
    swa_bias = _swa_bias(slopes)
    oa3, lse_a = _swa_fwd("swa_fwd", qkv3, slopes, sink_rows, swa_bias)
    ob3, lse_b = _fox_fwd("fox_fwd", qkv3, c_col, c_row)
    oa = oa3.reshape(t, A_WIDTH)
    ob = ob3.reshape(t, B_WIDTH)
    mixed, ya, yb = _branch_gate_fwd("branch_gate_fwd", oa, ob, wa, wb, gates, dep=weights("ffn2:forward", ob))
    h2, n2 = _mm_res_norm("mix_out_fwd", mixed, wo, h1, g2)
    w2i, w2o = weights("ffn2", h2)
    gu2, a2 = _ffn_in_fwd("ffn2_in_fwd", n2, w2i)

    dh3, loss_blk, dgf = _ffn_out_loss("ffn2_out_loss", a2, w2o, h2, gf, tgt, 0.5)

    def ffn_bwd(tag, dh, h_in, g_norm, n_in, gu, a, w_in_blk, w_out, one_send, examples=None):
        dw_out = _mm_tn(tag + "_out_bwd_w", a, dh, alpha=0.5)
        dgu = _ffn_out_bwd_x(tag + "_out_bwd_x", dh, w_out, gu, dep=None if one_send else send(tag + "_out", (dw_out,)))
        dw_in = _ffn_in_bwd_w(tag + "_in_bwd_w", n_in, dgu)
        token = send(tag, (dw_in, dw_out)) if one_send else send(tag + "_in", (dw_in,))
        return _ffn_in_bwd_x(tag + "_in_bwd_x", dgu, w_in_blk, h_in, g_norm, dh, dep=token, examples=examples)

    dh2, dg2 = ffn_bwd("ffn2", dh3, h2, g2, n2, gu2, a2, w2i, w2o, True)

    dwo = _mm_tn("mix_out_bwd_w", mixed, dh2)
    dya, dyb, dgates = _mix_out_gate_bwd("mix_out_gate_bwd", dh2, wo, gates, ya, yb)
    doa, dob = _branch_bwd_x("branch_bwd_x", dya, dyb, wa, wb)
    dwa, dwb = _branch_bwd_w("branch_bwd_w", oa, ob, dya, dyb)
    dqkv3, dcq, dck = _fox_bwd("fox_bwd", qkv3, c_col, c_row, ob3, lse_b, dob.reshape(b, l, B_WIDTH))
    dqkv3, dsink = _swa_bwd("swa_bwd", qkv3, oa3, lse_a, doa.reshape(b, l, A_WIDTH), slopes, sink_rows, swa_bias, dqkv3)
    dqkv = dqkv3.reshape(t, QKV_W)
    df3, dbf = _fgate_bwd("fgate_bwd", f3, bf_row, dcq, dck)
    df = df3.reshape(t, 128)
    dwi_qkv = _mm_tn("proj_qkv_bwd_w", um, dqkv, tm_c=(512,), tn_c=(768,))
    dwi_g = _mm_tn("proj_gates_bwd_w", um, dgates, tm_c=(512,), tn_c=(512,))
    dwi_f = _mm_tn("proj_f_bwd_w", um, df, tm_c=(512,), tn_c=(128,))
    token = send("mix", (dwi_qkv, dwi_g, dwi_f, dwa, dwb, dwo))
    dh1, dgm = _proj_bwd_x("proj_bwd_x", dqkv, dgates, df, wi, h1, gm, dh2, dep=token)

    grad_x, dmeta3, dg1 = ffn_bwd("ffn1", dh1, h0, g1, n1, gu1, a1, w1i, w1o, False, examples=b)
    dmeta = dmeta3.reshape(b * N_META, D_MODEL)

    misc = jnp.concatenate([dbf[:, 0:B_HEADS], dsink[:, 0].reshape(1, A_HEADS), loss_blk[0:1, 0:1]], axis=1)
    misc = jnp.pad(misc, ((0, 0), (0, D_MODEL - misc.shape[1])))
    row = lax.broadcasted_iota(jnp.int32, (8, D_MODEL), 0)
    vec = jnp.zeros((8, D_MODEL), F32)
    for i, piece in enumerate((dg1, dgm, dg2, dgf, misc)):
        vec = jnp.where(row == i, piece, vec)
    small = jnp.concatenate([vec, dmeta], axis=0)
    return grad_x, small


def _pad_to(a, rows, cols):
    return jnp.pad(a, ((0, rows - a.shape[0]), (0, cols - a.shape[1])))


def _ffn_out_from_gathered(name, g):
    def body(g_ref, o_ref):
        o_ref[0:FFO_SHARD, :] = g_ref[0]
        o_ref[FFO_SHARD:FF_SHARD, :] = g_ref[1]
        o_ref[FF_SHARD:FF_SHARD_P, :] = jnp.zeros((FF_SHARD_P - FF_SHARD, D_MODEL), BF16)

    return pl.pallas_call(
        body, name=name, grid=(4,),
        in_specs=[pl.BlockSpec((2, FFO_SHARD, D_MODEL), lambda j: (j, 0, 0))],
        out_specs=pl.BlockSpec((FF_SHARD_P, D_MODEL), lambda j: (j, 0)),
        out_shape=jax.ShapeDtypeStruct((D_FF_P, D_MODEL), BF16), compiler_params=_params(("parallel",)),
    )(g)


def _ffn_out_to_scatter(dw):
    return dw.reshape(4, FF_SHARD_P, D_MODEL)[:, 0:FF_SHARD, :].reshape(N_DEV, FFO_SHARD, D_MODEL)


def _proj_segments():
    segs = [(HEAD_DIM * h, HEAD_DIM, 0, B_SEG + HEAD_DIM * A_HEAD_ORDER.index(h)) for h in range(A_HEADS)]
    segs += [(512, 128, 0, B_SEG + A_WIDTH), (640, 128, 0, B_SEG + A_WIDTH + 128)]
    for first, off in ((768, 0), (1280, 128), (1792, 256)):
        segs += [(first + 128 * hp, 128, 0, PAIR_W * hp + off) for hp in range(4)]
    segs += [(2304, B_HEADS, 2, 0), (2312, 2 * D_MODEL, 1, 0)]
    return segs


_RELAYOUT_ROWS = 256


def _proj_from_gathered(name, g):
    rows = _RELAYOUT_ROWS

    def body(g_ref, o_ref):
        def cols(first, width):
            out = []
            for p in range(N_DEV):
                lo, hi = max(first, WIN_SHARD * p), min(first + width, WIN_SHARD * (p + 1))
                if lo < hi:
                    out.append(g_ref[p, :, lo - WIN_SHARD * p:hi - WIN_SHARD * p])
            return out

        parts = []
        for arr in (0, 1, 2):
            for first, width, _, _ in sorted((s for s in _proj_segments() if s[2] == arr), key=lambda s: s[3]):
                parts += cols(first, width)
            if arr == 0:
                parts.append(jnp.zeros((rows, P_GATES - QKV_W), BF16))
        parts.append(jnp.zeros((rows, 128 - B_HEADS), BF16))
        o_ref[...] = jnp.concatenate(parts, axis=1)

    return pl.pallas_call(
        body, name=name, grid=(D_MODEL // rows,),
        in_specs=[pl.BlockSpec((N_DEV, rows, WIN_SHARD_P), lambda i: (0, i, 0))],
        out_specs=pl.BlockSpec((rows, PROJ_P), lambda i: (i, 0)),
        out_shape=jax.ShapeDtypeStruct((D_MODEL, PROJ_P), BF16), compiler_params=_params(("parallel",)),
    )(g)


def _proj_to_scatter(name, dqkv_w, dg_w, df_w):
    rows = _RELAYOUT_ROWS
    segs = sorted(_proj_segments())

    def body(q_ref, g_ref, f_ref, o_ref):
        arrays = (q_ref, g_ref, f_ref)
        for p in range(N_DEV):
            parts = []
            for first, width, arr, at in segs:
                lo, hi = max(first, WIN_SHARD * p), min(first + width, WIN_SHARD * (p + 1))
                if lo < hi:
                    parts.append(arrays[arr][:, at + lo - first:at + hi - first])
            parts.append(jnp.zeros((rows, WIN_SHARD_P - WIN_SHARD), BF16))
            o_ref[p] = jnp.concatenate(parts, axis=1)

    return pl.pallas_call(
        body, name=name, grid=(D_MODEL // rows,),
        in_specs=[pl.BlockSpec((rows, QKV_W), lambda i: (i, 0)), pl.BlockSpec((rows, 2 * D_MODEL), lambda i: (i, 0)),
                  pl.BlockSpec((rows, 128), lambda i: (i, 0))],
        out_specs=pl.BlockSpec((N_DEV, rows, WIN_SHARD_P), lambda i: (0, i, 0)),
        out_shape=jax.ShapeDtypeStruct((N_DEV, D_MODEL, WIN_SHARD_P), BF16), compiler_params=_params(("parallel",)),
    )(dqkv_w, dg_w, df_w)


def _a_rows_from_natural(w):
    return jnp.concatenate([w[HEAD_DIM * h:HEAD_DIM * (h + 1)] for h in A_HEAD_ORDER], axis=0)


def _a_rows_to_natural(w):
    return jnp.concatenate([w[HEAD_DIM * A_HEAD_ORDER.index(h):HEAD_DIM * (A_HEAD_ORDER.index(h) + 1)]
                            for h in range(A_HEADS)], axis=0)


def kernel(x, meta_tokens, ffn1_norm, ffn1_w_in, ffn1_w_out, mix_norm, w_in, b_forget, attn_sinks, w_branch_a, w_branch_b, w_out, ffn2_norm, ffn2_w_in, ffn2_w_out, final_norm, loss_target, m_meta_tokens, m_ffn1_norm, m_ffn1_w_in, m_ffn1_w_out, m_mix_norm, m_w_in, m_b_forget, m_attn_sinks, m_w_branch_a, m_w_branch_b, m_w_out, m_ffn2_norm, m_ffn2_w_in, m_ffn2_w_out, m_final_norm, v_meta_tokens, v_ffn1_norm, v_ffn1_w_in, v_ffn1_w_out, v_mix_norm, v_w_in, v_b_forget, v_attn_sinks, v_w_branch_a, v_w_branch_b, v_w_out, v_ffn2_norm, v_ffn2_w_in, v_ffn2_w_out, v_final_norm):
    me = 4 * lax.axis_index("x") + 2 * lax.axis_index("y") + lax.axis_index("c")

    def shard(w, tok, rows=None, cols=None):
        piece = (w[0] if tok is None else w[0] + tok[0, 0]).astype(BF16)
        return piece if rows is None else _pad_to(piece, rows, cols)

    first_level, second_level = {}, {}
    first_level["meta"], tok = _gather2_start("gather_meta_start", (meta_tokens,))
    first_level["ffn1_in"], tok = _gather2_start(
        "gather_ffn1_in_start", (shard(ffn1_w_in, None, D_MODEL, FF_SHARD_P),), after=tok)
    first_level["ffn1_out"], tok = _gather2_start("gather_ffn1_out_start", (shard(ffn1_w_out, tok),), after=tok)
    first_level["mix"], tok = _gather2_start(
        "gather_mix_start", (shard(w_in, tok, D_MODEL, WIN_SHARD_P), shard(w_branch_a, tok), shard(w_branch_b, tok),
                             shard(w_out, tok)), after=tok)
    first_level["ffn2"], tok = _gather2_start(
        "gather_ffn2_start", (shard(ffn2_w_in, tok, D_MODEL, FF_SHARD_P), shard(ffn2_w_out, tok)), after=tok)
    started = {"tok": tok}

    def weights(group, after):
        if group.endswith(":forward"):
            group = group[:-len(":forward")]
            second_level[group], token = _gather2_forward("gather_" + group + "_forward", first_level[group], after)
            return token
        if group == "meta":
            after = weights("meta:forward", started["tok"])
        if group == "ffn1_in":
            after = weights("ffn1_in:forward", after)
        got = _gather2_wait("gather_" + group + "_wait", second_level[group], after)
        if group == "mix":
            gwi, gwa, gwb, gwo = got
            return (_proj_from_gathered("proj_w_relayout", gwi), _a_rows_from_natural(gwa.transpose(1, 0, 2).reshape(A_WIDTH, D_MODEL)),
                    gwb.transpose(1, 0, 2).reshape(B_WIDTH, D_MODEL), gwo.reshape(D_MODEL, D_MODEL))
        if group == "ffn1_out":
            return (_ffn_out_from_gathered("ffn1_w_out_relayout", got[0]),)
        if group == "meta":
            return (got[0].transpose(1, 0, 2).reshape(N_META, D_MODEL),)
        if group == "ffn1_in":
            return (got[0].reshape(2, 4, D_MODEL, FF_SHARD_P),)
        return got[0].reshape(2, 4, D_MODEL, FF_SHARD_P), _ffn_out_from_gathered("ffn2_w_out_relayout", got[1])

    scatter_state = {}

    def send(group, grads):
        if group == "mix":
            dwi_qkv, dwi_g, dwi_f, dwa, dwb, dwo = grads
            dwa = _a_rows_to_natural(dwa)
            blocks = (_proj_to_scatter("proj_dw_relayout", dwi_qkv, dwi_g, dwi_f), dwa.reshape(A_WIDTH, N_DEV, 128).transpose(1, 0, 2),
                      dwb.reshape(B_WIDTH, N_DEV, 128).transpose(1, 0, 2), dwo.reshape(N_DEV, 128, D_MODEL))
        elif group.endswith("_in"):
            blocks = (grads[0].reshape(N_DEV, D_MODEL, FF_SHARD_P),)
        elif group.endswith("_out"):
            blocks = (_ffn_out_to_scatter(grads[0]),)
        else:
            blocks = (grads[0].reshape(N_DEV, D_MODEL, FF_SHARD_P), _ffn_out_to_scatter(grads[1]))
        scatter_state[group], token = _xchg_start("scatter_" + group + "_start", (), blocks)
        return token

    gf = final_norm.reshape(1, D_MODEL)
    grad_x, small = _local_step(x, loss_target, ffn1_norm, mix_norm, ffn2_norm, gf, b_forget, attn_sinks, weights, send)

    small_state, after = _xchg_start("gather_small_start", (small,), ())
    out = {}
    updates = (
        ("ffn2", (("ffn2_w_in", ffn2_w_in, m_ffn2_w_in, v_ffn2_w_in), ("ffn2_w_out", ffn2_w_out, m_ffn2_w_out, v_ffn2_w_out))),
        ("ffn1_out", (("ffn1_w_out", ffn1_w_out, m_ffn1_w_out, v_ffn1_w_out),)),
        ("mix", (("w_in", w_in, m_w_in, v_w_in), ("w_branch_a", w_branch_a, m_w_branch_a, v_w_branch_a),
                 ("w_branch_b", w_branch_b, m_w_branch_b, v_w_branch_b), ("w_out", w_out, m_w_out, v_w_out))),
    )
    last_update = ("ffn1_in", (("ffn1_w_in", ffn1_w_in, m_ffn1_w_in, v_ffn1_w_in),))

    def update(group, members, after):
        parts_list, blocks_list = _xchg_wait("scatter_" + group + "_wait", scatter_state[group], after)
        prev = None
        for (nm, w, m, v), parts, blocks in zip(members, parts_list, blocks_list):
            if nm.endswith("w_in"):
                res4 = _adam("adam_" + nm, w[0].T, m[0].T, v[0].T, parts, transposed=True, dep=prev, own=(blocks, me_arr))
                out[nm] = tuple(r.T[None] for r in res4)
            else:
                res4 = _adam("adam_" + nm, w[0], m[0], v[0], parts, dep=prev, own=(blocks, me_arr))
                out[nm] = tuple(r[None] for r in res4)
            prev = res4[0]
        return prev

    me_arr = me.reshape(1).astype(jnp.int32)
    for group, members in updates + (last_update,):
        after = update(group, members, after)
    (packs,), _ = _xchg_wait("gather_small_wait", small_state, after)

    row = lambda a: a.reshape(1, D_MODEL)
    tot, small_res = _small_update("small_update", packs, (
        (ffn1_norm, m_ffn1_norm, v_ffn1_norm), (mix_norm, m_mix_norm, v_mix_norm), (ffn2_norm, m_ffn2_norm, v_ffn2_norm),
        (row(final_norm), row(m_final_norm), row(v_final_norm)), (b_forget, m_b_forget, v_b_forget),
        (attn_sinks, m_attn_sinks, v_attn_sinks)))
    for nm, res4 in zip(("ffn1_norm", "mix_norm", "ffn2_norm", "final_norm", "b_forget", "attn_sinks"), small_res):
        out[nm] = tuple(r.reshape(D_MODEL) for r in res4) if nm == "final_norm" else res4
    loss = tot[4, 2 * B_HEADS]
    g_meta = lax.dynamic_slice(tot[8:24, :], (0, me * 128), (N_META, 128))
    out["meta_tokens"] = tuple(_adam("adam_meta_tokens", meta_tokens, m_meta_tokens, v_meta_tokens, g_meta[None]))

    names = ("meta_tokens", "ffn1_norm", "ffn1_w_in", "ffn1_w_out", "mix_norm", "w_in", "b_forget", "attn_sinks",
             "w_branch_a", "w_branch_b", "w_out", "ffn2_norm", "ffn2_w_in", "ffn2_w_out", "final_norm")
    return (loss, grad_x) + tuple(out[nm][kind] for kind in range(4) for nm in names)
```

```python
import jax
import jax.numpy as jnp
from jax import lax
from jax.experimental import pallas as pl
from jax.experimental.pallas import tpu as pltpu

F32 = jnp.float32
BF16 = jnp.bfloat16

D_MODEL = 1024
N_META = 16
BLOCK = 128
PREFIX = 128
N_PAD = PREFIX - N_META
HEAD_DIM = 64
A_HEADS = 8
B_HEADS = 8
A_WIDTH = 512
A_KV_WIDTH = 128
B_WIDTH = 512
D_FF = 2816
N_DEV = 8
FF_SHARD = 2 * D_FF // N_DEV
FF_SHARD_P = 768
FFO_SHARD = D_FF // N_DEV
D_FF_P = 4 * FF_SHARD_P
W_IN_COLS = 4360
WIN_SHARD = W_IN_COLS // N_DEV
WIN_SHARD_P = 640
PAIR_W = 3 * 128
B_SEG = 4 * PAIR_W
A_SEG = A_WIDTH + 2 * A_KV_WIDTH
QKV_W = B_SEG + A_SEG
P_GATES = 2 * (2 * D_MODEL)
P_F = P_GATES + 2 * D_MODEL
PROJ_P = P_F + 128
A_HEAD_ORDER = (0, 4, 1, 5, 2, 6, 3, 7)
EPS = 1e-6
NEG = -1e30
SCALE = HEAD_DIM ** -0.5
ADAM_LR = 0.001
ADAM_B1 = 0.9
ADAM_B2 = 0.999
ADAM_EPS = 1e-08
ADAM_WD = 0.01
ADAM_STEP = 10
VMEM_LIMIT = 56 * 1024 * 1024
MESH_ID = pl.DeviceIdType.MESH
SMALL_ROWS = 40

_NN = (((1,), (0,)), ((), ()))
_NT = (((1,), (1,)), ((), ()))
_TN = (((0,), (0,)), ((), ()))


def _params(sem=None):
    return pltpu.CompilerParams(dimension_semantics=sem, vmem_limit_bytes=VMEM_LIMIT)


def _pick(n, cands):
    for c in cands:
        if n % c == 0:
            return c
    raise ValueError(f"no tile for {n}")


def _bf(v):
    return v if v.dtype == BF16 else v.astype(BF16)


def _mm(name, a, b, dims, grid, a_spec, b_spec, o_spec, out_shape, out_dtype, alpha=1.0):
    def body(a_ref, b_ref, o_ref):
        acc = lax.dot_general(_bf(a_ref[...]), _bf(b_ref[...]), dims, preferred_element_type=F32)
        if alpha != 1.0:
            acc = acc * alpha
        o_ref[...] = acc.astype(o_ref.dtype)

    return pl.pallas_call(
        body, name=name, grid=grid, in_specs=[a_spec, b_spec], out_specs=o_spec,
        out_shape=jax.ShapeDtypeStruct(out_shape, out_dtype),
        compiler_params=_params(("parallel",) * len(grid)),
    )(a, b)


def _mm_res_norm(name, a, w, res, g_next, alpha=1.0):
    t, k = a.shape
    tm = _pick(t, (544, 384, 256, 128))

    def body(a_ref, w_ref, r_ref, g_ref, h_ref, n_ref):
        acc = lax.dot_general(_bf(a_ref[...]), w_ref[...], _NN, preferred_element_type=F32)
        if alpha != 1.0:
            acc = acc * alpha
        hv = acc + r_ref[...]
        h_ref[...] = hv
        r = lax.rsqrt(jnp.mean(hv * hv, axis=-1, keepdims=True) + EPS)
        n_ref[...] = ((hv * r) * g_ref[...]).astype(BF16)

    row = pl.BlockSpec((tm, D_MODEL), lambda i: (i, 0))
    return pl.pallas_call(
        body, name=name, grid=(t // tm,),
        in_specs=[pl.BlockSpec((tm, k), lambda i: (i, 0)), pl.BlockSpec((k, D_MODEL), lambda i: (0, 0)), row,
                  pl.BlockSpec((1, D_MODEL), lambda i: (0, 0))],
        out_specs=[row, row],
        out_shape=[jax.ShapeDtypeStruct((t, D_MODEL), F32), jax.ShapeDtypeStruct((t, D_MODEL), BF16)],
        compiler_params=_params(("parallel",)),
    )(a, w, res, g_next)


def _mm_tn(name, a, b, out_dtype=BF16, alpha=1.0, tm_c=(768, 512, 256, 128), tn_c=(512, 640, 256, 128)):
    t, m = a.shape
    n = b.shape[1]
    tm = _pick(m, tm_c)
    tn = _pick(n, tn_c)
    bytes_a, bytes_b = a.size * a.dtype.itemsize, b.size * b.dtype.itemsize
    if bytes_a + bytes_b * (m // tm) <= bytes_b + bytes_a * (n // tn):
        return _mm(name, a, b, _TN, (m // tm, n // tn),
                   pl.BlockSpec((t, tm), lambda i, j: (0, i)), pl.BlockSpec((t, tn), lambda i, j: (0, j)),
                   pl.BlockSpec((tm, tn), lambda i, j: (i, j)), (m, n), out_dtype, alpha=alpha)
    return _mm(name, a, b, _TN, (n // tn, m // tm),
               pl.BlockSpec((t, tm), lambda j, i: (0, i)), pl.BlockSpec((t, tn), lambda j, i: (0, j)),
               pl.BlockSpec((tm, tn), lambda j, i: (i, j)), (m, n), out_dtype, alpha=alpha)


def _ffn_in_fwd(name, n, wblk, dep=None):
    t = n.shape[0]
    tm = _pick(t, (1088, 768, 512, 256, 128))
    has_dep = dep is not None

    def body(n_ref, w_ref, *rest):
        gu_ref, a_ref = rest[-2], rest[-1]
        nv = n_ref[...]
        g = lax.dot_general(nv, w_ref[0], _NN, preferred_element_type=F32)
        u = lax.dot_general(nv, w_ref[1], _NN, preferred_element_type=F32)
        sg = jax.nn.sigmoid(g)
        silu = g * sg
        a_ref[...] = (silu * u).astype(BF16)
        gu_ref[0] = ((0.5 * u) * (sg + silu * (1.0 - sg))).astype(BF16)
        gu_ref[1] = (0.5 * silu).astype(BF16)

    return pl.pallas_call(
        body, name=name, grid=(t // tm, 4),
        in_specs=[pl.BlockSpec((tm, D_MODEL), lambda i, j: (i, 0)),
                  pl.BlockSpec((2, None, D_MODEL, FF_SHARD_P), lambda i, j: (0, j, 0, 0))]
        + ([pl.BlockSpec(memory_space=pl.ANY)] if has_dep else []),
        out_specs=[pl.BlockSpec((2, tm, FF_SHARD_P), lambda i, j: (0, i, j)),
                   pl.BlockSpec((tm, FF_SHARD_P), lambda i, j: (i, j))],
        out_shape=[jax.ShapeDtypeStruct((2, t, D_FF_P), BF16), jax.ShapeDtypeStruct((t, D_FF_P), BF16)],
        compiler_params=_params(("parallel", "parallel")),
    )(*((n, wblk) + ((dep,) if has_dep else ())))


def _ffn_out_bwd_x(name, dh, w_out, gu, dep=None):
    t = dh.shape[0]
    tm = _pick(t, (1088, 768, 512, 256, 128))
    has_dep = dep is not None

    def body(dh_ref, w_ref, gu_ref, *rest):
        o_ref = rest[-1]
        da = lax.dot_general(_bf(dh_ref[...]), w_ref[...], _NT, preferred_element_type=F32)
        o_ref[0] = (da * gu_ref[0].astype(F32)).astype(BF16)
        o_ref[1] = (da * gu_ref[1].astype(F32)).astype(BF16)

    gu_spec = pl.BlockSpec((2, tm, FF_SHARD_P), lambda i, j: (0, i, j))
    return pl.pallas_call(
        body, name=name, grid=(t // tm, 4),
        in_specs=[pl.BlockSpec((tm, D_MODEL), lambda i, j: (i, 0)), pl.BlockSpec((FF_SHARD_P, D_MODEL), lambda i, j: (j, 0)),
                  gu_spec] + ([pl.BlockSpec(memory_space=pl.ANY)] if has_dep else []),
        out_specs=gu_spec, out_shape=jax.ShapeDtypeStruct((2, t, D_FF_P), BF16),
        compiler_params=_params(("parallel", "parallel")),
    )(*((dh, w_out, gu) + ((dep,) if has_dep else ())))


def _rms_bwd_rows(dn, h, g, dres):
    r = lax.rsqrt(jnp.mean(h * h, axis=-1, keepdims=True) + EPS)
    tv = dn * g
    dot = jnp.mean(tv * h, axis=-1, keepdims=True)
    return dres + (r * tv - h * (r * r * r * dot)), jnp.sum(dn * (h * r), axis=0, keepdims=True)


def _accumulate_rows(ref, part, first):
    @pl.when(first)
    def _():
        ref[...] = part

    @pl.when(jnp.logical_not(first))
    def _():
        ref[...] += part


def _ffn_in_bwd_x(name, dgu, wblk, h_in, g_norm, dres, dep=None, examples=None):
    t = dgu.shape[1]
    tm = _pick(t, (544, 384, 256, 128))
    has_dep = dep is not None
    split = examples is not None
    if split:
        l = t // examples
        per = l // tm
        assert per * tm == l and tm > PREFIX

    def body(d_ref, w_ref, h_ref, g_ref, r_ref, *rest):
        acc = None
        for s in range(2):
            for j in range(4):
                part = lax.dot_general(d_ref[s, :, FF_SHARD_P * j:FF_SHARD_P * (j + 1)], w_ref[s, j], _NT,
                                       preferred_element_type=F32)
                acc = part if acc is None else acc + part
        dh, dg = _rms_bwd_rows(acc, h_ref[...], g_ref[...], r_ref[...])
        i = pl.program_id(0)
        if not split:
            dh_ref, dg_ref = rest[-2], rest[-1]
            dh_ref[...] = dh
        else:
            gx_ref, meta_ref, dg_ref, buf, sem = rest[-5:]

            def out_copy(step):
                bi, r = step // per, step % per
                head = pltpu.make_async_copy(buf.at[pl.ds(PREFIX, tm - PREFIX)], gx_ref.at[bi, pl.ds(0, tm - PREFIX)], sem)
                if per == 1:
                    return r == 0, head, None
                return r == 0, head, pltpu.make_async_copy(
                    buf, gx_ref.at[bi, pl.ds(pl.multiple_of(jnp.maximum(r, 1) * tm - PREFIX, 8), tm)], sem)

            def run(step, method):
                is_head, head, later = out_copy(step)

                @pl.when(is_head)
                def _():
                    getattr(head, method)()

                if later is not None:
                    @pl.when(jnp.logical_not(is_head))
                    def _():
                        getattr(later, method)()

            @pl.when(i > 0)
            def _():
                run(i - 1, "wait")

            buf[...] = dh

            @pl.when(i % per == 0)
            def _():
                meta_ref[...] = dh[N_PAD:PREFIX]

            run(i, "start")

            @pl.when(i == pl.num_programs(0) - 1)
            def _():
                run(i, "wait")

        _accumulate_rows(dg_ref, dg, i == 0)

    row = pl.BlockSpec((tm, D_MODEL), lambda i: (i, 0))
    vec = pl.BlockSpec((1, D_MODEL), lambda i: (0, 0))
    if split:
        out_specs = [pl.BlockSpec(memory_space=pl.ANY), pl.BlockSpec((None, N_META, D_MODEL), lambda i: (i // per, 0, 0)), vec]
        out_shape = [jax.ShapeDtypeStruct((examples, l - PREFIX, D_MODEL), F32),
                     jax.ShapeDtypeStruct((examples, N_META, D_MODEL), F32), jax.ShapeDtypeStruct((1, D_MODEL), F32)]
        scratch = [pltpu.VMEM((tm, D_MODEL), F32), pltpu.SemaphoreType.DMA(())]
    else:
        out_specs = [row, vec]
        out_shape = [jax.ShapeDtypeStruct((t, D_MODEL), F32), jax.ShapeDtypeStruct((1, D_MODEL), F32)]
        scratch = []
    return pl.pallas_call(
        body, name=name, grid=(t // tm,),
        in_specs=[pl.BlockSpec((2, tm, D_FF_P), lambda i: (0, i, 0)),
                  pl.BlockSpec((2, 4, D_MODEL, FF_SHARD_P), lambda i: (0, 0, 0, 0), pipeline_mode=pl.Buffered(1)),
                  row, vec, row]
        + ([pl.BlockSpec(memory_space=pl.ANY)] if has_dep else []),
        out_specs=out_specs, out_shape=out_shape, scratch_shapes=scratch,
        compiler_params=_params(("arbitrary",)),
    )(*((dgu, wblk, h_in, g_norm, dres) + ((dep,) if has_dep else ())))


def _proj_fwd(name, um, wi):
    t = um.shape[0]
    tm = _pick(t, (544, 384, 256, 128))

    def body(u_ref, w_ref, q_ref, g_ref, f_ref):
        uv = u_ref[...]
        q_ref[...] = lax.dot_general(uv, w_ref[:, 0:QKV_W], _NN, preferred_element_type=F32).astype(BF16)
        g_ref[...] = lax.dot_general(uv, w_ref[:, P_GATES:P_F], _NN, preferred_element_type=F32)
        f_ref[...] = lax.dot_general(uv, w_ref[:, P_F:PROJ_P], _NN, preferred_element_type=F32)

    return pl.pallas_call(
        body, name=name, grid=(t // tm,),
        in_specs=[pl.BlockSpec((tm, D_MODEL), lambda i: (i, 0)),
                  pl.BlockSpec((D_MODEL, PROJ_P), lambda i: (0, 0), pipeline_mode=pl.Buffered(1))],
        out_specs=[pl.BlockSpec((tm, QKV_W), lambda i: (i, 0)), pl.BlockSpec((tm, 2 * D_MODEL), lambda i: (i, 0)),
                   pl.BlockSpec((tm, 128), lambda i: (i, 0))],
        out_shape=[jax.ShapeDtypeStruct((t, QKV_W), BF16), jax.ShapeDtypeStruct((t, 2 * D_MODEL), F32),
                   jax.ShapeDtypeStruct((t, 128), F32)],
        compiler_params=_params(("parallel",)),
    )(um, wi)


def _proj_bwd_x(name, dqkv, dgates, df, wi, h_in, g_norm, dres, dep=None):
    t = dqkv.shape[0]
    tm = _pick(t, (544, 384, 256, 128))
    has_dep = dep is not None

    def body(q_ref, gt_ref, f_ref, w_ref, h_ref, g_ref, r_ref, *rest):
        dh_ref, dg_ref = rest[-2], rest[-1]
        acc = lax.dot_general(q_ref[...], w_ref[:, 0:QKV_W], _NT, preferred_element_type=F32)
        acc = acc + lax.dot_general(gt_ref[...], w_ref[:, P_GATES:P_F], _NT, preferred_element_type=F32)
        acc = acc + lax.dot_general(f_ref[...], w_ref[:, P_F:PROJ_P], _NT, preferred_element_type=F32)
        dh, dg = _rms_bwd_rows(acc, h_ref[...], g_ref[...], r_ref[...])
        dh_ref[...] = dh
        _accumulate_rows(dg_ref, dg, pl.program_id(0) == 0)

    row = pl.BlockSpec((tm, D_MODEL), lambda i: (i, 0))
    vec = pl.BlockSpec((1, D_MODEL), lambda i: (0, 0))
    return pl.pallas_call(
        body, name=name, grid=(t // tm,),
        in_specs=[pl.BlockSpec((tm, QKV_W), lambda i: (i, 0)), pl.BlockSpec((tm, 2 * D_MODEL), lambda i: (i, 0)),
                  pl.BlockSpec((tm, 128), lambda i: (i, 0)),
                  pl.BlockSpec((D_MODEL, PROJ_P), lambda i: (0, 0), pipeline_mode=pl.Buffered(1)), row, vec, row]
        + ([pl.BlockSpec(memory_space=pl.ANY)] if has_dep else []),
        out_specs=[row, vec],
        out_shape=[jax.ShapeDtypeStruct((t, D_MODEL), F32), jax.ShapeDtypeStruct((1, D_MODEL), F32)],
        compiler_params=_params(("arbitrary",)),
    )(*((dqkv, dgates, df, wi, h_in, g_norm, dres) + ((dep,) if has_dep else ())))


def _ffn_in_bwd_w(name, n, dgu):
    t = n.shape[0]
    return _mm(name, n, dgu, _TN, (2, 4),
               pl.BlockSpec((t, D_MODEL), lambda s, j: (0, 0)),
               pl.BlockSpec((None, t, FF_SHARD_P), lambda s, j: (s, 0, j)),
               pl.BlockSpec((None, None, D_MODEL, FF_SHARD_P), lambda s, j: (s, j, 0, 0)),
               (2, 4, D_MODEL, FF_SHARD_P), BF16)


def _embed_norm(name, x, meta, g):
    b, s, _ = x.shape
    half = (s + PREFIX) // 2
    first = half - PREFIX
    assert first > 0 and half % 16 == 0

    def body(x_ref, m_ref, g_ref, h_ref, n_ref, buf, sem):
        bi, k = pl.program_id(0), pl.program_id(1)

        def tokens(example, second, method):
            if second:
                cp = pltpu.make_async_copy(x_ref.at[example, pl.ds(first, half)], buf.at[1], sem.at[1])
            else:
                cp = pltpu.make_async_copy(x_ref.at[example, pl.ds(0, first)], buf.at[0, pl.ds(PREFIX, first)], sem.at[0])
            getattr(cp, method)()

        @pl.when((bi == 0) & (k == 0))
        def _():
            tokens(0, False, "start")

        @pl.when(k == 0)
        def _():
            tokens(bi, True, "start")
            tokens(bi, False, "wait")
            buf[0, 0:N_PAD, :] = jnp.zeros((N_PAD, D_MODEL), F32)
            buf[0, N_PAD:PREFIX, :] = m_ref[...]

        @pl.when(k == 1)
        def _():
            @pl.when(bi + 1 < b)
            def _():
                tokens(bi + 1, False, "start")

            tokens(bi, True, "wait")

        hv = buf[k]
        h_ref[...] = hv
        r = lax.rsqrt(jnp.mean(hv * hv, axis=-1, keepdims=True) + EPS)
        n_ref[...] = ((hv * r) * g_ref[...]).astype(BF16)

    rows = pl.BlockSpec((half, D_MODEL), lambda bi, k: (2 * bi + k, 0))
    return pl.pallas_call(
        body, name=name, grid=(b, 2),
        in_specs=[pl.BlockSpec(memory_space=pl.ANY), pl.BlockSpec((N_META, D_MODEL), lambda bi, k: (0, 0)),
                  pl.BlockSpec((1, D_MODEL), lambda bi, k: (0, 0))],
        out_specs=[rows, rows],
        out_shape=[jax.ShapeDtypeStruct((2 * b * half, D_MODEL), F32), jax.ShapeDtypeStruct((2 * b * half, D_MODEL), BF16)],
        scratch_shapes=[pltpu.VMEM((2, half, D_MODEL), F32), pltpu.SemaphoreType.DMA((2,))],
        compiler_params=_params(("arbitrary", "arbitrary")),
    )(x, meta, g)


def _branch_gate_fwd(name, oa, ob, wa, wb, gates, dep=None):
    t = gates.shape[0]
    tm = _pick(t, (544, 384, 256, 128))
    has_dep = dep is not None

    def body(oa_ref, ob_ref, wa_ref, wb_ref, g_ref, *rest):
        o_ref, ya_ref, yb_ref = rest[-3:]
        ya = lax.dot_general(_bf(oa_ref[...]), wa_ref[...], _NN, preferred_element_type=F32)
        yb = lax.dot_general(_bf(ob_ref[...]), wb_ref[...], _NN, preferred_element_type=F32)
        sa = jax.nn.sigmoid(g_ref[:, 0:D_MODEL])
        sb = jax.nn.sigmoid(g_ref[:, D_MODEL:2 * D_MODEL])
        o_ref[...] = (sa * ya + sb * yb).astype(BF16)
        ya_ref[...] = ya.astype(BF16)
        yb_ref[...] = yb.astype(BF16)

    blk = pl.BlockSpec((tm, D_MODEL), lambda i: (i, 0))
    narrow = pl.BlockSpec((tm, A_WIDTH), lambda i: (i, 0))
    wide = pl.BlockSpec((tm, 2 * D_MODEL), lambda i: (i, 0))
    wspec = pl.BlockSpec((A_WIDTH, D_MODEL), lambda i: (0, 0))
    out = jax.ShapeDtypeStruct((t, D_MODEL), BF16)
    return pl.pallas_call(
        body, name=name, grid=(t // tm,),
        in_specs=[narrow, narrow, wspec, wspec, wide] + ([pl.BlockSpec(memory_space=pl.ANY)] if has_dep else []),
        out_specs=[blk, blk, blk], out_shape=[out, out, out], compiler_params=_params(("parallel",)),
    )(*((oa, ob, wa, wb, gates) + ((dep,) if has_dep else ())))


def _branch_bwd_x(name, dya, dyb, wa, wb):
    t = dya.shape[0]
    tm = _pick(t, (1088, 768, 512, 256, 128))

    def body(da_ref, db_ref, wa_ref, wb_ref, oa_ref, ob_ref):
        oa_ref[...] = lax.dot_general(da_ref[...], wa_ref[...], _NT, preferred_element_type=F32)
        ob_ref[...] = lax.dot_general(db_ref[...], wb_ref[...], _NT, preferred_element_type=F32).astype(BF16)

    blk = pl.BlockSpec((tm, D_MODEL), lambda i: (i, 0))
    narrow = pl.BlockSpec((tm, A_WIDTH), lambda i: (i, 0))
    wspec = pl.BlockSpec((A_WIDTH, D_MODEL), lambda i: (0, 0), pipeline_mode=pl.Buffered(1))
    return pl.pallas_call(
        body, name=name, grid=(t // tm,), in_specs=[blk, blk, wspec, wspec], out_specs=[narrow, narrow],
        out_shape=[jax.ShapeDtypeStruct((t, A_WIDTH), F32), jax.ShapeDtypeStruct((t, B_WIDTH), BF16)],
        compiler_params=_params(("parallel",)),
    )(dya, dyb, wa, wb)


def _branch_bwd_w(name, oa, ob, dya, dyb):
    t = oa.shape[0]
    tn = 512

    def body(oa_ref, ob_ref, da_ref, db_ref, wa_ref, wb_ref):
        wa_ref[...] = lax.dot_general(_bf(oa_ref[...]), da_ref[...], _TN, preferred_element_type=F32).astype(BF16)
        wb_ref[...] = lax.dot_general(_bf(ob_ref[...]), db_ref[...], _TN, preferred_element_type=F32).astype(BF16)

    whole = pl.BlockSpec((t, A_WIDTH), lambda j: (0, 0), pipeline_mode=pl.Buffered(1))
    cols = pl.BlockSpec((t, tn), lambda j: (0, j))
    out_spec = pl.BlockSpec((A_WIDTH, tn), lambda j: (0, j))
    out = jax.ShapeDtypeStruct((A_WIDTH, D_MODEL), BF16)
    return pl.pallas_call(
        body, name=name, grid=(D_MODEL // tn,), in_specs=[whole, whole, cols, cols], out_specs=[out_spec, out_spec],
        out_shape=[out, out], compiler_params=_params(("parallel",)),
    )(oa, ob, dya, dyb)


def _mix_out_gate_bwd(name, dh, wo, gates, ya, yb):
    t = gates.shape[0]
    tm = _pick(t, (544, 384, 256, 128))

    def body(dh_ref, w_ref, g_ref, ya_ref, yb_ref, dya_ref, dyb_ref, dg_ref):
        dm = lax.dot_general(_bf(dh_ref[...]), w_ref[...], _NT, preferred_element_type=F32)
        sa = jax.nn.sigmoid(g_ref[:, 0:D_MODEL])
        sb = jax.nn.sigmoid(g_ref[:, D_MODEL:2 * D_MODEL])
        dya_ref[...] = (dm * sa).astype(BF16)
        dyb_ref[...] = (dm * sb).astype(BF16)
        dg_ref[:, 0:D_MODEL] = (dm * ya_ref[...].astype(F32) * (sa * (1.0 - sa))).astype(BF16)
        dg_ref[:, D_MODEL:2 * D_MODEL] = (dm * yb_ref[...].astype(F32) * (sb * (1.0 - sb))).astype(BF16)

    blk = pl.BlockSpec((tm, D_MODEL), lambda i: (i, 0))
    wide = pl.BlockSpec((tm, 2 * D_MODEL), lambda i: (i, 0))
    out = jax.ShapeDtypeStruct((t, D_MODEL), BF16)
    return pl.pallas_call(
        body, name=name, grid=(t // tm,),
        in_specs=[blk, pl.BlockSpec((D_MODEL, D_MODEL), lambda i: (0, 0)), wide, blk, blk], out_specs=[blk, blk, wide],
        out_shape=[out, out, jax.ShapeDtypeStruct((t, 2 * D_MODEL), BF16)], compiler_params=_params(("parallel",)),
    )(dh, wo, gates, ya, yb)


def _ffn_out_loss(name, a, w, res, gf, tgt, alpha):
    t, k = a.shape
    b, s, _ = tgt.shape
    l = t // b
    tm = _pick(t, (544, 384, 256, 128))
    per = l // tm
    assert per * tm == l and tm > PREFIX and l - PREFIX == s

    def body(a_ref, w_ref, r_ref, g_ref, t_ref, dh_ref, loss_ref, dg_ref, buf, sem):
        i = pl.program_id(0)
        bi, r = i // per, i % per

        def first_rows():
            return pltpu.make_async_copy(t_ref.at[bi, pl.ds(0, tm - PREFIX)], buf.at[pl.ds(PREFIX, tm - PREFIX)], sem)

        def later_rows():
            return pltpu.make_async_copy(t_ref.at[bi, pl.ds(pl.multiple_of(r * tm - PREFIX, 8), tm)], buf, sem)

        @pl.when(r == 0)
        def _():
            buf[0:PREFIX, :] = jnp.zeros((PREFIX, D_MODEL), F32)
            first_rows().start()

        if per > 1:
            @pl.when(r > 0)
            def _():
                later_rows().start()

        acc = lax.dot_general(a_ref[...], w_ref[...], _NN, preferred_element_type=F32)

        @pl.when(r == 0)
        def _():
            first_rows().wait()

        if per > 1:
            @pl.when(r > 0)
            def _():
                later_rows().wait()

        hv = acc * alpha + r_ref[...]
        row = lax.broadcasted_iota(jnp.int32, (tm, 1), 0)
        real = ((r > 0) | (row >= PREFIX)).astype(F32)
        g = g_ref[...]
        rn = lax.rsqrt(jnp.mean(hv * hv, axis=-1, keepdims=True) + EPS)
        xn = hv * rn
        err = (xn * g - buf[...]) * real
        lpart = 0.5 * jnp.sum(jnp.mean(err * err, axis=-1, keepdims=True), axis=0, keepdims=True)
        dy = err * (1.0 / D_MODEL)
        tv = dy * g
        dot = jnp.mean(tv * hv, axis=-1, keepdims=True)
        dh_ref[...] = rn * tv - hv * (rn * rn * rn * dot)
        gpart = jnp.sum(dy * xn, axis=0, keepdims=True)

        @pl.when(i == 0)
        def _():
            loss_ref[...] = jnp.zeros_like(loss_ref)
            dg_ref[...] = jnp.zeros_like(dg_ref)

        loss_ref[...] += jnp.broadcast_to(lpart, loss_ref.shape)
        dg_ref[...] += gpart

    row_spec = pl.BlockSpec((tm, D_MODEL), lambda i: (i, 0))
    vec = pl.BlockSpec((1, D_MODEL), lambda i: (0, 0))
    return pl.pallas_call(
        body, name=name, grid=(t // tm,),
        in_specs=[pl.BlockSpec((tm, k), lambda i: (i, 0)),
                  pl.BlockSpec((k, D_MODEL), lambda i: (0, 0), pipeline_mode=pl.Buffered(1)), row_spec, vec,
                  pl.BlockSpec(memory_space=pl.ANY)],
        out_specs=[row_spec, pl.BlockSpec((8, 128), lambda i: (0, 0)), vec],
        out_shape=[jax.ShapeDtypeStruct((t, D_MODEL), F32), jax.ShapeDtypeStruct((8, 128), F32),
                   jax.ShapeDtypeStruct((1, D_MODEL), F32)],
        scratch_shapes=[pltpu.VMEM((tm, D_MODEL), F32), pltpu.SemaphoreType.DMA(())],
        compiler_params=_params(("arbitrary",)),
    )(a, w, res, gf, tgt)


def _fgate_fwd(name, f3, bf_row):
    b, l, _ = f3.shape
    nb = l // BLOCK

    def body(f_ref, b_ref, cc_ref, cr_ref):
        r_i = lax.broadcasted_iota(jnp.int32, (BLOCK, BLOCK), 0)
        c_i = lax.broadcasted_iota(jnp.int32, (BLOCK, BLOCK), 1)
        tri = (r_i >= c_i).astype(F32)
        carry = jnp.zeros((1, 128), F32)
        for blk in range(nb):
            rows = slice(blk * BLOCK, (blk + 1) * BLOCK)
            z = f_ref[rows, :] + b_ref[...]
            lf = jnp.minimum(z, 0.0) - jnp.log(1.0 + jnp.exp(-jnp.abs(z)))
            cb = jnp.dot(tri, lf, preferred_element_type=F32, precision=lax.Precision.HIGHEST) + carry
            carry = cb[BLOCK - 1:BLOCK, :]
            cbt = cb.T
            for hh in range(B_HEADS):
                cc_ref[hh, rows, :] = jnp.sum(jnp.where(c_i == hh, cb, 0.0), axis=1, keepdims=True)
                cr_ref[hh, :, rows] = cbt[hh:hh + 1, :]

    return pl.pallas_call(
        body, name=name, grid=(b,),
        in_specs=[pl.BlockSpec((None, l, 128), lambda bi: (bi, 0, 0)),
                  pl.BlockSpec((1, 128), lambda bi: (0, 0))],
        out_specs=[pl.BlockSpec((None, B_HEADS, l, 1), lambda bi: (bi, 0, 0, 0)),
                   pl.BlockSpec((None, B_HEADS, 1, l), lambda bi: (bi, 0, 0, 0))],
        out_shape=[jax.ShapeDtypeStruct((b, B_HEADS, l, 1), F32), jax.ShapeDtypeStruct((b, B_HEADS, 1, l), F32)],
        compiler_params=_params(("parallel",)),
    )(f3, bf_row)


def _fgate_bwd(name, f3, bf_row, dcq, dck):
    b, l, _ = f3.shape
    nb = l // BLOCK

    def body(f_ref, b_ref, dcq_ref, dck_ref, df_ref, db_ref):
        r_i = lax.broadcasted_iota(jnp.int32, (BLOCK, BLOCK), 0)
        c_i = lax.broadcasted_iota(jnp.int32, (BLOCK, BLOCK), 1)
        tri = (r_i <= c_i).astype(F32)
        carry = jnp.zeros((1, 128), F32)
        total = jnp.zeros((1, 128), F32)
        for blk in range(nb - 1, -1, -1):
            rows = slice(blk * BLOCK, (blk + 1) * BLOCK)
            krows = jnp.concatenate([dck_ref[hh, :, rows] for hh in range(B_HEADS)]
                                    + [jnp.zeros((BLOCK - B_HEADS, BLOCK), F32)], axis=0)
            dcb = krows.T
            for hh in range(B_HEADS):
                dcb = dcb + jnp.where(c_i == hh, dcq_ref[hh, rows, :], 0.0)
            rc = jnp.dot(tri, dcb, preferred_element_type=F32, precision=lax.Precision.HIGHEST) + carry
            carry = rc[0:1, :]
            z = f_ref[rows, :] + b_ref[...]
            df = rc * (1.0 / (1.0 + jnp.exp(z)))
            df_ref[rows, :] = df.astype(BF16)
            total = total + jnp.sum(df, axis=0, keepdims=True)

        @pl.when(pl.program_id(0) == 0)
        def _():
            db_ref[...] = total

        @pl.when(pl.program_id(0) > 0)
        def _():
            db_ref[...] += total

    return pl.pallas_call(
        body, name=name, grid=(b,),
        in_specs=[pl.BlockSpec((None, l, 128), lambda bi: (bi, 0, 0)),
                  pl.BlockSpec((1, 128), lambda bi: (0, 0)),
                  pl.BlockSpec((None, B_HEADS, l, 1), lambda bi: (bi, 0, 0, 0)),
                  pl.BlockSpec((None, B_HEADS, 1, l), lambda bi: (bi, 0, 0, 0))],
        out_specs=[pl.BlockSpec((None, l, 128), lambda bi: (bi, 0, 0)), pl.BlockSpec((1, 128), lambda bi: (0, 0))],
        out_shape=[jax.ShapeDtypeStruct((b, l, 128), BF16), jax.ShapeDtypeStruct((1, 128), F32)],
        compiler_params=_params(("arbitrary",)),
    )(f3, bf_row, dcq, dck)


A_Q_BLK = B_SEG // A_WIDTH
A_K_BLK = (B_SEG + A_WIDTH) // 128
A_V_BLK = A_K_BLK + 1
A_SEG_BLK = B_SEG // A_SEG
STACK = A_HEADS * BLOCK


def _lane_lo():
    return lax.broadcasted_iota(jnp.int32, (1, 128), 1) < HEAD_DIM


def _stack_heads(x, masked):
    lo = _lane_lo()
    blks = [x[:, 128 * j:128 * (j + 1)] for j in range(4)]
    if not masked:
        return jnp.concatenate(blks + blks, axis=0)
    zero = jnp.zeros_like(blks[0])
    return jnp.concatenate([jnp.where(lo, bk, zero) for bk in blks] + [jnp.where(lo, zero, bk) for bk in blks], axis=0)


def _unstack_heads(y):
    lo = _lane_lo()
    return jnp.concatenate([jnp.where(lo, y[128 * j:128 * (j + 1)], y[128 * (4 + j):128 * (5 + j)]) for j in range(4)], axis=1)


def _swa_bias(slopes):
    r_i = jnp.arange(STACK)[:, None]
    c_i = jnp.arange(3 * BLOCK)[None, :]
    seg = c_i >> 7
    out = []
    for n in range(3):
        qpos = n * BLOCK + (r_i & (BLOCK - 1))
        kpos = jnp.where(seg == 0, c_i, (n - 2) * BLOCK + c_i)
        dist = qpos - kpos
        band = (seg != 0) & (dist < BLOCK) & (kpos >= PREFIX)
        meta = (seg == 0) & (c_i >= N_PAD)
        out.append(jnp.where((dist >= 0) & (band | meta), -slopes * dist.astype(F32), NEG))
    return jnp.stack(out, axis=0)


def _swa_scores(q, kcat, n, slope, bias):
    s = lax.dot_general(q, kcat, _NT, preferred_element_type=F32) + bias
    further = slope * (-BLOCK * jnp.maximum(n - 2, 0)).astype(F32)
    return jnp.concatenate([s[:, 0:BLOCK] + further, s[:, BLOCK:]], axis=1)


def _swa_specs():
    def kv(col_blk):
        return [pl.BlockSpec((None, BLOCK, 128), lambda b, n: (b, 0, col_blk)),
                pl.BlockSpec((None, BLOCK, 128), lambda b, n: (b, jnp.maximum(n - 1, 0), col_blk)),
                pl.BlockSpec((None, BLOCK, 128), lambda b, n: (b, n, col_blk))]

    q_spec = pl.BlockSpec((None, BLOCK, A_WIDTH), lambda b, n: (b, n, A_Q_BLK))
    o_spec = pl.BlockSpec((None, BLOCK, A_WIDTH), lambda b, n: (b, n, 0))
    col = pl.BlockSpec((STACK, 1), lambda b, n: (0, 0))
    bias = pl.BlockSpec((None, STACK, 3 * BLOCK), lambda b, n: (jnp.minimum(n, 2), 0, 0))
    lse_spec = pl.BlockSpec((None, A_HEADS, BLOCK, 1), lambda b, n: (b, 0, n, 0))
    return q_spec, kv(A_K_BLK), kv(A_V_BLK), o_spec, [col, col, bias], lse_spec


def _swa_fwd(name, qkv, slopes, sinks, bias):
    b, l, _ = qkv.shape
    nb = l // BLOCK

    def body(q_ref, k0_ref, kp_ref, kc_ref, v0_ref, vp_ref, vc_ref, sl_ref, sk_ref, bias_ref, o_ref, lse_ref):
        n = pl.program_id(1)
        qs = _stack_heads(q_ref[...], True) * SCALE
        kcat = jnp.concatenate([k0_ref[...], kp_ref[...], kc_ref[...]], axis=0)
        vcat = jnp.concatenate([v0_ref[...], vp_ref[...], vc_ref[...]], axis=0)
        s = _swa_scores(qs, kcat, n, sl_ref[...], bias_ref[...])
        sink = sk_ref[...]
        m = jnp.maximum(jnp.max(s, axis=-1, keepdims=True), sink)
        p = jnp.exp(s - m)
        den = jnp.sum(p, axis=-1, keepdims=True) + jnp.exp(sink - m)
        o = lax.dot_general(p.astype(BF16), vcat, _NN, preferred_element_type=F32) / den
        o_ref[...] = _unstack_heads(o)
        lse_ref[...] = (m + jnp.log(den)).reshape(A_HEADS, BLOCK, 1)

    q_spec, k_specs, v_specs, o_spec, consts, lse_spec = _swa_specs()
    return pl.pallas_call(
        body, name=name, grid=(b, nb),
        in_specs=[q_spec] + k_specs + v_specs + consts, out_specs=[o_spec, lse_spec],
        out_shape=[jax.ShapeDtypeStruct((b, l, A_WIDTH), F32), jax.ShapeDtypeStruct((b, A_HEADS, l, 1), F32)],
        compiler_params=_params(("parallel", "parallel")),
    )(qkv, qkv, qkv, qkv, qkv, qkv, qkv, slopes, sinks, bias)


def _swa_bwd(name, qkv, o, lse, do, slopes, sinks, bias, dqkv):
    b, l, _ = qkv.shape
    nb = l // BLOCK

    def body(q_ref, k0_ref, kp_ref, kc_ref, v0_ref, vp_ref, vc_ref, o_ref, lse_ref, do_ref, sl_ref, sk_ref, bias_ref, _,
             dx_ref, ds_ref, dk_acc, dv_acc):
        bi = pl.program_id(0)
        n = pl.program_id(1)
        qs = _stack_heads(q_ref[...], True) * SCALE
        dos32 = _stack_heads(do_ref[...], True)
        dos = dos32.astype(BF16)
        os_ = _stack_heads(o_ref[...], False)
        lsev = lse_ref[...].reshape(STACK, 1)
        kcat = jnp.concatenate([k0_ref[...], kp_ref[...], kc_ref[...]], axis=0)
        vcat = jnp.concatenate([v0_ref[...], vp_ref[...], vc_ref[...]], axis=0)
        s = _swa_scores(qs, kcat, n, sl_ref[...], bias_ref[...])
        p = jnp.exp(s - lsev)
        dsum = jnp.sum(dos32 * os_, axis=-1, keepdims=True)
        dp = lax.dot_general(dos, vcat, _NT, preferred_element_type=F32)
        dsc = (p * (dp - dsum)).astype(BF16)
        dq = lax.dot_general(dsc, kcat, _NN, preferred_element_type=F32) * SCALE
        row0 = pl.multiple_of(n * BLOCK, BLOCK)
        dx_ref[pl.ds(row0, BLOCK), 0:A_WIDTH] = _unstack_heads(dq).astype(BF16)
        dkc = lax.dot_general(dsc, qs, _TN, preferred_element_type=F32)
        dvc = lax.dot_general(p.astype(BF16), dos, _TN, preferred_element_type=F32)

        @pl.when(n == 0)
        def _():
            dk_acc[...] = jnp.zeros_like(dk_acc)
            dv_acc[...] = jnp.zeros_like(dv_acc)

        starts = (0, pl.multiple_of(jnp.maximum(n - 1, 0) * BLOCK, BLOCK), row0)
        for t, st in enumerate(starts):
            dk_acc[pl.ds(st, BLOCK), :] += dkc[t * BLOCK:(t + 1) * BLOCK, :]
            dv_acc[pl.ds(st, BLOCK), :] += dvc[t * BLOCK:(t + 1) * BLOCK, :]

        @pl.when(n == nb - 1)
        def _():
            dx_ref[:, A_WIDTH:A_WIDTH + 128] = dk_acc[...].astype(BF16)
            dx_ref[:, A_WIDTH + 128:A_SEG] = dv_acc[...].astype(BF16)

        dsink = -(jnp.exp(sk_ref[...] - lsev) * dsum)
        r8 = lax.broadcasted_iota(jnp.int32, (8, 128), 0)
        acc = jnp.zeros((8, 128), F32)
        for hh in range(A_HEADS):
            acc = acc + jnp.where(r8 == hh, jnp.sum(dsink[hh * BLOCK:(hh + 1) * BLOCK, :]), 0.0)

        @pl.when((bi == 0) & (n == 0))
        def _():
            ds_ref[...] = jnp.zeros_like(ds_ref)

        ds_ref[...] += acc

    q_spec, k_specs, v_specs, o_spec, consts, lse_spec = _swa_specs()
    return pl.pallas_call(
        body, name=name, grid=(b, nb),
        in_specs=[q_spec] + k_specs + v_specs + [o_spec, lse_spec, o_spec] + consts + [pl.BlockSpec(memory_space=pl.ANY)],
        out_specs=[pl.BlockSpec((None, l, A_SEG), lambda bb, n: (bb, 0, A_SEG_BLK)),
                   pl.BlockSpec((8, 128), lambda bb, n: (0, 0))],
        out_shape=[jax.ShapeDtypeStruct(dqkv.shape, BF16), jax.ShapeDtypeStruct((8, 128), F32)],
        scratch_shapes=[pltpu.VMEM((l, 128), F32), pltpu.VMEM((l, 128), F32)],
        input_output_aliases={13: 0},
        compiler_params=_params(("arbitrary", "arbitrary")),
    )(qkv, qkv, qkv, qkv, qkv, qkv, qkv, o, lse, do, slopes, sinks, bias, dqkv)


def _fox_mask(qk, ck, i):
    kh = qk.shape[1]
    qpos = i * BLOCK + lax.broadcasted_iota(jnp.int32, (BLOCK, kh), 0)
    kpos = lax.broadcasted_iota(jnp.int32, (BLOCK, kh), 1)
    return jnp.where((kpos <= qpos) & (kpos >= N_PAD), qk - ck, NEG)


def _pick_head(x, hh):
    lo = _lane_lo()
    return jnp.where(lo if hh == 0 else jnp.logical_not(lo), x, jnp.zeros_like(x))


def _both_heads(x):
    return jnp.concatenate([_pick_head(x, 0), _pick_head(x, 1)], axis=0)


def _fox_specs(l):
    pair = pl.BlockSpec((None, l, PAIR_W), lambda bi, hp: (bi, 0, hp))
    half = pl.BlockSpec((None, l, 128), lambda bi, hp: (bi, 0, hp))
    colv = pl.BlockSpec((None, 2, l, 1), lambda bi, hp: (bi, hp, 0, 0))
    rowv = pl.BlockSpec((None, 2, 1, l), lambda bi, hp: (bi, hp, 0, 0))
    return pair, half, colv, rowv


def _fox_fwd(name, qkv, c_col, c_row):
    b, l, _ = qkv.shape
    nb = l // BLOCK

    def body(x_ref, cc_ref, cr_ref, o_ref, lse_ref):
        for i in range(nb):
            rows = slice(i * BLOCK, (i + 1) * BLOCK)
            kh = (i + 1) * BLOCK
            qblk = x_ref[rows, 0:128]
            kv = x_ref[0:kh, 128:256]
            vv = x_ref[0:kh, 256:384]
            qk = lax.dot_general(_both_heads(qblk) * SCALE, kv, _NT, preferred_element_type=F32)
            ps, dens = [], []
            for hh in range(2):
                s = _fox_mask(qk[hh * BLOCK:(hh + 1) * BLOCK], cr_ref[hh, :, 0:kh], i)
                m = jnp.max(s, axis=-1, keepdims=True)
                p = jnp.exp(s - m)
                den = jnp.sum(p, axis=-1, keepdims=True)
                ps.append(p.astype(BF16))
                dens.append(den)
                lse_ref[hh, rows, :] = (m + jnp.log(den)) + cc_ref[hh, rows, :]
            pv = lax.dot_general(jnp.concatenate(ps, axis=0), vv, _NN, preferred_element_type=F32)
            o_ref[rows, :] = jnp.where(_lane_lo(), pv[0:BLOCK] / dens[0], pv[BLOCK:2 * BLOCK] / dens[1]).astype(BF16)

    pair, half, colv, rowv = _fox_specs(l)
    return pl.pallas_call(
        body, name=name, grid=(b, 4), in_specs=[pair, colv, rowv], out_specs=[half, colv],
        out_shape=[jax.ShapeDtypeStruct((b, l, B_WIDTH), BF16), jax.ShapeDtypeStruct((b, B_HEADS, l, 1), F32)],
        compiler_params=_params(("parallel", "parallel")),
    )(qkv, c_col, c_row)


def _fox_bwd(name, qkv, c_col, c_row, o, lse, do):
    b, l, _ = qkv.shape
    nb = l // BLOCK

    def body(x_ref, cc_ref, cr_ref, o_ref, lse_ref, do_ref, dx_ref, dcq_ref, dck_ref, dk_acc, dv_acc):
        dk_acc[...] = jnp.zeros_like(dk_acc)
        dv_acc[...] = jnp.zeros_like(dv_acc)
        dck_ref[...] = jnp.zeros_like(dck_ref)
        for i in range(nb):
            rows = slice(i * BLOCK, (i + 1) * BLOCK)
            kh = (i + 1) * BLOCK
            qblk = x_ref[rows, 0:128]
            kv = x_ref[0:kh, 128:256]
            vv = x_ref[0:kh, 256:384]
            doblk = do_ref[rows, :]
            ov = o_ref[rows, :].astype(F32)
            q2 = _both_heads(qblk) * SCALE
            do2 = _both_heads(doblk)
            qk = lax.dot_general(q2, kv, _NT, preferred_element_type=F32)
            dp = lax.dot_general(do2, vv, _NT, preferred_element_type=F32)
            ps, dss = [], []
            for hh in range(2):
                half = slice(hh * BLOCK, (hh + 1) * BLOCK)
                s = _fox_mask(qk[half], cr_ref[hh, :, 0:kh], i)
                p = jnp.exp(s - (lse_ref[hh, rows, :] - cc_ref[hh, rows, :]))
                dsum = jnp.sum(do2[half].astype(F32) * ov, axis=-1, keepdims=True)
                ds = p * (dp[half] - dsum)
                ps.append(p.astype(BF16))
                dss.append(ds.astype(BF16))
                dcq_ref[hh, rows, :] = jnp.sum(ds, axis=-1, keepdims=True)
                dck_ref[hh, :, 0:kh] -= jnp.sum(ds, axis=0, keepdims=True)
            p2 = jnp.concatenate(ps, axis=0)
            ds2 = jnp.concatenate(dss, axis=0)
            dq = lax.dot_general(ds2, kv, _NN, preferred_element_type=F32) * SCALE
            dk_acc[0:kh, :] += lax.dot_general(ds2, q2, _TN, preferred_element_type=F32)
            dv_acc[0:kh, :] += lax.dot_general(p2, do2, _TN, preferred_element_type=F32)
            dx_ref[rows, 0:128] = jnp.where(_lane_lo(), dq[0:BLOCK], dq[BLOCK:2 * BLOCK]).astype(BF16)
        dx_ref[:, 128:256] = dk_acc[...].astype(BF16)
        dx_ref[:, 256:384] = dv_acc[...].astype(BF16)

    pair, half, colv, rowv = _fox_specs(l)
    return pl.pallas_call(
        body, name=name, grid=(b, 4), in_specs=[pair, colv, rowv, half, colv, half],
        out_specs=[pair, colv, rowv],
        out_shape=[jax.ShapeDtypeStruct(qkv.shape, BF16), jax.ShapeDtypeStruct((b, B_HEADS, l, 1), F32),
                   jax.ShapeDtypeStruct((b, B_HEADS, 1, l), F32)],
        scratch_shapes=[pltpu.VMEM((l, 128), F32), pltpu.VMEM((l, 128), F32)],
        compiler_params=_params(("parallel", "parallel")),
    )(qkv, c_col, c_row, o, lse, do)


_FLIPS = ((0, 0, 1), (0, 1, 0), (0, 1, 1), (1, 0, 0), (1, 0, 1), (1, 1, 0), (1, 1, 1))


def _peer_table():
    x, y, c = lax.axis_index("x"), lax.axis_index("y"), lax.axis_index("c")
    me = 4 * x + 2 * y + c
    peers = []
    for fx, fy, fc in _FLIPS:
        px = 1 - x if fx else x
        py = 1 - y if fy else y
        pc = 1 - c if fc else c
        peers.append(((px, py, pc), 4 * px + 2 * py + pc))
    return me, peers


_HBM = pl.BlockSpec(memory_space=pltpu.HBM)
_SEM = pl.BlockSpec(memory_space=pltpu.SEMAPHORE)
_ANY = pl.BlockSpec(memory_space=pl.ANY)
_EFFECT = pltpu.SideEffectType.DATAFLOW_SIDE_EFFECTING


def _split_copy(srcs_are_pieces, src_refs, land_refs, send_sem, recv_sem, a, kk, me, peers, arriving):
    dev, lin = peers[kk]
    npeer = len(_FLIPS)
    src = src_refs[a] if srcs_are_pieces[a] else src_refs[a].at[lin]
    dst = land_refs[a].at[lin] if arriving else land_refs[a].at[me]
    return pltpu.make_async_remote_copy(src_ref=src, dst_ref=dst, send_sem=send_sem.at[a * npeer + kk],
                                        recv_sem=recv_sem.at[a * npeer + kk], device_id=dev, device_id_type=MESH_ID)


def _xchg_start(name, gather, scatter, after=None):
    me_out = 4 * lax.axis_index("x") + 2 * lax.axis_index("y") + lax.axis_index("c")
    srcs = list(gather) + list(scatter)
    is_piece = [True] * len(gather) + [False] * len(scatter)
    lands = []
    for a, piece in zip(srcs, is_piece):
        if piece:
            lands.append(lax.dynamic_update_slice(lax.empty((N_DEV,) + tuple(a.shape), a.dtype), a[None],
                                                  (me_out,) + (0,) * a.ndim))
        else:
            lands.append(lax.empty(tuple(a.shape), a.dtype))
    n = len(srcs)
    nsem = n * len(_FLIPS)
    has_after = after is not None

    def body(*refs):
        src_refs = refs[:n]
        land_refs = refs[n:2 * n]
        outs = refs[2 * n + (1 if has_after else 0):]
        send_sem, recv_sem = outs[0], outs[1]
        token = outs[-1]
        me, peers = _peer_table()
        for kk in range(len(_FLIPS)):
            for a in range(n):
                _split_copy(is_piece, src_refs, land_refs, send_sem, recv_sem, a, kk, me, peers, False).start()
        token[...] = jnp.zeros_like(token)

    out_shape = ([pltpu.SemaphoreType.DMA((nsem,)), pltpu.SemaphoreType.DMA((nsem,))]
                 + [pltpu.HBM(tuple(a.shape), a.dtype) for a in srcs] + [pltpu.HBM(tuple(a.shape), a.dtype) for a in lands]
                 + [jax.ShapeDtypeStruct((8, 128), F32)])
    args = [pltpu.with_memory_space_constraint(a, pltpu.HBM) for a in srcs + lands] + ([after] if has_after else [])
    res = pl.pallas_call(
        body, name=name, out_shape=out_shape,
        in_specs=[_HBM] * (2 * n) + ([_ANY] if has_after else []),
        out_specs=[_SEM, _SEM] + [_HBM] * (2 * n) + [pl.BlockSpec(memory_space=pltpu.VMEM)],
        input_output_aliases={i: 2 + i for i in range(2 * n)},
        compiler_params=pltpu.CompilerParams(has_side_effects=_EFFECT),
    )(*args)
    state = (res[0], res[1], list(res[2:2 + n]), list(res[2 + n:2 + 2 * n]), is_piece)
    return state, res[-1]


def _xchg_wait(name, state, after):
    send_sem, recv_sem, srcs, lands, is_piece = state
    n = len(srcs)

    def body(*refs):
        src_refs = refs[:n]
        land_refs = refs[n:2 * n]
        s_sem, r_sem = refs[2 * n], refs[2 * n + 1]
        me, peers = _peer_table()
        for kk in range(len(_FLIPS)):
            for a in range(n):
                cp = _split_copy(is_piece, src_refs, land_refs, s_sem, r_sem, a, kk, me, peers, True)
                cp.wait_send()
                cp.wait_recv()

    out_shape = [pltpu.HBM(tuple(a.shape), a.dtype) for a in srcs] + [pltpu.HBM(tuple(a.shape), a.dtype) for a in lands]
    res = pl.pallas_call(
        body, name=name, out_shape=out_shape,
        in_specs=[_HBM] * (2 * n) + [_SEM, _SEM, _ANY], out_specs=[_HBM] * (2 * n),
        input_output_aliases={i: i for i in range(2 * n)},
        compiler_params=pltpu.CompilerParams(has_side_effects=_EFFECT),
    )(*srcs, *lands, send_sem, recv_sem, after)
    return list(res[n:]), list(res[:n])


_SIB = (0, 0, 1)
_ICI = ((0, 1, 0), (1, 0, 0), (1, 1, 0))


def _flip(fl):
    x, y, c = lax.axis_index("x"), lax.axis_index("y"), lax.axis_index("c")
    px = 1 - x if fl[0] else x
    py = 1 - y if fl[1] else y
    pc = 1 - c if fl[2] else c
    return (px, py, pc), 4 * px + 2 * py + pc


def _gather2_start(name, pieces, after=None):
    me_out = 4 * lax.axis_index("x") + 2 * lax.axis_index("y") + lax.axis_index("c")
    pieces = list(pieces)
    n = len(pieces)
    lands = [lax.dynamic_update_slice(lax.empty((N_DEV,) + tuple(a.shape), a.dtype), a[None],
                                      (me_out,) + (0,) * a.ndim) for a in pieces]
    first = (_SIB,) + _ICI
    has_after = after is not None

    def body(*refs):
        src_refs, land_refs = refs[:n], refs[n:2 * n]
        outs = refs[2 * n + (1 if has_after else 0):]
        send_sem, recv_sem, token = outs[0], outs[1], outs[-1]
        _, me = _flip((0, 0, 0))
        for kk, fl in enumerate(first):
            dev, _ = _flip(fl)
            for a in range(n):
                pltpu.make_async_remote_copy(src_ref=src_refs[a], dst_ref=land_refs[a].at[me],
                                             send_sem=send_sem.at[a * 4 + kk], recv_sem=recv_sem.at[a * 4 + kk],
                                             device_id=dev, device_id_type=MESH_ID).start()
        token[...] = jnp.zeros_like(token)

    hbm = [pltpu.HBM(tuple(a.shape), a.dtype) for a in pieces + lands]
    res = pl.pallas_call(
        body, name=name,
        out_shape=[pltpu.SemaphoreType.DMA((4 * n,)), pltpu.SemaphoreType.DMA((4 * n,))] + hbm
        + [jax.ShapeDtypeStruct((8, 128), F32)],
        in_specs=[_HBM] * (2 * n) + ([_ANY] if has_after else []),
        out_specs=[_SEM, _SEM] + [_HBM] * (2 * n) + [pl.BlockSpec(memory_space=pltpu.VMEM)],
        input_output_aliases={i: 2 + i for i in range(2 * n)},
        compiler_params=pltpu.CompilerParams(has_side_effects=_EFFECT),
    )(*([pltpu.with_memory_space_constraint(a, pltpu.HBM) for a in pieces + lands] + ([after] if has_after else [])))
    return (res[0], res[1], list(res[2:2 + n]), list(res[2 + n:2 + 2 * n])), res[-1]


def _gather2_forward(name, state, after):
    send_a, recv_a, pieces, lands = state
    n = len(pieces)
    first = (_SIB,) + _ICI

    def body(*refs):
        src_refs, land_refs = refs[:n], refs[n:2 * n]
        s_a, r_a = refs[2 * n], refs[2 * n + 1]
        outs = refs[2 * n + 3:]
        send_b, recv_b, token = outs[0], outs[1], outs[-1]
        for kk, fl in enumerate(first):
            dev, lin = _flip(fl)
            for a in range(n):
                cp = pltpu.make_async_remote_copy(src_ref=src_refs[a], dst_ref=land_refs[a].at[lin],
                                                  send_sem=s_a.at[a * 4 + kk], recv_sem=r_a.at[a * 4 + kk],
                                                  device_id=dev, device_id_type=MESH_ID)
                cp.wait_send()
                cp.wait_recv()
        sib, _ = _flip(_SIB)
        for j, fl in enumerate(_ICI):
            _, lin = _flip(fl)
            for a in range(n):
                pltpu.make_async_remote_copy(src_ref=land_refs[a].at[lin], dst_ref=land_refs[a].at[lin],
                                             send_sem=send_b.at[a * 3 + j], recv_sem=recv_b.at[a * 3 + j],
                                             device_id=sib, device_id_type=MESH_ID).start()
        token[...] = jnp.zeros_like(token)

    hbm = [pltpu.HBM(tuple(a.shape), a.dtype) for a in pieces + lands]
    res = pl.pallas_call(
        body, name=name,
        out_shape=[pltpu.SemaphoreType.DMA((3 * n,)), pltpu.SemaphoreType.DMA((3 * n,))] + hbm
        + [jax.ShapeDtypeStruct((8, 128), F32)],
        in_specs=[_HBM] * (2 * n) + [_SEM, _SEM, _ANY],
        out_specs=[_SEM, _SEM] + [_HBM] * (2 * n) + [pl.BlockSpec(memory_space=pltpu.VMEM)],
        input_output_aliases={i: 2 + i for i in range(2 * n)},
        compiler_params=pltpu.CompilerParams(has_side_effects=_EFFECT),
    )(*pieces, *lands, send_a, recv_a, after)
    return (res[0], res[1], list(res[2 + n:2 + 2 * n])), res[-1]


def _gather2_wait(name, state, after):
    send_b, recv_b, lands = state
    n = len(lands)

    def body(*refs):
        land_refs = refs[:n]
        s_b, r_b = refs[n], refs[n + 1]
        sib, _ = _flip(_SIB)
        for j, fl in enumerate(_ICI):
            _, sent = _flip(fl)
            _, arriving = _flip((fl[0], fl[1], 1))
            for a in range(n):
                cp = pltpu.make_async_remote_copy(src_ref=land_refs[a].at[sent], dst_ref=land_refs[a].at[arriving],
                                                  send_sem=s_b.at[a * 3 + j], recv_sem=r_b.at[a * 3 + j],
                                                  device_id=sib, device_id_type=MESH_ID)
                cp.wait_send()
                cp.wait_recv()

    res = pl.pallas_call(
        body, name=name, out_shape=[pltpu.HBM(tuple(a.shape), a.dtype) for a in lands],
        in_specs=[_HBM] * n + [_SEM, _SEM, _ANY], out_specs=[_HBM] * n,
        input_output_aliases={i: i for i in range(n)},
        compiler_params=pltpu.CompilerParams(has_side_effects=_EFFECT),
    )(*lands, send_b, recv_b, after)
    return list(res)


def _adam_math(w, g, m, v):
    m = ADAM_B1 * m + (1.0 - ADAM_B1) * g
    v = ADAM_B2 * v + (1.0 - ADAM_B2) * (g * g)
    m_hat = m / (1.0 - ADAM_B1 ** ADAM_STEP)
    v_hat = v / (1.0 - ADAM_B2 ** ADAM_STEP)
    delta = -ADAM_LR * (m_hat / (jnp.sqrt(v_hat) + ADAM_EPS) + ADAM_WD * w)
    return delta, m, v


def _adam(name, w, m, v, parts, transposed=False, dep=None, own=None):
    npart, _, cp = parts.shape
    has_dep = dep is not None
    has_own = own is not None
    if transposed:
        c, r = w.shape
        tr = _pick(r, (256, 128))
        blk = pl.BlockSpec((c, tr), lambda i, *_: (0, i))
    else:
        r, c = w.shape
        tr = _pick(r, (256, 176, 128, 64, 16, 8, 1))
        blk = pl.BlockSpec((tr, c), lambda i, *_: (i, 0))

    def body(*refs):
        refs = list(refs)
        me = refs.pop(0)[0] if has_own else None
        w_ref, m_ref, v_ref, p_ref = refs[:4]
        own_ref = refs[4] if has_own else None
        g_ref, d_ref, mo_ref, vo_ref = refs[-4:]
        g = None
        for pp in range(npart):
            part = p_ref[pp] if not has_own else jnp.where(me == pp, own_ref[...], p_ref[pp])
            g = part.astype(F32) if g is None else g + part.astype(F32)
        g = g.T[0:c, :] if transposed else g[:, 0:c]
        delta, mn, vn = _adam_math(w_ref[...], g, m_ref[...], v_ref[...])
        g_ref[...] = g
        d_ref[...] = delta
        mo_ref[...] = mn
        vo_ref[...] = vn

    in_specs = [blk, blk, blk, pl.BlockSpec((npart, tr, cp), lambda i, *_: (0, i, 0))]
    args = [w, m, v, parts]
    if has_own:
        in_specs.append(pl.BlockSpec((None, tr, cp), lambda i, me_ref: (me_ref[0], i, 0)))
        args.append(own[0])
    if has_dep:
        in_specs.append(pl.BlockSpec(memory_space=pl.ANY))
        args.append(dep)
    out = jax.ShapeDtypeStruct(w.shape, F32)
    grid_spec = pltpu.PrefetchScalarGridSpec(num_scalar_prefetch=1 if has_own else 0, grid=(r // tr,),
                                             in_specs=in_specs, out_specs=[blk, blk, blk, blk])
    return pl.pallas_call(
        body, name=name, grid_spec=grid_spec, out_shape=[out, out, out, out], compiler_params=_params(("parallel",)),
    )(*(([own[1]] if has_own else []) + args))


def _small_update(name, packs, wmv):
    nparam = len(wmv)

    def body(p_ref, *refs):
        ins, outs = refs[:3 * nparam], refs[3 * nparam:]
        tot = p_ref[0]
        for pp in range(1, N_DEV):
            tot = tot + p_ref[pp]
        outs[0][0:8, :] = tot[0:8, :]
        outs[0][8:24, :] = tot[8:24, :] + tot[24:40, :]
        grads = [tot[i:i + 1, :] for i in range(4)] + [tot[4:5, 0:B_HEADS], tot[4:5, B_HEADS:2 * B_HEADS]]
        for i, g in enumerate(grads):
            w_ref, m_ref, v_ref = ins[3 * i:3 * i + 3]
            delta, mn, vn = _adam_math(w_ref[...], g, m_ref[...], v_ref[...])
            for o_ref, val in zip(outs[1 + 4 * i:5 + 4 * i], (g, delta, mn, vn)):
                o_ref[...] = val

    flat = [a for trio in wmv for a in trio]
    out_shape = [jax.ShapeDtypeStruct((24, D_MODEL), F32)]
    for w, _, _ in wmv:
        out_shape += [jax.ShapeDtypeStruct(w.shape, F32)] * 4
    res = pl.pallas_call(body, name=name, out_shape=out_shape, compiler_params=_params())(packs, *flat)
    return res[0], [tuple(res[1 + 4 * i:5 + 4 * i]) for i in range(nparam)]


def _local_step(x, tgt, g1, gm, g2, gf, b_forget, sinks, weights, send):
    b, s, _ = x.shape
    l = s + PREFIX
    t = b * l
    (meta,) = weights("meta", x)

    h0, n1 = _embed_norm("embed_rms1_fwd", x, meta, g1)
    (w1i,) = weights("ffn1_in", n1)
    gu1, a1 = _ffn_in_fwd("ffn1_in_fwd", n1, w1i)
    (w1o,) = weights("ffn1_out", weights("ffn1_out:forward", a1))
    h1, um = _mm_res_norm("ffn1_out_fwd", a1, w1o, h0, gm, alpha=0.5)
    wi, wa, wb, wo = weights("mix", weights("mix:forward", um))
    qkv, gates, f2 = _proj_fwd("proj_fwd", um, wi)
    qkv3 = qkv.reshape(b, l, QKV_W)
    f3 = f2.reshape(b, l, 128)
    bf_row = jnp.pad(b_forget, ((0, 0), (0, 128 - B_HEADS)))
    c_col, c_row = _fgate_fwd("fgate_fwd", f3, bf_row)
    head_of_row = jnp.arange(STACK) // BLOCK
    slopes = jnp.exp2(-8.0 * (head_of_row + 1).astype(F32) / A_HEADS).reshape(STACK, 1)
    sink_rows = jnp.repeat(sinks.reshape(A_HEADS), BLOCK).reshape(STACK, 1)
    swa_bias = _swa_bias(slopes)
    oa3, lse_a = _swa_fwd("swa_fwd", qkv3, slopes, sink_rows, swa_bias)
    ob3, lse_b = _fox_fwd("fox_fwd", qkv3, c_col, c_row)
    oa = oa3.reshape(t, A_WIDTH)
    ob = ob3.reshape(t, B_WIDTH)
    mixed, ya, yb = _branch_gate_fwd("branch_gate_fwd", oa, ob, wa, wb, gates, dep=weights("ffn2:forward", ob))
    h2, n2 = _mm_res_norm("mix_out_fwd", mixed, wo, h1, g2)
    w2i, w2o = weights("ffn2", h2)
    gu2, a2 = _ffn_in_fwd("ffn2_in_fwd", n2, w2i)

    dh3, loss_blk, dgf = _ffn_out_loss("ffn2_out_loss", a2, w2o, h2, gf, tgt, 0.5)

    def ffn_bwd(tag, dh, h_in, g_norm, n_in, gu, a, w_in_blk, w_out, one_send, examples=None):
        dw_out = _mm_tn(tag + "_out_bwd_w", a, dh, alpha=0.5)
        dgu = _ffn_out_bwd_x(tag + "_out_bwd_x", dh, w_out, gu, dep=None if one_send else send(tag + "_out", (dw_out,)))
        dw_in = _ffn_in_bwd_w(tag + "_in_bwd_w", n_in, dgu)
        token = send(tag, (dw_in, dw_out)) if one_send else send(tag + "_in", (dw_in,))
        return _ffn_in_bwd_x(tag + "_in_bwd_x", dgu, w_in_blk, h_in, g_norm, dh, dep=token, examples=examples)

    dh2, dg2 = ffn_bwd("ffn2", dh3, h2, g2, n2, gu2, a2, w2i, w2o, True)

    dwo = _mm_tn("mix_out_bwd_w", mixed, dh2)
    dya, dyb, dgates = _mix_out_gate_bwd("mix_out_gate_bwd", dh2, wo, gates, ya, yb)
    doa, dob = _branch_bwd_x("branch_bwd_x", dya, dyb, wa, wb)
    dwa, dwb = _branch_bwd_w("branch_bwd_w", oa, ob, dya, dyb)
    dqkv3, dcq, dck = _fox_bwd("fox_bwd", qkv3, c_col, c_row, ob3, lse_b, dob.reshape(b, l, B_WIDTH))
    dqkv3, dsink = _swa_bwd("swa_bwd", qkv3, oa3, lse_a, doa.reshape(b, l, A_WIDTH), slopes, sink_rows, swa_bias, dqkv3)
    dqkv = dqkv3.reshape(t, QKV_W)
    df3, dbf = _fgate_bwd("fgate_bwd", f3, bf_row, dcq, dck)
    df = df3.reshape(t, 128)
    dwi_qkv = _mm_tn("proj_qkv_bwd_w", um, dqkv, tm_c=(512,), tn_c=(768,))
    dwi_g = _mm_tn("proj_gates_bwd_w", um, dgates, tm_c=(512,), tn_c=(512,))
    dwi_f = _mm_tn("proj_f_bwd_w", um, df, tm_c=(512,), tn_c=(128,))
    token = send("mix", (dwi_qkv, dwi_g, dwi_f, dwa, dwb, dwo))
    dh1, dgm = _proj_bwd_x("proj_bwd_x", dqkv, dgates, df, wi, h1, gm, dh2, dep=token)

    grad_x, dmeta3, dg1 = ffn_bwd("ffn1", dh1, h0, g1, n1, gu1, a1, w1i, w1o, False, examples=b)
    dmeta = dmeta3.reshape(b * N_META, D_MODEL)

    misc = jnp.concatenate([dbf[:, 0:B_HEADS], dsink[:, 0].reshape(1, A_HEADS), loss_blk[0:1, 0:1]], axis=1)
    misc = jnp.pad(misc, ((0, 0), (0, D_MODEL - misc.shape[1])))
    row = lax.broadcasted_iota(jnp.int32, (8, D_MODEL), 0)
    vec = jnp.zeros((8, D_MODEL), F32)
    for i, piece in enumerate((dg1, dgm, dg2, dgf, misc)):
        vec = jnp.where(row == i, piece, vec)
    small = jnp.concatenate([vec, dmeta], axis=0)
    return grad_x, small


def _pad_to(a, rows, cols):
    return jnp.pad(a, ((0, rows - a.shape[0]), (0, cols - a.shape[1])))


def _ffn_out_from_gathered(name, g):
    def body(g_ref, o_ref):
        o_ref[0:FFO_SHARD, :] = g_ref[0]
        o_ref[FFO_SHARD:FF_SHARD, :] = g_ref[1]
        o_ref[FF_SHARD:FF_SHARD_P, :] = jnp.zeros((FF_SHARD_P - FF_SHARD, D_MODEL), BF16)

    return pl.pallas_call(
        body, name=name, grid=(4,),
        in_specs=[pl.BlockSpec((2, FFO_SHARD, D_MODEL), lambda j: (j, 0, 0))],
        out_specs=pl.BlockSpec((FF_SHARD_P, D_MODEL), lambda j: (j, 0)),
        out_shape=jax.ShapeDtypeStruct((D_FF_P, D_MODEL), BF16), compiler_params=_params(("parallel",)),
    )(g)


def _ffn_out_to_scatter(name, dw):
    def body(x_ref, o_ref):
        o_ref[0] = x_ref[0:FFO_SHARD, :]
        o_ref[1] = x_ref[FFO_SHARD:FF_SHARD, :]

    return pl.pallas_call(
        body, name=name, grid=(4,),
        in_specs=[pl.BlockSpec((FF_SHARD_P, D_MODEL), lambda j: (j, 0))],
        out_specs=pl.BlockSpec((2, FFO_SHARD, D_MODEL), lambda j: (j, 0, 0)),
        out_shape=jax.ShapeDtypeStruct((N_DEV, FFO_SHARD, D_MODEL), dw.dtype), compiler_params=_params(("parallel",)),
    )(dw)


def _proj_segments():
    segs = [(HEAD_DIM * h, HEAD_DIM, 0, B_SEG + HEAD_DIM * A_HEAD_ORDER.index(h)) for h in range(A_HEADS)]
    segs += [(512, 128, 0, B_SEG + A_WIDTH), (640, 128, 0, B_SEG + A_WIDTH + 128)]
    for first, off in ((768, 0), (1280, 128), (1792, 256)):
        segs += [(first + 128 * hp, 128, 0, PAIR_W * hp + off) for hp in range(4)]
    segs += [(2304, B_HEADS, 2, 0), (2312, 2 * D_MODEL, 1, 0)]
    return segs


_RELAYOUT_ROWS = 256


def _proj_from_gathered(name, g):
    rows = _RELAYOUT_ROWS

    def body(g_ref, o_ref):
        def cols(first, width):
            out = []
            for p in range(N_DEV):
                lo, hi = max(first, WIN_SHARD * p), min(first + width, WIN_SHARD * (p + 1))
                if lo < hi:
                    out.append(g_ref[p, :, lo - WIN_SHARD * p:hi - WIN_SHARD * p])
            return out

        parts = []
        for arr in (0, 1, 2):
            for first, width, _, _ in sorted((s for s in _proj_segments() if s[2] == arr), key=lambda s: s[3]):
                parts += cols(first, width)
            if arr == 0:
                parts.append(jnp.zeros((rows, P_GATES - QKV_W), BF16))
        parts.append(jnp.zeros((rows, 128 - B_HEADS), BF16))
        o_ref[...] = jnp.concatenate(parts, axis=1)

    return pl.pallas_call(
        body, name=name, grid=(D_MODEL // rows,),
        in_specs=[pl.BlockSpec((N_DEV, rows, WIN_SHARD_P), lambda i: (0, i, 0))],
        out_specs=pl.BlockSpec((rows, PROJ_P), lambda i: (i, 0)),
        out_shape=jax.ShapeDtypeStruct((D_MODEL, PROJ_P), BF16), compiler_params=_params(("parallel",)),
    )(g)


def _proj_to_scatter(name, dqkv_w, dg_w, df_w):
    rows = _RELAYOUT_ROWS
    segs = sorted(_proj_segments())

    def body(q_ref, g_ref, f_ref, o_ref):
        arrays = (q_ref, g_ref, f_ref)
        for p in range(N_DEV):
            parts = []
            for first, width, arr, at in segs:
                lo, hi = max(first, WIN_SHARD * p), min(first + width, WIN_SHARD * (p + 1))
                if lo < hi:
                    parts.append(arrays[arr][:, at + lo - first:at + hi - first])
            parts.append(jnp.zeros((rows, WIN_SHARD_P - WIN_SHARD), BF16))
            o_ref[p] = jnp.concatenate(parts, axis=1)

    return pl.pallas_call(
        body, name=name, grid=(D_MODEL // rows,),
        in_specs=[pl.BlockSpec((rows, QKV_W), lambda i: (i, 0)), pl.BlockSpec((rows, 2 * D_MODEL), lambda i: (i, 0)),
                  pl.BlockSpec((rows, 128), lambda i: (i, 0))],
        out_specs=pl.BlockSpec((N_DEV, rows, WIN_SHARD_P), lambda i: (0, i, 0)),
        out_shape=jax.ShapeDtypeStruct((N_DEV, D_MODEL, WIN_SHARD_P), BF16), compiler_params=_params(("parallel",)),
    )(dqkv_w, dg_w, df_w)


def _a_rows_from_natural(w):
    return jnp.concatenate([w[HEAD_DIM * h:HEAD_DIM * (h + 1)] for h in A_HEAD_ORDER], axis=0)


def _a_rows_to_natural(w):
    return jnp.concatenate([w[HEAD_DIM * A_HEAD_ORDER.index(h):HEAD_DIM * (A_HEAD_ORDER.index(h) + 1)]
                            for h in range(A_HEADS)], axis=0)


def kernel(x, meta_tokens, ffn1_norm, ffn1_w_in, ffn1_w_out, mix_norm, w_in, b_forget, attn_sinks, w_branch_a, w_branch_b, w_out, ffn2_norm, ffn2_w_in, ffn2_w_out, final_norm, loss_target, m_meta_tokens, m_ffn1_norm, m_ffn1_w_in, m_ffn1_w_out, m_mix_norm, m_w_in, m_b_forget, m_attn_sinks, m_w_branch_a, m_w_branch_b, m_w_out, m_ffn2_norm, m_ffn2_w_in, m_ffn2_w_out, m_final_norm, v_meta_tokens, v_ffn1_norm, v_ffn1_w_in, v_ffn1_w_out, v_mix_norm, v_w_in, v_b_forget, v_attn_sinks, v_w_branch_a, v_w_branch_b, v_w_out, v_ffn2_norm, v_ffn2_w_in, v_ffn2_w_out, v_final_norm):
    me = 4 * lax.axis_index("x") + 2 * lax.axis_index("y") + lax.axis_index("c")

    def shard(w, tok, rows=None, cols=None):
        piece = (w[0] if tok is None else w[0] + tok[0, 0]).astype(BF16)
        return piece if rows is None else _pad_to(piece, rows, cols)

    first_level, second_level = {}, {}
    first_level["meta"], tok = _gather2_start("gather_meta_start", (meta_tokens,))
    first_level["ffn1_in"], tok = _gather2_start(
        "gather_ffn1_in_start", (shard(ffn1_w_in, None, D_MODEL, FF_SHARD_P),), after=tok)
    first_level["ffn1_out"], tok = _gather2_start("gather_ffn1_out_start", (shard(ffn1_w_out, tok),), after=tok)
    first_level["mix"], tok = _gather2_start(
        "gather_mix_start", (shard(w_in, tok, D_MODEL, WIN_SHARD_P), shard(w_branch_a, tok), shard(w_branch_b, tok),
                             shard(w_out, tok)), after=tok)
    first_level["ffn2"], tok = _gather2_start(
        "gather_ffn2_start", (shard(ffn2_w_in, tok, D_MODEL, FF_SHARD_P), shard(ffn2_w_out, tok)), after=tok)
    started = {"tok": tok}

    def weights(group, after):
        if group.endswith(":forward"):
            group = group[:-len(":forward")]
            second_level[group], token = _gather2_forward("gather_" + group + "_forward", first_level[group], after)
            return token
        if group == "meta":
            after = weights("meta:forward", started["tok"])
        if group == "ffn1_in":
            after = weights("ffn1_in:forward", after)
        got = _gather2_wait("gather_" + group + "_wait", second_level[group], after)
        if group == "mix":
            gwi, gwa, gwb, gwo = got
            return (_proj_from_gathered("proj_w_relayout", gwi), _a_rows_from_natural(gwa.transpose(1, 0, 2).reshape(A_WIDTH, D_MODEL)),
                    gwb.transpose(1, 0, 2).reshape(B_WIDTH, D_MODEL), gwo.reshape(D_MODEL, D_MODEL))
        if group == "ffn1_out":
            return (_ffn_out_from_gathered("ffn1_w_out_relayout", got[0]),)
        if group == "meta":
            return (got[0].transpose(1, 0, 2).reshape(N_META, D_MODEL),)
        if group == "ffn1_in":
            return (got[0].reshape(2, 4, D_MODEL, FF_SHARD_P),)
        return got[0].reshape(2, 4, D_MODEL, FF_SHARD_P), _ffn_out_from_gathered("ffn2_w_out_relayout", got[1])

    scatter_state = {}

    def send(group, grads):
        if group == "mix":
            dwi_qkv, dwi_g, dwi_f, dwa, dwb, dwo = grads
            dwa = _a_rows_to_natural(dwa)
            blocks = (_proj_to_scatter("proj_dw_relayout", dwi_qkv, dwi_g, dwi_f), dwa.reshape(A_WIDTH, N_DEV, 128).transpose(1, 0, 2),
                      dwb.reshape(B_WIDTH, N_DEV, 128).transpose(1, 0, 2), dwo.reshape(N_DEV, 128, D_MODEL))
        elif group.endswith("_in"):
            blocks = (grads[0].reshape(N_DEV, D_MODEL, FF_SHARD_P),)
        elif group.endswith("_out"):
            blocks = (_ffn_out_to_scatter(group + "_dw_relayout", grads[0]),)
        else:
            blocks = (grads[0].reshape(N_DEV, D_MODEL, FF_SHARD_P), _ffn_out_to_scatter(group + "_dw_out_relayout", grads[1]))
        scatter_state[group], token = _xchg_start("scatter_" + group + "_start", (), blocks)
        return token

    gf = final_norm.reshape(1, D_MODEL)
    grad_x, small = _local_step(x, loss_target, ffn1_norm, mix_norm, ffn2_norm, gf, b_forget, attn_sinks, weights, send)

    small_state, after = _xchg_start("gather_small_start", (small,), ())
    out = {}
    updates = (
        ("ffn2", (("ffn2_w_in", ffn2_w_in, m_ffn2_w_in, v_ffn2_w_in), ("ffn2_w_out", ffn2_w_out, m_ffn2_w_out, v_ffn2_w_out))),
        ("ffn1_out", (("ffn1_w_out", ffn1_w_out, m_ffn1_w_out, v_ffn1_w_out),)),
        ("mix", (("w_in", w_in, m_w_in, v_w_in), ("w_branch_a", w_branch_a, m_w_branch_a, v_w_branch_a),
                 ("w_branch_b", w_branch_b, m_w_branch_b, v_w_branch_b), ("w_out", w_out, m_w_out, v_w_out))),
    )
    last_update = ("ffn1_in", (("ffn1_w_in", ffn1_w_in, m_ffn1_w_in, v_ffn1_w_in),))

    def update(group, members, after):
        parts_list, blocks_list = _xchg_wait("scatter_" + group + "_wait", scatter_state[group], after)
        prev = None
        for (nm, w, m, v), parts, blocks in zip(members, parts_list, blocks_list):
            if nm.endswith("w_in"):
                res4 = _adam("adam_" + nm, w[0].T, m[0].T, v[0].T, parts, transposed=True, dep=prev, own=(blocks, me_arr))
                out[nm] = tuple(r.T[None] for r in res4)
            else:
                res4 = _adam("adam_" + nm, w[0], m[0], v[0], parts, dep=prev, own=(blocks, me_arr))
                out[nm] = tuple(r[None] for r in res4)
            prev = res4[0]
        return prev

    me_arr = me.reshape(1).astype(jnp.int32)
    for group, members in updates + (last_update,):
        after = update(group, members, after)
    (packs,), _ = _xchg_wait("gather_small_wait", small_state, after)

    row = lambda a: a.reshape(1, D_MODEL)
    tot, small_res = _small_update("small_update", packs, (
        (ffn1_norm, m_ffn1_norm, v_ffn1_norm), (mix_norm, m_mix_norm, v_mix_norm), (ffn2_norm, m_ffn2_norm, v_ffn2_norm),
        (row(final_norm), row(m_final_norm), row(v_final_norm)), (b_forget, m_b_forget, v_b_forget),
        (attn_sinks, m_attn_sinks, v_attn_sinks)))
    for nm, res4 in zip(("ffn1_norm", "mix_norm", "ffn2_norm", "final_norm", "b_forget", "attn_sinks"), small_res):
        out[nm] = tuple(r.reshape(D_MODEL) for r in res4) if nm == "final_norm" else res4
    loss = tot[4, 2 * B_HEADS]
    g_meta = lax.dynamic_slice(tot[8:24, :], (0, me * 128), (N_META, 128))
    out["meta_tokens"] = tuple(_adam("adam_meta_tokens", meta_tokens, m_meta_tokens, v_meta_tokens, g_meta[None]))

    names = ("meta_tokens", "ffn1_norm", "ffn1_w_in", "ffn1_w_out", "mix_norm", "w_in", "b_forget", "attn_sinks",
             "w_branch_a", "w_branch_b", "w_out", "ffn2_norm", "ffn2_w_in", "ffn2_w_out", "final_norm")
    return (loss, grad_x) + tuple(out[nm][kind] for kind in range(4) for nm in names)
```

```python
import jax
import jax.numpy as jnp
from jax import lax
from jax.experimental import pallas as pl
from jax.experimental.pallas import tpu as pltpu

F32 = jnp.float32
BF16 = jnp.bfloat16

D_MODEL = 1024
N_META = 16
BLOCK = 128
PREFIX = 128
N_PAD = PREFIX - N_META
HEAD_DIM = 64
A_HEADS = 8
B_HEADS = 8
A_WIDTH = 512
A_KV_WIDTH = 128
B_WIDTH = 512
D_FF = 2816
N_DEV = 8
FF_SHARD = 2 * D_FF // N_DEV
FF_SHARD_P = 768
FFO_SHARD = D_FF // N_DEV
D_FF_P = 4 * FF_SHARD_P
W_IN_COLS = 4360
WIN_SHARD = W_IN_COLS // N_DEV
WIN_SHARD_P = 640
PAIR_W = 3 * 128
B_SEG = 4 * PAIR_W
A_SEG = A_WIDTH + 2 * A_KV_WIDTH
QKV_W = B_SEG + A_SEG
P_GATES = 2 * (2 * D_MODEL)
P_F = P_GATES + 2 * D_MODEL
PROJ_P = P_F + 128
A_HEAD_ORDER = (0, 4, 1, 5, 2, 6, 3, 7)
EPS = 1e-6
NEG = -1e30
SCALE = HEAD_DIM ** -0.5
ADAM_LR = 0.001
ADAM_B1 = 0.9
ADAM_B2 = 0.999
ADAM_EPS = 1e-08
ADAM_WD = 0.01
ADAM_STEP = 10
VMEM_LIMIT = 56 * 1024 * 1024
MESH_ID = pl.DeviceIdType.MESH
SMALL_ROWS = 40

_NN = (((1,), (0,)), ((), ()))
_NT = (((1,), (1,)), ((), ()))
_TN = (((0,), (0,)), ((), ()))


def _params(sem=None):
    return pltpu.CompilerParams(dimension_semantics=sem, vmem_limit_bytes=VMEM_LIMIT)


def _pick(n, cands):
    for c in cands:
        if n % c == 0:
            return c
    raise ValueError(f"no tile for {n}")


def _bf(v):
    return v if v.dtype == BF16 else v.astype(BF16)


def _mm(name, a, b, dims, grid, a_spec, b_spec, o_spec, out_shape, out_dtype, alpha=1.0):
    def body(a_ref, b_ref, o_ref):
        acc = lax.dot_general(_bf(a_ref[...]), _bf(b_ref[...]), dims, preferred_element_type=F32)
        if alpha != 1.0:
            acc = acc * alpha
        o_ref[...] = acc.astype(o_ref.dtype)

    return pl.pallas_call(
        body, name=name, grid=grid, in_specs=[a_spec, b_spec], out_specs=o_spec,
        out_shape=jax.ShapeDtypeStruct(out_shape, out_dtype),
        compiler_params=_params(("parallel",) * len(grid)),
    )(a, b)


def _mm_res_norm(name, a, w, res, g_next, alpha=1.0):
    t, k = a.shape
    tm = _pick(t, (544, 384, 256, 128))

    def body(a_ref, w_ref, r_ref, g_ref, h_ref, n_ref):
        acc = lax.dot_general(_bf(a_ref[...]), w_ref[...], _NN, preferred_element_type=F32)
        if alpha != 1.0:
            acc = acc * alpha
        hv = acc + r_ref[...]
        h_ref[...] = hv
        r = lax.rsqrt(jnp.mean(hv * hv, axis=-1, keepdims=True) + EPS)
        n_ref[...] = ((hv * r) * g_ref[...]).astype(BF16)

    row = pl.BlockSpec((tm, D_MODEL), lambda i: (i, 0))
    return pl.pallas_call(
        body, name=name, grid=(t // tm,),
        in_specs=[pl.BlockSpec((tm, k), lambda i: (i, 0)), pl.BlockSpec((k, D_MODEL), lambda i: (0, 0)), row,
                  pl.BlockSpec((1, D_MODEL), lambda i: (0, 0))],
        out_specs=[row, row],
        out_shape=[jax.ShapeDtypeStruct((t, D_MODEL), F32), jax.ShapeDtypeStruct((t, D_MODEL), BF16)],
        compiler_params=_params(("parallel",)),
    )(a, w, res, g_next)


def _mm_tn(name, a, b, out_dtype=BF16, alpha=1.0, tm_c=(768, 512, 256, 128), tn_c=(512, 640, 256, 128)):
    t, m = a.shape
    n = b.shape[1]
    tm = _pick(m, tm_c)
    tn = _pick(n, tn_c)
    bytes_a, bytes_b = a.size * a.dtype.itemsize, b.size * b.dtype.itemsize
    if bytes_a + bytes_b * (m // tm) <= bytes_b + bytes_a * (n // tn):
        return _mm(name, a, b, _TN, (m // tm, n // tn),
                   pl.BlockSpec((t, tm), lambda i, j: (0, i)), pl.BlockSpec((t, tn), lambda i, j: (0, j)),
                   pl.BlockSpec((tm, tn), lambda i, j: (i, j)), (m, n), out_dtype, alpha=alpha)
    return _mm(name, a, b, _TN, (n // tn, m // tm),
               pl.BlockSpec((t, tm), lambda j, i: (0, i)), pl.BlockSpec((t, tn), lambda j, i: (0, j)),
               pl.BlockSpec((tm, tn), lambda j, i: (i, j)), (m, n), out_dtype, alpha=alpha)


def _ffn_in_fwd(name, n, wblk, dep=None):
    t = n.shape[0]
    tm = _pick(t, (1088, 768, 512, 256, 128))
    has_dep = dep is not None

    def body(n_ref, w_ref, *rest):
        gu_ref, a_ref = rest[-2], rest[-1]
        nv = n_ref[...]
        g = lax.dot_general(nv, w_ref[0], _NN, preferred_element_type=F32)
        u = lax.dot_general(nv, w_ref[1], _NN, preferred_element_type=F32)
        sg = jax.nn.sigmoid(g)
        silu = g * sg
        a_ref[...] = (silu * u).astype(BF16)
        gu_ref[0] = ((0.5 * u) * (sg + silu * (1.0 - sg))).astype(BF16)
        gu_ref[1] = (0.5 * silu).astype(BF16)

    return pl.pallas_call(
        body, name=name, grid=(t // tm, 4),
        in_specs=[pl.BlockSpec((tm, D_MODEL), lambda i, j: (i, 0)),
                  pl.BlockSpec((2, None, D_MODEL, FF_SHARD_P), lambda i, j: (0, j, 0, 0))]
        + ([pl.BlockSpec(memory_space=pl.ANY)] if has_dep else []),
        out_specs=[pl.BlockSpec((2, tm, FF_SHARD_P), lambda i, j: (0, i, j)),
                   pl.BlockSpec((tm, FF_SHARD_P), lambda i, j: (i, j))],
        out_shape=[jax.ShapeDtypeStruct((2, t, D_FF_P), BF16), jax.ShapeDtypeStruct((t, D_FF_P), BF16)],
        compiler_params=_params(("parallel", "parallel")),
    )(*((n, wblk) + ((dep,) if has_dep else ())))


def _ffn_out_bwd_x(name, dh, w_out, gu, dep=None):
    t = dh.shape[0]
    tm = _pick(t, (1088, 768, 512, 256, 128))
    has_dep = dep is not None

    def body(dh_ref, w_ref, gu_ref, *rest):
        o_ref = rest[-1]
        da = lax.dot_general(_bf(dh_ref[...]), w_ref[...], _NT, preferred_element_type=F32)
        o_ref[0] = (da * gu_ref[0].astype(F32)).astype(BF16)
        o_ref[1] = (da * gu_ref[1].astype(F32)).astype(BF16)

    gu_spec = pl.BlockSpec((2, tm, FF_SHARD_P), lambda i, j: (0, i, j))
    return pl.pallas_call(
        body, name=name, grid=(t // tm, 4),
        in_specs=[pl.BlockSpec((tm, D_MODEL), lambda i, j: (i, 0)), pl.BlockSpec((FF_SHARD_P, D_MODEL), lambda i, j: (j, 0)),
                  gu_spec] + ([pl.BlockSpec(memory_space=pl.ANY)] if has_dep else []),
        out_specs=gu_spec, out_shape=jax.ShapeDtypeStruct((2, t, D_FF_P), BF16),
        compiler_params=_params(("parallel", "parallel")),
    )(*((dh, w_out, gu) + ((dep,) if has_dep else ())))


def _rms_bwd_rows(dn, h, g, dres):
    r = lax.rsqrt(jnp.mean(h * h, axis=-1, keepdims=True) + EPS)
    tv = dn * g
    dot = jnp.mean(tv * h, axis=-1, keepdims=True)
    return dres + (r * tv - h * (r * r * r * dot)), jnp.sum(dn * (h * r), axis=0, keepdims=True)


def _accumulate_rows(ref, part, first):
    @pl.when(first)
    def _():
        ref[...] = part

    @pl.when(jnp.logical_not(first))
    def _():
        ref[...] += part


def _ffn_in_bwd_x(name, dgu, wblk, h_in, g_norm, dres, dep=None, examples=None):
    t = dgu.shape[1]
    tm = _pick(t, (544, 384, 256, 128))
    has_dep = dep is not None
    split = examples is not None
    if split:
        l = t // examples
        per = l // tm
        assert per * tm == l and tm > PREFIX

    def body(d_ref, w_ref, h_ref, g_ref, r_ref, *rest):
        acc = None
        for s in range(2):
            for j in range(4):
                part = lax.dot_general(d_ref[s, :, FF_SHARD_P * j:FF_SHARD_P * (j + 1)], w_ref[s, j], _NT,
                                       preferred_element_type=F32)
                acc = part if acc is None else acc + part
        dh, dg = _rms_bwd_rows(acc, h_ref[...], g_ref[...], r_ref[...])
        i = pl.program_id(0)
        if not split:
            dh_ref, dg_ref = rest[-2], rest[-1]
            dh_ref[...] = dh
        else:
            gx_ref, meta_ref, dg_ref, buf, sem = rest[-5:]

            def out_copy(step):
                bi, r = step // per, step % per
                head = pltpu.make_async_copy(buf.at[pl.ds(PREFIX, tm - PREFIX)], gx_ref.at[bi, pl.ds(0, tm - PREFIX)], sem)
                if per == 1:
                    return r == 0, head, None
                return r == 0, head, pltpu.make_async_copy(
                    buf, gx_ref.at[bi, pl.ds(pl.multiple_of(jnp.maximum(r, 1) * tm - PREFIX, 8), tm)], sem)

            def run(step, method):
                is_head, head, later = out_copy(step)

                @pl.when(is_head)
                def _():
                    getattr(head, method)()

                if later is not None:
                    @pl.when(jnp.logical_not(is_head))
                    def _():
                        getattr(later, method)()

            @pl.when(i > 0)
            def _():
                run(i - 1, "wait")

            buf[...] = dh

            @pl.when(i % per == 0)
            def _():
                meta_ref[...] = dh[N_PAD:PREFIX]

            run(i, "start")

            @pl.when(i == pl.num_programs(0) - 1)
            def _():
                run(i, "wait")

        _accumulate_rows(dg_ref, dg, i == 0)

    row = pl.BlockSpec((tm, D_MODEL), lambda i: (i, 0))
    vec = pl.BlockSpec((1, D_MODEL), lambda i: (0, 0))
    if split:
        out_specs = [pl.BlockSpec(memory_space=pl.ANY), pl.BlockSpec((None, N_META, D_MODEL), lambda i: (i // per, 0, 0)), vec]
        out_shape = [jax.ShapeDtypeStruct((examples, l - PREFIX, D_MODEL), F32),
                     jax.ShapeDtypeStruct((examples, N_META, D_MODEL), F32), jax.ShapeDtypeStruct((1, D_MODEL), F32)]
        scratch = [pltpu.VMEM((tm, D_MODEL), F32), pltpu.SemaphoreType.DMA(())]
    else:
        out_specs = [row, vec]
        out_shape = [jax.ShapeDtypeStruct((t, D_MODEL), F32), jax.ShapeDtypeStruct((1, D_MODEL), F32)]
        scratch = []
    return pl.pallas_call(
        body, name=name, grid=(t // tm,),
        in_specs=[pl.BlockSpec((2, tm, D_FF_P), lambda i: (0, i, 0)),
                  pl.BlockSpec((2, 4, D_MODEL, FF_SHARD_P), lambda i: (0, 0, 0, 0), pipeline_mode=pl.Buffered(1)),
                  row, vec, row]
        + ([pl.BlockSpec(memory_space=pl.ANY)] if has_dep else []),
        out_specs=out_specs, out_shape=out_shape, scratch_shapes=scratch,
        compiler_params=_params(("arbitrary",)),
    )(*((dgu, wblk, h_in, g_norm, dres) + ((dep,) if has_dep else ())))


def _proj_fwd(name, um, wi):
    t = um.shape[0]
    tm = _pick(t, (544, 384, 256, 128))

    def body(u_ref, w_ref, q_ref, g_ref, f_ref):
        uv = u_ref[...]
        q_ref[...] = lax.dot_general(uv, w_ref[:, 0:QKV_W], _NN, preferred_element_type=F32).astype(BF16)
        g_ref[...] = lax.dot_general(uv, w_ref[:, P_GATES:P_F], _NN, preferred_element_type=F32)
        f_ref[...] = lax.dot_general(uv, w_ref[:, P_F:PROJ_P], _NN, preferred_element_type=F32)

    return pl.pallas_call(
        body, name=name, grid=(t // tm,),
        in_specs=[pl.BlockSpec((tm, D_MODEL), lambda i: (i, 0)),
                  pl.BlockSpec((D_MODEL, PROJ_P), lambda i: (0, 0), pipeline_mode=pl.Buffered(1))],
        out_specs=[pl.BlockSpec((tm, QKV_W), lambda i: (i, 0)), pl.BlockSpec((tm, 2 * D_MODEL), lambda i: (i, 0)),
                   pl.BlockSpec((tm, 128), lambda i: (i, 0))],
        out_shape=[jax.ShapeDtypeStruct((t, QKV_W), BF16), jax.ShapeDtypeStruct((t, 2 * D_MODEL), F32),
                   jax.ShapeDtypeStruct((t, 128), F32)],
        compiler_params=_params(("parallel",)),
    )(um, wi)


def _proj_bwd_x(name, dqkv, dgates, df, wi, h_in, g_norm, dres, dep=None):
    t = dqkv.shape[0]
    tm = _pick(t, (544, 384, 256, 128))
    has_dep = dep is not None

    def body(q_ref, gt_ref, f_ref, w_ref, h_ref, g_ref, r_ref, *rest):
        dh_ref, dg_ref = rest[-2], rest[-1]
        acc = lax.dot_general(q_ref[...], w_ref[:, 0:QKV_W], _NT, preferred_element_type=F32)
        acc = acc + lax.dot_general(gt_ref[...], w_ref[:, P_GATES:P_F], _NT, preferred_element_type=F32)
        acc = acc + lax.dot_general(f_ref[...], w_ref[:, P_F:PROJ_P], _NT, preferred_element_type=F32)
        dh, dg = _rms_bwd_rows(acc, h_ref[...], g_ref[...], r_ref[...])
        dh_ref[...] = dh
        _accumulate_rows(dg_ref, dg, pl.program_id(0) == 0)

    row = pl.BlockSpec((tm, D_MODEL), lambda i: (i, 0))
    vec = pl.BlockSpec((1, D_MODEL), lambda i: (0, 0))
    return pl.pallas_call(
        body, name=name, grid=(t // tm,),
        in_specs=[pl.BlockSpec((tm, QKV_W), lambda i: (i, 0)), pl.BlockSpec((tm, 2 * D_MODEL), lambda i: (i, 0)),
                  pl.BlockSpec((tm, 128), lambda i: (i, 0)),
                  pl.BlockSpec((D_MODEL, PROJ_P), lambda i: (0, 0), pipeline_mode=pl.Buffered(1)), row, vec, row]
        + ([pl.BlockSpec(memory_space=pl.ANY)] if has_dep else []),
        out_specs=[row, vec],
        out_shape=[jax.ShapeDtypeStruct((t, D_MODEL), F32), jax.ShapeDtypeStruct((1, D_MODEL), F32)],
        compiler_params=_params(("arbitrary",)),
    )(*((dqkv, dgates, df, wi, h_in, g_norm, dres) + ((dep,) if has_dep else ())))


def _ffn_in_bwd_w(name, n, dgu):
    t = n.shape[0]
    return _mm(name, n, dgu, _TN, (2, 4),
               pl.BlockSpec((t, D_MODEL), lambda s, j: (0, 0)),
               pl.BlockSpec((None, t, FF_SHARD_P), lambda s, j: (s, 0, j)),
               pl.BlockSpec((None, None, D_MODEL, FF_SHARD_P), lambda s, j: (s, j, 0, 0)),
               (2, 4, D_MODEL, FF_SHARD_P), BF16)


def _embed_norm(name, x, meta, g):
    b, s, _ = x.shape
    half = (s + PREFIX) // 2
    first = half - PREFIX
    assert first > 0 and half % 16 == 0

    def body(x_ref, m_ref, g_ref, h_ref, n_ref, buf, sem):
        bi, k = pl.program_id(0), pl.program_id(1)

        def tokens(example, second, method):
            if second:
                cp = pltpu.make_async_copy(x_ref.at[example, pl.ds(first, half)], buf.at[1], sem.at[1])
            else:
                cp = pltpu.make_async_copy(x_ref.at[example, pl.ds(0, first)], buf.at[0, pl.ds(PREFIX, first)], sem.at[0])
            getattr(cp, method)()

        @pl.when((bi == 0) & (k == 0))
        def _():
            tokens(0, False, "start")

        @pl.when(k == 0)
        def _():
            tokens(bi, True, "start")
            tokens(bi, False, "wait")
            buf[0, 0:N_PAD, :] = jnp.zeros((N_PAD, D_MODEL), F32)
            buf[0, N_PAD:PREFIX, :] = m_ref[...]

        @pl.when(k == 1)
        def _():
            @pl.when(bi + 1 < b)
            def _():
                tokens(bi + 1, False, "start")

            tokens(bi, True, "wait")

        hv = buf[k]
        h_ref[...] = hv
        r = lax.rsqrt(jnp.mean(hv * hv, axis=-1, keepdims=True) + EPS)
        n_ref[...] = ((hv * r) * g_ref[...]).astype(BF16)

    rows = pl.BlockSpec((half, D_MODEL), lambda bi, k: (2 * bi + k, 0))
    return pl.pallas_call(
        body, name=name, grid=(b, 2),
        in_specs=[pl.BlockSpec(memory_space=pl.ANY), pl.BlockSpec((N_META, D_MODEL), lambda bi, k: (0, 0)),
                  pl.BlockSpec((1, D_MODEL), lambda bi, k: (0, 0))],
        out_specs=[rows, rows],
        out_shape=[jax.ShapeDtypeStruct((2 * b * half, D_MODEL), F32), jax.ShapeDtypeStruct((2 * b * half, D_MODEL), BF16)],
        scratch_shapes=[pltpu.VMEM((2, half, D_MODEL), F32), pltpu.SemaphoreType.DMA((2,))],
        compiler_params=_params(("arbitrary", "arbitrary")),
    )(x, meta, g)


def _branch_gate_fwd(name, oa, ob, wa, wb, gates, dep=None):
    t = gates.shape[0]
    tm = _pick(t, (544, 384, 256, 128))
    has_dep = dep is not None

    def body(oa_ref, ob_ref, wa_ref, wb_ref, g_ref, *rest):
        o_ref, ya_ref, yb_ref = rest[-3:]
        ya = lax.dot_general(_bf(oa_ref[...]), wa_ref[...], _NN, preferred_element_type=F32)
        yb = lax.dot_general(_bf(ob_ref[...]), wb_ref[...], _NN, preferred_element_type=F32)
        sa = jax.nn.sigmoid(g_ref[:, 0:D_MODEL])
        sb = jax.nn.sigmoid(g_ref[:, D_MODEL:2 * D_MODEL])
        o_ref[...] = (sa * ya + sb * yb).astype(BF16)
        ya_ref[...] = ya.astype(BF16)
        yb_ref[...] = yb.astype(BF16)

    blk = pl.BlockSpec((tm, D_MODEL), lambda i: (i, 0))
    narrow = pl.BlockSpec((tm, A_WIDTH), lambda i: (i, 0))
    wide = pl.BlockSpec((tm, 2 * D_MODEL), lambda i: (i, 0))
    wspec = pl.BlockSpec((A_WIDTH, D_MODEL), lambda i: (0, 0))
    out = jax.ShapeDtypeStruct((t, D_MODEL), BF16)
    return pl.pallas_call(
        body, name=name, grid=(t // tm,),
        in_specs=[narrow, narrow, wspec, wspec, wide] + ([pl.BlockSpec(memory_space=pl.ANY)] if has_dep else []),
        out_specs=[blk, blk, blk], out_shape=[out, out, out], compiler_params=_params(("parallel",)),
    )(*((oa, ob, wa, wb, gates) + ((dep,) if has_dep else ())))


def _branch_bwd_x(name, dya, dyb, wa, wb):
    t = dya.shape[0]
    tm = _pick(t, (1088, 768, 512, 256, 128))

    def body(da_ref, db_ref, wa_ref, wb_ref, oa_ref, ob_ref):
        oa_ref[...] = lax.dot_general(da_ref[...], wa_ref[...], _NT, preferred_element_type=F32)
        ob_ref[...] = lax.dot_general(db_ref[...], wb_ref[...], _NT, preferred_element_type=F32).astype(BF16)

    blk = pl.BlockSpec((tm, D_MODEL), lambda i: (i, 0))
    narrow = pl.BlockSpec((tm, A_WIDTH), lambda i: (i, 0))
    wspec = pl.BlockSpec((A_WIDTH, D_MODEL), lambda i: (0, 0), pipeline_mode=pl.Buffered(1))
    return pl.pallas_call(
        body, name=name, grid=(t // tm,), in_specs=[blk, blk, wspec, wspec], out_specs=[narrow, narrow],
        out_shape=[jax.ShapeDtypeStruct((t, A_WIDTH), F32), jax.ShapeDtypeStruct((t, B_WIDTH), BF16)],
        compiler_params=_params(("parallel",)),
    )(dya, dyb, wa, wb)


def _branch_bwd_w(name, oa, ob, dya, dyb):
    t = oa.shape[0]
    tn = 512

    def body(oa_ref, ob_ref, da_ref, db_ref, wa_ref, wb_ref):
        wa_ref[...] = lax.dot_general(_bf(oa_ref[...]), da_ref[...], _TN, preferred_element_type=F32).astype(BF16)
        wb_ref[...] = lax.dot_general(_bf(ob_ref[...]), db_ref[...], _TN, preferred_element_type=F32).astype(BF16)

    whole = pl.BlockSpec((t, A_WIDTH), lambda j: (0, 0), pipeline_mode=pl.Buffered(1))
    cols = pl.BlockSpec((t, tn), lambda j: (0, j))
    out_spec = pl.BlockSpec((A_WIDTH, tn), lambda j: (0, j))
    out = jax.ShapeDtypeStruct((A_WIDTH, D_MODEL), BF16)
    return pl.pallas_call(
        body, name=name, grid=(D_MODEL // tn,), in_specs=[whole, whole, cols, cols], out_specs=[out_spec, out_spec],
        out_shape=[out, out], compiler_params=_params(("parallel",)),
    )(oa, ob, dya, dyb)


def _mix_out_gate_bwd(name, dh, wo, gates, ya, yb):
    t = gates.shape[0]
    tm = _pick(t, (544, 384, 256, 128))

    def body(dh_ref, w_ref, g_ref, ya_ref, yb_ref, dya_ref, dyb_ref, dg_ref):
        dm = lax.dot_general(_bf(dh_ref[...]), w_ref[...], _NT, preferred_element_type=F32)
        sa = jax.nn.sigmoid(g_ref[:, 0:D_MODEL])
        sb = jax.nn.sigmoid(g_ref[:, D_MODEL:2 * D_MODEL])
        dya_ref[...] = (dm * sa).astype(BF16)
        dyb_ref[...] = (dm * sb).astype(BF16)
        dg_ref[:, 0:D_MODEL] = (dm * ya_ref[...].astype(F32) * (sa * (1.0 - sa))).astype(BF16)
        dg_ref[:, D_MODEL:2 * D_MODEL] = (dm * yb_ref[...].astype(F32) * (sb * (1.0 - sb))).astype(BF16)

    blk = pl.BlockSpec((tm, D_MODEL), lambda i: (i, 0))
    wide = pl.BlockSpec((tm, 2 * D_MODEL), lambda i: (i, 0))
    out = jax.ShapeDtypeStruct((t, D_MODEL), BF16)
    return pl.pallas_call(
        body, name=name, grid=(t // tm,),
        in_specs=[blk, pl.BlockSpec((D_MODEL, D_MODEL), lambda i: (0, 0)), wide, blk, blk], out_specs=[blk, blk, wide],
        out_shape=[out, out, jax.ShapeDtypeStruct((t, 2 * D_MODEL), BF16)], compiler_params=_params(("parallel",)),
    )(dh, wo, gates, ya, yb)


def _ffn_out_loss(name, a, w, res, gf, tgt, alpha):
    t, k = a.shape
    b, s, _ = tgt.shape
    l = t // b
    tm = _pick(t, (544, 384, 256, 128))
    per = l // tm
    assert per * tm == l and tm > PREFIX and l - PREFIX == s

    def body(a_ref, w_ref, r_ref, g_ref, t_ref, dh_ref, loss_ref, dg_ref, buf, sem):
        i = pl.program_id(0)
        bi, r = i // per, i % per

        def first_rows():
            return pltpu.make_async_copy(t_ref.at[bi, pl.ds(0, tm - PREFIX)], buf.at[pl.ds(PREFIX, tm - PREFIX)], sem)

        def later_rows():
            return pltpu.make_async_copy(t_ref.at[bi, pl.ds(pl.multiple_of(r * tm - PREFIX, 8), tm)], buf, sem)

        @pl.when(r == 0)
        def _():
            buf[0:PREFIX, :] = jnp.zeros((PREFIX, D_MODEL), F32)
            first_rows().start()

        if per > 1:
            @pl.when(r > 0)
            def _():
                later_rows().start()

        acc = lax.dot_general(a_ref[...], w_ref[...], _NN, preferred_element_type=F32)

        @pl.when(r == 0)
        def _():
            first_rows().wait()

        if per > 1:
            @pl.when(r > 0)
            def _():
                later_rows().wait()

        hv = acc * alpha + r_ref[...]
        row = lax.broadcasted_iota(jnp.int32, (tm, 1), 0)
        real = ((r > 0) | (row >= PREFIX)).astype(F32)
        g = g_ref[...]
        rn = lax.rsqrt(jnp.mean(hv * hv, axis=-1, keepdims=True) + EPS)
        xn = hv * rn
        err = (xn * g - buf[...]) * real
        lpart = 0.5 * jnp.sum(jnp.mean(err * err, axis=-1, keepdims=True), axis=0, keepdims=True)
        dy = err * (1.0 / D_MODEL)
        tv = dy * g
        dot = jnp.mean(tv * hv, axis=-1, keepdims=True)
        dh_ref[...] = rn * tv - hv * (rn * rn * rn * dot)
        gpart = jnp.sum(dy * xn, axis=0, keepdims=True)

        @pl.when(i == 0)
        def _():
            loss_ref[...] = jnp.zeros_like(loss_ref)
            dg_ref[...] = jnp.zeros_like(dg_ref)

        loss_ref[...] += jnp.broadcast_to(lpart, loss_ref.shape)
        dg_ref[...] += gpart

    row_spec = pl.BlockSpec((tm, D_MODEL), lambda i: (i, 0))
    vec = pl.BlockSpec((1, D_MODEL), lambda i: (0, 0))
    return pl.pallas_call(
        body, name=name, grid=(t // tm,),
        in_specs=[pl.BlockSpec((tm, k), lambda i: (i, 0)),
                  pl.BlockSpec((k, D_MODEL), lambda i: (0, 0), pipeline_mode=pl.Buffered(1)), row_spec, vec,
                  pl.BlockSpec(memory_space=pl.ANY)],
        out_specs=[row_spec, pl.BlockSpec((8, 128), lambda i: (0, 0)), vec],
        out_shape=[jax.ShapeDtypeStruct((t, D_MODEL), F32), jax.ShapeDtypeStruct((8, 128), F32),
                   jax.ShapeDtypeStruct((1, D_MODEL), F32)],
        scratch_shapes=[pltpu.VMEM((tm, D_MODEL), F32), pltpu.SemaphoreType.DMA(())],
        compiler_params=_params(("arbitrary",)),
    )(a, w, res, gf, tgt)


def _fgate_fwd(name, f3, bf_row):
    b, l, _ = f3.shape
    nb = l // BLOCK

    def body(f_ref, b_ref, cc_ref, cr_ref):
        r_i = lax.broadcasted_iota(jnp.int32, (BLOCK, BLOCK), 0)
        c_i = lax.broadcasted_iota(jnp.int32, (BLOCK, BLOCK), 1)
        tri = (r_i >= c_i).astype(F32)
        carry = jnp.zeros((1, 128), F32)
        for blk in range(nb):
            rows = slice(blk * BLOCK, (blk + 1) * BLOCK)
            z = f_ref[rows, :] + b_ref[...]
            lf = jnp.minimum(z, 0.0) - jnp.log(1.0 + jnp.exp(-jnp.abs(z)))
            cb = jnp.dot(tri, lf, preferred_element_type=F32, precision=lax.Precision.HIGHEST) + carry
            carry = cb[BLOCK - 1:BLOCK, :]
            cbt = cb.T
            for hh in range(B_HEADS):
                cc_ref[hh, rows, :] = jnp.sum(jnp.where(c_i == hh, cb, 0.0), axis=1, keepdims=True)
                cr_ref[hh, :, rows] = cbt[hh:hh + 1, :]

    return pl.pallas_call(
        body, name=name, grid=(b,),
        in_specs=[pl.BlockSpec((None, l, 128), lambda bi: (bi, 0, 0)),
                  pl.BlockSpec((1, 128), lambda bi: (0, 0))],
        out_specs=[pl.BlockSpec((None, B_HEADS, l, 1), lambda bi: (bi, 0, 0, 0)),
                   pl.BlockSpec((None, B_HEADS, 1, l), lambda bi: (bi, 0, 0, 0))],
        out_shape=[jax.ShapeDtypeStruct((b, B_HEADS, l, 1), F32), jax.ShapeDtypeStruct((b, B_HEADS, 1, l), F32)],
        compiler_params=_params(("parallel",)),
    )(f3, bf_row)


def _fgate_bwd(name, f3, bf_row, dcq, dck):
    b, l, _ = f3.shape
    nb = l // BLOCK

    def body(f_ref, b_ref, dcq_ref, dck_ref, df_ref, db_ref):
        r_i = lax.broadcasted_iota(jnp.int32, (BLOCK, BLOCK), 0)
        c_i = lax.broadcasted_iota(jnp.int32, (BLOCK, BLOCK), 1)
        tri = (r_i <= c_i).astype(F32)
        carry = jnp.zeros((1, 128), F32)
        total = jnp.zeros((1, 128), F32)
        for blk in range(nb - 1, -1, -1):
            rows = slice(blk * BLOCK, (blk + 1) * BLOCK)
            krows = jnp.concatenate([dck_ref[hh, :, rows] for hh in range(B_HEADS)]
                                    + [jnp.zeros((BLOCK - B_HEADS, BLOCK), F32)], axis=0)
            dcb = krows.T
            for hh in range(B_HEADS):
                dcb = dcb + jnp.where(c_i == hh, dcq_ref[hh, rows, :], 0.0)
            rc = jnp.dot(tri, dcb, preferred_element_type=F32, precision=lax.Precision.HIGHEST) + carry
            carry = rc[0:1, :]
            z = f_ref[rows, :] + b_ref[...]
            df = rc * (1.0 / (1.0 + jnp.exp(z)))
            df_ref[rows, :] = df.astype(BF16)
            total = total + jnp.sum(df, axis=0, keepdims=True)

        @pl.when(pl.program_id(0) == 0)
        def _():
            db_ref[...] = total

        @pl.when(pl.program_id(0) > 0)
        def _():
            db_ref[...] += total

    return pl.pallas_call(
        body, name=name, grid=(b,),
        in_specs=[pl.BlockSpec((None, l, 128), lambda bi: (bi, 0, 0)),
                  pl.BlockSpec((1, 128), lambda bi: (0, 0)),
                  pl.BlockSpec((None, B_HEADS, l, 1), lambda bi: (bi, 0, 0, 0)),
                  pl.BlockSpec((None, B_HEADS, 1, l), lambda bi: (bi, 0, 0, 0))],
        out_specs=[pl.BlockSpec((None, l, 128), lambda bi: (bi, 0, 0)), pl.BlockSpec((1, 128), lambda bi: (0, 0))],
        out_shape=[jax.ShapeDtypeStruct((b, l, 128), BF16), jax.ShapeDtypeStruct((1, 128), F32)],
        compiler_params=_params(("arbitrary",)),
    )(f3, bf_row, dcq, dck)


A_Q_BLK = B_SEG // A_WIDTH
A_K_BLK = (B_SEG + A_WIDTH) // 128
A_V_BLK = A_K_BLK + 1
A_SEG_BLK = B_SEG // A_SEG
STACK = A_HEADS * BLOCK


def _lane_lo():
    return lax.broadcasted_iota(jnp.int32, (1, 128), 1) < HEAD_DIM


def _stack_heads(x, masked):
    lo = _lane_lo()
    blks = [x[:, 128 * j:128 * (j + 1)] for j in range(4)]
    if not masked:
        return jnp.concatenate(blks + blks, axis=0)
    zero = jnp.zeros_like(blks[0])
    return jnp.concatenate([jnp.where(lo, bk, zero) for bk in blks] + [jnp.where(lo, zero, bk) for bk in blks], axis=0)


def _unstack_heads(y):
    lo = _lane_lo()
    return jnp.concatenate([jnp.where(lo, y[128 * j:128 * (j + 1)], y[128 * (4 + j):128 * (5 + j)]) for j in range(4)], axis=1)


def _swa_bias(slopes):
    r_i = jnp.arange(STACK)[:, None]
    c_i = jnp.arange(3 * BLOCK)[None, :]
    seg = c_i >> 7
    out = []
    for n in range(3):
        qpos = n * BLOCK + (r_i & (BLOCK - 1))
        kpos = jnp.where(seg == 0, c_i, (n - 2) * BLOCK + c_i)
        dist = qpos - kpos
        band = (seg != 0) & (dist < BLOCK) & (kpos >= PREFIX)
        meta = (seg == 0) & (c_i >= N_PAD)
        out.append(jnp.where((dist >= 0) & (band | meta), -slopes * dist.astype(F32), NEG))
    return jnp.stack(out, axis=0)


def _swa_scores(q, kcat, n, slope, bias):
    s = lax.dot_general(q, kcat, _NT, preferred_element_type=F32) + bias
    further = slope * (-BLOCK * jnp.maximum(n - 2, 0)).astype(F32)
    return jnp.concatenate([s[:, 0:BLOCK] + further, s[:, BLOCK:]], axis=1)


def _swa_specs():
    def kv(col_blk):
        return [pl.BlockSpec((None, BLOCK, 128), lambda b, n: (b, 0, col_blk)),
                pl.BlockSpec((None, BLOCK, 128), lambda b, n: (b, jnp.maximum(n - 1, 0), col_blk)),
                pl.BlockSpec((None, BLOCK, 128), lambda b, n: (b, n, col_blk))]

    q_spec = pl.BlockSpec((None, BLOCK, A_WIDTH), lambda b, n: (b, n, A_Q_BLK))
    o_spec = pl.BlockSpec((None, BLOCK, A_WIDTH), lambda b, n: (b, n, 0))
    col = pl.BlockSpec((STACK, 1), lambda b, n: (0, 0))
    bias = pl.BlockSpec((None, STACK, 3 * BLOCK), lambda b, n: (jnp.minimum(n, 2), 0, 0))
    lse_spec = pl.BlockSpec((None, A_HEADS, BLOCK, 1), lambda b, n: (b, 0, n, 0))
    return q_spec, kv(A_K_BLK), kv(A_V_BLK), o_spec, [col, col, bias], lse_spec


def _swa_fwd(name, qkv, slopes, sinks, bias):
    b, l, _ = qkv.shape
    nb = l // BLOCK

    def body(q_ref, k0_ref, kp_ref, kc_ref, v0_ref, vp_ref, vc_ref, sl_ref, sk_ref, bias_ref, o_ref, lse_ref):
        n = pl.program_id(1)
        qs = _stack_heads(q_ref[...], True) * SCALE
        kcat = jnp.concatenate([k0_ref[...], kp_ref[...], kc_ref[...]], axis=0)
        vcat = jnp.concatenate([v0_ref[...], vp_ref[...], vc_ref[...]], axis=0)
        s = _swa_scores(qs, kcat, n, sl_ref[...], bias_ref[...])
        sink = sk_ref[...]
        m = jnp.maximum(jnp.max(s, axis=-1, keepdims=True), sink)
        p = jnp.exp(s - m)
        den = jnp.sum(p, axis=-1, keepdims=True) + jnp.exp(sink - m)
        o = lax.dot_general(p.astype(BF16), vcat, _NN, preferred_element_type=F32) / den
        o_ref[...] = _unstack_heads(o)
        lse_ref[...] = (m + jnp.log(den)).reshape(A_HEADS, BLOCK, 1)

    q_spec, k_specs, v_specs, o_spec, consts, lse_spec = _swa_specs()
    return pl.pallas_call(
        body, name=name, grid=(b, nb),
        in_specs=[q_spec] + k_specs + v_specs + consts, out_specs=[o_spec, lse_spec],
        out_shape=[jax.ShapeDtypeStruct((b, l, A_WIDTH), F32), jax.ShapeDtypeStruct((b, A_HEADS, l, 1), F32)],
        compiler_params=_params(("parallel", "parallel")),
    )(qkv, qkv, qkv, qkv, qkv, qkv, qkv, slopes, sinks, bias)


def _swa_bwd(name, qkv, o, lse, do, slopes, sinks, bias, dqkv):
    b, l, _ = qkv.shape
    nb = l // BLOCK

    def body(q_ref, k0_ref, kp_ref, kc_ref, v0_ref, vp_ref, vc_ref, o_ref, lse_ref, do_ref, sl_ref, sk_ref, bias_ref, _,
             dx_ref, ds_ref, dk_acc, dv_acc):
        bi = pl.program_id(0)
        n = pl.program_id(1)
        qs = _stack_heads(q_ref[...], True) * SCALE
        dos32 = _stack_heads(do_ref[...], True)
        dos = dos32.astype(BF16)
        os_ = _stack_heads(o_ref[...], False)
        lsev = lse_ref[...].reshape(STACK, 1)
        kcat = jnp.concatenate([k0_ref[...], kp_ref[...], kc_ref[...]], axis=0)
        vcat = jnp.concatenate([v0_ref[...], vp_ref[...], vc_ref[...]], axis=0)
        s = _swa_scores(qs, kcat, n, sl_ref[...], bias_ref[...])
        p = jnp.exp(s - lsev)
        dsum = jnp.sum(dos32 * os_, axis=-1, keepdims=True)
        dp = lax.dot_general(dos, vcat, _NT, preferred_element_type=F32)
        dsc = (p * (dp - dsum)).astype(BF16)
        dq = lax.dot_general(dsc, kcat, _NN, preferred_element_type=F32) * SCALE
        row0 = pl.multiple_of(n * BLOCK, BLOCK)
        dx_ref[pl.ds(row0, BLOCK), 0:A_WIDTH] = _unstack_heads(dq).astype(BF16)
        dkc = lax.dot_general(dsc, qs, _TN, preferred_element_type=F32)
        dvc = lax.dot_general(p.astype(BF16), dos, _TN, preferred_element_type=F32)

        @pl.when(n == 0)
        def _():
            dk_acc[...] = jnp.zeros_like(dk_acc)
            dv_acc[...] = jnp.zeros_like(dv_acc)

        starts = (0, pl.multiple_of(jnp.maximum(n - 1, 0) * BLOCK, BLOCK), row0)
        for t, st in enumerate(starts):
            dk_acc[pl.ds(st, BLOCK), :] += dkc[t * BLOCK:(t + 1) * BLOCK, :]
            dv_acc[pl.ds(st, BLOCK), :] += dvc[t * BLOCK:(t + 1) * BLOCK, :]

        @pl.when(n == nb - 1)
        def _():
            dx_ref[:, A_WIDTH:A_WIDTH + 128] = dk_acc[...].astype(BF16)
            dx_ref[:, A_WIDTH + 128:A_SEG] = dv_acc[...].astype(BF16)

        dsink = -(jnp.exp(sk_ref[...] - lsev) * dsum)
        r8 = lax.broadcasted_iota(jnp.int32, (8, 128), 0)
        acc = jnp.zeros((8, 128), F32)
        for hh in range(A_HEADS):
            acc = acc + jnp.where(r8 == hh, jnp.sum(dsink[hh * BLOCK:(hh + 1) * BLOCK, :]), 0.0)

        @pl.when((bi == 0) & (n == 0))
        def _():
            ds_ref[...] = jnp.zeros_like(ds_ref)

        ds_ref[...] += acc

    q_spec, k_specs, v_specs, o_spec, consts, lse_spec = _swa_specs()
    return pl.pallas_call(
        body, name=name, grid=(b, nb),
        in_specs=[q_spec] + k_specs + v_specs + [o_spec, lse_spec, o_spec] + consts + [pl.BlockSpec(memory_space=pl.ANY)],
        out_specs=[pl.BlockSpec((None, l, A_SEG), lambda bb, n: (bb, 0, A_SEG_BLK)),
                   pl.BlockSpec((8, 128), lambda bb, n: (0, 0))],
        out_shape=[jax.ShapeDtypeStruct(dqkv.shape, BF16), jax.ShapeDtypeStruct((8, 128), F32)],
        scratch_shapes=[pltpu.VMEM((l, 128), F32), pltpu.VMEM((l, 128), F32)],
        input_output_aliases={13: 0},
        compiler_params=_params(("arbitrary", "arbitrary")),
    )(qkv, qkv, qkv, qkv, qkv, qkv, qkv, o, lse, do, slopes, sinks, bias, dqkv)


def _fox_mask(qk, ck, i):
    kh = qk.shape[1]
    qpos = i * BLOCK + lax.broadcasted_iota(jnp.int32, (BLOCK, kh), 0)
    kpos = lax.broadcasted_iota(jnp.int32, (BLOCK, kh), 1)
    return jnp.where((kpos <= qpos) & (kpos >= N_PAD), qk - ck, NEG)


def _pick_head(x, hh):
    lo = _lane_lo()
    return jnp.where(lo if hh == 0 else jnp.logical_not(lo), x, jnp.zeros_like(x))


def _both_heads(x):
    return jnp.concatenate([_pick_head(x, 0), _pick_head(x, 1)], axis=0)


def _fox_specs(l):
    pair = pl.BlockSpec((None, l, PAIR_W), lambda bi, hp: (bi, 0, hp))
    half = pl.BlockSpec((None, l, 128), lambda bi, hp: (bi, 0, hp))
    colv = pl.BlockSpec((None, 2, l, 1), lambda bi, hp: (bi, hp, 0, 0))
    rowv = pl.BlockSpec((None, 2, 1, l), lambda bi, hp: (bi, hp, 0, 0))
    return pair, half, colv, rowv


def _fox_fwd(name, qkv, c_col, c_row):
    b, l, _ = qkv.shape
    nb = l // BLOCK

    def body(x_ref, cc_ref, cr_ref, o_ref, lse_ref):
        for i in range(nb):
            rows = slice(i * BLOCK, (i + 1) * BLOCK)
            kh = (i + 1) * BLOCK
            qblk = x_ref[rows, 0:128]
            kv = x_ref[0:kh, 128:256]
            vv = x_ref[0:kh, 256:384]
            qk = lax.dot_general(_both_heads(qblk) * SCALE, kv, _NT, preferred_element_type=F32)
            ps, dens = [], []
            for hh in range(2):
                s = _fox_mask(qk[hh * BLOCK:(hh + 1) * BLOCK], cr_ref[hh, :, 0:kh], i)
                m = jnp.max(s, axis=-1, keepdims=True)
                p = jnp.exp(s - m)
                den = jnp.sum(p, axis=-1, keepdims=True)
                ps.append(p.astype(BF16))
                dens.append(den)
                lse_ref[hh, rows, :] = (m + jnp.log(den)) + cc_ref[hh, rows, :]
            pv = lax.dot_general(jnp.concatenate(ps, axis=0), vv, _NN, preferred_element_type=F32)
            o_ref[rows, :] = jnp.where(_lane_lo(), pv[0:BLOCK] / dens[0], pv[BLOCK:2 * BLOCK] / dens[1]).astype(BF16)

    pair, half, colv, rowv = _fox_specs(l)
    return pl.pallas_call(
        body, name=name, grid=(b, 4), in_specs=[pair, colv, rowv], out_specs=[half, colv],
        out_shape=[jax.ShapeDtypeStruct((b, l, B_WIDTH), BF16), jax.ShapeDtypeStruct((b, B_HEADS, l, 1), F32)],
        compiler_params=_params(("parallel", "parallel")),
    )(qkv, c_col, c_row)


def _fox_bwd(name, qkv, c_col, c_row, o, lse, do):
    b, l, _ = qkv.shape
    nb = l // BLOCK

    def body(x_ref, cc_ref, cr_ref, o_ref, lse_ref, do_ref, dx_ref, dcq_ref, dck_ref, dk_acc, dv_acc):
        dk_acc[...] = jnp.zeros_like(dk_acc)
        dv_acc[...] = jnp.zeros_like(dv_acc)
        dck_ref[...] = jnp.zeros_like(dck_ref)
        for i in range(nb):
            rows = slice(i * BLOCK, (i + 1) * BLOCK)
            kh = (i + 1) * BLOCK
            qblk = x_ref[rows, 0:128]
            kv = x_ref[0:kh, 128:256]
            vv = x_ref[0:kh, 256:384]
            doblk = do_ref[rows, :]
            ov = o_ref[rows, :].astype(F32)
            q2 = _both_heads(qblk) * SCALE
            do2 = _both_heads(doblk)
            qk = lax.dot_general(q2, kv, _NT, preferred_element_type=F32)
            dp = lax.dot_general(do2, vv, _NT, preferred_element_type=F32)
            ps, dss = [], []
            for hh in range(2):
                half = slice(hh * BLOCK, (hh + 1) * BLOCK)
                s = _fox_mask(qk[half], cr_ref[hh, :, 0:kh], i)
                p = jnp.exp(s - (lse_ref[hh, rows, :] - cc_ref[hh, rows, :]))
                dsum = jnp.sum(do2[half].astype(F32) * ov, axis=-1, keepdims=True)
                ds = p * (dp[half] - dsum)
                ps.append(p.astype(BF16))
                dss.append(ds.astype(BF16))
                dcq_ref[hh, rows, :] = jnp.sum(ds, axis=-1, keepdims=True)
                dck_ref[hh, :, 0:kh] -= jnp.sum(ds, axis=0, keepdims=True)
            p2 = jnp.concatenate(ps, axis=0)
            ds2 = jnp.concatenate(dss, axis=0)
            dq = lax.dot_general(ds2, kv, _NN, preferred_element_type=F32) * SCALE
            dk_acc[0:kh, :] += lax.dot_general(ds2, q2, _TN, preferred_element_type=F32)
            dv_acc[0:kh, :] += lax.dot_general(p2, do2, _TN, preferred_element_type=F32)
            dx_ref[rows, 0:128] = jnp.where(_lane_lo(), dq[0:BLOCK], dq[BLOCK:2 * BLOCK]).astype(BF16)
        dx_ref[:, 128:256] = dk_acc[...].astype(BF16)
        dx_ref[:, 256:384] = dv_acc[...].astype(BF16)

    pair, half, colv, rowv = _fox_specs(l)
    return pl.pallas_call(
        body, name=name, grid=(b, 4), in_specs=[pair, colv, rowv, half, colv, half],
        out_specs=[pair, colv, rowv],
        out_shape=[jax.ShapeDtypeStruct(qkv.shape, BF16), jax.ShapeDtypeStruct((b, B_HEADS, l, 1), F32),
                   jax.ShapeDtypeStruct((b, B_HEADS, 1, l), F32)],
        scratch_shapes=[pltpu.VMEM((l, 128), F32), pltpu.VMEM((l, 128), F32)],
        compiler_params=_params(("parallel", "parallel")),
    )(qkv, c_col, c_row, o, lse, do)


_FLIPS = ((0, 0, 1), (0, 1, 0), (0, 1, 1), (1, 0, 0), (1, 0, 1), (1, 1, 0), (1, 1, 1))


def _peer_table():
    x, y, c = lax.axis_index("x"), lax.axis_index("y"), lax.axis_index("c")
    me = 4 * x + 2 * y + c
    peers = []
    for fx, fy, fc in _FLIPS:
        px = 1 - x if fx else x
        py = 1 - y if fy else y
        pc = 1 - c if fc else c
        peers.append(((px, py, pc), 4 * px + 2 * py + pc))
    return me, peers


_HBM = pl.BlockSpec(memory_space=pltpu.HBM)
_SEM = pl.BlockSpec(memory_space=pltpu.SEMAPHORE)
_ANY = pl.BlockSpec(memory_space=pl.ANY)
_EFFECT = pltpu.SideEffectType.DATAFLOW_SIDE_EFFECTING


def _split_copy(srcs_are_pieces, src_refs, land_refs, send_sem, recv_sem, a, kk, me, peers, arriving):
    dev, lin = peers[kk]
    npeer = len(_FLIPS)
    src = src_refs[a] if srcs_are_pieces[a] else src_refs[a].at[lin]
    dst = land_refs[a].at[lin] if arriving else land_refs[a].at[me]
    return pltpu.make_async_remote_copy(src_ref=src, dst_ref=dst, send_sem=send_sem.at[a * npeer + kk],
                                        recv_sem=recv_sem.at[a * npeer + kk], device_id=dev, device_id_type=MESH_ID)


def _xchg_start(name, gather, scatter, after=None):
    me_out = 4 * lax.axis_index("x") + 2 * lax.axis_index("y") + lax.axis_index("c")
    srcs = list(gather) + list(scatter)
    is_piece = [True] * len(gather) + [False] * len(scatter)
    lands = []
    for a, piece in zip(srcs, is_piece):
        if piece:
            lands.append(lax.dynamic_update_slice(lax.empty((N_DEV,) + tuple(a.shape), a.dtype), a[None],
                                                  (me_out,) + (0,) * a.ndim))
        else:
            lands.append(lax.empty(tuple(a.shape), a.dtype))
    n = len(srcs)
    nsem = n * len(_FLIPS)
    has_after = after is not None

    def body(*refs):
        src_refs = refs[:n]
        land_refs = refs[n:2 * n]
        outs = refs[2 * n + (1 if has_after else 0):]
        send_sem, recv_sem = outs[0], outs[1]
        token = outs[-1]
        me, peers = _peer_table()
        for kk in range(len(_FLIPS)):
            for a in range(n):
                _split_copy(is_piece, src_refs, land_refs, send_sem, recv_sem, a, kk, me, peers, False).start()
        token[...] = jnp.zeros_like(token)

    out_shape = ([pltpu.SemaphoreType.DMA((nsem,)), pltpu.SemaphoreType.DMA((nsem,))]
                 + [pltpu.HBM(tuple(a.shape), a.dtype) for a in srcs] + [pltpu.HBM(tuple(a.shape), a.dtype) for a in lands]
                 + [jax.ShapeDtypeStruct((8, 128), F32)])
    args = [pltpu.with_memory_space_constraint(a, pltpu.HBM) for a in srcs + lands] + ([after] if has_after else [])
    res = pl.pallas_call(
        body, name=name, out_shape=out_shape,
        in_specs=[_HBM] * (2 * n) + ([_ANY] if has_after else []),
        out_specs=[_SEM, _SEM] + [_HBM] * (2 * n) + [pl.BlockSpec(memory_space=pltpu.VMEM)],
        input_output_aliases={i: 2 + i for i in range(2 * n)},
        compiler_params=pltpu.CompilerParams(has_side_effects=_EFFECT),
    )(*args)
    state = (res[0], res[1], list(res[2:2 + n]), list(res[2 + n:2 + 2 * n]), is_piece)
    return state, res[-1]


def _xchg_wait(name, state, after):
    send_sem, recv_sem, srcs, lands, is_piece = state
    n = len(srcs)

    def body(*refs):
        src_refs = refs[:n]
        land_refs = refs[n:2 * n]
        s_sem, r_sem = refs[2 * n], refs[2 * n + 1]
        me, peers = _peer_table()
        for kk in range(len(_FLIPS)):
            for a in range(n):
                cp = _split_copy(is_piece, src_refs, land_refs, s_sem, r_sem, a, kk, me, peers, True)
                cp.wait_send()
                cp.wait_recv()

    out_shape = [pltpu.HBM(tuple(a.shape), a.dtype) for a in srcs] + [pltpu.HBM(tuple(a.shape), a.dtype) for a in lands]
    res = pl.pallas_call(
        body, name=name, out_shape=out_shape,
        in_specs=[_HBM] * (2 * n) + [_SEM, _SEM, _ANY], out_specs=[_HBM] * (2 * n),
        input_output_aliases={i: i for i in range(2 * n)},
        compiler_params=pltpu.CompilerParams(has_side_effects=_EFFECT),
    )(*srcs, *lands, send_sem, recv_sem, after)
    return list(res[n:]), list(res[:n])


_SIB = (0, 0, 1)
_ICI = ((0, 1, 0), (1, 0, 0), (1, 1, 0))


def _flip(fl):
    x, y, c = lax.axis_index("x"), lax.axis_index("y"), lax.axis_index("c")
    px = 1 - x if fl[0] else x
    py = 1 - y if fl[1] else y
    pc = 1 - c if fl[2] else c
    return (px, py, pc), 4 * px + 2 * py + pc


def _gather2_start(name, pieces, after=None):
    me_out = 4 * lax.axis_index("x") + 2 * lax.axis_index("y") + lax.axis_index("c")
    pieces = list(pieces)
    n = len(pieces)
    lands = [lax.dynamic_update_slice(lax.empty((N_DEV,) + tuple(a.shape), a.dtype), a[None],
                                      (me_out,) + (0,) * a.ndim) for a in pieces]
    first = (_SIB,) + _ICI
    has_after = after is not None

    def body(*refs):
        src_refs, land_refs = refs[:n], refs[n:2 * n]
        outs = refs[2 * n + (1 if has_after else 0):]
        send_sem, recv_sem, token = outs[0], outs[1], outs[-1]
        _, me = _flip((0, 0, 0))
        for kk, fl in enumerate(first):
            dev, _ = _flip(fl)
            for a in range(n):
                pltpu.make_async_remote_copy(src_ref=src_refs[a], dst_ref=land_refs[a].at[me],
                                             send_sem=send_sem.at[a * 4 + kk], recv_sem=recv_sem.at[a * 4 + kk],
                                             device_id=dev, device_id_type=MESH_ID).start()
        token[...] = jnp.zeros_like(token)

    hbm = [pltpu.HBM(tuple(a.shape), a.dtype) for a in pieces + lands]
    res = pl.pallas_call(
        body, name=name,
        out_shape=[pltpu.SemaphoreType.DMA((4 * n,)), pltpu.SemaphoreType.DMA((4 * n,))] + hbm
        + [jax.ShapeDtypeStruct((8, 128), F32)],
        in_specs=[_HBM] * (2 * n) + ([_ANY] if has_after else []),
        out_specs=[_SEM, _SEM] + [_HBM] * (2 * n) + [pl.BlockSpec(memory_space=pltpu.VMEM)],
        input_output_aliases={i: 2 + i for i in range(2 * n)},
        compiler_params=pltpu.CompilerParams(has_side_effects=_EFFECT),
    )(*([pltpu.with_memory_space_constraint(a, pltpu.HBM) for a in pieces + lands] + ([after] if has_after else [])))
    return (res[0], res[1], list(res[2:2 + n]), list(res[2 + n:2 + 2 * n])), res[-1]


def _gather2_forward(name, state, after):
    send_a, recv_a, pieces, lands = state
    n = len(pieces)
    first = (_SIB,) + _ICI

    def body(*refs):
        src_refs, land_refs = refs[:n], refs[n:2 * n]
        s_a, r_a = refs[2 * n], refs[2 * n + 1]
        outs = refs[2 * n + 3:]
        send_b, recv_b, token = outs[0], outs[1], outs[-1]
        for kk, fl in enumerate(first):
            dev, lin = _flip(fl)
            for a in range(n):
                cp = pltpu.make_async_remote_copy(src_ref=src_refs[a], dst_ref=land_refs[a].at[lin],
                                                  send_sem=s_a.at[a * 4 + kk], recv_sem=r_a.at[a * 4 + kk],
                                                  device_id=dev, device_id_type=MESH_ID)
                cp.wait_send()
                cp.wait_recv()
        sib, _ = _flip(_SIB)
        for j, fl in enumerate(_ICI):
            _, lin = _flip(fl)
            for a in range(n):
                pltpu.make_async_remote_copy(src_ref=land_refs[a].at[lin], dst_ref=land_refs[a].at[lin],
                                             send_sem=send_b.at[a * 3 + j], recv_sem=recv_b.at[a * 3 + j],
                                             device_id=sib, device_id_type=MESH_ID).start()
        token[...] = jnp.zeros_like(token)

    hbm = [pltpu.HBM(tuple(a.shape), a.dtype) for a in pieces + lands]
    res = pl.pallas_call(
        body, name=name,
        out_shape=[pltpu.SemaphoreType.DMA((3 * n,)), pltpu.SemaphoreType.DMA((3 * n,))] + hbm
        + [jax.ShapeDtypeStruct((8, 128), F32)],
        in_specs=[_HBM] * (2 * n) + [_SEM, _SEM, _ANY],
        out_specs=[_SEM, _SEM] + [_HBM] * (2 * n) + [pl.BlockSpec(memory_space=pltpu.VMEM)],
        input_output_aliases={i: 2 + i for i in range(2 * n)},
        compiler_params=pltpu.CompilerParams(has_side_effects=_EFFECT),
    )(*pieces, *lands, send_a, recv_a, after)
    return (res[0], res[1], list(res[2 + n:2 + 2 * n])), res[-1]


def _gather2_wait(name, state, after):
    send_b, recv_b, lands = state
    n = len(lands)

    def body(*refs):
        land_refs = refs[:n]
        s_b, r_b = refs[n], refs[n + 1]
        sib, _ = _flip(_SIB)
        for j, fl in enumerate(_ICI):
            _, sent = _flip(fl)
            _, arriving = _flip((fl[0], fl[1], 1))
            for a in range(n):
                cp = pltpu.make_async_remote_copy(src_ref=land_refs[a].at[sent], dst_ref=land_refs[a].at[arriving],
                                                  send_sem=s_b.at[a * 3 + j], recv_sem=r_b.at[a * 3 + j],
                                                  device_id=sib, device_id_type=MESH_ID)
                cp.wait_send()
                cp.wait_recv()

    res = pl.pallas_call(
        body, name=name, out_shape=[pltpu.HBM(tuple(a.shape), a.dtype) for a in lands],
        in_specs=[_HBM] * n + [_SEM, _SEM, _ANY], out_specs=[_HBM] * n,
        input_output_aliases={i: i for i in range(n)},
        compiler_params=pltpu.CompilerParams(has_side_effects=_EFFECT),
    )(*lands, send_b, recv_b, after)
    return list(res)


def _adam_math(w, g, m, v):
    m = ADAM_B1 * m + (1.0 - ADAM_B1) * g
    v = ADAM_B2 * v + (1.0 - ADAM_B2) * (g * g)
    m_hat = m / (1.0 - ADAM_B1 ** ADAM_STEP)
    v_hat = v / (1.0 - ADAM_B2 ** ADAM_STEP)
    delta = -ADAM_LR * (m_hat / (jnp.sqrt(v_hat) + ADAM_EPS) + ADAM_WD * w)
    return delta, m, v


def _adam(name, w, m, v, parts, transposed=False, dep=None, own=None):
    npart, _, cp = parts.shape
    has_dep = dep is not None
    has_own = own is not None
    if transposed:
        c, r = w.shape
        tr = _pick(r, (256, 128))
        blk = pl.BlockSpec((c, tr), lambda i, *_: (0, i))
    else:
        r, c = w.shape
        tr = _pick(r, (256, 176, 128, 64, 16, 8, 1))
        blk = pl.BlockSpec((tr, c), lambda i, *_: (i, 0))

    def body(*refs):
        refs = list(refs)
        me = refs.pop(0)[0] if has_own else None
        w_ref, m_ref, v_ref, p_ref = refs[:4]
        own_ref = refs[4] if has_own else None
        g_ref, d_ref, mo_ref, vo_ref = refs[-4:]
        g = None
        for pp in range(npart):
            part = p_ref[pp] if not has_own else jnp.where(me == pp, own_ref[...], p_ref[pp])
            g = part.astype(F32) if g is None else g + part.astype(F32)
        g = g.T[0:c, :] if transposed else g[:, 0:c]
        delta, mn, vn = _adam_math(w_ref[...], g, m_ref[...], v_ref[...])
        g_ref[...] = g
        d_ref[...] = delta
        mo_ref[...] = mn
        vo_ref[...] = vn

    in_specs = [blk, blk, blk, pl.BlockSpec((npart, tr, cp), lambda i, *_: (0, i, 0))]
    args = [w, m, v, parts]
    if has_own:
        in_specs.append(pl.BlockSpec((None, tr, cp), lambda i, me_ref: (me_ref[0], i, 0)))
        args.append(own[0])
    if has_dep:
        in_specs.append(pl.BlockSpec(memory_space=pl.ANY))
        args.append(dep)
    out = jax.ShapeDtypeStruct(w.shape, F32)
    grid_spec = pltpu.PrefetchScalarGridSpec(num_scalar_prefetch=1 if has_own else 0, grid=(r // tr,),
                                             in_specs=in_specs, out_specs=[blk, blk, blk, blk])
    return pl.pallas_call(
        body, name=name, grid_spec=grid_spec, out_shape=[out, out, out, out], compiler_params=_params(("parallel",)),
    )(*(([own[1]] if has_own else []) + args))


def _small_update(name, packs, wmv):
    nparam = len(wmv)

    def body(p_ref, *refs):
        ins, outs = refs[:3 * nparam], refs[3 * nparam:]
        tot = p_ref[0]
        for pp in range(1, N_DEV):
            tot = tot + p_ref[pp]
        outs[0][0:8, :] = tot[0:8, :]
        outs[0][8:24, :] = tot[8:24, :] + tot[24:40, :]
        grads = [tot[i:i + 1, :] for i in range(4)] + [tot[4:5, 0:B_HEADS], tot[4:5, B_HEADS:2 * B_HEADS]]
        for i, g in enumerate(grads):
            w_ref, m_ref, v_ref = ins[3 * i:3 * i + 3]
            delta, mn, vn = _adam_math(w_ref[...], g, m_ref[...], v_ref[...])
            for o_ref, val in zip(outs[1 + 4 * i:5 + 4 * i], (g, delta, mn, vn)):
                o_ref[...] = val

    flat = [a for trio in wmv for a in trio]
    out_shape = [jax.ShapeDtypeStruct((24, D_MODEL), F32)]
    for w, _, _ in wmv:
        out_shape += [jax.ShapeDtypeStruct(w.shape, F32)] * 4
    res = pl.pallas_call(body, name=name, out_shape=out_shape, compiler_params=_params())(packs, *flat)
    return res[0], [tuple(res[1 + 4 * i:5 + 4 * i]) for i in range(nparam)]


def _local_step(x, tgt, g1, gm, g2, gf, b_forget, sinks, weights, send):
    b, s, _ = x.shape
    l = s + PREFIX
    t = b * l
    (meta,) = weights("meta", x)

    h0, n1 = _embed_norm("embed_rms1_fwd", x, meta, g1)
    (w1i,) = weights("ffn1_in", n1)
    gu1, a1 = _ffn_in_fwd("ffn1_in_fwd", n1, w1i)
    (w1o,) = weights("ffn1_out", weights("ffn1_out:forward", a1))
    h1, um = _mm_res_norm("ffn1_out_fwd", a1, w1o, h0, gm, alpha=0.5)
    wi, wa, wb, wo = weights("mix", weights("mix:forward", um))
    qkv, gates, f2 = _proj_fwd("proj_fwd", um, wi)
    qkv3 = qkv.reshape(b, l, QKV_W)
    f3 = f2.reshape(b, l, 128)
    bf_row = jnp.pad(b_forget, ((0, 0), (0, 128 - B_HEADS)))
    c_col, c_row = _fgate_fwd("fgate_fwd", f3, bf_row)
    head_of_row = jnp.arange(STACK) // BLOCK
    slopes = jnp.exp2(-8.0 * (head_of_row + 1).astype(F32) / A_HEADS).reshape(STACK, 1)
    sink_rows = jnp.repeat(sinks.reshape(A_HEADS), BLOCK).reshape(STACK, 1)
    swa_bias = _swa_bias(slopes)
    oa3, lse_a = _swa_fwd("swa_fwd", qkv3, slopes, sink_rows, swa_bias)
    ob3, lse_b = _fox_fwd("fox_fwd", qkv3, c_col, c_row)
    oa = oa3.reshape(t, A_WIDTH)
    ob = ob3.reshape(t, B_WIDTH)
    mixed, ya, yb = _branch_gate_fwd("branch_gate_fwd", oa, ob, wa, wb, gates, dep=weights("ffn2:forward", ob))
    h2, n2 = _mm_res_norm("mix_out_fwd", mixed, wo, h1, g2)
    w2i, w2o = weights("ffn2", h2)
    gu2, a2 = _ffn_in_fwd("ffn2_in_fwd", n2, w2i)

    dh3, loss_blk, dgf = _ffn_out_loss("ffn2_out_loss", a2, w2o, h2, gf, tgt, 0.5)

    def ffn_bwd(tag, dh, h_in, g_norm, n_in, gu, a, w_in_blk, w_out, one_send, examples=None):
        dw_out = _ffn_out_bwd_w(tag + "_out_bwd_w", a, dh)
        dgu = _ffn_out_bwd_x(tag + "_out_bwd_x", dh, w_out, gu, dep=None if one_send else send(tag + "_out", (dw_out,)))
        dw_in = _ffn_in_bwd_w(tag + "_in_bwd_w", n_in, dgu)
        token = send(tag, (dw_in, dw_out)) if one_send else send(tag + "_in", (dw_in,))
        return _ffn_in_bwd_x(tag + "_in_bwd_x", dgu, w_in_blk, h_in, g_norm, dh, dep=token, examples=examples)

    dh2, dg2 = ffn_bwd("ffn2", dh3, h2, g2, n2, gu2, a2, w2i, w2o, True)

    dwo = _mm_tn("mix_out_bwd_w", mixed, dh2)
    dya, dyb, dgates = _mix_out_gate_bwd("mix_out_gate_bwd", dh2, wo, gates, ya, yb)
    doa, dob = _branch_bwd_x("branch_bwd_x", dya, dyb, wa, wb)
    dwa, dwb = _branch_bwd_w("branch_bwd_w", oa, ob, dya, dyb)
    dqkv3, dcq, dck = _fox_bwd("fox_bwd", qkv3, c_col, c_row, ob3, lse_b, dob.reshape(b, l, B_WIDTH))
    dqkv3, dsink = _swa_bwd("swa_bwd", qkv3, oa3, lse_a, doa.reshape(b, l, A_WIDTH), slopes, sink_rows, swa_bias, dqkv3)
    dqkv = dqkv3.reshape(t, QKV_W)
    df3, dbf = _fgate_bwd("fgate_bwd", f3, bf_row, dcq, dck)
    df = df3.reshape(t, 128)
    dwi_qkv = _mm_tn("proj_qkv_bwd_w", um, dqkv, tm_c=(512,), tn_c=(768,))
    dwi_g = _mm_tn("proj_gates_bwd_w", um, dgates, tm_c=(512,), tn_c=(512,))
    dwi_f = _mm_tn("proj_f_bwd_w", um, df, tm_c=(512,), tn_c=(128,))
    token = send("mix", (dwi_qkv, dwi_g, dwi_f, dwa, dwb, dwo))
    dh1, dgm = _proj_bwd_x("proj_bwd_x", dqkv, dgates, df, wi, h1, gm, dh2, dep=token)

    grad_x, dmeta3, dg1 = ffn_bwd("ffn1", dh1, h0, g1, n1, gu1, a1, w1i, w1o, False, examples=b)
    dmeta = dmeta3.reshape(b * N_META, D_MODEL)

    misc = jnp.concatenate([dbf[:, 0:B_HEADS], dsink[:, 0].reshape(1, A_HEADS), loss_blk[0:1, 0:1]], axis=1)
    misc = jnp.pad(misc, ((0, 0), (0, D_MODEL - misc.shape[1])))
    row = lax.broadcasted_iota(jnp.int32, (8, D_MODEL), 0)
    vec = jnp.zeros((8, D_MODEL), F32)
    for i, piece in enumerate((dg1, dgm, dg2, dgf, misc)):
        vec = jnp.where(row == i, piece, vec)
    small = jnp.concatenate([vec, dmeta], axis=0)
    return grad_x, small


def _pad_to(a, rows, cols):
    return jnp.pad(a, ((0, rows - a.shape[0]), (0, cols - a.shape[1])))


def _ffn_out_from_gathered(name, g):
    def body(g_ref, o_ref):
        o_ref[0:FFO_SHARD, :] = g_ref[0]
        o_ref[FFO_SHARD:FF_SHARD, :] = g_ref[1]
        o_ref[FF_SHARD:FF_SHARD_P, :] = jnp.zeros((FF_SHARD_P - FF_SHARD, D_MODEL), BF16)

    return pl.pallas_call(
        body, name=name, grid=(4,),
        in_specs=[pl.BlockSpec((2, FFO_SHARD, D_MODEL), lambda j: (j, 0, 0))],
        out_specs=pl.BlockSpec((FF_SHARD_P, D_MODEL), lambda j: (j, 0)),
        out_shape=jax.ShapeDtypeStruct((D_FF_P, D_MODEL), BF16), compiler_params=_params(("parallel",)),
    )(g)


def _ffn_out_bwd_w(name, a, dh):
    t = a.shape[0]
    tn = D_MODEL // 2

    def body(a_ref, b_ref, o_ref):
        acc = lax.dot_general(a_ref[...], _bf(b_ref[...]), _TN, preferred_element_type=F32) * 0.5
        o_ref[0] = acc[0:FFO_SHARD].astype(BF16)
        o_ref[1] = acc[FFO_SHARD:FF_SHARD].astype(BF16)

    return pl.pallas_call(
        body, name=name, grid=(4, 2),
        in_specs=[pl.BlockSpec((t, FF_SHARD_P), lambda i, j: (0, i)), pl.BlockSpec((t, tn), lambda i, j: (0, j))],
        out_specs=pl.BlockSpec((2, FFO_SHARD, tn), lambda i, j: (i, 0, j)),
        out_shape=jax.ShapeDtypeStruct((N_DEV, FFO_SHARD, D_MODEL), BF16), compiler_params=_params(("parallel", "parallel")),
    )(a, dh)


def _proj_segments():
    segs = [(HEAD_DIM * h, HEAD_DIM, 0, B_SEG + HEAD_DIM * A_HEAD_ORDER.index(h)) for h in range(A_HEADS)]
    segs += [(512, 128, 0, B_SEG + A_WIDTH), (640, 128, 0, B_SEG + A_WIDTH + 128)]
    for first, off in ((768, 0), (1280, 128), (1792, 256)):
        segs += [(first + 128 * hp, 128, 0, PAIR_W * hp + off) for hp in range(4)]
    segs += [(2304, B_HEADS, 2, 0), (2312, 2 * D_MODEL, 1, 0)]
    return segs


_RELAYOUT_ROWS = 256


def _proj_from_gathered(name, g):
    rows = _RELAYOUT_ROWS

    def body(g_ref, o_ref):
        def cols(first, width):
            out = []
            for p in range(N_DEV):
                lo, hi = max(first, WIN_SHARD * p), min(first + width, WIN_SHARD * (p + 1))
                if lo < hi:
                    out.append(g_ref[p, :, lo - WIN_SHARD * p:hi - WIN_SHARD * p])
            return out

        parts = []
        for arr in (0, 1, 2):
            for first, width, _, _ in sorted((s for s in _proj_segments() if s[2] == arr), key=lambda s: s[3]):
                parts += cols(first, width)
            if arr == 0:
                parts.append(jnp.zeros((rows, P_GATES - QKV_W), BF16))
        parts.append(jnp.zeros((rows, 128 - B_HEADS), BF16))
        o_ref[...] = jnp.concatenate(parts, axis=1)

    return pl.pallas_call(
        body, name=name, grid=(D_MODEL // rows,),
        in_specs=[pl.BlockSpec((N_DEV, rows, WIN_SHARD_P), lambda i: (0, i, 0))],
        out_specs=pl.BlockSpec((rows, PROJ_P), lambda i: (i, 0)),
        out_shape=jax.ShapeDtypeStruct((D_MODEL, PROJ_P), BF16), compiler_params=_params(("parallel",)),
    )(g)


def _proj_to_scatter(name, dqkv_w, dg_w, df_w):
    rows = _RELAYOUT_ROWS
    segs = sorted(_proj_segments())

    def body(q_ref, g_ref, f_ref, o_ref):
        arrays = (q_ref, g_ref, f_ref)
        for p in range(N_DEV):
            parts = []
            for first, width, arr, at in segs:
                lo, hi = max(first, WIN_SHARD * p), min(first + width, WIN_SHARD * (p + 1))
                if lo < hi:
                    parts.append(arrays[arr][:, at + lo - first:at + hi - first])
            parts.append(jnp.zeros((rows, WIN_SHARD_P - WIN_SHARD), BF16))
            o_ref[p] = jnp.concatenate(parts, axis=1)

    return pl.pallas_call(
        body, name=name, grid=(D_MODEL // rows,),
        in_specs=[pl.BlockSpec((rows, QKV_W), lambda i: (i, 0)), pl.BlockSpec((rows, 2 * D_MODEL), lambda i: (i, 0)),
                  pl.BlockSpec((rows, 128), lambda i: (i, 0))],
        out_specs=pl.BlockSpec((N_DEV, rows, WIN_SHARD_P), lambda i: (0, i, 0)),
        out_shape=jax.ShapeDtypeStruct((N_DEV, D_MODEL, WIN_SHARD_P), BF16), compiler_params=_params(("parallel",)),
    )(dqkv_w, dg_w, df_w)


def _a_rows_from_natural(w):
    return jnp.concatenate([w[HEAD_DIM * h:HEAD_DIM * (h + 1)] for h in A_HEAD_ORDER], axis=0)


def _a_rows_to_natural(w):
    return jnp.concatenate([w[HEAD_DIM * A_HEAD_ORDER.index(h):HEAD_DIM * (A_HEAD_ORDER.index(h) + 1)]
                            for h in range(A_HEADS)], axis=0)


def kernel(x, meta_tokens, ffn1_norm, ffn1_w_in, ffn1_w_out, mix_norm, w_in, b_forget, attn_sinks, w_branch_a, w_branch_b, w_out, ffn2_norm, ffn2_w_in, ffn2_w_out, final_norm, loss_target, m_meta_tokens, m_ffn1_norm, m_ffn1_w_in, m_ffn1_w_out, m_mix_norm, m_w_in, m_b_forget, m_attn_sinks, m_w_branch_a, m_w_branch_b, m_w_out, m_ffn2_norm, m_ffn2_w_in, m_ffn2_w_out, m_final_norm, v_meta_tokens, v_ffn1_norm, v_ffn1_w_in, v_ffn1_w_out, v_mix_norm, v_w_in, v_b_forget, v_attn_sinks, v_w_branch_a, v_w_branch_b, v_w_out, v_ffn2_norm, v_ffn2_w_in, v_ffn2_w_out, v_final_norm):
    me = 4 * lax.axis_index("x") + 2 * lax.axis_index("y") + lax.axis_index("c")

    def shard(w, tok, rows=None, cols=None):
        piece = (w[0] if tok is None else w[0] + tok[0, 0]).astype(BF16)
        return piece if rows is None else _pad_to(piece, rows, cols)

    first_level, second_level = {}, {}
    first_level["meta"], tok = _gather2_start("gather_meta_start", (meta_tokens,))
    first_level["ffn1_in"], tok = _gather2_start(
        "gather_ffn1_in_start", (shard(ffn1_w_in, None, D_MODEL, FF_SHARD_P),), after=tok)
    first_level["ffn1_out"], tok = _gather2_start("gather_ffn1_out_start", (shard(ffn1_w_out, tok),), after=tok)
    first_level["mix"], tok = _gather2_start(
        "gather_mix_start", (shard(w_in, tok, D_MODEL, WIN_SHARD_P), shard(w_branch_a, tok), shard(w_branch_b, tok),
                             shard(w_out, tok)), after=tok)
    first_level["ffn2"], tok = _gather2_start(
        "gather_ffn2_start", (shard(ffn2_w_in, tok, D_MODEL, FF_SHARD_P), shard(ffn2_w_out, tok)), after=tok)
    started = {"tok": tok}

    def weights(group, after):
        if group.endswith(":forward"):
            group = group[:-len(":forward")]
            second_level[group], token = _gather2_forward("gather_" + group + "_forward", first_level[group], after)
            return token
        if group == "meta":
            after = weights("meta:forward", started["tok"])
        if group == "ffn1_in":
            after = weights("ffn1_in:forward", after)
        got = _gather2_wait("gather_" + group + "_wait", second_level[group], after)
        if group == "mix":
            gwi, gwa, gwb, gwo = got
            return (_proj_from_gathered("proj_w_relayout", gwi), _a_rows_from_natural(gwa.transpose(1, 0, 2).reshape(A_WIDTH, D_MODEL)),
                    gwb.transpose(1, 0, 2).reshape(B_WIDTH, D_MODEL), gwo.reshape(D_MODEL, D_MODEL))
        if group == "ffn1_out":
            return (_ffn_out_from_gathered("ffn1_w_out_relayout", got[0]),)
        if group == "meta":
            return (got[0].transpose(1, 0, 2).reshape(N_META, D_MODEL),)
        if group == "ffn1_in":
            return (got[0].reshape(2, 4, D_MODEL, FF_SHARD_P),)
        return got[0].reshape(2, 4, D_MODEL, FF_SHARD_P), _ffn_out_from_gathered("ffn2_w_out_relayout", got[1])

    scatter_state = {}

    def send(group, grads):
        if group == "mix":
            dwi_qkv, dwi_g, dwi_f, dwa, dwb, dwo = grads
            dwa = _a_rows_to_natural(dwa)
            blocks = (_proj_to_scatter("proj_dw_relayout", dwi_qkv, dwi_g, dwi_f), dwa.reshape(A_WIDTH, N_DEV, 128).transpose(1, 0, 2),
                      dwb.reshape(B_WIDTH, N_DEV, 128).transpose(1, 0, 2), dwo.reshape(N_DEV, 128, D_MODEL))
        elif group.endswith("_in"):
            blocks = (grads[0].reshape(N_DEV, D_MODEL, FF_SHARD_P),)
        elif group.endswith("_out"):
            blocks = (grads[0],)
        else:
            blocks = (grads[0].reshape(N_DEV, D_MODEL, FF_SHARD_P), grads[1])
        scatter_state[group], token = _xchg_start("scatter_" + group + "_start", (), blocks)
        return token

    gf = final_norm.reshape(1, D_MODEL)
    grad_x, small = _local_step(x, loss_target, ffn1_norm, mix_norm, ffn2_norm, gf, b_forget, attn_sinks, weights, send)

    small_state, after = _xchg_start("gather_small_start", (small,), ())
    out = {}
    updates = (
        ("ffn2", (("ffn2_w_in", ffn2_w_in, m_ffn2_w_in, v_ffn2_w_in), ("ffn2_w_out", ffn2_w_out, m_ffn2_w_out, v_ffn2_w_out))),
        ("ffn1_out", (("ffn1_w_out", ffn1_w_out, m_ffn1_w_out, v_ffn1_w_out),)),
        ("mix", (("w_in", w_in, m_w_in, v_w_in), ("w_branch_a", w_branch_a, m_w_branch_a, v_w_branch_a),
                 ("w_branch_b", w_branch_b, m_w_branch_b, v_w_branch_b), ("w_out", w_out, m_w_out, v_w_out))),
    )
    last_update = ("ffn1_in", (("ffn1_w_in", ffn1_w_in, m_ffn1_w_in, v_ffn1_w_in),))

    def update(group, members, after):
        parts_list, blocks_list = _xchg_wait("scatter_" + group + "_wait", scatter_state[group], after)
        prev = None
        for (nm, w, m, v), parts, blocks in zip(members, parts_list, blocks_list):
            if nm.endswith("w_in"):
                res4 = _adam("adam_" + nm, w[0].T, m[0].T, v[0].T, parts, transposed=True, dep=prev, own=(blocks, me_arr))
                out[nm] = tuple(r.T[None] for r in res4)
            else:
                res4 = _adam("adam_" + nm, w[0], m[0], v[0], parts, dep=prev, own=(blocks, me_arr))
                out[nm] = tuple(r[None] for r in res4)
            prev = res4[0]
        return prev

    me_arr = me.reshape(1).astype(jnp.int32)
    for group, members in updates + (last_update,):
        after = update(group, members, after)
    (packs,), _ = _xchg_wait("gather_small_wait", small_state, after)

    row = lambda a: a.reshape(1, D_MODEL)
    tot, small_res = _small_update("small_update", packs, (
        (ffn1_norm, m_ffn1_norm, v_ffn1_norm), (mix_norm, m_mix_norm, v_mix_norm), (ffn2_norm, m_ffn2_norm, v_ffn2_norm),
        (row(final_norm), row(m_final_norm), row(v_final_norm)), (b_forget, m_b_forget, v_b_forget),
        (attn_sinks, m_attn_sinks, v_attn_sinks)))
    for nm, res4 in zip(("ffn1_norm", "mix_norm", "ffn2_norm", "final_norm", "b_forget", "attn_sinks"), small_res):
        out[nm] = tuple(r.reshape(D_MODEL) for r in res4) if nm == "final_norm" else res4
    loss = tot[4, 2 * B_HEADS]
    g_meta = lax.dynamic_slice(tot[8:24, :], (0, me * 128), (N_META, 128))
    out["meta_tokens"] = tuple(_adam("adam_meta_tokens", meta_tokens, m_meta_tokens, v_meta_tokens, g_meta[None]))

    names = ("meta_tokens", "ffn1_norm", "ffn1_w_in", "ffn1_w_out", "mix_norm", "w_in", "b_forget", "attn_sinks",
             "w_branch_a", "w_branch_b", "w_out", "ffn2_norm", "ffn2_w_in", "ffn2_w_out", "final_norm")
    return (loss, grad_x) + tuple(out[nm][kind] for kind in range(4) for nm in names)
```

```python
import jax
import jax.numpy as jnp
from jax import lax
from jax.experimental import pallas as pl
from jax.experimental.pallas import tpu as pltpu

F32 = jnp.float32
BF16 = jnp.bfloat16

D_MODEL = 1024
N_META = 16
BLOCK = 128
PREFIX = 128
N_PAD = PREFIX - N_META
HEAD_DIM = 64
A_HEADS = 8
B_HEADS = 8
A_WIDTH = 512
A_KV_WIDTH = 128
B_WIDTH = 512
D_FF = 2816
N_DEV = 8
FF_SHARD = 2 * D_FF // N_DEV
FF_SHARD_P = 768
FFO_SHARD = D_FF // N_DEV
D_FF_P = 4 * FF_SHARD_P
W_IN_COLS = 4360
WIN_SHARD = W_IN_COLS // N_DEV
WIN_SHARD_P = 640
PAIR_W = 3 * 128
B_SEG = 4 * PAIR_W
A_SEG = A_WIDTH + 2 * A_KV_WIDTH
QKV_W = B_SEG + A_SEG
P_GATES = 2 * (2 * D_MODEL)
P_F = P_GATES + 2 * D_MODEL
PROJ_P = P_F + 128
A_HEAD_ORDER = (0, 4, 1, 5, 2, 6, 3, 7)
EPS = 1e-6
NEG = -1e30
SCALE = HEAD_DIM ** -0.5
ADAM_LR = 0.001
ADAM_B1 = 0.9
ADAM_B2 = 0.999
ADAM_EPS = 1e-08
ADAM_WD = 0.01
ADAM_STEP = 10
VMEM_LIMIT = 56 * 1024 * 1024
MESH_ID = pl.DeviceIdType.MESH
SMALL_ROWS = 40

_NN = (((1,), (0,)), ((), ()))
_NT = (((1,), (1,)), ((), ()))
_TN = (((0,), (0,)), ((), ()))


def _params(sem=None):
    return pltpu.CompilerParams(dimension_semantics=sem, vmem_limit_bytes=VMEM_LIMIT)


def _pick(n, cands):
    for c in cands:
        if n % c == 0:
            return c
    raise ValueError(f"no tile for {n}")


def _bf(v):
    return v if v.dtype == BF16 else v.astype(BF16)


def _mm(name, a, b, dims, grid, a_spec, b_spec, o_spec, out_shape, out_dtype, alpha=1.0):
    def body(a_ref, b_ref, o_ref):
        acc = lax.dot_general(_bf(a_ref[...]), _bf(b_ref[...]), dims, preferred_element_type=F32)
        if alpha != 1.0:
            acc = acc * alpha
        o_ref[...] = acc.astype(o_ref.dtype)

    return pl.pallas_call(
        body, name=name, grid=grid, in_specs=[a_spec, b_spec], out_specs=o_spec,
        out_shape=jax.ShapeDtypeStruct(out_shape, out_dtype),
        compiler_params=_params(("parallel",) * len(grid)),
    )(a, b)


def _mm_res_norm(name, a, w, res, g_next, alpha=1.0):
    t, k = a.shape
    tm = _pick(t, (544, 384, 256, 128))

    def body(a_ref, w_ref, r_ref, g_ref, h_ref, n_ref):
        acc = lax.dot_general(_bf(a_ref[...]), w_ref[...], _NN, preferred_element_type=F32)
        if alpha != 1.0:
            acc = acc * alpha
        hv = acc + r_ref[...]
        h_ref[...] = hv
        r = lax.rsqrt(jnp.mean(hv * hv, axis=-1, keepdims=True) + EPS)
        n_ref[...] = ((hv * r) * g_ref[...]).astype(BF16)

    row = pl.BlockSpec((tm, D_MODEL), lambda i: (i, 0))
    return pl.pallas_call(
        body, name=name, grid=(t // tm,),
        in_specs=[pl.BlockSpec((tm, k), lambda i: (i, 0)), pl.BlockSpec((k, D_MODEL), lambda i: (0, 0)), row,
                  pl.BlockSpec((1, D_MODEL), lambda i: (0, 0))],
        out_specs=[row, row],
        out_shape=[jax.ShapeDtypeStruct((t, D_MODEL), F32), jax.ShapeDtypeStruct((t, D_MODEL), BF16)],
        compiler_params=_params(("parallel",)),
    )(a, w, res, g_next)


def _mm_tn(name, a, b, out_dtype=BF16, alpha=1.0, tm_c=(768, 512, 256, 128), tn_c=(512, 640, 256, 128)):
    t, m = a.shape
    n = b.shape[1]
    tm = _pick(m, tm_c)
    tn = _pick(n, tn_c)
    bytes_a, bytes_b = a.size * a.dtype.itemsize, b.size * b.dtype.itemsize
    if bytes_a + bytes_b * (m // tm) <= bytes_b + bytes_a * (n // tn):
        return _mm(name, a, b, _TN, (m // tm, n // tn),
                   pl.BlockSpec((t, tm), lambda i, j: (0, i)), pl.BlockSpec((t, tn), lambda i, j: (0, j)),
                   pl.BlockSpec((tm, tn), lambda i, j: (i, j)), (m, n), out_dtype, alpha=alpha)
    return _mm(name, a, b, _TN, (n // tn, m // tm),
               pl.BlockSpec((t, tm), lambda j, i: (0, i)), pl.BlockSpec((t, tn), lambda j, i: (0, j)),
               pl.BlockSpec((tm, tn), lambda j, i: (i, j)), (m, n), out_dtype, alpha=alpha)


def _ffn_in_fwd(name, n, wblk, dep=None):
    t = n.shape[0]
    tm = _pick(t, (1088, 768, 512, 256, 128))
    has_dep = dep is not None

    def body(n_ref, w_ref, *rest):
        gu_ref, a_ref = rest[-2], rest[-1]
        nv = n_ref[...]
        g = lax.dot_general(nv, w_ref[0], _NN, preferred_element_type=F32)
        u = lax.dot_general(nv, w_ref[1], _NN, preferred_element_type=F32)
        sg = jax.nn.sigmoid(g)
        silu = g * sg
        a_ref[...] = (silu * u).astype(BF16)
        gu_ref[0] = ((0.5 * u) * (sg + silu * (1.0 - sg))).astype(BF16)
        gu_ref[1] = (0.5 * silu).astype(BF16)

    return pl.pallas_call(
        body, name=name, grid=(t // tm, 4),
        in_specs=[pl.BlockSpec((tm, D_MODEL), lambda i, j: (i, 0)),
                  pl.BlockSpec((2, None, D_MODEL, FF_SHARD_P), lambda i, j: (0, j, 0, 0))]
        + ([pl.BlockSpec(memory_space=pl.ANY)] if has_dep else []),
        out_specs=[pl.BlockSpec((2, tm, FF_SHARD_P), lambda i, j: (0, i, j)),
                   pl.BlockSpec((tm, FF_SHARD_P), lambda i, j: (i, j))],
        out_shape=[jax.ShapeDtypeStruct((2, t, D_FF_P), BF16), jax.ShapeDtypeStruct((t, D_FF_P), BF16)],
        compiler_params=_params(("parallel", "parallel")),
    )(*((n, wblk) + ((dep,) if has_dep else ())))


def _ffn_out_bwd_x(name, dh, w_out, gu, dep=None):
    t = dh.shape[0]
    tm = _pick(t, (1088, 768, 512, 256, 128))
    has_dep = dep is not None

    def body(dh_ref, w_ref, gu_ref, *rest):
        o_ref = rest[-1]
        da = lax.dot_general(_bf(dh_ref[...]), w_ref[...], _NT, preferred_element_type=F32)
        o_ref[0] = (da * gu_ref[0].astype(F32)).astype(BF16)
        o_ref[1] = (da * gu_ref[1].astype(F32)).astype(BF16)

    gu_spec = pl.BlockSpec((2, tm, FF_SHARD_P), lambda i, j: (0, i, j))
    return pl.pallas_call(
        body, name=name, grid=(t // tm, 4),
        in_specs=[pl.BlockSpec((tm, D_MODEL), lambda i, j: (i, 0)), pl.BlockSpec((FF_SHARD_P, D_MODEL), lambda i, j: (j, 0)),
                  gu_spec] + ([pl.BlockSpec(memory_space=pl.ANY)] if has_dep else []),
        out_specs=gu_spec, out_shape=jax.ShapeDtypeStruct((2, t, D_FF_P), BF16),
        compiler_params=_params(("parallel", "parallel")),
    )(*((dh, w_out, gu) + ((dep,) if has_dep else ())))


def _rms_bwd_rows(dn, h, g, dres):
    r = lax.rsqrt(jnp.mean(h * h, axis=-1, keepdims=True) + EPS)
    tv = dn * g
    dot = jnp.mean(tv * h, axis=-1, keepdims=True)
    return dres + (r * tv - h * (r * r * r * dot)), jnp.sum(dn * (h * r), axis=0, keepdims=True)


def _accumulate_rows(ref, part, first):
    @pl.when(first)
    def _():
        ref[...] = part

    @pl.when(jnp.logical_not(first))
    def _():
        ref[...] += part


def _ffn_in_bwd_x(name, dgu, wblk, h_in, g_norm, dres, dep=None, examples=None):
    t = dgu.shape[1]
    tm = _pick(t, (544, 384, 256, 128))
    has_dep = dep is not None
    split = examples is not None
    if split:
        l = t // examples
        per = l // tm
        assert per * tm == l and tm > PREFIX

    def body(d_ref, w_ref, h_ref, g_ref, r_ref, *rest):
        acc = None
        for s in range(2):
            for j in range(4):
                part = lax.dot_general(d_ref[s, :, FF_SHARD_P * j:FF_SHARD_P * (j + 1)], w_ref[s, j], _NT,
                                       preferred_element_type=F32)
                acc = part if acc is None else acc + part
        dh, dg = _rms_bwd_rows(acc, h_ref[...], g_ref[...], r_ref[...])
        i = pl.program_id(0)
        if not split:
            dh_ref, dg_ref = rest[-2], rest[-1]
            dh_ref[...] = dh
        else:
            gx_ref, meta_ref, dg_ref, buf, sem = rest[-5:]

            def out_copy(step):
                bi, r = step // per, step % per
                head = pltpu.make_async_copy(buf.at[pl.ds(PREFIX, tm - PREFIX)], gx_ref.at[bi, pl.ds(0, tm - PREFIX)], sem)
                if per == 1:
                    return r == 0, head, None
                return r == 0, head, pltpu.make_async_copy(
                    buf, gx_ref.at[bi, pl.ds(pl.multiple_of(jnp.maximum(r, 1) * tm - PREFIX, 8), tm)], sem)

            def run(step, method):
                is_head, head, later = out_copy(step)

                @pl.when(is_head)
                def _():
                    getattr(head, method)()

                if later is not None:
                    @pl.when(jnp.logical_not(is_head))
                    def _():
                        getattr(later, method)()

            @pl.when(i > 0)
            def _():
                run(i - 1, "wait")

            buf[...] = dh

            @pl.when(i % per == 0)
            def _():
                meta_ref[...] = dh[N_PAD:PREFIX]

            run(i, "start")

            @pl.when(i == pl.num_programs(0) - 1)
            def _():
                run(i, "wait")

        _accumulate_rows(dg_ref, dg, i == 0)

    row = pl.BlockSpec((tm, D_MODEL), lambda i: (i, 0))
    vec = pl.BlockSpec((1, D_MODEL), lambda i: (0, 0))
    if split:
        out_specs = [pl.BlockSpec(memory_space=pl.ANY), pl.BlockSpec((None, N_META, D_MODEL), lambda i: (i // per, 0, 0)), vec]
        out_shape = [jax.ShapeDtypeStruct((examples, l - PREFIX, D_MODEL), F32),
                     jax.ShapeDtypeStruct((examples, N_META, D_MODEL), F32), jax.ShapeDtypeStruct((1, D_MODEL), F32)]
        scratch = [pltpu.VMEM((tm, D_MODEL), F32), pltpu.SemaphoreType.DMA(())]
    else:
        out_specs = [row, vec]
        out_shape = [jax.ShapeDtypeStruct((t, D_MODEL), F32), jax.ShapeDtypeStruct((1, D_MODEL), F32)]
        scratch = []
    return pl.pallas_call(
        body, name=name, grid=(t // tm,),
        in_specs=[pl.BlockSpec((2, tm, D_FF_P), lambda i: (0, i, 0)),
                  pl.BlockSpec((2, 4, D_MODEL, FF_SHARD_P), lambda i: (0, 0, 0, 0), pipeline_mode=pl.Buffered(1)),
                  row, vec, row]
        + ([pl.BlockSpec(memory_space=pl.ANY)] if has_dep else []),
        out_specs=out_specs, out_shape=out_shape, scratch_shapes=scratch,
        compiler_params=_params(("arbitrary",)),
    )(*((dgu, wblk, h_in, g_norm, dres) + ((dep,) if has_dep else ())))


def _proj_fwd(name, um, wi):
    t = um.shape[0]
    tm = _pick(t, (544, 384, 256, 128))

    def body(u_ref, w_ref, q_ref, g_ref, f_ref):
        uv = u_ref[...]
        q_ref[...] = lax.dot_general(uv, w_ref[:, 0:QKV_W], _NN, preferred_element_type=F32).astype(BF16)
        g_ref[...] = lax.dot_general(uv, w_ref[:, P_GATES:P_F], _NN, preferred_element_type=F32)
        f_ref[...] = lax.dot_general(uv, w_ref[:, P_F:PROJ_P], _NN, preferred_element_type=F32)

    return pl.pallas_call(
        body, name=name, grid=(t // tm,),
        in_specs=[pl.BlockSpec((tm, D_MODEL), lambda i: (i, 0)),
                  pl.BlockSpec((D_MODEL, PROJ_P), lambda i: (0, 0), pipeline_mode=pl.Buffered(1))],
        out_specs=[pl.BlockSpec((tm, QKV_W), lambda i: (i, 0)), pl.BlockSpec((tm, 2 * D_MODEL), lambda i: (i, 0)),
                   pl.BlockSpec((tm, 128), lambda i: (i, 0))],
        out_shape=[jax.ShapeDtypeStruct((t, QKV_W), BF16), jax.ShapeDtypeStruct((t, 2 * D_MODEL), F32),
                   jax.ShapeDtypeStruct((t, 128), F32)],
        compiler_params=_params(("parallel",)),
    )(um, wi)


def _proj_bwd_x(name, dqkv, dgates, df, wi, h_in, g_norm, dres, dep=None):
    t = dqkv.shape[0]
    tm = _pick(t, (544, 384, 256, 128))
    has_dep = dep is not None

    def body(q_ref, gt_ref, f_ref, w_ref, h_ref, g_ref, r_ref, *rest):
        dh_ref, dg_ref = rest[-2], rest[-1]
        acc = lax.dot_general(q_ref[...], w_ref[:, 0:QKV_W], _NT, preferred_element_type=F32)
        acc = acc + lax.dot_general(gt_ref[...], w_ref[:, P_GATES:P_F], _NT, preferred_element_type=F32)
        acc = acc + lax.dot_general(f_ref[...], w_ref[:, P_F:PROJ_P], _NT, preferred_element_type=F32)
        dh, dg = _rms_bwd_rows(acc, h_ref[...], g_ref[...], r_ref[...])
        dh_ref[...] = dh
        _accumulate_rows(dg_ref, dg, pl.program_id(0) == 0)

    row = pl.BlockSpec((tm, D_MODEL), lambda i: (i, 0))
    vec = pl.BlockSpec((1, D_MODEL), lambda i: (0, 0))
    return pl.pallas_call(
        body, name=name, grid=(t // tm,),
        in_specs=[pl.BlockSpec((tm, QKV_W), lambda i: (i, 0)), pl.BlockSpec((tm, 2 * D_MODEL), lambda i: (i, 0)),
                  pl.BlockSpec((tm, 128), lambda i: (i, 0)),
                  pl.BlockSpec((D_MODEL, PROJ_P), lambda i: (0, 0), pipeline_mode=pl.Buffered(1)), row, vec, row]
        + ([pl.BlockSpec(memory_space=pl.ANY)] if has_dep else []),
        out_specs=[row, vec],
        out_shape=[jax.ShapeDtypeStruct((t, D_MODEL), F32), jax.ShapeDtypeStruct((1, D_MODEL), F32)],
        compiler_params=_params(("arbitrary",)),
    )(*((dqkv, dgates, df, wi, h_in, g_norm, dres) + ((dep,) if has_dep else ())))


def _ffn_in_bwd_w(name, n, dgu):
    t = n.shape[0]
    return _mm(name, n, dgu, _TN, (2, 4),
               pl.BlockSpec((t, D_MODEL), lambda s, j: (0, 0)),
               pl.BlockSpec((None, t, FF_SHARD_P), lambda s, j: (s, 0, j)),
               pl.BlockSpec((None, None, D_MODEL, FF_SHARD_P), lambda s, j: (s, j, 0, 0)),
               (2, 4, D_MODEL, FF_SHARD_P), BF16)


def _embed_norm(name, x, meta, g):
    b, s, _ = x.shape
    half = (s + PREFIX) // 2
    first = half - PREFIX
    assert first > 0 and half % 16 == 0

    def body(x_ref, m_ref, g_ref, h_ref, n_ref, buf, sem):
        bi, k = pl.program_id(0), pl.program_id(1)

        def tokens(example, second, method):
            if second:
                cp = pltpu.make_async_copy(x_ref.at[example, pl.ds(first, half)], buf.at[1], sem.at[1])
            else:
                cp = pltpu.make_async_copy(x_ref.at[example, pl.ds(0, first)], buf.at[0, pl.ds(PREFIX, first)], sem.at[0])
            getattr(cp, method)()

        @pl.when((bi == 0) & (k == 0))
        def _():
            tokens(0, False, "start")

        @pl.when(k == 0)
        def _():
            tokens(bi, True, "start")
            tokens(bi, False, "wait")
            buf[0, 0:N_PAD, :] = jnp.zeros((N_PAD, D_MODEL), F32)
            buf[0, N_PAD:PREFIX, :] = m_ref[...]

        @pl.when(k == 1)
        def _():
            @pl.when(bi + 1 < b)
            def _():
                tokens(bi + 1, False, "start")

            tokens(bi, True, "wait")

        hv = buf[k]
        h_ref[...] = hv
        r = lax.rsqrt(jnp.mean(hv * hv, axis=-1, keepdims=True) + EPS)
        n_ref[...] = ((hv * r) * g_ref[...]).astype(BF16)

    rows = pl.BlockSpec((half, D_MODEL), lambda bi, k: (2 * bi + k, 0))
    return pl.pallas_call(
        body, name=name, grid=(b, 2),
        in_specs=[pl.BlockSpec(memory_space=pl.ANY), pl.BlockSpec((N_META, D_MODEL), lambda bi, k: (0, 0)),
                  pl.BlockSpec((1, D_MODEL), lambda bi, k: (0, 0))],
        out_specs=[rows, rows],
        out_shape=[jax.ShapeDtypeStruct((2 * b * half, D_MODEL), F32), jax.ShapeDtypeStruct((2 * b * half, D_MODEL), BF16)],
        scratch_shapes=[pltpu.VMEM((2, half, D_MODEL), F32), pltpu.SemaphoreType.DMA((2,))],
        compiler_params=_params(("arbitrary", "arbitrary")),
    )(x, meta, g)


def _branch_gate_fwd(name, oa, ob, wa, wb, gates, dep=None):
    t = gates.shape[0]
    tm = _pick(t, (544, 384, 256, 128))
    has_dep = dep is not None

    def body(oa_ref, ob_ref, wa_ref, wb_ref, g_ref, *rest):
        o_ref, ya_ref, yb_ref = rest[-3:]
        ya = lax.dot_general(_bf(oa_ref[...]), wa_ref[...], _NN, preferred_element_type=F32)
        yb = lax.dot_general(_bf(ob_ref[...]), wb_ref[...], _NN, preferred_element_type=F32)
        sa = jax.nn.sigmoid(g_ref[:, 0:D_MODEL])
        sb = jax.nn.sigmoid(g_ref[:, D_MODEL:2 * D_MODEL])
        o_ref[...] = (sa * ya + sb * yb).astype(BF16)
        ya_ref[...] = ya.astype(BF16)
        yb_ref[...] = yb.astype(BF16)

    blk = pl.BlockSpec((tm, D_MODEL), lambda i: (i, 0))
    narrow = pl.BlockSpec((tm, A_WIDTH), lambda i: (i, 0))
    wide = pl.BlockSpec((tm, 2 * D_MODEL), lambda i: (i, 0))
    wspec = pl.BlockSpec((A_WIDTH, D_MODEL), lambda i: (0, 0))
    out = jax.ShapeDtypeStruct((t, D_MODEL), BF16)
    return pl.pallas_call(
        body, name=name, grid=(t // tm,),
        in_specs=[narrow, narrow, wspec, wspec, wide] + ([pl.BlockSpec(memory_space=pl.ANY)] if has_dep else []),
        out_specs=[blk, blk, blk], out_shape=[out, out, out], compiler_params=_params(("parallel",)),
    )(*((oa, ob, wa, wb, gates) + ((dep,) if has_dep else ())))


def _branch_bwd_x(name, dya, dyb, wa, wb):
    t = dya.shape[0]
    tm = _pick(t, (1088, 768, 512, 256, 128))

    def body(da_ref, db_ref, wa_ref, wb_ref, oa_ref, ob_ref):
        oa_ref[...] = lax.dot_general(da_ref[...], wa_ref[...], _NT, preferred_element_type=F32)
        ob_ref[...] = lax.dot_general(db_ref[...], wb_ref[...], _NT, preferred_element_type=F32).astype(BF16)

    blk = pl.BlockSpec((tm, D_MODEL), lambda i: (i, 0))
    narrow = pl.BlockSpec((tm, A_WIDTH), lambda i: (i, 0))
    wspec = pl.BlockSpec((A_WIDTH, D_MODEL), lambda i: (0, 0), pipeline_mode=pl.Buffered(1))
    return pl.pallas_call(
        body, name=name, grid=(t // tm,), in_specs=[blk, blk, wspec, wspec], out_specs=[narrow, narrow],
        out_shape=[jax.ShapeDtypeStruct((t, A_WIDTH), F32), jax.ShapeDtypeStruct((t, B_WIDTH), BF16)],
        compiler_params=_params(("parallel",)),
    )(dya, dyb, wa, wb)


def _branch_bwd_w(name, oa, ob, dya, dyb):
    t = oa.shape[0]
    tn = 512

    def body(oa_ref, ob_ref, da_ref, db_ref, wa_ref, wb_ref):
        wa_ref[...] = lax.dot_general(_bf(oa_ref[...]), da_ref[...], _TN, preferred_element_type=F32).astype(BF16)
        wb_ref[...] = lax.dot_general(_bf(ob_ref[...]), db_ref[...], _TN, preferred_element_type=F32).astype(BF16)

    whole = pl.BlockSpec((t, A_WIDTH), lambda j: (0, 0), pipeline_mode=pl.Buffered(1))
    cols = pl.BlockSpec((t, tn), lambda j: (0, j))
    out_spec = pl.BlockSpec((A_WIDTH, tn), lambda j: (0, j))
    out = jax.ShapeDtypeStruct((A_WIDTH, D_MODEL), BF16)
    return pl.pallas_call(
        body, name=name, grid=(D_MODEL // tn,), in_specs=[whole, whole, cols, cols], out_specs=[out_spec, out_spec],
        out_shape=[out, out], compiler_params=_params(("parallel",)),
    )(oa, ob, dya, dyb)


def _mix_out_gate_bwd(name, dh, wo, gates, ya, yb):
    t = gates.shape[0]
    tm = _pick(t, (544, 384, 256, 128))

    def body(dh_ref, w_ref, g_ref, ya_ref, yb_ref, dya_ref, dyb_ref, dg_ref):
        dm = lax.dot_general(_bf(dh_ref[...]), w_ref[...], _NT, preferred_element_type=F32)
        sa = jax.nn.sigmoid(g_ref[:, 0:D_MODEL])
        sb = jax.nn.sigmoid(g_ref[:, D_MODEL:2 * D_MODEL])
        dya_ref[...] = (dm * sa).astype(BF16)
        dyb_ref[...] = (dm * sb).astype(BF16)
        dg_ref[:, 0:D_MODEL] = (dm * ya_ref[...].astype(F32) * (sa * (1.0 - sa))).astype(BF16)
        dg_ref[:, D_MODEL:2 * D_MODEL] = (dm * yb_ref[...].astype(F32) * (sb * (1.0 - sb))).astype(BF16)

    blk = pl.BlockSpec((tm, D_MODEL), lambda i: (i, 0))
    wide = pl.BlockSpec((tm, 2 * D_MODEL), lambda i: (i, 0))
    out = jax.ShapeDtypeStruct((t, D_MODEL), BF16)
    return pl.pallas_call(
        body, name=name, grid=(t // tm,),
        in_specs=[blk, pl.BlockSpec((D_MODEL, D_MODEL), lambda i: (0, 0)), wide, blk, blk], out_specs=[blk, blk, wide],
        out_shape=[out, out, jax.ShapeDtypeStruct((t, 2 * D_MODEL), BF16)], compiler_params=_params(("parallel",)),
    )(dh, wo, gates, ya, yb)


def _ffn_out_loss(name, a, w, res, gf, tgt, alpha):
    t, k = a.shape
    b, s, _ = tgt.shape
    l = t // b
    tm = _pick(t, (544, 384, 256, 128))
    per = l // tm
    assert per * tm == l and tm > PREFIX and l - PREFIX == s

    def body(a_ref, w_ref, r_ref, g_ref, t_ref, dh_ref, loss_ref, dg_ref, buf, sem):
        i = pl.program_id(0)
        bi, r = i // per, i % per

        def first_rows():
            return pltpu.make_async_copy(t_ref.at[bi, pl.ds(0, tm - PREFIX)], buf.at[pl.ds(PREFIX, tm - PREFIX)], sem)

        def later_rows():
            return pltpu.make_async_copy(t_ref.at[bi, pl.ds(pl.multiple_of(r * tm - PREFIX, 8), tm)], buf, sem)

        @pl.when(r == 0)
        def _():
            buf[0:PREFIX, :] = jnp.zeros((PREFIX, D_MODEL), F32)
            first_rows().start()

        if per > 1:
            @pl.when(r > 0)
            def _():
                later_rows().start()

        acc = lax.dot_general(a_ref[...], w_ref[...], _NN, preferred_element_type=F32)

        @pl.when(r == 0)
        def _():
            first_rows().wait()

        if per > 1:
            @pl.when(r > 0)
            def _():
                later_rows().wait()

        hv = acc * alpha + r_ref[...]
        row = lax.broadcasted_iota(jnp.int32, (tm, 1), 0)
        real = ((r > 0) | (row >= PREFIX)).astype(F32)
        g = g_ref[...]
        rn = lax.rsqrt(jnp.mean(hv * hv, axis=-1, keepdims=True) + EPS)
        xn = hv * rn
        err = (xn * g - buf[...]) * real
        lpart = 0.5 * jnp.sum(jnp.mean(err * err, axis=-1, keepdims=True), axis=0, keepdims=True)
        dy = err * (1.0 / D_MODEL)
        tv = dy * g
        dot = jnp.mean(tv * hv, axis=-1, keepdims=True)
        dh_ref[...] = rn * tv - hv * (rn * rn * rn * dot)
        gpart = jnp.sum(dy * xn, axis=0, keepdims=True)

        @pl.when(i == 0)
        def _():
            loss_ref[...] = jnp.zeros_like(loss_ref)
            dg_ref[...] = jnp.zeros_like(dg_ref)

        loss_ref[...] += jnp.broadcast_to(lpart, loss_ref.shape)
        dg_ref[...] += gpart

    row_spec = pl.BlockSpec((tm, D_MODEL), lambda i: (i, 0))
    vec = pl.BlockSpec((1, D_MODEL), lambda i: (0, 0))
    return pl.pallas_call(
        body, name=name, grid=(t // tm,),
        in_specs=[pl.BlockSpec((tm, k), lambda i: (i, 0)),
                  pl.BlockSpec((k, D_MODEL), lambda i: (0, 0), pipeline_mode=pl.Buffered(1)), row_spec, vec,
                  pl.BlockSpec(memory_space=pl.ANY)],
        out_specs=[row_spec, pl.BlockSpec((8, 128), lambda i: (0, 0)), vec],
        out_shape=[jax.ShapeDtypeStruct((t, D_MODEL), F32), jax.ShapeDtypeStruct((8, 128), F32),
                   jax.ShapeDtypeStruct((1, D_MODEL), F32)],
        scratch_shapes=[pltpu.VMEM((tm, D_MODEL), F32), pltpu.SemaphoreType.DMA(())],
        compiler_params=_params(("arbitrary",)),
    )(a, w, res, gf, tgt)


def _fgate_fwd(name, f3, bf_row):
    b, l, _ = f3.shape
    nb = l // BLOCK

    def body(f_ref, b_ref, cc_ref, cr_ref):
        r_i = lax.broadcasted_iota(jnp.int32, (BLOCK, BLOCK), 0)
        c_i = lax.broadcasted_iota(jnp.int32, (BLOCK, BLOCK), 1)
        tri = (r_i >= c_i).astype(F32)
        carry = jnp.zeros((1, 128), F32)
        for blk in range(nb):
            rows = slice(blk * BLOCK, (blk + 1) * BLOCK)
            z = f_ref[rows, :] + b_ref[...]
            lf = jnp.minimum(z, 0.0) - jnp.log(1.0 + jnp.exp(-jnp.abs(z)))
            cb = jnp.dot(tri, lf, preferred_element_type=F32, precision=lax.Precision.HIGHEST) + carry
            carry = cb[BLOCK - 1:BLOCK, :]
            cbt = cb.T
            for hh in range(B_HEADS):
                cc_ref[hh, rows, :] = jnp.sum(jnp.where(c_i == hh, cb, 0.0), axis=1, keepdims=True)
                cr_ref[hh, :, rows] = cbt[hh:hh + 1, :]

    return pl.pallas_call(
        body, name=name, grid=(b,),
        in_specs=[pl.BlockSpec((None, l, 128), lambda bi: (bi, 0, 0)),
                  pl.BlockSpec((1, 128), lambda bi: (0, 0))],
        out_specs=[pl.BlockSpec((None, B_HEADS, l, 1), lambda bi: (bi, 0, 0, 0)),
                   pl.BlockSpec((None, B_HEADS, 1, l), lambda bi: (bi, 0, 0, 0))],
        out_shape=[jax.ShapeDtypeStruct((b, B_HEADS, l, 1), F32), jax.ShapeDtypeStruct((b, B_HEADS, 1, l), F32)],
        compiler_params=_params(("parallel",)),
    )(f3, bf_row)


def _fgate_bwd(name, f3, bf_row, dcq, dck):
    b, l, _ = f3.shape
    nb = l // BLOCK

    def body(f_ref, b_ref, dcq_ref, dck_ref, df_ref, db_ref):
        r_i = lax.broadcasted_iota(jnp.int32, (BLOCK, BLOCK), 0)
        c_i = lax.broadcasted_iota(jnp.int32, (BLOCK, BLOCK), 1)
        tri = (r_i <= c_i).astype(F32)
        carry = jnp.zeros((1, 128), F32)
        total = jnp.zeros((1, 128), F32)
        for blk in range(nb - 1, -1, -1):
            rows = slice(blk * BLOCK, (blk + 1) * BLOCK)
            krows = jnp.concatenate([dck_ref[hh, :, rows] for hh in range(B_HEADS)]
                                    + [jnp.zeros((BLOCK - B_HEADS, BLOCK), F32)], axis=0)
            dcb = krows.T
            for hh in range(B_HEADS):
                dcb = dcb + jnp.where(c_i == hh, dcq_ref[hh, rows, :], 0.0)
            rc = jnp.dot(tri, dcb, preferred_element_type=F32, precision=lax.Precision.HIGHEST) + carry
            carry = rc[0:1, :]
            z = f_ref[rows, :] + b_ref[...]
            df = rc * (1.0 / (1.0 + jnp.exp(z)))
            df_ref[rows, :] = df.astype(BF16)
            total = total + jnp.sum(df, axis=0, keepdims=True)

        @pl.when(pl.program_id(0) == 0)
        def _():
            db_ref[...] = total

        @pl.when(pl.program_id(0) > 0)
        def _():
            db_ref[...] += total

    return pl.pallas_call(
        body, name=name, grid=(b,),
        in_specs=[pl.BlockSpec((None, l, 128), lambda bi: (bi, 0, 0)),
                  pl.BlockSpec((1, 128), lambda bi: (0, 0)),
                  pl.BlockSpec((None, B_HEADS, l, 1), lambda bi: (bi, 0, 0, 0)),
                  pl.BlockSpec((None, B_HEADS, 1, l), lambda bi: (bi, 0, 0, 0))],
        out_specs=[pl.BlockSpec((None, l, 128), lambda bi: (bi, 0, 0)), pl.BlockSpec((1, 128), lambda bi: (0, 0))],
        out_shape=[jax.ShapeDtypeStruct((b, l, 128), BF16), jax.ShapeDtypeStruct((1, 128), F32)],
        compiler_params=_params(("arbitrary",)),
    )(f3, bf_row, dcq, dck)


A_Q_BLK = B_SEG // A_WIDTH
A_K_BLK = (B_SEG + A_WIDTH) // 128
A_V_BLK = A_K_BLK + 1
A_SEG_BLK = B_SEG // A_SEG
STACK = A_HEADS * BLOCK


def _lane_lo():
    return lax.broadcasted_iota(jnp.int32, (1, 128), 1) < HEAD_DIM


def _stack_heads(x, masked):
    lo = _lane_lo()
    blks = [x[:, 128 * j:128 * (j + 1)] for j in range(4)]
    if not masked:
        return jnp.concatenate(blks + blks, axis=0)
    zero = jnp.zeros_like(blks[0])
    return jnp.concatenate([jnp.where(lo, bk, zero) for bk in blks] + [jnp.where(lo, zero, bk) for bk in blks], axis=0)


def _unstack_heads(y):
    lo = _lane_lo()
    return jnp.concatenate([jnp.where(lo, y[128 * j:128 * (j + 1)], y[128 * (4 + j):128 * (5 + j)]) for j in range(4)], axis=1)


def _swa_bias(slopes):
    r_i = jnp.arange(STACK)[:, None]
    c_i = jnp.arange(3 * BLOCK)[None, :]
    seg = c_i >> 7
    out = []
    for n in range(3):
        qpos = n * BLOCK + (r_i & (BLOCK - 1))
        kpos = jnp.where(seg == 0, c_i, (n - 2) * BLOCK + c_i)
        dist = qpos - kpos
        band = (seg != 0) & (dist < BLOCK) & (kpos >= PREFIX)
        meta = (seg == 0) & (c_i >= N_PAD)
        out.append(jnp.where((dist >= 0) & (band | meta), -slopes * dist.astype(F32), NEG))
    return jnp.stack(out, axis=0)


def _swa_scores(q, kcat, n, slope, bias):
    s = lax.dot_general(q, kcat, _NT, preferred_element_type=F32) + bias
    further = slope * (-BLOCK * jnp.maximum(n - 2, 0)).astype(F32)
    return jnp.concatenate([s[:, 0:BLOCK] + further, s[:, BLOCK:]], axis=1)


def _swa_specs():
    def kv(col_blk):
        return [pl.BlockSpec((None, BLOCK, 128), lambda b, n: (b, 0, col_blk)),
                pl.BlockSpec((None, BLOCK, 128), lambda b, n: (b, jnp.maximum(n - 1, 0), col_blk)),
                pl.BlockSpec((None, BLOCK, 128), lambda b, n: (b, n, col_blk))]

    q_spec = pl.BlockSpec((None, BLOCK, A_WIDTH), lambda b, n: (b, n, A_Q_BLK))
    o_spec = pl.BlockSpec((None, BLOCK, A_WIDTH), lambda b, n: (b, n, 0))
    col = pl.BlockSpec((STACK, 1), lambda b, n: (0, 0))
    bias = pl.BlockSpec((None, STACK, 3 * BLOCK), lambda b, n: (jnp.minimum(n, 2), 0, 0))
    lse_spec = pl.BlockSpec((None, A_HEADS, BLOCK, 1), lambda b, n: (b, 0, n, 0))
    return q_spec, kv(A_K_BLK), kv(A_V_BLK), o_spec, [col, col, bias], lse_spec


def _swa_fwd(name, qkv, slopes, sinks, bias):
    b, l, _ = qkv.shape
    nb = l // BLOCK

    def body(q_ref, k0_ref, kp_ref, kc_ref, v0_ref, vp_ref, vc_ref, sl_ref, sk_ref, bias_ref, o_ref, lse_ref):
        n = pl.program_id(1)
        qs = _stack_heads(q_ref[...], True) * SCALE
        kcat = jnp.concatenate([k0_ref[...], kp_ref[...], kc_ref[...]], axis=0)
        vcat = jnp.concatenate([v0_ref[...], vp_ref[...], vc_ref[...]], axis=0)
        s = _swa_scores(qs, kcat, n, sl_ref[...], bias_ref[...])
        sink = sk_ref[...]
        m = jnp.maximum(jnp.max(s, axis=-1, keepdims=True), sink)
        p = jnp.exp(s - m)
        den = jnp.sum(p, axis=-1, keepdims=True) + jnp.exp(sink - m)
        o = lax.dot_general(p.astype(BF16), vcat, _NN, preferred_element_type=F32) / den
        o_ref[...] = _unstack_heads(o)
        lse_ref[...] = (m + jnp.log(den)).reshape(A_HEADS, BLOCK, 1)

    q_spec, k_specs, v_specs, o_spec, consts, lse_spec = _swa_specs()
    return pl.pallas_call(
        body, name=name, grid=(b, nb),
        in_specs=[q_spec] + k_specs + v_specs + consts, out_specs=[o_spec, lse_spec],
        out_shape=[jax.ShapeDtypeStruct((b, l, A_WIDTH), F32), jax.ShapeDtypeStruct((b, A_HEADS, l, 1), F32)],
        compiler_params=_params(("parallel", "parallel")),
    )(qkv, qkv, qkv, qkv, qkv, qkv, qkv, slopes, sinks, bias)


def _swa_bwd(name, qkv, o, lse, do, slopes, sinks, bias, dqkv):
    b, l, _ = qkv.shape
    nb = l // BLOCK

    def body(q_ref, k0_ref, kp_ref, kc_ref, v0_ref, vp_ref, vc_ref, o_ref, lse_ref, do_ref, sl_ref, sk_ref, bias_ref, _,
             dx_ref, ds_ref, dk_acc, dv_acc):
        bi = pl.program_id(0)
        n = pl.program_id(1)
        qs = _stack_heads(q_ref[...], True) * SCALE
        dos32 = _stack_heads(do_ref[...], True)
        dos = dos32.astype(BF16)
        os_ = _stack_heads(o_ref[...], False)
        lsev = lse_ref[...].reshape(STACK, 1)
        kcat = jnp.concatenate([k0_ref[...], kp_ref[...], kc_ref[...]], axis=0)
        vcat = jnp.concatenate([v0_ref[...], vp_ref[...], vc_ref[...]], axis=0)
        s = _swa_scores(qs, kcat, n, sl_ref[...], bias_ref[...])
        p = jnp.exp(s - lsev)
        dsum = jnp.sum(dos32 * os_, axis=-1, keepdims=True)
        dp = lax.dot_general(dos, vcat, _NT, preferred_element_type=F32)
        dsc = (p * (dp - dsum)).astype(BF16)
        dq = lax.dot_general(dsc, kcat, _NN, preferred_element_type=F32) * SCALE
        row0 = pl.multiple_of(n * BLOCK, BLOCK)
        dx_ref[pl.ds(row0, BLOCK), 0:A_WIDTH] = _unstack_heads(dq).astype(BF16)
        dkc = lax.dot_general(dsc, qs, _TN, preferred_element_type=F32)
        dvc = lax.dot_general(p.astype(BF16), dos, _TN, preferred_element_type=F32)

        @pl.when(n == 0)
        def _():
            dk_acc[...] = jnp.zeros_like(dk_acc)
            dv_acc[...] = jnp.zeros_like(dv_acc)

        starts = (0, pl.multiple_of(jnp.maximum(n - 1, 0) * BLOCK, BLOCK), row0)
        for t, st in enumerate(starts):
            dk_acc[pl.ds(st, BLOCK), :] += dkc[t * BLOCK:(t + 1) * BLOCK, :]
            dv_acc[pl.ds(st, BLOCK), :] += dvc[t * BLOCK:(t + 1) * BLOCK, :]

        @pl.when(n == nb - 1)
        def _():
            dx_ref[:, A_WIDTH:A_WIDTH + 128] = dk_acc[...].astype(BF16)
            dx_ref[:, A_WIDTH + 128:A_SEG] = dv_acc[...].astype(BF16)

        dsink = -(jnp.exp(sk_ref[...] - lsev) * dsum)
        r8 = lax.broadcasted_iota(jnp.int32, (8, 128), 0)
        acc = jnp.zeros((8, 128), F32)
        for hh in range(A_HEADS):
            acc = acc + jnp.where(r8 == hh, jnp.sum(dsink[hh * BLOCK:(hh + 1) * BLOCK, :]), 0.0)

        @pl.when((bi == 0) & (n == 0))
        def _():
            ds_ref[...] = jnp.zeros_like(ds_ref)

        ds_ref[...] += acc

    q_spec, k_specs, v_specs, o_spec, consts, lse_spec = _swa_specs()
    return pl.pallas_call(
        body, name=name, grid=(b, nb),
        in_specs=[q_spec] + k_specs + v_specs + [o_spec, lse_spec, o_spec] + consts + [pl.BlockSpec(memory_space=pl.ANY)],
        out_specs=[pl.BlockSpec((None, l, A_SEG), lambda bb, n: (bb, 0, A_SEG_BLK)),
                   pl.BlockSpec((8, 128), lambda bb, n: (0, 0))],
        out_shape=[jax.ShapeDtypeStruct(dqkv.shape, BF16), jax.ShapeDtypeStruct((8, 128), F32)],
        scratch_shapes=[pltpu.VMEM((l, 128), F32), pltpu.VMEM((l, 128), F32)],
        input_output_aliases={13: 0},
        compiler_params=_params(("arbitrary", "arbitrary")),
    )(qkv, qkv, qkv, qkv, qkv, qkv, qkv, o, lse, do, slopes, sinks, bias, dqkv)


def _fox_mask(qk, ck, i):
    kh = qk.shape[1]
    qpos = i * BLOCK + lax.broadcasted_iota(jnp.int32, (BLOCK, kh), 0)
    kpos = lax.broadcasted_iota(jnp.int32, (BLOCK, kh), 1)
    return jnp.where((kpos <= qpos) & (kpos >= N_PAD), qk - ck, NEG)


def _pick_head(x, hh):
    lo = _lane_lo()
    return jnp.where(lo if hh == 0 else jnp.logical_not(lo), x, jnp.zeros_like(x))


def _both_heads(x):
    return jnp.concatenate([_pick_head(x, 0), _pick_head(x, 1)], axis=0)


def _fox_specs(l):
    pair = pl.BlockSpec((None, l, PAIR_W), lambda bi, hp: (bi, 0, hp))
    half = pl.BlockSpec((None, l, 128), lambda bi, hp: (bi, 0, hp))
    colv = pl.BlockSpec((None, 2, l, 1), lambda bi, hp: (bi, hp, 0, 0))
    rowv = pl.BlockSpec((None, 2, 1, l), lambda bi, hp: (bi, hp, 0, 0))
    return pair, half, colv, rowv


def _fox_fwd(name, qkv, c_col, c_row):
    b, l, _ = qkv.shape
    nb = l // BLOCK

    def body(x_ref, cc_ref, cr_ref, o_ref, lse_ref):
        for i in range(nb):
            rows = slice(i * BLOCK, (i + 1) * BLOCK)
            kh = (i + 1) * BLOCK
            qblk = x_ref[rows, 0:128]
            kv = x_ref[0:kh, 128:256]
            vv = x_ref[0:kh, 256:384]
            qk = lax.dot_general(_both_heads(qblk) * SCALE, kv, _NT, preferred_element_type=F32)
            ps, dens = [], []
            for hh in range(2):
                s = _fox_mask(qk[hh * BLOCK:(hh + 1) * BLOCK], cr_ref[hh, :, 0:kh], i)
                m = jnp.max(s, axis=-1, keepdims=True)
                p = jnp.exp(s - m)
                den = jnp.sum(p, axis=-1, keepdims=True)
                ps.append(p.astype(BF16))
                dens.append(den)
                lse_ref[hh, rows, :] = (m + jnp.log(den)) + cc_ref[hh, rows, :]
            pv = lax.dot_general(jnp.concatenate(ps, axis=0), vv, _NN, preferred_element_type=F32)
            o_ref[rows, :] = jnp.where(_lane_lo(), pv[0:BLOCK] / dens[0], pv[BLOCK:2 * BLOCK] / dens[1]).astype(BF16)

    pair, half, colv, rowv = _fox_specs(l)
    return pl.pallas_call(
        body, name=name, grid=(b, 4), in_specs=[pair, colv, rowv], out_specs=[half, colv],
        out_shape=[jax.ShapeDtypeStruct((b, l, B_WIDTH), BF16), jax.ShapeDtypeStruct((b, B_HEADS, l, 1), F32)],
        compiler_params=_params(("parallel", "parallel")),
    )(qkv, c_col, c_row)


def _fox_bwd(name, qkv, c_col, c_row, o, lse, do):
    b, l, _ = qkv.shape
    nb = l // BLOCK

    def body(x_ref, cc_ref, cr_ref, o_ref, lse_ref, do_ref, dx_ref, dcq_ref, dck_ref, dk_acc, dv_acc):
        dk_acc[...] = jnp.zeros_like(dk_acc)
        dv_acc[...] = jnp.zeros_like(dv_acc)
        dck_ref[...] = jnp.zeros_like(dck_ref)
        for i in range(nb):
            rows = slice(i * BLOCK, (i + 1) * BLOCK)
            kh = (i + 1) * BLOCK
            qblk = x_ref[rows, 0:128]
            kv = x_ref[0:kh, 128:256]
            vv = x_ref[0:kh, 256:384]
            doblk = do_ref[rows, :]
            ov = o_ref[rows, :].astype(F32)
            q2 = _both_heads(qblk) * SCALE
            do2 = _both_heads(doblk)
            qk = lax.dot_general(q2, kv, _NT, preferred_element_type=F32)
            dp = lax.dot_general(do2, vv, _NT, preferred_element_type=F32)
            ps, dss = [], []
            for hh in range(2):
                half = slice(hh * BLOCK, (hh + 1) * BLOCK)
                s = _fox_mask(qk[half], cr_ref[hh, :, 0:kh], i)
                p = jnp.exp(s - (lse_ref[hh, rows, :] - cc_ref[hh, rows, :]))
                dsum = jnp.sum(do2[half].astype(F32) * ov, axis=-1, keepdims=True)
                ds = p * (dp[half] - dsum)
                ps.append(p.astype(BF16))
                dss.append(ds.astype(BF16))
                dcq_ref[hh, rows, :] = jnp.sum(ds, axis=-1, keepdims=True)
                dck_ref[hh, :, 0:kh] -= jnp.sum(ds, axis=0, keepdims=True)
            p2 = jnp.concatenate(ps, axis=0)
            ds2 = jnp.concatenate(dss, axis=0)
            dq = lax.dot_general(ds2, kv, _NN, preferred_element_type=F32) * SCALE
            dk_acc[0:kh, :] += lax.dot_general(ds2, q2, _TN, preferred_element_type=F32)
            dv_acc[0:kh, :] += lax.dot_general(p2, do2, _TN, preferred_element_type=F32)
            dx_ref[rows, 0:128] = jnp.where(_lane_lo(), dq[0:BLOCK], dq[BLOCK:2 * BLOCK]).astype(BF16)
        dx_ref[:, 128:256] = dk_acc[...].astype(BF16)
        dx_ref[:, 256:384] = dv_acc[...].astype(BF16)

    pair, half, colv, rowv = _fox_specs(l)
    return pl.pallas_call(
        body, name=name, grid=(b, 4), in_specs=[pair, colv, rowv, half, colv, half],
        out_specs=[pair, colv, rowv],
        out_shape=[jax.ShapeDtypeStruct(qkv.shape, BF16), jax.ShapeDtypeStruct((b, B_HEADS, l, 1), F32),
                   jax.ShapeDtypeStruct((b, B_HEADS, 1, l), F32)],
        scratch_shapes=[pltpu.VMEM((l, 128), F32), pltpu.VMEM((l, 128), F32)],
        compiler_params=_params(("parallel", "parallel")),
    )(qkv, c_col, c_row, o, lse, do)


_FLIPS = ((0, 0, 1), (0, 1, 0), (0, 1, 1), (1, 0, 0), (1, 0, 1), (1, 1, 0), (1, 1, 1))


def _peer_table():
    x, y, c = lax.axis_index("x"), lax.axis_index("y"), lax.axis_index("c")
    me = 4 * x + 2 * y + c
    peers = []
    for fx, fy, fc in _FLIPS:
        px = 1 - x if fx else x
        py = 1 - y if fy else y
        pc = 1 - c if fc else c
        peers.append(((px, py, pc), 4 * px + 2 * py + pc))
    return me, peers


_HBM = pl.BlockSpec(memory_space=pltpu.HBM)
_SEM = pl.BlockSpec(memory_space=pltpu.SEMAPHORE)
_ANY = pl.BlockSpec(memory_space=pl.ANY)
_EFFECT = pltpu.SideEffectType.DATAFLOW_SIDE_EFFECTING


def _split_copy(srcs_are_pieces, src_refs, land_refs, send_sem, recv_sem, a, kk, me, peers, arriving):
    dev, lin = peers[kk]
    npeer = len(_FLIPS)
    src = src_refs[a] if srcs_are_pieces[a] else src_refs[a].at[lin]
    dst = land_refs[a].at[lin] if arriving else land_refs[a].at[me]
    return pltpu.make_async_remote_copy(src_ref=src, dst_ref=dst, send_sem=send_sem.at[a * npeer + kk],
                                        recv_sem=recv_sem.at[a * npeer + kk], device_id=dev, device_id_type=MESH_ID)


def _xchg_start(name, gather, scatter, after=None):
    me_out = 4 * lax.axis_index("x") + 2 * lax.axis_index("y") + lax.axis_index("c")
    srcs = list(gather) + list(scatter)
    is_piece = [True] * len(gather) + [False] * len(scatter)
    lands = []
    for a, piece in zip(srcs, is_piece):
        if piece:
            lands.append(lax.dynamic_update_slice(lax.empty((N_DEV,) + tuple(a.shape), a.dtype), a[None],
                                                  (me_out,) + (0,) * a.ndim))
        else:
            lands.append(lax.empty(tuple(a.shape), a.dtype))
    n = len(srcs)
    nsem = n * len(_FLIPS)
    has_after = after is not None

    def body(*refs):
        src_refs = refs[:n]
        land_refs = refs[n:2 * n]
        outs = refs[2 * n + (1 if has_after else 0):]
        send_sem, recv_sem = outs[0], outs[1]
        token = outs[-1]
        me, peers = _peer_table()
        for kk in range(len(_FLIPS)):
            for a in range(n):
                _split_copy(is_piece, src_refs, land_refs, send_sem, recv_sem, a, kk, me, peers, False).start()
        token[...] = jnp.zeros_like(token)

    out_shape = ([pltpu.SemaphoreType.DMA((nsem,)), pltpu.SemaphoreType.DMA((nsem,))]
                 + [pltpu.HBM(tuple(a.shape), a.dtype) for a in srcs] + [pltpu.HBM(tuple(a.shape), a.dtype) for a in lands]
                 + [jax.ShapeDtypeStruct((8, 128), F32)])
    args = [pltpu.with_memory_space_constraint(a, pltpu.HBM) for a in srcs + lands] + ([after] if has_after else [])
    res = pl.pallas_call(
        body, name=name, out_shape=out_shape,
        in_specs=[_HBM] * (2 * n) + ([_ANY] if has_after else []),
        out_specs=[_SEM, _SEM] + [_HBM] * (2 * n) + [pl.BlockSpec(memory_space=pltpu.VMEM)],
        input_output_aliases={i: 2 + i for i in range(2 * n)},
        compiler_params=pltpu.CompilerParams(has_side_effects=_EFFECT),
    )(*args)
    state = (res[0], res[1], list(res[2:2 + n]), list(res[2 + n:2 + 2 * n]), is_piece)
    return state, res[-1]


def _xchg_wait(name, state, after):
    send_sem, recv_sem, srcs, lands, is_piece = state
    n = len(srcs)

    def body(*refs):
        src_refs = refs[:n]
        land_refs = refs[n:2 * n]
        s_sem, r_sem = refs[2 * n], refs[2 * n + 1]
        me, peers = _peer_table()
        for kk in range(len(_FLIPS)):
            for a in range(n):
                cp = _split_copy(is_piece, src_refs, land_refs, s_sem, r_sem, a, kk, me, peers, True)
                cp.wait_send()
                cp.wait_recv()

    out_shape = [pltpu.HBM(tuple(a.shape), a.dtype) for a in srcs] + [pltpu.HBM(tuple(a.shape), a.dtype) for a in lands]
    res = pl.pallas_call(
        body, name=name, out_shape=out_shape,
        in_specs=[_HBM] * (2 * n) + [_SEM, _SEM, _ANY], out_specs=[_HBM] * (2 * n),
        input_output_aliases={i: i for i in range(2 * n)},
        compiler_params=pltpu.CompilerParams(has_side_effects=_EFFECT),
    )(*srcs, *lands, send_sem, recv_sem, after)
    return list(res[n:]), list(res[:n])


_SIB = (0, 0, 1)
_ICI = ((0, 1, 0), (1, 0, 0), (1, 1, 0))


def _flip(fl):
    x, y, c = lax.axis_index("x"), lax.axis_index("y"), lax.axis_index("c")
    px = 1 - x if fl[0] else x
    py = 1 - y if fl[1] else y
    pc = 1 - c if fl[2] else c
    return (px, py, pc), 4 * px + 2 * py + pc


def _gather2_start(name, pieces, after=None):
    me_out = 4 * lax.axis_index("x") + 2 * lax.axis_index("y") + lax.axis_index("c")
    pieces = list(pieces)
    n = len(pieces)
    lands = [lax.dynamic_update_slice(lax.empty((N_DEV,) + tuple(a.shape), a.dtype), a[None],
                                      (me_out,) + (0,) * a.ndim) for a in pieces]
    first = (_SIB,) + _ICI
    has_after = after is not None

    def body(*refs):
        src_refs, land_refs = refs[:n], refs[n:2 * n]
        outs = refs[2 * n + (1 if has_after else 0):]
        send_sem, recv_sem, token = outs[0], outs[1], outs[-1]
        _, me = _flip((0, 0, 0))
        for kk, fl in enumerate(first):
            dev, _ = _flip(fl)
            for a in range(n):
                pltpu.make_async_remote_copy(src_ref=src_refs[a], dst_ref=land_refs[a].at[me],
                                             send_sem=send_sem.at[a * 4 + kk], recv_sem=recv_sem.at[a * 4 + kk],
                                             device_id=dev, device_id_type=MESH_ID).start()
        token[...] = jnp.zeros_like(token)

    hbm = [pltpu.HBM(tuple(a.shape), a.dtype) for a in pieces + lands]
    res = pl.pallas_call(
        body, name=name,
        out_shape=[pltpu.SemaphoreType.DMA((4 * n,)), pltpu.SemaphoreType.DMA((4 * n,))] + hbm
        + [jax.ShapeDtypeStruct((8, 128), F32)],
        in_specs=[_HBM] * (2 * n) + ([_ANY] if has_after else []),
        out_specs=[_SEM, _SEM] + [_HBM] * (2 * n) + [pl.BlockSpec(memory_space=pltpu.VMEM)],
        input_output_aliases={i: 2 + i for i in range(2 * n)},
        compiler_params=pltpu.CompilerParams(has_side_effects=_EFFECT),
    )(*([pltpu.with_memory_space_constraint(a, pltpu.HBM) for a in pieces + lands] + ([after] if has_after else [])))
    return (res[0], res[1], list(res[2:2 + n]), list(res[2 + n:2 + 2 * n])), res[-1]


def _gather2_forward(name, state, after):
    send_a, recv_a, pieces, lands = state
    n = len(pieces)
    first = (_SIB,) + _ICI

    def body(*refs):
        src_refs, land_refs = refs[:n], refs[n:2 * n]
        s_a, r_a = refs[2 * n], refs[2 * n + 1]
        outs = refs[2 * n + 3:]
        send_b, recv_b, token = outs[0], outs[1], outs[-1]
        for kk, fl in enumerate(first):
            dev, lin = _flip(fl)
            for a in range(n):
                cp = pltpu.make_async_remote_copy(src_ref=src_refs[a], dst_ref=land_refs[a].at[lin],
                                                  send_sem=s_a.at[a * 4 + kk], recv_sem=r_a.at[a * 4 + kk],
                                                  device_id=dev, device_id_type=MESH_ID)
                cp.wait_send()
                cp.wait_recv()
        sib, _ = _flip(_SIB)
        for j, fl in enumerate(_ICI):
            _, lin = _flip(fl)
            for a in range(n):
                pltpu.make_async_remote_copy(src_ref=land_refs[a].at[lin], dst_ref=land_refs[a].at[lin],
                                             send_sem=send_b.at[a * 3 + j], recv_sem=recv_b.at[a * 3 + j],
                                             device_id=sib, device_id_type=MESH_ID).start()
        token[...] = jnp.zeros_like(token)

    hbm = [pltpu.HBM(tuple(a.shape), a.dtype) for a in pieces + lands]
    res = pl.pallas_call(
        body, name=name,
        out_shape=[pltpu.SemaphoreType.DMA((3 * n,)), pltpu.SemaphoreType.DMA((3 * n,))] + hbm
        + [jax.ShapeDtypeStruct((8, 128), F32)],
        in_specs=[_HBM] * (2 * n) + [_SEM, _SEM, _ANY],
        out_specs=[_SEM, _SEM] + [_HBM] * (2 * n) + [pl.BlockSpec(memory_space=pltpu.VMEM)],
        input_output_aliases={i: 2 + i for i in range(2 * n)},
        compiler_params=pltpu.CompilerParams(has_side_effects=_EFFECT),
    )(*pieces, *lands, send_a, recv_a, after)
    return (res[0], res[1], list(res[2 + n:2 + 2 * n])), res[-1]


def _gather2_wait(name, state, after):
    send_b, recv_b, lands = state
    n = len(lands)

    def body(*refs):
        land_refs = refs[:n]
        s_b, r_b = refs[n], refs[n + 1]
        sib, _ = _flip(_SIB)
        for j, fl in enumerate(_ICI):
            _, sent = _flip(fl)
            _, arriving = _flip((fl[0], fl[1], 1))
            for a in range(n):
                cp = pltpu.make_async_remote_copy(src_ref=land_refs[a].at[sent], dst_ref=land_refs[a].at[arriving],
                                                  send_sem=s_b.at[a * 3 + j], recv_sem=r_b.at[a * 3 + j],
                                                  device_id=sib, device_id_type=MESH_ID)
                cp.wait_send()
                cp.wait_recv()

    res = pl.pallas_call(
        body, name=name, out_shape=[pltpu.HBM(tuple(a.shape), a.dtype) for a in lands],
        in_specs=[_HBM] * n + [_SEM, _SEM, _ANY], out_specs=[_HBM] * n,
        input_output_aliases={i: i for i in range(n)},
        compiler_params=pltpu.CompilerParams(has_side_effects=_EFFECT),
    )(*lands, send_b, recv_b, after)
    return list(res)


def _adam_math(w, g, m, v):
    m = ADAM_B1 * m + (1.0 - ADAM_B1) * g
    v = ADAM_B2 * v + (1.0 - ADAM_B2) * (g * g)
    m_hat = m / (1.0 - ADAM_B1 ** ADAM_STEP)
    v_hat = v / (1.0 - ADAM_B2 ** ADAM_STEP)
    delta = -ADAM_LR * (m_hat / (jnp.sqrt(v_hat) + ADAM_EPS) + ADAM_WD * w)
    return delta, m, v


def _adam(name, w, m, v, parts, transposed=False, dep=None, own=None):
    npart, _, cp = parts.shape
    has_dep = dep is not None
    has_own = own is not None
    if transposed:
        c, r = w.shape
        tr = _pick(r, (256, 128))
        blk = pl.BlockSpec((c, tr), lambda i, *_: (0, i))
    else:
        r, c = w.shape
        tr = _pick(r, (256, 176, 128, 64, 16, 8, 1))
        blk = pl.BlockSpec((tr, c), lambda i, *_: (i, 0))

    def body(*refs):
        refs = list(refs)
        me = refs.pop(0)[0] if has_own else None
        w_ref, m_ref, v_ref, p_ref = refs[:4]
        own_ref = refs[4] if has_own else None
        g_ref, d_ref, mo_ref, vo_ref = refs[-4:]
        g = None
        for pp in range(npart):
            part = p_ref[pp] if not has_own else jnp.where(me == pp, own_ref[...], p_ref[pp])
            g = part.astype(F32) if g is None else g + part.astype(F32)
        g = g.T[0:c, :] if transposed else g[:, 0:c]
        delta, mn, vn = _adam_math(w_ref[...], g, m_ref[...], v_ref[...])
        g_ref[...] = g
        d_ref[...] = delta
        mo_ref[...] = mn
        vo_ref[...] = vn

    in_specs = [blk, blk, blk, pl.BlockSpec((npart, tr, cp), lambda i, *_: (0, i, 0))]
    args = [w, m, v, parts]
    if has_own:
        in_specs.append(pl.BlockSpec((None, tr, cp), lambda i, me_ref: (me_ref[0], i, 0)))
        args.append(own[0])
    if has_dep:
        in_specs.append(pl.BlockSpec(memory_space=pl.ANY))
        args.append(dep)
    out = jax.ShapeDtypeStruct(w.shape, F32)
    grid_spec = pltpu.PrefetchScalarGridSpec(num_scalar_prefetch=1 if has_own else 0, grid=(r // tr,),
                                             in_specs=in_specs, out_specs=[blk, blk, blk, blk])
    return pl.pallas_call(
        body, name=name, grid_spec=grid_spec, out_shape=[out, out, out, out], compiler_params=_params(("parallel",)),
    )(*(([own[1]] if has_own else []) + args))


def _small_update(name, packs, wmv):
    nparam = len(wmv)

    def body(p_ref, *refs):
        ins, outs = refs[:3 * nparam], refs[3 * nparam:]
        tot = p_ref[0]
        for pp in range(1, N_DEV):
            tot = tot + p_ref[pp]
        outs[0][0:8, :] = tot[0:8, :]
        outs[0][8:24, :] = tot[8:24, :] + tot[24:40, :]
        grads = [tot[i:i + 1, :] for i in range(4)] + [tot[4:5, 0:B_HEADS], tot[4:5, B_HEADS:2 * B_HEADS]]
        for i, g in enumerate(grads):
            w_ref, m_ref, v_ref = ins[3 * i:3 * i + 3]
            delta, mn, vn = _adam_math(w_ref[...], g, m_ref[...], v_ref[...])
            for o_ref, val in zip(outs[1 + 4 * i:5 + 4 * i], (g, delta, mn, vn)):
                o_ref[...] = val

    flat = [a for trio in wmv for a in trio]
    out_shape = [jax.ShapeDtypeStruct((24, D_MODEL), F32)]
    for w, _, _ in wmv:
        out_shape += [jax.ShapeDtypeStruct(w.shape, F32)] * 4
    res = pl.pallas_call(body, name=name, out_shape=out_shape, compiler_params=_params())(packs, *flat)
    return res[0], [tuple(res[1 + 4 * i:5 + 4 * i]) for i in range(nparam)]


def _local_step(x, tgt, g1, gm, g2, gf, b_forget, sinks, weights, send):
    b, s, _ = x.shape
    l = s + PREFIX
    t = b * l
    (meta,) = weights("meta", x)

    h0, n1 = _embed_norm("embed_rms1_fwd", x, meta, g1)
    (w1i,) = weights("ffn1_in", n1)
    gu1, a1 = _ffn_in_fwd("ffn1_in_fwd", n1, w1i)
    (w1o,) = weights("ffn1_out", weights("ffn1_out:forward", a1))
    h1, um = _mm_res_norm("ffn1_out_fwd", a1, w1o, h0, gm, alpha=0.5)
    wi, wa, wb, wo = weights("mix", weights("mix:forward", um))
    qkv, gates, f2 = _proj_fwd("proj_fwd", um, wi)
    qkv3 = qkv.reshape(b, l, QKV_W)
    f3 = f2.reshape(b, l, 128)
    bf_row = jnp.pad(b_forget, ((0, 0), (0, 128 - B_HEADS)))
    c_col, c_row = _fgate_fwd("fgate_fwd", f3, bf_row)
    head_of_row = jnp.arange(STACK) // BLOCK
    slopes = jnp.exp2(-8.0 * (head_of_row + 1).astype(F32) / A_HEADS).reshape(STACK, 1)
    sink_rows = jnp.repeat(sinks.reshape(A_HEADS), BLOCK).reshape(STACK, 1)
    swa_bias = _swa_bias(slopes)
    oa3, lse_a = _swa_fwd("swa_fwd", qkv3, slopes, sink_rows, swa_bias)
    ob3, lse_b = _fox_fwd("fox_fwd", qkv3, c_col, c_row)
    oa = oa3.reshape(t, A_WIDTH)
    ob = ob3.reshape(t, B_WIDTH)
    mixed, ya, yb = _branch_gate_fwd("branch_gate_fwd", oa, ob, wa, wb, gates, dep=weights("ffn2:forward", ob))
    h2, n2 = _mm_res_norm("mix_out_fwd", mixed, wo, h1, g2)
    w2i, w2o = weights("ffn2", h2)
    gu2, a2 = _ffn_in_fwd("ffn2_in_fwd", n2, w2i)

    dh3, loss_blk, dgf = _ffn_out_loss("ffn2_out_loss", a2, w2o, h2, gf, tgt, 0.5)

    def ffn_bwd(tag, dh, h_in, g_norm, n_in, gu, a, w_in_blk, w_out, one_send, examples=None):
        dw_out = _ffn_out_bwd_w(tag + "_out_bwd_w", a, dh)
        dgu = _ffn_out_bwd_x(tag + "_out_bwd_x", dh, w_out, gu, dep=None if one_send else send(tag + "_out", (dw_out,)))
        dw_in = _ffn_in_bwd_w(tag + "_in_bwd_w", n_in, dgu)
        token = send(tag, (dw_in, dw_out)) if one_send else send(tag + "_in", (dw_in,))
        return _ffn_in_bwd_x(tag + "_in_bwd_x", dgu, w_in_blk, h_in, g_norm, dh, dep=token, examples=examples)

    dh2, dg2 = ffn_bwd("ffn2", dh3, h2, g2, n2, gu2, a2, w2i, w2o, True)

    dwo = _mm_tn("mix_out_bwd_w", mixed, dh2)
    dya, dyb, dgates = _mix_out_gate_bwd("mix_out_gate_bwd", dh2, wo, gates, ya, yb)
    doa, dob = _branch_bwd_x("branch_bwd_x", dya, dyb, wa, wb)
    dwa, dwb = _branch_bwd_w("branch_bwd_w", oa, ob, dya, dyb)
    dqkv3, dcq, dck = _fox_bwd("fox_bwd", qkv3, c_col, c_row, ob3, lse_b, dob.reshape(b, l, B_WIDTH))
    dqkv3, dsink = _swa_bwd("swa_bwd", qkv3, oa3, lse_a, doa.reshape(b, l, A_WIDTH), slopes, sink_rows, swa_bias, dqkv3)
    dqkv = dqkv3.reshape(t, QKV_W)
    df3, dbf = _fgate_bwd("fgate_bwd", f3, bf_row, dcq, dck)
    df = df3.reshape(t, 128)
    dwi_qkv = _mm_tn("proj_qkv_bwd_w", um, dqkv, tm_c=(512,), tn_c=(768,))
    dwi_g = _mm_tn("proj_gates_bwd_w", um, dgates, tm_c=(512,), tn_c=(512,))
    dwi_f = _mm_tn("proj_f_bwd_w", um, df, tm_c=(512,), tn_c=(128,))
    token = send("mix", (dwi_qkv, dwi_g, dwi_f, dwa, dwb, dwo))
    dh1, dgm = _proj_bwd_x("proj_bwd_x", dqkv, dgates, df, wi, h1, gm, dh2, dep=token)

    grad_x, dmeta3, dg1 = ffn_bwd("ffn1", dh1, h0, g1, n1, gu1, a1, w1i, w1o, False, examples=b)
    dmeta = dmeta3.reshape(b * N_META, D_MODEL)

    misc = jnp.concatenate([dbf[:, 0:B_HEADS], dsink[:, 0].reshape(1, A_HEADS), loss_blk[0:1, 0:1]], axis=1)
    misc = jnp.pad(misc, ((0, 0), (0, D_MODEL - misc.shape[1])))
    row = lax.broadcasted_iota(jnp.int32, (8, D_MODEL), 0)
    vec = jnp.zeros((8, D_MODEL), F32)
    for i, piece in enumerate((dg1, dgm, dg2, dgf, misc)):
        vec = jnp.where(row == i, piece, vec)
    small = jnp.concatenate([vec, dmeta], axis=0)
    return grad_x, small


def _pad_to(a, rows, cols):
    return jnp.pad(a, ((0, rows - a.shape[0]), (0, cols - a.shape[1])))


def _ffn_out_from_gathered(name, g):
    def body(g_ref, o_ref):
        o_ref[0:FFO_SHARD, :] = g_ref[0]
        o_ref[FFO_SHARD:FF_SHARD, :] = g_ref[1]
        o_ref[FF_SHARD:FF_SHARD_P, :] = jnp.zeros((FF_SHARD_P - FF_SHARD, D_MODEL), BF16)

    return pl.pallas_call(
        body, name=name, grid=(4,),
        in_specs=[pl.BlockSpec((2, FFO_SHARD, D_MODEL), lambda j: (j, 0, 0))],
        out_specs=pl.BlockSpec((FF_SHARD_P, D_MODEL), lambda j: (j, 0)),
        out_shape=jax.ShapeDtypeStruct((D_FF_P, D_MODEL), BF16), compiler_params=_params(("parallel",)),
    )(g)


def _ffn_out_bwd_w(name, a, dh):
    t = a.shape[0]
    tn = D_MODEL // 2

    def body(a_ref, b_ref, o_ref):
        acc = lax.dot_general(a_ref[...], _bf(b_ref[...]), _TN, preferred_element_type=F32) * 0.5
        o_ref[0] = acc[0:FFO_SHARD].astype(BF16)
        o_ref[1] = acc[FFO_SHARD:FF_SHARD].astype(BF16)

    return pl.pallas_call(
        body, name=name, grid=(2, 4),
        in_specs=[pl.BlockSpec((t, FF_SHARD_P), lambda j, i: (0, i)), pl.BlockSpec((t, tn), lambda j, i: (0, j))],
        out_specs=pl.BlockSpec((2, FFO_SHARD, tn), lambda j, i: (i, 0, j)),
        out_shape=jax.ShapeDtypeStruct((N_DEV, FFO_SHARD, D_MODEL), BF16), compiler_params=_params(("parallel", "parallel")),
    )(a, dh)


def _proj_segments():
    segs = [(HEAD_DIM * h, HEAD_DIM, 0, B_SEG + HEAD_DIM * A_HEAD_ORDER.index(h)) for h in range(A_HEADS)]
    segs += [(512, 128, 0, B_SEG + A_WIDTH), (640, 128, 0, B_SEG + A_WIDTH + 128)]
    for first, off in ((768, 0), (1280, 128), (1792, 256)):
        segs += [(first + 128 * hp, 128, 0, PAIR_W * hp + off) for hp in range(4)]
    segs += [(2304, B_HEADS, 2, 0), (2312, 2 * D_MODEL, 1, 0)]
    return segs


_RELAYOUT_ROWS = 256


def _proj_from_gathered(name, g):
    rows = _RELAYOUT_ROWS

    def body(g_ref, o_ref):
        def cols(first, width):
            out = []
            for p in range(N_DEV):
                lo, hi = max(first, WIN_SHARD * p), min(first + width, WIN_SHARD * (p + 1))
                if lo < hi:
                    out.append(g_ref[p, :, lo - WIN_SHARD * p:hi - WIN_SHARD * p])
            return out

        parts = []
        for arr in (0, 1, 2):
            for first, width, _, _ in sorted((s for s in _proj_segments() if s[2] == arr), key=lambda s: s[3]):
                parts += cols(first, width)
            if arr == 0:
                parts.append(jnp.zeros((rows, P_GATES - QKV_W), BF16))
        parts.append(jnp.zeros((rows, 128 - B_HEADS), BF16))
        o_ref[...] = jnp.concatenate(parts, axis=1)

    return pl.pallas_call(
        body, name=name, grid=(D_MODEL // rows,),
        in_specs=[pl.BlockSpec((N_DEV, rows, WIN_SHARD_P), lambda i: (0, i, 0))],
        out_specs=pl.BlockSpec((rows, PROJ_P), lambda i: (i, 0)),
        out_shape=jax.ShapeDtypeStruct((D_MODEL, PROJ_P), BF16), compiler_params=_params(("parallel",)),
    )(g)


def _proj_to_scatter(name, dqkv_w, dg_w, df_w):
    rows = _RELAYOUT_ROWS
    segs = sorted(_proj_segments())

    def body(q_ref, g_ref, f_ref, o_ref):
        arrays = (q_ref, g_ref, f_ref)
        for p in range(N_DEV):
            parts = []
            for first, width, arr, at in segs:
                lo, hi = max(first, WIN_SHARD * p), min(first + width, WIN_SHARD * (p + 1))
                if lo < hi:
                    parts.append(arrays[arr][:, at + lo - first:at + hi - first])
            parts.append(jnp.zeros((rows, WIN_SHARD_P - WIN_SHARD), BF16))
            o_ref[p] = jnp.concatenate(parts, axis=1)

    return pl.pallas_call(
        body, name=name, grid=(D_MODEL // rows,),
        in_specs=[pl.BlockSpec((rows, QKV_W), lambda i: (i, 0)), pl.BlockSpec((rows, 2 * D_MODEL), lambda i: (i, 0)),
                  pl.BlockSpec((rows, 128), lambda i: (i, 0))],
        out_specs=pl.BlockSpec((N_DEV, rows, WIN_SHARD_P), lambda i: (0, i, 0)),
        out_shape=jax.ShapeDtypeStruct((N_DEV, D_MODEL, WIN_SHARD_P), BF16), compiler_params=_params(("parallel",)),
    )(dqkv_w, dg_w, df_w)


def _a_rows_from_natural(w):
    return jnp.concatenate([w[HEAD_DIM * h:HEAD_DIM * (h + 1)] for h in A_HEAD_ORDER], axis=0)


def _a_rows_to_natural(w):
    return jnp.concatenate([w[HEAD_DIM * A_HEAD_ORDER.index(h):HEAD_DIM * (A_HEAD_ORDER.index(h) + 1)]
                            for h in range(A_HEADS)], axis=0)


def kernel(x, meta_tokens, ffn1_norm, ffn1_w_in, ffn1_w_out, mix_norm, w_in, b_forget, attn_sinks, w_branch_a, w_branch_b, w_out, ffn2_norm, ffn2_w_in, ffn2_w_out, final_norm, loss_target, m_meta_tokens, m_ffn1_norm, m_ffn1_w_in, m_ffn1_w_out, m_mix_norm, m_w_in, m_b_forget, m_attn_sinks, m_w_branch_a, m_w_branch_b, m_w_out, m_ffn2_norm, m_ffn2_w_in, m_ffn2_w_out, m_final_norm, v_meta_tokens, v_ffn1_norm, v_ffn1_w_in, v_ffn1_w_out, v_mix_norm, v_w_in, v_b_forget, v_attn_sinks, v_w_branch_a, v_w_branch_b, v_w_out, v_ffn2_norm, v_ffn2_w_in, v_ffn2_w_out, v_final_norm):
    me = 4 * lax.axis_index("x") + 2 * lax.axis_index("y") + lax.axis_index("c")

    def shard(w, tok, rows=None, cols=None):
        piece = (w[0] if tok is None else w[0] + tok[0, 0]).astype(BF16)
        return piece if rows is None else _pad_to(piece, rows, cols)

    first_level, second_level = {}, {}
    first_level["meta"], tok = _gather2_start("gather_meta_start", (meta_tokens,))
    first_level["ffn1_in"], tok = _gather2_start(
        "gather_ffn1_in_start", (shard(ffn1_w_in, None, D_MODEL, FF_SHARD_P),), after=tok)
    first_level["ffn1_out"], tok = _gather2_start("gather_ffn1_out_start", (shard(ffn1_w_out, tok),), after=tok)
    first_level["mix"], tok = _gather2_start(
        "gather_mix_start", (shard(w_in, tok, D_MODEL, WIN_SHARD_P), shard(w_branch_a, tok), shard(w_branch_b, tok),
                             shard(w_out, tok)), after=tok)
    first_level["ffn2"], tok = _gather2_start(
        "gather_ffn2_start", (shard(ffn2_w_in, tok, D_MODEL, FF_SHARD_P), shard(ffn2_w_out, tok)), after=tok)
    started = {"tok": tok}

    def weights(group, after):
        if group.endswith(":forward"):
            group = group[:-len(":forward")]
            second_level[group], token = _gather2_forward("gather_" + group + "_forward", first_level[group], after)
            return token
        if group == "meta":
            after = weights("meta:forward", started["tok"])
        if group == "ffn1_in":
            after = weights("ffn1_in:forward", after)
        got = _gather2_wait("gather_" + group + "_wait", second_level[group], after)
        if group == "mix":
            gwi, gwa, gwb, gwo = got
            return (_proj_from_gathered("proj_w_relayout", gwi), _a_rows_from_natural(gwa.transpose(1, 0, 2).reshape(A_WIDTH, D_MODEL)),
                    gwb.transpose(1, 0, 2).reshape(B_WIDTH, D_MODEL), gwo.reshape(D_MODEL, D_MODEL))
        if group == "ffn1_out":
            return (_ffn_out_from_gathered("ffn1_w_out_relayout", got[0]),)
        if group == "meta":
            return (got[0].transpose(1, 0, 2).reshape(N_META, D_MODEL),)
        if group == "ffn1_in":
            return (got[0].reshape(2, 4, D_MODEL, FF_SHARD_P),)
        return got[0].reshape(2, 4, D_MODEL, FF_SHARD_P), _ffn_out_from_gathered("ffn2_w_out_relayout", got[1])

    scatter_state = {}

    def send(group, grads):
        if group == "mix":
            dwi_qkv, dwi_g, dwi_f, dwa, dwb, dwo = grads
            dwa = _a_rows_to_natural(dwa)
            blocks = (_proj_to_scatter("proj_dw_relayout", dwi_qkv, dwi_g, dwi_f), dwa.reshape(A_WIDTH, N_DEV, 128).transpose(1, 0, 2),
                      dwb.reshape(B_WIDTH, N_DEV, 128).transpose(1, 0, 2), dwo.reshape(N_DEV, 128, D_MODEL))
        elif group.endswith("_in"):
            blocks = (grads[0].reshape(N_DEV, D_MODEL, FF_SHARD_P),)
        elif group.endswith("_out"):
            blocks = (grads[0],)
        else:
            blocks = (grads[0].reshape(N_DEV, D_MODEL, FF_SHARD_P), grads[1])
        scatter_state[group], token = _xchg_start("scatter_" + group + "_start", (), blocks)
        return token

    gf = final_norm.reshape(1, D_MODEL)
    grad_x, small = _local_step(x, loss_target, ffn1_norm, mix_norm, ffn2_norm, gf, b_forget, attn_sinks, weights, send)

    small_state, after = _xchg_start("gather_small_start", (small,), ())
    out = {}
    updates = (
        ("ffn2", (("ffn2_w_in", ffn2_w_in, m_ffn2_w_in, v_ffn2_w_in), ("ffn2_w_out", ffn2_w_out, m_ffn2_w_out, v_ffn2_w_out))),
        ("ffn1_out", (("ffn1_w_out", ffn1_w_out, m_ffn1_w_out, v_ffn1_w_out),)),
        ("mix", (("w_in", w_in, m_w_in, v_w_in), ("w_branch_a", w_branch_a, m_w_branch_a, v_w_branch_a),
                 ("w_branch_b", w_branch_b, m_w_branch_b, v_w_branch_b), ("w_out", w_out, m_w_out, v_w_out))),
    )
    last_update = ("ffn1_in", (("ffn1_w_in", ffn1_w_in, m_ffn1_w_in, v_ffn1_w_in),))

    def update(group, members, after):
        parts_list, blocks_list = _xchg_wait("scatter_" + group + "_wait", scatter_state[group], after)
        prev = None
        for (nm, w, m, v), parts, blocks in zip(members, parts_list, blocks_list):
            if nm.endswith("w_in"):
                res4 = _adam("adam_" + nm, w[0].T, m[0].T, v[0].T, parts, transposed=True, dep=prev, own=(blocks, me_arr))
                out[nm] = tuple(r.T[None] for r in res4)
            else:
                res4 = _adam("adam_" + nm, w[0], m[0], v[0], parts, dep=prev, own=(blocks, me_arr))
                out[nm] = tuple(r[None] for r in res4)
            prev = res4[0]
        return prev

    me_arr = me.reshape(1).astype(jnp.int32)
    for group, members in updates + (last_update,):
        after = update(group, members, after)
    (packs,), _ = _xchg_wait("gather_small_wait", small_state, after)

    row = lambda a: a.reshape(1, D_MODEL)
    tot, small_res = _small_update("small_update", packs, (
        (ffn1_norm, m_ffn1_norm, v_ffn1_norm), (mix_norm, m_mix_norm, v_mix_norm), (ffn2_norm, m_ffn2_norm, v_ffn2_norm),
        (row(final_norm), row(m_final_norm), row(v_final_norm)), (b_forget, m_b_forget, v_b_forget),
        (attn_sinks, m_attn_sinks, v_attn_sinks)))
    for nm, res4 in zip(("ffn1_norm", "mix_norm", "ffn2_norm", "final_norm", "b_forget", "attn_sinks"), small_res):
        out[nm] = tuple(r.reshape(D_MODEL) for r in res4) if nm == "final_norm" else res4
    loss = tot[4, 2 * B_HEADS]
    g_meta = lax.dynamic_slice(tot[8:24, :], (0, me * 128), (N_META, 128))
    out["meta_tokens"] = tuple(_adam("adam_meta_tokens", meta_tokens, m_meta_tokens, v_meta_tokens, g_meta[None]))

    names = ("meta_tokens", "ffn1_norm", "ffn1_w_in", "ffn1_w_out", "mix_norm", "w_in", "b_forget", "attn_sinks",
             "w_branch_a", "w_branch_b", "w_out", "ffn2_norm", "ffn2_w_in", "ffn2_w_out", "final_norm")
    return (loss, grad_x) + tuple(out[nm][kind] for kind in range(4) for nm in names)
```

```python
import jax
import jax.numpy as jnp
from jax import lax
from jax.experimental import pallas as pl
from jax.experimental.pallas import tpu as pltpu

F32 = jnp.float32
BF16 = jnp.bfloat16

D_MODEL = 1024
N_META = 16
BLOCK = 128
PREFIX = 128
N_PAD = PREFIX - N_META
HEAD_DIM = 64
A_HEADS = 8
B_HEADS = 8
A_WIDTH = 512
A_KV_WIDTH = 128
B_WIDTH = 512
D_FF = 2816
N_DEV = 8
FF_SHARD = 2 * D_FF // N_DEV
FF_SHARD_P = 768
FFO_SHARD = D_FF // N_DEV
D_FF_P = 4 * FF_SHARD_P
W_IN_COLS = 4360
WIN_SHARD = W_IN_COLS // N_DEV
WIN_SHARD_P = 640
PAIR_W = 3 * 128
B_SEG = 4 * PAIR_W
A_SEG = A_WIDTH + 2 * A_KV_WIDTH
QKV_W = B_SEG + A_SEG
P_GATES = 2 * (2 * D_MODEL)
P_F = P_GATES + 2 * D_MODEL
PROJ_P = P_F + 128
A_HEAD_ORDER = (0, 4, 1, 5, 2, 6, 3, 7)
EPS = 1e-6
NEG = -1e30
SCALE = HEAD_DIM ** -0.5
ADAM_LR = 0.001
ADAM_B1 = 0.9
ADAM_B2 = 0.999
ADAM_EPS = 1e-08
ADAM_WD = 0.01
ADAM_STEP = 10
VMEM_LIMIT = 56 * 1024 * 1024
MESH_ID = pl.DeviceIdType.MESH
SMALL_ROWS = 40

_NN = (((1,), (0,)), ((), ()))
_NT = (((1,), (1,)), ((), ()))
_TN = (((0,), (0,)), ((), ()))


def _params(sem=None):
    return pltpu.CompilerParams(dimension_semantics=sem, vmem_limit_bytes=VMEM_LIMIT)


def _pick(n, cands):
    for c in cands:
        if n % c == 0:
            return c
    raise ValueError(f"no tile for {n}")


def _bf(v):
    return v if v.dtype == BF16 else v.astype(BF16)


def _mm(name, a, b, dims, grid, a_spec, b_spec, o_spec, out_shape, out_dtype, alpha=1.0):
    def body(a_ref, b_ref, o_ref):
        acc = lax.dot_general(_bf(a_ref[...]), _bf(b_ref[...]), dims, preferred_element_type=F32)
        if alpha != 1.0:
            acc = acc * alpha
        o_ref[...] = acc.astype(o_ref.dtype)

    return pl.pallas_call(
        body, name=name, grid=grid, in_specs=[a_spec, b_spec], out_specs=o_spec,
        out_shape=jax.ShapeDtypeStruct(out_shape, out_dtype),
        compiler_params=_params(("parallel",) * len(grid)),
    )(a, b)


def _mm_res_norm(name, a, w, res, g_next, alpha=1.0):
    t, k = a.shape
    tm = _pick(t, (544, 384, 256, 128))

    def body(a_ref, w_ref, r_ref, g_ref, h_ref, n_ref):
        acc = lax.dot_general(_bf(a_ref[...]), w_ref[...], _NN, preferred_element_type=F32)
        if alpha != 1.0:
            acc = acc * alpha
        hv = acc + r_ref[...]
        h_ref[...] = hv
        r = lax.rsqrt(jnp.mean(hv * hv, axis=-1, keepdims=True) + EPS)
        n_ref[...] = ((hv * r) * g_ref[...]).astype(BF16)

    row = pl.BlockSpec((tm, D_MODEL), lambda i: (i, 0))
    return pl.pallas_call(
        body, name=name, grid=(t // tm,),
        in_specs=[pl.BlockSpec((tm, k), lambda i: (i, 0)), pl.BlockSpec((k, D_MODEL), lambda i: (0, 0)), row,
                  pl.BlockSpec((1, D_MODEL), lambda i: (0, 0))],
        out_specs=[row, row],
        out_shape=[jax.ShapeDtypeStruct((t, D_MODEL), F32), jax.ShapeDtypeStruct((t, D_MODEL), BF16)],
        compiler_params=_params(("parallel",)),
    )(a, w, res, g_next)


def _mm_tn(name, a, b, out_dtype=BF16, alpha=1.0, tm_c=(768, 512, 256, 128), tn_c=(512, 640, 256, 128)):
    t, m = a.shape
    n = b.shape[1]
    tm = _pick(m, tm_c)
    tn = _pick(n, tn_c)
    bytes_a, bytes_b = a.size * a.dtype.itemsize, b.size * b.dtype.itemsize
    if bytes_a + bytes_b * (m // tm) <= bytes_b + bytes_a * (n // tn):
        return _mm(name, a, b, _TN, (m // tm, n // tn),
                   pl.BlockSpec((t, tm), lambda i, j: (0, i)), pl.BlockSpec((t, tn), lambda i, j: (0, j)),
                   pl.BlockSpec((tm, tn), lambda i, j: (i, j)), (m, n), out_dtype, alpha=alpha)
    return _mm(name, a, b, _TN, (n // tn, m // tm),
               pl.BlockSpec((t, tm), lambda j, i: (0, i)), pl.BlockSpec((t, tn), lambda j, i: (0, j)),
               pl.BlockSpec((tm, tn), lambda j, i: (i, j)), (m, n), out_dtype, alpha=alpha)


def _ffn_in_fwd(name, n, wblk, dep=None):
    t = n.shape[0]
    tm = _pick(t, (1088, 768, 512, 256, 128))
    has_dep = dep is not None

    def body(n_ref, w_ref, *rest):
        gu_ref, a_ref = rest[-2], rest[-1]
        nv = n_ref[...]
        g = lax.dot_general(nv, w_ref[0], _NN, preferred_element_type=F32)
        u = lax.dot_general(nv, w_ref[1], _NN, preferred_element_type=F32)
        sg = jax.nn.sigmoid(g)
        silu = g * sg
        a_ref[...] = (silu * u).astype(BF16)
        gu_ref[0] = ((0.5 * u) * (sg + silu * (1.0 - sg))).astype(BF16)
        gu_ref[1] = (0.5 * silu).astype(BF16)

    return pl.pallas_call(
        body, name=name, grid=(t // tm, 4),
        in_specs=[pl.BlockSpec((tm, D_MODEL), lambda i, j: (i, 0)),
                  pl.BlockSpec((2, None, D_MODEL, FF_SHARD_P), lambda i, j: (0, j, 0, 0))]
        + ([pl.BlockSpec(memory_space=pl.ANY)] if has_dep else []),
        out_specs=[pl.BlockSpec((2, tm, FF_SHARD_P), lambda i, j: (0, i, j)),
                   pl.BlockSpec((tm, FF_SHARD_P), lambda i, j: (i, j))],
        out_shape=[jax.ShapeDtypeStruct((2, t, D_FF_P), BF16), jax.ShapeDtypeStruct((t, D_FF_P), BF16)],
        compiler_params=_params(("parallel", "parallel")),
    )(*((n, wblk) + ((dep,) if has_dep else ())))


def _ffn_out_bwd_x(name, dh, w_out, gu, dep=None):
    t = dh.shape[0]
    tm = _pick(t, (1088, 768, 512, 256, 128))
    has_dep = dep is not None

    def body(dh_ref, w_ref, gu_ref, *rest):
        o_ref = rest[-1]
        da = lax.dot_general(_bf(dh_ref[...]), w_ref[...], _NT, preferred_element_type=F32)
        o_ref[0] = (da * gu_ref[0].astype(F32)).astype(BF16)
        o_ref[1] = (da * gu_ref[1].astype(F32)).astype(BF16)

    gu_spec = pl.BlockSpec((2, tm, FF_SHARD_P), lambda i, j: (0, i, j))
    return pl.pallas_call(
        body, name=name, grid=(t // tm, 4),
        in_specs=[pl.BlockSpec((tm, D_MODEL), lambda i, j: (i, 0)), pl.BlockSpec((FF_SHARD_P, D_MODEL), lambda i, j: (j, 0)),
                  gu_spec] + ([pl.BlockSpec(memory_space=pl.ANY)] if has_dep else []),
        out_specs=gu_spec, out_shape=jax.ShapeDtypeStruct((2, t, D_FF_P), BF16),
        compiler_params=_params(("parallel", "parallel")),
    )(*((dh, w_out, gu) + ((dep,) if has_dep else ())))


def _rms_bwd_rows(dn, h, g, dres):
    r = lax.rsqrt(jnp.mean(h * h, axis=-1, keepdims=True) + EPS)
    tv = dn * g
    dot = jnp.mean(tv * h, axis=-1, keepdims=True)
    return dres + (r * tv - h * (r * r * r * dot)), jnp.sum(dn * (h * r), axis=0, keepdims=True)


def _accumulate_rows(ref, part, first):
    @pl.when(first)
    def _():
        ref[...] = part

    @pl.when(jnp.logical_not(first))
    def _():
        ref[...] += part


def _ffn_in_bwd_x(name, dgu, wblk, h_in, g_norm, dres, dep=None, examples=None):
    t = dgu.shape[1]
    tm = _pick(t, (544, 384, 256, 128))
    has_dep = dep is not None
    split = examples is not None
    if split:
        l = t // examples
        per = l // tm
        assert per * tm == l and tm > PREFIX

    def body(d_ref, w_ref, h_ref, g_ref, r_ref, *rest):
        acc = None
        for s in range(2):
            for j in range(4):
                part = lax.dot_general(d_ref[s, :, FF_SHARD_P * j:FF_SHARD_P * (j + 1)], w_ref[s, j], _NT,
                                       preferred_element_type=F32)
                acc = part if acc is None else acc + part
        dh, dg = _rms_bwd_rows(acc, h_ref[...], g_ref[...], r_ref[...])
        i = pl.program_id(0)
        if not split:
            dh_ref, dg_ref = rest[-2], rest[-1]
            dh_ref[...] = dh
        else:
            gx_ref, meta_ref, dg_ref, buf, sem = rest[-5:]

            def out_copy(step):
                bi, r = step // per, step % per
                head = pltpu.make_async_copy(buf.at[pl.ds(PREFIX, tm - PREFIX)], gx_ref.at[bi, pl.ds(0, tm - PREFIX)], sem)
                if per == 1:
                    return r == 0, head, None
                return r == 0, head, pltpu.make_async_copy(
                    buf, gx_ref.at[bi, pl.ds(pl.multiple_of(jnp.maximum(r, 1) * tm - PREFIX, 8), tm)], sem)

            def run(step, method):
                is_head, head, later = out_copy(step)

                @pl.when(is_head)
                def _():
                    getattr(head, method)()

                if later is not None:
                    @pl.when(jnp.logical_not(is_head))
                    def _():
                        getattr(later, method)()

            @pl.when(i > 0)
            def _():
                run(i - 1, "wait")

            buf[...] = dh

            @pl.when(i % per == 0)
            def _():
                meta_ref[...] = dh[N_PAD:PREFIX]

            run(i, "start")

            @pl.when(i == pl.num_programs(0) - 1)
            def _():
                run(i, "wait")

        _accumulate_rows(dg_ref, dg, i == 0)

    row = pl.BlockSpec((tm, D_MODEL), lambda i: (i, 0))
    vec = pl.BlockSpec((1, D_MODEL), lambda i: (0, 0))
    if split:
        out_specs = [pl.BlockSpec(memory_space=pl.ANY), pl.BlockSpec((None, N_META, D_MODEL), lambda i: (i // per, 0, 0)), vec]
        out_shape = [jax.ShapeDtypeStruct((examples, l - PREFIX, D_MODEL), F32),
                     jax.ShapeDtypeStruct((examples, N_META, D_MODEL), F32), jax.ShapeDtypeStruct((1, D_MODEL), F32)]
        scratch = [pltpu.VMEM((tm, D_MODEL), F32), pltpu.SemaphoreType.DMA(())]
    else:
        out_specs = [row, vec]
        out_shape = [jax.ShapeDtypeStruct((t, D_MODEL), F32), jax.ShapeDtypeStruct((1, D_MODEL), F32)]
        scratch = []
    return pl.pallas_call(
        body, name=name, grid=(t // tm,),
        in_specs=[pl.BlockSpec((2, tm, D_FF_P), lambda i: (0, i, 0)),
                  pl.BlockSpec((2, 4, D_MODEL, FF_SHARD_P), lambda i: (0, 0, 0, 0), pipeline_mode=pl.Buffered(1)),
                  row, vec, row]
        + ([pl.BlockSpec(memory_space=pl.ANY)] if has_dep else []),
        out_specs=out_specs, out_shape=out_shape, scratch_shapes=scratch,
        compiler_params=_params(("arbitrary",)),
    )(*((dgu, wblk, h_in, g_norm, dres) + ((dep,) if has_dep else ())))


def _proj_fwd(name, um, wi):
    t = um.shape[0]
    tm = _pick(t, (544, 384, 256, 128))

    def body(u_ref, w_ref, q_ref, g_ref, f_ref):
        uv = u_ref[...]
        q_ref[...] = lax.dot_general(uv, w_ref[:, 0:QKV_W], _NN, preferred_element_type=F32).astype(BF16)
        g_ref[...] = lax.dot_general(uv, w_ref[:, P_GATES:P_F], _NN, preferred_element_type=F32).astype(BF16)
        f_ref[...] = lax.dot_general(uv, w_ref[:, P_F:PROJ_P], _NN, preferred_element_type=F32)

    return pl.pallas_call(
        body, name=name, grid=(t // tm,),
        in_specs=[pl.BlockSpec((tm, D_MODEL), lambda i: (i, 0)),
                  pl.BlockSpec((D_MODEL, PROJ_P), lambda i: (0, 0), pipeline_mode=pl.Buffered(1))],
        out_specs=[pl.BlockSpec((tm, QKV_W), lambda i: (i, 0)), pl.BlockSpec((tm, 2 * D_MODEL), lambda i: (i, 0)),
                   pl.BlockSpec((tm, 128), lambda i: (i, 0))],
        out_shape=[jax.ShapeDtypeStruct((t, QKV_W), BF16), jax.ShapeDtypeStruct((t, 2 * D_MODEL), BF16),
                   jax.ShapeDtypeStruct((t, 128), F32)],
        compiler_params=_params(("parallel",)),
    )(um, wi)


def _proj_bwd_x(name, dqkv, dgates, df, wi, h_in, g_norm, dres, dep=None):
    t = dqkv.shape[0]
    tm = _pick(t, (544, 384, 256, 128))
    has_dep = dep is not None

    def body(q_ref, gt_ref, f_ref, w_ref, h_ref, g_ref, r_ref, *rest):
        dh_ref, dg_ref = rest[-2], rest[-1]
        acc = lax.dot_general(q_ref[...], w_ref[:, 0:QKV_W], _NT, preferred_element_type=F32)
        acc = acc + lax.dot_general(gt_ref[...], w_ref[:, P_GATES:P_F], _NT, preferred_element_type=F32)
        acc = acc + lax.dot_general(f_ref[...], w_ref[:, P_F:PROJ_P], _NT, preferred_element_type=F32)
        dh, dg = _rms_bwd_rows(acc, h_ref[...], g_ref[...], r_ref[...])
        dh_ref[...] = dh
        _accumulate_rows(dg_ref, dg, pl.program_id(0) == 0)

    row = pl.BlockSpec((tm, D_MODEL), lambda i: (i, 0))
    vec = pl.BlockSpec((1, D_MODEL), lambda i: (0, 0))
    return pl.pallas_call(
        body, name=name, grid=(t // tm,),
        in_specs=[pl.BlockSpec((tm, QKV_W), lambda i: (i, 0)), pl.BlockSpec((tm, 2 * D_MODEL), lambda i: (i, 0)),
                  pl.BlockSpec((tm, 128), lambda i: (i, 0)),
                  pl.BlockSpec((D_MODEL, PROJ_P), lambda i: (0, 0), pipeline_mode=pl.Buffered(1)), row, vec, row]
        + ([pl.BlockSpec(memory_space=pl.ANY)] if has_dep else []),
        out_specs=[row, vec],
        out_shape=[jax.ShapeDtypeStruct((t, D_MODEL), F32), jax.ShapeDtypeStruct((1, D_MODEL), F32)],
        compiler_params=_params(("arbitrary",)),
    )(*((dqkv, dgates, df, wi, h_in, g_norm, dres) + ((dep,) if has_dep else ())))


def _ffn_in_bwd_w(name, n, dgu):
    t = n.shape[0]
    return _mm(name, n, dgu, _TN, (2, 4),
               pl.BlockSpec((t, D_MODEL), lambda s, j: (0, 0)),
               pl.BlockSpec((None, t, FF_SHARD_P), lambda s, j: (s, 0, j)),
               pl.BlockSpec((None, None, D_MODEL, FF_SHARD_P), lambda s, j: (s, j, 0, 0)),
               (2, 4, D_MODEL, FF_SHARD_P), BF16)


def _embed_norm(name, x, meta, g):
    b, s, _ = x.shape
    half = (s + PREFIX) // 2
    first = half - PREFIX
    assert first > 0 and half % 16 == 0

    def body(x_ref, m_ref, g_ref, h_ref, n_ref, buf, sem):
        bi, k = pl.program_id(0), pl.program_id(1)

        def tokens(example, second, method):
            if second:
                cp = pltpu.make_async_copy(x_ref.at[example, pl.ds(first, half)], buf.at[1], sem.at[1])
            else:
                cp = pltpu.make_async_copy(x_ref.at[example, pl.ds(0, first)], buf.at[0, pl.ds(PREFIX, first)], sem.at[0])
            getattr(cp, method)()

        @pl.when((bi == 0) & (k == 0))
        def _():
            tokens(0, False, "start")

        @pl.when(k == 0)
        def _():
            tokens(bi, True, "start")
            tokens(bi, False, "wait")
            buf[0, 0:N_PAD, :] = jnp.zeros((N_PAD, D_MODEL), F32)
            buf[0, N_PAD:PREFIX, :] = m_ref[...]

        @pl.when(k == 1)
        def _():
            @pl.when(bi + 1 < b)
            def _():
                tokens(bi + 1, False, "start")

            tokens(bi, True, "wait")

        hv = buf[k]
        h_ref[...] = hv
        r = lax.rsqrt(jnp.mean(hv * hv, axis=-1, keepdims=True) + EPS)
        n_ref[...] = ((hv * r) * g_ref[...]).astype(BF16)

    rows = pl.BlockSpec((half, D_MODEL), lambda bi, k: (2 * bi + k, 0))
    return pl.pallas_call(
        body, name=name, grid=(b, 2),
        in_specs=[pl.BlockSpec(memory_space=pl.ANY), pl.BlockSpec((N_META, D_MODEL), lambda bi, k: (0, 0)),
                  pl.BlockSpec((1, D_MODEL), lambda bi, k: (0, 0))],
        out_specs=[rows, rows],
        out_shape=[jax.ShapeDtypeStruct((2 * b * half, D_MODEL), F32), jax.ShapeDtypeStruct((2 * b * half, D_MODEL), BF16)],
        scratch_shapes=[pltpu.VMEM((2, half, D_MODEL), F32), pltpu.SemaphoreType.DMA((2,))],
        compiler_params=_params(("arbitrary", "arbitrary")),
    )(x, meta, g)


def _branch_gate_fwd(name, oa, ob, wa, wb, gates, dep=None):
    t = gates.shape[0]
    tm = _pick(t, (544, 384, 256, 128))
    has_dep = dep is not None

    def body(oa_ref, ob_ref, wa_ref, wb_ref, g_ref, *rest):
        o_ref, ya_ref, yb_ref = rest[-3:]
        ya = lax.dot_general(_bf(oa_ref[...]), wa_ref[...], _NN, preferred_element_type=F32)
        yb = lax.dot_general(_bf(ob_ref[...]), wb_ref[...], _NN, preferred_element_type=F32)
        sa = jax.nn.sigmoid(g_ref[:, 0:D_MODEL].astype(F32))
        sb = jax.nn.sigmoid(g_ref[:, D_MODEL:2 * D_MODEL].astype(F32))
        o_ref[...] = (sa * ya + sb * yb).astype(BF16)
        ya_ref[...] = ya.astype(BF16)
        yb_ref[...] = yb.astype(BF16)

    blk = pl.BlockSpec((tm, D_MODEL), lambda i: (i, 0))
    narrow = pl.BlockSpec((tm, A_WIDTH), lambda i: (i, 0))
    wide = pl.BlockSpec((tm, 2 * D_MODEL), lambda i: (i, 0))
    wspec = pl.BlockSpec((A_WIDTH, D_MODEL), lambda i: (0, 0))
    out = jax.ShapeDtypeStruct((t, D_MODEL), BF16)
    return pl.pallas_call(
        body, name=name, grid=(t // tm,),
        in_specs=[narrow, narrow, wspec, wspec, wide] + ([pl.BlockSpec(memory_space=pl.ANY)] if has_dep else []),
        out_specs=[blk, blk, blk], out_shape=[out, out, out], compiler_params=_params(("parallel",)),
    )(*((oa, ob, wa, wb, gates) + ((dep,) if has_dep else ())))


def _branch_bwd_x(name, dya, dyb, wa, wb):
    t = dya.shape[0]
    tm = _pick(t, (1088, 768, 512, 256, 128))

    def body(da_ref, db_ref, wa_ref, wb_ref, oa_ref, ob_ref):
        oa_ref[...] = lax.dot_general(da_ref[...], wa_ref[...], _NT, preferred_element_type=F32)
        ob_ref[...] = lax.dot_general(db_ref[...], wb_ref[...], _NT, preferred_element_type=F32).astype(BF16)

    blk = pl.BlockSpec((tm, D_MODEL), lambda i: (i, 0))
    narrow = pl.BlockSpec((tm, A_WIDTH), lambda i: (i, 0))
    wspec = pl.BlockSpec((A_WIDTH, D_MODEL), lambda i: (0, 0), pipeline_mode=pl.Buffered(1))
    return pl.pallas_call(
        body, name=name, grid=(t // tm,), in_specs=[blk, blk, wspec, wspec], out_specs=[narrow, narrow],
        out_shape=[jax.ShapeDtypeStruct((t, A_WIDTH), F32), jax.ShapeDtypeStruct((t, B_WIDTH), BF16)],
        compiler_params=_params(("parallel",)),
    )(dya, dyb, wa, wb)


def _branch_bwd_w(name, oa, ob, dya, dyb):
    t = oa.shape[0]
    tn = 512

    def body(oa_ref, ob_ref, da_ref, db_ref, wa_ref, wb_ref):
        wa_ref[...] = lax.dot_general(_bf(oa_ref[...]), da_ref[...], _TN, preferred_element_type=F32).astype(BF16)
        wb_ref[...] = lax.dot_general(_bf(ob_ref[...]), db_ref[...], _TN, preferred_element_type=F32).astype(BF16)

    whole = pl.BlockSpec((t, A_WIDTH), lambda j: (0, 0), pipeline_mode=pl.Buffered(1))
    cols = pl.BlockSpec((t, tn), lambda j: (0, j))
    out_spec = pl.BlockSpec((A_WIDTH, tn), lambda j: (0, j))
    out = jax.ShapeDtypeStruct((A_WIDTH, D_MODEL), BF16)
    return pl.pallas_call(
        body, name=name, grid=(D_MODEL // tn,), in_specs=[whole, whole, cols, cols], out_specs=[out_spec, out_spec],
        out_shape=[out, out], compiler_params=_params(("parallel",)),
    )(oa, ob, dya, dyb)


def _mix_out_gate_bwd(name, dh, wo, gates, ya, yb):
    t = gates.shape[0]
    tm = _pick(t, (544, 384, 256, 128))

    def body(dh_ref, w_ref, g_ref, ya_ref, yb_ref, dya_ref, dyb_ref, dg_ref):
        dm = lax.dot_general(_bf(dh_ref[...]), w_ref[...], _NT, preferred_element_type=F32)
        sa = jax.nn.sigmoid(g_ref[:, 0:D_MODEL].astype(F32))
        sb = jax.nn.sigmoid(g_ref[:, D_MODEL:2 * D_MODEL].astype(F32))
        dya_ref[...] = (dm * sa).astype(BF16)
        dyb_ref[...] = (dm * sb).astype(BF16)
        dg_ref[:, 0:D_MODEL] = (dm * ya_ref[...].astype(F32) * (sa * (1.0 - sa))).astype(BF16)
        dg_ref[:, D_MODEL:2 * D_MODEL] = (dm * yb_ref[...].astype(F32) * (sb * (1.0 - sb))).astype(BF16)

    blk = pl.BlockSpec((tm, D_MODEL), lambda i: (i, 0))
    wide = pl.BlockSpec((tm, 2 * D_MODEL), lambda i: (i, 0))
    out = jax.ShapeDtypeStruct((t, D_MODEL), BF16)
    return pl.pallas_call(
        body, name=name, grid=(t // tm,),
        in_specs=[blk, pl.BlockSpec((D_MODEL, D_MODEL), lambda i: (0, 0)), wide, blk, blk], out_specs=[blk, blk, wide],
        out_shape=[out, out, jax.ShapeDtypeStruct((t, 2 * D_MODEL), BF16)], compiler_params=_params(("parallel",)),
    )(dh, wo, gates, ya, yb)


def _ffn_out_loss(name, a, w, res, gf, tgt, alpha):
    t, k = a.shape
    b, s, _ = tgt.shape
    l = t // b
    tm = _pick(t, (544, 384, 256, 128))
    per = l // tm
    assert per * tm == l and tm > PREFIX and l - PREFIX == s

    def body(a_ref, w_ref, r_ref, g_ref, t_ref, dh_ref, loss_ref, dg_ref, buf, sem):
        i = pl.program_id(0)
        bi, r = i // per, i % per

        def first_rows():
            return pltpu.make_async_copy(t_ref.at[bi, pl.ds(0, tm - PREFIX)], buf.at[pl.ds(PREFIX, tm - PREFIX)], sem)

        def later_rows():
            return pltpu.make_async_copy(t_ref.at[bi, pl.ds(pl.multiple_of(r * tm - PREFIX, 8), tm)], buf, sem)

        @pl.when(r == 0)
        def _():
            buf[0:PREFIX, :] = jnp.zeros((PREFIX, D_MODEL), F32)
            first_rows().start()

        if per > 1:
            @pl.when(r > 0)
            def _():
                later_rows().start()

        acc = lax.dot_general(a_ref[...], w_ref[...], _NN, preferred_element_type=F32)

        @pl.when(r == 0)
        def _():
            first_rows().wait()

        if per > 1:
            @pl.when(r > 0)
            def _():
                later_rows().wait()

        hv = acc * alpha + r_ref[...]
        row = lax.broadcasted_iota(jnp.int32, (tm, 1), 0)
        real = ((r > 0) | (row >= PREFIX)).astype(F32)
        g = g_ref[...]
        rn = lax.rsqrt(jnp.mean(hv * hv, axis=-1, keepdims=True) + EPS)
        xn = hv * rn
        err = (xn * g - buf[...]) * real
        lpart = 0.5 * jnp.sum(jnp.mean(err * err, axis=-1, keepdims=True), axis=0, keepdims=True)
        dy = err * (1.0 / D_MODEL)
        tv = dy * g
        dot = jnp.mean(tv * hv, axis=-1, keepdims=True)
        dh_ref[...] = rn * tv - hv * (rn * rn * rn * dot)
        gpart = jnp.sum(dy * xn, axis=0, keepdims=True)

        @pl.when(i == 0)
        def _():
            loss_ref[...] = jnp.zeros_like(loss_ref)
            dg_ref[...] = jnp.zeros_like(dg_ref)

        loss_ref[...] += jnp.broadcast_to(lpart, loss_ref.shape)
        dg_ref[...] += gpart

    row_spec = pl.BlockSpec((tm, D_MODEL), lambda i: (i, 0))
    vec = pl.BlockSpec((1, D_MODEL), lambda i: (0, 0))
    return pl.pallas_call(
        body, name=name, grid=(t // tm,),
        in_specs=[pl.BlockSpec((tm, k), lambda i: (i, 0)),
                  pl.BlockSpec((k, D_MODEL), lambda i: (0, 0), pipeline_mode=pl.Buffered(1)), row_spec, vec,
                  pl.BlockSpec(memory_space=pl.ANY)],
        out_specs=[row_spec, pl.BlockSpec((8, 128), lambda i: (0, 0)), vec],
        out_shape=[jax.ShapeDtypeStruct((t, D_MODEL), F32), jax.ShapeDtypeStruct((8, 128), F32),
                   jax.ShapeDtypeStruct((1, D_MODEL), F32)],
        scratch_shapes=[pltpu.VMEM((tm, D_MODEL), F32), pltpu.SemaphoreType.DMA(())],
        compiler_params=_params(("arbitrary",)),
    )(a, w, res, gf, tgt)


def _fgate_fwd(name, f3, bf_row):
    b, l, _ = f3.shape
    nb = l // BLOCK

    def body(f_ref, b_ref, cc_ref, cr_ref):
        r_i = lax.broadcasted_iota(jnp.int32, (BLOCK, BLOCK), 0)
        c_i = lax.broadcasted_iota(jnp.int32, (BLOCK, BLOCK), 1)
        tri = (r_i >= c_i).astype(F32)
        carry = jnp.zeros((1, 128), F32)
        for blk in range(nb):
            rows = slice(blk * BLOCK, (blk + 1) * BLOCK)
            z = f_ref[rows, :] + b_ref[...]
            lf = jnp.minimum(z, 0.0) - jnp.log(1.0 + jnp.exp(-jnp.abs(z)))
            cb = jnp.dot(tri, lf, preferred_element_type=F32, precision=lax.Precision.HIGHEST) + carry
            carry = cb[BLOCK - 1:BLOCK, :]
            cbt = cb.T
            for hh in range(B_HEADS):
                cc_ref[hh, rows, :] = jnp.sum(jnp.where(c_i == hh, cb, 0.0), axis=1, keepdims=True)
                cr_ref[hh, :, rows] = cbt[hh:hh + 1, :]

    return pl.pallas_call(
        body, name=name, grid=(b,),
        in_specs=[pl.BlockSpec((None, l, 128), lambda bi: (bi, 0, 0)),
                  pl.BlockSpec((1, 128), lambda bi: (0, 0))],
        out_specs=[pl.BlockSpec((None, B_HEADS, l, 1), lambda bi: (bi, 0, 0, 0)),
                   pl.BlockSpec((None, B_HEADS, 1, l), lambda bi: (bi, 0, 0, 0))],
        out_shape=[jax.ShapeDtypeStruct((b, B_HEADS, l, 1), F32), jax.ShapeDtypeStruct((b, B_HEADS, 1, l), F32)],
        compiler_params=_params(("parallel",)),
    )(f3, bf_row)


def _fgate_bwd(name, f3, bf_row, dcq, dck):
    b, l, _ = f3.shape
    nb = l // BLOCK

    def body(f_ref, b_ref, dcq_ref, dck_ref, df_ref, db_ref):
        r_i = lax.broadcasted_iota(jnp.int32, (BLOCK, BLOCK), 0)
        c_i = lax.broadcasted_iota(jnp.int32, (BLOCK, BLOCK), 1)
        tri = (r_i <= c_i).astype(F32)
        carry = jnp.zeros((1, 128), F32)
        total = jnp.zeros((1, 128), F32)
        for blk in range(nb - 1, -1, -1):
            rows = slice(blk * BLOCK, (blk + 1) * BLOCK)
            krows = jnp.concatenate([dck_ref[hh, :, rows] for hh in range(B_HEADS)]
                                    + [jnp.zeros((BLOCK - B_HEADS, BLOCK), F32)], axis=0)
            dcb = krows.T
            for hh in range(B_HEADS):
                dcb = dcb + jnp.where(c_i == hh, dcq_ref[hh, rows, :], 0.0)
            rc = jnp.dot(tri, dcb, preferred_element_type=F32, precision=lax.Precision.HIGHEST) + carry
            carry = rc[0:1, :]
            z = f_ref[rows, :] + b_ref[...]
            df = rc * (1.0 / (1.0 + jnp.exp(z)))
            df_ref[rows, :] = df.astype(BF16)
            total = total + jnp.sum(df, axis=0, keepdims=True)

        @pl.when(pl.program_id(0) == 0)
        def _():
            db_ref[...] = total

        @pl.when(pl.program_id(0) > 0)
        def _():
            db_ref[...] += total

    return pl.pallas_call(
        body, name=name, grid=(b,),
        in_specs=[pl.BlockSpec((None, l, 128), lambda bi: (bi, 0, 0)),
                  pl.BlockSpec((1, 128), lambda bi: (0, 0)),
                  pl.BlockSpec((None, B_HEADS, l, 1), lambda bi: (bi, 0, 0, 0)),
                  pl.BlockSpec((None, B_HEADS, 1, l), lambda bi: (bi, 0, 0, 0))],
        out_specs=[pl.BlockSpec((None, l, 128), lambda bi: (bi, 0, 0)), pl.BlockSpec((1, 128), lambda bi: (0, 0))],
        out_shape=[jax.ShapeDtypeStruct((b, l, 128), BF16), jax.ShapeDtypeStruct((1, 128), F32)],
        compiler_params=_params(("arbitrary",)),
    )(f3, bf_row, dcq, dck)


A_Q_BLK = B_SEG // A_WIDTH
A_K_BLK = (B_SEG + A_WIDTH) // 128
A_V_BLK = A_K_BLK + 1
A_SEG_BLK = B_SEG // A_SEG
STACK = A_HEADS * BLOCK


def _lane_lo():
    return lax.broadcasted_iota(jnp.int32, (1, 128), 1) < HEAD_DIM


def _stack_heads(x, masked):
    lo = _lane_lo()
    blks = [x[:, 128 * j:128 * (j + 1)] for j in range(4)]
    if not masked:
        return jnp.concatenate(blks + blks, axis=0)
    zero = jnp.zeros_like(blks[0])
    return jnp.concatenate([jnp.where(lo, bk, zero) for bk in blks] + [jnp.where(lo, zero, bk) for bk in blks], axis=0)


def _unstack_heads(y):
    lo = _lane_lo()
    return jnp.concatenate([jnp.where(lo, y[128 * j:128 * (j + 1)], y[128 * (4 + j):128 * (5 + j)]) for j in range(4)], axis=1)


def _swa_bias(slopes):
    r_i = jnp.arange(STACK)[:, None]
    c_i = jnp.arange(3 * BLOCK)[None, :]
    seg = c_i >> 7
    out = []
    for n in range(3):
        qpos = n * BLOCK + (r_i & (BLOCK - 1))
        kpos = jnp.where(seg == 0, c_i, (n - 2) * BLOCK + c_i)
        dist = qpos - kpos
        band = (seg != 0) & (dist < BLOCK) & (kpos >= PREFIX)
        meta = (seg == 0) & (c_i >= N_PAD)
        out.append(jnp.where((dist >= 0) & (band | meta), -slopes * dist.astype(F32), NEG))
    return jnp.stack(out, axis=0)


def _swa_scores(q, kcat, n, slope, bias):
    s = lax.dot_general(q, kcat, _NT, preferred_element_type=F32) + bias
    further = slope * (-BLOCK * jnp.maximum(n - 2, 0)).astype(F32)
    return jnp.concatenate([s[:, 0:BLOCK] + further, s[:, BLOCK:]], axis=1)


def _swa_specs():
    def kv(col_blk):
        return [pl.BlockSpec((None, BLOCK, 128), lambda b, n: (b, 0, col_blk)),
                pl.BlockSpec((None, BLOCK, 128), lambda b, n: (b, jnp.maximum(n - 1, 0), col_blk)),
                pl.BlockSpec((None, BLOCK, 128), lambda b, n: (b, n, col_blk))]

    q_spec = pl.BlockSpec((None, BLOCK, A_WIDTH), lambda b, n: (b, n, A_Q_BLK))
    o_spec = pl.BlockSpec((None, BLOCK, A_WIDTH), lambda b, n: (b, n, 0))
    col = pl.BlockSpec((STACK, 1), lambda b, n: (0, 0))
    bias = pl.BlockSpec((None, STACK, 3 * BLOCK), lambda b, n: (jnp.minimum(n, 2), 0, 0))
    lse_spec = pl.BlockSpec((None, A_HEADS, BLOCK, 1), lambda b, n: (b, 0, n, 0))
    return q_spec, kv(A_K_BLK), kv(A_V_BLK), o_spec, [col, col, bias], lse_spec


def _swa_fwd(name, qkv, slopes, sinks, bias):
    b, l, _ = qkv.shape
    nb = l // BLOCK

    def body(q_ref, k0_ref, kp_ref, kc_ref, v0_ref, vp_ref, vc_ref, sl_ref, sk_ref, bias_ref, o_ref, lse_ref):
        n = pl.program_id(1)
        qs = _stack_heads(q_ref[...], True) * SCALE
        kcat = jnp.concatenate([k0_ref[...], kp_ref[...], kc_ref[...]], axis=0)
        vcat = jnp.concatenate([v0_ref[...], vp_ref[...], vc_ref[...]], axis=0)
        s = _swa_scores(qs, kcat, n, sl_ref[...], bias_ref[...])
        sink = sk_ref[...]
        m = jnp.maximum(jnp.max(s, axis=-1, keepdims=True), sink)
        p = jnp.exp(s - m)
        den = jnp.sum(p, axis=-1, keepdims=True) + jnp.exp(sink - m)
        o = lax.dot_general(p.astype(BF16), vcat, _NN, preferred_element_type=F32) / den
        o_ref[...] = _unstack_heads(o)
        lse_ref[...] = (m + jnp.log(den)).reshape(A_HEADS, BLOCK, 1)

    q_spec, k_specs, v_specs, o_spec, consts, lse_spec = _swa_specs()
    return pl.pallas_call(
        body, name=name, grid=(b, nb),
        in_specs=[q_spec] + k_specs + v_specs + consts, out_specs=[o_spec, lse_spec],
        out_shape=[jax.ShapeDtypeStruct((b, l, A_WIDTH), F32), jax.ShapeDtypeStruct((b, A_HEADS, l, 1), F32)],
        compiler_params=_params(("parallel", "parallel")),
    )(qkv, qkv, qkv, qkv, qkv, qkv, qkv, slopes, sinks, bias)


def _swa_bwd(name, qkv, o, lse, do, slopes, sinks, bias, dqkv):
    b, l, _ = qkv.shape
    nb = l // BLOCK

    def body(q_ref, k0_ref, kp_ref, kc_ref, v0_ref, vp_ref, vc_ref, o_ref, lse_ref, do_ref, sl_ref, sk_ref, bias_ref, _,
             dx_ref, ds_ref, dk_acc, dv_acc):
        bi = pl.program_id(0)
        n = pl.program_id(1)
        qs = _stack_heads(q_ref[...], True) * SCALE
        dos32 = _stack_heads(do_ref[...], True)
        dos = dos32.astype(BF16)
        os_ = _stack_heads(o_ref[...], False)
        lsev = lse_ref[...].reshape(STACK, 1)
        kcat = jnp.concatenate([k0_ref[...], kp_ref[...], kc_ref[...]], axis=0)
        vcat = jnp.concatenate([v0_ref[...], vp_ref[...], vc_ref[...]], axis=0)
        s = _swa_scores(qs, kcat, n, sl_ref[...], bias_ref[...])
        p = jnp.exp(s - lsev)
        dsum = jnp.sum(dos32 * os_, axis=-1, keepdims=True)
        dp = lax.dot_general(dos, vcat, _NT, preferred_element_type=F32)
        dsc = (p * (dp - dsum)).astype(BF16)
        dq = lax.dot_general(dsc, kcat, _NN, preferred_element_type=F32) * SCALE
        row0 = pl.multiple_of(n * BLOCK, BLOCK)
        dx_ref[pl.ds(row0, BLOCK), 0:A_WIDTH] = _unstack_heads(dq).astype(BF16)
        dkc = lax.dot_general(dsc, qs, _TN, preferred_element_type=F32)
        dvc = lax.dot_general(p.astype(BF16), dos, _TN, preferred_element_type=F32)

        @pl.when(n == 0)
        def _():
            dk_acc[...] = jnp.zeros_like(dk_acc)
            dv_acc[...] = jnp.zeros_like(dv_acc)

        starts = (0, pl.multiple_of(jnp.maximum(n - 1, 0) * BLOCK, BLOCK), row0)
        for t, st in enumerate(starts):
            dk_acc[pl.ds(st, BLOCK), :] += dkc[t * BLOCK:(t + 1) * BLOCK, :]
            dv_acc[pl.ds(st, BLOCK), :] += dvc[t * BLOCK:(t + 1) * BLOCK, :]

        @pl.when(n == nb - 1)
        def _():
            dx_ref[:, A_WIDTH:A_WIDTH + 128] = dk_acc[...].astype(BF16)
            dx_ref[:, A_WIDTH + 128:A_SEG] = dv_acc[...].astype(BF16)

        dsink = -(jnp.exp(sk_ref[...] - lsev) * dsum)
        r8 = lax.broadcasted_iota(jnp.int32, (8, 128), 0)
        acc = jnp.zeros((8, 128), F32)
        for hh in range(A_HEADS):
            acc = acc + jnp.where(r8 == hh, jnp.sum(dsink[hh * BLOCK:(hh + 1) * BLOCK, :]), 0.0)

        @pl.when((bi == 0) & (n == 0))
        def _():
            ds_ref[...] = jnp.zeros_like(ds_ref)

        ds_ref[...] += acc

    q_spec, k_specs, v_specs, o_spec, consts, lse_spec = _swa_specs()
    return pl.pallas_call(
        body, name=name, grid=(b, nb),
        in_specs=[q_spec] + k_specs + v_specs + [o_spec, lse_spec, o_spec] + consts + [pl.BlockSpec(memory_space=pl.ANY)],
        out_specs=[pl.BlockSpec((None, l, A_SEG), lambda bb, n: (bb, 0, A_SEG_BLK)),
                   pl.BlockSpec((8, 128), lambda bb, n: (0, 0))],
        out_shape=[jax.ShapeDtypeStruct(dqkv.shape, BF16), jax.ShapeDtypeStruct((8, 128), F32)],
        scratch_shapes=[pltpu.VMEM((l, 128), F32), pltpu.VMEM((l, 128), F32)],
        input_output_aliases={13: 0},
        compiler_params=_params(("arbitrary", "arbitrary")),
    )(qkv, qkv, qkv, qkv, qkv, qkv, qkv, o, lse, do, slopes, sinks, bias, dqkv)


def _fox_mask(qk, ck, i):
    kh = qk.shape[1]
    qpos = i * BLOCK + lax.broadcasted_iota(jnp.int32, (BLOCK, kh), 0)
    kpos = lax.broadcasted_iota(jnp.int32, (BLOCK, kh), 1)
    return jnp.where((kpos <= qpos) & (kpos >= N_PAD), qk - ck, NEG)


def _pick_head(x, hh):
    lo = _lane_lo()
    return jnp.where(lo if hh == 0 else jnp.logical_not(lo), x, jnp.zeros_like(x))


def _both_heads(x):
    return jnp.concatenate([_pick_head(x, 0), _pick_head(x, 1)], axis=0)


def _fox_specs(l):
    pair = pl.BlockSpec((None, l, PAIR_W), lambda bi, hp: (bi, 0, hp))
    half = pl.BlockSpec((None, l, 128), lambda bi, hp: (bi, 0, hp))
    colv = pl.BlockSpec((None, 2, l, 1), lambda bi, hp: (bi, hp, 0, 0))
    rowv = pl.BlockSpec((None, 2, 1, l), lambda bi, hp: (bi, hp, 0, 0))
    return pair, half, colv, rowv


def _fox_fwd(name, qkv, c_col, c_row):
    b, l, _ = qkv.shape
    nb = l // BLOCK

    def body(x_ref, cc_ref, cr_ref, o_ref, lse_ref):
        for i in range(nb):
            rows = slice(i * BLOCK, (i + 1) * BLOCK)
            kh = (i + 1) * BLOCK
            qblk = x_ref[rows, 0:128]
            kv = x_ref[0:kh, 128:256]
            vv = x_ref[0:kh, 256:384]
            qk = lax.dot_general(_both_heads(qblk) * SCALE, kv, _NT, preferred_element_type=F32)
            ps, dens = [], []
            for hh in range(2):
                s = _fox_mask(qk[hh * BLOCK:(hh + 1) * BLOCK], cr_ref[hh, :, 0:kh], i)
                m = jnp.max(s, axis=-1, keepdims=True)
                p = jnp.exp(s - m)
                den = jnp.sum(p, axis=-1, keepdims=True)
                ps.append(p.astype(BF16))
                dens.append(den)
                lse_ref[hh, rows, :] = (m + jnp.log(den)) + cc_ref[hh, rows, :]
            pv = lax.dot_general(jnp.concatenate(ps, axis=0), vv, _NN, preferred_element_type=F32)
            o_ref[rows, :] = jnp.where(_lane_lo(), pv[0:BLOCK] / dens[0], pv[BLOCK:2 * BLOCK] / dens[1]).astype(BF16)

    pair, half, colv, rowv = _fox_specs(l)
    return pl.pallas_call(
        body, name=name, grid=(b, 4), in_specs=[pair, colv, rowv], out_specs=[half, colv],
        out_shape=[jax.ShapeDtypeStruct((b, l, B_WIDTH), BF16), jax.ShapeDtypeStruct((b, B_HEADS, l, 1), F32)],
        compiler_params=_params(("parallel", "parallel")),
    )(qkv, c_col, c_row)


def _fox_bwd(name, qkv, c_col, c_row, o, lse, do):
    b, l, _ = qkv.shape
    nb = l // BLOCK

    def body(x_ref, cc_ref, cr_ref, o_ref, lse_ref, do_ref, dx_ref, dcq_ref, dck_ref, dk_acc, dv_acc):
        dk_acc[...] = jnp.zeros_like(dk_acc)
        dv_acc[...] = jnp.zeros_like(dv_acc)
        dck_ref[...] = jnp.zeros_like(dck_ref)
        for i in range(nb):
            rows = slice(i * BLOCK, (i + 1) * BLOCK)
            kh = (i + 1) * BLOCK
            qblk = x_ref[rows, 0:128]
            kv = x_ref[0:kh, 128:256]
            vv = x_ref[0:kh, 256:384]
            doblk = do_ref[rows, :]
            ov = o_ref[rows, :].astype(F32)
            q2 = _both_heads(qblk) * SCALE
            do2 = _both_heads(doblk)
            qk = lax.dot_general(q2, kv, _NT, preferred_element_type=F32)
            dp = lax.dot_general(do2, vv, _NT, preferred_element_type=F32)
            ps, dss = [], []
            for hh in range(2):
                half = slice(hh * BLOCK, (hh + 1) * BLOCK)
                s = _fox_mask(qk[half], cr_ref[hh, :, 0:kh], i)
                p = jnp.exp(s - (lse_ref[hh, rows, :] - cc_ref[hh, rows, :]))
                dsum = jnp.sum(do2[half].astype(F32) * ov, axis=-1, keepdims=True)
                ds = p * (dp[half] - dsum)
                ps.append(p.astype(BF16))
                dss.append(ds.astype(BF16))
                dcq_ref[hh, rows, :] = jnp.sum(ds, axis=-1, keepdims=True)
                dck_ref[hh, :, 0:kh] -= jnp.sum(ds, axis=0, keepdims=True)
            p2 = jnp.concatenate(ps, axis=0)
            ds2 = jnp.concatenate(dss, axis=0)
            dq = lax.dot_general(ds2, kv, _NN, preferred_element_type=F32) * SCALE
            dk_acc[0:kh, :] += lax.dot_general(ds2, q2, _TN, preferred_element_type=F32)
            dv_acc[0:kh, :] += lax.dot_general(p2, do2, _TN, preferred_element_type=F32)
            dx_ref[rows, 0:128] = jnp.where(_lane_lo(), dq[0:BLOCK], dq[BLOCK:2 * BLOCK]).astype(BF16)
        dx_ref[:, 128:256] = dk_acc[...].astype(BF16)
        dx_ref[:, 256:384] = dv_acc[...].astype(BF16)

    pair, half, colv, rowv = _fox_specs(l)
    return pl.pallas_call(
        body, name=name, grid=(b, 4), in_specs=[pair, colv, rowv, half, colv, half],
        out_specs=[pair, colv, rowv],
        out_shape=[jax.ShapeDtypeStruct(qkv.shape, BF16), jax.ShapeDtypeStruct((b, B_HEADS, l, 1), F32),
                   jax.ShapeDtypeStruct((b, B_HEADS, 1, l), F32)],
        scratch_shapes=[pltpu.VMEM((l, 128), F32), pltpu.VMEM((l, 128), F32)],
        compiler_params=_params(("parallel", "parallel")),
    )(qkv, c_col, c_row, o, lse, do)


_FLIPS = ((0, 0, 1), (0, 1, 0), (0, 1, 1), (1, 0, 0), (1, 0, 1), (1, 1, 0), (1, 1, 1))


def _peer_table():
    x, y, c = lax.axis_index("x"), lax.axis_index("y"), lax.axis_index("c")
    me = 4 * x + 2 * y + c
    peers = []
    for fx, fy, fc in _FLIPS:
        px = 1 - x if fx else x
        py = 1 - y if fy else y
        pc = 1 - c if fc else c
        peers.append(((px, py, pc), 4 * px + 2 * py + pc))
    return me, peers


_HBM = pl.BlockSpec(memory_space=pltpu.HBM)
_SEM = pl.BlockSpec(memory_space=pltpu.SEMAPHORE)
_ANY = pl.BlockSpec(memory_space=pl.ANY)
_EFFECT = pltpu.SideEffectType.DATAFLOW_SIDE_EFFECTING


def _split_copy(srcs_are_pieces, src_refs, land_refs, send_sem, recv_sem, a, kk, me, peers, arriving):
    dev, lin = peers[kk]
    npeer = len(_FLIPS)
    src = src_refs[a] if srcs_are_pieces[a] else src_refs[a].at[lin]
    dst = land_refs[a].at[lin] if arriving else land_refs[a].at[me]
    return pltpu.make_async_remote_copy(src_ref=src, dst_ref=dst, send_sem=send_sem.at[a * npeer + kk],
                                        recv_sem=recv_sem.at[a * npeer + kk], device_id=dev, device_id_type=MESH_ID)


def _xchg_start(name, gather, scatter, after=None):
    me_out = 4 * lax.axis_index("x") + 2 * lax.axis_index("y") + lax.axis_index("c")
    srcs = list(gather) + list(scatter)
    is_piece = [True] * len(gather) + [False] * len(scatter)
    lands = []
    for a, piece in zip(srcs, is_piece):
        if piece:
            lands.append(lax.dynamic_update_slice(lax.empty((N_DEV,) + tuple(a.shape), a.dtype), a[None],
                                                  (me_out,) + (0,) * a.ndim))
        else:
            lands.append(lax.empty(tuple(a.shape), a.dtype))
    n = len(srcs)
    nsem = n * len(_FLIPS)
    has_after = after is not None

    def body(*refs):
        src_refs = refs[:n]
        land_refs = refs[n:2 * n]
        outs = refs[2 * n + (1 if has_after else 0):]
        send_sem, recv_sem = outs[0], outs[1]
        token = outs[-1]
        me, peers = _peer_table()
        for kk in range(len(_FLIPS)):
            for a in range(n):
                _split_copy(is_piece, src_refs, land_refs, send_sem, recv_sem, a, kk, me, peers, False).start()
        token[...] = jnp.zeros_like(token)

    out_shape = ([pltpu.SemaphoreType.DMA((nsem,)), pltpu.SemaphoreType.DMA((nsem,))]
                 + [pltpu.HBM(tuple(a.shape), a.dtype) for a in srcs] + [pltpu.HBM(tuple(a.shape), a.dtype) for a in lands]
                 + [jax.ShapeDtypeStruct((8, 128), F32)])
    args = [pltpu.with_memory_space_constraint(a, pltpu.HBM) for a in srcs + lands] + ([after] if has_after else [])
    res = pl.pallas_call(
        body, name=name, out_shape=out_shape,
        in_specs=[_HBM] * (2 * n) + ([_ANY] if has_after else []),
        out_specs=[_SEM, _SEM] + [_HBM] * (2 * n) + [pl.BlockSpec(memory_space=pltpu.VMEM)],
        input_output_aliases={i: 2 + i for i in range(2 * n)},
        compiler_params=pltpu.CompilerParams(has_side_effects=_EFFECT),
    )(*args)
    state = (res[0], res[1], list(res[2:2 + n]), list(res[2 + n:2 + 2 * n]), is_piece)
    return state, res[-1]


def _xchg_wait(name, state, after):
    send_sem, recv_sem, srcs, lands, is_piece = state
    n = len(srcs)

    def body(*refs):
        src_refs = refs[:n]
        land_refs = refs[n:2 * n]
        s_sem, r_sem = refs[2 * n], refs[2 * n + 1]
        me, peers = _peer_table()
        for kk in range(len(_FLIPS)):
            for a in range(n):
                cp = _split_copy(is_piece, src_refs, land_refs, s_sem, r_sem, a, kk, me, peers, True)
                cp.wait_send()
                cp.wait_recv()

    out_shape = [pltpu.HBM(tuple(a.shape), a.dtype) for a in srcs] + [pltpu.HBM(tuple(a.shape), a.dtype) for a in lands]
    res = pl.pallas_call(
        body, name=name, out_shape=out_shape,
        in_specs=[_HBM] * (2 * n) + [_SEM, _SEM, _ANY], out_specs=[_HBM] * (2 * n),
        input_output_aliases={i: i for i in range(2 * n)},
        compiler_params=pltpu.CompilerParams(has_side_effects=_EFFECT),
    )(*srcs, *lands, send_sem, recv_sem, after)
    return list(res[n:]), list(res[:n])


_SIB = (0, 0, 1)
_ICI = ((0, 1, 0), (1, 0, 0), (1, 1, 0))


def _flip(fl):
    x, y, c = lax.axis_index("x"), lax.axis_index("y"), lax.axis_index("c")
    px = 1 - x if fl[0] else x
    py = 1 - y if fl[1] else y
    pc = 1 - c if fl[2] else c
    return (px, py, pc), 4 * px + 2 * py + pc


def _gather2_start(name, pieces, after=None):
    me_out = 4 * lax.axis_index("x") + 2 * lax.axis_index("y") + lax.axis_index("c")
    pieces = list(pieces)
    n = len(pieces)
    lands = [lax.dynamic_update_slice(lax.empty((N_DEV,) + tuple(a.shape), a.dtype), a[None],
                                      (me_out,) + (0,) * a.ndim) for a in pieces]
    first = (_SIB,) + _ICI
    has_after = after is not None

    def body(*refs):
        src_refs, land_refs = refs[:n], refs[n:2 * n]
        outs = refs[2 * n + (1 if has_after else 0):]
        send_sem, recv_sem, token = outs[0], outs[1], outs[-1]
        _, me = _flip((0, 0, 0))
        for kk, fl in enumerate(first):
            dev, _ = _flip(fl)
            for a in range(n):
                pltpu.make_async_remote_copy(src_ref=src_refs[a], dst_ref=land_refs[a].at[me],
                                             send_sem=send_sem.at[a * 4 + kk], recv_sem=recv_sem.at[a * 4 + kk],
                                             device_id=dev, device_id_type=MESH_ID).start()
        token[...] = jnp.zeros_like(token)

    hbm = [pltpu.HBM(tuple(a.shape), a.dtype) for a in pieces + lands]
    res = pl.pallas_call(
        body, name=name,
        out_shape=[pltpu.SemaphoreType.DMA((4 * n,)), pltpu.SemaphoreType.DMA((4 * n,))] + hbm
        + [jax.ShapeDtypeStruct((8, 128), F32)],
        in_specs=[_HBM] * (2 * n) + ([_ANY] if has_after else []),
        out_specs=[_SEM, _SEM] + [_HBM] * (2 * n) + [pl.BlockSpec(memory_space=pltpu.VMEM)],
        input_output_aliases={i: 2 + i for i in range(2 * n)},
        compiler_params=pltpu.CompilerParams(has_side_effects=_EFFECT),
    )(*([pltpu.with_memory_space_constraint(a, pltpu.HBM) for a in pieces + lands] + ([after] if has_after else [])))
    return (res[0], res[1], list(res[2:2 + n]), list(res[2 + n:2 + 2 * n])), res[-1]


def _gather2_forward(name, state, after):
    send_a, recv_a, pieces, lands = state
    n = len(pieces)
    first = (_SIB,) + _ICI

    def body(*refs):
        src_refs, land_refs = refs[:n], refs[n:2 * n]
        s_a, r_a = refs[2 * n], refs[2 * n + 1]
        outs = refs[2 * n + 3:]
        send_b, recv_b, token = outs[0], outs[1], outs[-1]
        for kk, fl in enumerate(first):
            dev, lin = _flip(fl)
            for a in range(n):
                cp = pltpu.make_async_remote_copy(src_ref=src_refs[a], dst_ref=land_refs[a].at[lin],
                                                  send_sem=s_a.at[a * 4 + kk], recv_sem=r_a.at[a * 4 + kk],
                                                  device_id=dev, device_id_type=MESH_ID)
                cp.wait_send()
                cp.wait_recv()
        sib, _ = _flip(_SIB)
        for j, fl in enumerate(_ICI):
            _, lin = _flip(fl)
            for a in range(n):
                pltpu.make_async_remote_copy(src_ref=land_refs[a].at[lin], dst_ref=land_refs[a].at[lin],
                                             send_sem=send_b.at[a * 3 + j], recv_sem=recv_b.at[a * 3 + j],
                                             device_id=sib, device_id_type=MESH_ID).start()
        token[...] = jnp.zeros_like(token)

    hbm = [pltpu.HBM(tuple(a.shape), a.dtype) for a in pieces + lands]
    res = pl.pallas_call(
        body, name=name,
        out_shape=[pltpu.SemaphoreType.DMA((3 * n,)), pltpu.SemaphoreType.DMA((3 * n,))] + hbm
        + [jax.ShapeDtypeStruct((8, 128), F32)],
        in_specs=[_HBM] * (2 * n) + [_SEM, _SEM, _ANY],
        out_specs=[_SEM, _SEM] + [_HBM] * (2 * n) + [pl.BlockSpec(memory_space=pltpu.VMEM)],
        input_output_aliases={i: 2 + i for i in range(2 * n)},
        compiler_params=pltpu.CompilerParams(has_side_effects=_EFFECT),
    )(*pieces, *lands, send_a, recv_a, after)
    return (res[0], res[1], list(res[2 + n:2 + 2 * n])), res[-1]


def _gather2_wait(name, state, after):
    send_b, recv_b, lands = state
    n = len(lands)

    def body(*refs):
        land_refs = refs[:n]
        s_b, r_b = refs[n], refs[n + 1]
        sib, _ = _flip(_SIB)
        for j, fl in enumerate(_ICI):
            _, sent = _flip(fl)
            _, arriving = _flip((fl[0], fl[1], 1))
            for a in range(n):
                cp = pltpu.make_async_remote_copy(src_ref=land_refs[a].at[sent], dst_ref=land_refs[a].at[arriving],
                                                  send_sem=s_b.at[a * 3 + j], recv_sem=r_b.at[a * 3 + j],
                                                  device_id=sib, device_id_type=MESH_ID)
                cp.wait_send()
                cp.wait_recv()

    res = pl.pallas_call(
        body, name=name, out_shape=[pltpu.HBM(tuple(a.shape), a.dtype) for a in lands],
        in_specs=[_HBM] * n + [_SEM, _SEM, _ANY], out_specs=[_HBM] * n,
        input_output_aliases={i: i for i in range(n)},
        compiler_params=pltpu.CompilerParams(has_side_effects=_EFFECT),
    )(*lands, send_b, recv_b, after)
    return list(res)


def _adam_math(w, g, m, v):
    m = ADAM_B1 * m + (1.0 - ADAM_B1) * g
    v = ADAM_B2 * v + (1.0 - ADAM_B2) * (g * g)
    m_hat = m / (1.0 - ADAM_B1 ** ADAM_STEP)
    v_hat = v / (1.0 - ADAM_B2 ** ADAM_STEP)
    delta = -ADAM_LR * (m_hat / (jnp.sqrt(v_hat) + ADAM_EPS) + ADAM_WD * w)
    return delta, m, v


def _adam(name, w, m, v, parts, transposed=False, dep=None, own=None):
    npart, _, cp = parts.shape
    has_dep = dep is not None
    has_own = own is not None
    if transposed:
        c, r = w.shape
        tr = _pick(r, (256, 128))
        blk = pl.BlockSpec((c, tr), lambda i, *_: (0, i))
    else:
        r, c = w.shape
        tr = _pick(r, (256, 176, 128, 64, 16, 8, 1))
        blk = pl.BlockSpec((tr, c), lambda i, *_: (i, 0))

    def body(*refs):
        refs = list(refs)
        me = refs.pop(0)[0] if has_own else None
        w_ref, m_ref, v_ref, p_ref = refs[:4]
        own_ref = refs[4] if has_own else None
        g_ref, d_ref, mo_ref, vo_ref = refs[-4:]
        g = None
        for pp in range(npart):
            part = p_ref[pp] if not has_own else jnp.where(me == pp, own_ref[...], p_ref[pp])
            g = part.astype(F32) if g is None else g + part.astype(F32)
        g = g.T[0:c, :] if transposed else g[:, 0:c]
        delta, mn, vn = _adam_math(w_ref[...], g, m_ref[...], v_ref[...])
        g_ref[...] = g
        d_ref[...] = delta
        mo_ref[...] = mn
        vo_ref[...] = vn

    in_specs = [blk, blk, blk, pl.BlockSpec((npart, tr, cp), lambda i, *_: (0, i, 0))]
    args = [w, m, v, parts]
    if has_own:
        in_specs.append(pl.BlockSpec((None, tr, cp), lambda i, me_ref: (me_ref[0], i, 0)))
        args.append(own[0])
    if has_dep:
        in_specs.append(pl.BlockSpec(memory_space=pl.ANY))
        args.append(dep)
    out = jax.ShapeDtypeStruct(w.shape, F32)
    grid_spec = pltpu.PrefetchScalarGridSpec(num_scalar_prefetch=1 if has_own else 0, grid=(r // tr,),
                                             in_specs=in_specs, out_specs=[blk, blk, blk, blk])
    return pl.pallas_call(
        body, name=name, grid_spec=grid_spec, out_shape=[out, out, out, out], compiler_params=_params(("parallel",)),
    )(*(([own[1]] if has_own else []) + args))


def _small_update(name, packs, wmv):
    nparam = len(wmv)

    def body(p_ref, *refs):
        ins, outs = refs[:3 * nparam], refs[3 * nparam:]
        tot = p_ref[0]
        for pp in range(1, N_DEV):
            tot = tot + p_ref[pp]
        outs[0][0:8, :] = tot[0:8, :]
        outs[0][8:24, :] = tot[8:24, :] + tot[24:40, :]
        grads = [tot[i:i + 1, :] for i in range(4)] + [tot[4:5, 0:B_HEADS], tot[4:5, B_HEADS:2 * B_HEADS]]
        for i, g in enumerate(grads):
            w_ref, m_ref, v_ref = ins[3 * i:3 * i + 3]
            delta, mn, vn = _adam_math(w_ref[...], g, m_ref[...], v_ref[...])
            for o_ref, val in zip(outs[1 + 4 * i:5 + 4 * i], (g, delta, mn, vn)):
                o_ref[...] = val

    flat = [a for trio in wmv for a in trio]
    out_shape = [jax.ShapeDtypeStruct((24, D_MODEL), F32)]
    for w, _, _ in wmv:
        out_shape += [jax.ShapeDtypeStruct(w.shape, F32)] * 4
    res = pl.pallas_call(body, name=name, out_shape=out_shape, compiler_params=_params())(packs, *flat)
    return res[0], [tuple(res[1 + 4 * i:5 + 4 * i]) for i in range(nparam)]


def _local_step(x, tgt, g1, gm, g2, gf, b_forget, sinks, weights, send):
    b, s, _ = x.shape
    l = s + PREFIX
    t = b * l
    (meta,) = weights("meta", x)

    h0, n1 = _embed_norm("embed_rms1_fwd", x, meta, g1)
    (w1i,) = weights("ffn1_in", n1)
    gu1, a1 = _ffn_in_fwd("ffn1_in_fwd", n1, w1i)
    (w1o,) = weights("ffn1_out", weights("ffn1_out:forward", a1))
    h1, um = _mm_res_norm("ffn1_out_fwd", a1, w1o, h0, gm, alpha=0.5)
    wi, wa, wb, wo = weights("mix", weights("mix:forward", um))
    qkv, gates, f2 = _proj_fwd("proj_fwd", um, wi)
    qkv3 = qkv.reshape(b, l, QKV_W)
    f3 = f2.reshape(b, l, 128)
    bf_row = jnp.pad(b_forget, ((0, 0), (0, 128 - B_HEADS)))
    c_col, c_row = _fgate_fwd("fgate_fwd", f3, bf_row)
    head_of_row = jnp.arange(STACK) // BLOCK
    slopes = jnp.exp2(-8.0 * (head_of_row + 1).astype(F32) / A_HEADS).reshape(STACK, 1)
    sink_rows = jnp.repeat(sinks.reshape(A_HEADS), BLOCK).reshape(STACK, 1)
    swa_bias = _swa_bias(slopes)
    oa3, lse_a = _swa_fwd("swa_fwd", qkv3, slopes, sink_rows, swa_bias)
    ob3, lse_b = _fox_fwd("fox_fwd", qkv3, c_col, c_row)
    oa = oa3.reshape(t, A_WIDTH)
    ob = ob3.reshape(t, B_WIDTH)
    mixed, ya, yb = _branch_gate_fwd("branch_gate_fwd", oa, ob, wa, wb, gates, dep=weights("ffn2:forward", ob))
    h2, n2 = _mm_res_norm("mix_out_fwd", mixed, wo, h1, g2)
    w2i, w2o = weights("ffn2", h2)
    gu2, a2 = _ffn_in_fwd("ffn2_in_fwd", n2, w2i)

    dh3, loss_blk, dgf = _ffn_out_loss("ffn2_out_loss", a2, w2o, h2, gf, tgt, 0.5)

    def ffn_bwd(tag, dh, h_in, g_norm, n_in, gu, a, w_in_blk, w_out, one_send, examples=None):
        dw_out = _ffn_out_bwd_w(tag + "_out_bwd_w", a, dh)
        dgu = _ffn_out_bwd_x(tag + "_out_bwd_x", dh, w_out, gu, dep=None if one_send else send(tag + "_out", (dw_out,)))
        dw_in = _ffn_in_bwd_w(tag + "_in_bwd_w", n_in, dgu)
        token = send(tag, (dw_in, dw_out)) if one_send else send(tag + "_in", (dw_in,))
        return _ffn_in_bwd_x(tag + "_in_bwd_x", dgu, w_in_blk, h_in, g_norm, dh, dep=token, examples=examples)

    dh2, dg2 = ffn_bwd("ffn2", dh3, h2, g2, n2, gu2, a2, w2i, w2o, True)

    dwo = _mm_tn("mix_out_bwd_w", mixed, dh2)
    dya, dyb, dgates = _mix_out_gate_bwd("mix_out_gate_bwd", dh2, wo, gates, ya, yb)
    doa, dob = _branch_bwd_x("branch_bwd_x", dya, dyb, wa, wb)
    dwa, dwb = _branch_bwd_w("branch_bwd_w", oa, ob, dya, dyb)
    dqkv3, dcq, dck = _fox_bwd("fox_bwd", qkv3, c_col, c_row, ob3, lse_b, dob.reshape(b, l, B_WIDTH))
    dqkv3, dsink = _swa_bwd("swa_bwd", qkv3, oa3, lse_a, doa.reshape(b, l, A_WIDTH), slopes, sink_rows, swa_bias, dqkv3)
    dqkv = dqkv3.reshape(t, QKV_W)
    df3, dbf = _fgate_bwd("fgate_bwd", f3, bf_row, dcq, dck)
    df = df3.reshape(t, 128)
    dwi_qkv = _mm_tn("proj_qkv_bwd_w", um, dqkv, tm_c=(512,), tn_c=(768,))
    dwi_g = _mm_tn("proj_gates_bwd_w", um, dgates, tm_c=(512,), tn_c=(512,))
    dwi_f = _mm_tn("proj_f_bwd_w", um, df, tm_c=(512,), tn_c=(128,))
    token = send("mix", (dwi_qkv, dwi_g, dwi_f, dwa, dwb, dwo))
    dh1, dgm = _proj_bwd_x("proj_bwd_x", dqkv, dgates, df, wi, h1, gm, dh2, dep=token)

    grad_x, dmeta3, dg1 = ffn_bwd("ffn1", dh1, h0, g1, n1, gu1, a1, w1i, w1o, False, examples=b)
    dmeta = dmeta3.reshape(b * N_META, D_MODEL)

    misc = jnp.concatenate([dbf[:, 0:B_HEADS], dsink[:, 0].reshape(1, A_HEADS), loss_blk[0:1, 0:1]], axis=1)
    misc = jnp.pad(misc, ((0, 0), (0, D_MODEL - misc.shape[1])))
    row = lax.broadcasted_iota(jnp.int32, (8, D_MODEL), 0)
    vec = jnp.zeros((8, D_MODEL), F32)
    for i, piece in enumerate((dg1, dgm, dg2, dgf, misc)):
        vec = jnp.where(row == i, piece, vec)
    small = jnp.concatenate([vec, dmeta], axis=0)
    return grad_x, small


def _pad_to(a, rows, cols):
    return jnp.pad(a, ((0, rows - a.shape[0]), (0, cols - a.shape[1])))


def _ffn_out_from_gathered(name, g):
    def body(g_ref, o_ref):
        o_ref[0:FFO_SHARD, :] = g_ref[0]
        o_ref[FFO_SHARD:FF_SHARD, :] = g_ref[1]
        o_ref[FF_SHARD:FF_SHARD_P, :] = jnp.zeros((FF_SHARD_P - FF_SHARD, D_MODEL), BF16)

    return pl.pallas_call(
        body, name=name, grid=(4,),
        in_specs=[pl.BlockSpec((2, FFO_SHARD, D_MODEL), lambda j: (j, 0, 0))],
        out_specs=pl.BlockSpec((FF_SHARD_P, D_MODEL), lambda j: (j, 0)),
        out_shape=jax.ShapeDtypeStruct((D_FF_P, D_MODEL), BF16), compiler_params=_params(("parallel",)),
    )(g)


def _ffn_out_bwd_w(name, a, dh):
    t = a.shape[0]
    tn = D_MODEL // 2

    def body(a_ref, b_ref, o_ref):
        acc = lax.dot_general(a_ref[...], _bf(b_ref[...]), _TN, preferred_element_type=F32) * 0.5
        o_ref[0] = acc[0:FFO_SHARD].astype(BF16)
        o_ref[1] = acc[FFO_SHARD:FF_SHARD].astype(BF16)

    return pl.pallas_call(
        body, name=name, grid=(2, 4),
        in_specs=[pl.BlockSpec((t, FF_SHARD_P), lambda j, i: (0, i)), pl.BlockSpec((t, tn), lambda j, i: (0, j))],
        out_specs=pl.BlockSpec((2, FFO_SHARD, tn), lambda j, i: (i, 0, j)),
        out_shape=jax.ShapeDtypeStruct((N_DEV, FFO_SHARD, D_MODEL), BF16), compiler_params=_params(("parallel", "parallel")),
    )(a, dh)


def _proj_segments():
    segs = [(HEAD_DIM * h, HEAD_DIM, 0, B_SEG + HEAD_DIM * A_HEAD_ORDER.index(h)) for h in range(A_HEADS)]
    segs += [(512, 128, 0, B_SEG + A_WIDTH), (640, 128, 0, B_SEG + A_WIDTH + 128)]
    for first, off in ((768, 0), (1280, 128), (1792, 256)):
        segs += [(first + 128 * hp, 128, 0, PAIR_W * hp + off) for hp in range(4)]
    segs += [(2304, B_HEADS, 2, 0), (2312, 2 * D_MODEL, 1, 0)]
    return segs


_RELAYOUT_ROWS = 256


def _proj_from_gathered(name, g):
    rows = _RELAYOUT_ROWS

    def body(g_ref, o_ref):
        def cols(first, width):
            out = []
            for p in range(N_DEV):
                lo, hi = max(first, WIN_SHARD * p), min(first + width, WIN_SHARD * (p + 1))
                if lo < hi:
                    out.append(g_ref[p, :, lo - WIN_SHARD * p:hi - WIN_SHARD * p])
            return out

        parts = []
        for arr in (0, 1, 2):
            for first, width, _, _ in sorted((s for s in _proj_segments() if s[2] == arr), key=lambda s: s[3]):
                parts += cols(first, width)
            if arr == 0:
                parts.append(jnp.zeros((rows, P_GATES - QKV_W), BF16))
        parts.append(jnp.zeros((rows, 128 - B_HEADS), BF16))
        o_ref[...] = jnp.concatenate(parts, axis=1)

    return pl.pallas_call(
        body, name=name, grid=(D_MODEL // rows,),
        in_specs=[pl.BlockSpec((N_DEV, rows, WIN_SHARD_P), lambda i: (0, i, 0))],
        out_specs=pl.BlockSpec((rows, PROJ_P), lambda i: (i, 0)),
        out_shape=jax.ShapeDtypeStruct((D_MODEL, PROJ_P), BF16), compiler_params=_params(("parallel",)),
    )(g)


def _proj_to_scatter(name, dqkv_w, dg_w, df_w):
    rows = _RELAYOUT_ROWS
    segs = sorted(_proj_segments())

    def body(q_ref, g_ref, f_ref, o_ref):
        arrays = (q_ref, g_ref, f_ref)
        for p in range(N_DEV):
            parts = []
            for first, width, arr, at in segs:
                lo, hi = max(first, WIN_SHARD * p), min(first + width, WIN_SHARD * (p + 1))
                if lo < hi:
                    parts.append(arrays[arr][:, at + lo - first:at + hi - first])
            parts.append(jnp.zeros((rows, WIN_SHARD_P - WIN_SHARD), BF16))
            o_ref[p] = jnp.concatenate(parts, axis=1)

    return pl.pallas_call(
        body, name=name, grid=(D_MODEL // rows,),
        in_specs=[pl.BlockSpec((rows, QKV_W), lambda i: (i, 0)), pl.BlockSpec((rows, 2 * D_MODEL), lambda i: (i, 0)),
                  pl.BlockSpec((rows, 128), lambda i: (i, 0))],
        out_specs=pl.BlockSpec((N_DEV, rows, WIN_SHARD_P), lambda i: (0, i, 0)),
        out_shape=jax.ShapeDtypeStruct((N_DEV, D_MODEL, WIN_SHARD_P), BF16), compiler_params=_params(("parallel",)),
    )(dqkv_w, dg_w, df_w)


def _a_rows_from_natural(w):
    return jnp.concatenate([w[HEAD_DIM * h:HEAD_DIM * (h + 1)] for h in A_HEAD_ORDER], axis=0)


def _a_rows_to_natural(w):
    return jnp.concatenate([w[HEAD_DIM * A_HEAD_ORDER.index(h):HEAD_DIM * (A_HEAD_ORDER.index(h) + 1)]
                            for h in range(A_HEADS)], axis=0)


def kernel(x, meta_tokens, ffn1_norm, ffn1_w_in, ffn1_w_out, mix_norm, w_in, b_forget, attn_sinks, w_branch_a, w_branch_b, w_out, ffn2_norm, ffn2_w_in, ffn2_w_out, final_norm, loss_target, m_meta_tokens, m_ffn1_norm, m_ffn1_w_in, m_ffn1_w_out, m_mix_norm, m_w_in, m_b_forget, m_attn_sinks, m_w_branch_a, m_w_branch_b, m_w_out, m_ffn2_norm, m_ffn2_w_in, m_ffn2_w_out, m_final_norm, v_meta_tokens, v_ffn1_norm, v_ffn1_w_in, v_ffn1_w_out, v_mix_norm, v_w_in, v_b_forget, v_attn_sinks, v_w_branch_a, v_w_branch_b, v_w_out, v_ffn2_norm, v_ffn2_w_in, v_ffn2_w_out, v_final_norm):
    me = 4 * lax.axis_index("x") + 2 * lax.axis_index("y") + lax.axis_index("c")

    def shard(w, tok, rows=None, cols=None):
        piece = (w[0] if tok is None else w[0] + tok[0, 0]).astype(BF16)
        return piece if rows is None else _pad_to(piece, rows, cols)

    first_level, second_level = {}, {}
    first_level["meta"], tok = _gather2_start("gather_meta_start", (meta_tokens,))
    first_level["ffn1_in"], tok = _gather2_start(
        "gather_ffn1_in_start", (shard(ffn1_w_in, None, D_MODEL, FF_SHARD_P),), after=tok)
    first_level["ffn1_out"], tok = _gather2_start("gather_ffn1_out_start", (shard(ffn1_w_out, tok),), after=tok)
    first_level["mix"], tok = _gather2_start(
        "gather_mix_start", (shard(w_in, tok, D_MODEL, WIN_SHARD_P), shard(w_branch_a, tok), shard(w_branch_b, tok),
                             shard(w_out, tok)), after=tok)
    first_level["ffn2"], tok = _gather2_start(
        "gather_ffn2_start", (shard(ffn2_w_in, tok, D_MODEL, FF_SHARD_P), shard(ffn2_w_out, tok)), after=tok)
    started = {"tok": tok}

    def weights(group, after):
        if group.endswith(":forward"):
            group = group[:-len(":forward")]
            second_level[group], token = _gather2_forward("gather_" + group + "_forward", first_level[group], after)
            return token
        if group == "meta":
            after = weights("meta:forward", started["tok"])
        if group == "ffn1_in":
            after = weights("ffn1_in:forward", after)
        got = _gather2_wait("gather_" + group + "_wait", second_level[group], after)
        if group == "mix":
            gwi, gwa, gwb, gwo = got
            return (_proj_from_gathered("proj_w_relayout", gwi), _a_rows_from_natural(gwa.transpose(1, 0, 2).reshape(A_WIDTH, D_MODEL)),
                    gwb.transpose(1, 0, 2).reshape(B_WIDTH, D_MODEL), gwo.reshape(D_MODEL, D_MODEL))
        if group == "ffn1_out":
            return (_ffn_out_from_gathered("ffn1_w_out_relayout", got[0]),)
        if group == "meta":
            return (got[0].transpose(1, 0, 2).reshape(N_META, D_MODEL),)
        if group == "ffn1_in":
            return (got[0].reshape(2, 4, D_MODEL, FF_SHARD_P),)
        return got[0].reshape(2, 4, D_MODEL, FF_SHARD_P), _ffn_out_from_gathered("ffn2_w_out_relayout", got[1])

    scatter_state = {}

    def send(group, grads):
        if group == "mix":
            dwi_qkv, dwi_g, dwi_f, dwa, dwb, dwo = grads
            dwa = _a_rows_to_natural(dwa)
            blocks = (_proj_to_scatter("proj_dw_relayout", dwi_qkv, dwi_g, dwi_f), dwa.reshape(A_WIDTH, N_DEV, 128).transpose(1, 0, 2),
                      dwb.reshape(B_WIDTH, N_DEV, 128).transpose(1, 0, 2), dwo.reshape(N_DEV, 128, D_MODEL))
        elif group.endswith("_in"):
            blocks = (grads[0].reshape(N_DEV, D_MODEL, FF_SHARD_P),)
        elif group.endswith("_out"):
            blocks = (grads[0],)
        else:
            blocks = (grads[0].reshape(N_DEV, D_MODEL, FF_SHARD_P), grads[1])
        scatter_state[group], token = _xchg_start("scatter_" + group + "_start", (), blocks)
        return token

    gf = final_norm.reshape(1, D_MODEL)
    grad_x, small = _local_step(x, loss_target, ffn1_norm, mix_norm, ffn2_norm, gf, b_forget, attn_sinks, weights, send)

    small_state, after = _xchg_start("gather_small_start", (small,), ())
    out = {}
    updates = (
        ("ffn2", (("ffn2_w_in", ffn2_w_in, m_ffn2_w_in, v_ffn2_w_in), ("ffn2_w_out", ffn2_w_out, m_ffn2_w_out, v_ffn2_w_out))),
        ("ffn1_out", (("ffn1_w_out", ffn1_w_out, m_ffn1_w_out, v_ffn1_w_out),)),
        ("mix", (("w_in", w_in, m_w_in, v_w_in), ("w_branch_a", w_branch_a, m_w_branch_a, v_w_branch_a),
                 ("w_branch_b", w_branch_b, m_w_branch_b, v_w_branch_b), ("w_out", w_out, m_w_out, v_w_out))),
    )
    last_update = ("ffn1_in", (("ffn1_w_in", ffn1_w_in, m_ffn1_w_in, v_ffn1_w_in),))

    def update(group, members, after):
        parts_list, blocks_list = _xchg_wait("scatter_" + group + "_wait", scatter_state[group], after)
        prev = None
        for (nm, w, m, v), parts, blocks in zip(members, parts_list, blocks_list):
            if nm.endswith("w_in"):
                res4 = _adam("adam_" + nm, w[0].T, m[0].T, v[0].T, parts, transposed=True, dep=prev, own=(blocks, me_arr))
                out[nm] = tuple(r.T[None] for r in res4)
            else:
                res4 = _adam("adam_" + nm, w[0], m[0], v[0], parts, dep=prev, own=(blocks, me_arr))
                out[nm] = tuple(r[None] for r in res4)
            prev = res4[0]
        return prev

    me_arr = me.reshape(1).astype(jnp.int32)
    for group, members in updates + (last_update,):
        after = update(group, members, after)
    (packs,), _ = _xchg_wait("gather_small_wait", small_state, after)

    row = lambda a: a.reshape(1, D_MODEL)
    tot, small_res = _small_update("small_update", packs, (
        (ffn1_norm, m_ffn1_norm, v_ffn1_norm), (mix_norm, m_mix_norm, v_mix_norm), (ffn2_norm, m_ffn2_norm, v_ffn2_norm),
        (row(final_norm), row(m_final_norm), row(v_final_norm)), (b_forget, m_b_forget, v_b_forget),
        (attn_sinks, m_attn_sinks, v_attn_sinks)))
    for nm, res4 in zip(("ffn1_norm", "mix_norm", "ffn2_norm", "final_norm", "b_forget", "attn_sinks"), small_res):
        out[nm] = tuple(r.reshape(D_MODEL) for r in res4) if nm == "final_norm" else res4
    loss = tot[4, 2 * B_HEADS]
    g_meta = lax.dynamic_slice(tot[8:24, :], (0, me * 128), (N_META, 128))
    out["meta_tokens"] = tuple(_adam("adam_meta_tokens", meta_tokens, m_meta_tokens, v_meta_tokens, g_meta[None]))

    names = ("meta_tokens", "ffn1_norm", "ffn1_w_in", "ffn1_w_out", "mix_norm", "w_in", "b_forget", "attn_sinks",
             "w_branch_a", "w_branch_b", "w_out", "ffn2_norm", "ffn2_w_in", "ffn2_w_out", "final_norm")
    return (loss, grad_x) + tuple(out[nm][kind] for kind in range(4) for nm in names)
```

```python
import jax
import jax.numpy as jnp
from jax import lax
from jax.experimental import pallas as pl
from jax.experimental.pallas import tpu as pltpu

F32 = jnp.float32
BF16 = jnp.bfloat16

D_MODEL = 1024
N_META = 16
BLOCK = 128
PREFIX = 128
N_PAD = PREFIX - N_META
HEAD_DIM = 64
A_HEADS = 8
B_HEADS = 8
A_WIDTH = 512
A_KV_WIDTH = 128
B_WIDTH = 512
D_FF = 2816
N_DEV = 8
FF_SHARD = 2 * D_FF // N_DEV
FF_SHARD_P = 768
FFO_SHARD = D_FF // N_DEV
D_FF_P = 4 * FF_SHARD_P
W_IN_COLS = 4360
WIN_SHARD = W_IN_COLS // N_DEV
WIN_SHARD_P = 640
PAIR_W = 3 * 128
B_SEG = 4 * PAIR_W
A_SEG = A_WIDTH + 2 * A_KV_WIDTH
QKV_W = B_SEG + A_SEG
P_GATES = 2 * (2 * D_MODEL)
P_F = P_GATES + 2 * D_MODEL
PROJ_P = P_F + 128
A_HEAD_ORDER = (0, 4, 1, 5, 2, 6, 3, 7)
EPS = 1e-6
NEG = -1e30
SCALE = HEAD_DIM ** -0.5
ADAM_LR = 0.001
ADAM_B1 = 0.9
ADAM_B2 = 0.999
ADAM_EPS = 1e-08
ADAM_WD = 0.01
ADAM_STEP = 10
VMEM_LIMIT = 56 * 1024 * 1024
MESH_ID = pl.DeviceIdType.MESH
SMALL_ROWS = 40

_NN = (((1,), (0,)), ((), ()))
_NT = (((1,), (1,)), ((), ()))
_TN = (((0,), (0,)), ((), ()))


def _params(sem=None):
    return pltpu.CompilerParams(dimension_semantics=sem, vmem_limit_bytes=VMEM_LIMIT)


def _pick(n, cands):
    for c in cands:
        if n % c == 0:
            return c
    raise ValueError(f"no tile for {n}")


def _bf(v):
    return v if v.dtype == BF16 else v.astype(BF16)


def _mm(name, a, b, dims, grid, a_spec, b_spec, o_spec, out_shape, out_dtype, alpha=1.0):
    def body(a_ref, b_ref, o_ref):
        acc = lax.dot_general(_bf(a_ref[...]), _bf(b_ref[...]), dims, preferred_element_type=F32)
        if alpha != 1.0:
            acc = acc * alpha
        o_ref[...] = acc.astype(o_ref.dtype)

    return pl.pallas_call(
        body, name=name, grid=grid, in_specs=[a_spec, b_spec], out_specs=o_spec,
        out_shape=jax.ShapeDtypeStruct(out_shape, out_dtype),
        compiler_params=_params(("parallel",) * len(grid)),
    )(a, b)


def _mm_res_norm(name, a, w, res, g_next, alpha=1.0):
    t, k = a.shape
    tm = _pick(t, (544, 384, 256, 128))

    def body(a_ref, w_ref, r_ref, g_ref, h_ref, n_ref):
        acc = lax.dot_general(_bf(a_ref[...]), w_ref[...], _NN, preferred_element_type=F32)
        if alpha != 1.0:
            acc = acc * alpha
        hv = acc + r_ref[...]
        h_ref[...] = hv
        r = lax.rsqrt(jnp.mean(hv * hv, axis=-1, keepdims=True) + EPS)
        n_ref[...] = ((hv * r) * g_ref[...]).astype(BF16)

    row = pl.BlockSpec((tm, D_MODEL), lambda i: (i, 0))
    return pl.pallas_call(
        body, name=name, grid=(t // tm,),
        in_specs=[pl.BlockSpec((tm, k), lambda i: (i, 0)), pl.BlockSpec((k, D_MODEL), lambda i: (0, 0)), row,
                  pl.BlockSpec((1, D_MODEL), lambda i: (0, 0))],
        out_specs=[row, row],
        out_shape=[jax.ShapeDtypeStruct((t, D_MODEL), F32), jax.ShapeDtypeStruct((t, D_MODEL), BF16)],
        compiler_params=_params(("parallel",)),
    )(a, w, res, g_next)


def _mm_tn(name, a, b, out_dtype=BF16, alpha=1.0, tm_c=(768, 512, 256, 128), tn_c=(512, 640, 256, 128)):
    t, m = a.shape
    n = b.shape[1]
    tm = _pick(m, tm_c)
    tn = _pick(n, tn_c)
    bytes_a, bytes_b = a.size * a.dtype.itemsize, b.size * b.dtype.itemsize
    if bytes_a + bytes_b * (m // tm) <= bytes_b + bytes_a * (n // tn):
        return _mm(name, a, b, _TN, (m // tm, n // tn),
                   pl.BlockSpec((t, tm), lambda i, j: (0, i)), pl.BlockSpec((t, tn), lambda i, j: (0, j)),
                   pl.BlockSpec((tm, tn), lambda i, j: (i, j)), (m, n), out_dtype, alpha=alpha)
    return _mm(name, a, b, _TN, (n // tn, m // tm),
               pl.BlockSpec((t, tm), lambda j, i: (0, i)), pl.BlockSpec((t, tn), lambda j, i: (0, j)),
               pl.BlockSpec((tm, tn), lambda j, i: (i, j)), (m, n), out_dtype, alpha=alpha)


def _ffn_in_fwd(name, n, wblk, dep=None):
    t = n.shape[0]
    tm = _pick(t, (1088, 768, 512, 256, 128))
    has_dep = dep is not None

    def body(n_ref, w_ref, *rest):
        gu_ref, a_ref = rest[-2], rest[-1]
        nv = n_ref[...]
        g = lax.dot_general(nv, w_ref[0], _NN, preferred_element_type=F32)
        u = lax.dot_general(nv, w_ref[1], _NN, preferred_element_type=F32)
        sg = jax.nn.sigmoid(g)
        silu = g * sg
        a_ref[...] = (silu * u).astype(BF16)
        gu_ref[0] = ((0.5 * u) * (sg + silu * (1.0 - sg))).astype(BF16)
        gu_ref[1] = (0.5 * silu).astype(BF16)

    return pl.pallas_call(
        body, name=name, grid=(t // tm, 4),
        in_specs=[pl.BlockSpec((tm, D_MODEL), lambda i, j: (i, 0)),
                  pl.BlockSpec((2, None, D_MODEL, FF_SHARD_P), lambda i, j: (0, j, 0, 0))]
        + ([pl.BlockSpec(memory_space=pl.ANY)] if has_dep else []),
        out_specs=[pl.BlockSpec((2, tm, FF_SHARD_P), lambda i, j: (0, i, j)),
                   pl.BlockSpec((tm, FF_SHARD_P), lambda i, j: (i, j))],
        out_shape=[jax.ShapeDtypeStruct((2, t, D_FF_P), BF16), jax.ShapeDtypeStruct((t, D_FF_P), BF16)],
        compiler_params=_params(("parallel", "parallel")),
    )(*((n, wblk) + ((dep,) if has_dep else ())))


def _ffn_out_bwd_x(name, dh, w_out, gu, dep=None):
    t = dh.shape[0]
    tm = _pick(t, (1088, 768, 512, 256, 128))
    has_dep = dep is not None

    def body(dh_ref, w_ref, gu_ref, *rest):
        o_ref = rest[-1]
        da = lax.dot_general(_bf(dh_ref[...]), w_ref[...], _NT, preferred_element_type=F32)
        o_ref[0] = (da * gu_ref[0].astype(F32)).astype(BF16)
        o_ref[1] = (da * gu_ref[1].astype(F32)).astype(BF16)

    gu_spec = pl.BlockSpec((2, tm, FF_SHARD_P), lambda i, j: (0, i, j))
    return pl.pallas_call(
        body, name=name, grid=(t // tm, 4),
        in_specs=[pl.BlockSpec((tm, D_MODEL), lambda i, j: (i, 0)), pl.BlockSpec((FF_SHARD_P, D_MODEL), lambda i, j: (j, 0)),
                  gu_spec] + ([pl.BlockSpec(memory_space=pl.ANY)] if has_dep else []),
        out_specs=gu_spec, out_shape=jax.ShapeDtypeStruct((2, t, D_FF_P), BF16),
        compiler_params=_params(("parallel", "parallel")),
    )(*((dh, w_out, gu) + ((dep,) if has_dep else ())))


def _rms_bwd_rows(dn, h, g, dres):
    r = lax.rsqrt(jnp.mean(h * h, axis=-1, keepdims=True) + EPS)
    tv = dn * g
    dot = jnp.mean(tv * h, axis=-1, keepdims=True)
    return dres + (r * tv - h * (r * r * r * dot)), jnp.sum(dn * (h * r), axis=0, keepdims=True)


def _accumulate_rows(ref, part, first):
    @pl.when(first)
    def _():
        ref[...] = part

    @pl.when(jnp.logical_not(first))
    def _():
        ref[...] += part


def _ffn_in_bwd_x(name, dgu, wblk, h_in, g_norm, dres, dep=None, examples=None):
    t = dgu.shape[1]
    tm = _pick(t, (544, 384, 256, 128))
    has_dep = dep is not None
    split = examples is not None
    if split:
        l = t // examples
        per = l // tm
        assert per * tm == l and tm > PREFIX

    def body(d_ref, w_ref, h_ref, g_ref, r_ref, *rest):
        acc = None
        for s in range(2):
            for j in range(4):
                part = lax.dot_general(d_ref[s, :, FF_SHARD_P * j:FF_SHARD_P * (j + 1)], w_ref[s, j], _NT,
                                       preferred_element_type=F32)
                acc = part if acc is None else acc + part
        dh, dg = _rms_bwd_rows(acc, h_ref[...], g_ref[...], r_ref[...])
        i = pl.program_id(0)
        if not split:
            dh_ref, dhb_ref, dg_ref = rest[-3:]
            dh_ref[...] = dh
            dhb_ref[...] = dh.astype(BF16)
        else:
            gx_ref, meta_ref, dg_ref, buf, sem = rest[-5:]

            def out_copy(step):
                bi, r = step // per, step % per
                head = pltpu.make_async_copy(buf.at[pl.ds(PREFIX, tm - PREFIX)], gx_ref.at[bi, pl.ds(0, tm - PREFIX)], sem)
                if per == 1:
                    return r == 0, head, None
                return r == 0, head, pltpu.make_async_copy(
                    buf, gx_ref.at[bi, pl.ds(pl.multiple_of(jnp.maximum(r, 1) * tm - PREFIX, 8), tm)], sem)

            def run(step, method):
                is_head, head, later = out_copy(step)

                @pl.when(is_head)
                def _():
                    getattr(head, method)()

                if later is not None:
                    @pl.when(jnp.logical_not(is_head))
                    def _():
                        getattr(later, method)()

            @pl.when(i > 0)
            def _():
                run(i - 1, "wait")

            buf[...] = dh

            @pl.when(i % per == 0)
            def _():
                meta_ref[...] = dh[N_PAD:PREFIX]

            run(i, "start")

            @pl.when(i == pl.num_programs(0) - 1)
            def _():
                run(i, "wait")

        _accumulate_rows(dg_ref, dg, i == 0)

    row = pl.BlockSpec((tm, D_MODEL), lambda i: (i, 0))
    vec = pl.BlockSpec((1, D_MODEL), lambda i: (0, 0))
    if split:
        out_specs = [pl.BlockSpec(memory_space=pl.ANY), pl.BlockSpec((None, N_META, D_MODEL), lambda i: (i // per, 0, 0)), vec]
        out_shape = [jax.ShapeDtypeStruct((examples, l - PREFIX, D_MODEL), F32),
                     jax.ShapeDtypeStruct((examples, N_META, D_MODEL), F32), jax.ShapeDtypeStruct((1, D_MODEL), F32)]
        scratch = [pltpu.VMEM((tm, D_MODEL), F32), pltpu.SemaphoreType.DMA(())]
    else:
        out_specs = [row, row, vec]
        out_shape = [jax.ShapeDtypeStruct((t, D_MODEL), F32), jax.ShapeDtypeStruct((t, D_MODEL), BF16),
                     jax.ShapeDtypeStruct((1, D_MODEL), F32)]
        scratch = []
    return pl.pallas_call(
        body, name=name, grid=(t // tm,),
        in_specs=[pl.BlockSpec((2, tm, D_FF_P), lambda i: (0, i, 0)),
                  pl.BlockSpec((2, 4, D_MODEL, FF_SHARD_P), lambda i: (0, 0, 0, 0), pipeline_mode=pl.Buffered(1)),
                  row, vec, row]
        + ([pl.BlockSpec(memory_space=pl.ANY)] if has_dep else []),
        out_specs=out_specs, out_shape=out_shape, scratch_shapes=scratch,
        compiler_params=_params(("arbitrary",)),
    )(*((dgu, wblk, h_in, g_norm, dres) + ((dep,) if has_dep else ())))


def _proj_fwd(name, um, wi):
    t = um.shape[0]
    tm = _pick(t, (544, 384, 256, 128))

    def body(u_ref, w_ref, q_ref, g_ref, f_ref):
        uv = u_ref[...]
        q_ref[...] = lax.dot_general(uv, w_ref[:, 0:QKV_W], _NN, preferred_element_type=F32).astype(BF16)
        g_ref[...] = lax.dot_general(uv, w_ref[:, P_GATES:P_F], _NN, preferred_element_type=F32).astype(BF16)
        f_ref[...] = lax.dot_general(uv, w_ref[:, P_F:PROJ_P], _NN, preferred_element_type=F32)

    return pl.pallas_call(
        body, name=name, grid=(t // tm,),
        in_specs=[pl.BlockSpec((tm, D_MODEL), lambda i: (i, 0)),
                  pl.BlockSpec((D_MODEL, PROJ_P), lambda i: (0, 0), pipeline_mode=pl.Buffered(1))],
        out_specs=[pl.BlockSpec((tm, QKV_W), lambda i: (i, 0)), pl.BlockSpec((tm, 2 * D_MODEL), lambda i: (i, 0)),
                   pl.BlockSpec((tm, 128), lambda i: (i, 0))],
        out_shape=[jax.ShapeDtypeStruct((t, QKV_W), BF16), jax.ShapeDtypeStruct((t, 2 * D_MODEL), BF16),
                   jax.ShapeDtypeStruct((t, 128), F32)],
        compiler_params=_params(("parallel",)),
    )(um, wi)


def _proj_bwd_x(name, dqkv, dgates, df, wi, h_in, g_norm, dres, dep=None):
    t = dqkv.shape[0]
    tm = _pick(t, (544, 384, 256, 128))
    has_dep = dep is not None

    def body(q_ref, gt_ref, f_ref, w_ref, h_ref, g_ref, r_ref, *rest):
        dh_ref, dhb_ref, dg_ref = rest[-3:]
        acc = lax.dot_general(q_ref[...], w_ref[:, 0:QKV_W], _NT, preferred_element_type=F32)
        acc = acc + lax.dot_general(gt_ref[...], w_ref[:, P_GATES:P_F], _NT, preferred_element_type=F32)
        acc = acc + lax.dot_general(f_ref[...], w_ref[:, P_F:PROJ_P], _NT, preferred_element_type=F32)
        dh, dg = _rms_bwd_rows(acc, h_ref[...], g_ref[...], r_ref[...])
        dh_ref[...] = dh
        dhb_ref[...] = dh.astype(BF16)
        _accumulate_rows(dg_ref, dg, pl.program_id(0) == 0)

    row = pl.BlockSpec((tm, D_MODEL), lambda i: (i, 0))
    vec = pl.BlockSpec((1, D_MODEL), lambda i: (0, 0))
    return pl.pallas_call(
        body, name=name, grid=(t // tm,),
        in_specs=[pl.BlockSpec((tm, QKV_W), lambda i: (i, 0)), pl.BlockSpec((tm, 2 * D_MODEL), lambda i: (i, 0)),
                  pl.BlockSpec((tm, 128), lambda i: (i, 0)),
                  pl.BlockSpec((D_MODEL, PROJ_P), lambda i: (0, 0), pipeline_mode=pl.Buffered(1)), row, vec, row]
        + ([pl.BlockSpec(memory_space=pl.ANY)] if has_dep else []),
        out_specs=[row, row, vec],
        out_shape=[jax.ShapeDtypeStruct((t, D_MODEL), F32), jax.ShapeDtypeStruct((t, D_MODEL), BF16),
                   jax.ShapeDtypeStruct((1, D_MODEL), F32)],
        compiler_params=_params(("arbitrary",)),
    )(*((dqkv, dgates, df, wi, h_in, g_norm, dres) + ((dep,) if has_dep else ())))


def _ffn_in_bwd_w(name, n, dgu):
    t = n.shape[0]
    return _mm(name, n, dgu, _TN, (2, 4),
               pl.BlockSpec((t, D_MODEL), lambda s, j: (0, 0)),
               pl.BlockSpec((None, t, FF_SHARD_P), lambda s, j: (s, 0, j)),
               pl.BlockSpec((None, None, D_MODEL, FF_SHARD_P), lambda s, j: (s, j, 0, 0)),
               (2, 4, D_MODEL, FF_SHARD_P), BF16)


def _embed_norm(name, x, meta, g):
    b, s, _ = x.shape
    half = (s + PREFIX) // 2
    first = half - PREFIX
    assert first > 0 and half % 16 == 0

    def body(x_ref, m_ref, g_ref, h_ref, n_ref, buf, sem):
        bi, k = pl.program_id(0), pl.program_id(1)

        def tokens(example, second, method):
            if second:
                cp = pltpu.make_async_copy(x_ref.at[example, pl.ds(first, half)], buf.at[1], sem.at[1])
            else:
                cp = pltpu.make_async_copy(x_ref.at[example, pl.ds(0, first)], buf.at[0, pl.ds(PREFIX, first)], sem.at[0])
            getattr(cp, method)()

        @pl.when((bi == 0) & (k == 0))
        def _():
            tokens(0, False, "start")

        @pl.when(k == 0)
        def _():
            tokens(bi, True, "start")
            tokens(bi, False, "wait")
            buf[0, 0:N_PAD, :] = jnp.zeros((N_PAD, D_MODEL), F32)
            buf[0, N_PAD:PREFIX, :] = m_ref[...]

        @pl.when(k == 1)
        def _():
            @pl.when(bi + 1 < b)
            def _():
                tokens(bi + 1, False, "start")

            tokens(bi, True, "wait")

        hv = buf[k]
        h_ref[...] = hv
        r = lax.rsqrt(jnp.mean(hv * hv, axis=-1, keepdims=True) + EPS)
        n_ref[...] = ((hv * r) * g_ref[...]).astype(BF16)

    rows = pl.BlockSpec((half, D_MODEL), lambda bi, k: (2 * bi + k, 0))
    return pl.pallas_call(
        body, name=name, grid=(b, 2),
        in_specs=[pl.BlockSpec(memory_space=pl.ANY), pl.BlockSpec((N_META, D_MODEL), lambda bi, k: (0, 0)),
                  pl.BlockSpec((1, D_MODEL), lambda bi, k: (0, 0))],
        out_specs=[rows, rows],
        out_shape=[jax.ShapeDtypeStruct((2 * b * half, D_MODEL), F32), jax.ShapeDtypeStruct((2 * b * half, D_MODEL), BF16)],
        scratch_shapes=[pltpu.VMEM((2, half, D_MODEL), F32), pltpu.SemaphoreType.DMA((2,))],
        compiler_params=_params(("arbitrary", "arbitrary")),
    )(x, meta, g)


def _branch_gate_fwd(name, oa, ob, wa, wb, gates, dep=None):
    t = gates.shape[0]
    tm = _pick(t, (544, 384, 256, 128))
    has_dep = dep is not None

    def body(oa_ref, ob_ref, wa_ref, wb_ref, g_ref, *rest):
        o_ref, ya_ref, yb_ref = rest[-3:]
        ya = lax.dot_general(_bf(oa_ref[...]), wa_ref[...], _NN, preferred_element_type=F32)
        yb = lax.dot_general(_bf(ob_ref[...]), wb_ref[...], _NN, preferred_element_type=F32)
        sa = jax.nn.sigmoid(g_ref[:, 0:D_MODEL].astype(F32))
        sb = jax.nn.sigmoid(g_ref[:, D_MODEL:2 * D_MODEL].astype(F32))
        o_ref[...] = (sa * ya + sb * yb).astype(BF16)
        ya_ref[...] = ya.astype(BF16)
        yb_ref[...] = yb.astype(BF16)

    blk = pl.BlockSpec((tm, D_MODEL), lambda i: (i, 0))
    narrow = pl.BlockSpec((tm, A_WIDTH), lambda i: (i, 0))
    wide = pl.BlockSpec((tm, 2 * D_MODEL), lambda i: (i, 0))
    wspec = pl.BlockSpec((A_WIDTH, D_MODEL), lambda i: (0, 0))
    out = jax.ShapeDtypeStruct((t, D_MODEL), BF16)
    return pl.pallas_call(
        body, name=name, grid=(t // tm,),
        in_specs=[narrow, narrow, wspec, wspec, wide] + ([pl.BlockSpec(memory_space=pl.ANY)] if has_dep else []),
        out_specs=[blk, blk, blk], out_shape=[out, out, out], compiler_params=_params(("parallel",)),
    )(*((oa, ob, wa, wb, gates) + ((dep,) if has_dep else ())))


def _branch_bwd_x(name, dya, dyb, wa, wb):
    t = dya.shape[0]
    tm = _pick(t, (1088, 768, 512, 256, 128))

    def body(da_ref, db_ref, wa_ref, wb_ref, oa_ref, ob_ref):
        oa_ref[...] = lax.dot_general(da_ref[...], wa_ref[...], _NT, preferred_element_type=F32)
        ob_ref[...] = lax.dot_general(db_ref[...], wb_ref[...], _NT, preferred_element_type=F32).astype(BF16)

    blk = pl.BlockSpec((tm, D_MODEL), lambda i: (i, 0))
    narrow = pl.BlockSpec((tm, A_WIDTH), lambda i: (i, 0))
    wspec = pl.BlockSpec((A_WIDTH, D_MODEL), lambda i: (0, 0), pipeline_mode=pl.Buffered(1))
    return pl.pallas_call(
        body, name=name, grid=(t // tm,), in_specs=[blk, blk, wspec, wspec], out_specs=[narrow, narrow],
        out_shape=[jax.ShapeDtypeStruct((t, A_WIDTH), F32), jax.ShapeDtypeStruct((t, B_WIDTH), BF16)],
        compiler_params=_params(("parallel",)),
    )(dya, dyb, wa, wb)


def _branch_bwd_w(name, oa, ob, dya, dyb):
    t = oa.shape[0]
    tn = 512

    def body(oa_ref, ob_ref, da_ref, db_ref, wa_ref, wb_ref):
        wa_ref[...] = lax.dot_general(_bf(oa_ref[...]), da_ref[...], _TN, preferred_element_type=F32).astype(BF16)
        wb_ref[...] = lax.dot_general(_bf(ob_ref[...]), db_ref[...], _TN, preferred_element_type=F32).astype(BF16)

    whole = pl.BlockSpec((t, A_WIDTH), lambda j: (0, 0), pipeline_mode=pl.Buffered(1))
    cols = pl.BlockSpec((t, tn), lambda j: (0, j))
    out_spec = pl.BlockSpec((A_WIDTH, tn), lambda j: (0, j))
    out = jax.ShapeDtypeStruct((A_WIDTH, D_MODEL), BF16)
    return pl.pallas_call(
        body, name=name, grid=(D_MODEL // tn,), in_specs=[whole, whole, cols, cols], out_specs=[out_spec, out_spec],
        out_shape=[out, out], compiler_params=_params(("parallel",)),
    )(oa, ob, dya, dyb)


def _mix_out_gate_bwd(name, dh, wo, gates, ya, yb):
    t = gates.shape[0]
    tm = _pick(t, (544, 384, 256, 128))

    def body(dh_ref, w_ref, g_ref, ya_ref, yb_ref, dya_ref, dyb_ref, dg_ref):
        dm = lax.dot_general(_bf(dh_ref[...]), w_ref[...], _NT, preferred_element_type=F32)
        sa = jax.nn.sigmoid(g_ref[:, 0:D_MODEL].astype(F32))
        sb = jax.nn.sigmoid(g_ref[:, D_MODEL:2 * D_MODEL].astype(F32))
        dya_ref[...] = (dm * sa).astype(BF16)
        dyb_ref[...] = (dm * sb).astype(BF16)
        dg_ref[:, 0:D_MODEL] = (dm * ya_ref[...].astype(F32) * (sa * (1.0 - sa))).astype(BF16)
        dg_ref[:, D_MODEL:2 * D_MODEL] = (dm * yb_ref[...].astype(F32) * (sb * (1.0 - sb))).astype(BF16)

    blk = pl.BlockSpec((tm, D_MODEL), lambda i: (i, 0))
    wide = pl.BlockSpec((tm, 2 * D_MODEL), lambda i: (i, 0))
    out = jax.ShapeDtypeStruct((t, D_MODEL), BF16)
    return pl.pallas_call(
        body, name=name, grid=(t // tm,),
        in_specs=[blk, pl.BlockSpec((D_MODEL, D_MODEL), lambda i: (0, 0)), wide, blk, blk], out_specs=[blk, blk, wide],
        out_shape=[out, out, jax.ShapeDtypeStruct((t, 2 * D_MODEL), BF16)], compiler_params=_params(("parallel",)),
    )(dh, wo, gates, ya, yb)


def _ffn_out_loss(name, a, w, res, gf, tgt, alpha):
    t, k = a.shape
    b, s, _ = tgt.shape
    l = t // b
    tm = _pick(t, (544, 384, 256, 128))
    per = l // tm
    assert per * tm == l and tm > PREFIX and l - PREFIX == s

    def body(a_ref, w_ref, r_ref, g_ref, t_ref, dh_ref, dhb_ref, loss_ref, dg_ref, buf, sem):
        i = pl.program_id(0)
        bi, r = i // per, i % per

        def first_rows():
            return pltpu.make_async_copy(t_ref.at[bi, pl.ds(0, tm - PREFIX)], buf.at[pl.ds(PREFIX, tm - PREFIX)], sem)

        def later_rows():
            return pltpu.make_async_copy(t_ref.at[bi, pl.ds(pl.multiple_of(r * tm - PREFIX, 8), tm)], buf, sem)

        @pl.when(r == 0)
        def _():
            buf[0:PREFIX, :] = jnp.zeros((PREFIX, D_MODEL), F32)
            first_rows().start()

        if per > 1:
            @pl.when(r > 0)
            def _():
                later_rows().start()

        acc = lax.dot_general(a_ref[...], w_ref[...], _NN, preferred_element_type=F32)

        @pl.when(r == 0)
        def _():
            first_rows().wait()

        if per > 1:
            @pl.when(r > 0)
            def _():
                later_rows().wait()

        hv = acc * alpha + r_ref[...]
        row = lax.broadcasted_iota(jnp.int32, (tm, 1), 0)
        real = ((r > 0) | (row >= PREFIX)).astype(F32)
        g = g_ref[...]
        rn = lax.rsqrt(jnp.mean(hv * hv, axis=-1, keepdims=True) + EPS)
        xn = hv * rn
        err = (xn * g - buf[...]) * real
        lpart = 0.5 * jnp.sum(jnp.mean(err * err, axis=-1, keepdims=True), axis=0, keepdims=True)
        dy = err * (1.0 / D_MODEL)
        tv = dy * g
        dot = jnp.mean(tv * hv, axis=-1, keepdims=True)
        dh = rn * tv - hv * (rn * rn * rn * dot)
        dh_ref[...] = dh
        dhb_ref[...] = dh.astype(BF16)
        gpart = jnp.sum(dy * xn, axis=0, keepdims=True)

        @pl.when(i == 0)
        def _():
            loss_ref[...] = jnp.zeros_like(loss_ref)
            dg_ref[...] = jnp.zeros_like(dg_ref)

        loss_ref[...] += jnp.broadcast_to(lpart, loss_ref.shape)
        dg_ref[...] += gpart

    row_spec = pl.BlockSpec((tm, D_MODEL), lambda i: (i, 0))
    vec = pl.BlockSpec((1, D_MODEL), lambda i: (0, 0))
    return pl.pallas_call(
        body, name=name, grid=(t // tm,),
        in_specs=[pl.BlockSpec((tm, k), lambda i: (i, 0)),
                  pl.BlockSpec((k, D_MODEL), lambda i: (0, 0), pipeline_mode=pl.Buffered(1)), row_spec, vec,
                  pl.BlockSpec(memory_space=pl.ANY)],
        out_specs=[row_spec, row_spec, pl.BlockSpec((8, 128), lambda i: (0, 0)), vec],
        out_shape=[jax.ShapeDtypeStruct((t, D_MODEL), F32), jax.ShapeDtypeStruct((t, D_MODEL), BF16),
                   jax.ShapeDtypeStruct((8, 128), F32), jax.ShapeDtypeStruct((1, D_MODEL), F32)],
        scratch_shapes=[pltpu.VMEM((tm, D_MODEL), F32), pltpu.SemaphoreType.DMA(())],
        compiler_params=_params(("arbitrary",)),
    )(a, w, res, gf, tgt)


def _fgate_fwd(name, f3, bf_row):
    b, l, _ = f3.shape
    nb = l // BLOCK

    def body(f_ref, b_ref, cc_ref, cr_ref):
        r_i = lax.broadcasted_iota(jnp.int32, (BLOCK, BLOCK), 0)
        c_i = lax.broadcasted_iota(jnp.int32, (BLOCK, BLOCK), 1)
        tri = (r_i >= c_i).astype(F32)
        carry = jnp.zeros((1, 128), F32)
        for blk in range(nb):
            rows = slice(blk * BLOCK, (blk + 1) * BLOCK)
            z = f_ref[rows, :] + b_ref[...]
            lf = jnp.minimum(z, 0.0) - jnp.log(1.0 + jnp.exp(-jnp.abs(z)))
            cb = jnp.dot(tri, lf, preferred_element_type=F32, precision=lax.Precision.HIGHEST) + carry
            carry = cb[BLOCK - 1:BLOCK, :]
            cbt = cb.T
            for hh in range(B_HEADS):
                cc_ref[hh, rows, :] = jnp.sum(jnp.where(c_i == hh, cb, 0.0), axis=1, keepdims=True)
                cr_ref[hh, :, rows] = cbt[hh:hh + 1, :]

    return pl.pallas_call(
        body, name=name, grid=(b,),
        in_specs=[pl.BlockSpec((None, l, 128), lambda bi: (bi, 0, 0)),
                  pl.BlockSpec((1, 128), lambda bi: (0, 0))],
        out_specs=[pl.BlockSpec((None, B_HEADS, l, 1), lambda bi: (bi, 0, 0, 0)),
                   pl.BlockSpec((None, B_HEADS, 1, l), lambda bi: (bi, 0, 0, 0))],
        out_shape=[jax.ShapeDtypeStruct((b, B_HEADS, l, 1), F32), jax.ShapeDtypeStruct((b, B_HEADS, 1, l), F32)],
        compiler_params=_params(("parallel",)),
    )(f3, bf_row)


def _fgate_bwd(name, f3, bf_row, dcq, dck):
    b, l, _ = f3.shape
    nb = l // BLOCK

    def body(f_ref, b_ref, dcq_ref, dck_ref, df_ref, db_ref):
        r_i = lax.broadcasted_iota(jnp.int32, (BLOCK, BLOCK), 0)
        c_i = lax.broadcasted_iota(jnp.int32, (BLOCK, BLOCK), 1)
        tri = (r_i <= c_i).astype(F32)
        carry = jnp.zeros((1, 128), F32)
        total = jnp.zeros((1, 128), F32)
        for blk in range(nb - 1, -1, -1):
            rows = slice(blk * BLOCK, (blk + 1) * BLOCK)
            krows = jnp.concatenate([dck_ref[hh, :, rows] for hh in range(B_HEADS)]
                                    + [jnp.zeros((BLOCK - B_HEADS, BLOCK), F32)], axis=0)
            dcb = krows.T
            for hh in range(B_HEADS):
                dcb = dcb + jnp.where(c_i == hh, dcq_ref[hh, rows, :], 0.0)
            rc = jnp.dot(tri, dcb, preferred_element_type=F32, precision=lax.Precision.HIGHEST) + carry
            carry = rc[0:1, :]
            z = f_ref[rows, :] + b_ref[...]
            df = rc * (1.0 / (1.0 + jnp.exp(z)))
            df_ref[rows, :] = df.astype(BF16)
            total = total + jnp.sum(df, axis=0, keepdims=True)

        @pl.when(pl.program_id(0) == 0)
        def _():
            db_ref[...] = total

        @pl.when(pl.program_id(0) > 0)
        def _():
            db_ref[...] += total

    return pl.pallas_call(
        body, name=name, grid=(b,),
        in_specs=[pl.BlockSpec((None, l, 128), lambda bi: (bi, 0, 0)),
                  pl.BlockSpec((1, 128), lambda bi: (0, 0)),
                  pl.BlockSpec((None, B_HEADS, l, 1), lambda bi: (bi, 0, 0, 0)),
                  pl.BlockSpec((None, B_HEADS, 1, l), lambda bi: (bi, 0, 0, 0))],
        out_specs=[pl.BlockSpec((None, l, 128), lambda bi: (bi, 0, 0)), pl.BlockSpec((1, 128), lambda bi: (0, 0))],
        out_shape=[jax.ShapeDtypeStruct((b, l, 128), BF16), jax.ShapeDtypeStruct((1, 128), F32)],
        compiler_params=_params(("arbitrary",)),
    )(f3, bf_row, dcq, dck)


A_Q_BLK = B_SEG // A_WIDTH
A_K_BLK = (B_SEG + A_WIDTH) // 128
A_V_BLK = A_K_BLK + 1
A_SEG_BLK = B_SEG // A_SEG
STACK = A_HEADS * BLOCK


def _lane_lo():
    return lax.broadcasted_iota(jnp.int32, (1, 128), 1) < HEAD_DIM


def _stack_heads(x, masked):
    lo = _lane_lo()
    blks = [x[:, 128 * j:128 * (j + 1)] for j in range(4)]
    if not masked:
        return jnp.concatenate(blks + blks, axis=0)
    zero = jnp.zeros_like(blks[0])
    return jnp.concatenate([jnp.where(lo, bk, zero) for bk in blks] + [jnp.where(lo, zero, bk) for bk in blks], axis=0)


def _unstack_heads(y):
    lo = _lane_lo()
    return jnp.concatenate([jnp.where(lo, y[128 * j:128 * (j + 1)], y[128 * (4 + j):128 * (5 + j)]) for j in range(4)], axis=1)


def _swa_bias(slopes):
    r_i = jnp.arange(STACK)[:, None]
    c_i = jnp.arange(3 * BLOCK)[None, :]
    seg = c_i >> 7
    out = []
    for n in range(3):
        qpos = n * BLOCK + (r_i & (BLOCK - 1))
        kpos = jnp.where(seg == 0, c_i, (n - 2) * BLOCK + c_i)
        dist = qpos - kpos
        band = (seg != 0) & (dist < BLOCK) & (kpos >= PREFIX)
        meta = (seg == 0) & (c_i >= N_PAD)
        out.append(jnp.where((dist >= 0) & (band | meta), -slopes * dist.astype(F32), NEG))
    return jnp.stack(out, axis=0)


def _swa_scores(q, kcat, n, slope, bias):
    s = lax.dot_general(q, kcat, _NT, preferred_element_type=F32) + bias
    further = slope * (-BLOCK * jnp.maximum(n - 2, 0)).astype(F32)
    return jnp.concatenate([s[:, 0:BLOCK] + further, s[:, BLOCK:]], axis=1)


def _swa_specs():
    def kv(col_blk):
        return [pl.BlockSpec((None, BLOCK, 128), lambda b, n: (b, 0, col_blk)),
                pl.BlockSpec((None, BLOCK, 128), lambda b, n: (b, jnp.maximum(n - 1, 0), col_blk)),
                pl.BlockSpec((None, BLOCK, 128), lambda b, n: (b, n, col_blk))]

    q_spec = pl.BlockSpec((None, BLOCK, A_WIDTH), lambda b, n: (b, n, A_Q_BLK))
    o_spec = pl.BlockSpec((None, BLOCK, A_WIDTH), lambda b, n: (b, n, 0))
    col = pl.BlockSpec((STACK, 1), lambda b, n: (0, 0))
    bias = pl.BlockSpec((None, STACK, 3 * BLOCK), lambda b, n: (jnp.minimum(n, 2), 0, 0))
    lse_spec = pl.BlockSpec((None, A_HEADS, BLOCK, 1), lambda b, n: (b, 0, n, 0))
    return q_spec, kv(A_K_BLK), kv(A_V_BLK), o_spec, [col, col, bias], lse_spec


def _swa_fwd(name, qkv, slopes, sinks, bias):
    b, l, _ = qkv.shape
    nb = l // BLOCK

    def body(q_ref, k0_ref, kp_ref, kc_ref, v0_ref, vp_ref, vc_ref, sl_ref, sk_ref, bias_ref, o_ref, lse_ref):
        n = pl.program_id(1)
        qs = _stack_heads(q_ref[...], True) * SCALE
        kcat = jnp.concatenate([k0_ref[...], kp_ref[...], kc_ref[...]], axis=0)
        vcat = jnp.concatenate([v0_ref[...], vp_ref[...], vc_ref[...]], axis=0)
        s = _swa_scores(qs, kcat, n, sl_ref[...], bias_ref[...])
        sink = sk_ref[...]
        m = jnp.maximum(jnp.max(s, axis=-1, keepdims=True), sink)
        p = jnp.exp(s - m)
        den = jnp.sum(p, axis=-1, keepdims=True) + jnp.exp(sink - m)
        o = lax.dot_general(p.astype(BF16), vcat, _NN, preferred_element_type=F32) / den
        o_ref[...] = _unstack_heads(o)
        lse_ref[...] = (m + jnp.log(den)).reshape(A_HEADS, BLOCK, 1)

    q_spec, k_specs, v_specs, o_spec, consts, lse_spec = _swa_specs()
    return pl.pallas_call(
        body, name=name, grid=(b, nb),
        in_specs=[q_spec] + k_specs + v_specs + consts, out_specs=[o_spec, lse_spec],
        out_shape=[jax.ShapeDtypeStruct((b, l, A_WIDTH), F32), jax.ShapeDtypeStruct((b, A_HEADS, l, 1), F32)],
        compiler_params=_params(("parallel", "parallel")),
    )(qkv, qkv, qkv, qkv, qkv, qkv, qkv, slopes, sinks, bias)


def _swa_bwd(name, qkv, o, lse, do, slopes, sinks, bias, dqkv):
    b, l, _ = qkv.shape
    nb = l // BLOCK

    def body(q_ref, k0_ref, kp_ref, kc_ref, v0_ref, vp_ref, vc_ref, o_ref, lse_ref, do_ref, sl_ref, sk_ref, bias_ref, _,
             dx_ref, ds_ref, dk_acc, dv_acc):
        bi = pl.program_id(0)
        n = pl.program_id(1)
        qs = _stack_heads(q_ref[...], True) * SCALE
        dos32 = _stack_heads(do_ref[...], True)
        dos = dos32.astype(BF16)
        os_ = _stack_heads(o_ref[...], False)
        lsev = lse_ref[...].reshape(STACK, 1)
        kcat = jnp.concatenate([k0_ref[...], kp_ref[...], kc_ref[...]], axis=0)
        vcat = jnp.concatenate([v0_ref[...], vp_ref[...], vc_ref[...]], axis=0)
        s = _swa_scores(qs, kcat, n, sl_ref[...], bias_ref[...])
        p = jnp.exp(s - lsev)
        dsum = jnp.sum(dos32 * os_, axis=-1, keepdims=True)
        dp = lax.dot_general(dos, vcat, _NT, preferred_element_type=F32)
        dsc = (p * (dp - dsum)).astype(BF16)
        dq = lax.dot_general(dsc, kcat, _NN, preferred_element_type=F32) * SCALE
        row0 = pl.multiple_of(n * BLOCK, BLOCK)
        dx_ref[pl.ds(row0, BLOCK), 0:A_WIDTH] = _unstack_heads(dq).astype(BF16)
        dkc = lax.dot_general(dsc, qs, _TN, preferred_element_type=F32)
        dvc = lax.dot_general(p.astype(BF16), dos, _TN, preferred_element_type=F32)

        @pl.when(n == 0)
        def _():
            dk_acc[...] = jnp.zeros_like(dk_acc)
            dv_acc[...] = jnp.zeros_like(dv_acc)

        starts = (0, pl.multiple_of(jnp.maximum(n - 1, 0) * BLOCK, BLOCK), row0)
        for t, st in enumerate(starts):
            dk_acc[pl.ds(st, BLOCK), :] += dkc[t * BLOCK:(t + 1) * BLOCK, :]
            dv_acc[pl.ds(st, BLOCK), :] += dvc[t * BLOCK:(t + 1) * BLOCK, :]

        @pl.when(n == nb - 1)
        def _():
            dx_ref[:, A_WIDTH:A_WIDTH + 128] = dk_acc[...].astype(BF16)
            dx_ref[:, A_WIDTH + 128:A_SEG] = dv_acc[...].astype(BF16)

        dsink = -(jnp.exp(sk_ref[...] - lsev) * dsum)
        r8 = lax.broadcasted_iota(jnp.int32, (8, 128), 0)
        acc = jnp.zeros((8, 128), F32)
        for hh in range(A_HEADS):
            acc = acc + jnp.where(r8 == hh, jnp.sum(dsink[hh * BLOCK:(hh + 1) * BLOCK, :]), 0.0)

        @pl.when((bi == 0) & (n == 0))
        def _():
            ds_ref[...] = jnp.zeros_like(ds_ref)

        ds_ref[...] += acc

    q_spec, k_specs, v_specs, o_spec, consts, lse_spec = _swa_specs()
    return pl.pallas_call(
        body, name=name, grid=(b, nb),
        in_specs=[q_spec] + k_specs + v_specs + [o_spec, lse_spec, o_spec] + consts + [pl.BlockSpec(memory_space=pl.ANY)],
        out_specs=[pl.BlockSpec((None, l, A_SEG), lambda bb, n: (bb, 0, A_SEG_BLK)),
                   pl.BlockSpec((8, 128), lambda bb, n: (0, 0))],
        out_shape=[jax.ShapeDtypeStruct(dqkv.shape, BF16), jax.ShapeDtypeStruct((8, 128), F32)],
        scratch_shapes=[pltpu.VMEM((l, 128), F32), pltpu.VMEM((l, 128), F32)],
        input_output_aliases={13: 0},
        compiler_params=_params(("arbitrary", "arbitrary")),
    )(qkv, qkv, qkv, qkv, qkv, qkv, qkv, o, lse, do, slopes, sinks, bias, dqkv)


def _fox_mask(qk, ck, i):
    kh = qk.shape[1]
    qpos = i * BLOCK + lax.broadcasted_iota(jnp.int32, (BLOCK, kh), 0)
    kpos = lax.broadcasted_iota(jnp.int32, (BLOCK, kh), 1)
    return jnp.where((kpos <= qpos) & (kpos >= N_PAD), qk - ck, NEG)


def _pick_head(x, hh):
    lo = _lane_lo()
    return jnp.where(lo if hh == 0 else jnp.logical_not(lo), x, jnp.zeros_like(x))


def _both_heads(x):
    return jnp.concatenate([_pick_head(x, 0), _pick_head(x, 1)], axis=0)


def _fox_specs(l):
    pair = pl.BlockSpec((None, l, PAIR_W), lambda bi, hp: (bi, 0, hp))
    half = pl.BlockSpec((None, l, 128), lambda bi, hp: (bi, 0, hp))
    colv = pl.BlockSpec((None, 2, l, 1), lambda bi, hp: (bi, hp, 0, 0))
    rowv = pl.BlockSpec((None, 2, 1, l), lambda bi, hp: (bi, hp, 0, 0))
    return pair, half, colv, rowv


def _fox_fwd(name, qkv, c_col, c_row):
    b, l, _ = qkv.shape
    nb = l // BLOCK

    def body(x_ref, cc_ref, cr_ref, o_ref, lse_ref):
        for i in range(nb):
            rows = slice(i * BLOCK, (i + 1) * BLOCK)
            kh = (i + 1) * BLOCK
            qblk = x_ref[rows, 0:128]
            kv = x_ref[0:kh, 128:256]
            vv = x_ref[0:kh, 256:384]
            qk = lax.dot_general(_both_heads(qblk) * SCALE, kv, _NT, preferred_element_type=F32)
            ps, dens = [], []
            for hh in range(2):
                s = _fox_mask(qk[hh * BLOCK:(hh + 1) * BLOCK], cr_ref[hh, :, 0:kh], i)
                m = jnp.max(s, axis=-1, keepdims=True)
                p = jnp.exp(s - m)
                den = jnp.sum(p, axis=-1, keepdims=True)
                ps.append(p.astype(BF16))
                dens.append(den)
                lse_ref[hh, rows, :] = (m + jnp.log(den)) + cc_ref[hh, rows, :]
            pv = lax.dot_general(jnp.concatenate(ps, axis=0), vv, _NN, preferred_element_type=F32)
            o_ref[rows, :] = jnp.where(_lane_lo(), pv[0:BLOCK] / dens[0], pv[BLOCK:2 * BLOCK] / dens[1]).astype(BF16)

    pair, half, colv, rowv = _fox_specs(l)
    return pl.pallas_call(
        body, name=name, grid=(b, 4), in_specs=[pair, colv, rowv], out_specs=[half, colv],
        out_shape=[jax.ShapeDtypeStruct((b, l, B_WIDTH), BF16), jax.ShapeDtypeStruct((b, B_HEADS, l, 1), F32)],
        compiler_params=_params(("parallel", "parallel")),
    )(qkv, c_col, c_row)


def _fox_bwd(name, qkv, c_col, c_row, o, lse, do):
    b, l, _ = qkv.shape
    nb = l // BLOCK

    def body(x_ref, cc_ref, cr_ref, o_ref, lse_ref, do_ref, dx_ref, dcq_ref, dck_ref, dk_acc, dv_acc):
        dk_acc[...] = jnp.zeros_like(dk_acc)
        dv_acc[...] = jnp.zeros_like(dv_acc)
        dck_ref[...] = jnp.zeros_like(dck_ref)
        for i in range(nb):
            rows = slice(i * BLOCK, (i + 1) * BLOCK)
            kh = (i + 1) * BLOCK
            qblk = x_ref[rows, 0:128]
            kv = x_ref[0:kh, 128:256]
            vv = x_ref[0:kh, 256:384]
            doblk = do_ref[rows, :]
            ov = o_ref[rows, :].astype(F32)
            q2 = _both_heads(qblk) * SCALE
            do2 = _both_heads(doblk)
            qk = lax.dot_general(q2, kv, _NT, preferred_element_type=F32)
            dp = lax.dot_general(do2, vv, _NT, preferred_element_type=F32)
            ps, dss = [], []
            for hh in range(2):
                half = slice(hh * BLOCK, (hh + 1) * BLOCK)
                s = _fox_mask(qk[half], cr_ref[hh, :, 0:kh], i)
                p = jnp.exp(s - (lse_ref[hh, rows, :] - cc_ref[hh, rows, :]))
                dsum = jnp.sum(do2[half].astype(F32) * ov, axis=-1, keepdims=True)
                ds = p * (dp[half] - dsum)
                ps.append(p.astype(BF16))
                dss.append(ds.astype(BF16))
                dcq_ref[hh, rows, :] = jnp.sum(ds, axis=-1, keepdims=True)
                dck_ref[hh, :, 0:kh] -= jnp.sum(ds, axis=0, keepdims=True)
            p2 = jnp.concatenate(ps, axis=0)
            ds2 = jnp.concatenate(dss, axis=0)
            dq = lax.dot_general(ds2, kv, _NN, preferred_element_type=F32) * SCALE
            dk_acc[0:kh, :] += lax.dot_general(ds2, q2, _TN, preferred_element_type=F32)
            dv_acc[0:kh, :] += lax.dot_general(p2, do2, _TN, preferred_element_type=F32)
            dx_ref[rows, 0:128] = jnp.where(_lane_lo(), dq[0:BLOCK], dq[BLOCK:2 * BLOCK]).astype(BF16)
        dx_ref[:, 128:256] = dk_acc[...].astype(BF16)
        dx_ref[:, 256:384] = dv_acc[...].astype(BF16)

    pair, half, colv, rowv = _fox_specs(l)
    return pl.pallas_call(
        body, name=name, grid=(b, 4), in_specs=[pair, colv, rowv, half, colv, half],
        out_specs=[pair, colv, rowv],
        out_shape=[jax.ShapeDtypeStruct(qkv.shape, BF16), jax.ShapeDtypeStruct((b, B_HEADS, l, 1), F32),
                   jax.ShapeDtypeStruct((b, B_HEADS, 1, l), F32)],
        scratch_shapes=[pltpu.VMEM((l, 128), F32), pltpu.VMEM((l, 128), F32)],
        compiler_params=_params(("parallel", "parallel")),
    )(qkv, c_col, c_row, o, lse, do)


_FLIPS = ((0, 0, 1), (0, 1, 0), (0, 1, 1), (1, 0, 0), (1, 0, 1), (1, 1, 0), (1, 1, 1))


def _peer_table():
    x, y, c = lax.axis_index("x"), lax.axis_index("y"), lax.axis_index("c")
    me = 4 * x + 2 * y + c
    peers = []
    for fx, fy, fc in _FLIPS:
        px = 1 - x if fx else x
        py = 1 - y if fy else y
        pc = 1 - c if fc else c
        peers.append(((px, py, pc), 4 * px + 2 * py + pc))
    return me, peers


_HBM = pl.BlockSpec(memory_space=pltpu.HBM)
_SEM = pl.BlockSpec(memory_space=pltpu.SEMAPHORE)
_ANY = pl.BlockSpec(memory_space=pl.ANY)
_EFFECT = pltpu.SideEffectType.DATAFLOW_SIDE_EFFECTING


def _split_copy(srcs_are_pieces, src_refs, land_refs, send_sem, recv_sem, a, kk, me, peers, arriving):
    dev, lin = peers[kk]
    npeer = len(_FLIPS)
    src = src_refs[a] if srcs_are_pieces[a] else src_refs[a].at[lin]
    dst = land_refs[a].at[lin] if arriving else land_refs[a].at[me]
    return pltpu.make_async_remote_copy(src_ref=src, dst_ref=dst, send_sem=send_sem.at[a * npeer + kk],
                                        recv_sem=recv_sem.at[a * npeer + kk], device_id=dev, device_id_type=MESH_ID)


def _xchg_start(name, gather, scatter, after=None):
    me_out = 4 * lax.axis_index("x") + 2 * lax.axis_index("y") + lax.axis_index("c")
    srcs = list(gather) + list(scatter)
    is_piece = [True] * len(gather) + [False] * len(scatter)
    lands = []
    for a, piece in zip(srcs, is_piece):
        if piece:
            lands.append(lax.dynamic_update_slice(lax.empty((N_DEV,) + tuple(a.shape), a.dtype), a[None],
                                                  (me_out,) + (0,) * a.ndim))
        else:
            lands.append(lax.empty(tuple(a.shape), a.dtype))
    n = len(srcs)
    nsem = n * len(_FLIPS)
    has_after = after is not None

    def body(*refs):
        src_refs = refs[:n]
        land_refs = refs[n:2 * n]
        outs = refs[2 * n + (1 if has_after else 0):]
        send_sem, recv_sem = outs[0], outs[1]
        token = outs[-1]
        me, peers = _peer_table()
        for kk in range(len(_FLIPS)):
            for a in range(n):
                _split_copy(is_piece, src_refs, land_refs, send_sem, recv_sem, a, kk, me, peers, False).start()
        token[...] = jnp.zeros_like(token)

    out_shape = ([pltpu.SemaphoreType.DMA((nsem,)), pltpu.SemaphoreType.DMA((nsem,))]
                 + [pltpu.HBM(tuple(a.shape), a.dtype) for a in srcs] + [pltpu.HBM(tuple(a.shape), a.dtype) for a in lands]
                 + [jax.ShapeDtypeStruct((8, 128), F32)])
    args = [pltpu.with_memory_space_constraint(a, pltpu.HBM) for a in srcs + lands] + ([after] if has_after else [])
    res = pl.pallas_call(
        body, name=name, out_shape=out_shape,
        in_specs=[_HBM] * (2 * n) + ([_ANY] if has_after else []),
        out_specs=[_SEM, _SEM] + [_HBM] * (2 * n) + [pl.BlockSpec(memory_space=pltpu.VMEM)],
        input_output_aliases={i: 2 + i for i in range(2 * n)},
        compiler_params=pltpu.CompilerParams(has_side_effects=_EFFECT),
    )(*args)
    state = (res[0], res[1], list(res[2:2 + n]), list(res[2 + n:2 + 2 * n]), is_piece)
    return state, res[-1]


def _xchg_wait(name, state, after):
    send_sem, recv_sem, srcs, lands, is_piece = state
    n = len(srcs)

    def body(*refs):
        src_refs = refs[:n]
        land_refs = refs[n:2 * n]
        s_sem, r_sem = refs[2 * n], refs[2 * n + 1]
        me, peers = _peer_table()
        for kk in range(len(_FLIPS)):
            for a in range(n):
                cp = _split_copy(is_piece, src_refs, land_refs, s_sem, r_sem, a, kk, me, peers, True)
                cp.wait_send()
                cp.wait_recv()

    out_shape = [pltpu.HBM(tuple(a.shape), a.dtype) for a in srcs] + [pltpu.HBM(tuple(a.shape), a.dtype) for a in lands]
    res = pl.pallas_call(
        body, name=name, out_shape=out_shape,
        in_specs=[_HBM] * (2 * n) + [_SEM, _SEM, _ANY], out_specs=[_HBM] * (2 * n),
        input_output_aliases={i: i for i in range(2 * n)},
        compiler_params=pltpu.CompilerParams(has_side_effects=_EFFECT),
    )(*srcs, *lands, send_sem, recv_sem, after)
    return list(res[n:]), list(res[:n])


_SIB = (0, 0, 1)
_ICI = ((0, 1, 0), (1, 0, 0), (1, 1, 0))


def _flip(fl):
    x, y, c = lax.axis_index("x"), lax.axis_index("y"), lax.axis_index("c")
    px = 1 - x if fl[0] else x
    py = 1 - y if fl[1] else y
    pc = 1 - c if fl[2] else c
    return (px, py, pc), 4 * px + 2 * py + pc


def _gather2_start(name, pieces, after=None):
    me_out = 4 * lax.axis_index("x") + 2 * lax.axis_index("y") + lax.axis_index("c")
    pieces = list(pieces)
    n = len(pieces)
    lands = [lax.dynamic_update_slice(lax.empty((N_DEV,) + tuple(a.shape), a.dtype), a[None],
                                      (me_out,) + (0,) * a.ndim) for a in pieces]
    first = (_SIB,) + _ICI
    has_after = after is not None

    def body(*refs):
        src_refs, land_refs = refs[:n], refs[n:2 * n]
        outs = refs[2 * n + (1 if has_after else 0):]
        send_sem, recv_sem, token = outs[0], outs[1], outs[-1]
        _, me = _flip((0, 0, 0))
        for kk, fl in enumerate(first):
            dev, _ = _flip(fl)
            for a in range(n):
                pltpu.make_async_remote_copy(src_ref=src_refs[a], dst_ref=land_refs[a].at[me],
                                             send_sem=send_sem.at[a * 4 + kk], recv_sem=recv_sem.at[a * 4 + kk],
                                             device_id=dev, device_id_type=MESH_ID).start()
        token[...] = jnp.zeros_like(token)

    hbm = [pltpu.HBM(tuple(a.shape), a.dtype) for a in pieces + lands]
    res = pl.pallas_call(
        body, name=name,
        out_shape=[pltpu.SemaphoreType.DMA((4 * n,)), pltpu.SemaphoreType.DMA((4 * n,))] + hbm
        + [jax.ShapeDtypeStruct((8, 128), F32)],
        in_specs=[_HBM] * (2 * n) + ([_ANY] if has_after else []),
        out_specs=[_SEM, _SEM] + [_HBM] * (2 * n) + [pl.BlockSpec(memory_space=pltpu.VMEM)],
        input_output_aliases={i: 2 + i for i in range(2 * n)},
        compiler_params=pltpu.CompilerParams(has_side_effects=_EFFECT),
    )(*([pltpu.with_memory_space_constraint(a, pltpu.HBM) for a in pieces + lands] + ([after] if has_after else [])))
    return (res[0], res[1], list(res[2:2 + n]), list(res[2 + n:2 + 2 * n])), res[-1]


def _gather2_forward(name, state, after):
    send_a, recv_a, pieces, lands = state
    n = len(pieces)
    first = (_SIB,) + _ICI

    def body(*refs):
        src_refs, land_refs = refs[:n], refs[n:2 * n]
        s_a, r_a = refs[2 * n], refs[2 * n + 1]
        outs = refs[2 * n + 3:]
        send_b, recv_b, token = outs[0], outs[1], outs[-1]
        for kk, fl in enumerate(first):
            dev, lin = _flip(fl)
            for a in range(n):
                cp = pltpu.make_async_remote_copy(src_ref=src_refs[a], dst_ref=land_refs[a].at[lin],
                                                  send_sem=s_a.at[a * 4 + kk], recv_sem=r_a.at[a * 4 + kk],
                                                  device_id=dev, device_id_type=MESH_ID)
                cp.wait_send()
                cp.wait_recv()
        sib, _ = _flip(_SIB)
        for j, fl in enumerate(_ICI):
            _, lin = _flip(fl)
            for a in range(n):
                pltpu.make_async_remote_copy(src_ref=land_refs[a].at[lin], dst_ref=land_refs[a].at[lin],
                                             send_sem=send_b.at[a * 3 + j], recv_sem=recv_b.at[a * 3 + j],
                                             device_id=sib, device_id_type=MESH_ID).start()
        token[...] = jnp.zeros_like(token)

    hbm = [pltpu.HBM(tuple(a.shape), a.dtype) for a in pieces + lands]
    res = pl.pallas_call(
        body, name=name,
        out_shape=[pltpu.SemaphoreType.DMA((3 * n,)), pltpu.SemaphoreType.DMA((3 * n,))] + hbm
        + [jax.ShapeDtypeStruct((8, 128), F32)],
        in_specs=[_HBM] * (2 * n) + [_SEM, _SEM, _ANY],
        out_specs=[_SEM, _SEM] + [_HBM] * (2 * n) + [pl.BlockSpec(memory_space=pltpu.VMEM)],
        input_output_aliases={i: 2 + i for i in range(2 * n)},
        compiler_params=pltpu.CompilerParams(has_side_effects=_EFFECT),
    )(*pieces, *lands, send_a, recv_a, after)
    return (res[0], res[1], list(res[2 + n:2 + 2 * n])), res[-1]


def _gather2_wait(name, state, after):
    send_b, recv_b, lands = state
    n = len(lands)

    def body(*refs):
        land_refs = refs[:n]
        s_b, r_b = refs[n], refs[n + 1]
        sib, _ = _flip(_SIB)
        for j, fl in enumerate(_ICI):
            _, sent = _flip(fl)
            _, arriving = _flip((fl[0], fl[1], 1))
            for a in range(n):
                cp = pltpu.make_async_remote_copy(src_ref=land_refs[a].at[sent], dst_ref=land_refs[a].at[arriving],
                                                  send_sem=s_b.at[a * 3 + j], recv_sem=r_b.at[a * 3 + j],
                                                  device_id=sib, device_id_type=MESH_ID)
                cp.wait_send()
                cp.wait_recv()

    res = pl.pallas_call(
        body, name=name, out_shape=[pltpu.HBM(tuple(a.shape), a.dtype) for a in lands],
        in_specs=[_HBM] * n + [_SEM, _SEM, _ANY], out_specs=[_HBM] * n,
        input_output_aliases={i: i for i in range(n)},
        compiler_params=pltpu.CompilerParams(has_side_effects=_EFFECT),
    )(*lands, send_b, recv_b, after)
    return list(res)


def _adam_math(w, g, m, v):
    m = ADAM_B1 * m + (1.0 - ADAM_B1) * g
    v = ADAM_B2 * v + (1.0 - ADAM_B2) * (g * g)
    m_hat = m / (1.0 - ADAM_B1 ** ADAM_STEP)
    v_hat = v / (1.0 - ADAM_B2 ** ADAM_STEP)
    delta = -ADAM_LR * (m_hat / (jnp.sqrt(v_hat) + ADAM_EPS) + ADAM_WD * w)
    return delta, m, v


def _adam(name, w, m, v, parts, transposed=False, dep=None, own=None):
    npart, _, cp = parts.shape
    has_dep = dep is not None
    has_own = own is not None
    if transposed:
        c, r = w.shape
        tr = _pick(r, (256, 128))
        blk = pl.BlockSpec((c, tr), lambda i, *_: (0, i))
    else:
        r, c = w.shape
        tr = _pick(r, (256, 176, 128, 64, 16, 8, 1))
        blk = pl.BlockSpec((tr, c), lambda i, *_: (i, 0))

    def body(*refs):
        refs = list(refs)
        me = refs.pop(0)[0] if has_own else None
        w_ref, m_ref, v_ref, p_ref = refs[:4]
        own_ref = refs[4] if has_own else None
        g_ref, d_ref, mo_ref, vo_ref = refs[-4:]
        g = None
        for pp in range(npart):
            part = p_ref[pp] if not has_own else jnp.where(me == pp, own_ref[...], p_ref[pp])
            g = part.astype(F32) if g is None else g + part.astype(F32)
        g = g.T[0:c, :] if transposed else g[:, 0:c]
        delta, mn, vn = _adam_math(w_ref[...], g, m_ref[...], v_ref[...])
        g_ref[...] = g
        d_ref[...] = delta
        mo_ref[...] = mn
        vo_ref[...] = vn

    in_specs = [blk, blk, blk, pl.BlockSpec((npart, tr, cp), lambda i, *_: (0, i, 0))]
    args = [w, m, v, parts]
    if has_own:
        in_specs.append(pl.BlockSpec((None, tr, cp), lambda i, me_ref: (me_ref[0], i, 0)))
        args.append(own[0])
    if has_dep:
        in_specs.append(pl.BlockSpec(memory_space=pl.ANY))
        args.append(dep)
    out = jax.ShapeDtypeStruct(w.shape, F32)
    grid_spec = pltpu.PrefetchScalarGridSpec(num_scalar_prefetch=1 if has_own else 0, grid=(r // tr,),
                                             in_specs=in_specs, out_specs=[blk, blk, blk, blk])
    return pl.pallas_call(
        body, name=name, grid_spec=grid_spec, out_shape=[out, out, out, out], compiler_params=_params(("parallel",)),
    )(*(([own[1]] if has_own else []) + args))


def _small_update(name, packs, wmv):
    nparam = len(wmv)

    def body(p_ref, *refs):
        ins, outs = refs[:3 * nparam], refs[3 * nparam:]
        tot = p_ref[0]
        for pp in range(1, N_DEV):
            tot = tot + p_ref[pp]
        outs[0][0:8, :] = tot[0:8, :]
        outs[0][8:24, :] = tot[8:24, :] + tot[24:40, :]
        grads = [tot[i:i + 1, :] for i in range(4)] + [tot[4:5, 0:B_HEADS], tot[4:5, B_HEADS:2 * B_HEADS]]
        for i, g in enumerate(grads):
            w_ref, m_ref, v_ref = ins[3 * i:3 * i + 3]
            delta, mn, vn = _adam_math(w_ref[...], g, m_ref[...], v_ref[...])
            for o_ref, val in zip(outs[1 + 4 * i:5 + 4 * i], (g, delta, mn, vn)):
                o_ref[...] = val

    flat = [a for trio in wmv for a in trio]
    out_shape = [jax.ShapeDtypeStruct((24, D_MODEL), F32)]
    for w, _, _ in wmv:
        out_shape += [jax.ShapeDtypeStruct(w.shape, F32)] * 4
    res = pl.pallas_call(body, name=name, out_shape=out_shape, compiler_params=_params())(packs, *flat)
    return res[0], [tuple(res[1 + 4 * i:5 + 4 * i]) for i in range(nparam)]


def _local_step(x, tgt, g1, gm, g2, gf, b_forget, sinks, weights, send):
    b, s, _ = x.shape
    l = s + PREFIX
    t = b * l
    (meta,) = weights("meta", x)

    h0, n1 = _embed_norm("embed_rms1_fwd", x, meta, g1)
    (w1i,) = weights("ffn1_in", n1)
    gu1, a1 = _ffn_in_fwd("ffn1_in_fwd", n1, w1i)
    (w1o,) = weights("ffn1_out", weights("ffn1_out:forward", a1))
    h1, um = _mm_res_norm("ffn1_out_fwd", a1, w1o, h0, gm, alpha=0.5)
    wi, wa, wb, wo = weights("mix", weights("mix:forward", um))
    qkv, gates, f2 = _proj_fwd("proj_fwd", um, wi)
    qkv3 = qkv.reshape(b, l, QKV_W)
    f3 = f2.reshape(b, l, 128)
    bf_row = jnp.pad(b_forget, ((0, 0), (0, 128 - B_HEADS)))
    c_col, c_row = _fgate_fwd("fgate_fwd", f3, bf_row)
    head_of_row = jnp.arange(STACK) // BLOCK
    slopes = jnp.exp2(-8.0 * (head_of_row + 1).astype(F32) / A_HEADS).reshape(STACK, 1)
    sink_rows = jnp.repeat(sinks.reshape(A_HEADS), BLOCK).reshape(STACK, 1)
    swa_bias = _swa_bias(slopes)
    oa3, lse_a = _swa_fwd("swa_fwd", qkv3, slopes, sink_rows, swa_bias)
    ob3, lse_b = _fox_fwd("fox_fwd", qkv3, c_col, c_row)
    oa = oa3.reshape(t, A_WIDTH)
    ob = ob3.reshape(t, B_WIDTH)
    mixed, ya, yb = _branch_gate_fwd("branch_gate_fwd", oa, ob, wa, wb, gates, dep=weights("ffn2:forward", ob))
    h2, n2 = _mm_res_norm("mix_out_fwd", mixed, wo, h1, g2)
    w2i, w2o = weights("ffn2", h2)
    gu2, a2 = _ffn_in_fwd("ffn2_in_fwd", n2, w2i)

    dh3, dh3_bf, loss_blk, dgf = _ffn_out_loss("ffn2_out_loss", a2, w2o, h2, gf, tgt, 0.5)

    def ffn_bwd(tag, dh, dh_bf, h_in, g_norm, n_in, gu, a, w_in_blk, w_out, one_send, examples=None):
        dw_out = _ffn_out_bwd_w(tag + "_out_bwd_w", a, dh_bf)
        dgu = _ffn_out_bwd_x(tag + "_out_bwd_x", dh_bf, w_out, gu, dep=None if one_send else send(tag + "_out", (dw_out,)))
        dw_in = _ffn_in_bwd_w(tag + "_in_bwd_w", n_in, dgu)
        token = send(tag, (dw_in, dw_out)) if one_send else send(tag + "_in", (dw_in,))
        return _ffn_in_bwd_x(tag + "_in_bwd_x", dgu, w_in_blk, h_in, g_norm, dh, dep=token, examples=examples)

    dh2, dh2_bf, dg2 = ffn_bwd("ffn2", dh3, dh3_bf, h2, g2, n2, gu2, a2, w2i, w2o, True)

    dwo = _mm_tn("mix_out_bwd_w", mixed, dh2_bf)
    dya, dyb, dgates = _mix_out_gate_bwd("mix_out_gate_bwd", dh2_bf, wo, gates, ya, yb)
    doa, dob = _branch_bwd_x("branch_bwd_x", dya, dyb, wa, wb)
    dwa, dwb = _branch_bwd_w("branch_bwd_w", oa, ob, dya, dyb)
    dqkv3, dcq, dck = _fox_bwd("fox_bwd", qkv3, c_col, c_row, ob3, lse_b, dob.reshape(b, l, B_WIDTH))
    dqkv3, dsink = _swa_bwd("swa_bwd", qkv3, oa3, lse_a, doa.reshape(b, l, A_WIDTH), slopes, sink_rows, swa_bias, dqkv3)
    dqkv = dqkv3.reshape(t, QKV_W)
    df3, dbf = _fgate_bwd("fgate_bwd", f3, bf_row, dcq, dck)
    df = df3.reshape(t, 128)
    dwi_qkv = _mm_tn("proj_qkv_bwd_w", um, dqkv, tm_c=(512,), tn_c=(768,))
    dwi_g = _mm_tn("proj_gates_bwd_w", um, dgates, tm_c=(512,), tn_c=(512,))
    dwi_f = _mm_tn("proj_f_bwd_w", um, df, tm_c=(512,), tn_c=(128,))
    token = send("mix", (dwi_qkv, dwi_g, dwi_f, dwa, dwb, dwo))
    dh1, dh1_bf, dgm = _proj_bwd_x("proj_bwd_x", dqkv, dgates, df, wi, h1, gm, dh2, dep=token)

    grad_x, dmeta3, dg1 = ffn_bwd("ffn1", dh1, dh1_bf, h0, g1, n1, gu1, a1, w1i, w1o, False, examples=b)
    dmeta = dmeta3.reshape(b * N_META, D_MODEL)

    misc = jnp.concatenate([dbf[:, 0:B_HEADS], dsink[:, 0].reshape(1, A_HEADS), loss_blk[0:1, 0:1]], axis=1)
    misc = jnp.pad(misc, ((0, 0), (0, D_MODEL - misc.shape[1])))
    row = lax.broadcasted_iota(jnp.int32, (8, D_MODEL), 0)
    vec = jnp.zeros((8, D_MODEL), F32)
    for i, piece in enumerate((dg1, dgm, dg2, dgf, misc)):
        vec = jnp.where(row == i, piece, vec)
    small = jnp.concatenate([vec, dmeta], axis=0)
    return grad_x, small


def _pad_to(a, rows, cols):
    return jnp.pad(a, ((0, rows - a.shape[0]), (0, cols - a.shape[1])))


def _ffn_out_from_gathered(name, g):
    def body(g_ref, o_ref):
        o_ref[0:FFO_SHARD, :] = g_ref[0]
        o_ref[FFO_SHARD:FF_SHARD, :] = g_ref[1]
        o_ref[FF_SHARD:FF_SHARD_P, :] = jnp.zeros((FF_SHARD_P - FF_SHARD, D_MODEL), BF16)

    return pl.pallas_call(
        body, name=name, grid=(4,),
        in_specs=[pl.BlockSpec((2, FFO_SHARD, D_MODEL), lambda j: (j, 0, 0))],
        out_specs=pl.BlockSpec((FF_SHARD_P, D_MODEL), lambda j: (j, 0)),
        out_shape=jax.ShapeDtypeStruct((D_FF_P, D_MODEL), BF16), compiler_params=_params(("parallel",)),
    )(g)


def _ffn_out_bwd_w(name, a, dh):
    t = a.shape[0]
    tn = D_MODEL // 2

    def body(a_ref, b_ref, o_ref):
        acc = lax.dot_general(a_ref[...], _bf(b_ref[...]), _TN, preferred_element_type=F32) * 0.5
        o_ref[0] = acc[0:FFO_SHARD].astype(BF16)
        o_ref[1] = acc[FFO_SHARD:FF_SHARD].astype(BF16)

    return pl.pallas_call(
        body, name=name, grid=(2, 4),
        in_specs=[pl.BlockSpec((t, FF_SHARD_P), lambda j, i: (0, i)), pl.BlockSpec((t, tn), lambda j, i: (0, j))],
        out_specs=pl.BlockSpec((2, FFO_SHARD, tn), lambda j, i: (i, 0, j)),
        out_shape=jax.ShapeDtypeStruct((N_DEV, FFO_SHARD, D_MODEL), BF16), compiler_params=_params(("parallel", "parallel")),
    )(a, dh)


def _proj_segments():
    segs = [(HEAD_DIM * h, HEAD_DIM, 0, B_SEG + HEAD_DIM * A_HEAD_ORDER.index(h)) for h in range(A_HEADS)]
    segs += [(512, 128, 0, B_SEG + A_WIDTH), (640, 128, 0, B_SEG + A_WIDTH + 128)]
    for first, off in ((768, 0), (1280, 128), (1792, 256)):
        segs += [(first + 128 * hp, 128, 0, PAIR_W * hp + off) for hp in range(4)]
    segs += [(2304, B_HEADS, 2, 0), (2312, 2 * D_MODEL, 1, 0)]
    return segs


_RELAYOUT_ROWS = 256


def _proj_from_gathered(name, g):
    rows = _RELAYOUT_ROWS

    def body(g_ref, o_ref):
        def cols(first, width):
            out = []
            for p in range(N_DEV):
                lo, hi = max(first, WIN_SHARD * p), min(first + width, WIN_SHARD * (p + 1))
                if lo < hi:
                    out.append(g_ref[p, :, lo - WIN_SHARD * p:hi - WIN_SHARD * p])
            return out

        parts = []
        for arr in (0, 1, 2):
            for first, width, _, _ in sorted((s for s in _proj_segments() if s[2] == arr), key=lambda s: s[3]):
                parts += cols(first, width)
            if arr == 0:
                parts.append(jnp.zeros((rows, P_GATES - QKV_W), BF16))
        parts.append(jnp.zeros((rows, 128 - B_HEADS), BF16))
        o_ref[...] = jnp.concatenate(parts, axis=1)

    return pl.pallas_call(
        body, name=name, grid=(D_MODEL // rows,),
        in_specs=[pl.BlockSpec((N_DEV, rows, WIN_SHARD_P), lambda i: (0, i, 0))],
        out_specs=pl.BlockSpec((rows, PROJ_P), lambda i: (i, 0)),
        out_shape=jax.ShapeDtypeStruct((D_MODEL, PROJ_P), BF16), compiler_params=_params(("parallel",)),
    )(g)


def _proj_to_scatter(name, dqkv_w, dg_w, df_w):
    rows = _RELAYOUT_ROWS
    segs = sorted(_proj_segments())

    def body(q_ref, g_ref, f_ref, o_ref):
        arrays = (q_ref, g_ref, f_ref)
        for p in range(N_DEV):
            parts = []
            for first, width, arr, at in segs:
                lo, hi = max(first, WIN_SHARD * p), min(first + width, WIN_SHARD * (p + 1))
                if lo < hi:
                    parts.append(arrays[arr][:, at + lo - first:at + hi - first])
            parts.append(jnp.zeros((rows, WIN_SHARD_P - WIN_SHARD), BF16))
            o_ref[p] = jnp.concatenate(parts, axis=1)

    return pl.pallas_call(
        body, name=name, grid=(D_MODEL // rows,),
        in_specs=[pl.BlockSpec((rows, QKV_W), lambda i: (i, 0)), pl.BlockSpec((rows, 2 * D_MODEL), lambda i: (i, 0)),
                  pl.BlockSpec((rows, 128), lambda i: (i, 0))],
        out_specs=pl.BlockSpec((N_DEV, rows, WIN_SHARD_P), lambda i: (0, i, 0)),
        out_shape=jax.ShapeDtypeStruct((N_DEV, D_MODEL, WIN_SHARD_P), BF16), compiler_params=_params(("parallel",)),
    )(dqkv_w, dg_w, df_w)


def _a_rows_from_natural(w):
    return jnp.concatenate([w[HEAD_DIM * h:HEAD_DIM * (h + 1)] for h in A_HEAD_ORDER], axis=0)


def _a_rows_to_natural(w):
    return jnp.concatenate([w[HEAD_DIM * A_HEAD_ORDER.index(h):HEAD_DIM * (A_HEAD_ORDER.index(h) + 1)]
                            for h in range(A_HEADS)], axis=0)


def kernel(x, meta_tokens, ffn1_norm, ffn1_w_in, ffn1_w_out, mix_norm, w_in, b_forget, attn_sinks, w_branch_a, w_branch_b, w_out, ffn2_norm, ffn2_w_in, ffn2_w_out, final_norm, loss_target, m_meta_tokens, m_ffn1_norm, m_ffn1_w_in, m_ffn1_w_out, m_mix_norm, m_w_in, m_b_forget, m_attn_sinks, m_w_branch_a, m_w_branch_b, m_w_out, m_ffn2_norm, m_ffn2_w_in, m_ffn2_w_out, m_final_norm, v_meta_tokens, v_ffn1_norm, v_ffn1_w_in, v_ffn1_w_out, v_mix_norm, v_w_in, v_b_forget, v_attn_sinks, v_w_branch_a, v_w_branch_b, v_w_out, v_ffn2_norm, v_ffn2_w_in, v_ffn2_w_out, v_final_norm):
    me = 4 * lax.axis_index("x") + 2 * lax.axis_index("y") + lax.axis_index("c")

    def shard(w, tok, rows=None, cols=None):
        piece = (w[0] if tok is None else w[0] + tok[0, 0]).astype(BF16)
        return piece if rows is None else _pad_to(piece, rows, cols)

    first_level, second_level = {}, {}
    first_level["meta"], tok = _gather2_start("gather_meta_start", (meta_tokens,))
    first_level["ffn1_in"], tok = _gather2_start(
        "gather_ffn1_in_start", (shard(ffn1_w_in, None, D_MODEL, FF_SHARD_P),), after=tok)
    first_level["ffn1_out"], tok = _gather2_start("gather_ffn1_out_start", (shard(ffn1_w_out, tok),), after=tok)
    first_level["mix"], tok = _gather2_start(
        "gather_mix_start", (shard(w_in, tok, D_MODEL, WIN_SHARD_P), shard(w_branch_a, tok), shard(w_branch_b, tok),
                             shard(w_out, tok)), after=tok)
    first_level["ffn2"], tok = _gather2_start(
        "gather_ffn2_start", (shard(ffn2_w_in, tok, D_MODEL, FF_SHARD_P), shard(ffn2_w_out, tok)), after=tok)
    started = {"tok": tok}

    def weights(group, after):
        if group.endswith(":forward"):
            group = group[:-len(":forward")]
            second_level[group], token = _gather2_forward("gather_" + group + "_forward", first_level[group], after)
            return token
        if group == "meta":
            after = weights("meta:forward", started["tok"])
        if group == "ffn1_in":
            after = weights("ffn1_in:forward", after)
        got = _gather2_wait("gather_" + group + "_wait", second_level[group], after)
        if group == "mix":
            gwi, gwa, gwb, gwo = got
            return (_proj_from_gathered("proj_w_relayout", gwi), _a_rows_from_natural(gwa.transpose(1, 0, 2).reshape(A_WIDTH, D_MODEL)),
                    gwb.transpose(1, 0, 2).reshape(B_WIDTH, D_MODEL), gwo.reshape(D_MODEL, D_MODEL))
        if group == "ffn1_out":
            return (_ffn_out_from_gathered("ffn1_w_out_relayout", got[0]),)
        if group == "meta":
            return (got[0].transpose(1, 0, 2).reshape(N_META, D_MODEL),)
        if group == "ffn1_in":
            return (got[0].reshape(2, 4, D_MODEL, FF_SHARD_P),)
        return got[0].reshape(2, 4, D_MODEL, FF_SHARD_P), _ffn_out_from_gathered("ffn2_w_out_relayout", got[1])

    scatter_state = {}

    def send(group, grads):
        if group == "mix":
            dwi_qkv, dwi_g, dwi_f, dwa, dwb, dwo = grads
            dwa = _a_rows_to_natural(dwa)
            blocks = (_proj_to_scatter("proj_dw_relayout", dwi_qkv, dwi_g, dwi_f), dwa.reshape(A_WIDTH, N_DEV, 128).transpose(1, 0, 2),
                      dwb.reshape(B_WIDTH, N_DEV, 128).transpose(1, 0, 2), dwo.reshape(N_DEV, 128, D_MODEL))
        elif group.endswith("_in"):
            blocks = (grads[0].reshape(N_DEV, D_MODEL, FF_SHARD_P),)
        elif group.endswith("_out"):
            blocks = (grads[0],)
        else:
            blocks = (grads[0].reshape(N_DEV, D_MODEL, FF_SHARD_P), grads[1])
        scatter_state[group], token = _xchg_start("scatter_" + group + "_start", (), blocks)
        return token

    gf = final_norm.reshape(1, D_MODEL)
    grad_x, small = _local_step(x, loss_target, ffn1_norm, mix_norm, ffn2_norm, gf, b_forget, attn_sinks, weights, send)

    small_state, after = _xchg_start("gather_small_start", (small,), ())
    out = {}
    updates = (
        ("ffn2", (("ffn2_w_in", ffn2_w_in, m_ffn2_w_in, v_ffn2_w_in), ("ffn2_w_out", ffn2_w_out, m_ffn2_w_out, v_ffn2_w_out))),
        ("ffn1_out", (("ffn1_w_out", ffn1_w_out, m_ffn1_w_out, v_ffn1_w_out),)),
        ("mix", (("w_in", w_in, m_w_in, v_w_in), ("w_branch_a", w_branch_a, m_w_branch_a, v_w_branch_a),
                 ("w_branch_b", w_branch_b, m_w_branch_b, v_w_branch_b), ("w_out", w_out, m_w_out, v_w_out))),
    )
    last_update = ("ffn1_in", (("ffn1_w_in", ffn1_w_in, m_ffn1_w_in, v_ffn1_w_in),))

    def update(group, members, after):
        parts_list, blocks_list = _xchg_wait("scatter_" + group + "_wait", scatter_state[group], after)
        prev = None
        for (nm, w, m, v), parts, blocks in zip(members, parts_list, blocks_list):
            if nm.endswith("w_in"):
                res4 = _adam("adam_" + nm, w[0].T, m[0].T, v[0].T, parts, transposed=True, dep=prev, own=(blocks, me_arr))
                out[nm] = tuple(r.T[None] for r in res4)
            else:
                res4 = _adam("adam_" + nm, w[0], m[0], v[0], parts, dep=prev, own=(blocks, me_arr))
                out[nm] = tuple(r[None] for r in res4)
            prev = res4[0]
        return prev

    me_arr = me.reshape(1).astype(jnp.int32)
    for group, members in updates + (last_update,):
        after = update(group, members, after)
    (packs,), _ = _xchg_wait("gather_small_wait", small_state, after)

    row = lambda a: a.reshape(1, D_MODEL)
    tot, small_res = _small_update("small_update", packs, (
        (ffn1_norm, m_ffn1_norm, v_ffn1_norm), (mix_norm, m_mix_norm, v_mix_norm), (ffn2_norm, m_ffn2_norm, v_ffn2_norm),
        (row(final_norm), row(m_final_norm), row(v_final_norm)), (b_forget, m_b_forget, v_b_forget),
        (attn_sinks, m_attn_sinks, v_attn_sinks)))
    for nm, res4 in zip(("ffn1_norm", "mix_norm", "ffn2_norm", "final_norm", "b_forget", "attn_sinks"), small_res):
        out[nm] = tuple(r.reshape(D_MODEL) for r in res4) if nm == "final_norm" else res4
    loss = tot[4, 2 * B_HEADS]
    g_meta = lax.dynamic_slice(tot[8:24, :], (0, me * 128), (N_META, 128))
    out["meta_tokens"] = tuple(_adam("adam_meta_tokens", meta_tokens, m_meta_tokens, v_meta_tokens, g_meta[None]))

    names = ("meta_tokens", "ffn1_norm", "ffn1_w_in", "ffn1_w_out", "mix_norm", "w_in", "b_forget", "attn_sinks",
             "w_branch_a", "w_branch_b", "w_out", "ffn2_norm", "ffn2_w_in", "ffn2_w_out", "final_norm")
    return (loss, grad_x) + tuple(out[nm][kind] for kind in range(4) for nm in names)
```

```python
import jax
import jax.numpy as jnp
from jax import lax
from jax.experimental import pallas as pl
from jax.experimental.pallas import tpu as pltpu

F32 = jnp.float32
BF16 = jnp.bfloat16

D_MODEL = 1024
N_META = 16
BLOCK = 128
PREFIX = 128
N_PAD = PREFIX - N_META
HEAD_DIM = 64
A_HEADS = 8
B_HEADS = 8
A_WIDTH = 512
A_KV_WIDTH = 128
B_WIDTH = 512
D_FF = 2816
N_DEV = 8
FF_SHARD = 2 * D_FF // N_DEV
FF_SHARD_P = 768
FFO_SHARD = D_FF // N_DEV
D_FF_P = 4 * FF_SHARD_P
W_IN_COLS = 4360
WIN_SHARD = W_IN_COLS // N_DEV
WIN_SHARD_P = 640
PAIR_W = 3 * 128
B_SEG = 4 * PAIR_W
A_SEG = A_WIDTH + 2 * A_KV_WIDTH
QKV_W = B_SEG + A_SEG
P_GATES = 2 * (2 * D_MODEL)
P_F = P_GATES + 2 * D_MODEL
PROJ_P = P_F + 128
A_HEAD_ORDER = (0, 4, 1, 5, 2, 6, 3, 7)
EPS = 1e-6
NEG = -1e30
SCALE = HEAD_DIM ** -0.5
ADAM_LR = 0.001
ADAM_B1 = 0.9
ADAM_B2 = 0.999
ADAM_EPS = 1e-08
ADAM_WD = 0.01
ADAM_STEP = 10
VMEM_LIMIT = 56 * 1024 * 1024
MESH_ID = pl.DeviceIdType.MESH
SMALL_ROWS = 40

_NN = (((1,), (0,)), ((), ()))
_NT = (((1,), (1,)), ((), ()))
_TN = (((0,), (0,)), ((), ()))


def _params(sem=None):
    return pltpu.CompilerParams(dimension_semantics=sem, vmem_limit_bytes=VMEM_LIMIT)


def _pick(n, cands):
    for c in cands:
        if n % c == 0:
            return c
    raise ValueError(f"no tile for {n}")


def _bf(v):
    return v if v.dtype == BF16 else v.astype(BF16)


def _mm(name, a, b, dims, grid, a_spec, b_spec, o_spec, out_shape, out_dtype, alpha=1.0):
    def body(a_ref, b_ref, o_ref):
        acc = lax.dot_general(_bf(a_ref[...]), _bf(b_ref[...]), dims, preferred_element_type=F32)
        if alpha != 1.0:
            acc = acc * alpha
        o_ref[...] = acc.astype(o_ref.dtype)

    return pl.pallas_call(
        body, name=name, grid=grid, in_specs=[a_spec, b_spec], out_specs=o_spec,
        out_shape=jax.ShapeDtypeStruct(out_shape, out_dtype),
        compiler_params=_params(("parallel",) * len(grid)),
    )(a, b)


def _mm_res_norm(name, a, w, res, g_next, alpha=1.0):
    t, k = a.shape
    tm = _pick(t, (544, 384, 256, 128))

    def body(a_ref, w_ref, r_ref, g_ref, h_ref, n_ref):
        acc = lax.dot_general(_bf(a_ref[...]), w_ref[...], _NN, preferred_element_type=F32)
        if alpha != 1.0:
            acc = acc * alpha
        hv = acc + r_ref[...]
        h_ref[...] = hv
        r = lax.rsqrt(jnp.mean(hv * hv, axis=-1, keepdims=True) + EPS)
        n_ref[...] = ((hv * r) * g_ref[...]).astype(BF16)

    row = pl.BlockSpec((tm, D_MODEL), lambda i: (i, 0))
    return pl.pallas_call(
        body, name=name, grid=(t // tm,),
        in_specs=[pl.BlockSpec((tm, k), lambda i: (i, 0)), pl.BlockSpec((k, D_MODEL), lambda i: (0, 0)), row,
                  pl.BlockSpec((1, D_MODEL), lambda i: (0, 0))],
        out_specs=[row, row],
        out_shape=[jax.ShapeDtypeStruct((t, D_MODEL), F32), jax.ShapeDtypeStruct((t, D_MODEL), BF16)],
        compiler_params=_params(("parallel",)),
    )(a, w, res, g_next)


def _mm_tn(name, a, b, out_dtype=BF16, alpha=1.0, tm_c=(768, 512, 256, 128), tn_c=(512, 640, 256, 128)):
    t, m = a.shape
    n = b.shape[1]
    tm = _pick(m, tm_c)
    tn = _pick(n, tn_c)
    bytes_a, bytes_b = a.size * a.dtype.itemsize, b.size * b.dtype.itemsize
    if bytes_a + bytes_b * (m // tm) <= bytes_b + bytes_a * (n // tn):
        return _mm(name, a, b, _TN, (m // tm, n // tn),
                   pl.BlockSpec((t, tm), lambda i, j: (0, i)), pl.BlockSpec((t, tn), lambda i, j: (0, j)),
                   pl.BlockSpec((tm, tn), lambda i, j: (i, j)), (m, n), out_dtype, alpha=alpha)
    return _mm(name, a, b, _TN, (n // tn, m // tm),
               pl.BlockSpec((t, tm), lambda j, i: (0, i)), pl.BlockSpec((t, tn), lambda j, i: (0, j)),
               pl.BlockSpec((tm, tn), lambda j, i: (i, j)), (m, n), out_dtype, alpha=alpha)


def _ffn_in_fwd(name, n, wblk, dep=None):
    t = n.shape[0]
    tm = _pick(t, (1088, 768, 512, 256, 128))
    has_dep = dep is not None

    def body(n_ref, w_ref, *rest):
        gu_ref, a_ref = rest[-2], rest[-1]
        nv = n_ref[...]
        g = lax.dot_general(nv, w_ref[0], _NN, preferred_element_type=F32)
        u = lax.dot_general(nv, w_ref[1], _NN, preferred_element_type=F32)
        sg = jax.nn.sigmoid(g)
        silu = g * sg
        a_ref[...] = (silu * u).astype(BF16)
        gu_ref[0] = ((0.5 * u) * (sg + silu * (1.0 - sg))).astype(BF16)
        gu_ref[1] = (0.5 * silu).astype(BF16)

    return pl.pallas_call(
        body, name=name, grid=(t // tm, 4),
        in_specs=[pl.BlockSpec((tm, D_MODEL), lambda i, j: (i, 0)),
                  pl.BlockSpec((2, None, D_MODEL, FF_SHARD_P), lambda i, j: (0, j, 0, 0))]
        + ([pl.BlockSpec(memory_space=pl.ANY)] if has_dep else []),
        out_specs=[pl.BlockSpec((2, tm, FF_SHARD_P), lambda i, j: (0, i, j)),
                   pl.BlockSpec((tm, FF_SHARD_P), lambda i, j: (i, j))],
        out_shape=[jax.ShapeDtypeStruct((2, t, D_FF_P), BF16), jax.ShapeDtypeStruct((t, D_FF_P), BF16)],
        compiler_params=_params(("parallel", "parallel")),
    )(*((n, wblk) + ((dep,) if has_dep else ())))


def _ffn_out_bwd_x(name, dh, w_out, gu, dep=None):
    t = dh.shape[0]
    tm = _pick(t, (1088, 768, 512, 256, 128))
    has_dep = dep is not None

    def body(dh_ref, w_ref, gu_ref, *rest):
        o_ref = rest[-1]
        w = w_ref[pl.ds(pl.multiple_of(pl.program_id(1) * FF_SHARD_P, FF_SHARD_P), FF_SHARD_P), :]
        da = lax.dot_general(_bf(dh_ref[...]), w, _NT, preferred_element_type=F32)
        o_ref[0] = (da * gu_ref[0].astype(F32)).astype(BF16)
        o_ref[1] = (da * gu_ref[1].astype(F32)).astype(BF16)

    gu_spec = pl.BlockSpec((2, tm, FF_SHARD_P), lambda i, j: (0, i, j))
    return pl.pallas_call(
        body, name=name, grid=(t // tm, 4),
        in_specs=[pl.BlockSpec((tm, D_MODEL), lambda i, j: (i, 0)),
                  pl.BlockSpec((D_FF_P, D_MODEL), lambda i, j: (0, 0), pipeline_mode=pl.Buffered(1)),
                  gu_spec] + ([pl.BlockSpec(memory_space=pl.ANY)] if has_dep else []),
        out_specs=gu_spec, out_shape=jax.ShapeDtypeStruct((2, t, D_FF_P), BF16),
        compiler_params=_params(("parallel", "parallel")),
    )(*((dh, w_out, gu) + ((dep,) if has_dep else ())))


def _rms_bwd_rows(dn, h, g, dres):
    r = lax.rsqrt(jnp.mean(h * h, axis=-1, keepdims=True) + EPS)
    tv = dn * g
    dot = jnp.mean(tv * h, axis=-1, keepdims=True)
    return dres + (r * tv - h * (r * r * r * dot)), jnp.sum(dn * (h * r), axis=0, keepdims=True)


def _accumulate_rows(ref, part, first):
    @pl.when(first)
    def _():
        ref[...] = part

    @pl.when(jnp.logical_not(first))
    def _():
        ref[...] += part


def _ffn_in_bwd_x(name, dgu, wblk, h_in, g_norm, dres, dep=None, examples=None):
    t = dgu.shape[1]
    tm = _pick(t, (544, 384, 256, 128))
    has_dep = dep is not None
    split = examples is not None
    if split:
        l = t // examples
        per = l // tm
        assert per * tm == l and tm > PREFIX

    def body(d_ref, w_ref, h_ref, g_ref, r_ref, *rest):
        acc = None
        for s in range(2):
            for j in range(4):
                part = lax.dot_general(d_ref[s, :, FF_SHARD_P * j:FF_SHARD_P * (j + 1)], w_ref[s, j], _NT,
                                       preferred_element_type=F32)
                acc = part if acc is None else acc + part
        dh, dg = _rms_bwd_rows(acc, h_ref[...], g_ref[...], r_ref[...])
        i = pl.program_id(0)
        if not split:
            dh_ref, dhb_ref, dg_ref = rest[-3:]
            dh_ref[...] = dh
            dhb_ref[...] = dh.astype(BF16)
        else:
            gx_ref, meta_ref, dg_ref, buf, sem = rest[-5:]

            def out_copy(step):
                bi, r = step // per, step % per
                head = pltpu.make_async_copy(buf.at[pl.ds(PREFIX, tm - PREFIX)], gx_ref.at[bi, pl.ds(0, tm - PREFIX)], sem)
                if per == 1:
                    return r == 0, head, None
                return r == 0, head, pltpu.make_async_copy(
                    buf, gx_ref.at[bi, pl.ds(pl.multiple_of(jnp.maximum(r, 1) * tm - PREFIX, 8), tm)], sem)

            def run(step, method):
                is_head, head, later = out_copy(step)

                @pl.when(is_head)
                def _():
                    getattr(head, method)()

                if later is not None:
                    @pl.when(jnp.logical_not(is_head))
                    def _():
                        getattr(later, method)()

            @pl.when(i > 0)
            def _():
                run(i - 1, "wait")

            buf[...] = dh

            @pl.when(i % per == 0)
            def _():
                meta_ref[...] = dh[N_PAD:PREFIX]

            run(i, "start")

            @pl.when(i == pl.num_programs(0) - 1)
            def _():
                run(i, "wait")

        _accumulate_rows(dg_ref, dg, i == 0)

    row = pl.BlockSpec((tm, D_MODEL), lambda i: (i, 0))
    vec = pl.BlockSpec((1, D_MODEL), lambda i: (0, 0))
    if split:
        out_specs = [pl.BlockSpec(memory_space=pl.ANY), pl.BlockSpec((None, N_META, D_MODEL), lambda i: (i // per, 0, 0)), vec]
        out_shape = [jax.ShapeDtypeStruct((examples, l - PREFIX, D_MODEL), F32),
                     jax.ShapeDtypeStruct((examples, N_META, D_MODEL), F32), jax.ShapeDtypeStruct((1, D_MODEL), F32)]
        scratch = [pltpu.VMEM((tm, D_MODEL), F32), pltpu.SemaphoreType.DMA(())]
    else:
        out_specs = [row, row, vec]
        out_shape = [jax.ShapeDtypeStruct((t, D_MODEL), F32), jax.ShapeDtypeStruct((t, D_MODEL), BF16),
                     jax.ShapeDtypeStruct((1, D_MODEL), F32)]
        scratch = []
    return pl.pallas_call(
        body, name=name, grid=(t // tm,),
        in_specs=[pl.BlockSpec((2, tm, D_FF_P), lambda i: (0, i, 0)),
                  pl.BlockSpec((2, 4, D_MODEL, FF_SHARD_P), lambda i: (0, 0, 0, 0), pipeline_mode=pl.Buffered(1)),
                  row, vec, row]
        + ([pl.BlockSpec(memory_space=pl.ANY)] if has_dep else []),
        out_specs=out_specs, out_shape=out_shape, scratch_shapes=scratch,
        compiler_params=_params(("arbitrary",)),
    )(*((dgu, wblk, h_in, g_norm, dres) + ((dep,) if has_dep else ())))


def _proj_fwd(name, um, wi):
    t = um.shape[0]
    tm = _pick(t, (544, 384, 256, 128))

    def body(u_ref, w_ref, q_ref, g_ref, f_ref):
        uv = u_ref[...]
        q_ref[...] = lax.dot_general(uv, w_ref[:, 0:QKV_W], _NN, preferred_element_type=F32).astype(BF16)
        g_ref[...] = lax.dot_general(uv, w_ref[:, P_GATES:P_F], _NN, preferred_element_type=F32).astype(BF16)
        f_ref[...] = lax.dot_general(uv, w_ref[:, P_F:PROJ_P], _NN, preferred_element_type=F32)

    return pl.pallas_call(
        body, name=name, grid=(t // tm,),
        in_specs=[pl.BlockSpec((tm, D_MODEL), lambda i: (i, 0)),
                  pl.BlockSpec((D_MODEL, PROJ_P), lambda i: (0, 0), pipeline_mode=pl.Buffered(1))],
        out_specs=[pl.BlockSpec((tm, QKV_W), lambda i: (i, 0)), pl.BlockSpec((tm, 2 * D_MODEL), lambda i: (i, 0)),
                   pl.BlockSpec((tm, 128), lambda i: (i, 0))],
        out_shape=[jax.ShapeDtypeStruct((t, QKV_W), BF16), jax.ShapeDtypeStruct((t, 2 * D_MODEL), BF16),
                   jax.ShapeDtypeStruct((t, 128), F32)],
        compiler_params=_params(("parallel",)),
    )(um, wi)


def _proj_bwd_x(name, dqkv, dgates, df, wi, h_in, g_norm, dres, dep=None):
    t = dqkv.shape[0]
    tm = _pick(t, (544, 384, 256, 128))
    has_dep = dep is not None

    def body(q_ref, gt_ref, f_ref, w_ref, h_ref, g_ref, r_ref, *rest):
        dh_ref, dhb_ref, dg_ref = rest[-3:]
        acc = lax.dot_general(q_ref[...], w_ref[:, 0:QKV_W], _NT, preferred_element_type=F32)
        acc = acc + lax.dot_general(gt_ref[...], w_ref[:, P_GATES:P_F], _NT, preferred_element_type=F32)
        acc = acc + lax.dot_general(f_ref[...], w_ref[:, P_F:PROJ_P], _NT, preferred_element_type=F32)
        dh, dg = _rms_bwd_rows(acc, h_ref[...], g_ref[...], r_ref[...])
        dh_ref[...] = dh
        dhb_ref[...] = dh.astype(BF16)
        _accumulate_rows(dg_ref, dg, pl.program_id(0) == 0)

    row = pl.BlockSpec((tm, D_MODEL), lambda i: (i, 0))
    vec = pl.BlockSpec((1, D_MODEL), lambda i: (0, 0))
    return pl.pallas_call(
        body, name=name, grid=(t // tm,),
        in_specs=[pl.BlockSpec((tm, QKV_W), lambda i: (i, 0)), pl.BlockSpec((tm, 2 * D_MODEL), lambda i: (i, 0)),
                  pl.BlockSpec((tm, 128), lambda i: (i, 0)),
                  pl.BlockSpec((D_MODEL, PROJ_P), lambda i: (0, 0), pipeline_mode=pl.Buffered(1)), row, vec, row]
        + ([pl.BlockSpec(memory_space=pl.ANY)] if has_dep else []),
        out_specs=[row, row, vec],
        out_shape=[jax.ShapeDtypeStruct((t, D_MODEL), F32), jax.ShapeDtypeStruct((t, D_MODEL), BF16),
                   jax.ShapeDtypeStruct((1, D_MODEL), F32)],
        compiler_params=_params(("arbitrary",)),
    )(*((dqkv, dgates, df, wi, h_in, g_norm, dres) + ((dep,) if has_dep else ())))


def _ffn_in_bwd_w(name, n, dgu):
    t = n.shape[0]
    return _mm(name, n, dgu, _TN, (2, 4),
               pl.BlockSpec((t, D_MODEL), lambda s, j: (0, 0)),
               pl.BlockSpec((None, t, FF_SHARD_P), lambda s, j: (s, 0, j)),
               pl.BlockSpec((None, None, D_MODEL, FF_SHARD_P), lambda s, j: (s, j, 0, 0)),
               (2, 4, D_MODEL, FF_SHARD_P), BF16)


def _embed_norm(name, x, meta, g):
    b, s, _ = x.shape
    half = (s + PREFIX) // 2
    first = half - PREFIX
    assert first > 0 and half % 16 == 0

    def body(x_ref, m_ref, g_ref, h_ref, n_ref, buf, sem):
        bi, k = pl.program_id(0), pl.program_id(1)

        def tokens(example, second, method):
            if second:
                cp = pltpu.make_async_copy(x_ref.at[example, pl.ds(first, half)], buf.at[1], sem.at[1])
            else:
                cp = pltpu.make_async_copy(x_ref.at[example, pl.ds(0, first)], buf.at[0, pl.ds(PREFIX, first)], sem.at[0])
            getattr(cp, method)()

        @pl.when((bi == 0) & (k == 0))
        def _():
            tokens(0, False, "start")

        @pl.when(k == 0)
        def _():
            tokens(bi, True, "start")
            tokens(bi, False, "wait")
            buf[0, 0:N_PAD, :] = jnp.zeros((N_PAD, D_MODEL), F32)
            buf[0, N_PAD:PREFIX, :] = m_ref[...]

        @pl.when(k == 1)
        def _():
            @pl.when(bi + 1 < b)
            def _():
                tokens(bi + 1, False, "start")

            tokens(bi, True, "wait")

        hv = buf[k]
        h_ref[...] = hv
        r = lax.rsqrt(jnp.mean(hv * hv, axis=-1, keepdims=True) + EPS)
        n_ref[...] = ((hv * r) * g_ref[...]).astype(BF16)

    rows = pl.BlockSpec((half, D_MODEL), lambda bi, k: (2 * bi + k, 0))
    return pl.pallas_call(
        body, name=name, grid=(b, 2),
        in_specs=[pl.BlockSpec(memory_space=pl.ANY), pl.BlockSpec((N_META, D_MODEL), lambda bi, k: (0, 0)),
                  pl.BlockSpec((1, D_MODEL), lambda bi, k: (0, 0))],
        out_specs=[rows, rows],
        out_shape=[jax.ShapeDtypeStruct((2 * b * half, D_MODEL), F32), jax.ShapeDtypeStruct((2 * b * half, D_MODEL), BF16)],
        scratch_shapes=[pltpu.VMEM((2, half, D_MODEL), F32), pltpu.SemaphoreType.DMA((2,))],
        compiler_params=_params(("arbitrary", "arbitrary")),
    )(x, meta, g)


def _branch_gate_fwd(name, oa, ob, wa, wb, gates, dep=None):
    t = gates.shape[0]
    tm = _pick(t, (544, 384, 256, 128))
    has_dep = dep is not None

    def body(oa_ref, ob_ref, wa_ref, wb_ref, g_ref, *rest):
        o_ref, ya_ref, yb_ref = rest[-3:]
        ya = lax.dot_general(_bf(oa_ref[...]), wa_ref[...], _NN, preferred_element_type=F32)
        yb = lax.dot_general(_bf(ob_ref[...]), wb_ref[...], _NN, preferred_element_type=F32)
        sa = jax.nn.sigmoid(g_ref[:, 0:D_MODEL].astype(F32))
        sb = jax.nn.sigmoid(g_ref[:, D_MODEL:2 * D_MODEL].astype(F32))
        o_ref[...] = (sa * ya + sb * yb).astype(BF16)
        ya_ref[...] = ya.astype(BF16)
        yb_ref[...] = yb.astype(BF16)

    blk = pl.BlockSpec((tm, D_MODEL), lambda i: (i, 0))
    narrow = pl.BlockSpec((tm, A_WIDTH), lambda i: (i, 0))
    wide = pl.BlockSpec((tm, 2 * D_MODEL), lambda i: (i, 0))
    wspec = pl.BlockSpec((A_WIDTH, D_MODEL), lambda i: (0, 0))
    out = jax.ShapeDtypeStruct((t, D_MODEL), BF16)
    return pl.pallas_call(
        body, name=name, grid=(t // tm,),
        in_specs=[narrow, narrow, wspec, wspec, wide] + ([pl.BlockSpec(memory_space=pl.ANY)] if has_dep else []),
        out_specs=[blk, blk, blk], out_shape=[out, out, out], compiler_params=_params(("parallel",)),
    )(*((oa, ob, wa, wb, gates) + ((dep,) if has_dep else ())))


def _branch_bwd_x(name, dya, dyb, wa, wb):
    t = dya.shape[0]
    tm = _pick(t, (1088, 768, 512, 256, 128))

    def body(da_ref, db_ref, wa_ref, wb_ref, oa_ref, ob_ref):
        oa_ref[...] = lax.dot_general(da_ref[...], wa_ref[...], _NT, preferred_element_type=F32)
        ob_ref[...] = lax.dot_general(db_ref[...], wb_ref[...], _NT, preferred_element_type=F32).astype(BF16)

    blk = pl.BlockSpec((tm, D_MODEL), lambda i: (i, 0))
    narrow = pl.BlockSpec((tm, A_WIDTH), lambda i: (i, 0))
    wspec = pl.BlockSpec((A_WIDTH, D_MODEL), lambda i: (0, 0), pipeline_mode=pl.Buffered(1))
    return pl.pallas_call(
        body, name=name, grid=(t // tm,), in_specs=[blk, blk, wspec, wspec], out_specs=[narrow, narrow],
        out_shape=[jax.ShapeDtypeStruct((t, A_WIDTH), F32), jax.ShapeDtypeStruct((t, B_WIDTH), BF16)],
        compiler_params=_params(("parallel",)),
    )(dya, dyb, wa, wb)


def _branch_bwd_w(name, oa, ob, dya, dyb):
    t = oa.shape[0]
    tn = 512

    def body(oa_ref, ob_ref, da_ref, db_ref, wa_ref, wb_ref):
        wa_ref[...] = lax.dot_general(_bf(oa_ref[...]), da_ref[...], _TN, preferred_element_type=F32).astype(BF16)
        wb_ref[...] = lax.dot_general(_bf(ob_ref[...]), db_ref[...], _TN, preferred_element_type=F32).astype(BF16)

    whole = pl.BlockSpec((t, A_WIDTH), lambda j: (0, 0), pipeline_mode=pl.Buffered(1))
    cols = pl.BlockSpec((t, tn), lambda j: (0, j))
    out_spec = pl.BlockSpec((A_WIDTH, tn), lambda j: (0, j))
    out = jax.ShapeDtypeStruct((A_WIDTH, D_MODEL), BF16)
    return pl.pallas_call(
        body, name=name, grid=(D_MODEL // tn,), in_specs=[whole, whole, cols, cols], out_specs=[out_spec, out_spec],
        out_shape=[out, out], compiler_params=_params(("parallel",)),
    )(oa, ob, dya, dyb)


def _mix_out_gate_bwd(name, dh, wo, gates, ya, yb):
    t = gates.shape[0]
    tm = _pick(t, (544, 384, 256, 128))

    def body(dh_ref, w_ref, g_ref, ya_ref, yb_ref, dya_ref, dyb_ref, dg_ref):
        dm = lax.dot_general(_bf(dh_ref[...]), w_ref[...], _NT, preferred_element_type=F32)
        sa = jax.nn.sigmoid(g_ref[:, 0:D_MODEL].astype(F32))
        sb = jax.nn.sigmoid(g_ref[:, D_MODEL:2 * D_MODEL].astype(F32))
        dya_ref[...] = (dm * sa).astype(BF16)
        dyb_ref[...] = (dm * sb).astype(BF16)
        dg_ref[:, 0:D_MODEL] = (dm * ya_ref[...].astype(F32) * (sa * (1.0 - sa))).astype(BF16)
        dg_ref[:, D_MODEL:2 * D_MODEL] = (dm * yb_ref[...].astype(F32) * (sb * (1.0 - sb))).astype(BF16)

    blk = pl.BlockSpec((tm, D_MODEL), lambda i: (i, 0))
    wide = pl.BlockSpec((tm, 2 * D_MODEL), lambda i: (i, 0))
    out = jax.ShapeDtypeStruct((t, D_MODEL), BF16)
    return pl.pallas_call(
        body, name=name, grid=(t // tm,),
        in_specs=[blk, pl.BlockSpec((D_MODEL, D_MODEL), lambda i: (0, 0)), wide, blk, blk], out_specs=[blk, blk, wide],
        out_shape=[out, out, jax.ShapeDtypeStruct((t, 2 * D_MODEL), BF16)], compiler_params=_params(("parallel",)),
    )(dh, wo, gates, ya, yb)


def _ffn_out_loss(name, a, w, res, gf, tgt, alpha):
    t, k = a.shape
    b, s, _ = tgt.shape
    l = t // b
    tm = _pick(t, (544, 384, 256, 128))
    per = l // tm
    assert per * tm == l and tm > PREFIX and l - PREFIX == s

    def body(a_ref, w_ref, r_ref, g_ref, t_ref, dh_ref, dhb_ref, loss_ref, dg_ref, buf, sem):
        i = pl.program_id(0)
        bi, r = i // per, i % per

        def first_rows():
            return pltpu.make_async_copy(t_ref.at[bi, pl.ds(0, tm - PREFIX)], buf.at[pl.ds(PREFIX, tm - PREFIX)], sem)

        def later_rows():
            return pltpu.make_async_copy(t_ref.at[bi, pl.ds(pl.multiple_of(r * tm - PREFIX, 8), tm)], buf, sem)

        @pl.when(r == 0)
        def _():
            buf[0:PREFIX, :] = jnp.zeros((PREFIX, D_MODEL), F32)
            first_rows().start()

        if per > 1:
            @pl.when(r > 0)
            def _():
                later_rows().start()

        acc = lax.dot_general(a_ref[...], w_ref[...], _NN, preferred_element_type=F32)

        @pl.when(r == 0)
        def _():
            first_rows().wait()

        if per > 1:
            @pl.when(r > 0)
            def _():
                later_rows().wait()

        hv = acc * alpha + r_ref[...]
        row = lax.broadcasted_iota(jnp.int32, (tm, 1), 0)
        real = ((r > 0) | (row >= PREFIX)).astype(F32)
        g = g_ref[...]
        rn = lax.rsqrt(jnp.mean(hv * hv, axis=-1, keepdims=True) + EPS)
        xn = hv * rn
        err = (xn * g - buf[...]) * real
        lpart = 0.5 * jnp.sum(jnp.mean(err * err, axis=-1, keepdims=True), axis=0, keepdims=True)
        dy = err * (1.0 / D_MODEL)
        tv = dy * g
        dot = jnp.mean(tv * hv, axis=-1, keepdims=True)
        dh = rn * tv - hv * (rn * rn * rn * dot)
        dh_ref[...] = dh
        dhb_ref[...] = dh.astype(BF16)
        gpart = jnp.sum(dy * xn, axis=0, keepdims=True)

        @pl.when(i == 0)
        def _():
            loss_ref[...] = jnp.zeros_like(loss_ref)
            dg_ref[...] = jnp.zeros_like(dg_ref)

        loss_ref[...] += jnp.broadcast_to(lpart, loss_ref.shape)
        dg_ref[...] += gpart

    row_spec = pl.BlockSpec((tm, D_MODEL), lambda i: (i, 0))
    vec = pl.BlockSpec((1, D_MODEL), lambda i: (0, 0))
    return pl.pallas_call(
        body, name=name, grid=(t // tm,),
        in_specs=[pl.BlockSpec((tm, k), lambda i: (i, 0)),
                  pl.BlockSpec((k, D_MODEL), lambda i: (0, 0), pipeline_mode=pl.Buffered(1)), row_spec, vec,
                  pl.BlockSpec(memory_space=pl.ANY)],
        out_specs=[row_spec, row_spec, pl.BlockSpec((8, 128), lambda i: (0, 0)), vec],
        out_shape=[jax.ShapeDtypeStruct((t, D_MODEL), F32), jax.ShapeDtypeStruct((t, D_MODEL), BF16),
                   jax.ShapeDtypeStruct((8, 128), F32), jax.ShapeDtypeStruct((1, D_MODEL), F32)],
        scratch_shapes=[pltpu.VMEM((tm, D_MODEL), F32), pltpu.SemaphoreType.DMA(())],
        compiler_params=_params(("arbitrary",)),
    )(a, w, res, gf, tgt)


def _fgate_fwd(name, f3, bf_row):
    b, l, _ = f3.shape
    nb = l // BLOCK

    def body(f_ref, b_ref, cc_ref, cr_ref):
        r_i = lax.broadcasted_iota(jnp.int32, (BLOCK, BLOCK), 0)
        c_i = lax.broadcasted_iota(jnp.int32, (BLOCK, BLOCK), 1)
        tri = (r_i >= c_i).astype(F32)
        carry = jnp.zeros((1, 128), F32)
        for blk in range(nb):
            rows = slice(blk * BLOCK, (blk + 1) * BLOCK)
            z = f_ref[rows, :] + b_ref[...]
            lf = jnp.minimum(z, 0.0) - jnp.log(1.0 + jnp.exp(-jnp.abs(z)))
            cb = jnp.dot(tri, lf, preferred_element_type=F32, precision=lax.Precision.HIGHEST) + carry
            carry = cb[BLOCK - 1:BLOCK, :]
            cbt = cb.T
            for hh in range(B_HEADS):
                cc_ref[hh, rows, :] = jnp.sum(jnp.where(c_i == hh, cb, 0.0), axis=1, keepdims=True)
                cr_ref[hh, :, rows] = cbt[hh:hh + 1, :]

    return pl.pallas_call(
        body, name=name, grid=(b,),
        in_specs=[pl.BlockSpec((None, l, 128), lambda bi: (bi, 0, 0)),
                  pl.BlockSpec((1, 128), lambda bi: (0, 0))],
        out_specs=[pl.BlockSpec((None, B_HEADS, l, 1), lambda bi: (bi, 0, 0, 0)),
                   pl.BlockSpec((None, B_HEADS, 1, l), lambda bi: (bi, 0, 0, 0))],
        out_shape=[jax.ShapeDtypeStruct((b, B_HEADS, l, 1), F32), jax.ShapeDtypeStruct((b, B_HEADS, 1, l), F32)],
        compiler_params=_params(("parallel",)),
    )(f3, bf_row)


def _fgate_bwd(name, f3, bf_row, dcq, dck):
    b, l, _ = f3.shape
    nb = l // BLOCK

    def body(f_ref, b_ref, dcq_ref, dck_ref, df_ref, db_ref):
        r_i = lax.broadcasted_iota(jnp.int32, (BLOCK, BLOCK), 0)
        c_i = lax.broadcasted_iota(jnp.int32, (BLOCK, BLOCK), 1)
        tri = (r_i <= c_i).astype(F32)
        carry = jnp.zeros((1, 128), F32)
        total = jnp.zeros((1, 128), F32)
        for blk in range(nb - 1, -1, -1):
            rows = slice(blk * BLOCK, (blk + 1) * BLOCK)
            krows = jnp.concatenate([dck_ref[hh, :, rows] for hh in range(B_HEADS)]
                                    + [jnp.zeros((BLOCK - B_HEADS, BLOCK), F32)], axis=0)
            dcb = krows.T
            for hh in range(B_HEADS):
                dcb = dcb + jnp.where(c_i == hh, dcq_ref[hh, rows, :], 0.0)
            rc = jnp.dot(tri, dcb, preferred_element_type=F32, precision=lax.Precision.HIGHEST) + carry
            carry = rc[0:1, :]
            z = f_ref[rows, :] + b_ref[...]
            df = rc * (1.0 / (1.0 + jnp.exp(z)))
            df_ref[rows, :] = df.astype(BF16)
            total = total + jnp.sum(df, axis=0, keepdims=True)

        @pl.when(pl.program_id(0) == 0)
        def _():
            db_ref[...] = total

        @pl.when(pl.program_id(0) > 0)
        def _():
            db_ref[...] += total

    return pl.pallas_call(
        body, name=name, grid=(b,),
        in_specs=[pl.BlockSpec((None, l, 128), lambda bi: (bi, 0, 0)),
                  pl.BlockSpec((1, 128), lambda bi: (0, 0)),
                  pl.BlockSpec((None, B_HEADS, l, 1), lambda bi: (bi, 0, 0, 0)),
                  pl.BlockSpec((None, B_HEADS, 1, l), lambda bi: (bi, 0, 0, 0))],
        out_specs=[pl.BlockSpec((None, l, 128), lambda bi: (bi, 0, 0)), pl.BlockSpec((1, 128), lambda bi: (0, 0))],
        out_shape=[jax.ShapeDtypeStruct((b, l, 128), BF16), jax.ShapeDtypeStruct((1, 128), F32)],
        compiler_params=_params(("arbitrary",)),
    )(f3, bf_row, dcq, dck)


A_Q_BLK = B_SEG // A_WIDTH
A_K_BLK = (B_SEG + A_WIDTH) // 128
A_V_BLK = A_K_BLK + 1
A_SEG_BLK = B_SEG // A_SEG
STACK = A_HEADS * BLOCK


def _lane_lo():
    return lax.broadcasted_iota(jnp.int32, (1, 128), 1) < HEAD_DIM


def _stack_heads(x, masked):
    lo = _lane_lo()
    blks = [x[:, 128 * j:128 * (j + 1)] for j in range(4)]
    if not masked:
        return jnp.concatenate(blks + blks, axis=0)
    zero = jnp.zeros_like(blks[0])
    return jnp.concatenate([jnp.where(lo, bk, zero) for bk in blks] + [jnp.where(lo, zero, bk) for bk in blks], axis=0)


def _unstack_heads(y):
    lo = _lane_lo()
    return jnp.concatenate([jnp.where(lo, y[128 * j:128 * (j + 1)], y[128 * (4 + j):128 * (5 + j)]) for j in range(4)], axis=1)


def _swa_bias(slopes):
    r_i = jnp.arange(STACK)[:, None]
    c_i = jnp.arange(3 * BLOCK)[None, :]
    seg = c_i >> 7
    out = []
    for n in range(3):
        qpos = n * BLOCK + (r_i & (BLOCK - 1))
        kpos = jnp.where(seg == 0, c_i, (n - 2) * BLOCK + c_i)
        dist = qpos - kpos
        band = (seg != 0) & (dist < BLOCK) & (kpos >= PREFIX)
        meta = (seg == 0) & (c_i >= N_PAD)
        out.append(jnp.where((dist >= 0) & (band | meta), -slopes * dist.astype(F32), NEG))
    return jnp.stack(out, axis=0)


def _swa_scores(q, kcat, n, slope, bias):
    s = lax.dot_general(q, kcat, _NT, preferred_element_type=F32) + bias
    further = slope * (-BLOCK * jnp.maximum(n - 2, 0)).astype(F32)
    return jnp.concatenate([s[:, 0:BLOCK] + further, s[:, BLOCK:]], axis=1)


def _swa_specs():
    def kv(col_blk):
        return [pl.BlockSpec((None, BLOCK, 128), lambda b, n: (b, 0, col_blk)),
                pl.BlockSpec((None, BLOCK, 128), lambda b, n: (b, jnp.maximum(n - 1, 0), col_blk)),
                pl.BlockSpec((None, BLOCK, 128), lambda b, n: (b, n, col_blk))]

    q_spec = pl.BlockSpec((None, BLOCK, A_WIDTH), lambda b, n: (b, n, A_Q_BLK))
    o_spec = pl.BlockSpec((None, BLOCK, A_WIDTH), lambda b, n: (b, n, 0))
    col = pl.BlockSpec((STACK, 1), lambda b, n: (0, 0))
    bias = pl.BlockSpec((None, STACK, 3 * BLOCK), lambda b, n: (jnp.minimum(n, 2), 0, 0))
    lse_spec = pl.BlockSpec((None, A_HEADS, BLOCK, 1), lambda b, n: (b, 0, n, 0))
    return q_spec, kv(A_K_BLK), kv(A_V_BLK), o_spec, [col, col, bias], lse_spec


def _swa_fwd(name, qkv, slopes, sinks, bias):
    b, l, _ = qkv.shape
    nb = l // BLOCK

    def body(q_ref, k0_ref, kp_ref, kc_ref, v0_ref, vp_ref, vc_ref, sl_ref, sk_ref, bias_ref, o_ref, lse_ref):
        n = pl.program_id(1)
        qs = _stack_heads(q_ref[...], True) * SCALE
        kcat = jnp.concatenate([k0_ref[...], kp_ref[...], kc_ref[...]], axis=0)
        vcat = jnp.concatenate([v0_ref[...], vp_ref[...], vc_ref[...]], axis=0)
        s = _swa_scores(qs, kcat, n, sl_ref[...], bias_ref[...])
        sink = sk_ref[...]
        m = jnp.maximum(jnp.max(s, axis=-1, keepdims=True), sink)
        p = jnp.exp(s - m)
        den = jnp.sum(p, axis=-1, keepdims=True) + jnp.exp(sink - m)
        o = lax.dot_general(p.astype(BF16), vcat, _NN, preferred_element_type=F32) / den
        o_ref[...] = _unstack_heads(o)
        lse_ref[...] = (m + jnp.log(den)).reshape(A_HEADS, BLOCK, 1)

    q_spec, k_specs, v_specs, o_spec, consts, lse_spec = _swa_specs()
    return pl.pallas_call(
        body, name=name, grid=(b, nb),
        in_specs=[q_spec] + k_specs + v_specs + consts, out_specs=[o_spec, lse_spec],
        out_shape=[jax.ShapeDtypeStruct((b, l, A_WIDTH), F32), jax.ShapeDtypeStruct((b, A_HEADS, l, 1), F32)],
        compiler_params=_params(("parallel", "parallel")),
    )(qkv, qkv, qkv, qkv, qkv, qkv, qkv, slopes, sinks, bias)


def _swa_bwd(name, qkv, o, lse, do, slopes, sinks, bias, dqkv):
    b, l, _ = qkv.shape
    nb = l // BLOCK

    def body(q_ref, k0_ref, kp_ref, kc_ref, v0_ref, vp_ref, vc_ref, o_ref, lse_ref, do_ref, sl_ref, sk_ref, bias_ref, _,
             dx_ref, ds_ref, dk_acc, dv_acc):
        bi = pl.program_id(0)
        n = pl.program_id(1)
        qs = _stack_heads(q_ref[...], True) * SCALE
        dos32 = _stack_heads(do_ref[...], True)
        dos = dos32.astype(BF16)
        os_ = _stack_heads(o_ref[...], False)
        lsev = lse_ref[...].reshape(STACK, 1)
        kcat = jnp.concatenate([k0_ref[...], kp_ref[...], kc_ref[...]], axis=0)
        vcat = jnp.concatenate([v0_ref[...], vp_ref[...], vc_ref[...]], axis=0)
        s = _swa_scores(qs, kcat, n, sl_ref[...], bias_ref[...])
        p = jnp.exp(s - lsev)
        dsum = jnp.sum(dos32 * os_, axis=-1, keepdims=True)
        dp = lax.dot_general(dos, vcat, _NT, preferred_element_type=F32)
        dsc = (p * (dp - dsum)).astype(BF16)
        dq = lax.dot_general(dsc, kcat, _NN, preferred_element_type=F32) * SCALE
        row0 = pl.multiple_of(n * BLOCK, BLOCK)
        dx_ref[pl.ds(row0, BLOCK), 0:A_WIDTH] = _unstack_heads(dq).astype(BF16)
        dkc = lax.dot_general(dsc, qs, _TN, preferred_element_type=F32)
        dvc = lax.dot_general(p.astype(BF16), dos, _TN, preferred_element_type=F32)

        @pl.when(n == 0)
        def _():
            dk_acc[...] = jnp.zeros_like(dk_acc)
            dv_acc[...] = jnp.zeros_like(dv_acc)

        starts = (0, pl.multiple_of(jnp.maximum(n - 1, 0) * BLOCK, BLOCK), row0)
        for t, st in enumerate(starts):
            dk_acc[pl.ds(st, BLOCK), :] += dkc[t * BLOCK:(t + 1) * BLOCK, :]
            dv_acc[pl.ds(st, BLOCK), :] += dvc[t * BLOCK:(t + 1) * BLOCK, :]

        @pl.when(n == nb - 1)
        def _():
            dx_ref[:, A_WIDTH:A_WIDTH + 128] = dk_acc[...].astype(BF16)
            dx_ref[:, A_WIDTH + 128:A_SEG] = dv_acc[...].astype(BF16)

        dsink = -(jnp.exp(sk_ref[...] - lsev) * dsum)
        r8 = lax.broadcasted_iota(jnp.int32, (8, 128), 0)
        acc = jnp.zeros((8, 128), F32)
        for hh in range(A_HEADS):
            acc = acc + jnp.where(r8 == hh, jnp.sum(dsink[hh * BLOCK:(hh + 1) * BLOCK, :]), 0.0)

        @pl.when((bi == 0) & (n == 0))
        def _():
            ds_ref[...] = jnp.zeros_like(ds_ref)

        ds_ref[...] += acc

    q_spec, k_specs, v_specs, o_spec, consts, lse_spec = _swa_specs()
    return pl.pallas_call(
        body, name=name, grid=(b, nb),
        in_specs=[q_spec] + k_specs + v_specs + [o_spec, lse_spec, o_spec] + consts + [pl.BlockSpec(memory_space=pl.ANY)],
        out_specs=[pl.BlockSpec((None, l, A_SEG), lambda bb, n: (bb, 0, A_SEG_BLK)),
                   pl.BlockSpec((8, 128), lambda bb, n: (0, 0))],
        out_shape=[jax.ShapeDtypeStruct(dqkv.shape, BF16), jax.ShapeDtypeStruct((8, 128), F32)],
        scratch_shapes=[pltpu.VMEM((l, 128), F32), pltpu.VMEM((l, 128), F32)],
        input_output_aliases={13: 0},
        compiler_params=_params(("arbitrary", "arbitrary")),
    )(qkv, qkv, qkv, qkv, qkv, qkv, qkv, o, lse, do, slopes, sinks, bias, dqkv)


def _fox_mask(qk, ck, i):
    kh = qk.shape[1]
    qpos = i * BLOCK + lax.broadcasted_iota(jnp.int32, (BLOCK, kh), 0)
    kpos = lax.broadcasted_iota(jnp.int32, (BLOCK, kh), 1)
    return jnp.where((kpos <= qpos) & (kpos >= N_PAD), qk - ck, NEG)


def _pick_head(x, hh):
    lo = _lane_lo()
    return jnp.where(lo if hh == 0 else jnp.logical_not(lo), x, jnp.zeros_like(x))


def _both_heads(x):
    return jnp.concatenate([_pick_head(x, 0), _pick_head(x, 1)], axis=0)


def _fox_specs(l):
    pair = pl.BlockSpec((None, l, PAIR_W), lambda bi, hp: (bi, 0, hp))
    half = pl.BlockSpec((None, l, 128), lambda bi, hp: (bi, 0, hp))
    colv = pl.BlockSpec((None, 2, l, 1), lambda bi, hp: (bi, hp, 0, 0))
    rowv = pl.BlockSpec((None, 2, 1, l), lambda bi, hp: (bi, hp, 0, 0))
    return pair, half, colv, rowv


def _fox_fwd(name, qkv, c_col, c_row):
    b, l, _ = qkv.shape
    nb = l // BLOCK

    def body(x_ref, cc_ref, cr_ref, o_ref, lse_ref):
        for i in range(nb):
            rows = slice(i * BLOCK, (i + 1) * BLOCK)
            kh = (i + 1) * BLOCK
            qblk = x_ref[rows, 0:128]
            kv = x_ref[0:kh, 128:256]
            vv = x_ref[0:kh, 256:384]
            qk = lax.dot_general(_both_heads(qblk) * SCALE, kv, _NT, preferred_element_type=F32)
            ps, dens = [], []
            for hh in range(2):
                s = _fox_mask(qk[hh * BLOCK:(hh + 1) * BLOCK], cr_ref[hh, :, 0:kh], i)
                m = jnp.max(s, axis=-1, keepdims=True)
                p = jnp.exp(s - m)
                den = jnp.sum(p, axis=-1, keepdims=True)
                ps.append(p.astype(BF16))
                dens.append(den)
                lse_ref[hh, rows, :] = (m + jnp.log(den)) + cc_ref[hh, rows, :]
            pv = lax.dot_general(jnp.concatenate(ps, axis=0), vv, _NN, preferred_element_type=F32)
            o_ref[rows, :] = jnp.where(_lane_lo(), pv[0:BLOCK] / dens[0], pv[BLOCK:2 * BLOCK] / dens[1]).astype(BF16)

    pair, half, colv, rowv = _fox_specs(l)
    return pl.pallas_call(
        body, name=name, grid=(b, 4), in_specs=[pair, colv, rowv], out_specs=[half, colv],
        out_shape=[jax.ShapeDtypeStruct((b, l, B_WIDTH), BF16), jax.ShapeDtypeStruct((b, B_HEADS, l, 1), F32)],
        compiler_params=_params(("parallel", "parallel")),
    )(qkv, c_col, c_row)


def _fox_bwd(name, qkv, c_col, c_row, o, lse, do):
    b, l, _ = qkv.shape
    nb = l // BLOCK

    def body(x_ref, cc_ref, cr_ref, o_ref, lse_ref, do_ref, dx_ref, dcq_ref, dck_ref, dk_acc, dv_acc):
        dk_acc[...] = jnp.zeros_like(dk_acc)
        dv_acc[...] = jnp.zeros_like(dv_acc)
        dck_ref[...] = jnp.zeros_like(dck_ref)
        for i in range(nb):
            rows = slice(i * BLOCK, (i + 1) * BLOCK)
            kh = (i + 1) * BLOCK
            qblk = x_ref[rows, 0:128]
            kv = x_ref[0:kh, 128:256]
            vv = x_ref[0:kh, 256:384]
            doblk = do_ref[rows, :]
            ov = o_ref[rows, :].astype(F32)
            q2 = _both_heads(qblk) * SCALE
            do2 = _both_heads(doblk)
            qk = lax.dot_general(q2, kv, _NT, preferred_element_type=F32)
            dp = lax.dot_general(do2, vv, _NT, preferred_element_type=F32)
            ps, dss = [], []
            for hh in range(2):
                half = slice(hh * BLOCK, (hh + 1) * BLOCK)
                s = _fox_mask(qk[half], cr_ref[hh, :, 0:kh], i)
                p = jnp.exp(s - (lse_ref[hh, rows, :] - cc_ref[hh, rows, :]))
                dsum = jnp.sum(do2[half].astype(F32) * ov, axis=-1, keepdims=True)
                ds = p * (dp[half] - dsum)
                ps.append(p.astype(BF16))
                dss.append(ds.astype(BF16))
                dcq_ref[hh, rows, :] = jnp.sum(ds, axis=-1, keepdims=True)
                dck_ref[hh, :, 0:kh] -= jnp.sum(ds, axis=0, keepdims=True)
            p2 = jnp.concatenate(ps, axis=0)
            ds2 = jnp.concatenate(dss, axis=0)
            dq = lax.dot_general(ds2, kv, _NN, preferred_element_type=F32) * SCALE
            dk_acc[0:kh, :] += lax.dot_general(ds2, q2, _TN, preferred_element_type=F32)
            dv_acc[0:kh, :] += lax.dot_general(p2, do2, _TN, preferred_element_type=F32)
            dx_ref[rows, 0:128] = jnp.where(_lane_lo(), dq[0:BLOCK], dq[BLOCK:2 * BLOCK]).astype(BF16)
        dx_ref[:, 128:256] = dk_acc[...].astype(BF16)
        dx_ref[:, 256:384] = dv_acc[...].astype(BF16)

    pair, half, colv, rowv = _fox_specs(l)
    return pl.pallas_call(
        body, name=name, grid=(b, 4), in_specs=[pair, colv, rowv, half, colv, half],
        out_specs=[pair, colv, rowv],
        out_shape=[jax.ShapeDtypeStruct(qkv.shape, BF16), jax.ShapeDtypeStruct((b, B_HEADS, l, 1), F32),
                   jax.ShapeDtypeStruct((b, B_HEADS, 1, l), F32)],
        scratch_shapes=[pltpu.VMEM((l, 128), F32), pltpu.VMEM((l, 128), F32)],
        compiler_params=_params(("parallel", "parallel")),
    )(qkv, c_col, c_row, o, lse, do)


_FLIPS = ((0, 0, 1), (0, 1, 0), (0, 1, 1), (1, 0, 0), (1, 0, 1), (1, 1, 0), (1, 1, 1))


def _peer_table():
    x, y, c = lax.axis_index("x"), lax.axis_index("y"), lax.axis_index("c")
    me = 4 * x + 2 * y + c
    peers = []
    for fx, fy, fc in _FLIPS:
        px = 1 - x if fx else x
        py = 1 - y if fy else y
        pc = 1 - c if fc else c
        peers.append(((px, py, pc), 4 * px + 2 * py + pc))
    return me, peers


_HBM = pl.BlockSpec(memory_space=pltpu.HBM)
_SEM = pl.BlockSpec(memory_space=pltpu.SEMAPHORE)
_ANY = pl.BlockSpec(memory_space=pl.ANY)
_EFFECT = pltpu.SideEffectType.DATAFLOW_SIDE_EFFECTING


def _split_copy(srcs_are_pieces, src_refs, land_refs, send_sem, recv_sem, a, kk, me, peers, arriving):
    dev, lin = peers[kk]
    npeer = len(_FLIPS)
    src = src_refs[a] if srcs_are_pieces[a] else src_refs[a].at[lin]
    dst = land_refs[a].at[lin] if arriving else land_refs[a].at[me]
    return pltpu.make_async_remote_copy(src_ref=src, dst_ref=dst, send_sem=send_sem.at[a * npeer + kk],
                                        recv_sem=recv_sem.at[a * npeer + kk], device_id=dev, device_id_type=MESH_ID)


def _xchg_start(name, gather, scatter, after=None):
    me_out = 4 * lax.axis_index("x") + 2 * lax.axis_index("y") + lax.axis_index("c")
    srcs = list(gather) + list(scatter)
    is_piece = [True] * len(gather) + [False] * len(scatter)
    lands = []
    for a, piece in zip(srcs, is_piece):
        if piece:
            lands.append(lax.dynamic_update_slice(lax.empty((N_DEV,) + tuple(a.shape), a.dtype), a[None],
                                                  (me_out,) + (0,) * a.ndim))
        else:
            lands.append(lax.empty(tuple(a.shape), a.dtype))
    n = len(srcs)
    nsem = n * len(_FLIPS)
    has_after = after is not None

    def body(*refs):
        src_refs = refs[:n]
        land_refs = refs[n:2 * n]
        outs = refs[2 * n + (1 if has_after else 0):]
        send_sem, recv_sem = outs[0], outs[1]
        token = outs[-1]
        me, peers = _peer_table()
        for kk in range(len(_FLIPS)):
            for a in range(n):
                _split_copy(is_piece, src_refs, land_refs, send_sem, recv_sem, a, kk, me, peers, False).start()
        token[...] = jnp.zeros_like(token)

    out_shape = ([pltpu.SemaphoreType.DMA((nsem,)), pltpu.SemaphoreType.DMA((nsem,))]
                 + [pltpu.HBM(tuple(a.shape), a.dtype) for a in srcs] + [pltpu.HBM(tuple(a.shape), a.dtype) for a in lands]
                 + [jax.ShapeDtypeStruct((8, 128), F32)])
    args = [pltpu.with_memory_space_constraint(a, pltpu.HBM) for a in srcs + lands] + ([after] if has_after else [])
    res = pl.pallas_call(
        body, name=name, out_shape=out_shape,
        in_specs=[_HBM] * (2 * n) + ([_ANY] if has_after else []),
        out_specs=[_SEM, _SEM] + [_HBM] * (2 * n) + [pl.BlockSpec(memory_space=pltpu.VMEM)],
        input_output_aliases={i: 2 + i for i in range(2 * n)},
        compiler_params=pltpu.CompilerParams(has_side_effects=_EFFECT),
    )(*args)
    state = (res[0], res[1], list(res[2:2 + n]), list(res[2 + n:2 + 2 * n]), is_piece)
    return state, res[-1]


def _xchg_wait(name, state, after):
    send_sem, recv_sem, srcs, lands, is_piece = state
    n = len(srcs)

    def body(*refs):
        src_refs = refs[:n]
        land_refs = refs[n:2 * n]
        s_sem, r_sem = refs[2 * n], refs[2 * n + 1]
        me, peers = _peer_table()
        for kk in range(len(_FLIPS)):
            for a in range(n):
                cp = _split_copy(is_piece, src_refs, land_refs, s_sem, r_sem, a, kk, me, peers, True)
                cp.wait_send()
                cp.wait_recv()

    out_shape = [pltpu.HBM(tuple(a.shape), a.dtype) for a in srcs] + [pltpu.HBM(tuple(a.shape), a.dtype) for a in lands]
    res = pl.pallas_call(
        body, name=name, out_shape=out_shape,
        in_specs=[_HBM] * (2 * n) + [_SEM, _SEM, _ANY], out_specs=[_HBM] * (2 * n),
        input_output_aliases={i: i for i in range(2 * n)},
        compiler_params=pltpu.CompilerParams(has_side_effects=_EFFECT),
    )(*srcs, *lands, send_sem, recv_sem, after)
    return list(res[n:]), list(res[:n])


_SIB = (0, 0, 1)
_ICI = ((0, 1, 0), (1, 0, 0), (1, 1, 0))


def _flip(fl):
    x, y, c = lax.axis_index("x"), lax.axis_index("y"), lax.axis_index("c")
    px = 1 - x if fl[0] else x
    py = 1 - y if fl[1] else y
    pc = 1 - c if fl[2] else c
    return (px, py, pc), 4 * px + 2 * py + pc


def _gather2_start(name, pieces, after=None):
    me_out = 4 * lax.axis_index("x") + 2 * lax.axis_index("y") + lax.axis_index("c")
    pieces = list(pieces)
    n = len(pieces)
    lands = [lax.dynamic_update_slice(lax.empty((N_DEV,) + tuple(a.shape), a.dtype), a[None],
                                      (me_out,) + (0,) * a.ndim) for a in pieces]
    first = (_SIB,) + _ICI
    has_after = after is not None

    def body(*refs):
        src_refs, land_refs = refs[:n], refs[n:2 * n]
        outs = refs[2 * n + (1 if has_after else 0):]
        send_sem, recv_sem, token = outs[0], outs[1], outs[-1]
        _, me = _flip((0, 0, 0))
        for kk, fl in enumerate(first):
            dev, _ = _flip(fl)
            for a in range(n):
                pltpu.make_async_remote_copy(src_ref=src_refs[a], dst_ref=land_refs[a].at[me],
                                             send_sem=send_sem.at[a * 4 + kk], recv_sem=recv_sem.at[a * 4 + kk],
                                             device_id=dev, device_id_type=MESH_ID).start()
        token[...] = jnp.zeros_like(token)

    hbm = [pltpu.HBM(tuple(a.shape), a.dtype) for a in pieces + lands]
    res = pl.pallas_call(
        body, name=name,
        out_shape=[pltpu.SemaphoreType.DMA((4 * n,)), pltpu.SemaphoreType.DMA((4 * n,))] + hbm
        + [jax.ShapeDtypeStruct((8, 128), F32)],
        in_specs=[_HBM] * (2 * n) + ([_ANY] if has_after else []),
        out_specs=[_SEM, _SEM] + [_HBM] * (2 * n) + [pl.BlockSpec(memory_space=pltpu.VMEM)],
        input_output_aliases={i: 2 + i for i in range(2 * n)},
        compiler_params=pltpu.CompilerParams(has_side_effects=_EFFECT),
    )(*([pltpu.with_memory_space_constraint(a, pltpu.HBM) for a in pieces + lands] + ([after] if has_after else [])))
    return (res[0], res[1], list(res[2:2 + n]), list(res[2 + n:2 + 2 * n])), res[-1]


def _gather2_forward(name, state, after):
    send_a, recv_a, pieces, lands = state
    n = len(pieces)
    first = (_SIB,) + _ICI

    def body(*refs):
        src_refs, land_refs = refs[:n], refs[n:2 * n]
        s_a, r_a = refs[2 * n], refs[2 * n + 1]
        outs = refs[2 * n + 3:]
        send_b, recv_b, token = outs[0], outs[1], outs[-1]
        for kk, fl in enumerate(first):
            dev, lin = _flip(fl)
            for a in range(n):
                cp = pltpu.make_async_remote_copy(src_ref=src_refs[a], dst_ref=land_refs[a].at[lin],
                                                  send_sem=s_a.at[a * 4 + kk], recv_sem=r_a.at[a * 4 + kk],
                                                  device_id=dev, device_id_type=MESH_ID)
                cp.wait_send()
                cp.wait_recv()
        sib, _ = _flip(_SIB)
        for j, fl in enumerate(_ICI):
            _, lin = _flip(fl)
            for a in range(n):
                pltpu.make_async_remote_copy(src_ref=land_refs[a].at[lin], dst_ref=land_refs[a].at[lin],
                                             send_sem=send_b.at[a * 3 + j], recv_sem=recv_b.at[a * 3 + j],
                                             device_id=sib, device_id_type=MESH_ID).start()
        token[...] = jnp.zeros_like(token)

    hbm = [pltpu.HBM(tuple(a.shape), a.dtype) for a in pieces + lands]
    res = pl.pallas_call(
        body, name=name,
        out_shape=[pltpu.SemaphoreType.DMA((3 * n,)), pltpu.SemaphoreType.DMA((3 * n,))] + hbm
        + [jax.ShapeDtypeStruct((8, 128), F32)],
        in_specs=[_HBM] * (2 * n) + [_SEM, _SEM, _ANY],
        out_specs=[_SEM, _SEM] + [_HBM] * (2 * n) + [pl.BlockSpec(memory_space=pltpu.VMEM)],
        input_output_aliases={i: 2 + i for i in range(2 * n)},
        compiler_params=pltpu.CompilerParams(has_side_effects=_EFFECT),
    )(*pieces, *lands, send_a, recv_a, after)
    return (res[0], res[1], list(res[2 + n:2 + 2 * n])), res[-1]


def _gather2_wait(name, state, after):
    send_b, recv_b, lands = state
    n = len(lands)

    def body(*refs):
        land_refs = refs[:n]
        s_b, r_b = refs[n], refs[n + 1]
        sib, _ = _flip(_SIB)
        for j, fl in enumerate(_ICI):
            _, sent = _flip(fl)
            _, arriving = _flip((fl[0], fl[1], 1))
            for a in range(n):
                cp = pltpu.make_async_remote_copy(src_ref=land_refs[a].at[sent], dst_ref=land_refs[a].at[arriving],
                                                  send_sem=s_b.at[a * 3 + j], recv_sem=r_b.at[a * 3 + j],
                                                  device_id=sib, device_id_type=MESH_ID)
                cp.wait_send()
                cp.wait_recv()

    res = pl.pallas_call(
        body, name=name, out_shape=[pltpu.HBM(tuple(a.shape), a.dtype) for a in lands],
        in_specs=[_HBM] * n + [_SEM, _SEM, _ANY], out_specs=[_HBM] * n,
        input_output_aliases={i: i for i in range(n)},
        compiler_params=pltpu.CompilerParams(has_side_effects=_EFFECT),
    )(*lands, send_b, recv_b, after)
    return list(res)


def _adam_math(w, g, m, v):
    m = ADAM_B1 * m + (1.0 - ADAM_B1) * g
    v = ADAM_B2 * v + (1.0 - ADAM_B2) * (g * g)
    m_hat = m / (1.0 - ADAM_B1 ** ADAM_STEP)
    v_hat = v / (1.0 - ADAM_B2 ** ADAM_STEP)
    delta = -ADAM_LR * (m_hat / (jnp.sqrt(v_hat) + ADAM_EPS) + ADAM_WD * w)
    return delta, m, v


def _adam(name, w, m, v, parts, transposed=False, dep=None, own=None):
    npart, _, cp = parts.shape
    has_dep = dep is not None
    has_own = own is not None
    if transposed:
        c, r = w.shape
        tr = _pick(r, (256, 128))
        blk = pl.BlockSpec((c, tr), lambda i, *_: (0, i))
    else:
        r, c = w.shape
        tr = _pick(r, (256, 176, 128, 64, 16, 8, 1))
        blk = pl.BlockSpec((tr, c), lambda i, *_: (i, 0))

    def body(*refs):
        refs = list(refs)
        me = refs.pop(0)[0] if has_own else None
        w_ref, m_ref, v_ref, p_ref = refs[:4]
        own_ref = refs[4] if has_own else None
        g_ref, d_ref, mo_ref, vo_ref = refs[-4:]
        g = None
        for pp in range(npart):
            part = p_ref[pp] if not has_own else jnp.where(me == pp, own_ref[...], p_ref[pp])
            g = part.astype(F32) if g is None else g + part.astype(F32)
        g = g.T[0:c, :] if transposed else g[:, 0:c]
        delta, mn, vn = _adam_math(w_ref[...], g, m_ref[...], v_ref[...])
        g_ref[...] = g
        d_ref[...] = delta
        mo_ref[...] = mn
        vo_ref[...] = vn

    in_specs = [blk, blk, blk, pl.BlockSpec((npart, tr, cp), lambda i, *_: (0, i, 0))]
    args = [w, m, v, parts]
    if has_own:
        in_specs.append(pl.BlockSpec((None, tr, cp), lambda i, me_ref: (me_ref[0], i, 0)))
        args.append(own[0])
    if has_dep:
        in_specs.append(pl.BlockSpec(memory_space=pl.ANY))
        args.append(dep)
    out = jax.ShapeDtypeStruct(w.shape, F32)
    grid_spec = pltpu.PrefetchScalarGridSpec(num_scalar_prefetch=1 if has_own else 0, grid=(r // tr,),
                                             in_specs=in_specs, out_specs=[blk, blk, blk, blk])
    return pl.pallas_call(
        body, name=name, grid_spec=grid_spec, out_shape=[out, out, out, out], compiler_params=_params(("parallel",)),
    )(*(([own[1]] if has_own else []) + args))


def _small_update(name, packs, wmv):
    nparam = len(wmv)

    def body(p_ref, *refs):
        ins, outs = refs[:3 * nparam], refs[3 * nparam:]
        tot = p_ref[0]
        for pp in range(1, N_DEV):
            tot = tot + p_ref[pp]
        outs[0][0:8, :] = tot[0:8, :]
        outs[0][8:24, :] = tot[8:24, :] + tot[24:40, :]
        grads = [tot[i:i + 1, :] for i in range(4)] + [tot[4:5, 0:B_HEADS], tot[4:5, B_HEADS:2 * B_HEADS]]
        for i, g in enumerate(grads):
            w_ref, m_ref, v_ref = ins[3 * i:3 * i + 3]
            delta, mn, vn = _adam_math(w_ref[...], g, m_ref[...], v_ref[...])
            for o_ref, val in zip(outs[1 + 4 * i:5 + 4 * i], (g, delta, mn, vn)):
                o_ref[...] = val

    flat = [a for trio in wmv for a in trio]
    out_shape = [jax.ShapeDtypeStruct((24, D_MODEL), F32)]
    for w, _, _ in wmv:
        out_shape += [jax.ShapeDtypeStruct(w.shape, F32)] * 4
    res = pl.pallas_call(body, name=name, out_shape=out_shape, compiler_params=_params())(packs, *flat)
    return res[0], [tuple(res[1 + 4 * i:5 + 4 * i]) for i in range(nparam)]


def _local_step(x, tgt, g1, gm, g2, gf, b_forget, sinks, weights, send):
    b, s, _ = x.shape
    l = s + PREFIX
    t = b * l
    (meta,) = weights("meta", x)

    h0, n1 = _embed_norm("embed_rms1_fwd", x, meta, g1)
    (w1i,) = weights("ffn1_in", n1)
    gu1, a1 = _ffn_in_fwd("ffn1_in_fwd", n1, w1i)
    (w1o,) = weights("ffn1_out", weights("ffn1_out:forward", a1))
    h1, um = _mm_res_norm("ffn1_out_fwd", a1, w1o, h0, gm, alpha=0.5)
    wi, wa, wb, wo = weights("mix", weights("mix:forward", um))
    qkv, gates, f2 = _proj_fwd("proj_fwd", um, wi)
    qkv3 = qkv.reshape(b, l, QKV_W)
    f3 = f2.reshape(b, l, 128)
    bf_row = jnp.pad(b_forget, ((0, 0), (0, 128 - B_HEADS)))
    c_col, c_row = _fgate_fwd("fgate_fwd", f3, bf_row)
    head_of_row = jnp.arange(STACK) // BLOCK
    slopes = jnp.exp2(-8.0 * (head_of_row + 1).astype(F32) / A_HEADS).reshape(STACK, 1)
    sink_rows = jnp.repeat(sinks.reshape(A_HEADS), BLOCK).reshape(STACK, 1)
    swa_bias = _swa_bias(slopes)
    oa3, lse_a = _swa_fwd("swa_fwd", qkv3, slopes, sink_rows, swa_bias)
    ob3, lse_b = _fox_fwd("fox_fwd", qkv3, c_col, c_row)
    oa = oa3.reshape(t, A_WIDTH)
    ob = ob3.reshape(t, B_WIDTH)
    mixed, ya, yb = _branch_gate_fwd("branch_gate_fwd", oa, ob, wa, wb, gates, dep=weights("ffn2:forward", ob))
    h2, n2 = _mm_res_norm("mix_out_fwd", mixed, wo, h1, g2)
    w2i, w2o = weights("ffn2", h2)
    gu2, a2 = _ffn_in_fwd("ffn2_in_fwd", n2, w2i)

    dh3, dh3_bf, loss_blk, dgf = _ffn_out_loss("ffn2_out_loss", a2, w2o, h2, gf, tgt, 0.5)

    def ffn_bwd(tag, dh, dh_bf, h_in, g_norm, n_in, gu, a, w_in_blk, w_out, one_send, examples=None):
        dw_out = _ffn_out_bwd_w(tag + "_out_bwd_w", a, dh_bf)
        dgu = _ffn_out_bwd_x(tag + "_out_bwd_x", dh_bf, w_out, gu, dep=None if one_send else send(tag + "_out", (dw_out,)))
        dw_in = _ffn_in_bwd_w(tag + "_in_bwd_w", n_in, dgu)
        token = send(tag, (dw_in, dw_out)) if one_send else send(tag + "_in", (dw_in,))
        return _ffn_in_bwd_x(tag + "_in_bwd_x", dgu, w_in_blk, h_in, g_norm, dh, dep=token, examples=examples)

    dh2, dh2_bf, dg2 = ffn_bwd("ffn2", dh3, dh3_bf, h2, g2, n2, gu2, a2, w2i, w2o, True)

    dwo = _mm_tn("mix_out_bwd_w", mixed, dh2_bf)
    dya, dyb, dgates = _mix_out_gate_bwd("mix_out_gate_bwd", dh2_bf, wo, gates, ya, yb)
    doa, dob = _branch_bwd_x("branch_bwd_x", dya, dyb, wa, wb)
    dwa, dwb = _branch_bwd_w("branch_bwd_w", oa, ob, dya, dyb)
    dqkv3, dcq, dck = _fox_bwd("fox_bwd", qkv3, c_col, c_row, ob3, lse_b, dob.reshape(b, l, B_WIDTH))
    dqkv3, dsink = _swa_bwd("swa_bwd", qkv3, oa3, lse_a, doa.reshape(b, l, A_WIDTH), slopes, sink_rows, swa_bias, dqkv3)
    dqkv = dqkv3.reshape(t, QKV_W)
    df3, dbf = _fgate_bwd("fgate_bwd", f3, bf_row, dcq, dck)
    df = df3.reshape(t, 128)
    dwi_qkv = _mm_tn("proj_qkv_bwd_w", um, dqkv, tm_c=(512,), tn_c=(768,))
    dwi_g = _mm_tn("proj_gates_bwd_w", um, dgates, tm_c=(512,), tn_c=(512,))
    dwi_f = _mm_tn("proj_f_bwd_w", um, df, tm_c=(512,), tn_c=(128,))
    token = send("mix", (dwi_qkv, dwi_g, dwi_f, dwa, dwb, dwo))
    dh1, dh1_bf, dgm = _proj_bwd_x("proj_bwd_x", dqkv, dgates, df, wi, h1, gm, dh2, dep=token)

    grad_x, dmeta3, dg1 = ffn_bwd("ffn1", dh1, dh1_bf, h0, g1, n1, gu1, a1, w1i, w1o, False, examples=b)
    dmeta = dmeta3.reshape(b * N_META, D_MODEL)

    misc = jnp.concatenate([dbf[:, 0:B_HEADS], dsink[:, 0].reshape(1, A_HEADS), loss_blk[0:1, 0:1]], axis=1)
    misc = jnp.pad(misc, ((0, 0), (0, D_MODEL - misc.shape[1])))
    row = lax.broadcasted_iota(jnp.int32, (8, D_MODEL), 0)
    vec = jnp.zeros((8, D_MODEL), F32)
    for i, piece in enumerate((dg1, dgm, dg2, dgf, misc)):
        vec = jnp.where(row == i, piece, vec)
    small = jnp.concatenate([vec, dmeta], axis=0)
    return grad_x, small


def _pad_to(a, rows, cols):
    return jnp.pad(a, ((0, rows - a.shape[0]), (0, cols - a.shape[1])))


def _ffn_out_from_gathered(name, g):
    def body(g_ref, o_ref):
        o_ref[0:FFO_SHARD, :] = g_ref[0]
        o_ref[FFO_SHARD:FF_SHARD, :] = g_ref[1]
        o_ref[FF_SHARD:FF_SHARD_P, :] = jnp.zeros((FF_SHARD_P - FF_SHARD, D_MODEL), BF16)

    return pl.pallas_call(
        body, name=name, grid=(4,),
        in_specs=[pl.BlockSpec((2, FFO_SHARD, D_MODEL), lambda j: (j, 0, 0))],
        out_specs=pl.BlockSpec((FF_SHARD_P, D_MODEL), lambda j: (j, 0)),
        out_shape=jax.ShapeDtypeStruct((D_FF_P, D_MODEL), BF16), compiler_params=_params(("parallel",)),
    )(g)


def _ffn_out_bwd_w(name, a, dh):
    t = a.shape[0]
    tn = D_MODEL // 2

    def body(a_ref, b_ref, o_ref):
        acc = lax.dot_general(a_ref[...], _bf(b_ref[...]), _TN, preferred_element_type=F32) * 0.5
        o_ref[0] = acc[0:FFO_SHARD].astype(BF16)
        o_ref[1] = acc[FFO_SHARD:FF_SHARD].astype(BF16)

    return pl.pallas_call(
        body, name=name, grid=(2, 4),
        in_specs=[pl.BlockSpec((t, FF_SHARD_P), lambda j, i: (0, i)), pl.BlockSpec((t, tn), lambda j, i: (0, j))],
        out_specs=pl.BlockSpec((2, FFO_SHARD, tn), lambda j, i: (i, 0, j)),
        out_shape=jax.ShapeDtypeStruct((N_DEV, FFO_SHARD, D_MODEL), BF16), compiler_params=_params(("parallel", "parallel")),
    )(a, dh)


def _proj_segments():
    segs = [(HEAD_DIM * h, HEAD_DIM, 0, B_SEG + HEAD_DIM * A_HEAD_ORDER.index(h)) for h in range(A_HEADS)]
    segs += [(512, 128, 0, B_SEG + A_WIDTH), (640, 128, 0, B_SEG + A_WIDTH + 128)]
    for first, off in ((768, 0), (1280, 128), (1792, 256)):
        segs += [(first + 128 * hp, 128, 0, PAIR_W * hp + off) for hp in range(4)]
    segs += [(2304, B_HEADS, 2, 0), (2312, 2 * D_MODEL, 1, 0)]
    return segs


_RELAYOUT_ROWS = 256


def _proj_from_gathered(name, g):
    rows = _RELAYOUT_ROWS

    def body(g_ref, o_ref):
        def cols(first, width):
            out = []
            for p in range(N_DEV):
                lo, hi = max(first, WIN_SHARD * p), min(first + width, WIN_SHARD * (p + 1))
                if lo < hi:
                    out.append(g_ref[p, :, lo - WIN_SHARD * p:hi - WIN_SHARD * p])
            return out

        parts = []
        for arr in (0, 1, 2):
            for first, width, _, _ in sorted((s for s in _proj_segments() if s[2] == arr), key=lambda s: s[3]):
                parts += cols(first, width)
            if arr == 0:
                parts.append(jnp.zeros((rows, P_GATES - QKV_W), BF16))
        parts.append(jnp.zeros((rows, 128 - B_HEADS), BF16))
        o_ref[...] = jnp.concatenate(parts, axis=1)

    return pl.pallas_call(
        body, name=name, grid=(D_MODEL // rows,),
        in_specs=[pl.BlockSpec((N_DEV, rows, WIN_SHARD_P), lambda i: (0, i, 0))],
        out_specs=pl.BlockSpec((rows, PROJ_P), lambda i: (i, 0)),
        out_shape=jax.ShapeDtypeStruct((D_MODEL, PROJ_P), BF16), compiler_params=_params(("parallel",)),
    )(g)


def _proj_to_scatter(name, dqkv_w, dg_w, df_w):
    rows = _RELAYOUT_ROWS
    segs = sorted(_proj_segments())

    def body(q_ref, g_ref, f_ref, o_ref):
        arrays = (q_ref, g_ref, f_ref)
        for p in range(N_DEV):
            parts = []
            for first, width, arr, at in segs:
                lo, hi = max(first, WIN_SHARD * p), min(first + width, WIN_SHARD * (p + 1))
                if lo < hi:
                    parts.append(arrays[arr][:, at + lo - first:at + hi - first])
            parts.append(jnp.zeros((rows, WIN_SHARD_P - WIN_SHARD), BF16))
            o_ref[p] = jnp.concatenate(parts, axis=1)

    return pl.pallas_call(
        body, name=name, grid=(D_MODEL // rows,),
        in_specs=[pl.BlockSpec((rows, QKV_W), lambda i: (i, 0)), pl.BlockSpec((rows, 2 * D_MODEL), lambda i: (i, 0)),
                  pl.BlockSpec((rows, 128), lambda i: (i, 0))],
        out_specs=pl.BlockSpec((N_DEV, rows, WIN_SHARD_P), lambda i: (0, i, 0)),
        out_shape=jax.ShapeDtypeStruct((N_DEV, D_MODEL, WIN_SHARD_P), BF16), compiler_params=_params(("parallel",)),
    )(dqkv_w, dg_w, df_w)


def _a_rows_from_natural(w):
    return jnp.concatenate([w[HEAD_DIM * h:HEAD_DIM * (h + 1)] for h in A_HEAD_ORDER], axis=0)


def _a_rows_to_natural(w):
    return jnp.concatenate([w[HEAD_DIM * A_HEAD_ORDER.index(h):HEAD_DIM * (A_HEAD_ORDER.index(h) + 1)]
                            for h in range(A_HEADS)], axis=0)


def kernel(x, meta_tokens, ffn1_norm, ffn1_w_in, ffn1_w_out, mix_norm, w_in, b_forget, attn_sinks, w_branch_a, w_branch_b, w_out, ffn2_norm, ffn2_w_in, ffn2_w_out, final_norm, loss_target, m_meta_tokens, m_ffn1_norm, m_ffn1_w_in, m_ffn1_w_out, m_mix_norm, m_w_in, m_b_forget, m_attn_sinks, m_w_branch_a, m_w_branch_b, m_w_out, m_ffn2_norm, m_ffn2_w_in, m_ffn2_w_out, m_final_norm, v_meta_tokens, v_ffn1_norm, v_ffn1_w_in, v_ffn1_w_out, v_mix_norm, v_w_in, v_b_forget, v_attn_sinks, v_w_branch_a, v_w_branch_b, v_w_out, v_ffn2_norm, v_ffn2_w_in, v_ffn2_w_out, v_final_norm):
    me = 4 * lax.axis_index("x") + 2 * lax.axis_index("y") + lax.axis_index("c")

    def shard(w, tok, rows=None, cols=None):
        piece = (w[0] if tok is None else w[0] + tok[0, 0]).astype(BF16)
        return piece if rows is None else _pad_to(piece, rows, cols)

    first_level, second_level = {}, {}
    first_level["meta"], tok = _gather2_start("gather_meta_start", (meta_tokens,))
    first_level["ffn1_in"], tok = _gather2_start(
        "gather_ffn1_in_start", (shard(ffn1_w_in, None, D_MODEL, FF_SHARD_P),), after=tok)
    first_level["ffn1_out"], tok = _gather2_start("gather_ffn1_out_start", (shard(ffn1_w_out, tok),), after=tok)
    first_level["mix"], tok = _gather2_start(
        "gather_mix_start", (shard(w_in, tok, D_MODEL, WIN_SHARD_P), shard(w_branch_a, tok), shard(w_branch_b, tok),
                             shard(w_out, tok)), after=tok)
    first_level["ffn2"], tok = _gather2_start(
        "gather_ffn2_start", (shard(ffn2_w_in, tok, D_MODEL, FF_SHARD_P), shard(ffn2_w_out, tok)), after=tok)
    started = {"tok": tok}

    def weights(group, after):
        if group.endswith(":forward"):
            group = group[:-len(":forward")]
            second_level[group], token = _gather2_forward("gather_" + group + "_forward", first_level[group], after)
            return token
        if group == "meta":
            after = weights("meta:forward", started["tok"])
        if group == "ffn1_in":
            after = weights("ffn1_in:forward", after)
        got = _gather2_wait("gather_" + group + "_wait", second_level[group], after)
        if group == "mix":
            gwi, gwa, gwb, gwo = got
            return (_proj_from_gathered("proj_w_relayout", gwi), _a_rows_from_natural(gwa.transpose(1, 0, 2).reshape(A_WIDTH, D_MODEL)),
                    gwb.transpose(1, 0, 2).reshape(B_WIDTH, D_MODEL), gwo.reshape(D_MODEL, D_MODEL))
        if group == "ffn1_out":
            return (_ffn_out_from_gathered("ffn1_w_out_relayout", got[0]),)
        if group == "meta":
            return (got[0].transpose(1, 0, 2).reshape(N_META, D_MODEL),)
        if group == "ffn1_in":
            return (got[0].reshape(2, 4, D_MODEL, FF_SHARD_P),)
        return got[0].reshape(2, 4, D_MODEL, FF_SHARD_P), _ffn_out_from_gathered("ffn2_w_out_relayout", got[1])

    scatter_state = {}

    def send(group, grads):
        if group == "mix":
            dwi_qkv, dwi_g, dwi_f, dwa, dwb, dwo = grads
            dwa = _a_rows_to_natural(dwa)
            blocks = (_proj_to_scatter("proj_dw_relayout", dwi_qkv, dwi_g, dwi_f), dwa.reshape(A_WIDTH, N_DEV, 128).transpose(1, 0, 2),
                      dwb.reshape(B_WIDTH, N_DEV, 128).transpose(1, 0, 2), dwo.reshape(N_DEV, 128, D_MODEL))
        elif group.endswith("_in"):
            blocks = (grads[0].reshape(N_DEV, D_MODEL, FF_SHARD_P),)
        elif group.endswith("_out"):
            blocks = (grads[0],)
        else:
            blocks = (grads[0].reshape(N_DEV, D_MODEL, FF_SHARD_P), grads[1])
        scatter_state[group], token = _xchg_start("scatter_" + group + "_start", (), blocks)
        return token

    gf = final_norm.reshape(1, D_MODEL)
    grad_x, small = _local_step(x, loss_target, ffn1_norm, mix_norm, ffn2_norm, gf, b_forget, attn_sinks, weights, send)

    small_state, after = _xchg_start("gather_small_start", (small,), ())
    out = {}
    updates = (
        ("ffn2", (("ffn2_w_in", ffn2_w_in, m_ffn2_w_in, v_ffn2_w_in), ("ffn2_w_out", ffn2_w_out, m_ffn2_w_out, v_ffn2_w_out))),
        ("ffn1_out", (("ffn1_w_out", ffn1_w_out, m_ffn1_w_out, v_ffn1_w_out),)),
        ("mix", (("w_in", w_in, m_w_in, v_w_in), ("w_branch_a", w_branch_a, m_w_branch_a, v_w_branch_a),
                 ("w_branch_b", w_branch_b, m_w_branch_b, v_w_branch_b), ("w_out", w_out, m_w_out, v_w_out))),
    )
    last_update = ("ffn1_in", (("ffn1_w_in", ffn1_w_in, m_ffn1_w_in, v_ffn1_w_in),))

    def update(group, members, after):
        parts_list, blocks_list = _xchg_wait("scatter_" + group + "_wait", scatter_state[group], after)
        prev = None
        for (nm, w, m, v), parts, blocks in zip(members, parts_list, blocks_list):
            if nm.endswith("w_in"):
                res4 = _adam("adam_" + nm, w[0].T, m[0].T, v[0].T, parts, transposed=True, dep=prev, own=(blocks, me_arr))
                out[nm] = tuple(r.T[None] for r in res4)
            else:
                res4 = _adam("adam_" + nm, w[0], m[0], v[0], parts, dep=prev, own=(blocks, me_arr))
                out[nm] = tuple(r[None] for r in res4)
            prev = res4[0]
        return prev

    me_arr = me.reshape(1).astype(jnp.int32)
    for group, members in updates + (last_update,):
        after = update(group, members, after)
    (packs,), _ = _xchg_wait("gather_small_wait", small_state, after)

    row = lambda a: a.reshape(1, D_MODEL)
    tot, small_res = _small_update("small_update", packs, (
        (ffn1_norm, m_ffn1_norm, v_ffn1_norm), (mix_norm, m_mix_norm, v_mix_norm), (ffn2_norm, m_ffn2_norm, v_ffn2_norm),
        (row(final_norm), row(m_final_norm), row(v_final_norm)), (b_forget, m_b_forget, v_b_forget),
        (attn_sinks, m_attn_sinks, v_attn_sinks)))
    for nm, res4 in zip(("ffn1_norm", "mix_norm", "ffn2_norm", "final_norm", "b_forget", "attn_sinks"), small_res):
        out[nm] = tuple(r.reshape(D_MODEL) for r in res4) if nm == "final_norm" else res4
    loss = tot[4, 2 * B_HEADS]
    g_meta = lax.dynamic_slice(tot[8:24, :], (0, me * 128), (N_META, 128))
    out["meta_tokens"] = tuple(_adam("adam_meta_tokens", meta_tokens, m_meta_tokens, v_meta_tokens, g_meta[None]))

    names = ("meta_tokens", "ffn1_norm", "ffn1_w_in", "ffn1_w_out", "mix_norm", "w_in", "b_forget", "attn_sinks",
             "w_branch_a", "w_branch_b", "w_out", "ffn2_norm", "ffn2_w_in", "ffn2_w_out", "final_norm")
    return (loss, grad_x) + tuple(out[nm][kind] for kind in range(4) for nm in names)
```

```python
import jax
import jax.numpy as jnp
from jax import lax
from jax.experimental import pallas as pl
from jax.experimental.pallas import tpu as pltpu

F32 = jnp.float32
BF16 = jnp.bfloat16

D_MODEL = 1024
N_META = 16
BLOCK = 128
PREFIX = 128
N_PAD = PREFIX - N_META
HEAD_DIM = 64
A_HEADS = 8
B_HEADS = 8
A_WIDTH = 512
A_KV_WIDTH = 128
B_WIDTH = 512
D_FF = 2816
N_DEV = 8
FF_SHARD = 2 * D_FF // N_DEV
FF_SHARD_P = 768
FFO_SHARD = D_FF // N_DEV
D_FF_P = 4 * FF_SHARD_P
W_IN_COLS = 4360
WIN_SHARD = W_IN_COLS // N_DEV
WIN_SHARD_P = 640
PAIR_W = 3 * 128
B_SEG = 4 * PAIR_W
A_SEG = A_WIDTH + 2 * A_KV_WIDTH
QKV_W = B_SEG + A_SEG
P_GATES = 2 * (2 * D_MODEL)
P_F = P_GATES + 2 * D_MODEL
PROJ_P = P_F + 128
A_HEAD_ORDER = (0, 4, 1, 5, 2, 6, 3, 7)
EPS = 1e-6
NEG = -1e30
SCALE = HEAD_DIM ** -0.5
ADAM_LR = 0.001
ADAM_B1 = 0.9
ADAM_B2 = 0.999
ADAM_EPS = 1e-08
ADAM_WD = 0.01
ADAM_STEP = 10
VMEM_LIMIT = 56 * 1024 * 1024
MESH_ID = pl.DeviceIdType.MESH
SMALL_ROWS = 40

_NN = (((1,), (0,)), ((), ()))
_NT = (((1,), (1,)), ((), ()))
_TN = (((0,), (0,)), ((), ()))


def _params(sem=None):
    return pltpu.CompilerParams(dimension_semantics=sem, vmem_limit_bytes=VMEM_LIMIT)


def _pick(n, cands):
    for c in cands:
        if n % c == 0:
            return c
    raise ValueError(f"no tile for {n}")


def _bf(v):
    return v if v.dtype == BF16 else v.astype(BF16)


def _mm(name, a, b, dims, grid, a_spec, b_spec, o_spec, out_shape, out_dtype, alpha=1.0):
    def body(a_ref, b_ref, o_ref):
        acc = lax.dot_general(_bf(a_ref[...]), _bf(b_ref[...]), dims, preferred_element_type=F32)
        if alpha != 1.0:
            acc = acc * alpha
        o_ref[...] = acc.astype(o_ref.dtype)

    return pl.pallas_call(
        body, name=name, grid=grid, in_specs=[a_spec, b_spec], out_specs=o_spec,
        out_shape=jax.ShapeDtypeStruct(out_shape, out_dtype),
        compiler_params=_params(("parallel",) * len(grid)),
    )(a, b)


def _mm_res_norm(name, a, w, res, g_next, alpha=1.0):
    t, k = a.shape
    tm = _pick(t, (544, 384, 256, 128))

    def body(a_ref, w_ref, r_ref, g_ref, h_ref, n_ref):
        acc = lax.dot_general(_bf(a_ref[...]), w_ref[...], _NN, preferred_element_type=F32)
        if alpha != 1.0:
            acc = acc * alpha
        hv = acc + r_ref[...]
        h_ref[...] = hv
        r = lax.rsqrt(jnp.mean(hv * hv, axis=-1, keepdims=True) + EPS)
        n_ref[...] = ((hv * r) * g_ref[...]).astype(BF16)

    row = pl.BlockSpec((tm, D_MODEL), lambda i: (i, 0))
    return pl.pallas_call(
        body, name=name, grid=(t // tm,),
        in_specs=[pl.BlockSpec((tm, k), lambda i: (i, 0)), pl.BlockSpec((k, D_MODEL), lambda i: (0, 0)), row,
                  pl.BlockSpec((1, D_MODEL), lambda i: (0, 0))],
        out_specs=[row, row],
        out_shape=[jax.ShapeDtypeStruct((t, D_MODEL), F32), jax.ShapeDtypeStruct((t, D_MODEL), BF16)],
        compiler_params=_params(("parallel",)),
    )(a, w, res, g_next)


def _mm_tn(name, a, b, out_dtype=BF16, alpha=1.0, tm_c=(768, 512, 256, 128), tn_c=(512, 640, 256, 128)):
    t, m = a.shape
    n = b.shape[1]
    tm = _pick(m, tm_c)
    tn = _pick(n, tn_c)
    bytes_a, bytes_b = a.size * a.dtype.itemsize, b.size * b.dtype.itemsize
    if bytes_a + bytes_b * (m // tm) <= bytes_b + bytes_a * (n // tn):
        return _mm(name, a, b, _TN, (m // tm, n // tn),
                   pl.BlockSpec((t, tm), lambda i, j: (0, i)), pl.BlockSpec((t, tn), lambda i, j: (0, j)),
                   pl.BlockSpec((tm, tn), lambda i, j: (i, j)), (m, n), out_dtype, alpha=alpha)
    return _mm(name, a, b, _TN, (n // tn, m // tm),
               pl.BlockSpec((t, tm), lambda j, i: (0, i)), pl.BlockSpec((t, tn), lambda j, i: (0, j)),
               pl.BlockSpec((tm, tn), lambda j, i: (i, j)), (m, n), out_dtype, alpha=alpha)


def _ffn_in_fwd(name, n, wblk, dep=None):
    t = n.shape[0]
    tm = _pick(t, (1088, 768, 512, 256, 128))
    has_dep = dep is not None

    def body(n_ref, w_ref, *rest):
        gu_ref, a_ref = rest[-2], rest[-1]
        nv = n_ref[...]
        g = lax.dot_general(nv, w_ref[0], _NN, preferred_element_type=F32)
        u = lax.dot_general(nv, w_ref[1], _NN, preferred_element_type=F32)
        sg = jax.nn.sigmoid(g)
        silu = g * sg
        a_ref[...] = (silu * u).astype(BF16)
        gu_ref[0] = ((0.5 * u) * (sg + silu * (1.0 - sg))).astype(BF16)
        gu_ref[1] = (0.5 * silu).astype(BF16)

    return pl.pallas_call(
        body, name=name, grid=(t // tm, 4),
        in_specs=[pl.BlockSpec((tm, D_MODEL), lambda i, j: (i, 0)),
                  pl.BlockSpec((2, None, D_MODEL, FF_SHARD_P), lambda i, j: (0, j, 0, 0))]
        + ([pl.BlockSpec(memory_space=pl.ANY)] if has_dep else []),
        out_specs=[pl.BlockSpec((2, tm, FF_SHARD_P), lambda i, j: (0, i, j)),
                   pl.BlockSpec((tm, FF_SHARD_P), lambda i, j: (i, j))],
        out_shape=[jax.ShapeDtypeStruct((2, t, D_FF_P), BF16), jax.ShapeDtypeStruct((t, D_FF_P), BF16)],
        compiler_params=_params(("parallel", "parallel")),
    )(*((n, wblk) + ((dep,) if has_dep else ())))


def _ffn_out_bwd_x(name, dh, w_out, gu, dep=None):
    t = dh.shape[0]
    tm = _pick(t, (1088, 768, 512, 256, 128))
    has_dep = dep is not None

    def body(dh_ref, w_ref, gu_ref, *rest):
        o_ref = rest[-1]
        w = w_ref[pl.ds(pl.multiple_of(pl.program_id(1) * FF_SHARD_P, FF_SHARD_P), FF_SHARD_P), :]
        da = lax.dot_general(_bf(dh_ref[...]), w, _NT, preferred_element_type=F32)
        o_ref[0] = (da * gu_ref[0].astype(F32)).astype(BF16)
        o_ref[1] = (da * gu_ref[1].astype(F32)).astype(BF16)

    gu_spec = pl.BlockSpec((2, tm, FF_SHARD_P), lambda i, j: (0, i, j))
    return pl.pallas_call(
        body, name=name, grid=(t // tm, 4),
        in_specs=[pl.BlockSpec((tm, D_MODEL), lambda i, j: (i, 0)),
                  pl.BlockSpec((D_FF_P, D_MODEL), lambda i, j: (0, 0), pipeline_mode=pl.Buffered(1)),
                  gu_spec] + ([pl.BlockSpec(memory_space=pl.ANY)] if has_dep else []),
        out_specs=gu_spec, out_shape=jax.ShapeDtypeStruct((2, t, D_FF_P), BF16),
        compiler_params=_params(("parallel", "parallel")),
    )(*((dh, w_out, gu) + ((dep,) if has_dep else ())))


def _rms_bwd_rows(dn, h, g, dres):
    r = lax.rsqrt(jnp.mean(h * h, axis=-1, keepdims=True) + EPS)
    tv = dn * g
    dot = jnp.mean(tv * h, axis=-1, keepdims=True)
    return dres + (r * tv - h * (r * r * r * dot)), jnp.sum(dn * (h * r), axis=0, keepdims=True)


def _accumulate_rows(ref, part, first):
    @pl.when(first)
    def _():
        ref[...] = part

    @pl.when(jnp.logical_not(first))
    def _():
        ref[...] += part


def _ffn_in_bwd_x(name, dgu, wblk, h_in, g_norm, dres, dep=None, examples=None):
    t = dgu.shape[1]
    tm = _pick(t, (544, 384, 256, 128))
    has_dep = dep is not None
    split = examples is not None
    if split:
        l = t // examples
        per = l // tm
        assert per * tm == l and tm > PREFIX

    def body(d_ref, w_ref, h_ref, g_ref, r_ref, *rest):
        acc = None
        for s in range(2):
            for j in range(4):
                part = lax.dot_general(d_ref[s, :, FF_SHARD_P * j:FF_SHARD_P * (j + 1)], w_ref[s, j], _NT,
                                       preferred_element_type=F32)
                acc = part if acc is None else acc + part
        dh, dg = _rms_bwd_rows(acc, h_ref[...], g_ref[...], r_ref[...])
        i = pl.program_id(0)
        if not split:
            dh_ref, dhb_ref, dg_ref = rest[-3:]
            dh_ref[...] = dh
            dhb_ref[...] = dh.astype(BF16)
        else:
            gx_ref, meta_ref, dg_ref, buf, sem = rest[-5:]

            def out_copy(step):
                bi, r = step // per, step % per
                head = pltpu.make_async_copy(buf.at[pl.ds(PREFIX, tm - PREFIX)], gx_ref.at[bi, pl.ds(0, tm - PREFIX)], sem)
                if per == 1:
                    return r == 0, head, None
                return r == 0, head, pltpu.make_async_copy(
                    buf, gx_ref.at[bi, pl.ds(pl.multiple_of(jnp.maximum(r, 1) * tm - PREFIX, 8), tm)], sem)

            def run(step, method):
                is_head, head, later = out_copy(step)

                @pl.when(is_head)
                def _():
                    getattr(head, method)()

                if later is not None:
                    @pl.when(jnp.logical_not(is_head))
                    def _():
                        getattr(later, method)()

            @pl.when(i > 0)
            def _():
                run(i - 1, "wait")

            buf[...] = dh

            @pl.when(i % per == 0)
            def _():
                meta_ref[...] = dh[N_PAD:PREFIX]

            run(i, "start")

            @pl.when(i == pl.num_programs(0) - 1)
            def _():
                run(i, "wait")

        _accumulate_rows(dg_ref, dg, i == 0)

    row = pl.BlockSpec((tm, D_MODEL), lambda i: (i, 0))
    vec = pl.BlockSpec((1, D_MODEL), lambda i: (0, 0))
    if split:
        out_specs = [pl.BlockSpec(memory_space=pl.ANY), pl.BlockSpec((None, N_META, D_MODEL), lambda i: (i // per, 0, 0)), vec]
        out_shape = [jax.ShapeDtypeStruct((examples, l - PREFIX, D_MODEL), F32),
                     jax.ShapeDtypeStruct((examples, N_META, D_MODEL), F32), jax.ShapeDtypeStruct((1, D_MODEL), F32)]
        scratch = [pltpu.VMEM((tm, D_MODEL), F32), pltpu.SemaphoreType.DMA(())]
    else:
        out_specs = [row, row, vec]
        out_shape = [jax.ShapeDtypeStruct((t, D_MODEL), F32), jax.ShapeDtypeStruct((t, D_MODEL), BF16),
                     jax.ShapeDtypeStruct((1, D_MODEL), F32)]
        scratch = []
    return pl.pallas_call(
        body, name=name, grid=(t // tm,),
        in_specs=[pl.BlockSpec((2, tm, D_FF_P), lambda i: (0, i, 0)),
                  pl.BlockSpec((2, 4, D_MODEL, FF_SHARD_P), lambda i: (0, 0, 0, 0), pipeline_mode=pl.Buffered(1)),
                  row, vec, row]
        + ([pl.BlockSpec(memory_space=pl.ANY)] if has_dep else []),
        out_specs=out_specs, out_shape=out_shape, scratch_shapes=scratch,
        compiler_params=_params(("arbitrary",)),
    )(*((dgu, wblk, h_in, g_norm, dres) + ((dep,) if has_dep else ())))


def _proj_fwd(name, um, wi):
    t = um.shape[0]
    tm = _pick(t, (544, 384, 256, 128))

    def body(u_ref, w_ref, q_ref, g_ref, f_ref):
        uv = u_ref[...]
        q_ref[...] = lax.dot_general(uv, w_ref[:, 0:QKV_W], _NN, preferred_element_type=F32).astype(BF16)
        g_ref[...] = lax.dot_general(uv, w_ref[:, P_GATES:P_F], _NN, preferred_element_type=F32).astype(BF16)
        f_ref[...] = lax.dot_general(uv, w_ref[:, P_F:PROJ_P], _NN, preferred_element_type=F32)

    return pl.pallas_call(
        body, name=name, grid=(t // tm,),
        in_specs=[pl.BlockSpec((tm, D_MODEL), lambda i: (i, 0)),
                  pl.BlockSpec((D_MODEL, PROJ_P), lambda i: (0, 0), pipeline_mode=pl.Buffered(1))],
        out_specs=[pl.BlockSpec((tm, QKV_W), lambda i: (i, 0)), pl.BlockSpec((tm, 2 * D_MODEL), lambda i: (i, 0)),
                   pl.BlockSpec((tm, 128), lambda i: (i, 0))],
        out_shape=[jax.ShapeDtypeStruct((t, QKV_W), BF16), jax.ShapeDtypeStruct((t, 2 * D_MODEL), BF16),
                   jax.ShapeDtypeStruct((t, 128), F32)],
        compiler_params=_params(("parallel",)),
    )(um, wi)


def _proj_bwd_x(name, dqkv, dgates, df, wi, h_in, g_norm, dres, dep=None):
    t = dqkv.shape[0]
    tm = _pick(t, (544, 384, 256, 128))
    has_dep = dep is not None

    def body(q_ref, gt_ref, f_ref, w_ref, h_ref, g_ref, r_ref, *rest):
        dh_ref, dhb_ref, dg_ref = rest[-3:]
        acc = lax.dot_general(q_ref[...], w_ref[:, 0:QKV_W], _NT, preferred_element_type=F32)
        acc = acc + lax.dot_general(gt_ref[...], w_ref[:, P_GATES:P_F], _NT, preferred_element_type=F32)
        acc = acc + lax.dot_general(f_ref[...], w_ref[:, P_F:PROJ_P], _NT, preferred_element_type=F32)
        dh, dg = _rms_bwd_rows(acc, h_ref[...], g_ref[...], r_ref[...])
        dh_ref[...] = dh
        dhb_ref[...] = dh.astype(BF16)
        _accumulate_rows(dg_ref, dg, pl.program_id(0) == 0)

    row = pl.BlockSpec((tm, D_MODEL), lambda i: (i, 0))
    vec = pl.BlockSpec((1, D_MODEL), lambda i: (0, 0))
    return pl.pallas_call(
        body, name=name, grid=(t // tm,),
        in_specs=[pl.BlockSpec((tm, QKV_W), lambda i: (i, 0)), pl.BlockSpec((tm, 2 * D_MODEL), lambda i: (i, 0)),
                  pl.BlockSpec((tm, 128), lambda i: (i, 0)),
                  pl.BlockSpec((D_MODEL, PROJ_P), lambda i: (0, 0), pipeline_mode=pl.Buffered(1)), row, vec, row]
        + ([pl.BlockSpec(memory_space=pl.ANY)] if has_dep else []),
        out_specs=[row, row, vec],
        out_shape=[jax.ShapeDtypeStruct((t, D_MODEL), F32), jax.ShapeDtypeStruct((t, D_MODEL), BF16),
                   jax.ShapeDtypeStruct((1, D_MODEL), F32)],
        compiler_params=_params(("arbitrary",)),
    )(*((dqkv, dgates, df, wi, h_in, g_norm, dres) + ((dep,) if has_dep else ())))


def _ffn_in_bwd_w(name, n, dgu):
    t = n.shape[0]
    return _mm(name, n, dgu, _TN, (2, 4),
               pl.BlockSpec((t, D_MODEL), lambda s, j: (0, 0)),
               pl.BlockSpec((None, t, FF_SHARD_P), lambda s, j: (s, 0, j)),
               pl.BlockSpec((None, None, D_MODEL, FF_SHARD_P), lambda s, j: (s, j, 0, 0)),
               (2, 4, D_MODEL, FF_SHARD_P), BF16)


def _embed_norm(name, x, meta, g):
    b, s, _ = x.shape
    half = (s + PREFIX) // 2
    first = half - PREFIX
    assert first > 0 and half % 16 == 0

    def body(x_ref, m_ref, g_ref, h_ref, n_ref, buf, sem):
        bi, k = pl.program_id(0), pl.program_id(1)

        def tokens(example, second, method):
            if second:
                cp = pltpu.make_async_copy(x_ref.at[example, pl.ds(first, half)], buf.at[1], sem.at[1])
            else:
                cp = pltpu.make_async_copy(x_ref.at[example, pl.ds(0, first)], buf.at[0, pl.ds(PREFIX, first)], sem.at[0])
            getattr(cp, method)()

        @pl.when((bi == 0) & (k == 0))
        def _():
            tokens(0, False, "start")

        @pl.when(k == 0)
        def _():
            tokens(bi, True, "start")
            tokens(bi, False, "wait")
            buf[0, 0:N_PAD, :] = jnp.zeros((N_PAD, D_MODEL), F32)
            buf[0, N_PAD:PREFIX, :] = m_ref[...]

        @pl.when(k == 1)
        def _():
            @pl.when(bi + 1 < b)
            def _():
                tokens(bi + 1, False, "start")

            tokens(bi, True, "wait")

        hv = buf[k]
        h_ref[...] = hv
        r = lax.rsqrt(jnp.mean(hv * hv, axis=-1, keepdims=True) + EPS)
        n_ref[...] = ((hv * r) * g_ref[...]).astype(BF16)

    rows = pl.BlockSpec((half, D_MODEL), lambda bi, k: (2 * bi + k, 0))
    return pl.pallas_call(
        body, name=name, grid=(b, 2),
        in_specs=[pl.BlockSpec(memory_space=pl.ANY), pl.BlockSpec((N_META, D_MODEL), lambda bi, k: (0, 0)),
                  pl.BlockSpec((1, D_MODEL), lambda bi, k: (0, 0))],
        out_specs=[rows, rows],
        out_shape=[jax.ShapeDtypeStruct((2 * b * half, D_MODEL), F32), jax.ShapeDtypeStruct((2 * b * half, D_MODEL), BF16)],
        scratch_shapes=[pltpu.VMEM((2, half, D_MODEL), F32), pltpu.SemaphoreType.DMA((2,))],
        compiler_params=_params(("arbitrary", "arbitrary")),
    )(x, meta, g)


def _branch_gate_fwd(name, oa, ob, wa, wb, gates, dep=None):
    t = gates.shape[0]
    tm = _pick(t, (544, 384, 256, 128))
    has_dep = dep is not None

    def body(oa_ref, ob_ref, wa_ref, wb_ref, g_ref, *rest):
        o_ref, ya_ref, yb_ref = rest[-3:]
        ya = lax.dot_general(_bf(oa_ref[...]), wa_ref[...], _NN, preferred_element_type=F32)
        yb = lax.dot_general(_bf(ob_ref[...]), wb_ref[...], _NN, preferred_element_type=F32)
        sa = jax.nn.sigmoid(g_ref[:, 0:D_MODEL].astype(F32))
        sb = jax.nn.sigmoid(g_ref[:, D_MODEL:2 * D_MODEL].astype(F32))
        o_ref[...] = (sa * ya + sb * yb).astype(BF16)
        ya_ref[...] = ya.astype(BF16)
        yb_ref[...] = yb.astype(BF16)

    blk = pl.BlockSpec((tm, D_MODEL), lambda i: (i, 0))
    narrow = pl.BlockSpec((tm, A_WIDTH), lambda i: (i, 0))
    wide = pl.BlockSpec((tm, 2 * D_MODEL), lambda i: (i, 0))
    wspec = pl.BlockSpec((A_WIDTH, D_MODEL), lambda i: (0, 0))
    out = jax.ShapeDtypeStruct((t, D_MODEL), BF16)
    return pl.pallas_call(
        body, name=name, grid=(t // tm,),
        in_specs=[narrow, narrow, wspec, wspec, wide] + ([pl.BlockSpec(memory_space=pl.ANY)] if has_dep else []),
        out_specs=[blk, blk, blk], out_shape=[out, out, out], compiler_params=_params(("parallel",)),
    )(*((oa, ob, wa, wb, gates) + ((dep,) if has_dep else ())))


def _branch_bwd_x(name, dya, dyb, wa, wb):
    t = dya.shape[0]
    tm = _pick(t, (1088, 768, 512, 256, 128))

    def body(da_ref, db_ref, wa_ref, wb_ref, oa_ref, ob_ref):
        oa_ref[...] = lax.dot_general(da_ref[...], wa_ref[...], _NT, preferred_element_type=F32)
        ob_ref[...] = lax.dot_general(db_ref[...], wb_ref[...], _NT, preferred_element_type=F32).astype(BF16)

    blk = pl.BlockSpec((tm, D_MODEL), lambda i: (i, 0))
    narrow = pl.BlockSpec((tm, A_WIDTH), lambda i: (i, 0))
    wspec = pl.BlockSpec((A_WIDTH, D_MODEL), lambda i: (0, 0), pipeline_mode=pl.Buffered(1))
    return pl.pallas_call(
        body, name=name, grid=(t // tm,), in_specs=[blk, blk, wspec, wspec], out_specs=[narrow, narrow],
        out_shape=[jax.ShapeDtypeStruct((t, A_WIDTH), F32), jax.ShapeDtypeStruct((t, B_WIDTH), BF16)],
        compiler_params=_params(("parallel",)),
    )(dya, dyb, wa, wb)


def _branch_bwd_w(name, oa, ob, dya, dyb):
    t = oa.shape[0]
    tn = 512

    def body(oa_ref, ob_ref, da_ref, db_ref, wa_ref, wb_ref):
        wa_ref[...] = lax.dot_general(_bf(oa_ref[...]), da_ref[...], _TN, preferred_element_type=F32).astype(BF16)
        wb_ref[...] = lax.dot_general(_bf(ob_ref[...]), db_ref[...], _TN, preferred_element_type=F32).astype(BF16)

    whole = pl.BlockSpec((t, A_WIDTH), lambda j: (0, 0), pipeline_mode=pl.Buffered(1))
    cols = pl.BlockSpec((t, tn), lambda j: (0, j))
    out_spec = pl.BlockSpec((A_WIDTH, tn), lambda j: (0, j))
    out = jax.ShapeDtypeStruct((A_WIDTH, D_MODEL), BF16)
    return pl.pallas_call(
        body, name=name, grid=(D_MODEL // tn,), in_specs=[whole, whole, cols, cols], out_specs=[out_spec, out_spec],
        out_shape=[out, out], compiler_params=_params(("parallel",)),
    )(oa, ob, dya, dyb)


def _mix_out_gate_bwd(name, dh, wo, gates, ya, yb):
    t = gates.shape[0]
    tm = _pick(t, (544, 384, 256, 128))

    def body(dh_ref, w_ref, g_ref, ya_ref, yb_ref, dya_ref, dyb_ref, dg_ref):
        dm = lax.dot_general(_bf(dh_ref[...]), w_ref[...], _NT, preferred_element_type=F32)
        sa = jax.nn.sigmoid(g_ref[:, 0:D_MODEL].astype(F32))
        sb = jax.nn.sigmoid(g_ref[:, D_MODEL:2 * D_MODEL].astype(F32))
        dya_ref[...] = (dm * sa).astype(BF16)
        dyb_ref[...] = (dm * sb).astype(BF16)
        dg_ref[:, 0:D_MODEL] = (dm * ya_ref[...].astype(F32) * (sa * (1.0 - sa))).astype(BF16)
        dg_ref[:, D_MODEL:2 * D_MODEL] = (dm * yb_ref[...].astype(F32) * (sb * (1.0 - sb))).astype(BF16)

    blk = pl.BlockSpec((tm, D_MODEL), lambda i: (i, 0))
    wide = pl.BlockSpec((tm, 2 * D_MODEL), lambda i: (i, 0))
    out = jax.ShapeDtypeStruct((t, D_MODEL), BF16)
    return pl.pallas_call(
        body, name=name, grid=(t // tm,),
        in_specs=[blk, pl.BlockSpec((D_MODEL, D_MODEL), lambda i: (0, 0)), wide, blk, blk], out_specs=[blk, blk, wide],
        out_shape=[out, out, jax.ShapeDtypeStruct((t, 2 * D_MODEL), BF16)], compiler_params=_params(("parallel",)),
    )(dh, wo, gates, ya, yb)


def _ffn_out_loss(name, a, w, res, gf, tgt, alpha):
    t, k = a.shape
    b, s, _ = tgt.shape
    l = t // b
    tm = _pick(t, (544, 384, 256, 128))
    per = l // tm
    assert per * tm == l and tm > PREFIX and l - PREFIX == s

    def body(a_ref, w_ref, r_ref, g_ref, t_ref, dh_ref, dhb_ref, loss_ref, dg_ref, buf, sem):
        i = pl.program_id(0)
        bi, r = i // per, i % per

        def first_rows():
            return pltpu.make_async_copy(t_ref.at[bi, pl.ds(0, tm - PREFIX)], buf.at[pl.ds(PREFIX, tm - PREFIX)], sem)

        def later_rows():
            return pltpu.make_async_copy(t_ref.at[bi, pl.ds(pl.multiple_of(r * tm - PREFIX, 8), tm)], buf, sem)

        @pl.when(r == 0)
        def _():
            buf[0:PREFIX, :] = jnp.zeros((PREFIX, D_MODEL), F32)
            first_rows().start()

        if per > 1:
            @pl.when(r > 0)
            def _():
                later_rows().start()

        acc = lax.dot_general(a_ref[...], w_ref[...], _NN, preferred_element_type=F32)

        @pl.when(r == 0)
        def _():
            first_rows().wait()

        if per > 1:
            @pl.when(r > 0)
            def _():
                later_rows().wait()

        hv = acc * alpha + r_ref[...]
        row = lax.broadcasted_iota(jnp.int32, (tm, 1), 0)
        real = ((r > 0) | (row >= PREFIX)).astype(F32)
        g = g_ref[...]
        rn = lax.rsqrt(jnp.mean(hv * hv, axis=-1, keepdims=True) + EPS)
        xn = hv * rn
        err = (xn * g - buf[...]) * real
        lpart = 0.5 * jnp.sum(jnp.mean(err * err, axis=-1, keepdims=True), axis=0, keepdims=True)
        dy = err * (1.0 / D_MODEL)
        tv = dy * g
        dot = jnp.mean(tv * hv, axis=-1, keepdims=True)
        dh = rn * tv - hv * (rn * rn * rn * dot)
        dh_ref[...] = dh
        dhb_ref[...] = dh.astype(BF16)
        gpart = jnp.sum(dy * xn, axis=0, keepdims=True)

        @pl.when(i == 0)
        def _():
            loss_ref[...] = jnp.zeros_like(loss_ref)
            dg_ref[...] = jnp.zeros_like(dg_ref)

        loss_ref[...] += jnp.broadcast_to(lpart, loss_ref.shape)
        dg_ref[...] += gpart

    row_spec = pl.BlockSpec((tm, D_MODEL), lambda i: (i, 0))
    vec = pl.BlockSpec((1, D_MODEL), lambda i: (0, 0))
    return pl.pallas_call(
        body, name=name, grid=(t // tm,),
        in_specs=[pl.BlockSpec((tm, k), lambda i: (i, 0)),
                  pl.BlockSpec((k, D_MODEL), lambda i: (0, 0), pipeline_mode=pl.Buffered(1)), row_spec, vec,
                  pl.BlockSpec(memory_space=pl.ANY)],
        out_specs=[row_spec, row_spec, pl.BlockSpec((8, 128), lambda i: (0, 0)), vec],
        out_shape=[jax.ShapeDtypeStruct((t, D_MODEL), F32), jax.ShapeDtypeStruct((t, D_MODEL), BF16),
                   jax.ShapeDtypeStruct((8, 128), F32), jax.ShapeDtypeStruct((1, D_MODEL), F32)],
        scratch_shapes=[pltpu.VMEM((tm, D_MODEL), F32), pltpu.SemaphoreType.DMA(())],
        compiler_params=_params(("arbitrary",)),
    )(a, w, res, gf, tgt)


def _fgate_fwd(name, f3, bf_row):
    b, l, _ = f3.shape
    nb = l // BLOCK

    def body(f_ref, b_ref, cc_ref, cr_ref):
        r_i = lax.broadcasted_iota(jnp.int32, (BLOCK, BLOCK), 0)
        c_i = lax.broadcasted_iota(jnp.int32, (BLOCK, BLOCK), 1)
        tri = (r_i >= c_i).astype(F32)
        carry = jnp.zeros((1, 128), F32)
        for blk in range(nb):
            rows = slice(blk * BLOCK, (blk + 1) * BLOCK)
            z = f_ref[rows, :] + b_ref[...]
            lf = jnp.minimum(z, 0.0) - jnp.log(1.0 + jnp.exp(-jnp.abs(z)))
            cb = jnp.dot(tri, lf, preferred_element_type=F32, precision=lax.Precision.HIGHEST) + carry
            carry = cb[BLOCK - 1:BLOCK, :]
            cbt = cb.T
            for hh in range(B_HEADS):
                cc_ref[hh, rows, :] = jnp.sum(jnp.where(c_i == hh, cb, 0.0), axis=1, keepdims=True)
                cr_ref[hh, :, rows] = cbt[hh:hh + 1, :]

    return pl.pallas_call(
        body, name=name, grid=(b,),
        in_specs=[pl.BlockSpec((None, l, 128), lambda bi: (bi, 0, 0)),
                  pl.BlockSpec((1, 128), lambda bi: (0, 0))],
        out_specs=[pl.BlockSpec((None, B_HEADS, l, 1), lambda bi: (bi, 0, 0, 0)),
                   pl.BlockSpec((None, B_HEADS, 1, l), lambda bi: (bi, 0, 0, 0))],
        out_shape=[jax.ShapeDtypeStruct((b, B_HEADS, l, 1), F32), jax.ShapeDtypeStruct((b, B_HEADS, 1, l), F32)],
        compiler_params=_params(("parallel",)),
    )(f3, bf_row)


def _fgate_bwd(name, f3, bf_row, dcq, dck):
    b, l, _ = f3.shape
    nb = l // BLOCK

    def body(f_ref, b_ref, dcq_ref, dck_ref, df_ref, db_ref):
        r_i = lax.broadcasted_iota(jnp.int32, (BLOCK, BLOCK), 0)
        c_i = lax.broadcasted_iota(jnp.int32, (BLOCK, BLOCK), 1)
        tri = (r_i <= c_i).astype(F32)
        carry = jnp.zeros((1, 128), F32)
        total = jnp.zeros((1, 128), F32)
        for blk in range(nb - 1, -1, -1):
            rows = slice(blk * BLOCK, (blk + 1) * BLOCK)
            krows = jnp.concatenate([dck_ref[hh, :, rows] for hh in range(B_HEADS)]
                                    + [jnp.zeros((BLOCK - B_HEADS, BLOCK), F32)], axis=0)
            dcb = krows.T
            for hh in range(B_HEADS):
                dcb = dcb + jnp.where(c_i == hh, dcq_ref[hh, rows, :], 0.0)
            rc = jnp.dot(tri, dcb, preferred_element_type=F32, precision=lax.Precision.HIGHEST) + carry
            carry = rc[0:1, :]
            z = f_ref[rows, :] + b_ref[...]
            df = rc * (1.0 / (1.0 + jnp.exp(z)))
            df_ref[rows, :] = df.astype(BF16)
            total = total + jnp.sum(df, axis=0, keepdims=True)

        @pl.when(pl.program_id(0) == 0)
        def _():
            db_ref[...] = total

        @pl.when(pl.program_id(0) > 0)
        def _():
            db_ref[...] += total

    return pl.pallas_call(
        body, name=name, grid=(b,),
        in_specs=[pl.BlockSpec((None, l, 128), lambda bi: (bi, 0, 0)),
                  pl.BlockSpec((1, 128), lambda bi: (0, 0)),
                  pl.BlockSpec((None, B_HEADS, l, 1), lambda bi: (bi, 0, 0, 0)),
                  pl.BlockSpec((None, B_HEADS, 1, l), lambda bi: (bi, 0, 0, 0))],
        out_specs=[pl.BlockSpec((None, l, 128), lambda bi: (bi, 0, 0)), pl.BlockSpec((1, 128), lambda bi: (0, 0))],
        out_shape=[jax.ShapeDtypeStruct((b, l, 128), BF16), jax.ShapeDtypeStruct((1, 128), F32)],
        compiler_params=_params(("arbitrary",)),
    )(f3, bf_row, dcq, dck)


A_Q_BLK = B_SEG // A_WIDTH
A_K_BLK = (B_SEG + A_WIDTH) // 128
A_V_BLK = A_K_BLK + 1
A_SEG_BLK = B_SEG // A_SEG
STACK = A_HEADS * BLOCK


def _lane_lo():
    return lax.broadcasted_iota(jnp.int32, (1, 128), 1) < HEAD_DIM


def _stack_heads(x, masked):
    lo = _lane_lo()
    blks = [x[:, 128 * j:128 * (j + 1)] for j in range(4)]
    if not masked:
        return jnp.concatenate(blks + blks, axis=0)
    zero = jnp.zeros_like(blks[0])
    return jnp.concatenate([jnp.where(lo, bk, zero) for bk in blks] + [jnp.where(lo, zero, bk) for bk in blks], axis=0)


def _unstack_heads(y):
    lo = _lane_lo()
    return jnp.concatenate([jnp.where(lo, y[128 * j:128 * (j + 1)], y[128 * (4 + j):128 * (5 + j)]) for j in range(4)], axis=1)


def _swa_bias(slopes):
    r_i = jnp.arange(STACK)[:, None]
    c_i = jnp.arange(3 * BLOCK)[None, :]
    seg = c_i >> 7
    out = []
    for n in range(3):
        qpos = n * BLOCK + (r_i & (BLOCK - 1))
        kpos = jnp.where(seg == 0, c_i, (n - 2) * BLOCK + c_i)
        dist = qpos - kpos
        band = (seg != 0) & (dist < BLOCK) & (kpos >= PREFIX)
        meta = (seg == 0) & (c_i >= N_PAD)
        out.append(jnp.where((dist >= 0) & (band | meta), -slopes * dist.astype(F32), NEG))
    return jnp.stack(out, axis=0)


def _swa_scores(q, kcat, n, slope, bias):
    s = lax.dot_general(q, kcat, _NT, preferred_element_type=F32) + bias
    further = slope * (-BLOCK * jnp.maximum(n - 2, 0)).astype(F32)
    return jnp.concatenate([s[:, 0:BLOCK] + further, s[:, BLOCK:]], axis=1)


def _swa_specs():
    def kv(col_blk):
        return [pl.BlockSpec((None, BLOCK, 128), lambda b, n: (b, 0, col_blk)),
                pl.BlockSpec((None, BLOCK, 128), lambda b, n: (b, jnp.maximum(n - 1, 0), col_blk)),
                pl.BlockSpec((None, BLOCK, 128), lambda b, n: (b, n, col_blk))]

    q_spec = pl.BlockSpec((None, BLOCK, A_WIDTH), lambda b, n: (b, n, A_Q_BLK))
    o_spec = pl.BlockSpec((None, BLOCK, A_WIDTH), lambda b, n: (b, n, 0))
    col = pl.BlockSpec((STACK, 1), lambda b, n: (0, 0))
    bias = pl.BlockSpec((None, STACK, 3 * BLOCK), lambda b, n: (jnp.minimum(n, 2), 0, 0))
    lse_spec = pl.BlockSpec((None, A_HEADS, BLOCK, 1), lambda b, n: (b, 0, n, 0))
    return q_spec, kv(A_K_BLK), kv(A_V_BLK), o_spec, [col, col, bias], lse_spec


def _swa_fwd(name, qkv, slopes, sinks, bias):
    b, l, _ = qkv.shape
    nb = l // BLOCK

    def body(q_ref, k0_ref, kp_ref, kc_ref, v0_ref, vp_ref, vc_ref, sl_ref, sk_ref, bias_ref, o_ref, lse_ref):
        n = pl.program_id(1)
        qs = _stack_heads(q_ref[...], True) * SCALE
        kcat = jnp.concatenate([k0_ref[...], kp_ref[...], kc_ref[...]], axis=0)
        vcat = jnp.concatenate([v0_ref[...], vp_ref[...], vc_ref[...]], axis=0)
        s = _swa_scores(qs, kcat, n, sl_ref[...], bias_ref[...])
        sink = sk_ref[...]
        m = jnp.maximum(jnp.max(s, axis=-1, keepdims=True), sink)
        p = jnp.exp(s - m)
        den = jnp.sum(p, axis=-1, keepdims=True) + jnp.exp(sink - m)
        o = lax.dot_general(p.astype(BF16), vcat, _NN, preferred_element_type=F32) / den
        o_ref[...] = _unstack_heads(o)
        lse_ref[...] = (m + jnp.log(den)).reshape(A_HEADS, BLOCK, 1)

    q_spec, k_specs, v_specs, o_spec, consts, lse_spec = _swa_specs()
    return pl.pallas_call(
        body, name=name, grid=(b, nb),
        in_specs=[q_spec] + k_specs + v_specs + consts, out_specs=[o_spec, lse_spec],
        out_shape=[jax.ShapeDtypeStruct((b, l, A_WIDTH), F32), jax.ShapeDtypeStruct((b, A_HEADS, l, 1), F32)],
        compiler_params=_params(("parallel", "parallel")),
    )(qkv, qkv, qkv, qkv, qkv, qkv, qkv, slopes, sinks, bias)


def _swa_bwd(name, qkv, o, lse, do, slopes, sinks, bias, dqkv):
    b, l, _ = qkv.shape
    nb = l // BLOCK

    def body(q_ref, k0_ref, kp_ref, kc_ref, v0_ref, vp_ref, vc_ref, o_ref, lse_ref, do_ref, sl_ref, sk_ref, bias_ref, _,
             dx_ref, ds_ref, dk_acc, dv_acc):
        bi = pl.program_id(0)
        n = pl.program_id(1)
        qs = _stack_heads(q_ref[...], True) * SCALE
        dos32 = _stack_heads(do_ref[...], True)
        dos = dos32.astype(BF16)
        os_ = _stack_heads(o_ref[...], False)
        lsev = lse_ref[...].reshape(STACK, 1)
        kcat = jnp.concatenate([k0_ref[...], kp_ref[...], kc_ref[...]], axis=0)
        vcat = jnp.concatenate([v0_ref[...], vp_ref[...], vc_ref[...]], axis=0)
        s = _swa_scores(qs, kcat, n, sl_ref[...], bias_ref[...])
        p = jnp.exp(s - lsev)
        dsum = jnp.sum(dos32 * os_, axis=-1, keepdims=True)
        dp = lax.dot_general(dos, vcat, _NT, preferred_element_type=F32)
        dsc = (p * (dp - dsum)).astype(BF16)
        dq = lax.dot_general(dsc, kcat, _NN, preferred_element_type=F32) * SCALE
        row0 = pl.multiple_of(n * BLOCK, BLOCK)
        dx_ref[pl.ds(row0, BLOCK), 0:A_WIDTH] = _unstack_heads(dq).astype(BF16)
        dkc = lax.dot_general(dsc, qs, _TN, preferred_element_type=F32)
        dvc = lax.dot_general(p.astype(BF16), dos, _TN, preferred_element_type=F32)

        @pl.when(n == 0)
        def _():
            dk_acc[...] = jnp.zeros_like(dk_acc)
            dv_acc[...] = jnp.zeros_like(dv_acc)

        starts = (0, pl.multiple_of(jnp.maximum(n - 1, 0) * BLOCK, BLOCK), row0)
        for t, st in enumerate(starts):
            dk_acc[pl.ds(st, BLOCK), :] += dkc[t * BLOCK:(t + 1) * BLOCK, :]
            dv_acc[pl.ds(st, BLOCK), :] += dvc[t * BLOCK:(t + 1) * BLOCK, :]

        @pl.when(n == nb - 1)
        def _():
            dx_ref[:, A_WIDTH:A_WIDTH + 128] = dk_acc[...].astype(BF16)
            dx_ref[:, A_WIDTH + 128:A_SEG] = dv_acc[...].astype(BF16)

        dsink = -(jnp.exp(sk_ref[...] - lsev) * dsum)
        r8 = lax.broadcasted_iota(jnp.int32, (8, 128), 0)
        acc = jnp.zeros((8, 128), F32)
        for hh in range(A_HEADS):
            acc = acc + jnp.where(r8 == hh, jnp.sum(dsink[hh * BLOCK:(hh + 1) * BLOCK, :]), 0.0)

        @pl.when((bi == 0) & (n == 0))
        def _():
            ds_ref[...] = jnp.zeros_like(ds_ref)

        ds_ref[...] += acc

    q_spec, k_specs, v_specs, o_spec, consts, lse_spec = _swa_specs()
    return pl.pallas_call(
        body, name=name, grid=(b, nb),
        in_specs=[q_spec] + k_specs + v_specs + [o_spec, lse_spec, o_spec] + consts + [pl.BlockSpec(memory_space=pl.ANY)],
        out_specs=[pl.BlockSpec((None, l, A_SEG), lambda bb, n: (bb, 0, A_SEG_BLK)),
                   pl.BlockSpec((8, 128), lambda bb, n: (0, 0))],
        out_shape=[jax.ShapeDtypeStruct(dqkv.shape, BF16), jax.ShapeDtypeStruct((8, 128), F32)],
        scratch_shapes=[pltpu.VMEM((l, 128), F32), pltpu.VMEM((l, 128), F32)],
        input_output_aliases={13: 0},
        compiler_params=_params(("arbitrary", "arbitrary")),
    )(qkv, qkv, qkv, qkv, qkv, qkv, qkv, o, lse, do, slopes, sinks, bias, dqkv)


def _fox_mask(qk, ck, i):
    kh = qk.shape[1]
    qpos = i * BLOCK + lax.broadcasted_iota(jnp.int32, (BLOCK, kh), 0)
    kpos = lax.broadcasted_iota(jnp.int32, (BLOCK, kh), 1)
    return jnp.where((kpos <= qpos) & (kpos >= N_PAD), qk - ck, NEG)


def _pick_head(x, hh):
    lo = _lane_lo()
    return jnp.where(lo if hh == 0 else jnp.logical_not(lo), x, jnp.zeros_like(x))


def _both_heads(x):
    return jnp.concatenate([_pick_head(x, 0), _pick_head(x, 1)], axis=0)


def _fox_specs(l):
    pair = pl.BlockSpec((None, l, PAIR_W), lambda bi, hp: (bi, 0, hp))
    half = pl.BlockSpec((None, l, 128), lambda bi, hp: (bi, 0, hp))
    colv = pl.BlockSpec((None, 2, l, 1), lambda bi, hp: (bi, hp, 0, 0))
    rowv = pl.BlockSpec((None, 2, 1, l), lambda bi, hp: (bi, hp, 0, 0))
    return pair, half, colv, rowv


def _fox_fwd(name, qkv, c_col, c_row):
    b, l, _ = qkv.shape
    nb = l // BLOCK

    def body(x_ref, cc_ref, cr_ref, o_ref, lse_ref):
        for i in range(nb):
            rows = slice(i * BLOCK, (i + 1) * BLOCK)
            kh = (i + 1) * BLOCK
            qblk = x_ref[rows, 0:128]
            kv = x_ref[0:kh, 128:256]
            vv = x_ref[0:kh, 256:384]
            qk = lax.dot_general(_both_heads(qblk) * SCALE, kv, _NT, preferred_element_type=F32)
            ps, dens = [], []
            for hh in range(2):
                s = _fox_mask(qk[hh * BLOCK:(hh + 1) * BLOCK], cr_ref[hh, :, 0:kh], i)
                m = jnp.max(s, axis=-1, keepdims=True)
                p = jnp.exp(s - m)
                den = jnp.sum(p, axis=-1, keepdims=True)
                ps.append(p.astype(BF16))
                dens.append(den)
                lse_ref[hh, rows, :] = (m + jnp.log(den)) + cc_ref[hh, rows, :]
            pv = lax.dot_general(jnp.concatenate(ps, axis=0), vv, _NN, preferred_element_type=F32)
            o_ref[rows, :] = jnp.where(_lane_lo(), pv[0:BLOCK] / dens[0], pv[BLOCK:2 * BLOCK] / dens[1]).astype(BF16)

    pair, half, colv, rowv = _fox_specs(l)
    return pl.pallas_call(
        body, name=name, grid=(b, 4), in_specs=[pair, colv, rowv], out_specs=[half, colv],
        out_shape=[jax.ShapeDtypeStruct((b, l, B_WIDTH), BF16), jax.ShapeDtypeStruct((b, B_HEADS, l, 1), F32)],
        compiler_params=_params(("parallel", "parallel")),
    )(qkv, c_col, c_row)


def _fox_bwd(name, qkv, c_col, c_row, o, lse, do):
    b, l, _ = qkv.shape
    nb = l // BLOCK

    def body(x_ref, cc_ref, cr_ref, o_ref, lse_ref, do_ref, dx_ref, dcq_ref, dck_ref, dk_acc, dv_acc):
        dk_acc[...] = jnp.zeros_like(dk_acc)
        dv_acc[...] = jnp.zeros_like(dv_acc)
        dck_ref[...] = jnp.zeros_like(dck_ref)
        for i in range(nb):
            rows = slice(i * BLOCK, (i + 1) * BLOCK)
            kh = (i + 1) * BLOCK
            qblk = x_ref[rows, 0:128]
            kv = x_ref[0:kh, 128:256]
            vv = x_ref[0:kh, 256:384]
            doblk = do_ref[rows, :]
            ov = o_ref[rows, :].astype(F32)
            q2 = _both_heads(qblk) * SCALE
            do2 = _both_heads(doblk)
            qk = lax.dot_general(q2, kv, _NT, preferred_element_type=F32)
            dp = lax.dot_general(do2, vv, _NT, preferred_element_type=F32)
            ps, dss = [], []
            for hh in range(2):
                half = slice(hh * BLOCK, (hh + 1) * BLOCK)
                s = _fox_mask(qk[half], cr_ref[hh, :, 0:kh], i)
                p = jnp.exp(s - (lse_ref[hh, rows, :] - cc_ref[hh, rows, :]))
                dsum = jnp.sum(do2[half].astype(F32) * ov, axis=-1, keepdims=True)
                ds = p * (dp[half] - dsum)
                ps.append(p.astype(BF16))
                dss.append(ds.astype(BF16))
                dcq_ref[hh, rows, :] = jnp.sum(ds, axis=-1, keepdims=True)
                dck_ref[hh, :, 0:kh] -= jnp.sum(ds, axis=0, keepdims=True)
            p2 = jnp.concatenate(ps, axis=0)
            ds2 = jnp.concatenate(dss, axis=0)
            dq = lax.dot_general(ds2, kv, _NN, preferred_element_type=F32) * SCALE
            dk_acc[0:kh, :] += lax.dot_general(ds2, q2, _TN, preferred_element_type=F32)
            dv_acc[0:kh, :] += lax.dot_general(p2, do2, _TN, preferred_element_type=F32)
            dx_ref[rows, 0:128] = jnp.where(_lane_lo(), dq[0:BLOCK], dq[BLOCK:2 * BLOCK]).astype(BF16)
        dx_ref[:, 128:256] = dk_acc[...].astype(BF16)
        dx_ref[:, 256:384] = dv_acc[...].astype(BF16)

    pair, half, colv, rowv = _fox_specs(l)
    return pl.pallas_call(
        body, name=name, grid=(b, 4), in_specs=[pair, colv, rowv, half, colv, half],
        out_specs=[pair, colv, rowv],
        out_shape=[jax.ShapeDtypeStruct(qkv.shape, BF16), jax.ShapeDtypeStruct((b, B_HEADS, l, 1), F32),
                   jax.ShapeDtypeStruct((b, B_HEADS, 1, l), F32)],
        scratch_shapes=[pltpu.VMEM((l, 128), F32), pltpu.VMEM((l, 128), F32)],
        compiler_params=_params(("parallel", "parallel")),
    )(qkv, c_col, c_row, o, lse, do)


_FLIPS = ((0, 0, 1), (0, 1, 0), (0, 1, 1), (1, 0, 0), (1, 0, 1), (1, 1, 0), (1, 1, 1))


def _peer_table():
    x, y, c = lax.axis_index("x"), lax.axis_index("y"), lax.axis_index("c")
    me = 4 * x + 2 * y + c
    peers = []
    for fx, fy, fc in _FLIPS:
        px = 1 - x if fx else x
        py = 1 - y if fy else y
        pc = 1 - c if fc else c
        peers.append(((px, py, pc), 4 * px + 2 * py + pc))
    return me, peers


_HBM = pl.BlockSpec(memory_space=pltpu.HBM)
_SEM = pl.BlockSpec(memory_space=pltpu.SEMAPHORE)
_ANY = pl.BlockSpec(memory_space=pl.ANY)
_EFFECT = pltpu.SideEffectType.DATAFLOW_SIDE_EFFECTING


def _split_copy(srcs_are_pieces, src_refs, land_refs, send_sem, recv_sem, a, kk, me, peers, arriving):
    dev, lin = peers[kk]
    npeer = len(_FLIPS)
    src = src_refs[a] if srcs_are_pieces[a] else src_refs[a].at[lin]
    dst = land_refs[a].at[lin] if arriving else land_refs[a].at[me]
    return pltpu.make_async_remote_copy(src_ref=src, dst_ref=dst, send_sem=send_sem.at[a * npeer + kk],
                                        recv_sem=recv_sem.at[a * npeer + kk], device_id=dev, device_id_type=MESH_ID)


def _xchg_start(name, gather, scatter, after=None):
    me_out = 4 * lax.axis_index("x") + 2 * lax.axis_index("y") + lax.axis_index("c")
    srcs = list(gather) + list(scatter)
    is_piece = [True] * len(gather) + [False] * len(scatter)
    lands = []
    for a, piece in zip(srcs, is_piece):
        if piece:
            lands.append(lax.dynamic_update_slice(lax.empty((N_DEV,) + tuple(a.shape), a.dtype), a[None],
                                                  (me_out,) + (0,) * a.ndim))
        else:
            lands.append(lax.empty(tuple(a.shape), a.dtype))
    n = len(srcs)
    nsem = n * len(_FLIPS)
    has_after = after is not None

    def body(*refs):
        src_refs = refs[:n]
        land_refs = refs[n:2 * n]
        outs = refs[2 * n + (1 if has_after else 0):]
        send_sem, recv_sem = outs[0], outs[1]
        token = outs[-1]
        me, peers = _peer_table()
        for kk in range(len(_FLIPS)):
            for a in range(n):
                _split_copy(is_piece, src_refs, land_refs, send_sem, recv_sem, a, kk, me, peers, False).start()
        token[...] = jnp.zeros_like(token)

    out_shape = ([pltpu.SemaphoreType.DMA((nsem,)), pltpu.SemaphoreType.DMA((nsem,))]
                 + [pltpu.HBM(tuple(a.shape), a.dtype) for a in srcs] + [pltpu.HBM(tuple(a.shape), a.dtype) for a in lands]
                 + [jax.ShapeDtypeStruct((8, 128), F32)])
    args = [pltpu.with_memory_space_constraint(a, pltpu.HBM) for a in srcs + lands] + ([after] if has_after else [])
    res = pl.pallas_call(
        body, name=name, out_shape=out_shape,
        in_specs=[_HBM] * (2 * n) + ([_ANY] if has_after else []),
        out_specs=[_SEM, _SEM] + [_HBM] * (2 * n) + [pl.BlockSpec(memory_space=pltpu.VMEM)],
        input_output_aliases={i: 2 + i for i in range(2 * n)},
        compiler_params=pltpu.CompilerParams(has_side_effects=_EFFECT),
    )(*args)
    state = (res[0], res[1], list(res[2:2 + n]), list(res[2 + n:2 + 2 * n]), is_piece)
    return state, res[-1]


def _xchg_wait(name, state, after):
    send_sem, recv_sem, srcs, lands, is_piece = state
    n = len(srcs)

    def body(*refs):
        src_refs = refs[:n]
        land_refs = refs[n:2 * n]
        s_sem, r_sem = refs[2 * n], refs[2 * n + 1]
        me, peers = _peer_table()
        for kk in range(len(_FLIPS)):
            for a in range(n):
                cp = _split_copy(is_piece, src_refs, land_refs, s_sem, r_sem, a, kk, me, peers, True)
                cp.wait_send()
                cp.wait_recv()

    out_shape = [pltpu.HBM(tuple(a.shape), a.dtype) for a in srcs] + [pltpu.HBM(tuple(a.shape), a.dtype) for a in lands]
    res = pl.pallas_call(
        body, name=name, out_shape=out_shape,
        in_specs=[_HBM] * (2 * n) + [_SEM, _SEM, _ANY], out_specs=[_HBM] * (2 * n),
        input_output_aliases={i: i for i in range(2 * n)},
        compiler_params=pltpu.CompilerParams(has_side_effects=_EFFECT),
    )(*srcs, *lands, send_sem, recv_sem, after)
    return list(res[n:]), list(res[:n])


_SIB = (0, 0, 1)
_ICI = ((0, 1, 0), (1, 0, 0), (1, 1, 0))


def _flip(fl):
    x, y, c = lax.axis_index("x"), lax.axis_index("y"), lax.axis_index("c")
    px = 1 - x if fl[0] else x
    py = 1 - y if fl[1] else y
    pc = 1 - c if fl[2] else c
    return (px, py, pc), 4 * px + 2 * py + pc


def _gather2_start(name, pieces, after=None):
    me_out = 4 * lax.axis_index("x") + 2 * lax.axis_index("y") + lax.axis_index("c")
    pieces = list(pieces)
    n = len(pieces)
    lands = [lax.dynamic_update_slice(lax.empty((N_DEV,) + tuple(a.shape), a.dtype), a[None],
                                      (me_out,) + (0,) * a.ndim) for a in pieces]
    first = (_SIB,) + _ICI
    has_after = after is not None

    def body(*refs):
        src_refs, land_refs = refs[:n], refs[n:2 * n]
        outs = refs[2 * n + (1 if has_after else 0):]
        send_sem, recv_sem, token = outs[0], outs[1], outs[-1]
        _, me = _flip((0, 0, 0))
        for kk, fl in enumerate(first):
            dev, _ = _flip(fl)
            for a in range(n):
                pltpu.make_async_remote_copy(src_ref=src_refs[a], dst_ref=land_refs[a].at[me],
                                             send_sem=send_sem.at[a * 4 + kk], recv_sem=recv_sem.at[a * 4 + kk],
                                             device_id=dev, device_id_type=MESH_ID).start()
        token[...] = jnp.zeros_like(token)

    hbm = [pltpu.HBM(tuple(a.shape), a.dtype) for a in pieces + lands]
    res = pl.pallas_call(
        body, name=name,
        out_shape=[pltpu.SemaphoreType.DMA((4 * n,)), pltpu.SemaphoreType.DMA((4 * n,))] + hbm
        + [jax.ShapeDtypeStruct((8, 128), F32)],
        in_specs=[_HBM] * (2 * n) + ([_ANY] if has_after else []),
        out_specs=[_SEM, _SEM] + [_HBM] * (2 * n) + [pl.BlockSpec(memory_space=pltpu.VMEM)],
        input_output_aliases={i: 2 + i for i in range(2 * n)},
        compiler_params=pltpu.CompilerParams(has_side_effects=_EFFECT),
    )(*([pltpu.with_memory_space_constraint(a, pltpu.HBM) for a in pieces + lands] + ([after] if has_after else [])))
    return (res[0], res[1], list(res[2:2 + n]), list(res[2 + n:2 + 2 * n])), res[-1]


def _gather2_forward(name, state, after):
    send_a, recv_a, pieces, lands = state
    n = len(pieces)
    first = (_SIB,) + _ICI

    def body(*refs):
        src_refs, land_refs = refs[:n], refs[n:2 * n]
        s_a, r_a = refs[2 * n], refs[2 * n + 1]
        outs = refs[2 * n + 3:]
        send_b, recv_b, token = outs[0], outs[1], outs[-1]
        for kk, fl in enumerate(first):
            dev, lin = _flip(fl)
            for a in range(n):
                cp = pltpu.make_async_remote_copy(src_ref=src_refs[a], dst_ref=land_refs[a].at[lin],
                                                  send_sem=s_a.at[a * 4 + kk], recv_sem=r_a.at[a * 4 + kk],
                                                  device_id=dev, device_id_type=MESH_ID)
                cp.wait_send()
                cp.wait_recv()
        sib, _ = _flip(_SIB)
        for j, fl in enumerate(_ICI):
            _, lin = _flip(fl)
            for a in range(n):
                pltpu.make_async_remote_copy(src_ref=land_refs[a].at[lin], dst_ref=land_refs[a].at[lin],
                                             send_sem=send_b.at[a * 3 + j], recv_sem=recv_b.at[a * 3 + j],
                                             device_id=sib, device_id_type=MESH_ID).start()
        token[...] = jnp.zeros_like(token)

    hbm = [pltpu.HBM(tuple(a.shape), a.dtype) for a in pieces + lands]
    res = pl.pallas_call(
        body, name=name,
        out_shape=[pltpu.SemaphoreType.DMA((3 * n,)), pltpu.SemaphoreType.DMA((3 * n,))] + hbm
        + [jax.ShapeDtypeStruct((8, 128), F32)],
        in_specs=[_HBM] * (2 * n) + [_SEM, _SEM, _ANY],
        out_specs=[_SEM, _SEM] + [_HBM] * (2 * n) + [pl.BlockSpec(memory_space=pltpu.VMEM)],
        input_output_aliases={i: 2 + i for i in range(2 * n)},
        compiler_params=pltpu.CompilerParams(has_side_effects=_EFFECT),
    )(*pieces, *lands, send_a, recv_a, after)
    return (res[0], res[1], list(res[2 + n:2 + 2 * n])), res[-1]


def _gather2_wait(name, state, after):
    send_b, recv_b, lands = state
    n = len(lands)

    def body(*refs):
        land_refs = refs[:n]
        s_b, r_b = refs[n], refs[n + 1]
        sib, _ = _flip(_SIB)
        for j, fl in enumerate(_ICI):
            _, sent = _flip(fl)
            _, arriving = _flip((fl[0], fl[1], 1))
            for a in range(n):
                cp = pltpu.make_async_remote_copy(src_ref=land_refs[a].at[sent], dst_ref=land_refs[a].at[arriving],
                                                  send_sem=s_b.at[a * 3 + j], recv_sem=r_b.at[a * 3 + j],
                                                  device_id=sib, device_id_type=MESH_ID)
                cp.wait_send()
                cp.wait_recv()

    res = pl.pallas_call(
        body, name=name, out_shape=[pltpu.HBM(tuple(a.shape), a.dtype) for a in lands],
        in_specs=[_HBM] * n + [_SEM, _SEM, _ANY], out_specs=[_HBM] * n,
        input_output_aliases={i: i for i in range(n)},
        compiler_params=pltpu.CompilerParams(has_side_effects=_EFFECT),
    )(*lands, send_b, recv_b, after)
    return list(res)


def _adam_math(w, g, m, v):
    m = ADAM_B1 * m + (1.0 - ADAM_B1) * g
    v = ADAM_B2 * v + (1.0 - ADAM_B2) * (g * g)
    m_hat = m / (1.0 - ADAM_B1 ** ADAM_STEP)
    v_hat = v / (1.0 - ADAM_B2 ** ADAM_STEP)
    delta = -ADAM_LR * (m_hat / (jnp.sqrt(v_hat) + ADAM_EPS) + ADAM_WD * w)
    return delta, m, v


def _adam(name, w, m, v, parts, transposed=False, dep=None, own=None):
    npart, _, cp = parts.shape
    has_dep = dep is not None
    has_own = own is not None
    if transposed:
        c, r = w.shape
        tr = _pick(r, (256, 128))
        blk = pl.BlockSpec((c, tr), lambda i, *_: (0, i))
    else:
        r, c = w.shape
        tr = _pick(r, (256, 176, 128, 64, 16, 8, 1))
        blk = pl.BlockSpec((tr, c), lambda i, *_: (i, 0))

    def body(*refs):
        refs = list(refs)
        me = refs.pop(0)[0] if has_own else None
        w_ref, m_ref, v_ref, p_ref = refs[:4]
        own_ref = refs[4] if has_own else None
        g_ref, d_ref, mo_ref, vo_ref = refs[-4:]
        g = None
        for pp in range(npart):
            part = p_ref[pp] if not has_own else jnp.where(me == pp, own_ref[...], p_ref[pp])
            g = part.astype(F32) if g is None else g + part.astype(F32)
        g = g.T[0:c, :] if transposed else g[:, 0:c]
        delta, mn, vn = _adam_math(w_ref[...], g, m_ref[...], v_ref[...])
        g_ref[...] = g
        d_ref[...] = delta
        mo_ref[...] = mn
        vo_ref[...] = vn

    in_specs = [blk, blk, blk, pl.BlockSpec((npart, tr, cp), lambda i, *_: (0, i, 0))]
    args = [w, m, v, parts]
    if has_own:
        in_specs.append(pl.BlockSpec((None, tr, cp), lambda i, me_ref: (me_ref[0], i, 0)))
        args.append(own[0])
    if has_dep:
        in_specs.append(pl.BlockSpec(memory_space=pl.ANY))
        args.append(dep)
    out = jax.ShapeDtypeStruct(w.shape, F32)
    grid_spec = pltpu.PrefetchScalarGridSpec(num_scalar_prefetch=1 if has_own else 0, grid=(r // tr,),
                                             in_specs=in_specs, out_specs=[blk, blk, blk, blk])
    return pl.pallas_call(
        body, name=name, grid_spec=grid_spec, out_shape=[out, out, out, out], compiler_params=_params(("parallel",)),
    )(*(([own[1]] if has_own else []) + args))


def _adam_rows(name, w, m, v, parts, dep, own):
    c = w.shape[0]
    npart, r, cp = parts.shape
    steps = 5
    tc = c // steps
    assert tc * steps == c and r == 8 * 128

    has_dep = dep is not None

    def body(me_ref, w_ref, m_ref, v_ref, p_ref, own_ref, *rest):
        g_ref, d_ref, mo_ref, vo_ref, gsum = rest[-5:]
        i = pl.program_id(0)

        @pl.when(i == 0)
        def _():
            me = me_ref[0]
            g = None
            for pp in range(npart):
                part = jnp.where(me == pp, own_ref[...], p_ref[pp]).astype(F32)
                g = part if g is None else g + part
            gsum[...] = g.T.reshape(cp, 8, 128)

        g = gsum[pl.ds(i * tc, tc)]
        delta, mn, vn = _adam_math(w_ref[...], g, m_ref[...], v_ref[...])
        g_ref[...] = g
        d_ref[...] = delta
        mo_ref[...] = mn
        vo_ref[...] = vn

    blk = pl.BlockSpec((tc, 8, 128), lambda i, *_: (i, 0, 0))
    out = jax.ShapeDtypeStruct(w.shape, F32)
    grid_spec = pltpu.PrefetchScalarGridSpec(
        num_scalar_prefetch=1, grid=(steps,),
        in_specs=[blk, blk, blk, pl.BlockSpec((npart, r, cp), lambda i, *_: (0, 0, 0), pipeline_mode=pl.Buffered(1)),
                  pl.BlockSpec((None, r, cp), lambda i, me_ref: (me_ref[0], 0, 0), pipeline_mode=pl.Buffered(1))]
        + ([pl.BlockSpec(memory_space=pl.ANY)] if has_dep else []),
        out_specs=[blk, blk, blk, blk], scratch_shapes=[pltpu.VMEM((cp, 8, 128), F32)])
    return pl.pallas_call(
        body, name=name, grid_spec=grid_spec, out_shape=[out, out, out, out], compiler_params=_params(("arbitrary",)),
    )(*((own[1], w, m, v, parts, own[0]) + ((dep,) if has_dep else ())))


def _small_update(name, packs, wmv):
    nparam = len(wmv)

    def body(p_ref, *refs):
        ins, outs = refs[:3 * nparam], refs[3 * nparam:]
        tot = p_ref[0]
        for pp in range(1, N_DEV):
            tot = tot + p_ref[pp]
        outs[0][0:8, :] = tot[0:8, :]
        outs[0][8:24, :] = tot[8:24, :] + tot[24:40, :]
        grads = [tot[i:i + 1, :] for i in range(4)] + [tot[4:5, 0:B_HEADS], tot[4:5, B_HEADS:2 * B_HEADS]]
        for i, g in enumerate(grads):
            w_ref, m_ref, v_ref = ins[3 * i:3 * i + 3]
            delta, mn, vn = _adam_math(w_ref[...], g, m_ref[...], v_ref[...])
            for o_ref, val in zip(outs[1 + 4 * i:5 + 4 * i], (g, delta, mn, vn)):
                o_ref[...] = val

    flat = [a for trio in wmv for a in trio]
    out_shape = [jax.ShapeDtypeStruct((24, D_MODEL), F32)]
    for w, _, _ in wmv:
        out_shape += [jax.ShapeDtypeStruct(w.shape, F32)] * 4
    res = pl.pallas_call(body, name=name, out_shape=out_shape, compiler_params=_params())(packs, *flat)
    return res[0], [tuple(res[1 + 4 * i:5 + 4 * i]) for i in range(nparam)]


def _local_step(x, tgt, g1, gm, g2, gf, b_forget, sinks, weights, send):
    b, s, _ = x.shape
    l = s + PREFIX
    t = b * l
    (meta,) = weights("meta", x)

    h0, n1 = _embed_norm("embed_rms1_fwd", x, meta, g1)
    (w1i,) = weights("ffn1_in", n1)
    gu1, a1 = _ffn_in_fwd("ffn1_in_fwd", n1, w1i)
    (w1o,) = weights("ffn1_out", weights("ffn1_out:forward", a1))
    h1, um = _mm_res_norm("ffn1_out_fwd", a1, w1o, h0, gm, alpha=0.5)
    wi, wa, wb, wo = weights("mix", weights("mix:forward", um))
    qkv, gates, f2 = _proj_fwd("proj_fwd", um, wi)
    qkv3 = qkv.reshape(b, l, QKV_W)
    f3 = f2.reshape(b, l, 128)
    bf_row = jnp.pad(b_forget, ((0, 0), (0, 128 - B_HEADS)))
    c_col, c_row = _fgate_fwd("fgate_fwd", f3, bf_row)
    head_of_row = jnp.arange(STACK) // BLOCK
    slopes = jnp.exp2(-8.0 * (head_of_row + 1).astype(F32) / A_HEADS).reshape(STACK, 1)
    sink_rows = jnp.repeat(sinks.reshape(A_HEADS), BLOCK).reshape(STACK, 1)
    swa_bias = _swa_bias(slopes)
    oa3, lse_a = _swa_fwd("swa_fwd", qkv3, slopes, sink_rows, swa_bias)
    ob3, lse_b = _fox_fwd("fox_fwd", qkv3, c_col, c_row)
    oa = oa3.reshape(t, A_WIDTH)
    ob = ob3.reshape(t, B_WIDTH)
    mixed, ya, yb = _branch_gate_fwd("branch_gate_fwd", oa, ob, wa, wb, gates, dep=weights("ffn2:forward", ob))
    h2, n2 = _mm_res_norm("mix_out_fwd", mixed, wo, h1, g2)
    w2i, w2o = weights("ffn2", h2)
    gu2, a2 = _ffn_in_fwd("ffn2_in_fwd", n2, w2i)

    dh3, dh3_bf, loss_blk, dgf = _ffn_out_loss("ffn2_out_loss", a2, w2o, h2, gf, tgt, 0.5)

    def ffn_bwd(tag, dh, dh_bf, h_in, g_norm, n_in, gu, a, w_in_blk, w_out, one_send, examples=None):
        dw_out = _ffn_out_bwd_w(tag + "_out_bwd_w", a, dh_bf)
        dgu = _ffn_out_bwd_x(tag + "_out_bwd_x", dh_bf, w_out, gu, dep=None if one_send else send(tag + "_out", (dw_out,)))
        dw_in = _ffn_in_bwd_w(tag + "_in_bwd_w", n_in, dgu)
        token = send(tag, (dw_in, dw_out)) if one_send else send(tag + "_in", (dw_in,))
        return _ffn_in_bwd_x(tag + "_in_bwd_x", dgu, w_in_blk, h_in, g_norm, dh, dep=token, examples=examples)

    dh2, dh2_bf, dg2 = ffn_bwd("ffn2", dh3, dh3_bf, h2, g2, n2, gu2, a2, w2i, w2o, True)

    dwo = _mm_tn("mix_out_bwd_w", mixed, dh2_bf)
    dya, dyb, dgates = _mix_out_gate_bwd("mix_out_gate_bwd", dh2_bf, wo, gates, ya, yb)
    doa, dob = _branch_bwd_x("branch_bwd_x", dya, dyb, wa, wb)
    dwa, dwb = _branch_bwd_w("branch_bwd_w", oa, ob, dya, dyb)
    dqkv3, dcq, dck = _fox_bwd("fox_bwd", qkv3, c_col, c_row, ob3, lse_b, dob.reshape(b, l, B_WIDTH))
    dqkv3, dsink = _swa_bwd("swa_bwd", qkv3, oa3, lse_a, doa.reshape(b, l, A_WIDTH), slopes, sink_rows, swa_bias, dqkv3)
    dqkv = dqkv3.reshape(t, QKV_W)
    df3, dbf = _fgate_bwd("fgate_bwd", f3, bf_row, dcq, dck)
    df = df3.reshape(t, 128)
    dwi_qkv = _mm_tn("proj_qkv_bwd_w", um, dqkv, tm_c=(512,), tn_c=(768,))
    dwi_g = _mm_tn("proj_gates_bwd_w", um, dgates, tm_c=(512,), tn_c=(512,))
    dwi_f = _mm_tn("proj_f_bwd_w", um, df, tm_c=(512,), tn_c=(128,))
    token = send("mix", (dwi_qkv, dwi_g, dwi_f, dwa, dwb, dwo))
    dh1, dh1_bf, dgm = _proj_bwd_x("proj_bwd_x", dqkv, dgates, df, wi, h1, gm, dh2, dep=token)

    grad_x, dmeta3, dg1 = ffn_bwd("ffn1", dh1, dh1_bf, h0, g1, n1, gu1, a1, w1i, w1o, False, examples=b)
    dmeta = dmeta3.reshape(b * N_META, D_MODEL)

    misc = jnp.concatenate([dbf[:, 0:B_HEADS], dsink[:, 0].reshape(1, A_HEADS), loss_blk[0:1, 0:1]], axis=1)
    misc = jnp.pad(misc, ((0, 0), (0, D_MODEL - misc.shape[1])))
    row = lax.broadcasted_iota(jnp.int32, (8, D_MODEL), 0)
    vec = jnp.zeros((8, D_MODEL), F32)
    for i, piece in enumerate((dg1, dgm, dg2, dgf, misc)):
        vec = jnp.where(row == i, piece, vec)
    small = jnp.concatenate([vec, dmeta], axis=0)
    return grad_x, small


def _pad_to(a, rows, cols):
    return jnp.pad(a, ((0, rows - a.shape[0]), (0, cols - a.shape[1])))


def _ffn_out_from_gathered(name, g):
    def body(g_ref, o_ref):
        o_ref[0:FFO_SHARD, :] = g_ref[0]
        o_ref[FFO_SHARD:FF_SHARD, :] = g_ref[1]
        o_ref[FF_SHARD:FF_SHARD_P, :] = jnp.zeros((FF_SHARD_P - FF_SHARD, D_MODEL), BF16)

    return pl.pallas_call(
        body, name=name, grid=(4,),
        in_specs=[pl.BlockSpec((2, FFO_SHARD, D_MODEL), lambda j: (j, 0, 0))],
        out_specs=pl.BlockSpec((FF_SHARD_P, D_MODEL), lambda j: (j, 0)),
        out_shape=jax.ShapeDtypeStruct((D_FF_P, D_MODEL), BF16), compiler_params=_params(("parallel",)),
    )(g)


def _ffn_out_bwd_w(name, a, dh):
    t = a.shape[0]
    tn = D_MODEL // 2

    def body(a_ref, b_ref, o_ref):
        acc = lax.dot_general(a_ref[...], _bf(b_ref[...]), _TN, preferred_element_type=F32) * 0.5
        o_ref[0] = acc[0:FFO_SHARD].astype(BF16)
        o_ref[1] = acc[FFO_SHARD:FF_SHARD].astype(BF16)

    return pl.pallas_call(
        body, name=name, grid=(2, 4),
        in_specs=[pl.BlockSpec((t, FF_SHARD_P), lambda j, i: (0, i)), pl.BlockSpec((t, tn), lambda j, i: (0, j))],
        out_specs=pl.BlockSpec((2, FFO_SHARD, tn), lambda j, i: (i, 0, j)),
        out_shape=jax.ShapeDtypeStruct((N_DEV, FFO_SHARD, D_MODEL), BF16), compiler_params=_params(("parallel", "parallel")),
    )(a, dh)


def _proj_segments():
    segs = [(HEAD_DIM * h, HEAD_DIM, 0, B_SEG + HEAD_DIM * A_HEAD_ORDER.index(h)) for h in range(A_HEADS)]
    segs += [(512, 128, 0, B_SEG + A_WIDTH), (640, 128, 0, B_SEG + A_WIDTH + 128)]
    for first, off in ((768, 0), (1280, 128), (1792, 256)):
        segs += [(first + 128 * hp, 128, 0, PAIR_W * hp + off) for hp in range(4)]
    segs += [(2304, B_HEADS, 2, 0), (2312, 2 * D_MODEL, 1, 0)]
    return segs


_RELAYOUT_ROWS = 256


def _proj_from_gathered(name, g):
    rows = _RELAYOUT_ROWS

    def body(g_ref, o_ref):
        def cols(first, width):
            out = []
            for p in range(N_DEV):
                lo, hi = max(first, WIN_SHARD * p), min(first + width, WIN_SHARD * (p + 1))
                if lo < hi:
                    out.append(g_ref[p, :, lo - WIN_SHARD * p:hi - WIN_SHARD * p])
            return out

        parts = []
        for arr in (0, 1, 2):
            for first, width, _, _ in sorted((s for s in _proj_segments() if s[2] == arr), key=lambda s: s[3]):
                parts += cols(first, width)
            if arr == 0:
                parts.append(jnp.zeros((rows, P_GATES - QKV_W), BF16))
        parts.append(jnp.zeros((rows, 128 - B_HEADS), BF16))
        o_ref[...] = jnp.concatenate(parts, axis=1)

    return pl.pallas_call(
        body, name=name, grid=(D_MODEL // rows,),
        in_specs=[pl.BlockSpec((N_DEV, rows, WIN_SHARD_P), lambda i: (0, i, 0))],
        out_specs=pl.BlockSpec((rows, PROJ_P), lambda i: (i, 0)),
        out_shape=jax.ShapeDtypeStruct((D_MODEL, PROJ_P), BF16), compiler_params=_params(("parallel",)),
    )(g)


def _proj_to_scatter(name, dqkv_w, dg_w, df_w):
    rows = _RELAYOUT_ROWS
    segs = sorted(_proj_segments())

    def body(q_ref, g_ref, f_ref, o_ref):
        arrays = (q_ref, g_ref, f_ref)
        for p in range(N_DEV):
            parts = []
            for first, width, arr, at in segs:
                lo, hi = max(first, WIN_SHARD * p), min(first + width, WIN_SHARD * (p + 1))
                if lo < hi:
                    parts.append(arrays[arr][:, at + lo - first:at + hi - first])
            parts.append(jnp.zeros((rows, WIN_SHARD_P - WIN_SHARD), BF16))
            o_ref[p] = jnp.concatenate(parts, axis=1)

    return pl.pallas_call(
        body, name=name, grid=(D_MODEL // rows,),
        in_specs=[pl.BlockSpec((rows, QKV_W), lambda i: (i, 0)), pl.BlockSpec((rows, 2 * D_MODEL), lambda i: (i, 0)),
                  pl.BlockSpec((rows, 128), lambda i: (i, 0))],
        out_specs=pl.BlockSpec((N_DEV, rows, WIN_SHARD_P), lambda i: (0, i, 0)),
        out_shape=jax.ShapeDtypeStruct((N_DEV, D_MODEL, WIN_SHARD_P), BF16), compiler_params=_params(("parallel",)),
    )(dqkv_w, dg_w, df_w)


def _a_rows_from_natural(w):
    return jnp.concatenate([w[HEAD_DIM * h:HEAD_DIM * (h + 1)] for h in A_HEAD_ORDER], axis=0)


def _a_rows_to_natural(w):
    return jnp.concatenate([w[HEAD_DIM * A_HEAD_ORDER.index(h):HEAD_DIM * (A_HEAD_ORDER.index(h) + 1)]
                            for h in range(A_HEADS)], axis=0)


def kernel(x, meta_tokens, ffn1_norm, ffn1_w_in, ffn1_w_out, mix_norm, w_in, b_forget, attn_sinks, w_branch_a, w_branch_b, w_out, ffn2_norm, ffn2_w_in, ffn2_w_out, final_norm, loss_target, m_meta_tokens, m_ffn1_norm, m_ffn1_w_in, m_ffn1_w_out, m_mix_norm, m_w_in, m_b_forget, m_attn_sinks, m_w_branch_a, m_w_branch_b, m_w_out, m_ffn2_norm, m_ffn2_w_in, m_ffn2_w_out, m_final_norm, v_meta_tokens, v_ffn1_norm, v_ffn1_w_in, v_ffn1_w_out, v_mix_norm, v_w_in, v_b_forget, v_attn_sinks, v_w_branch_a, v_w_branch_b, v_w_out, v_ffn2_norm, v_ffn2_w_in, v_ffn2_w_out, v_final_norm):
    me = 4 * lax.axis_index("x") + 2 * lax.axis_index("y") + lax.axis_index("c")

    def shard(w, tok, rows=None, cols=None):
        piece = (w[0] if tok is None else w[0] + tok[0, 0]).astype(BF16)
        return piece if rows is None else _pad_to(piece, rows, cols)

    first_level, second_level = {}, {}
    first_level["meta"], tok = _gather2_start("gather_meta_start", (meta_tokens,))
    first_level["ffn1_in"], tok = _gather2_start(
        "gather_ffn1_in_start", (shard(ffn1_w_in, None, D_MODEL, FF_SHARD_P),), after=tok)
    first_level["ffn1_out"], tok = _gather2_start("gather_ffn1_out_start", (shard(ffn1_w_out, tok),), after=tok)
    first_level["mix"], tok = _gather2_start(
        "gather_mix_start", (shard(w_in, tok, D_MODEL, WIN_SHARD_P), shard(w_branch_a, tok), shard(w_branch_b, tok),
                             shard(w_out, tok)), after=tok)
    first_level["ffn2"], tok = _gather2_start(
        "gather_ffn2_start", (shard(ffn2_w_in, tok, D_MODEL, FF_SHARD_P), shard(ffn2_w_out, tok)), after=tok)
    started = {"tok": tok}

    def weights(group, after):
        if group.endswith(":forward"):
            group = group[:-len(":forward")]
            second_level[group], token = _gather2_forward("gather_" + group + "_forward", first_level[group], after)
            return token
        if group == "meta":
            after = weights("meta:forward", started["tok"])
        if group == "ffn1_in":
            after = weights("ffn1_in:forward", after)
        got = _gather2_wait("gather_" + group + "_wait", second_level[group], after)
        if group == "mix":
            gwi, gwa, gwb, gwo = got
            return (_proj_from_gathered("proj_w_relayout", gwi), _a_rows_from_natural(gwa.transpose(1, 0, 2).reshape(A_WIDTH, D_MODEL)),
                    gwb.transpose(1, 0, 2).reshape(B_WIDTH, D_MODEL), gwo.reshape(D_MODEL, D_MODEL))
        if group == "ffn1_out":
            return (_ffn_out_from_gathered("ffn1_w_out_relayout", got[0]),)
        if group == "meta":
            return (got[0].transpose(1, 0, 2).reshape(N_META, D_MODEL),)
        if group == "ffn1_in":
            return (got[0].reshape(2, 4, D_MODEL, FF_SHARD_P),)
        return got[0].reshape(2, 4, D_MODEL, FF_SHARD_P), _ffn_out_from_gathered("ffn2_w_out_relayout", got[1])

    scatter_state = {}

    def send(group, grads):
        if group == "mix":
            dwi_qkv, dwi_g, dwi_f, dwa, dwb, dwo = grads
            dwa = _a_rows_to_natural(dwa)
            blocks = (_proj_to_scatter("proj_dw_relayout", dwi_qkv, dwi_g, dwi_f), dwa.reshape(A_WIDTH, N_DEV, 128).transpose(1, 0, 2),
                      dwb.reshape(B_WIDTH, N_DEV, 128).transpose(1, 0, 2), dwo.reshape(N_DEV, 128, D_MODEL))
        elif group.endswith("_in"):
            blocks = (grads[0].reshape(N_DEV, D_MODEL, FF_SHARD_P),)
        elif group.endswith("_out"):
            blocks = (grads[0],)
        else:
            blocks = (grads[0].reshape(N_DEV, D_MODEL, FF_SHARD_P), grads[1])
        scatter_state[group], token = _xchg_start("scatter_" + group + "_start", (), blocks)
        return token

    gf = final_norm.reshape(1, D_MODEL)
    grad_x, small = _local_step(x, loss_target, ffn1_norm, mix_norm, ffn2_norm, gf, b_forget, attn_sinks, weights, send)

    small_state, after = _xchg_start("gather_small_start", (small,), ())
    out = {}
    updates = (
        ("ffn2", (("ffn2_w_in", ffn2_w_in, m_ffn2_w_in, v_ffn2_w_in), ("ffn2_w_out", ffn2_w_out, m_ffn2_w_out, v_ffn2_w_out))),
        ("ffn1_out", (("ffn1_w_out", ffn1_w_out, m_ffn1_w_out, v_ffn1_w_out),)),
        ("mix", (("w_in", w_in, m_w_in, v_w_in), ("w_branch_a", w_branch_a, m_w_branch_a, v_w_branch_a),
                 ("w_branch_b", w_branch_b, m_w_branch_b, v_w_branch_b), ("w_out", w_out, m_w_out, v_w_out))),
    )
    last_update = ("ffn1_in", (("ffn1_w_in", ffn1_w_in, m_ffn1_w_in, v_ffn1_w_in),))

    def update(group, members, after):
        parts_list, blocks_list = _xchg_wait("scatter_" + group + "_wait", scatter_state[group], after)
        prev = None
        for (nm, w, m, v), parts, blocks in zip(members, parts_list, blocks_list):
            if nm.endswith("w_in") and w.shape[2] % 8:
                rows = lambda a: a[0].T.reshape(a.shape[2], 8, 128)
                res4 = _adam_rows("adam_" + nm, rows(w), rows(m), rows(v), parts, prev, (blocks, me_arr))
                out[nm] = tuple(r.reshape(r.shape[0], D_MODEL).T[None] for r in res4)
            elif nm.endswith("w_in"):
                res4 = _adam("adam_" + nm, w[0].T, m[0].T, v[0].T, parts, transposed=True, dep=prev, own=(blocks, me_arr))
                out[nm] = tuple(r.T[None] for r in res4)
            else:
                res4 = _adam("adam_" + nm, w[0], m[0], v[0], parts, dep=prev, own=(blocks, me_arr))
                out[nm] = tuple(r[None] for r in res4)
            prev = res4[0]
        return prev

    me_arr = me.reshape(1).astype(jnp.int32)
    for group, members in updates + (last_update,):
        after = update(group, members, after)
    (packs,), _ = _xchg_wait("gather_small_wait", small_state, after)

    row = lambda a: a.reshape(1, D_MODEL)
    tot, small_res = _small_update("small_update", packs, (
        (ffn1_norm, m_ffn1_norm, v_ffn1_norm), (mix_norm, m_mix_norm, v_mix_norm), (ffn2_norm, m_ffn2_norm, v_ffn2_norm),
        (row(final_norm), row(m_final_norm), row(v_final_norm)), (b_forget, m_b_forget, v_b_forget),
        (attn_sinks, m_attn_sinks, v_attn_sinks)))
    for nm, res4 in zip(("ffn1_norm", "mix_norm", "ffn2_norm", "final_norm", "b_forget", "attn_sinks"), small_res):
        out[nm] = tuple(r.reshape(D_MODEL) for r in res4) if nm == "final_norm" else res4
    loss = tot[4, 2 * B_HEADS]
    g_meta = lax.dynamic_slice(tot[8:24, :], (0, me * 128), (N_META, 128))
    out["meta_tokens"] = tuple(_adam("adam_meta_tokens", meta_tokens, m_meta_tokens, v_meta_tokens, g_meta[None]))

    names = ("meta_tokens", "ffn1_norm", "ffn1_w_in", "ffn1_w_out", "mix_norm", "w_in", "b_forget", "attn_sinks",
             "w_branch_a", "w_branch_b", "w_out", "ffn2_norm", "ffn2_w_in", "ffn2_w_out", "final_norm")
    return (loss, grad_x) + tuple(out[nm][kind] for kind in range(4) for nm in names)
```

```python
import jax
import jax.numpy as jnp
from jax import lax
from jax.experimental import pallas as pl
from jax.experimental.pallas import tpu as pltpu

F32 = jnp.float32
BF16 = jnp.bfloat16

D_MODEL = 1024
N_META = 16
BLOCK = 128
PREFIX = 128
N_PAD = PREFIX - N_META
HEAD_DIM = 64
A_HEADS = 8
B_HEADS = 8
A_WIDTH = 512
A_KV_WIDTH = 128
B_WIDTH = 512
D_FF = 2816
N_DEV = 8
FF_SHARD = 2 * D_FF // N_DEV
FF_SHARD_P = 768
FFO_SHARD = D_FF // N_DEV
D_FF_P = 4 * FF_SHARD_P
W_IN_COLS = 4360
WIN_SHARD = W_IN_COLS // N_DEV
WIN_SHARD_P = 640
PAIR_W = 3 * 128
B_SEG = 4 * PAIR_W
A_SEG = A_WIDTH + 2 * A_KV_WIDTH
QKV_W = B_SEG + A_SEG
P_GATES = 2 * (2 * D_MODEL)
P_F = P_GATES + 2 * D_MODEL
PROJ_P = P_F + 128
A_HEAD_ORDER = (0, 4, 1, 5, 2, 6, 3, 7)
EPS = 1e-6
NEG = -1e30
SCALE = HEAD_DIM ** -0.5
ADAM_LR = 0.001
ADAM_B1 = 0.9
ADAM_B2 = 0.999
ADAM_EPS = 1e-08
ADAM_WD = 0.01
ADAM_STEP = 10
VMEM_LIMIT = 56 * 1024 * 1024
MESH_ID = pl.DeviceIdType.MESH
SMALL_ROWS = 40

_NN = (((1,), (0,)), ((), ()))
_NT = (((1,), (1,)), ((), ()))
_TN = (((0,), (0,)), ((), ()))


def _params(sem=None):
    return pltpu.CompilerParams(dimension_semantics=sem, vmem_limit_bytes=VMEM_LIMIT)


def _pick(n, cands):
    for c in cands:
        if n % c == 0:
            return c
    raise ValueError(f"no tile for {n}")


def _bf(v):
    return v if v.dtype == BF16 else v.astype(BF16)


def _mm(name, a, b, dims, grid, a_spec, b_spec, o_spec, out_shape, out_dtype, alpha=1.0):
    def body(a_ref, b_ref, o_ref):
        acc = lax.dot_general(_bf(a_ref[...]), _bf(b_ref[...]), dims, preferred_element_type=F32)
        if alpha != 1.0:
            acc = acc * alpha
        o_ref[...] = acc.astype(o_ref.dtype)

    return pl.pallas_call(
        body, name=name, grid=grid, in_specs=[a_spec, b_spec], out_specs=o_spec,
        out_shape=jax.ShapeDtypeStruct(out_shape, out_dtype),
        compiler_params=_params(("parallel",) * len(grid)),
    )(a, b)


def _mm_res_norm(name, a, w, res, g_next, alpha=1.0, dep=None):
    t, k = a.shape
    tm = _pick(t, (544, 384, 256, 128))
    has_dep = dep is not None

    def body(a_ref, w_ref, r_ref, g_ref, *rest):
        h_ref, n_ref = rest[-2], rest[-1]
        acc = lax.dot_general(_bf(a_ref[...]), w_ref[...], _NN, preferred_element_type=F32)
        if alpha != 1.0:
            acc = acc * alpha
        hv = acc + r_ref[...]
        h_ref[...] = hv
        r = lax.rsqrt(jnp.mean(hv * hv, axis=-1, keepdims=True) + EPS)
        n_ref[...] = ((hv * r) * g_ref[...]).astype(BF16)

    row = pl.BlockSpec((tm, D_MODEL), lambda i: (i, 0))
    return pl.pallas_call(
        body, name=name, grid=(t // tm,),
        in_specs=[pl.BlockSpec((tm, k), lambda i: (i, 0)), pl.BlockSpec((k, D_MODEL), lambda i: (0, 0)), row,
                  pl.BlockSpec((1, D_MODEL), lambda i: (0, 0))] + ([pl.BlockSpec(memory_space=pl.ANY)] if has_dep else []),
        out_specs=[row, row],
        out_shape=[jax.ShapeDtypeStruct((t, D_MODEL), F32), jax.ShapeDtypeStruct((t, D_MODEL), BF16)],
        compiler_params=_params(("parallel",)),
    )(*((a, w, res, g_next) + ((dep,) if has_dep else ())))


def _mm_tn(name, a, b, out_dtype=BF16, alpha=1.0, tm_c=(768, 512, 256, 128), tn_c=(512, 640, 256, 128)):
    t, m = a.shape
    n = b.shape[1]
    tm = _pick(m, tm_c)
    tn = _pick(n, tn_c)
    bytes_a, bytes_b = a.size * a.dtype.itemsize, b.size * b.dtype.itemsize
    if bytes_a + bytes_b * (m // tm) <= bytes_b + bytes_a * (n // tn):
        return _mm(name, a, b, _TN, (m // tm, n // tn),
                   pl.BlockSpec((t, tm), lambda i, j: (0, i)), pl.BlockSpec((t, tn), lambda i, j: (0, j)),
                   pl.BlockSpec((tm, tn), lambda i, j: (i, j)), (m, n), out_dtype, alpha=alpha)
    return _mm(name, a, b, _TN, (n // tn, m // tm),
               pl.BlockSpec((t, tm), lambda j, i: (0, i)), pl.BlockSpec((t, tn), lambda j, i: (0, j)),
               pl.BlockSpec((tm, tn), lambda j, i: (i, j)), (m, n), out_dtype, alpha=alpha)


def _ffn_in_fwd(name, n, wblk, dep=None):
    t = n.shape[0]
    tm = _pick(t, (1088, 768, 512, 256, 128))
    has_dep = dep is not None

    def body(n_ref, w_ref, *rest):
        gu_ref, a_ref = rest[-2], rest[-1]
        nv = n_ref[...]
        g = lax.dot_general(nv, w_ref[0], _NN, preferred_element_type=F32)
        u = lax.dot_general(nv, w_ref[1], _NN, preferred_element_type=F32)
        sg = jax.nn.sigmoid(g)
        silu = g * sg
        a_ref[...] = (silu * u).astype(BF16)
        gu_ref[0] = ((0.5 * u) * (sg + silu * (1.0 - sg))).astype(BF16)
        gu_ref[1] = (0.5 * silu).astype(BF16)

    return pl.pallas_call(
        body, name=name, grid=(t // tm, 4),
        in_specs=[pl.BlockSpec((tm, D_MODEL), lambda i, j: (i, 0)),
                  pl.BlockSpec((2, None, D_MODEL, FF_SHARD_P), lambda i, j: (0, j, 0, 0))]
        + ([pl.BlockSpec(memory_space=pl.ANY)] if has_dep else []),
        out_specs=[pl.BlockSpec((2, tm, FF_SHARD_P), lambda i, j: (0, i, j)),
                   pl.BlockSpec((tm, FF_SHARD_P), lambda i, j: (i, j))],
        out_shape=[jax.ShapeDtypeStruct((2, t, D_FF_P), BF16), jax.ShapeDtypeStruct((t, D_FF_P), BF16)],
        compiler_params=_params(("parallel", "parallel")),
    )(*((n, wblk) + ((dep,) if has_dep else ())))


def _ffn_out_bwd_x(name, dh, w_out, gu, dep=None):
    t = dh.shape[0]
    tm = _pick(t, (1088, 768, 512, 256, 128))
    has_dep = dep is not None

    def body(dh_ref, w_ref, gu_ref, *rest):
        o_ref = rest[-1]
        w = w_ref[pl.ds(pl.multiple_of(pl.program_id(1) * FF_SHARD_P, FF_SHARD_P), FF_SHARD_P), :]
        da = lax.dot_general(_bf(dh_ref[...]), w, _NT, preferred_element_type=F32)
        o_ref[0] = (da * gu_ref[0].astype(F32)).astype(BF16)
        o_ref[1] = (da * gu_ref[1].astype(F32)).astype(BF16)

    gu_spec = pl.BlockSpec((2, tm, FF_SHARD_P), lambda i, j: (0, i, j))
    return pl.pallas_call(
        body, name=name, grid=(t // tm, 4),
        in_specs=[pl.BlockSpec((tm, D_MODEL), lambda i, j: (i, 0)),
                  pl.BlockSpec((D_FF_P, D_MODEL), lambda i, j: (0, 0), pipeline_mode=pl.Buffered(1)),
                  gu_spec] + ([pl.BlockSpec(memory_space=pl.ANY)] if has_dep else []),
        out_specs=gu_spec, out_shape=jax.ShapeDtypeStruct((2, t, D_FF_P), BF16),
        compiler_params=_params(("parallel", "parallel")),
    )(*((dh, w_out, gu) + ((dep,) if has_dep else ())))


def _rms_bwd_rows(dn, h, g, dres):
    r = lax.rsqrt(jnp.mean(h * h, axis=-1, keepdims=True) + EPS)
    tv = dn * g
    dot = jnp.mean(tv * h, axis=-1, keepdims=True)
    return dres + (r * tv - h * (r * r * r * dot)), jnp.sum(dn * (h * r), axis=0, keepdims=True)


def _accumulate_rows(ref, part, first):
    @pl.when(first)
    def _():
        ref[...] = part

    @pl.when(jnp.logical_not(first))
    def _():
        ref[...] += part


def _ffn_in_bwd_x(name, dgu, wblk, h_in, g_norm, dres, dep=None, examples=None):
    t = dgu.shape[1]
    tm = _pick(t, (544, 384, 256, 128))
    has_dep = dep is not None
    split = examples is not None
    if split:
        l = t // examples
        per = l // tm
        assert per * tm == l and tm > PREFIX

    def body(d_ref, w_ref, h_ref, g_ref, r_ref, *rest):
        acc = None
        for s in range(2):
            for j in range(4):
                part = lax.dot_general(d_ref[s, :, FF_SHARD_P * j:FF_SHARD_P * (j + 1)], w_ref[s, j], _NT,
                                       preferred_element_type=F32)
                acc = part if acc is None else acc + part
        dh, dg = _rms_bwd_rows(acc, h_ref[...], g_ref[...], r_ref[...])
        i = pl.program_id(0)
        if not split:
            dh_ref, dhb_ref, dg_ref = rest[-3:]
            dh_ref[...] = dh
            dhb_ref[...] = dh.astype(BF16)
        else:
            gx_ref, meta_ref, dg_ref, buf, sem = rest[-5:]

            def out_copy(step):
                bi, r = step // per, step % per
                head = pltpu.make_async_copy(buf.at[pl.ds(PREFIX, tm - PREFIX)], gx_ref.at[bi, pl.ds(0, tm - PREFIX)], sem)
                if per == 1:
                    return r == 0, head, None
                return r == 0, head, pltpu.make_async_copy(
                    buf, gx_ref.at[bi, pl.ds(pl.multiple_of(jnp.maximum(r, 1) * tm - PREFIX, 8), tm)], sem)

            def run(step, method):
                is_head, head, later = out_copy(step)

                @pl.when(is_head)
                def _():
                    getattr(head, method)()

                if later is not None:
                    @pl.when(jnp.logical_not(is_head))
                    def _():
                        getattr(later, method)()

            @pl.when(i > 0)
            def _():
                run(i - 1, "wait")

            buf[...] = dh

            @pl.when(i % per == 0)
            def _():
                meta_ref[...] = dh[N_PAD:PREFIX]

            run(i, "start")

            @pl.when(i == pl.num_programs(0) - 1)
            def _():
                run(i, "wait")

        _accumulate_rows(dg_ref, dg, i == 0)

    row = pl.BlockSpec((tm, D_MODEL), lambda i: (i, 0))
    vec = pl.BlockSpec((1, D_MODEL), lambda i: (0, 0))
    if split:
        out_specs = [pl.BlockSpec(memory_space=pl.ANY), pl.BlockSpec((None, N_META, D_MODEL), lambda i: (i // per, 0, 0)), vec]
        out_shape = [jax.ShapeDtypeStruct((examples, l - PREFIX, D_MODEL), F32),
                     jax.ShapeDtypeStruct((examples, N_META, D_MODEL), F32), jax.ShapeDtypeStruct((1, D_MODEL), F32)]
        scratch = [pltpu.VMEM((tm, D_MODEL), F32), pltpu.SemaphoreType.DMA(())]
    else:
        out_specs = [row, row, vec]
        out_shape = [jax.ShapeDtypeStruct((t, D_MODEL), F32), jax.ShapeDtypeStruct((t, D_MODEL), BF16),
                     jax.ShapeDtypeStruct((1, D_MODEL), F32)]
        scratch = []
    return pl.pallas_call(
        body, name=name, grid=(t // tm,),
        in_specs=[pl.BlockSpec((2, tm, D_FF_P), lambda i: (0, i, 0)),
                  pl.BlockSpec((2, 4, D_MODEL, FF_SHARD_P), lambda i: (0, 0, 0, 0), pipeline_mode=pl.Buffered(1)),
                  row, vec, row]
        + ([pl.BlockSpec(memory_space=pl.ANY)] if has_dep else []),
        out_specs=out_specs, out_shape=out_shape, scratch_shapes=scratch,
        compiler_params=_params(("arbitrary",)),
    )(*((dgu, wblk, h_in, g_norm, dres) + ((dep,) if has_dep else ())))


def _proj_fwd(name, um, wi, dep=None):
    t = um.shape[0]
    tm = _pick(t, (544, 384, 256, 128))

    has_dep = dep is not None

    def body(u_ref, w_ref, *rest):
        q_ref, g_ref, f_ref = rest[-3:]
        uv = u_ref[...]
        q_ref[...] = lax.dot_general(uv, w_ref[:, 0:QKV_W], _NN, preferred_element_type=F32).astype(BF16)
        g_ref[...] = lax.dot_general(uv, w_ref[:, P_GATES:P_F], _NN, preferred_element_type=F32).astype(BF16)
        f_ref[...] = lax.dot_general(uv, w_ref[:, P_F:PROJ_P], _NN, preferred_element_type=F32)

    return pl.pallas_call(
        body, name=name, grid=(t // tm,),
        in_specs=[pl.BlockSpec((tm, D_MODEL), lambda i: (i, 0)),
                  pl.BlockSpec((D_MODEL, PROJ_P), lambda i: (0, 0), pipeline_mode=pl.Buffered(1))]
        + ([pl.BlockSpec(memory_space=pl.ANY)] if has_dep else []),
        out_specs=[pl.BlockSpec((tm, QKV_W), lambda i: (i, 0)), pl.BlockSpec((tm, 2 * D_MODEL), lambda i: (i, 0)),
                   pl.BlockSpec((tm, 128), lambda i: (i, 0))],
        out_shape=[jax.ShapeDtypeStruct((t, QKV_W), BF16), jax.ShapeDtypeStruct((t, 2 * D_MODEL), BF16),
                   jax.ShapeDtypeStruct((t, 128), F32)],
        compiler_params=_params(("parallel",)),
    )(*((um, wi) + ((dep,) if has_dep else ())))


def _proj_bwd_x(name, dqkv, dgates, df, wi, h_in, g_norm, dres, dep=None):
    t = dqkv.shape[0]
    tm = _pick(t, (544, 384, 256, 128))
    has_dep = dep is not None

    def body(q_ref, gt_ref, f_ref, w_ref, h_ref, g_ref, r_ref, *rest):
        dh_ref, dhb_ref, dg_ref = rest[-3:]
        acc = lax.dot_general(q_ref[...], w_ref[:, 0:QKV_W], _NT, preferred_element_type=F32)
        acc = acc + lax.dot_general(gt_ref[...], w_ref[:, P_GATES:P_F], _NT, preferred_element_type=F32)
        acc = acc + lax.dot_general(f_ref[...], w_ref[:, P_F:PROJ_P], _NT, preferred_element_type=F32)
        dh, dg = _rms_bwd_rows(acc, h_ref[...], g_ref[...], r_ref[...])
        dh_ref[...] = dh
        dhb_ref[...] = dh.astype(BF16)
        _accumulate_rows(dg_ref, dg, pl.program_id(0) == 0)

    row = pl.BlockSpec((tm, D_MODEL), lambda i: (i, 0))
    vec = pl.BlockSpec((1, D_MODEL), lambda i: (0, 0))
    return pl.pallas_call(
        body, name=name, grid=(t // tm,),
        in_specs=[pl.BlockSpec((tm, QKV_W), lambda i: (i, 0)), pl.BlockSpec((tm, 2 * D_MODEL), lambda i: (i, 0)),
                  pl.BlockSpec((tm, 128), lambda i: (i, 0)),
                  pl.BlockSpec((D_MODEL, PROJ_P), lambda i: (0, 0), pipeline_mode=pl.Buffered(1)), row, vec, row]
        + ([pl.BlockSpec(memory_space=pl.ANY)] if has_dep else []),
        out_specs=[row, row, vec],
        out_shape=[jax.ShapeDtypeStruct((t, D_MODEL), F32), jax.ShapeDtypeStruct((t, D_MODEL), BF16),
                   jax.ShapeDtypeStruct((1, D_MODEL), F32)],
        compiler_params=_params(("arbitrary",)),
    )(*((dqkv, dgates, df, wi, h_in, g_norm, dres) + ((dep,) if has_dep else ())))


def _ffn_in_bwd_w(name, n, dgu):
    t = n.shape[0]
    return _mm(name, n, dgu, _TN, (2, 4),
               pl.BlockSpec((t, D_MODEL), lambda s, j: (0, 0)),
               pl.BlockSpec((None, t, FF_SHARD_P), lambda s, j: (s, 0, j)),
               pl.BlockSpec((None, None, D_MODEL, FF_SHARD_P), lambda s, j: (s, j, 0, 0)),
               (2, 4, D_MODEL, FF_SHARD_P), BF16)


def _embed_norm(name, x, meta, g):
    b, s, _ = x.shape
    half = (s + PREFIX) // 2
    first = half - PREFIX
    assert first > 0 and half % 16 == 0

    def body(x_ref, m_ref, g_ref, h_ref, n_ref, buf, sem):
        bi, k = pl.program_id(0), pl.program_id(1)

        def tokens(example, second, method):
            if second:
                cp = pltpu.make_async_copy(x_ref.at[example, pl.ds(first, half)], buf.at[1], sem.at[1])
            else:
                cp = pltpu.make_async_copy(x_ref.at[example, pl.ds(0, first)], buf.at[0, pl.ds(PREFIX, first)], sem.at[0])
            getattr(cp, method)()

        @pl.when((bi == 0) & (k == 0))
        def _():
            tokens(0, False, "start")

        @pl.when(k == 0)
        def _():
            tokens(bi, True, "start")
            tokens(bi, False, "wait")
            buf[0, 0:N_PAD, :] = jnp.zeros((N_PAD, D_MODEL), F32)
            buf[0, N_PAD:PREFIX, :] = m_ref[...]

        @pl.when(k == 1)
        def _():
            @pl.when(bi + 1 < b)
            def _():
                tokens(bi + 1, False, "start")

            tokens(bi, True, "wait")

        hv = buf[k]
        h_ref[...] = hv
        r = lax.rsqrt(jnp.mean(hv * hv, axis=-1, keepdims=True) + EPS)
        n_ref[...] = ((hv * r) * g_ref[...]).astype(BF16)

    rows = pl.BlockSpec((half, D_MODEL), lambda bi, k: (2 * bi + k, 0))
    return pl.pallas_call(
        body, name=name, grid=(b, 2),
        in_specs=[pl.BlockSpec(memory_space=pl.ANY), pl.BlockSpec((N_META, D_MODEL), lambda bi, k: (0, 0)),
                  pl.BlockSpec((1, D_MODEL), lambda bi, k: (0, 0))],
        out_specs=[rows, rows],
        out_shape=[jax.ShapeDtypeStruct((2 * b * half, D_MODEL), F32), jax.ShapeDtypeStruct((2 * b * half, D_MODEL), BF16)],
        scratch_shapes=[pltpu.VMEM((2, half, D_MODEL), F32), pltpu.SemaphoreType.DMA((2,))],
        compiler_params=_params(("arbitrary", "arbitrary")),
    )(x, meta, g)


def _branch_gate_fwd(name, oa, ob, wa, wb, gates, dep=None):
    t = gates.shape[0]
    tm = _pick(t, (544, 384, 256, 128))
    has_dep = dep is not None

    def body(oa_ref, ob_ref, wa_ref, wb_ref, g_ref, *rest):
        o_ref, ya_ref, yb_ref = rest[-3:]
        ya = lax.dot_general(_bf(oa_ref[...]), wa_ref[...], _NN, preferred_element_type=F32)
        yb = lax.dot_general(_bf(ob_ref[...]), wb_ref[...], _NN, preferred_element_type=F32)
        sa = jax.nn.sigmoid(g_ref[:, 0:D_MODEL].astype(F32))
        sb = jax.nn.sigmoid(g_ref[:, D_MODEL:2 * D_MODEL].astype(F32))
        o_ref[...] = (sa * ya + sb * yb).astype(BF16)
        ya_ref[...] = ya.astype(BF16)
        yb_ref[...] = yb.astype(BF16)

    blk = pl.BlockSpec((tm, D_MODEL), lambda i: (i, 0))
    narrow = pl.BlockSpec((tm, A_WIDTH), lambda i: (i, 0))
    wide = pl.BlockSpec((tm, 2 * D_MODEL), lambda i: (i, 0))
    wspec = pl.BlockSpec((A_WIDTH, D_MODEL), lambda i: (0, 0))
    out = jax.ShapeDtypeStruct((t, D_MODEL), BF16)
    return pl.pallas_call(
        body, name=name, grid=(t // tm,),
        in_specs=[narrow, narrow, wspec, wspec, wide] + ([pl.BlockSpec(memory_space=pl.ANY)] if has_dep else []),
        out_specs=[blk, blk, blk], out_shape=[out, out, out], compiler_params=_params(("parallel",)),
    )(*((oa, ob, wa, wb, gates) + ((dep,) if has_dep else ())))


def _branch_bwd_x(name, dya, dyb, wa, wb):
    t = dya.shape[0]
    tm = _pick(t, (1088, 768, 512, 256, 128))

    def body(da_ref, db_ref, wa_ref, wb_ref, oa_ref, ob_ref):
        oa_ref[...] = lax.dot_general(da_ref[...], wa_ref[...], _NT, preferred_element_type=F32)
        ob_ref[...] = lax.dot_general(db_ref[...], wb_ref[...], _NT, preferred_element_type=F32).astype(BF16)

    blk = pl.BlockSpec((tm, D_MODEL), lambda i: (i, 0))
    narrow = pl.BlockSpec((tm, A_WIDTH), lambda i: (i, 0))
    wspec = pl.BlockSpec((A_WIDTH, D_MODEL), lambda i: (0, 0), pipeline_mode=pl.Buffered(1))
    return pl.pallas_call(
        body, name=name, grid=(t // tm,), in_specs=[blk, blk, wspec, wspec], out_specs=[narrow, narrow],
        out_shape=[jax.ShapeDtypeStruct((t, A_WIDTH), F32), jax.ShapeDtypeStruct((t, B_WIDTH), BF16)],
        compiler_params=_params(("parallel",)),
    )(dya, dyb, wa, wb)


def _branch_bwd_w(name, oa, ob, dya, dyb):
    t = oa.shape[0]
    tn = 512

    def body(oa_ref, ob_ref, da_ref, db_ref, wa_ref, wb_ref):
        wa_ref[...] = lax.dot_general(_bf(oa_ref[...]), da_ref[...], _TN, preferred_element_type=F32).astype(BF16)
        wb_ref[...] = lax.dot_general(_bf(ob_ref[...]), db_ref[...], _TN, preferred_element_type=F32).astype(BF16)

    whole = pl.BlockSpec((t, A_WIDTH), lambda j: (0, 0), pipeline_mode=pl.Buffered(1))
    cols = pl.BlockSpec((t, tn), lambda j: (0, j))
    out_spec = pl.BlockSpec((A_WIDTH, tn), lambda j: (0, j))
    out = jax.ShapeDtypeStruct((A_WIDTH, D_MODEL), BF16)
    return pl.pallas_call(
        body, name=name, grid=(D_MODEL // tn,), in_specs=[whole, whole, cols, cols], out_specs=[out_spec, out_spec],
        out_shape=[out, out], compiler_params=_params(("parallel",)),
    )(oa, ob, dya, dyb)


def _mix_out_gate_bwd(name, dh, wo, gates, ya, yb):
    t = gates.shape[0]
    tm = _pick(t, (544, 384, 256, 128))

    def body(dh_ref, w_ref, g_ref, ya_ref, yb_ref, dya_ref, dyb_ref, dg_ref):
        dm = lax.dot_general(_bf(dh_ref[...]), w_ref[...], _NT, preferred_element_type=F32)
        sa = jax.nn.sigmoid(g_ref[:, 0:D_MODEL].astype(F32))
        sb = jax.nn.sigmoid(g_ref[:, D_MODEL:2 * D_MODEL].astype(F32))
        dya_ref[...] = (dm * sa).astype(BF16)
        dyb_ref[...] = (dm * sb).astype(BF16)
        dg_ref[:, 0:D_MODEL] = (dm * ya_ref[...].astype(F32) * (sa * (1.0 - sa))).astype(BF16)
        dg_ref[:, D_MODEL:2 * D_MODEL] = (dm * yb_ref[...].astype(F32) * (sb * (1.0 - sb))).astype(BF16)

    blk = pl.BlockSpec((tm, D_MODEL), lambda i: (i, 0))
    wide = pl.BlockSpec((tm, 2 * D_MODEL), lambda i: (i, 0))
    out = jax.ShapeDtypeStruct((t, D_MODEL), BF16)
    return pl.pallas_call(
        body, name=name, grid=(t // tm,),
        in_specs=[blk, pl.BlockSpec((D_MODEL, D_MODEL), lambda i: (0, 0)), wide, blk, blk], out_specs=[blk, blk, wide],
        out_shape=[out, out, jax.ShapeDtypeStruct((t, 2 * D_MODEL), BF16)], compiler_params=_params(("parallel",)),
    )(dh, wo, gates, ya, yb)


def _ffn_out_loss(name, a, w, res, gf, tgt, alpha):
    t, k = a.shape
    b, s, _ = tgt.shape
    l = t // b
    tm = _pick(t, (544, 384, 256, 128))
    per = l // tm
    assert per * tm == l and tm > PREFIX and l - PREFIX == s

    def body(a_ref, w_ref, r_ref, g_ref, t_ref, dh_ref, dhb_ref, loss_ref, dg_ref, buf, sem):
        i = pl.program_id(0)
        bi, r = i // per, i % per

        def first_rows():
            return pltpu.make_async_copy(t_ref.at[bi, pl.ds(0, tm - PREFIX)], buf.at[pl.ds(PREFIX, tm - PREFIX)], sem)

        def later_rows():
            return pltpu.make_async_copy(t_ref.at[bi, pl.ds(pl.multiple_of(r * tm - PREFIX, 8), tm)], buf, sem)

        @pl.when(r == 0)
        def _():
            buf[0:PREFIX, :] = jnp.zeros((PREFIX, D_MODEL), F32)
            first_rows().start()

        if per > 1:
            @pl.when(r > 0)
            def _():
                later_rows().start()

        acc = lax.dot_general(a_ref[...], w_ref[...], _NN, preferred_element_type=F32)

        @pl.when(r == 0)
        def _():
            first_rows().wait()

        if per > 1:
            @pl.when(r > 0)
            def _():
                later_rows().wait()

        hv = acc * alpha + r_ref[...]
        row = lax.broadcasted_iota(jnp.int32, (tm, 1), 0)
        real = ((r > 0) | (row >= PREFIX)).astype(F32)
        g = g_ref[...]
        rn = lax.rsqrt(jnp.mean(hv * hv, axis=-1, keepdims=True) + EPS)
        xn = hv * rn
        err = (xn * g - buf[...]) * real
        lpart = 0.5 * jnp.sum(jnp.mean(err * err, axis=-1, keepdims=True), axis=0, keepdims=True)
        dy = err * (1.0 / D_MODEL)
        tv = dy * g
        dot = jnp.mean(tv * hv, axis=-1, keepdims=True)
        dh = rn * tv - hv * (rn * rn * rn * dot)
        dh_ref[...] = dh
        dhb_ref[...] = dh.astype(BF16)
        gpart = jnp.sum(dy * xn, axis=0, keepdims=True)

        @pl.when(i == 0)
        def _():
            loss_ref[...] = jnp.zeros_like(loss_ref)
            dg_ref[...] = jnp.zeros_like(dg_ref)

        loss_ref[...] += jnp.broadcast_to(lpart, loss_ref.shape)
        dg_ref[...] += gpart

    row_spec = pl.BlockSpec((tm, D_MODEL), lambda i: (i, 0))
    vec = pl.BlockSpec((1, D_MODEL), lambda i: (0, 0))
    return pl.pallas_call(
        body, name=name, grid=(t // tm,),
        in_specs=[pl.BlockSpec((tm, k), lambda i: (i, 0)),
                  pl.BlockSpec((k, D_MODEL), lambda i: (0, 0), pipeline_mode=pl.Buffered(1)), row_spec, vec,
                  pl.BlockSpec(memory_space=pl.ANY)],
        out_specs=[row_spec, row_spec, pl.BlockSpec((8, 128), lambda i: (0, 0)), vec],
        out_shape=[jax.ShapeDtypeStruct((t, D_MODEL), F32), jax.ShapeDtypeStruct((t, D_MODEL), BF16),
                   jax.ShapeDtypeStruct((8, 128), F32), jax.ShapeDtypeStruct((1, D_MODEL), F32)],
        scratch_shapes=[pltpu.VMEM((tm, D_MODEL), F32), pltpu.SemaphoreType.DMA(())],
        compiler_params=_params(("arbitrary",)),
    )(a, w, res, gf, tgt)


def _fgate_fwd(name, f3, bf_row):
    b, l, _ = f3.shape
    nb = l // BLOCK

    def body(f_ref, b_ref, cc_ref, cr_ref):
        r_i = lax.broadcasted_iota(jnp.int32, (BLOCK, BLOCK), 0)
        c_i = lax.broadcasted_iota(jnp.int32, (BLOCK, BLOCK), 1)
        tri = (r_i >= c_i).astype(F32)
        carry = jnp.zeros((1, 128), F32)
        for blk in range(nb):
            rows = slice(blk * BLOCK, (blk + 1) * BLOCK)
            z = f_ref[rows, :] + b_ref[...]
            lf = jnp.minimum(z, 0.0) - jnp.log(1.0 + jnp.exp(-jnp.abs(z)))
            cb = jnp.dot(tri, lf, preferred_element_type=F32, precision=lax.Precision.HIGHEST) + carry
            carry = cb[BLOCK - 1:BLOCK, :]
            cbt = cb.T
            for hh in range(B_HEADS):
                cc_ref[hh, rows, :] = jnp.sum(jnp.where(c_i == hh, cb, 0.0), axis=1, keepdims=True)
                cr_ref[hh, :, rows] = cbt[hh:hh + 1, :]

    return pl.pallas_call(
        body, name=name, grid=(b,),
        in_specs=[pl.BlockSpec((None, l, 128), lambda bi: (bi, 0, 0)),
                  pl.BlockSpec((1, 128), lambda bi: (0, 0))],
        out_specs=[pl.BlockSpec((None, B_HEADS, l, 1), lambda bi: (bi, 0, 0, 0)),
                   pl.BlockSpec((None, B_HEADS, 1, l), lambda bi: (bi, 0, 0, 0))],
        out_shape=[jax.ShapeDtypeStruct((b, B_HEADS, l, 1), F32), jax.ShapeDtypeStruct((b, B_HEADS, 1, l), F32)],
        compiler_params=_params(("parallel",)),
    )(f3, bf_row)


def _fgate_bwd(name, f3, bf_row, dcq, dck):
    b, l, _ = f3.shape
    nb = l // BLOCK

    def body(f_ref, b_ref, dcq_ref, dck_ref, df_ref, db_ref):
        r_i = lax.broadcasted_iota(jnp.int32, (BLOCK, BLOCK), 0)
        c_i = lax.broadcasted_iota(jnp.int32, (BLOCK, BLOCK), 1)
        tri = (r_i <= c_i).astype(F32)
        carry = jnp.zeros((1, 128), F32)
        total = jnp.zeros((1, 128), F32)
        for blk in range(nb - 1, -1, -1):
            rows = slice(blk * BLOCK, (blk + 1) * BLOCK)
            krows = jnp.concatenate([dck_ref[hh, :, rows] for hh in range(B_HEADS)]
                                    + [jnp.zeros((BLOCK - B_HEADS, BLOCK), F32)], axis=0)
            dcb = krows.T
            for hh in range(B_HEADS):
                dcb = dcb + jnp.where(c_i == hh, dcq_ref[hh, rows, :], 0.0)
            rc = jnp.dot(tri, dcb, preferred_element_type=F32, precision=lax.Precision.HIGHEST) + carry
            carry = rc[0:1, :]
            z = f_ref[rows, :] + b_ref[...]
            df = rc * (1.0 / (1.0 + jnp.exp(z)))
            df_ref[rows, :] = df.astype(BF16)
            total = total + jnp.sum(df, axis=0, keepdims=True)

        @pl.when(pl.program_id(0) == 0)
        def _():
            db_ref[...] = total

        @pl.when(pl.program_id(0) > 0)
        def _():
            db_ref[...] += total

    return pl.pallas_call(
        body, name=name, grid=(b,),
        in_specs=[pl.BlockSpec((None, l, 128), lambda bi: (bi, 0, 0)),
                  pl.BlockSpec((1, 128), lambda bi: (0, 0)),
                  pl.BlockSpec((None, B_HEADS, l, 1), lambda bi: (bi, 0, 0, 0)),
                  pl.BlockSpec((None, B_HEADS, 1, l), lambda bi: (bi, 0, 0, 0))],
        out_specs=[pl.BlockSpec((None, l, 128), lambda bi: (bi, 0, 0)), pl.BlockSpec((1, 128), lambda bi: (0, 0))],
        out_shape=[jax.ShapeDtypeStruct((b, l, 128), BF16), jax.ShapeDtypeStruct((1, 128), F32)],
        compiler_params=_params(("arbitrary",)),
    )(f3, bf_row, dcq, dck)


A_Q_BLK = B_SEG // A_WIDTH
A_K_BLK = (B_SEG + A_WIDTH) // 128
A_V_BLK = A_K_BLK + 1
A_SEG_BLK = B_SEG // A_SEG
STACK = A_HEADS * BLOCK


def _lane_lo():
    return lax.broadcasted_iota(jnp.int32, (1, 128), 1) < HEAD_DIM


def _stack_heads(x, masked):
    lo = _lane_lo()
    blks = [x[:, 128 * j:128 * (j + 1)] for j in range(4)]
    if not masked:
        return jnp.concatenate(blks + blks, axis=0)
    zero = jnp.zeros_like(blks[0])
    return jnp.concatenate([jnp.where(lo, bk, zero) for bk in blks] + [jnp.where(lo, zero, bk) for bk in blks], axis=0)


def _unstack_heads(y):
    lo = _lane_lo()
    return jnp.concatenate([jnp.where(lo, y[128 * j:128 * (j + 1)], y[128 * (4 + j):128 * (5 + j)]) for j in range(4)], axis=1)


def _swa_bias(slopes):
    r_i = jnp.arange(STACK)[:, None]
    c_i = jnp.arange(3 * BLOCK)[None, :]
    seg = c_i >> 7
    out = []
    for n in range(3):
        qpos = n * BLOCK + (r_i & (BLOCK - 1))
        kpos = jnp.where(seg == 0, c_i, (n - 2) * BLOCK + c_i)
        dist = qpos - kpos
        band = (seg != 0) & (dist < BLOCK) & (kpos >= PREFIX)
        meta = (seg == 0) & (c_i >= N_PAD)
        out.append(jnp.where((dist >= 0) & (band | meta), -slopes * dist.astype(F32), NEG))
    return jnp.stack(out, axis=0)


def _swa_scores(q, kcat, n, slope, bias):
    s = lax.dot_general(q, kcat, _NT, preferred_element_type=F32) + bias
    further = slope * (-BLOCK * jnp.maximum(n - 2, 0)).astype(F32)
    return jnp.concatenate([s[:, 0:BLOCK] + further, s[:, BLOCK:]], axis=1)


def _swa_specs():
    def kv(col_blk):
        return [pl.BlockSpec((None, BLOCK, 128), lambda b, n: (b, 0, col_blk)),
                pl.BlockSpec((None, BLOCK, 128), lambda b, n: (b, jnp.maximum(n - 1, 0), col_blk)),
                pl.BlockSpec((None, BLOCK, 128), lambda b, n: (b, n, col_blk))]

    q_spec = pl.BlockSpec((None, BLOCK, A_WIDTH), lambda b, n: (b, n, A_Q_BLK))
    o_spec = pl.BlockSpec((None, BLOCK, A_WIDTH), lambda b, n: (b, n, 0))
    col = pl.BlockSpec((STACK, 1), lambda b, n: (0, 0))
    bias = pl.BlockSpec((None, STACK, 3 * BLOCK), lambda b, n: (jnp.minimum(n, 2), 0, 0))
    lse_spec = pl.BlockSpec((None, A_HEADS, BLOCK, 1), lambda b, n: (b, 0, n, 0))
    return q_spec, kv(A_K_BLK), kv(A_V_BLK), o_spec, [col, col, bias], lse_spec


def _swa_fwd(name, qkv, slopes, sinks, bias):
    b, l, _ = qkv.shape
    nb = l // BLOCK

    def body(q_ref, k0_ref, kp_ref, kc_ref, v0_ref, vp_ref, vc_ref, sl_ref, sk_ref, bias_ref, o_ref, lse_ref):
        n = pl.program_id(1)
        qs = _stack_heads(q_ref[...], True) * SCALE
        kcat = jnp.concatenate([k0_ref[...], kp_ref[...], kc_ref[...]], axis=0)
        vcat = jnp.concatenate([v0_ref[...], vp_ref[...], vc_ref[...]], axis=0)
        s = _swa_scores(qs, kcat, n, sl_ref[...], bias_ref[...])
        sink = sk_ref[...]
        m = jnp.maximum(jnp.max(s, axis=-1, keepdims=True), sink)
        p = jnp.exp(s - m)
        den = jnp.sum(p, axis=-1, keepdims=True) + jnp.exp(sink - m)
        o = lax.dot_general(p.astype(BF16), vcat, _NN, preferred_element_type=F32) / den
        o_ref[...] = _unstack_heads(o)
        lse_ref[...] = (m + jnp.log(den)).reshape(A_HEADS, BLOCK, 1)

    q_spec, k_specs, v_specs, o_spec, consts, lse_spec = _swa_specs()
    return pl.pallas_call(
        body, name=name, grid=(b, nb),
        in_specs=[q_spec] + k_specs + v_specs + consts, out_specs=[o_spec, lse_spec],
        out_shape=[jax.ShapeDtypeStruct((b, l, A_WIDTH), F32), jax.ShapeDtypeStruct((b, A_HEADS, l, 1), F32)],
        compiler_params=_params(("parallel", "parallel")),
    )(qkv, qkv, qkv, qkv, qkv, qkv, qkv, slopes, sinks, bias)


def _swa_bwd(name, qkv, o, lse, do, slopes, sinks, bias, dqkv):
    b, l, _ = qkv.shape
    nb = l // BLOCK

    def body(q_ref, k0_ref, kp_ref, kc_ref, v0_ref, vp_ref, vc_ref, o_ref, lse_ref, do_ref, sl_ref, sk_ref, bias_ref, _,
             dx_ref, ds_ref, dk_acc, dv_acc):
        bi = pl.program_id(0)
        n = pl.program_id(1)
        qs = _stack_heads(q_ref[...], True) * SCALE
        dos32 = _stack_heads(do_ref[...], True)
        dos = dos32.astype(BF16)
        os_ = _stack_heads(o_ref[...], False)
        lsev = lse_ref[...].reshape(STACK, 1)
        kcat = jnp.concatenate([k0_ref[...], kp_ref[...], kc_ref[...]], axis=0)
        vcat = jnp.concatenate([v0_ref[...], vp_ref[...], vc_ref[...]], axis=0)
        s = _swa_scores(qs, kcat, n, sl_ref[...], bias_ref[...])
        p = jnp.exp(s - lsev)
        dsum = jnp.sum(dos32 * os_, axis=-1, keepdims=True)
        dp = lax.dot_general(dos, vcat, _NT, preferred_element_type=F32)
        dsc = (p * (dp - dsum)).astype(BF16)
        dq = lax.dot_general(dsc, kcat, _NN, preferred_element_type=F32) * SCALE
        row0 = pl.multiple_of(n * BLOCK, BLOCK)
        dx_ref[pl.ds(row0, BLOCK), 0:A_WIDTH] = _unstack_heads(dq).astype(BF16)
        dkc = lax.dot_general(dsc, qs, _TN, preferred_element_type=F32)
        dvc = lax.dot_general(p.astype(BF16), dos, _TN, preferred_element_type=F32)

        @pl.when(n == 0)
        def _():
            dk_acc[...] = jnp.zeros_like(dk_acc)
            dv_acc[...] = jnp.zeros_like(dv_acc)

        starts = (0, pl.multiple_of(jnp.maximum(n - 1, 0) * BLOCK, BLOCK), row0)
        for t, st in enumerate(starts):
            dk_acc[pl.ds(st, BLOCK), :] += dkc[t * BLOCK:(t + 1) * BLOCK, :]
            dv_acc[pl.ds(st, BLOCK), :] += dvc[t * BLOCK:(t + 1) * BLOCK, :]

        @pl.when(n == nb - 1)
        def _():
            dx_ref[:, A_WIDTH:A_WIDTH + 128] = dk_acc[...].astype(BF16)
            dx_ref[:, A_WIDTH + 128:A_SEG] = dv_acc[...].astype(BF16)

        dsink = -(jnp.exp(sk_ref[...] - lsev) * dsum)
        r8 = lax.broadcasted_iota(jnp.int32, (8, 128), 0)
        acc = jnp.zeros((8, 128), F32)
        for hh in range(A_HEADS):
            acc = acc + jnp.where(r8 == hh, jnp.sum(dsink[hh * BLOCK:(hh + 1) * BLOCK, :]), 0.0)

        @pl.when((bi == 0) & (n == 0))
        def _():
            ds_ref[...] = jnp.zeros_like(ds_ref)

        ds_ref[...] += acc

    q_spec, k_specs, v_specs, o_spec, consts, lse_spec = _swa_specs()
    return pl.pallas_call(
        body, name=name, grid=(b, nb),
        in_specs=[q_spec] + k_specs + v_specs + [o_spec, lse_spec, o_spec] + consts + [pl.BlockSpec(memory_space=pl.ANY)],
        out_specs=[pl.BlockSpec((None, l, A_SEG), lambda bb, n: (bb, 0, A_SEG_BLK)),
                   pl.BlockSpec((8, 128), lambda bb, n: (0, 0))],
        out_shape=[jax.ShapeDtypeStruct(dqkv.shape, BF16), jax.ShapeDtypeStruct((8, 128), F32)],
        scratch_shapes=[pltpu.VMEM((l, 128), F32), pltpu.VMEM((l, 128), F32)],
        input_output_aliases={13: 0},
        compiler_params=_params(("arbitrary", "arbitrary")),
    )(qkv, qkv, qkv, qkv, qkv, qkv, qkv, o, lse, do, slopes, sinks, bias, dqkv)


def _fox_mask(qk, ck, i):
    kh = qk.shape[1]
    qpos = i * BLOCK + lax.broadcasted_iota(jnp.int32, (BLOCK, kh), 0)
    kpos = lax.broadcasted_iota(jnp.int32, (BLOCK, kh), 1)
    return jnp.where((kpos <= qpos) & (kpos >= N_PAD), qk - ck, NEG)


def _pick_head(x, hh):
    lo = _lane_lo()
    return jnp.where(lo if hh == 0 else jnp.logical_not(lo), x, jnp.zeros_like(x))


def _both_heads(x):
    return jnp.concatenate([_pick_head(x, 0), _pick_head(x, 1)], axis=0)


def _fox_specs(l):
    pair = pl.BlockSpec((None, l, PAIR_W), lambda bi, hp: (bi, 0, hp))
    half = pl.BlockSpec((None, l, 128), lambda bi, hp: (bi, 0, hp))
    colv = pl.BlockSpec((None, 2, l, 1), lambda bi, hp: (bi, hp, 0, 0))
    rowv = pl.BlockSpec((None, 2, 1, l), lambda bi, hp: (bi, hp, 0, 0))
    return pair, half, colv, rowv


def _fox_fwd(name, qkv, c_col, c_row):
    b, l, _ = qkv.shape
    nb = l // BLOCK

    def body(x_ref, cc_ref, cr_ref, o_ref, lse_ref):
        for i in range(nb):
            rows = slice(i * BLOCK, (i + 1) * BLOCK)
            kh = (i + 1) * BLOCK
            qblk = x_ref[rows, 0:128]
            kv = x_ref[0:kh, 128:256]
            vv = x_ref[0:kh, 256:384]
            qk = lax.dot_general(_both_heads(qblk) * SCALE, kv, _NT, preferred_element_type=F32)
            ps, dens = [], []
            for hh in range(2):
                s = _fox_mask(qk[hh * BLOCK:(hh + 1) * BLOCK], cr_ref[hh, :, 0:kh], i)
                m = jnp.max(s, axis=-1, keepdims=True)
                p = jnp.exp(s - m)
                den = jnp.sum(p, axis=-1, keepdims=True)
                ps.append(p.astype(BF16))
                dens.append(den)
                lse_ref[hh, rows, :] = (m + jnp.log(den)) + cc_ref[hh, rows, :]
            pv = lax.dot_general(jnp.concatenate(ps, axis=0), vv, _NN, preferred_element_type=F32)
            o_ref[rows, :] = jnp.where(_lane_lo(), pv[0:BLOCK] / dens[0], pv[BLOCK:2 * BLOCK] / dens[1]).astype(BF16)

    pair, half, colv, rowv = _fox_specs(l)
    return pl.pallas_call(
        body, name=name, grid=(b, 4), in_specs=[pair, colv, rowv], out_specs=[half, colv],
        out_shape=[jax.ShapeDtypeStruct((b, l, B_WIDTH), BF16), jax.ShapeDtypeStruct((b, B_HEADS, l, 1), F32)],
        compiler_params=_params(("parallel", "parallel")),
    )(qkv, c_col, c_row)


def _fox_bwd(name, qkv, c_col, c_row, o, lse, do):
    b, l, _ = qkv.shape
    nb = l // BLOCK

    def body(x_ref, cc_ref, cr_ref, o_ref, lse_ref, do_ref, dx_ref, dcq_ref, dck_ref, dk_acc, dv_acc):
        dk_acc[...] = jnp.zeros_like(dk_acc)
        dv_acc[...] = jnp.zeros_like(dv_acc)
        dck_ref[...] = jnp.zeros_like(dck_ref)
        for i in range(nb):
            rows = slice(i * BLOCK, (i + 1) * BLOCK)
            kh = (i + 1) * BLOCK
            qblk = x_ref[rows, 0:128]
            kv = x_ref[0:kh, 128:256]
            vv = x_ref[0:kh, 256:384]
            doblk = do_ref[rows, :]
            ov = o_ref[rows, :].astype(F32)
            q2 = _both_heads(qblk) * SCALE
            do2 = _both_heads(doblk)
            qk = lax.dot_general(q2, kv, _NT, preferred_element_type=F32)
            dp = lax.dot_general(do2, vv, _NT, preferred_element_type=F32)
            ps, dss = [], []
            for hh in range(2):
                half = slice(hh * BLOCK, (hh + 1) * BLOCK)
                s = _fox_mask(qk[half], cr_ref[hh, :, 0:kh], i)
                p = jnp.exp(s - (lse_ref[hh, rows, :] - cc_ref[hh, rows, :]))
                dsum = jnp.sum(do2[half].astype(F32) * ov, axis=-1, keepdims=True)
                ds = p * (dp[half] - dsum)
                ps.append(p.astype(BF16))
                dss.append(ds.astype(BF16))
                dcq_ref[hh, rows, :] = jnp.sum(ds, axis=-1, keepdims=True)
                dck_ref[hh, :, 0:kh] -= jnp.sum(ds, axis=0, keepdims=True)
            p2 = jnp.concatenate(ps, axis=0)
            ds2 = jnp.concatenate(dss, axis=0)
            dq = lax.dot_general(ds2, kv, _NN, preferred_element_type=F32) * SCALE
            dk_acc[0:kh, :] += lax.dot_general(ds2, q2, _TN, preferred_element_type=F32)
            dv_acc[0:kh, :] += lax.dot_general(p2, do2, _TN, preferred_element_type=F32)
            dx_ref[rows, 0:128] = jnp.where(_lane_lo(), dq[0:BLOCK], dq[BLOCK:2 * BLOCK]).astype(BF16)
        dx_ref[:, 128:256] = dk_acc[...].astype(BF16)
        dx_ref[:, 256:384] = dv_acc[...].astype(BF16)

    pair, half, colv, rowv = _fox_specs(l)
    return pl.pallas_call(
        body, name=name, grid=(b, 4), in_specs=[pair, colv, rowv, half, colv, half],
        out_specs=[pair, colv, rowv],
        out_shape=[jax.ShapeDtypeStruct(qkv.shape, BF16), jax.ShapeDtypeStruct((b, B_HEADS, l, 1), F32),
                   jax.ShapeDtypeStruct((b, B_HEADS, 1, l), F32)],
        scratch_shapes=[pltpu.VMEM((l, 128), F32), pltpu.VMEM((l, 128), F32)],
        compiler_params=_params(("parallel", "parallel")),
    )(qkv, c_col, c_row, o, lse, do)


_FLIPS = ((0, 0, 1), (0, 1, 0), (0, 1, 1), (1, 0, 0), (1, 0, 1), (1, 1, 0), (1, 1, 1))


def _peer_table():
    x, y, c = lax.axis_index("x"), lax.axis_index("y"), lax.axis_index("c")
    me = 4 * x + 2 * y + c
    peers = []
    for fx, fy, fc in _FLIPS:
        px = 1 - x if fx else x
        py = 1 - y if fy else y
        pc = 1 - c if fc else c
        peers.append(((px, py, pc), 4 * px + 2 * py + pc))
    return me, peers


_HBM = pl.BlockSpec(memory_space=pltpu.HBM)
_SEM = pl.BlockSpec(memory_space=pltpu.SEMAPHORE)
_ANY = pl.BlockSpec(memory_space=pl.ANY)
_EFFECT = pltpu.SideEffectType.DATAFLOW_SIDE_EFFECTING


def _split_copy(srcs_are_pieces, src_refs, land_refs, send_sem, recv_sem, a, kk, me, peers, arriving):
    dev, lin = peers[kk]
    npeer = len(_FLIPS)
    src = src_refs[a] if srcs_are_pieces[a] else src_refs[a].at[lin]
    dst = land_refs[a].at[lin] if arriving else land_refs[a].at[me]
    return pltpu.make_async_remote_copy(src_ref=src, dst_ref=dst, send_sem=send_sem.at[a * npeer + kk],
                                        recv_sem=recv_sem.at[a * npeer + kk], device_id=dev, device_id_type=MESH_ID)


def _xchg_start(name, gather, scatter, after=None):
    me_out = 4 * lax.axis_index("x") + 2 * lax.axis_index("y") + lax.axis_index("c")
    srcs = list(gather) + list(scatter)
    is_piece = [True] * len(gather) + [False] * len(scatter)
    lands = []
    for a, piece in zip(srcs, is_piece):
        if piece:
            lands.append(lax.dynamic_update_slice(lax.empty((N_DEV,) + tuple(a.shape), a.dtype), a[None],
                                                  (me_out,) + (0,) * a.ndim))
        else:
            lands.append(lax.empty(tuple(a.shape), a.dtype))
    n = len(srcs)
    nsem = n * len(_FLIPS)
    has_after = after is not None

    def body(*refs):
        src_refs = refs[:n]
        land_refs = refs[n:2 * n]
        outs = refs[2 * n + (1 if has_after else 0):]
        send_sem, recv_sem = outs[0], outs[1]
        token = outs[-1]
        me, peers = _peer_table()
        for kk in range(len(_FLIPS)):
            for a in range(n):
                _split_copy(is_piece, src_refs, land_refs, send_sem, recv_sem, a, kk, me, peers, False).start()
        token[...] = jnp.zeros_like(token)

    out_shape = ([pltpu.SemaphoreType.DMA((nsem,)), pltpu.SemaphoreType.DMA((nsem,))]
                 + [pltpu.HBM(tuple(a.shape), a.dtype) for a in srcs] + [pltpu.HBM(tuple(a.shape), a.dtype) for a in lands]
                 + [jax.ShapeDtypeStruct((8, 128), F32)])
    args = [pltpu.with_memory_space_constraint(a, pltpu.HBM) for a in srcs + lands] + ([after] if has_after else [])
    res = pl.pallas_call(
        body, name=name, out_shape=out_shape,
        in_specs=[_HBM] * (2 * n) + ([_ANY] if has_after else []),
        out_specs=[_SEM, _SEM] + [_HBM] * (2 * n) + [pl.BlockSpec(memory_space=pltpu.VMEM)],
        input_output_aliases={i: 2 + i for i in range(2 * n)},
        compiler_params=pltpu.CompilerParams(has_side_effects=_EFFECT),
    )(*args)
    state = (res[0], res[1], list(res[2:2 + n]), list(res[2 + n:2 + 2 * n]), is_piece)
    return state, res[-1]


def _xchg_wait(name, state, after):
    send_sem, recv_sem, srcs, lands, is_piece = state
    n = len(srcs)

    def body(*refs):
        src_refs = refs[:n]
        land_refs = refs[n:2 * n]
        s_sem, r_sem = refs[2 * n], refs[2 * n + 1]
        me, peers = _peer_table()
        for kk in range(len(_FLIPS)):
            for a in range(n):
                cp = _split_copy(is_piece, src_refs, land_refs, s_sem, r_sem, a, kk, me, peers, True)
                cp.wait_send()
                cp.wait_recv()

    out_shape = [pltpu.HBM(tuple(a.shape), a.dtype) for a in srcs] + [pltpu.HBM(tuple(a.shape), a.dtype) for a in lands]
    res = pl.pallas_call(
        body, name=name, out_shape=out_shape,
        in_specs=[_HBM] * (2 * n) + [_SEM, _SEM, _ANY], out_specs=[_HBM] * (2 * n),
        input_output_aliases={i: i for i in range(2 * n)},
        compiler_params=pltpu.CompilerParams(has_side_effects=_EFFECT),
    )(*srcs, *lands, send_sem, recv_sem, after)
    return list(res[n:]), list(res[:n])


_SIB = (0, 0, 1)
_ICI = ((0, 1, 0), (1, 0, 0), (1, 1, 0))


def _flip(fl):
    x, y, c = lax.axis_index("x"), lax.axis_index("y"), lax.axis_index("c")
    px = 1 - x if fl[0] else x
    py = 1 - y if fl[1] else y
    pc = 1 - c if fl[2] else c
    return (px, py, pc), 4 * px + 2 * py + pc


def _gather2_start(name, pieces, after=None):
    me_out = 4 * lax.axis_index("x") + 2 * lax.axis_index("y") + lax.axis_index("c")
    pieces = list(pieces)
    n = len(pieces)
    lands = [lax.dynamic_update_slice(lax.empty((N_DEV,) + tuple(a.shape), a.dtype), a[None],
                                      (me_out,) + (0,) * a.ndim) for a in pieces]
    first = (_SIB,) + _ICI
    has_after = after is not None

    def body(*refs):
        src_refs, land_refs = refs[:n], refs[n:2 * n]
        outs = refs[2 * n + (1 if has_after else 0):]
        send_sem, recv_sem, token = outs[0], outs[1], outs[-1]
        _, me = _flip((0, 0, 0))
        for kk, fl in enumerate(first):
            dev, _ = _flip(fl)
            for a in range(n):
                pltpu.make_async_remote_copy(src_ref=src_refs[a], dst_ref=land_refs[a].at[me],
                                             send_sem=send_sem.at[a * 4 + kk], recv_sem=recv_sem.at[a * 4 + kk],
                                             device_id=dev, device_id_type=MESH_ID).start()
        token[...] = jnp.zeros_like(token)

    hbm = [pltpu.HBM(tuple(a.shape), a.dtype) for a in pieces + lands]
    res = pl.pallas_call(
        body, name=name,
        out_shape=[pltpu.SemaphoreType.DMA((4 * n,)), pltpu.SemaphoreType.DMA((4 * n,))] + hbm
        + [jax.ShapeDtypeStruct((8, 128), F32)],
        in_specs=[_HBM] * (2 * n) + ([_ANY] if has_after else []),
        out_specs=[_SEM, _SEM] + [_HBM] * (2 * n) + [pl.BlockSpec(memory_space=pltpu.VMEM)],
        input_output_aliases={i: 2 + i for i in range(2 * n)},
        compiler_params=pltpu.CompilerParams(has_side_effects=_EFFECT),
    )(*([pltpu.with_memory_space_constraint(a, pltpu.HBM) for a in pieces + lands] + ([after] if has_after else [])))
    return (res[0], res[1], list(res[2:2 + n]), list(res[2 + n:2 + 2 * n])), res[-1]


def _gather2_forward(name, state, after):
    send_a, recv_a, pieces, lands = state
    n = len(pieces)
    first = (_SIB,) + _ICI

    def body(*refs):
        src_refs, land_refs = refs[:n], refs[n:2 * n]
        s_a, r_a = refs[2 * n], refs[2 * n + 1]
        outs = refs[2 * n + 3:]
        send_b, recv_b, token = outs[0], outs[1], outs[-1]
        for kk, fl in enumerate(first):
            dev, lin = _flip(fl)
            for a in range(n):
                cp = pltpu.make_async_remote_copy(src_ref=src_refs[a], dst_ref=land_refs[a].at[lin],
                                                  send_sem=s_a.at[a * 4 + kk], recv_sem=r_a.at[a * 4 + kk],
                                                  device_id=dev, device_id_type=MESH_ID)
                cp.wait_send()
                cp.wait_recv()
        sib, _ = _flip(_SIB)
        for j, fl in enumerate(_ICI):
            _, lin = _flip(fl)
            for a in range(n):
                pltpu.make_async_remote_copy(src_ref=land_refs[a].at[lin], dst_ref=land_refs[a].at[lin],
                                             send_sem=send_b.at[a * 3 + j], recv_sem=recv_b.at[a * 3 + j],
                                             device_id=sib, device_id_type=MESH_ID).start()
        token[...] = jnp.zeros_like(token)

    hbm = [pltpu.HBM(tuple(a.shape), a.dtype) for a in pieces + lands]
    res = pl.pallas_call(
        body, name=name,
        out_shape=[pltpu.SemaphoreType.DMA((3 * n,)), pltpu.SemaphoreType.DMA((3 * n,))] + hbm
        + [jax.ShapeDtypeStruct((8, 128), F32)],
        in_specs=[_HBM] * (2 * n) + [_SEM, _SEM, _ANY],
        out_specs=[_SEM, _SEM] + [_HBM] * (2 * n) + [pl.BlockSpec(memory_space=pltpu.VMEM)],
        input_output_aliases={i: 2 + i for i in range(2 * n)},
        compiler_params=pltpu.CompilerParams(has_side_effects=_EFFECT),
    )(*pieces, *lands, send_a, recv_a, after)
    return (res[0], res[1], list(res[2 + n:2 + 2 * n])), res[-1]


def _gather2_wait(name, state, after):
    send_b, recv_b, lands = state
    n = len(lands)

    def body(*refs):
        land_refs = refs[:n]
        s_b, r_b = refs[n], refs[n + 1]
        sib, _ = _flip(_SIB)
        for j, fl in enumerate(_ICI):
            _, sent = _flip(fl)
            _, arriving = _flip((fl[0], fl[1], 1))
            for a in range(n):
                cp = pltpu.make_async_remote_copy(src_ref=land_refs[a].at[sent], dst_ref=land_refs[a].at[arriving],
                                                  send_sem=s_b.at[a * 3 + j], recv_sem=r_b.at[a * 3 + j],
                                                  device_id=sib, device_id_type=MESH_ID)
                cp.wait_send()
                cp.wait_recv()

    res = pl.pallas_call(
        body, name=name, out_shape=[pltpu.HBM(tuple(a.shape), a.dtype) for a in lands],
        in_specs=[_HBM] * n + [_SEM, _SEM, _ANY], out_specs=[_HBM] * n,
        input_output_aliases={i: i for i in range(n)},
        compiler_params=pltpu.CompilerParams(has_side_effects=_EFFECT),
    )(*lands, send_b, recv_b, after)
    return list(res)


def _adam_math(w, g, m, v):
    m = ADAM_B1 * m + (1.0 - ADAM_B1) * g
    v = ADAM_B2 * v + (1.0 - ADAM_B2) * (g * g)
    m_hat = m / (1.0 - ADAM_B1 ** ADAM_STEP)
    v_hat = v / (1.0 - ADAM_B2 ** ADAM_STEP)
    delta = -ADAM_LR * (m_hat / (jnp.sqrt(v_hat) + ADAM_EPS) + ADAM_WD * w)
    return delta, m, v


def _adam(name, w, m, v, parts, transposed=False, dep=None, own=None):
    npart, _, cp = parts.shape
    has_dep = dep is not None
    has_own = own is not None
    if transposed:
        c, r = w.shape
        tr = _pick(r, (256, 128))
        blk = pl.BlockSpec((c, tr), lambda i, *_: (0, i))
    else:
        r, c = w.shape
        tr = _pick(r, (256, 176, 128, 64, 16, 8, 1))
        blk = pl.BlockSpec((tr, c), lambda i, *_: (i, 0))

    def body(*refs):
        refs = list(refs)
        me = refs.pop(0)[0] if has_own else None
        w_ref, m_ref, v_ref, p_ref = refs[:4]
        own_ref = refs[4] if has_own else None
        g_ref, d_ref, mo_ref, vo_ref = refs[-4:]
        g = None
        for pp in range(npart):
            part = p_ref[pp] if not has_own else jnp.where(me == pp, own_ref[...], p_ref[pp])
            g = part.astype(F32) if g is None else g + part.astype(F32)
        g = g.T[0:c, :] if transposed else g[:, 0:c]
        delta, mn, vn = _adam_math(w_ref[...], g, m_ref[...], v_ref[...])
        g_ref[...] = g
        d_ref[...] = delta
        mo_ref[...] = mn
        vo_ref[...] = vn

    in_specs = [blk, blk, blk, pl.BlockSpec((npart, tr, cp), lambda i, *_: (0, i, 0))]
    args = [w, m, v, parts]
    if has_own:
        in_specs.append(pl.BlockSpec((None, tr, cp), lambda i, me_ref: (me_ref[0], i, 0)))
        args.append(own[0])
    if has_dep:
        in_specs.append(pl.BlockSpec(memory_space=pl.ANY))
        args.append(dep)
    out = jax.ShapeDtypeStruct(w.shape, F32)
    grid_spec = pltpu.PrefetchScalarGridSpec(num_scalar_prefetch=1 if has_own else 0, grid=(r // tr,),
                                             in_specs=in_specs, out_specs=[blk, blk, blk, blk])
    return pl.pallas_call(
        body, name=name, grid_spec=grid_spec, out_shape=[out, out, out, out], compiler_params=_params(("parallel",)),
    )(*(([own[1]] if has_own else []) + args))


def _small_update(name, packs, wmv):
    nparam = len(wmv)

    def body(p_ref, *refs):
        ins, outs = refs[:3 * nparam], refs[3 * nparam:]
        tot = p_ref[0]
        for pp in range(1, N_DEV):
            tot = tot + p_ref[pp]
        outs[0][0:8, :] = tot[0:8, :]
        outs[0][8:24, :] = tot[8:24, :] + tot[24:40, :]
        grads = [tot[i:i + 1, :] for i in range(4)] + [tot[4:5, 0:B_HEADS], tot[4:5, B_HEADS:2 * B_HEADS]]
        for i, g in enumerate(grads):
            w_ref, m_ref, v_ref = ins[3 * i:3 * i + 3]
            delta, mn, vn = _adam_math(w_ref[...], g, m_ref[...], v_ref[...])
            for o_ref, val in zip(outs[1 + 4 * i:5 + 4 * i], (g, delta, mn, vn)):
                o_ref[...] = val

    flat = [a for trio in wmv for a in trio]
    out_shape = [jax.ShapeDtypeStruct((24, D_MODEL), F32)]
    for w, _, _ in wmv:
        out_shape += [jax.ShapeDtypeStruct(w.shape, F32)] * 4
    res = pl.pallas_call(body, name=name, out_shape=out_shape, compiler_params=_params())(packs, *flat)
    return res[0], [tuple(res[1 + 4 * i:5 + 4 * i]) for i in range(nparam)]


def _local_step(x, tgt, g1, gm, g2, gf, b_forget, sinks, weights, send):
    b, s, _ = x.shape
    l = s + PREFIX
    t = b * l
    (meta,) = weights("meta", x)

    h0, n1 = _embed_norm("embed_rms1_fwd", x, meta, g1)
    (w1i,) = weights("ffn1_in", n1)
    gu1, a1 = _ffn_in_fwd("ffn1_in_fwd", n1, w1i)
    (w1o,) = weights("ffn1_out", weights("ffn1_out:forward", a1))
    h1, um = _mm_res_norm("ffn1_out_fwd", a1, w1o, h0, gm, alpha=0.5, dep=weights("mix_in:forward", w1o))
    (wi,) = weights("mix_in", um)
    qkv, gates, f2 = _proj_fwd("proj_fwd", um, wi, dep=weights("mix_rest:forward", wi))
    qkv3 = qkv.reshape(b, l, QKV_W)
    f3 = f2.reshape(b, l, 128)
    bf_row = jnp.pad(b_forget, ((0, 0), (0, 128 - B_HEADS)))
    c_col, c_row = _fgate_fwd("fgate_fwd", f3, bf_row)
    head_of_row = jnp.arange(STACK) // BLOCK
    slopes = jnp.exp2(-8.0 * (head_of_row + 1).astype(F32) / A_HEADS).reshape(STACK, 1)
    sink_rows = jnp.repeat(sinks.reshape(A_HEADS), BLOCK).reshape(STACK, 1)
    swa_bias = _swa_bias(slopes)
    oa3, lse_a = _swa_fwd("swa_fwd", qkv3, slopes, sink_rows, swa_bias)
    ob3, lse_b = _fox_fwd("fox_fwd", qkv3, c_col, c_row)
    oa = oa3.reshape(t, A_WIDTH)
    ob = ob3.reshape(t, B_WIDTH)
    wa, wb, wo = weights("mix_rest", ob)
    mixed, ya, yb = _branch_gate_fwd("branch_gate_fwd", oa, ob, wa, wb, gates, dep=weights("ffn2:forward", wb))
    h2, n2 = _mm_res_norm("mix_out_fwd", mixed, wo, h1, g2)
    w2i, w2o = weights("ffn2", h2)
    gu2, a2 = _ffn_in_fwd("ffn2_in_fwd", n2, w2i)

    dh3, dh3_bf, loss_blk, dgf = _ffn_out_loss("ffn2_out_loss", a2, w2o, h2, gf, tgt, 0.5)

    def ffn_bwd(tag, dh, dh_bf, h_in, g_norm, n_in, gu, a, w_in_blk, w_out, one_send, examples=None):
        dw_out = _ffn_out_bwd_w(tag + "_out_bwd_w", a, dh_bf)
        dgu = _ffn_out_bwd_x(tag + "_out_bwd_x", dh_bf, w_out, gu, dep=None if one_send else send(tag + "_out", (dw_out,)))
        dw_in = _ffn_in_bwd_w(tag + "_in_bwd_w", n_in, dgu)
        token = send(tag, (dw_in, dw_out)) if one_send else send(tag + "_in", (dw_in,))
        return _ffn_in_bwd_x(tag + "_in_bwd_x", dgu, w_in_blk, h_in, g_norm, dh, dep=token, examples=examples)

    dh2, dh2_bf, dg2 = ffn_bwd("ffn2", dh3, dh3_bf, h2, g2, n2, gu2, a2, w2i, w2o, True)

    dwo = _mm_tn("mix_out_bwd_w", mixed, dh2_bf)
    dya, dyb, dgates = _mix_out_gate_bwd("mix_out_gate_bwd", dh2_bf, wo, gates, ya, yb)
    doa, dob = _branch_bwd_x("branch_bwd_x", dya, dyb, wa, wb)
    dwa, dwb = _branch_bwd_w("branch_bwd_w", oa, ob, dya, dyb)
    dqkv3, dcq, dck = _fox_bwd("fox_bwd", qkv3, c_col, c_row, ob3, lse_b, dob.reshape(b, l, B_WIDTH))
    dqkv3, dsink = _swa_bwd("swa_bwd", qkv3, oa3, lse_a, doa.reshape(b, l, A_WIDTH), slopes, sink_rows, swa_bias, dqkv3)
    dqkv = dqkv3.reshape(t, QKV_W)
    df3, dbf = _fgate_bwd("fgate_bwd", f3, bf_row, dcq, dck)
    df = df3.reshape(t, 128)
    dwi_qkv = _mm_tn("proj_qkv_bwd_w", um, dqkv, tm_c=(512,), tn_c=(768,))
    dwi_g = _mm_tn("proj_gates_bwd_w", um, dgates, tm_c=(512,), tn_c=(512,))
    dwi_f = _mm_tn("proj_f_bwd_w", um, df, tm_c=(512,), tn_c=(128,))
    token = send("mix", (dwi_qkv, dwi_g, dwi_f, dwa, dwb, dwo))
    dh1, dh1_bf, dgm = _proj_bwd_x("proj_bwd_x", dqkv, dgates, df, wi, h1, gm, dh2, dep=token)

    grad_x, dmeta3, dg1 = ffn_bwd("ffn1", dh1, dh1_bf, h0, g1, n1, gu1, a1, w1i, w1o, False, examples=b)
    dmeta = dmeta3.reshape(b * N_META, D_MODEL)

    misc = jnp.concatenate([dbf[:, 0:B_HEADS], dsink[:, 0].reshape(1, A_HEADS), loss_blk[0:1, 0:1]], axis=1)
    misc = jnp.pad(misc, ((0, 0), (0, D_MODEL - misc.shape[1])))
    row = lax.broadcasted_iota(jnp.int32, (8, D_MODEL), 0)
    vec = jnp.zeros((8, D_MODEL), F32)
    for i, piece in enumerate((dg1, dgm, dg2, dgf, misc)):
        vec = jnp.where(row == i, piece, vec)
    small = jnp.concatenate([vec, dmeta], axis=0)
    return grad_x, small


def _pad_to(a, rows, cols):
    return jnp.pad(a, ((0, rows - a.shape[0]), (0, cols - a.shape[1])))


def _ffn_out_from_gathered(name, g):
    def body(g_ref, o_ref):
        o_ref[0:FFO_SHARD, :] = g_ref[0]
        o_ref[FFO_SHARD:FF_SHARD, :] = g_ref[1]
        o_ref[FF_SHARD:FF_SHARD_P, :] = jnp.zeros((FF_SHARD_P - FF_SHARD, D_MODEL), BF16)

    return pl.pallas_call(
        body, name=name, grid=(4,),
        in_specs=[pl.BlockSpec((2, FFO_SHARD, D_MODEL), lambda j: (j, 0, 0))],
        out_specs=pl.BlockSpec((FF_SHARD_P, D_MODEL), lambda j: (j, 0)),
        out_shape=jax.ShapeDtypeStruct((D_FF_P, D_MODEL), BF16), compiler_params=_params(("parallel",)),
    )(g)


def _ffn_out_bwd_w(name, a, dh):
    t = a.shape[0]
    tn = D_MODEL // 2

    def body(a_ref, b_ref, o_ref):
        acc = lax.dot_general(a_ref[...], _bf(b_ref[...]), _TN, preferred_element_type=F32) * 0.5
        o_ref[0] = acc[0:FFO_SHARD].astype(BF16)
        o_ref[1] = acc[FFO_SHARD:FF_SHARD].astype(BF16)

    return pl.pallas_call(
        body, name=name, grid=(2, 4),
        in_specs=[pl.BlockSpec((t, FF_SHARD_P), lambda j, i: (0, i)), pl.BlockSpec((t, tn), lambda j, i: (0, j))],
        out_specs=pl.BlockSpec((2, FFO_SHARD, tn), lambda j, i: (i, 0, j)),
        out_shape=jax.ShapeDtypeStruct((N_DEV, FFO_SHARD, D_MODEL), BF16), compiler_params=_params(("parallel", "parallel")),
    )(a, dh)


def _proj_segments():
    segs = [(HEAD_DIM * h, HEAD_DIM, 0, B_SEG + HEAD_DIM * A_HEAD_ORDER.index(h)) for h in range(A_HEADS)]
    segs += [(512, 128, 0, B_SEG + A_WIDTH), (640, 128, 0, B_SEG + A_WIDTH + 128)]
    for first, off in ((768, 0), (1280, 128), (1792, 256)):
        segs += [(first + 128 * hp, 128, 0, PAIR_W * hp + off) for hp in range(4)]
    segs += [(2304, B_HEADS, 2, 0), (2312, 2 * D_MODEL, 1, 0)]
    return segs


_RELAYOUT_ROWS = 256


def _proj_from_gathered(name, g):
    rows = _RELAYOUT_ROWS

    def body(g_ref, o_ref):
        def cols(first, width):
            out = []
            for p in range(N_DEV):
                lo, hi = max(first, WIN_SHARD * p), min(first + width, WIN_SHARD * (p + 1))
                if lo < hi:
                    out.append(g_ref[p, :, lo - WIN_SHARD * p:hi - WIN_SHARD * p])
            return out

        parts = []
        for arr in (0, 1, 2):
            for first, width, _, _ in sorted((s for s in _proj_segments() if s[2] == arr), key=lambda s: s[3]):
                parts += cols(first, width)
            if arr == 0:
                parts.append(jnp.zeros((rows, P_GATES - QKV_W), BF16))
        parts.append(jnp.zeros((rows, 128 - B_HEADS), BF16))
        o_ref[...] = jnp.concatenate(parts, axis=1)

    return pl.pallas_call(
        body, name=name, grid=(D_MODEL // rows,),
        in_specs=[pl.BlockSpec((N_DEV, rows, WIN_SHARD_P), lambda i: (0, i, 0))],
        out_specs=pl.BlockSpec((rows, PROJ_P), lambda i: (i, 0)),
        out_shape=jax.ShapeDtypeStruct((D_MODEL, PROJ_P), BF16), compiler_params=_params(("parallel",)),
    )(g)


def _proj_to_scatter(name, dqkv_w, dg_w, df_w):
    rows = _RELAYOUT_ROWS
    segs = sorted(_proj_segments())

    def body(q_ref, g_ref, f_ref, o_ref):
        arrays = (q_ref, g_ref, f_ref)
        for p in range(N_DEV):
            parts = []
            for first, width, arr, at in segs:
                lo, hi = max(first, WIN_SHARD * p), min(first + width, WIN_SHARD * (p + 1))
                if lo < hi:
                    parts.append(arrays[arr][:, at + lo - first:at + hi - first])
            parts.append(jnp.zeros((rows, WIN_SHARD_P - WIN_SHARD), BF16))
            o_ref[p] = jnp.concatenate(parts, axis=1)

    return pl.pallas_call(
        body, name=name, grid=(D_MODEL // rows,),
        in_specs=[pl.BlockSpec((rows, QKV_W), lambda i: (i, 0)), pl.BlockSpec((rows, 2 * D_MODEL), lambda i: (i, 0)),
                  pl.BlockSpec((rows, 128), lambda i: (i, 0))],
        out_specs=pl.BlockSpec((N_DEV, rows, WIN_SHARD_P), lambda i: (0, i, 0)),
        out_shape=jax.ShapeDtypeStruct((N_DEV, D_MODEL, WIN_SHARD_P), BF16), compiler_params=_params(("parallel",)),
    )(dqkv_w, dg_w, df_w)


def _a_rows_from_natural(w):
    return jnp.concatenate([w[HEAD_DIM * h:HEAD_DIM * (h + 1)] for h in A_HEAD_ORDER], axis=0)


def _a_rows_to_natural(w):
    return jnp.concatenate([w[HEAD_DIM * A_HEAD_ORDER.index(h):HEAD_DIM * (A_HEAD_ORDER.index(h) + 1)]
                            for h in range(A_HEADS)], axis=0)


def kernel(x, meta_tokens, ffn1_norm, ffn1_w_in, ffn1_w_out, mix_norm, w_in, b_forget, attn_sinks, w_branch_a, w_branch_b, w_out, ffn2_norm, ffn2_w_in, ffn2_w_out, final_norm, loss_target, m_meta_tokens, m_ffn1_norm, m_ffn1_w_in, m_ffn1_w_out, m_mix_norm, m_w_in, m_b_forget, m_attn_sinks, m_w_branch_a, m_w_branch_b, m_w_out, m_ffn2_norm, m_ffn2_w_in, m_ffn2_w_out, m_final_norm, v_meta_tokens, v_ffn1_norm, v_ffn1_w_in, v_ffn1_w_out, v_mix_norm, v_w_in, v_b_forget, v_attn_sinks, v_w_branch_a, v_w_branch_b, v_w_out, v_ffn2_norm, v_ffn2_w_in, v_ffn2_w_out, v_final_norm):
    me = 4 * lax.axis_index("x") + 2 * lax.axis_index("y") + lax.axis_index("c")

    def shard(w, tok, rows=None, cols=None):
        piece = (w[0] if tok is None else w[0] + tok[0, 0]).astype(BF16)
        return piece if rows is None else _pad_to(piece, rows, cols)

    first_level, second_level = {}, {}
    first_level["meta"], tok = _gather2_start("gather_meta_start", (meta_tokens,))
    first_level["ffn1_in"], tok = _gather2_start(
        "gather_ffn1_in_start", (shard(ffn1_w_in, None, D_MODEL, FF_SHARD_P),), after=tok)
    first_level["ffn1_out"], tok = _gather2_start("gather_ffn1_out_start", (shard(ffn1_w_out, tok),), after=tok)
    first_level["mix_in"], tok = _gather2_start("gather_mix_in_start", (shard(w_in, tok, D_MODEL, WIN_SHARD_P),), after=tok)
    first_level["mix_rest"], tok = _gather2_start(
        "gather_mix_rest_start", (shard(w_branch_a, tok), shard(w_branch_b, tok), shard(w_out, tok)), after=tok)
    first_level["ffn2"], tok = _gather2_start(
        "gather_ffn2_start", (shard(ffn2_w_in, tok, D_MODEL, FF_SHARD_P), shard(ffn2_w_out, tok)), after=tok)
    started = {"tok": tok}

    def weights(group, after):
        if group.endswith(":forward"):
            group = group[:-len(":forward")]
            second_level[group], token = _gather2_forward("gather_" + group + "_forward", first_level[group], after)
            return token
        if group == "meta":
            after = weights("meta:forward", started["tok"])
        if group == "ffn1_in":
            after = weights("ffn1_in:forward", after)
        got = _gather2_wait("gather_" + group + "_wait", second_level[group], after)
        if group == "mix_in":
            return (_proj_from_gathered("proj_w_relayout", got[0]),)
        if group == "mix_rest":
            gwa, gwb, gwo = got
            return (_a_rows_from_natural(gwa.transpose(1, 0, 2).reshape(A_WIDTH, D_MODEL)),
                    gwb.transpose(1, 0, 2).reshape(B_WIDTH, D_MODEL), gwo.reshape(D_MODEL, D_MODEL))
        if group == "ffn1_out":
            return (_ffn_out_from_gathered("ffn1_w_out_relayout", got[0]),)
        if group == "meta":
            return (got[0].transpose(1, 0, 2).reshape(N_META, D_MODEL),)
        if group == "ffn1_in":
            return (got[0].reshape(2, 4, D_MODEL, FF_SHARD_P),)
        return got[0].reshape(2, 4, D_MODEL, FF_SHARD_P), _ffn_out_from_gathered("ffn2_w_out_relayout", got[1])

    scatter_state = {}

    def send(group, grads):
        if group == "mix":
            dwi_qkv, dwi_g, dwi_f, dwa, dwb, dwo = grads
            dwa = _a_rows_to_natural(dwa)
            blocks = (_proj_to_scatter("proj_dw_relayout", dwi_qkv, dwi_g, dwi_f), dwa.reshape(A_WIDTH, N_DEV, 128).transpose(1, 0, 2),
                      dwb.reshape(B_WIDTH, N_DEV, 128).transpose(1, 0, 2), dwo.reshape(N_DEV, 128, D_MODEL))
        elif group.endswith("_in"):
            blocks = (grads[0].reshape(N_DEV, D_MODEL, FF_SHARD_P),)
        elif group.endswith("_out"):
            blocks = (grads[0],)
        else:
            blocks = (grads[0].reshape(N_DEV, D_MODEL, FF_SHARD_P), grads[1])
        scatter_state[group], token = _xchg_start("scatter_" + group + "_start", (), blocks)
        return token

    gf = final_norm.reshape(1, D_MODEL)
    grad_x, small = _local_step(x, loss_target, ffn1_norm, mix_norm, ffn2_norm, gf, b_forget, attn_sinks, weights, send)

    small_state, after = _xchg_start("gather_small_start", (small,), ())
    out = {}
    updates = (
        ("ffn2", (("ffn2_w_in", ffn2_w_in, m_ffn2_w_in, v_ffn2_w_in), ("ffn2_w_out", ffn2_w_out, m_ffn2_w_out, v_ffn2_w_out))),
        ("ffn1_out", (("ffn1_w_out", ffn1_w_out, m_ffn1_w_out, v_ffn1_w_out),)),
        ("mix", (("w_in", w_in, m_w_in, v_w_in), ("w_branch_a", w_branch_a, m_w_branch_a, v_w_branch_a),
                 ("w_branch_b", w_branch_b, m_w_branch_b, v_w_branch_b), ("w_out", w_out, m_w_out, v_w_out))),
    )
    last_update = ("ffn1_in", (("ffn1_w_in", ffn1_w_in, m_ffn1_w_in, v_ffn1_w_in),))

    def update(group, members, after):
        parts_list, blocks_list = _xchg_wait("scatter_" + group + "_wait", scatter_state[group], after)
        prev = None
        for (nm, w, m, v), parts, blocks in zip(members, parts_list, blocks_list):
            if nm.endswith("w_in"):
                res4 = _adam("adam_" + nm, w[0].T, m[0].T, v[0].T, parts, transposed=True, dep=prev, own=(blocks, me_arr))
                out[nm] = tuple(r.T[None] for r in res4)
            else:
                res4 = _adam("adam_" + nm, w[0], m[0], v[0], parts, dep=prev, own=(blocks, me_arr))
                out[nm] = tuple(r[None] for r in res4)
            prev = res4[0]
        return prev

    me_arr = me.reshape(1).astype(jnp.int32)
    for group, members in updates + (last_update,):
        after = update(group, members, after)
    (packs,), _ = _xchg_wait("gather_small_wait", small_state, after)

    row = lambda a: a.reshape(1, D_MODEL)
    tot, small_res = _small_update("small_update", packs, (
        (ffn1_norm, m_ffn1_norm, v_ffn1_norm), (mix_norm, m_mix_norm, v_mix_norm), (ffn2_norm, m_ffn2_norm, v_ffn2_norm),
        (row(final_norm), row(m_final_norm), row(v_final_norm)), (b_forget, m_b_forget, v_b_forget),
        (attn_sinks, m_attn_sinks, v_attn_sinks)))
    for nm, res4 in zip(("ffn1_norm", "mix_norm", "ffn2_norm", "final_norm", "b_forget", "attn_sinks"), small_res):
        out[nm] = tuple(r.reshape(D_MODEL) for r in res4) if nm == "final_norm" else res4
    loss = tot[4, 2 * B_HEADS]
    g_meta = lax.dynamic_slice(tot[8:24, :], (0, me * 128), (N_META, 128))
    out["meta_tokens"] = tuple(_adam("adam_meta_tokens", meta_tokens, m_meta_tokens, v_meta_tokens, g_meta[None]))

    names = ("meta_tokens", "ffn1_norm", "ffn1_w_in", "ffn1_w_out", "mix_norm", "w_in", "b_forget", "attn_sinks",
             "w_branch_a", "w_branch_b", "w_out", "ffn2_norm", "ffn2_w_in", "ffn2_w_out", "final_norm")
    return (loss, grad_x) + tuple(out[nm][kind] for kind in range(4) for nm in names)
```

```python
import jax
import jax.numpy as jnp
import numpy as np
from jax import lax
from jax.experimental import pallas as pl
from jax.experimental.pallas import tpu as pltpu

F32 = jnp.float32
BF16 = jnp.bfloat16

D_MODEL = 1024
N_META = 16
BLOCK = 128
PREFIX = 128
N_PAD = PREFIX - N_META
HEAD_DIM = 64
A_HEADS = 8
B_HEADS = 8
A_WIDTH = 512
A_KV_WIDTH = 128
B_WIDTH = 512
D_FF = 2816
N_DEV = 8
FF_SHARD = 2 * D_FF // N_DEV
FF_SHARD_P = 768
FFO_SHARD = D_FF // N_DEV
D_FF_P = 4 * FF_SHARD_P
W_IN_COLS = 4360
WIN_SHARD = W_IN_COLS // N_DEV
WIN_SHARD_P = 640
PAIR_W = 3 * 128
B_SEG = 4 * PAIR_W
A_SEG = A_WIDTH + 2 * A_KV_WIDTH
QKV_W = B_SEG + A_SEG
P_GATES = 2 * (2 * D_MODEL)
P_F = P_GATES + 2 * D_MODEL
PROJ_P = P_F + 128
A_HEAD_ORDER = (0, 4, 1, 5, 2, 6, 3, 7)
EPS = 1e-6
NEG = -1e30
SCALE = HEAD_DIM ** -0.5
ADAM_LR = 0.001
ADAM_B1 = 0.9
ADAM_B2 = 0.999
ADAM_EPS = 1e-08
ADAM_WD = 0.01
ADAM_STEP = 10
VMEM_LIMIT = 56 * 1024 * 1024
MESH_ID = pl.DeviceIdType.MESH
SMALL_ROWS = 40

_NN = (((1,), (0,)), ((), ()))
_NT = (((1,), (1,)), ((), ()))
_TN = (((0,), (0,)), ((), ()))


def _params(sem=None):
    return pltpu.CompilerParams(dimension_semantics=sem, vmem_limit_bytes=VMEM_LIMIT)


def _pick(n, cands):
    for c in cands:
        if n % c == 0:
            return c
    raise ValueError(f"no tile for {n}")


def _bf(v):
    return v if v.dtype == BF16 else v.astype(BF16)


def _mm(name, a, b, dims, grid, a_spec, b_spec, o_spec, out_shape, out_dtype, alpha=1.0):
    def body(a_ref, b_ref, o_ref):
        acc = lax.dot_general(_bf(a_ref[...]), _bf(b_ref[...]), dims, preferred_element_type=F32)
        if alpha != 1.0:
            acc = acc * alpha
        o_ref[...] = acc.astype(o_ref.dtype)

    return pl.pallas_call(
        body, name=name, grid=grid, in_specs=[a_spec, b_spec], out_specs=o_spec,
        out_shape=jax.ShapeDtypeStruct(out_shape, out_dtype),
        compiler_params=_params(("parallel",) * len(grid)),
    )(a, b)


def _mm_res_norm(name, a, w, res, g_next, alpha=1.0):
    t, k = a.shape
    tm = _pick(t, (544, 384, 256, 128))

    def body(a_ref, w_ref, r_ref, g_ref, h_ref, n_ref):
        acc = lax.dot_general(_bf(a_ref[...]), w_ref[...], _NN, preferred_element_type=F32)
        if alpha != 1.0:
            acc = acc * alpha
        hv = acc + r_ref[...]
        h_ref[...] = hv
        r = lax.rsqrt(jnp.mean(hv * hv, axis=-1, keepdims=True) + EPS)
        n_ref[...] = ((hv * r) * g_ref[...]).astype(BF16)

    row = pl.BlockSpec((tm, D_MODEL), lambda i: (i, 0))
    return pl.pallas_call(
        body, name=name, grid=(t // tm,),
        in_specs=[pl.BlockSpec((tm, k), lambda i: (i, 0)), pl.BlockSpec((k, D_MODEL), lambda i: (0, 0)), row,
                  pl.BlockSpec((1, D_MODEL), lambda i: (0, 0))],
        out_specs=[row, row],
        out_shape=[jax.ShapeDtypeStruct((t, D_MODEL), F32), jax.ShapeDtypeStruct((t, D_MODEL), BF16)],
        compiler_params=_params(("parallel",)),
    )(a, w, res, g_next)


def _mm_tn(name, a, b, out_dtype=BF16, alpha=1.0, tm_c=(768, 512, 256, 128), tn_c=(512, 640, 256, 128)):
    t, m = a.shape
    n = b.shape[1]
    tm = _pick(m, tm_c)
    tn = _pick(n, tn_c)
    bytes_a, bytes_b = a.size * a.dtype.itemsize, b.size * b.dtype.itemsize
    if bytes_a + bytes_b * (m // tm) <= bytes_b + bytes_a * (n // tn):
        return _mm(name, a, b, _TN, (m // tm, n // tn),
                   pl.BlockSpec((t, tm), lambda i, j: (0, i)), pl.BlockSpec((t, tn), lambda i, j: (0, j)),
                   pl.BlockSpec((tm, tn), lambda i, j: (i, j)), (m, n), out_dtype, alpha=alpha)
    return _mm(name, a, b, _TN, (n // tn, m // tm),
               pl.BlockSpec((t, tm), lambda j, i: (0, i)), pl.BlockSpec((t, tn), lambda j, i: (0, j)),
               pl.BlockSpec((tm, tn), lambda j, i: (i, j)), (m, n), out_dtype, alpha=alpha)


def _ffn_in_fwd(name, n, wblk, dep=None):
    t = n.shape[0]
    tm = _pick(t, (1088, 768, 512, 256, 128))
    has_dep = dep is not None

    def body(n_ref, w_ref, *rest):
        gu_ref, a_ref = rest[-2], rest[-1]
        nv = n_ref[...]
        g = lax.dot_general(nv, w_ref[0], _NN, preferred_element_type=F32)
        u = lax.dot_general(nv, w_ref[1], _NN, preferred_element_type=F32)
        sg = jax.nn.sigmoid(g)
        silu = g * sg
        a_ref[...] = (silu * u).astype(BF16)
        gu_ref[0] = ((0.5 * u) * (sg + silu * (1.0 - sg))).astype(BF16)
        gu_ref[1] = (0.5 * silu).astype(BF16)

    return pl.pallas_call(
        body, name=name, grid=(t // tm, 4),
        in_specs=[pl.BlockSpec((tm, D_MODEL), lambda i, j: (i, 0)),
                  pl.BlockSpec((2, None, D_MODEL, FF_SHARD_P), lambda i, j: (0, j, 0, 0))]
        + ([pl.BlockSpec(memory_space=pl.ANY)] if has_dep else []),
        out_specs=[pl.BlockSpec((2, tm, FF_SHARD_P), lambda i, j: (0, i, j)),
                   pl.BlockSpec((tm, FF_SHARD_P), lambda i, j: (i, j))],
        out_shape=[jax.ShapeDtypeStruct((2, t, D_FF_P), BF16), jax.ShapeDtypeStruct((t, D_FF_P), BF16)],
        compiler_params=_params(("parallel", "parallel")),
    )(*((n, wblk) + ((dep,) if has_dep else ())))


def _ffn_out_bwd_x(name, dh, w_out, gu, dep=None):
    t = dh.shape[0]
    tm = _pick(t, (1088, 768, 512, 256, 128))
    has_dep = dep is not None

    def body(dh_ref, w_ref, gu_ref, *rest):
        o_ref = rest[-1]
        w = w_ref[pl.ds(pl.multiple_of(pl.program_id(1) * FF_SHARD_P, FF_SHARD_P), FF_SHARD_P), :]
        da = lax.dot_general(_bf(dh_ref[...]), w, _NT, preferred_element_type=F32)
        o_ref[0] = (da * gu_ref[0].astype(F32)).astype(BF16)
        o_ref[1] = (da * gu_ref[1].astype(F32)).astype(BF16)

    gu_spec = pl.BlockSpec((2, tm, FF_SHARD_P), lambda i, j: (0, i, j))
    return pl.pallas_call(
        body, name=name, grid=(t // tm, 4),
        in_specs=[pl.BlockSpec((tm, D_MODEL), lambda i, j: (i, 0)),
                  pl.BlockSpec((D_FF_P, D_MODEL), lambda i, j: (0, 0), pipeline_mode=pl.Buffered(1)),
                  gu_spec] + ([pl.BlockSpec(memory_space=pl.ANY)] if has_dep else []),
        out_specs=gu_spec, out_shape=jax.ShapeDtypeStruct((2, t, D_FF_P), BF16),
        compiler_params=_params(("parallel", "parallel")),
    )(*((dh, w_out, gu) + ((dep,) if has_dep else ())))


def _rms_bwd_rows(dn, h, g, dres):
    r = lax.rsqrt(jnp.mean(h * h, axis=-1, keepdims=True) + EPS)
    tv = dn * g
    dot = jnp.mean(tv * h, axis=-1, keepdims=True)
    return dres + (r * tv - h * (r * r * r * dot)), jnp.sum(dn * (h * r), axis=0, keepdims=True)


def _accumulate_rows(ref, part, first):
    @pl.when(first)
    def _():
        ref[...] = part

    @pl.when(jnp.logical_not(first))
    def _():
        ref[...] += part


def _ffn_in_bwd_x(name, dgu, wblk, h_in, g_norm, dres, dep=None, examples=None):
    t = dgu.shape[1]
    tm = _pick(t, (544, 384, 256, 128))
    has_dep = dep is not None
    split = examples is not None
    if split:
        l = t // examples
        per = l // tm
        assert per * tm == l and tm > PREFIX

    def body(d_ref, w_ref, h_ref, g_ref, r_ref, *rest):
        acc = None
        for s in range(2):
            for j in range(4):
                part = lax.dot_general(d_ref[s, :, FF_SHARD_P * j:FF_SHARD_P * (j + 1)], w_ref[s, j], _NT,
                                       preferred_element_type=F32)
                acc = part if acc is None else acc + part
        dh, dg = _rms_bwd_rows(acc, h_ref[...], g_ref[...], r_ref[...])
        i = pl.program_id(0)
        if not split:
            dh_ref, dhb_ref, dg_ref = rest[-3:]
            dh_ref[...] = dh
            dhb_ref[...] = dh.astype(BF16)
        else:
            gx_ref, meta_ref, dg_ref, buf, sem = rest[-5:]

            def out_copy(step):
                bi, r = step // per, step % per
                head = pltpu.make_async_copy(buf.at[pl.ds(PREFIX, tm - PREFIX)], gx_ref.at[bi, pl.ds(0, tm - PREFIX)], sem)
                if per == 1:
                    return r == 0, head, None
                return r == 0, head, pltpu.make_async_copy(
                    buf, gx_ref.at[bi, pl.ds(pl.multiple_of(jnp.maximum(r, 1) * tm - PREFIX, 8), tm)], sem)

            def run(step, method):
                is_head, head, later = out_copy(step)

                @pl.when(is_head)
                def _():
                    getattr(head, method)()

                if later is not None:
                    @pl.when(jnp.logical_not(is_head))
                    def _():
                        getattr(later, method)()

            @pl.when(i > 0)
            def _():
                run(i - 1, "wait")

            buf[...] = dh

            @pl.when(i % per == 0)
            def _():
                meta_ref[...] = dh[N_PAD:PREFIX]

            run(i, "start")

            @pl.when(i == pl.num_programs(0) - 1)
            def _():
                run(i, "wait")

        _accumulate_rows(dg_ref, dg, i == 0)

    row = pl.BlockSpec((tm, D_MODEL), lambda i: (i, 0))
    vec = pl.BlockSpec((1, D_MODEL), lambda i: (0, 0))
    if split:
        out_specs = [pl.BlockSpec(memory_space=pl.ANY), pl.BlockSpec((None, N_META, D_MODEL), lambda i: (i // per, 0, 0)), vec]
        out_shape = [jax.ShapeDtypeStruct((examples, l - PREFIX, D_MODEL), F32),
                     jax.ShapeDtypeStruct((examples, N_META, D_MODEL), F32), jax.ShapeDtypeStruct((1, D_MODEL), F32)]
        scratch = [pltpu.VMEM((tm, D_MODEL), F32), pltpu.SemaphoreType.DMA(())]
    else:
        out_specs = [row, row, vec]
        out_shape = [jax.ShapeDtypeStruct((t, D_MODEL), F32), jax.ShapeDtypeStruct((t, D_MODEL), BF16),
                     jax.ShapeDtypeStruct((1, D_MODEL), F32)]
        scratch = []
    return pl.pallas_call(
        body, name=name, grid=(t // tm,),
        in_specs=[pl.BlockSpec((2, tm, D_FF_P), lambda i: (0, i, 0)),
                  pl.BlockSpec((2, 4, D_MODEL, FF_SHARD_P), lambda i: (0, 0, 0, 0), pipeline_mode=pl.Buffered(1)),
                  row, vec, row]
        + ([pl.BlockSpec(memory_space=pl.ANY)] if has_dep else []),
        out_specs=out_specs, out_shape=out_shape, scratch_shapes=scratch,
        compiler_params=_params(("arbitrary",)),
    )(*((dgu, wblk, h_in, g_norm, dres) + ((dep,) if has_dep else ())))


def _proj_fwd(name, um, wi):
    t = um.shape[0]
    tm = _pick(t, (544, 384, 256, 128))

    def body(u_ref, w_ref, q_ref, g_ref, f_ref):
        uv = u_ref[...]
        q_ref[...] = lax.dot_general(uv, w_ref[:, 0:QKV_W], _NN, preferred_element_type=F32).astype(BF16)
        g_ref[...] = lax.dot_general(uv, w_ref[:, P_GATES:P_F], _NN, preferred_element_type=F32).astype(BF16)
        f_ref[...] = lax.dot_general(uv, w_ref[:, P_F:PROJ_P], _NN, preferred_element_type=F32)

    return pl.pallas_call(
        body, name=name, grid=(t // tm,),
        in_specs=[pl.BlockSpec((tm, D_MODEL), lambda i: (i, 0)),
                  pl.BlockSpec((D_MODEL, PROJ_P), lambda i: (0, 0), pipeline_mode=pl.Buffered(1))],
        out_specs=[pl.BlockSpec((tm, QKV_W), lambda i: (i, 0)), pl.BlockSpec((tm, 2 * D_MODEL), lambda i: (i, 0)),
                   pl.BlockSpec((tm, 128), lambda i: (i, 0))],
        out_shape=[jax.ShapeDtypeStruct((t, QKV_W), BF16), jax.ShapeDtypeStruct((t, 2 * D_MODEL), BF16),
                   jax.ShapeDtypeStruct((t, 128), F32)],
        compiler_params=_params(("parallel",)),
    )(um, wi)


def _proj_bwd_x(name, dqkv, dgates, df, wi, h_in, g_norm, dres, dep=None):
    t = dqkv.shape[0]
    tm = _pick(t, (544, 384, 256, 128))
    has_dep = dep is not None

    def body(q_ref, gt_ref, f_ref, w_ref, h_ref, g_ref, r_ref, *rest):
        dh_ref, dhb_ref, dg_ref = rest[-3:]
        acc = lax.dot_general(q_ref[...], w_ref[:, 0:QKV_W], _NT, preferred_element_type=F32)
        acc = acc + lax.dot_general(gt_ref[...], w_ref[:, P_GATES:P_F], _NT, preferred_element_type=F32)
        acc = acc + lax.dot_general(f_ref[...], w_ref[:, P_F:PROJ_P], _NT, preferred_element_type=F32)
        dh, dg = _rms_bwd_rows(acc, h_ref[...], g_ref[...], r_ref[...])
        dh_ref[...] = dh
        dhb_ref[...] = dh.astype(BF16)
        _accumulate_rows(dg_ref, dg, pl.program_id(0) == 0)

    row = pl.BlockSpec((tm, D_MODEL), lambda i: (i, 0))
    vec = pl.BlockSpec((1, D_MODEL), lambda i: (0, 0))
    return pl.pallas_call(
        body, name=name, grid=(t // tm,),
        in_specs=[pl.BlockSpec((tm, QKV_W), lambda i: (i, 0)), pl.BlockSpec((tm, 2 * D_MODEL), lambda i: (i, 0)),
                  pl.BlockSpec((tm, 128), lambda i: (i, 0)),
                  pl.BlockSpec((D_MODEL, PROJ_P), lambda i: (0, 0), pipeline_mode=pl.Buffered(1)), row, vec, row]
        + ([pl.BlockSpec(memory_space=pl.ANY)] if has_dep else []),
        out_specs=[row, row, vec],
        out_shape=[jax.ShapeDtypeStruct((t, D_MODEL), F32), jax.ShapeDtypeStruct((t, D_MODEL), BF16),
                   jax.ShapeDtypeStruct((1, D_MODEL), F32)],
        compiler_params=_params(("arbitrary",)),
    )(*((dqkv, dgates, df, wi, h_in, g_norm, dres) + ((dep,) if has_dep else ())))


def _ffn_in_bwd_w(name, n, dgu):
    t = n.shape[0]
    return _mm(name, n, dgu, _TN, (2, 4),
               pl.BlockSpec((t, D_MODEL), lambda s, j: (0, 0)),
               pl.BlockSpec((None, t, FF_SHARD_P), lambda s, j: (s, 0, j)),
               pl.BlockSpec((None, None, D_MODEL, FF_SHARD_P), lambda s, j: (s, j, 0, 0)),
               (2, 4, D_MODEL, FF_SHARD_P), BF16)


def _embed_norm(name, x, meta, g):
    b, s, _ = x.shape
    half = (s + PREFIX) // 2
    first = half - PREFIX
    assert first > 0 and half % 16 == 0

    def body(x_ref, m_ref, g_ref, h_ref, n_ref, buf, sem):
        bi, k = pl.program_id(0), pl.program_id(1)

        def tokens(example, second, method):
            if second:
                cp = pltpu.make_async_copy(x_ref.at[example, pl.ds(first, half)], buf.at[1], sem.at[1])
            else:
                cp = pltpu.make_async_copy(x_ref.at[example, pl.ds(0, first)], buf.at[0, pl.ds(PREFIX, first)], sem.at[0])
            getattr(cp, method)()

        @pl.when((bi == 0) & (k == 0))
        def _():
            tokens(0, False, "start")

        @pl.when(k == 0)
        def _():
            tokens(bi, True, "start")
            tokens(bi, False, "wait")
            buf[0, 0:N_PAD, :] = jnp.zeros((N_PAD, D_MODEL), F32)
            buf[0, N_PAD:PREFIX, :] = m_ref[...]

        @pl.when(k == 1)
        def _():
            @pl.when(bi + 1 < b)
            def _():
                tokens(bi + 1, False, "start")

            tokens(bi, True, "wait")

        hv = buf[k]
        h_ref[...] = hv
        r = lax.rsqrt(jnp.mean(hv * hv, axis=-1, keepdims=True) + EPS)
        n_ref[...] = ((hv * r) * g_ref[...]).astype(BF16)

    rows = pl.BlockSpec((half, D_MODEL), lambda bi, k: (2 * bi + k, 0))
    return pl.pallas_call(
        body, name=name, grid=(b, 2),
        in_specs=[pl.BlockSpec(memory_space=pl.ANY), pl.BlockSpec((N_META, D_MODEL), lambda bi, k: (0, 0)),
                  pl.BlockSpec((1, D_MODEL), lambda bi, k: (0, 0))],
        out_specs=[rows, rows],
        out_shape=[jax.ShapeDtypeStruct((2 * b * half, D_MODEL), F32), jax.ShapeDtypeStruct((2 * b * half, D_MODEL), BF16)],
        scratch_shapes=[pltpu.VMEM((2, half, D_MODEL), F32), pltpu.SemaphoreType.DMA((2,))],
        compiler_params=_params(("arbitrary", "arbitrary")),
    )(x, meta, g)


def _branch_gate_fwd(name, oa, ob, wa, wb, gates, dep=None):
    t = gates.shape[0]
    tm = _pick(t, (544, 384, 256, 128))
    has_dep = dep is not None

    def body(oa_ref, ob_ref, wa_ref, wb_ref, g_ref, *rest):
        o_ref, ya_ref, yb_ref = rest[-3:]
        ya = lax.dot_general(_bf(oa_ref[...]), wa_ref[...], _NN, preferred_element_type=F32)
        yb = lax.dot_general(_bf(ob_ref[...]), wb_ref[...], _NN, preferred_element_type=F32)
        sa = jax.nn.sigmoid(g_ref[:, 0:D_MODEL].astype(F32))
        sb = jax.nn.sigmoid(g_ref[:, D_MODEL:2 * D_MODEL].astype(F32))
        o_ref[...] = (sa * ya + sb * yb).astype(BF16)
        ya_ref[...] = ya.astype(BF16)
        yb_ref[...] = yb.astype(BF16)

    blk = pl.BlockSpec((tm, D_MODEL), lambda i: (i, 0))
    narrow = pl.BlockSpec((tm, A_WIDTH), lambda i: (i, 0))
    wide = pl.BlockSpec((tm, 2 * D_MODEL), lambda i: (i, 0))
    wspec = pl.BlockSpec((A_WIDTH, D_MODEL), lambda i: (0, 0))
    out = jax.ShapeDtypeStruct((t, D_MODEL), BF16)
    return pl.pallas_call(
        body, name=name, grid=(t // tm,),
        in_specs=[narrow, narrow, wspec, wspec, wide] + ([pl.BlockSpec(memory_space=pl.ANY)] if has_dep else []),
        out_specs=[blk, blk, blk], out_shape=[out, out, out], compiler_params=_params(("parallel",)),
    )(*((oa, ob, wa, wb, gates) + ((dep,) if has_dep else ())))


def _branch_bwd_x(name, dya, dyb, wa, wb):
    t = dya.shape[0]
    tm = _pick(t, (1088, 768, 512, 256, 128))

    def body(da_ref, db_ref, wa_ref, wb_ref, oa_ref, ob_ref):
        oa_ref[...] = lax.dot_general(da_ref[...], wa_ref[...], _NT, preferred_element_type=F32)
        ob_ref[...] = lax.dot_general(db_ref[...], wb_ref[...], _NT, preferred_element_type=F32).astype(BF16)

    blk = pl.BlockSpec((tm, D_MODEL), lambda i: (i, 0))
    narrow = pl.BlockSpec((tm, A_WIDTH), lambda i: (i, 0))
    wspec = pl.BlockSpec((A_WIDTH, D_MODEL), lambda i: (0, 0), pipeline_mode=pl.Buffered(1))
    return pl.pallas_call(
        body, name=name, grid=(t // tm,), in_specs=[blk, blk, wspec, wspec], out_specs=[narrow, narrow],
        out_shape=[jax.ShapeDtypeStruct((t, A_WIDTH), F32), jax.ShapeDtypeStruct((t, B_WIDTH), BF16)],
        compiler_params=_params(("parallel",)),
    )(dya, dyb, wa, wb)


def _branch_bwd_w(name, oa, ob, dya, dyb):
    t = oa.shape[0]
    tn = 512

    def body(oa_ref, ob_ref, da_ref, db_ref, wa_ref, wb_ref):
        wa_ref[...] = lax.dot_general(_bf(oa_ref[...]), da_ref[...], _TN, preferred_element_type=F32).astype(BF16)
        wb_ref[...] = lax.dot_general(_bf(ob_ref[...]), db_ref[...], _TN, preferred_element_type=F32).astype(BF16)

    whole = pl.BlockSpec((t, A_WIDTH), lambda j: (0, 0), pipeline_mode=pl.Buffered(1))
    cols = pl.BlockSpec((t, tn), lambda j: (0, j))
    out_spec = pl.BlockSpec((A_WIDTH, tn), lambda j: (0, j))
    out = jax.ShapeDtypeStruct((A_WIDTH, D_MODEL), BF16)
    return pl.pallas_call(
        body, name=name, grid=(D_MODEL // tn,), in_specs=[whole, whole, cols, cols], out_specs=[out_spec, out_spec],
        out_shape=[out, out], compiler_params=_params(("parallel",)),
    )(oa, ob, dya, dyb)


def _mix_out_gate_bwd(name, dh, wo, gates, ya, yb):
    t = gates.shape[0]
    tm = _pick(t, (544, 384, 256, 128))

    def body(dh_ref, w_ref, g_ref, ya_ref, yb_ref, dya_ref, dyb_ref, dg_ref):
        dm = lax.dot_general(_bf(dh_ref[...]), w_ref[...], _NT, preferred_element_type=F32)
        sa = jax.nn.sigmoid(g_ref[:, 0:D_MODEL].astype(F32))
        sb = jax.nn.sigmoid(g_ref[:, D_MODEL:2 * D_MODEL].astype(F32))
        dya_ref[...] = (dm * sa).astype(BF16)
        dyb_ref[...] = (dm * sb).astype(BF16)
        dg_ref[:, 0:D_MODEL] = (dm * ya_ref[...].astype(F32) * (sa * (1.0 - sa))).astype(BF16)
        dg_ref[:, D_MODEL:2 * D_MODEL] = (dm * yb_ref[...].astype(F32) * (sb * (1.0 - sb))).astype(BF16)

    blk = pl.BlockSpec((tm, D_MODEL), lambda i: (i, 0))
    wide = pl.BlockSpec((tm, 2 * D_MODEL), lambda i: (i, 0))
    out = jax.ShapeDtypeStruct((t, D_MODEL), BF16)
    return pl.pallas_call(
        body, name=name, grid=(t // tm,),
        in_specs=[blk, pl.BlockSpec((D_MODEL, D_MODEL), lambda i: (0, 0)), wide, blk, blk], out_specs=[blk, blk, wide],
        out_shape=[out, out, jax.ShapeDtypeStruct((t, 2 * D_MODEL), BF16)], compiler_params=_params(("parallel",)),
    )(dh, wo, gates, ya, yb)


def _ffn_out_loss(name, a, w, res, gf, tgt, alpha):
    t, k = a.shape
    b, s, _ = tgt.shape
    l = t // b
    tm = _pick(t, (544, 384, 256, 128))
    per = l // tm
    assert per * tm == l and tm > PREFIX and l - PREFIX == s

    def body(a_ref, w_ref, r_ref, g_ref, t_ref, dh_ref, dhb_ref, loss_ref, dg_ref, buf, sem):
        i = pl.program_id(0)
        bi, r = i // per, i % per

        def first_rows():
            return pltpu.make_async_copy(t_ref.at[bi, pl.ds(0, tm - PREFIX)], buf.at[pl.ds(PREFIX, tm - PREFIX)], sem)

        def later_rows():
            return pltpu.make_async_copy(t_ref.at[bi, pl.ds(pl.multiple_of(r * tm - PREFIX, 8), tm)], buf, sem)

        @pl.when(r == 0)
        def _():
            buf[0:PREFIX, :] = jnp.zeros((PREFIX, D_MODEL), F32)
            first_rows().start()

        if per > 1:
            @pl.when(r > 0)
            def _():
                later_rows().start()

        acc = lax.dot_general(a_ref[...], w_ref[...], _NN, preferred_element_type=F32)

        @pl.when(r == 0)
        def _():
            first_rows().wait()

        if per > 1:
            @pl.when(r > 0)
            def _():
                later_rows().wait()

        hv = acc * alpha + r_ref[...]
        row = lax.broadcasted_iota(jnp.int32, (tm, 1), 0)
        real = ((r > 0) | (row >= PREFIX)).astype(F32)
        g = g_ref[...]
        rn = lax.rsqrt(jnp.mean(hv * hv, axis=-1, keepdims=True) + EPS)
        xn = hv * rn
        err = (xn * g - buf[...]) * real
        lpart = 0.5 * jnp.sum(jnp.mean(err * err, axis=-1, keepdims=True), axis=0, keepdims=True)
        dy = err * (1.0 / D_MODEL)
        tv = dy * g
        dot = jnp.mean(tv * hv, axis=-1, keepdims=True)
        dh = rn * tv - hv * (rn * rn * rn * dot)
        dh_ref[...] = dh
        dhb_ref[...] = dh.astype(BF16)
        gpart = jnp.sum(dy * xn, axis=0, keepdims=True)

        @pl.when(i == 0)
        def _():
            loss_ref[...] = jnp.zeros_like(loss_ref)
            dg_ref[...] = jnp.zeros_like(dg_ref)

        loss_ref[...] += jnp.broadcast_to(lpart, loss_ref.shape)
        dg_ref[...] += gpart

    row_spec = pl.BlockSpec((tm, D_MODEL), lambda i: (i, 0))
    vec = pl.BlockSpec((1, D_MODEL), lambda i: (0, 0))
    return pl.pallas_call(
        body, name=name, grid=(t // tm,),
        in_specs=[pl.BlockSpec((tm, k), lambda i: (i, 0)),
                  pl.BlockSpec((k, D_MODEL), lambda i: (0, 0), pipeline_mode=pl.Buffered(1)), row_spec, vec,
                  pl.BlockSpec(memory_space=pl.ANY)],
        out_specs=[row_spec, row_spec, pl.BlockSpec((8, 128), lambda i: (0, 0)), vec],
        out_shape=[jax.ShapeDtypeStruct((t, D_MODEL), F32), jax.ShapeDtypeStruct((t, D_MODEL), BF16),
                   jax.ShapeDtypeStruct((8, 128), F32), jax.ShapeDtypeStruct((1, D_MODEL), F32)],
        scratch_shapes=[pltpu.VMEM((tm, D_MODEL), F32), pltpu.SemaphoreType.DMA(())],
        compiler_params=_params(("arbitrary",)),
    )(a, w, res, gf, tgt)


def _fgate_fwd(name, f3, bf_row):
    b, l, _ = f3.shape
    nb = l // BLOCK

    def body(f_ref, b_ref, cc_ref, cr_ref):
        r_i = lax.broadcasted_iota(jnp.int32, (BLOCK, BLOCK), 0)
        c_i = lax.broadcasted_iota(jnp.int32, (BLOCK, BLOCK), 1)
        tri = (r_i >= c_i).astype(F32)
        carry = jnp.zeros((1, 128), F32)
        for blk in range(nb):
            rows = slice(blk * BLOCK, (blk + 1) * BLOCK)
            z = f_ref[rows, :] + b_ref[...]
            lf = jnp.minimum(z, 0.0) - jnp.log(1.0 + jnp.exp(-jnp.abs(z)))
            cb = jnp.dot(tri, lf, preferred_element_type=F32, precision=lax.Precision.HIGHEST) + carry
            carry = cb[BLOCK - 1:BLOCK, :]
            cbt = cb.T
            for hh in range(B_HEADS):
                cc_ref[hh, rows, :] = jnp.sum(jnp.where(c_i == hh, cb, 0.0), axis=1, keepdims=True)
                cr_ref[hh, :, rows] = cbt[hh:hh + 1, :]

    return pl.pallas_call(
        body, name=name, grid=(b,),
        in_specs=[pl.BlockSpec((None, l, 128), lambda bi: (bi, 0, 0)),
                  pl.BlockSpec((1, 128), lambda bi: (0, 0))],
        out_specs=[pl.BlockSpec((None, B_HEADS, l, 1), lambda bi: (bi, 0, 0, 0)),
                   pl.BlockSpec((None, B_HEADS, 1, l), lambda bi: (bi, 0, 0, 0))],
        out_shape=[jax.ShapeDtypeStruct((b, B_HEADS, l, 1), F32), jax.ShapeDtypeStruct((b, B_HEADS, 1, l), F32)],
        compiler_params=_params(("parallel",)),
    )(f3, bf_row)


def _fgate_bwd(name, f3, bf_row, dcq, dck):
    b, l, _ = f3.shape
    nb = l // BLOCK

    def body(f_ref, b_ref, dcq_ref, dck_ref, df_ref, db_ref):
        r_i = lax.broadcasted_iota(jnp.int32, (BLOCK, BLOCK), 0)
        c_i = lax.broadcasted_iota(jnp.int32, (BLOCK, BLOCK), 1)
        tri = (r_i <= c_i).astype(F32)
        carry = jnp.zeros((1, 128), F32)
        total = jnp.zeros((1, 128), F32)
        for blk in range(nb - 1, -1, -1):
            rows = slice(blk * BLOCK, (blk + 1) * BLOCK)
            krows = jnp.concatenate([dck_ref[hh, :, rows] for hh in range(B_HEADS)]
                                    + [jnp.zeros((BLOCK - B_HEADS, BLOCK), F32)], axis=0)
            dcb = krows.T
            for hh in range(B_HEADS):
                dcb = dcb + jnp.where(c_i == hh, dcq_ref[hh, rows, :], 0.0)
            rc = jnp.dot(tri, dcb, preferred_element_type=F32, precision=lax.Precision.HIGHEST) + carry
            carry = rc[0:1, :]
            z = f_ref[rows, :] + b_ref[...]
            df = rc * (1.0 / (1.0 + jnp.exp(z)))
            df_ref[rows, :] = df.astype(BF16)
            total = total + jnp.sum(df, axis=0, keepdims=True)

        @pl.when(pl.program_id(0) == 0)
        def _():
            db_ref[...] = total

        @pl.when(pl.program_id(0) > 0)
        def _():
            db_ref[...] += total

    return pl.pallas_call(
        body, name=name, grid=(b,),
        in_specs=[pl.BlockSpec((None, l, 128), lambda bi: (bi, 0, 0)),
                  pl.BlockSpec((1, 128), lambda bi: (0, 0)),
                  pl.BlockSpec((None, B_HEADS, l, 1), lambda bi: (bi, 0, 0, 0)),
                  pl.BlockSpec((None, B_HEADS, 1, l), lambda bi: (bi, 0, 0, 0))],
        out_specs=[pl.BlockSpec((None, l, 128), lambda bi: (bi, 0, 0)), pl.BlockSpec((1, 128), lambda bi: (0, 0))],
        out_shape=[jax.ShapeDtypeStruct((b, l, 128), BF16), jax.ShapeDtypeStruct((1, 128), F32)],
        compiler_params=_params(("arbitrary",)),
    )(f3, bf_row, dcq, dck)


A_Q_BLK = B_SEG // A_WIDTH
A_K_BLK = (B_SEG + A_WIDTH) // 128
A_V_BLK = A_K_BLK + 1
A_SEG_BLK = B_SEG // A_SEG
STACK = A_HEADS * BLOCK


def _lane_lo():
    return lax.broadcasted_iota(jnp.int32, (1, 128), 1) < HEAD_DIM


def _stack_heads(x, masked):
    lo = _lane_lo()
    blks = [x[:, 128 * j:128 * (j + 1)] for j in range(4)]
    if not masked:
        return jnp.concatenate(blks + blks, axis=0)
    zero = jnp.zeros_like(blks[0])
    return jnp.concatenate([jnp.where(lo, bk, zero) for bk in blks] + [jnp.where(lo, zero, bk) for bk in blks], axis=0)


def _unstack_heads(y):
    lo = _lane_lo()
    return jnp.concatenate([jnp.where(lo, y[128 * j:128 * (j + 1)], y[128 * (4 + j):128 * (5 + j)]) for j in range(4)], axis=1)


def _swa_bias(slopes):
    slopes = np.asarray(slopes, np.float32)
    r_i = np.arange(STACK)[:, None]
    c_i = np.arange(3 * BLOCK)[None, :]
    seg = c_i >> 7
    out = []
    for n in range(3):
        qpos = n * BLOCK + (r_i & (BLOCK - 1))
        kpos = np.where(seg == 0, c_i, (n - 2) * BLOCK + c_i)
        dist = qpos - kpos
        band = (seg != 0) & (dist < BLOCK) & (kpos >= PREFIX)
        meta = (seg == 0) & (c_i >= N_PAD)
        out.append(np.where((dist >= 0) & (band | meta), -slopes * dist.astype(np.float32), np.float32(NEG)))
    return jnp.asarray(np.stack(out, axis=0), F32)


def _swa_scores(q, kcat, n, slope, bias):
    s = lax.dot_general(q, kcat, _NT, preferred_element_type=F32) + bias
    further = slope * (-BLOCK * jnp.maximum(n - 2, 0)).astype(F32)
    return jnp.concatenate([s[:, 0:BLOCK] + further, s[:, BLOCK:]], axis=1)


def _swa_specs():
    def kv(col_blk):
        return [pl.BlockSpec((None, BLOCK, 128), lambda b, n: (b, 0, col_blk)),
                pl.BlockSpec((None, BLOCK, 128), lambda b, n: (b, jnp.maximum(n - 1, 0), col_blk)),
                pl.BlockSpec((None, BLOCK, 128), lambda b, n: (b, n, col_blk))]

    q_spec = pl.BlockSpec((None, BLOCK, A_WIDTH), lambda b, n: (b, n, A_Q_BLK))
    o_spec = pl.BlockSpec((None, BLOCK, A_WIDTH), lambda b, n: (b, n, 0))
    col = pl.BlockSpec((STACK, 1), lambda b, n: (0, 0))
    bias = pl.BlockSpec((None, STACK, 3 * BLOCK), lambda b, n: (jnp.minimum(n, 2), 0, 0))
    lse_spec = pl.BlockSpec((None, A_HEADS, BLOCK, 1), lambda b, n: (b, 0, n, 0))
    return q_spec, kv(A_K_BLK), kv(A_V_BLK), o_spec, [col, col, bias], lse_spec


def _swa_fwd(name, qkv, slopes, sinks, bias):
    b, l, _ = qkv.shape
    nb = l // BLOCK

    def body(q_ref, k0_ref, kp_ref, kc_ref, v0_ref, vp_ref, vc_ref, sl_ref, sk_ref, bias_ref, o_ref, o16_ref, lse_ref):
        n = pl.program_id(1)
        qs = _stack_heads(q_ref[...], True) * SCALE
        kcat = jnp.concatenate([k0_ref[...], kp_ref[...], kc_ref[...]], axis=0)
        vcat = jnp.concatenate([v0_ref[...], vp_ref[...], vc_ref[...]], axis=0)
        s = _swa_scores(qs, kcat, n, sl_ref[...], bias_ref[...])
        sink = sk_ref[...]
        m = jnp.maximum(jnp.max(s, axis=-1, keepdims=True), sink)
        p = jnp.exp(s - m)
        den = jnp.sum(p, axis=-1, keepdims=True) + jnp.exp(sink - m)
        o = lax.dot_general(p.astype(BF16), vcat, _NN, preferred_element_type=F32) / den
        ov = _unstack_heads(o)
        o_ref[...] = ov
        o16_ref[...] = ov.astype(BF16)
        lse_ref[...] = (m + jnp.log(den)).reshape(A_HEADS, BLOCK, 1)

    q_spec, k_specs, v_specs, o_spec, consts, lse_spec = _swa_specs()
    return pl.pallas_call(
        body, name=name, grid=(b, nb),
        in_specs=[q_spec] + k_specs + v_specs + consts, out_specs=[o_spec, o_spec, lse_spec],
        out_shape=[jax.ShapeDtypeStruct((b, l, A_WIDTH), F32), jax.ShapeDtypeStruct((b, l, A_WIDTH), BF16),
                   jax.ShapeDtypeStruct((b, A_HEADS, l, 1), F32)],
        compiler_params=_params(("parallel", "parallel")),
    )(qkv, qkv, qkv, qkv, qkv, qkv, qkv, slopes, sinks, bias)


def _swa_bwd(name, qkv, o, lse, do, slopes, sinks, bias, dqkv):
    b, l, _ = qkv.shape
    nb = l // BLOCK

    def body(q_ref, k0_ref, kp_ref, kc_ref, v0_ref, vp_ref, vc_ref, o_ref, lse_ref, do_ref, sl_ref, sk_ref, bias_ref, _,
             dx_ref, ds_ref, dk_acc, dv_acc):
        bi = pl.program_id(0)
        n = pl.program_id(1)
        qs = _stack_heads(q_ref[...], True) * SCALE
        dos32 = _stack_heads(do_ref[...], True)
        dos = dos32.astype(BF16)
        os_ = _stack_heads(o_ref[...], False)
        lsev = lse_ref[...].reshape(STACK, 1)
        kcat = jnp.concatenate([k0_ref[...], kp_ref[...], kc_ref[...]], axis=0)
        vcat = jnp.concatenate([v0_ref[...], vp_ref[...], vc_ref[...]], axis=0)
        s = _swa_scores(qs, kcat, n, sl_ref[...], bias_ref[...])
        p = jnp.exp(s - lsev)
        dsum = jnp.sum(dos32 * os_, axis=-1, keepdims=True)
        dp = lax.dot_general(dos, vcat, _NT, preferred_element_type=F32)
        dsc = (p * (dp - dsum)).astype(BF16)
        dq = lax.dot_general(dsc, kcat, _NN, preferred_element_type=F32) * SCALE
        row0 = pl.multiple_of(n * BLOCK, BLOCK)
        dx_ref[pl.ds(row0, BLOCK), 0:A_WIDTH] = _unstack_heads(dq).astype(BF16)
        dkc = lax.dot_general(dsc, qs, _TN, preferred_element_type=F32)
        dvc = lax.dot_general(p.astype(BF16), dos, _TN, preferred_element_type=F32)

        @pl.when(n == 0)
        def _():
            dk_acc[...] = jnp.zeros_like(dk_acc)
            dv_acc[...] = jnp.zeros_like(dv_acc)

        starts = (0, pl.multiple_of(jnp.maximum(n - 1, 0) * BLOCK, BLOCK), row0)
        for t, st in enumerate(starts):
            dk_acc[pl.ds(st, BLOCK), :] += dkc[t * BLOCK:(t + 1) * BLOCK, :]
            dv_acc[pl.ds(st, BLOCK), :] += dvc[t * BLOCK:(t + 1) * BLOCK, :]

        @pl.when(n == nb - 1)
        def _():
            dx_ref[:, A_WIDTH:A_WIDTH + 128] = dk_acc[...].astype(BF16)
            dx_ref[:, A_WIDTH + 128:A_SEG] = dv_acc[...].astype(BF16)

        dsink = -(jnp.exp(sk_ref[...] - lsev) * dsum)
        r8 = lax.broadcasted_iota(jnp.int32, (8, 128), 0)
        acc = jnp.zeros((8, 128), F32)
        for hh in range(A_HEADS):
            acc = acc + jnp.where(r8 == hh, jnp.sum(dsink[hh * BLOCK:(hh + 1) * BLOCK, :]), 0.0)

        @pl.when((bi == 0) & (n == 0))
        def _():
            ds_ref[...] = jnp.zeros_like(ds_ref)

        ds_ref[...] += acc

    q_spec, k_specs, v_specs, o_spec, consts, lse_spec = _swa_specs()
    return pl.pallas_call(
        body, name=name, grid=(b, nb),
        in_specs=[q_spec] + k_specs + v_specs + [o_spec, lse_spec, o_spec] + consts + [pl.BlockSpec(memory_space=pl.ANY)],
        out_specs=[pl.BlockSpec((None, l, A_SEG), lambda bb, n: (bb, 0, A_SEG_BLK)),
                   pl.BlockSpec((8, 128), lambda bb, n: (0, 0))],
        out_shape=[jax.ShapeDtypeStruct(dqkv.shape, BF16), jax.ShapeDtypeStruct((8, 128), F32)],
        scratch_shapes=[pltpu.VMEM((l, 128), F32), pltpu.VMEM((l, 128), F32)],
        input_output_aliases={13: 0},
        compiler_params=_params(("arbitrary", "arbitrary")),
    )(qkv, qkv, qkv, qkv, qkv, qkv, qkv, o, lse, do, slopes, sinks, bias, dqkv)


def _fox_mask(qk, ck, i):
    kh = qk.shape[1]
    qpos = i * BLOCK + lax.broadcasted_iota(jnp.int32, (BLOCK, kh), 0)
    kpos = lax.broadcasted_iota(jnp.int32, (BLOCK, kh), 1)
    return jnp.where((kpos <= qpos) & (kpos >= N_PAD), qk - ck, NEG)


def _pick_head(x, hh):
    lo = _lane_lo()
    return jnp.where(lo if hh == 0 else jnp.logical_not(lo), x, jnp.zeros_like(x))


def _both_heads(x):
    return jnp.concatenate([_pick_head(x, 0), _pick_head(x, 1)], axis=0)


def _fox_specs(l):
    pair = pl.BlockSpec((None, l, PAIR_W), lambda bi, hp: (bi, 0, hp))
    half = pl.BlockSpec((None, l, 128), lambda bi, hp: (bi, 0, hp))
    colv = pl.BlockSpec((None, 2, l, 1), lambda bi, hp: (bi, hp, 0, 0))
    rowv = pl.BlockSpec((None, 2, 1, l), lambda bi, hp: (bi, hp, 0, 0))
    return pair, half, colv, rowv


def _fox_fwd(name, qkv, c_col, c_row):
    b, l, _ = qkv.shape
    nb = l // BLOCK

    def body(x_ref, cc_ref, cr_ref, o_ref, lse_ref):
        for i in range(nb):
            rows = slice(i * BLOCK, (i + 1) * BLOCK)
            kh = (i + 1) * BLOCK
            qblk = x_ref[rows, 0:128]
            kv = x_ref[0:kh, 128:256]
            vv = x_ref[0:kh, 256:384]
            qk = lax.dot_general(_both_heads(qblk) * SCALE, kv, _NT, preferred_element_type=F32)
            ps, dens = [], []
            for hh in range(2):
                s = _fox_mask(qk[hh * BLOCK:(hh + 1) * BLOCK], cr_ref[hh, :, 0:kh], i)
                m = jnp.max(s, axis=-1, keepdims=True)
                p = jnp.exp(s - m)
                den = jnp.sum(p, axis=-1, keepdims=True)
                ps.append(p.astype(BF16))
                dens.append(den)
                lse_ref[hh, rows, :] = (m + jnp.log(den)) + cc_ref[hh, rows, :]
            pv = lax.dot_general(jnp.concatenate(ps, axis=0), vv, _NN, preferred_element_type=F32)
            o_ref[rows, :] = jnp.where(_lane_lo(), pv[0:BLOCK] / dens[0], pv[BLOCK:2 * BLOCK] / dens[1]).astype(BF16)

    pair, half, colv, rowv = _fox_specs(l)
    return pl.pallas_call(
        body, name=name, grid=(b, 4), in_specs=[pair, colv, rowv], out_specs=[half, colv],
        out_shape=[jax.ShapeDtypeStruct((b, l, B_WIDTH), BF16), jax.ShapeDtypeStruct((b, B_HEADS, l, 1), F32)],
        compiler_params=_params(("parallel", "parallel")),
    )(qkv, c_col, c_row)


def _fox_bwd(name, qkv, c_col, c_row, o, lse, do):
    b, l, _ = qkv.shape
    nb = l // BLOCK

    def body(x_ref, cc_ref, cr_ref, o_ref, lse_ref, do_ref, dx_ref, dcq_ref, dck_ref, dk_acc, dv_acc):
        dk_acc[...] = jnp.zeros_like(dk_acc)
        dv_acc[...] = jnp.zeros_like(dv_acc)
        dck_ref[...] = jnp.zeros_like(dck_ref)
        for i in range(nb):
            rows = slice(i * BLOCK, (i + 1) * BLOCK)
            kh = (i + 1) * BLOCK
            qblk = x_ref[rows, 0:128]
            kv = x_ref[0:kh, 128:256]
            vv = x_ref[0:kh, 256:384]
            doblk = do_ref[rows, :]
            ov = o_ref[rows, :].astype(F32)
            q2 = _both_heads(qblk) * SCALE
            do2 = _both_heads(doblk)
            qk = lax.dot_general(q2, kv, _NT, preferred_element_type=F32)
            dp = lax.dot_general(do2, vv, _NT, preferred_element_type=F32)
            ps, dss = [], []
            for hh in range(2):
                half = slice(hh * BLOCK, (hh + 1) * BLOCK)
                s = _fox_mask(qk[half], cr_ref[hh, :, 0:kh], i)
                p = jnp.exp(s - (lse_ref[hh, rows, :] - cc_ref[hh, rows, :]))
                dsum = jnp.sum(do2[half].astype(F32) * ov, axis=-1, keepdims=True)
                ds = p * (dp[half] - dsum)
                ps.append(p.astype(BF16))
                dss.append(ds.astype(BF16))
                dcq_ref[hh, rows, :] = jnp.sum(ds, axis=-1, keepdims=True)
                dck_ref[hh, :, 0:kh] -= jnp.sum(ds, axis=0, keepdims=True)
            p2 = jnp.concatenate(ps, axis=0)
            ds2 = jnp.concatenate(dss, axis=0)
            dq = lax.dot_general(ds2, kv, _NN, preferred_element_type=F32) * SCALE
            dk_acc[0:kh, :] += lax.dot_general(ds2, q2, _TN, preferred_element_type=F32)
            dv_acc[0:kh, :] += lax.dot_general(p2, do2, _TN, preferred_element_type=F32)
            dx_ref[rows, 0:128] = jnp.where(_lane_lo(), dq[0:BLOCK], dq[BLOCK:2 * BLOCK]).astype(BF16)
        dx_ref[:, 128:256] = dk_acc[...].astype(BF16)
        dx_ref[:, 256:384] = dv_acc[...].astype(BF16)

    pair, half, colv, rowv = _fox_specs(l)
    return pl.pallas_call(
        body, name=name, grid=(b, 4), in_specs=[pair, colv, rowv, half, colv, half],
        out_specs=[pair, colv, rowv],
        out_shape=[jax.ShapeDtypeStruct(qkv.shape, BF16), jax.ShapeDtypeStruct((b, B_HEADS, l, 1), F32),
                   jax.ShapeDtypeStruct((b, B_HEADS, 1, l), F32)],
        scratch_shapes=[pltpu.VMEM((l, 128), F32), pltpu.VMEM((l, 128), F32)],
        compiler_params=_params(("parallel", "parallel")),
    )(qkv, c_col, c_row, o, lse, do)


_FLIPS = ((0, 0, 1), (0, 1, 0), (0, 1, 1), (1, 0, 0), (1, 0, 1), (1, 1, 0), (1, 1, 1))


def _peer_table():
    x, y, c = lax.axis_index("x"), lax.axis_index("y"), lax.axis_index("c")
    me = 4 * x + 2 * y + c
    peers = []
    for fx, fy, fc in _FLIPS:
        px = 1 - x if fx else x
        py = 1 - y if fy else y
        pc = 1 - c if fc else c
        peers.append(((px, py, pc), 4 * px + 2 * py + pc))
    return me, peers


_HBM = pl.BlockSpec(memory_space=pltpu.HBM)
_SEM = pl.BlockSpec(memory_space=pltpu.SEMAPHORE)
_ANY = pl.BlockSpec(memory_space=pl.ANY)
_EFFECT = pltpu.SideEffectType.DATAFLOW_SIDE_EFFECTING


def _split_copy(srcs_are_pieces, src_refs, land_refs, send_sem, recv_sem, a, kk, me, peers, arriving):
    dev, lin = peers[kk]
    npeer = len(_FLIPS)
    src = src_refs[a] if srcs_are_pieces[a] else src_refs[a].at[lin]
    dst = land_refs[a].at[lin] if arriving else land_refs[a].at[me]
    return pltpu.make_async_remote_copy(src_ref=src, dst_ref=dst, send_sem=send_sem.at[a * npeer + kk],
                                        recv_sem=recv_sem.at[a * npeer + kk], device_id=dev, device_id_type=MESH_ID)


def _xchg_start(name, gather, scatter, after=None):
    me_out = 4 * lax.axis_index("x") + 2 * lax.axis_index("y") + lax.axis_index("c")
    srcs = list(gather) + list(scatter)
    is_piece = [True] * len(gather) + [False] * len(scatter)
    lands = []
    for a, piece in zip(srcs, is_piece):
        if piece:
            lands.append(lax.dynamic_update_slice(lax.empty((N_DEV,) + tuple(a.shape), a.dtype), a[None],
                                                  (me_out,) + (0,) * a.ndim))
        else:
            lands.append(lax.empty(tuple(a.shape), a.dtype))
    n = len(srcs)
    nsem = n * len(_FLIPS)
    has_after = after is not None

    def body(*refs):
        src_refs = refs[:n]
        land_refs = refs[n:2 * n]
        outs = refs[2 * n + (1 if has_after else 0):]
        send_sem, recv_sem = outs[0], outs[1]
        token = outs[-1]
        me, peers = _peer_table()
        for kk in range(len(_FLIPS)):
            for a in range(n):
                _split_copy(is_piece, src_refs, land_refs, send_sem, recv_sem, a, kk, me, peers, False).start()
        token[...] = jnp.zeros_like(token)

    out_shape = ([pltpu.SemaphoreType.DMA((nsem,)), pltpu.SemaphoreType.DMA((nsem,))]
                 + [pltpu.HBM(tuple(a.shape), a.dtype) for a in srcs] + [pltpu.HBM(tuple(a.shape), a.dtype) for a in lands]
                 + [jax.ShapeDtypeStruct((8, 128), F32)])
    args = [pltpu.with_memory_space_constraint(a, pltpu.HBM) for a in srcs + lands] + ([after] if has_after else [])
    res = pl.pallas_call(
        body, name=name, out_shape=out_shape,
        in_specs=[_HBM] * (2 * n) + ([_ANY] if has_after else []),
        out_specs=[_SEM, _SEM] + [_HBM] * (2 * n) + [pl.BlockSpec(memory_space=pltpu.VMEM)],
        input_output_aliases={i: 2 + i for i in range(2 * n)},
        compiler_params=pltpu.CompilerParams(has_side_effects=_EFFECT),
    )(*args)
    state = (res[0], res[1], list(res[2:2 + n]), list(res[2 + n:2 + 2 * n]), is_piece)
    return state, res[-1]


def _xchg_wait(name, state, after):
    send_sem, recv_sem, srcs, lands, is_piece = state
    n = len(srcs)

    def body(*refs):
        src_refs = refs[:n]
        land_refs = refs[n:2 * n]
        s_sem, r_sem = refs[2 * n], refs[2 * n + 1]
        me, peers = _peer_table()
        for kk in range(len(_FLIPS)):
            for a in range(n):
                cp = _split_copy(is_piece, src_refs, land_refs, s_sem, r_sem, a, kk, me, peers, True)
                cp.wait_send()
                cp.wait_recv()

    out_shape = [pltpu.HBM(tuple(a.shape), a.dtype) for a in srcs] + [pltpu.HBM(tuple(a.shape), a.dtype) for a in lands]
    res = pl.pallas_call(
        body, name=name, out_shape=out_shape,
        in_specs=[_HBM] * (2 * n) + [_SEM, _SEM, _ANY], out_specs=[_HBM] * (2 * n),
        input_output_aliases={i: i for i in range(2 * n)},
        compiler_params=pltpu.CompilerParams(has_side_effects=_EFFECT),
    )(*srcs, *lands, send_sem, recv_sem, after)
    return list(res[n:]), list(res[:n])


_SIB = (0, 0, 1)
_ICI = ((0, 1, 0), (1, 0, 0), (1, 1, 0))


def _flip(fl):
    x, y, c = lax.axis_index("x"), lax.axis_index("y"), lax.axis_index("c")
    px = 1 - x if fl[0] else x
    py = 1 - y if fl[1] else y
    pc = 1 - c if fl[2] else c
    return (px, py, pc), 4 * px + 2 * py + pc


def _gather2_start(name, pieces, after=None):
    me_out = 4 * lax.axis_index("x") + 2 * lax.axis_index("y") + lax.axis_index("c")
    pieces = list(pieces)
    n = len(pieces)
    lands = [lax.dynamic_update_slice(lax.empty((N_DEV,) + tuple(a.shape), a.dtype), a[None],
                                      (me_out,) + (0,) * a.ndim) for a in pieces]
    first = (_SIB,) + _ICI
    has_after = after is not None

    def body(*refs):
        src_refs, land_refs = refs[:n], refs[n:2 * n]
        outs = refs[2 * n + (1 if has_after else 0):]
        send_sem, recv_sem, token = outs[0], outs[1], outs[-1]
        _, me = _flip((0, 0, 0))
        for kk, fl in enumerate(first):
            dev, _ = _flip(fl)
            for a in range(n):
                pltpu.make_async_remote_copy(src_ref=src_refs[a], dst_ref=land_refs[a].at[me],
                                             send_sem=send_sem.at[a * 4 + kk], recv_sem=recv_sem.at[a * 4 + kk],
                                             device_id=dev, device_id_type=MESH_ID).start()
        token[...] = jnp.zeros_like(token)

    hbm = [pltpu.HBM(tuple(a.shape), a.dtype) for a in pieces + lands]
    res = pl.pallas_call(
        body, name=name,
        out_shape=[pltpu.SemaphoreType.DMA((4 * n,)), pltpu.SemaphoreType.DMA((4 * n,))] + hbm
        + [jax.ShapeDtypeStruct((8, 128), F32)],
        in_specs=[_HBM] * (2 * n) + ([_ANY] if has_after else []),
        out_specs=[_SEM, _SEM] + [_HBM] * (2 * n) + [pl.BlockSpec(memory_space=pltpu.VMEM)],
        input_output_aliases={i: 2 + i for i in range(2 * n)},
        compiler_params=pltpu.CompilerParams(has_side_effects=_EFFECT),
    )(*([pltpu.with_memory_space_constraint(a, pltpu.HBM) for a in pieces + lands] + ([after] if has_after else [])))
    return (res[0], res[1], list(res[2:2 + n]), list(res[2 + n:2 + 2 * n])), res[-1]


def _gather2_forward(name, state, after):
    send_a, recv_a, pieces, lands = state
    n = len(pieces)
    first = (_SIB,) + _ICI

    def body(*refs):
        src_refs, land_refs = refs[:n], refs[n:2 * n]
        s_a, r_a = refs[2 * n], refs[2 * n + 1]
        outs = refs[2 * n + 3:]
        send_b, recv_b, token = outs[0], outs[1], outs[-1]
        for kk, fl in enumerate(first):
            dev, lin = _flip(fl)
            for a in range(n):
                cp = pltpu.make_async_remote_copy(src_ref=src_refs[a], dst_ref=land_refs[a].at[lin],
                                                  send_sem=s_a.at[a * 4 + kk], recv_sem=r_a.at[a * 4 + kk],
                                                  device_id=dev, device_id_type=MESH_ID)
                cp.wait_send()
                cp.wait_recv()
        sib, _ = _flip(_SIB)
        for j, fl in enumerate(_ICI):
            _, lin = _flip(fl)
            for a in range(n):
                pltpu.make_async_remote_copy(src_ref=land_refs[a].at[lin], dst_ref=land_refs[a].at[lin],
                                             send_sem=send_b.at[a * 3 + j], recv_sem=recv_b.at[a * 3 + j],
                                             device_id=sib, device_id_type=MESH_ID).start()
        token[...] = jnp.zeros_like(token)

    hbm = [pltpu.HBM(tuple(a.shape), a.dtype) for a in pieces + lands]
    res = pl.pallas_call(
        body, name=name,
        out_shape=[pltpu.SemaphoreType.DMA((3 * n,)), pltpu.SemaphoreType.DMA((3 * n,))] + hbm
        + [jax.ShapeDtypeStruct((8, 128), F32)],
        in_specs=[_HBM] * (2 * n) + [_SEM, _SEM, _ANY],
        out_specs=[_SEM, _SEM] + [_HBM] * (2 * n) + [pl.BlockSpec(memory_space=pltpu.VMEM)],
        input_output_aliases={i: 2 + i for i in range(2 * n)},
        compiler_params=pltpu.CompilerParams(has_side_effects=_EFFECT),
    )(*pieces, *lands, send_a, recv_a, after)
    return (res[0], res[1], list(res[2 + n:2 + 2 * n])), res[-1]


def _gather2_wait(name, state, after):
    send_b, recv_b, lands = state
    n = len(lands)

    def body(*refs):
        land_refs = refs[:n]
        s_b, r_b = refs[n], refs[n + 1]
        sib, _ = _flip(_SIB)
        for j, fl in enumerate(_ICI):
            _, sent = _flip(fl)
            _, arriving = _flip((fl[0], fl[1], 1))
            for a in range(n):
                cp = pltpu.make_async_remote_copy(src_ref=land_refs[a].at[sent], dst_ref=land_refs[a].at[arriving],
                                                  send_sem=s_b.at[a * 3 + j], recv_sem=r_b.at[a * 3 + j],
                                                  device_id=sib, device_id_type=MESH_ID)
                cp.wait_send()
                cp.wait_recv()

    res = pl.pallas_call(
        body, name=name, out_shape=[pltpu.HBM(tuple(a.shape), a.dtype) for a in lands],
        in_specs=[_HBM] * n + [_SEM, _SEM, _ANY], out_specs=[_HBM] * n,
        input_output_aliases={i: i for i in range(n)},
        compiler_params=pltpu.CompilerParams(has_side_effects=_EFFECT),
    )(*lands, send_b, recv_b, after)
    return list(res)


def _adam_math(w, g, m, v):
    m = ADAM_B1 * m + (1.0 - ADAM_B1) * g
    v = ADAM_B2 * v + (1.0 - ADAM_B2) * (g * g)
    m_hat = m / (1.0 - ADAM_B1 ** ADAM_STEP)
    v_hat = v / (1.0 - ADAM_B2 ** ADAM_STEP)
    delta = -ADAM_LR * (m_hat / (jnp.sqrt(v_hat) + ADAM_EPS) + ADAM_WD * w)
    return delta, m, v


def _adam(name, w, m, v, parts, transposed=False, dep=None, own=None):
    npart, _, cp = parts.shape
    has_dep = dep is not None
    has_own = own is not None
    if transposed:
        c, r = w.shape
        tr = _pick(r, (256, 128))
        blk = pl.BlockSpec((c, tr), lambda i, *_: (0, i))
    else:
        r, c = w.shape
        tr = _pick(r, (256, 176, 128, 64, 16, 8, 1))
        blk = pl.BlockSpec((tr, c), lambda i, *_: (i, 0))

    def body(*refs):
        refs = list(refs)
        me = refs.pop(0)[0] if has_own else None
        w_ref, m_ref, v_ref, p_ref = refs[:4]
        own_ref = refs[4] if has_own else None
        g_ref, d_ref, mo_ref, vo_ref = refs[-4:]
        g = None
        for pp in range(npart):
            part = p_ref[pp] if not has_own else jnp.where(me == pp, own_ref[...], p_ref[pp])
            g = part.astype(F32) if g is None else g + part.astype(F32)
        g = g.T[0:c, :] if transposed else g[:, 0:c]
        delta, mn, vn = _adam_math(w_ref[...], g, m_ref[...], v_ref[...])
        g_ref[...] = g
        d_ref[...] = delta
        mo_ref[...] = mn
        vo_ref[...] = vn

    in_specs = [blk, blk, blk, pl.BlockSpec((npart, tr, cp), lambda i, *_: (0, i, 0))]
    args = [w, m, v, parts]
    if has_own:
        in_specs.append(pl.BlockSpec((None, tr, cp), lambda i, me_ref: (me_ref[0], i, 0)))
        args.append(own[0])
    if has_dep:
        in_specs.append(pl.BlockSpec(memory_space=pl.ANY))
        args.append(dep)
    out = jax.ShapeDtypeStruct(w.shape, F32)
    grid_spec = pltpu.PrefetchScalarGridSpec(num_scalar_prefetch=1 if has_own else 0, grid=(r // tr,),
                                             in_specs=in_specs, out_specs=[blk, blk, blk, blk])
    return pl.pallas_call(
        body, name=name, grid_spec=grid_spec, out_shape=[out, out, out, out], compiler_params=_params(("parallel",)),
    )(*(([own[1]] if has_own else []) + args))


def _small_update(name, packs, wmv):
    nparam = len(wmv)

    def body(p_ref, *refs):
        ins, outs = refs[:3 * nparam], refs[3 * nparam:]
        tot = p_ref[0]
        for pp in range(1, N_DEV):
            tot = tot + p_ref[pp]
        outs[0][0:8, :] = tot[0:8, :]
        outs[0][8:24, :] = tot[8:24, :] + tot[24:40, :]
        grads = [tot[i:i + 1, :] for i in range(4)] + [tot[4:5, 0:B_HEADS], tot[4:5, B_HEADS:2 * B_HEADS]]
        for i, g in enumerate(grads):
            w_ref, m_ref, v_ref = ins[3 * i:3 * i + 3]
            delta, mn, vn = _adam_math(w_ref[...], g, m_ref[...], v_ref[...])
            for o_ref, val in zip(outs[1 + 4 * i:5 + 4 * i], (g, delta, mn, vn)):
                o_ref[...] = val

    flat = [a for trio in wmv for a in trio]
    out_shape = [jax.ShapeDtypeStruct((24, D_MODEL), F32)]
    for w, _, _ in wmv:
        out_shape += [jax.ShapeDtypeStruct(w.shape, F32)] * 4
    res = pl.pallas_call(body, name=name, out_shape=out_shape, compiler_params=_params())(packs, *flat)
    return res[0], [tuple(res[1 + 4 * i:5 + 4 * i]) for i in range(nparam)]


def _local_step(x, tgt, g1, gm, g2, gf, b_forget, sinks, weights, send):
    b, s, _ = x.shape
    l = s + PREFIX
    t = b * l
    (meta,) = weights("meta", x)

    h0, n1 = _embed_norm("embed_rms1_fwd", x, meta, g1)
    (w1i,) = weights("ffn1_in", n1)
    gu1, a1 = _ffn_in_fwd("ffn1_in_fwd", n1, w1i)
    (w1o,) = weights("ffn1_out", weights("ffn1_out:forward", a1))
    h1, um = _mm_res_norm("ffn1_out_fwd", a1, w1o, h0, gm, alpha=0.5)
    wi, wa, wb, wo = weights("mix", weights("mix:forward", um))
    qkv, gates, f2 = _proj_fwd("proj_fwd", um, wi)
    qkv3 = qkv.reshape(b, l, QKV_W)
    f3 = f2.reshape(b, l, 128)
    bf_row = jnp.pad(b_forget, ((0, 0), (0, 128 - B_HEADS)))
    c_col, c_row = _fgate_fwd("fgate_fwd", f3, bf_row)
    head_of_row = np.arange(STACK) // BLOCK
    slopes_np = np.exp2(-8.0 * (head_of_row + 1) / A_HEADS).astype(np.float32).reshape(STACK, 1)
    slopes = jnp.asarray(slopes_np)
    sink_rows = jnp.repeat(sinks.reshape(A_HEADS), BLOCK).reshape(STACK, 1)
    swa_bias = _swa_bias(slopes_np)
    oa3, oa3_bf, lse_a = _swa_fwd("swa_fwd", qkv3, slopes, sink_rows, swa_bias)
    ob3, lse_b = _fox_fwd("fox_fwd", qkv3, c_col, c_row)
    oa = oa3_bf.reshape(t, A_WIDTH)
    ob = ob3.reshape(t, B_WIDTH)
    mixed, ya, yb = _branch_gate_fwd("branch_gate_fwd", oa, ob, wa, wb, gates, dep=weights("ffn2:forward", ob))
    h2, n2 = _mm_res_norm("mix_out_fwd", mixed, wo, h1, g2)
    w2i, w2o = weights("ffn2", h2)
    gu2, a2 = _ffn_in_fwd("ffn2_in_fwd", n2, w2i)

    dh3, dh3_bf, loss_blk, dgf = _ffn_out_loss("ffn2_out_loss", a2, w2o, h2, gf, tgt, 0.5)

    def ffn_bwd(tag, dh, dh_bf, h_in, g_norm, n_in, gu, a, w_in_blk, w_out, one_send, examples=None):
        dw_out = _ffn_out_bwd_w(tag + "_out_bwd_w", a, dh_bf)
        dgu = _ffn_out_bwd_x(tag + "_out_bwd_x", dh_bf, w_out, gu, dep=None if one_send else send(tag + "_out", (dw_out,)))
        dw_in = _ffn_in_bwd_w(tag + "_in_bwd_w", n_in, dgu)
        token = send(tag, (dw_in, dw_out)) if one_send else send(tag + "_in", (dw_in,))
        return _ffn_in_bwd_x(tag + "_in_bwd_x", dgu, w_in_blk, h_in, g_norm, dh, dep=token, examples=examples)

    dh2, dh2_bf, dg2 = ffn_bwd("ffn2", dh3, dh3_bf, h2, g2, n2, gu2, a2, w2i, w2o, True)

    dwo = _mm_tn("mix_out_bwd_w", mixed, dh2_bf)
    dya, dyb, dgates = _mix_out_gate_bwd("mix_out_gate_bwd", dh2_bf, wo, gates, ya, yb)
    doa, dob = _branch_bwd_x("branch_bwd_x", dya, dyb, wa, wb)
    dwa, dwb = _branch_bwd_w("branch_bwd_w", oa, ob, dya, dyb)
    dqkv3, dcq, dck = _fox_bwd("fox_bwd", qkv3, c_col, c_row, ob3, lse_b, dob.reshape(b, l, B_WIDTH))
    dqkv3, dsink = _swa_bwd("swa_bwd", qkv3, oa3, lse_a, doa.reshape(b, l, A_WIDTH), slopes, sink_rows, swa_bias, dqkv3)
    dqkv = dqkv3.reshape(t, QKV_W)
    df3, dbf = _fgate_bwd("fgate_bwd", f3, bf_row, dcq, dck)
    df = df3.reshape(t, 128)
    dwi_qkv = _mm_tn("proj_qkv_bwd_w", um, dqkv, tm_c=(512,), tn_c=(768,))
    dwi_g = _mm_tn("proj_gates_bwd_w", um, dgates, tm_c=(512,), tn_c=(512,))
    dwi_f = _mm_tn("proj_f_bwd_w", um, df, tm_c=(512,), tn_c=(128,))
    token = send("mix", (dwi_qkv, dwi_g, dwi_f, dwa, dwb, dwo))
    dh1, dh1_bf, dgm = _proj_bwd_x("proj_bwd_x", dqkv, dgates, df, wi, h1, gm, dh2, dep=token)

    grad_x, dmeta3, dg1 = ffn_bwd("ffn1", dh1, dh1_bf, h0, g1, n1, gu1, a1, w1i, w1o, False, examples=b)
    dmeta = dmeta3.reshape(b * N_META, D_MODEL)

    misc = jnp.concatenate([dbf[:, 0:B_HEADS], dsink[:, 0].reshape(1, A_HEADS), loss_blk[0:1, 0:1]], axis=1)
    misc = jnp.pad(misc, ((0, 0), (0, D_MODEL - misc.shape[1])))
    row = lax.broadcasted_iota(jnp.int32, (8, D_MODEL), 0)
    vec = jnp.zeros((8, D_MODEL), F32)
    for i, piece in enumerate((dg1, dgm, dg2, dgf, misc)):
        vec = jnp.where(row == i, piece, vec)
    small = jnp.concatenate([vec, dmeta], axis=0)
    return grad_x, small


def _pad_to(a, rows, cols):
    return jnp.pad(a, ((0, rows - a.shape[0]), (0, cols - a.shape[1])))


def _ffn_out_from_gathered(name, g):
    def body(g_ref, o_ref):
        o_ref[0:FFO_SHARD, :] = g_ref[0]
        o_ref[FFO_SHARD:FF_SHARD, :] = g_ref[1]
        o_ref[FF_SHARD:FF_SHARD_P, :] = jnp.zeros((FF_SHARD_P - FF_SHARD, D_MODEL), BF16)

    return pl.pallas_call(
        body, name=name, grid=(4,),
        in_specs=[pl.BlockSpec((2, FFO_SHARD, D_MODEL), lambda j: (j, 0, 0))],
        out_specs=pl.BlockSpec((FF_SHARD_P, D_MODEL), lambda j: (j, 0)),
        out_shape=jax.ShapeDtypeStruct((D_FF_P, D_MODEL), BF16), compiler_params=_params(("parallel",)),
    )(g)


def _ffn_out_bwd_w(name, a, dh):
    t = a.shape[0]
    tn = D_MODEL // 2

    def body(a_ref, b_ref, o_ref):
        acc = lax.dot_general(a_ref[...], _bf(b_ref[...]), _TN, preferred_element_type=F32) * 0.5
        o_ref[0] = acc[0:FFO_SHARD].astype(BF16)
        o_ref[1] = acc[FFO_SHARD:FF_SHARD].astype(BF16)

    return pl.pallas_call(
        body, name=name, grid=(2, 4),
        in_specs=[pl.BlockSpec((t, FF_SHARD_P), lambda j, i: (0, i)), pl.BlockSpec((t, tn), lambda j, i: (0, j))],
        out_specs=pl.BlockSpec((2, FFO_SHARD, tn), lambda j, i: (i, 0, j)),
        out_shape=jax.ShapeDtypeStruct((N_DEV, FFO_SHARD, D_MODEL), BF16), compiler_params=_params(("parallel", "parallel")),
    )(a, dh)


def _proj_segments():
    segs = [(HEAD_DIM * h, HEAD_DIM, 0, B_SEG + HEAD_DIM * A_HEAD_ORDER.index(h)) for h in range(A_HEADS)]
    segs += [(512, 128, 0, B_SEG + A_WIDTH), (640, 128, 0, B_SEG + A_WIDTH + 128)]
    for first, off in ((768, 0), (1280, 128), (1792, 256)):
        segs += [(first + 128 * hp, 128, 0, PAIR_W * hp + off) for hp in range(4)]
    segs += [(2304, B_HEADS, 2, 0), (2312, 2 * D_MODEL, 1, 0)]
    return segs


_RELAYOUT_ROWS = 256


def _proj_from_gathered(name, g):
    rows = _RELAYOUT_ROWS

    def body(g_ref, o_ref):
        def cols(first, width):
            out = []
            for p in range(N_DEV):
                lo, hi = max(first, WIN_SHARD * p), min(first + width, WIN_SHARD * (p + 1))
                if lo < hi:
                    out.append(g_ref[p, :, lo - WIN_SHARD * p:hi - WIN_SHARD * p])
            return out

        parts = []
        for arr in (0, 1, 2):
            for first, width, _, _ in sorted((s for s in _proj_segments() if s[2] == arr), key=lambda s: s[3]):
                parts += cols(first, width)
            if arr == 0:
                parts.append(jnp.zeros((rows, P_GATES - QKV_W), BF16))
        parts.append(jnp.zeros((rows, 128 - B_HEADS), BF16))
        o_ref[...] = jnp.concatenate(parts, axis=1)

    return pl.pallas_call(
        body, name=name, grid=(D_MODEL // rows,),
        in_specs=[pl.BlockSpec((N_DEV, rows, WIN_SHARD_P), lambda i: (0, i, 0))],
        out_specs=pl.BlockSpec((rows, PROJ_P), lambda i: (i, 0)),
        out_shape=jax.ShapeDtypeStruct((D_MODEL, PROJ_P), BF16), compiler_params=_params(("parallel",)),
    )(g)


def _proj_to_scatter(name, dqkv_w, dg_w, df_w):
    rows = _RELAYOUT_ROWS
    segs = sorted(_proj_segments())

    def body(q_ref, g_ref, f_ref, o_ref):
        arrays = (q_ref, g_ref, f_ref)
        for p in range(N_DEV):
            parts = []
            for first, width, arr, at in segs:
                lo, hi = max(first, WIN_SHARD * p), min(first + width, WIN_SHARD * (p + 1))
                if lo < hi:
                    parts.append(arrays[arr][:, at + lo - first:at + hi - first])
            parts.append(jnp.zeros((rows, WIN_SHARD_P - WIN_SHARD), BF16))
            o_ref[p] = jnp.concatenate(parts, axis=1)

    return pl.pallas_call(
        body, name=name, grid=(D_MODEL // rows,),
        in_specs=[pl.BlockSpec((rows, QKV_W), lambda i: (i, 0)), pl.BlockSpec((rows, 2 * D_MODEL), lambda i: (i, 0)),
                  pl.BlockSpec((rows, 128), lambda i: (i, 0))],
        out_specs=pl.BlockSpec((N_DEV, rows, WIN_SHARD_P), lambda i: (0, i, 0)),
        out_shape=jax.ShapeDtypeStruct((N_DEV, D_MODEL, WIN_SHARD_P), BF16), compiler_params=_params(("parallel",)),
    )(dqkv_w, dg_w, df_w)


def _a_rows_from_natural(w):
    return jnp.concatenate([w[HEAD_DIM * h:HEAD_DIM * (h + 1)] for h in A_HEAD_ORDER], axis=0)


def _a_rows_to_natural(w):
    return jnp.concatenate([w[HEAD_DIM * A_HEAD_ORDER.index(h):HEAD_DIM * (A_HEAD_ORDER.index(h) + 1)]
                            for h in range(A_HEADS)], axis=0)


def kernel(x, meta_tokens, ffn1_norm, ffn1_w_in, ffn1_w_out, mix_norm, w_in, b_forget, attn_sinks, w_branch_a, w_branch_b, w_out, ffn2_norm, ffn2_w_in, ffn2_w_out, final_norm, loss_target, m_meta_tokens, m_ffn1_norm, m_ffn1_w_in, m_ffn1_w_out, m_mix_norm, m_w_in, m_b_forget, m_attn_sinks, m_w_branch_a, m_w_branch_b, m_w_out, m_ffn2_norm, m_ffn2_w_in, m_ffn2_w_out, m_final_norm, v_meta_tokens, v_ffn1_norm, v_ffn1_w_in, v_ffn1_w_out, v_mix_norm, v_w_in, v_b_forget, v_attn_sinks, v_w_branch_a, v_w_branch_b, v_w_out, v_ffn2_norm, v_ffn2_w_in, v_ffn2_w_out, v_final_norm):
    me = 4 * lax.axis_index("x") + 2 * lax.axis_index("y") + lax.axis_index("c")

    def shard(w, tok, rows=None, cols=None):
        piece = (w[0] if tok is None else w[0] + tok[0, 0]).astype(BF16)
        return piece if rows is None else _pad_to(piece, rows, cols)

    first_level, second_level = {}, {}
    first_level["meta"], tok = _gather2_start("gather_meta_start", (meta_tokens,))
    first_level["ffn1_in"], tok = _gather2_start(
        "gather_ffn1_in_start", (shard(ffn1_w_in, None, D_MODEL, FF_SHARD_P),), after=tok)
    first_level["ffn1_out"], tok = _gather2_start("gather_ffn1_out_start", (shard(ffn1_w_out, tok),), after=tok)
    first_level["mix"], tok = _gather2_start(
        "gather_mix_start", (shard(w_in, tok, D_MODEL, WIN_SHARD_P), shard(w_branch_a, tok), shard(w_branch_b, tok),
                             shard(w_out, tok)), after=tok)
    first_level["ffn2"], tok = _gather2_start(
        "gather_ffn2_start", (shard(ffn2_w_in, tok, D_MODEL, FF_SHARD_P), shard(ffn2_w_out, tok)), after=tok)
    started = {"tok": tok}

    def weights(group, after):
        if group.endswith(":forward"):
            group = group[:-len(":forward")]
            second_level[group], token = _gather2_forward("gather_" + group + "_forward", first_level[group], after)
            return token
        if group == "meta":
            after = weights("meta:forward", started["tok"])
        if group == "ffn1_in":
            after = weights("ffn1_in:forward", after)
        got = _gather2_wait("gather_" + group + "_wait", second_level[group], after)
        if group == "mix":
            gwi, gwa, gwb, gwo = got
            return (_proj_from_gathered("proj_w_relayout", gwi), _a_rows_from_natural(gwa.transpose(1, 0, 2).reshape(A_WIDTH, D_MODEL)),
                    gwb.transpose(1, 0, 2).reshape(B_WIDTH, D_MODEL), gwo.reshape(D_MODEL, D_MODEL))
        if group == "ffn1_out":
            return (_ffn_out_from_gathered("ffn1_w_out_relayout", got[0]),)
        if group == "meta":
            return (got[0].transpose(1, 0, 2).reshape(N_META, D_MODEL),)
        if group == "ffn1_in":
            return (got[0].reshape(2, 4, D_MODEL, FF_SHARD_P),)
        return got[0].reshape(2, 4, D_MODEL, FF_SHARD_P), _ffn_out_from_gathered("ffn2_w_out_relayout", got[1])

    scatter_state = {}

    def send(group, grads):
        if group == "mix":
            dwi_qkv, dwi_g, dwi_f, dwa, dwb, dwo = grads
            dwa = _a_rows_to_natural(dwa)
            blocks = (_proj_to_scatter("proj_dw_relayout", dwi_qkv, dwi_g, dwi_f), dwa.reshape(A_WIDTH, N_DEV, 128).transpose(1, 0, 2),
                      dwb.reshape(B_WIDTH, N_DEV, 128).transpose(1, 0, 2), dwo.reshape(N_DEV, 128, D_MODEL))
        elif group.endswith("_in"):
            blocks = (grads[0].reshape(N_DEV, D_MODEL, FF_SHARD_P),)
        elif group.endswith("_out"):
            blocks = (grads[0],)
        else:
            blocks = (grads[0].reshape(N_DEV, D_MODEL, FF_SHARD_P), grads[1])
        scatter_state[group], token = _xchg_start("scatter_" + group + "_start", (), blocks)
        return token

    gf = final_norm.reshape(1, D_MODEL)
    grad_x, small = _local_step(x, loss_target, ffn1_norm, mix_norm, ffn2_norm, gf, b_forget, attn_sinks, weights, send)

    small_state, after = _xchg_start("gather_small_start", (small,), ())
    out = {}
    updates = (
        ("ffn2", (("ffn2_w_in", ffn2_w_in, m_ffn2_w_in, v_ffn2_w_in), ("ffn2_w_out", ffn2_w_out, m_ffn2_w_out, v_ffn2_w_out))),
        ("ffn1_out", (("ffn1_w_out", ffn1_w_out, m_ffn1_w_out, v_ffn1_w_out),)),
        ("mix", (("w_in", w_in, m_w_in, v_w_in), ("w_branch_a", w_branch_a, m_w_branch_a, v_w_branch_a),
                 ("w_branch_b", w_branch_b, m_w_branch_b, v_w_branch_b), ("w_out", w_out, m_w_out, v_w_out))),
    )
    last_update = ("ffn1_in", (("ffn1_w_in", ffn1_w_in, m_ffn1_w_in, v_ffn1_w_in),))

    def update(group, members, after):
        parts_list, blocks_list = _xchg_wait("scatter_" + group + "_wait", scatter_state[group], after)
        prev = None
        for (nm, w, m, v), parts, blocks in zip(members, parts_list, blocks_list):
            if nm.endswith("w_in"):
                res4 = _adam("adam_" + nm, w[0].T, m[0].T, v[0].T, parts, transposed=True, dep=prev, own=(blocks, me_arr))
                out[nm] = tuple(r.T[None] for r in res4)
            else:
                res4 = _adam("adam_" + nm, w[0], m[0], v[0], parts, dep=prev, own=(blocks, me_arr))
                out[nm] = tuple(r[None] for r in res4)
            prev = res4[0]
        return prev

    me_arr = me.reshape(1).astype(jnp.int32)
    for group, members in updates + (last_update,):
        after = update(group, members, after)
    (packs,), _ = _xchg_wait("gather_small_wait", small_state, after)

    row = lambda a: a.reshape(1, D_MODEL)
    tot, small_res = _small_update("small_update", packs, (
        (ffn1_norm, m_ffn1_norm, v_ffn1_norm), (mix_norm, m_mix_norm, v_mix_norm), (ffn2_norm, m_ffn2_norm, v_ffn2_norm),
        (row(final_norm), row(m_final_norm), row(v_final_norm)), (b_forget, m_b_forget, v_b_forget),
        (attn_sinks, m_attn_sinks, v_attn_sinks)))
    for nm, res4 in zip(("ffn1_norm", "mix_norm", "ffn2_norm", "final_norm", "b_forget", "attn_sinks"), small_res):
        out[nm] = tuple(r.reshape(D_MODEL) for r in res4) if nm == "final_norm" else res4
    loss = tot[4, 2 * B_HEADS]
    g_meta = lax.dynamic_slice(tot[8:24, :], (0, me * 128), (N_META, 128))
    out["meta_tokens"] = tuple(_adam("adam_meta_tokens", meta_tokens, m_meta_tokens, v_meta_tokens, g_meta[None]))

    names = ("meta_tokens", "ffn1_norm", "ffn1_w_in", "ffn1_w_out", "mix_norm", "w_in", "b_forget", "attn_sinks",
             "w_branch_a", "w_branch_b", "w_out", "ffn2_norm", "ffn2_w_in", "ffn2_w_out", "final_norm")
    return (loss, grad_x) + tuple(out[nm][kind] for kind in range(4) for nm in names)
```

```python
import jax
import jax.numpy as jnp
import numpy as np
from jax import lax
from jax.experimental import pallas as pl
from jax.experimental.pallas import tpu as pltpu

F32 = jnp.float32
BF16 = jnp.bfloat16

D_MODEL = 1024
N_META = 16
BLOCK = 128
PREFIX = 128
N_PAD = PREFIX - N_META
HEAD_DIM = 64
A_HEADS = 8
B_HEADS = 8
A_WIDTH = 512
A_KV_WIDTH = 128
B_WIDTH = 512
D_FF = 2816
N_DEV = 8
FF_SHARD = 2 * D_FF // N_DEV
FF_SHARD_P = 768
FFO_SHARD = D_FF // N_DEV
D_FF_P = 4 * FF_SHARD_P
W_IN_COLS = 4360
WIN_SHARD = W_IN_COLS // N_DEV
WIN_SHARD_P = 640
PAIR_W = 3 * 128
B_SEG = 4 * PAIR_W
A_SEG = A_WIDTH + 2 * A_KV_WIDTH
QKV_W = B_SEG + A_SEG
P_GATES = 2 * (2 * D_MODEL)
P_F = P_GATES + 2 * D_MODEL
PROJ_P = P_F + 128
A_HEAD_ORDER = (0, 4, 1, 5, 2, 6, 3, 7)
EPS = 1e-6
NEG = -1e30
SCALE = HEAD_DIM ** -0.5
ADAM_LR = 0.001
ADAM_B1 = 0.9
ADAM_B2 = 0.999
ADAM_EPS = 1e-08
ADAM_WD = 0.01
ADAM_STEP = 10
VMEM_LIMIT = 56 * 1024 * 1024
MESH_ID = pl.DeviceIdType.MESH
SMALL_ROWS = 40

_NN = (((1,), (0,)), ((), ()))
_NT = (((1,), (1,)), ((), ()))
_TN = (((0,), (0,)), ((), ()))


def _params(sem=None):
    return pltpu.CompilerParams(dimension_semantics=sem, vmem_limit_bytes=VMEM_LIMIT)


def _pick(n, cands):
    for c in cands:
        if n % c == 0:
            return c
    raise ValueError(f"no tile for {n}")


def _bf(v):
    return v if v.dtype == BF16 else v.astype(BF16)


def _mm(name, a, b, dims, grid, a_spec, b_spec, o_spec, out_shape, out_dtype, alpha=1.0):
    def body(a_ref, b_ref, o_ref):
        acc = lax.dot_general(_bf(a_ref[...]), _bf(b_ref[...]), dims, preferred_element_type=F32)
        if alpha != 1.0:
            acc = acc * alpha
        o_ref[...] = acc.astype(o_ref.dtype)

    return pl.pallas_call(
        body, name=name, grid=grid, in_specs=[a_spec, b_spec], out_specs=o_spec,
        out_shape=jax.ShapeDtypeStruct(out_shape, out_dtype),
        compiler_params=_params(("parallel",) * len(grid)),
    )(a, b)


def _mm_res_norm(name, a, w, res, g_next, alpha=1.0):
    t, k = a.shape
    tm = _pick(t, (544, 384, 256, 128))

    def body(a_ref, w_ref, r_ref, g_ref, h_ref, n_ref):
        acc = lax.dot_general(_bf(a_ref[...]), w_ref[...], _NN, preferred_element_type=F32)
        if alpha != 1.0:
            acc = acc * alpha
        hv = acc + r_ref[...]
        h_ref[...] = hv
        r = lax.rsqrt(jnp.mean(hv * hv, axis=-1, keepdims=True) + EPS)
        n_ref[...] = ((hv * r) * g_ref[...]).astype(BF16)

    row = pl.BlockSpec((tm, D_MODEL), lambda i: (i, 0))
    return pl.pallas_call(
        body, name=name, grid=(t // tm,),
        in_specs=[pl.BlockSpec((tm, k), lambda i: (i, 0)), pl.BlockSpec((k, D_MODEL), lambda i: (0, 0)), row,
                  pl.BlockSpec((1, D_MODEL), lambda i: (0, 0))],
        out_specs=[row, row],
        out_shape=[jax.ShapeDtypeStruct((t, D_MODEL), F32), jax.ShapeDtypeStruct((t, D_MODEL), BF16)],
        compiler_params=_params(("parallel",)),
    )(a, w, res, g_next)


def _mm_tn(name, a, b, out_dtype=BF16, alpha=1.0, tm_c=(768, 512, 256, 128), tn_c=(512, 640, 256, 128)):
    t, m = a.shape
    n = b.shape[1]
    tm = _pick(m, tm_c)
    tn = _pick(n, tn_c)
    bytes_a, bytes_b = a.size * a.dtype.itemsize, b.size * b.dtype.itemsize
    if bytes_a + bytes_b * (m // tm) <= bytes_b + bytes_a * (n // tn):
        return _mm(name, a, b, _TN, (m // tm, n // tn),
                   pl.BlockSpec((t, tm), lambda i, j: (0, i)), pl.BlockSpec((t, tn), lambda i, j: (0, j)),
                   pl.BlockSpec((tm, tn), lambda i, j: (i, j)), (m, n), out_dtype, alpha=alpha)
    return _mm(name, a, b, _TN, (n // tn, m // tm),
               pl.BlockSpec((t, tm), lambda j, i: (0, i)), pl.BlockSpec((t, tn), lambda j, i: (0, j)),
               pl.BlockSpec((tm, tn), lambda j, i: (i, j)), (m, n), out_dtype, alpha=alpha)


def _ffn_in_fwd(name, n, wblk, dep=None):
    t = n.shape[0]
    tm = _pick(t, (1088, 768, 512, 256, 128))
    has_dep = dep is not None

    def body(n_ref, w_ref, *rest):
        gu_ref, a_ref = rest[-2], rest[-1]
        nv = n_ref[...]
        j = pl.program_id(1)
        g = lax.dot_general(nv, w_ref[0, j], _NN, preferred_element_type=F32)
        u = lax.dot_general(nv, w_ref[1, j], _NN, preferred_element_type=F32)
        sg = jax.nn.sigmoid(g)
        silu = g * sg
        a_ref[...] = (silu * u).astype(BF16)
        gu_ref[0] = ((0.5 * u) * (sg + silu * (1.0 - sg))).astype(BF16)
        gu_ref[1] = (0.5 * silu).astype(BF16)

    return pl.pallas_call(
        body, name=name, grid=(t // tm, 4),
        in_specs=[pl.BlockSpec((tm, D_MODEL), lambda i, j: (i, 0)),
                  pl.BlockSpec((2, 4, D_MODEL, FF_SHARD_P), lambda i, j: (0, 0, 0, 0), pipeline_mode=pl.Buffered(1))]
        + ([pl.BlockSpec(memory_space=pl.ANY)] if has_dep else []),
        out_specs=[pl.BlockSpec((2, tm, FF_SHARD_P), lambda i, j: (0, i, j)),
                   pl.BlockSpec((tm, FF_SHARD_P), lambda i, j: (i, j))],
        out_shape=[jax.ShapeDtypeStruct((2, t, D_FF_P), BF16), jax.ShapeDtypeStruct((t, D_FF_P), BF16)],
        compiler_params=_params(("parallel", "parallel")),
    )(*((n, wblk) + ((dep,) if has_dep else ())))


def _ffn_out_bwd_x(name, dh, w_out, gu, dep=None):
    t = dh.shape[0]
    tm = _pick(t, (1088, 768, 512, 256, 128))
    has_dep = dep is not None

    def body(dh_ref, w_ref, gu_ref, *rest):
        o_ref = rest[-1]
        w = w_ref[pl.ds(pl.multiple_of(pl.program_id(1) * FF_SHARD_P, FF_SHARD_P), FF_SHARD_P), :]
        da = lax.dot_general(_bf(dh_ref[...]), w, _NT, preferred_element_type=F32)
        o_ref[0] = (da * gu_ref[0].astype(F32)).astype(BF16)
        o_ref[1] = (da * gu_ref[1].astype(F32)).astype(BF16)

    gu_spec = pl.BlockSpec((2, tm, FF_SHARD_P), lambda i, j: (0, i, j))
    return pl.pallas_call(
        body, name=name, grid=(t // tm, 4),
        in_specs=[pl.BlockSpec((tm, D_MODEL), lambda i, j: (i, 0)),
                  pl.BlockSpec((D_FF_P, D_MODEL), lambda i, j: (0, 0), pipeline_mode=pl.Buffered(1)),
                  gu_spec] + ([pl.BlockSpec(memory_space=pl.ANY)] if has_dep else []),
        out_specs=gu_spec, out_shape=jax.ShapeDtypeStruct((2, t, D_FF_P), BF16),
        compiler_params=_params(("parallel", "parallel")),
    )(*((dh, w_out, gu) + ((dep,) if has_dep else ())))


def _rms_bwd_rows(dn, h, g, dres):
    r = lax.rsqrt(jnp.mean(h * h, axis=-1, keepdims=True) + EPS)
    tv = dn * g
    dot = jnp.mean(tv * h, axis=-1, keepdims=True)
    return dres + (r * tv - h * (r * r * r * dot)), jnp.sum(dn * (h * r), axis=0, keepdims=True)


def _accumulate_rows(ref, part, first):
    @pl.when(first)
    def _():
        ref[...] = part

    @pl.when(jnp.logical_not(first))
    def _():
        ref[...] += part


def _ffn_in_bwd_x(name, dgu, wblk, h_in, g_norm, dres, dep=None, examples=None):
    t = dgu.shape[1]
    tm = _pick(t, (544, 384, 256, 128))
    has_dep = dep is not None
    split = examples is not None
    if split:
        l = t // examples
        per = l // tm
        assert per * tm == l and tm > PREFIX

    def body(d_ref, w_ref, h_ref, g_ref, r_ref, *rest):
        acc = None
        for s in range(2):
            for j in range(4):
                part = lax.dot_general(d_ref[s, :, FF_SHARD_P * j:FF_SHARD_P * (j + 1)], w_ref[s, j], _NT,
                                       preferred_element_type=F32)
                acc = part if acc is None else acc + part
        dh, dg = _rms_bwd_rows(acc, h_ref[...], g_ref[...], r_ref[...])
        i = pl.program_id(0)
        if not split:
            dh_ref, dhb_ref, dg_ref = rest[-3:]
            dh_ref[...] = dh
            dhb_ref[...] = dh.astype(BF16)
        else:
            gx_ref, meta_ref, dg_ref, buf, sem = rest[-5:]

            def out_copy(step):
                bi, r = step // per, step % per
                head = pltpu.make_async_copy(buf.at[pl.ds(PREFIX, tm - PREFIX)], gx_ref.at[bi, pl.ds(0, tm - PREFIX)], sem)
                if per == 1:
                    return r == 0, head, None
                return r == 0, head, pltpu.make_async_copy(
                    buf, gx_ref.at[bi, pl.ds(pl.multiple_of(jnp.maximum(r, 1) * tm - PREFIX, 8), tm)], sem)

            def run(step, method):
                is_head, head, later = out_copy(step)

                @pl.when(is_head)
                def _():
                    getattr(head, method)()

                if later is not None:
                    @pl.when(jnp.logical_not(is_head))
                    def _():
                        getattr(later, method)()

            @pl.when(i > 0)
            def _():
                run(i - 1, "wait")

            buf[...] = dh

            @pl.when(i % per == 0)
            def _():
                meta_ref[...] = dh[N_PAD:PREFIX]

            run(i, "start")

            @pl.when(i == pl.num_programs(0) - 1)
            def _():
                run(i, "wait")

        _accumulate_rows(dg_ref, dg, i == 0)

    row = pl.BlockSpec((tm, D_MODEL), lambda i: (i, 0))
    vec = pl.BlockSpec((1, D_MODEL), lambda i: (0, 0))
    if split:
        out_specs = [pl.BlockSpec(memory_space=pl.ANY), pl.BlockSpec((None, N_META, D_MODEL), lambda i: (i // per, 0, 0)), vec]
        out_shape = [jax.ShapeDtypeStruct((examples, l - PREFIX, D_MODEL), F32),
                     jax.ShapeDtypeStruct((examples, N_META, D_MODEL), F32), jax.ShapeDtypeStruct((1, D_MODEL), F32)]
        scratch = [pltpu.VMEM((tm, D_MODEL), F32), pltpu.SemaphoreType.DMA(())]
    else:
        out_specs = [row, row, vec]
        out_shape = [jax.ShapeDtypeStruct((t, D_MODEL), F32), jax.ShapeDtypeStruct((t, D_MODEL), BF16),
                     jax.ShapeDtypeStruct((1, D_MODEL), F32)]
        scratch = []
    return pl.pallas_call(
        body, name=name, grid=(t // tm,),
        in_specs=[pl.BlockSpec((2, tm, D_FF_P), lambda i: (0, i, 0)),
                  pl.BlockSpec((2, 4, D_MODEL, FF_SHARD_P), lambda i: (0, 0, 0, 0), pipeline_mode=pl.Buffered(1)),
                  row, vec, row]
        + ([pl.BlockSpec(memory_space=pl.ANY)] if has_dep else []),
        out_specs=out_specs, out_shape=out_shape, scratch_shapes=scratch,
        compiler_params=_params(("arbitrary",)),
    )(*((dgu, wblk, h_in, g_norm, dres) + ((dep,) if has_dep else ())))


def _proj_fwd(name, um, wi):
    t = um.shape[0]
    tm = _pick(t, (544, 384, 256, 128))

    def body(u_ref, w_ref, q_ref, g_ref, f_ref):
        uv = u_ref[...]
        q_ref[...] = lax.dot_general(uv, w_ref[:, 0:QKV_W], _NN, preferred_element_type=F32).astype(BF16)
        g_ref[...] = lax.dot_general(uv, w_ref[:, P_GATES:P_F], _NN, preferred_element_type=F32).astype(BF16)
        f_ref[...] = lax.dot_general(uv, w_ref[:, P_F:PROJ_P], _NN, preferred_element_type=F32)

    return pl.pallas_call(
        body, name=name, grid=(t // tm,),
        in_specs=[pl.BlockSpec((tm, D_MODEL), lambda i: (i, 0)),
                  pl.BlockSpec((D_MODEL, PROJ_P), lambda i: (0, 0), pipeline_mode=pl.Buffered(1))],
        out_specs=[pl.BlockSpec((tm, QKV_W), lambda i: (i, 0)), pl.BlockSpec((tm, 2 * D_MODEL), lambda i: (i, 0)),
                   pl.BlockSpec((tm, 128), lambda i: (i, 0))],
        out_shape=[jax.ShapeDtypeStruct((t, QKV_W), BF16), jax.ShapeDtypeStruct((t, 2 * D_MODEL), BF16),
                   jax.ShapeDtypeStruct((t, 128), F32)],
        compiler_params=_params(("parallel",)),
    )(um, wi)


def _proj_bwd_x(name, dqkv, dgates, df, wi, h_in, g_norm, dres, dep=None):
    t = dqkv.shape[0]
    tm = _pick(t, (544, 384, 256, 128))
    has_dep = dep is not None

    def body(q_ref, gt_ref, f_ref, w_ref, h_ref, g_ref, r_ref, *rest):
        dh_ref, dhb_ref, dg_ref = rest[-3:]
        acc = lax.dot_general(q_ref[...], w_ref[:, 0:QKV_W], _NT, preferred_element_type=F32)
        acc = acc + lax.dot_general(gt_ref[...], w_ref[:, P_GATES:P_F], _NT, preferred_element_type=F32)
        acc = acc + lax.dot_general(f_ref[...], w_ref[:, P_F:PROJ_P], _NT, preferred_element_type=F32)
        dh, dg = _rms_bwd_rows(acc, h_ref[...], g_ref[...], r_ref[...])
        dh_ref[...] = dh
        dhb_ref[...] = dh.astype(BF16)
        _accumulate_rows(dg_ref, dg, pl.program_id(0) == 0)

    row = pl.BlockSpec((tm, D_MODEL), lambda i: (i, 0))
    vec = pl.BlockSpec((1, D_MODEL), lambda i: (0, 0))
    return pl.pallas_call(
        body, name=name, grid=(t // tm,),
        in_specs=[pl.BlockSpec((tm, QKV_W), lambda i: (i, 0)), pl.BlockSpec((tm, 2 * D_MODEL), lambda i: (i, 0)),
                  pl.BlockSpec((tm, 128), lambda i: (i, 0)),
                  pl.BlockSpec((D_MODEL, PROJ_P), lambda i: (0, 0), pipeline_mode=pl.Buffered(1)), row, vec, row]
        + ([pl.BlockSpec(memory_space=pl.ANY)] if has_dep else []),
        out_specs=[row, row, vec],
        out_shape=[jax.ShapeDtypeStruct((t, D_MODEL), F32), jax.ShapeDtypeStruct((t, D_MODEL), BF16),
                   jax.ShapeDtypeStruct((1, D_MODEL), F32)],
        compiler_params=_params(("arbitrary",)),
    )(*((dqkv, dgates, df, wi, h_in, g_norm, dres) + ((dep,) if has_dep else ())))


def _ffn_in_bwd_w(name, n, dgu):
    t = n.shape[0]
    return _mm(name, n, dgu, _TN, (2, 4),
               pl.BlockSpec((t, D_MODEL), lambda s, j: (0, 0)),
               pl.BlockSpec((None, t, FF_SHARD_P), lambda s, j: (s, 0, j)),
               pl.BlockSpec((None, None, D_MODEL, FF_SHARD_P), lambda s, j: (s, j, 0, 0)),
               (2, 4, D_MODEL, FF_SHARD_P), BF16)


def _embed_norm(name, x, meta, g):
    b, s, _ = x.shape
    half = (s + PREFIX) // 2
    first = half - PREFIX
    assert first > 0 and half % 16 == 0

    def body(x_ref, m_ref, g_ref, h_ref, n_ref, buf, sem):
        bi, k = pl.program_id(0), pl.program_id(1)

        def tokens(example, second, method):
            if second:
                cp = pltpu.make_async_copy(x_ref.at[example, pl.ds(first, half)], buf.at[1], sem.at[1])
            else:
                cp = pltpu.make_async_copy(x_ref.at[example, pl.ds(0, first)], buf.at[0, pl.ds(PREFIX, first)], sem.at[0])
            getattr(cp, method)()

        @pl.when((bi == 0) & (k == 0))
        def _():
            tokens(0, False, "start")

        @pl.when(k == 0)
        def _():
            tokens(bi, True, "start")
            tokens(bi, False, "wait")
            buf[0, 0:N_PAD, :] = jnp.zeros((N_PAD, D_MODEL), F32)
            buf[0, N_PAD:PREFIX, :] = m_ref[...]

        @pl.when(k == 1)
        def _():
            @pl.when(bi + 1 < b)
            def _():
                tokens(bi + 1, False, "start")

            tokens(bi, True, "wait")

        hv = buf[k]
        h_ref[...] = hv
        r = lax.rsqrt(jnp.mean(hv * hv, axis=-1, keepdims=True) + EPS)
        n_ref[...] = ((hv * r) * g_ref[...]).astype(BF16)

    rows = pl.BlockSpec((half, D_MODEL), lambda bi, k: (2 * bi + k, 0))
    return pl.pallas_call(
        body, name=name, grid=(b, 2),
        in_specs=[pl.BlockSpec(memory_space=pl.ANY), pl.BlockSpec((N_META, D_MODEL), lambda bi, k: (0, 0)),
                  pl.BlockSpec((1, D_MODEL), lambda bi, k: (0, 0))],
        out_specs=[rows, rows],
        out_shape=[jax.ShapeDtypeStruct((2 * b * half, D_MODEL), F32), jax.ShapeDtypeStruct((2 * b * half, D_MODEL), BF16)],
        scratch_shapes=[pltpu.VMEM((2, half, D_MODEL), F32), pltpu.SemaphoreType.DMA((2,))],
        compiler_params=_params(("arbitrary", "arbitrary")),
    )(x, meta, g)


def _branch_gate_fwd(name, oa, ob, wa, wb, gates, dep=None):
    t = gates.shape[0]
    tm = _pick(t, (544, 384, 256, 128))
    has_dep = dep is not None

    def body(oa_ref, ob_ref, wa_ref, wb_ref, g_ref, *rest):
        o_ref, ya_ref, yb_ref = rest[-3:]
        ya = lax.dot_general(_bf(oa_ref[...]), wa_ref[...], _NN, preferred_element_type=F32)
        yb = lax.dot_general(_bf(ob_ref[...]), wb_ref[...], _NN, preferred_element_type=F32)
        sa = jax.nn.sigmoid(g_ref[:, 0:D_MODEL].astype(F32))
        sb = jax.nn.sigmoid(g_ref[:, D_MODEL:2 * D_MODEL].astype(F32))
        o_ref[...] = (sa * ya + sb * yb).astype(BF16)
        ya_ref[...] = ya.astype(BF16)
        yb_ref[...] = yb.astype(BF16)

    blk = pl.BlockSpec((tm, D_MODEL), lambda i: (i, 0))
    narrow = pl.BlockSpec((tm, A_WIDTH), lambda i: (i, 0))
    wide = pl.BlockSpec((tm, 2 * D_MODEL), lambda i: (i, 0))
    wspec = pl.BlockSpec((A_WIDTH, D_MODEL), lambda i: (0, 0))
    out = jax.ShapeDtypeStruct((t, D_MODEL), BF16)
    return pl.pallas_call(
        body, name=name, grid=(t // tm,),
        in_specs=[narrow, narrow, wspec, wspec, wide] + ([pl.BlockSpec(memory_space=pl.ANY)] if has_dep else []),
        out_specs=[blk, blk, blk], out_shape=[out, out, out], compiler_params=_params(("parallel",)),
    )(*((oa, ob, wa, wb, gates) + ((dep,) if has_dep else ())))


def _branch_bwd_x(name, dya, dyb, wa, wb):
    t = dya.shape[0]
    tm = _pick(t, (1088, 768, 512, 256, 128))

    def body(da_ref, db_ref, wa_ref, wb_ref, oa_ref, ob_ref):
        oa_ref[...] = lax.dot_general(da_ref[...], wa_ref[...], _NT, preferred_element_type=F32)
        ob_ref[...] = lax.dot_general(db_ref[...], wb_ref[...], _NT, preferred_element_type=F32).astype(BF16)

    blk = pl.BlockSpec((tm, D_MODEL), lambda i: (i, 0))
    narrow = pl.BlockSpec((tm, A_WIDTH), lambda i: (i, 0))
    wspec = pl.BlockSpec((A_WIDTH, D_MODEL), lambda i: (0, 0), pipeline_mode=pl.Buffered(1))
    return pl.pallas_call(
        body, name=name, grid=(t // tm,), in_specs=[blk, blk, wspec, wspec], out_specs=[narrow, narrow],
        out_shape=[jax.ShapeDtypeStruct((t, A_WIDTH), F32), jax.ShapeDtypeStruct((t, B_WIDTH), BF16)],
        compiler_params=_params(("parallel",)),
    )(dya, dyb, wa, wb)


def _branch_bwd_w(name, oa, ob, dya, dyb):
    t = oa.shape[0]
    tn = 512

    def body(oa_ref, ob_ref, da_ref, db_ref, wa_ref, wb_ref):
        wa_ref[...] = lax.dot_general(_bf(oa_ref[...]), da_ref[...], _TN, preferred_element_type=F32).astype(BF16)
        wb_ref[...] = lax.dot_general(_bf(ob_ref[...]), db_ref[...], _TN, preferred_element_type=F32).astype(BF16)

    whole = pl.BlockSpec((t, A_WIDTH), lambda j: (0, 0), pipeline_mode=pl.Buffered(1))
    cols = pl.BlockSpec((t, tn), lambda j: (0, j))
    out_spec = pl.BlockSpec((A_WIDTH, tn), lambda j: (0, j))
    out = jax.ShapeDtypeStruct((A_WIDTH, D_MODEL), BF16)
    return pl.pallas_call(
        body, name=name, grid=(D_MODEL // tn,), in_specs=[whole, whole, cols, cols], out_specs=[out_spec, out_spec],
        out_shape=[out, out], compiler_params=_params(("parallel",)),
    )(oa, ob, dya, dyb)


def _mix_out_gate_bwd(name, dh, wo, gates, ya, yb):
    t = gates.shape[0]
    tm = _pick(t, (544, 384, 256, 128))

    def body(dh_ref, w_ref, g_ref, ya_ref, yb_ref, dya_ref, dyb_ref, dg_ref):
        dm = lax.dot_general(_bf(dh_ref[...]), w_ref[...], _NT, preferred_element_type=F32)
        sa = jax.nn.sigmoid(g_ref[:, 0:D_MODEL].astype(F32))
        sb = jax.nn.sigmoid(g_ref[:, D_MODEL:2 * D_MODEL].astype(F32))
        dya_ref[...] = (dm * sa).astype(BF16)
        dyb_ref[...] = (dm * sb).astype(BF16)
        dg_ref[:, 0:D_MODEL] = (dm * ya_ref[...].astype(F32) * (sa * (1.0 - sa))).astype(BF16)
        dg_ref[:, D_MODEL:2 * D_MODEL] = (dm * yb_ref[...].astype(F32) * (sb * (1.0 - sb))).astype(BF16)

    blk = pl.BlockSpec((tm, D_MODEL), lambda i: (i, 0))
    wide = pl.BlockSpec((tm, 2 * D_MODEL), lambda i: (i, 0))
    out = jax.ShapeDtypeStruct((t, D_MODEL), BF16)
    return pl.pallas_call(
        body, name=name, grid=(t // tm,),
        in_specs=[blk, pl.BlockSpec((D_MODEL, D_MODEL), lambda i: (0, 0)), wide, blk, blk], out_specs=[blk, blk, wide],
        out_shape=[out, out, jax.ShapeDtypeStruct((t, 2 * D_MODEL), BF16)], compiler_params=_params(("parallel",)),
    )(dh, wo, gates, ya, yb)


def _ffn_out_loss(name, a, w, res, gf, tgt, alpha):
    t, k = a.shape
    b, s, _ = tgt.shape
    l = t // b
    tm = _pick(t, (544, 384, 256, 128))
    per = l // tm
    assert per * tm == l and tm > PREFIX and l - PREFIX == s

    def body(a_ref, w_ref, r_ref, g_ref, t_ref, dh_ref, dhb_ref, loss_ref, dg_ref, buf, sem):
        i = pl.program_id(0)
        bi, r = i // per, i % per

        def first_rows():
            return pltpu.make_async_copy(t_ref.at[bi, pl.ds(0, tm - PREFIX)], buf.at[pl.ds(PREFIX, tm - PREFIX)], sem)

        def later_rows():
            return pltpu.make_async_copy(t_ref.at[bi, pl.ds(pl.multiple_of(r * tm - PREFIX, 8), tm)], buf, sem)

        @pl.when(r == 0)
        def _():
            buf[0:PREFIX, :] = jnp.zeros((PREFIX, D_MODEL), F32)
            first_rows().start()

        if per > 1:
            @pl.when(r > 0)
            def _():
                later_rows().start()

        acc = lax.dot_general(a_ref[...], w_ref[...], _NN, preferred_element_type=F32)

        @pl.when(r == 0)
        def _():
            first_rows().wait()

        if per > 1:
            @pl.when(r > 0)
            def _():
                later_rows().wait()

        hv = acc * alpha + r_ref[...]
        row = lax.broadcasted_iota(jnp.int32, (tm, 1), 0)
        real = ((r > 0) | (row >= PREFIX)).astype(F32)
        g = g_ref[...]
        rn = lax.rsqrt(jnp.mean(hv * hv, axis=-1, keepdims=True) + EPS)
        xn = hv * rn
        err = (xn * g - buf[...]) * real
        lpart = 0.5 * jnp.sum(jnp.mean(err * err, axis=-1, keepdims=True), axis=0, keepdims=True)
        dy = err * (1.0 / D_MODEL)
        tv = dy * g
        dot = jnp.mean(tv * hv, axis=-1, keepdims=True)
        dh = rn * tv - hv * (rn * rn * rn * dot)
        dh_ref[...] = dh
        dhb_ref[...] = dh.astype(BF16)
        gpart = jnp.sum(dy * xn, axis=0, keepdims=True)

        @pl.when(i == 0)
        def _():
            loss_ref[...] = jnp.zeros_like(loss_ref)
            dg_ref[...] = jnp.zeros_like(dg_ref)

        loss_ref[...] += jnp.broadcast_to(lpart, loss_ref.shape)
        dg_ref[...] += gpart

    row_spec = pl.BlockSpec((tm, D_MODEL), lambda i: (i, 0))
    vec = pl.BlockSpec((1, D_MODEL), lambda i: (0, 0))
    return pl.pallas_call(
        body, name=name, grid=(t // tm,),
        in_specs=[pl.BlockSpec((tm, k), lambda i: (i, 0)),
                  pl.BlockSpec((k, D_MODEL), lambda i: (0, 0), pipeline_mode=pl.Buffered(1)), row_spec, vec,
                  pl.BlockSpec(memory_space=pl.ANY)],
        out_specs=[row_spec, row_spec, pl.BlockSpec((8, 128), lambda i: (0, 0)), vec],
        out_shape=[jax.ShapeDtypeStruct((t, D_MODEL), F32), jax.ShapeDtypeStruct((t, D_MODEL), BF16),
                   jax.ShapeDtypeStruct((8, 128), F32), jax.ShapeDtypeStruct((1, D_MODEL), F32)],
        scratch_shapes=[pltpu.VMEM((tm, D_MODEL), F32), pltpu.SemaphoreType.DMA(())],
        compiler_params=_params(("arbitrary",)),
    )(a, w, res, gf, tgt)


def _fgate_fwd(name, f3, bf_row):
    b, l, _ = f3.shape
    nb = l // BLOCK

    def body(f_ref, b_ref, cc_ref, cr_ref):
        r_i = lax.broadcasted_iota(jnp.int32, (BLOCK, BLOCK), 0)
        c_i = lax.broadcasted_iota(jnp.int32, (BLOCK, BLOCK), 1)
        tri = (r_i >= c_i).astype(F32)
        carry = jnp.zeros((1, 128), F32)
        for blk in range(nb):
            rows = slice(blk * BLOCK, (blk + 1) * BLOCK)
            z = f_ref[rows, :] + b_ref[...]
            lf = jnp.minimum(z, 0.0) - jnp.log(1.0 + jnp.exp(-jnp.abs(z)))
            cb = jnp.dot(tri, lf, preferred_element_type=F32, precision=lax.Precision.HIGHEST) + carry
            carry = cb[BLOCK - 1:BLOCK, :]
            cbt = cb.T
            for hh in range(B_HEADS):
                cc_ref[hh, rows, :] = jnp.sum(jnp.where(c_i == hh, cb, 0.0), axis=1, keepdims=True)
                cr_ref[hh, :, rows] = cbt[hh:hh + 1, :]

    return pl.pallas_call(
        body, name=name, grid=(b,),
        in_specs=[pl.BlockSpec((None, l, 128), lambda bi: (bi, 0, 0)),
                  pl.BlockSpec((1, 128), lambda bi: (0, 0))],
        out_specs=[pl.BlockSpec((None, B_HEADS, l, 1), lambda bi: (bi, 0, 0, 0)),
                   pl.BlockSpec((None, B_HEADS, 1, l), lambda bi: (bi, 0, 0, 0))],
        out_shape=[jax.ShapeDtypeStruct((b, B_HEADS, l, 1), F32), jax.ShapeDtypeStruct((b, B_HEADS, 1, l), F32)],
        compiler_params=_params(("parallel",)),
    )(f3, bf_row)


def _fgate_bwd(name, f3, bf_row, dcq, dck):
    b, l, _ = f3.shape
    nb = l // BLOCK

    def body(f_ref, b_ref, dcq_ref, dck_ref, df_ref, db_ref):
        r_i = lax.broadcasted_iota(jnp.int32, (BLOCK, BLOCK), 0)
        c_i = lax.broadcasted_iota(jnp.int32, (BLOCK, BLOCK), 1)
        tri = (r_i <= c_i).astype(F32)
        carry = jnp.zeros((1, 128), F32)
        total = jnp.zeros((1, 128), F32)
        for blk in range(nb - 1, -1, -1):
            rows = slice(blk * BLOCK, (blk + 1) * BLOCK)
            krows = jnp.concatenate([dck_ref[hh, :, rows] for hh in range(B_HEADS)]
                                    + [jnp.zeros((BLOCK - B_HEADS, BLOCK), F32)], axis=0)
            dcb = krows.T
            for hh in range(B_HEADS):
                dcb = dcb + jnp.where(c_i == hh, dcq_ref[hh, rows, :], 0.0)
            rc = jnp.dot(tri, dcb, preferred_element_type=F32, precision=lax.Precision.HIGHEST) + carry
            carry = rc[0:1, :]
            z = f_ref[rows, :] + b_ref[...]
            df = rc * (1.0 / (1.0 + jnp.exp(z)))
            df_ref[rows, :] = df.astype(BF16)
            total = total + jnp.sum(df, axis=0, keepdims=True)

        @pl.when(pl.program_id(0) == 0)
        def _():
            db_ref[...] = total

        @pl.when(pl.program_id(0) > 0)
        def _():
            db_ref[...] += total

    return pl.pallas_call(
        body, name=name, grid=(b,),
        in_specs=[pl.BlockSpec((None, l, 128), lambda bi: (bi, 0, 0)),
                  pl.BlockSpec((1, 128), lambda bi: (0, 0)),
                  pl.BlockSpec((None, B_HEADS, l, 1), lambda bi: (bi, 0, 0, 0)),
                  pl.BlockSpec((None, B_HEADS, 1, l), lambda bi: (bi, 0, 0, 0))],
        out_specs=[pl.BlockSpec((None, l, 128), lambda bi: (bi, 0, 0)), pl.BlockSpec((1, 128), lambda bi: (0, 0))],
        out_shape=[jax.ShapeDtypeStruct((b, l, 128), BF16), jax.ShapeDtypeStruct((1, 128), F32)],
        compiler_params=_params(("arbitrary",)),
    )(f3, bf_row, dcq, dck)


A_Q_BLK = B_SEG // A_WIDTH
A_K_BLK = (B_SEG + A_WIDTH) // 128
A_V_BLK = A_K_BLK + 1
A_SEG_BLK = B_SEG // A_SEG
STACK = A_HEADS * BLOCK


def _lane_lo():
    return lax.broadcasted_iota(jnp.int32, (1, 128), 1) < HEAD_DIM


def _stack_heads(x, masked):
    lo = _lane_lo()
    blks = [x[:, 128 * j:128 * (j + 1)] for j in range(4)]
    if not masked:
        return jnp.concatenate(blks + blks, axis=0)
    zero = jnp.zeros_like(blks[0])
    return jnp.concatenate([jnp.where(lo, bk, zero) for bk in blks] + [jnp.where(lo, zero, bk) for bk in blks], axis=0)


def _unstack_heads(y):
    lo = _lane_lo()
    return jnp.concatenate([jnp.where(lo, y[128 * j:128 * (j + 1)], y[128 * (4 + j):128 * (5 + j)]) for j in range(4)], axis=1)


def _swa_bias(slopes):
    slopes = np.asarray(slopes, np.float32)
    r_i = np.arange(STACK)[:, None]
    c_i = np.arange(3 * BLOCK)[None, :]
    seg = c_i >> 7
    out = []
    for n in range(3):
        qpos = n * BLOCK + (r_i & (BLOCK - 1))
        kpos = np.where(seg == 0, c_i, (n - 2) * BLOCK + c_i)
        dist = qpos - kpos
        band = (seg != 0) & (dist < BLOCK) & (kpos >= PREFIX)
        meta = (seg == 0) & (c_i >= N_PAD)
        out.append(np.where((dist >= 0) & (band | meta), -slopes * dist.astype(np.float32), np.float32(NEG)))
    return jnp.asarray(np.stack(out, axis=0), F32)


def _swa_scores(q, kcat, n, slope, bias):
    s = lax.dot_general(q, kcat, _NT, preferred_element_type=F32) + bias
    further = slope * (-BLOCK * jnp.maximum(n - 2, 0)).astype(F32)
    return jnp.concatenate([s[:, 0:BLOCK] + further, s[:, BLOCK:]], axis=1)


def _swa_specs():
    def kv(col_blk):
        return [pl.BlockSpec((None, BLOCK, 128), lambda b, n: (b, 0, col_blk)),
                pl.BlockSpec((None, BLOCK, 128), lambda b, n: (b, jnp.maximum(n - 1, 0), col_blk)),
                pl.BlockSpec((None, BLOCK, 128), lambda b, n: (b, n, col_blk))]

    q_spec = pl.BlockSpec((None, BLOCK, A_WIDTH), lambda b, n: (b, n, A_Q_BLK))
    o_spec = pl.BlockSpec((None, BLOCK, A_WIDTH), lambda b, n: (b, n, 0))
    col = pl.BlockSpec((STACK, 1), lambda b, n: (0, 0))
    bias = pl.BlockSpec((None, STACK, 3 * BLOCK), lambda b, n: (jnp.minimum(n, 2), 0, 0))
    lse_spec = pl.BlockSpec((None, A_HEADS, BLOCK, 1), lambda b, n: (b, 0, n, 0))
    return q_spec, kv(A_K_BLK), kv(A_V_BLK), o_spec, [col, col, bias], lse_spec


def _swa_fwd(name, qkv, slopes, sinks, bias):
    b, l, _ = qkv.shape
    nb = l // BLOCK

    def body(q_ref, k0_ref, kp_ref, kc_ref, v0_ref, vp_ref, vc_ref, sl_ref, sk_ref, bias_ref, o_ref, o16_ref, lse_ref):
        n = pl.program_id(1)
        qs = _stack_heads(q_ref[...], True) * SCALE
        kcat = jnp.concatenate([k0_ref[...], kp_ref[...], kc_ref[...]], axis=0)
        vcat = jnp.concatenate([v0_ref[...], vp_ref[...], vc_ref[...]], axis=0)
        s = _swa_scores(qs, kcat, n, sl_ref[...], bias_ref[...])
        sink = sk_ref[...]
        m = jnp.maximum(jnp.max(s, axis=-1, keepdims=True), sink)
        p = jnp.exp(s - m)
        den = jnp.sum(p, axis=-1, keepdims=True) + jnp.exp(sink - m)
        o = lax.dot_general(p.astype(BF16), vcat, _NN, preferred_element_type=F32) / den
        ov = _unstack_heads(o)
        o_ref[...] = ov
        o16_ref[...] = ov.astype(BF16)
        lse_ref[...] = (m + jnp.log(den)).reshape(A_HEADS, BLOCK, 1)

    q_spec, k_specs, v_specs, o_spec, consts, lse_spec = _swa_specs()
    return pl.pallas_call(
        body, name=name, grid=(b, nb),
        in_specs=[q_spec] + k_specs + v_specs + consts, out_specs=[o_spec, o_spec, lse_spec],
        out_shape=[jax.ShapeDtypeStruct((b, l, A_WIDTH), F32), jax.ShapeDtypeStruct((b, l, A_WIDTH), BF16),
                   jax.ShapeDtypeStruct((b, A_HEADS, l, 1), F32)],
        compiler_params=_params(("parallel", "parallel")),
    )(qkv, qkv, qkv, qkv, qkv, qkv, qkv, slopes, sinks, bias)


def _swa_bwd(name, qkv, o, lse, do, slopes, sinks, bias, dqkv):
    b, l, _ = qkv.shape
    nb = l // BLOCK

    def body(q_ref, k0_ref, kp_ref, kc_ref, v0_ref, vp_ref, vc_ref, o_ref, lse_ref, do_ref, sl_ref, sk_ref, bias_ref, _,
             dx_ref, ds_ref, dk_acc, dv_acc):
        bi = pl.program_id(0)
        n = pl.program_id(1)
        qs = _stack_heads(q_ref[...], True) * SCALE
        dos32 = _stack_heads(do_ref[...], True)
        dos = dos32.astype(BF16)
        os_ = _stack_heads(o_ref[...], False)
        lsev = lse_ref[...].reshape(STACK, 1)
        kcat = jnp.concatenate([k0_ref[...], kp_ref[...], kc_ref[...]], axis=0)
        vcat = jnp.concatenate([v0_ref[...], vp_ref[...], vc_ref[...]], axis=0)
        s = _swa_scores(qs, kcat, n, sl_ref[...], bias_ref[...])
        p = jnp.exp(s - lsev)
        dsum = jnp.sum(dos32 * os_, axis=-1, keepdims=True)
        dp = lax.dot_general(dos, vcat, _NT, preferred_element_type=F32)
        dsc = (p * (dp - dsum)).astype(BF16)
        dq = lax.dot_general(dsc, kcat, _NN, preferred_element_type=F32) * SCALE
        row0 = pl.multiple_of(n * BLOCK, BLOCK)
        dx_ref[pl.ds(row0, BLOCK), 0:A_WIDTH] = _unstack_heads(dq).astype(BF16)
        dkc = lax.dot_general(dsc, qs, _TN, preferred_element_type=F32)
        dvc = lax.dot_general(p.astype(BF16), dos, _TN, preferred_element_type=F32)

        @pl.when(n == 0)
        def _():
            dk_acc[...] = jnp.zeros_like(dk_acc)
            dv_acc[...] = jnp.zeros_like(dv_acc)

        starts = (0, pl.multiple_of(jnp.maximum(n - 1, 0) * BLOCK, BLOCK), row0)
        for t, st in enumerate(starts):
            dk_acc[pl.ds(st, BLOCK), :] += dkc[t * BLOCK:(t + 1) * BLOCK, :]
            dv_acc[pl.ds(st, BLOCK), :] += dvc[t * BLOCK:(t + 1) * BLOCK, :]

        @pl.when(n == nb - 1)
        def _():
            dx_ref[:, A_WIDTH:A_WIDTH + 128] = dk_acc[...].astype(BF16)
            dx_ref[:, A_WIDTH + 128:A_SEG] = dv_acc[...].astype(BF16)

        dsink = -(jnp.exp(sk_ref[...] - lsev) * dsum)
        r8 = lax.broadcasted_iota(jnp.int32, (8, 128), 0)
        acc = jnp.zeros((8, 128), F32)
        for hh in range(A_HEADS):
            acc = acc + jnp.where(r8 == hh, jnp.sum(dsink[hh * BLOCK:(hh + 1) * BLOCK, :]), 0.0)

        @pl.when((bi == 0) & (n == 0))
        def _():
            ds_ref[...] = jnp.zeros_like(ds_ref)

        ds_ref[...] += acc

    q_spec, k_specs, v_specs, o_spec, consts, lse_spec = _swa_specs()
    return pl.pallas_call(
        body, name=name, grid=(b, nb),
        in_specs=[q_spec] + k_specs + v_specs + [o_spec, lse_spec, o_spec] + consts + [pl.BlockSpec(memory_space=pl.ANY)],
        out_specs=[pl.BlockSpec((None, l, A_SEG), lambda bb, n: (bb, 0, A_SEG_BLK)),
                   pl.BlockSpec((8, 128), lambda bb, n: (0, 0))],
        out_shape=[jax.ShapeDtypeStruct(dqkv.shape, BF16), jax.ShapeDtypeStruct((8, 128), F32)],
        scratch_shapes=[pltpu.VMEM((l, 128), F32), pltpu.VMEM((l, 128), F32)],
        input_output_aliases={13: 0},
        compiler_params=_params(("arbitrary", "arbitrary")),
    )(qkv, qkv, qkv, qkv, qkv, qkv, qkv, o, lse, do, slopes, sinks, bias, dqkv)


def _fox_mask(qk, ck, i):
    kh = qk.shape[1]
    qpos = i * BLOCK + lax.broadcasted_iota(jnp.int32, (BLOCK, kh), 0)
    kpos = lax.broadcasted_iota(jnp.int32, (BLOCK, kh), 1)
    return jnp.where((kpos <= qpos) & (kpos >= N_PAD), qk - ck, NEG)


def _pick_head(x, hh):
    lo = _lane_lo()
    return jnp.where(lo if hh == 0 else jnp.logical_not(lo), x, jnp.zeros_like(x))


def _both_heads(x):
    return jnp.concatenate([_pick_head(x, 0), _pick_head(x, 1)], axis=0)


def _fox_specs(l):
    pair = pl.BlockSpec((None, l, PAIR_W), lambda bi, hp: (bi, 0, hp))
    half = pl.BlockSpec((None, l, 128), lambda bi, hp: (bi, 0, hp))
    colv = pl.BlockSpec((None, 2, l, 1), lambda bi, hp: (bi, hp, 0, 0))
    rowv = pl.BlockSpec((None, 2, 1, l), lambda bi, hp: (bi, hp, 0, 0))
    return pair, half, colv, rowv


def _fox_fwd(name, qkv, c_col, c_row):
    b, l, _ = qkv.shape
    nb = l // BLOCK

    def body(x_ref, cc_ref, cr_ref, o_ref, lse_ref):
        for i in range(nb):
            rows = slice(i * BLOCK, (i + 1) * BLOCK)
            kh = (i + 1) * BLOCK
            qblk = x_ref[rows, 0:128]
            kv = x_ref[0:kh, 128:256]
            vv = x_ref[0:kh, 256:384]
            qk = lax.dot_general(_both_heads(qblk) * SCALE, kv, _NT, preferred_element_type=F32)
            ps, dens = [], []
            for hh in range(2):
                s = _fox_mask(qk[hh * BLOCK:(hh + 1) * BLOCK], cr_ref[hh, :, 0:kh], i)
                m = jnp.max(s, axis=-1, keepdims=True)
                p = jnp.exp(s - m)
                den = jnp.sum(p, axis=-1, keepdims=True)
                ps.append(p.astype(BF16))
                dens.append(den)
                lse_ref[hh, rows, :] = (m + jnp.log(den)) + cc_ref[hh, rows, :]
            pv = lax.dot_general(jnp.concatenate(ps, axis=0), vv, _NN, preferred_element_type=F32)
            o_ref[rows, :] = jnp.where(_lane_lo(), pv[0:BLOCK] / dens[0], pv[BLOCK:2 * BLOCK] / dens[1]).astype(BF16)

    pair, half, colv, rowv = _fox_specs(l)
    return pl.pallas_call(
        body, name=name, grid=(b, 4), in_specs=[pair, colv, rowv], out_specs=[half, colv],
        out_shape=[jax.ShapeDtypeStruct((b, l, B_WIDTH), BF16), jax.ShapeDtypeStruct((b, B_HEADS, l, 1), F32)],
        compiler_params=_params(("parallel", "parallel")),
    )(qkv, c_col, c_row)


def _fox_bwd(name, qkv, c_col, c_row, o, lse, do):
    b, l, _ = qkv.shape
    nb = l // BLOCK

    def body(x_ref, cc_ref, cr_ref, o_ref, lse_ref, do_ref, dx_ref, dcq_ref, dck_ref, dk_acc, dv_acc):
        dk_acc[...] = jnp.zeros_like(dk_acc)
        dv_acc[...] = jnp.zeros_like(dv_acc)
        dck_ref[...] = jnp.zeros_like(dck_ref)
        for i in range(nb):
            rows = slice(i * BLOCK, (i + 1) * BLOCK)
            kh = (i + 1) * BLOCK
            qblk = x_ref[rows, 0:128]
            kv = x_ref[0:kh, 128:256]
            vv = x_ref[0:kh, 256:384]
            doblk = do_ref[rows, :]
            ov = o_ref[rows, :].astype(F32)
            q2 = _both_heads(qblk) * SCALE
            do2 = _both_heads(doblk)
            qk = lax.dot_general(q2, kv, _NT, preferred_element_type=F32)
            dp = lax.dot_general(do2, vv, _NT, preferred_element_type=F32)
            ps, dss = [], []
            for hh in range(2):
                half = slice(hh * BLOCK, (hh + 1) * BLOCK)
                s = _fox_mask(qk[half], cr_ref[hh, :, 0:kh], i)
                p = jnp.exp(s - (lse_ref[hh, rows, :] - cc_ref[hh, rows, :]))
                dsum = jnp.sum(do2[half].astype(F32) * ov, axis=-1, keepdims=True)
                ds = p * (dp[half] - dsum)
                ps.append(p.astype(BF16))
                dss.append(ds.astype(BF16))
                dcq_ref[hh, rows, :] = jnp.sum(ds, axis=-1, keepdims=True)
                dck_ref[hh, :, 0:kh] -= jnp.sum(ds, axis=0, keepdims=True)
            p2 = jnp.concatenate(ps, axis=0)
            ds2 = jnp.concatenate(dss, axis=0)
            dq = lax.dot_general(ds2, kv, _NN, preferred_element_type=F32) * SCALE
            dk_acc[0:kh, :] += lax.dot_general(ds2, q2, _TN, preferred_element_type=F32)
            dv_acc[0:kh, :] += lax.dot_general(p2, do2, _TN, preferred_element_type=F32)
            dx_ref[rows, 0:128] = jnp.where(_lane_lo(), dq[0:BLOCK], dq[BLOCK:2 * BLOCK]).astype(BF16)
        dx_ref[:, 128:256] = dk_acc[...].astype(BF16)
        dx_ref[:, 256:384] = dv_acc[...].astype(BF16)

    pair, half, colv, rowv = _fox_specs(l)
    return pl.pallas_call(
        body, name=name, grid=(b, 4), in_specs=[pair, colv, rowv, half, colv, half],
        out_specs=[pair, colv, rowv],
        out_shape=[jax.ShapeDtypeStruct(qkv.shape, BF16), jax.ShapeDtypeStruct((b, B_HEADS, l, 1), F32),
                   jax.ShapeDtypeStruct((b, B_HEADS, 1, l), F32)],
        scratch_shapes=[pltpu.VMEM((l, 128), F32), pltpu.VMEM((l, 128), F32)],
        compiler_params=_params(("parallel", "parallel")),
    )(qkv, c_col, c_row, o, lse, do)


_FLIPS = ((0, 0, 1), (0, 1, 0), (0, 1, 1), (1, 0, 0), (1, 0, 1), (1, 1, 0), (1, 1, 1))


def _peer_table():
    x, y, c = lax.axis_index("x"), lax.axis_index("y"), lax.axis_index("c")
    me = 4 * x + 2 * y + c
    peers = []
    for fx, fy, fc in _FLIPS:
        px = 1 - x if fx else x
        py = 1 - y if fy else y
        pc = 1 - c if fc else c
        peers.append(((px, py, pc), 4 * px + 2 * py + pc))
    return me, peers


_HBM = pl.BlockSpec(memory_space=pltpu.HBM)
_SEM = pl.BlockSpec(memory_space=pltpu.SEMAPHORE)
_ANY = pl.BlockSpec(memory_space=pl.ANY)
_EFFECT = pltpu.SideEffectType.DATAFLOW_SIDE_EFFECTING


def _split_copy(srcs_are_pieces, src_refs, land_refs, send_sem, recv_sem, a, kk, me, peers, arriving):
    dev, lin = peers[kk]
    npeer = len(_FLIPS)
    src = src_refs[a] if srcs_are_pieces[a] else src_refs[a].at[lin]
    dst = land_refs[a].at[lin] if arriving else land_refs[a].at[me]
    return pltpu.make_async_remote_copy(src_ref=src, dst_ref=dst, send_sem=send_sem.at[a * npeer + kk],
                                        recv_sem=recv_sem.at[a * npeer + kk], device_id=dev, device_id_type=MESH_ID)


def _xchg_start(name, gather, scatter, after=None):
    me_out = 4 * lax.axis_index("x") + 2 * lax.axis_index("y") + lax.axis_index("c")
    srcs = list(gather) + list(scatter)
    is_piece = [True] * len(gather) + [False] * len(scatter)
    lands = []
    for a, piece in zip(srcs, is_piece):
        if piece:
            lands.append(lax.dynamic_update_slice(lax.empty((N_DEV,) + tuple(a.shape), a.dtype), a[None],
                                                  (me_out,) + (0,) * a.ndim))
        else:
            lands.append(lax.empty(tuple(a.shape), a.dtype))
    n = len(srcs)
    nsem = n * len(_FLIPS)
    has_after = after is not None

    def body(*refs):
        src_refs = refs[:n]
        land_refs = refs[n:2 * n]
        outs = refs[2 * n + (1 if has_after else 0):]
        send_sem, recv_sem = outs[0], outs[1]
        token = outs[-1]
        me, peers = _peer_table()
        for kk in range(len(_FLIPS)):
            for a in range(n):
                _split_copy(is_piece, src_refs, land_refs, send_sem, recv_sem, a, kk, me, peers, False).start()
        token[...] = jnp.zeros_like(token)

    out_shape = ([pltpu.SemaphoreType.DMA((nsem,)), pltpu.SemaphoreType.DMA((nsem,))]
                 + [pltpu.HBM(tuple(a.shape), a.dtype) for a in srcs] + [pltpu.HBM(tuple(a.shape), a.dtype) for a in lands]
                 + [jax.ShapeDtypeStruct((8, 128), F32)])
    args = [pltpu.with_memory_space_constraint(a, pltpu.HBM) for a in srcs + lands] + ([after] if has_after else [])
    res = pl.pallas_call(
        body, name=name, out_shape=out_shape,
        in_specs=[_HBM] * (2 * n) + ([_ANY] if has_after else []),
        out_specs=[_SEM, _SEM] + [_HBM] * (2 * n) + [pl.BlockSpec(memory_space=pltpu.VMEM)],
        input_output_aliases={i: 2 + i for i in range(2 * n)},
        compiler_params=pltpu.CompilerParams(has_side_effects=_EFFECT),
    )(*args)
    state = (res[0], res[1], list(res[2:2 + n]), list(res[2 + n:2 + 2 * n]), is_piece)
    return state, res[-1]


def _xchg_wait(name, state, after):
    send_sem, recv_sem, srcs, lands, is_piece = state
    n = len(srcs)

    def body(*refs):
        src_refs = refs[:n]
        land_refs = refs[n:2 * n]
        s_sem, r_sem = refs[2 * n], refs[2 * n + 1]
        me, peers = _peer_table()
        for kk in range(len(_FLIPS)):
            for a in range(n):
                cp = _split_copy(is_piece, src_refs, land_refs, s_sem, r_sem, a, kk, me, peers, True)
                cp.wait_send()
                cp.wait_recv()

    out_shape = [pltpu.HBM(tuple(a.shape), a.dtype) for a in srcs] + [pltpu.HBM(tuple(a.shape), a.dtype) for a in lands]
    res = pl.pallas_call(
        body, name=name, out_shape=out_shape,
        in_specs=[_HBM] * (2 * n) + [_SEM, _SEM, _ANY], out_specs=[_HBM] * (2 * n),
        input_output_aliases={i: i for i in range(2 * n)},
        compiler_params=pltpu.CompilerParams(has_side_effects=_EFFECT),
    )(*srcs, *lands, send_sem, recv_sem, after)
    return list(res[n:]), list(res[:n])


_SIB = (0, 0, 1)
_ICI = ((0, 1, 0), (1, 0, 0), (1, 1, 0))


def _flip(fl):
    x, y, c = lax.axis_index("x"), lax.axis_index("y"), lax.axis_index("c")
    px = 1 - x if fl[0] else x
    py = 1 - y if fl[1] else y
    pc = 1 - c if fl[2] else c
    return (px, py, pc), 4 * px + 2 * py + pc


def _gather2_start(name, pieces, after=None):
    me_out = 4 * lax.axis_index("x") + 2 * lax.axis_index("y") + lax.axis_index("c")
    pieces = list(pieces)
    n = len(pieces)
    lands = [lax.dynamic_update_slice(lax.empty((N_DEV,) + tuple(a.shape), a.dtype), a[None],
                                      (me_out,) + (0,) * a.ndim) for a in pieces]
    first = (_SIB,) + _ICI
    has_after = after is not None

    def body(*refs):
        src_refs, land_refs = refs[:n], refs[n:2 * n]
        outs = refs[2 * n + (1 if has_after else 0):]
        send_sem, recv_sem, token = outs[0], outs[1], outs[-1]
        _, me = _flip((0, 0, 0))
        for kk, fl in enumerate(first):
            dev, _ = _flip(fl)
            for a in range(n):
                pltpu.make_async_remote_copy(src_ref=src_refs[a], dst_ref=land_refs[a].at[me],
                                             send_sem=send_sem.at[a * 4 + kk], recv_sem=recv_sem.at[a * 4 + kk],
                                             device_id=dev, device_id_type=MESH_ID).start()
        token[...] = jnp.zeros_like(token)

    hbm = [pltpu.HBM(tuple(a.shape), a.dtype) for a in pieces + lands]
    res = pl.pallas_call(
        body, name=name,
        out_shape=[pltpu.SemaphoreType.DMA((4 * n,)), pltpu.SemaphoreType.DMA((4 * n,))] + hbm
        + [jax.ShapeDtypeStruct((8, 128), F32)],
        in_specs=[_HBM] * (2 * n) + ([_ANY] if has_after else []),
        out_specs=[_SEM, _SEM] + [_HBM] * (2 * n) + [pl.BlockSpec(memory_space=pltpu.VMEM)],
        input_output_aliases={i: 2 + i for i in range(2 * n)},
        compiler_params=pltpu.CompilerParams(has_side_effects=_EFFECT),
    )(*([pltpu.with_memory_space_constraint(a, pltpu.HBM) for a in pieces + lands] + ([after] if has_after else [])))
    return (res[0], res[1], list(res[2:2 + n]), list(res[2 + n:2 + 2 * n])), res[-1]


def _gather2_forward(name, state, after):
    send_a, recv_a, pieces, lands = state
    n = len(pieces)
    first = (_SIB,) + _ICI

    def body(*refs):
        src_refs, land_refs = refs[:n], refs[n:2 * n]
        s_a, r_a = refs[2 * n], refs[2 * n + 1]
        outs = refs[2 * n + 3:]
        send_b, recv_b, token = outs[0], outs[1], outs[-1]
        for kk, fl in enumerate(first):
            dev, lin = _flip(fl)
            for a in range(n):
                cp = pltpu.make_async_remote_copy(src_ref=src_refs[a], dst_ref=land_refs[a].at[lin],
                                                  send_sem=s_a.at[a * 4 + kk], recv_sem=r_a.at[a * 4 + kk],
                                                  device_id=dev, device_id_type=MESH_ID)
                cp.wait_send()
                cp.wait_recv()
        sib, _ = _flip(_SIB)
        for j, fl in enumerate(_ICI):
            _, lin = _flip(fl)
            for a in range(n):
                pltpu.make_async_remote_copy(src_ref=land_refs[a].at[lin], dst_ref=land_refs[a].at[lin],
                                             send_sem=send_b.at[a * 3 + j], recv_sem=recv_b.at[a * 3 + j],
                                             device_id=sib, device_id_type=MESH_ID).start()
        token[...] = jnp.zeros_like(token)

    hbm = [pltpu.HBM(tuple(a.shape), a.dtype) for a in pieces + lands]
    res = pl.pallas_call(
        body, name=name,
        out_shape=[pltpu.SemaphoreType.DMA((3 * n,)), pltpu.SemaphoreType.DMA((3 * n,))] + hbm
        + [jax.ShapeDtypeStruct((8, 128), F32)],
        in_specs=[_HBM] * (2 * n) + [_SEM, _SEM, _ANY],
        out_specs=[_SEM, _SEM] + [_HBM] * (2 * n) + [pl.BlockSpec(memory_space=pltpu.VMEM)],
        input_output_aliases={i: 2 + i for i in range(2 * n)},
        compiler_params=pltpu.CompilerParams(has_side_effects=_EFFECT),
    )(*pieces, *lands, send_a, recv_a, after)
    return (res[0], res[1], list(res[2 + n:2 + 2 * n])), res[-1]


def _gather2_wait(name, state, after):
    send_b, recv_b, lands = state
    n = len(lands)

    def body(*refs):
        land_refs = refs[:n]
        s_b, r_b = refs[n], refs[n + 1]
        sib, _ = _flip(_SIB)
        for j, fl in enumerate(_ICI):
            _, sent = _flip(fl)
            _, arriving = _flip((fl[0], fl[1], 1))
            for a in range(n):
                cp = pltpu.make_async_remote_copy(src_ref=land_refs[a].at[sent], dst_ref=land_refs[a].at[arriving],
                                                  send_sem=s_b.at[a * 3 + j], recv_sem=r_b.at[a * 3 + j],
                                                  device_id=sib, device_id_type=MESH_ID)
                cp.wait_send()
                cp.wait_recv()

    res = pl.pallas_call(
        body, name=name, out_shape=[pltpu.HBM(tuple(a.shape), a.dtype) for a in lands],
        in_specs=[_HBM] * n + [_SEM, _SEM, _ANY], out_specs=[_HBM] * n,
        input_output_aliases={i: i for i in range(n)},
        compiler_params=pltpu.CompilerParams(has_side_effects=_EFFECT),
    )(*lands, send_b, recv_b, after)
    return list(res)


def _adam_math(w, g, m, v):
    m = ADAM_B1 * m + (1.0 - ADAM_B1) * g
    v = ADAM_B2 * v + (1.0 - ADAM_B2) * (g * g)
    m_hat = m / (1.0 - ADAM_B1 ** ADAM_STEP)
    v_hat = v / (1.0 - ADAM_B2 ** ADAM_STEP)
    delta = -ADAM_LR * (m_hat / (jnp.sqrt(v_hat) + ADAM_EPS) + ADAM_WD * w)
    return delta, m, v


def _adam(name, w, m, v, parts, transposed=False, dep=None, own=None):
    npart, _, cp = parts.shape
    has_dep = dep is not None
    has_own = own is not None
    if transposed:
        c, r = w.shape
        tr = _pick(r, (256, 128))
        blk = pl.BlockSpec((c, tr), lambda i, *_: (0, i))
    else:
        r, c = w.shape
        tr = _pick(r, (256, 176, 128, 64, 16, 8, 1))
        blk = pl.BlockSpec((tr, c), lambda i, *_: (i, 0))

    def body(*refs):
        refs = list(refs)
        me = refs.pop(0)[0] if has_own else None
        w_ref, m_ref, v_ref, p_ref = refs[:4]
        own_ref = refs[4] if has_own else None
        g_ref, d_ref, mo_ref, vo_ref = refs[-4:]
        g = None
        for pp in range(npart):
            part = p_ref[pp] if not has_own else jnp.where(me == pp, own_ref[...], p_ref[pp])
            g = part.astype(F32) if g is None else g + part.astype(F32)
        g = g.T[0:c, :] if transposed else g[:, 0:c]
        delta, mn, vn = _adam_math(w_ref[...], g, m_ref[...], v_ref[...])
        g_ref[...] = g
        d_ref[...] = delta
        mo_ref[...] = mn
        vo_ref[...] = vn

    in_specs = [blk, blk, blk, pl.BlockSpec((npart, tr, cp), lambda i, *_: (0, i, 0))]
    args = [w, m, v, parts]
    if has_own:
        in_specs.append(pl.BlockSpec((None, tr, cp), lambda i, me_ref: (me_ref[0], i, 0)))
        args.append(own[0])
    if has_dep:
        in_specs.append(pl.BlockSpec(memory_space=pl.ANY))
        args.append(dep)
    out = jax.ShapeDtypeStruct(w.shape, F32)
    grid_spec = pltpu.PrefetchScalarGridSpec(num_scalar_prefetch=1 if has_own else 0, grid=(r // tr,),
                                             in_specs=in_specs, out_specs=[blk, blk, blk, blk])
    return pl.pallas_call(
        body, name=name, grid_spec=grid_spec, out_shape=[out, out, out, out], compiler_params=_params(("parallel",)),
    )(*(([own[1]] if has_own else []) + args))


def _small_update(name, packs, wmv):
    nparam = len(wmv)

    def body(p_ref, *refs):
        ins, outs = refs[:3 * nparam], refs[3 * nparam:]
        tot = p_ref[0]
        for pp in range(1, N_DEV):
            tot = tot + p_ref[pp]
        outs[0][0:8, :] = tot[0:8, :]
        outs[0][8:24, :] = tot[8:24, :] + tot[24:40, :]
        grads = [tot[i:i + 1, :] for i in range(4)] + [tot[4:5, 0:B_HEADS], tot[4:5, B_HEADS:2 * B_HEADS]]
        for i, g in enumerate(grads):
            w_ref, m_ref, v_ref = ins[3 * i:3 * i + 3]
            delta, mn, vn = _adam_math(w_ref[...], g, m_ref[...], v_ref[...])
            for o_ref, val in zip(outs[1 + 4 * i:5 + 4 * i], (g, delta, mn, vn)):
                o_ref[...] = val

    flat = [a for trio in wmv for a in trio]
    out_shape = [jax.ShapeDtypeStruct((24, D_MODEL), F32)]
    for w, _, _ in wmv:
        out_shape += [jax.ShapeDtypeStruct(w.shape, F32)] * 4
    res = pl.pallas_call(body, name=name, out_shape=out_shape, compiler_params=_params())(packs, *flat)
    return res[0], [tuple(res[1 + 4 * i:5 + 4 * i]) for i in range(nparam)]


def _local_step(x, tgt, g1, gm, g2, gf, b_forget, sinks, weights, send):
    b, s, _ = x.shape
    l = s + PREFIX
    t = b * l
    (meta,) = weights("meta", x)

    h0, n1 = _embed_norm("embed_rms1_fwd", x, meta, g1)
    (w1i,) = weights("ffn1_in", n1)
    gu1, a1 = _ffn_in_fwd("ffn1_in_fwd", n1, w1i)
    (w1o,) = weights("ffn1_out", weights("ffn1_out:forward", a1))
    h1, um = _mm_res_norm("ffn1_out_fwd", a1, w1o, h0, gm, alpha=0.5)
    wi, wa, wb, wo = weights("mix", weights("mix:forward", um))
    qkv, gates, f2 = _proj_fwd("proj_fwd", um, wi)
    qkv3 = qkv.reshape(b, l, QKV_W)
    f3 = f2.reshape(b, l, 128)
    bf_row = jnp.pad(b_forget, ((0, 0), (0, 128 - B_HEADS)))
    c_col, c_row = _fgate_fwd("fgate_fwd", f3, bf_row)
    head_of_row = np.arange(STACK) // BLOCK
    slopes_np = np.exp2(-8.0 * (head_of_row + 1) / A_HEADS).astype(np.float32).reshape(STACK, 1)
    slopes = jnp.asarray(slopes_np)
    sink_rows = jnp.repeat(sinks.reshape(A_HEADS), BLOCK).reshape(STACK, 1)
    swa_bias = _swa_bias(slopes_np)
    oa3, oa3_bf, lse_a = _swa_fwd("swa_fwd", qkv3, slopes, sink_rows, swa_bias)
    ob3, lse_b = _fox_fwd("fox_fwd", qkv3, c_col, c_row)
    oa = oa3_bf.reshape(t, A_WIDTH)
    ob = ob3.reshape(t, B_WIDTH)
    mixed, ya, yb = _branch_gate_fwd("branch_gate_fwd", oa, ob, wa, wb, gates, dep=weights("ffn2:forward", ob))
    h2, n2 = _mm_res_norm("mix_out_fwd", mixed, wo, h1, g2)
    w2i, w2o = weights("ffn2", h2)
    gu2, a2 = _ffn_in_fwd("ffn2_in_fwd", n2, w2i)

    dh3, dh3_bf, loss_blk, dgf = _ffn_out_loss("ffn2_out_loss", a2, w2o, h2, gf, tgt, 0.5)

    def ffn_bwd(tag, dh, dh_bf, h_in, g_norm, n_in, gu, a, w_in_blk, w_out, one_send, examples=None):
        dw_out = _ffn_out_bwd_w(tag + "_out_bwd_w", a, dh_bf)
        dgu = _ffn_out_bwd_x(tag + "_out_bwd_x", dh_bf, w_out, gu, dep=None if one_send else send(tag + "_out", (dw_out,)))
        dw_in = _ffn_in_bwd_w(tag + "_in_bwd_w", n_in, dgu)
        token = send(tag, (dw_in, dw_out)) if one_send else send(tag + "_in", (dw_in,))
        return _ffn_in_bwd_x(tag + "_in_bwd_x", dgu, w_in_blk, h_in, g_norm, dh, dep=token, examples=examples)

    dh2, dh2_bf, dg2 = ffn_bwd("ffn2", dh3, dh3_bf, h2, g2, n2, gu2, a2, w2i, w2o, True)

    dwo = _mm_tn("mix_out_bwd_w", mixed, dh2_bf)
    dya, dyb, dgates = _mix_out_gate_bwd("mix_out_gate_bwd", dh2_bf, wo, gates, ya, yb)
    doa, dob = _branch_bwd_x("branch_bwd_x", dya, dyb, wa, wb)
    dwa, dwb = _branch_bwd_w("branch_bwd_w", oa, ob, dya, dyb)
    dqkv3, dcq, dck = _fox_bwd("fox_bwd", qkv3, c_col, c_row, ob3, lse_b, dob.reshape(b, l, B_WIDTH))
    dqkv3, dsink = _swa_bwd("swa_bwd", qkv3, oa3, lse_a, doa.reshape(b, l, A_WIDTH), slopes, sink_rows, swa_bias, dqkv3)
    dqkv = dqkv3.reshape(t, QKV_W)
    df3, dbf = _fgate_bwd("fgate_bwd", f3, bf_row, dcq, dck)
    df = df3.reshape(t, 128)
    dwi_qkv = _mm_tn("proj_qkv_bwd_w", um, dqkv, tm_c=(512,), tn_c=(768,))
    dwi_g = _mm_tn("proj_gates_bwd_w", um, dgates, tm_c=(512,), tn_c=(512,))
    dwi_f = _mm_tn("proj_f_bwd_w", um, df, tm_c=(512,), tn_c=(128,))
    token = send("mix", (dwi_qkv, dwi_g, dwi_f, dwa, dwb, dwo))
    dh1, dh1_bf, dgm = _proj_bwd_x("proj_bwd_x", dqkv, dgates, df, wi, h1, gm, dh2, dep=token)

    grad_x, dmeta3, dg1 = ffn_bwd("ffn1", dh1, dh1_bf, h0, g1, n1, gu1, a1, w1i, w1o, False, examples=b)
    dmeta = dmeta3.reshape(b * N_META, D_MODEL)

    misc = jnp.concatenate([dbf[:, 0:B_HEADS], dsink[:, 0].reshape(1, A_HEADS), loss_blk[0:1, 0:1]], axis=1)
    misc = jnp.pad(misc, ((0, 0), (0, D_MODEL - misc.shape[1])))
    row = lax.broadcasted_iota(jnp.int32, (8, D_MODEL), 0)
    vec = jnp.zeros((8, D_MODEL), F32)
    for i, piece in enumerate((dg1, dgm, dg2, dgf, misc)):
        vec = jnp.where(row == i, piece, vec)
    small = jnp.concatenate([vec, dmeta], axis=0)
    return grad_x, small


def _pad_to(a, rows, cols):
    return jnp.pad(a, ((0, rows - a.shape[0]), (0, cols - a.shape[1])))


def _ffn_out_from_gathered(name, g):
    def body(g_ref, o_ref):
        o_ref[0:FFO_SHARD, :] = g_ref[0]
        o_ref[FFO_SHARD:FF_SHARD, :] = g_ref[1]
        o_ref[FF_SHARD:FF_SHARD_P, :] = jnp.zeros((FF_SHARD_P - FF_SHARD, D_MODEL), BF16)

    return pl.pallas_call(
        body, name=name, grid=(4,),
        in_specs=[pl.BlockSpec((2, FFO_SHARD, D_MODEL), lambda j: (j, 0, 0))],
        out_specs=pl.BlockSpec((FF_SHARD_P, D_MODEL), lambda j: (j, 0)),
        out_shape=jax.ShapeDtypeStruct((D_FF_P, D_MODEL), BF16), compiler_params=_params(("parallel",)),
    )(g)


def _ffn_out_bwd_w(name, a, dh):
    t = a.shape[0]
    tn = D_MODEL // 2

    def body(a_ref, b_ref, o_ref):
        acc = lax.dot_general(a_ref[...], _bf(b_ref[...]), _TN, preferred_element_type=F32) * 0.5
        o_ref[0] = acc[0:FFO_SHARD].astype(BF16)
        o_ref[1] = acc[FFO_SHARD:FF_SHARD].astype(BF16)

    return pl.pallas_call(
        body, name=name, grid=(2, 4),
        in_specs=[pl.BlockSpec((t, FF_SHARD_P), lambda j, i: (0, i)), pl.BlockSpec((t, tn), lambda j, i: (0, j))],
        out_specs=pl.BlockSpec((2, FFO_SHARD, tn), lambda j, i: (i, 0, j)),
        out_shape=jax.ShapeDtypeStruct((N_DEV, FFO_SHARD, D_MODEL), BF16), compiler_params=_params(("parallel", "parallel")),
    )(a, dh)


def _proj_segments():
    segs = [(HEAD_DIM * h, HEAD_DIM, 0, B_SEG + HEAD_DIM * A_HEAD_ORDER.index(h)) for h in range(A_HEADS)]
    segs += [(512, 128, 0, B_SEG + A_WIDTH), (640, 128, 0, B_SEG + A_WIDTH + 128)]
    for first, off in ((768, 0), (1280, 128), (1792, 256)):
        segs += [(first + 128 * hp, 128, 0, PAIR_W * hp + off) for hp in range(4)]
    segs += [(2304, B_HEADS, 2, 0), (2312, 2 * D_MODEL, 1, 0)]
    return segs


_RELAYOUT_ROWS = 256


def _proj_from_gathered(name, g):
    rows = _RELAYOUT_ROWS

    def body(g_ref, o_ref):
        def cols(first, width):
            out = []
            for p in range(N_DEV):
                lo, hi = max(first, WIN_SHARD * p), min(first + width, WIN_SHARD * (p + 1))
                if lo < hi:
                    out.append(g_ref[p, :, lo - WIN_SHARD * p:hi - WIN_SHARD * p])
            return out

        parts = []
        for arr in (0, 1, 2):
            for first, width, _, _ in sorted((s for s in _proj_segments() if s[2] == arr), key=lambda s: s[3]):
                parts += cols(first, width)
            if arr == 0:
                parts.append(jnp.zeros((rows, P_GATES - QKV_W), BF16))
        parts.append(jnp.zeros((rows, 128 - B_HEADS), BF16))
        o_ref[...] = jnp.concatenate(parts, axis=1)

    return pl.pallas_call(
        body, name=name, grid=(D_MODEL // rows,),
        in_specs=[pl.BlockSpec((N_DEV, rows, WIN_SHARD_P), lambda i: (0, i, 0))],
        out_specs=pl.BlockSpec((rows, PROJ_P), lambda i: (i, 0)),
        out_shape=jax.ShapeDtypeStruct((D_MODEL, PROJ_P), BF16), compiler_params=_params(("parallel",)),
    )(g)


def _proj_to_scatter(name, dqkv_w, dg_w, df_w):
    rows = _RELAYOUT_ROWS
    segs = sorted(_proj_segments())

    def body(q_ref, g_ref, f_ref, o_ref):
        arrays = (q_ref, g_ref, f_ref)
        for p in range(N_DEV):
            parts = []
            for first, width, arr, at in segs:
                lo, hi = max(first, WIN_SHARD * p), min(first + width, WIN_SHARD * (p + 1))
                if lo < hi:
                    parts.append(arrays[arr][:, at + lo - first:at + hi - first])
            parts.append(jnp.zeros((rows, WIN_SHARD_P - WIN_SHARD), BF16))
            o_ref[p] = jnp.concatenate(parts, axis=1)

    return pl.pallas_call(
        body, name=name, grid=(D_MODEL // rows,),
        in_specs=[pl.BlockSpec((rows, QKV_W), lambda i: (i, 0)), pl.BlockSpec((rows, 2 * D_MODEL), lambda i: (i, 0)),
                  pl.BlockSpec((rows, 128), lambda i: (i, 0))],
        out_specs=pl.BlockSpec((N_DEV, rows, WIN_SHARD_P), lambda i: (0, i, 0)),
        out_shape=jax.ShapeDtypeStruct((N_DEV, D_MODEL, WIN_SHARD_P), BF16), compiler_params=_params(("parallel",)),
    )(dqkv_w, dg_w, df_w)


def _a_rows_from_natural(w):
    return jnp.concatenate([w[HEAD_DIM * h:HEAD_DIM * (h + 1)] for h in A_HEAD_ORDER], axis=0)


def _a_rows_to_natural(w):
    return jnp.concatenate([w[HEAD_DIM * A_HEAD_ORDER.index(h):HEAD_DIM * (A_HEAD_ORDER.index(h) + 1)]
                            for h in range(A_HEADS)], axis=0)


def kernel(x, meta_tokens, ffn1_norm, ffn1_w_in, ffn1_w_out, mix_norm, w_in, b_forget, attn_sinks, w_branch_a, w_branch_b, w_out, ffn2_norm, ffn2_w_in, ffn2_w_out, final_norm, loss_target, m_meta_tokens, m_ffn1_norm, m_ffn1_w_in, m_ffn1_w_out, m_mix_norm, m_w_in, m_b_forget, m_attn_sinks, m_w_branch_a, m_w_branch_b, m_w_out, m_ffn2_norm, m_ffn2_w_in, m_ffn2_w_out, m_final_norm, v_meta_tokens, v_ffn1_norm, v_ffn1_w_in, v_ffn1_w_out, v_mix_norm, v_w_in, v_b_forget, v_attn_sinks, v_w_branch_a, v_w_branch_b, v_w_out, v_ffn2_norm, v_ffn2_w_in, v_ffn2_w_out, v_final_norm):
    me = 4 * lax.axis_index("x") + 2 * lax.axis_index("y") + lax.axis_index("c")

    def shard(w, tok, rows=None, cols=None):
        piece = (w[0] if tok is None else w[0] + tok[0, 0]).astype(BF16)
        return piece if rows is None else _pad_to(piece, rows, cols)

    first_level, second_level = {}, {}
    first_level["meta"], tok = _gather2_start("gather_meta_start", (meta_tokens,))
    first_level["ffn1_in"], tok = _gather2_start(
        "gather_ffn1_in_start", (shard(ffn1_w_in, None, D_MODEL, FF_SHARD_P),), after=tok)
    first_level["ffn1_out"], tok = _gather2_start("gather_ffn1_out_start", (shard(ffn1_w_out, tok),), after=tok)
    first_level["mix"], tok = _gather2_start(
        "gather_mix_start", (shard(w_in, tok, D_MODEL, WIN_SHARD_P), shard(w_branch_a, tok), shard(w_branch_b, tok),
                             shard(w_out, tok)), after=tok)
    first_level["ffn2"], tok = _gather2_start(
        "gather_ffn2_start", (shard(ffn2_w_in, tok, D_MODEL, FF_SHARD_P), shard(ffn2_w_out, tok)), after=tok)
    started = {"tok": tok}

    def weights(group, after):
        if group.endswith(":forward"):
            group = group[:-len(":forward")]
            second_level[group], token = _gather2_forward("gather_" + group + "_forward", first_level[group], after)
            return token
        if group == "meta":
            after = weights("meta:forward", started["tok"])
        if group == "ffn1_in":
            after = weights("ffn1_in:forward", after)
        got = _gather2_wait("gather_" + group + "_wait", second_level[group], after)
        if group == "mix":
            gwi, gwa, gwb, gwo = got
            return (_proj_from_gathered("proj_w_relayout", gwi), _a_rows_from_natural(gwa.transpose(1, 0, 2).reshape(A_WIDTH, D_MODEL)),
                    gwb.transpose(1, 0, 2).reshape(B_WIDTH, D_MODEL), gwo.reshape(D_MODEL, D_MODEL))
        if group == "ffn1_out":
            return (_ffn_out_from_gathered("ffn1_w_out_relayout", got[0]),)
        if group == "meta":
            return (got[0].transpose(1, 0, 2).reshape(N_META, D_MODEL),)
        if group == "ffn1_in":
            return (got[0].reshape(2, 4, D_MODEL, FF_SHARD_P),)
        return got[0].reshape(2, 4, D_MODEL, FF_SHARD_P), _ffn_out_from_gathered("ffn2_w_out_relayout", got[1])

    scatter_state = {}

    def send(group, grads):
        if group == "mix":
            dwi_qkv, dwi_g, dwi_f, dwa, dwb, dwo = grads
            dwa = _a_rows_to_natural(dwa)
            blocks = (_proj_to_scatter("proj_dw_relayout", dwi_qkv, dwi_g, dwi_f), dwa.reshape(A_WIDTH, N_DEV, 128).transpose(1, 0, 2),
                      dwb.reshape(B_WIDTH, N_DEV, 128).transpose(1, 0, 2), dwo.reshape(N_DEV, 128, D_MODEL))
        elif group.endswith("_in"):
            blocks = (grads[0].reshape(N_DEV, D_MODEL, FF_SHARD_P),)
        elif group.endswith("_out"):
            blocks = (grads[0],)
        else:
            blocks = (grads[0].reshape(N_DEV, D_MODEL, FF_SHARD_P), grads[1])
        scatter_state[group], token = _xchg_start("scatter_" + group + "_start", (), blocks)
        return token

    gf = final_norm.reshape(1, D_MODEL)
    grad_x, small = _local_step(x, loss_target, ffn1_norm, mix_norm, ffn2_norm, gf, b_forget, attn_sinks, weights, send)

    small_state, after = _xchg_start("gather_small_start", (small,), ())
    out = {}
    updates = (
        ("ffn2", (("ffn2_w_in", ffn2_w_in, m_ffn2_w_in, v_ffn2_w_in), ("ffn2_w_out", ffn2_w_out, m_ffn2_w_out, v_ffn2_w_out))),
        ("ffn1_out", (("ffn1_w_out", ffn1_w_out, m_ffn1_w_out, v_ffn1_w_out),)),
        ("mix", (("w_in", w_in, m_w_in, v_w_in), ("w_branch_a", w_branch_a, m_w_branch_a, v_w_branch_a),
                 ("w_branch_b", w_branch_b, m_w_branch_b, v_w_branch_b), ("w_out", w_out, m_w_out, v_w_out))),
    )
    last_update = ("ffn1_in", (("ffn1_w_in", ffn1_w_in, m_ffn1_w_in, v_ffn1_w_in),))

    def update(group, members, after):
        parts_list, blocks_list = _xchg_wait("scatter_" + group + "_wait", scatter_state[group], after)
        prev = None
        for (nm, w, m, v), parts, blocks in zip(members, parts_list, blocks_list):
            if nm.endswith("w_in"):
                res4 = _adam("adam_" + nm, w[0].T, m[0].T, v[0].T, parts, transposed=True, dep=prev, own=(blocks, me_arr))
                out[nm] = tuple(r.T[None] for r in res4)
            else:
                res4 = _adam("adam_" + nm, w[0], m[0], v[0], parts, dep=prev, own=(blocks, me_arr))
                out[nm] = tuple(r[None] for r in res4)
            prev = res4[0]
        return prev

    me_arr = me.reshape(1).astype(jnp.int32)
    for group, members in updates + (last_update,):
        after = update(group, members, after)
    (packs,), _ = _xchg_wait("gather_small_wait", small_state, after)

    row = lambda a: a.reshape(1, D_MODEL)
    tot, small_res = _small_update("small_update", packs, (
        (ffn1_norm, m_ffn1_norm, v_ffn1_norm), (mix_norm, m_mix_norm, v_mix_norm), (ffn2_norm, m_ffn2_norm, v_ffn2_norm),
        (row(final_norm), row(m_final_norm), row(v_final_norm)), (b_forget, m_b_forget, v_b_forget),
        (attn_sinks, m_attn_sinks, v_attn_sinks)))
    for nm, res4 in zip(("ffn1_norm", "mix_norm", "ffn2_norm", "final_norm", "b_forget", "attn_sinks"), small_res):
        out[nm] = tuple(r.reshape(D_MODEL) for r in res4) if nm == "final_norm" else res4
    loss = tot[4, 2 * B_HEADS]
    g_meta = lax.dynamic_slice(tot[8:24, :], (0, me * 128), (N_META, 128))
    out["meta_tokens"] = tuple(_adam("adam_meta_tokens", meta_tokens, m_meta_tokens, v_meta_tokens, g_meta[None]))

    names = ("meta_tokens", "ffn1_norm", "ffn1_w_in", "ffn1_w_out", "mix_norm", "w_in", "b_forget", "attn_sinks",
             "w_branch_a", "w_branch_b", "w_out", "ffn2_norm", "ffn2_w_in", "ffn2_w_out", "final_norm")
    return (loss, grad_x) + tuple(out[nm][kind] for kind in range(4) for nm in names)
```

```python
import jax
import jax.numpy as jnp
import numpy as np
from jax import lax
from jax.experimental import pallas as pl
from jax.experimental.pallas import tpu as pltpu

F32 = jnp.float32
BF16 = jnp.bfloat16

D_MODEL = 1024
N_META = 16
BLOCK = 128
PREFIX = 128
N_PAD = PREFIX - N_META
HEAD_DIM = 64
A_HEADS = 8
B_HEADS = 8
A_WIDTH = 512
A_KV_WIDTH = 128
B_WIDTH = 512
D_FF = 2816
N_DEV = 8
FF_SHARD = 2 * D_FF // N_DEV
FF_SHARD_P = 768
FFO_SHARD = D_FF // N_DEV
D_FF_P = 4 * FF_SHARD_P
W_IN_COLS = 4360
WIN_SHARD = W_IN_COLS // N_DEV
WIN_SHARD_P = 640
PAIR_W = 3 * 128
B_SEG = 4 * PAIR_W
A_SEG = A_WIDTH + 2 * A_KV_WIDTH
QKV_W = B_SEG + A_SEG
P_GATES = 2 * (2 * D_MODEL)
P_F = P_GATES + 2 * D_MODEL
PROJ_P = P_F + 128
A_HEAD_ORDER = (0, 4, 1, 5, 2, 6, 3, 7)
EPS = 1e-6
NEG = -1e30
SCALE = HEAD_DIM ** -0.5
ADAM_LR = 0.001
ADAM_B1 = 0.9
ADAM_B2 = 0.999
ADAM_EPS = 1e-08
ADAM_WD = 0.01
ADAM_STEP = 10
VMEM_LIMIT = 56 * 1024 * 1024
MESH_ID = pl.DeviceIdType.MESH
SMALL_ROWS = 40

_NN = (((1,), (0,)), ((), ()))
_NT = (((1,), (1,)), ((), ()))
_TN = (((0,), (0,)), ((), ()))


def _params(sem=None):
    return pltpu.CompilerParams(dimension_semantics=sem, vmem_limit_bytes=VMEM_LIMIT)


def _pick(n, cands):
    for c in cands:
        if n % c == 0:
            return c
    raise ValueError(f"no tile for {n}")


def _bf(v):
    return v if v.dtype == BF16 else v.astype(BF16)


def _mm(name, a, b, dims, grid, a_spec, b_spec, o_spec, out_shape, out_dtype, alpha=1.0):
    def body(a_ref, b_ref, o_ref):
        acc = lax.dot_general(_bf(a_ref[...]), _bf(b_ref[...]), dims, preferred_element_type=F32)
        if alpha != 1.0:
            acc = acc * alpha
        o_ref[...] = acc.astype(o_ref.dtype)

    return pl.pallas_call(
        body, name=name, grid=grid, in_specs=[a_spec, b_spec], out_specs=o_spec,
        out_shape=jax.ShapeDtypeStruct(out_shape, out_dtype),
        compiler_params=_params(("parallel",) * len(grid)),
    )(a, b)


def _mm_res_norm(name, a, w, res, g_next, alpha=1.0):
    t, k = a.shape
    tm = _pick(t, (544, 384, 256, 128))

    def body(a_ref, w_ref, r_ref, g_ref, h_ref, n_ref):
        acc = lax.dot_general(_bf(a_ref[...]), w_ref[...], _NN, preferred_element_type=F32)
        if alpha != 1.0:
            acc = acc * alpha
        hv = acc + r_ref[...]
        h_ref[...] = hv
        r = lax.rsqrt(jnp.mean(hv * hv, axis=-1, keepdims=True) + EPS)
        n_ref[...] = ((hv * r) * g_ref[...]).astype(BF16)

    row = pl.BlockSpec((tm, D_MODEL), lambda i: (i, 0))
    return pl.pallas_call(
        body, name=name, grid=(t // tm,),
        in_specs=[pl.BlockSpec((tm, k), lambda i: (i, 0)), pl.BlockSpec((k, D_MODEL), lambda i: (0, 0)), row,
                  pl.BlockSpec((1, D_MODEL), lambda i: (0, 0))],
        out_specs=[row, row],
        out_shape=[jax.ShapeDtypeStruct((t, D_MODEL), F32), jax.ShapeDtypeStruct((t, D_MODEL), BF16)],
        compiler_params=_params(("parallel",)),
    )(a, w, res, g_next)


def _mm_tn(name, a, b, out_dtype=BF16, alpha=1.0, tm_c=(768, 512, 256, 128), tn_c=(512, 640, 256, 128)):
    t, m = a.shape
    n = b.shape[1]
    tm = _pick(m, tm_c)
    tn = _pick(n, tn_c)
    bytes_a, bytes_b = a.size * a.dtype.itemsize, b.size * b.dtype.itemsize
    if bytes_a + bytes_b * (m // tm) <= bytes_b + bytes_a * (n // tn):
        return _mm(name, a, b, _TN, (m // tm, n // tn),
                   pl.BlockSpec((t, tm), lambda i, j: (0, i)), pl.BlockSpec((t, tn), lambda i, j: (0, j)),
                   pl.BlockSpec((tm, tn), lambda i, j: (i, j)), (m, n), out_dtype, alpha=alpha)
    return _mm(name, a, b, _TN, (n // tn, m // tm),
               pl.BlockSpec((t, tm), lambda j, i: (0, i)), pl.BlockSpec((t, tn), lambda j, i: (0, j)),
               pl.BlockSpec((tm, tn), lambda j, i: (i, j)), (m, n), out_dtype, alpha=alpha)


def _ffn_in_fwd(name, n, wblk, dep=None):
    t = n.shape[0]
    tm = _pick(t, (1088, 768, 512, 256, 128))
    has_dep = dep is not None

    def body(n_ref, w_ref, *rest):
        gu_ref, a_ref = rest[-2], rest[-1]
        nv = n_ref[...]
        g = lax.dot_general(nv, w_ref[0], _NN, preferred_element_type=F32)
        u = lax.dot_general(nv, w_ref[1], _NN, preferred_element_type=F32)
        sg = jax.nn.sigmoid(g)
        silu = g * sg
        a_ref[...] = (silu * u).astype(BF16)
        gu_ref[0] = ((0.5 * u) * (sg + silu * (1.0 - sg))).astype(BF16)
        gu_ref[1] = (0.5 * silu).astype(BF16)

    return pl.pallas_call(
        body, name=name, grid=(t // tm, 4),
        in_specs=[pl.BlockSpec((tm, D_MODEL), lambda i, j: (i, 0)),
                  pl.BlockSpec((2, None, D_MODEL, FF_SHARD_P), lambda i, j: (0, j, 0, 0))]
        + ([pl.BlockSpec(memory_space=pl.ANY)] if has_dep else []),
        out_specs=[pl.BlockSpec((2, tm, FF_SHARD_P), lambda i, j: (0, i, j)),
                   pl.BlockSpec((tm, FF_SHARD_P), lambda i, j: (i, j))],
        out_shape=[jax.ShapeDtypeStruct((2, t, D_FF_P), BF16), jax.ShapeDtypeStruct((t, D_FF_P), BF16)],
        compiler_params=_params(("parallel", "parallel")),
    )(*((n, wblk) + ((dep,) if has_dep else ())))


def _ffn_out_bwd_x(name, dh, w_out, gu, dep=None):
    t = dh.shape[0]
    tm = _pick(t, (1088, 768, 512, 256, 128))
    has_dep = dep is not None

    def body(dh_ref, w_ref, gu_ref, *rest):
        o_ref = rest[-1]
        w = w_ref[pl.ds(pl.multiple_of(pl.program_id(1) * FF_SHARD_P, FF_SHARD_P), FF_SHARD_P), :]
        da = lax.dot_general(_bf(dh_ref[...]), w, _NT, preferred_element_type=F32)
        o_ref[0] = (da * gu_ref[0].astype(F32)).astype(BF16)
        o_ref[1] = (da * gu_ref[1].astype(F32)).astype(BF16)

    gu_spec = pl.BlockSpec((2, tm, FF_SHARD_P), lambda i, j: (0, i, j))
    return pl.pallas_call(
        body, name=name, grid=(t // tm, 4),
        in_specs=[pl.BlockSpec((tm, D_MODEL), lambda i, j: (i, 0)),
                  pl.BlockSpec((D_FF_P, D_MODEL), lambda i, j: (0, 0), pipeline_mode=pl.Buffered(1)),
                  gu_spec] + ([pl.BlockSpec(memory_space=pl.ANY)] if has_dep else []),
        out_specs=gu_spec, out_shape=jax.ShapeDtypeStruct((2, t, D_FF_P), BF16),
        compiler_params=_params(("parallel", "parallel")),
    )(*((dh, w_out, gu) + ((dep,) if has_dep else ())))


def _rms_bwd_rows(dn, h, g, dres):
    r = lax.rsqrt(jnp.mean(h * h, axis=-1, keepdims=True) + EPS)
    tv = dn * g
    dot = jnp.mean(tv * h, axis=-1, keepdims=True)
    return dres + (r * tv - h * (r * r * r * dot)), jnp.sum(dn * (h * r), axis=0, keepdims=True)


def _accumulate_rows(ref, part, first):
    @pl.when(first)
    def _():
        ref[...] = part

    @pl.when(jnp.logical_not(first))
    def _():
        ref[...] += part


def _ffn_in_bwd_x(name, dgu, wblk, h_in, g_norm, dres, dep=None, examples=None):
    t = dgu.shape[1]
    tm = _pick(t, (544, 384, 256, 128))
    has_dep = dep is not None
    split = examples is not None
    if split:
        l = t // examples
        per = l // tm
        assert per * tm == l and tm > PREFIX

    def body(d_ref, w_ref, h_ref, g_ref, r_ref, *rest):
        acc = None
        for s in range(2):
            for j in range(4):
                part = lax.dot_general(d_ref[s, :, FF_SHARD_P * j:FF_SHARD_P * (j + 1)], w_ref[s, j], _NT,
                                       preferred_element_type=F32)
                acc = part if acc is None else acc + part
        dh, dg = _rms_bwd_rows(acc, h_ref[...], g_ref[...], r_ref[...])
        i = pl.program_id(0)
        if not split:
            dh_ref, dhb_ref, dg_ref = rest[-3:]
            dh_ref[...] = dh
            dhb_ref[...] = dh.astype(BF16)
        else:
            gx_ref, meta_ref, dg_ref, buf, sem = rest[-5:]

            def out_copy(step):
                bi, r = step // per, step % per
                head = pltpu.make_async_copy(buf.at[pl.ds(PREFIX, tm - PREFIX)], gx_ref.at[bi, pl.ds(0, tm - PREFIX)], sem)
                if per == 1:
                    return r == 0, head, None
                return r == 0, head, pltpu.make_async_copy(
                    buf, gx_ref.at[bi, pl.ds(pl.multiple_of(jnp.maximum(r, 1) * tm - PREFIX, 8), tm)], sem)

            def run(step, method):
                is_head, head, later = out_copy(step)

                @pl.when(is_head)
                def _():
                    getattr(head, method)()

                if later is not None:
                    @pl.when(jnp.logical_not(is_head))
                    def _():
                        getattr(later, method)()

            @pl.when(i > 0)
            def _():
                run(i - 1, "wait")

            buf[...] = dh

            @pl.when(i % per == 0)
            def _():
                meta_ref[...] = dh[N_PAD:PREFIX]

            run(i, "start")

            @pl.when(i == pl.num_programs(0) - 1)
            def _():
                run(i, "wait")

        _accumulate_rows(dg_ref, dg, i == 0)

    row = pl.BlockSpec((tm, D_MODEL), lambda i: (i, 0))
    vec = pl.BlockSpec((1, D_MODEL), lambda i: (0, 0))
    if split:
        out_specs = [pl.BlockSpec(memory_space=pl.ANY), pl.BlockSpec((None, N_META, D_MODEL), lambda i: (i // per, 0, 0)), vec]
        out_shape = [jax.ShapeDtypeStruct((examples, l - PREFIX, D_MODEL), F32),
                     jax.ShapeDtypeStruct((examples, N_META, D_MODEL), F32), jax.ShapeDtypeStruct((1, D_MODEL), F32)]
        scratch = [pltpu.VMEM((tm, D_MODEL), F32), pltpu.SemaphoreType.DMA(())]
    else:
        out_specs = [row, row, vec]
        out_shape = [jax.ShapeDtypeStruct((t, D_MODEL), F32), jax.ShapeDtypeStruct((t, D_MODEL), BF16),
                     jax.ShapeDtypeStruct((1, D_MODEL), F32)]
        scratch = []
    return pl.pallas_call(
        body, name=name, grid=(t // tm,),
        in_specs=[pl.BlockSpec((2, tm, D_FF_P), lambda i: (0, i, 0)),
                  pl.BlockSpec((2, 4, D_MODEL, FF_SHARD_P), lambda i: (0, 0, 0, 0), pipeline_mode=pl.Buffered(1)),
                  row, vec, row]
        + ([pl.BlockSpec(memory_space=pl.ANY)] if has_dep else []),
        out_specs=out_specs, out_shape=out_shape, scratch_shapes=scratch,
        compiler_params=_params(("arbitrary",)),
    )(*((dgu, wblk, h_in, g_norm, dres) + ((dep,) if has_dep else ())))


def _proj_fwd(name, um, wi):
    t = um.shape[0]
    tm = _pick(t, (544, 384, 256, 128))

    def body(u_ref, w_ref, q_ref, g_ref, f_ref):
        uv = u_ref[...]
        q_ref[...] = lax.dot_general(uv, w_ref[:, 0:QKV_W], _NN, preferred_element_type=F32).astype(BF16)
        g_ref[...] = lax.dot_general(uv, w_ref[:, P_GATES:P_F], _NN, preferred_element_type=F32).astype(BF16)
        f_ref[...] = lax.dot_general(uv, w_ref[:, P_F:PROJ_P], _NN, preferred_element_type=F32)

    return pl.pallas_call(
        body, name=name, grid=(t // tm,),
        in_specs=[pl.BlockSpec((tm, D_MODEL), lambda i: (i, 0)),
                  pl.BlockSpec((D_MODEL, PROJ_P), lambda i: (0, 0), pipeline_mode=pl.Buffered(1))],
        out_specs=[pl.BlockSpec((tm, QKV_W), lambda i: (i, 0)), pl.BlockSpec((tm, 2 * D_MODEL), lambda i: (i, 0)),
                   pl.BlockSpec((tm, 128), lambda i: (i, 0))],
        out_shape=[jax.ShapeDtypeStruct((t, QKV_W), BF16), jax.ShapeDtypeStruct((t, 2 * D_MODEL), BF16),
                   jax.ShapeDtypeStruct((t, 128), F32)],
        compiler_params=_params(("parallel",)),
    )(um, wi)


def _proj_bwd_x(name, dqkv, dgates, df, wi, h_in, g_norm, dres, dep=None):
    t = dqkv.shape[0]
    tm = _pick(t, (544, 384, 256, 128))
    has_dep = dep is not None

    def body(q_ref, gt_ref, f_ref, w_ref, h_ref, g_ref, r_ref, *rest):
        dh_ref, dhb_ref, dg_ref = rest[-3:]
        acc = lax.dot_general(q_ref[...], w_ref[:, 0:QKV_W], _NT, preferred_element_type=F32)
        acc = acc + lax.dot_general(gt_ref[...], w_ref[:, P_GATES:P_F], _NT, preferred_element_type=F32)
        acc = acc + lax.dot_general(f_ref[...], w_ref[:, P_F:PROJ_P], _NT, preferred_element_type=F32)
        dh, dg = _rms_bwd_rows(acc, h_ref[...], g_ref[...], r_ref[...])
        dh_ref[...] = dh
        dhb_ref[...] = dh.astype(BF16)
        _accumulate_rows(dg_ref, dg, pl.program_id(0) == 0)

    row = pl.BlockSpec((tm, D_MODEL), lambda i: (i, 0))
    vec = pl.BlockSpec((1, D_MODEL), lambda i: (0, 0))
    return pl.pallas_call(
        body, name=name, grid=(t // tm,),
        in_specs=[pl.BlockSpec((tm, QKV_W), lambda i: (i, 0)), pl.BlockSpec((tm, 2 * D_MODEL), lambda i: (i, 0)),
                  pl.BlockSpec((tm, 128), lambda i: (i, 0)),
                  pl.BlockSpec((D_MODEL, PROJ_P), lambda i: (0, 0), pipeline_mode=pl.Buffered(1)), row, vec, row]
        + ([pl.BlockSpec(memory_space=pl.ANY)] if has_dep else []),
        out_specs=[row, row, vec],
        out_shape=[jax.ShapeDtypeStruct((t, D_MODEL), F32), jax.ShapeDtypeStruct((t, D_MODEL), BF16),
                   jax.ShapeDtypeStruct((1, D_MODEL), F32)],
        compiler_params=_params(("arbitrary",)),
    )(*((dqkv, dgates, df, wi, h_in, g_norm, dres) + ((dep,) if has_dep else ())))


def _ffn_in_bwd_w(name, n, dgu):
    t = n.shape[0]
    return _mm(name, dgu, n, _TN, (2, 4),
               pl.BlockSpec((None, t, FF_SHARD_P), lambda s, j: (s, 0, j)),
               pl.BlockSpec((t, D_MODEL), lambda s, j: (0, 0)),
               pl.BlockSpec((None, None, FF_SHARD_P, D_MODEL), lambda s, j: (s, j, 0, 0)),
               (2, 4, FF_SHARD_P, D_MODEL), BF16)


def _embed_norm(name, x, meta, g):
    b, s, _ = x.shape
    half = (s + PREFIX) // 2
    first = half - PREFIX
    assert first > 0 and half % 16 == 0

    def body(x_ref, m_ref, g_ref, h_ref, n_ref, buf, sem):
        bi, k = pl.program_id(0), pl.program_id(1)

        def tokens(example, second, method):
            if second:
                cp = pltpu.make_async_copy(x_ref.at[example, pl.ds(first, half)], buf.at[1], sem.at[1])
            else:
                cp = pltpu.make_async_copy(x_ref.at[example, pl.ds(0, first)], buf.at[0, pl.ds(PREFIX, first)], sem.at[0])
            getattr(cp, method)()

        @pl.when((bi == 0) & (k == 0))
        def _():
            tokens(0, False, "start")

        @pl.when(k == 0)
        def _():
            tokens(bi, True, "start")
            tokens(bi, False, "wait")
            buf[0, 0:N_PAD, :] = jnp.zeros((N_PAD, D_MODEL), F32)
            buf[0, N_PAD:PREFIX, :] = m_ref[...]

        @pl.when(k == 1)
        def _():
            @pl.when(bi + 1 < b)
            def _():
                tokens(bi + 1, False, "start")

            tokens(bi, True, "wait")

        hv = buf[k]
        h_ref[...] = hv
        r = lax.rsqrt(jnp.mean(hv * hv, axis=-1, keepdims=True) + EPS)
        n_ref[...] = ((hv * r) * g_ref[...]).astype(BF16)

    rows = pl.BlockSpec((half, D_MODEL), lambda bi, k: (2 * bi + k, 0))
    return pl.pallas_call(
        body, name=name, grid=(b, 2),
        in_specs=[pl.BlockSpec(memory_space=pl.ANY), pl.BlockSpec((N_META, D_MODEL), lambda bi, k: (0, 0)),
                  pl.BlockSpec((1, D_MODEL), lambda bi, k: (0, 0))],
        out_specs=[rows, rows],
        out_shape=[jax.ShapeDtypeStruct((2 * b * half, D_MODEL), F32), jax.ShapeDtypeStruct((2 * b * half, D_MODEL), BF16)],
        scratch_shapes=[pltpu.VMEM((2, half, D_MODEL), F32), pltpu.SemaphoreType.DMA((2,))],
        compiler_params=_params(("arbitrary", "arbitrary")),
    )(x, meta, g)


def _branch_gate_fwd(name, oa, ob, wa, wb, gates, dep=None):
    t = gates.shape[0]
    tm = _pick(t, (544, 384, 256, 128))
    has_dep = dep is not None

    def body(oa_ref, ob_ref, wa_ref, wb_ref, g_ref, *rest):
        o_ref, ya_ref, yb_ref = rest[-3:]
        ya = lax.dot_general(_bf(oa_ref[...]), wa_ref[...], _NN, preferred_element_type=F32)
        yb = lax.dot_general(_bf(ob_ref[...]), wb_ref[...], _NN, preferred_element_type=F32)
        sa = jax.nn.sigmoid(g_ref[:, 0:D_MODEL].astype(F32))
        sb = jax.nn.sigmoid(g_ref[:, D_MODEL:2 * D_MODEL].astype(F32))
        o_ref[...] = (sa * ya + sb * yb).astype(BF16)
        ya_ref[...] = ya.astype(BF16)
        yb_ref[...] = yb.astype(BF16)

    blk = pl.BlockSpec((tm, D_MODEL), lambda i: (i, 0))
    narrow = pl.BlockSpec((tm, A_WIDTH), lambda i: (i, 0))
    wide = pl.BlockSpec((tm, 2 * D_MODEL), lambda i: (i, 0))
    wspec = pl.BlockSpec((A_WIDTH, D_MODEL), lambda i: (0, 0))
    out = jax.ShapeDtypeStruct((t, D_MODEL), BF16)
    return pl.pallas_call(
        body, name=name, grid=(t // tm,),
        in_specs=[narrow, narrow, wspec, wspec, wide] + ([pl.BlockSpec(memory_space=pl.ANY)] if has_dep else []),
        out_specs=[blk, blk, blk], out_shape=[out, out, out], compiler_params=_params(("parallel",)),
    )(*((oa, ob, wa, wb, gates) + ((dep,) if has_dep else ())))


def _branch_bwd_x(name, dya, dyb, wa, wb):
    t = dya.shape[0]
    tm = _pick(t, (1088, 768, 512, 256, 128))

    def body(da_ref, db_ref, wa_ref, wb_ref, oa_ref, ob_ref):
        oa_ref[...] = lax.dot_general(da_ref[...], wa_ref[...], _NT, preferred_element_type=F32)
        ob_ref[...] = lax.dot_general(db_ref[...], wb_ref[...], _NT, preferred_element_type=F32).astype(BF16)

    blk = pl.BlockSpec((tm, D_MODEL), lambda i: (i, 0))
    narrow = pl.BlockSpec((tm, A_WIDTH), lambda i: (i, 0))
    wspec = pl.BlockSpec((A_WIDTH, D_MODEL), lambda i: (0, 0), pipeline_mode=pl.Buffered(1))
    return pl.pallas_call(
        body, name=name, grid=(t // tm,), in_specs=[blk, blk, wspec, wspec], out_specs=[narrow, narrow],
        out_shape=[jax.ShapeDtypeStruct((t, A_WIDTH), F32), jax.ShapeDtypeStruct((t, B_WIDTH), BF16)],
        compiler_params=_params(("parallel",)),
    )(dya, dyb, wa, wb)


def _branch_bwd_w(name, oa, ob, dya, dyb):
    t = oa.shape[0]
    tn = 512

    def body(oa_ref, ob_ref, da_ref, db_ref, wa_ref, wb_ref):
        wa_ref[...] = lax.dot_general(_bf(oa_ref[...]), da_ref[...], _TN, preferred_element_type=F32).astype(BF16)
        wb_ref[...] = lax.dot_general(_bf(ob_ref[...]), db_ref[...], _TN, preferred_element_type=F32).astype(BF16)

    whole = pl.BlockSpec((t, A_WIDTH), lambda j: (0, 0), pipeline_mode=pl.Buffered(1))
    cols = pl.BlockSpec((t, tn), lambda j: (0, j))
    out_spec = pl.BlockSpec((A_WIDTH, tn), lambda j: (0, j))
    out = jax.ShapeDtypeStruct((A_WIDTH, D_MODEL), BF16)
    return pl.pallas_call(
        body, name=name, grid=(D_MODEL // tn,), in_specs=[whole, whole, cols, cols], out_specs=[out_spec, out_spec],
        out_shape=[out, out], compiler_params=_params(("parallel",)),
    )(oa, ob, dya, dyb)


def _mix_out_gate_bwd(name, dh, wo, gates, ya, yb):
    t = gates.shape[0]
    tm = _pick(t, (544, 384, 256, 128))

    def body(dh_ref, w_ref, g_ref, ya_ref, yb_ref, dya_ref, dyb_ref, dg_ref):
        dm = lax.dot_general(_bf(dh_ref[...]), w_ref[...], _NT, preferred_element_type=F32)
        sa = jax.nn.sigmoid(g_ref[:, 0:D_MODEL].astype(F32))
        sb = jax.nn.sigmoid(g_ref[:, D_MODEL:2 * D_MODEL].astype(F32))
        dya_ref[...] = (dm * sa).astype(BF16)
        dyb_ref[...] = (dm * sb).astype(BF16)
        dg_ref[:, 0:D_MODEL] = (dm * ya_ref[...].astype(F32) * (sa * (1.0 - sa))).astype(BF16)
        dg_ref[:, D_MODEL:2 * D_MODEL] = (dm * yb_ref[...].astype(F32) * (sb * (1.0 - sb))).astype(BF16)

    blk = pl.BlockSpec((tm, D_MODEL), lambda i: (i, 0))
    wide = pl.BlockSpec((tm, 2 * D_MODEL), lambda i: (i, 0))
    out = jax.ShapeDtypeStruct((t, D_MODEL), BF16)
    return pl.pallas_call(
        body, name=name, grid=(t // tm,),
        in_specs=[blk, pl.BlockSpec((D_MODEL, D_MODEL), lambda i: (0, 0)), wide, blk, blk], out_specs=[blk, blk, wide],
        out_shape=[out, out, jax.ShapeDtypeStruct((t, 2 * D_MODEL), BF16)], compiler_params=_params(("parallel",)),
    )(dh, wo, gates, ya, yb)


def _ffn_out_loss(name, a, w, res, gf, tgt, alpha):
    t, k = a.shape
    b, s, _ = tgt.shape
    l = t // b
    tm = _pick(t, (544, 384, 256, 128))
    per = l // tm
    assert per * tm == l and tm > PREFIX and l - PREFIX == s

    def body(a_ref, w_ref, r_ref, g_ref, t_ref, dh_ref, dhb_ref, loss_ref, dg_ref, buf, sem):
        i = pl.program_id(0)
        bi, r = i // per, i % per

        def first_rows():
            return pltpu.make_async_copy(t_ref.at[bi, pl.ds(0, tm - PREFIX)], buf.at[pl.ds(PREFIX, tm - PREFIX)], sem)

        def later_rows():
            return pltpu.make_async_copy(t_ref.at[bi, pl.ds(pl.multiple_of(r * tm - PREFIX, 8), tm)], buf, sem)

        @pl.when(r == 0)
        def _():
            buf[0:PREFIX, :] = jnp.zeros((PREFIX, D_MODEL), F32)
            first_rows().start()

        if per > 1:
            @pl.when(r > 0)
            def _():
                later_rows().start()

        acc = lax.dot_general(a_ref[...], w_ref[...], _NN, preferred_element_type=F32)

        @pl.when(r == 0)
        def _():
            first_rows().wait()

        if per > 1:
            @pl.when(r > 0)
            def _():
                later_rows().wait()

        hv = acc * alpha + r_ref[...]
        row = lax.broadcasted_iota(jnp.int32, (tm, 1), 0)
        real = ((r > 0) | (row >= PREFIX)).astype(F32)
        g = g_ref[...]
        rn = lax.rsqrt(jnp.mean(hv * hv, axis=-1, keepdims=True) + EPS)
        xn = hv * rn
        err = (xn * g - buf[...]) * real
        lpart = 0.5 * jnp.sum(jnp.mean(err * err, axis=-1, keepdims=True), axis=0, keepdims=True)
        dy = err * (1.0 / D_MODEL)
        tv = dy * g
        dot = jnp.mean(tv * hv, axis=-1, keepdims=True)
        dh = rn * tv - hv * (rn * rn * rn * dot)
        dh_ref[...] = dh
        dhb_ref[...] = dh.astype(BF16)
        gpart = jnp.sum(dy * xn, axis=0, keepdims=True)

        @pl.when(i == 0)
        def _():
            loss_ref[...] = jnp.zeros_like(loss_ref)
            dg_ref[...] = jnp.zeros_like(dg_ref)

        loss_ref[...] += jnp.broadcast_to(lpart, loss_ref.shape)
        dg_ref[...] += gpart

    row_spec = pl.BlockSpec((tm, D_MODEL), lambda i: (i, 0))
    vec = pl.BlockSpec((1, D_MODEL), lambda i: (0, 0))
    return pl.pallas_call(
        body, name=name, grid=(t // tm,),
        in_specs=[pl.BlockSpec((tm, k), lambda i: (i, 0)),
                  pl.BlockSpec((k, D_MODEL), lambda i: (0, 0), pipeline_mode=pl.Buffered(1)), row_spec, vec,
                  pl.BlockSpec(memory_space=pl.ANY)],
        out_specs=[row_spec, row_spec, pl.BlockSpec((8, 128), lambda i: (0, 0)), vec],
        out_shape=[jax.ShapeDtypeStruct((t, D_MODEL), F32), jax.ShapeDtypeStruct((t, D_MODEL), BF16),
                   jax.ShapeDtypeStruct((8, 128), F32), jax.ShapeDtypeStruct((1, D_MODEL), F32)],
        scratch_shapes=[pltpu.VMEM((tm, D_MODEL), F32), pltpu.SemaphoreType.DMA(())],
        compiler_params=_params(("arbitrary",)),
    )(a, w, res, gf, tgt)


def _fgate_fwd(name, f3, bf_row):
    b, l, _ = f3.shape
    nb = l // BLOCK

    def body(f_ref, b_ref, cc_ref, cr_ref):
        r_i = lax.broadcasted_iota(jnp.int32, (BLOCK, BLOCK), 0)
        c_i = lax.broadcasted_iota(jnp.int32, (BLOCK, BLOCK), 1)
        tri = (r_i >= c_i).astype(F32)
        carry = jnp.zeros((1, 128), F32)
        for blk in range(nb):
            rows = slice(blk * BLOCK, (blk + 1) * BLOCK)
            z = f_ref[rows, :] + b_ref[...]
            lf = jnp.minimum(z, 0.0) - jnp.log(1.0 + jnp.exp(-jnp.abs(z)))
            cb = jnp.dot(tri, lf, preferred_element_type=F32, precision=lax.Precision.HIGHEST) + carry
            carry = cb[BLOCK - 1:BLOCK, :]
            cbt = cb.T
            for hh in range(B_HEADS):
                cc_ref[hh, rows, :] = jnp.sum(jnp.where(c_i == hh, cb, 0.0), axis=1, keepdims=True)
                cr_ref[hh, :, rows] = cbt[hh:hh + 1, :]

    return pl.pallas_call(
        body, name=name, grid=(b,),
        in_specs=[pl.BlockSpec((None, l, 128), lambda bi: (bi, 0, 0)),
                  pl.BlockSpec((1, 128), lambda bi: (0, 0))],
        out_specs=[pl.BlockSpec((None, B_HEADS, l, 1), lambda bi: (bi, 0, 0, 0)),
                   pl.BlockSpec((None, B_HEADS, 1, l), lambda bi: (bi, 0, 0, 0))],
        out_shape=[jax.ShapeDtypeStruct((b, B_HEADS, l, 1), F32), jax.ShapeDtypeStruct((b, B_HEADS, 1, l), F32)],
        compiler_params=_params(("parallel",)),
    )(f3, bf_row)


def _fgate_bwd(name, f3, bf_row, dcq, dck):
    b, l, _ = f3.shape
    nb = l // BLOCK

    def body(f_ref, b_ref, dcq_ref, dck_ref, df_ref, db_ref):
        r_i = lax.broadcasted_iota(jnp.int32, (BLOCK, BLOCK), 0)
        c_i = lax.broadcasted_iota(jnp.int32, (BLOCK, BLOCK), 1)
        tri = (r_i <= c_i).astype(F32)
        carry = jnp.zeros((1, 128), F32)
        total = jnp.zeros((1, 128), F32)
        for blk in range(nb - 1, -1, -1):
            rows = slice(blk * BLOCK, (blk + 1) * BLOCK)
            krows = jnp.concatenate([dck_ref[hh, :, rows] for hh in range(B_HEADS)]
                                    + [jnp.zeros((BLOCK - B_HEADS, BLOCK), F32)], axis=0)
            dcb = krows.T
            for hh in range(B_HEADS):
                dcb = dcb + jnp.where(c_i == hh, dcq_ref[hh, rows, :], 0.0)
            rc = jnp.dot(tri, dcb, preferred_element_type=F32, precision=lax.Precision.HIGHEST) + carry
            carry = rc[0:1, :]
            z = f_ref[rows, :] + b_ref[...]
            df = rc * (1.0 / (1.0 + jnp.exp(z)))
            df_ref[rows, :] = df.astype(BF16)
            total = total + jnp.sum(df, axis=0, keepdims=True)

        @pl.when(pl.program_id(0) == 0)
        def _():
            db_ref[...] = total

        @pl.when(pl.program_id(0) > 0)
        def _():
            db_ref[...] += total

    return pl.pallas_call(
        body, name=name, grid=(b,),
        in_specs=[pl.BlockSpec((None, l, 128), lambda bi: (bi, 0, 0)),
                  pl.BlockSpec((1, 128), lambda bi: (0, 0)),
                  pl.BlockSpec((None, B_HEADS, l, 1), lambda bi: (bi, 0, 0, 0)),
                  pl.BlockSpec((None, B_HEADS, 1, l), lambda bi: (bi, 0, 0, 0))],
        out_specs=[pl.BlockSpec((None, l, 128), lambda bi: (bi, 0, 0)), pl.BlockSpec((1, 128), lambda bi: (0, 0))],
        out_shape=[jax.ShapeDtypeStruct((b, l, 128), BF16), jax.ShapeDtypeStruct((1, 128), F32)],
        compiler_params=_params(("arbitrary",)),
    )(f3, bf_row, dcq, dck)


A_Q_BLK = B_SEG // A_WIDTH
A_K_BLK = (B_SEG + A_WIDTH) // 128
A_V_BLK = A_K_BLK + 1
A_SEG_BLK = B_SEG // A_SEG
STACK = A_HEADS * BLOCK


def _lane_lo():
    return lax.broadcasted_iota(jnp.int32, (1, 128), 1) < HEAD_DIM


def _stack_heads(x, masked):
    lo = _lane_lo()
    blks = [x[:, 128 * j:128 * (j + 1)] for j in range(4)]
    if not masked:
        return jnp.concatenate(blks + blks, axis=0)
    zero = jnp.zeros_like(blks[0])
    return jnp.concatenate([jnp.where(lo, bk, zero) for bk in blks] + [jnp.where(lo, zero, bk) for bk in blks], axis=0)


def _unstack_heads(y):
    lo = _lane_lo()
    return jnp.concatenate([jnp.where(lo, y[128 * j:128 * (j + 1)], y[128 * (4 + j):128 * (5 + j)]) for j in range(4)], axis=1)


def _swa_bias(slopes):
    slopes = np.asarray(slopes, np.float32)
    r_i = np.arange(STACK)[:, None]
    c_i = np.arange(3 * BLOCK)[None, :]
    seg = c_i >> 7
    out = []
    for n in range(3):
        qpos = n * BLOCK + (r_i & (BLOCK - 1))
        kpos = np.where(seg == 0, c_i, (n - 2) * BLOCK + c_i)
        dist = qpos - kpos
        band = (seg != 0) & (dist < BLOCK) & (kpos >= PREFIX)
        meta = (seg == 0) & (c_i >= N_PAD)
        out.append(np.where((dist >= 0) & (band | meta), -slopes * dist.astype(np.float32), np.float32(NEG)))
    return jnp.asarray(np.stack(out, axis=0), F32)


def _swa_scores(q, kcat, n, slope, bias):
    s = lax.dot_general(q, kcat, _NT, preferred_element_type=F32) + bias
    further = slope * (-BLOCK * jnp.maximum(n - 2, 0)).astype(F32)
    return jnp.concatenate([s[:, 0:BLOCK] + further, s[:, BLOCK:]], axis=1)


def _swa_specs():
    def kv(col_blk):
        return [pl.BlockSpec((None, BLOCK, 128), lambda b, n: (b, 0, col_blk)),
                pl.BlockSpec((None, BLOCK, 128), lambda b, n: (b, jnp.maximum(n - 1, 0), col_blk)),
                pl.BlockSpec((None, BLOCK, 128), lambda b, n: (b, n, col_blk))]

    q_spec = pl.BlockSpec((None, BLOCK, A_WIDTH), lambda b, n: (b, n, A_Q_BLK))
    o_spec = pl.BlockSpec((None, BLOCK, A_WIDTH), lambda b, n: (b, n, 0))
    col = pl.BlockSpec((STACK, 1), lambda b, n: (0, 0))
    bias = pl.BlockSpec((None, STACK, 3 * BLOCK), lambda b, n: (jnp.minimum(n, 2), 0, 0))
    lse_spec = pl.BlockSpec((None, A_HEADS, BLOCK, 1), lambda b, n: (b, 0, n, 0))
    return q_spec, kv(A_K_BLK), kv(A_V_BLK), o_spec, [col, col, bias], lse_spec


def _swa_fwd(name, qkv, slopes, sinks, bias):
    b, l, _ = qkv.shape
    nb = l // BLOCK

    def body(q_ref, k0_ref, kp_ref, kc_ref, v0_ref, vp_ref, vc_ref, sl_ref, sk_ref, bias_ref, o_ref, o16_ref, lse_ref):
        n = pl.program_id(1)
        qs = _stack_heads(q_ref[...], True) * SCALE
        kcat = jnp.concatenate([k0_ref[...], kp_ref[...], kc_ref[...]], axis=0)
        vcat = jnp.concatenate([v0_ref[...], vp_ref[...], vc_ref[...]], axis=0)
        s = _swa_scores(qs, kcat, n, sl_ref[...], bias_ref[...])
        sink = sk_ref[...]
        m = jnp.maximum(jnp.max(s, axis=-1, keepdims=True), sink)
        p = jnp.exp(s - m)
        den = jnp.sum(p, axis=-1, keepdims=True) + jnp.exp(sink - m)
        o = lax.dot_general(p.astype(BF16), vcat, _NN, preferred_element_type=F32) / den
        ov = _unstack_heads(o)
        o_ref[...] = ov
        o16_ref[...] = ov.astype(BF16)
        lse_ref[...] = (m + jnp.log(den)).reshape(A_HEADS, BLOCK, 1)

    q_spec, k_specs, v_specs, o_spec, consts, lse_spec = _swa_specs()
    return pl.pallas_call(
        body, name=name, grid=(b, nb),
        in_specs=[q_spec] + k_specs + v_specs + consts, out_specs=[o_spec, o_spec, lse_spec],
        out_shape=[jax.ShapeDtypeStruct((b, l, A_WIDTH), F32), jax.ShapeDtypeStruct((b, l, A_WIDTH), BF16),
                   jax.ShapeDtypeStruct((b, A_HEADS, l, 1), F32)],
        compiler_params=_params(("parallel", "parallel")),
    )(qkv, qkv, qkv, qkv, qkv, qkv, qkv, slopes, sinks, bias)


def _swa_bwd(name, qkv, o, lse, do, slopes, sinks, bias, dqkv):
    b, l, _ = qkv.shape
    nb = l // BLOCK

    def body(q_ref, k0_ref, kp_ref, kc_ref, v0_ref, vp_ref, vc_ref, o_ref, lse_ref, do_ref, sl_ref, sk_ref, bias_ref, _,
             dx_ref, ds_ref, dk_acc, dv_acc):
        bi = pl.program_id(0)
        n = pl.program_id(1)
        qs = _stack_heads(q_ref[...], True) * SCALE
        dos32 = _stack_heads(do_ref[...], True)
        dos = dos32.astype(BF16)
        os_ = _stack_heads(o_ref[...], False)
        lsev = lse_ref[...].reshape(STACK, 1)
        kcat = jnp.concatenate([k0_ref[...], kp_ref[...], kc_ref[...]], axis=0)
        vcat = jnp.concatenate([v0_ref[...], vp_ref[...], vc_ref[...]], axis=0)
        s = _swa_scores(qs, kcat, n, sl_ref[...], bias_ref[...])
        p = jnp.exp(s - lsev)
        dsum = jnp.sum(dos32 * os_, axis=-1, keepdims=True)
        dp = lax.dot_general(dos, vcat, _NT, preferred_element_type=F32)
        dsc = (p * (dp - dsum)).astype(BF16)
        dq = lax.dot_general(dsc, kcat, _NN, preferred_element_type=F32) * SCALE
        row0 = pl.multiple_of(n * BLOCK, BLOCK)
        dx_ref[pl.ds(row0, BLOCK), 0:A_WIDTH] = _unstack_heads(dq).astype(BF16)
        dkc = lax.dot_general(dsc, qs, _TN, preferred_element_type=F32)
        dvc = lax.dot_general(p.astype(BF16), dos, _TN, preferred_element_type=F32)

        @pl.when(n == 0)
        def _():
            dk_acc[...] = jnp.zeros_like(dk_acc)
            dv_acc[...] = jnp.zeros_like(dv_acc)

        starts = (0, pl.multiple_of(jnp.maximum(n - 1, 0) * BLOCK, BLOCK), row0)
        for t, st in enumerate(starts):
            dk_acc[pl.ds(st, BLOCK), :] += dkc[t * BLOCK:(t + 1) * BLOCK, :]
            dv_acc[pl.ds(st, BLOCK), :] += dvc[t * BLOCK:(t + 1) * BLOCK, :]

        @pl.when(n == nb - 1)
        def _():
            dx_ref[:, A_WIDTH:A_WIDTH + 128] = dk_acc[...].astype(BF16)
            dx_ref[:, A_WIDTH + 128:A_SEG] = dv_acc[...].astype(BF16)

        dsink = -(jnp.exp(sk_ref[...] - lsev) * dsum)
        r8 = lax.broadcasted_iota(jnp.int32, (8, 128), 0)
        acc = jnp.zeros((8, 128), F32)
        for hh in range(A_HEADS):
            acc = acc + jnp.where(r8 == hh, jnp.sum(dsink[hh * BLOCK:(hh + 1) * BLOCK, :]), 0.0)

        @pl.when((bi == 0) & (n == 0))
        def _():
            ds_ref[...] = jnp.zeros_like(ds_ref)

        ds_ref[...] += acc

    q_spec, k_specs, v_specs, o_spec, consts, lse_spec = _swa_specs()
    return pl.pallas_call(
        body, name=name, grid=(b, nb),
        in_specs=[q_spec] + k_specs + v_specs + [o_spec, lse_spec, o_spec] + consts + [pl.BlockSpec(memory_space=pl.ANY)],
        out_specs=[pl.BlockSpec((None, l, A_SEG), lambda bb, n: (bb, 0, A_SEG_BLK)),
                   pl.BlockSpec((8, 128), lambda bb, n: (0, 0))],
        out_shape=[jax.ShapeDtypeStruct(dqkv.shape, BF16), jax.ShapeDtypeStruct((8, 128), F32)],
        scratch_shapes=[pltpu.VMEM((l, 128), F32), pltpu.VMEM((l, 128), F32)],
        input_output_aliases={13: 0},
        compiler_params=_params(("arbitrary", "arbitrary")),
    )(qkv, qkv, qkv, qkv, qkv, qkv, qkv, o, lse, do, slopes, sinks, bias, dqkv)


def _fox_mask(qk, ck, i):
    kh = qk.shape[1]
    qpos = i * BLOCK + lax.broadcasted_iota(jnp.int32, (BLOCK, kh), 0)
    kpos = lax.broadcasted_iota(jnp.int32, (BLOCK, kh), 1)
    return jnp.where((kpos <= qpos) & (kpos >= N_PAD), qk - ck, NEG)


def _pick_head(x, hh):
    lo = _lane_lo()
    return jnp.where(lo if hh == 0 else jnp.logical_not(lo), x, jnp.zeros_like(x))


def _both_heads(x):
    return jnp.concatenate([_pick_head(x, 0), _pick_head(x, 1)], axis=0)


def _fox_specs(l):
    pair = pl.BlockSpec((None, l, PAIR_W), lambda bi, hp: (bi, 0, hp))
    half = pl.BlockSpec((None, l, 128), lambda bi, hp: (bi, 0, hp))
    colv = pl.BlockSpec((None, 2, l, 1), lambda bi, hp: (bi, hp, 0, 0))
    rowv = pl.BlockSpec((None, 2, 1, l), lambda bi, hp: (bi, hp, 0, 0))
    return pair, half, colv, rowv


def _fox_fwd(name, qkv, c_col, c_row):
    b, l, _ = qkv.shape
    nb = l // BLOCK

    def body(x_ref, cc_ref, cr_ref, o_ref, lse_ref):
        for i in range(nb):
            rows = slice(i * BLOCK, (i + 1) * BLOCK)
            kh = (i + 1) * BLOCK
            qblk = x_ref[rows, 0:128]
            kv = x_ref[0:kh, 128:256]
            vv = x_ref[0:kh, 256:384]
            qk = lax.dot_general(_both_heads(qblk) * SCALE, kv, _NT, preferred_element_type=F32)
            ps, dens = [], []
            for hh in range(2):
                s = _fox_mask(qk[hh * BLOCK:(hh + 1) * BLOCK], cr_ref[hh, :, 0:kh], i)
                m = jnp.max(s, axis=-1, keepdims=True)
                p = jnp.exp(s - m)
                den = jnp.sum(p, axis=-1, keepdims=True)
                ps.append(p.astype(BF16))
                dens.append(den)
                lse_ref[hh, rows, :] = (m + jnp.log(den)) + cc_ref[hh, rows, :]
            pv = lax.dot_general(jnp.concatenate(ps, axis=0), vv, _NN, preferred_element_type=F32)
            o_ref[rows, :] = jnp.where(_lane_lo(), pv[0:BLOCK] / dens[0], pv[BLOCK:2 * BLOCK] / dens[1]).astype(BF16)

    pair, half, colv, rowv = _fox_specs(l)
    return pl.pallas_call(
        body, name=name, grid=(b, 4), in_specs=[pair, colv, rowv], out_specs=[half, colv],
        out_shape=[jax.ShapeDtypeStruct((b, l, B_WIDTH), BF16), jax.ShapeDtypeStruct((b, B_HEADS, l, 1), F32)],
        compiler_params=_params(("parallel", "parallel")),
    )(qkv, c_col, c_row)


def _fox_bwd(name, qkv, c_col, c_row, o, lse, do):
    b, l, _ = qkv.shape
    nb = l // BLOCK

    def body(x_ref, cc_ref, cr_ref, o_ref, lse_ref, do_ref, dx_ref, dcq_ref, dck_ref, dk_acc, dv_acc):
        dk_acc[...] = jnp.zeros_like(dk_acc)
        dv_acc[...] = jnp.zeros_like(dv_acc)
        dck_ref[...] = jnp.zeros_like(dck_ref)
        for i in range(nb):
            rows = slice(i * BLOCK, (i + 1) * BLOCK)
            kh = (i + 1) * BLOCK
            qblk = x_ref[rows, 0:128]
            kv = x_ref[0:kh, 128:256]
            vv = x_ref[0:kh, 256:384]
            doblk = do_ref[rows, :]
            ov = o_ref[rows, :].astype(F32)
            q2 = _both_heads(qblk) * SCALE
            do2 = _both_heads(doblk)
            qk = lax.dot_general(q2, kv, _NT, preferred_element_type=F32)
            dp = lax.dot_general(do2, vv, _NT, preferred_element_type=F32)
            ps, dss = [], []
            for hh in range(2):
                half = slice(hh * BLOCK, (hh + 1) * BLOCK)
                s = _fox_mask(qk[half], cr_ref[hh, :, 0:kh], i)
                p = jnp.exp(s - (lse_ref[hh, rows, :] - cc_ref[hh, rows, :]))
                dsum = jnp.sum(do2[half].astype(F32) * ov, axis=-1, keepdims=True)
                ds = p * (dp[half] - dsum)
                ps.append(p.astype(BF16))
                dss.append(ds.astype(BF16))
                dcq_ref[hh, rows, :] = jnp.sum(ds, axis=-1, keepdims=True)
                dck_ref[hh, :, 0:kh] -= jnp.sum(ds, axis=0, keepdims=True)
            p2 = jnp.concatenate(ps, axis=0)
            ds2 = jnp.concatenate(dss, axis=0)
            dq = lax.dot_general(ds2, kv, _NN, preferred_element_type=F32) * SCALE
            dk_acc[0:kh, :] += lax.dot_general(ds2, q2, _TN, preferred_element_type=F32)
            dv_acc[0:kh, :] += lax.dot_general(p2, do2, _TN, preferred_element_type=F32)
            dx_ref[rows, 0:128] = jnp.where(_lane_lo(), dq[0:BLOCK], dq[BLOCK:2 * BLOCK]).astype(BF16)
        dx_ref[:, 128:256] = dk_acc[...].astype(BF16)
        dx_ref[:, 256:384] = dv_acc[...].astype(BF16)

    pair, half, colv, rowv = _fox_specs(l)
    return pl.pallas_call(
        body, name=name, grid=(b, 4), in_specs=[pair, colv, rowv, half, colv, half],
        out_specs=[pair, colv, rowv],
        out_shape=[jax.ShapeDtypeStruct(qkv.shape, BF16), jax.ShapeDtypeStruct((b, B_HEADS, l, 1), F32),
                   jax.ShapeDtypeStruct((b, B_HEADS, 1, l), F32)],
        scratch_shapes=[pltpu.VMEM((l, 128), F32), pltpu.VMEM((l, 128), F32)],
        compiler_params=_params(("parallel", "parallel")),
    )(qkv, c_col, c_row, o, lse, do)


_FLIPS = ((0, 0, 1), (0, 1, 0), (0, 1, 1), (1, 0, 0), (1, 0, 1), (1, 1, 0), (1, 1, 1))


def _peer_table():
    x, y, c = lax.axis_index("x"), lax.axis_index("y"), lax.axis_index("c")
    me = 4 * x + 2 * y + c
    peers = []
    for fx, fy, fc in _FLIPS:
        px = 1 - x if fx else x
        py = 1 - y if fy else y
        pc = 1 - c if fc else c
        peers.append(((px, py, pc), 4 * px + 2 * py + pc))
    return me, peers


_HBM = pl.BlockSpec(memory_space=pltpu.HBM)
_SEM = pl.BlockSpec(memory_space=pltpu.SEMAPHORE)
_ANY = pl.BlockSpec(memory_space=pl.ANY)
_EFFECT = pltpu.SideEffectType.DATAFLOW_SIDE_EFFECTING


def _split_copy(srcs_are_pieces, src_refs, land_refs, send_sem, recv_sem, a, kk, me, peers, arriving):
    dev, lin = peers[kk]
    npeer = len(_FLIPS)
    src = src_refs[a] if srcs_are_pieces[a] else src_refs[a].at[lin]
    dst = land_refs[a].at[lin] if arriving else land_refs[a].at[me]
    return pltpu.make_async_remote_copy(src_ref=src, dst_ref=dst, send_sem=send_sem.at[a * npeer + kk],
                                        recv_sem=recv_sem.at[a * npeer + kk], device_id=dev, device_id_type=MESH_ID)


def _xchg_start(name, gather, scatter, after=None):
    me_out = 4 * lax.axis_index("x") + 2 * lax.axis_index("y") + lax.axis_index("c")
    srcs = list(gather) + list(scatter)
    is_piece = [True] * len(gather) + [False] * len(scatter)
    lands = []
    for a, piece in zip(srcs, is_piece):
        if piece:
            lands.append(lax.dynamic_update_slice(lax.empty((N_DEV,) + tuple(a.shape), a.dtype), a[None],
                                                  (me_out,) + (0,) * a.ndim))
        else:
            lands.append(lax.empty(tuple(a.shape), a.dtype))
    n = len(srcs)
    nsem = n * len(_FLIPS)
    has_after = after is not None

    def body(*refs):
        src_refs = refs[:n]
        land_refs = refs[n:2 * n]
        outs = refs[2 * n + (1 if has_after else 0):]
        send_sem, recv_sem = outs[0], outs[1]
        token = outs[-1]
        me, peers = _peer_table()
        for kk in range(len(_FLIPS)):
            for a in range(n):
                _split_copy(is_piece, src_refs, land_refs, send_sem, recv_sem, a, kk, me, peers, False).start()
        token[...] = jnp.zeros_like(token)

    out_shape = ([pltpu.SemaphoreType.DMA((nsem,)), pltpu.SemaphoreType.DMA((nsem,))]
                 + [pltpu.HBM(tuple(a.shape), a.dtype) for a in srcs] + [pltpu.HBM(tuple(a.shape), a.dtype) for a in lands]
                 + [jax.ShapeDtypeStruct((8, 128), F32)])
    args = [pltpu.with_memory_space_constraint(a, pltpu.HBM) for a in srcs + lands] + ([after] if has_after else [])
    res = pl.pallas_call(
        body, name=name, out_shape=out_shape,
        in_specs=[_HBM] * (2 * n) + ([_ANY] if has_after else []),
        out_specs=[_SEM, _SEM] + [_HBM] * (2 * n) + [pl.BlockSpec(memory_space=pltpu.VMEM)],
        input_output_aliases={i: 2 + i for i in range(2 * n)},
        compiler_params=pltpu.CompilerParams(has_side_effects=_EFFECT),
    )(*args)
    state = (res[0], res[1], list(res[2:2 + n]), list(res[2 + n:2 + 2 * n]), is_piece)
    return state, res[-1]


def _xchg_wait(name, state, after):
    send_sem, recv_sem, srcs, lands, is_piece = state
    n = len(srcs)

    def body(*refs):
        src_refs = refs[:n]
        land_refs = refs[n:2 * n]
        s_sem, r_sem = refs[2 * n], refs[2 * n + 1]
        me, peers = _peer_table()
        for kk in range(len(_FLIPS)):
            for a in range(n):
                cp = _split_copy(is_piece, src_refs, land_refs, s_sem, r_sem, a, kk, me, peers, True)
                cp.wait_send()
                cp.wait_recv()

    out_shape = [pltpu.HBM(tuple(a.shape), a.dtype) for a in srcs] + [pltpu.HBM(tuple(a.shape), a.dtype) for a in lands]
    res = pl.pallas_call(
        body, name=name, out_shape=out_shape,
        in_specs=[_HBM] * (2 * n) + [_SEM, _SEM, _ANY], out_specs=[_HBM] * (2 * n),
        input_output_aliases={i: i for i in range(2 * n)},
        compiler_params=pltpu.CompilerParams(has_side_effects=_EFFECT),
    )(*srcs, *lands, send_sem, recv_sem, after)
    return list(res[n:]), list(res[:n])


_SIB = (0, 0, 1)
_ICI = ((0, 1, 0), (1, 0, 0), (1, 1, 0))


def _flip(fl):
    x, y, c = lax.axis_index("x"), lax.axis_index("y"), lax.axis_index("c")
    px = 1 - x if fl[0] else x
    py = 1 - y if fl[1] else y
    pc = 1 - c if fl[2] else c
    return (px, py, pc), 4 * px + 2 * py + pc


def _gather2_start(name, pieces, after=None):
    me_out = 4 * lax.axis_index("x") + 2 * lax.axis_index("y") + lax.axis_index("c")
    pieces = list(pieces)
    n = len(pieces)
    lands = [lax.dynamic_update_slice(lax.empty((N_DEV,) + tuple(a.shape), a.dtype), a[None],
                                      (me_out,) + (0,) * a.ndim) for a in pieces]
    first = (_SIB,) + _ICI
    has_after = after is not None

    def body(*refs):
        src_refs, land_refs = refs[:n], refs[n:2 * n]
        outs = refs[2 * n + (1 if has_after else 0):]
        send_sem, recv_sem, token = outs[0], outs[1], outs[-1]
        _, me = _flip((0, 0, 0))
        for kk, fl in enumerate(first):
            dev, _ = _flip(fl)
            for a in range(n):
                pltpu.make_async_remote_copy(src_ref=src_refs[a], dst_ref=land_refs[a].at[me],
                                             send_sem=send_sem.at[a * 4 + kk], recv_sem=recv_sem.at[a * 4 + kk],
                                             device_id=dev, device_id_type=MESH_ID).start()
        token[...] = jnp.zeros_like(token)

    hbm = [pltpu.HBM(tuple(a.shape), a.dtype) for a in pieces + lands]
    res = pl.pallas_call(
        body, name=name,
        out_shape=[pltpu.SemaphoreType.DMA((4 * n,)), pltpu.SemaphoreType.DMA((4 * n,))] + hbm
        + [jax.ShapeDtypeStruct((8, 128), F32)],
        in_specs=[_HBM] * (2 * n) + ([_ANY] if has_after else []),
        out_specs=[_SEM, _SEM] + [_HBM] * (2 * n) + [pl.BlockSpec(memory_space=pltpu.VMEM)],
        input_output_aliases={i: 2 + i for i in range(2 * n)},
        compiler_params=pltpu.CompilerParams(has_side_effects=_EFFECT),
    )(*([pltpu.with_memory_space_constraint(a, pltpu.HBM) for a in pieces + lands] + ([after] if has_after else [])))
    return (res[0], res[1], list(res[2:2 + n]), list(res[2 + n:2 + 2 * n])), res[-1]


def _gather2_forward(name, state, after):
    send_a, recv_a, pieces, lands = state
    n = len(pieces)
    first = (_SIB,) + _ICI

    def body(*refs):
        src_refs, land_refs = refs[:n], refs[n:2 * n]
        s_a, r_a = refs[2 * n], refs[2 * n + 1]
        outs = refs[2 * n + 3:]
        send_b, recv_b, token = outs[0], outs[1], outs[-1]
        for kk, fl in enumerate(first):
            dev, lin = _flip(fl)
            for a in range(n):
                cp = pltpu.make_async_remote_copy(src_ref=src_refs[a], dst_ref=land_refs[a].at[lin],
                                                  send_sem=s_a.at[a * 4 + kk], recv_sem=r_a.at[a * 4 + kk],
                                                  device_id=dev, device_id_type=MESH_ID)
                cp.wait_send()
                cp.wait_recv()
        sib, _ = _flip(_SIB)
        for j, fl in enumerate(_ICI):
            _, lin = _flip(fl)
            for a in range(n):
                pltpu.make_async_remote_copy(src_ref=land_refs[a].at[lin], dst_ref=land_refs[a].at[lin],
                                             send_sem=send_b.at[a * 3 + j], recv_sem=recv_b.at[a * 3 + j],
                                             device_id=sib, device_id_type=MESH_ID).start()
        token[...] = jnp.zeros_like(token)

    hbm = [pltpu.HBM(tuple(a.shape), a.dtype) for a in pieces + lands]
    res = pl.pallas_call(
        body, name=name,
        out_shape=[pltpu.SemaphoreType.DMA((3 * n,)), pltpu.SemaphoreType.DMA((3 * n,))] + hbm
        + [jax.ShapeDtypeStruct((8, 128), F32)],
        in_specs=[_HBM] * (2 * n) + [_SEM, _SEM, _ANY],
        out_specs=[_SEM, _SEM] + [_HBM] * (2 * n) + [pl.BlockSpec(memory_space=pltpu.VMEM)],
        input_output_aliases={i: 2 + i for i in range(2 * n)},
        compiler_params=pltpu.CompilerParams(has_side_effects=_EFFECT),
    )(*pieces, *lands, send_a, recv_a, after)
    return (res[0], res[1], list(res[2 + n:2 + 2 * n])), res[-1]


def _gather2_wait(name, state, after):
    send_b, recv_b, lands = state
    n = len(lands)

    def body(*refs):
        land_refs = refs[:n]
        s_b, r_b = refs[n], refs[n + 1]
        sib, _ = _flip(_SIB)
        for j, fl in enumerate(_ICI):
            _, sent = _flip(fl)
            _, arriving = _flip((fl[0], fl[1], 1))
            for a in range(n):
                cp = pltpu.make_async_remote_copy(src_ref=land_refs[a].at[sent], dst_ref=land_refs[a].at[arriving],
                                                  send_sem=s_b.at[a * 3 + j], recv_sem=r_b.at[a * 3 + j],
                                                  device_id=sib, device_id_type=MESH_ID)
                cp.wait_send()
                cp.wait_recv()

    res = pl.pallas_call(
        body, name=name, out_shape=[pltpu.HBM(tuple(a.shape), a.dtype) for a in lands],
        in_specs=[_HBM] * n + [_SEM, _SEM, _ANY], out_specs=[_HBM] * n,
        input_output_aliases={i: i for i in range(n)},
        compiler_params=pltpu.CompilerParams(has_side_effects=_EFFECT),
    )(*lands, send_b, recv_b, after)
    return list(res)


def _adam_math(w, g, m, v):
    m = ADAM_B1 * m + (1.0 - ADAM_B1) * g
    v = ADAM_B2 * v + (1.0 - ADAM_B2) * (g * g)
    m_hat = m / (1.0 - ADAM_B1 ** ADAM_STEP)
    v_hat = v / (1.0 - ADAM_B2 ** ADAM_STEP)
    delta = -ADAM_LR * (m_hat / (jnp.sqrt(v_hat) + ADAM_EPS) + ADAM_WD * w)
    return delta, m, v


def _adam(name, w, m, v, parts, transposed=False, dep=None, own=None):
    npart, _, cp = parts.shape
    has_dep = dep is not None
    has_own = own is not None
    if transposed:
        c, r = w.shape
        tr = _pick(r, (256, 128))
        blk = pl.BlockSpec((c, tr), lambda i, *_: (0, i))
    else:
        r, c = w.shape
        tr = _pick(r, (256, 176, 128, 64, 16, 8, 1))
        blk = pl.BlockSpec((tr, c), lambda i, *_: (i, 0))

    def body(*refs):
        refs = list(refs)
        me = refs.pop(0)[0] if has_own else None
        w_ref, m_ref, v_ref, p_ref = refs[:4]
        own_ref = refs[4] if has_own else None
        g_ref, d_ref, mo_ref, vo_ref = refs[-4:]
        g = None
        for pp in range(npart):
            part = p_ref[pp] if not has_own else jnp.where(me == pp, own_ref[...], p_ref[pp])
            g = part.astype(F32) if g is None else g + part.astype(F32)
        g = g.T[0:c, :] if transposed else g[:, 0:c]
        delta, mn, vn = _adam_math(w_ref[...], g, m_ref[...], v_ref[...])
        g_ref[...] = g
        d_ref[...] = delta
        mo_ref[...] = mn
        vo_ref[...] = vn

    in_specs = [blk, blk, blk, pl.BlockSpec((npart, tr, cp), lambda i, *_: (0, i, 0))]
    args = [w, m, v, parts]
    if has_own:
        in_specs.append(pl.BlockSpec((None, tr, cp), lambda i, me_ref: (me_ref[0], i, 0)))
        args.append(own[0])
    if has_dep:
        in_specs.append(pl.BlockSpec(memory_space=pl.ANY))
        args.append(dep)
    out = jax.ShapeDtypeStruct(w.shape, F32)
    grid_spec = pltpu.PrefetchScalarGridSpec(num_scalar_prefetch=1 if has_own else 0, grid=(r // tr,),
                                             in_specs=in_specs, out_specs=[blk, blk, blk, blk])
    return pl.pallas_call(
        body, name=name, grid_spec=grid_spec, out_shape=[out, out, out, out], compiler_params=_params(("parallel",)),
    )(*(([own[1]] if has_own else []) + args))


def _small_update(name, packs, wmv):
    nparam = len(wmv)

    def body(p_ref, *refs):
        ins, outs = refs[:3 * nparam], refs[3 * nparam:]
        tot = p_ref[0]
        for pp in range(1, N_DEV):
            tot = tot + p_ref[pp]
        outs[0][0:8, :] = tot[0:8, :]
        outs[0][8:24, :] = tot[8:24, :] + tot[24:40, :]
        grads = [tot[i:i + 1, :] for i in range(4)] + [tot[4:5, 0:B_HEADS], tot[4:5, B_HEADS:2 * B_HEADS]]
        for i, g in enumerate(grads):
            w_ref, m_ref, v_ref = ins[3 * i:3 * i + 3]
            delta, mn, vn = _adam_math(w_ref[...], g, m_ref[...], v_ref[...])
            for o_ref, val in zip(outs[1 + 4 * i:5 + 4 * i], (g, delta, mn, vn)):
                o_ref[...] = val

    flat = [a for trio in wmv for a in trio]
    out_shape = [jax.ShapeDtypeStruct((24, D_MODEL), F32)]
    for w, _, _ in wmv:
        out_shape += [jax.ShapeDtypeStruct(w.shape, F32)] * 4
    res = pl.pallas_call(body, name=name, out_shape=out_shape, compiler_params=_params())(packs, *flat)
    return res[0], [tuple(res[1 + 4 * i:5 + 4 * i]) for i in range(nparam)]


def _local_step(x, tgt, g1, gm, g2, gf, b_forget, sinks, weights, send):
    b, s, _ = x.shape
    l = s + PREFIX
    t = b * l
    (meta,) = weights("meta", x)

    h0, n1 = _embed_norm("embed_rms1_fwd", x, meta, g1)
    (w1i,) = weights("ffn1_in", n1)
    gu1, a1 = _ffn_in_fwd("ffn1_in_fwd", n1, w1i)
    (w1o,) = weights("ffn1_out", weights("ffn1_out:forward", a1))
    h1, um = _mm_res_norm("ffn1_out_fwd", a1, w1o, h0, gm, alpha=0.5)
    wi, wa, wb, wo = weights("mix", weights("mix:forward", um))
    qkv, gates, f2 = _proj_fwd("proj_fwd", um, wi)
    qkv3 = qkv.reshape(b, l, QKV_W)
    f3 = f2.reshape(b, l, 128)
    bf_row = jnp.pad(b_forget, ((0, 0), (0, 128 - B_HEADS)))
    c_col, c_row = _fgate_fwd("fgate_fwd", f3, bf_row)
    head_of_row = np.arange(STACK) // BLOCK
    slopes_np = np.exp2(-8.0 * (head_of_row + 1) / A_HEADS).astype(np.float32).reshape(STACK, 1)
    slopes = jnp.asarray(slopes_np)
    sink_rows = jnp.repeat(sinks.reshape(A_HEADS), BLOCK).reshape(STACK, 1)
    swa_bias = _swa_bias(slopes_np)
    oa3, oa3_bf, lse_a = _swa_fwd("swa_fwd", qkv3, slopes, sink_rows, swa_bias)
    ob3, lse_b = _fox_fwd("fox_fwd", qkv3, c_col, c_row)
    oa = oa3_bf.reshape(t, A_WIDTH)
    ob = ob3.reshape(t, B_WIDTH)
    mixed, ya, yb = _branch_gate_fwd("branch_gate_fwd", oa, ob, wa, wb, gates, dep=weights("ffn2:forward", ob))
    h2, n2 = _mm_res_norm("mix_out_fwd", mixed, wo, h1, g2)
    w2i, w2o = weights("ffn2", h2)
    gu2, a2 = _ffn_in_fwd("ffn2_in_fwd", n2, w2i)

    dh3, dh3_bf, loss_blk, dgf = _ffn_out_loss("ffn2_out_loss", a2, w2o, h2, gf, tgt, 0.5)

    def ffn_bwd(tag, dh, dh_bf, h_in, g_norm, n_in, gu, a, w_in_blk, w_out, one_send, examples=None):
        dw_out = _ffn_out_bwd_w(tag + "_out_bwd_w", a, dh_bf)
        dgu = _ffn_out_bwd_x(tag + "_out_bwd_x", dh_bf, w_out, gu, dep=None if one_send else send(tag + "_out", (dw_out,)))
        dw_in = _ffn_in_bwd_w(tag + "_in_bwd_w", n_in, dgu)
        token = send(tag, (dw_in, dw_out)) if one_send else send(tag + "_in", (dw_in,))
        return _ffn_in_bwd_x(tag + "_in_bwd_x", dgu, w_in_blk, h_in, g_norm, dh, dep=token, examples=examples)

    dh2, dh2_bf, dg2 = ffn_bwd("ffn2", dh3, dh3_bf, h2, g2, n2, gu2, a2, w2i, w2o, True)

    dwo = _mm_tn("mix_out_bwd_w", mixed, dh2_bf)
    dya, dyb, dgates = _mix_out_gate_bwd("mix_out_gate_bwd", dh2_bf, wo, gates, ya, yb)
    doa, dob = _branch_bwd_x("branch_bwd_x", dya, dyb, wa, wb)
    dwa, dwb = _branch_bwd_w("branch_bwd_w", oa, ob, dya, dyb)
    dqkv3, dcq, dck = _fox_bwd("fox_bwd", qkv3, c_col, c_row, ob3, lse_b, dob.reshape(b, l, B_WIDTH))
    dqkv3, dsink = _swa_bwd("swa_bwd", qkv3, oa3, lse_a, doa.reshape(b, l, A_WIDTH), slopes, sink_rows, swa_bias, dqkv3)
    dqkv = dqkv3.reshape(t, QKV_W)
    df3, dbf = _fgate_bwd("fgate_bwd", f3, bf_row, dcq, dck)
    df = df3.reshape(t, 128)
    dwi_qkv = _mm_tn("proj_qkv_bwd_w", um, dqkv, tm_c=(512,), tn_c=(768,))
    dwi_g = _mm_tn("proj_gates_bwd_w", um, dgates, tm_c=(512,), tn_c=(512,))
    dwi_f = _mm_tn("proj_f_bwd_w", um, df, tm_c=(512,), tn_c=(128,))
    token = send("mix", (dwi_qkv, dwi_g, dwi_f, dwa, dwb, dwo))
    dh1, dh1_bf, dgm = _proj_bwd_x("proj_bwd_x", dqkv, dgates, df, wi, h1, gm, dh2, dep=token)

    grad_x, dmeta3, dg1 = ffn_bwd("ffn1", dh1, dh1_bf, h0, g1, n1, gu1, a1, w1i, w1o, False, examples=b)
    dmeta = dmeta3.reshape(b * N_META, D_MODEL)

    misc = jnp.concatenate([dbf[:, 0:B_HEADS], dsink[:, 0].reshape(1, A_HEADS), loss_blk[0:1, 0:1]], axis=1)
    misc = jnp.pad(misc, ((0, 0), (0, D_MODEL - misc.shape[1])))
    row = lax.broadcasted_iota(jnp.int32, (8, D_MODEL), 0)
    vec = jnp.zeros((8, D_MODEL), F32)
    for i, piece in enumerate((dg1, dgm, dg2, dgf, misc)):
        vec = jnp.where(row == i, piece, vec)
    small = jnp.concatenate([vec, dmeta], axis=0)
    return grad_x, small


def _pad_to(a, rows, cols):
    return jnp.pad(a, ((0, rows - a.shape[0]), (0, cols - a.shape[1])))


def _ffn_out_from_gathered(name, g):
    def body(g_ref, o_ref):
        o_ref[0:FFO_SHARD, :] = g_ref[0]
        o_ref[FFO_SHARD:FF_SHARD, :] = g_ref[1]
        o_ref[FF_SHARD:FF_SHARD_P, :] = jnp.zeros((FF_SHARD_P - FF_SHARD, D_MODEL), BF16)

    return pl.pallas_call(
        body, name=name, grid=(4,),
        in_specs=[pl.BlockSpec((2, FFO_SHARD, D_MODEL), lambda j: (j, 0, 0))],
        out_specs=pl.BlockSpec((FF_SHARD_P, D_MODEL), lambda j: (j, 0)),
        out_shape=jax.ShapeDtypeStruct((D_FF_P, D_MODEL), BF16), compiler_params=_params(("parallel",)),
    )(g)


def _ffn_out_bwd_w(name, a, dh):
    t = a.shape[0]
    tn = D_MODEL // 2

    def body(a_ref, b_ref, o_ref):
        acc = lax.dot_general(a_ref[...], _bf(b_ref[...]), _TN, preferred_element_type=F32) * 0.5
        o_ref[0] = acc[0:FFO_SHARD].astype(BF16)
        o_ref[1] = acc[FFO_SHARD:FF_SHARD].astype(BF16)

    return pl.pallas_call(
        body, name=name, grid=(2, 4),
        in_specs=[pl.BlockSpec((t, FF_SHARD_P), lambda j, i: (0, i)), pl.BlockSpec((t, tn), lambda j, i: (0, j))],
        out_specs=pl.BlockSpec((2, FFO_SHARD, tn), lambda j, i: (i, 0, j)),
        out_shape=jax.ShapeDtypeStruct((N_DEV, FFO_SHARD, D_MODEL), BF16), compiler_params=_params(("parallel", "parallel")),
    )(a, dh)


def _proj_segments():
    segs = [(HEAD_DIM * h, HEAD_DIM, 0, B_SEG + HEAD_DIM * A_HEAD_ORDER.index(h)) for h in range(A_HEADS)]
    segs += [(512, 128, 0, B_SEG + A_WIDTH), (640, 128, 0, B_SEG + A_WIDTH + 128)]
    for first, off in ((768, 0), (1280, 128), (1792, 256)):
        segs += [(first + 128 * hp, 128, 0, PAIR_W * hp + off) for hp in range(4)]
    segs += [(2304, B_HEADS, 2, 0), (2312, 2 * D_MODEL, 1, 0)]
    return segs


_RELAYOUT_ROWS = 256


def _proj_from_gathered(name, g):
    rows = _RELAYOUT_ROWS

    def body(g_ref, o_ref):
        def cols(first, width):
            out = []
            for p in range(N_DEV):
                lo, hi = max(first, WIN_SHARD * p), min(first + width, WIN_SHARD * (p + 1))
                if lo < hi:
                    out.append(g_ref[p, :, lo - WIN_SHARD * p:hi - WIN_SHARD * p])
            return out

        parts = []
        for arr in (0, 1, 2):
            for first, width, _, _ in sorted((s for s in _proj_segments() if s[2] == arr), key=lambda s: s[3]):
                parts += cols(first, width)
            if arr == 0:
                parts.append(jnp.zeros((rows, P_GATES - QKV_W), BF16))
        parts.append(jnp.zeros((rows, 128 - B_HEADS), BF16))
        o_ref[...] = jnp.concatenate(parts, axis=1)

    return pl.pallas_call(
        body, name=name, grid=(D_MODEL // rows,),
        in_specs=[pl.BlockSpec((N_DEV, rows, WIN_SHARD_P), lambda i: (0, i, 0))],
        out_specs=pl.BlockSpec((rows, PROJ_P), lambda i: (i, 0)),
        out_shape=jax.ShapeDtypeStruct((D_MODEL, PROJ_P), BF16), compiler_params=_params(("parallel",)),
    )(g)


def _proj_to_scatter(name, dqkv_w, dg_w, df_w):
    rows = _RELAYOUT_ROWS
    segs = sorted(_proj_segments())

    def body(q_ref, g_ref, f_ref, o_ref):
        arrays = (q_ref, g_ref, f_ref)
        for p in range(N_DEV):
            parts = []
            for first, width, arr, at in segs:
                lo, hi = max(first, WIN_SHARD * p), min(first + width, WIN_SHARD * (p + 1))
                if lo < hi:
                    parts.append(arrays[arr][:, at + lo - first:at + hi - first])
            parts.append(jnp.zeros((rows, WIN_SHARD_P - WIN_SHARD), BF16))
            o_ref[p] = jnp.concatenate(parts, axis=1)

    return pl.pallas_call(
        body, name=name, grid=(D_MODEL // rows,),
        in_specs=[pl.BlockSpec((rows, QKV_W), lambda i: (i, 0)), pl.BlockSpec((rows, 2 * D_MODEL), lambda i: (i, 0)),
                  pl.BlockSpec((rows, 128), lambda i: (i, 0))],
        out_specs=pl.BlockSpec((N_DEV, rows, WIN_SHARD_P), lambda i: (0, i, 0)),
        out_shape=jax.ShapeDtypeStruct((N_DEV, D_MODEL, WIN_SHARD_P), BF16), compiler_params=_params(("parallel",)),
    )(dqkv_w, dg_w, df_w)


def _a_rows_from_natural(w):
    return jnp.concatenate([w[HEAD_DIM * h:HEAD_DIM * (h + 1)] for h in A_HEAD_ORDER], axis=0)


def _a_rows_to_natural(w):
    return jnp.concatenate([w[HEAD_DIM * A_HEAD_ORDER.index(h):HEAD_DIM * (A_HEAD_ORDER.index(h) + 1)]
                            for h in range(A_HEADS)], axis=0)


def kernel(x, meta_tokens, ffn1_norm, ffn1_w_in, ffn1_w_out, mix_norm, w_in, b_forget, attn_sinks, w_branch_a, w_branch_b, w_out, ffn2_norm, ffn2_w_in, ffn2_w_out, final_norm, loss_target, m_meta_tokens, m_ffn1_norm, m_ffn1_w_in, m_ffn1_w_out, m_mix_norm, m_w_in, m_b_forget, m_attn_sinks, m_w_branch_a, m_w_branch_b, m_w_out, m_ffn2_norm, m_ffn2_w_in, m_ffn2_w_out, m_final_norm, v_meta_tokens, v_ffn1_norm, v_ffn1_w_in, v_ffn1_w_out, v_mix_norm, v_w_in, v_b_forget, v_attn_sinks, v_w_branch_a, v_w_branch_b, v_w_out, v_ffn2_norm, v_ffn2_w_in, v_ffn2_w_out, v_final_norm):
    me = 4 * lax.axis_index("x") + 2 * lax.axis_index("y") + lax.axis_index("c")

    def shard(w, tok, rows=None, cols=None):
        piece = (w[0] if tok is None else w[0] + tok[0, 0]).astype(BF16)
        return piece if rows is None else _pad_to(piece, rows, cols)

    first_level, second_level = {}, {}
    first_level["meta"], tok = _gather2_start("gather_meta_start", (meta_tokens,))
    first_level["ffn1_in"], tok = _gather2_start(
        "gather_ffn1_in_start", (shard(ffn1_w_in, None, D_MODEL, FF_SHARD_P),), after=tok)
    first_level["ffn1_out"], tok = _gather2_start("gather_ffn1_out_start", (shard(ffn1_w_out, tok),), after=tok)
    first_level["mix"], tok = _gather2_start(
        "gather_mix_start", (shard(w_in, tok, D_MODEL, WIN_SHARD_P), shard(w_branch_a, tok), shard(w_branch_b, tok),
                             shard(w_out, tok)), after=tok)
    first_level["ffn2"], tok = _gather2_start(
        "gather_ffn2_start", (shard(ffn2_w_in, tok, D_MODEL, FF_SHARD_P), shard(ffn2_w_out, tok)), after=tok)
    started = {"tok": tok}

    def weights(group, after):
        if group.endswith(":forward"):
            group = group[:-len(":forward")]
            second_level[group], token = _gather2_forward("gather_" + group + "_forward", first_level[group], after)
            return token
        if group == "meta":
            after = weights("meta:forward", started["tok"])
        if group == "ffn1_in":
            after = weights("ffn1_in:forward", after)
        got = _gather2_wait("gather_" + group + "_wait", second_level[group], after)
        if group == "mix":
            gwi, gwa, gwb, gwo = got
            return (_proj_from_gathered("proj_w_relayout", gwi), _a_rows_from_natural(gwa.transpose(1, 0, 2).reshape(A_WIDTH, D_MODEL)),
                    gwb.transpose(1, 0, 2).reshape(B_WIDTH, D_MODEL), gwo.reshape(D_MODEL, D_MODEL))
        if group == "ffn1_out":
            return (_ffn_out_from_gathered("ffn1_w_out_relayout", got[0]),)
        if group == "meta":
            return (got[0].transpose(1, 0, 2).reshape(N_META, D_MODEL),)
        if group == "ffn1_in":
            return (got[0].reshape(2, 4, D_MODEL, FF_SHARD_P),)
        return got[0].reshape(2, 4, D_MODEL, FF_SHARD_P), _ffn_out_from_gathered("ffn2_w_out_relayout", got[1])

    scatter_state = {}

    def send(group, grads):
        if group == "mix":
            dwi_qkv, dwi_g, dwi_f, dwa, dwb, dwo = grads
            dwa = _a_rows_to_natural(dwa)
            blocks = (_proj_to_scatter("proj_dw_relayout", dwi_qkv, dwi_g, dwi_f), dwa.reshape(A_WIDTH, N_DEV, 128).transpose(1, 0, 2),
                      dwb.reshape(B_WIDTH, N_DEV, 128).transpose(1, 0, 2), dwo.reshape(N_DEV, 128, D_MODEL))
        elif group.endswith("_in"):
            blocks = (grads[0].reshape(N_DEV, FF_SHARD_P, D_MODEL),)
        elif group.endswith("_out"):
            blocks = (grads[0],)
        else:
            blocks = (grads[0].reshape(N_DEV, FF_SHARD_P, D_MODEL), grads[1])
        scatter_state[group], token = _xchg_start("scatter_" + group + "_start", (), blocks)
        return token

    gf = final_norm.reshape(1, D_MODEL)
    grad_x, small = _local_step(x, loss_target, ffn1_norm, mix_norm, ffn2_norm, gf, b_forget, attn_sinks, weights, send)

    small_state, after = _xchg_start("gather_small_start", (small,), ())
    out = {}
    updates = (
        ("ffn2", (("ffn2_w_in", ffn2_w_in, m_ffn2_w_in, v_ffn2_w_in), ("ffn2_w_out", ffn2_w_out, m_ffn2_w_out, v_ffn2_w_out))),
        ("ffn1_out", (("ffn1_w_out", ffn1_w_out, m_ffn1_w_out, v_ffn1_w_out),)),
        ("mix", (("w_in", w_in, m_w_in, v_w_in), ("w_branch_a", w_branch_a, m_w_branch_a, v_w_branch_a),
                 ("w_branch_b", w_branch_b, m_w_branch_b, v_w_branch_b), ("w_out", w_out, m_w_out, v_w_out))),
    )
    last_update = ("ffn1_in", (("ffn1_w_in", ffn1_w_in, m_ffn1_w_in, v_ffn1_w_in),))

    def update(group, members, after):
        parts_list, blocks_list = _xchg_wait("scatter_" + group + "_wait", scatter_state[group], after)
        prev = None
        for (nm, w, m, v), parts, blocks in zip(members, parts_list, blocks_list):
            if nm.endswith("_w_in"):
                res4 = _adam("adam_" + nm, w[0].T, m[0].T, v[0].T, parts, dep=prev, own=(blocks, me_arr))
                out[nm] = tuple(r.T[None] for r in res4)
            elif nm == "w_in":
                res4 = _adam("adam_" + nm, w[0].T, m[0].T, v[0].T, parts, transposed=True, dep=prev, own=(blocks, me_arr))
                out[nm] = tuple(r.T[None] for r in res4)
            else:
                res4 = _adam("adam_" + nm, w[0], m[0], v[0], parts, dep=prev, own=(blocks, me_arr))
                out[nm] = tuple(r[None] for r in res4)
            prev = res4[0]
        return prev

    me_arr = me.reshape(1).astype(jnp.int32)
    for group, members in updates + (last_update,):
        after = update(group, members, after)
    (packs,), _ = _xchg_wait("gather_small_wait", small_state, after)

    row = lambda a: a.reshape(1, D_MODEL)
    tot, small_res = _small_update("small_update", packs, (
        (ffn1_norm, m_ffn1_norm, v_ffn1_norm), (mix_norm, m_mix_norm, v_mix_norm), (ffn2_norm, m_ffn2_norm, v_ffn2_norm),
        (row(final_norm), row(m_final_norm), row(v_final_norm)), (b_forget, m_b_forget, v_b_forget),
        (attn_sinks, m_attn_sinks, v_attn_sinks)))
    for nm, res4 in zip(("ffn1_norm", "mix_norm", "ffn2_norm", "final_norm", "b_forget", "attn_sinks"), small_res):
        out[nm] = tuple(r.reshape(D_MODEL) for r in res4) if nm == "final_norm" else res4
    loss = tot[4, 2 * B_HEADS]
    g_meta = lax.dynamic_slice(tot[8:24, :], (0, me * 128), (N_META, 128))
    out["meta_tokens"] = tuple(_adam("adam_meta_tokens", meta_tokens, m_meta_tokens, v_meta_tokens, g_meta[None]))

    names = ("meta_tokens", "ffn1_norm", "ffn1_w_in", "ffn1_w_out", "mix_norm", "w_in", "b_forget", "attn_sinks",
             "w_branch_a", "w_branch_b", "w_out", "ffn2_norm", "ffn2_w_in", "ffn2_w_out", "final_norm")
    return (loss, grad_x) + tuple(out[nm][kind] for kind in range(4) for nm in names)
```

```python
import jax
import jax.numpy as jnp
import numpy as np
from jax import lax
from jax.experimental import pallas as pl
from jax.experimental.pallas import tpu as pltpu

F32 = jnp.float32
BF16 = jnp.bfloat16

D_MODEL = 1024
N_META = 16
BLOCK = 128
PREFIX = 128
N_PAD = PREFIX - N_META
HEAD_DIM = 64
A_HEADS = 8
B_HEADS = 8
A_WIDTH = 512
A_KV_WIDTH = 128
B_WIDTH = 512
D_FF = 2816
N_DEV = 8
FF_SHARD = 2 * D_FF // N_DEV
FF_SHARD_P = 768
FFO_SHARD = D_FF // N_DEV
D_FF_P = 4 * FF_SHARD_P
W_IN_COLS = 4360
WIN_SHARD = W_IN_COLS // N_DEV
WIN_SHARD_P = 640
PAIR_W = 3 * 128
B_SEG = 4 * PAIR_W
A_SEG = A_WIDTH + 2 * A_KV_WIDTH
QKV_W = B_SEG + A_SEG
P_GATES = 2 * (2 * D_MODEL)
P_F = P_GATES + 2 * D_MODEL
PROJ_P = P_F + 128
A_HEAD_ORDER = (0, 4, 1, 5, 2, 6, 3, 7)
EPS = 1e-6
NEG = -1e30
SCALE = HEAD_DIM ** -0.5
ADAM_LR = 0.001
ADAM_B1 = 0.9
ADAM_B2 = 0.999
ADAM_EPS = 1e-08
ADAM_WD = 0.01
ADAM_STEP = 10
VMEM_LIMIT = 56 * 1024 * 1024
MESH_ID = pl.DeviceIdType.MESH
SMALL_ROWS = 40

_NN = (((1,), (0,)), ((), ()))
_NT = (((1,), (1,)), ((), ()))
_TN = (((0,), (0,)), ((), ()))


def _params(sem=None):
    return pltpu.CompilerParams(dimension_semantics=sem, vmem_limit_bytes=VMEM_LIMIT)


def _pick(n, cands):
    for c in cands:
        if n % c == 0:
            return c
    raise ValueError(f"no tile for {n}")


def _bf(v):
    return v if v.dtype == BF16 else v.astype(BF16)


def _mm(name, a, b, dims, grid, a_spec, b_spec, o_spec, out_shape, out_dtype, alpha=1.0):
    def body(a_ref, b_ref, o_ref):
        acc = lax.dot_general(_bf(a_ref[...]), _bf(b_ref[...]), dims, preferred_element_type=F32)
        if alpha != 1.0:
            acc = acc * alpha
        o_ref[...] = acc.astype(o_ref.dtype)

    return pl.pallas_call(
        body, name=name, grid=grid, in_specs=[a_spec, b_spec], out_specs=o_spec,
        out_shape=jax.ShapeDtypeStruct(out_shape, out_dtype),
        compiler_params=_params(("parallel",) * len(grid)),
    )(a, b)


def _mm_res_norm(name, a, w, res, g_next, alpha=1.0):
    t, k = a.shape
    tm = _pick(t, (544, 384, 256, 128))

    def body(a_ref, w_ref, r_ref, g_ref, h_ref, n_ref):
        acc = lax.dot_general(_bf(a_ref[...]), w_ref[...], _NN, preferred_element_type=F32)
        if alpha != 1.0:
            acc = acc * alpha
        hv = acc + r_ref[...]
        h_ref[...] = hv
        r = lax.rsqrt(jnp.mean(hv * hv, axis=-1, keepdims=True) + EPS)
        n_ref[...] = ((hv * r) * g_ref[...]).astype(BF16)

    row = pl.BlockSpec((tm, D_MODEL), lambda i: (i, 0))
    return pl.pallas_call(
        body, name=name, grid=(t // tm,),
        in_specs=[pl.BlockSpec((tm, k), lambda i: (i, 0)), pl.BlockSpec((k, D_MODEL), lambda i: (0, 0)), row,
                  pl.BlockSpec((1, D_MODEL), lambda i: (0, 0))],
        out_specs=[row, row],
        out_shape=[jax.ShapeDtypeStruct((t, D_MODEL), F32), jax.ShapeDtypeStruct((t, D_MODEL), BF16)],
        compiler_params=_params(("parallel",)),
    )(a, w, res, g_next)


def _mm_tn(name, a, b, out_dtype=BF16, alpha=1.0, tm_c=(768, 512, 256, 128), tn_c=(512, 640, 256, 128)):
    t, m = a.shape
    n = b.shape[1]
    tm = _pick(m, tm_c)
    tn = _pick(n, tn_c)
    bytes_a, bytes_b = a.size * a.dtype.itemsize, b.size * b.dtype.itemsize
    if bytes_a + bytes_b * (m // tm) <= bytes_b + bytes_a * (n // tn):
        return _mm(name, a, b, _TN, (m // tm, n // tn),
                   pl.BlockSpec((t, tm), lambda i, j: (0, i)), pl.BlockSpec((t, tn), lambda i, j: (0, j)),
                   pl.BlockSpec((tm, tn), lambda i, j: (i, j)), (m, n), out_dtype, alpha=alpha)
    return _mm(name, a, b, _TN, (n // tn, m // tm),
               pl.BlockSpec((t, tm), lambda j, i: (0, i)), pl.BlockSpec((t, tn), lambda j, i: (0, j)),
               pl.BlockSpec((tm, tn), lambda j, i: (i, j)), (m, n), out_dtype, alpha=alpha)


def _ffn_in_fwd(name, n, wblk, dep=None):
    t = n.shape[0]
    tm = _pick(t, (1088, 768, 512, 256, 128))
    has_dep = dep is not None

    def body(n_ref, w_ref, *rest):
        gu_ref, a_ref = rest[-2], rest[-1]
        nv = n_ref[...]
        g = lax.dot_general(nv, w_ref[0], _NN, preferred_element_type=F32)
        u = lax.dot_general(nv, w_ref[1], _NN, preferred_element_type=F32)
        sg = jax.nn.sigmoid(g)
        silu = g * sg
        a_ref[...] = (silu * u).astype(BF16)
        gu_ref[0] = ((0.5 * u) * (sg + silu * (1.0 - sg))).astype(BF16)
        gu_ref[1] = (0.5 * silu).astype(BF16)

    return pl.pallas_call(
        body, name=name, grid=(t // tm, 4),
        in_specs=[pl.BlockSpec((tm, D_MODEL), lambda i, j: (i, 0)),
                  pl.BlockSpec((2, None, D_MODEL, FF_SHARD_P), lambda i, j: (0, j, 0, 0))]
        + ([pl.BlockSpec(memory_space=pl.ANY)] if has_dep else []),
        out_specs=[pl.BlockSpec((2, tm, FF_SHARD_P), lambda i, j: (0, i, j)),
                   pl.BlockSpec((tm, FF_SHARD_P), lambda i, j: (i, j))],
        out_shape=[jax.ShapeDtypeStruct((2, t, D_FF_P), BF16), jax.ShapeDtypeStruct((t, D_FF_P), BF16)],
        compiler_params=_params(("parallel", "parallel")),
    )(*((n, wblk) + ((dep,) if has_dep else ())))


def _ffn_out_bwd_x(name, dh, w_out, gu, dep=None):
    t = dh.shape[0]
    tm = _pick(t, (1088, 768, 512, 256, 128))
    has_dep = dep is not None

    def body(dh_ref, w_ref, gu_ref, *rest):
        o_ref = rest[-1]
        w = w_ref[pl.ds(pl.multiple_of(pl.program_id(1) * FF_SHARD_P, FF_SHARD_P), FF_SHARD_P), :]
        da = lax.dot_general(_bf(dh_ref[...]), w, _NT, preferred_element_type=F32)
        o_ref[0] = (da * gu_ref[0].astype(F32)).astype(BF16)
        o_ref[1] = (da * gu_ref[1].astype(F32)).astype(BF16)

    gu_spec = pl.BlockSpec((2, tm, FF_SHARD_P), lambda i, j: (0, i, j))
    return pl.pallas_call(
        body, name=name, grid=(t // tm, 4),
        in_specs=[pl.BlockSpec((tm, D_MODEL), lambda i, j: (i, 0)),
                  pl.BlockSpec((D_FF_P, D_MODEL), lambda i, j: (0, 0), pipeline_mode=pl.Buffered(1)),
                  gu_spec] + ([pl.BlockSpec(memory_space=pl.ANY)] if has_dep else []),
        out_specs=gu_spec, out_shape=jax.ShapeDtypeStruct((2, t, D_FF_P), BF16),
        compiler_params=_params(("parallel", "parallel")),
    )(*((dh, w_out, gu) + ((dep,) if has_dep else ())))


def _rms_bwd_rows(dn, h, g, dres):
    r = lax.rsqrt(jnp.mean(h * h, axis=-1, keepdims=True) + EPS)
    tv = dn * g
    dot = jnp.mean(tv * h, axis=-1, keepdims=True)
    return dres + (r * tv - h * (r * r * r * dot)), jnp.sum(dn * (h * r), axis=0, keepdims=True)


def _accumulate_rows(ref, part, first):
    @pl.when(first)
    def _():
        ref[...] = part

    @pl.when(jnp.logical_not(first))
    def _():
        ref[...] += part


def _ffn_in_bwd_x(name, dgu, wblk, h_in, g_norm, dres, dep=None, examples=None):
    t = dgu.shape[1]
    tm = _pick(t, (544, 384, 256, 128))
    has_dep = dep is not None
    split = examples is not None
    if split:
        l = t // examples
        per = l // tm
        assert per * tm == l and tm > PREFIX

    def body(d_ref, w_ref, h_ref, g_ref, r_ref, *rest):
        acc = None
        for s in range(2):
            for j in range(4):
                part = lax.dot_general(d_ref[s, :, FF_SHARD_P * j:FF_SHARD_P * (j + 1)], w_ref[s, j], _NT,
                                       preferred_element_type=F32)
                acc = part if acc is None else acc + part
        dh, dg = _rms_bwd_rows(acc, h_ref[...], g_ref[...], r_ref[...])
        i = pl.program_id(0)
        if not split:
            dh_ref, dhb_ref, dg_ref = rest[-3:]
            dh_ref[...] = dh
            dhb_ref[...] = dh.astype(BF16)
        else:
            gx_ref, meta_ref, dg_ref, buf, sem = rest[-5:]

            def out_copy(step):
                bi, r = step // per, step % per
                head = pltpu.make_async_copy(buf.at[pl.ds(PREFIX, tm - PREFIX)], gx_ref.at[bi, pl.ds(0, tm - PREFIX)], sem)
                if per == 1:
                    return r == 0, head, None
                return r == 0, head, pltpu.make_async_copy(
                    buf, gx_ref.at[bi, pl.ds(pl.multiple_of(jnp.maximum(r, 1) * tm - PREFIX, 8), tm)], sem)

            def run(step, method):
                is_head, head, later = out_copy(step)

                @pl.when(is_head)
                def _():
                    getattr(head, method)()

                if later is not None:
                    @pl.when(jnp.logical_not(is_head))
                    def _():
                        getattr(later, method)()

            @pl.when(i > 0)
            def _():
                run(i - 1, "wait")

            buf[...] = dh

            @pl.when(i % per == 0)
            def _():
                meta_ref[...] = dh[N_PAD:PREFIX]

            run(i, "start")

            @pl.when(i == pl.num_programs(0) - 1)
            def _():
                run(i, "wait")

        _accumulate_rows(dg_ref, dg, i == 0)

    row = pl.BlockSpec((tm, D_MODEL), lambda i: (i, 0))
    vec = pl.BlockSpec((1, D_MODEL), lambda i: (0, 0))
    if split:
        out_specs = [pl.BlockSpec(memory_space=pl.ANY), pl.BlockSpec((None, N_META, D_MODEL), lambda i: (i // per, 0, 0)), vec]
        out_shape = [jax.ShapeDtypeStruct((examples, l - PREFIX, D_MODEL), F32),
                     jax.ShapeDtypeStruct((examples, N_META, D_MODEL), F32), jax.ShapeDtypeStruct((1, D_MODEL), F32)]
        scratch = [pltpu.VMEM((tm, D_MODEL), F32), pltpu.SemaphoreType.DMA(())]
    else:
        out_specs = [row, row, vec]
        out_shape = [jax.ShapeDtypeStruct((t, D_MODEL), F32), jax.ShapeDtypeStruct((t, D_MODEL), BF16),
                     jax.ShapeDtypeStruct((1, D_MODEL), F32)]
        scratch = []
    return pl.pallas_call(
        body, name=name, grid=(t // tm,),
        in_specs=[pl.BlockSpec((2, tm, D_FF_P), lambda i: (0, i, 0)),
                  pl.BlockSpec((2, 4, D_MODEL, FF_SHARD_P), lambda i: (0, 0, 0, 0), pipeline_mode=pl.Buffered(1)),
                  row, vec, row]
        + ([pl.BlockSpec(memory_space=pl.ANY)] if has_dep else []),
        out_specs=out_specs, out_shape=out_shape, scratch_shapes=scratch,
        compiler_params=_params(("arbitrary",)),
    )(*((dgu, wblk, h_in, g_norm, dres) + ((dep,) if has_dep else ())))


def _proj_fwd(name, um, wi):
    t = um.shape[0]
    tm = _pick(t, (544, 384, 256, 128))

    def body(u_ref, w_ref, q_ref, g_ref, f_ref):
        uv = u_ref[...]
        q_ref[...] = lax.dot_general(uv, w_ref[:, 0:QKV_W], _NN, preferred_element_type=F32).astype(BF16)
        g_ref[...] = lax.dot_general(uv, w_ref[:, P_GATES:P_F], _NN, preferred_element_type=F32).astype(BF16)
        f_ref[...] = lax.dot_general(uv, w_ref[:, P_F:PROJ_P], _NN, preferred_element_type=F32)

    return pl.pallas_call(
        body, name=name, grid=(t // tm,),
        in_specs=[pl.BlockSpec((tm, D_MODEL), lambda i: (i, 0)),
                  pl.BlockSpec((D_MODEL, PROJ_P), lambda i: (0, 0), pipeline_mode=pl.Buffered(1))],
        out_specs=[pl.BlockSpec((tm, QKV_W), lambda i: (i, 0)), pl.BlockSpec((tm, 2 * D_MODEL), lambda i: (i, 0)),
                   pl.BlockSpec((tm, 128), lambda i: (i, 0))],
        out_shape=[jax.ShapeDtypeStruct((t, QKV_W), BF16), jax.ShapeDtypeStruct((t, 2 * D_MODEL), BF16),
                   jax.ShapeDtypeStruct((t, 128), F32)],
        compiler_params=_params(("parallel",)),
    )(um, wi)


def _proj_bwd_x(name, dqkv, dgates, df, wi, h_in, g_norm, dres, dep=None):
    t = dqkv.shape[0]
    tm = _pick(t, (544, 384, 256, 128))
    has_dep = dep is not None

    def body(q_ref, gt_ref, f_ref, w_ref, h_ref, g_ref, r_ref, *rest):
        dh_ref, dhb_ref, dg_ref = rest[-3:]
        acc = lax.dot_general(q_ref[...], w_ref[:, 0:QKV_W], _NT, preferred_element_type=F32)
        acc = acc + lax.dot_general(gt_ref[...], w_ref[:, P_GATES:P_F], _NT, preferred_element_type=F32)
        acc = acc + lax.dot_general(f_ref[...], w_ref[:, P_F:PROJ_P], _NT, preferred_element_type=F32)
        dh, dg = _rms_bwd_rows(acc, h_ref[...], g_ref[...], r_ref[...])
        dh_ref[...] = dh
        dhb_ref[...] = dh.astype(BF16)
        _accumulate_rows(dg_ref, dg, pl.program_id(0) == 0)

    row = pl.BlockSpec((tm, D_MODEL), lambda i: (i, 0))
    vec = pl.BlockSpec((1, D_MODEL), lambda i: (0, 0))
    return pl.pallas_call(
        body, name=name, grid=(t // tm,),
        in_specs=[pl.BlockSpec((tm, QKV_W), lambda i: (i, 0)), pl.BlockSpec((tm, 2 * D_MODEL), lambda i: (i, 0)),
                  pl.BlockSpec((tm, 128), lambda i: (i, 0)),
                  pl.BlockSpec((D_MODEL, PROJ_P), lambda i: (0, 0), pipeline_mode=pl.Buffered(1)), row, vec, row]
        + ([pl.BlockSpec(memory_space=pl.ANY)] if has_dep else []),
        out_specs=[row, row, vec],
        out_shape=[jax.ShapeDtypeStruct((t, D_MODEL), F32), jax.ShapeDtypeStruct((t, D_MODEL), BF16),
                   jax.ShapeDtypeStruct((1, D_MODEL), F32)],
        compiler_params=_params(("arbitrary",)),
    )(*((dqkv, dgates, df, wi, h_in, g_norm, dres) + ((dep,) if has_dep else ())))


def _ffn_in_bwd_w(name, n, dgu):
    t = n.shape[0]
    return _mm(name, dgu, n, _TN, (2, 4),
               pl.BlockSpec((None, t, FF_SHARD_P), lambda s, j: (s, 0, j)),
               pl.BlockSpec((t, D_MODEL), lambda s, j: (0, 0)),
               pl.BlockSpec((None, None, FF_SHARD_P, D_MODEL), lambda s, j: (s, j, 0, 0)),
               (2, 4, FF_SHARD_P, D_MODEL), BF16)


def _embed_norm(name, x, meta, g):
    b, s, _ = x.shape
    half = (s + PREFIX) // 2
    first = half - PREFIX
    assert first > 0 and half % 16 == 0

    def body(x_ref, m_ref, g_ref, h_ref, n_ref, buf, sem):
        bi, k = pl.program_id(0), pl.program_id(1)

        def tokens(example, second, method):
            if second:
                cp = pltpu.make_async_copy(x_ref.at[example, pl.ds(first, half)], buf.at[1], sem.at[1])
            else:
                cp = pltpu.make_async_copy(x_ref.at[example, pl.ds(0, first)], buf.at[0, pl.ds(PREFIX, first)], sem.at[0])
            getattr(cp, method)()

        @pl.when((bi == 0) & (k == 0))
        def _():
            tokens(0, False, "start")

        @pl.when(k == 0)
        def _():
            tokens(bi, True, "start")
            tokens(bi, False, "wait")
            buf[0, 0:N_PAD, :] = jnp.zeros((N_PAD, D_MODEL), F32)
            buf[0, N_PAD:PREFIX, :] = m_ref[...]

        @pl.when(k == 1)
        def _():
            @pl.when(bi + 1 < b)
            def _():
                tokens(bi + 1, False, "start")

            tokens(bi, True, "wait")

        hv = buf[k]
        h_ref[...] = hv
        r = lax.rsqrt(jnp.mean(hv * hv, axis=-1, keepdims=True) + EPS)
        n_ref[...] = ((hv * r) * g_ref[...]).astype(BF16)

    rows = pl.BlockSpec((half, D_MODEL), lambda bi, k: (2 * bi + k, 0))
    return pl.pallas_call(
        body, name=name, grid=(b, 2),
        in_specs=[pl.BlockSpec(memory_space=pl.ANY), pl.BlockSpec((N_META, D_MODEL), lambda bi, k: (0, 0)),
                  pl.BlockSpec((1, D_MODEL), lambda bi, k: (0, 0))],
        out_specs=[rows, rows],
        out_shape=[jax.ShapeDtypeStruct((2 * b * half, D_MODEL), F32), jax.ShapeDtypeStruct((2 * b * half, D_MODEL), BF16)],
        scratch_shapes=[pltpu.VMEM((2, half, D_MODEL), F32), pltpu.SemaphoreType.DMA((2,))],
        compiler_params=_params(("arbitrary", "arbitrary")),
    )(x, meta, g)


def _branch_gate_fwd(name, oa, ob, wa, wb, gates, dep=None):
    t = gates.shape[0]
    tm = _pick(t, (544, 384, 256, 128))
    has_dep = dep is not None

    def body(oa_ref, ob_ref, wa_ref, wb_ref, g_ref, *rest):
        o_ref, ya_ref, yb_ref = rest[-3:]
        ya = lax.dot_general(_bf(oa_ref[...]), wa_ref[...], _NN, preferred_element_type=F32)
        yb = lax.dot_general(_bf(ob_ref[...]), wb_ref[...], _NN, preferred_element_type=F32)
        sa = jax.nn.sigmoid(g_ref[:, 0:D_MODEL].astype(F32))
        sb = jax.nn.sigmoid(g_ref[:, D_MODEL:2 * D_MODEL].astype(F32))
        o_ref[...] = (sa * ya + sb * yb).astype(BF16)
        ya_ref[...] = ya.astype(BF16)
        yb_ref[...] = yb.astype(BF16)

    blk = pl.BlockSpec((tm, D_MODEL), lambda i: (i, 0))
    narrow = pl.BlockSpec((tm, A_WIDTH), lambda i: (i, 0))
    wide = pl.BlockSpec((tm, 2 * D_MODEL), lambda i: (i, 0))
    wspec = pl.BlockSpec((A_WIDTH, D_MODEL), lambda i: (0, 0))
    out = jax.ShapeDtypeStruct((t, D_MODEL), BF16)
    return pl.pallas_call(
        body, name=name, grid=(t // tm,),
        in_specs=[narrow, narrow, wspec, wspec, wide] + ([pl.BlockSpec(memory_space=pl.ANY)] if has_dep else []),
        out_specs=[blk, blk, blk], out_shape=[out, out, out], compiler_params=_params(("parallel",)),
    )(*((oa, ob, wa, wb, gates) + ((dep,) if has_dep else ())))


def _branch_bwd_x(name, dya, dyb, wa, wb):
    t = dya.shape[0]
    tm = _pick(t, (1088, 768, 512, 256, 128))

    def body(da_ref, db_ref, wa_ref, wb_ref, oa_ref, ob_ref):
        oa_ref[...] = lax.dot_general(da_ref[...], wa_ref[...], _NT, preferred_element_type=F32)
        ob_ref[...] = lax.dot_general(db_ref[...], wb_ref[...], _NT, preferred_element_type=F32).astype(BF16)

    blk = pl.BlockSpec((tm, D_MODEL), lambda i: (i, 0))
    narrow = pl.BlockSpec((tm, A_WIDTH), lambda i: (i, 0))
    wspec = pl.BlockSpec((A_WIDTH, D_MODEL), lambda i: (0, 0), pipeline_mode=pl.Buffered(1))
    return pl.pallas_call(
        body, name=name, grid=(t // tm,), in_specs=[blk, blk, wspec, wspec], out_specs=[narrow, narrow],
        out_shape=[jax.ShapeDtypeStruct((t, A_WIDTH), F32), jax.ShapeDtypeStruct((t, B_WIDTH), BF16)],
        compiler_params=_params(("parallel",)),
    )(dya, dyb, wa, wb)


def _branch_bwd_w(name, oa, ob, dya, dyb):
    t = oa.shape[0]
    tn = 512

    def body(oa_ref, ob_ref, da_ref, db_ref, wa_ref, wb_ref):
        wa_ref[...] = lax.dot_general(_bf(oa_ref[...]), da_ref[...], _TN, preferred_element_type=F32).astype(BF16)
        wb_ref[...] = lax.dot_general(_bf(ob_ref[...]), db_ref[...], _TN, preferred_element_type=F32).astype(BF16)

    whole = pl.BlockSpec((t, A_WIDTH), lambda j: (0, 0), pipeline_mode=pl.Buffered(1))
    cols = pl.BlockSpec((t, tn), lambda j: (0, j))
    out_spec = pl.BlockSpec((A_WIDTH, tn), lambda j: (0, j))
    out = jax.ShapeDtypeStruct((A_WIDTH, D_MODEL), BF16)
    return pl.pallas_call(
        body, name=name, grid=(D_MODEL // tn,), in_specs=[whole, whole, cols, cols], out_specs=[out_spec, out_spec],
        out_shape=[out, out], compiler_params=_params(("parallel",)),
    )(oa, ob, dya, dyb)


def _mix_out_gate_bwd(name, dh, wo, gates, ya, yb):
    t = gates.shape[0]
    tm = _pick(t, (544, 384, 256, 128))

    def body(dh_ref, w_ref, g_ref, ya_ref, yb_ref, dya_ref, dyb_ref, dg_ref):
        dm = lax.dot_general(_bf(dh_ref[...]), w_ref[...], _NT, preferred_element_type=F32)
        sa = jax.nn.sigmoid(g_ref[:, 0:D_MODEL].astype(F32))
        sb = jax.nn.sigmoid(g_ref[:, D_MODEL:2 * D_MODEL].astype(F32))
        dya_ref[...] = (dm * sa).astype(BF16)
        dyb_ref[...] = (dm * sb).astype(BF16)
        dg_ref[:, 0:D_MODEL] = (dm * ya_ref[...].astype(F32) * (sa * (1.0 - sa))).astype(BF16)
        dg_ref[:, D_MODEL:2 * D_MODEL] = (dm * yb_ref[...].astype(F32) * (sb * (1.0 - sb))).astype(BF16)

    blk = pl.BlockSpec((tm, D_MODEL), lambda i: (i, 0))
    wide = pl.BlockSpec((tm, 2 * D_MODEL), lambda i: (i, 0))
    out = jax.ShapeDtypeStruct((t, D_MODEL), BF16)
    return pl.pallas_call(
        body, name=name, grid=(t // tm,),
        in_specs=[blk, pl.BlockSpec((D_MODEL, D_MODEL), lambda i: (0, 0)), wide, blk, blk], out_specs=[blk, blk, wide],
        out_shape=[out, out, jax.ShapeDtypeStruct((t, 2 * D_MODEL), BF16)], compiler_params=_params(("parallel",)),
    )(dh, wo, gates, ya, yb)


def _ffn_out_loss(name, a, w, res, gf, tgt, alpha):
    t, k = a.shape
    b, s, _ = tgt.shape
    l = t // b
    tm = _pick(t, (544, 384, 256, 128))
    per = l // tm
    assert per * tm == l and tm > PREFIX and l - PREFIX == s

    def body(a_ref, w_ref, r_ref, g_ref, t_ref, dh_ref, dhb_ref, loss_ref, dg_ref, buf, sem):
        i = pl.program_id(0)
        bi, r = i // per, i % per

        def first_rows():
            return pltpu.make_async_copy(t_ref.at[bi, pl.ds(0, tm - PREFIX)], buf.at[pl.ds(PREFIX, tm - PREFIX)], sem)

        def later_rows():
            return pltpu.make_async_copy(t_ref.at[bi, pl.ds(pl.multiple_of(r * tm - PREFIX, 8), tm)], buf, sem)

        @pl.when(r == 0)
        def _():
            buf[0:PREFIX, :] = jnp.zeros((PREFIX, D_MODEL), F32)
            first_rows().start()

        if per > 1:
            @pl.when(r > 0)
            def _():
                later_rows().start()

        acc = lax.dot_general(a_ref[...], w_ref[...], _NN, preferred_element_type=F32)

        @pl.when(r == 0)
        def _():
            first_rows().wait()

        if per > 1:
            @pl.when(r > 0)
            def _():
                later_rows().wait()

        hv = acc * alpha + r_ref[...]
        row = lax.broadcasted_iota(jnp.int32, (tm, 1), 0)
        real = ((r > 0) | (row >= PREFIX)).astype(F32)
        g = g_ref[...]
        rn = lax.rsqrt(jnp.mean(hv * hv, axis=-1, keepdims=True) + EPS)
        xn = hv * rn
        err = (xn * g - buf[...]) * real
        lpart = 0.5 * jnp.sum(jnp.mean(err * err, axis=-1, keepdims=True), axis=0, keepdims=True)
        dy = err * (1.0 / D_MODEL)
        tv = dy * g
        dot = jnp.mean(tv * hv, axis=-1, keepdims=True)
        dh = rn * tv - hv * (rn * rn * rn * dot)
        dh_ref[...] = dh
        dhb_ref[...] = dh.astype(BF16)
        gpart = jnp.sum(dy * xn, axis=0, keepdims=True)

        @pl.when(i == 0)
        def _():
            loss_ref[...] = jnp.zeros_like(loss_ref)
            dg_ref[...] = jnp.zeros_like(dg_ref)

        loss_ref[...] += jnp.broadcast_to(lpart, loss_ref.shape)
        dg_ref[...] += gpart

    row_spec = pl.BlockSpec((tm, D_MODEL), lambda i: (i, 0))
    vec = pl.BlockSpec((1, D_MODEL), lambda i: (0, 0))
    return pl.pallas_call(
        body, name=name, grid=(t // tm,),
        in_specs=[pl.BlockSpec((tm, k), lambda i: (i, 0)),
                  pl.BlockSpec((k, D_MODEL), lambda i: (0, 0), pipeline_mode=pl.Buffered(1)), row_spec, vec,
                  pl.BlockSpec(memory_space=pl.ANY)],
        out_specs=[row_spec, row_spec, pl.BlockSpec((8, 128), lambda i: (0, 0)), vec],
        out_shape=[jax.ShapeDtypeStruct((t, D_MODEL), F32), jax.ShapeDtypeStruct((t, D_MODEL), BF16),
                   jax.ShapeDtypeStruct((8, 128), F32), jax.ShapeDtypeStruct((1, D_MODEL), F32)],
        scratch_shapes=[pltpu.VMEM((tm, D_MODEL), F32), pltpu.SemaphoreType.DMA(())],
        compiler_params=_params(("arbitrary",)),
    )(a, w, res, gf, tgt)


def _fgate_fwd(name, f3, bf_row):
    b, l, _ = f3.shape
    nb = l // BLOCK

    def body(f_ref, b_ref, cc_ref, cr_ref):
        r_i = lax.broadcasted_iota(jnp.int32, (BLOCK, BLOCK), 0)
        c_i = lax.broadcasted_iota(jnp.int32, (BLOCK, BLOCK), 1)
        tri = (r_i >= c_i).astype(F32)
        carry = jnp.zeros((1, 128), F32)
        for blk in range(nb):
            rows = slice(blk * BLOCK, (blk + 1) * BLOCK)
            z = f_ref[rows, :] + b_ref[...]
            lf = jnp.minimum(z, 0.0) - jnp.log(1.0 + jnp.exp(-jnp.abs(z)))
            cb = jnp.dot(tri, lf, preferred_element_type=F32, precision=lax.Precision.HIGHEST) + carry
            carry = cb[BLOCK - 1:BLOCK, :]
            cbt = cb.T
            for hh in range(B_HEADS):
                cc_ref[hh, rows, :] = jnp.sum(jnp.where(c_i == hh, cb, 0.0), axis=1, keepdims=True)
                cr_ref[hh, :, rows] = cbt[hh:hh + 1, :]

    return pl.pallas_call(
        body, name=name, grid=(b,),
        in_specs=[pl.BlockSpec((None, l, 128), lambda bi: (bi, 0, 0)),
                  pl.BlockSpec((1, 128), lambda bi: (0, 0))],
        out_specs=[pl.BlockSpec((None, B_HEADS, l, 1), lambda bi: (bi, 0, 0, 0)),
                   pl.BlockSpec((None, B_HEADS, 1, l), lambda bi: (bi, 0, 0, 0))],
        out_shape=[jax.ShapeDtypeStruct((b, B_HEADS, l, 1), F32), jax.ShapeDtypeStruct((b, B_HEADS, 1, l), F32)],
        compiler_params=_params(("parallel",)),
    )(f3, bf_row)


def _fgate_bwd(name, f3, bf_row, dcq, dck):
    b, l, _ = f3.shape
    nb = l // BLOCK

    def body(f_ref, b_ref, dcq_ref, dck_ref, df_ref, db_ref):
        r_i = lax.broadcasted_iota(jnp.int32, (BLOCK, BLOCK), 0)
        c_i = lax.broadcasted_iota(jnp.int32, (BLOCK, BLOCK), 1)
        tri = (r_i <= c_i).astype(F32)
        carry = jnp.zeros((1, 128), F32)
        total = jnp.zeros((1, 128), F32)
        for blk in range(nb - 1, -1, -1):
            rows = slice(blk * BLOCK, (blk + 1) * BLOCK)
            krows = jnp.concatenate([dck_ref[hh, :, rows] for hh in range(B_HEADS)]
                                    + [jnp.zeros((BLOCK - B_HEADS, BLOCK), F32)], axis=0)
            dcb = krows.T
            for hh in range(B_HEADS):
                dcb = dcb + jnp.where(c_i == hh, dcq_ref[hh, rows, :], 0.0)
            rc = jnp.dot(tri, dcb, preferred_element_type=F32, precision=lax.Precision.HIGHEST) + carry
            carry = rc[0:1, :]
            z = f_ref[rows, :] + b_ref[...]
            df = rc * (1.0 / (1.0 + jnp.exp(z)))
            df_ref[rows, :] = df.astype(BF16)
            total = total + jnp.sum(df, axis=0, keepdims=True)

        @pl.when(pl.program_id(0) == 0)
        def _():
            db_ref[...] = total

        @pl.when(pl.program_id(0) > 0)
        def _():
            db_ref[...] += total

    return pl.pallas_call(
        body, name=name, grid=(b,),
        in_specs=[pl.BlockSpec((None, l, 128), lambda bi: (bi, 0, 0)),
                  pl.BlockSpec((1, 128), lambda bi: (0, 0)),
                  pl.BlockSpec((None, B_HEADS, l, 1), lambda bi: (bi, 0, 0, 0)),
                  pl.BlockSpec((None, B_HEADS, 1, l), lambda bi: (bi, 0, 0, 0))],
        out_specs=[pl.BlockSpec((None, l, 128), lambda bi: (bi, 0, 0)), pl.BlockSpec((1, 128), lambda bi: (0, 0))],
        out_shape=[jax.ShapeDtypeStruct((b, l, 128), BF16), jax.ShapeDtypeStruct((1, 128), F32)],
        compiler_params=_params(("arbitrary",)),
    )(f3, bf_row, dcq, dck)


A_Q_BLK = B_SEG // A_WIDTH
A_K_BLK = (B_SEG + A_WIDTH) // 128
A_V_BLK = A_K_BLK + 1
A_SEG_BLK = B_SEG // A_SEG
STACK = A_HEADS * BLOCK


def _lane_lo():
    return lax.broadcasted_iota(jnp.int32, (1, 128), 1) < HEAD_DIM


def _stack_heads(x, masked):
    lo = _lane_lo()
    blks = [x[:, 128 * j:128 * (j + 1)] for j in range(4)]
    if not masked:
        return jnp.concatenate(blks + blks, axis=0)
    zero = jnp.zeros_like(blks[0])
    return jnp.concatenate([jnp.where(lo, bk, zero) for bk in blks] + [jnp.where(lo, zero, bk) for bk in blks], axis=0)


def _unstack_heads(y):
    lo = _lane_lo()
    return jnp.concatenate([jnp.where(lo, y[128 * j:128 * (j + 1)], y[128 * (4 + j):128 * (5 + j)]) for j in range(4)], axis=1)


def _swa_bias(slopes):
    slopes = np.asarray(slopes, np.float32)
    r_i = np.arange(STACK)[:, None]
    c_i = np.arange(3 * BLOCK)[None, :]
    seg = c_i >> 7
    out = []
    for n in range(3):
        qpos = n * BLOCK + (r_i & (BLOCK - 1))
        kpos = np.where(seg == 0, c_i, (n - 2) * BLOCK + c_i)
        dist = qpos - kpos
        band = (seg != 0) & (dist < BLOCK) & (kpos >= PREFIX)
        meta = (seg == 0) & (c_i >= N_PAD)
        out.append(np.where((dist >= 0) & (band | meta), -slopes * dist.astype(np.float32), np.float32(NEG)))
    return jnp.asarray(np.stack(out, axis=0), F32)


def _swa_scores(q, kcat, n, slope, bias):
    s = lax.dot_general(q, kcat, _NT, preferred_element_type=F32) + bias
    further = slope * (-BLOCK * jnp.maximum(n - 2, 0)).astype(F32)
    return jnp.concatenate([s[:, 0:BLOCK] + further, s[:, BLOCK:]], axis=1)


def _swa_specs():
    def kv(col_blk):
        return [pl.BlockSpec((None, BLOCK, 128), lambda b, n: (b, 0, col_blk)),
                pl.BlockSpec((None, BLOCK, 128), lambda b, n: (b, jnp.maximum(n - 1, 0), col_blk)),
                pl.BlockSpec((None, BLOCK, 128), lambda b, n: (b, n, col_blk))]

    q_spec = pl.BlockSpec((None, BLOCK, A_WIDTH), lambda b, n: (b, n, A_Q_BLK))
    o_spec = pl.BlockSpec((None, BLOCK, A_WIDTH), lambda b, n: (b, n, 0))
    col = pl.BlockSpec((STACK, 1), lambda b, n: (0, 0))
    bias = pl.BlockSpec((None, STACK, 3 * BLOCK), lambda b, n: (jnp.minimum(n, 2), 0, 0))
    lse_spec = pl.BlockSpec((None, A_HEADS, BLOCK, 1), lambda b, n: (b, 0, n, 0))
    return q_spec, kv(A_K_BLK), kv(A_V_BLK), o_spec, [col, col, bias], lse_spec


def _swa_fwd(name, qkv, slopes, sinks, bias):
    b, l, _ = qkv.shape
    nb = l // BLOCK

    def body(q_ref, k0_ref, kp_ref, kc_ref, v0_ref, vp_ref, vc_ref, sl_ref, sk_ref, bias_ref, o_ref, o16_ref, lse_ref):
        n = pl.program_id(1)
        qs = _stack_heads(q_ref[...], True) * SCALE
        kcat = jnp.concatenate([k0_ref[...], kp_ref[...], kc_ref[...]], axis=0)
        vcat = jnp.concatenate([v0_ref[...], vp_ref[...], vc_ref[...]], axis=0)
        s = _swa_scores(qs, kcat, n, sl_ref[...], bias_ref[...])
        sink = sk_ref[...]
        m = jnp.maximum(jnp.max(s, axis=-1, keepdims=True), sink)
        p = jnp.exp(s - m)
        den = jnp.sum(p, axis=-1, keepdims=True) + jnp.exp(sink - m)
        o = lax.dot_general(p.astype(BF16), vcat, _NN, preferred_element_type=F32) / den
        ov = _unstack_heads(o)
        o_ref[...] = ov
        o16_ref[...] = ov.astype(BF16)
        lse_ref[...] = (m + jnp.log(den)).reshape(A_HEADS, BLOCK, 1)

    q_spec, k_specs, v_specs, o_spec, consts, lse_spec = _swa_specs()
    return pl.pallas_call(
        body, name=name, grid=(b, nb),
        in_specs=[q_spec] + k_specs + v_specs + consts, out_specs=[o_spec, o_spec, lse_spec],
        out_shape=[jax.ShapeDtypeStruct((b, l, A_WIDTH), F32), jax.ShapeDtypeStruct((b, l, A_WIDTH), BF16),
                   jax.ShapeDtypeStruct((b, A_HEADS, l, 1), F32)],
        compiler_params=_params(("parallel", "parallel")),
    )(qkv, qkv, qkv, qkv, qkv, qkv, qkv, slopes, sinks, bias)


def _swa_bwd(name, qkv, o, lse, do, slopes, sinks, bias, dqkv):
    b, l, _ = qkv.shape
    nb = l // BLOCK

    def body(q_ref, k0_ref, kp_ref, kc_ref, v0_ref, vp_ref, vc_ref, o_ref, lse_ref, do_ref, sl_ref, sk_ref, bias_ref, _,
             dx_ref, ds_ref, dk_acc, dv_acc):
        bi = pl.program_id(0)
        n = pl.program_id(1)
        qs = _stack_heads(q_ref[...], True) * SCALE
        dos32 = _stack_heads(do_ref[...], True)
        dos = dos32.astype(BF16)
        os_ = _stack_heads(o_ref[...], False)
        lsev = lse_ref[...].reshape(STACK, 1)
        kcat = jnp.concatenate([k0_ref[...], kp_ref[...], kc_ref[...]], axis=0)
        vcat = jnp.concatenate([v0_ref[...], vp_ref[...], vc_ref[...]], axis=0)
        s = _swa_scores(qs, kcat, n, sl_ref[...], bias_ref[...])
        p = jnp.exp(s - lsev)
        dsum = jnp.sum(dos32 * os_, axis=-1, keepdims=True)
        dp = lax.dot_general(dos, vcat, _NT, preferred_element_type=F32)
        dsc = (p * (dp - dsum)).astype(BF16)
        dq = lax.dot_general(dsc, kcat, _NN, preferred_element_type=F32) * SCALE
        row0 = pl.multiple_of(n * BLOCK, BLOCK)
        dx_ref[pl.ds(row0, BLOCK), 0:A_WIDTH] = _unstack_heads(dq).astype(BF16)
        dkc = lax.dot_general(dsc, qs, _TN, preferred_element_type=F32)
        dvc = lax.dot_general(p.astype(BF16), dos, _TN, preferred_element_type=F32)

        @pl.when(n == 0)
        def _():
            dk_acc[...] = jnp.zeros_like(dk_acc)
            dv_acc[...] = jnp.zeros_like(dv_acc)

        starts = (0, pl.multiple_of(jnp.maximum(n - 1, 0) * BLOCK, BLOCK), row0)
        for t, st in enumerate(starts):
            dk_acc[pl.ds(st, BLOCK), :] += dkc[t * BLOCK:(t + 1) * BLOCK, :]
            dv_acc[pl.ds(st, BLOCK), :] += dvc[t * BLOCK:(t + 1) * BLOCK, :]

        @pl.when(n == nb - 1)
        def _():
            dx_ref[:, A_WIDTH:A_WIDTH + 128] = dk_acc[...].astype(BF16)
            dx_ref[:, A_WIDTH + 128:A_SEG] = dv_acc[...].astype(BF16)

        dsink = -(jnp.exp(sk_ref[...] - lsev) * dsum)
        r8 = lax.broadcasted_iota(jnp.int32, (8, 128), 0)
        acc = jnp.zeros((8, 128), F32)
        for hh in range(A_HEADS):
            acc = acc + jnp.where(r8 == hh, jnp.sum(dsink[hh * BLOCK:(hh + 1) * BLOCK, :]), 0.0)

        @pl.when((bi == 0) & (n == 0))
        def _():
            ds_ref[...] = jnp.zeros_like(ds_ref)

        ds_ref[...] += acc

    q_spec, k_specs, v_specs, o_spec, consts, lse_spec = _swa_specs()
    return pl.pallas_call(
        body, name=name, grid=(b, nb),
        in_specs=[q_spec] + k_specs + v_specs + [o_spec, lse_spec, o_spec] + consts + [pl.BlockSpec(memory_space=pl.ANY)],
        out_specs=[pl.BlockSpec((None, l, A_SEG), lambda bb, n: (bb, 0, A_SEG_BLK)),
                   pl.BlockSpec((8, 128), lambda bb, n: (0, 0))],
        out_shape=[jax.ShapeDtypeStruct(dqkv.shape, BF16), jax.ShapeDtypeStruct((8, 128), F32)],
        scratch_shapes=[pltpu.VMEM((l, 128), F32), pltpu.VMEM((l, 128), F32)],
        input_output_aliases={13: 0},
        compiler_params=_params(("arbitrary", "arbitrary")),
    )(qkv, qkv, qkv, qkv, qkv, qkv, qkv, o, lse, do, slopes, sinks, bias, dqkv)


def _fox_mask(qk, ck, i):
    kh = qk.shape[1]
    qpos = i * BLOCK + lax.broadcasted_iota(jnp.int32, (BLOCK, kh), 0)
    kpos = lax.broadcasted_iota(jnp.int32, (BLOCK, kh), 1)
    return jnp.where((kpos <= qpos) & (kpos >= N_PAD), qk - ck, NEG)


def _pick_head(x, hh):
    lo = _lane_lo()
    return jnp.where(lo if hh == 0 else jnp.logical_not(lo), x, jnp.zeros_like(x))


def _both_heads(x):
    return jnp.concatenate([_pick_head(x, 0), _pick_head(x, 1)], axis=0)


def _fox_specs(l):
    pair = pl.BlockSpec((None, l, PAIR_W), lambda bi, hp: (bi, 0, hp))
    half = pl.BlockSpec((None, l, 128), lambda bi, hp: (bi, 0, hp))
    colv = pl.BlockSpec((None, 2, l, 1), lambda bi, hp: (bi, hp, 0, 0))
    rowv = pl.BlockSpec((None, 2, 1, l), lambda bi, hp: (bi, hp, 0, 0))
    return pair, half, colv, rowv


def _fox_fwd(name, qkv, c_col, c_row):
    b, l, _ = qkv.shape
    nb = l // BLOCK

    def body(x_ref, cc_ref, cr_ref, o_ref, lse_ref):
        for i in range(nb):
            rows = slice(i * BLOCK, (i + 1) * BLOCK)
            kh = (i + 1) * BLOCK
            qblk = x_ref[rows, 0:128]
            kv = x_ref[0:kh, 128:256]
            vv = x_ref[0:kh, 256:384]
            qk = lax.dot_general(_both_heads(qblk) * SCALE, kv, _NT, preferred_element_type=F32)
            ps, dens = [], []
            for hh in range(2):
                s = _fox_mask(qk[hh * BLOCK:(hh + 1) * BLOCK], cr_ref[hh, :, 0:kh], i)
                m = jnp.max(s, axis=-1, keepdims=True)
                p = jnp.exp(s - m)
                den = jnp.sum(p, axis=-1, keepdims=True)
                ps.append(p.astype(BF16))
                dens.append(den)
                lse_ref[hh, rows, :] = (m + jnp.log(den)) + cc_ref[hh, rows, :]
            pv = lax.dot_general(jnp.concatenate(ps, axis=0), vv, _NN, preferred_element_type=F32)
            o_ref[rows, :] = jnp.where(_lane_lo(), pv[0:BLOCK] / dens[0], pv[BLOCK:2 * BLOCK] / dens[1]).astype(BF16)

    pair, half, colv, rowv = _fox_specs(l)
    return pl.pallas_call(
        body, name=name, grid=(b, 4), in_specs=[pair, colv, rowv], out_specs=[half, colv],
        out_shape=[jax.ShapeDtypeStruct((b, l, B_WIDTH), BF16), jax.ShapeDtypeStruct((b, B_HEADS, l, 1), F32)],
        compiler_params=_params(("parallel", "parallel")),
    )(qkv, c_col, c_row)


def _fox_bwd(name, qkv, c_col, c_row, o, lse, do):
    b, l, _ = qkv.shape
    nb = l // BLOCK

    def body(x_ref, cc_ref, cr_ref, o_ref, lse_ref, do_ref, dx_ref, dcq_ref, dck_ref, dk_acc, dv_acc):
        dk_acc[...] = jnp.zeros_like(dk_acc)
        dv_acc[...] = jnp.zeros_like(dv_acc)
        dck_ref[...] = jnp.zeros_like(dck_ref)
        for i in range(nb):
            rows = slice(i * BLOCK, (i + 1) * BLOCK)
            kh = (i + 1) * BLOCK
            qblk = x_ref[rows, 0:128]
            kv = x_ref[0:kh, 128:256]
            vv = x_ref[0:kh, 256:384]
            doblk = do_ref[rows, :]
            ov = o_ref[rows, :].astype(F32)
            q2 = _both_heads(qblk) * SCALE
            do2 = _both_heads(doblk)
            qk = lax.dot_general(q2, kv, _NT, preferred_element_type=F32)
            dp = lax.dot_general(do2, vv, _NT, preferred_element_type=F32)
            ps, dss = [], []
            for hh in range(2):
                half = slice(hh * BLOCK, (hh + 1) * BLOCK)
                s = _fox_mask(qk[half], cr_ref[hh, :, 0:kh], i)
                p = jnp.exp(s - (lse_ref[hh, rows, :] - cc_ref[hh, rows, :]))
                dsum = jnp.sum(do2[half].astype(F32) * ov, axis=-1, keepdims=True)
                ds = p * (dp[half] - dsum)
                ps.append(p.astype(BF16))
                dss.append(ds.astype(BF16))
                dcq_ref[hh, rows, :] = jnp.sum(ds, axis=-1, keepdims=True)
                dck_ref[hh, :, 0:kh] -= jnp.sum(ds, axis=0, keepdims=True)
            p2 = jnp.concatenate(ps, axis=0)
            ds2 = jnp.concatenate(dss, axis=0)
            dq = lax.dot_general(ds2, kv, _NN, preferred_element_type=F32) * SCALE
            dk_acc[0:kh, :] += lax.dot_general(ds2, q2, _TN, preferred_element_type=F32)
            dv_acc[0:kh, :] += lax.dot_general(p2, do2, _TN, preferred_element_type=F32)
            dx_ref[rows, 0:128] = jnp.where(_lane_lo(), dq[0:BLOCK], dq[BLOCK:2 * BLOCK]).astype(BF16)
        dx_ref[:, 128:256] = dk_acc[...].astype(BF16)
        dx_ref[:, 256:384] = dv_acc[...].astype(BF16)

    pair, half, colv, rowv = _fox_specs(l)
    return pl.pallas_call(
        body, name=name, grid=(b, 4), in_specs=[pair, colv, rowv, half, colv, half],
        out_specs=[pair, colv, rowv],
        out_shape=[jax.ShapeDtypeStruct(qkv.shape, BF16), jax.ShapeDtypeStruct((b, B_HEADS, l, 1), F32),
                   jax.ShapeDtypeStruct((b, B_HEADS, 1, l), F32)],
        scratch_shapes=[pltpu.VMEM((l, 128), F32), pltpu.VMEM((l, 128), F32)],
        compiler_params=_params(("parallel", "parallel")),
    )(qkv, c_col, c_row, o, lse, do)


_FLIPS = ((0, 0, 1), (0, 1, 0), (0, 1, 1), (1, 0, 0), (1, 0, 1), (1, 1, 0), (1, 1, 1))


def _peer_table():
    x, y, c = lax.axis_index("x"), lax.axis_index("y"), lax.axis_index("c")
    me = 4 * x + 2 * y + c
    peers = []
    for fx, fy, fc in _FLIPS:
        px = 1 - x if fx else x
        py = 1 - y if fy else y
        pc = 1 - c if fc else c
        peers.append(((px, py, pc), 4 * px + 2 * py + pc))
    return me, peers


_HBM = pl.BlockSpec(memory_space=pltpu.HBM)
_SEM = pl.BlockSpec(memory_space=pltpu.SEMAPHORE)
_ANY = pl.BlockSpec(memory_space=pl.ANY)
_EFFECT = pltpu.SideEffectType.DATAFLOW_SIDE_EFFECTING


def _split_copy(srcs_are_pieces, src_refs, land_refs, send_sem, recv_sem, a, kk, me, peers, arriving):
    dev, lin = peers[kk]
    npeer = len(_FLIPS)
    src = src_refs[a] if srcs_are_pieces[a] else src_refs[a].at[lin]
    dst = land_refs[a].at[lin] if arriving else land_refs[a].at[me]
    return pltpu.make_async_remote_copy(src_ref=src, dst_ref=dst, send_sem=send_sem.at[a * npeer + kk],
                                        recv_sem=recv_sem.at[a * npeer + kk], device_id=dev, device_id_type=MESH_ID)


def _xchg_start(name, gather, scatter, after=None):
    me_out = 4 * lax.axis_index("x") + 2 * lax.axis_index("y") + lax.axis_index("c")
    srcs = list(gather) + list(scatter)
    is_piece = [True] * len(gather) + [False] * len(scatter)
    lands = []
    for a, piece in zip(srcs, is_piece):
        if piece:
            lands.append(lax.dynamic_update_slice(lax.empty((N_DEV,) + tuple(a.shape), a.dtype), a[None],
                                                  (me_out,) + (0,) * a.ndim))
        else:
            lands.append(lax.empty(tuple(a.shape), a.dtype))
    n = len(srcs)
    nsem = n * len(_FLIPS)
    has_after = after is not None

    def body(*refs):
        src_refs = refs[:n]
        land_refs = refs[n:2 * n]
        outs = refs[2 * n + (1 if has_after else 0):]
        send_sem, recv_sem = outs[0], outs[1]
        token = outs[-1]
        me, peers = _peer_table()
        for kk in range(len(_FLIPS)):
            for a in range(n):
                _split_copy(is_piece, src_refs, land_refs, send_sem, recv_sem, a, kk, me, peers, False).start()
        token[...] = jnp.zeros_like(token)

    out_shape = ([pltpu.SemaphoreType.DMA((nsem,)), pltpu.SemaphoreType.DMA((nsem,))]
                 + [pltpu.HBM(tuple(a.shape), a.dtype) for a in srcs] + [pltpu.HBM(tuple(a.shape), a.dtype) for a in lands]
                 + [jax.ShapeDtypeStruct((8, 128), F32)])
    args = [pltpu.with_memory_space_constraint(a, pltpu.HBM) for a in srcs + lands] + ([after] if has_after else [])
    res = pl.pallas_call(
        body, name=name, out_shape=out_shape,
        in_specs=[_HBM] * (2 * n) + ([_ANY] if has_after else []),
        out_specs=[_SEM, _SEM] + [_HBM] * (2 * n) + [pl.BlockSpec(memory_space=pltpu.VMEM)],
        input_output_aliases={i: 2 + i for i in range(2 * n)},
        compiler_params=pltpu.CompilerParams(has_side_effects=_EFFECT),
    )(*args)
    state = (res[0], res[1], list(res[2:2 + n]), list(res[2 + n:2 + 2 * n]), is_piece)
    return state, res[-1]


def _xchg_wait(name, state, after):
    send_sem, recv_sem, srcs, lands, is_piece = state
    n = len(srcs)

    def body(*refs):
        src_refs = refs[:n]
        land_refs = refs[n:2 * n]
        s_sem, r_sem = refs[2 * n], refs[2 * n + 1]
        me, peers = _peer_table()
        for kk in range(len(_FLIPS)):
            for a in range(n):
                cp = _split_copy(is_piece, src_refs, land_refs, s_sem, r_sem, a, kk, me, peers, True)
                cp.wait_send()
                cp.wait_recv()

    out_shape = [pltpu.HBM(tuple(a.shape), a.dtype) for a in srcs] + [pltpu.HBM(tuple(a.shape), a.dtype) for a in lands]
    res = pl.pallas_call(
        body, name=name, out_shape=out_shape,
        in_specs=[_HBM] * (2 * n) + [_SEM, _SEM, _ANY], out_specs=[_HBM] * (2 * n),
        input_output_aliases={i: i for i in range(2 * n)},
        compiler_params=pltpu.CompilerParams(has_side_effects=_EFFECT),
    )(*srcs, *lands, send_sem, recv_sem, after)
    return list(res[n:]), list(res[:n])


_SIB = (0, 0, 1)
_ICI = ((0, 1, 0), (1, 0, 0), (1, 1, 0))


def _flip(fl):
    x, y, c = lax.axis_index("x"), lax.axis_index("y"), lax.axis_index("c")
    px = 1 - x if fl[0] else x
    py = 1 - y if fl[1] else y
    pc = 1 - c if fl[2] else c
    return (px, py, pc), 4 * px + 2 * py + pc


def _gather2_start(name, pieces, after=None):
    me_out = 4 * lax.axis_index("x") + 2 * lax.axis_index("y") + lax.axis_index("c")
    pieces = list(pieces)
    n = len(pieces)
    lands = [lax.dynamic_update_slice(lax.empty((N_DEV,) + tuple(a.shape), a.dtype), a[None],
                                      (me_out,) + (0,) * a.ndim) for a in pieces]
    first = (_SIB,) + _ICI
    has_after = after is not None

    def body(*refs):
        src_refs, land_refs = refs[:n], refs[n:2 * n]
        outs = refs[2 * n + (1 if has_after else 0):]
        send_sem, recv_sem, token = outs[0], outs[1], outs[-1]
        _, me = _flip((0, 0, 0))
        for kk, fl in enumerate(first):
            dev, _ = _flip(fl)
            for a in range(n):
                pltpu.make_async_remote_copy(src_ref=src_refs[a], dst_ref=land_refs[a].at[me],
                                             send_sem=send_sem.at[a * 4 + kk], recv_sem=recv_sem.at[a * 4 + kk],
                                             device_id=dev, device_id_type=MESH_ID).start()
        token[...] = jnp.zeros_like(token)

    hbm = [pltpu.HBM(tuple(a.shape), a.dtype) for a in pieces + lands]
    res = pl.pallas_call(
        body, name=name,
        out_shape=[pltpu.SemaphoreType.DMA((4 * n,)), pltpu.SemaphoreType.DMA((4 * n,))] + hbm
        + [jax.ShapeDtypeStruct((8, 128), F32)],
        in_specs=[_HBM] * (2 * n) + ([_ANY] if has_after else []),
        out_specs=[_SEM, _SEM] + [_HBM] * (2 * n) + [pl.BlockSpec(memory_space=pltpu.VMEM)],
        input_output_aliases={i: 2 + i for i in range(2 * n)},
        compiler_params=pltpu.CompilerParams(has_side_effects=_EFFECT),
    )(*([pltpu.with_memory_space_constraint(a, pltpu.HBM) for a in pieces + lands] + ([after] if has_after else [])))
    return (res[0], res[1], list(res[2:2 + n]), list(res[2 + n:2 + 2 * n])), res[-1]


def _gather2_forward(name, state, after):
    send_a, recv_a, pieces, lands = state
    n = len(pieces)
    first = (_SIB,) + _ICI

    def body(*refs):
        src_refs, land_refs = refs[:n], refs[n:2 * n]
        s_a, r_a = refs[2 * n], refs[2 * n + 1]
        outs = refs[2 * n + 3:]
        send_b, recv_b, token = outs[0], outs[1], outs[-1]
        for kk, fl in enumerate(first):
            dev, lin = _flip(fl)
            for a in range(n):
                cp = pltpu.make_async_remote_copy(src_ref=src_refs[a], dst_ref=land_refs[a].at[lin],
                                                  send_sem=s_a.at[a * 4 + kk], recv_sem=r_a.at[a * 4 + kk],
                                                  device_id=dev, device_id_type=MESH_ID)
                cp.wait_send()
                cp.wait_recv()
        sib, _ = _flip(_SIB)
        for j, fl in enumerate(_ICI):
            _, lin = _flip(fl)
            for a in range(n):
                pltpu.make_async_remote_copy(src_ref=land_refs[a].at[lin], dst_ref=land_refs[a].at[lin],
                                             send_sem=send_b.at[a * 3 + j], recv_sem=recv_b.at[a * 3 + j],
                                             device_id=sib, device_id_type=MESH_ID).start()
        token[...] = jnp.zeros_like(token)

    hbm = [pltpu.HBM(tuple(a.shape), a.dtype) for a in pieces + lands]
    res = pl.pallas_call(
        body, name=name,
        out_shape=[pltpu.SemaphoreType.DMA((3 * n,)), pltpu.SemaphoreType.DMA((3 * n,))] + hbm
        + [jax.ShapeDtypeStruct((8, 128), F32)],
        in_specs=[_HBM] * (2 * n) + [_SEM, _SEM, _ANY],
        out_specs=[_SEM, _SEM] + [_HBM] * (2 * n) + [pl.BlockSpec(memory_space=pltpu.VMEM)],
        input_output_aliases={i: 2 + i for i in range(2 * n)},
        compiler_params=pltpu.CompilerParams(has_side_effects=_EFFECT),
    )(*pieces, *lands, send_a, recv_a, after)
    return (res[0], res[1], list(res[2 + n:2 + 2 * n])), res[-1]


def _gather2_wait(name, state, after):
    send_b, recv_b, lands = state
    n = len(lands)

    def body(*refs):
        land_refs = refs[:n]
        s_b, r_b = refs[n], refs[n + 1]
        sib, _ = _flip(_SIB)
        for j, fl in enumerate(_ICI):
            _, sent = _flip(fl)
            _, arriving = _flip((fl[0], fl[1], 1))
            for a in range(n):
                cp = pltpu.make_async_remote_copy(src_ref=land_refs[a].at[sent], dst_ref=land_refs[a].at[arriving],
                                                  send_sem=s_b.at[a * 3 + j], recv_sem=r_b.at[a * 3 + j],
                                                  device_id=sib, device_id_type=MESH_ID)
                cp.wait_send()
                cp.wait_recv()

    res = pl.pallas_call(
        body, name=name, out_shape=[pltpu.HBM(tuple(a.shape), a.dtype) for a in lands],
        in_specs=[_HBM] * n + [_SEM, _SEM, _ANY], out_specs=[_HBM] * n,
        input_output_aliases={i: i for i in range(n)},
        compiler_params=pltpu.CompilerParams(has_side_effects=_EFFECT),
    )(*lands, send_b, recv_b, after)
    return list(res)


def _adam_math(w, g, m, v):
    m = ADAM_B1 * m + (1.0 - ADAM_B1) * g
    v = ADAM_B2 * v + (1.0 - ADAM_B2) * (g * g)
    m_hat = m / (1.0 - ADAM_B1 ** ADAM_STEP)
    v_hat = v / (1.0 - ADAM_B2 ** ADAM_STEP)
    delta = -ADAM_LR * (m_hat / (jnp.sqrt(v_hat) + ADAM_EPS) + ADAM_WD * w)
    return delta, m, v


def _adam(name, w, m, v, parts, transposed=False, dep=None, own=None):
    npart, _, cp = parts.shape
    has_dep = dep is not None
    has_own = own is not None
    if transposed:
        c, r = w.shape
        tr = _pick(r, (256, 128))
        blk = pl.BlockSpec((c, tr), lambda i, *_: (0, i))
    else:
        r, c = w.shape
        tr = _pick(r, (256, 176, 128, 64, 16, 8, 1))
        blk = pl.BlockSpec((tr, c), lambda i, *_: (i, 0))

    def body(*refs):
        refs = list(refs)
        me = refs.pop(0)[0] if has_own else None
        w_ref, m_ref, v_ref, p_ref = refs[:4]
        own_ref = refs[4] if has_own else None
        g_ref, d_ref, mo_ref, vo_ref = refs[-4:]
        g = None
        for pp in range(npart):
            part = p_ref[pp] if not has_own else jnp.where(me == pp, own_ref[...], p_ref[pp])
            g = part.astype(F32) if g is None else g + part.astype(F32)
        g = g.T[0:c, :] if transposed else g[:, 0:c]
        delta, mn, vn = _adam_math(w_ref[...], g, m_ref[...], v_ref[...])
        g_ref[...] = g
        d_ref[...] = delta
        mo_ref[...] = mn
        vo_ref[...] = vn

    in_specs = [blk, blk, blk, pl.BlockSpec((npart, tr, cp), lambda i, *_: (0, i, 0))]
    args = [w, m, v, parts]
    if has_own:
        in_specs.append(pl.BlockSpec((None, tr, cp), lambda i, me_ref: (me_ref[0], i, 0)))
        args.append(own[0])
    if has_dep:
        in_specs.append(pl.BlockSpec(memory_space=pl.ANY))
        args.append(dep)
    out = jax.ShapeDtypeStruct(w.shape, F32)
    grid_spec = pltpu.PrefetchScalarGridSpec(num_scalar_prefetch=1 if has_own else 0, grid=(r // tr,),
                                             in_specs=in_specs, out_specs=[blk, blk, blk, blk])
    return pl.pallas_call(
        body, name=name, grid_spec=grid_spec, out_shape=[out, out, out, out], compiler_params=_params(("parallel",)),
    )(*(([own[1]] if has_own else []) + args))


def _small_update(name, packs, wmv):
    nparam = len(wmv)

    def body(p_ref, *refs):
        ins, outs = refs[:3 * nparam], refs[3 * nparam:]
        tot = p_ref[0]
        for pp in range(1, N_DEV):
            tot = tot + p_ref[pp]
        outs[0][0:8, :] = tot[0:8, :]
        outs[0][8:24, :] = tot[8:24, :] + tot[24:40, :]
        grads = [tot[i:i + 1, :] for i in range(4)] + [tot[4:5, 0:B_HEADS], tot[4:5, B_HEADS:2 * B_HEADS]]
        for i, g in enumerate(grads):
            w_ref, m_ref, v_ref = ins[3 * i:3 * i + 3]
            delta, mn, vn = _adam_math(w_ref[...], g, m_ref[...], v_ref[...])
            for o_ref, val in zip(outs[1 + 4 * i:5 + 4 * i], (g, delta, mn, vn)):
                o_ref[...] = val

    flat = [a for trio in wmv for a in trio]
    out_shape = [jax.ShapeDtypeStruct((24, D_MODEL), F32)]
    for w, _, _ in wmv:
        out_shape += [jax.ShapeDtypeStruct(w.shape, F32)] * 4
    res = pl.pallas_call(body, name=name, out_shape=out_shape, compiler_params=_params())(packs, *flat)
    return res[0], [tuple(res[1 + 4 * i:5 + 4 * i]) for i in range(nparam)]


def _local_step(x, tgt, g1, gm, g2, gf, b_forget, sinks, weights, send):
    b, s, _ = x.shape
    l = s + PREFIX
    t = b * l
    (meta,) = weights("meta", x)

    h0, n1 = _embed_norm("embed_rms1_fwd", x, meta, g1)
    (w1i,) = weights("ffn1_in", n1)
    gu1, a1 = _ffn_in_fwd("ffn1_in_fwd", n1, w1i)
    (w1o,) = weights("ffn1_out", weights("ffn1_out:forward", a1))
    h1, um = _mm_res_norm("ffn1_out_fwd", a1, w1o, h0, gm, alpha=0.5)
    wi, wa, wb, wo = weights("mix", weights("mix:forward", um))
    qkv, gates, f2 = _proj_fwd("proj_fwd", um, wi)
    qkv3 = qkv.reshape(b, l, QKV_W)
    f3 = f2.reshape(b, l, 128)
    bf_row = jnp.pad(b_forget, ((0, 0), (0, 128 - B_HEADS)))
    c_col, c_row = _fgate_fwd("fgate_fwd", f3, bf_row)
    head_of_row = np.arange(STACK) // BLOCK
    slopes_np = np.exp2(-8.0 * (head_of_row + 1) / A_HEADS).astype(np.float32).reshape(STACK, 1)
    slopes = jnp.asarray(slopes_np)
    sink_rows = jnp.repeat(sinks.reshape(A_HEADS), BLOCK).reshape(STACK, 1)
    swa_bias = _swa_bias(slopes_np)
    oa3, oa3_bf, lse_a = _swa_fwd("swa_fwd", qkv3, slopes, sink_rows, swa_bias)
    ob3, lse_b = _fox_fwd("fox_fwd", qkv3, c_col, c_row)
    oa = oa3_bf.reshape(t, A_WIDTH)
    ob = ob3.reshape(t, B_WIDTH)
    mixed, ya, yb = _branch_gate_fwd("branch_gate_fwd", oa, ob, wa, wb, gates, dep=weights("ffn2:forward", ob))
    h2, n2 = _mm_res_norm("mix_out_fwd", mixed, wo, h1, g2)
    w2i, w2o = weights("ffn2", h2)
    gu2, a2 = _ffn_in_fwd("ffn2_in_fwd", n2, w2i)

    dh3, dh3_bf, loss_blk, dgf = _ffn_out_loss("ffn2_out_loss", a2, w2o, h2, gf, tgt, 0.5)

    def ffn_bwd(tag, dh, dh_bf, h_in, g_norm, n_in, gu, a, w_in_blk, w_out, one_send, examples=None):
        dw_out = _ffn_out_bwd_w(tag + "_out_bwd_w", a, dh_bf)
        dgu = _ffn_out_bwd_x(tag + "_out_bwd_x", dh_bf, w_out, gu, dep=None if one_send else send(tag + "_out", (dw_out,)))
        dw_in = _ffn_in_bwd_w(tag + "_in_bwd_w", n_in, dgu)
        token = send(tag, (dw_in, dw_out)) if one_send else send(tag + "_in", (dw_in,))
        return _ffn_in_bwd_x(tag + "_in_bwd_x", dgu, w_in_blk, h_in, g_norm, dh, dep=token, examples=examples)

    dh2, dh2_bf, dg2 = ffn_bwd("ffn2", dh3, dh3_bf, h2, g2, n2, gu2, a2, w2i, w2o, True)

    dwo = _mm_tn("mix_out_bwd_w", mixed, dh2_bf)
    dya, dyb, dgates = _mix_out_gate_bwd("mix_out_gate_bwd", dh2_bf, wo, gates, ya, yb)
    doa, dob = _branch_bwd_x("branch_bwd_x", dya, dyb, wa, wb)
    dwa, dwb = _branch_bwd_w("branch_bwd_w", oa, ob, dya, dyb)
    dqkv3, dcq, dck = _fox_bwd("fox_bwd", qkv3, c_col, c_row, ob3, lse_b, dob.reshape(b, l, B_WIDTH))
    dqkv3, dsink = _swa_bwd("swa_bwd", qkv3, oa3, lse_a, doa.reshape(b, l, A_WIDTH), slopes, sink_rows, swa_bias, dqkv3)
    dqkv = dqkv3.reshape(t, QKV_W)
    df3, dbf = _fgate_bwd("fgate_bwd", f3, bf_row, dcq, dck)
    df = df3.reshape(t, 128)
    dwi_qkv = _mm_tn("proj_qkv_bwd_w", um, dqkv, tm_c=(512,), tn_c=(768,))
    dwi_g = _mm_tn("proj_gates_bwd_w", um, dgates, tm_c=(512,), tn_c=(512,))
    dwi_f = _mm_tn("proj_f_bwd_w", um, df, tm_c=(512,), tn_c=(128,))
    token = send("mix", (dwi_qkv, dwi_g, dwi_f, dwa, dwb, dwo))
    dh1, dh1_bf, dgm = _proj_bwd_x("proj_bwd_x", dqkv, dgates, df, wi, h1, gm, dh2, dep=token)

    grad_x, dmeta3, dg1 = ffn_bwd("ffn1", dh1, dh1_bf, h0, g1, n1, gu1, a1, w1i, w1o, False, examples=b)
    dmeta = dmeta3.reshape(b * N_META, D_MODEL)

    misc = jnp.concatenate([dbf[:, 0:B_HEADS], dsink[:, 0].reshape(1, A_HEADS), loss_blk[0:1, 0:1]], axis=1)
    misc = jnp.pad(misc, ((0, 0), (0, D_MODEL - misc.shape[1])))
    row = lax.broadcasted_iota(jnp.int32, (8, D_MODEL), 0)
    vec = jnp.zeros((8, D_MODEL), F32)
    for i, piece in enumerate((dg1, dgm, dg2, dgf, misc)):
        vec = jnp.where(row == i, piece, vec)
    small = jnp.concatenate([vec, dmeta], axis=0)
    return grad_x, small


def _pad_to(a, rows, cols):
    return jnp.pad(a, ((0, rows - a.shape[0]), (0, cols - a.shape[1])))


def _ffn_out_from_gathered(name, g):
    def body(g_ref, o_ref):
        o_ref[0:FFO_SHARD, :] = g_ref[0]
        o_ref[FFO_SHARD:FF_SHARD, :] = g_ref[1]
        o_ref[FF_SHARD:FF_SHARD_P, :] = jnp.zeros((FF_SHARD_P - FF_SHARD, D_MODEL), BF16)

    return pl.pallas_call(
        body, name=name, grid=(4,),
        in_specs=[pl.BlockSpec((2, FFO_SHARD, D_MODEL), lambda j: (j, 0, 0))],
        out_specs=pl.BlockSpec((FF_SHARD_P, D_MODEL), lambda j: (j, 0)),
        out_shape=jax.ShapeDtypeStruct((D_FF_P, D_MODEL), BF16), compiler_params=_params(("parallel",)),
    )(g)


def _ffn_out_bwd_w(name, a, dh):
    t = a.shape[0]

    def body(a_ref, b_ref, o_ref):
        acc = lax.dot_general(a_ref[...], _bf(b_ref[...]), _TN, preferred_element_type=F32) * 0.5
        o_ref[0] = acc[0:FFO_SHARD].astype(BF16)
        o_ref[1] = acc[FFO_SHARD:FF_SHARD].astype(BF16)

    return pl.pallas_call(
        body, name=name, grid=(4,),
        in_specs=[pl.BlockSpec((t, FF_SHARD_P), lambda i: (0, i)),
                  pl.BlockSpec((t, D_MODEL), lambda i: (0, 0), pipeline_mode=pl.Buffered(1))],
        out_specs=pl.BlockSpec((2, FFO_SHARD, D_MODEL), lambda i: (i, 0, 0)),
        out_shape=jax.ShapeDtypeStruct((N_DEV, FFO_SHARD, D_MODEL), BF16), compiler_params=_params(("parallel",)),
    )(a, dh)


def _proj_segments():
    segs = [(HEAD_DIM * h, HEAD_DIM, 0, B_SEG + HEAD_DIM * A_HEAD_ORDER.index(h)) for h in range(A_HEADS)]
    segs += [(512, 128, 0, B_SEG + A_WIDTH), (640, 128, 0, B_SEG + A_WIDTH + 128)]
    for first, off in ((768, 0), (1280, 128), (1792, 256)):
        segs += [(first + 128 * hp, 128, 0, PAIR_W * hp + off) for hp in range(4)]
    segs += [(2304, B_HEADS, 2, 0), (2312, 2 * D_MODEL, 1, 0)]
    return segs


_RELAYOUT_ROWS = 256


def _proj_from_gathered(name, g):
    rows = _RELAYOUT_ROWS

    def body(g_ref, o_ref):
        def cols(first, width):
            out = []
            for p in range(N_DEV):
                lo, hi = max(first, WIN_SHARD * p), min(first + width, WIN_SHARD * (p + 1))
                if lo < hi:
                    out.append(g_ref[p, :, lo - WIN_SHARD * p:hi - WIN_SHARD * p])
            return out

        parts = []
        for arr in (0, 1, 2):
            for first, width, _, _ in sorted((s for s in _proj_segments() if s[2] == arr), key=lambda s: s[3]):
                parts += cols(first, width)
            if arr == 0:
                parts.append(jnp.zeros((rows, P_GATES - QKV_W), BF16))
        parts.append(jnp.zeros((rows, 128 - B_HEADS), BF16))
        o_ref[...] = jnp.concatenate(parts, axis=1)

    return pl.pallas_call(
        body, name=name, grid=(D_MODEL // rows,),
        in_specs=[pl.BlockSpec((N_DEV, rows, WIN_SHARD_P), lambda i: (0, i, 0))],
        out_specs=pl.BlockSpec((rows, PROJ_P), lambda i: (i, 0)),
        out_shape=jax.ShapeDtypeStruct((D_MODEL, PROJ_P), BF16), compiler_params=_params(("parallel",)),
    )(g)


def _proj_to_scatter(name, dqkv_w, dg_w, df_w):
    rows = _RELAYOUT_ROWS
    segs = sorted(_proj_segments())

    def body(q_ref, g_ref, f_ref, o_ref):
        arrays = (q_ref, g_ref, f_ref)
        for p in range(N_DEV):
            parts = []
            for first, width, arr, at in segs:
                lo, hi = max(first, WIN_SHARD * p), min(first + width, WIN_SHARD * (p + 1))
                if lo < hi:
                    parts.append(arrays[arr][:, at + lo - first:at + hi - first])
            parts.append(jnp.zeros((rows, WIN_SHARD_P - WIN_SHARD), BF16))
            o_ref[p] = jnp.concatenate(parts, axis=1)

    return pl.pallas_call(
        body, name=name, grid=(D_MODEL // rows,),
        in_specs=[pl.BlockSpec((rows, QKV_W), lambda i: (i, 0)), pl.BlockSpec((rows, 2 * D_MODEL), lambda i: (i, 0)),
                  pl.BlockSpec((rows, 128), lambda i: (i, 0))],
        out_specs=pl.BlockSpec((N_DEV, rows, WIN_SHARD_P), lambda i: (0, i, 0)),
        out_shape=jax.ShapeDtypeStruct((N_DEV, D_MODEL, WIN_SHARD_P), BF16), compiler_params=_params(("parallel",)),
    )(dqkv_w, dg_w, df_w)


def _a_rows_from_natural(w):
    return jnp.concatenate([w[HEAD_DIM * h:HEAD_DIM * (h + 1)] for h in A_HEAD_ORDER], axis=0)


def _a_rows_to_natural(w):
    return jnp.concatenate([w[HEAD_DIM * A_HEAD_ORDER.index(h):HEAD_DIM * (A_HEAD_ORDER.index(h) + 1)]
                            for h in range(A_HEADS)], axis=0)


def kernel(x, meta_tokens, ffn1_norm, ffn1_w_in, ffn1_w_out, mix_norm, w_in, b_forget, attn_sinks, w_branch_a, w_branch_b, w_out, ffn2_norm, ffn2_w_in, ffn2_w_out, final_norm, loss_target, m_meta_tokens, m_ffn1_norm, m_ffn1_w_in, m_ffn1_w_out, m_mix_norm, m_w_in, m_b_forget, m_attn_sinks, m_w_branch_a, m_w_branch_b, m_w_out, m_ffn2_norm, m_ffn2_w_in, m_ffn2_w_out, m_final_norm, v_meta_tokens, v_ffn1_norm, v_ffn1_w_in, v_ffn1_w_out, v_mix_norm, v_w_in, v_b_forget, v_attn_sinks, v_w_branch_a, v_w_branch_b, v_w_out, v_ffn2_norm, v_ffn2_w_in, v_ffn2_w_out, v_final_norm):
    me = 4 * lax.axis_index("x") + 2 * lax.axis_index("y") + lax.axis_index("c")

    def shard(w, tok, rows=None, cols=None):
        piece = (w[0] if tok is None else w[0] + tok[0, 0]).astype(BF16)
        return piece if rows is None else _pad_to(piece, rows, cols)

    first_level, second_level = {}, {}
    first_level["meta"], tok = _gather2_start("gather_meta_start", (meta_tokens,))
    first_level["ffn1_in"], tok = _gather2_start(
        "gather_ffn1_in_start", (shard(ffn1_w_in, None, D_MODEL, FF_SHARD_P),), after=tok)
    first_level["ffn1_out"], tok = _gather2_start("gather_ffn1_out_start", (shard(ffn1_w_out, tok),), after=tok)
    first_level["mix"], tok = _gather2_start(
        "gather_mix_start", (shard(w_in, tok, D_MODEL, WIN_SHARD_P), shard(w_branch_a, tok), shard(w_branch_b, tok),
                             shard(w_out, tok)), after=tok)
    first_level["ffn2"], tok = _gather2_start(
        "gather_ffn2_start", (shard(ffn2_w_in, tok, D_MODEL, FF_SHARD_P), shard(ffn2_w_out, tok)), after=tok)
    started = {"tok": tok}

    def weights(group, after):
        if group.endswith(":forward"):
            group = group[:-len(":forward")]
            second_level[group], token = _gather2_forward("gather_" + group + "_forward", first_level[group], after)
            return token
        if group == "meta":
            after = weights("meta:forward", started["tok"])
        if group == "ffn1_in":
            after = weights("ffn1_in:forward", after)
        got = _gather2_wait("gather_" + group + "_wait", second_level[group], after)
        if group == "mix":
            gwi, gwa, gwb, gwo = got
            return (_proj_from_gathered("proj_w_relayout", gwi), _a_rows_from_natural(gwa.transpose(1, 0, 2).reshape(A_WIDTH, D_MODEL)),
                    gwb.transpose(1, 0, 2).reshape(B_WIDTH, D_MODEL), gwo.reshape(D_MODEL, D_MODEL))
        if group == "ffn1_out":
            return (_ffn_out_from_gathered("ffn1_w_out_relayout", got[0]),)
        if group == "meta":
            return (got[0].transpose(1, 0, 2).reshape(N_META, D_MODEL),)
        if group == "ffn1_in":
            return (got[0].reshape(2, 4, D_MODEL, FF_SHARD_P),)
        return got[0].reshape(2, 4, D_MODEL, FF_SHARD_P), _ffn_out_from_gathered("ffn2_w_out_relayout", got[1])

    scatter_state = {}

    def send(group, grads):
        if group == "mix":
            dwi_qkv, dwi_g, dwi_f, dwa, dwb, dwo = grads
            dwa = _a_rows_to_natural(dwa)
            blocks = (_proj_to_scatter("proj_dw_relayout", dwi_qkv, dwi_g, dwi_f), dwa.reshape(A_WIDTH, N_DEV, 128).transpose(1, 0, 2),
                      dwb.reshape(B_WIDTH, N_DEV, 128).transpose(1, 0, 2), dwo.reshape(N_DEV, 128, D_MODEL))
        elif group.endswith("_in"):
            blocks = (grads[0].reshape(N_DEV, FF_SHARD_P, D_MODEL),)
        elif group.endswith("_out"):
            blocks = (grads[0],)
        else:
            blocks = (grads[0].reshape(N_DEV, FF_SHARD_P, D_MODEL), grads[1])
        scatter_state[group], token = _xchg_start("scatter_" + group + "_start", (), blocks)
        return token

    gf = final_norm.reshape(1, D_MODEL)
    grad_x, small = _local_step(x, loss_target, ffn1_norm, mix_norm, ffn2_norm, gf, b_forget, attn_sinks, weights, send)

    small_state, after = _xchg_start("gather_small_start", (small,), ())
    out = {}
    updates = (
        ("ffn2", (("ffn2_w_in", ffn2_w_in, m_ffn2_w_in, v_ffn2_w_in), ("ffn2_w_out", ffn2_w_out, m_ffn2_w_out, v_ffn2_w_out))),
        ("ffn1_out", (("ffn1_w_out", ffn1_w_out, m_ffn1_w_out, v_ffn1_w_out),)),
        ("mix", (("w_in", w_in, m_w_in, v_w_in), ("w_branch_a", w_branch_a, m_w_branch_a, v_w_branch_a),
                 ("w_branch_b", w_branch_b, m_w_branch_b, v_w_branch_b), ("w_out", w_out, m_w_out, v_w_out))),
    )
    last_update = ("ffn1_in", (("ffn1_w_in", ffn1_w_in, m_ffn1_w_in, v_ffn1_w_in),))

    def update(group, members, after):
        parts_list, blocks_list = _xchg_wait("scatter_" + group + "_wait", scatter_state[group], after)
        prev = None
        for (nm, w, m, v), parts, blocks in zip(members, parts_list, blocks_list):
            if nm.endswith("_w_in"):
                res4 = _adam("adam_" + nm, w[0].T, m[0].T, v[0].T, parts, dep=prev, own=(blocks, me_arr))
                out[nm] = tuple(r.T[None] for r in res4)
            elif nm == "w_in":
                res4 = _adam("adam_" + nm, w[0].T, m[0].T, v[0].T, parts, transposed=True, dep=prev, own=(blocks, me_arr))
                out[nm] = tuple(r.T[None] for r in res4)
            else:
                res4 = _adam("adam_" + nm, w[0], m[0], v[0], parts, dep=prev, own=(blocks, me_arr))
                out[nm] = tuple(r[None] for r in res4)
            prev = res4[0]
        return prev

    me_arr = me.reshape(1).astype(jnp.int32)
    for group, members in updates + (last_update,):
        after = update(group, members, after)
    (packs,), _ = _xchg_wait("gather_small_wait", small_state, after)

    row = lambda a: a.reshape(1, D_MODEL)
    tot, small_res = _small_update("small_update", packs, (
        (ffn1_norm, m_ffn1_norm, v_ffn1_norm), (mix_norm, m_mix_norm, v_mix_norm), (ffn2_norm, m_ffn2_norm, v_ffn2_norm),
        (row(final_norm), row(m_final_norm), row(v_final_norm)), (b_forget, m_b_forget, v_b_forget),
        (attn_sinks, m_attn_sinks, v_attn_sinks)))
    for nm, res4 in zip(("ffn1_norm", "mix_norm", "ffn2_norm", "final_norm", "b_forget", "attn_sinks"), small_res):
        out[nm] = tuple(r.reshape(D_MODEL) for r in res4) if nm == "final_norm" else res4
    loss = tot[4, 2 * B_HEADS]
    g_meta = lax.dynamic_slice(tot[8:24, :], (0, me * 128), (N_META, 128))
    out["meta_tokens"] = tuple(_adam("adam_meta_tokens", meta_tokens, m_meta_tokens, v_meta_tokens, g_meta[None]))

    names = ("meta_tokens", "ffn1_norm", "ffn1_w_in", "ffn1_w_out", "mix_norm", "w_in", "b_forget", "attn_sinks",
             "w_branch_a", "w_branch_b", "w_out", "ffn2_norm", "ffn2_w_in", "ffn2_w_out", "final_norm")
    return (loss, grad_x) + tuple(out[nm][kind] for kind in range(4) for nm in names)
```

```python
import jax
import jax.numpy as jnp
import numpy as np
from jax import lax
from jax.experimental import pallas as pl
from jax.experimental.pallas import tpu as pltpu

F32 = jnp.float32
BF16 = jnp.bfloat16

D_MODEL = 1024
N_META = 16
BLOCK = 128
PREFIX = 128
N_PAD = PREFIX - N_META
HEAD_DIM = 64
A_HEADS = 8
B_HEADS = 8
A_WIDTH = 512
A_KV_WIDTH = 128
B_WIDTH = 512
D_FF = 2816
N_DEV = 8
FF_SHARD = 2 * D_FF // N_DEV
FF_SHARD_P = 768
FFO_SHARD = D_FF // N_DEV
D_FF_P = 4 * FF_SHARD_P
W_IN_COLS = 4360
WIN_SHARD = W_IN_COLS // N_DEV
WIN_SHARD_P = 640
PAIR_W = 3 * 128
B_SEG = 4 * PAIR_W
A_SEG = A_WIDTH + 2 * A_KV_WIDTH
QKV_W = B_SEG + A_SEG
P_GATES = 2 * (2 * D_MODEL)
P_F = P_GATES + 2 * D_MODEL
PROJ_P = P_F + 128
A_HEAD_ORDER = (0, 4, 1, 5, 2, 6, 3, 7)
EPS = 1e-6
NEG = -1e30
SCALE = HEAD_DIM ** -0.5
ADAM_LR = 0.001
ADAM_B1 = 0.9
ADAM_B2 = 0.999
ADAM_EPS = 1e-08
ADAM_WD = 0.01
ADAM_STEP = 10
VMEM_LIMIT = 56 * 1024 * 1024
MESH_ID = pl.DeviceIdType.MESH
SMALL_ROWS = 40

_NN = (((1,), (0,)), ((), ()))
_NT = (((1,), (1,)), ((), ()))
_TN = (((0,), (0,)), ((), ()))


def _params(sem=None):
    return pltpu.CompilerParams(dimension_semantics=sem, vmem_limit_bytes=VMEM_LIMIT)


def _pick(n, cands):
    for c in cands:
        if n % c == 0:
            return c
    raise ValueError(f"no tile for {n}")


def _bf(v):
    return v if v.dtype == BF16 else v.astype(BF16)


def _mm(name, a, b, dims, grid, a_spec, b_spec, o_spec, out_shape, out_dtype, alpha=1.0):
    def body(a_ref, b_ref, o_ref):
        acc = lax.dot_general(_bf(a_ref[...]), _bf(b_ref[...]), dims, preferred_element_type=F32)
        if alpha != 1.0:
            acc = acc * alpha
        o_ref[...] = acc.astype(o_ref.dtype)

    return pl.pallas_call(
        body, name=name, grid=grid, in_specs=[a_spec, b_spec], out_specs=o_spec,
        out_shape=jax.ShapeDtypeStruct(out_shape, out_dtype),
        compiler_params=_params(("parallel",) * len(grid)),
    )(a, b)


def _mm_res_norm(name, a, w, res, g_next, alpha=1.0):
    t, k = a.shape
    tm = _pick(t, (544, 384, 256, 128))

    def body(a_ref, w_ref, r_ref, g_ref, h_ref, n_ref):
        acc = lax.dot_general(_bf(a_ref[...]), w_ref[...], _NN, preferred_element_type=F32)
        if alpha != 1.0:
            acc = acc * alpha
        hv = acc + r_ref[...]
        h_ref[...] = hv
        r = lax.rsqrt(jnp.mean(hv * hv, axis=-1, keepdims=True) + EPS)
        n_ref[...] = ((hv * r) * g_ref[...]).astype(BF16)

    row = pl.BlockSpec((tm, D_MODEL), lambda i: (i, 0))
    return pl.pallas_call(
        body, name=name, grid=(t // tm,),
        in_specs=[pl.BlockSpec((tm, k), lambda i: (i, 0)), pl.BlockSpec((k, D_MODEL), lambda i: (0, 0)), row,
                  pl.BlockSpec((1, D_MODEL), lambda i: (0, 0))],
        out_specs=[row, row],
        out_shape=[jax.ShapeDtypeStruct((t, D_MODEL), F32), jax.ShapeDtypeStruct((t, D_MODEL), BF16)],
        compiler_params=_params(("parallel",)),
    )(a, w, res, g_next)


def _mm_tn(name, a, b, out_dtype=BF16, alpha=1.0, tm_c=(768, 512, 256, 128), tn_c=(512, 640, 256, 128)):
    t, m = a.shape
    n = b.shape[1]
    tm = _pick(m, tm_c)
    tn = _pick(n, tn_c)
    bytes_a, bytes_b = a.size * a.dtype.itemsize, b.size * b.dtype.itemsize
    if bytes_a + bytes_b * (m // tm) <= bytes_b + bytes_a * (n // tn):
        return _mm(name, a, b, _TN, (m // tm, n // tn),
                   pl.BlockSpec((t, tm), lambda i, j: (0, i)), pl.BlockSpec((t, tn), lambda i, j: (0, j)),
                   pl.BlockSpec((tm, tn), lambda i, j: (i, j)), (m, n), out_dtype, alpha=alpha)
    return _mm(name, a, b, _TN, (n // tn, m // tm),
               pl.BlockSpec((t, tm), lambda j, i: (0, i)), pl.BlockSpec((t, tn), lambda j, i: (0, j)),
               pl.BlockSpec((tm, tn), lambda j, i: (i, j)), (m, n), out_dtype, alpha=alpha)


def _ffn_in_fwd(name, n, wblk, dep=None):
    t = n.shape[0]
    tm = _pick(t, (1088, 768, 512, 256, 128))
    has_dep = dep is not None

    def body(n_ref, w_ref, *rest):
        gu_ref, a_ref = rest[-2], rest[-1]
        nv = n_ref[...]
        g = lax.dot_general(nv, w_ref[0], _NN, preferred_element_type=F32)
        u = lax.dot_general(nv, w_ref[1], _NN, preferred_element_type=F32)
        sg = jax.nn.sigmoid(g)
        silu = g * sg
        a_ref[...] = (silu * u).astype(BF16)
        gu_ref[0] = ((0.5 * u) * (sg + silu * (1.0 - sg))).astype(BF16)
        gu_ref[1] = (0.5 * silu).astype(BF16)

    return pl.pallas_call(
        body, name=name, grid=(t // tm, 4),
        in_specs=[pl.BlockSpec((tm, D_MODEL), lambda i, j: (i, 0)),
                  pl.BlockSpec((2, None, D_MODEL, FF_SHARD_P), lambda i, j: (0, j, 0, 0))]
        + ([pl.BlockSpec(memory_space=pl.ANY)] if has_dep else []),
        out_specs=[pl.BlockSpec((2, tm, FF_SHARD_P), lambda i, j: (0, i, j)),
                   pl.BlockSpec((tm, FF_SHARD_P), lambda i, j: (i, j))],
        out_shape=[jax.ShapeDtypeStruct((2, t, D_FF_P), BF16), jax.ShapeDtypeStruct((t, D_FF_P), BF16)],
        compiler_params=_params(("parallel", "parallel")),
    )(*((n, wblk) + ((dep,) if has_dep else ())))


def _ffn_out_bwd_x(name, dh, w_out, gu, dep=None):
    t = dh.shape[0]
    tm = _pick(t, (1088, 768, 512, 256, 128))
    has_dep = dep is not None

    def body(dh_ref, w_ref, gu_ref, *rest):
        o_ref = rest[-1]
        w = w_ref[pl.ds(pl.multiple_of(pl.program_id(1) * FF_SHARD_P, FF_SHARD_P), FF_SHARD_P), :]
        da = lax.dot_general(_bf(dh_ref[...]), w, _NT, preferred_element_type=F32)
        o_ref[0] = (da * gu_ref[0].astype(F32)).astype(BF16)
        o_ref[1] = (da * gu_ref[1].astype(F32)).astype(BF16)

    gu_spec = pl.BlockSpec((2, tm, FF_SHARD_P), lambda i, j: (0, i, j))
    return pl.pallas_call(
        body, name=name, grid=(t // tm, 4),
        in_specs=[pl.BlockSpec((tm, D_MODEL), lambda i, j: (i, 0)),
                  pl.BlockSpec((D_FF_P, D_MODEL), lambda i, j: (0, 0), pipeline_mode=pl.Buffered(1)),
                  gu_spec] + ([pl.BlockSpec(memory_space=pl.ANY)] if has_dep else []),
        out_specs=gu_spec, out_shape=jax.ShapeDtypeStruct((2, t, D_FF_P), BF16),
        compiler_params=_params(("parallel", "parallel")),
    )(*((dh, w_out, gu) + ((dep,) if has_dep else ())))


def _rms_bwd_rows(dn, h, g, dres):
    r = lax.rsqrt(jnp.mean(h * h, axis=-1, keepdims=True) + EPS)
    tv = dn * g
    dot = jnp.mean(tv * h, axis=-1, keepdims=True)
    return dres + (r * tv - h * (r * r * r * dot)), jnp.sum(dn * (h * r), axis=0, keepdims=True)


def _accumulate_rows(ref, part, first):
    @pl.when(first)
    def _():
        ref[...] = part

    @pl.when(jnp.logical_not(first))
    def _():
        ref[...] += part


def _ffn_in_bwd_x(name, dgu, wblk, h_in, g_norm, dres, dep=None, examples=None):
    t = dgu.shape[1]
    tm = _pick(t, (544, 384, 256, 128))
    has_dep = dep is not None
    split = examples is not None
    if split:
        l = t // examples
        per = l // tm
        assert per * tm == l and tm > PREFIX

    def body(d_ref, w_ref, h_ref, g_ref, r_ref, *rest):
        acc = None
        for s in range(2):
            for j in range(4):
                part = lax.dot_general(d_ref[s, :, FF_SHARD_P * j:FF_SHARD_P * (j + 1)], w_ref[s, j], _NT,
                                       preferred_element_type=F32)
                acc = part if acc is None else acc + part
        dh, dg = _rms_bwd_rows(acc, h_ref[...], g_ref[...], r_ref[...])
        i = pl.program_id(0)
        if not split:
            dh_ref, dhb_ref, dg_ref = rest[-3:]
            dh_ref[...] = dh
            dhb_ref[...] = dh.astype(BF16)
        else:
            gx_ref, meta_ref, dg_ref, buf, sem = rest[-5:]

            def out_copy(step):
                bi, r = step // per, step % per
                head = pltpu.make_async_copy(buf.at[pl.ds(PREFIX, tm - PREFIX)], gx_ref.at[bi, pl.ds(0, tm - PREFIX)], sem)
                if per == 1:
                    return r == 0, head, None
                return r == 0, head, pltpu.make_async_copy(
                    buf, gx_ref.at[bi, pl.ds(pl.multiple_of(jnp.maximum(r, 1) * tm - PREFIX, 8), tm)], sem)

            def run(step, method):
                is_head, head, later = out_copy(step)

                @pl.when(is_head)
                def _():
                    getattr(head, method)()

                if later is not None:
                    @pl.when(jnp.logical_not(is_head))
                    def _():
                        getattr(later, method)()

            @pl.when(i > 0)
            def _():
                run(i - 1, "wait")

            buf[...] = dh

            @pl.when(i % per == 0)
            def _():
                meta_ref[...] = dh[N_PAD:PREFIX]

            run(i, "start")

            @pl.when(i == pl.num_programs(0) - 1)
            def _():
                run(i, "wait")

        _accumulate_rows(dg_ref, dg, i == 0)

    row = pl.BlockSpec((tm, D_MODEL), lambda i: (i, 0))
    vec = pl.BlockSpec((1, D_MODEL), lambda i: (0, 0))
    if split:
        out_specs = [pl.BlockSpec(memory_space=pl.ANY), pl.BlockSpec((None, N_META, D_MODEL), lambda i: (i // per, 0, 0)), vec]
        out_shape = [jax.ShapeDtypeStruct((examples, l - PREFIX, D_MODEL), F32),
                     jax.ShapeDtypeStruct((examples, N_META, D_MODEL), F32), jax.ShapeDtypeStruct((1, D_MODEL), F32)]
        scratch = [pltpu.VMEM((tm, D_MODEL), F32), pltpu.SemaphoreType.DMA(())]
    else:
        out_specs = [row, row, vec]
        out_shape = [jax.ShapeDtypeStruct((t, D_MODEL), F32), jax.ShapeDtypeStruct((t, D_MODEL), BF16),
                     jax.ShapeDtypeStruct((1, D_MODEL), F32)]
        scratch = []
    return pl.pallas_call(
        body, name=name, grid=(t // tm,),
        in_specs=[pl.BlockSpec((2, tm, D_FF_P), lambda i: (0, i, 0)),
                  pl.BlockSpec((2, 4, D_MODEL, FF_SHARD_P), lambda i: (0, 0, 0, 0), pipeline_mode=pl.Buffered(1)),
                  row, vec, row]
        + ([pl.BlockSpec(memory_space=pl.ANY)] if has_dep else []),
        out_specs=out_specs, out_shape=out_shape, scratch_shapes=scratch,
        compiler_params=_params(("arbitrary",)),
    )(*((dgu, wblk, h_in, g_norm, dres) + ((dep,) if has_dep else ())))


def _proj_fwd(name, um, wi):
    t = um.shape[0]
    tm = _pick(t, (544, 384, 256, 128))

    def body(u_ref, w_ref, q_ref, g_ref, f_ref):
        uv = u_ref[...]
        q_ref[...] = lax.dot_general(uv, w_ref[:, 0:QKV_W], _NN, preferred_element_type=F32).astype(BF16)
        g_ref[...] = lax.dot_general(uv, w_ref[:, P_GATES:P_F], _NN, preferred_element_type=F32).astype(BF16)
        f_ref[...] = lax.dot_general(uv, w_ref[:, P_F:PROJ_P], _NN, preferred_element_type=F32)

    return pl.pallas_call(
        body, name=name, grid=(t // tm,),
        in_specs=[pl.BlockSpec((tm, D_MODEL), lambda i: (i, 0)),
                  pl.BlockSpec((D_MODEL, PROJ_P), lambda i: (0, 0), pipeline_mode=pl.Buffered(1))],
        out_specs=[pl.BlockSpec((tm, QKV_W), lambda i: (i, 0)), pl.BlockSpec((tm, 2 * D_MODEL), lambda i: (i, 0)),
                   pl.BlockSpec((tm, 128), lambda i: (i, 0))],
        out_shape=[jax.ShapeDtypeStruct((t, QKV_W), BF16), jax.ShapeDtypeStruct((t, 2 * D_MODEL), BF16),
                   jax.ShapeDtypeStruct((t, 128), F32)],
        compiler_params=_params(("parallel",)),
    )(um, wi)


def _proj_bwd_x(name, dqkv, dgates, df, wi, h_in, g_norm, dres, dep=None):
    t = dqkv.shape[0]
    tm = _pick(t, (544, 384, 256, 128))
    has_dep = dep is not None

    def body(q_ref, gt_ref, f_ref, w_ref, h_ref, g_ref, r_ref, *rest):
        dh_ref, dhb_ref, dg_ref = rest[-3:]
        acc = lax.dot_general(q_ref[...], w_ref[:, 0:QKV_W], _NT, preferred_element_type=F32)
        acc = acc + lax.dot_general(gt_ref[...], w_ref[:, P_GATES:P_F], _NT, preferred_element_type=F32)
        acc = acc + lax.dot_general(f_ref[...], w_ref[:, P_F:PROJ_P], _NT, preferred_element_type=F32)
        dh, dg = _rms_bwd_rows(acc, h_ref[...], g_ref[...], r_ref[...])
        dh_ref[...] = dh
        dhb_ref[...] = dh.astype(BF16)
        _accumulate_rows(dg_ref, dg, pl.program_id(0) == 0)

    row = pl.BlockSpec((tm, D_MODEL), lambda i: (i, 0))
    vec = pl.BlockSpec((1, D_MODEL), lambda i: (0, 0))
    return pl.pallas_call(
        body, name=name, grid=(t // tm,),
        in_specs=[pl.BlockSpec((tm, QKV_W), lambda i: (i, 0)), pl.BlockSpec((tm, 2 * D_MODEL), lambda i: (i, 0)),
                  pl.BlockSpec((tm, 128), lambda i: (i, 0)),
                  pl.BlockSpec((D_MODEL, PROJ_P), lambda i: (0, 0), pipeline_mode=pl.Buffered(1)), row, vec, row]
        + ([pl.BlockSpec(memory_space=pl.ANY)] if has_dep else []),
        out_specs=[row, row, vec],
        out_shape=[jax.ShapeDtypeStruct((t, D_MODEL), F32), jax.ShapeDtypeStruct((t, D_MODEL), BF16),
                   jax.ShapeDtypeStruct((1, D_MODEL), F32)],
        compiler_params=_params(("arbitrary",)),
    )(*((dqkv, dgates, df, wi, h_in, g_norm, dres) + ((dep,) if has_dep else ())))


def _ffn_in_bwd_w(name, n, dgu):
    t = n.shape[0]
    return _mm(name, dgu, n, _TN, (2, 4),
               pl.BlockSpec((None, t, FF_SHARD_P), lambda s, j: (s, 0, j)),
               pl.BlockSpec((t, D_MODEL), lambda s, j: (0, 0)),
               pl.BlockSpec((None, None, FF_SHARD_P, D_MODEL), lambda s, j: (s, j, 0, 0)),
               (2, 4, FF_SHARD_P, D_MODEL), BF16)


def _embed_norm(name, x, meta, g):
    b, s, _ = x.shape
    half = (s + PREFIX) // 2
    first = half - PREFIX
    assert first > 0 and half % 16 == 0

    def body(x_ref, m_ref, g_ref, h_ref, n_ref, buf, sem):
        bi, k = pl.program_id(0), pl.program_id(1)

        def tokens(example, second, method):
            if second:
                cp = pltpu.make_async_copy(x_ref.at[example, pl.ds(first, half)], buf.at[1], sem.at[1])
            else:
                cp = pltpu.make_async_copy(x_ref.at[example, pl.ds(0, first)], buf.at[0, pl.ds(PREFIX, first)], sem.at[0])
            getattr(cp, method)()

        @pl.when((bi == 0) & (k == 0))
        def _():
            tokens(0, False, "start")

        @pl.when(k == 0)
        def _():
            tokens(bi, True, "start")
            tokens(bi, False, "wait")
            buf[0, 0:N_PAD, :] = jnp.zeros((N_PAD, D_MODEL), F32)
            buf[0, N_PAD:PREFIX, :] = m_ref[...]

        @pl.when(k == 1)
        def _():
            @pl.when(bi + 1 < b)
            def _():
                tokens(bi + 1, False, "start")

            tokens(bi, True, "wait")

        hv = buf[k]
        h_ref[...] = hv
        r = lax.rsqrt(jnp.mean(hv * hv, axis=-1, keepdims=True) + EPS)
        n_ref[...] = ((hv * r) * g_ref[...]).astype(BF16)

    rows = pl.BlockSpec((half, D_MODEL), lambda bi, k: (2 * bi + k, 0))
    return pl.pallas_call(
        body, name=name, grid=(b, 2),
        in_specs=[pl.BlockSpec(memory_space=pl.ANY), pl.BlockSpec((N_META, D_MODEL), lambda bi, k: (0, 0)),
                  pl.BlockSpec((1, D_MODEL), lambda bi, k: (0, 0))],
        out_specs=[rows, rows],
        out_shape=[jax.ShapeDtypeStruct((2 * b * half, D_MODEL), F32), jax.ShapeDtypeStruct((2 * b * half, D_MODEL), BF16)],
        scratch_shapes=[pltpu.VMEM((2, half, D_MODEL), F32), pltpu.SemaphoreType.DMA((2,))],
        compiler_params=_params(("arbitrary", "arbitrary")),
    )(x, meta, g)


def _branch_gate_fwd(name, oa, ob, wa, wb, gates, dep=None):
    t = gates.shape[0]
    tm = _pick(t, (544, 384, 256, 128))
    has_dep = dep is not None

    def body(oa_ref, ob_ref, wa_ref, wb_ref, g_ref, *rest):
        o_ref, ya_ref, yb_ref = rest[-3:]
        ya = lax.dot_general(_bf(oa_ref[...]), wa_ref[...], _NN, preferred_element_type=F32)
        yb = lax.dot_general(_bf(ob_ref[...]), wb_ref[...], _NN, preferred_element_type=F32)
        sa = jax.nn.sigmoid(g_ref[:, 0:D_MODEL].astype(F32))
        sb = jax.nn.sigmoid(g_ref[:, D_MODEL:2 * D_MODEL].astype(F32))
        o_ref[...] = (sa * ya + sb * yb).astype(BF16)
        ya_ref[...] = ya.astype(BF16)
        yb_ref[...] = yb.astype(BF16)

    blk = pl.BlockSpec((tm, D_MODEL), lambda i: (i, 0))
    narrow = pl.BlockSpec((tm, A_WIDTH), lambda i: (i, 0))
    wide = pl.BlockSpec((tm, 2 * D_MODEL), lambda i: (i, 0))
    wspec = pl.BlockSpec((A_WIDTH, D_MODEL), lambda i: (0, 0))
    out = jax.ShapeDtypeStruct((t, D_MODEL), BF16)
    return pl.pallas_call(
        body, name=name, grid=(t // tm,),
        in_specs=[narrow, narrow, wspec, wspec, wide] + ([pl.BlockSpec(memory_space=pl.ANY)] if has_dep else []),
        out_specs=[blk, blk, blk], out_shape=[out, out, out], compiler_params=_params(("parallel",)),
    )(*((oa, ob, wa, wb, gates) + ((dep,) if has_dep else ())))


def _branch_bwd_x(name, dya, dyb, wa, wb):
    t = dya.shape[0]
    tm = _pick(t, (1088, 768, 512, 256, 128))

    def body(da_ref, db_ref, wa_ref, wb_ref, oa_ref, ob_ref):
        oa_ref[...] = lax.dot_general(da_ref[...], wa_ref[...], _NT, preferred_element_type=F32)
        ob_ref[...] = lax.dot_general(db_ref[...], wb_ref[...], _NT, preferred_element_type=F32).astype(BF16)

    blk = pl.BlockSpec((tm, D_MODEL), lambda i: (i, 0))
    narrow = pl.BlockSpec((tm, A_WIDTH), lambda i: (i, 0))
    wspec = pl.BlockSpec((A_WIDTH, D_MODEL), lambda i: (0, 0), pipeline_mode=pl.Buffered(1))
    return pl.pallas_call(
        body, name=name, grid=(t // tm,), in_specs=[blk, blk, wspec, wspec], out_specs=[narrow, narrow],
        out_shape=[jax.ShapeDtypeStruct((t, A_WIDTH), F32), jax.ShapeDtypeStruct((t, B_WIDTH), BF16)],
        compiler_params=_params(("parallel",)),
    )(dya, dyb, wa, wb)


def _branch_bwd_w(name, oa, ob, dya, dyb):
    t = oa.shape[0]
    tn = 512

    def body(oa_ref, ob_ref, da_ref, db_ref, wa_ref, wb_ref):
        wa_ref[...] = lax.dot_general(_bf(oa_ref[...]), da_ref[...], _TN, preferred_element_type=F32).astype(BF16)
        wb_ref[...] = lax.dot_general(_bf(ob_ref[...]), db_ref[...], _TN, preferred_element_type=F32).astype(BF16)

    whole = pl.BlockSpec((t, A_WIDTH), lambda j: (0, 0), pipeline_mode=pl.Buffered(1))
    cols = pl.BlockSpec((t, tn), lambda j: (0, j))
    out_spec = pl.BlockSpec((A_WIDTH, tn), lambda j: (0, j))
    out = jax.ShapeDtypeStruct((A_WIDTH, D_MODEL), BF16)
    return pl.pallas_call(
        body, name=name, grid=(D_MODEL // tn,), in_specs=[whole, whole, cols, cols], out_specs=[out_spec, out_spec],
        out_shape=[out, out], compiler_params=_params(("parallel",)),
    )(oa, ob, dya, dyb)


def _mix_out_gate_bwd(name, dh, wo, gates, ya, yb):
    t = gates.shape[0]
    tm = _pick(t, (544, 384, 256, 128))

    def body(dh_ref, w_ref, g_ref, ya_ref, yb_ref, dya_ref, dyb_ref, dg_ref):
        dm = lax.dot_general(_bf(dh_ref[...]), w_ref[...], _NT, preferred_element_type=F32)
        sa = jax.nn.sigmoid(g_ref[:, 0:D_MODEL].astype(F32))
        sb = jax.nn.sigmoid(g_ref[:, D_MODEL:2 * D_MODEL].astype(F32))
        dya_ref[...] = (dm * sa).astype(BF16)
        dyb_ref[...] = (dm * sb).astype(BF16)
        dg_ref[:, 0:D_MODEL] = (dm * ya_ref[...].astype(F32) * (sa * (1.0 - sa))).astype(BF16)
        dg_ref[:, D_MODEL:2 * D_MODEL] = (dm * yb_ref[...].astype(F32) * (sb * (1.0 - sb))).astype(BF16)

    blk = pl.BlockSpec((tm, D_MODEL), lambda i: (i, 0))
    wide = pl.BlockSpec((tm, 2 * D_MODEL), lambda i: (i, 0))
    out = jax.ShapeDtypeStruct((t, D_MODEL), BF16)
    return pl.pallas_call(
        body, name=name, grid=(t // tm,),
        in_specs=[blk, pl.BlockSpec((D_MODEL, D_MODEL), lambda i: (0, 0)), wide, blk, blk], out_specs=[blk, blk, wide],
        out_shape=[out, out, jax.ShapeDtypeStruct((t, 2 * D_MODEL), BF16)], compiler_params=_params(("parallel",)),
    )(dh, wo, gates, ya, yb)


def _ffn_out_loss(name, a, w, res, gf, tgt, alpha):
    t, k = a.shape
    b, s, _ = tgt.shape
    l = t // b
    tm = _pick(t, (544, 384, 256, 128))
    per = l // tm
    assert per * tm == l and tm > PREFIX and l - PREFIX == s

    def body(a_ref, w_ref, r_ref, g_ref, t_ref, dh_ref, dhb_ref, loss_ref, dg_ref, buf, sem):
        i = pl.program_id(0)
        bi, r = i // per, i % per

        def first_rows():
            return pltpu.make_async_copy(t_ref.at[bi, pl.ds(0, tm - PREFIX)], buf.at[pl.ds(PREFIX, tm - PREFIX)], sem)

        def later_rows():
            return pltpu.make_async_copy(t_ref.at[bi, pl.ds(pl.multiple_of(r * tm - PREFIX, 8), tm)], buf, sem)

        @pl.when(r == 0)
        def _():
            buf[0:PREFIX, :] = jnp.zeros((PREFIX, D_MODEL), F32)
            first_rows().start()

        if per > 1:
            @pl.when(r > 0)
            def _():
                later_rows().start()

        acc = lax.dot_general(a_ref[...], w_ref[...], _NN, preferred_element_type=F32)

        @pl.when(r == 0)
        def _():
            first_rows().wait()

        if per > 1:
            @pl.when(r > 0)
            def _():
                later_rows().wait()

        hv = acc * alpha + r_ref[...]
        row = lax.broadcasted_iota(jnp.int32, (tm, 1), 0)
        real = ((r > 0) | (row >= PREFIX)).astype(F32)
        g = g_ref[...]
        rn = lax.rsqrt(jnp.mean(hv * hv, axis=-1, keepdims=True) + EPS)
        xn = hv * rn
        err = (xn * g - buf[...]) * real
        lpart = 0.5 * jnp.sum(jnp.mean(err * err, axis=-1, keepdims=True), axis=0, keepdims=True)
        dy = err * (1.0 / D_MODEL)
        tv = dy * g
        dot = jnp.mean(tv * hv, axis=-1, keepdims=True)
        dh = rn * tv - hv * (rn * rn * rn * dot)
        dh_ref[...] = dh
        dhb_ref[...] = dh.astype(BF16)
        gpart = jnp.sum(dy * xn, axis=0, keepdims=True)

        @pl.when(i == 0)
        def _():
            loss_ref[...] = jnp.zeros_like(loss_ref)
            dg_ref[...] = jnp.zeros_like(dg_ref)

        loss_ref[...] += jnp.broadcast_to(lpart, loss_ref.shape)
        dg_ref[...] += gpart

    row_spec = pl.BlockSpec((tm, D_MODEL), lambda i: (i, 0))
    vec = pl.BlockSpec((1, D_MODEL), lambda i: (0, 0))
    return pl.pallas_call(
        body, name=name, grid=(t // tm,),
        in_specs=[pl.BlockSpec((tm, k), lambda i: (i, 0)),
                  pl.BlockSpec((k, D_MODEL), lambda i: (0, 0), pipeline_mode=pl.Buffered(1)), row_spec, vec,
                  pl.BlockSpec(memory_space=pl.ANY)],
        out_specs=[row_spec, row_spec, pl.BlockSpec((8, 128), lambda i: (0, 0)), vec],
        out_shape=[jax.ShapeDtypeStruct((t, D_MODEL), F32), jax.ShapeDtypeStruct((t, D_MODEL), BF16),
                   jax.ShapeDtypeStruct((8, 128), F32), jax.ShapeDtypeStruct((1, D_MODEL), F32)],
        scratch_shapes=[pltpu.VMEM((tm, D_MODEL), F32), pltpu.SemaphoreType.DMA(())],
        compiler_params=_params(("arbitrary",)),
    )(a, w, res, gf, tgt)


def _fgate_fwd(name, f3, bf_row):
    b, l, _ = f3.shape
    nb = l // BLOCK

    def body(f_ref, b_ref, cc_ref, cr_ref):
        r_i = lax.broadcasted_iota(jnp.int32, (BLOCK, BLOCK), 0)
        c_i = lax.broadcasted_iota(jnp.int32, (BLOCK, BLOCK), 1)
        tri = (r_i >= c_i).astype(F32)
        carry = jnp.zeros((1, 128), F32)
        for blk in range(nb):
            rows = slice(blk * BLOCK, (blk + 1) * BLOCK)
            z = f_ref[rows, :] + b_ref[...]
            lf = jnp.minimum(z, 0.0) - jnp.log(1.0 + jnp.exp(-jnp.abs(z)))
            cb = jnp.dot(tri, lf, preferred_element_type=F32, precision=lax.Precision.HIGHEST) + carry
            carry = cb[BLOCK - 1:BLOCK, :]
            cbt = cb.T
            for hh in range(B_HEADS):
                cc_ref[hh, rows, :] = jnp.sum(jnp.where(c_i == hh, cb, 0.0), axis=1, keepdims=True)
                cr_ref[hh, :, rows] = cbt[hh:hh + 1, :]

    return pl.pallas_call(
        body, name=name, grid=(b,),
        in_specs=[pl.BlockSpec((None, l, 128), lambda bi: (bi, 0, 0)),
                  pl.BlockSpec((1, 128), lambda bi: (0, 0))],
        out_specs=[pl.BlockSpec((None, B_HEADS, l, 1), lambda bi: (bi, 0, 0, 0)),
                   pl.BlockSpec((None, B_HEADS, 1, l), lambda bi: (bi, 0, 0, 0))],
        out_shape=[jax.ShapeDtypeStruct((b, B_HEADS, l, 1), F32), jax.ShapeDtypeStruct((b, B_HEADS, 1, l), F32)],
        compiler_params=_params(("parallel",)),
    )(f3, bf_row)


def _fgate_bwd(name, f3, bf_row, dcq, dck):
    b, l, _ = f3.shape
    nb = l // BLOCK

    def body(f_ref, b_ref, dcq_ref, dck_ref, df_ref, db_ref):
        r_i = lax.broadcasted_iota(jnp.int32, (BLOCK, BLOCK), 0)
        c_i = lax.broadcasted_iota(jnp.int32, (BLOCK, BLOCK), 1)
        tri = (r_i <= c_i).astype(F32)
        carry = jnp.zeros((1, 128), F32)
        total = jnp.zeros((1, 128), F32)
        for blk in range(nb - 1, -1, -1):
            rows = slice(blk * BLOCK, (blk + 1) * BLOCK)
            krows = jnp.concatenate([dck_ref[hh, :, rows] for hh in range(B_HEADS)]
                                    + [jnp.zeros((BLOCK - B_HEADS, BLOCK), F32)], axis=0)
            dcb = krows.T
            for hh in range(B_HEADS):
                dcb = dcb + jnp.where(c_i == hh, dcq_ref[hh, rows, :], 0.0)
            rc = jnp.dot(tri, dcb, preferred_element_type=F32, precision=lax.Precision.HIGHEST) + carry
            carry = rc[0:1, :]
            z = f_ref[rows, :] + b_ref[...]
            df = rc * (1.0 / (1.0 + jnp.exp(z)))
            df_ref[rows, :] = df.astype(BF16)
            total = total + jnp.sum(df, axis=0, keepdims=True)

        @pl.when(pl.program_id(0) == 0)
        def _():
            db_ref[...] = total

        @pl.when(pl.program_id(0) > 0)
        def _():
            db_ref[...] += total

    return pl.pallas_call(
        body, name=name, grid=(b,),
        in_specs=[pl.BlockSpec((None, l, 128), lambda bi: (bi, 0, 0)),
                  pl.BlockSpec((1, 128), lambda bi: (0, 0)),
                  pl.BlockSpec((None, B_HEADS, l, 1), lambda bi: (bi, 0, 0, 0)),
                  pl.BlockSpec((None, B_HEADS, 1, l), lambda bi: (bi, 0, 0, 0))],
        out_specs=[pl.BlockSpec((None, l, 128), lambda bi: (bi, 0, 0)), pl.BlockSpec((1, 128), lambda bi: (0, 0))],
        out_shape=[jax.ShapeDtypeStruct((b, l, 128), BF16), jax.ShapeDtypeStruct((1, 128), F32)],
        compiler_params=_params(("arbitrary",)),
    )(f3, bf_row, dcq, dck)


A_Q_BLK = B_SEG // A_WIDTH
A_K_BLK = (B_SEG + A_WIDTH) // 128
A_V_BLK = A_K_BLK + 1
A_SEG_BLK = B_SEG // A_SEG
STACK = A_HEADS * BLOCK


def _lane_lo():
    return lax.broadcasted_iota(jnp.int32, (1, 128), 1) < HEAD_DIM


def _stack_heads(x, masked):
    lo = _lane_lo()
    blks = [x[:, 128 * j:128 * (j + 1)] for j in range(4)]
    if not masked:
        return jnp.concatenate(blks + blks, axis=0)
    zero = jnp.zeros_like(blks[0])
    return jnp.concatenate([jnp.where(lo, bk, zero) for bk in blks] + [jnp.where(lo, zero, bk) for bk in blks], axis=0)


def _unstack_heads(y):
    lo = _lane_lo()
    return jnp.concatenate([jnp.where(lo, y[128 * j:128 * (j + 1)], y[128 * (4 + j):128 * (5 + j)]) for j in range(4)], axis=1)


def _swa_bias(slopes):
    slopes = np.asarray(slopes, np.float32)
    r_i = np.arange(STACK)[:, None]
    c_i = np.arange(3 * BLOCK)[None, :]
    seg = c_i >> 7
    out = []
    for n in range(3):
        qpos = n * BLOCK + (r_i & (BLOCK - 1))
        kpos = np.where(seg == 0, c_i, (n - 2) * BLOCK + c_i)
        dist = qpos - kpos
        band = (seg != 0) & (dist < BLOCK) & (kpos >= PREFIX)
        meta = (seg == 0) & (c_i >= N_PAD)
        out.append(np.where((dist >= 0) & (band | meta), -slopes * dist.astype(np.float32), np.float32(NEG)))
    return jnp.asarray(np.stack(out, axis=0), F32)


def _swa_scores(q, kcat, n, slope, bias):
    s = lax.dot_general(q, kcat, _NT, preferred_element_type=F32) + bias
    further = slope * (-BLOCK * jnp.maximum(n - 2, 0)).astype(F32)
    return jnp.concatenate([s[:, 0:BLOCK] + further, s[:, BLOCK:]], axis=1)


def _swa_specs():
    def kv(col_blk):
        return [pl.BlockSpec((None, BLOCK, 128), lambda b, n: (b, 0, col_blk)),
                pl.BlockSpec((None, BLOCK, 128), lambda b, n: (b, jnp.maximum(n - 1, 0), col_blk)),
                pl.BlockSpec((None, BLOCK, 128), lambda b, n: (b, n, col_blk))]

    q_spec = pl.BlockSpec((None, BLOCK, A_WIDTH), lambda b, n: (b, n, A_Q_BLK))
    o_spec = pl.BlockSpec((None, BLOCK, A_WIDTH), lambda b, n: (b, n, 0))
    col = pl.BlockSpec((STACK, 1), lambda b, n: (0, 0))
    bias = pl.BlockSpec((None, STACK, 3 * BLOCK), lambda b, n: (jnp.minimum(n, 2), 0, 0))
    lse_spec = pl.BlockSpec((None, A_HEADS, BLOCK, 1), lambda b, n: (b, 0, n, 0))
    return q_spec, kv(A_K_BLK), kv(A_V_BLK), o_spec, [col, col, bias], lse_spec


def _swa_fwd(name, qkv, slopes, sinks, bias):
    b, l, _ = qkv.shape
    nb = l // BLOCK

    def body(q_ref, k0_ref, kp_ref, kc_ref, v0_ref, vp_ref, vc_ref, sl_ref, sk_ref, bias_ref, o_ref, o16_ref, lse_ref):
        n = pl.program_id(1)
        qs = _stack_heads(q_ref[...], True) * SCALE
        kcat = jnp.concatenate([k0_ref[...], kp_ref[...], kc_ref[...]], axis=0)
        vcat = jnp.concatenate([v0_ref[...], vp_ref[...], vc_ref[...]], axis=0)
        s = _swa_scores(qs, kcat, n, sl_ref[...], bias_ref[...])
        sink = sk_ref[...]
        m = jnp.maximum(jnp.max(s, axis=-1, keepdims=True), sink)
        p = jnp.exp(s - m)
        den = jnp.sum(p, axis=-1, keepdims=True) + jnp.exp(sink - m)
        o = lax.dot_general(p.astype(BF16), vcat, _NN, preferred_element_type=F32) / den
        ov = _unstack_heads(o)
        o_ref[...] = ov
        o16_ref[...] = ov.astype(BF16)
        lse_ref[...] = (m + jnp.log(den)).reshape(A_HEADS, BLOCK, 1)

    q_spec, k_specs, v_specs, o_spec, consts, lse_spec = _swa_specs()
    return pl.pallas_call(
        body, name=name, grid=(b, nb),
        in_specs=[q_spec] + k_specs + v_specs + consts, out_specs=[o_spec, o_spec, lse_spec],
        out_shape=[jax.ShapeDtypeStruct((b, l, A_WIDTH), F32), jax.ShapeDtypeStruct((b, l, A_WIDTH), BF16),
                   jax.ShapeDtypeStruct((b, A_HEADS, l, 1), F32)],
        compiler_params=_params(("parallel", "parallel")),
    )(qkv, qkv, qkv, qkv, qkv, qkv, qkv, slopes, sinks, bias)


def _swa_bwd(name, qkv, o, lse, do, slopes, sinks, bias, dqkv):
    b, l, _ = qkv.shape
    nb = l // BLOCK

    def body(q_ref, k0_ref, kp_ref, kc_ref, v0_ref, vp_ref, vc_ref, o_ref, lse_ref, do_ref, sl_ref, sk_ref, bias_ref, _,
             dx_ref, ds_ref, dk_acc, dv_acc):
        bi = pl.program_id(0)
        n = pl.program_id(1)
        qs = _stack_heads(q_ref[...], True) * SCALE
        dos32 = _stack_heads(do_ref[...], True)
        dos = dos32.astype(BF16)
        os_ = _stack_heads(o_ref[...], False)
        lsev = lse_ref[...].reshape(STACK, 1)
        kcat = jnp.concatenate([k0_ref[...], kp_ref[...], kc_ref[...]], axis=0)
        vcat = jnp.concatenate([v0_ref[...], vp_ref[...], vc_ref[...]], axis=0)
        s = _swa_scores(qs, kcat, n, sl_ref[...], bias_ref[...])
        p = jnp.exp(s - lsev)
        dsum = jnp.sum(dos32 * os_, axis=-1, keepdims=True)
        dp = lax.dot_general(dos, vcat, _NT, preferred_element_type=F32)
        dsc = (p * (dp - dsum)).astype(BF16)
        dq = lax.dot_general(dsc, kcat, _NN, preferred_element_type=F32) * SCALE
        row0 = pl.multiple_of(n * BLOCK, BLOCK)
        dx_ref[pl.ds(row0, BLOCK), 0:A_WIDTH] = _unstack_heads(dq).astype(BF16)
        dkc = lax.dot_general(dsc, qs, _TN, preferred_element_type=F32)
        dvc = lax.dot_general(p.astype(BF16), dos, _TN, preferred_element_type=F32)

        @pl.when(n == 0)
        def _():
            dk_acc[...] = jnp.zeros_like(dk_acc)
            dv_acc[...] = jnp.zeros_like(dv_acc)

        starts = (0, pl.multiple_of(jnp.maximum(n - 1, 0) * BLOCK, BLOCK), row0)
        for t, st in enumerate(starts):
            dk_acc[pl.ds(st, BLOCK), :] += dkc[t * BLOCK:(t + 1) * BLOCK, :]
            dv_acc[pl.ds(st, BLOCK), :] += dvc[t * BLOCK:(t + 1) * BLOCK, :]

        @pl.when(n == nb - 1)
        def _():
            dx_ref[:, A_WIDTH:A_WIDTH + 128] = dk_acc[...].astype(BF16)
            dx_ref[:, A_WIDTH + 128:A_SEG] = dv_acc[...].astype(BF16)

        dsink = -(jnp.exp(sk_ref[...] - lsev) * dsum)
        r8 = lax.broadcasted_iota(jnp.int32, (8, 128), 0)
        acc = jnp.zeros((8, 128), F32)
        for hh in range(A_HEADS):
            acc = acc + jnp.where(r8 == hh, jnp.sum(dsink[hh * BLOCK:(hh + 1) * BLOCK, :]), 0.0)

        @pl.when((bi == 0) & (n == 0))
        def _():
            ds_ref[...] = jnp.zeros_like(ds_ref)

        ds_ref[...] += acc

    q_spec, k_specs, v_specs, o_spec, consts, lse_spec = _swa_specs()
    return pl.pallas_call(
        body, name=name, grid=(b, nb),
        in_specs=[q_spec] + k_specs + v_specs + [o_spec, lse_spec, o_spec] + consts + [pl.BlockSpec(memory_space=pl.ANY)],
        out_specs=[pl.BlockSpec((None, l, A_SEG), lambda bb, n: (bb, 0, A_SEG_BLK)),
                   pl.BlockSpec((8, 128), lambda bb, n: (0, 0))],
        out_shape=[jax.ShapeDtypeStruct(dqkv.shape, BF16), jax.ShapeDtypeStruct((8, 128), F32)],
        scratch_shapes=[pltpu.VMEM((l, 128), F32), pltpu.VMEM((l, 128), F32)],
        input_output_aliases={13: 0},
        compiler_params=_params(("arbitrary", "arbitrary")),
    )(qkv, qkv, qkv, qkv, qkv, qkv, qkv, o, lse, do, slopes, sinks, bias, dqkv)


def _fox_mask(qk, ck, i):
    kh = qk.shape[1]
    qpos = i * BLOCK + lax.broadcasted_iota(jnp.int32, (BLOCK, kh), 0)
    kpos = lax.broadcasted_iota(jnp.int32, (BLOCK, kh), 1)
    return jnp.where((kpos <= qpos) & (kpos >= N_PAD), qk - ck, NEG)


def _pick_head(x, hh):
    lo = _lane_lo()
    return jnp.where(lo if hh == 0 else jnp.logical_not(lo), x, jnp.zeros_like(x))


def _both_heads(x):
    return jnp.concatenate([_pick_head(x, 0), _pick_head(x, 1)], axis=0)


def _fox_specs(l):
    pair = pl.BlockSpec((None, l, PAIR_W), lambda bi, hp: (bi, 0, hp))
    half = pl.BlockSpec((None, l, 128), lambda bi, hp: (bi, 0, hp))
    colv = pl.BlockSpec((None, 2, l, 1), lambda bi, hp: (bi, hp, 0, 0))
    rowv = pl.BlockSpec((None, 2, 1, l), lambda bi, hp: (bi, hp, 0, 0))
    return pair, half, colv, rowv


def _fox_fwd(name, qkv, c_col, c_row):
    b, l, _ = qkv.shape
    nb = l // BLOCK

    def body(x_ref, cc_ref, cr_ref, o_ref, lse_ref):
        for i in range(nb):
            rows = slice(i * BLOCK, (i + 1) * BLOCK)
            kh = (i + 1) * BLOCK
            qblk = x_ref[rows, 0:128]
            kv = x_ref[0:kh, 128:256]
            vv = x_ref[0:kh, 256:384]
            qk = lax.dot_general(_both_heads(qblk) * SCALE, kv, _NT, preferred_element_type=F32)
            ps, dens = [], []
            for hh in range(2):
                s = _fox_mask(qk[hh * BLOCK:(hh + 1) * BLOCK], cr_ref[hh, :, 0:kh], i)
                m = jnp.max(s, axis=-1, keepdims=True)
                p = jnp.exp(s - m)
                den = jnp.sum(p, axis=-1, keepdims=True)
                ps.append(p.astype(BF16))
                dens.append(den)
                lse_ref[hh, rows, :] = (m + jnp.log(den)) + cc_ref[hh, rows, :]
            pv = lax.dot_general(jnp.concatenate(ps, axis=0), vv, _NN, preferred_element_type=F32)
            o_ref[rows, :] = jnp.where(_lane_lo(), pv[0:BLOCK] / dens[0], pv[BLOCK:2 * BLOCK] / dens[1]).astype(BF16)

    pair, half, colv, rowv = _fox_specs(l)
    return pl.pallas_call(
        body, name=name, grid=(b, 4), in_specs=[pair, colv, rowv], out_specs=[half, colv],
        out_shape=[jax.ShapeDtypeStruct((b, l, B_WIDTH), BF16), jax.ShapeDtypeStruct((b, B_HEADS, l, 1), F32)],
        compiler_params=_params(("parallel", "parallel")),
    )(qkv, c_col, c_row)


def _fox_bwd(name, qkv, c_col, c_row, o, lse, do):
    b, l, _ = qkv.shape
    nb = l // BLOCK

    def body(x_ref, cc_ref, cr_ref, o_ref, lse_ref, do_ref, dx_ref, dcq_ref, dck_ref, dk_acc, dv_acc):
        dk_acc[...] = jnp.zeros_like(dk_acc)
        dv_acc[...] = jnp.zeros_like(dv_acc)
        dck_ref[...] = jnp.zeros_like(dck_ref)
        for i in range(nb):
            rows = slice(i * BLOCK, (i + 1) * BLOCK)
            kh = (i + 1) * BLOCK
            qblk = x_ref[rows, 0:128]
            kv = x_ref[0:kh, 128:256]
            vv = x_ref[0:kh, 256:384]
            doblk = do_ref[rows, :]
            ov = o_ref[rows, :].astype(F32)
            q2 = _both_heads(qblk) * SCALE
            do2 = _both_heads(doblk)
            qk = lax.dot_general(q2, kv, _NT, preferred_element_type=F32)
            dp = lax.dot_general(do2, vv, _NT, preferred_element_type=F32)
            ps, dss = [], []
            for hh in range(2):
                half = slice(hh * BLOCK, (hh + 1) * BLOCK)
                s = _fox_mask(qk[half], cr_ref[hh, :, 0:kh], i)
                p = jnp.exp(s - (lse_ref[hh, rows, :] - cc_ref[hh, rows, :]))
                dsum = jnp.sum(do2[half].astype(F32) * ov, axis=-1, keepdims=True)
                ds = p * (dp[half] - dsum)
                ps.append(p.astype(BF16))
                dss.append(ds.astype(BF16))
                dcq_ref[hh, rows, :] = jnp.sum(ds, axis=-1, keepdims=True)
                dck_ref[hh, :, 0:kh] -= jnp.sum(ds, axis=0, keepdims=True)
            p2 = jnp.concatenate(ps, axis=0)
            ds2 = jnp.concatenate(dss, axis=0)
            dq = lax.dot_general(ds2, kv, _NN, preferred_element_type=F32) * SCALE
            dk_acc[0:kh, :] += lax.dot_general(ds2, q2, _TN, preferred_element_type=F32)
            dv_acc[0:kh, :] += lax.dot_general(p2, do2, _TN, preferred_element_type=F32)
            dx_ref[rows, 0:128] = jnp.where(_lane_lo(), dq[0:BLOCK], dq[BLOCK:2 * BLOCK]).astype(BF16)
        dx_ref[:, 128:256] = dk_acc[...].astype(BF16)
        dx_ref[:, 256:384] = dv_acc[...].astype(BF16)

    pair, half, colv, rowv = _fox_specs(l)
    return pl.pallas_call(
        body, name=name, grid=(b, 4), in_specs=[pair, colv, rowv, half, colv, half],
        out_specs=[pair, colv, rowv],
        out_shape=[jax.ShapeDtypeStruct(qkv.shape, BF16), jax.ShapeDtypeStruct((b, B_HEADS, l, 1), F32),
                   jax.ShapeDtypeStruct((b, B_HEADS, 1, l), F32)],
        scratch_shapes=[pltpu.VMEM((l, 128), F32), pltpu.VMEM((l, 128), F32)],
        compiler_params=_params(("parallel", "parallel")),
    )(qkv, c_col, c_row, o, lse, do)


_FLIPS = ((0, 0, 1), (0, 1, 0), (0, 1, 1), (1, 0, 0), (1, 0, 1), (1, 1, 0), (1, 1, 1))


def _peer_table():
    x, y, c = lax.axis_index("x"), lax.axis_index("y"), lax.axis_index("c")
    me = 4 * x + 2 * y + c
    peers = []
    for fx, fy, fc in _FLIPS:
        px = 1 - x if fx else x
        py = 1 - y if fy else y
        pc = 1 - c if fc else c
        peers.append(((px, py, pc), 4 * px + 2 * py + pc))
    return me, peers


_HBM = pl.BlockSpec(memory_space=pltpu.HBM)
_SEM = pl.BlockSpec(memory_space=pltpu.SEMAPHORE)
_ANY = pl.BlockSpec(memory_space=pl.ANY)
_EFFECT = pltpu.SideEffectType.DATAFLOW_SIDE_EFFECTING


def _split_copy(srcs_are_pieces, src_refs, land_refs, send_sem, recv_sem, a, kk, me, peers, arriving):
    dev, lin = peers[kk]
    npeer = len(_FLIPS)
    src = src_refs[a] if srcs_are_pieces[a] else src_refs[a].at[lin]
    dst = land_refs[a].at[lin] if arriving else land_refs[a].at[me]
    return pltpu.make_async_remote_copy(src_ref=src, dst_ref=dst, send_sem=send_sem.at[a * npeer + kk],
                                        recv_sem=recv_sem.at[a * npeer + kk], device_id=dev, device_id_type=MESH_ID)


def _xchg_start(name, gather, scatter, after=None):
    me_out = 4 * lax.axis_index("x") + 2 * lax.axis_index("y") + lax.axis_index("c")
    srcs = list(gather) + list(scatter)
    is_piece = [True] * len(gather) + [False] * len(scatter)
    lands = []
    for a, piece in zip(srcs, is_piece):
        if piece:
            lands.append(lax.dynamic_update_slice(lax.empty((N_DEV,) + tuple(a.shape), a.dtype), a[None],
                                                  (me_out,) + (0,) * a.ndim))
        else:
            lands.append(lax.empty(tuple(a.shape), a.dtype))
    n = len(srcs)
    nsem = n * len(_FLIPS)
    has_after = after is not None

    def body(*refs):
        src_refs = refs[:n]
        land_refs = refs[n:2 * n]
        outs = refs[2 * n + (1 if has_after else 0):]
        send_sem, recv_sem = outs[0], outs[1]
        token = outs[-1]
        me, peers = _peer_table()
        for kk in range(len(_FLIPS)):
            for a in range(n):
                _split_copy(is_piece, src_refs, land_refs, send_sem, recv_sem, a, kk, me, peers, False).start()
        token[...] = jnp.zeros_like(token)

    out_shape = ([pltpu.SemaphoreType.DMA((nsem,)), pltpu.SemaphoreType.DMA((nsem,))]
                 + [pltpu.HBM(tuple(a.shape), a.dtype) for a in srcs] + [pltpu.HBM(tuple(a.shape), a.dtype) for a in lands]
                 + [jax.ShapeDtypeStruct((8, 128), F32)])
    args = [pltpu.with_memory_space_constraint(a, pltpu.HBM) for a in srcs + lands] + ([after] if has_after else [])
    res = pl.pallas_call(
        body, name=name, out_shape=out_shape,
        in_specs=[_HBM] * (2 * n) + ([_ANY] if has_after else []),
        out_specs=[_SEM, _SEM] + [_HBM] * (2 * n) + [pl.BlockSpec(memory_space=pltpu.VMEM)],
        input_output_aliases={i: 2 + i for i in range(2 * n)},
        compiler_params=pltpu.CompilerParams(has_side_effects=_EFFECT),
    )(*args)
    state = (res[0], res[1], list(res[2:2 + n]), list(res[2 + n:2 + 2 * n]), is_piece)
    return state, res[-1]


def _xchg_wait(name, state, after):
    send_sem, recv_sem, srcs, lands, is_piece = state
    n = len(srcs)

    def body(*refs):
        src_refs = refs[:n]
        land_refs = refs[n:2 * n]
        s_sem, r_sem = refs[2 * n], refs[2 * n + 1]
        me, peers = _peer_table()
        for kk in range(len(_FLIPS)):
            for a in range(n):
                cp = _split_copy(is_piece, src_refs, land_refs, s_sem, r_sem, a, kk, me, peers, True)
                cp.wait_send()
                cp.wait_recv()

    out_shape = [pltpu.HBM(tuple(a.shape), a.dtype) for a in srcs] + [pltpu.HBM(tuple(a.shape), a.dtype) for a in lands]
    res = pl.pallas_call(
        body, name=name, out_shape=out_shape,
        in_specs=[_HBM] * (2 * n) + [_SEM, _SEM, _ANY], out_specs=[_HBM] * (2 * n),
        input_output_aliases={i: i for i in range(2 * n)},
        compiler_params=pltpu.CompilerParams(has_side_effects=_EFFECT),
    )(*srcs, *lands, send_sem, recv_sem, after)
    return list(res[n:]), list(res[:n])


_SIB = (0, 0, 1)
_ICI = ((0, 1, 0), (1, 0, 0), (1, 1, 0))


def _flip(fl):
    x, y, c = lax.axis_index("x"), lax.axis_index("y"), lax.axis_index("c")
    px = 1 - x if fl[0] else x
    py = 1 - y if fl[1] else y
    pc = 1 - c if fl[2] else c
    return (px, py, pc), 4 * px + 2 * py + pc


def _gather2_start(name, pieces, after=None):
    me_out = 4 * lax.axis_index("x") + 2 * lax.axis_index("y") + lax.axis_index("c")
    pieces = list(pieces)
    n = len(pieces)
    lands = [lax.dynamic_update_slice(lax.empty((N_DEV,) + tuple(a.shape), a.dtype), a[None],
                                      (me_out,) + (0,) * a.ndim) for a in pieces]
    first = (_SIB,) + _ICI
    has_after = after is not None

    def body(*refs):
        src_refs, land_refs = refs[:n], refs[n:2 * n]
        outs = refs[2 * n + (1 if has_after else 0):]
        send_sem, recv_sem, token = outs[0], outs[1], outs[-1]
        _, me = _flip((0, 0, 0))
        for kk, fl in enumerate(first):
            dev, _ = _flip(fl)
            for a in range(n):
                pltpu.make_async_remote_copy(src_ref=src_refs[a], dst_ref=land_refs[a].at[me],
                                             send_sem=send_sem.at[a * 4 + kk], recv_sem=recv_sem.at[a * 4 + kk],
                                             device_id=dev, device_id_type=MESH_ID).start()
        token[...] = jnp.zeros_like(token)

    hbm = [pltpu.HBM(tuple(a.shape), a.dtype) for a in pieces + lands]
    res = pl.pallas_call(
        body, name=name,
        out_shape=[pltpu.SemaphoreType.DMA((4 * n,)), pltpu.SemaphoreType.DMA((4 * n,))] + hbm
        + [jax.ShapeDtypeStruct((8, 128), F32)],
        in_specs=[_HBM] * (2 * n) + ([_ANY] if has_after else []),
        out_specs=[_SEM, _SEM] + [_HBM] * (2 * n) + [pl.BlockSpec(memory_space=pltpu.VMEM)],
        input_output_aliases={i: 2 + i for i in range(2 * n)},
        compiler_params=pltpu.CompilerParams(has_side_effects=_EFFECT),
    )(*([pltpu.with_memory_space_constraint(a, pltpu.HBM) for a in pieces + lands] + ([after] if has_after else [])))
    return (res[0], res[1], list(res[2:2 + n]), list(res[2 + n:2 + 2 * n])), res[-1]


def _gather2_forward(name, state, after):
    send_a, recv_a, pieces, lands = state
    n = len(pieces)
    first = (_SIB,) + _ICI

    def body(*refs):
        src_refs, land_refs = refs[:n], refs[n:2 * n]
        s_a, r_a = refs[2 * n], refs[2 * n + 1]
        outs = refs[2 * n + 3:]
        send_b, recv_b, token = outs[0], outs[1], outs[-1]
        for kk, fl in enumerate(first):
            dev, lin = _flip(fl)
            for a in range(n):
                cp = pltpu.make_async_remote_copy(src_ref=src_refs[a], dst_ref=land_refs[a].at[lin],
                                                  send_sem=s_a.at[a * 4 + kk], recv_sem=r_a.at[a * 4 + kk],
                                                  device_id=dev, device_id_type=MESH_ID)
                cp.wait_send()
                cp.wait_recv()
        sib, _ = _flip(_SIB)
        for j, fl in enumerate(_ICI):
            _, lin = _flip(fl)
            for a in range(n):
                pltpu.make_async_remote_copy(src_ref=land_refs[a].at[lin], dst_ref=land_refs[a].at[lin],
                                             send_sem=send_b.at[a * 3 + j], recv_sem=recv_b.at[a * 3 + j],
                                             device_id=sib, device_id_type=MESH_ID).start()
        token[...] = jnp.zeros_like(token)

    hbm = [pltpu.HBM(tuple(a.shape), a.dtype) for a in pieces + lands]
    res = pl.pallas_call(
        body, name=name,
        out_shape=[pltpu.SemaphoreType.DMA((3 * n,)), pltpu.SemaphoreType.DMA((3 * n,))] + hbm
        + [jax.ShapeDtypeStruct((8, 128), F32)],
        in_specs=[_HBM] * (2 * n) + [_SEM, _SEM, _ANY],
        out_specs=[_SEM, _SEM] + [_HBM] * (2 * n) + [pl.BlockSpec(memory_space=pltpu.VMEM)],
        input_output_aliases={i: 2 + i for i in range(2 * n)},
        compiler_params=pltpu.CompilerParams(has_side_effects=_EFFECT),
    )(*pieces, *lands, send_a, recv_a, after)
    return (res[0], res[1], list(res[2 + n:2 + 2 * n])), res[-1]


def _gather2_wait(name, state, after):
    send_b, recv_b, lands = state
    n = len(lands)

    def body(*refs):
        land_refs = refs[:n]
        s_b, r_b = refs[n], refs[n + 1]
        sib, _ = _flip(_SIB)
        for j, fl in enumerate(_ICI):
            _, sent = _flip(fl)
            _, arriving = _flip((fl[0], fl[1], 1))
            for a in range(n):
                cp = pltpu.make_async_remote_copy(src_ref=land_refs[a].at[sent], dst_ref=land_refs[a].at[arriving],
                                                  send_sem=s_b.at[a * 3 + j], recv_sem=r_b.at[a * 3 + j],
                                                  device_id=sib, device_id_type=MESH_ID)
                cp.wait_send()
                cp.wait_recv()

    res = pl.pallas_call(
        body, name=name, out_shape=[pltpu.HBM(tuple(a.shape), a.dtype) for a in lands],
        in_specs=[_HBM] * n + [_SEM, _SEM, _ANY], out_specs=[_HBM] * n,
        input_output_aliases={i: i for i in range(n)},
        compiler_params=pltpu.CompilerParams(has_side_effects=_EFFECT),
    )(*lands, send_b, recv_b, after)
    return list(res)


def _adam_math(w, g, m, v):
    m = ADAM_B1 * m + (1.0 - ADAM_B1) * g
    v = ADAM_B2 * v + (1.0 - ADAM_B2) * (g * g)
    m_hat = m / (1.0 - ADAM_B1 ** ADAM_STEP)
    v_hat = v / (1.0 - ADAM_B2 ** ADAM_STEP)
    delta = -ADAM_LR * (m_hat / (jnp.sqrt(v_hat) + ADAM_EPS) + ADAM_WD * w)
    return delta, m, v


def _adam(name, w, m, v, parts, transposed=False, dep=None, own=None):
    npart, _, cp = parts.shape
    has_dep = dep is not None
    has_own = own is not None
    if transposed:
        c, r = w.shape
        tr = _pick(r, (256, 128))
        blk = pl.BlockSpec((c, tr), lambda i, *_: (0, i))
    else:
        r, c = w.shape
        tr = _pick(r, (256, 176, 128, 64, 16, 8, 1))
        blk = pl.BlockSpec((tr, c), lambda i, *_: (i, 0))

    def body(*refs):
        refs = list(refs)
        me = refs.pop(0)[0] if has_own else None
        w_ref, m_ref, v_ref, p_ref = refs[:4]
        own_ref = refs[4] if has_own else None
        g_ref, d_ref, mo_ref, vo_ref = refs[-4:]
        g = None
        for pp in range(npart):
            part = p_ref[pp] if not has_own else jnp.where(me == pp, own_ref[...], p_ref[pp])
            g = part.astype(F32) if g is None else g + part.astype(F32)
        g = g.T[0:c, :] if transposed else g[:, 0:c]
        delta, mn, vn = _adam_math(w_ref[...], g, m_ref[...], v_ref[...])
        g_ref[...] = g
        d_ref[...] = delta
        mo_ref[...] = mn
        vo_ref[...] = vn

    in_specs = [blk, blk, blk, pl.BlockSpec((npart, tr, cp), lambda i, *_: (0, i, 0))]
    args = [w, m, v, parts]
    if has_own:
        in_specs.append(pl.BlockSpec((None, tr, cp), lambda i, me_ref: (me_ref[0], i, 0)))
        args.append(own[0])
    if has_dep:
        in_specs.append(pl.BlockSpec(memory_space=pl.ANY))
        args.append(dep)
    out = jax.ShapeDtypeStruct(w.shape, F32)
    grid_spec = pltpu.PrefetchScalarGridSpec(num_scalar_prefetch=1 if has_own else 0, grid=(r // tr,),
                                             in_specs=in_specs, out_specs=[blk, blk, blk, blk])
    return pl.pallas_call(
        body, name=name, grid_spec=grid_spec, out_shape=[out, out, out, out], compiler_params=_params(("parallel",)),
    )(*(([own[1]] if has_own else []) + args))


def _small_update(name, packs, wmv):
    nparam = len(wmv)

    def body(p_ref, *refs):
        ins, outs = refs[:3 * nparam], refs[3 * nparam:]
        tot = p_ref[0]
        for pp in range(1, N_DEV):
            tot = tot + p_ref[pp]
        outs[0][0:8, :] = tot[0:8, :]
        outs[0][8:24, :] = tot[8:24, :] + tot[24:40, :]
        grads = [tot[i:i + 1, :] for i in range(4)] + [tot[4:5, 0:B_HEADS], tot[4:5, B_HEADS:2 * B_HEADS]]
        for i, g in enumerate(grads):
            w_ref, m_ref, v_ref = ins[3 * i:3 * i + 3]
            delta, mn, vn = _adam_math(w_ref[...], g, m_ref[...], v_ref[...])
            for o_ref, val in zip(outs[1 + 4 * i:5 + 4 * i], (g, delta, mn, vn)):
                o_ref[...] = val

    flat = [a for trio in wmv for a in trio]
    out_shape = [jax.ShapeDtypeStruct((24, D_MODEL), F32)]
    for w, _, _ in wmv:
        out_shape += [jax.ShapeDtypeStruct(w.shape, F32)] * 4
    res = pl.pallas_call(body, name=name, out_shape=out_shape, compiler_params=_params())(packs, *flat)
    return res[0], [tuple(res[1 + 4 * i:5 + 4 * i]) for i in range(nparam)]


def _local_step(x, tgt, g1, gm, g2, gf, b_forget, sinks, weights, send):
    b, s, _ = x.shape
    l = s + PREFIX
    t = b * l
    (meta,) = weights("meta", x)

    h0, n1 = _embed_norm("embed_rms1_fwd", x, meta, g1)
    (w1i,) = weights("ffn1_in", n1)
    gu1, a1 = _ffn_in_fwd("ffn1_in_fwd", n1, w1i)
    (w1o,) = weights("ffn1_out", weights("ffn1_out:forward", a1))
    h1, um = _mm_res_norm("ffn1_out_fwd", a1, w1o, h0, gm, alpha=0.5)
    wi, wa, wb, wo = weights("mix", weights("mix:forward", um))
    qkv, gates, f2 = _proj_fwd("proj_fwd", um, wi)
    qkv3 = qkv.reshape(b, l, QKV_W)
    f3 = f2.reshape(b, l, 128)
    bf_row = jnp.pad(b_forget, ((0, 0), (0, 128 - B_HEADS)))
    c_col, c_row = _fgate_fwd("fgate_fwd", f3, bf_row)
    head_of_row = np.arange(STACK) // BLOCK
    slopes_np = np.exp2(-8.0 * (head_of_row + 1) / A_HEADS).astype(np.float32).reshape(STACK, 1)
    slopes = jnp.asarray(slopes_np)
    sink_rows = jnp.repeat(sinks.reshape(A_HEADS), BLOCK).reshape(STACK, 1)
    swa_bias = _swa_bias(slopes_np)
    oa3, oa3_bf, lse_a = _swa_fwd("swa_fwd", qkv3, slopes, sink_rows, swa_bias)
    ob3, lse_b = _fox_fwd("fox_fwd", qkv3, c_col, c_row)
    oa = oa3_bf.reshape(t, A_WIDTH)
    ob = ob3.reshape(t, B_WIDTH)
    mixed, ya, yb = _branch_gate_fwd("branch_gate_fwd", oa, ob, wa, wb, gates, dep=weights("ffn2:forward", ob))
    h2, n2 = _mm_res_norm("mix_out_fwd", mixed, wo, h1, g2)
    w2i, w2o = weights("ffn2", h2)
    gu2, a2 = _ffn_in_fwd("ffn2_in_fwd", n2, w2i)

    dh3, dh3_bf, loss_blk, dgf = _ffn_out_loss("ffn2_out_loss", a2, w2o, h2, gf, tgt, 0.5)

    def ffn_bwd(tag, dh, dh_bf, h_in, g_norm, n_in, gu, a, w_in_blk, w_out, one_send, examples=None):
        dw_out = _ffn_out_bwd_w(tag + "_out_bwd_w", a, dh_bf)
        dgu = _ffn_out_bwd_x(tag + "_out_bwd_x", dh_bf, w_out, gu, dep=None if one_send else send(tag + "_out", (dw_out,)))
        dw_in = _ffn_in_bwd_w(tag + "_in_bwd_w", n_in, dgu)
        token = send(tag, (dw_in, dw_out)) if one_send else send(tag + "_in", (dw_in,))
        return _ffn_in_bwd_x(tag + "_in_bwd_x", dgu, w_in_blk, h_in, g_norm, dh, dep=token, examples=examples)

    dh2, dh2_bf, dg2 = ffn_bwd("ffn2", dh3, dh3_bf, h2, g2, n2, gu2, a2, w2i, w2o, True)

    dwo = _mm_tn("mix_out_bwd_w", mixed, dh2_bf, tn_c=(D_MODEL,))
    dya, dyb, dgates = _mix_out_gate_bwd("mix_out_gate_bwd", dh2_bf, wo, gates, ya, yb)
    doa, dob = _branch_bwd_x("branch_bwd_x", dya, dyb, wa, wb)
    dwa, dwb = _branch_bwd_w("branch_bwd_w", oa, ob, dya, dyb)
    dqkv3, dcq, dck = _fox_bwd("fox_bwd", qkv3, c_col, c_row, ob3, lse_b, dob.reshape(b, l, B_WIDTH))
    dqkv3, dsink = _swa_bwd("swa_bwd", qkv3, oa3, lse_a, doa.reshape(b, l, A_WIDTH), slopes, sink_rows, swa_bias, dqkv3)
    dqkv = dqkv3.reshape(t, QKV_W)
    df3, dbf = _fgate_bwd("fgate_bwd", f3, bf_row, dcq, dck)
    df = df3.reshape(t, 128)
    dwi_qkv = _mm_tn("proj_qkv_bwd_w", um, dqkv, tm_c=(512,), tn_c=(768,))
    dwi_g = _mm_tn("proj_gates_bwd_w", um, dgates, tm_c=(512,), tn_c=(512,))
    dwi_f = _mm_tn("proj_f_bwd_w", um, df, tm_c=(512,), tn_c=(128,))
    token = send("mix", (dwi_qkv, dwi_g, dwi_f, dwa, dwb, dwo))
    dh1, dh1_bf, dgm = _proj_bwd_x("proj_bwd_x", dqkv, dgates, df, wi, h1, gm, dh2, dep=token)

    grad_x, dmeta3, dg1 = ffn_bwd("ffn1", dh1, dh1_bf, h0, g1, n1, gu1, a1, w1i, w1o, False, examples=b)
    dmeta = dmeta3.reshape(b * N_META, D_MODEL)

    misc = jnp.concatenate([dbf[:, 0:B_HEADS], dsink[:, 0].reshape(1, A_HEADS), loss_blk[0:1, 0:1]], axis=1)
    misc = jnp.pad(misc, ((0, 0), (0, D_MODEL - misc.shape[1])))
    row = lax.broadcasted_iota(jnp.int32, (8, D_MODEL), 0)
    vec = jnp.zeros((8, D_MODEL), F32)
    for i, piece in enumerate((dg1, dgm, dg2, dgf, misc)):
        vec = jnp.where(row == i, piece, vec)
    small = jnp.concatenate([vec, dmeta], axis=0)
    return grad_x, small


def _pad_to(a, rows, cols):
    return jnp.pad(a, ((0, rows - a.shape[0]), (0, cols - a.shape[1])))


def _ffn_out_from_gathered(name, g):
    def body(g_ref, o_ref):
        o_ref[0:FFO_SHARD, :] = g_ref[0]
        o_ref[FFO_SHARD:FF_SHARD, :] = g_ref[1]
        o_ref[FF_SHARD:FF_SHARD_P, :] = jnp.zeros((FF_SHARD_P - FF_SHARD, D_MODEL), BF16)

    return pl.pallas_call(
        body, name=name, grid=(4,),
        in_specs=[pl.BlockSpec((2, FFO_SHARD, D_MODEL), lambda j: (j, 0, 0))],
        out_specs=pl.BlockSpec((FF_SHARD_P, D_MODEL), lambda j: (j, 0)),
        out_shape=jax.ShapeDtypeStruct((D_FF_P, D_MODEL), BF16), compiler_params=_params(("parallel",)),
    )(g)


def _ffn_out_bwd_w(name, a, dh):
    t = a.shape[0]

    def body(a_ref, b_ref, o_ref):
        acc = lax.dot_general(a_ref[...], _bf(b_ref[...]), _TN, preferred_element_type=F32) * 0.5
        o_ref[0] = acc[0:FFO_SHARD].astype(BF16)
        o_ref[1] = acc[FFO_SHARD:FF_SHARD].astype(BF16)

    return pl.pallas_call(
        body, name=name, grid=(4,),
        in_specs=[pl.BlockSpec((t, FF_SHARD_P), lambda i: (0, i)),
                  pl.BlockSpec((t, D_MODEL), lambda i: (0, 0), pipeline_mode=pl.Buffered(1))],
        out_specs=pl.BlockSpec((2, FFO_SHARD, D_MODEL), lambda i: (i, 0, 0)),
        out_shape=jax.ShapeDtypeStruct((N_DEV, FFO_SHARD, D_MODEL), BF16), compiler_params=_params(("parallel",)),
    )(a, dh)


def _proj_segments():
    segs = [(HEAD_DIM * h, HEAD_DIM, 0, B_SEG + HEAD_DIM * A_HEAD_ORDER.index(h)) for h in range(A_HEADS)]
    segs += [(512, 128, 0, B_SEG + A_WIDTH), (640, 128, 0, B_SEG + A_WIDTH + 128)]
    for first, off in ((768, 0), (1280, 128), (1792, 256)):
        segs += [(first + 128 * hp, 128, 0, PAIR_W * hp + off) for hp in range(4)]
    segs += [(2304, B_HEADS, 2, 0), (2312, 2 * D_MODEL, 1, 0)]
    return segs


_RELAYOUT_ROWS = 256


def _proj_from_gathered(name, g):
    rows = _RELAYOUT_ROWS

    def body(g_ref, o_ref):
        def cols(first, width):
            out = []
            for p in range(N_DEV):
                lo, hi = max(first, WIN_SHARD * p), min(first + width, WIN_SHARD * (p + 1))
                if lo < hi:
                    out.append(g_ref[p, :, lo - WIN_SHARD * p:hi - WIN_SHARD * p])
            return out

        parts = []
        for arr in (0, 1, 2):
            for first, width, _, _ in sorted((s for s in _proj_segments() if s[2] == arr), key=lambda s: s[3]):
                parts += cols(first, width)
            if arr == 0:
                parts.append(jnp.zeros((rows, P_GATES - QKV_W), BF16))
        parts.append(jnp.zeros((rows, 128 - B_HEADS), BF16))
        o_ref[...] = jnp.concatenate(parts, axis=1)

    return pl.pallas_call(
        body, name=name, grid=(D_MODEL // rows,),
        in_specs=[pl.BlockSpec((N_DEV, rows, WIN_SHARD_P), lambda i: (0, i, 0))],
        out_specs=pl.BlockSpec((rows, PROJ_P), lambda i: (i, 0)),
        out_shape=jax.ShapeDtypeStruct((D_MODEL, PROJ_P), BF16), compiler_params=_params(("parallel",)),
    )(g)


def _proj_to_scatter(name, dqkv_w, dg_w, df_w):
    rows = _RELAYOUT_ROWS
    segs = sorted(_proj_segments())

    def body(q_ref, g_ref, f_ref, o_ref):
        arrays = (q_ref, g_ref, f_ref)
        for p in range(N_DEV):
            parts = []
            for first, width, arr, at in segs:
                lo, hi = max(first, WIN_SHARD * p), min(first + width, WIN_SHARD * (p + 1))
                if lo < hi:
                    parts.append(arrays[arr][:, at + lo - first:at + hi - first])
            parts.append(jnp.zeros((rows, WIN_SHARD_P - WIN_SHARD), BF16))
            o_ref[p] = jnp.concatenate(parts, axis=1)

    return pl.pallas_call(
        body, name=name, grid=(D_MODEL // rows,),
        in_specs=[pl.BlockSpec((rows, QKV_W), lambda i: (i, 0)), pl.BlockSpec((rows, 2 * D_MODEL), lambda i: (i, 0)),
                  pl.BlockSpec((rows, 128), lambda i: (i, 0))],
        out_specs=pl.BlockSpec((N_DEV, rows, WIN_SHARD_P), lambda i: (0, i, 0)),
        out_shape=jax.ShapeDtypeStruct((N_DEV, D_MODEL, WIN_SHARD_P), BF16), compiler_params=_params(("parallel",)),
    )(dqkv_w, dg_w, df_w)


def _a_rows_from_natural(w):
    return jnp.concatenate([w[HEAD_DIM * h:HEAD_DIM * (h + 1)] for h in A_HEAD_ORDER], axis=0)


def _a_rows_to_natural(w):
    return jnp.concatenate([w[HEAD_DIM * A_HEAD_ORDER.index(h):HEAD_DIM * (A_HEAD_ORDER.index(h) + 1)]
                            for h in range(A_HEADS)], axis=0)


def kernel(x, meta_tokens, ffn1_norm, ffn1_w_in, ffn1_w_out, mix_norm, w_in, b_forget, attn_sinks, w_branch_a, w_branch_b, w_out, ffn2_norm, ffn2_w_in, ffn2_w_out, final_norm, loss_target, m_meta_tokens, m_ffn1_norm, m_ffn1_w_in, m_ffn1_w_out, m_mix_norm, m_w_in, m_b_forget, m_attn_sinks, m_w_branch_a, m_w_branch_b, m_w_out, m_ffn2_norm, m_ffn2_w_in, m_ffn2_w_out, m_final_norm, v_meta_tokens, v_ffn1_norm, v_ffn1_w_in, v_ffn1_w_out, v_mix_norm, v_w_in, v_b_forget, v_attn_sinks, v_w_branch_a, v_w_branch_b, v_w_out, v_ffn2_norm, v_ffn2_w_in, v_ffn2_w_out, v_final_norm):
    me = 4 * lax.axis_index("x") + 2 * lax.axis_index("y") + lax.axis_index("c")

    def shard(w, tok, rows=None, cols=None):
        piece = (w[0] if tok is None else w[0] + tok[0, 0]).astype(BF16)
        return piece if rows is None else _pad_to(piece, rows, cols)

    first_level, second_level = {}, {}
    first_level["meta"], tok = _gather2_start("gather_meta_start", (meta_tokens,))
    first_level["ffn1_in"], tok = _gather2_start(
        "gather_ffn1_in_start", (shard(ffn1_w_in, None, D_MODEL, FF_SHARD_P),), after=tok)
    first_level["ffn1_out"], tok = _gather2_start("gather_ffn1_out_start", (shard(ffn1_w_out, tok),), after=tok)
    first_level["mix"], tok = _gather2_start(
        "gather_mix_start", (shard(w_in, tok, D_MODEL, WIN_SHARD_P), shard(w_branch_a, tok), shard(w_branch_b, tok),
                             shard(w_out, tok)), after=tok)
    first_level["ffn2"], tok = _gather2_start(
        "gather_ffn2_start", (shard(ffn2_w_in, tok, D_MODEL, FF_SHARD_P), shard(ffn2_w_out, tok)), after=tok)
    started = {"tok": tok}

    def weights(group, after):
        if group.endswith(":forward"):
            group = group[:-len(":forward")]
            second_level[group], token = _gather2_forward("gather_" + group + "_forward", first_level[group], after)
            return token
        if group == "meta":
            after = weights("meta:forward", started["tok"])
        if group == "ffn1_in":
            after = weights("ffn1_in:forward", after)
        got = _gather2_wait("gather_" + group + "_wait", second_level[group], after)
        if group == "mix":
            gwi, gwa, gwb, gwo = got
            return (_proj_from_gathered("proj_w_relayout", gwi), _a_rows_from_natural(gwa.transpose(1, 0, 2).reshape(A_WIDTH, D_MODEL)),
                    gwb.transpose(1, 0, 2).reshape(B_WIDTH, D_MODEL), gwo.reshape(D_MODEL, D_MODEL))
        if group == "ffn1_out":
            return (_ffn_out_from_gathered("ffn1_w_out_relayout", got[0]),)
        if group == "meta":
            return (got[0].transpose(1, 0, 2).reshape(N_META, D_MODEL),)
        if group == "ffn1_in":
            return (got[0].reshape(2, 4, D_MODEL, FF_SHARD_P),)
        return got[0].reshape(2, 4, D_MODEL, FF_SHARD_P), _ffn_out_from_gathered("ffn2_w_out_relayout", got[1])

    scatter_state = {}

    def send(group, grads):
        if group == "mix":
            dwi_qkv, dwi_g, dwi_f, dwa, dwb, dwo = grads
            dwa = _a_rows_to_natural(dwa)
            blocks = (_proj_to_scatter("proj_dw_relayout", dwi_qkv, dwi_g, dwi_f), dwa.reshape(A_WIDTH, N_DEV, 128).transpose(1, 0, 2),
                      dwb.reshape(B_WIDTH, N_DEV, 128).transpose(1, 0, 2), dwo.reshape(N_DEV, 128, D_MODEL))
        elif group.endswith("_in"):
            blocks = (grads[0].reshape(N_DEV, FF_SHARD_P, D_MODEL),)
        elif group.endswith("_out"):
            blocks = (grads[0],)
        else:
            blocks = (grads[0].reshape(N_DEV, FF_SHARD_P, D_MODEL), grads[1])
        scatter_state[group], token = _xchg_start("scatter_" + group + "_start", (), blocks)
        return token

    gf = final_norm.reshape(1, D_MODEL)
    grad_x, small = _local_step(x, loss_target, ffn1_norm, mix_norm, ffn2_norm, gf, b_forget, attn_sinks, weights, send)

    small_state, after = _xchg_start("gather_small_start", (small,), ())
    out = {}
    updates = (
        ("ffn2", (("ffn2_w_in", ffn2_w_in, m_ffn2_w_in, v_ffn2_w_in), ("ffn2_w_out", ffn2_w_out, m_ffn2_w_out, v_ffn2_w_out))),
        ("ffn1_out", (("ffn1_w_out", ffn1_w_out, m_ffn1_w_out, v_ffn1_w_out),)),
        ("mix", (("w_in", w_in, m_w_in, v_w_in), ("w_branch_a", w_branch_a, m_w_branch_a, v_w_branch_a),
                 ("w_branch_b", w_branch_b, m_w_branch_b, v_w_branch_b), ("w_out", w_out, m_w_out, v_w_out))),
    )
    last_update = ("ffn1_in", (("ffn1_w_in", ffn1_w_in, m_ffn1_w_in, v_ffn1_w_in),))

    def update(group, members, after):
        parts_list, blocks_list = _xchg_wait("scatter_" + group + "_wait", scatter_state[group], after)
        prev = None
        for (nm, w, m, v), parts, blocks in zip(members, parts_list, blocks_list):
            if nm.endswith("_w_in"):
                res4 = _adam("adam_" + nm, w[0].T, m[0].T, v[0].T, parts, dep=prev, own=(blocks, me_arr))
                out[nm] = tuple(r.T[None] for r in res4)
            elif nm == "w_in":
                res4 = _adam("adam_" + nm, w[0].T, m[0].T, v[0].T, parts, transposed=True, dep=prev, own=(blocks, me_arr))
                out[nm] = tuple(r.T[None] for r in res4)
            else:
                res4 = _adam("adam_" + nm, w[0], m[0], v[0], parts, dep=prev, own=(blocks, me_arr))
                out[nm] = tuple(r[None] for r in res4)
            prev = res4[0]
        return prev

    me_arr = me.reshape(1).astype(jnp.int32)
    for group, members in updates + (last_update,):
        after = update(group, members, after)
    (packs,), _ = _xchg_wait("gather_small_wait", small_state, after)

    row = lambda a: a.reshape(1, D_MODEL)
    tot, small_res = _small_update("small_update", packs, (
        (ffn1_norm, m_ffn1_norm, v_ffn1_norm), (mix_norm, m_mix_norm, v_mix_norm), (ffn2_norm, m_ffn2_norm, v_ffn2_norm),
        (row(final_norm), row(m_final_norm), row(v_final_norm)), (b_forget, m_b_forget, v_b_forget),
        (attn_sinks, m_attn_sinks, v_attn_sinks)))
    for nm, res4 in zip(("ffn1_norm", "mix_norm", "ffn2_norm", "final_norm", "b_forget", "attn_sinks"), small_res):
        out[nm] = tuple(r.reshape(D_MODEL) for r in res4) if nm == "final_norm" else res4
    loss = tot[4, 2 * B_HEADS]
    g_meta = lax.dynamic_slice(tot[8:24, :], (0, me * 128), (N_META, 128))
    out["meta_tokens"] = tuple(_adam("adam_meta_tokens", meta_tokens, m_meta_tokens, v_meta_tokens, g_meta[None]))

    names = ("meta_tokens", "ffn1_norm", "ffn1_w_in", "ffn1_w_out", "mix_norm", "w_in", "b_forget", "attn_sinks",
             "w_branch_a", "w_branch_b", "w_out", "ffn2_norm", "ffn2_w_in", "ffn2_w_out", "final_norm")
    return (loss, grad_x) + tuple(out[nm][kind] for kind in range(4) for nm in names)
```
